```python
import math
import jax, jax.numpy as jnp
from jax import lax
import numpy as np

D_MODEL = 1024
BATCH = 8
SEQ = 4096
DEPTH = 1

EPS = 1e-6
D_FF = 2816
CONV_CH = D_MODEL
CONV_WIDTH = 31
HEAD_DIM = 64
N_Q_HEADS = 16
N_KV_HEADS = 4
GROUP = N_Q_HEADS // N_KV_HEADS
WINDOW = 128
BLOCK = WINDOW
N_BUCKETS = 32
MAX_DISTANCE = 128
Q_W = N_Q_HEADS * HEAD_DIM
KV_W = N_KV_HEADS * HEAD_DIM
SPLITS = (2 * CONV_CH, Q_W, KV_W, KV_W, D_MODEL, D_MODEL)
IN_W = sum(SPLITS)

kernel_name = "hybrid_conformer_conv_swa_sink_macaron"


def rmsnorm(x, g):
    xf = x.astype(jnp.float32)
    y = xf * lax.rsqrt(jnp.mean(xf * xf, axis=-1, keepdims=True) + EPS)
    return (y * g.astype(jnp.float32)).astype(x.dtype)


def layernorm(x, g, b):
    xf = x.astype(jnp.float32)
    mu = jnp.mean(xf, axis=-1, keepdims=True)
    var = jnp.mean(jnp.square(xf - mu), axis=-1, keepdims=True)
    y = (xf - mu) * lax.rsqrt(var + EPS)
    return (y * g.astype(jnp.float32) + b.astype(jnp.float32)).astype(x.dtype)


def swiglu_ffn(x, w_in, w_out):
    a, b = jnp.split(x @ w_in, 2, axis=-1)
    return (jax.nn.silu(a) * b) @ w_out


def t5_causal_bucket(dist):
    max_exact = N_BUCKETS // 2
    d = jnp.maximum(dist, 1).astype(jnp.float32)
    large = max_exact + (jnp.log(d / max_exact) / math.log(MAX_DISTANCE / max_exact)
                         * (N_BUCKETS - max_exact)).astype(jnp.int32)
    large = jnp.minimum(large, N_BUCKETS - 1)
    return jnp.where(dist < max_exact, dist, large)


def conformer_conv(u, dw_kernel, dw_bias, ln_g, ln_b, w_proj):
    a, g = jnp.split(u, 2, axis=-1)
    z = a * jax.nn.sigmoid(g)
    z = lax.conv_general_dilated(
        z, dw_kernel[:, None, :].astype(z.dtype),
        window_strides=(1,), padding=((CONV_WIDTH - 1, 0),),
        dimension_numbers=("NWC", "WIO", "NWC"),
        feature_group_count=CONV_CH) + dw_bias
    z = jax.nn.silu(layernorm(z, ln_g, ln_b))
    return z @ w_proj


def sliding_window_gqa(q, k, v, q_norm_g, k_norm_g, sinks, rel_bias, w_o):
    B, S = q.shape[0], q.shape[1]
    nb = S // BLOCK
    q = rmsnorm(q, q_norm_g)
    k = rmsnorm(k, k_norm_g)
    qb = q.reshape(B, nb, BLOCK, N_KV_HEADS, GROUP, HEAD_DIM)
    kb = k.reshape(B, nb, BLOCK, N_KV_HEADS, HEAD_DIM)
    vb = v.reshape(B, nb, BLOCK, N_KV_HEADS, HEAD_DIM)
    kpad = jnp.zeros_like(kb[:, :1])
    vpad = jnp.zeros_like(vb[:, :1])
    kw = jnp.concatenate([jnp.concatenate([kpad, kb[:, :-1]], 1), kb], axis=2)
    vw = jnp.concatenate([jnp.concatenate([vpad, vb[:, :-1]], 1), vb], axis=2)

    s = jnp.einsum("bnqhgd,bnkhd->bnhgqk", qb, kw).astype(jnp.float32)
    s = s * (1.0 / math.sqrt(HEAD_DIM))

    qi = jnp.arange(BLOCK, dtype=jnp.int32)[:, None]
    kj = jnp.arange(2 * BLOCK, dtype=jnp.int32)[None, :]
    dist = qi + BLOCK - kj
    in_win = (dist >= 0) & (dist < WINDOW)
    bucket = t5_causal_bucket(jnp.maximum(dist, 0))
    bias = rel_bias.astype(jnp.float32)[bucket]
    bias = jnp.transpose(bias, (2, 0, 1)).reshape(N_KV_HEADS, GROUP, BLOCK, 2 * BLOCK)
    key_pos = jnp.arange(nb, dtype=jnp.int32)[:, None] * BLOCK - BLOCK + kj
    valid = in_win[None] & (key_pos >= 0)[:, None, :]

    s = s + bias[None, None]
    s = jnp.where(valid[None, :, None, None], s, jnp.finfo(jnp.float32).min)
    sink = sinks.astype(jnp.float32).reshape(N_KV_HEADS, GROUP)[None, None, :, :, None, None]
    m = jnp.maximum(jnp.max(s, axis=-1, keepdims=True), sink)
    p = jnp.exp(s - m)
    p = p / (jnp.sum(p, axis=-1, keepdims=True) + jnp.exp(sink - m))
    o = jnp.einsum("bnhgqk,bnkhd->bnqhgd", p.astype(vw.dtype), vw)
    return o.reshape(B, S, Q_W) @ w_o


def _fwd_setup_inputs(seed: int = 0) -> dict:
    key = jax.random.key(seed)
    ks = jax.random.split(key, 24)
    f32 = jnp.float32

    def w(k, shape, fan_in):
        return jax.random.normal(k, shape, f32) * (fan_in ** -0.5)

    def gain(k, n):
        return jnp.ones((n,), f32) + 0.01 * jax.random.normal(k, (n,), f32)

    return {
        "x": jax.random.normal(ks[0], (BATCH, SEQ, D_MODEL), f32),
        "ffn1_norm": gain(ks[1], D_MODEL),
        "ffn1_w_in": w(ks[2], (D_MODEL, 2 * D_FF), D_MODEL),
        "ffn1_w_out": w(ks[3], (D_FF, D_MODEL), D_FF),
        "mix_norm": gain(ks[4], D_MODEL),
        "w_in": w(ks[5], (D_MODEL, IN_W), D_MODEL),
        "conv_dw_kernel": w(ks[6], (CONV_WIDTH, CONV_CH), CONV_WIDTH),
        "conv_dw_bias": 0.01 * jax.random.normal(ks[7], (CONV_CH,), f32),
        "conv_ln_g": gain(ks[8], CONV_CH),
        "conv_ln_b": 0.01 * jax.random.normal(ks[9], (CONV_CH,), f32),
        "conv_w_proj": w(ks[10], (CONV_CH, D_MODEL), CONV_CH),
        "q_norm": gain(ks[11], HEAD_DIM),
        "k_norm": gain(ks[12], HEAD_DIM),
        "attn_sinks": jax.random.normal(ks[13], (N_Q_HEADS,), f32),
        "rel_bias": 0.1 * jax.random.normal(ks[14], (N_BUCKETS, N_Q_HEADS), f32),
        "attn_w_o": w(ks[15], (Q_W, D_MODEL), Q_W),
        "w_out": w(ks[16], (D_MODEL, D_MODEL), D_MODEL),
        "ffn2_norm": gain(ks[17], D_MODEL),
        "ffn2_w_in": w(ks[18], (D_MODEL, 2 * D_FF), D_MODEL),
        "ffn2_w_out": w(ks[19], (D_FF, D_MODEL), D_FF),
    }


def _fwd_reference(x, ffn1_norm, ffn1_w_in, ffn1_w_out, mix_norm, w_in,
              conv_dw_kernel, conv_dw_bias, conv_ln_g, conv_ln_b, conv_w_proj,
              q_norm, k_norm, attn_sinks, rel_bias, attn_w_o, w_out,
              ffn2_norm, ffn2_w_in, ffn2_w_out):
    B, S = x.shape[0], x.shape[1]
    for _ in range(DEPTH):
        x = x + 0.5 * swiglu_ffn(rmsnorm(x, ffn1_norm), ffn1_w_in, ffn1_w_out)

        h = rmsnorm(x, mix_norm)
        idx = np.cumsum(SPLITS)[:-1].tolist()
        u_conv, q, k, v, g_conv, g_attn = jnp.split(h @ w_in, idx, axis=-1)
        a = conformer_conv(u_conv, conv_dw_kernel, conv_dw_bias, conv_ln_g, conv_ln_b, conv_w_proj)
        b = sliding_window_gqa(q.reshape(B, S, N_Q_HEADS, HEAD_DIM),
                               k.reshape(B, S, N_KV_HEADS, HEAD_DIM),
                               v.reshape(B, S, N_KV_HEADS, HEAD_DIM),
                               q_norm, k_norm, attn_sinks, rel_bias, attn_w_o)
        merged = jax.nn.sigmoid(g_conv) * a + jax.nn.sigmoid(g_attn) * b
        x = x + merged @ w_out

        x = x + 0.5 * swiglu_ffn(rmsnorm(x, ffn2_norm), ffn2_w_in, ffn2_w_out)
    return x


import jax as _jax
import jax.numpy as _jnp

TWIN_FORMAT = 'train_step'
FWD_PARAMS = ['x', 'ffn1_norm', 'ffn1_w_in', 'ffn1_w_out', 'mix_norm', 'w_in', 'conv_dw_kernel', 'conv_dw_bias', 'conv_ln_g', 'conv_ln_b', 'conv_w_proj', 'q_norm', 'k_norm', 'attn_sinks', 'rel_bias', 'attn_w_o', 'w_out', 'ffn2_norm', 'ffn2_w_in', 'ffn2_w_out']
TWIN_WEIGHTS = ['ffn1_norm', 'ffn1_w_in', 'ffn1_w_out', 'mix_norm', 'w_in', 'conv_dw_kernel', 'conv_dw_bias', 'conv_ln_g', 'conv_ln_b', 'conv_w_proj', 'q_norm', 'k_norm', 'attn_sinks', 'rel_bias', 'attn_w_o', 'w_out', 'ffn2_norm', 'ffn2_w_in', 'ffn2_w_out']
TWIN_DIFF_INPUT = 'x'
TWIN_INPUTS = ['x', 'ffn1_norm', 'ffn1_w_in', 'ffn1_w_out', 'mix_norm', 'w_in', 'conv_dw_kernel', 'conv_dw_bias', 'conv_ln_g', 'conv_ln_b', 'conv_w_proj', 'q_norm', 'k_norm', 'attn_sinks', 'rel_bias', 'attn_w_o', 'w_out', 'ffn2_norm', 'ffn2_w_in', 'ffn2_w_out', 'loss_target', 'm_ffn1_norm', 'm_ffn1_w_in', 'm_ffn1_w_out', 'm_mix_norm', 'm_w_in', 'm_conv_dw_kernel', 'm_conv_dw_bias', 'm_conv_ln_g', 'm_conv_ln_b', 'm_conv_w_proj', 'm_q_norm', 'm_k_norm', 'm_attn_sinks', 'm_rel_bias', 'm_attn_w_o', 'm_w_out', 'm_ffn2_norm', 'm_ffn2_w_in', 'm_ffn2_w_out', 'v_ffn1_norm', 'v_ffn1_w_in', 'v_ffn1_w_out', 'v_mix_norm', 'v_w_in', 'v_conv_dw_kernel', 'v_conv_dw_bias', 'v_conv_ln_g', 'v_conv_ln_b', 'v_conv_w_proj', 'v_q_norm', 'v_k_norm', 'v_attn_sinks', 'v_rel_bias', 'v_attn_w_o', 'v_w_out', 'v_ffn2_norm', 'v_ffn2_w_in', 'v_ffn2_w_out']
TWIN_OUTPUTS = ['loss', 'grad_x', 'grad_ffn1_norm', 'grad_ffn1_w_in', 'grad_ffn1_w_out', 'grad_mix_norm', 'grad_w_in', 'grad_conv_dw_kernel', 'grad_conv_dw_bias', 'grad_conv_ln_g', 'grad_conv_ln_b', 'grad_conv_w_proj', 'grad_q_norm', 'grad_k_norm', 'grad_attn_sinks', 'grad_rel_bias', 'grad_attn_w_o', 'grad_w_out', 'grad_ffn2_norm', 'grad_ffn2_w_in', 'grad_ffn2_w_out', 'delta_ffn1_norm', 'delta_ffn1_w_in', 'delta_ffn1_w_out', 'delta_mix_norm', 'delta_w_in', 'delta_conv_dw_kernel', 'delta_conv_dw_bias', 'delta_conv_ln_g', 'delta_conv_ln_b', 'delta_conv_w_proj', 'delta_q_norm', 'delta_k_norm', 'delta_attn_sinks', 'delta_rel_bias', 'delta_attn_w_o', 'delta_w_out', 'delta_ffn2_norm', 'delta_ffn2_w_in', 'delta_ffn2_w_out', 'new_m_ffn1_norm', 'new_m_ffn1_w_in', 'new_m_ffn1_w_out', 'new_m_mix_norm', 'new_m_w_in', 'new_m_conv_dw_kernel', 'new_m_conv_dw_bias', 'new_m_conv_ln_g', 'new_m_conv_ln_b', 'new_m_conv_w_proj', 'new_m_q_norm', 'new_m_k_norm', 'new_m_attn_sinks', 'new_m_rel_bias', 'new_m_attn_w_o', 'new_m_w_out', 'new_m_ffn2_norm', 'new_m_ffn2_w_in', 'new_m_ffn2_w_out', 'new_v_ffn1_norm', 'new_v_ffn1_w_in', 'new_v_ffn1_w_out', 'new_v_mix_norm', 'new_v_w_in', 'new_v_conv_dw_kernel', 'new_v_conv_dw_bias', 'new_v_conv_ln_g', 'new_v_conv_ln_b', 'new_v_conv_w_proj', 'new_v_q_norm', 'new_v_k_norm', 'new_v_attn_sinks', 'new_v_rel_bias', 'new_v_attn_w_o', 'new_v_w_out', 'new_v_ffn2_norm', 'new_v_ffn2_w_in', 'new_v_ffn2_w_out']
TWIN_LEAF_KINDS = {'loss': 'loss', 'grad_x': 'grad_x', 'grad_ffn1_norm': 'grad_w', 'grad_ffn1_w_in': 'grad_w', 'grad_ffn1_w_out': 'grad_w', 'grad_mix_norm': 'grad_w', 'grad_w_in': 'grad_w', 'grad_conv_dw_kernel': 'grad_w', 'grad_conv_dw_bias': 'grad_w', 'grad_conv_ln_g': 'grad_w', 'grad_conv_ln_b': 'grad_w', 'grad_conv_w_proj': 'grad_w', 'grad_q_norm': 'grad_w', 'grad_k_norm': 'grad_w', 'grad_attn_sinks': 'grad_w', 'grad_rel_bias': 'grad_w', 'grad_attn_w_o': 'grad_w', 'grad_w_out': 'grad_w', 'grad_ffn2_norm': 'grad_w', 'grad_ffn2_w_in': 'grad_w', 'grad_ffn2_w_out': 'grad_w', 'delta_ffn1_norm': 'delta_w', 'delta_ffn1_w_in': 'delta_w', 'delta_ffn1_w_out': 'delta_w', 'delta_mix_norm': 'delta_w', 'delta_w_in': 'delta_w', 'delta_conv_dw_kernel': 'delta_w', 'delta_conv_dw_bias': 'delta_w', 'delta_conv_ln_g': 'delta_w', 'delta_conv_ln_b': 'delta_w', 'delta_conv_w_proj': 'delta_w', 'delta_q_norm': 'delta_w', 'delta_k_norm': 'delta_w', 'delta_attn_sinks': 'delta_w', 'delta_rel_bias': 'delta_w', 'delta_attn_w_o': 'delta_w', 'delta_w_out': 'delta_w', 'delta_ffn2_norm': 'delta_w', 'delta_ffn2_w_in': 'delta_w', 'delta_ffn2_w_out': 'delta_w', 'new_m_ffn1_norm': 'new_m', 'new_m_ffn1_w_in': 'new_m', 'new_m_ffn1_w_out': 'new_m', 'new_m_mix_norm': 'new_m', 'new_m_w_in': 'new_m', 'new_m_conv_dw_kernel': 'new_m', 'new_m_conv_dw_bias': 'new_m', 'new_m_conv_ln_g': 'new_m', 'new_m_conv_ln_b': 'new_m', 'new_m_conv_w_proj': 'new_m', 'new_m_q_norm': 'new_m', 'new_m_k_norm': 'new_m', 'new_m_attn_sinks': 'new_m', 'new_m_rel_bias': 'new_m', 'new_m_attn_w_o': 'new_m', 'new_m_w_out': 'new_m', 'new_m_ffn2_norm': 'new_m', 'new_m_ffn2_w_in': 'new_m', 'new_m_ffn2_w_out': 'new_m', 'new_v_ffn1_norm': 'new_v', 'new_v_ffn1_w_in': 'new_v', 'new_v_ffn1_w_out': 'new_v', 'new_v_mix_norm': 'new_v', 'new_v_w_in': 'new_v', 'new_v_conv_dw_kernel': 'new_v', 'new_v_conv_dw_bias': 'new_v', 'new_v_conv_ln_g': 'new_v', 'new_v_conv_ln_b': 'new_v', 'new_v_conv_w_proj': 'new_v', 'new_v_q_norm': 'new_v', 'new_v_k_norm': 'new_v', 'new_v_attn_sinks': 'new_v', 'new_v_rel_bias': 'new_v', 'new_v_attn_w_o': 'new_v', 'new_v_w_out': 'new_v', 'new_v_ffn2_norm': 'new_v', 'new_v_ffn2_w_in': 'new_v', 'new_v_ffn2_w_out': 'new_v'}


def _forward(args):
    return _fwd_reference(*[args[k] for k in FWD_PARAMS])


def _output_shape():
    out = _jax.eval_shape(lambda: _forward(_fwd_setup_inputs(0)))
    return out.shape, out.dtype

N_MICROBATCH = 1
ADAM_LR = 0.001
ADAM_B1 = 0.9
ADAM_B2 = 0.999
ADAM_EPS = 1e-08
ADAM_WD = 0.01
ADAM_STEP = 10
PER_EXAMPLE_BATCH_AXIS = {'x': 0, 'loss_target': 0}
SHARED_INPUTS = []
_WEIGHT_DTYPES = {'ffn1_norm': _jnp.float32, 'ffn1_w_in': _jnp.float32, 'ffn1_w_out': _jnp.float32, 'mix_norm': _jnp.float32, 'w_in': _jnp.float32, 'conv_dw_kernel': _jnp.float32, 'conv_dw_bias': _jnp.float32, 'conv_ln_g': _jnp.float32, 'conv_ln_b': _jnp.float32, 'conv_w_proj': _jnp.float32, 'q_norm': _jnp.float32, 'k_norm': _jnp.float32, 'attn_sinks': _jnp.float32, 'rel_bias': _jnp.float32, 'attn_w_o': _jnp.float32, 'w_out': _jnp.float32, 'ffn2_norm': _jnp.float32, 'ffn2_w_in': _jnp.float32, 'ffn2_w_out': _jnp.float32}
MOMENT_SCALE = {'ffn1_norm': 6.103824e+00, 'ffn1_w_in': 6.683505e-02, 'ffn1_w_out': 1.134921e-01, 'mix_norm': 3.940150e-01, 'w_in': 5.620276e-02, 'conv_dw_kernel': 1.400386e-01, 'conv_dw_bias': 2.087161e+00, 'conv_ln_g': 4.010857e+00, 'conv_ln_b': 2.746408e+00, 'conv_w_proj': 4.083576e-01, 'q_norm': 1.782725e+00, 'k_norm': 1.779693e+00, 'attn_sinks': 2.684736e-01, 'rel_bias': 4.341654e-02, 'attn_w_o': 3.792490e-02, 'w_out': 2.625969e-01, 'ffn2_norm': 6.091488e+00, 'ffn2_w_in': 7.011818e-02, 'ffn2_w_out': 1.110944e-01}


def _to_microbatches(a, axis):
    t = _jnp.moveaxis(a, axis, 0)
    t = t.reshape((N_MICROBATCH, t.shape[0] // N_MICROBATCH) + t.shape[1:])
    return _jnp.moveaxis(t, 1, axis + 1)


def setup_inputs(seed: int = 0) -> dict:
    inp = _fwd_setup_inputs(seed)
    key = _jax.random.fold_in(_jax.random.key(seed), 7919)
    shape, _ = _output_shape()
    out = dict(inp)
    out["loss_target"] = _jax.random.normal(_jax.random.fold_in(key, 0), shape, _jnp.float32)
    for i, name in enumerate(TWIN_WEIGHTS):
        w = inp[name].astype(_jnp.float32)
        if MOMENT_SCALE is None:
            s = _jnp.sqrt(_jnp.mean(_jnp.square(w)) + 1e-30)
        else:
            s = MOMENT_SCALE[name]
        km, kv = _jax.random.split(_jax.random.fold_in(key, i + 1))
        out[name] = w
        out["m_" + name] = s * _jax.random.normal(km, w.shape, _jnp.float32)
        out["v_" + name] = (s * s) * _jax.random.uniform(kv, w.shape, _jnp.float32, 0.5, 1.5)
    if N_MICROBATCH > 1:
        for name, axis in PER_EXAMPLE_BATCH_AXIS.items():
            out[name] = _to_microbatches(out[name], axis)
    return {'x': out['x'], 'ffn1_norm': out['ffn1_norm'], 'ffn1_w_in': out['ffn1_w_in'], 'ffn1_w_out': out['ffn1_w_out'], 'mix_norm': out['mix_norm'], 'w_in': out['w_in'], 'conv_dw_kernel': out['conv_dw_kernel'], 'conv_dw_bias': out['conv_dw_bias'], 'conv_ln_g': out['conv_ln_g'], 'conv_ln_b': out['conv_ln_b'], 'conv_w_proj': out['conv_w_proj'], 'q_norm': out['q_norm'], 'k_norm': out['k_norm'], 'attn_sinks': out['attn_sinks'], 'rel_bias': out['rel_bias'], 'attn_w_o': out['attn_w_o'], 'w_out': out['w_out'], 'ffn2_norm': out['ffn2_norm'], 'ffn2_w_in': out['ffn2_w_in'], 'ffn2_w_out': out['ffn2_w_out'], 'loss_target': out['loss_target'], 'm_ffn1_norm': out['m_ffn1_norm'], 'm_ffn1_w_in': out['m_ffn1_w_in'], 'm_ffn1_w_out': out['m_ffn1_w_out'], 'm_mix_norm': out['m_mix_norm'], 'm_w_in': out['m_w_in'], 'm_conv_dw_kernel': out['m_conv_dw_kernel'], 'm_conv_dw_bias': out['m_conv_dw_bias'], 'm_conv_ln_g': out['m_conv_ln_g'], 'm_conv_ln_b': out['m_conv_ln_b'], 'm_conv_w_proj': out['m_conv_w_proj'], 'm_q_norm': out['m_q_norm'], 'm_k_norm': out['m_k_norm'], 'm_attn_sinks': out['m_attn_sinks'], 'm_rel_bias': out['m_rel_bias'], 'm_attn_w_o': out['m_attn_w_o'], 'm_w_out': out['m_w_out'], 'm_ffn2_norm': out['m_ffn2_norm'], 'm_ffn2_w_in': out['m_ffn2_w_in'], 'm_ffn2_w_out': out['m_ffn2_w_out'], 'v_ffn1_norm': out['v_ffn1_norm'], 'v_ffn1_w_in': out['v_ffn1_w_in'], 'v_ffn1_w_out': out['v_ffn1_w_out'], 'v_mix_norm': out['v_mix_norm'], 'v_w_in': out['v_w_in'], 'v_conv_dw_kernel': out['v_conv_dw_kernel'], 'v_conv_dw_bias': out['v_conv_dw_bias'], 'v_conv_ln_g': out['v_conv_ln_g'], 'v_conv_ln_b': out['v_conv_ln_b'], 'v_conv_w_proj': out['v_conv_w_proj'], 'v_q_norm': out['v_q_norm'], 'v_k_norm': out['v_k_norm'], 'v_attn_sinks': out['v_attn_sinks'], 'v_rel_bias': out['v_rel_bias'], 'v_attn_w_o': out['v_attn_w_o'], 'v_w_out': out['v_w_out'], 'v_ffn2_norm': out['v_ffn2_norm'], 'v_ffn2_w_in': out['v_ffn2_w_in'], 'v_ffn2_w_out': out['v_ffn2_w_out']}


def _loss(weights, diff, rest, loss_target):
    with _jax.named_scope("forward"):
        args = {**rest, TWIN_DIFF_INPUT: diff, **{k: w.astype(_WEIGHT_DTYPES[k]) for k, w in weights.items()}}
        y = _forward(args)
    with _jax.named_scope("loss_head"):
        err = _jnp.square(y.astype(_jnp.float32) - loss_target)
        return 0.5 * _jnp.sum(_jnp.mean(err, axis=-1)) if err.ndim else 0.5 * err


def _adamw(w, g, m, v):
    m = ADAM_B1 * m + (1.0 - ADAM_B1) * g
    v = ADAM_B2 * v + (1.0 - ADAM_B2) * _jnp.square(g)
    m_hat = m / (1.0 - ADAM_B1 ** ADAM_STEP)
    v_hat = v / (1.0 - ADAM_B2 ** ADAM_STEP)
    delta = -ADAM_LR * (m_hat / (_jnp.sqrt(v_hat) + ADAM_EPS) + ADAM_WD * w)
    return delta, m, v


def reference(x, ffn1_norm, ffn1_w_in, ffn1_w_out, mix_norm, w_in, conv_dw_kernel, conv_dw_bias, conv_ln_g, conv_ln_b, conv_w_proj, q_norm, k_norm, attn_sinks, rel_bias, attn_w_o, w_out, ffn2_norm, ffn2_w_in, ffn2_w_out, loss_target, m_ffn1_norm, m_ffn1_w_in, m_ffn1_w_out, m_mix_norm, m_w_in, m_conv_dw_kernel, m_conv_dw_bias, m_conv_ln_g, m_conv_ln_b, m_conv_w_proj, m_q_norm, m_k_norm, m_attn_sinks, m_rel_bias, m_attn_w_o, m_w_out, m_ffn2_norm, m_ffn2_w_in, m_ffn2_w_out, v_ffn1_norm, v_ffn1_w_in, v_ffn1_w_out, v_mix_norm, v_w_in, v_conv_dw_kernel, v_conv_dw_bias, v_conv_ln_g, v_conv_ln_b, v_conv_w_proj, v_q_norm, v_k_norm, v_attn_sinks, v_rel_bias, v_attn_w_o, v_w_out, v_ffn2_norm, v_ffn2_w_in, v_ffn2_w_out):
    given = dict(x=x, ffn1_norm=ffn1_norm, ffn1_w_in=ffn1_w_in, ffn1_w_out=ffn1_w_out, mix_norm=mix_norm, w_in=w_in, conv_dw_kernel=conv_dw_kernel, conv_dw_bias=conv_dw_bias, conv_ln_g=conv_ln_g, conv_ln_b=conv_ln_b, conv_w_proj=conv_w_proj, q_norm=q_norm, k_norm=k_norm, attn_sinks=attn_sinks, rel_bias=rel_bias, attn_w_o=attn_w_o, w_out=w_out, ffn2_norm=ffn2_norm, ffn2_w_in=ffn2_w_in, ffn2_w_out=ffn2_w_out, loss_target=loss_target, m_ffn1_norm=m_ffn1_norm, m_ffn1_w_in=m_ffn1_w_in, m_ffn1_w_out=m_ffn1_w_out, m_mix_norm=m_mix_norm, m_w_in=m_w_in, m_conv_dw_kernel=m_conv_dw_kernel, m_conv_dw_bias=m_conv_dw_bias, m_conv_ln_g=m_conv_ln_g, m_conv_ln_b=m_conv_ln_b, m_conv_w_proj=m_conv_w_proj, m_q_norm=m_q_norm, m_k_norm=m_k_norm, m_attn_sinks=m_attn_sinks, m_rel_bias=m_rel_bias, m_attn_w_o=m_attn_w_o, m_w_out=m_w_out, m_ffn2_norm=m_ffn2_norm, m_ffn2_w_in=m_ffn2_w_in, m_ffn2_w_out=m_ffn2_w_out, v_ffn1_norm=v_ffn1_norm, v_ffn1_w_in=v_ffn1_w_in, v_ffn1_w_out=v_ffn1_w_out, v_mix_norm=v_mix_norm, v_w_in=v_w_in, v_conv_dw_kernel=v_conv_dw_kernel, v_conv_dw_bias=v_conv_dw_bias, v_conv_ln_g=v_conv_ln_g, v_conv_ln_b=v_conv_ln_b, v_conv_w_proj=v_conv_w_proj, v_q_norm=v_q_norm, v_k_norm=v_k_norm, v_attn_sinks=v_attn_sinks, v_rel_bias=v_rel_bias, v_attn_w_o=v_attn_w_o, v_w_out=v_w_out, v_ffn2_norm=v_ffn2_norm, v_ffn2_w_in=v_ffn2_w_in, v_ffn2_w_out=v_ffn2_w_out)
    weights = {n: given[n] for n in TWIN_WEIGHTS}
    shared = {n: given[n] for n in SHARED_INPUTS}
    per_example = {n: given[n] for n in ['x']}
    grad_fn = _jax.value_and_grad(_loss, argnums=(0, 1))

    def one_microbatch(ex, loss_target):
        ex = dict(ex)
        diff = ex.pop(TWIN_DIFF_INPUT)
        return grad_fn(weights, diff, {**shared, **ex}, loss_target)

    if N_MICROBATCH == 1:
        loss, (grad_w, grad_x) = one_microbatch(per_example, given["loss_target"])
    else:
        def body(carry, xs):
            loss_sum, grad_sum = carry
            l_k, (gw_k, gx_k) = one_microbatch(xs[0], xs[1])
            with _jax.named_scope("update"):
                return (loss_sum + l_k, _jax.tree.map(_jnp.add, grad_sum, gw_k)), gx_k

        init = (_jnp.zeros((), _jnp.float32), _jax.tree.map(_jnp.zeros_like, weights))
        (loss, grad_w), grad_x = _jax.lax.scan(body, init, (per_example, given["loss_target"]))
    with _jax.named_scope("update"):
        delta_w, new_m, new_v = {}, {}, {}
        for n in TWIN_WEIGHTS:
            delta_w[n], new_m[n], new_v[n] = _adamw(weights[n], grad_w[n], given["m_" + n], given["v_" + n])
    return (loss, grad_x, *[grad_w[n] for n in TWIN_WEIGHTS], *[delta_w[n] for n in TWIN_WEIGHTS],
            *[new_m[n] for n in TWIN_WEIGHTS], *[new_v[n] for n in TWIN_WEIGHTS])
```

```python
import functools
import math

import jax
import jax.numpy as jnp
from jax import lax
from jax.experimental import pallas as pl
from jax.experimental.pallas import tpu as pltpu

F32 = jnp.float32
BF16 = jnp.bfloat16
MESH = pl.DeviceIdType.MESH

EPS = 1e-6
D_MODEL = 1024
D_FF = 2816
N_CHIPS = 4
SHARD_W = 2 * D_FF // N_CHIPS
HEAD_DIM = 64
N_Q_HEADS = 16
N_KV_HEADS = 4
GROUP = N_Q_HEADS // N_KV_HEADS
BLOCK = 128
QROWS = GROUP * BLOCK
N_BUCKETS = 32
MAX_DISTANCE = 128
CONV_WIDTH = 31
CONV_PAD = 32
NEG = float(jnp.finfo(jnp.float32).min)

ADAM_LR = 0.001
ADAM_B1 = 0.9
ADAM_B2 = 0.999
ADAM_EPS = 1e-08
ADAM_WD = 0.01
ADAM_STEP = 10

VMEM_LIMIT_BYTES = 56 * 1024 * 1024
ROW_TILE = 512
CONV_TILE = 256
CONV_ROWS = 32
LANES = 128

COL_CONV_A, COL_CONV_G, COL_Q, COL_K, COL_V, COL_GC, COL_GA = 0, 1024, 2048, 3072, 3328, 3584, 4608
IN_W = 5632


def _params(sem, vmem=VMEM_LIMIT_BYTES):
    return pltpu.CompilerParams(dimension_semantics=sem, vmem_limit_bytes=vmem)


def _sigmoid(x):
    return 1.0 / (1.0 + jnp.exp(-x))


def _dot(a, b, trans_a=False, trans_b=False, precision=None):
    dn = (((0,) if trans_a else (1,), (1,) if trans_b else (0,)), ((), ()))
    return lax.dot_general(a, b, dn, preferred_element_type=F32, precision=precision)


def _mm(name, grid, a, a_spec, b, b_spec, acc_shape, *, trans_a=False, trans_b=False, a_pre=None, b_pre=None,
        extras=(), extra_specs=(), out_shape, out_specs, epilogue, sem=("parallel", "parallel", "arbitrary")):
    n_k = grid[2]
    n_extra = len(extras)
    n_out = len(out_shape)

    def body(a_ref, b_ref, *rest):
        ex = rest[:n_extra]
        outs = rest[n_extra:n_extra + n_out]
        ids = (pl.program_id(0), pl.program_id(1), pl.program_id(2))
        av = a_ref[...]
        bv = b_ref[...]
        if a_pre is not None:
            av = a_pre(av)
        if b_pre is not None:
            bv = b_pre(bv)
        part = _dot(av, bv, trans_a, trans_b)
        if n_k == 1:
            epilogue(part, ex, outs, ids)
        else:
            acc = rest[-1]

            @pl.when(ids[2] == 0)
            def _():
                acc[...] = part

            @pl.when(ids[2] > 0)
            def _():
                acc[...] += part

            @pl.when(ids[2] == n_k - 1)
            def _():
                epilogue(acc[...], ex, outs, ids)

    scratch = [] if n_k == 1 else [pltpu.VMEM(acc_shape, F32)]
    return pl.pallas_call(
        body, name=name, grid=grid,
        in_specs=[a_spec, b_spec, *extra_specs],
        out_specs=list(out_specs), out_shape=list(out_shape),
        scratch_shapes=scratch, compiler_params=_params(sem),
    )(a, b, *extras)


def _half_bf16(v):
    return (0.5 * v).astype(BF16)


def _to_bf16(v):
    return v.astype(BF16)


def _rmsnorm_fwd(name, x, g):
    t, d = x.shape
    tm = min(ROW_TILE, t)

    def body(x_ref, g_ref, o_ref):
        xv = x_ref[...]
        r = lax.rsqrt(jnp.mean(xv * xv, axis=-1, keepdims=True) + EPS)
        o_ref[...] = (xv * r * g_ref[...]).astype(BF16)

    return pl.pallas_call(
        body, name=name, grid=(t // tm,),
        in_specs=[pl.BlockSpec((tm, d), lambda i: (i, 0)), pl.BlockSpec((1, d), lambda i: (0, 0))],
        out_specs=pl.BlockSpec((tm, d), lambda i: (i, 0)),
        out_shape=jax.ShapeDtypeStruct((t, d), BF16),
        compiler_params=_params(("parallel",)),
    )(x, g)


def _rms_bwd_epilogue(acc, ex, outs, ids):
    x_ref, g_ref, dres_ref = ex
    out_ref, dg_ref = outs
    xv = x_ref[...]
    r = lax.rsqrt(jnp.mean(xv * xv, axis=-1, keepdims=True) + EPS)
    w = acc * g_ref[...]
    dx = r * w - xv * (r * r * r) * jnp.mean(xv * w, axis=-1, keepdims=True)
    out_ref[...] = dres_ref[...] + dx
    part = jnp.sum(acc * (xv * r), axis=0, keepdims=True)

    @pl.when(ids[0] == 0)
    def _():
        dg_ref[...] = part

    @pl.when(ids[0] > 0)
    def _():
        dg_ref[...] += part


def _ffn_in(name, n, w_in4):
    t, d = n.shape
    tm = min(ROW_TILE, t)

    def body(n_ref, wa_ref, wb_ref, ab_ref, h_ref):
        nv = n_ref[...]
        a = _dot(nv, wa_ref[...])
        b = _dot(nv, wb_ref[...])
        h_ref[...] = (a * _sigmoid(a) * b).astype(BF16)
        ab_ref[0] = a.astype(BF16)
        ab_ref[1] = b.astype(BF16)

    return pl.pallas_call(
        body, name=name, grid=(2, t // tm),
        in_specs=[pl.BlockSpec((tm, d), lambda j, i: (i, 0)),
                  pl.BlockSpec((None, d, SHARD_W), lambda j, i: (j, 0, 0)),
                  pl.BlockSpec((None, d, SHARD_W), lambda j, i: (j + 2, 0, 0))],
        out_specs=[pl.BlockSpec((2, tm, SHARD_W), lambda j, i: (0, i, j)),
                   pl.BlockSpec((tm, SHARD_W), lambda j, i: (i, j))],
        out_shape=[jax.ShapeDtypeStruct((2, t, D_FF), BF16), jax.ShapeDtypeStruct((t, D_FF), BF16)],
        compiler_params=_params(("parallel", "parallel")),
    )(n, w_in4, w_in4)


def _mm_residual(name, a, w, res, scale):
    t, k = a.shape
    n = w.shape[1]
    tm = min(ROW_TILE, t)

    def epilogue(acc, ex, outs, ids):
        outs[0][...] = ex[0][...] + scale * acc

    return _mm(name, (t // tm, 1, 1), a, pl.BlockSpec((tm, k), lambda i, j, kk: (i, 0)),
               w, pl.BlockSpec((k, n), lambda i, j, kk: (0, 0)), (tm, n),
               extras=(res,), extra_specs=(pl.BlockSpec((tm, n), lambda i, j, kk: (i, 0)),),
               out_shape=(jax.ShapeDtypeStruct((t, n), F32),),
               out_specs=(pl.BlockSpec((tm, n), lambda i, j, kk: (i, 0)),), epilogue=epilogue)[0]


def _ffn_fwd(tag, x, g, w_in4, w_out):
    n = _rmsnorm_fwd(tag + "_norm", x, g)
    ab, h = _ffn_in(tag + "_in", n, w_in4)
    y = _mm_residual(tag + "_out", h, w_out, x, 0.5)
    return y, (n, ab, h)


def _ffn_bwd(tag, dres, x, g, saved, w_in4, w_out):
    n, ab, h = saved
    t, d = x.shape
    tm = min(ROW_TILE, t)
    tk = min(ROW_TILE, t)
    half_w = SHARD_W

    def dact_epilogue(acc, ex, outs, ids):
        a = ex[0][0].astype(F32)
        b = ex[0][1].astype(F32)
        sig = _sigmoid(a)
        outs[0][0] = (acc * b * (sig * (1.0 + a * (1.0 - sig)))).astype(BF16)
        outs[0][1] = (acc * (a * sig)).astype(BF16)

    du = _mm(tag + "_dact", (2, t // tm, 1),
             dres, pl.BlockSpec((tm, d), lambda j, i, kk: (i, 0)),
             w_out, pl.BlockSpec((half_w, d), lambda j, i, kk: (j, 0)), (tm, half_w),
             trans_b=True, a_pre=_half_bf16,
             extras=(ab,), extra_specs=(pl.BlockSpec((2, tm, half_w), lambda j, i, kk: (0, i, j)),),
             out_shape=(jax.ShapeDtypeStruct((2, t, D_FF), BF16),),
             out_specs=(pl.BlockSpec((2, tm, half_w), lambda j, i, kk: (0, i, j)),),
             epilogue=dact_epilogue)[0]

    def store_epilogue(acc, ex, outs, ids):
        outs[0][...] = acc

    dw_out = _mm(tag + "_dwout", (2, 1, t // tk),
                 h, pl.BlockSpec((tk, half_w), lambda i, j, kk: (kk, i)),
                 dres, pl.BlockSpec((tk, d), lambda i, j, kk: (kk, 0)), (half_w, d),
                 trans_a=True, b_pre=_half_bf16,
                 out_shape=(jax.ShapeDtypeStruct((D_FF, d), F32),),
                 out_specs=(pl.BlockSpec((half_w, d), lambda i, j, kk: (i, 0)),),
                 epilogue=store_epilogue)[0]

    dw_in4 = _mm(tag + "_dwin", (1, N_CHIPS, t // tk),
                 n, pl.BlockSpec((tk, d), lambda i, j, kk: (kk, 0)),
                 du, pl.BlockSpec((None, tk, SHARD_W), lambda i, j, kk: (j // 2, kk, j % 2)), (d, SHARD_W),
                 trans_a=True,
                 out_shape=(jax.ShapeDtypeStruct((N_CHIPS, d, SHARD_W), F32),),
                 out_specs=(pl.BlockSpec((None, d, SHARD_W), lambda i, j, kk: (j, 0, 0)),),
                 epilogue=store_epilogue)[0]

    dx, dg = _mm(tag + "_dn", (t // tm, 1, N_CHIPS),
                 du, pl.BlockSpec((None, tm, SHARD_W), lambda i, j, kk: (kk // 2, i, kk % 2)),
                 w_in4, pl.BlockSpec((None, d, SHARD_W), lambda i, j, kk: (kk, 0, 0)), (tm, d),
                 trans_b=True,
                 extras=(x, g, dres),
                 extra_specs=(pl.BlockSpec((tm, d), lambda i, j, kk: (i, 0)),
                              pl.BlockSpec((1, d), lambda i, j, kk: (0, 0)),
                              pl.BlockSpec((tm, d), lambda i, j, kk: (i, 0))),
                 out_shape=(jax.ShapeDtypeStruct((t, d), F32), jax.ShapeDtypeStruct((1, d), F32)),
                 out_specs=(pl.BlockSpec((tm, d), lambda i, j, kk: (i, 0)),
                            pl.BlockSpec((1, d), lambda i, j, kk: (0, 0))),
                 epilogue=_rms_bwd_epilogue, sem=("arbitrary", "arbitrary", "arbitrary"))
    return dx, dw_in4, dw_out, dg


def _loss_head(y, target):
    t, d = y.shape
    tm = min(ROW_TILE, t)

    def body(y_ref, t_ref, dy_ref, loss_ref):
        diff = y_ref[...] - t_ref[...]
        dy_ref[...] = diff * (1.0 / d)
        part = jnp.full((8, LANES), 0.5 / d * jnp.sum(diff * diff), F32)
        i = pl.program_id(0)

        @pl.when(i == 0)
        def _():
            loss_ref[...] = part

        @pl.when(i > 0)
        def _():
            loss_ref[...] += part

    return pl.pallas_call(
        body, name="loss_head", grid=(t // tm,),
        in_specs=[pl.BlockSpec((tm, d), lambda i: (i, 0)), pl.BlockSpec((tm, d), lambda i: (i, 0))],
        out_specs=[pl.BlockSpec((tm, d), lambda i: (i, 0)), pl.BlockSpec((8, LANES), lambda i: (0, 0))],
        out_shape=[jax.ShapeDtypeStruct((t, d), F32), jax.ShapeDtypeStruct((8, LANES), F32)],
        compiler_params=_params(("arbitrary",)),
    )(y, target)


def _conv_fill(zp_ref, a_ref, g_ref, ah_ref, gh_ref, i):
    zh = ah_ref[...].astype(F32) * _sigmoid(gh_ref[...].astype(F32))
    zp_ref[pl.ds(0, CONV_PAD), :] = jnp.where(i > 0, zh, 0.0)
    zp_ref[pl.ds(CONV_PAD, a_ref.shape[0]), :] = a_ref[...].astype(F32) * _sigmoid(g_ref[...].astype(F32))


def _conv_taps(zp_ref, z1_ref, dw_ref, bias_ref, tm, ch):
    first = CONV_PAD - (CONV_WIDTH - 1)
    for cc in range(ch // LANES):
        lanes = pl.ds(cc * LANES, LANES)
        w = [dw_ref[pl.ds(j, 1), lanes] for j in range(CONV_WIDTH)]
        bias = bias_ref[:, lanes]
        for rb in range(tm // CONV_ROWS):
            acc = jnp.broadcast_to(bias, (CONV_ROWS, LANES))
            for j in range(CONV_WIDTH):
                acc = acc + w[j] * zp_ref[pl.ds(rb * CONV_ROWS + first + j, CONV_ROWS), lanes]
            z1_ref[pl.ds(rb * CONV_ROWS, CONV_ROWS), lanes] = acc


def _conv_specs(tm, ch):
    per = tm // CONV_PAD
    cb = COL_CONV_G // ch
    return [pl.BlockSpec((tm, ch), lambda i: (i, 0)),
            pl.BlockSpec((tm, ch), lambda i: (i, cb)),
            pl.BlockSpec((CONV_PAD, ch), lambda i: (jnp.maximum(i * per - 1, 0), 0)),
            pl.BlockSpec((CONV_PAD, ch), lambda i: (jnp.maximum(i * per - 1, 0), cb))]


def _conv_fwd(p, dw, bias, ln_g, ln_b):
    t = p.shape[0]
    ch = D_MODEL
    tm = min(CONV_TILE, t)

    def body(a_ref, g_ref, ah_ref, gh_ref, dw_ref, bias_ref, lg_ref, lb_ref, o_ref, zp_ref, z1_ref):
        i = pl.program_id(0)
        _conv_fill(zp_ref, a_ref, g_ref, ah_ref, gh_ref, i)
        _conv_taps(zp_ref, z1_ref, dw_ref, bias_ref, tm, ch)
        z1 = z1_ref[...]
        mu = jnp.mean(z1, axis=-1, keepdims=True)
        zc = z1 - mu
        rs = lax.rsqrt(jnp.mean(zc * zc, axis=-1, keepdims=True) + EPS)
        z2 = zc * rs * lg_ref[...] + lb_ref[...]
        o_ref[...] = (z2 * _sigmoid(z2)).astype(BF16)

    vec = pl.BlockSpec((1, ch), lambda i: (0, 0))
    return pl.pallas_call(
        body, name="conv_fwd", grid=(t // tm,),
        in_specs=_conv_specs(tm, ch) + [pl.BlockSpec((CONV_PAD, ch), lambda i: (0, 0)), vec, vec, vec],
        out_specs=pl.BlockSpec((tm, ch), lambda i: (i, 0)),
        out_shape=jax.ShapeDtypeStruct((t, ch), BF16),
        scratch_shapes=[pltpu.VMEM((CONV_PAD + tm, ch), F32), pltpu.VMEM((tm, ch), F32)],
        compiler_params=_params(("parallel",)),
    )(p, p, p, p, dw, bias, ln_g, ln_b)


def _conv_bwd_ln(p, dz3, dw, bias, ln_g, ln_b):
    t = p.shape[0]
    ch = D_MODEL
    tm = min(CONV_TILE, t)
    first = CONV_PAD - (CONV_WIDTH - 1)

    def body(a_ref, g_ref, ah_ref, gh_ref, dz3_ref, dw_ref, bias_ref, lg_ref, lb_ref,
             dz1_ref, ddw_ref, dbias_ref, dlg_ref, dlb_ref, zp_ref, z1_ref):
        i = pl.program_id(0)
        _conv_fill(zp_ref, a_ref, g_ref, ah_ref, gh_ref, i)
        _conv_taps(zp_ref, z1_ref, dw_ref, bias_ref, tm, ch)
        z1 = z1_ref[...]
        mu = jnp.mean(z1, axis=-1, keepdims=True)
        zc = z1 - mu
        rs = lax.rsqrt(jnp.mean(zc * zc, axis=-1, keepdims=True) + EPS)
        xh = zc * rs
        z2 = xh * lg_ref[...] + lb_ref[...]
        sig = _sigmoid(z2)
        dz2 = dz3_ref[...].astype(F32) * (sig * (1.0 + z2 * (1.0 - sig)))
        dxh = dz2 * lg_ref[...]
        dz1 = rs * (dxh - jnp.mean(dxh, axis=-1, keepdims=True) - xh * jnp.mean(dxh * xh, axis=-1, keepdims=True))
        dz1_ref[...] = dz1

        @pl.when(i == 0)
        def _():
            ddw_ref[...] = jnp.zeros_like(ddw_ref)
            dbias_ref[...] = jnp.zeros_like(dbias_ref)
            dlg_ref[...] = jnp.zeros_like(dlg_ref)
            dlb_ref[...] = jnp.zeros_like(dlb_ref)

        dlg_ref[...] += jnp.sum(dz2 * xh, axis=0, keepdims=True)
        dlb_ref[...] += jnp.sum(dz2, axis=0, keepdims=True)
        dbias_ref[...] += jnp.sum(dz1, axis=0, keepdims=True)
        for cc in range(ch // LANES):
            lanes = pl.ds(cc * LANES, LANES)
            accs = [jnp.zeros((8, LANES), F32) for _ in range(CONV_WIDTH)]
            for rb in range(tm // CONV_ROWS):
                dzc = dz1_ref[pl.ds(rb * CONV_ROWS, CONV_ROWS), lanes]
                for j in range(CONV_WIDTH):
                    prod = dzc * zp_ref[pl.ds(rb * CONV_ROWS + first + j, CONV_ROWS), lanes]
                    accs[j] = accs[j] + jnp.sum(prod.reshape(CONV_ROWS // 8, 8, LANES), axis=0)
            for j in range(CONV_WIDTH):
                ddw_ref[pl.ds(j, 1), lanes] += jnp.sum(accs[j], axis=0, keepdims=True)

    vec = pl.BlockSpec((1, ch), lambda i: (0, 0))
    return pl.pallas_call(
        body, name="conv_bwd_ln", grid=(t // tm,),
        in_specs=_conv_specs(tm, ch) + [pl.BlockSpec((tm, ch), lambda i: (i, 0)),
                                        pl.BlockSpec((CONV_PAD, ch), lambda i: (0, 0)), vec, vec, vec],
        out_specs=[pl.BlockSpec((tm, ch), lambda i: (i, 0)), pl.BlockSpec((CONV_PAD, ch), lambda i: (0, 0)), vec, vec, vec],
        out_shape=[jax.ShapeDtypeStruct((t, ch), F32), jax.ShapeDtypeStruct((CONV_PAD, ch), F32)]
        + [jax.ShapeDtypeStruct((1, ch), F32)] * 3,
        scratch_shapes=[pltpu.VMEM((CONV_PAD + tm, ch), F32), pltpu.VMEM((tm, ch), F32)],
        compiler_params=_params(("arbitrary",)),
    )(p, p, p, p, dz3, dw, bias, ln_g, ln_b)


def _conv_bwd_glu(p, dz1, dw):
    t = p.shape[0]
    ch = D_MODEL
    tm = min(CONV_TILE, t)
    per = tm // CONV_PAD
    n_halo = t // CONV_PAD
    cb = COL_CONV_G // ch

    def body(a_ref, g_ref, dz_ref, dzn_ref, dw_ref, o_ref, zp_ref, z0_ref):
        i = pl.program_id(0)
        zp_ref[pl.ds(0, tm), :] = dz_ref[...]
        zp_ref[pl.ds(tm, CONV_PAD), :] = jnp.where(i < t // tm - 1, dzn_ref[...], 0.0)
        for cc in range(ch // LANES):
            lanes = pl.ds(cc * LANES, LANES)
            w = [dw_ref[pl.ds(j, 1), lanes] for j in range(CONV_WIDTH)]
            for rb in range(tm // CONV_ROWS):
                acc = jnp.zeros((CONV_ROWS, LANES), F32)
                for j in range(CONV_WIDTH):
                    acc = acc + w[j] * zp_ref[pl.ds(rb * CONV_ROWS + (CONV_WIDTH - 1 - j), CONV_ROWS), lanes]
                z0_ref[pl.ds(rb * CONV_ROWS, CONV_ROWS), lanes] = acc
        dz0 = z0_ref[...]
        a = a_ref[...].astype(F32)
        sig = _sigmoid(g_ref[...].astype(F32))
        o_ref[:, pl.ds(0, ch)] = (dz0 * sig).astype(BF16)
        o_ref[:, pl.ds(ch, ch)] = (dz0 * a * sig * (1.0 - sig)).astype(BF16)

    return pl.pallas_call(
        body, name="conv_bwd_glu", grid=(t // tm,),
        in_specs=[pl.BlockSpec((tm, ch), lambda i: (i, 0)), pl.BlockSpec((tm, ch), lambda i: (i, cb)),
                  pl.BlockSpec((tm, ch), lambda i: (i, 0)),
                  pl.BlockSpec((CONV_PAD, ch), lambda i: (jnp.minimum((i + 1) * per, n_halo - 1), 0)),
                  pl.BlockSpec((CONV_PAD, ch), lambda i: (0, 0))],
        out_specs=pl.BlockSpec((tm, 2 * ch), lambda i: (i, 0)),
        out_shape=jax.ShapeDtypeStruct((t, 2 * ch), BF16),
        scratch_shapes=[pltpu.VMEM((tm + CONV_PAD, ch), F32), pltpu.VMEM((tm, ch), F32)],
        compiler_params=_params(("parallel",)),
    )(p, p, dz1, dz1, dw)


def _bucket_onehot():
    qi = jnp.arange(BLOCK, dtype=jnp.int32)[:, None]
    kj = jnp.arange(2 * BLOCK, dtype=jnp.int32)[None, :]
    dist = jnp.maximum(qi + BLOCK - kj, 0)
    max_exact = N_BUCKETS // 2
    dflt = jnp.maximum(dist, 1).astype(F32)
    large = max_exact + (jnp.log(dflt / max_exact) / math.log(MAX_DISTANCE / max_exact)
                         * (N_BUCKETS - max_exact)).astype(jnp.int32)
    large = jnp.minimum(large, N_BUCKETS - 1)
    bucket = jnp.where(dist < max_exact, dist, large)
    onehot = bucket[None] == jnp.arange(N_BUCKETS, dtype=jnp.int32)[:, None, None]
    return onehot.astype(F32).reshape(N_BUCKETS, BLOCK * 2 * BLOCK)


def _bias_table(rel_bias_t, onehot):
    def body(r_ref, oh_ref, o_ref):
        o_ref[...] = _dot(r_ref[...], oh_ref[...], precision=lax.Precision.HIGHEST)

    n = onehot.shape[1]
    tn = 4096
    return pl.pallas_call(
        body, name="bias_table", grid=(n // tn,),
        in_specs=[pl.BlockSpec((N_Q_HEADS, N_BUCKETS), lambda i: (0, 0)), pl.BlockSpec((N_BUCKETS, tn), lambda i: (0, i))],
        out_specs=pl.BlockSpec((N_Q_HEADS, tn), lambda i: (0, i)),
        out_shape=jax.ShapeDtypeStruct((N_Q_HEADS, n), F32),
        compiler_params=_params(("parallel",)),
    )(rel_bias_t, onehot)


def _bias_table_bwd(dbias, onehot):
    n = onehot.shape[1]
    tn = 4096

    def body(d_ref, oh_ref, o_ref):
        part = _dot(d_ref[...], oh_ref[...], trans_b=True, precision=lax.Precision.HIGHEST)
        i = pl.program_id(0)

        @pl.when(i == 0)
        def _():
            o_ref[...] = part

        @pl.when(i > 0)
        def _():
            o_ref[...] += part

    return pl.pallas_call(
        body, name="bias_table_bwd", grid=(n // tn,),
        in_specs=[pl.BlockSpec((N_Q_HEADS, tn), lambda i: (0, i)), pl.BlockSpec((N_BUCKETS, tn), lambda i: (0, i))],
        out_specs=pl.BlockSpec((N_Q_HEADS, N_BUCKETS), lambda i: (0, 0)),
        out_shape=jax.ShapeDtypeStruct((N_Q_HEADS, N_BUCKETS), F32),
        compiler_params=_params(("arbitrary",)),
    )(dbias, onehot)


def _attn_probs(q_ref, kp_ref, kc_ref, gq_ref, gk_ref, sink_ref, bias_ref, n):
    qf = q_ref[...].astype(F32)
    rq = lax.rsqrt(jnp.mean(qf * qf, axis=-1, keepdims=True) + EPS)
    qn = qf * rq * gq_ref[...]
    kf = jnp.concatenate([kp_ref[...], kc_ref[...]], axis=0).astype(F32)
    rk = lax.rsqrt(jnp.mean(kf * kf, axis=-1, keepdims=True) + EPS)
    kn = kf * rk * gk_ref[...]
    s = _dot(qn.astype(BF16), kn.astype(BF16), trans_b=True) * (1.0 / math.sqrt(HEAD_DIM)) + bias_ref[...]
    row = lax.broadcasted_iota(jnp.int32, (QROWS, 2 * BLOCK), 0) & (BLOCK - 1)
    col = lax.broadcasted_iota(jnp.int32, (QROWS, 2 * BLOCK), 1)
    dist = row + BLOCK - col
    valid = (dist >= 0) & (dist < BLOCK) & ((col >= BLOCK) | (n > 0))
    s = jnp.where(valid, s, NEG)
    sink = sink_ref[...]
    m = jnp.maximum(jnp.max(s, axis=-1, keepdims=True), sink)
    p = jnp.exp(s - m)
    es = jnp.exp(sink - m)
    inv = 1.0 / (jnp.sum(p, axis=-1, keepdims=True) + es)
    return qf, rq, qn, kf, rk, kn, p * inv, es * inv


def _attn_specs(nb):
    qspec = pl.BlockSpec((None, None, QROWS, HEAD_DIM), lambda h, n: (h, n, 0, 0))
    kprev = pl.BlockSpec((None, BLOCK, HEAD_DIM), lambda h, n: (h, jnp.maximum(n - 1, 0), 0))
    kcur = pl.BlockSpec((None, BLOCK, HEAD_DIM), lambda h, n: (h, n, 0))
    gain = pl.BlockSpec((1, HEAD_DIM), lambda h, n: (0, 0))
    sink = pl.BlockSpec((None, QROWS, 1), lambda h, n: (h, 0, 0))
    bias = pl.BlockSpec((None, QROWS, 2 * BLOCK), lambda h, n: (h, 0, 0))
    return qspec, kprev, kcur, gain, sink, bias


def _attn_fwd(q4, k3, v3, gq, gk, sink_rows, bias):
    nb = q4.shape[1]
    qspec, kprev, kcur, gain, sink, bspec = _attn_specs(nb)

    def body(q_ref, kp_ref, kc_ref, vp_ref, vc_ref, gq_ref, gk_ref, sink_ref, bias_ref, o_ref):
        n = pl.program_id(1)
        pn = _attn_probs(q_ref, kp_ref, kc_ref, gq_ref, gk_ref, sink_ref, bias_ref, n)[6]
        v = jnp.concatenate([vp_ref[...], vc_ref[...]], axis=0)
        o_ref[...] = _dot(pn.astype(BF16), v).astype(BF16)

    return pl.pallas_call(
        body, name="attn_fwd", grid=(N_KV_HEADS, nb),
        in_specs=[qspec, kprev, kcur, kprev, kcur, gain, gain, sink, bspec],
        out_specs=qspec, out_shape=jax.ShapeDtypeStruct(q4.shape, BF16),
        compiler_params=_params(("parallel", "parallel")),
    )(q4, k3, k3, v3, v3, gq, gk, sink_rows, bias)


def _attn_bwd(q4, k3, v3, do4, gq, gk, sink_rows, bias):
    nb = q4.shape[1]
    t = k3.shape[1]
    qspec, kprev, kcur, gain, sink, bspec = _attn_specs(nb)
    scale = 1.0 / math.sqrt(HEAD_DIM)

    def rms_bwd(dn, xf, r, g):
        w = dn * g
        dx = r * w - xf * (r * r * r) * jnp.mean(xf * w, axis=-1, keepdims=True)
        return dx, jnp.sum(dn * (xf * r), axis=0, keepdims=True)

    def body(q_ref, kp_ref, kc_ref, vp_ref, vc_ref, do_ref, gq_ref, gk_ref, sink_ref, bias_ref,
             dq_ref, dk_ref, dv_ref, dbias_ref, dsink_ref, dgq_ref, dgk_ref):
        n = pl.program_id(1)
        qf, rq, qn, kf, rk, kn, pn, psink = _attn_probs(q_ref, kp_ref, kc_ref, gq_ref, gk_ref, sink_ref, bias_ref, n)
        do = do_ref[...]
        v = jnp.concatenate([vp_ref[...], vc_ref[...]], axis=0)
        dv_win = _dot(pn.astype(BF16), do, trans_a=True)
        dp = _dot(do, v, trans_b=True)
        delta = jnp.sum(pn * dp, axis=-1, keepdims=True)
        ds = pn * (dp - delta)
        dsink = jnp.sum((-psink * delta).reshape(GROUP, BLOCK, 1), axis=1)
        dsc = (ds * scale).astype(BF16)
        dqn = _dot(dsc, kn.astype(BF16))
        dkn = _dot(dsc, qn.astype(BF16), trans_a=True)
        dq, dgq = rms_bwd(dqn, qf, rq, gq_ref[...])
        dk_win, dgk = rms_bwd(dkn, kf, rk, gk_ref[...])
        dq_ref[...] = dq.astype(BF16)

        @pl.when(n == 0)
        def _():
            dbias_ref[...] = ds
            dsink_ref[...] = dsink
            dgq_ref[...] = dgq
            dgk_ref[...] = dgk

        @pl.when(n > 0)
        def _():
            dbias_ref[...] += ds
            dsink_ref[...] += dsink
            dgq_ref[...] += dgq
            dgk_ref[...] += dgk
            prev = pl.ds(pl.multiple_of((n - 1) * BLOCK, BLOCK), BLOCK)
            dk_ref[prev, :] += dk_win[:BLOCK]
            dv_ref[prev, :] += dv_win[:BLOCK]

        cur = pl.ds(pl.multiple_of(n * BLOCK, BLOCK), BLOCK)
        dk_ref[cur, :] = dk_win[BLOCK:]
        dv_ref[cur, :] = dv_win[BLOCK:]

    kv_out = pl.BlockSpec((None, t, HEAD_DIM), lambda h, n: (h, 0, 0))
    gain_out = pl.BlockSpec((None, 1, HEAD_DIM), lambda h, n: (h, 0, 0))
    return pl.pallas_call(
        body, name="attn_bwd", grid=(N_KV_HEADS, nb),
        in_specs=[qspec, kprev, kcur, kprev, kcur, qspec, gain, gain, sink, bspec],
        out_specs=[qspec, kv_out, kv_out, bspec,
                   pl.BlockSpec((None, GROUP, 1), lambda h, n: (h, 0, 0)), gain_out, gain_out],
        out_shape=[jax.ShapeDtypeStruct(q4.shape, BF16),
                   jax.ShapeDtypeStruct((N_KV_HEADS, t, HEAD_DIM), F32),
                   jax.ShapeDtypeStruct((N_KV_HEADS, t, HEAD_DIM), F32),
                   jax.ShapeDtypeStruct((N_KV_HEADS, QROWS, 2 * BLOCK), F32),
                   jax.ShapeDtypeStruct((N_KV_HEADS, GROUP, 1), F32),
                   jax.ShapeDtypeStruct((N_KV_HEADS, 1, HEAD_DIM), F32),
                   jax.ShapeDtypeStruct((N_KV_HEADS, 1, HEAD_DIM), F32)],
        compiler_params=_params(("arbitrary", "arbitrary")),
    )(q4, k3, k3, v3, v3, do4, gq, gk, sink_rows, bias)


def _q_to_heads(cols, t):
    nb = t // BLOCK
    return cols.reshape(nb, BLOCK, N_KV_HEADS, GROUP, HEAD_DIM).transpose(2, 0, 3, 1, 4).reshape(N_KV_HEADS, nb, QROWS, HEAD_DIM)


def _q_from_heads(q4, t):
    nb = t // BLOCK
    return q4.reshape(N_KV_HEADS, nb, GROUP, BLOCK, HEAD_DIM).transpose(1, 3, 0, 2, 4).reshape(t, N_Q_HEADS * HEAD_DIM)


def _kv_to_heads(cols, t):
    return cols.reshape(t, N_KV_HEADS, HEAD_DIM).transpose(1, 0, 2)


def _kv_from_heads(k3, t):
    return k3.transpose(1, 0, 2).reshape(t, N_KV_HEADS * HEAD_DIM)


GATE_TILE = 512


def _merge_fwd(z3, o, p, w_proj, w_o):
    t, d = z3.shape
    tm = min(ROW_TILE, t)
    tn = GATE_TILE

    def body(z_ref, o_ref, gc_ref, ga_ref, wp_ref, wo_ref, m_ref, a_ref, b_ref):
        a = _dot(z_ref[...], wp_ref[...])
        b = _dot(o_ref[...], wo_ref[...])
        m_ref[...] = (_sigmoid(gc_ref[...].astype(F32)) * a + _sigmoid(ga_ref[...].astype(F32)) * b).astype(BF16)
        a_ref[...] = a.astype(BF16)
        b_ref[...] = b.astype(BF16)

    row = pl.BlockSpec((tm, d), lambda i, j: (i, 0))
    wspec = pl.BlockSpec((d, tn), lambda i, j: (0, j))
    ospec = pl.BlockSpec((tm, tn), lambda i, j: (i, j))
    return pl.pallas_call(
        body, name="merge_fwd", grid=(t // tm, d // tn),
        in_specs=[row, row,
                  pl.BlockSpec((tm, tn), lambda i, j: (i, COL_GC // tn + j)),
                  pl.BlockSpec((tm, tn), lambda i, j: (i, COL_GA // tn + j)), wspec, wspec],
        out_specs=[ospec, ospec, ospec],
        out_shape=[jax.ShapeDtypeStruct((t, d), BF16)] * 3,
        compiler_params=_params(("parallel", "parallel")),
    )(z3, o, p, p, w_proj, w_o)


def _merge_bwd(dres, w_out, a, b, p):
    t, d = dres.shape
    tm = min(ROW_TILE, t)
    tn = GATE_TILE

    def epilogue(acc, ex, outs, ids):
        a_ref, b_ref, gc_ref, ga_ref = ex
        sc = _sigmoid(gc_ref[...].astype(F32))
        sa = _sigmoid(ga_ref[...].astype(F32))
        outs[0][...] = (acc * sc).astype(BF16)
        outs[1][...] = (acc * sa).astype(BF16)
        outs[2][0] = (acc * a_ref[...].astype(F32) * sc * (1.0 - sc)).astype(BF16)
        outs[2][1] = (acc * b_ref[...].astype(F32) * sa * (1.0 - sa)).astype(BF16)

    ospec = pl.BlockSpec((tm, tn), lambda i, j, kk: (i, j))
    return _mm("merge_bwd", (t // tm, d // tn, 1),
               dres, pl.BlockSpec((tm, d), lambda i, j, kk: (i, 0)),
               w_out, pl.BlockSpec((tn, d), lambda i, j, kk: (j, 0)), (tm, tn),
               trans_b=True, a_pre=_to_bf16,
               extras=(a, b, p, p),
               extra_specs=(ospec, ospec,
                            pl.BlockSpec((tm, tn), lambda i, j, kk: (i, COL_GC // tn + j)),
                            pl.BlockSpec((tm, tn), lambda i, j, kk: (i, COL_GA // tn + j))),
               out_shape=(jax.ShapeDtypeStruct((t, d), BF16), jax.ShapeDtypeStruct((t, d), BF16),
                          jax.ShapeDtypeStruct((2, t, d), BF16)),
               out_specs=(ospec, ospec, pl.BlockSpec((2, tm, tn), lambda i, j, kk: (0, i, j))),
               epilogue=epilogue)


def _store_epilogue(acc, ex, outs, ids):
    outs[0][...] = acc


def _store_bf16_epilogue(acc, ex, outs, ids):
    outs[0][...] = acc.astype(BF16)


def _mm_nt(name, a, w, out_dtype=BF16):
    t, n = a.shape
    k = w.shape[0]
    tm = min(ROW_TILE, t)
    return _mm(name, (t // tm, 1, 1), a, pl.BlockSpec((tm, n), lambda i, j, kk: (i, 0)),
               w, pl.BlockSpec((k, n), lambda i, j, kk: (0, 0)), (tm, k), trans_b=True,
               out_shape=(jax.ShapeDtypeStruct((t, k), out_dtype),),
               out_specs=(pl.BlockSpec((tm, k), lambda i, j, kk: (i, 0)),),
               epilogue=_store_bf16_epilogue if out_dtype == BF16 else _store_epilogue)[0]


def _mm_tn(name, a, b, b_pre=None):
    t, m = a.shape
    n = b.shape[1]
    tk = min(ROW_TILE, t)
    return _mm(name, (1, 1, t // tk), a, pl.BlockSpec((tk, m), lambda i, j, kk: (kk, 0)),
               b, pl.BlockSpec((tk, n), lambda i, j, kk: (kk, 0)), (m, n), trans_a=True, b_pre=b_pre,
               out_shape=(jax.ShapeDtypeStruct((m, n), F32),),
               out_specs=(pl.BlockSpec((m, n), lambda i, j, kk: (0, 0)),), epilogue=_store_epilogue)[0]


def _local_step(x, target, w):
    t = x.shape[0]
    nb = t // BLOCK

    x1, ffn1_saved = _ffn_fwd("ffn1", x, w["ffn1_norm"], w["ffn1_w_in"], w["ffn1_w_out"])
    hm = _rmsnorm_fwd("mix_norm", x1, w["mix_norm"])
    tm = min(ROW_TILE, t)
    p = _mm("mix_in", (N_CHIPS, t // tm, 1),
            hm, pl.BlockSpec((tm, D_MODEL), lambda j, i, kk: (i, 0)),
            w["w_in"], pl.BlockSpec((None, D_MODEL, SHARD_W), lambda j, i, kk: (j, 0, 0)), (tm, SHARD_W),
            out_shape=(jax.ShapeDtypeStruct((t, IN_W), BF16),),
            out_specs=(pl.BlockSpec((tm, SHARD_W), lambda j, i, kk: (i, j)),),
            epilogue=_store_bf16_epilogue)[0]

    z3 = _conv_fwd(p, w["conv_dw_kernel"], w["conv_dw_bias"], w["conv_ln_g"], w["conv_ln_b"])

    onehot = _bucket_onehot()
    bias = _bias_table(w["rel_bias"].T, onehot).reshape(N_KV_HEADS, QROWS, 2 * BLOCK)
    sink_rows = jnp.repeat(w["attn_sinks"].reshape(N_KV_HEADS, GROUP), BLOCK, axis=1)[..., None]
    q4 = _q_to_heads(p[:, COL_Q:COL_K], t)
    k3 = _kv_to_heads(p[:, COL_K:COL_V], t)
    v3 = _kv_to_heads(p[:, COL_V:COL_GC], t)
    o4 = _attn_fwd(q4, k3, v3, w["q_norm"], w["k_norm"], sink_rows, bias)
    o = _q_from_heads(o4, t)

    merged, a, b = _merge_fwd(z3, o, p, w["conv_w_proj"], w["attn_w_o"])
    x2 = _mm_residual("mix_out", merged, w["w_out"], x1, 1.0)
    x3, ffn2_saved = _ffn_fwd("ffn2", x2, w["ffn2_norm"], w["ffn2_w_in"], w["ffn2_w_out"])
    dy, loss = _loss_head(x3, target)

    g = {}
    dres2, g["ffn2_w_in"], g["ffn2_w_out"], g["ffn2_norm"] = _ffn_bwd(
        "ffn2b", dy, x2, w["ffn2_norm"], ffn2_saved, w["ffn2_w_in"], w["ffn2_w_out"])

    g["w_out"] = _mm_tn("d_w_out", merged, dres2, b_pre=_to_bf16)
    da, db, dgates = _merge_bwd(dres2, w["w_out"], a, b, p)
    g["conv_w_proj"] = _mm_tn("d_w_proj", z3, da)
    g["attn_w_o"] = _mm_tn("d_w_o", o, db)
    dz3 = _mm_nt("d_z3", da, w["conv_w_proj"])
    do = _mm_nt("d_o", db, w["attn_w_o"])

    dq4, dk3, dv3, dbias, dsink, dgq, dgk = _attn_bwd(q4, k3, v3, _q_to_heads(do, t), w["q_norm"], w["k_norm"], sink_rows, bias)
    g["rel_bias"] = _bias_table_bwd(dbias.reshape(N_Q_HEADS, BLOCK * 2 * BLOCK), onehot).T
    g["attn_sinks"] = dsink.reshape(N_Q_HEADS)
    g["q_norm"] = jnp.sum(dgq, axis=0)
    g["k_norm"] = jnp.sum(dgk, axis=0)

    dz1, g["conv_dw_kernel"], g["conv_dw_bias"], g["conv_ln_g"], g["conv_ln_b"] = _conv_bwd_ln(
        p, dz3, w["conv_dw_kernel"], w["conv_dw_bias"], w["conv_ln_g"], w["conv_ln_b"])
    dconv = _conv_bwd_glu(p, dz1, w["conv_dw_kernel"])

    dp = jnp.concatenate([dconv, _q_from_heads(dq4, t), _kv_from_heads(dk3, t).astype(BF16),
                          _kv_from_heads(dv3, t).astype(BF16), dgates[0], dgates[1]], axis=1)
    g["w_in"] = _mm("d_w_in", (1, N_CHIPS, t // tm),
                    hm, pl.BlockSpec((tm, D_MODEL), lambda i, j, kk: (kk, 0)),
                    dp, pl.BlockSpec((tm, SHARD_W), lambda i, j, kk: (kk, j)), (D_MODEL, SHARD_W),
                    trans_a=True,
                    out_shape=(jax.ShapeDtypeStruct((N_CHIPS, D_MODEL, SHARD_W), F32),),
                    out_specs=(pl.BlockSpec((None, D_MODEL, SHARD_W), lambda i, j, kk: (j, 0, 0)),),
                    epilogue=_store_epilogue)[0]
    dres1, g["mix_norm"] = _mm("d_mix", (t // tm, 1, N_CHIPS),
                               dp, pl.BlockSpec((tm, SHARD_W), lambda i, j, kk: (i, kk)),
                               w["w_in"], pl.BlockSpec((None, D_MODEL, SHARD_W), lambda i, j, kk: (kk, 0, 0)),
                               (tm, D_MODEL), trans_b=True,
                               extras=(x1, w["mix_norm"], dres2),
                               extra_specs=(pl.BlockSpec((tm, D_MODEL), lambda i, j, kk: (i, 0)),
                                            pl.BlockSpec((1, D_MODEL), lambda i, j, kk: (0, 0)),
                                            pl.BlockSpec((tm, D_MODEL), lambda i, j, kk: (i, 0))),
                               out_shape=(jax.ShapeDtypeStruct((t, D_MODEL), F32), jax.ShapeDtypeStruct((1, D_MODEL), F32)),
                               out_specs=(pl.BlockSpec((tm, D_MODEL), lambda i, j, kk: (i, 0)),
                                          pl.BlockSpec((1, D_MODEL), lambda i, j, kk: (0, 0))),
                               epilogue=_rms_bwd_epilogue, sem=("arbitrary", "arbitrary", "arbitrary"))

    grad_x, g["ffn1_w_in"], g["ffn1_w_out"], g["ffn1_norm"] = _ffn_bwd(
        "ffn1b", dres1, x, w["ffn1_norm"], ffn1_saved, w["ffn1_w_in"], w["ffn1_w_out"])
    return loss[0, 0], grad_x, g


def _mesh_place():
    x, y, c = lax.axis_index("x"), lax.axis_index("y"), lax.axis_index("c")
    chips = [(1 - x, y), (x, 1 - y), (1 - x, 1 - y)]
    return x, y, c, chips


def _any_specs(n):
    return [pl.BlockSpec(memory_space=pl.ANY)] * n


def _all_gather_weights(shards):
    nw = len(shards)

    def body(*refs):
        s_refs, o_refs = refs[:nw], refs[nw:2 * nw]
        send_sems, recv_sems, local_sems = refs[2 * nw:]
        x, y, c, chips = _mesh_place()
        me, sib, jme = (x, y, c), (x, y, 1 - c), 2 * x + y

        def rc(w, k, src, dst, to):
            return pltpu.make_async_remote_copy(src_ref=src, dst_ref=dst, send_sem=send_sems.at[6 * w + k],
                                                recv_sem=recv_sems.at[6 * w + k], device_id=to, device_id_type=MESH)

        local = [pltpu.make_async_copy(s_refs[w], o_refs[w].at[jme], local_sems.at[w]) for w in range(nw)]
        for cp in local:
            cp.start()
        sent = []
        for w in range(nw):
            for k, chip in enumerate(chips):
                cp = rc(w, k, s_refs[w].at[c], o_refs[w].at[jme, c], (*chip, c))
                cp.start()
                sent.append(cp)
        for w in range(nw):
            for k, chip in enumerate(chips):
                jk = 2 * chip[0] + chip[1]
                landed = o_refs[w].at[jk, c]
                rc(w, k, landed, landed, me).wait_recv()
                cp = rc(w, 3 + k, landed, landed, sib)
                cp.start()
                sent.append(cp)
        for w in range(nw):
            for k, chip in enumerate(chips):
                jk = 2 * chip[0] + chip[1]
                landed = o_refs[w].at[jk, 1 - c]
                rc(w, 3 + k, landed, landed, me).wait_recv()
        for cp in sent:
            cp.wait_send()
        for cp in local:
            cp.wait()

    return pl.pallas_call(
        body, name="all_gather_weights",
        in_specs=_any_specs(nw), out_specs=_any_specs(nw),
        out_shape=[jax.ShapeDtypeStruct((N_CHIPS,) + s.shape, s.dtype) for s in shards],
        scratch_shapes=[pltpu.SemaphoreType.DMA((6 * nw,)), pltpu.SemaphoreType.DMA((6 * nw,)),
                        pltpu.SemaphoreType.DMA((nw,))],
    )(*shards)


def _exchange_halves(grads):
    nw = len(grads)

    def body(*refs):
        g_refs, o_refs = refs[:nw], refs[nw:2 * nw]
        send_sems, recv_sems = refs[2 * nw:]
        x, y, c, _ = _mesh_place()
        copies = []
        for w in range(nw):
            cp = pltpu.make_async_remote_copy(src_ref=g_refs[w].at[:, 1 - c], dst_ref=o_refs[w], send_sem=send_sems.at[w],
                                              recv_sem=recv_sems.at[w], device_id=(x, y, 1 - c), device_id_type=MESH)
            cp.start()
            copies.append(cp)
        for cp in copies:
            cp.wait()

    return pl.pallas_call(
        body, name="exchange_halves",
        in_specs=_any_specs(nw), out_specs=_any_specs(nw),
        out_shape=[jax.ShapeDtypeStruct((N_CHIPS,) + g.shape[2:], F32) for g in grads],
        scratch_shapes=[pltpu.SemaphoreType.DMA((nw,)), pltpu.SemaphoreType.DMA((nw,))],
    )(*grads)


def _row_tile(r):
    for cand in (256, 176, 128, 64, 32, 16, 8):
        if r % cand == 0:
            return cand
    return r


def _add_own_half(c_idx, grad, got):
    _, _, r, cols = grad.shape
    tr = _row_tile(r)

    def body(c_ref, g_ref, o_ref, out_ref):
        out_ref[...] = g_ref[...] + o_ref[...]

    return pl.pallas_call(
        body, name="add_own_half",
        grid_spec=pltpu.PrefetchScalarGridSpec(
            num_scalar_prefetch=1, grid=(N_CHIPS, r // tr),
            in_specs=[pl.BlockSpec((None, None, tr, cols), lambda j, i, c_ref: (j, c_ref[0], i, 0)),
                      pl.BlockSpec((None, tr, cols), lambda j, i, c_ref: (j, i, 0))],
            out_specs=pl.BlockSpec((None, tr, cols), lambda j, i, c_ref: (j, i, 0))),
        out_shape=jax.ShapeDtypeStruct((N_CHIPS, r, cols), F32),
        compiler_params=_params(("parallel", "parallel")),
    )(c_idx, grad, got)


def _scatter_to_chips(sums):
    nw = len(sums)

    def body(*refs):
        s_refs, o_refs = refs[:nw], refs[nw:2 * nw]
        send_sems, recv_sems, local_sems = refs[2 * nw:]
        x, y, c, chips = _mesh_place()
        jme = 2 * x + y
        local = [pltpu.make_async_copy(s_refs[w].at[jme], o_refs[w].at[jme], local_sems.at[w]) for w in range(nw)]
        for cp in local:
            cp.start()
        copies = []
        for w in range(nw):
            for k, chip in enumerate(chips):
                jk = 2 * chip[0] + chip[1]
                cp = pltpu.make_async_remote_copy(src_ref=s_refs[w].at[jk], dst_ref=o_refs[w].at[jme],
                                                  send_sem=send_sems.at[3 * w + k], recv_sem=recv_sems.at[3 * w + k],
                                                  device_id=(*chip, c), device_id_type=MESH)
                cp.start()
                copies.append(cp)
        for w in range(nw):
            for k, chip in enumerate(chips):
                landed = o_refs[w].at[2 * chip[0] + chip[1]]
                pltpu.make_async_remote_copy(src_ref=landed, dst_ref=landed, send_sem=send_sems.at[3 * w + k],
                                             recv_sem=recv_sems.at[3 * w + k], device_id=(x, y, c),
                                             device_id_type=MESH).wait_recv()
        for cp in copies:
            cp.wait_send()
        for cp in local:
            cp.wait()

    return pl.pallas_call(
        body, name="scatter_to_chips",
        in_specs=_any_specs(nw), out_specs=_any_specs(nw),
        out_shape=[jax.ShapeDtypeStruct(s.shape, F32) for s in sums],
        scratch_shapes=[pltpu.SemaphoreType.DMA((3 * nw,)), pltpu.SemaphoreType.DMA((3 * nw,)),
                        pltpu.SemaphoreType.DMA((nw,))],
    )(*sums)


def _sum_chips(parts):
    _, r, cols = parts.shape
    tr = _row_tile(r)

    def body(p_ref, o_ref):
        o_ref[...] = ((p_ref[0] + p_ref[1]) + p_ref[2]) + p_ref[3]

    return pl.pallas_call(
        body, name="sum_chips", grid=(r // tr,),
        in_specs=[pl.BlockSpec((N_CHIPS, tr, cols), lambda i: (0, i, 0))],
        out_specs=pl.BlockSpec((tr, cols), lambda i: (i, 0)),
        out_shape=jax.ShapeDtypeStruct((r, cols), F32),
        compiler_params=_params(("parallel",)),
    )(parts)


def _join_halves(halves):
    nw = len(halves)

    def body(*refs):
        h_refs, o_refs = refs[:nw], refs[nw:2 * nw]
        send_sems, recv_sems, local_sems = refs[2 * nw:]
        x, y, c, _ = _mesh_place()
        local = [pltpu.make_async_copy(h_refs[w], o_refs[w].at[c], local_sems.at[w]) for w in range(nw)]
        for cp in local:
            cp.start()
        copies = []
        for w in range(nw):
            cp = pltpu.make_async_remote_copy(src_ref=h_refs[w], dst_ref=o_refs[w].at[c], send_sem=send_sems.at[w],
                                              recv_sem=recv_sems.at[w], device_id=(x, y, 1 - c), device_id_type=MESH)
            cp.start()
            copies.append(cp)
        for w in range(nw):
            copies[w].wait_send()
            landed = o_refs[w].at[1 - c]
            pltpu.make_async_remote_copy(src_ref=landed, dst_ref=landed, send_sem=send_sems.at[w], recv_sem=recv_sems.at[w],
                                         device_id=(x, y, c), device_id_type=MESH).wait_recv()
        for cp in local:
            cp.wait()

    return pl.pallas_call(
        body, name="join_halves",
        in_specs=_any_specs(nw), out_specs=_any_specs(nw),
        out_shape=[jax.ShapeDtypeStruct((2,) + h.shape, F32) for h in halves],
        scratch_shapes=[pltpu.SemaphoreType.DMA((nw,)), pltpu.SemaphoreType.DMA((nw,)), pltpu.SemaphoreType.DMA((nw,))],
    )(*halves)


SMALL_ROWS = 8


def _all_reduce_small(pack):
    rows, cols = pack.shape
    n_dev = 8

    def body(p_ref, o_ref, slots, send_sems, recv_sems):
        x, y, c, _ = _mesh_place()
        me = 4 * x + 2 * y + c
        slots[me] = p_ref[...]
        copies = []
        for k in range(1, n_dev):
            peer = (me + k) % n_dev
            cp = pltpu.make_async_remote_copy(src_ref=p_ref, dst_ref=slots.at[me], send_sem=send_sems.at[k],
                                              recv_sem=recv_sems.at[k],
                                              device_id=(peer // 4, (peer // 2) % 2, peer % 2), device_id_type=MESH)
            cp.start()
            copies.append(cp)
        for k in range(1, n_dev):
            src = (me + n_dev - k) % n_dev
            pltpu.make_async_remote_copy(src_ref=p_ref, dst_ref=slots.at[src], send_sem=send_sems.at[k],
                                         recv_sem=recv_sems.at[k], device_id=(x, y, c), device_id_type=MESH).wait_recv()
        for cp in copies:
            cp.wait_send()
        total = slots[0]
        for s in range(1, n_dev):
            total = total + slots[s]
        o_ref[...] = total

    return pl.pallas_call(
        body, name="all_reduce_small",
        in_specs=[pl.BlockSpec(memory_space=pltpu.VMEM)], out_specs=pl.BlockSpec(memory_space=pltpu.VMEM),
        out_shape=jax.ShapeDtypeStruct((rows, cols), F32),
        scratch_shapes=[pltpu.VMEM((n_dev, rows, cols), F32), pltpu.SemaphoreType.DMA((n_dev,)),
                        pltpu.SemaphoreType.DMA((n_dev,))],
    )(pack)


def _adamw(name, w, g, m, v):
    r, cols = w.shape
    tr = _row_tile(r)

    def body(w_ref, g_ref, m_ref, v_ref, d_ref, nm_ref, nv_ref):
        gv = g_ref[...]
        nm = ADAM_B1 * m_ref[...] + (1.0 - ADAM_B1) * gv
        nv = ADAM_B2 * v_ref[...] + (1.0 - ADAM_B2) * (gv * gv)
        m_hat = nm / (1.0 - ADAM_B1 ** ADAM_STEP)
        v_hat = nv / (1.0 - ADAM_B2 ** ADAM_STEP)
        d_ref[...] = -ADAM_LR * (m_hat / (jnp.sqrt(v_hat) + ADAM_EPS) + ADAM_WD * w_ref[...])
        nm_ref[...] = nm
        nv_ref[...] = nv

    spec = pl.BlockSpec((tr, cols), lambda i: (i, 0))
    return pl.pallas_call(
        body, name=name, grid=(r // tr,),
        in_specs=[spec] * 4, out_specs=[spec] * 3,
        out_shape=[jax.ShapeDtypeStruct((r, cols), F32)] * 3,
        compiler_params=_params(("parallel",)),
    )(w, g, m, v)


BIG = ["ffn1_w_in", "ffn1_w_out", "w_in", "conv_w_proj", "attn_w_o", "w_out", "ffn2_w_in", "ffn2_w_out", "conv_dw_kernel"]
COL_SHARDED = ("ffn1_w_in", "w_in", "ffn2_w_in")
SMALL = ["ffn1_norm", "mix_norm", "ffn2_norm", "conv_dw_bias", "conv_ln_g", "conv_ln_b", "q_norm", "k_norm", "attn_sinks", "rel_bias"]
WEIGHTS = ["ffn1_norm", "ffn1_w_in", "ffn1_w_out", "mix_norm", "w_in", "conv_dw_kernel", "conv_dw_bias", "conv_ln_g",
           "conv_ln_b", "conv_w_proj", "q_norm", "k_norm", "attn_sinks", "rel_bias", "attn_w_o", "w_out", "ffn2_norm",
           "ffn2_w_in", "ffn2_w_out"]
SMALL_PLACE = {"ffn1_norm": (0, 0, 1024), "mix_norm": (1, 0, 1024), "ffn2_norm": (2, 0, 1024), "conv_dw_bias": (3, 0, 1024),
               "conv_ln_g": (4, 0, 1024), "conv_ln_b": (5, 0, 1024), "q_norm": (6, 0, 64), "k_norm": (6, 128, 64),
               "attn_sinks": (6, 256, 16), "rel_bias": (7, 0, 512)}
LOSS_PLACE = (6, 384)


def _pack_small(vals, fill=0.0, loss=None):
    pack = jnp.full((SMALL_ROWS, D_MODEL), fill, F32)
    for name, (row, lane, n) in SMALL_PLACE.items():
        pack = pack.at[row, lane:lane + n].set(vals[name].reshape(n))
    if loss is not None:
        pack = pack.at[LOSS_PLACE[0], LOSS_PLACE[1]].set(loss)
    return pack


def _unpack_small(pack, shapes):
    return {name: pack[row, lane:lane + n].reshape(shapes[name]) for name, (row, lane, n) in SMALL_PLACE.items()}


def _shard_halves(name, a):
    if name == "conv_dw_kernel":
        a = jnp.pad(a, ((0, CONV_PAD - CONV_WIDTH), (0, 0)))
    r, cols = a.shape
    return a.reshape(2, r // 2, cols)


def kernel(x, ffn1_norm, ffn1_w_in, ffn1_w_out, mix_norm, w_in, conv_dw_kernel, conv_dw_bias, conv_ln_g, conv_ln_b, conv_w_proj, q_norm, k_norm, attn_sinks, rel_bias, attn_w_o, w_out, ffn2_norm, ffn2_w_in, ffn2_w_out, loss_target, m_ffn1_norm, m_ffn1_w_in, m_ffn1_w_out, m_mix_norm, m_w_in, m_conv_dw_kernel, m_conv_dw_bias, m_conv_ln_g, m_conv_ln_b, m_conv_w_proj, m_q_norm, m_k_norm, m_attn_sinks, m_rel_bias, m_attn_w_o, m_w_out, m_ffn2_norm, m_ffn2_w_in, m_ffn2_w_out, v_ffn1_norm, v_ffn1_w_in, v_ffn1_w_out, v_mix_norm, v_w_in, v_conv_dw_kernel, v_conv_dw_bias, v_conv_ln_g, v_conv_ln_b, v_conv_w_proj, v_q_norm, v_k_norm, v_attn_sinks, v_rel_bias, v_attn_w_o, v_w_out, v_ffn2_norm, v_ffn2_w_in, v_ffn2_w_out):
    args = dict(locals())
    wts = {n: args[n] for n in WEIGHTS}
    mom = {n: args["m_" + n] for n in WEIGHTS}
    var = {n: args["v_" + n] for n in WEIGHTS}
    t = x.shape[1]

    shards = [_shard_halves(n, wts[n]) for n in BIG]
    shards = [s if n == "conv_dw_kernel" else s.astype(BF16) for n, s in zip(BIG, shards)]
    gathered = dict(zip(BIG, _all_gather_weights(shards)))
    full = {}
    for n in BIG:
        g4 = gathered[n]
        r, cols = g4.shape[2] * 2, g4.shape[3]
        if n in COL_SHARDED:
            full[n] = g4.reshape(N_CHIPS, r, cols)
        elif n == "conv_dw_kernel":
            full[n] = g4.reshape(N_CHIPS, r, cols).transpose(1, 0, 2).reshape(r, N_CHIPS * cols)
        else:
            full[n] = g4.reshape(N_CHIPS * r, cols)
    for n in SMALL:
        if n in ("attn_sinks", "rel_bias"):
            full[n] = wts[n]
        else:
            full[n] = wts[n].reshape(1, -1)

    loss_part, grad_x, g = _local_step(x[0], loss_target[0], full)

    small_sum = _all_reduce_small(_pack_small(g, loss=loss_part))
    loss = small_sum[LOSS_PLACE[0], LOSS_PLACE[1]]
    small_shapes = {n: wts[n].shape for n in SMALL}
    g_small = _unpack_small(small_sum, small_shapes)

    c_idx = lax.axis_index("c").astype(jnp.int32).reshape(1)
    g4 = []
    for n in BIG:
        a = g[n]
        if n == "conv_dw_kernel":
            a = a.reshape(CONV_PAD, N_CHIPS, -1).transpose(1, 0, 2)
        elif n not in COL_SHARDED:
            a = a.reshape(N_CHIPS, a.shape[0] // N_CHIPS, a.shape[1])
        g4.append(a.reshape(N_CHIPS, 2, a.shape[1] // 2, a.shape[2]))
    got = _exchange_halves(g4)
    sums = [_add_own_half(c_idx, a, b) for a, b in zip(g4, got)]
    parts = _scatter_to_chips(sums)
    halves = [_sum_chips(p) for p in parts]
    joined = _join_halves(halves)

    grads, delta, new_m, new_v = {}, {}, {}, {}
    for n, j in zip(BIG, joined):
        gs = j.reshape(j.shape[1] * 2, j.shape[2])
        pad = n == "conv_dw_kernel"
        ws, ms, vs = (_shard_halves(n, a).reshape(gs.shape) for a in (wts[n], mom[n], var[n]))
        d, nm, nv = _adamw("adamw_" + n, ws, gs, ms, vs)
        cut = (lambda a: a[:CONV_WIDTH]) if pad else (lambda a: a)
        grads[n], delta[n], new_m[n], new_v[n] = cut(gs), cut(d), cut(nm), cut(nv)
    d, nm, nv = _adamw("adamw_small", _pack_small(wts), small_sum, _pack_small(mom), _pack_small(var, fill=1.0))
    grads.update(g_small)
    delta.update(_unpack_small(d, small_shapes))
    new_m.update(_unpack_small(nm, small_shapes))
    new_v.update(_unpack_small(nv, small_shapes))

    return (loss, grad_x[None], *[grads[n] for n in WEIGHTS], *[delta[n] for n in WEIGHTS],
            *[new_m[n] for n in WEIGHTS], *[new_v[n] for n in WEIGHTS])
```

```python
import functools
import math

import jax
import jax.numpy as jnp
from jax import lax
from jax.experimental import pallas as pl
from jax.experimental.pallas import tpu as pltpu

F32 = jnp.float32
BF16 = jnp.bfloat16
MESH = pl.DeviceIdType.MESH

EPS = 1e-6
D_MODEL = 1024
D_FF = 2816
N_CHIPS = 4
SHARD_W = 2 * D_FF // N_CHIPS
HEAD_DIM = 64
N_Q_HEADS = 16
N_KV_HEADS = 4
GROUP = N_Q_HEADS // N_KV_HEADS
BLOCK = 128
QROWS = GROUP * BLOCK
N_BUCKETS = 32
MAX_DISTANCE = 128
CONV_WIDTH = 31
CONV_PAD = 32
NEG = float(jnp.finfo(jnp.float32).min)

ADAM_LR = 0.001
ADAM_B1 = 0.9
ADAM_B2 = 0.999
ADAM_EPS = 1e-08
ADAM_WD = 0.01
ADAM_STEP = 10

VMEM_LIMIT_BYTES = 56 * 1024 * 1024
ROW_TILE = 512
CONV_TILE = 256
CONV_ROWS = 32
LANES = 128

COL_CONV_A, COL_CONV_G, COL_Q, COL_K, COL_V, COL_GC, COL_GA = 0, 1024, 2048, 3072, 3328, 3584, 4608
IN_W = 5632


def _params(sem, vmem=VMEM_LIMIT_BYTES):
    return pltpu.CompilerParams(dimension_semantics=sem, vmem_limit_bytes=vmem)


def _sigmoid(x):
    return 1.0 / (1.0 + jnp.exp(-x))


def _dot(a, b, trans_a=False, trans_b=False, precision=None):
    dn = (((0,) if trans_a else (1,), (1,) if trans_b else (0,)), ((), ()))
    return lax.dot_general(a, b, dn, preferred_element_type=F32, precision=precision)


def _mm(name, grid, a, a_spec, b, b_spec, acc_shape, *, trans_a=False, trans_b=False, a_pre=None, b_pre=None,
        extras=(), extra_specs=(), tokens=(), out_shape, out_specs, epilogue, sem=("parallel", "parallel", "arbitrary")):
    n_k = grid[2]
    extras = tuple(extras) + tuple(tokens)
    extra_specs = tuple(extra_specs) + (pl.BlockSpec((8, LANES), lambda i, j, kk: (0, 0)),) * len(tokens)
    n_extra = len(extras)
    n_out = len(out_shape)

    def body(a_ref, b_ref, *rest):
        ex = rest[:n_extra]
        outs = rest[n_extra:n_extra + n_out]
        ids = (pl.program_id(0), pl.program_id(1), pl.program_id(2))
        av = a_ref[...]
        bv = b_ref[...]
        if a_pre is not None:
            av = a_pre(av)
        if b_pre is not None:
            bv = b_pre(bv)
        part = _dot(av, bv, trans_a, trans_b)
        if n_k == 1:
            epilogue(part, ex, outs, ids)
        else:
            acc = rest[-1]

            @pl.when(ids[2] == 0)
            def _():
                acc[...] = part

            @pl.when(ids[2] > 0)
            def _():
                acc[...] += part

            @pl.when(ids[2] == n_k - 1)
            def _():
                epilogue(acc[...], ex, outs, ids)

    scratch = [] if n_k == 1 else [pltpu.VMEM(acc_shape, F32)]
    return pl.pallas_call(
        body, name=name, grid=grid,
        in_specs=[a_spec, b_spec, *extra_specs],
        out_specs=list(out_specs), out_shape=list(out_shape),
        scratch_shapes=scratch, compiler_params=_params(sem),
    )(a, b, *extras)


def _half_bf16(v):
    return (0.5 * v).astype(BF16)


def _to_bf16(v):
    return v.astype(BF16)


def _rmsnorm_fwd(name, x, g, tokens=()):
    t, d = x.shape
    tm = min(ROW_TILE, t)

    def body(x_ref, g_ref, *rest):
        o_ref = rest[-1]
        xv = x_ref[...]
        r = lax.rsqrt(jnp.mean(xv * xv, axis=-1, keepdims=True) + EPS)
        o_ref[...] = (xv * r * g_ref[...]).astype(BF16)

    return pl.pallas_call(
        body, name=name, grid=(t // tm,),
        in_specs=[pl.BlockSpec((tm, d), lambda i: (i, 0)), pl.BlockSpec((1, d), lambda i: (0, 0))]
        + [pl.BlockSpec((8, LANES), lambda i: (0, 0))] * len(tokens),
        out_specs=pl.BlockSpec((tm, d), lambda i: (i, 0)),
        out_shape=jax.ShapeDtypeStruct((t, d), BF16),
        compiler_params=_params(("parallel",)),
    )(x, g, *tokens)


def _rms_bwd_epilogue(acc, ex, outs, ids):
    x_ref, g_ref, dres_ref = ex[:3]
    out_ref, dg_ref = outs
    xv = x_ref[...]
    r = lax.rsqrt(jnp.mean(xv * xv, axis=-1, keepdims=True) + EPS)
    w = acc * g_ref[...]
    dx = r * w - xv * (r * r * r) * jnp.mean(xv * w, axis=-1, keepdims=True)
    out_ref[...] = dres_ref[...] + dx
    part = jnp.sum(acc * (xv * r), axis=0, keepdims=True)

    @pl.when(ids[0] == 0)
    def _():
        dg_ref[...] = part

    @pl.when(ids[0] > 0)
    def _():
        dg_ref[...] += part


def _ffn_in(name, n, w_in4):
    t, d = n.shape
    tm = min(ROW_TILE, t)

    def body(n_ref, wa_ref, wb_ref, ab_ref, h_ref):
        nv = n_ref[...]
        a = _dot(nv, wa_ref[...])
        b = _dot(nv, wb_ref[...])
        h_ref[...] = (a * _sigmoid(a) * b).astype(BF16)
        ab_ref[0] = a.astype(BF16)
        ab_ref[1] = b.astype(BF16)

    return pl.pallas_call(
        body, name=name, grid=(2, t // tm),
        in_specs=[pl.BlockSpec((tm, d), lambda j, i: (i, 0)),
                  pl.BlockSpec((None, d, SHARD_W), lambda j, i: (j, 0, 0)),
                  pl.BlockSpec((None, d, SHARD_W), lambda j, i: (j + 2, 0, 0))],
        out_specs=[pl.BlockSpec((2, tm, SHARD_W), lambda j, i: (0, i, j)),
                   pl.BlockSpec((tm, SHARD_W), lambda j, i: (i, j))],
        out_shape=[jax.ShapeDtypeStruct((2, t, D_FF), BF16), jax.ShapeDtypeStruct((t, D_FF), BF16)],
        compiler_params=_params(("parallel", "parallel")),
    )(n, w_in4, w_in4)


def _mm_residual(name, a, w, res, scale):
    t, k = a.shape
    n = w.shape[1]
    tm = min(ROW_TILE, t)

    def epilogue(acc, ex, outs, ids):
        outs[0][...] = ex[0][...] + scale * acc

    return _mm(name, (t // tm, 1, 1), a, pl.BlockSpec((tm, k), lambda i, j, kk: (i, 0)),
               w, pl.BlockSpec((k, n), lambda i, j, kk: (0, 0)), (tm, n),
               extras=(res,), extra_specs=(pl.BlockSpec((tm, n), lambda i, j, kk: (i, 0)),),
               out_shape=(jax.ShapeDtypeStruct((t, n), F32),),
               out_specs=(pl.BlockSpec((tm, n), lambda i, j, kk: (i, 0)),), epilogue=epilogue)[0]


def _ffn_fwd(tag, x, g, w_in4, w_out, tokens=()):
    n = _rmsnorm_fwd(tag + "_norm", x, g, tokens)
    ab, h = _ffn_in(tag + "_in", n, w_in4)
    y = _mm_residual(tag + "_out", h, w_out, x, 0.5)
    return y, (n, ab, h)


def _ffn_bwd(tag, dres, x, g, saved, w_in4, w_out, tokens=(), on_weight_grads=None):
    n, ab, h = saved
    t, d = x.shape
    tm = min(ROW_TILE, t)
    tk = min(ROW_TILE, t)
    half_w = SHARD_W

    def dact_epilogue(acc, ex, outs, ids):
        a = ex[0][0].astype(F32)
        b = ex[0][1].astype(F32)
        sig = _sigmoid(a)
        outs[0][0] = (acc * b * (sig * (1.0 + a * (1.0 - sig)))).astype(BF16)
        outs[0][1] = (acc * (a * sig)).astype(BF16)

    du = _mm(tag + "_dact", (2, t // tm, 1),
             dres, pl.BlockSpec((tm, d), lambda j, i, kk: (i, 0)),
             w_out, pl.BlockSpec((half_w, d), lambda j, i, kk: (j, 0)), (tm, half_w),
             trans_b=True, a_pre=_half_bf16,
             extras=(ab,), extra_specs=(pl.BlockSpec((2, tm, half_w), lambda j, i, kk: (0, i, j)),), tokens=tokens,
             out_shape=(jax.ShapeDtypeStruct((2, t, D_FF), BF16),),
             out_specs=(pl.BlockSpec((2, tm, half_w), lambda j, i, kk: (0, i, j)),),
             epilogue=dact_epilogue)[0]

    def store_epilogue(acc, ex, outs, ids):
        outs[0][...] = acc

    dw_out = _mm(tag + "_dwout", (2, 1, t // tk),
                 h, pl.BlockSpec((tk, half_w), lambda i, j, kk: (kk, i)),
                 dres, pl.BlockSpec((tk, d), lambda i, j, kk: (kk, 0)), (half_w, d),
                 trans_a=True, b_pre=_half_bf16,
                 out_shape=(jax.ShapeDtypeStruct((D_FF, d), F32),),
                 out_specs=(pl.BlockSpec((half_w, d), lambda i, j, kk: (i, 0)),),
                 epilogue=store_epilogue)[0]

    dw_in4 = _mm(tag + "_dwin", (1, N_CHIPS, t // tk),
                 n, pl.BlockSpec((tk, d), lambda i, j, kk: (kk, 0)),
                 du, pl.BlockSpec((None, tk, SHARD_W), lambda i, j, kk: (j // 2, kk, j % 2)), (d, SHARD_W),
                 trans_a=True,
                 out_shape=(jax.ShapeDtypeStruct((N_CHIPS, d, SHARD_W), F32),),
                 out_specs=(pl.BlockSpec((None, d, SHARD_W), lambda i, j, kk: (j, 0, 0)),),
                 epilogue=store_epilogue)[0]

    late = () if on_weight_grads is None else on_weight_grads(dw_in4, dw_out)

    dx, dg = _mm(tag + "_dn", (t // tm, 1, N_CHIPS),
                 du, pl.BlockSpec((None, tm, SHARD_W), lambda i, j, kk: (kk // 2, i, kk % 2)),
                 w_in4, pl.BlockSpec((None, d, SHARD_W), lambda i, j, kk: (kk, 0, 0)), (tm, d),
                 trans_b=True,
                 extras=(x, g, dres),
                 extra_specs=(pl.BlockSpec((tm, d), lambda i, j, kk: (i, 0)),
                              pl.BlockSpec((1, d), lambda i, j, kk: (0, 0)),
                              pl.BlockSpec((tm, d), lambda i, j, kk: (i, 0))), tokens=late,
                 out_shape=(jax.ShapeDtypeStruct((t, d), F32), jax.ShapeDtypeStruct((1, d), F32)),
                 out_specs=(pl.BlockSpec((tm, d), lambda i, j, kk: (i, 0)),
                            pl.BlockSpec((1, d), lambda i, j, kk: (0, 0))),
                 epilogue=_rms_bwd_epilogue, sem=("arbitrary", "arbitrary", "arbitrary"))
    return dx, dw_in4, dw_out, dg


def _loss_head(y, target):
    t, d = y.shape
    tm = min(ROW_TILE, t)

    def body(y_ref, t_ref, dy_ref, loss_ref):
        diff = y_ref[...] - t_ref[...]
        dy_ref[...] = diff * (1.0 / d)
        part = jnp.full((8, LANES), 0.5 / d * jnp.sum(diff * diff), F32)
        i = pl.program_id(0)

        @pl.when(i == 0)
        def _():
            loss_ref[...] = part

        @pl.when(i > 0)
        def _():
            loss_ref[...] += part

    return pl.pallas_call(
        body, name="loss_head", grid=(t // tm,),
        in_specs=[pl.BlockSpec((tm, d), lambda i: (i, 0)), pl.BlockSpec((tm, d), lambda i: (i, 0))],
        out_specs=[pl.BlockSpec((tm, d), lambda i: (i, 0)), pl.BlockSpec((8, LANES), lambda i: (0, 0))],
        out_shape=[jax.ShapeDtypeStruct((t, d), F32), jax.ShapeDtypeStruct((8, LANES), F32)],
        compiler_params=_params(("arbitrary",)),
    )(y, target)


def _conv_fill(zp_ref, a_ref, g_ref, ah_ref, gh_ref, i):
    zh = ah_ref[...].astype(F32) * _sigmoid(gh_ref[...].astype(F32))
    zp_ref[pl.ds(0, CONV_PAD), :] = jnp.where(i > 0, zh, 0.0)
    zp_ref[pl.ds(CONV_PAD, a_ref.shape[0]), :] = a_ref[...].astype(F32) * _sigmoid(g_ref[...].astype(F32))


def _conv_taps(zp_ref, z1_ref, dw_ref, bias_ref, tm, ch):
    first = CONV_PAD - (CONV_WIDTH - 1)
    for cc in range(ch // LANES):
        lanes = pl.ds(cc * LANES, LANES)
        w = [dw_ref[pl.ds(j, 1), lanes] for j in range(CONV_WIDTH)]
        bias = bias_ref[:, lanes]
        for rb in range(tm // CONV_ROWS):
            acc = jnp.broadcast_to(bias, (CONV_ROWS, LANES))
            for j in range(CONV_WIDTH):
                acc = acc + w[j] * zp_ref[pl.ds(rb * CONV_ROWS + first + j, CONV_ROWS), lanes]
            z1_ref[pl.ds(rb * CONV_ROWS, CONV_ROWS), lanes] = acc


def _conv_specs(tm, ch):
    per = tm // CONV_PAD
    cb = COL_CONV_G // ch
    return [pl.BlockSpec((tm, ch), lambda i: (i, 0)),
            pl.BlockSpec((tm, ch), lambda i: (i, cb)),
            pl.BlockSpec((CONV_PAD, ch), lambda i: (jnp.maximum(i * per - 1, 0), 0)),
            pl.BlockSpec((CONV_PAD, ch), lambda i: (jnp.maximum(i * per - 1, 0), cb))]


def _conv_fwd(p, dw, bias, ln_g, ln_b):
    t = p.shape[0]
    ch = D_MODEL
    tm = min(CONV_TILE, t)

    def body(a_ref, g_ref, ah_ref, gh_ref, dw_ref, bias_ref, lg_ref, lb_ref, o_ref, zp_ref, z1_ref):
        i = pl.program_id(0)
        _conv_fill(zp_ref, a_ref, g_ref, ah_ref, gh_ref, i)
        _conv_taps(zp_ref, z1_ref, dw_ref, bias_ref, tm, ch)
        z1 = z1_ref[...]
        mu = jnp.mean(z1, axis=-1, keepdims=True)
        zc = z1 - mu
        rs = lax.rsqrt(jnp.mean(zc * zc, axis=-1, keepdims=True) + EPS)
        z2 = zc * rs * lg_ref[...] + lb_ref[...]
        o_ref[...] = (z2 * _sigmoid(z2)).astype(BF16)

    vec = pl.BlockSpec((1, ch), lambda i: (0, 0))
    return pl.pallas_call(
        body, name="conv_fwd", grid=(t // tm,),
        in_specs=_conv_specs(tm, ch) + [pl.BlockSpec((CONV_PAD, ch), lambda i: (0, 0)), vec, vec, vec],
        out_specs=pl.BlockSpec((tm, ch), lambda i: (i, 0)),
        out_shape=jax.ShapeDtypeStruct((t, ch), BF16),
        scratch_shapes=[pltpu.VMEM((CONV_PAD + tm, ch), F32), pltpu.VMEM((tm, ch), F32)],
        compiler_params=_params(("parallel",)),
    )(p, p, p, p, dw, bias, ln_g, ln_b)


def _conv_bwd_ln(p, dz3, dw, bias, ln_g, ln_b):
    t = p.shape[0]
    ch = D_MODEL
    tm = min(CONV_TILE, t)
    first = CONV_PAD - (CONV_WIDTH - 1)

    def body(a_ref, g_ref, ah_ref, gh_ref, dz3_ref, dw_ref, bias_ref, lg_ref, lb_ref,
             dz1_ref, ddw_ref, dbias_ref, dlg_ref, dlb_ref, zp_ref, z1_ref):
        i = pl.program_id(0)
        _conv_fill(zp_ref, a_ref, g_ref, ah_ref, gh_ref, i)
        _conv_taps(zp_ref, z1_ref, dw_ref, bias_ref, tm, ch)
        z1 = z1_ref[...]
        mu = jnp.mean(z1, axis=-1, keepdims=True)
        zc = z1 - mu
        rs = lax.rsqrt(jnp.mean(zc * zc, axis=-1, keepdims=True) + EPS)
        xh = zc * rs
        z2 = xh * lg_ref[...] + lb_ref[...]
        sig = _sigmoid(z2)
        dz2 = dz3_ref[...].astype(F32) * (sig * (1.0 + z2 * (1.0 - sig)))
        dxh = dz2 * lg_ref[...]
        dz1 = rs * (dxh - jnp.mean(dxh, axis=-1, keepdims=True) - xh * jnp.mean(dxh * xh, axis=-1, keepdims=True))
        dz1_ref[...] = dz1

        @pl.when(i == 0)
        def _():
            ddw_ref[...] = jnp.zeros_like(ddw_ref)
            dbias_ref[...] = jnp.zeros_like(dbias_ref)
            dlg_ref[...] = jnp.zeros_like(dlg_ref)
            dlb_ref[...] = jnp.zeros_like(dlb_ref)

        dlg_ref[...] += jnp.sum(dz2 * xh, axis=0, keepdims=True)
        dlb_ref[...] += jnp.sum(dz2, axis=0, keepdims=True)
        dbias_ref[...] += jnp.sum(dz1, axis=0, keepdims=True)
        for cc in range(ch // LANES):
            lanes = pl.ds(cc * LANES, LANES)
            accs = [jnp.zeros((8, LANES), F32) for _ in range(CONV_WIDTH)]
            for rb in range(tm // CONV_ROWS):
                dzc = dz1_ref[pl.ds(rb * CONV_ROWS, CONV_ROWS), lanes]
                for j in range(CONV_WIDTH):
                    prod = dzc * zp_ref[pl.ds(rb * CONV_ROWS + first + j, CONV_ROWS), lanes]
                    accs[j] = accs[j] + jnp.sum(prod.reshape(CONV_ROWS // 8, 8, LANES), axis=0)
            for j in range(CONV_WIDTH):
                ddw_ref[pl.ds(j, 1), lanes] += jnp.sum(accs[j], axis=0, keepdims=True)

    vec = pl.BlockSpec((1, ch), lambda i: (0, 0))
    return pl.pallas_call(
        body, name="conv_bwd_ln", grid=(t // tm,),
        in_specs=_conv_specs(tm, ch) + [pl.BlockSpec((tm, ch), lambda i: (i, 0)),
                                        pl.BlockSpec((CONV_PAD, ch), lambda i: (0, 0)), vec, vec, vec],
        out_specs=[pl.BlockSpec((tm, ch), lambda i: (i, 0)), pl.BlockSpec((CONV_PAD, ch), lambda i: (0, 0)), vec, vec, vec],
        out_shape=[jax.ShapeDtypeStruct((t, ch), F32), jax.ShapeDtypeStruct((CONV_PAD, ch), F32)]
        + [jax.ShapeDtypeStruct((1, ch), F32)] * 3,
        scratch_shapes=[pltpu.VMEM((CONV_PAD + tm, ch), F32), pltpu.VMEM((tm, ch), F32)],
        compiler_params=_params(("arbitrary",)),
    )(p, p, p, p, dz3, dw, bias, ln_g, ln_b)


def _conv_bwd_glu(p, dz1, dw):
    t = p.shape[0]
    ch = D_MODEL
    tm = min(CONV_TILE, t)
    per = tm // CONV_PAD
    n_halo = t // CONV_PAD
    cb = COL_CONV_G // ch

    def body(a_ref, g_ref, dz_ref, dzn_ref, dw_ref, o_ref, zp_ref, z0_ref):
        i = pl.program_id(0)
        zp_ref[pl.ds(0, tm), :] = dz_ref[...]
        zp_ref[pl.ds(tm, CONV_PAD), :] = jnp.where(i < t // tm - 1, dzn_ref[...], 0.0)
        for cc in range(ch // LANES):
            lanes = pl.ds(cc * LANES, LANES)
            w = [dw_ref[pl.ds(j, 1), lanes] for j in range(CONV_WIDTH)]
            for rb in range(tm // CONV_ROWS):
                acc = jnp.zeros((CONV_ROWS, LANES), F32)
                for j in range(CONV_WIDTH):
                    acc = acc + w[j] * zp_ref[pl.ds(rb * CONV_ROWS + (CONV_WIDTH - 1 - j), CONV_ROWS), lanes]
                z0_ref[pl.ds(rb * CONV_ROWS, CONV_ROWS), lanes] = acc
        dz0 = z0_ref[...]
        a = a_ref[...].astype(F32)
        sig = _sigmoid(g_ref[...].astype(F32))
        o_ref[:, pl.ds(0, ch)] = (dz0 * sig).astype(BF16)
        o_ref[:, pl.ds(ch, ch)] = (dz0 * a * sig * (1.0 - sig)).astype(BF16)

    return pl.pallas_call(
        body, name="conv_bwd_glu", grid=(t // tm,),
        in_specs=[pl.BlockSpec((tm, ch), lambda i: (i, 0)), pl.BlockSpec((tm, ch), lambda i: (i, cb)),
                  pl.BlockSpec((tm, ch), lambda i: (i, 0)),
                  pl.BlockSpec((CONV_PAD, ch), lambda i: (jnp.minimum((i + 1) * per, n_halo - 1), 0)),
                  pl.BlockSpec((CONV_PAD, ch), lambda i: (0, 0))],
        out_specs=pl.BlockSpec((tm, 2 * ch), lambda i: (i, 0)),
        out_shape=jax.ShapeDtypeStruct((t, 2 * ch), BF16),
        scratch_shapes=[pltpu.VMEM((tm + CONV_PAD, ch), F32), pltpu.VMEM((tm, ch), F32)],
        compiler_params=_params(("parallel",)),
    )(p, p, dz1, dz1, dw)


def _bucket_onehot():
    qi = jnp.arange(BLOCK, dtype=jnp.int32)[:, None]
    kj = jnp.arange(2 * BLOCK, dtype=jnp.int32)[None, :]
    dist = jnp.maximum(qi + BLOCK - kj, 0)
    max_exact = N_BUCKETS // 2
    dflt = jnp.maximum(dist, 1).astype(F32)
    large = max_exact + (jnp.log(dflt / max_exact) / math.log(MAX_DISTANCE / max_exact)
                         * (N_BUCKETS - max_exact)).astype(jnp.int32)
    large = jnp.minimum(large, N_BUCKETS - 1)
    bucket = jnp.where(dist < max_exact, dist, large)
    onehot = bucket[None] == jnp.arange(N_BUCKETS, dtype=jnp.int32)[:, None, None]
    return onehot.astype(F32).reshape(N_BUCKETS, BLOCK * 2 * BLOCK)


def _bias_table(rel_bias_t, onehot):
    def body(r_ref, oh_ref, o_ref):
        o_ref[...] = _dot(r_ref[...], oh_ref[...], precision=lax.Precision.HIGHEST)

    n = onehot.shape[1]
    tn = 4096
    return pl.pallas_call(
        body, name="bias_table", grid=(n // tn,),
        in_specs=[pl.BlockSpec((N_Q_HEADS, N_BUCKETS), lambda i: (0, 0)), pl.BlockSpec((N_BUCKETS, tn), lambda i: (0, i))],
        out_specs=pl.BlockSpec((N_Q_HEADS, tn), lambda i: (0, i)),
        out_shape=jax.ShapeDtypeStruct((N_Q_HEADS, n), F32),
        compiler_params=_params(("parallel",)),
    )(rel_bias_t, onehot)


def _bias_table_bwd(dbias, onehot):
    n = onehot.shape[1]
    tn = 4096

    def body(d_ref, oh_ref, o_ref):
        part = _dot(d_ref[...], oh_ref[...], trans_b=True, precision=lax.Precision.HIGHEST)
        i = pl.program_id(0)

        @pl.when(i == 0)
        def _():
            o_ref[...] = part

        @pl.when(i > 0)
        def _():
            o_ref[...] += part

    return pl.pallas_call(
        body, name="bias_table_bwd", grid=(n // tn,),
        in_specs=[pl.BlockSpec((N_Q_HEADS, tn), lambda i: (0, i)), pl.BlockSpec((N_BUCKETS, tn), lambda i: (0, i))],
        out_specs=pl.BlockSpec((N_Q_HEADS, N_BUCKETS), lambda i: (0, 0)),
        out_shape=jax.ShapeDtypeStruct((N_Q_HEADS, N_BUCKETS), F32),
        compiler_params=_params(("arbitrary",)),
    )(dbias, onehot)


def _attn_probs(q_ref, kp_ref, kc_ref, gq_ref, gk_ref, sink_ref, bias_ref, n):
    qf = q_ref[...].astype(F32)
    rq = lax.rsqrt(jnp.mean(qf * qf, axis=-1, keepdims=True) + EPS)
    qn = qf * rq * gq_ref[...]
    kf = jnp.concatenate([kp_ref[...], kc_ref[...]], axis=0).astype(F32)
    rk = lax.rsqrt(jnp.mean(kf * kf, axis=-1, keepdims=True) + EPS)
    kn = kf * rk * gk_ref[...]
    s = _dot(qn.astype(BF16), kn.astype(BF16), trans_b=True) * (1.0 / math.sqrt(HEAD_DIM)) + bias_ref[...]
    row = lax.broadcasted_iota(jnp.int32, (QROWS, 2 * BLOCK), 0) & (BLOCK - 1)
    col = lax.broadcasted_iota(jnp.int32, (QROWS, 2 * BLOCK), 1)
    dist = row + BLOCK - col
    valid = (dist >= 0) & (dist < BLOCK) & ((col >= BLOCK) | (n > 0))
    s = jnp.where(valid, s, NEG)
    sink = sink_ref[...]
    m = jnp.maximum(jnp.max(s, axis=-1, keepdims=True), sink)
    p = jnp.exp(s - m)
    es = jnp.exp(sink - m)
    inv = 1.0 / (jnp.sum(p, axis=-1, keepdims=True) + es)
    return qf, rq, qn, kf, rk, kn, p * inv, es * inv


def _attn_specs(nb):
    qspec = pl.BlockSpec((None, None, QROWS, HEAD_DIM), lambda h, n: (h, n, 0, 0))
    kprev = pl.BlockSpec((None, BLOCK, HEAD_DIM), lambda h, n: (h, jnp.maximum(n - 1, 0), 0))
    kcur = pl.BlockSpec((None, BLOCK, HEAD_DIM), lambda h, n: (h, n, 0))
    gain = pl.BlockSpec((1, HEAD_DIM), lambda h, n: (0, 0))
    sink = pl.BlockSpec((None, QROWS, 1), lambda h, n: (h, 0, 0))
    bias = pl.BlockSpec((None, QROWS, 2 * BLOCK), lambda h, n: (h, 0, 0))
    return qspec, kprev, kcur, gain, sink, bias


def _attn_fwd(q4, k3, v3, gq, gk, sink_rows, bias):
    nb = q4.shape[1]
    qspec, kprev, kcur, gain, sink, bspec = _attn_specs(nb)

    def body(q_ref, kp_ref, kc_ref, vp_ref, vc_ref, gq_ref, gk_ref, sink_ref, bias_ref, o_ref):
        n = pl.program_id(1)
        pn = _attn_probs(q_ref, kp_ref, kc_ref, gq_ref, gk_ref, sink_ref, bias_ref, n)[6]
        v = jnp.concatenate([vp_ref[...], vc_ref[...]], axis=0)
        o_ref[...] = _dot(pn.astype(BF16), v).astype(BF16)

    return pl.pallas_call(
        body, name="attn_fwd", grid=(N_KV_HEADS, nb),
        in_specs=[qspec, kprev, kcur, kprev, kcur, gain, gain, sink, bspec],
        out_specs=qspec, out_shape=jax.ShapeDtypeStruct(q4.shape, BF16),
        compiler_params=_params(("parallel", "parallel")),
    )(q4, k3, k3, v3, v3, gq, gk, sink_rows, bias)


def _attn_bwd(q4, k3, v3, do4, gq, gk, sink_rows, bias):
    nb = q4.shape[1]
    t = k3.shape[1]
    qspec, kprev, kcur, gain, sink, bspec = _attn_specs(nb)
    scale = 1.0 / math.sqrt(HEAD_DIM)

    def rms_bwd(dn, xf, r, g):
        w = dn * g
        dx = r * w - xf * (r * r * r) * jnp.mean(xf * w, axis=-1, keepdims=True)
        return dx, jnp.sum(dn * (xf * r), axis=0, keepdims=True)

    def body(q_ref, kp_ref, kc_ref, vp_ref, vc_ref, do_ref, gq_ref, gk_ref, sink_ref, bias_ref,
             dq_ref, dk_ref, dv_ref, dbias_ref, dsink_ref, dgq_ref, dgk_ref):
        n = pl.program_id(1)
        qf, rq, qn, kf, rk, kn, pn, psink = _attn_probs(q_ref, kp_ref, kc_ref, gq_ref, gk_ref, sink_ref, bias_ref, n)
        do = do_ref[...]
        v = jnp.concatenate([vp_ref[...], vc_ref[...]], axis=0)
        dv_win = _dot(pn.astype(BF16), do, trans_a=True)
        dp = _dot(do, v, trans_b=True)
        delta = jnp.sum(pn * dp, axis=-1, keepdims=True)
        ds = pn * (dp - delta)
        dsink = jnp.sum((-psink * delta).reshape(GROUP, BLOCK, 1), axis=1)
        dsc = (ds * scale).astype(BF16)
        dqn = _dot(dsc, kn.astype(BF16))
        dkn = _dot(dsc, qn.astype(BF16), trans_a=True)
        dq, dgq = rms_bwd(dqn, qf, rq, gq_ref[...])
        dk_win, dgk = rms_bwd(dkn, kf, rk, gk_ref[...])
        dq_ref[...] = dq.astype(BF16)

        @pl.when(n == 0)
        def _():
            dbias_ref[...] = ds
            dsink_ref[...] = dsink
            dgq_ref[...] = dgq
            dgk_ref[...] = dgk

        @pl.when(n > 0)
        def _():
            dbias_ref[...] += ds
            dsink_ref[...] += dsink
            dgq_ref[...] += dgq
            dgk_ref[...] += dgk
            prev = pl.ds(pl.multiple_of((n - 1) * BLOCK, BLOCK), BLOCK)
            dk_ref[prev, :] += dk_win[:BLOCK]
            dv_ref[prev, :] += dv_win[:BLOCK]

        cur = pl.ds(pl.multiple_of(n * BLOCK, BLOCK), BLOCK)
        dk_ref[cur, :] = dk_win[BLOCK:]
        dv_ref[cur, :] = dv_win[BLOCK:]

    kv_out = pl.BlockSpec((None, t, HEAD_DIM), lambda h, n: (h, 0, 0))
    gain_out = pl.BlockSpec((None, 1, HEAD_DIM), lambda h, n: (h, 0, 0))
    return pl.pallas_call(
        body, name="attn_bwd", grid=(N_KV_HEADS, nb),
        in_specs=[qspec, kprev, kcur, kprev, kcur, qspec, gain, gain, sink, bspec],
        out_specs=[qspec, kv_out, kv_out, bspec,
                   pl.BlockSpec((None, GROUP, 1), lambda h, n: (h, 0, 0)), gain_out, gain_out],
        out_shape=[jax.ShapeDtypeStruct(q4.shape, BF16),
                   jax.ShapeDtypeStruct((N_KV_HEADS, t, HEAD_DIM), F32),
                   jax.ShapeDtypeStruct((N_KV_HEADS, t, HEAD_DIM), F32),
                   jax.ShapeDtypeStruct((N_KV_HEADS, QROWS, 2 * BLOCK), F32),
                   jax.ShapeDtypeStruct((N_KV_HEADS, GROUP, 1), F32),
                   jax.ShapeDtypeStruct((N_KV_HEADS, 1, HEAD_DIM), F32),
                   jax.ShapeDtypeStruct((N_KV_HEADS, 1, HEAD_DIM), F32)],
        compiler_params=_params(("arbitrary", "arbitrary")),
    )(q4, k3, k3, v3, v3, do4, gq, gk, sink_rows, bias)


def _q_to_heads(cols, t):
    nb = t // BLOCK
    return cols.reshape(nb, BLOCK, N_KV_HEADS, GROUP, HEAD_DIM).transpose(2, 0, 3, 1, 4).reshape(N_KV_HEADS, nb, QROWS, HEAD_DIM)


def _q_from_heads(q4, t):
    nb = t // BLOCK
    return q4.reshape(N_KV_HEADS, nb, GROUP, BLOCK, HEAD_DIM).transpose(1, 3, 0, 2, 4).reshape(t, N_Q_HEADS * HEAD_DIM)


def _kv_to_heads(cols, t):
    return cols.reshape(t, N_KV_HEADS, HEAD_DIM).transpose(1, 0, 2)


def _kv_from_heads(k3, t):
    return k3.transpose(1, 0, 2).reshape(t, N_KV_HEADS * HEAD_DIM)


GATE_TILE = 512


def _merge_fwd(z3, o, p, w_proj, w_o):
    t, d = z3.shape
    tm = min(ROW_TILE, t)
    tn = GATE_TILE

    def body(z_ref, o_ref, gc_ref, ga_ref, wp_ref, wo_ref, m_ref, a_ref, b_ref):
        a = _dot(z_ref[...], wp_ref[...])
        b = _dot(o_ref[...], wo_ref[...])
        m_ref[...] = (_sigmoid(gc_ref[...].astype(F32)) * a + _sigmoid(ga_ref[...].astype(F32)) * b).astype(BF16)
        a_ref[...] = a.astype(BF16)
        b_ref[...] = b.astype(BF16)

    row = pl.BlockSpec((tm, d), lambda i, j: (i, 0))
    wspec = pl.BlockSpec((d, tn), lambda i, j: (0, j))
    ospec = pl.BlockSpec((tm, tn), lambda i, j: (i, j))
    return pl.pallas_call(
        body, name="merge_fwd", grid=(t // tm, d // tn),
        in_specs=[row, row,
                  pl.BlockSpec((tm, tn), lambda i, j: (i, COL_GC // tn + j)),
                  pl.BlockSpec((tm, tn), lambda i, j: (i, COL_GA // tn + j)), wspec, wspec],
        out_specs=[ospec, ospec, ospec],
        out_shape=[jax.ShapeDtypeStruct((t, d), BF16)] * 3,
        compiler_params=_params(("parallel", "parallel")),
    )(z3, o, p, p, w_proj, w_o)


def _merge_bwd(dres, w_out, a, b, p, tokens=()):
    t, d = dres.shape
    tm = min(ROW_TILE, t)
    tn = GATE_TILE

    def epilogue(acc, ex, outs, ids):
        a_ref, b_ref, gc_ref, ga_ref = ex[:4]
        sc = _sigmoid(gc_ref[...].astype(F32))
        sa = _sigmoid(ga_ref[...].astype(F32))
        outs[0][...] = (acc * sc).astype(BF16)
        outs[1][...] = (acc * sa).astype(BF16)
        outs[2][0] = (acc * a_ref[...].astype(F32) * sc * (1.0 - sc)).astype(BF16)
        outs[2][1] = (acc * b_ref[...].astype(F32) * sa * (1.0 - sa)).astype(BF16)

    ospec = pl.BlockSpec((tm, tn), lambda i, j, kk: (i, j))
    return _mm("merge_bwd", (t // tm, d // tn, 1),
               dres, pl.BlockSpec((tm, d), lambda i, j, kk: (i, 0)),
               w_out, pl.BlockSpec((tn, d), lambda i, j, kk: (j, 0)), (tm, tn),
               trans_b=True, a_pre=_to_bf16,
               extras=(a, b, p, p),
               extra_specs=(ospec, ospec,
                            pl.BlockSpec((tm, tn), lambda i, j, kk: (i, COL_GC // tn + j)),
                            pl.BlockSpec((tm, tn), lambda i, j, kk: (i, COL_GA // tn + j))), tokens=tokens,
               out_shape=(jax.ShapeDtypeStruct((t, d), BF16), jax.ShapeDtypeStruct((t, d), BF16),
                          jax.ShapeDtypeStruct((2, t, d), BF16)),
               out_specs=(ospec, ospec, pl.BlockSpec((2, tm, tn), lambda i, j, kk: (0, i, j))),
               epilogue=epilogue)


def _store_epilogue(acc, ex, outs, ids):
    outs[0][...] = acc


def _store_bf16_epilogue(acc, ex, outs, ids):
    outs[0][...] = acc.astype(BF16)


def _mm_nt(name, a, w, out_dtype=BF16):
    t, n = a.shape
    k = w.shape[0]
    tm = min(ROW_TILE, t)
    return _mm(name, (t // tm, 1, 1), a, pl.BlockSpec((tm, n), lambda i, j, kk: (i, 0)),
               w, pl.BlockSpec((k, n), lambda i, j, kk: (0, 0)), (tm, k), trans_b=True,
               out_shape=(jax.ShapeDtypeStruct((t, k), out_dtype),),
               out_specs=(pl.BlockSpec((tm, k), lambda i, j, kk: (i, 0)),),
               epilogue=_store_bf16_epilogue if out_dtype == BF16 else _store_epilogue)[0]


def _mm_tn(name, a, b, b_pre=None):
    t, m = a.shape
    n = b.shape[1]
    tk = min(ROW_TILE, t)
    return _mm(name, (1, 1, t // tk), a, pl.BlockSpec((tk, m), lambda i, j, kk: (kk, 0)),
               b, pl.BlockSpec((tk, n), lambda i, j, kk: (kk, 0)), (m, n), trans_a=True, b_pre=b_pre,
               out_shape=(jax.ShapeDtypeStruct((m, n), F32),),
               out_specs=(pl.BlockSpec((m, n), lambda i, j, kk: (0, 0)),), epilogue=_store_epilogue)[0]


def _local_step(x, target, small, comm):
    t = x.shape[0]
    w = dict(small)

    w.update(comm.weights("A", None))
    x1, ffn1_saved = _ffn_fwd("ffn1", x, w["ffn1_norm"], w["ffn1_w_in"], w["ffn1_w_out"], comm.tokens)
    w.update(comm.weights("B", x1))
    hm = _rmsnorm_fwd("mix_norm", x1, w["mix_norm"])
    tm = min(ROW_TILE, t)
    p = _mm("mix_in", (N_CHIPS, t // tm, 1),
            hm, pl.BlockSpec((tm, D_MODEL), lambda j, i, kk: (i, 0)),
            w["w_in"], pl.BlockSpec((None, D_MODEL, SHARD_W), lambda j, i, kk: (j, 0, 0)), (tm, SHARD_W),
            out_shape=(jax.ShapeDtypeStruct((t, IN_W), BF16),),
            out_specs=(pl.BlockSpec((tm, SHARD_W), lambda j, i, kk: (i, j)),),
            epilogue=_store_bf16_epilogue)[0]

    z3 = _conv_fwd(p, w["conv_dw_kernel"], w["conv_dw_bias"], w["conv_ln_g"], w["conv_ln_b"])

    onehot = _bucket_onehot()
    bias = _bias_table(w["rel_bias"].T, onehot).reshape(N_KV_HEADS, QROWS, 2 * BLOCK)
    sink_rows = jnp.repeat(w["attn_sinks"].reshape(N_KV_HEADS, GROUP), BLOCK, axis=1)[..., None]
    q4 = _q_to_heads(p[:, COL_Q:COL_K], t)
    k3 = _kv_to_heads(p[:, COL_K:COL_V], t)
    v3 = _kv_to_heads(p[:, COL_V:COL_GC], t)
    o4 = _attn_fwd(q4, k3, v3, w["q_norm"], w["k_norm"], sink_rows, bias)
    o = _q_from_heads(o4, t)

    merged, a, b = _merge_fwd(z3, o, p, w["conv_w_proj"], w["attn_w_o"])
    x2 = _mm_residual("mix_out", merged, w["w_out"], x1, 1.0)
    w.update(comm.weights("C", x2))
    x3, ffn2_saved = _ffn_fwd("ffn2", x2, w["ffn2_norm"], w["ffn2_w_in"], w["ffn2_w_out"])
    dy, loss = _loss_head(x3, target)

    g, big = {}, {}
    dres2, big["ffn2_w_in"], big["ffn2_w_out"], g["ffn2_norm"] = _ffn_bwd(
        "ffn2b", dy, x2, w["ffn2_norm"], ffn2_saved, w["ffn2_w_in"], w["ffn2_w_out"])
    tokens = comm.reduce_start("R1", big)

    da, db, dgates = _merge_bwd(dres2, w["w_out"], a, b, p, tokens)
    big = {}
    big["w_out"] = _mm_tn("d_w_out", merged, dres2, b_pre=_to_bf16)
    big["conv_w_proj"] = _mm_tn("d_w_proj", z3, da)
    big["attn_w_o"] = _mm_tn("d_w_o", o, db)
    dz3 = _mm_nt("d_z3", da, w["conv_w_proj"])
    do = _mm_nt("d_o", db, w["attn_w_o"])

    dq4, dk3, dv3, dbias, dsink, dgq, dgk = _attn_bwd(q4, k3, v3, _q_to_heads(do, t), w["q_norm"], w["k_norm"], sink_rows, bias)
    g["rel_bias"] = _bias_table_bwd(dbias.reshape(N_Q_HEADS, BLOCK * 2 * BLOCK), onehot).T
    g["attn_sinks"] = dsink.reshape(N_Q_HEADS)
    g["q_norm"] = jnp.sum(dgq, axis=0)
    g["k_norm"] = jnp.sum(dgk, axis=0)

    dz1, big["conv_dw_kernel"], g["conv_dw_bias"], g["conv_ln_g"], g["conv_ln_b"] = _conv_bwd_ln(
        p, dz3, w["conv_dw_kernel"], w["conv_dw_bias"], w["conv_ln_g"], w["conv_ln_b"])
    dconv = _conv_bwd_glu(p, dz1, w["conv_dw_kernel"])

    dp = jnp.concatenate([dconv, _q_from_heads(dq4, t), _kv_from_heads(dk3, t).astype(BF16),
                          _kv_from_heads(dv3, t).astype(BF16), dgates[0], dgates[1]], axis=1)
    big["w_in"] = _mm("d_w_in", (1, N_CHIPS, t // tm),
                    hm, pl.BlockSpec((tm, D_MODEL), lambda i, j, kk: (kk, 0)),
                    dp, pl.BlockSpec((tm, SHARD_W), lambda i, j, kk: (kk, j)), (D_MODEL, SHARD_W),
                    trans_a=True,
                    out_shape=(jax.ShapeDtypeStruct((N_CHIPS, D_MODEL, SHARD_W), F32),),
                    out_specs=(pl.BlockSpec((None, D_MODEL, SHARD_W), lambda i, j, kk: (j, 0, 0)),),
                    epilogue=_store_epilogue)[0]
    dres1, g["mix_norm"] = _mm("d_mix", (t // tm, 1, N_CHIPS),
                               dp, pl.BlockSpec((tm, SHARD_W), lambda i, j, kk: (i, kk)),
                               w["w_in"], pl.BlockSpec((None, D_MODEL, SHARD_W), lambda i, j, kk: (kk, 0, 0)),
                               (tm, D_MODEL), trans_b=True,
                               extras=(x1, w["mix_norm"], dres2),
                               extra_specs=(pl.BlockSpec((tm, D_MODEL), lambda i, j, kk: (i, 0)),
                                            pl.BlockSpec((1, D_MODEL), lambda i, j, kk: (0, 0)),
                                            pl.BlockSpec((tm, D_MODEL), lambda i, j, kk: (i, 0))),
                               out_shape=(jax.ShapeDtypeStruct((t, D_MODEL), F32), jax.ShapeDtypeStruct((1, D_MODEL), F32)),
                               out_specs=(pl.BlockSpec((tm, D_MODEL), lambda i, j, kk: (i, 0)),
                                          pl.BlockSpec((1, D_MODEL), lambda i, j, kk: (0, 0))),
                               epilogue=_rms_bwd_epilogue, sem=("arbitrary", "arbitrary", "arbitrary"))

    comm.reduce_finish("R1", dres1)
    tokens = comm.reduce_start("R2", big)

    def ffn1_grads(dw_in4, dw_out):
        return comm.reduce_start("R3", {"ffn1_w_in": dw_in4, "ffn1_w_out": dw_out})

    grad_x, _, _, g["ffn1_norm"] = _ffn_bwd(
        "ffn1b", dres1, x, w["ffn1_norm"], ffn1_saved, w["ffn1_w_in"], w["ffn1_w_out"], tokens, ffn1_grads)
    comm.reduce_finish("R2", grad_x)
    comm.reduce_finish("R3", grad_x)
    return loss[0, 0], grad_x, g


def _mesh_place():
    x, y, c = lax.axis_index("x"), lax.axis_index("y"), lax.axis_index("c")
    chips = [(1 - x, y), (x, 1 - y), (1 - x, 1 - y)]
    return x, y, c, chips


def _any_specs(n):
    return [pl.BlockSpec(memory_space=pl.ANY)] * n


HBM_SPEC = pl.BlockSpec(memory_space=pltpu.HBM)
SEM_SPEC = pl.BlockSpec(memory_space=pltpu.SEMAPHORE)
EFFECT = pltpu.SideEffectType.DATAFLOW_SIDE_EFFECTING


def _in_hbm(a):
    return pltpu.with_memory_space_constraint(a, pltpu.HBM)


def _copy_start(name, srcs, lands, plan, after=()):
    ns, nb = len(srcs), len(lands)
    n = 3 * ns

    def body(*refs):
        s_refs, l_refs = refs[:ns], refs[ns:ns + nb]
        send_sems, recv_sems = refs[ns + nb + len(after)], refs[ns + nb + len(after) + 1]
        token = refs[-1]
        for k, (src, dst, to, _) in enumerate(plan(s_refs, l_refs)):
            pltpu.make_async_remote_copy(src_ref=src, dst_ref=dst, send_sem=send_sems.at[k], recv_sem=recv_sems.at[k],
                                         device_id=to, device_id_type=MESH).start()
        token[...] = jnp.zeros_like(token)

    bufs = list(srcs) + list(lands)
    outs = pl.pallas_call(
        body, name=name,
        out_shape=(pltpu.SemaphoreType.DMA((n,)), pltpu.SemaphoreType.DMA((n,)),
                   *[pltpu.HBM(a.shape, a.dtype) for a in bufs], jax.ShapeDtypeStruct((8, LANES), F32)),
        in_specs=[HBM_SPEC] * len(bufs) + [pl.BlockSpec(memory_space=pl.ANY)] * len(after),
        out_specs=(SEM_SPEC, SEM_SPEC, *[HBM_SPEC] * len(bufs), pl.BlockSpec(memory_space=pltpu.VMEM)),
        input_output_aliases={i: 2 + i for i in range(len(bufs))},
        compiler_params=pltpu.CompilerParams(has_side_effects=EFFECT),
    )(*[_in_hbm(a) for a in bufs], *after)
    return outs[0], outs[1], list(outs[2:2 + ns]), list(outs[2 + ns:2 + ns + nb]), outs[-1]


def _copy_wait(name, send_sems, recv_sems, srcs, lands, after, plan):
    ns, nb = len(srcs), len(lands)

    def body(*refs):
        s_refs, l_refs = refs[:ns], refs[ns:ns + nb]
        send_sems, recv_sems = refs[ns + nb], refs[ns + nb + 1]
        for k, (src, _, to, mine) in enumerate(plan(s_refs, l_refs)):
            cp = pltpu.make_async_remote_copy(src_ref=src, dst_ref=mine, send_sem=send_sems.at[k], recv_sem=recv_sems.at[k],
                                              device_id=to, device_id_type=MESH)
            cp.wait_send()
            cp.wait_recv()

    bufs = list(srcs) + list(lands)
    outs = pl.pallas_call(
        body, name=name,
        out_shape=tuple(pltpu.HBM(a.shape, a.dtype) for a in bufs),
        in_specs=[HBM_SPEC] * len(bufs) + [SEM_SPEC, SEM_SPEC, pl.BlockSpec(memory_space=pl.ANY)],
        out_specs=tuple([HBM_SPEC] * len(bufs)),
        input_output_aliases={i: i for i in range(len(bufs))},
        compiler_params=pltpu.CompilerParams(has_side_effects=EFFECT),
    )(*bufs, send_sems, recv_sems, after)
    return list(outs[:ns]), list(outs[ns:])


def _gather_plan(s_refs, l_refs):
    x, y, c, chips = _mesh_place()
    jme = 2 * x + y
    return [(s.at[c], land.at[jme, c], (*chip, c), land.at[2 * chip[0] + chip[1], c])
            for s, land in zip(s_refs, l_refs) for chip in chips]


def _scatter_plan(s_refs, l_refs):
    x, y, c, chips = _mesh_place()
    return [(s.at[2 * chip[0] + chip[1]], land.at[k], (*chip, c), land.at[k])
            for s, land in zip(s_refs, l_refs) for k, chip in enumerate(chips)]


def _gather_forward(name, shards, landed):
    nw = len(shards)

    def body(*refs):
        s_refs, o_refs = refs[:nw], refs[2 * nw:3 * nw]
        send_sems, recv_sems, local_sems = refs[3 * nw:]
        x, y, c, chips = _mesh_place()
        me, sib, jme = (x, y, c), (x, y, 1 - c), 2 * x + y
        local = [pltpu.make_async_copy(s_refs[w], o_refs[w].at[jme], local_sems.at[w]) for w in range(nw)]
        for cp in local:
            cp.start()
        sent = []
        for w in range(nw):
            for k, chip in enumerate(chips):
                part = o_refs[w].at[2 * chip[0] + chip[1], c]
                cp = pltpu.make_async_remote_copy(src_ref=part, dst_ref=part, send_sem=send_sems.at[3 * w + k],
                                                  recv_sem=recv_sems.at[3 * w + k], device_id=sib, device_id_type=MESH)
                cp.start()
                sent.append(cp)
        for w in range(nw):
            for k, chip in enumerate(chips):
                part = o_refs[w].at[2 * chip[0] + chip[1], 1 - c]
                pltpu.make_async_remote_copy(src_ref=part, dst_ref=part, send_sem=send_sems.at[3 * w + k],
                                             recv_sem=recv_sems.at[3 * w + k], device_id=me, device_id_type=MESH).wait_recv()
        for cp in sent:
            cp.wait_send()
        for cp in local:
            cp.wait()

    return pl.pallas_call(
        body, name=name,
        in_specs=_any_specs(2 * nw), out_specs=_any_specs(nw),
        out_shape=[jax.ShapeDtypeStruct(a.shape, a.dtype) for a in landed],
        input_output_aliases={nw + i: i for i in range(nw)},
        scratch_shapes=[pltpu.SemaphoreType.DMA((3 * nw,)), pltpu.SemaphoreType.DMA((3 * nw,)),
                        pltpu.SemaphoreType.DMA((nw,))],
    )(*shards, *landed)


def _exchange_halves(name, grads):
    nw = len(grads)

    def body(*refs):
        g_refs, o_refs = refs[:nw], refs[nw:2 * nw]
        send_sems, recv_sems = refs[2 * nw:]
        x, y, c, _ = _mesh_place()
        copies = []
        for w in range(nw):
            cp = pltpu.make_async_remote_copy(src_ref=g_refs[w].at[:, 1 - c], dst_ref=o_refs[w], send_sem=send_sems.at[w],
                                              recv_sem=recv_sems.at[w], device_id=(x, y, 1 - c), device_id_type=MESH)
            cp.start()
            copies.append(cp)
        for cp in copies:
            cp.wait()

    return pl.pallas_call(
        body, name=name,
        in_specs=_any_specs(nw), out_specs=_any_specs(nw),
        out_shape=[jax.ShapeDtypeStruct((N_CHIPS,) + g.shape[2:], F32) for g in grads],
        scratch_shapes=[pltpu.SemaphoreType.DMA((nw,)), pltpu.SemaphoreType.DMA((nw,))],
    )(*grads)


def _row_tile(r):
    for cand in (256, 176, 128, 64, 32, 16, 8):
        if r % cand == 0:
            return cand
    return r


def _add_own_half(c_idx, grad, got):
    _, _, r, cols = grad.shape
    tr = _row_tile(r)

    def body(c_ref, g_ref, o_ref, out_ref):
        out_ref[...] = (g_ref[...] + o_ref[...]).astype(BF16)

    return pl.pallas_call(
        body, name="add_own_half",
        grid_spec=pltpu.PrefetchScalarGridSpec(
            num_scalar_prefetch=1, grid=(N_CHIPS, r // tr),
            in_specs=[pl.BlockSpec((None, None, tr, cols), lambda j, i, c_ref: (j, c_ref[0], i, 0)),
                      pl.BlockSpec((None, tr, cols), lambda j, i, c_ref: (j, i, 0))],
            out_specs=pl.BlockSpec((None, tr, cols), lambda j, i, c_ref: (j, i, 0))),
        out_shape=jax.ShapeDtypeStruct((N_CHIPS, r, cols), BF16),
        compiler_params=_params(("parallel", "parallel")),
    )(c_idx, grad, got)


def _sum_pieces(chip_idx, sums, landed):
    _, r, cols = sums.shape
    tr = _row_tile(r)

    def body(j_ref, own_ref, p_ref, o_ref):
        o_ref[...] = ((own_ref[...].astype(F32) + p_ref[0].astype(F32)) + p_ref[1].astype(F32)) + p_ref[2].astype(F32)

    return pl.pallas_call(
        body, name="sum_pieces",
        grid_spec=pltpu.PrefetchScalarGridSpec(
            num_scalar_prefetch=1, grid=(r // tr,),
            in_specs=[pl.BlockSpec((None, tr, cols), lambda i, j_ref: (j_ref[0], i, 0)),
                      pl.BlockSpec((N_CHIPS - 1, tr, cols), lambda i, j_ref: (0, i, 0))],
            out_specs=pl.BlockSpec((tr, cols), lambda i, j_ref: (i, 0))),
        out_shape=jax.ShapeDtypeStruct((r, cols), F32),
        compiler_params=_params(("parallel",)),
    )(chip_idx, sums, landed)


def _join_halves(name, halves):
    nw = len(halves)

    def body(*refs):
        h_refs, o_refs = refs[:nw], refs[nw:2 * nw]
        send_sems, recv_sems, local_sems = refs[2 * nw:]
        x, y, c, _ = _mesh_place()
        local = [pltpu.make_async_copy(h_refs[w], o_refs[w].at[c], local_sems.at[w]) for w in range(nw)]
        for cp in local:
            cp.start()
        copies = []
        for w in range(nw):
            cp = pltpu.make_async_remote_copy(src_ref=h_refs[w], dst_ref=o_refs[w].at[c], send_sem=send_sems.at[w],
                                              recv_sem=recv_sems.at[w], device_id=(x, y, 1 - c), device_id_type=MESH)
            cp.start()
            copies.append(cp)
        for w in range(nw):
            copies[w].wait_send()
            landed = o_refs[w].at[1 - c]
            pltpu.make_async_remote_copy(src_ref=landed, dst_ref=landed, send_sem=send_sems.at[w], recv_sem=recv_sems.at[w],
                                         device_id=(x, y, c), device_id_type=MESH).wait_recv()
        for cp in local:
            cp.wait()

    return pl.pallas_call(
        body, name=name,
        in_specs=_any_specs(nw), out_specs=_any_specs(nw),
        out_shape=[jax.ShapeDtypeStruct((2,) + h.shape, F32) for h in halves],
        scratch_shapes=[pltpu.SemaphoreType.DMA((nw,)), pltpu.SemaphoreType.DMA((nw,)), pltpu.SemaphoreType.DMA((nw,))],
    )(*halves)


SMALL_ROWS = 8


def _all_reduce_small(pack):
    rows, cols = pack.shape
    n_dev = 8

    def body(p_ref, o_ref, slots, send_sems, recv_sems):
        x, y, c, _ = _mesh_place()
        me = 4 * x + 2 * y + c
        slots[me] = p_ref[...]
        copies = []
        for k in range(1, n_dev):
            peer = (me + k) % n_dev
            cp = pltpu.make_async_remote_copy(src_ref=p_ref, dst_ref=slots.at[me], send_sem=send_sems.at[k],
                                              recv_sem=recv_sems.at[k],
                                              device_id=(peer // 4, (peer // 2) % 2, peer % 2), device_id_type=MESH)
            cp.start()
            copies.append(cp)
        for k in range(1, n_dev):
            src = (me + n_dev - k) % n_dev
            pltpu.make_async_remote_copy(src_ref=p_ref, dst_ref=slots.at[src], send_sem=send_sems.at[k],
                                         recv_sem=recv_sems.at[k], device_id=(x, y, c), device_id_type=MESH).wait_recv()
        for cp in copies:
            cp.wait_send()
        total = slots[0]
        for s in range(1, n_dev):
            total = total + slots[s]
        o_ref[...] = total

    return pl.pallas_call(
        body, name="all_reduce_small",
        in_specs=[pl.BlockSpec(memory_space=pltpu.VMEM)], out_specs=pl.BlockSpec(memory_space=pltpu.VMEM),
        out_shape=jax.ShapeDtypeStruct((rows, cols), F32),
        scratch_shapes=[pltpu.VMEM((n_dev, rows, cols), F32), pltpu.SemaphoreType.DMA((n_dev,)),
                        pltpu.SemaphoreType.DMA((n_dev,))],
    )(pack)


def _adamw(name, w, g, m, v):
    r, cols = w.shape
    tr = _row_tile(r)

    def body(w_ref, g_ref, m_ref, v_ref, d_ref, nm_ref, nv_ref):
        gv = g_ref[...]
        nm = ADAM_B1 * m_ref[...] + (1.0 - ADAM_B1) * gv
        nv = ADAM_B2 * v_ref[...] + (1.0 - ADAM_B2) * (gv * gv)
        m_hat = nm / (1.0 - ADAM_B1 ** ADAM_STEP)
        v_hat = nv / (1.0 - ADAM_B2 ** ADAM_STEP)
        d_ref[...] = -ADAM_LR * (m_hat / (jnp.sqrt(v_hat) + ADAM_EPS) + ADAM_WD * w_ref[...])
        nm_ref[...] = nm
        nv_ref[...] = nv

    spec = pl.BlockSpec((tr, cols), lambda i: (i, 0))
    return pl.pallas_call(
        body, name=name, grid=(r // tr,),
        in_specs=[spec] * 4, out_specs=[spec] * 3,
        out_shape=[jax.ShapeDtypeStruct((r, cols), F32)] * 3,
        compiler_params=_params(("parallel",)),
    )(w, g, m, v)


BIG = ["ffn1_w_in", "ffn1_w_out", "w_in", "conv_w_proj", "attn_w_o", "w_out", "ffn2_w_in", "ffn2_w_out", "conv_dw_kernel"]
COL_SHARDED = ("ffn1_w_in", "w_in", "ffn2_w_in")
SMALL = ["ffn1_norm", "mix_norm", "ffn2_norm", "conv_dw_bias", "conv_ln_g", "conv_ln_b", "q_norm", "k_norm", "attn_sinks", "rel_bias"]
WEIGHTS = ["ffn1_norm", "ffn1_w_in", "ffn1_w_out", "mix_norm", "w_in", "conv_dw_kernel", "conv_dw_bias", "conv_ln_g",
           "conv_ln_b", "conv_w_proj", "q_norm", "k_norm", "attn_sinks", "rel_bias", "attn_w_o", "w_out", "ffn2_norm",
           "ffn2_w_in", "ffn2_w_out"]
SMALL_PLACE = {"ffn1_norm": (0, 0, 1024), "mix_norm": (1, 0, 1024), "ffn2_norm": (2, 0, 1024), "conv_dw_bias": (3, 0, 1024),
               "conv_ln_g": (4, 0, 1024), "conv_ln_b": (5, 0, 1024), "q_norm": (6, 0, 64), "k_norm": (6, 128, 64),
               "attn_sinks": (6, 256, 16), "rel_bias": (7, 0, 512)}
LOSS_PLACE = (6, 384)


def _pack_small(vals, fill=0.0, loss=None):
    pack = jnp.full((SMALL_ROWS, D_MODEL), fill, F32)
    for name, (row, lane, n) in SMALL_PLACE.items():
        pack = pack.at[row, lane:lane + n].set(vals[name].reshape(n))
    if loss is not None:
        pack = pack.at[LOSS_PLACE[0], LOSS_PLACE[1]].set(loss)
    return pack


def _unpack_small(pack, shapes):
    return {name: pack[row, lane:lane + n].reshape(shapes[name]) for name, (row, lane, n) in SMALL_PLACE.items()}


def _shard_halves(name, a):
    if name == "conv_dw_kernel":
        a = jnp.pad(a, ((0, CONV_PAD - CONV_WIDTH), (0, 0)))
    r, cols = a.shape
    return a.reshape(2, r // 2, cols)


GATHER_GROUPS = {"A": ["ffn1_w_in", "ffn1_w_out"],
                 "B": ["w_in", "conv_dw_kernel", "conv_w_proj", "attn_w_o", "w_out"],
                 "C": ["ffn2_w_in", "ffn2_w_out"]}


class _MeshComm:
    def __init__(self, wts):
        self.c_idx = lax.axis_index("c").astype(jnp.int32).reshape(1)
        self.chip_idx = (2 * lax.axis_index("x") + lax.axis_index("y")).astype(jnp.int32).reshape(1)
        self.gathers, self.reductions, self.reduced, self.tokens = {}, {}, {}, []
        for group, names in GATHER_GROUPS.items():
            shards = [_shard_halves(n, wts[n]) if n == "conv_dw_kernel" else _shard_halves(n, wts[n]).astype(BF16)
                      for n in names]
            lands = [lax.empty((N_CHIPS,) + s.shape, s.dtype) for s in shards]
            started = _copy_start("gather_start_" + group, shards, lands, _gather_plan, after=self.tokens[-1:])
            self.gathers[group] = started
            self.tokens.append(started[-1])

    def weights(self, group, after):
        send_sems, recv_sems, shards, lands, token = self.gathers.pop(group)
        shards, lands = _copy_wait("gather_wait_" + group, send_sems, recv_sems, shards, lands,
                                   token if after is None else after, _gather_plan)
        out = {}
        for n, g4 in zip(GATHER_GROUPS[group], _gather_forward("gather_forward_" + group, shards, lands)):
            r, cols = g4.shape[2] * 2, g4.shape[3]
            if n in COL_SHARDED:
                out[n] = g4.reshape(N_CHIPS, r, cols)
            elif n == "conv_dw_kernel":
                out[n] = g4.reshape(N_CHIPS, r, cols).transpose(1, 0, 2).reshape(r, N_CHIPS * cols)
            else:
                out[n] = g4.reshape(N_CHIPS * r, cols)
        return out

    def reduce_start(self, group, grads):
        names = list(grads)
        g4 = []
        for n in names:
            a = grads[n]
            if n == "conv_dw_kernel":
                a = a.reshape(CONV_PAD, N_CHIPS, -1).transpose(1, 0, 2)
            elif n not in COL_SHARDED:
                a = a.reshape(N_CHIPS, a.shape[0] // N_CHIPS, a.shape[1])
            g4.append(a.reshape(N_CHIPS, 2, a.shape[1] // 2, a.shape[2]))
        got = _exchange_halves("exchange_halves_" + group, g4)
        sums = [_add_own_half(self.c_idx, a, b) for a, b in zip(g4, got)]
        lands = [lax.empty((N_CHIPS - 1,) + s.shape[1:], s.dtype) for s in sums]
        started = _copy_start("scatter_start_" + group, sums, lands, _scatter_plan)
        self.reductions[group] = (names,) + started
        return (started[-1],)

    def reduce_finish(self, group, after):
        names, send_sems, recv_sems, sums, lands, _ = self.reductions.pop(group)
        sums, lands = _copy_wait("scatter_wait_" + group, send_sems, recv_sems, sums, lands, after, _scatter_plan)
        halves = [_sum_pieces(self.chip_idx, s, p) for s, p in zip(sums, lands)]
        self.reduced.update(zip(names, _join_halves("join_halves_" + group, halves)))


def kernel(x, ffn1_norm, ffn1_w_in, ffn1_w_out, mix_norm, w_in, conv_dw_kernel, conv_dw_bias, conv_ln_g, conv_ln_b, conv_w_proj, q_norm, k_norm, attn_sinks, rel_bias, attn_w_o, w_out, ffn2_norm, ffn2_w_in, ffn2_w_out, loss_target, m_ffn1_norm, m_ffn1_w_in, m_ffn1_w_out, m_mix_norm, m_w_in, m_conv_dw_kernel, m_conv_dw_bias, m_conv_ln_g, m_conv_ln_b, m_conv_w_proj, m_q_norm, m_k_norm, m_attn_sinks, m_rel_bias, m_attn_w_o, m_w_out, m_ffn2_norm, m_ffn2_w_in, m_ffn2_w_out, v_ffn1_norm, v_ffn1_w_in, v_ffn1_w_out, v_mix_norm, v_w_in, v_conv_dw_kernel, v_conv_dw_bias, v_conv_ln_g, v_conv_ln_b, v_conv_w_proj, v_q_norm, v_k_norm, v_attn_sinks, v_rel_bias, v_attn_w_o, v_w_out, v_ffn2_norm, v_ffn2_w_in, v_ffn2_w_out):
    args = dict(locals())
    wts = {n: args[n] for n in WEIGHTS}
    mom = {n: args["m_" + n] for n in WEIGHTS}
    var = {n: args["v_" + n] for n in WEIGHTS}
    comm = _MeshComm(wts)
    small = {n: wts[n] if n in ("attn_sinks", "rel_bias") else wts[n].reshape(1, -1) for n in SMALL}
    loss_part, grad_x, g = _local_step(x[0], loss_target[0], small, comm)

    small_sum = _all_reduce_small(_pack_small(g, loss=loss_part))
    loss = small_sum[LOSS_PLACE[0], LOSS_PLACE[1]]
    small_shapes = {n: wts[n].shape for n in SMALL}
    g_small = _unpack_small(small_sum, small_shapes)

    grads, delta, new_m, new_v = {}, {}, {}, {}
    for n in BIG:
        j = comm.reduced[n]
        gs = j.reshape(j.shape[1] * 2, j.shape[2])
        pad = n == "conv_dw_kernel"
        ws, ms, vs = (_shard_halves(n, a).reshape(gs.shape) for a in (wts[n], mom[n], var[n]))
        d, nm, nv = _adamw("adamw_" + n, ws, gs, ms, vs)
        cut = (lambda a: a[:CONV_WIDTH]) if pad else (lambda a: a)
        grads[n], delta[n], new_m[n], new_v[n] = cut(gs), cut(d), cut(nm), cut(nv)
    d, nm, nv = _adamw("adamw_small", _pack_small(wts), small_sum, _pack_small(mom), _pack_small(var, fill=1.0))
    grads.update(g_small)
    delta.update(_unpack_small(d, small_shapes))
    new_m.update(_unpack_small(nm, small_shapes))
    new_v.update(_unpack_small(nv, small_shapes))

    return (loss, grad_x[None], *[grads[n] for n in WEIGHTS], *[delta[n] for n in WEIGHTS],
            *[new_m[n] for n in WEIGHTS], *[new_v[n] for n in WEIGHTS])
```

```python
import functools
import math

import jax
import jax.numpy as jnp
from jax import lax
from jax.experimental import pallas as pl
from jax.experimental.pallas import tpu as pltpu

F32 = jnp.float32
BF16 = jnp.bfloat16
MESH = pl.DeviceIdType.MESH

EPS = 1e-6
D_MODEL = 1024
D_FF = 2816
N_CHIPS = 4
SHARD_W = 2 * D_FF // N_CHIPS
HEAD_DIM = 64
N_Q_HEADS = 16
N_KV_HEADS = 4
GROUP = N_Q_HEADS // N_KV_HEADS
BLOCK = 128
QROWS = GROUP * BLOCK
N_BUCKETS = 32
MAX_DISTANCE = 128
CONV_WIDTH = 31
CONV_PAD = 32
NEG = float(jnp.finfo(jnp.float32).min)

ADAM_LR = 0.001
ADAM_B1 = 0.9
ADAM_B2 = 0.999
ADAM_EPS = 1e-08
ADAM_WD = 0.01
ADAM_STEP = 10

VMEM_LIMIT_BYTES = 56 * 1024 * 1024
ROW_TILE = 512
CONV_TILE = 256
CONV_ROWS = 32
LANES = 128

COL_CONV_A, COL_CONV_G, COL_Q, COL_K, COL_V, COL_GC, COL_GA = 0, 1024, 2048, 3072, 3328, 3584, 4608
IN_W = 5632


def _params(sem, vmem=VMEM_LIMIT_BYTES):
    return pltpu.CompilerParams(dimension_semantics=sem, vmem_limit_bytes=vmem)


def _sigmoid(x):
    return 1.0 / (1.0 + jnp.exp(-x))


def _dot(a, b, trans_a=False, trans_b=False, precision=None):
    dn = (((0,) if trans_a else (1,), (1,) if trans_b else (0,)), ((), ()))
    return lax.dot_general(a, b, dn, preferred_element_type=F32, precision=precision)


def _mm(name, grid, a, a_spec, b, b_spec, acc_shape, *, trans_a=False, trans_b=False, a_pre=None, b_pre=None,
        extras=(), extra_specs=(), tokens=(), out_shape, out_specs, epilogue, sem=("parallel", "parallel", "arbitrary")):
    n_k = grid[2]
    extras = tuple(extras) + tuple(tokens)
    extra_specs = tuple(extra_specs) + (pl.BlockSpec((8, LANES), lambda i, j, kk: (0, 0)),) * len(tokens)
    n_extra = len(extras)
    n_out = len(out_shape)

    def body(a_ref, b_ref, *rest):
        ex = rest[:n_extra]
        outs = rest[n_extra:n_extra + n_out]
        ids = (pl.program_id(0), pl.program_id(1), pl.program_id(2))
        av = a_ref[...]
        bv = b_ref[...]
        if a_pre is not None:
            av = a_pre(av)
        if b_pre is not None:
            bv = b_pre(bv)
        part = _dot(av, bv, trans_a, trans_b)
        if n_k == 1:
            epilogue(part, ex, outs, ids)
        else:
            acc = rest[-1]

            @pl.when(ids[2] == 0)
            def _():
                acc[...] = part

            @pl.when(ids[2] > 0)
            def _():
                acc[...] += part

            @pl.when(ids[2] == n_k - 1)
            def _():
                epilogue(acc[...], ex, outs, ids)

    scratch = [] if n_k == 1 else [pltpu.VMEM(acc_shape, F32)]
    return pl.pallas_call(
        body, name=name, grid=grid,
        in_specs=[a_spec, b_spec, *extra_specs],
        out_specs=list(out_specs), out_shape=list(out_shape),
        scratch_shapes=scratch, compiler_params=_params(sem),
    )(a, b, *extras)


def _half_bf16(v):
    return (0.5 * v).astype(BF16)


def _to_bf16(v):
    return v.astype(BF16)


def _rmsnorm_fwd(name, x, g, tokens=()):
    t, d = x.shape
    tm = min(ROW_TILE, t)

    def body(x_ref, g_ref, *rest):
        o_ref = rest[-1]
        xv = x_ref[...]
        r = lax.rsqrt(jnp.mean(xv * xv, axis=-1, keepdims=True) + EPS)
        o_ref[...] = (xv * r * g_ref[...]).astype(BF16)

    return pl.pallas_call(
        body, name=name, grid=(t // tm,),
        in_specs=[pl.BlockSpec((tm, d), lambda i: (i, 0)), pl.BlockSpec((1, d), lambda i: (0, 0))]
        + [pl.BlockSpec((8, LANES), lambda i: (0, 0))] * len(tokens),
        out_specs=pl.BlockSpec((tm, d), lambda i: (i, 0)),
        out_shape=jax.ShapeDtypeStruct((t, d), BF16),
        compiler_params=_params(("parallel",)),
    )(x, g, *tokens)


def _rms_bwd_epilogue(acc, ex, outs, ids):
    x_ref, g_ref, dres_ref = ex[:3]
    out_ref, dg_ref = outs
    xv = x_ref[...]
    r = lax.rsqrt(jnp.mean(xv * xv, axis=-1, keepdims=True) + EPS)
    w = acc * g_ref[...]
    dx = r * w - xv * (r * r * r) * jnp.mean(xv * w, axis=-1, keepdims=True)
    out_ref[...] = dres_ref[...] + dx
    part = jnp.sum(acc * (xv * r), axis=0, keepdims=True)

    @pl.when(ids[0] == 0)
    def _():
        dg_ref[...] = part

    @pl.when(ids[0] > 0)
    def _():
        dg_ref[...] += part


def _ffn_in(name, n, w_in4):
    t, d = n.shape
    tm = min(ROW_TILE, t)

    def body(n_ref, wa_ref, wb_ref, ab_ref, h_ref):
        nv = n_ref[...]
        a = _dot(nv, wa_ref[...])
        b = _dot(nv, wb_ref[...])
        h_ref[...] = (a * _sigmoid(a) * b).astype(BF16)
        ab_ref[0] = a.astype(BF16)
        ab_ref[1] = b.astype(BF16)

    return pl.pallas_call(
        body, name=name, grid=(2, t // tm),
        in_specs=[pl.BlockSpec((tm, d), lambda j, i: (i, 0)),
                  pl.BlockSpec((None, d, SHARD_W), lambda j, i: (j, 0, 0)),
                  pl.BlockSpec((None, d, SHARD_W), lambda j, i: (j + 2, 0, 0))],
        out_specs=[pl.BlockSpec((2, tm, SHARD_W), lambda j, i: (0, i, j)),
                   pl.BlockSpec((tm, SHARD_W), lambda j, i: (i, j))],
        out_shape=[jax.ShapeDtypeStruct((2, t, D_FF), BF16), jax.ShapeDtypeStruct((t, D_FF), BF16)],
        compiler_params=_params(("parallel", "parallel")),
    )(n, w_in4, w_in4)


def _mm_residual(name, a, w, res, scale):
    t, k = a.shape
    n = w.shape[1]
    tm = min(ROW_TILE, t)

    def epilogue(acc, ex, outs, ids):
        outs[0][...] = ex[0][...] + scale * acc

    return _mm(name, (t // tm, 1, 1), a, pl.BlockSpec((tm, k), lambda i, j, kk: (i, 0)),
               w, pl.BlockSpec((k, n), lambda i, j, kk: (0, 0)), (tm, n),
               extras=(res,), extra_specs=(pl.BlockSpec((tm, n), lambda i, j, kk: (i, 0)),),
               out_shape=(jax.ShapeDtypeStruct((t, n), F32),),
               out_specs=(pl.BlockSpec((tm, n), lambda i, j, kk: (i, 0)),), epilogue=epilogue)[0]


def _ffn_fwd(tag, x, g, w_in4, w_out, tokens=()):
    n = _rmsnorm_fwd(tag + "_norm", x, g, tokens)
    ab, h = _ffn_in(tag + "_in", n, w_in4)
    y = _mm_residual(tag + "_out", h, w_out, x, 0.5)
    return y, (n, ab, h)


def _ffn_bwd(tag, dres, x, g, saved, w_in4, w_out, tokens=(), on_weight_grads=None):
    n, ab, h = saved
    t, d = x.shape
    tm = min(ROW_TILE, t)
    tk = min(ROW_TILE, t)
    half_w = SHARD_W

    def dact_epilogue(acc, ex, outs, ids):
        a = ex[0][0].astype(F32)
        b = ex[0][1].astype(F32)
        sig = _sigmoid(a)
        outs[0][0] = (acc * b * (sig * (1.0 + a * (1.0 - sig)))).astype(BF16)
        outs[0][1] = (acc * (a * sig)).astype(BF16)

    du = _mm(tag + "_dact", (2, t // tm, 1),
             dres, pl.BlockSpec((tm, d), lambda j, i, kk: (i, 0)),
             w_out, pl.BlockSpec((half_w, d), lambda j, i, kk: (j, 0)), (tm, half_w),
             trans_b=True, a_pre=_half_bf16,
             extras=(ab,), extra_specs=(pl.BlockSpec((2, tm, half_w), lambda j, i, kk: (0, i, j)),), tokens=tokens,
             out_shape=(jax.ShapeDtypeStruct((2, t, D_FF), BF16),),
             out_specs=(pl.BlockSpec((2, tm, half_w), lambda j, i, kk: (0, i, j)),),
             epilogue=dact_epilogue)[0]

    def store_epilogue(acc, ex, outs, ids):
        outs[0][...] = acc

    dw_out = _mm(tag + "_dwout", (2, 1, t // tk),
                 h, pl.BlockSpec((tk, half_w), lambda i, j, kk: (kk, i)),
                 dres, pl.BlockSpec((tk, d), lambda i, j, kk: (kk, 0)), (half_w, d),
                 trans_a=True, b_pre=_half_bf16,
                 out_shape=(jax.ShapeDtypeStruct((D_FF, d), F32),),
                 out_specs=(pl.BlockSpec((half_w, d), lambda i, j, kk: (i, 0)),),
                 epilogue=store_epilogue)[0]

    dw_in4 = _mm(tag + "_dwin", (1, N_CHIPS, t // tk),
                 n, pl.BlockSpec((tk, d), lambda i, j, kk: (kk, 0)),
                 du, pl.BlockSpec((None, tk, SHARD_W), lambda i, j, kk: (j // 2, kk, j % 2)), (d, SHARD_W),
                 trans_a=True,
                 out_shape=(jax.ShapeDtypeStruct((N_CHIPS, d, SHARD_W), F32),),
                 out_specs=(pl.BlockSpec((None, d, SHARD_W), lambda i, j, kk: (j, 0, 0)),),
                 epilogue=store_epilogue)[0]

    late = () if on_weight_grads is None else on_weight_grads(dw_in4, dw_out)

    dx, dg = _mm(tag + "_dn", (t // tm, 1, N_CHIPS),
                 du, pl.BlockSpec((None, tm, SHARD_W), lambda i, j, kk: (kk // 2, i, kk % 2)),
                 w_in4, pl.BlockSpec((None, d, SHARD_W), lambda i, j, kk: (kk, 0, 0)), (tm, d),
                 trans_b=True,
                 extras=(x, g, dres),
                 extra_specs=(pl.BlockSpec((tm, d), lambda i, j, kk: (i, 0)),
                              pl.BlockSpec((1, d), lambda i, j, kk: (0, 0)),
                              pl.BlockSpec((tm, d), lambda i, j, kk: (i, 0))), tokens=late,
                 out_shape=(jax.ShapeDtypeStruct((t, d), F32), jax.ShapeDtypeStruct((1, d), F32)),
                 out_specs=(pl.BlockSpec((tm, d), lambda i, j, kk: (i, 0)),
                            pl.BlockSpec((1, d), lambda i, j, kk: (0, 0))),
                 epilogue=_rms_bwd_epilogue, sem=("arbitrary", "arbitrary", "arbitrary"))
    return dx, dw_in4, dw_out, dg


def _loss_head(y, target):
    t, d = y.shape
    tm = min(ROW_TILE, t)

    def body(y_ref, t_ref, dy_ref, loss_ref):
        diff = y_ref[...] - t_ref[...]
        dy_ref[...] = diff * (1.0 / d)
        part = jnp.full((8, LANES), 0.5 / d * jnp.sum(diff * diff), F32)
        i = pl.program_id(0)

        @pl.when(i == 0)
        def _():
            loss_ref[...] = part

        @pl.when(i > 0)
        def _():
            loss_ref[...] += part

    return pl.pallas_call(
        body, name="loss_head", grid=(t // tm,),
        in_specs=[pl.BlockSpec((tm, d), lambda i: (i, 0)), pl.BlockSpec((tm, d), lambda i: (i, 0))],
        out_specs=[pl.BlockSpec((tm, d), lambda i: (i, 0)), pl.BlockSpec((8, LANES), lambda i: (0, 0))],
        out_shape=[jax.ShapeDtypeStruct((t, d), F32), jax.ShapeDtypeStruct((8, LANES), F32)],
        compiler_params=_params(("arbitrary",)),
    )(y, target)


def _conv_fill(zp_ref, a_ref, g_ref, ah_ref, gh_ref, i):
    zh = ah_ref[...].astype(F32) * _sigmoid(gh_ref[...].astype(F32))
    zp_ref[pl.ds(0, CONV_PAD), :] = jnp.where(i > 0, zh, 0.0)
    zp_ref[pl.ds(CONV_PAD, a_ref.shape[0]), :] = a_ref[...].astype(F32) * _sigmoid(g_ref[...].astype(F32))


def _conv_taps(zp_ref, z1_ref, dw_ref, bias_ref, tm, ch):
    first = CONV_PAD - (CONV_WIDTH - 1)
    for cc in range(ch // LANES):
        lanes = pl.ds(cc * LANES, LANES)
        w = [dw_ref[pl.ds(j, 1), lanes] for j in range(CONV_WIDTH)]
        bias = bias_ref[:, lanes]
        for rb in range(tm // CONV_ROWS):
            acc = jnp.broadcast_to(bias, (CONV_ROWS, LANES))
            for j in range(CONV_WIDTH):
                acc = acc + w[j] * zp_ref[pl.ds(rb * CONV_ROWS + first + j, CONV_ROWS), lanes]
            z1_ref[pl.ds(rb * CONV_ROWS, CONV_ROWS), lanes] = acc


def _conv_specs(tm, ch):
    per = tm // CONV_PAD
    cb = COL_CONV_G // ch
    return [pl.BlockSpec((tm, ch), lambda i: (i, 0)),
            pl.BlockSpec((tm, ch), lambda i: (i, cb)),
            pl.BlockSpec((CONV_PAD, ch), lambda i: (jnp.maximum(i * per - 1, 0), 0)),
            pl.BlockSpec((CONV_PAD, ch), lambda i: (jnp.maximum(i * per - 1, 0), cb))]


def _conv_fwd(p, dw, bias, ln_g, ln_b):
    t = p.shape[0]
    ch = D_MODEL
    tm = min(CONV_TILE, t)

    def body(a_ref, g_ref, ah_ref, gh_ref, dw_ref, bias_ref, lg_ref, lb_ref, o_ref, zp_ref, z1_ref):
        i = pl.program_id(0)
        _conv_fill(zp_ref, a_ref, g_ref, ah_ref, gh_ref, i)
        _conv_taps(zp_ref, z1_ref, dw_ref, bias_ref, tm, ch)
        z1 = z1_ref[...]
        mu = jnp.mean(z1, axis=-1, keepdims=True)
        zc = z1 - mu
        rs = lax.rsqrt(jnp.mean(zc * zc, axis=-1, keepdims=True) + EPS)
        z2 = zc * rs * lg_ref[...] + lb_ref[...]
        o_ref[...] = (z2 * _sigmoid(z2)).astype(BF16)

    vec = pl.BlockSpec((1, ch), lambda i: (0, 0))
    return pl.pallas_call(
        body, name="conv_fwd", grid=(t // tm,),
        in_specs=_conv_specs(tm, ch) + [pl.BlockSpec((CONV_PAD, ch), lambda i: (0, 0)), vec, vec, vec],
        out_specs=pl.BlockSpec((tm, ch), lambda i: (i, 0)),
        out_shape=jax.ShapeDtypeStruct((t, ch), BF16),
        scratch_shapes=[pltpu.VMEM((CONV_PAD + tm, ch), F32), pltpu.VMEM((tm, ch), F32)],
        compiler_params=_params(("parallel",)),
    )(p, p, p, p, dw, bias, ln_g, ln_b)


def _conv_bwd_ln(p, dz3, dw, bias, ln_g, ln_b):
    t = p.shape[0]
    ch = D_MODEL
    tm = min(CONV_TILE, t)
    first = CONV_PAD - (CONV_WIDTH - 1)

    def body(a_ref, g_ref, ah_ref, gh_ref, dz3_ref, dw_ref, bias_ref, lg_ref, lb_ref,
             dz1_ref, ddw_ref, dbias_ref, dlg_ref, dlb_ref, zp_ref, z1_ref):
        i = pl.program_id(0)
        _conv_fill(zp_ref, a_ref, g_ref, ah_ref, gh_ref, i)
        _conv_taps(zp_ref, z1_ref, dw_ref, bias_ref, tm, ch)
        z1 = z1_ref[...]
        mu = jnp.mean(z1, axis=-1, keepdims=True)
        zc = z1 - mu
        rs = lax.rsqrt(jnp.mean(zc * zc, axis=-1, keepdims=True) + EPS)
        xh = zc * rs
        z2 = xh * lg_ref[...] + lb_ref[...]
        sig = _sigmoid(z2)
        dz2 = dz3_ref[...].astype(F32) * (sig * (1.0 + z2 * (1.0 - sig)))
        dxh = dz2 * lg_ref[...]
        dz1 = rs * (dxh - jnp.mean(dxh, axis=-1, keepdims=True) - xh * jnp.mean(dxh * xh, axis=-1, keepdims=True))
        dz1_ref[...] = dz1

        @pl.when(i == 0)
        def _():
            ddw_ref[...] = jnp.zeros_like(ddw_ref)
            dbias_ref[...] = jnp.zeros_like(dbias_ref)
            dlg_ref[...] = jnp.zeros_like(dlg_ref)
            dlb_ref[...] = jnp.zeros_like(dlb_ref)

        dlg_ref[...] += jnp.sum(dz2 * xh, axis=0, keepdims=True)
        dlb_ref[...] += jnp.sum(dz2, axis=0, keepdims=True)
        dbias_ref[...] += jnp.sum(dz1, axis=0, keepdims=True)
        for cc in range(ch // LANES):
            lanes = pl.ds(cc * LANES, LANES)
            accs = [jnp.zeros((8, LANES), F32) for _ in range(CONV_WIDTH)]
            for rb in range(tm // CONV_ROWS):
                dzc = dz1_ref[pl.ds(rb * CONV_ROWS, CONV_ROWS), lanes]
                for j in range(CONV_WIDTH):
                    prod = dzc * zp_ref[pl.ds(rb * CONV_ROWS + first + j, CONV_ROWS), lanes]
                    accs[j] = accs[j] + jnp.sum(prod.reshape(CONV_ROWS // 8, 8, LANES), axis=0)
            for j in range(CONV_WIDTH):
                ddw_ref[pl.ds(j, 1), lanes] += jnp.sum(accs[j], axis=0, keepdims=True)

    vec = pl.BlockSpec((1, ch), lambda i: (0, 0))
    return pl.pallas_call(
        body, name="conv_bwd_ln", grid=(t // tm,),
        in_specs=_conv_specs(tm, ch) + [pl.BlockSpec((tm, ch), lambda i: (i, 0)),
                                        pl.BlockSpec((CONV_PAD, ch), lambda i: (0, 0)), vec, vec, vec],
        out_specs=[pl.BlockSpec((tm, ch), lambda i: (i, 0)), pl.BlockSpec((CONV_PAD, ch), lambda i: (0, 0)), vec, vec, vec],
        out_shape=[jax.ShapeDtypeStruct((t, ch), F32), jax.ShapeDtypeStruct((CONV_PAD, ch), F32)]
        + [jax.ShapeDtypeStruct((1, ch), F32)] * 3,
        scratch_shapes=[pltpu.VMEM((CONV_PAD + tm, ch), F32), pltpu.VMEM((tm, ch), F32)],
        compiler_params=_params(("arbitrary",)),
    )(p, p, p, p, dz3, dw, bias, ln_g, ln_b)


def _conv_bwd_glu(p, dz1, dw):
    t = p.shape[0]
    ch = D_MODEL
    tm = min(CONV_TILE, t)
    per = tm // CONV_PAD
    n_halo = t // CONV_PAD
    cb = COL_CONV_G // ch

    def body(a_ref, g_ref, dz_ref, dzn_ref, dw_ref, o_ref, zp_ref, z0_ref):
        i = pl.program_id(0)
        zp_ref[pl.ds(0, tm), :] = dz_ref[...]
        zp_ref[pl.ds(tm, CONV_PAD), :] = jnp.where(i < t // tm - 1, dzn_ref[...], 0.0)
        for cc in range(ch // LANES):
            lanes = pl.ds(cc * LANES, LANES)
            w = [dw_ref[pl.ds(j, 1), lanes] for j in range(CONV_WIDTH)]
            for rb in range(tm // CONV_ROWS):
                acc = jnp.zeros((CONV_ROWS, LANES), F32)
                for j in range(CONV_WIDTH):
                    acc = acc + w[j] * zp_ref[pl.ds(rb * CONV_ROWS + (CONV_WIDTH - 1 - j), CONV_ROWS), lanes]
                z0_ref[pl.ds(rb * CONV_ROWS, CONV_ROWS), lanes] = acc
        dz0 = z0_ref[...]
        a = a_ref[...].astype(F32)
        sig = _sigmoid(g_ref[...].astype(F32))
        o_ref[:, pl.ds(0, ch)] = (dz0 * sig).astype(BF16)
        o_ref[:, pl.ds(ch, ch)] = (dz0 * a * sig * (1.0 - sig)).astype(BF16)

    return pl.pallas_call(
        body, name="conv_bwd_glu", grid=(t // tm,),
        in_specs=[pl.BlockSpec((tm, ch), lambda i: (i, 0)), pl.BlockSpec((tm, ch), lambda i: (i, cb)),
                  pl.BlockSpec((tm, ch), lambda i: (i, 0)),
                  pl.BlockSpec((CONV_PAD, ch), lambda i: (jnp.minimum((i + 1) * per, n_halo - 1), 0)),
                  pl.BlockSpec((CONV_PAD, ch), lambda i: (0, 0))],
        out_specs=pl.BlockSpec((tm, 2 * ch), lambda i: (i, 0)),
        out_shape=jax.ShapeDtypeStruct((t, 2 * ch), BF16),
        scratch_shapes=[pltpu.VMEM((tm + CONV_PAD, ch), F32), pltpu.VMEM((tm, ch), F32)],
        compiler_params=_params(("parallel",)),
    )(p, p, dz1, dz1, dw)


def _bucket_onehot():
    qi = jnp.arange(BLOCK, dtype=jnp.int32)[:, None]
    kj = jnp.arange(2 * BLOCK, dtype=jnp.int32)[None, :]
    dist = jnp.maximum(qi + BLOCK - kj, 0)
    max_exact = N_BUCKETS // 2
    dflt = jnp.maximum(dist, 1).astype(F32)
    large = max_exact + (jnp.log(dflt / max_exact) / math.log(MAX_DISTANCE / max_exact)
                         * (N_BUCKETS - max_exact)).astype(jnp.int32)
    large = jnp.minimum(large, N_BUCKETS - 1)
    bucket = jnp.where(dist < max_exact, dist, large)
    onehot = bucket[None] == jnp.arange(N_BUCKETS, dtype=jnp.int32)[:, None, None]
    return onehot.astype(F32).reshape(N_BUCKETS, BLOCK * 2 * BLOCK)


def _bias_table(rel_bias_t, onehot):
    def body(r_ref, oh_ref, o_ref):
        o_ref[...] = _dot(r_ref[...], oh_ref[...], precision=lax.Precision.HIGHEST)

    n = onehot.shape[1]
    tn = 4096
    return pl.pallas_call(
        body, name="bias_table", grid=(n // tn,),
        in_specs=[pl.BlockSpec((N_Q_HEADS, N_BUCKETS), lambda i: (0, 0)), pl.BlockSpec((N_BUCKETS, tn), lambda i: (0, i))],
        out_specs=pl.BlockSpec((N_Q_HEADS, tn), lambda i: (0, i)),
        out_shape=jax.ShapeDtypeStruct((N_Q_HEADS, n), F32),
        compiler_params=_params(("parallel",)),
    )(rel_bias_t, onehot)


def _bias_table_bwd(dbias, onehot):
    n = onehot.shape[1]
    tn = 4096

    def body(d_ref, oh_ref, o_ref):
        part = _dot(d_ref[...], oh_ref[...], trans_b=True, precision=lax.Precision.HIGHEST)
        i = pl.program_id(0)

        @pl.when(i == 0)
        def _():
            o_ref[...] = part

        @pl.when(i > 0)
        def _():
            o_ref[...] += part

    return pl.pallas_call(
        body, name="bias_table_bwd", grid=(n // tn,),
        in_specs=[pl.BlockSpec((N_Q_HEADS, tn), lambda i: (0, i)), pl.BlockSpec((N_BUCKETS, tn), lambda i: (0, i))],
        out_specs=pl.BlockSpec((N_Q_HEADS, N_BUCKETS), lambda i: (0, 0)),
        out_shape=jax.ShapeDtypeStruct((N_Q_HEADS, N_BUCKETS), F32),
        compiler_params=_params(("arbitrary",)),
    )(dbias, onehot)


def _attn_probs(q_ref, kp_ref, kc_ref, gq_ref, gk_ref, sink_ref, bias_ref, n):
    qf = q_ref[...].astype(F32)
    rq = lax.rsqrt(jnp.mean(qf * qf, axis=-1, keepdims=True) + EPS)
    qn = qf * rq * gq_ref[...]
    kf = jnp.concatenate([kp_ref[...], kc_ref[...]], axis=0).astype(F32)
    rk = lax.rsqrt(jnp.mean(kf * kf, axis=-1, keepdims=True) + EPS)
    kn = kf * rk * gk_ref[...]
    s = _dot(qn.astype(BF16), kn.astype(BF16), trans_b=True) * (1.0 / math.sqrt(HEAD_DIM)) + bias_ref[...]
    row = lax.broadcasted_iota(jnp.int32, (QROWS, 2 * BLOCK), 0) & (BLOCK - 1)
    col = lax.broadcasted_iota(jnp.int32, (QROWS, 2 * BLOCK), 1)
    dist = row + BLOCK - col
    valid = (dist >= 0) & (dist < BLOCK) & ((col >= BLOCK) | (n > 0))
    s = jnp.where(valid, s, NEG)
    sink = sink_ref[...]
    m = jnp.maximum(jnp.max(s, axis=-1, keepdims=True), sink)
    p = jnp.exp(s - m)
    es = jnp.exp(sink - m)
    inv = 1.0 / (jnp.sum(p, axis=-1, keepdims=True) + es)
    return qf, rq, qn, kf, rk, kn, p * inv, es * inv


def _attn_specs(nb):
    qspec = pl.BlockSpec((None, None, QROWS, HEAD_DIM), lambda h, n: (h, n, 0, 0))
    kprev = pl.BlockSpec((None, BLOCK, HEAD_DIM), lambda h, n: (h, jnp.maximum(n - 1, 0), 0))
    kcur = pl.BlockSpec((None, BLOCK, HEAD_DIM), lambda h, n: (h, n, 0))
    gain = pl.BlockSpec((1, HEAD_DIM), lambda h, n: (0, 0))
    sink = pl.BlockSpec((None, QROWS, 1), lambda h, n: (h, 0, 0))
    bias = pl.BlockSpec((None, QROWS, 2 * BLOCK), lambda h, n: (h, 0, 0))
    return qspec, kprev, kcur, gain, sink, bias


def _attn_fwd(q4, k3, v3, gq, gk, sink_rows, bias):
    nb = q4.shape[1]
    qspec, kprev, kcur, gain, sink, bspec = _attn_specs(nb)

    def body(q_ref, kp_ref, kc_ref, vp_ref, vc_ref, gq_ref, gk_ref, sink_ref, bias_ref, o_ref):
        n = pl.program_id(1)
        pn = _attn_probs(q_ref, kp_ref, kc_ref, gq_ref, gk_ref, sink_ref, bias_ref, n)[6]
        v = jnp.concatenate([vp_ref[...], vc_ref[...]], axis=0)
        o_ref[...] = _dot(pn.astype(BF16), v).astype(BF16)

    return pl.pallas_call(
        body, name="attn_fwd", grid=(N_KV_HEADS, nb),
        in_specs=[qspec, kprev, kcur, kprev, kcur, gain, gain, sink, bspec],
        out_specs=qspec, out_shape=jax.ShapeDtypeStruct(q4.shape, BF16),
        compiler_params=_params(("parallel", "parallel")),
    )(q4, k3, k3, v3, v3, gq, gk, sink_rows, bias)


def _attn_bwd(q4, k3, v3, do4, gq, gk, sink_rows, bias):
    nb = q4.shape[1]
    t = k3.shape[1]
    qspec, kprev, kcur, gain, sink, bspec = _attn_specs(nb)
    scale = 1.0 / math.sqrt(HEAD_DIM)

    def rms_bwd(dn, xf, r, g):
        w = dn * g
        dx = r * w - xf * (r * r * r) * jnp.mean(xf * w, axis=-1, keepdims=True)
        return dx, jnp.sum(dn * (xf * r), axis=0, keepdims=True)

    def body(q_ref, kp_ref, kc_ref, vp_ref, vc_ref, do_ref, gq_ref, gk_ref, sink_ref, bias_ref,
             dq_ref, dk_ref, dv_ref, dbias_ref, dsink_ref, dgq_ref, dgk_ref):
        n = pl.program_id(1)
        qf, rq, qn, kf, rk, kn, pn, psink = _attn_probs(q_ref, kp_ref, kc_ref, gq_ref, gk_ref, sink_ref, bias_ref, n)
        do = do_ref[...]
        v = jnp.concatenate([vp_ref[...], vc_ref[...]], axis=0)
        dv_win = _dot(pn.astype(BF16), do, trans_a=True)
        dp = _dot(do, v, trans_b=True)
        delta = jnp.sum(pn * dp, axis=-1, keepdims=True)
        ds = pn * (dp - delta)
        dsink = jnp.sum((-psink * delta).reshape(GROUP, BLOCK, 1), axis=1)
        dsc = (ds * scale).astype(BF16)
        dqn = _dot(dsc, kn.astype(BF16))
        dkn = _dot(dsc, qn.astype(BF16), trans_a=True)
        dq, dgq = rms_bwd(dqn, qf, rq, gq_ref[...])
        dk_win, dgk = rms_bwd(dkn, kf, rk, gk_ref[...])
        dq_ref[...] = dq.astype(BF16)

        @pl.when(n == 0)
        def _():
            dbias_ref[...] = ds
            dsink_ref[...] = dsink
            dgq_ref[...] = dgq
            dgk_ref[...] = dgk

        @pl.when(n > 0)
        def _():
            dbias_ref[...] += ds
            dsink_ref[...] += dsink
            dgq_ref[...] += dgq
            dgk_ref[...] += dgk
            prev = pl.ds(pl.multiple_of((n - 1) * BLOCK, BLOCK), BLOCK)
            dk_ref[prev, :] += dk_win[:BLOCK]
            dv_ref[prev, :] += dv_win[:BLOCK]

        cur = pl.ds(pl.multiple_of(n * BLOCK, BLOCK), BLOCK)
        dk_ref[cur, :] = dk_win[BLOCK:]
        dv_ref[cur, :] = dv_win[BLOCK:]

    kv_out = pl.BlockSpec((None, t, HEAD_DIM), lambda h, n: (h, 0, 0))
    gain_out = pl.BlockSpec((None, 1, HEAD_DIM), lambda h, n: (h, 0, 0))
    return pl.pallas_call(
        body, name="attn_bwd", grid=(N_KV_HEADS, nb),
        in_specs=[qspec, kprev, kcur, kprev, kcur, qspec, gain, gain, sink, bspec],
        out_specs=[qspec, kv_out, kv_out, bspec,
                   pl.BlockSpec((None, GROUP, 1), lambda h, n: (h, 0, 0)), gain_out, gain_out],
        out_shape=[jax.ShapeDtypeStruct(q4.shape, BF16),
                   jax.ShapeDtypeStruct((N_KV_HEADS, t, HEAD_DIM), F32),
                   jax.ShapeDtypeStruct((N_KV_HEADS, t, HEAD_DIM), F32),
                   jax.ShapeDtypeStruct((N_KV_HEADS, QROWS, 2 * BLOCK), F32),
                   jax.ShapeDtypeStruct((N_KV_HEADS, GROUP, 1), F32),
                   jax.ShapeDtypeStruct((N_KV_HEADS, 1, HEAD_DIM), F32),
                   jax.ShapeDtypeStruct((N_KV_HEADS, 1, HEAD_DIM), F32)],
        compiler_params=_params(("arbitrary", "arbitrary")),
    )(q4, k3, k3, v3, v3, do4, gq, gk, sink_rows, bias)


def _q_to_heads(cols, t):
    nb = t // BLOCK
    return cols.reshape(nb, BLOCK, N_KV_HEADS, GROUP, HEAD_DIM).transpose(2, 0, 3, 1, 4).reshape(N_KV_HEADS, nb, QROWS, HEAD_DIM)


def _q_from_heads(q4, t):
    nb = t // BLOCK
    return q4.reshape(N_KV_HEADS, nb, GROUP, BLOCK, HEAD_DIM).transpose(1, 3, 0, 2, 4).reshape(t, N_Q_HEADS * HEAD_DIM)


def _kv_to_heads(cols, t):
    return cols.reshape(t, N_KV_HEADS, HEAD_DIM).transpose(1, 0, 2)


def _kv_from_heads(k3, t):
    return k3.transpose(1, 0, 2).reshape(t, N_KV_HEADS * HEAD_DIM)


GATE_TILE = 512


def _merge_fwd(z3, o, p, w_proj, w_o):
    t, d = z3.shape
    tm = min(ROW_TILE, t)
    tn = GATE_TILE

    def body(z_ref, o_ref, gc_ref, ga_ref, wp_ref, wo_ref, m_ref, a_ref, b_ref):
        a = _dot(z_ref[...], wp_ref[...])
        b = _dot(o_ref[...], wo_ref[...])
        m_ref[...] = (_sigmoid(gc_ref[...].astype(F32)) * a + _sigmoid(ga_ref[...].astype(F32)) * b).astype(BF16)
        a_ref[...] = a.astype(BF16)
        b_ref[...] = b.astype(BF16)

    row = pl.BlockSpec((tm, d), lambda i, j: (i, 0))
    wspec = pl.BlockSpec((d, tn), lambda i, j: (0, j))
    ospec = pl.BlockSpec((tm, tn), lambda i, j: (i, j))
    return pl.pallas_call(
        body, name="merge_fwd", grid=(t // tm, d // tn),
        in_specs=[row, row,
                  pl.BlockSpec((tm, tn), lambda i, j: (i, COL_GC // tn + j)),
                  pl.BlockSpec((tm, tn), lambda i, j: (i, COL_GA // tn + j)), wspec, wspec],
        out_specs=[ospec, ospec, ospec],
        out_shape=[jax.ShapeDtypeStruct((t, d), BF16)] * 3,
        compiler_params=_params(("parallel", "parallel")),
    )(z3, o, p, p, w_proj, w_o)


def _merge_bwd(dres, w_out, a, b, p, tokens=()):
    t, d = dres.shape
    tm = min(ROW_TILE, t)
    tn = GATE_TILE

    def epilogue(acc, ex, outs, ids):
        a_ref, b_ref, gc_ref, ga_ref = ex[:4]
        sc = _sigmoid(gc_ref[...].astype(F32))
        sa = _sigmoid(ga_ref[...].astype(F32))
        outs[0][...] = (acc * sc).astype(BF16)
        outs[1][...] = (acc * sa).astype(BF16)
        outs[2][0] = (acc * a_ref[...].astype(F32) * sc * (1.0 - sc)).astype(BF16)
        outs[2][1] = (acc * b_ref[...].astype(F32) * sa * (1.0 - sa)).astype(BF16)

    ospec = pl.BlockSpec((tm, tn), lambda i, j, kk: (i, j))
    return _mm("merge_bwd", (t // tm, d // tn, 1),
               dres, pl.BlockSpec((tm, d), lambda i, j, kk: (i, 0)),
               w_out, pl.BlockSpec((tn, d), lambda i, j, kk: (j, 0)), (tm, tn),
               trans_b=True, a_pre=_to_bf16,
               extras=(a, b, p, p),
               extra_specs=(ospec, ospec,
                            pl.BlockSpec((tm, tn), lambda i, j, kk: (i, COL_GC // tn + j)),
                            pl.BlockSpec((tm, tn), lambda i, j, kk: (i, COL_GA // tn + j))), tokens=tokens,
               out_shape=(jax.ShapeDtypeStruct((t, d), BF16), jax.ShapeDtypeStruct((t, d), BF16),
                          jax.ShapeDtypeStruct((2, t, d), BF16)),
               out_specs=(ospec, ospec, pl.BlockSpec((2, tm, tn), lambda i, j, kk: (0, i, j))),
               epilogue=epilogue)


def _store_epilogue(acc, ex, outs, ids):
    outs[0][...] = acc


def _store_bf16_epilogue(acc, ex, outs, ids):
    outs[0][...] = acc.astype(BF16)


def _mm_nt(name, a, w, out_dtype=BF16):
    t, n = a.shape
    k = w.shape[0]
    tm = min(ROW_TILE, t)
    return _mm(name, (t // tm, 1, 1), a, pl.BlockSpec((tm, n), lambda i, j, kk: (i, 0)),
               w, pl.BlockSpec((k, n), lambda i, j, kk: (0, 0)), (tm, k), trans_b=True,
               out_shape=(jax.ShapeDtypeStruct((t, k), out_dtype),),
               out_specs=(pl.BlockSpec((tm, k), lambda i, j, kk: (i, 0)),),
               epilogue=_store_bf16_epilogue if out_dtype == BF16 else _store_epilogue)[0]


def _mm_tn(name, a, b, b_pre=None):
    t, m = a.shape
    n = b.shape[1]
    tk = min(ROW_TILE, t)
    return _mm(name, (1, 1, t // tk), a, pl.BlockSpec((tk, m), lambda i, j, kk: (kk, 0)),
               b, pl.BlockSpec((tk, n), lambda i, j, kk: (kk, 0)), (m, n), trans_a=True, b_pre=b_pre,
               out_shape=(jax.ShapeDtypeStruct((m, n), F32),),
               out_specs=(pl.BlockSpec((m, n), lambda i, j, kk: (0, 0)),), epilogue=_store_epilogue)[0]


def _local_step(x, target, small, comm):
    t = x.shape[0]
    w = dict(small)

    w.update(comm.weights("A", None))
    x1, ffn1_saved = _ffn_fwd("ffn1", x, w["ffn1_norm"], w["ffn1_w_in"], w["ffn1_w_out"], comm.tokens)
    w.update(comm.weights("B", x1))
    hm = _rmsnorm_fwd("mix_norm", x1, w["mix_norm"], comm.tokens)
    tm = min(ROW_TILE, t)
    p = _mm("mix_in", (N_CHIPS, t // tm, 1),
            hm, pl.BlockSpec((tm, D_MODEL), lambda j, i, kk: (i, 0)),
            w["w_in"], pl.BlockSpec((None, D_MODEL, SHARD_W), lambda j, i, kk: (j, 0, 0)), (tm, SHARD_W),
            out_shape=(jax.ShapeDtypeStruct((t, IN_W), BF16),),
            out_specs=(pl.BlockSpec((tm, SHARD_W), lambda j, i, kk: (i, j)),),
            epilogue=_store_bf16_epilogue)[0]

    z3 = _conv_fwd(p, w["conv_dw_kernel"], w["conv_dw_bias"], w["conv_ln_g"], w["conv_ln_b"])

    onehot = _bucket_onehot()
    bias = _bias_table(w["rel_bias"].T, onehot).reshape(N_KV_HEADS, QROWS, 2 * BLOCK)
    sink_rows = jnp.repeat(w["attn_sinks"].reshape(N_KV_HEADS, GROUP), BLOCK, axis=1)[..., None]
    q4 = _q_to_heads(p[:, COL_Q:COL_K], t)
    k3 = _kv_to_heads(p[:, COL_K:COL_V], t)
    v3 = _kv_to_heads(p[:, COL_V:COL_GC], t)
    o4 = _attn_fwd(q4, k3, v3, w["q_norm"], w["k_norm"], sink_rows, bias)
    o = _q_from_heads(o4, t)

    merged, a, b = _merge_fwd(z3, o, p, w["conv_w_proj"], w["attn_w_o"])
    x2 = _mm_residual("mix_out", merged, w["w_out"], x1, 1.0)
    w.update(comm.weights("C", x2))
    x3, ffn2_saved = _ffn_fwd("ffn2", x2, w["ffn2_norm"], w["ffn2_w_in"], w["ffn2_w_out"])
    dy, loss = _loss_head(x3, target)

    g, big = {}, {}
    dres2, big["ffn2_w_in"], big["ffn2_w_out"], g["ffn2_norm"] = _ffn_bwd(
        "ffn2b", dy, x2, w["ffn2_norm"], ffn2_saved, w["ffn2_w_in"], w["ffn2_w_out"])
    tokens = comm.reduce_start("R1", big)

    da, db, dgates = _merge_bwd(dres2, w["w_out"], a, b, p, tokens)
    big = {}
    big["w_out"] = _mm_tn("d_w_out", merged, dres2, b_pre=_to_bf16)
    big["conv_w_proj"] = _mm_tn("d_w_proj", z3, da)
    big["attn_w_o"] = _mm_tn("d_w_o", o, db)
    dz3 = _mm_nt("d_z3", da, w["conv_w_proj"])
    do = _mm_nt("d_o", db, w["attn_w_o"])

    dq4, dk3, dv3, dbias, dsink, dgq, dgk = _attn_bwd(q4, k3, v3, _q_to_heads(do, t), w["q_norm"], w["k_norm"], sink_rows, bias)
    g["rel_bias"] = _bias_table_bwd(dbias.reshape(N_Q_HEADS, BLOCK * 2 * BLOCK), onehot).T
    g["attn_sinks"] = dsink.reshape(N_Q_HEADS)
    g["q_norm"] = jnp.sum(dgq, axis=0)
    g["k_norm"] = jnp.sum(dgk, axis=0)

    dz1, big["conv_dw_kernel"], g["conv_dw_bias"], g["conv_ln_g"], g["conv_ln_b"] = _conv_bwd_ln(
        p, dz3, w["conv_dw_kernel"], w["conv_dw_bias"], w["conv_ln_g"], w["conv_ln_b"])
    dconv = _conv_bwd_glu(p, dz1, w["conv_dw_kernel"])

    dp = jnp.concatenate([dconv, _q_from_heads(dq4, t), _kv_from_heads(dk3, t).astype(BF16),
                          _kv_from_heads(dv3, t).astype(BF16), dgates[0], dgates[1]], axis=1)
    big["w_in"] = _mm("d_w_in", (1, N_CHIPS, t // tm),
                    hm, pl.BlockSpec((tm, D_MODEL), lambda i, j, kk: (kk, 0)),
                    dp, pl.BlockSpec((tm, SHARD_W), lambda i, j, kk: (kk, j)), (D_MODEL, SHARD_W),
                    trans_a=True,
                    out_shape=(jax.ShapeDtypeStruct((N_CHIPS, D_MODEL, SHARD_W), F32),),
                    out_specs=(pl.BlockSpec((None, D_MODEL, SHARD_W), lambda i, j, kk: (j, 0, 0)),),
                    epilogue=_store_epilogue)[0]
    dres1, g["mix_norm"] = _mm("d_mix", (t // tm, 1, N_CHIPS),
                               dp, pl.BlockSpec((tm, SHARD_W), lambda i, j, kk: (i, kk)),
                               w["w_in"], pl.BlockSpec((None, D_MODEL, SHARD_W), lambda i, j, kk: (kk, 0, 0)),
                               (tm, D_MODEL), trans_b=True,
                               extras=(x1, w["mix_norm"], dres2),
                               extra_specs=(pl.BlockSpec((tm, D_MODEL), lambda i, j, kk: (i, 0)),
                                            pl.BlockSpec((1, D_MODEL), lambda i, j, kk: (0, 0)),
                                            pl.BlockSpec((tm, D_MODEL), lambda i, j, kk: (i, 0))),
                               out_shape=(jax.ShapeDtypeStruct((t, D_MODEL), F32), jax.ShapeDtypeStruct((1, D_MODEL), F32)),
                               out_specs=(pl.BlockSpec((tm, D_MODEL), lambda i, j, kk: (i, 0)),
                                          pl.BlockSpec((1, D_MODEL), lambda i, j, kk: (0, 0))),
                               epilogue=_rms_bwd_epilogue, sem=("arbitrary", "arbitrary", "arbitrary"))

    comm.reduce_finish("R1", dres1)
    tokens = comm.reduce_start("R2", big)

    def ffn1_grads(dw_in4, dw_out):
        comm.reduce_finish("R2", dw_in4)
        return comm.reduce_start("R3", {"ffn1_w_in": dw_in4, "ffn1_w_out": dw_out})

    grad_x, _, _, g["ffn1_norm"] = _ffn_bwd(
        "ffn1b", dres1, x, w["ffn1_norm"], ffn1_saved, w["ffn1_w_in"], w["ffn1_w_out"], tokens, ffn1_grads)
    comm.reduce_finish("R3", grad_x)
    return loss[0, 0], grad_x, g


def _mesh_place():
    x, y, c = lax.axis_index("x"), lax.axis_index("y"), lax.axis_index("c")
    chips = [(1 - x, y), (x, 1 - y), (1 - x, 1 - y)]
    return x, y, c, chips


def _any_specs(n):
    return [pl.BlockSpec(memory_space=pl.ANY)] * n


HBM_SPEC = pl.BlockSpec(memory_space=pltpu.HBM)
SEM_SPEC = pl.BlockSpec(memory_space=pltpu.SEMAPHORE)
EFFECT = pltpu.SideEffectType.DATAFLOW_SIDE_EFFECTING


def _in_hbm(a):
    return pltpu.with_memory_space_constraint(a, pltpu.HBM)


def _copy_start(name, srcs, lands, plan, after=()):
    ns, nb = len(srcs), len(lands)
    n = 3 * ns

    def body(*refs):
        s_refs, l_refs = refs[:ns], refs[ns:ns + nb]
        send_sems, recv_sems = refs[ns + nb + len(after)], refs[ns + nb + len(after) + 1]
        token = refs[-1]
        for k, (src, dst, to, _) in enumerate(plan(s_refs, l_refs)):
            pltpu.make_async_remote_copy(src_ref=src, dst_ref=dst, send_sem=send_sems.at[k], recv_sem=recv_sems.at[k],
                                         device_id=to, device_id_type=MESH).start()
        token[...] = jnp.zeros_like(token)

    bufs = list(srcs) + list(lands)
    outs = pl.pallas_call(
        body, name=name,
        out_shape=(pltpu.SemaphoreType.DMA((n,)), pltpu.SemaphoreType.DMA((n,)),
                   *[pltpu.HBM(a.shape, a.dtype) for a in bufs], jax.ShapeDtypeStruct((8, LANES), F32)),
        in_specs=[HBM_SPEC] * len(bufs) + [pl.BlockSpec(memory_space=pl.ANY)] * len(after),
        out_specs=(SEM_SPEC, SEM_SPEC, *[HBM_SPEC] * len(bufs), pl.BlockSpec(memory_space=pltpu.VMEM)),
        input_output_aliases={i: 2 + i for i in range(len(bufs))},
        compiler_params=pltpu.CompilerParams(has_side_effects=EFFECT),
    )(*[_in_hbm(a) for a in bufs], *after)
    return outs[0], outs[1], list(outs[2:2 + ns]), list(outs[2 + ns:2 + ns + nb]), outs[-1]


def _copy_wait(name, send_sems, recv_sems, srcs, lands, after, plan):
    ns, nb = len(srcs), len(lands)

    def body(*refs):
        s_refs, l_refs = refs[:ns], refs[ns:ns + nb]
        send_sems, recv_sems = refs[ns + nb], refs[ns + nb + 1]
        for k, (src, _, to, mine) in enumerate(plan(s_refs, l_refs)):
            cp = pltpu.make_async_remote_copy(src_ref=src, dst_ref=mine, send_sem=send_sems.at[k], recv_sem=recv_sems.at[k],
                                              device_id=to, device_id_type=MESH)
            cp.wait_send()
            cp.wait_recv()

    bufs = list(srcs) + list(lands)
    outs = pl.pallas_call(
        body, name=name,
        out_shape=tuple(pltpu.HBM(a.shape, a.dtype) for a in bufs),
        in_specs=[HBM_SPEC] * len(bufs) + [SEM_SPEC, SEM_SPEC, pl.BlockSpec(memory_space=pl.ANY)],
        out_specs=tuple([HBM_SPEC] * len(bufs)),
        input_output_aliases={i: i for i in range(len(bufs))},
        compiler_params=pltpu.CompilerParams(has_side_effects=EFFECT),
    )(*bufs, send_sems, recv_sems, after)
    return list(outs[:ns]), list(outs[ns:])


def _gather_plan(s_refs, l_refs):
    x, y, c, chips = _mesh_place()
    jme = 2 * x + y
    return [(s.at[c], land.at[jme, c], (*chip, c), land.at[2 * chip[0] + chip[1], c])
            for s, land in zip(s_refs, l_refs) for chip in chips]


def _scatter_plan(s_refs, l_refs):
    x, y, c, chips = _mesh_place()
    return [(s.at[2 * chip[0] + chip[1]], land.at[k], (*chip, c), land.at[k])
            for s, land in zip(s_refs, l_refs) for k, chip in enumerate(chips)]


def _gather_forward(name, shards, landed):
    nw = len(shards)

    def body(*refs):
        s_refs, o_refs = refs[:nw], refs[2 * nw:3 * nw]
        send_sems, recv_sems, local_sems = refs[3 * nw:]
        x, y, c, chips = _mesh_place()
        me, sib, jme = (x, y, c), (x, y, 1 - c), 2 * x + y
        local = [pltpu.make_async_copy(s_refs[w], o_refs[w].at[jme], local_sems.at[w]) for w in range(nw)]
        for cp in local:
            cp.start()
        sent = []
        for w in range(nw):
            for k, chip in enumerate(chips):
                part = o_refs[w].at[2 * chip[0] + chip[1], c]
                cp = pltpu.make_async_remote_copy(src_ref=part, dst_ref=part, send_sem=send_sems.at[3 * w + k],
                                                  recv_sem=recv_sems.at[3 * w + k], device_id=sib, device_id_type=MESH)
                cp.start()
                sent.append(cp)
        for w in range(nw):
            for k, chip in enumerate(chips):
                part = o_refs[w].at[2 * chip[0] + chip[1], 1 - c]
                pltpu.make_async_remote_copy(src_ref=part, dst_ref=part, send_sem=send_sems.at[3 * w + k],
                                             recv_sem=recv_sems.at[3 * w + k], device_id=me, device_id_type=MESH).wait_recv()
        for cp in sent:
            cp.wait_send()
        for cp in local:
            cp.wait()

    return pl.pallas_call(
        body, name=name,
        in_specs=_any_specs(2 * nw), out_specs=_any_specs(nw),
        out_shape=[jax.ShapeDtypeStruct(a.shape, a.dtype) for a in landed],
        input_output_aliases={nw + i: i for i in range(nw)},
        scratch_shapes=[pltpu.SemaphoreType.DMA((3 * nw,)), pltpu.SemaphoreType.DMA((3 * nw,)),
                        pltpu.SemaphoreType.DMA((nw,))],
    )(*shards, *landed)


def _exchange_halves(name, grads, after=()):
    nw = len(grads)

    def body(*refs):
        g_refs, o_refs = refs[:nw], refs[nw + len(after):2 * nw + len(after)]
        send_sems, recv_sems = refs[2 * nw + len(after):]
        x, y, c, _ = _mesh_place()
        copies = []
        for w in range(nw):
            cp = pltpu.make_async_remote_copy(src_ref=g_refs[w].at[:, 1 - c], dst_ref=o_refs[w], send_sem=send_sems.at[w],
                                              recv_sem=recv_sems.at[w], device_id=(x, y, 1 - c), device_id_type=MESH)
            cp.start()
            copies.append(cp)
        for cp in copies:
            cp.wait()

    return pl.pallas_call(
        body, name=name,
        in_specs=_any_specs(nw + len(after)), out_specs=_any_specs(nw),
        out_shape=[jax.ShapeDtypeStruct((N_CHIPS,) + g.shape[2:], F32) for g in grads],
        scratch_shapes=[pltpu.SemaphoreType.DMA((nw,)), pltpu.SemaphoreType.DMA((nw,))],
    )(*grads, *after)


def _row_tile(r):
    for cand in (256, 176, 128, 64, 32, 16, 8):
        if r % cand == 0:
            return cand
    return r


def _add_own_half(c_idx, grad, got):
    _, _, r, cols = grad.shape
    tr = _row_tile(r)

    def body(c_ref, g_ref, o_ref, out_ref):
        out_ref[...] = (g_ref[...] + o_ref[...]).astype(BF16)

    return pl.pallas_call(
        body, name="add_own_half",
        grid_spec=pltpu.PrefetchScalarGridSpec(
            num_scalar_prefetch=1, grid=(N_CHIPS, r // tr),
            in_specs=[pl.BlockSpec((None, None, tr, cols), lambda j, i, c_ref: (j, c_ref[0], i, 0)),
                      pl.BlockSpec((None, tr, cols), lambda j, i, c_ref: (j, i, 0))],
            out_specs=pl.BlockSpec((None, tr, cols), lambda j, i, c_ref: (j, i, 0))),
        out_shape=jax.ShapeDtypeStruct((N_CHIPS, r, cols), BF16),
        compiler_params=_params(("parallel", "parallel")),
    )(c_idx, grad, got)


def _sum_pieces(chip_idx, sums, landed):
    _, r, cols = sums.shape
    tr = _row_tile(r)

    def body(j_ref, own_ref, p_ref, o_ref):
        o_ref[...] = ((own_ref[...].astype(F32) + p_ref[0].astype(F32)) + p_ref[1].astype(F32)) + p_ref[2].astype(F32)

    return pl.pallas_call(
        body, name="sum_pieces",
        grid_spec=pltpu.PrefetchScalarGridSpec(
            num_scalar_prefetch=1, grid=(r // tr,),
            in_specs=[pl.BlockSpec((None, tr, cols), lambda i, j_ref: (j_ref[0], i, 0)),
                      pl.BlockSpec((N_CHIPS - 1, tr, cols), lambda i, j_ref: (0, i, 0))],
            out_specs=pl.BlockSpec((tr, cols), lambda i, j_ref: (i, 0))),
        out_shape=jax.ShapeDtypeStruct((r, cols), F32),
        compiler_params=_params(("parallel",)),
    )(chip_idx, sums, landed)


def _join_halves(name, halves):
    nw = len(halves)

    def body(*refs):
        h_refs, o_refs = refs[:nw], refs[nw:2 * nw]
        send_sems, recv_sems, local_sems = refs[2 * nw:]
        x, y, c, _ = _mesh_place()
        local = [pltpu.make_async_copy(h_refs[w], o_refs[w].at[c], local_sems.at[w]) for w in range(nw)]
        for cp in local:
            cp.start()
        copies = []
        for w in range(nw):
            cp = pltpu.make_async_remote_copy(src_ref=h_refs[w], dst_ref=o_refs[w].at[c], send_sem=send_sems.at[w],
                                              recv_sem=recv_sems.at[w], device_id=(x, y, 1 - c), device_id_type=MESH)
            cp.start()
            copies.append(cp)
        for w in range(nw):
            copies[w].wait_send()
            landed = o_refs[w].at[1 - c]
            pltpu.make_async_remote_copy(src_ref=landed, dst_ref=landed, send_sem=send_sems.at[w], recv_sem=recv_sems.at[w],
                                         device_id=(x, y, c), device_id_type=MESH).wait_recv()
        for cp in local:
            cp.wait()

    return pl.pallas_call(
        body, name=name,
        in_specs=_any_specs(nw), out_specs=_any_specs(nw),
        out_shape=[jax.ShapeDtypeStruct((2,) + h.shape, F32) for h in halves],
        scratch_shapes=[pltpu.SemaphoreType.DMA((nw,)), pltpu.SemaphoreType.DMA((nw,)), pltpu.SemaphoreType.DMA((nw,))],
    )(*halves)


SMALL_ROWS = 8


def _all_reduce_small(pack):
    rows, cols = pack.shape
    n_dev = 8

    def body(p_ref, o_ref, slots, send_sems, recv_sems):
        x, y, c, _ = _mesh_place()
        me = 4 * x + 2 * y + c
        slots[me] = p_ref[...]
        copies = []
        for k in range(1, n_dev):
            peer = (me + k) % n_dev
            cp = pltpu.make_async_remote_copy(src_ref=p_ref, dst_ref=slots.at[me], send_sem=send_sems.at[k],
                                              recv_sem=recv_sems.at[k],
                                              device_id=(peer // 4, (peer // 2) % 2, peer % 2), device_id_type=MESH)
            cp.start()
            copies.append(cp)
        for k in range(1, n_dev):
            src = (me + n_dev - k) % n_dev
            pltpu.make_async_remote_copy(src_ref=p_ref, dst_ref=slots.at[src], send_sem=send_sems.at[k],
                                         recv_sem=recv_sems.at[k], device_id=(x, y, c), device_id_type=MESH).wait_recv()
        for cp in copies:
            cp.wait_send()
        total = slots[0]
        for s in range(1, n_dev):
            total = total + slots[s]
        o_ref[...] = total

    return pl.pallas_call(
        body, name="all_reduce_small",
        in_specs=[pl.BlockSpec(memory_space=pltpu.VMEM)], out_specs=pl.BlockSpec(memory_space=pltpu.VMEM),
        out_shape=jax.ShapeDtypeStruct((rows, cols), F32),
        scratch_shapes=[pltpu.VMEM((n_dev, rows, cols), F32), pltpu.SemaphoreType.DMA((n_dev,)),
                        pltpu.SemaphoreType.DMA((n_dev,))],
    )(pack)


def _adamw(name, w, g, m, v):
    r, cols = w.shape
    tr = _row_tile(r)

    def body(w_ref, g_ref, m_ref, v_ref, d_ref, nm_ref, nv_ref):
        gv = g_ref[...]
        nm = ADAM_B1 * m_ref[...] + (1.0 - ADAM_B1) * gv
        nv = ADAM_B2 * v_ref[...] + (1.0 - ADAM_B2) * (gv * gv)
        m_hat = nm / (1.0 - ADAM_B1 ** ADAM_STEP)
        v_hat = nv / (1.0 - ADAM_B2 ** ADAM_STEP)
        d_ref[...] = -ADAM_LR * (m_hat / (jnp.sqrt(v_hat) + ADAM_EPS) + ADAM_WD * w_ref[...])
        nm_ref[...] = nm
        nv_ref[...] = nv

    spec = pl.BlockSpec((tr, cols), lambda i: (i, 0))
    return pl.pallas_call(
        body, name=name, grid=(r // tr,),
        in_specs=[spec] * 4, out_specs=[spec] * 3,
        out_shape=[jax.ShapeDtypeStruct((r, cols), F32)] * 3,
        compiler_params=_params(("parallel",)),
    )(w, g, m, v)


BIG = ["ffn1_w_in", "ffn1_w_out", "w_in", "conv_w_proj", "attn_w_o", "w_out", "ffn2_w_in", "ffn2_w_out", "conv_dw_kernel"]
COL_SHARDED = ("ffn1_w_in", "w_in", "ffn2_w_in")
SMALL = ["ffn1_norm", "mix_norm", "ffn2_norm", "conv_dw_bias", "conv_ln_g", "conv_ln_b", "q_norm", "k_norm", "attn_sinks", "rel_bias"]
WEIGHTS = ["ffn1_norm", "ffn1_w_in", "ffn1_w_out", "mix_norm", "w_in", "conv_dw_kernel", "conv_dw_bias", "conv_ln_g",
           "conv_ln_b", "conv_w_proj", "q_norm", "k_norm", "attn_sinks", "rel_bias", "attn_w_o", "w_out", "ffn2_norm",
           "ffn2_w_in", "ffn2_w_out"]
SMALL_PLACE = {"ffn1_norm": (0, 0, 1024), "mix_norm": (1, 0, 1024), "ffn2_norm": (2, 0, 1024), "conv_dw_bias": (3, 0, 1024),
               "conv_ln_g": (4, 0, 1024), "conv_ln_b": (5, 0, 1024), "q_norm": (6, 0, 64), "k_norm": (6, 128, 64),
               "attn_sinks": (6, 256, 16), "rel_bias": (7, 0, 512)}
LOSS_PLACE = (6, 384)


def _pack_small(vals, fill=0.0, loss=None):
    pack = jnp.full((SMALL_ROWS, D_MODEL), fill, F32)
    for name, (row, lane, n) in SMALL_PLACE.items():
        pack = pack.at[row, lane:lane + n].set(vals[name].reshape(n))
    if loss is not None:
        pack = pack.at[LOSS_PLACE[0], LOSS_PLACE[1]].set(loss)
    return pack


def _unpack_small(pack, shapes):
    return {name: pack[row, lane:lane + n].reshape(shapes[name]) for name, (row, lane, n) in SMALL_PLACE.items()}


def _shard_halves(name, a):
    if name == "conv_dw_kernel":
        a = jnp.pad(a, ((0, CONV_PAD - CONV_WIDTH), (0, 0)))
    r, cols = a.shape
    return a.reshape(2, r // 2, cols)


GATHER_GROUPS = {"A": ["ffn1_w_in", "ffn1_w_out"],
                 "B": ["w_in", "conv_dw_kernel", "conv_w_proj", "attn_w_o", "w_out"],
                 "C": ["ffn2_w_in", "ffn2_w_out"]}


class _MeshComm:
    def __init__(self, wts):
        self.c_idx = lax.axis_index("c").astype(jnp.int32).reshape(1)
        self.chip_idx = (2 * lax.axis_index("x") + lax.axis_index("y")).astype(jnp.int32).reshape(1)
        self.wts, self.gathers, self.reductions, self.reduced = wts, {}, {}, {}
        self.tokens, self.last_join = (), ()
        self._gather_start("A", ())

    def _gather_start(self, group, after):
        names = GATHER_GROUPS[group]
        shards = [_shard_halves(n, self.wts[n]) if n == "conv_dw_kernel" else _shard_halves(n, self.wts[n]).astype(BF16)
                  for n in names]
        lands = [lax.empty((N_CHIPS,) + s.shape, s.dtype) for s in shards]
        self.gathers[group] = _copy_start("gather_start_" + group, shards, lands, _gather_plan, after=after)
        self.tokens = (self.gathers[group][-1],)

    def weights(self, group, after):
        send_sems, recv_sems, shards, lands, token = self.gathers.pop(group)
        shards, lands = _copy_wait("gather_wait_" + group, send_sems, recv_sems, shards, lands,
                                   token if after is None else after, _gather_plan)
        gathered = _gather_forward("gather_forward_" + group, shards, lands)
        self.tokens = ()
        following = {"A": "B", "B": "C"}.get(group)
        if following:
            self._gather_start(following, (gathered[0],))
        out = {}
        for n, g4 in zip(GATHER_GROUPS[group], gathered):
            r, cols = g4.shape[2] * 2, g4.shape[3]
            if n in COL_SHARDED:
                out[n] = g4.reshape(N_CHIPS, r, cols)
            elif n == "conv_dw_kernel":
                out[n] = g4.reshape(N_CHIPS, r, cols).transpose(1, 0, 2).reshape(r, N_CHIPS * cols)
            else:
                out[n] = g4.reshape(N_CHIPS * r, cols)
        return out

    def reduce_start(self, group, grads):
        names = list(grads)
        g4 = []
        for n in names:
            a = grads[n]
            if n == "conv_dw_kernel":
                a = a.reshape(CONV_PAD, N_CHIPS, -1).transpose(1, 0, 2)
            elif n not in COL_SHARDED:
                a = a.reshape(N_CHIPS, a.shape[0] // N_CHIPS, a.shape[1])
            g4.append(a.reshape(N_CHIPS, 2, a.shape[1] // 2, a.shape[2]))
        got = _exchange_halves("exchange_halves_" + group, g4, after=self.last_join)
        sums = [_add_own_half(self.c_idx, a, b) for a, b in zip(g4, got)]
        lands = [lax.empty((N_CHIPS - 1,) + s.shape[1:], s.dtype) for s in sums]
        started = _copy_start("scatter_start_" + group, sums, lands, _scatter_plan, after=self.last_join)
        self.reductions[group] = (names,) + started
        return (started[-1],)

    def reduce_finish(self, group, after):
        names, send_sems, recv_sems, sums, lands, _ = self.reductions.pop(group)
        sums, lands = _copy_wait("scatter_wait_" + group, send_sems, recv_sems, sums, lands, after, _scatter_plan)
        halves = [_sum_pieces(self.chip_idx, s, p) for s, p in zip(sums, lands)]
        joined = _join_halves("join_halves_" + group, halves)
        self.last_join = (joined[0],)
        self.reduced.update(zip(names, joined))


def kernel(x, ffn1_norm, ffn1_w_in, ffn1_w_out, mix_norm, w_in, conv_dw_kernel, conv_dw_bias, conv_ln_g, conv_ln_b, conv_w_proj, q_norm, k_norm, attn_sinks, rel_bias, attn_w_o, w_out, ffn2_norm, ffn2_w_in, ffn2_w_out, loss_target, m_ffn1_norm, m_ffn1_w_in, m_ffn1_w_out, m_mix_norm, m_w_in, m_conv_dw_kernel, m_conv_dw_bias, m_conv_ln_g, m_conv_ln_b, m_conv_w_proj, m_q_norm, m_k_norm, m_attn_sinks, m_rel_bias, m_attn_w_o, m_w_out, m_ffn2_norm, m_ffn2_w_in, m_ffn2_w_out, v_ffn1_norm, v_ffn1_w_in, v_ffn1_w_out, v_mix_norm, v_w_in, v_conv_dw_kernel, v_conv_dw_bias, v_conv_ln_g, v_conv_ln_b, v_conv_w_proj, v_q_norm, v_k_norm, v_attn_sinks, v_rel_bias, v_attn_w_o, v_w_out, v_ffn2_norm, v_ffn2_w_in, v_ffn2_w_out):
    args = dict(locals())
    wts = {n: args[n] for n in WEIGHTS}
    mom = {n: args["m_" + n] for n in WEIGHTS}
    var = {n: args["v_" + n] for n in WEIGHTS}
    comm = _MeshComm(wts)
    small = {n: wts[n] if n in ("attn_sinks", "rel_bias") else wts[n].reshape(1, -1) for n in SMALL}
    loss_part, grad_x, g = _local_step(x[0], loss_target[0], small, comm)

    small_sum = _all_reduce_small(_pack_small(g, loss=loss_part))
    loss = small_sum[LOSS_PLACE[0], LOSS_PLACE[1]]
    small_shapes = {n: wts[n].shape for n in SMALL}
    g_small = _unpack_small(small_sum, small_shapes)

    grads, delta, new_m, new_v = {}, {}, {}, {}
    for n in BIG:
        j = comm.reduced[n]
        gs = j.reshape(j.shape[1] * 2, j.shape[2])
        pad = n == "conv_dw_kernel"
        ws, ms, vs = (_shard_halves(n, a).reshape(gs.shape) for a in (wts[n], mom[n], var[n]))
        d, nm, nv = _adamw("adamw_" + n, ws, gs, ms, vs)
        cut = (lambda a: a[:CONV_WIDTH]) if pad else (lambda a: a)
        grads[n], delta[n], new_m[n], new_v[n] = cut(gs), cut(d), cut(nm), cut(nv)
    d, nm, nv = _adamw("adamw_small", _pack_small(wts), small_sum, _pack_small(mom), _pack_small(var, fill=1.0))
    grads.update(g_small)
    delta.update(_unpack_small(d, small_shapes))
    new_m.update(_unpack_small(nm, small_shapes))
    new_v.update(_unpack_small(nv, small_shapes))

    return (loss, grad_x[None], *[grads[n] for n in WEIGHTS], *[delta[n] for n in WEIGHTS],
            *[new_m[n] for n in WEIGHTS], *[new_v[n] for n in WEIGHTS])
```

```python
import functools
import math

import jax
import jax.numpy as jnp
from jax import lax
from jax.experimental import pallas as pl
from jax.experimental.pallas import tpu as pltpu

F32 = jnp.float32
BF16 = jnp.bfloat16
MESH = pl.DeviceIdType.MESH

EPS = 1e-6
D_MODEL = 1024
D_FF = 2816
N_CHIPS = 4
SHARD_W = 2 * D_FF // N_CHIPS
HEAD_DIM = 64
N_Q_HEADS = 16
N_KV_HEADS = 4
GROUP = N_Q_HEADS // N_KV_HEADS
BLOCK = 128
QROWS = GROUP * BLOCK
N_BUCKETS = 32
MAX_DISTANCE = 128
CONV_WIDTH = 31
CONV_PAD = 32
NEG = float(jnp.finfo(jnp.float32).min)

ADAM_LR = 0.001
ADAM_B1 = 0.9
ADAM_B2 = 0.999
ADAM_EPS = 1e-08
ADAM_WD = 0.01
ADAM_STEP = 10

VMEM_LIMIT_BYTES = 56 * 1024 * 1024
ROW_TILE = 512
CONV_TILE = 256
CONV_ROWS = 32
LANES = 128

COL_CONV_A, COL_CONV_G, COL_Q, COL_K, COL_V, COL_GC, COL_GA = 0, 1024, 2048, 3072, 3328, 3584, 4608
IN_W = 5632


def _params(sem, vmem=VMEM_LIMIT_BYTES):
    return pltpu.CompilerParams(dimension_semantics=sem, vmem_limit_bytes=vmem)


def _sigmoid(x):
    return 1.0 / (1.0 + jnp.exp(-x))


def _dot(a, b, trans_a=False, trans_b=False, precision=None):
    dn = (((0,) if trans_a else (1,), (1,) if trans_b else (0,)), ((), ()))
    return lax.dot_general(a, b, dn, preferred_element_type=F32, precision=precision)


def _mm(name, grid, a, a_spec, b, b_spec, acc_shape, *, trans_a=False, trans_b=False, a_pre=None, b_pre=None,
        extras=(), extra_specs=(), tokens=(), out_shape, out_specs, epilogue, sem=("parallel", "parallel", "arbitrary")):
    n_k = grid[2]
    extras = tuple(extras) + tuple(tokens)
    extra_specs = tuple(extra_specs) + (pl.BlockSpec((8, LANES), lambda i, j, kk: (0, 0)),) * len(tokens)
    n_extra = len(extras)
    n_out = len(out_shape)

    def body(a_ref, b_ref, *rest):
        ex = rest[:n_extra]
        outs = rest[n_extra:n_extra + n_out]
        ids = (pl.program_id(0), pl.program_id(1), pl.program_id(2))
        av = a_ref[...]
        bv = b_ref[...]
        if a_pre is not None:
            av = a_pre(av)
        if b_pre is not None:
            bv = b_pre(bv)
        part = _dot(av, bv, trans_a, trans_b)
        if n_k == 1:
            epilogue(part, ex, outs, ids)
        else:
            acc = rest[-1]

            @pl.when(ids[2] == 0)
            def _():
                acc[...] = part

            @pl.when(ids[2] > 0)
            def _():
                acc[...] += part

            @pl.when(ids[2] == n_k - 1)
            def _():
                epilogue(acc[...], ex, outs, ids)

    scratch = [] if n_k == 1 else [pltpu.VMEM(acc_shape, F32)]
    return pl.pallas_call(
        body, name=name, grid=grid,
        in_specs=[a_spec, b_spec, *extra_specs],
        out_specs=list(out_specs), out_shape=list(out_shape),
        scratch_shapes=scratch, compiler_params=_params(sem),
    )(a, b, *extras)


def _half_bf16(v):
    return (0.5 * v).astype(BF16)


def _to_bf16(v):
    return v.astype(BF16)


def _rmsnorm_fwd(name, x, g, tokens=()):
    t, d = x.shape
    tm = min(ROW_TILE, t)

    def body(x_ref, g_ref, *rest):
        o_ref = rest[-1]
        xv = x_ref[...]
        r = lax.rsqrt(jnp.mean(xv * xv, axis=-1, keepdims=True) + EPS)
        o_ref[...] = (xv * r * g_ref[...]).astype(BF16)

    return pl.pallas_call(
        body, name=name, grid=(t // tm,),
        in_specs=[pl.BlockSpec((tm, d), lambda i: (i, 0)), pl.BlockSpec((1, d), lambda i: (0, 0))]
        + [pl.BlockSpec((8, LANES), lambda i: (0, 0))] * len(tokens),
        out_specs=pl.BlockSpec((tm, d), lambda i: (i, 0)),
        out_shape=jax.ShapeDtypeStruct((t, d), BF16),
        compiler_params=_params(("parallel",)),
    )(x, g, *tokens)


def _rms_bwd_epilogue(acc, ex, outs, ids):
    x_ref, g_ref, dres_ref = ex[:3]
    out_ref, dg_ref = outs
    xv = x_ref[...]
    r = lax.rsqrt(jnp.mean(xv * xv, axis=-1, keepdims=True) + EPS)
    w = acc * g_ref[...]
    dx = r * w - xv * (r * r * r) * jnp.mean(xv * w, axis=-1, keepdims=True)
    out_ref[...] = dres_ref[...] + dx
    part = jnp.sum(acc * (xv * r), axis=0, keepdims=True)

    @pl.when(ids[0] == 0)
    def _():
        dg_ref[...] = part

    @pl.when(ids[0] > 0)
    def _():
        dg_ref[...] += part


def _ffn_in(name, n, w_in4):
    t, d = n.shape
    tm = min(ROW_TILE, t)

    def body(n_ref, wa_ref, wb_ref, ab_ref, h_ref):
        nv = n_ref[...]
        a = _dot(nv, wa_ref[...])
        b = _dot(nv, wb_ref[...])
        h_ref[...] = (a * _sigmoid(a) * b).astype(BF16)
        ab_ref[0] = a.astype(BF16)
        ab_ref[1] = b.astype(BF16)

    return pl.pallas_call(
        body, name=name, grid=(2, t // tm),
        in_specs=[pl.BlockSpec((tm, d), lambda j, i: (i, 0)),
                  pl.BlockSpec((None, d, SHARD_W), lambda j, i: (j, 0, 0)),
                  pl.BlockSpec((None, d, SHARD_W), lambda j, i: (j + 2, 0, 0))],
        out_specs=[pl.BlockSpec((2, tm, SHARD_W), lambda j, i: (0, i, j)),
                   pl.BlockSpec((tm, SHARD_W), lambda j, i: (i, j))],
        out_shape=[jax.ShapeDtypeStruct((2, t, D_FF), BF16), jax.ShapeDtypeStruct((t, D_FF), BF16)],
        compiler_params=_params(("parallel", "parallel")),
    )(n, w_in4, w_in4)


def _mm_residual(name, a, w, res, scale):
    t, k = a.shape
    n = w.shape[1]
    tm = min(ROW_TILE, t)

    def epilogue(acc, ex, outs, ids):
        outs[0][...] = ex[0][...] + scale * acc

    return _mm(name, (t // tm, 1, 1), a, pl.BlockSpec((tm, k), lambda i, j, kk: (i, 0)),
               w, pl.BlockSpec((k, n), lambda i, j, kk: (0, 0)), (tm, n),
               extras=(res,), extra_specs=(pl.BlockSpec((tm, n), lambda i, j, kk: (i, 0)),),
               out_shape=(jax.ShapeDtypeStruct((t, n), F32),),
               out_specs=(pl.BlockSpec((tm, n), lambda i, j, kk: (i, 0)),), epilogue=epilogue)[0]


def _ffn_fwd(tag, x, g, w_in4, w_out, tokens=()):
    n = _rmsnorm_fwd(tag + "_norm", x, g, tokens)
    ab, h = _ffn_in(tag + "_in", n, w_in4)
    y = _mm_residual(tag + "_out", h, w_out, x, 0.5)
    return y, (n, ab, h)


def _ffn_bwd(tag, dres, x, g, saved, w_in4, w_out, tokens=(), on_weight_grads=None):
    n, ab, h = saved
    t, d = x.shape
    tm = min(ROW_TILE, t)
    tk = min(ROW_TILE, t)
    half_w = SHARD_W

    def dact_epilogue(acc, ex, outs, ids):
        a = ex[0][0].astype(F32)
        b = ex[0][1].astype(F32)
        sig = _sigmoid(a)
        outs[0][0] = (acc * b * (sig * (1.0 + a * (1.0 - sig)))).astype(BF16)
        outs[0][1] = (acc * (a * sig)).astype(BF16)

    du = _mm(tag + "_dact", (2, t // tm, 1),
             dres, pl.BlockSpec((tm, d), lambda j, i, kk: (i, 0)),
             w_out, pl.BlockSpec((half_w, d), lambda j, i, kk: (j, 0)), (tm, half_w),
             trans_b=True, a_pre=_half_bf16,
             extras=(ab,), extra_specs=(pl.BlockSpec((2, tm, half_w), lambda j, i, kk: (0, i, j)),), tokens=tokens,
             out_shape=(jax.ShapeDtypeStruct((2, t, D_FF), BF16),),
             out_specs=(pl.BlockSpec((2, tm, half_w), lambda j, i, kk: (0, i, j)),),
             epilogue=dact_epilogue)[0]

    def store_epilogue(acc, ex, outs, ids):
        outs[0][...] = acc

    dw_out = _mm(tag + "_dwout", (2, 1, t // tk),
                 h, pl.BlockSpec((tk, half_w), lambda i, j, kk: (kk, i)),
                 dres, pl.BlockSpec((tk, d), lambda i, j, kk: (kk, 0)), (half_w, d),
                 trans_a=True, b_pre=_half_bf16,
                 out_shape=(jax.ShapeDtypeStruct((D_FF, d), F32),),
                 out_specs=(pl.BlockSpec((half_w, d), lambda i, j, kk: (i, 0)),),
                 epilogue=store_epilogue)[0]

    dw_in4 = _mm(tag + "_dwin", (1, N_CHIPS, t // tk),
                 n, pl.BlockSpec((tk, d), lambda i, j, kk: (kk, 0)),
                 du, pl.BlockSpec((None, tk, SHARD_W), lambda i, j, kk: (j // 2, kk, j % 2)), (d, SHARD_W),
                 trans_a=True,
                 out_shape=(jax.ShapeDtypeStruct((N_CHIPS, d, SHARD_W), F32),),
                 out_specs=(pl.BlockSpec((None, d, SHARD_W), lambda i, j, kk: (j, 0, 0)),),
                 epilogue=store_epilogue)[0]

    late = () if on_weight_grads is None else on_weight_grads(dw_in4, dw_out)

    dx, dg = _mm(tag + "_dn", (t // tm, 1, N_CHIPS),
                 du, pl.BlockSpec((None, tm, SHARD_W), lambda i, j, kk: (kk // 2, i, kk % 2)),
                 w_in4, pl.BlockSpec((None, d, SHARD_W), lambda i, j, kk: (kk, 0, 0)), (tm, d),
                 trans_b=True,
                 extras=(x, g, dres),
                 extra_specs=(pl.BlockSpec((tm, d), lambda i, j, kk: (i, 0)),
                              pl.BlockSpec((1, d), lambda i, j, kk: (0, 0)),
                              pl.BlockSpec((tm, d), lambda i, j, kk: (i, 0))), tokens=late,
                 out_shape=(jax.ShapeDtypeStruct((t, d), F32), jax.ShapeDtypeStruct((1, d), F32)),
                 out_specs=(pl.BlockSpec((tm, d), lambda i, j, kk: (i, 0)),
                            pl.BlockSpec((1, d), lambda i, j, kk: (0, 0))),
                 epilogue=_rms_bwd_epilogue, sem=("arbitrary", "arbitrary", "arbitrary"))
    return dx, dw_in4, dw_out, dg


def _loss_head(y, target):
    t, d = y.shape
    tm = min(ROW_TILE, t)

    def body(y_ref, t_ref, dy_ref, loss_ref):
        diff = y_ref[...] - t_ref[...]
        dy_ref[...] = diff * (1.0 / d)
        part = jnp.full((8, LANES), 0.5 / d * jnp.sum(diff * diff), F32)
        i = pl.program_id(0)

        @pl.when(i == 0)
        def _():
            loss_ref[...] = part

        @pl.when(i > 0)
        def _():
            loss_ref[...] += part

    return pl.pallas_call(
        body, name="loss_head", grid=(t // tm,),
        in_specs=[pl.BlockSpec((tm, d), lambda i: (i, 0)), pl.BlockSpec((tm, d), lambda i: (i, 0))],
        out_specs=[pl.BlockSpec((tm, d), lambda i: (i, 0)), pl.BlockSpec((8, LANES), lambda i: (0, 0))],
        out_shape=[jax.ShapeDtypeStruct((t, d), F32), jax.ShapeDtypeStruct((8, LANES), F32)],
        compiler_params=_params(("arbitrary",)),
    )(y, target)


def _conv_fill(zp_ref, a_ref, g_ref, ah_ref, gh_ref, i):
    zh = ah_ref[...].astype(F32) * _sigmoid(gh_ref[...].astype(F32))
    zp_ref[pl.ds(0, CONV_PAD), :] = jnp.where(i > 0, zh, 0.0)
    zp_ref[pl.ds(CONV_PAD, a_ref.shape[0]), :] = a_ref[...].astype(F32) * _sigmoid(g_ref[...].astype(F32))


def _conv_taps(zp_ref, z1_ref, dw_ref, bias_ref, tm, ch):
    first = CONV_PAD - (CONV_WIDTH - 1)
    for cc in range(ch // LANES):
        lanes = pl.ds(cc * LANES, LANES)
        w = [dw_ref[pl.ds(j, 1), lanes] for j in range(CONV_WIDTH)]
        bias = bias_ref[:, lanes]
        for rb in range(tm // CONV_ROWS):
            acc = jnp.broadcast_to(bias, (CONV_ROWS, LANES))
            for j in range(CONV_WIDTH):
                acc = acc + w[j] * zp_ref[pl.ds(rb * CONV_ROWS + first + j, CONV_ROWS), lanes]
            z1_ref[pl.ds(rb * CONV_ROWS, CONV_ROWS), lanes] = acc


def _conv_specs(tm, ch):
    per = tm // CONV_PAD
    cb = COL_CONV_G // ch
    return [pl.BlockSpec((tm, ch), lambda i: (i, 0)),
            pl.BlockSpec((tm, ch), lambda i: (i, cb)),
            pl.BlockSpec((CONV_PAD, ch), lambda i: (jnp.maximum(i * per - 1, 0), 0)),
            pl.BlockSpec((CONV_PAD, ch), lambda i: (jnp.maximum(i * per - 1, 0), cb))]


def _conv_fwd(p, dw, bias, ln_g, ln_b):
    t = p.shape[0]
    ch = D_MODEL
    tm = min(CONV_TILE, t)

    def body(a_ref, g_ref, ah_ref, gh_ref, dw_ref, bias_ref, lg_ref, lb_ref, o_ref, zp_ref, z1_ref):
        i = pl.program_id(0)
        _conv_fill(zp_ref, a_ref, g_ref, ah_ref, gh_ref, i)
        _conv_taps(zp_ref, z1_ref, dw_ref, bias_ref, tm, ch)
        z1 = z1_ref[...]
        mu = jnp.mean(z1, axis=-1, keepdims=True)
        zc = z1 - mu
        rs = lax.rsqrt(jnp.mean(zc * zc, axis=-1, keepdims=True) + EPS)
        z2 = zc * rs * lg_ref[...] + lb_ref[...]
        o_ref[...] = (z2 * _sigmoid(z2)).astype(BF16)

    vec = pl.BlockSpec((1, ch), lambda i: (0, 0))
    return pl.pallas_call(
        body, name="conv_fwd", grid=(t // tm,),
        in_specs=_conv_specs(tm, ch) + [pl.BlockSpec((CONV_PAD, ch), lambda i: (0, 0)), vec, vec, vec],
        out_specs=pl.BlockSpec((tm, ch), lambda i: (i, 0)),
        out_shape=jax.ShapeDtypeStruct((t, ch), BF16),
        scratch_shapes=[pltpu.VMEM((CONV_PAD + tm, ch), F32), pltpu.VMEM((tm, ch), F32)],
        compiler_params=_params(("parallel",)),
    )(p, p, p, p, dw, bias, ln_g, ln_b)


def _conv_bwd_ln(p, dz3, dw, bias, ln_g, ln_b):
    t = p.shape[0]
    ch = D_MODEL
    tm = min(CONV_TILE, t)
    first = CONV_PAD - (CONV_WIDTH - 1)

    def body(a_ref, g_ref, ah_ref, gh_ref, dz3_ref, dw_ref, bias_ref, lg_ref, lb_ref,
             dz1_ref, ddw_ref, dbias_ref, dlg_ref, dlb_ref, zp_ref, z1_ref):
        i = pl.program_id(0)
        _conv_fill(zp_ref, a_ref, g_ref, ah_ref, gh_ref, i)
        _conv_taps(zp_ref, z1_ref, dw_ref, bias_ref, tm, ch)
        z1 = z1_ref[...]
        mu = jnp.mean(z1, axis=-1, keepdims=True)
        zc = z1 - mu
        rs = lax.rsqrt(jnp.mean(zc * zc, axis=-1, keepdims=True) + EPS)
        xh = zc * rs
        z2 = xh * lg_ref[...] + lb_ref[...]
        sig = _sigmoid(z2)
        dz2 = dz3_ref[...].astype(F32) * (sig * (1.0 + z2 * (1.0 - sig)))
        dxh = dz2 * lg_ref[...]
        dz1 = rs * (dxh - jnp.mean(dxh, axis=-1, keepdims=True) - xh * jnp.mean(dxh * xh, axis=-1, keepdims=True))
        dz1_ref[...] = dz1

        @pl.when(i == 0)
        def _():
            ddw_ref[...] = jnp.zeros_like(ddw_ref)
            dbias_ref[...] = jnp.zeros_like(dbias_ref)
            dlg_ref[...] = jnp.zeros_like(dlg_ref)
            dlb_ref[...] = jnp.zeros_like(dlb_ref)

        dlg_ref[...] += jnp.sum(dz2 * xh, axis=0, keepdims=True)
        dlb_ref[...] += jnp.sum(dz2, axis=0, keepdims=True)
        dbias_ref[...] += jnp.sum(dz1, axis=0, keepdims=True)
        for cc in range(ch // LANES):
            lanes = pl.ds(cc * LANES, LANES)
            accs = [jnp.zeros((8, LANES), F32) for _ in range(CONV_WIDTH)]
            for rb in range(tm // CONV_ROWS):
                dzc = dz1_ref[pl.ds(rb * CONV_ROWS, CONV_ROWS), lanes]
                for j in range(CONV_WIDTH):
                    prod = dzc * zp_ref[pl.ds(rb * CONV_ROWS + first + j, CONV_ROWS), lanes]
                    accs[j] = accs[j] + jnp.sum(prod.reshape(CONV_ROWS // 8, 8, LANES), axis=0)
            for j in range(CONV_WIDTH):
                ddw_ref[pl.ds(j, 1), lanes] += jnp.sum(accs[j], axis=0, keepdims=True)

    vec = pl.BlockSpec((1, ch), lambda i: (0, 0))
    return pl.pallas_call(
        body, name="conv_bwd_ln", grid=(t // tm,),
        in_specs=_conv_specs(tm, ch) + [pl.BlockSpec((tm, ch), lambda i: (i, 0)),
                                        pl.BlockSpec((CONV_PAD, ch), lambda i: (0, 0)), vec, vec, vec],
        out_specs=[pl.BlockSpec((tm, ch), lambda i: (i, 0)), pl.BlockSpec((CONV_PAD, ch), lambda i: (0, 0)), vec, vec, vec],
        out_shape=[jax.ShapeDtypeStruct((t, ch), F32), jax.ShapeDtypeStruct((CONV_PAD, ch), F32)]
        + [jax.ShapeDtypeStruct((1, ch), F32)] * 3,
        scratch_shapes=[pltpu.VMEM((CONV_PAD + tm, ch), F32), pltpu.VMEM((tm, ch), F32)],
        compiler_params=_params(("arbitrary",)),
    )(p, p, p, p, dz3, dw, bias, ln_g, ln_b)


def _conv_bwd_glu(p, dz1, dw):
    t = p.shape[0]
    ch = D_MODEL
    tm = min(CONV_TILE, t)
    per = tm // CONV_PAD
    n_halo = t // CONV_PAD
    cb = COL_CONV_G // ch

    def body(a_ref, g_ref, dz_ref, dzn_ref, dw_ref, o_ref, zp_ref, z0_ref):
        i = pl.program_id(0)
        zp_ref[pl.ds(0, tm), :] = dz_ref[...]
        zp_ref[pl.ds(tm, CONV_PAD), :] = jnp.where(i < t // tm - 1, dzn_ref[...], 0.0)
        for cc in range(ch // LANES):
            lanes = pl.ds(cc * LANES, LANES)
            w = [dw_ref[pl.ds(j, 1), lanes] for j in range(CONV_WIDTH)]
            for rb in range(tm // CONV_ROWS):
                acc = jnp.zeros((CONV_ROWS, LANES), F32)
                for j in range(CONV_WIDTH):
                    acc = acc + w[j] * zp_ref[pl.ds(rb * CONV_ROWS + (CONV_WIDTH - 1 - j), CONV_ROWS), lanes]
                z0_ref[pl.ds(rb * CONV_ROWS, CONV_ROWS), lanes] = acc
        dz0 = z0_ref[...]
        a = a_ref[...].astype(F32)
        sig = _sigmoid(g_ref[...].astype(F32))
        o_ref[:, pl.ds(0, ch)] = (dz0 * sig).astype(BF16)
        o_ref[:, pl.ds(ch, ch)] = (dz0 * a * sig * (1.0 - sig)).astype(BF16)

    return pl.pallas_call(
        body, name="conv_bwd_glu", grid=(t // tm,),
        in_specs=[pl.BlockSpec((tm, ch), lambda i: (i, 0)), pl.BlockSpec((tm, ch), lambda i: (i, cb)),
                  pl.BlockSpec((tm, ch), lambda i: (i, 0)),
                  pl.BlockSpec((CONV_PAD, ch), lambda i: (jnp.minimum((i + 1) * per, n_halo - 1), 0)),
                  pl.BlockSpec((CONV_PAD, ch), lambda i: (0, 0))],
        out_specs=pl.BlockSpec((tm, 2 * ch), lambda i: (i, 0)),
        out_shape=jax.ShapeDtypeStruct((t, 2 * ch), BF16),
        scratch_shapes=[pltpu.VMEM((tm + CONV_PAD, ch), F32), pltpu.VMEM((tm, ch), F32)],
        compiler_params=_params(("parallel",)),
    )(p, p, dz1, dz1, dw)


def _bucket_onehot():
    qi = jnp.arange(BLOCK, dtype=jnp.int32)[:, None]
    kj = jnp.arange(2 * BLOCK, dtype=jnp.int32)[None, :]
    dist = jnp.maximum(qi + BLOCK - kj, 0)
    max_exact = N_BUCKETS // 2
    dflt = jnp.maximum(dist, 1).astype(F32)
    large = max_exact + (jnp.log(dflt / max_exact) / math.log(MAX_DISTANCE / max_exact)
                         * (N_BUCKETS - max_exact)).astype(jnp.int32)
    large = jnp.minimum(large, N_BUCKETS - 1)
    bucket = jnp.where(dist < max_exact, dist, large)
    onehot = bucket[None] == jnp.arange(N_BUCKETS, dtype=jnp.int32)[:, None, None]
    return onehot.astype(F32).reshape(N_BUCKETS, BLOCK * 2 * BLOCK)


def _bias_table(rel_bias_t, onehot):
    def body(r_ref, oh_ref, o_ref):
        o_ref[...] = _dot(r_ref[...], oh_ref[...], precision=lax.Precision.HIGHEST)

    n = onehot.shape[1]
    tn = 4096
    return pl.pallas_call(
        body, name="bias_table", grid=(n // tn,),
        in_specs=[pl.BlockSpec((N_Q_HEADS, N_BUCKETS), lambda i: (0, 0)), pl.BlockSpec((N_BUCKETS, tn), lambda i: (0, i))],
        out_specs=pl.BlockSpec((N_Q_HEADS, tn), lambda i: (0, i)),
        out_shape=jax.ShapeDtypeStruct((N_Q_HEADS, n), F32),
        compiler_params=_params(("parallel",)),
    )(rel_bias_t, onehot)


def _bias_table_bwd(dbias, onehot):
    n = onehot.shape[1]
    tn = 4096

    def body(d_ref, oh_ref, o_ref):
        part = _dot(d_ref[...], oh_ref[...], trans_b=True, precision=lax.Precision.HIGHEST)
        i = pl.program_id(0)

        @pl.when(i == 0)
        def _():
            o_ref[...] = part

        @pl.when(i > 0)
        def _():
            o_ref[...] += part

    return pl.pallas_call(
        body, name="bias_table_bwd", grid=(n // tn,),
        in_specs=[pl.BlockSpec((N_Q_HEADS, tn), lambda i: (0, i)), pl.BlockSpec((N_BUCKETS, tn), lambda i: (0, i))],
        out_specs=pl.BlockSpec((N_Q_HEADS, N_BUCKETS), lambda i: (0, 0)),
        out_shape=jax.ShapeDtypeStruct((N_Q_HEADS, N_BUCKETS), F32),
        compiler_params=_params(("arbitrary",)),
    )(dbias, onehot)


def _attn_probs(q_ref, kp_ref, kc_ref, gq_ref, gk_ref, sink_ref, bias_ref, n):
    qf = q_ref[...].astype(F32)
    rq = lax.rsqrt(jnp.mean(qf * qf, axis=-1, keepdims=True) + EPS)
    qn = qf * rq * gq_ref[...]
    kf = jnp.concatenate([kp_ref[...], kc_ref[...]], axis=0).astype(F32)
    rk = lax.rsqrt(jnp.mean(kf * kf, axis=-1, keepdims=True) + EPS)
    kn = kf * rk * gk_ref[...]
    s = _dot(qn.astype(BF16), kn.astype(BF16), trans_b=True) * (1.0 / math.sqrt(HEAD_DIM)) + bias_ref[...]
    row = lax.broadcasted_iota(jnp.int32, (QROWS, 2 * BLOCK), 0) & (BLOCK - 1)
    col = lax.broadcasted_iota(jnp.int32, (QROWS, 2 * BLOCK), 1)
    dist = row + BLOCK - col
    valid = (dist >= 0) & (dist < BLOCK) & ((col >= BLOCK) | (n > 0))
    s = jnp.where(valid, s, NEG)
    sink = sink_ref[...]
    m = jnp.maximum(jnp.max(s, axis=-1, keepdims=True), sink)
    p = jnp.exp(s - m)
    es = jnp.exp(sink - m)
    inv = 1.0 / (jnp.sum(p, axis=-1, keepdims=True) + es)
    return qf, rq, qn, kf, rk, kn, p * inv, es * inv


def _attn_specs(nb):
    qspec = pl.BlockSpec((None, None, QROWS, HEAD_DIM), lambda h, n: (h, n, 0, 0))
    kprev = pl.BlockSpec((None, BLOCK, HEAD_DIM), lambda h, n: (h, jnp.maximum(n - 1, 0), 0))
    kcur = pl.BlockSpec((None, BLOCK, HEAD_DIM), lambda h, n: (h, n, 0))
    gain = pl.BlockSpec((1, HEAD_DIM), lambda h, n: (0, 0))
    sink = pl.BlockSpec((None, QROWS, 1), lambda h, n: (h, 0, 0))
    bias = pl.BlockSpec((None, QROWS, 2 * BLOCK), lambda h, n: (h, 0, 0))
    return qspec, kprev, kcur, gain, sink, bias


def _attn_fwd(q4, k3, v3, gq, gk, sink_rows, bias):
    nb = q4.shape[1]
    qspec, kprev, kcur, gain, sink, bspec = _attn_specs(nb)

    def body(q_ref, kp_ref, kc_ref, vp_ref, vc_ref, gq_ref, gk_ref, sink_ref, bias_ref, o_ref):
        n = pl.program_id(1)
        pn = _attn_probs(q_ref, kp_ref, kc_ref, gq_ref, gk_ref, sink_ref, bias_ref, n)[6]
        v = jnp.concatenate([vp_ref[...], vc_ref[...]], axis=0)
        o_ref[...] = _dot(pn.astype(BF16), v).astype(BF16)

    return pl.pallas_call(
        body, name="attn_fwd", grid=(N_KV_HEADS, nb),
        in_specs=[qspec, kprev, kcur, kprev, kcur, gain, gain, sink, bspec],
        out_specs=qspec, out_shape=jax.ShapeDtypeStruct(q4.shape, BF16),
        compiler_params=_params(("parallel", "parallel")),
    )(q4, k3, k3, v3, v3, gq, gk, sink_rows, bias)


def _attn_bwd(q4, k3, v3, do4, gq, gk, sink_rows, bias):
    nb = q4.shape[1]
    t = k3.shape[1]
    qspec, kprev, kcur, gain, sink, bspec = _attn_specs(nb)
    scale = 1.0 / math.sqrt(HEAD_DIM)

    def rms_bwd(dn, xf, r, g):
        w = dn * g
        dx = r * w - xf * (r * r * r) * jnp.mean(xf * w, axis=-1, keepdims=True)
        return dx, jnp.sum(dn * (xf * r), axis=0, keepdims=True)

    def body(q_ref, kp_ref, kc_ref, vp_ref, vc_ref, do_ref, gq_ref, gk_ref, sink_ref, bias_ref,
             dq_ref, dk_ref, dv_ref, dbias_ref, dsink_ref, dgq_ref, dgk_ref):
        n = pl.program_id(1)
        qf, rq, qn, kf, rk, kn, pn, psink = _attn_probs(q_ref, kp_ref, kc_ref, gq_ref, gk_ref, sink_ref, bias_ref, n)
        do = do_ref[...]
        v = jnp.concatenate([vp_ref[...], vc_ref[...]], axis=0)
        dv_win = _dot(pn.astype(BF16), do, trans_a=True)
        dp = _dot(do, v, trans_b=True)
        delta = jnp.sum(pn * dp, axis=-1, keepdims=True)
        ds = pn * (dp - delta)
        dsink = jnp.sum((-psink * delta).reshape(GROUP, BLOCK, 1), axis=1)
        dsc = (ds * scale).astype(BF16)
        dqn = _dot(dsc, kn.astype(BF16))
        dkn = _dot(dsc, qn.astype(BF16), trans_a=True)
        dq, dgq = rms_bwd(dqn, qf, rq, gq_ref[...])
        dk_win, dgk = rms_bwd(dkn, kf, rk, gk_ref[...])
        dq_ref[...] = dq.astype(BF16)

        @pl.when(n == 0)
        def _():
            dbias_ref[...] = ds
            dsink_ref[...] = dsink
            dgq_ref[...] = dgq
            dgk_ref[...] = dgk

        @pl.when(n > 0)
        def _():
            dbias_ref[...] += ds
            dsink_ref[...] += dsink
            dgq_ref[...] += dgq
            dgk_ref[...] += dgk
            prev = pl.ds(pl.multiple_of((n - 1) * BLOCK, BLOCK), BLOCK)
            dk_ref[prev, :] += dk_win[:BLOCK]
            dv_ref[prev, :] += dv_win[:BLOCK]

        cur = pl.ds(pl.multiple_of(n * BLOCK, BLOCK), BLOCK)
        dk_ref[cur, :] = dk_win[BLOCK:]
        dv_ref[cur, :] = dv_win[BLOCK:]

    kv_out = pl.BlockSpec((None, t, HEAD_DIM), lambda h, n: (h, 0, 0))
    gain_out = pl.BlockSpec((None, 1, HEAD_DIM), lambda h, n: (h, 0, 0))
    return pl.pallas_call(
        body, name="attn_bwd", grid=(N_KV_HEADS, nb),
        in_specs=[qspec, kprev, kcur, kprev, kcur, qspec, gain, gain, sink, bspec],
        out_specs=[qspec, kv_out, kv_out, bspec,
                   pl.BlockSpec((None, GROUP, 1), lambda h, n: (h, 0, 0)), gain_out, gain_out],
        out_shape=[jax.ShapeDtypeStruct(q4.shape, BF16),
                   jax.ShapeDtypeStruct((N_KV_HEADS, t, HEAD_DIM), F32),
                   jax.ShapeDtypeStruct((N_KV_HEADS, t, HEAD_DIM), F32),
                   jax.ShapeDtypeStruct((N_KV_HEADS, QROWS, 2 * BLOCK), F32),
                   jax.ShapeDtypeStruct((N_KV_HEADS, GROUP, 1), F32),
                   jax.ShapeDtypeStruct((N_KV_HEADS, 1, HEAD_DIM), F32),
                   jax.ShapeDtypeStruct((N_KV_HEADS, 1, HEAD_DIM), F32)],
        compiler_params=_params(("arbitrary", "arbitrary")),
    )(q4, k3, k3, v3, v3, do4, gq, gk, sink_rows, bias)


def _q_to_heads(cols, t):
    nb = t // BLOCK
    return cols.reshape(nb, BLOCK, N_KV_HEADS, GROUP, HEAD_DIM).transpose(2, 0, 3, 1, 4).reshape(N_KV_HEADS, nb, QROWS, HEAD_DIM)


def _q_from_heads(q4, t):
    nb = t // BLOCK
    return q4.reshape(N_KV_HEADS, nb, GROUP, BLOCK, HEAD_DIM).transpose(1, 3, 0, 2, 4).reshape(t, N_Q_HEADS * HEAD_DIM)


def _kv_to_heads(cols, t):
    return cols.reshape(t, N_KV_HEADS, HEAD_DIM).transpose(1, 0, 2)


def _kv_from_heads(k3, t):
    return k3.transpose(1, 0, 2).reshape(t, N_KV_HEADS * HEAD_DIM)


GATE_TILE = 512


def _merge_fwd(z3, o, p, w_proj, w_o):
    t, d = z3.shape
    tm = min(ROW_TILE, t)
    tn = GATE_TILE

    def body(z_ref, o_ref, gc_ref, ga_ref, wp_ref, wo_ref, m_ref, a_ref, b_ref):
        a = _dot(z_ref[...], wp_ref[...])
        b = _dot(o_ref[...], wo_ref[...])
        m_ref[...] = (_sigmoid(gc_ref[...].astype(F32)) * a + _sigmoid(ga_ref[...].astype(F32)) * b).astype(BF16)
        a_ref[...] = a.astype(BF16)
        b_ref[...] = b.astype(BF16)

    row = pl.BlockSpec((tm, d), lambda i, j: (i, 0))
    wspec = pl.BlockSpec((d, tn), lambda i, j: (0, j))
    ospec = pl.BlockSpec((tm, tn), lambda i, j: (i, j))
    return pl.pallas_call(
        body, name="merge_fwd", grid=(t // tm, d // tn),
        in_specs=[row, row,
                  pl.BlockSpec((tm, tn), lambda i, j: (i, COL_GC // tn + j)),
                  pl.BlockSpec((tm, tn), lambda i, j: (i, COL_GA // tn + j)), wspec, wspec],
        out_specs=[ospec, ospec, ospec],
        out_shape=[jax.ShapeDtypeStruct((t, d), BF16)] * 3,
        compiler_params=_params(("parallel", "parallel")),
    )(z3, o, p, p, w_proj, w_o)


def _merge_bwd(dres, w_out, a, b, p, tokens=()):
    t, d = dres.shape
    tm = min(ROW_TILE, t)
    tn = GATE_TILE

    def epilogue(acc, ex, outs, ids):
        a_ref, b_ref, gc_ref, ga_ref = ex[:4]
        sc = _sigmoid(gc_ref[...].astype(F32))
        sa = _sigmoid(ga_ref[...].astype(F32))
        outs[0][...] = (acc * sc).astype(BF16)
        outs[1][...] = (acc * sa).astype(BF16)
        outs[2][0] = (acc * a_ref[...].astype(F32) * sc * (1.0 - sc)).astype(BF16)
        outs[2][1] = (acc * b_ref[...].astype(F32) * sa * (1.0 - sa)).astype(BF16)

    ospec = pl.BlockSpec((tm, tn), lambda i, j, kk: (i, j))
    return _mm("merge_bwd", (t // tm, d // tn, 1),
               dres, pl.BlockSpec((tm, d), lambda i, j, kk: (i, 0)),
               w_out, pl.BlockSpec((tn, d), lambda i, j, kk: (j, 0)), (tm, tn),
               trans_b=True, a_pre=_to_bf16,
               extras=(a, b, p, p),
               extra_specs=(ospec, ospec,
                            pl.BlockSpec((tm, tn), lambda i, j, kk: (i, COL_GC // tn + j)),
                            pl.BlockSpec((tm, tn), lambda i, j, kk: (i, COL_GA // tn + j))), tokens=tokens,
               out_shape=(jax.ShapeDtypeStruct((t, d), BF16), jax.ShapeDtypeStruct((t, d), BF16),
                          jax.ShapeDtypeStruct((2, t, d), BF16)),
               out_specs=(ospec, ospec, pl.BlockSpec((2, tm, tn), lambda i, j, kk: (0, i, j))),
               epilogue=epilogue)


def _store_epilogue(acc, ex, outs, ids):
    outs[0][...] = acc


def _store_bf16_epilogue(acc, ex, outs, ids):
    outs[0][...] = acc.astype(BF16)


def _mm_nt(name, a, w, out_dtype=BF16):
    t, n = a.shape
    k = w.shape[0]
    tm = min(ROW_TILE, t)
    return _mm(name, (t // tm, 1, 1), a, pl.BlockSpec((tm, n), lambda i, j, kk: (i, 0)),
               w, pl.BlockSpec((k, n), lambda i, j, kk: (0, 0)), (tm, k), trans_b=True,
               out_shape=(jax.ShapeDtypeStruct((t, k), out_dtype),),
               out_specs=(pl.BlockSpec((tm, k), lambda i, j, kk: (i, 0)),),
               epilogue=_store_bf16_epilogue if out_dtype == BF16 else _store_epilogue)[0]


def _mm_tn(name, a, b, b_pre=None):
    t, m = a.shape
    n = b.shape[1]
    tk = min(ROW_TILE, t)
    return _mm(name, (1, 1, t // tk), a, pl.BlockSpec((tk, m), lambda i, j, kk: (kk, 0)),
               b, pl.BlockSpec((tk, n), lambda i, j, kk: (kk, 0)), (m, n), trans_a=True, b_pre=b_pre,
               out_shape=(jax.ShapeDtypeStruct((m, n), F32),),
               out_specs=(pl.BlockSpec((m, n), lambda i, j, kk: (0, 0)),), epilogue=_store_epilogue)[0]


def _local_step(x, target, small, comm):
    t = x.shape[0]
    w = dict(small)

    w.update(comm.weights("A", None))
    x1, ffn1_saved = _ffn_fwd("ffn1", x, w["ffn1_norm"], w["ffn1_w_in"], w["ffn1_w_out"], comm.tokens)
    w.update(comm.weights("B", x1))
    hm = _rmsnorm_fwd("mix_norm", x1, w["mix_norm"], comm.tokens)
    tm = min(ROW_TILE, t)
    p = _mm("mix_in", (N_CHIPS, t // tm, 1),
            hm, pl.BlockSpec((tm, D_MODEL), lambda j, i, kk: (i, 0)),
            w["w_in"], pl.BlockSpec((None, D_MODEL, SHARD_W), lambda j, i, kk: (j, 0, 0)), (tm, SHARD_W),
            out_shape=(jax.ShapeDtypeStruct((t, IN_W), BF16),),
            out_specs=(pl.BlockSpec((tm, SHARD_W), lambda j, i, kk: (i, j)),),
            epilogue=_store_bf16_epilogue)[0]

    z3 = _conv_fwd(p, w["conv_dw_kernel"], w["conv_dw_bias"], w["conv_ln_g"], w["conv_ln_b"])

    onehot = _bucket_onehot()
    bias = _bias_table(w["rel_bias"].T, onehot).reshape(N_KV_HEADS, QROWS, 2 * BLOCK)
    sink_rows = jnp.repeat(w["attn_sinks"].reshape(N_KV_HEADS, GROUP), BLOCK, axis=1)[..., None]
    q4 = _q_to_heads(p[:, COL_Q:COL_K], t)
    k3 = _kv_to_heads(p[:, COL_K:COL_V], t)
    v3 = _kv_to_heads(p[:, COL_V:COL_GC], t)
    o4 = _attn_fwd(q4, k3, v3, w["q_norm"], w["k_norm"], sink_rows, bias)
    o = _q_from_heads(o4, t)

    merged, a, b = _merge_fwd(z3, o, p, w["conv_w_proj"], w["attn_w_o"])
    x2 = _mm_residual("mix_out", merged, w["w_out"], x1, 1.0)
    w.update(comm.weights("C", x2))
    x3, ffn2_saved = _ffn_fwd("ffn2", x2, w["ffn2_norm"], w["ffn2_w_in"], w["ffn2_w_out"])
    dy, loss = _loss_head(x3, target)

    g, big = {}, {}
    dres2, big["ffn2_w_in"], big["ffn2_w_out"], g["ffn2_norm"] = _ffn_bwd(
        "ffn2b", dy, x2, w["ffn2_norm"], ffn2_saved, w["ffn2_w_in"], w["ffn2_w_out"])
    tokens = comm.reduce_start("R1", big)

    da, db, dgates = _merge_bwd(dres2, w["w_out"], a, b, p, tokens)
    big = {}
    big["w_out"] = _mm_tn("d_w_out", merged, dres2, b_pre=_to_bf16)
    big["conv_w_proj"] = _mm_tn("d_w_proj", z3, da)
    big["attn_w_o"] = _mm_tn("d_w_o", o, db)
    dz3 = _mm_nt("d_z3", da, w["conv_w_proj"])
    do = _mm_nt("d_o", db, w["attn_w_o"])

    dq4, dk3, dv3, dbias, dsink, dgq, dgk = _attn_bwd(q4, k3, v3, _q_to_heads(do, t), w["q_norm"], w["k_norm"], sink_rows, bias)
    g["rel_bias"] = _bias_table_bwd(dbias.reshape(N_Q_HEADS, BLOCK * 2 * BLOCK), onehot).T
    g["attn_sinks"] = dsink.reshape(N_Q_HEADS)
    g["q_norm"] = jnp.sum(dgq, axis=0)
    g["k_norm"] = jnp.sum(dgk, axis=0)

    dz1, big["conv_dw_kernel"], g["conv_dw_bias"], g["conv_ln_g"], g["conv_ln_b"] = _conv_bwd_ln(
        p, dz3, w["conv_dw_kernel"], w["conv_dw_bias"], w["conv_ln_g"], w["conv_ln_b"])
    dconv = _conv_bwd_glu(p, dz1, w["conv_dw_kernel"])

    dp = jnp.concatenate([dconv, _q_from_heads(dq4, t), _kv_from_heads(dk3, t).astype(BF16),
                          _kv_from_heads(dv3, t).astype(BF16), dgates[0], dgates[1]], axis=1)
    big["w_in"] = _mm("d_w_in", (1, N_CHIPS, t // tm),
                    hm, pl.BlockSpec((tm, D_MODEL), lambda i, j, kk: (kk, 0)),
                    dp, pl.BlockSpec((tm, SHARD_W), lambda i, j, kk: (kk, j)), (D_MODEL, SHARD_W),
                    trans_a=True,
                    out_shape=(jax.ShapeDtypeStruct((N_CHIPS, D_MODEL, SHARD_W), F32),),
                    out_specs=(pl.BlockSpec((None, D_MODEL, SHARD_W), lambda i, j, kk: (j, 0, 0)),),
                    epilogue=_store_epilogue)[0]
    dres1, g["mix_norm"] = _mm("d_mix", (t // tm, 1, N_CHIPS),
                               dp, pl.BlockSpec((tm, SHARD_W), lambda i, j, kk: (i, kk)),
                               w["w_in"], pl.BlockSpec((None, D_MODEL, SHARD_W), lambda i, j, kk: (kk, 0, 0)),
                               (tm, D_MODEL), trans_b=True,
                               extras=(x1, w["mix_norm"], dres2),
                               extra_specs=(pl.BlockSpec((tm, D_MODEL), lambda i, j, kk: (i, 0)),
                                            pl.BlockSpec((1, D_MODEL), lambda i, j, kk: (0, 0)),
                                            pl.BlockSpec((tm, D_MODEL), lambda i, j, kk: (i, 0))),
                               out_shape=(jax.ShapeDtypeStruct((t, D_MODEL), F32), jax.ShapeDtypeStruct((1, D_MODEL), F32)),
                               out_specs=(pl.BlockSpec((tm, D_MODEL), lambda i, j, kk: (i, 0)),
                                          pl.BlockSpec((1, D_MODEL), lambda i, j, kk: (0, 0))),
                               epilogue=_rms_bwd_epilogue, sem=("arbitrary", "arbitrary", "arbitrary"))

    comm.reduce_finish("R1", dres1)
    tokens = comm.reduce_start("R2", big)

    def ffn1_grads(dw_in4, dw_out):
        comm.reduce_finish("R2", dw_in4)
        return comm.reduce_start("R3", {"ffn1_w_in": dw_in4, "ffn1_w_out": dw_out})

    grad_x, _, _, g["ffn1_norm"] = _ffn_bwd(
        "ffn1b", dres1, x, w["ffn1_norm"], ffn1_saved, w["ffn1_w_in"], w["ffn1_w_out"], tokens, ffn1_grads)
    comm.reduce_finish("R3", grad_x)
    return loss[0, 0], grad_x, g


def _mesh_place():
    x, y, c = lax.axis_index("x"), lax.axis_index("y"), lax.axis_index("c")
    chips = [(1 - x, y), (x, 1 - y), (1 - x, 1 - y)]
    return x, y, c, chips


def _any_specs(n):
    return [pl.BlockSpec(memory_space=pl.ANY)] * n


HBM_SPEC = pl.BlockSpec(memory_space=pltpu.HBM)
SEM_SPEC = pl.BlockSpec(memory_space=pltpu.SEMAPHORE)
EFFECT = pltpu.SideEffectType.DATAFLOW_SIDE_EFFECTING


def _in_hbm(a):
    return pltpu.with_memory_space_constraint(a, pltpu.HBM)


def _copy_start(name, srcs, lands, plan, after=()):
    ns, nb = len(srcs), len(lands)
    n = 3 * ns

    def body(*refs):
        s_refs, l_refs = refs[:ns], refs[ns:ns + nb]
        send_sems, recv_sems = refs[ns + nb + len(after)], refs[ns + nb + len(after) + 1]
        token = refs[-1]
        for k, (src, dst, to, _) in enumerate(plan(s_refs, l_refs)):
            pltpu.make_async_remote_copy(src_ref=src, dst_ref=dst, send_sem=send_sems.at[k], recv_sem=recv_sems.at[k],
                                         device_id=to, device_id_type=MESH).start()
        token[...] = jnp.zeros_like(token)

    bufs = list(srcs) + list(lands)
    outs = pl.pallas_call(
        body, name=name,
        out_shape=(pltpu.SemaphoreType.DMA((n,)), pltpu.SemaphoreType.DMA((n,)),
                   *[pltpu.HBM(a.shape, a.dtype) for a in bufs], jax.ShapeDtypeStruct((8, LANES), F32)),
        in_specs=[HBM_SPEC] * len(bufs) + [pl.BlockSpec(memory_space=pl.ANY)] * len(after),
        out_specs=(SEM_SPEC, SEM_SPEC, *[HBM_SPEC] * len(bufs), pl.BlockSpec(memory_space=pltpu.VMEM)),
        input_output_aliases={i: 2 + i for i in range(len(bufs))},
        compiler_params=pltpu.CompilerParams(has_side_effects=EFFECT),
    )(*[_in_hbm(a) for a in bufs], *after)
    return outs[0], outs[1], list(outs[2:2 + ns]), list(outs[2 + ns:2 + ns + nb]), outs[-1]


def _copy_wait(name, send_sems, recv_sems, srcs, lands, after, plan):
    ns, nb = len(srcs), len(lands)

    def body(*refs):
        s_refs, l_refs = refs[:ns], refs[ns:ns + nb]
        send_sems, recv_sems = refs[ns + nb], refs[ns + nb + 1]
        for k, (src, _, to, mine) in enumerate(plan(s_refs, l_refs)):
            cp = pltpu.make_async_remote_copy(src_ref=src, dst_ref=mine, send_sem=send_sems.at[k], recv_sem=recv_sems.at[k],
                                              device_id=to, device_id_type=MESH)
            cp.wait_send()
            cp.wait_recv()

    bufs = list(srcs) + list(lands)
    outs = pl.pallas_call(
        body, name=name,
        out_shape=tuple(pltpu.HBM(a.shape, a.dtype) for a in bufs),
        in_specs=[HBM_SPEC] * len(bufs) + [SEM_SPEC, SEM_SPEC, pl.BlockSpec(memory_space=pl.ANY)],
        out_specs=tuple([HBM_SPEC] * len(bufs)),
        input_output_aliases={i: i for i in range(len(bufs))},
        compiler_params=pltpu.CompilerParams(has_side_effects=EFFECT),
    )(*bufs, send_sems, recv_sems, after)
    return list(outs[:ns]), list(outs[ns:])


def _gather_plan(s_refs, l_refs):
    x, y, c, chips = _mesh_place()
    jme = 2 * x + y
    return [(s.at[c], land.at[jme, c], (*chip, c), land.at[2 * chip[0] + chip[1], c])
            for s, land in zip(s_refs, l_refs) for chip in chips]


def _scatter_plan(s_refs, l_refs):
    x, y, c, chips = _mesh_place()
    return [(s.at[2 * chip[0] + chip[1]], land.at[k], (*chip, c), land.at[k])
            for s, land in zip(s_refs, l_refs) for k, chip in enumerate(chips)]


def _gather_forward(name, shards, landed):
    nw = len(shards)

    def body(*refs):
        s_refs, o_refs = refs[:nw], refs[2 * nw:3 * nw]
        send_sems, recv_sems = refs[3 * nw:]
        x, y, c, chips = _mesh_place()
        me, sib, jme = (x, y, c), (x, y, 1 - c), 2 * x + y
        sent = []
        for w in range(nw):
            parts = [(o_refs[w].at[2 * chip[0] + chip[1], c], o_refs[w].at[2 * chip[0] + chip[1], c]) for chip in chips]
            parts.append((s_refs[w], o_refs[w].at[jme]))
            for k, (src, dst) in enumerate(parts):
                cp = pltpu.make_async_remote_copy(src_ref=src, dst_ref=dst, send_sem=send_sems.at[4 * w + k],
                                                  recv_sem=recv_sems.at[4 * w + k], device_id=sib, device_id_type=MESH)
                cp.start()
                sent.append(cp)
        for w in range(nw):
            parts = [o_refs[w].at[2 * chip[0] + chip[1], 1 - c] for chip in chips] + [o_refs[w].at[jme]]
            for k, part in enumerate(parts):
                pltpu.make_async_remote_copy(src_ref=part, dst_ref=part, send_sem=send_sems.at[4 * w + k],
                                             recv_sem=recv_sems.at[4 * w + k], device_id=me, device_id_type=MESH).wait_recv()
        for cp in sent:
            cp.wait_send()

    return pl.pallas_call(
        body, name=name,
        in_specs=_any_specs(2 * nw), out_specs=_any_specs(nw),
        out_shape=[jax.ShapeDtypeStruct(a.shape, a.dtype) for a in landed],
        input_output_aliases={nw + i: i for i in range(nw)},
        scratch_shapes=[pltpu.SemaphoreType.DMA((4 * nw,)), pltpu.SemaphoreType.DMA((4 * nw,))],
    )(*shards, *landed)


def _exchange_halves(name, grads, after=()):
    nw = len(grads)

    def body(*refs):
        g_refs, o_refs = refs[:nw], refs[nw + len(after):2 * nw + len(after)]
        send_sems, recv_sems = refs[2 * nw + len(after):]
        x, y, c, _ = _mesh_place()
        copies = []
        for w in range(nw):
            cp = pltpu.make_async_remote_copy(src_ref=g_refs[w].at[:, 1 - c], dst_ref=o_refs[w], send_sem=send_sems.at[w],
                                              recv_sem=recv_sems.at[w], device_id=(x, y, 1 - c), device_id_type=MESH)
            cp.start()
            copies.append(cp)
        for cp in copies:
            cp.wait()

    return pl.pallas_call(
        body, name=name,
        in_specs=_any_specs(nw + len(after)), out_specs=_any_specs(nw),
        out_shape=[jax.ShapeDtypeStruct((N_CHIPS,) + g.shape[2:], F32) for g in grads],
        scratch_shapes=[pltpu.SemaphoreType.DMA((nw,)), pltpu.SemaphoreType.DMA((nw,))],
    )(*grads, *after)


def _row_tile(r):
    for cand in (256, 176, 128, 64, 32, 16, 8):
        if r % cand == 0:
            return cand
    return r


def _add_own_half(c_idx, grad, got):
    _, _, r, cols = grad.shape
    tr = _row_tile(r)

    def body(c_ref, g_ref, o_ref, out_ref):
        out_ref[...] = (g_ref[...] + o_ref[...]).astype(BF16)

    return pl.pallas_call(
        body, name="add_own_half",
        grid_spec=pltpu.PrefetchScalarGridSpec(
            num_scalar_prefetch=1, grid=(N_CHIPS, r // tr),
            in_specs=[pl.BlockSpec((None, None, tr, cols), lambda j, i, c_ref: (j, c_ref[0], i, 0)),
                      pl.BlockSpec((None, tr, cols), lambda j, i, c_ref: (j, i, 0))],
            out_specs=pl.BlockSpec((None, tr, cols), lambda j, i, c_ref: (j, i, 0))),
        out_shape=jax.ShapeDtypeStruct((N_CHIPS, r, cols), BF16),
        compiler_params=_params(("parallel", "parallel")),
    )(c_idx, grad, got)


def _sum_pieces(place_idx, sums, landed):
    _, r, cols = sums.shape
    tr = _row_tile(r)

    def body(j_ref, own_ref, p_ref, o_ref):
        o_ref[...] = ((own_ref[...].astype(F32) + p_ref[0].astype(F32)) + p_ref[1].astype(F32)) + p_ref[2].astype(F32)

    return pl.pallas_call(
        body, name="sum_pieces",
        grid_spec=pltpu.PrefetchScalarGridSpec(
            num_scalar_prefetch=1, grid=(r // tr,),
            in_specs=[pl.BlockSpec((None, tr, cols), lambda i, j_ref: (j_ref[0], i, 0)),
                      pl.BlockSpec((N_CHIPS - 1, tr, cols), lambda i, j_ref: (0, i, 0))],
            out_specs=pl.BlockSpec((None, tr, cols), lambda i, j_ref: (j_ref[1], i, 0))),
        out_shape=jax.ShapeDtypeStruct((2, r, cols), F32),
        compiler_params=_params(("parallel",)),
    )(place_idx, sums, landed)


def _join_halves(name, halves):
    nw = len(halves)

    def body(*refs):
        o_refs = refs[nw:2 * nw]
        send_sems, recv_sems = refs[2 * nw:]
        x, y, c, _ = _mesh_place()
        copies = []
        for w in range(nw):
            cp = pltpu.make_async_remote_copy(src_ref=o_refs[w].at[c], dst_ref=o_refs[w].at[c], send_sem=send_sems.at[w],
                                              recv_sem=recv_sems.at[w], device_id=(x, y, 1 - c), device_id_type=MESH)
            cp.start()
            copies.append(cp)
        for w in range(nw):
            copies[w].wait_send()
            landed = o_refs[w].at[1 - c]
            pltpu.make_async_remote_copy(src_ref=landed, dst_ref=landed, send_sem=send_sems.at[w], recv_sem=recv_sems.at[w],
                                         device_id=(x, y, c), device_id_type=MESH).wait_recv()

    return pl.pallas_call(
        body, name=name,
        in_specs=_any_specs(nw), out_specs=_any_specs(nw),
        out_shape=[jax.ShapeDtypeStruct(h.shape, F32) for h in halves],
        input_output_aliases={i: i for i in range(nw)},
        scratch_shapes=[pltpu.SemaphoreType.DMA((nw,)), pltpu.SemaphoreType.DMA((nw,))],
    )(*halves)


SMALL_ROWS = 8


def _all_reduce_small(pack):
    rows, cols = pack.shape
    n_dev = 8

    def body(p_ref, o_ref, slots, send_sems, recv_sems):
        x, y, c, _ = _mesh_place()
        me = 4 * x + 2 * y + c
        slots[me] = p_ref[...]
        copies = []
        for k in range(1, n_dev):
            peer = (me + k) % n_dev
            cp = pltpu.make_async_remote_copy(src_ref=p_ref, dst_ref=slots.at[me], send_sem=send_sems.at[k],
                                              recv_sem=recv_sems.at[k],
                                              device_id=(peer // 4, (peer // 2) % 2, peer % 2), device_id_type=MESH)
            cp.start()
            copies.append(cp)
        for k in range(1, n_dev):
            src = (me + n_dev - k) % n_dev
            pltpu.make_async_remote_copy(src_ref=p_ref, dst_ref=slots.at[src], send_sem=send_sems.at[k],
                                         recv_sem=recv_sems.at[k], device_id=(x, y, c), device_id_type=MESH).wait_recv()
        for cp in copies:
            cp.wait_send()
        total = slots[0]
        for s in range(1, n_dev):
            total = total + slots[s]
        o_ref[...] = total

    return pl.pallas_call(
        body, name="all_reduce_small",
        in_specs=[pl.BlockSpec(memory_space=pltpu.VMEM)], out_specs=pl.BlockSpec(memory_space=pltpu.VMEM),
        out_shape=jax.ShapeDtypeStruct((rows, cols), F32),
        scratch_shapes=[pltpu.VMEM((n_dev, rows, cols), F32), pltpu.SemaphoreType.DMA((n_dev,)),
                        pltpu.SemaphoreType.DMA((n_dev,))],
    )(pack)


def _adamw(name, w, g, m, v):
    r, cols = w.shape
    tr = _row_tile(r)

    def body(w_ref, g_ref, m_ref, v_ref, d_ref, nm_ref, nv_ref):
        gv = g_ref[...]
        nm = ADAM_B1 * m_ref[...] + (1.0 - ADAM_B1) * gv
        nv = ADAM_B2 * v_ref[...] + (1.0 - ADAM_B2) * (gv * gv)
        m_hat = nm / (1.0 - ADAM_B1 ** ADAM_STEP)
        v_hat = nv / (1.0 - ADAM_B2 ** ADAM_STEP)
        d_ref[...] = -ADAM_LR * (m_hat / (jnp.sqrt(v_hat) + ADAM_EPS) + ADAM_WD * w_ref[...])
        nm_ref[...] = nm
        nv_ref[...] = nv

    spec = pl.BlockSpec((tr, cols), lambda i: (i, 0))
    return pl.pallas_call(
        body, name=name, grid=(r // tr,),
        in_specs=[spec] * 4, out_specs=[spec] * 3,
        out_shape=[jax.ShapeDtypeStruct((r, cols), F32)] * 3,
        compiler_params=_params(("parallel",)),
    )(w, g, m, v)


BIG = ["ffn1_w_in", "ffn1_w_out", "w_in", "conv_w_proj", "attn_w_o", "w_out", "ffn2_w_in", "ffn2_w_out", "conv_dw_kernel"]
COL_SHARDED = ("ffn1_w_in", "w_in", "ffn2_w_in")
SMALL = ["ffn1_norm", "mix_norm", "ffn2_norm", "conv_dw_bias", "conv_ln_g", "conv_ln_b", "q_norm", "k_norm", "attn_sinks", "rel_bias"]
WEIGHTS = ["ffn1_norm", "ffn1_w_in", "ffn1_w_out", "mix_norm", "w_in", "conv_dw_kernel", "conv_dw_bias", "conv_ln_g",
           "conv_ln_b", "conv_w_proj", "q_norm", "k_norm", "attn_sinks", "rel_bias", "attn_w_o", "w_out", "ffn2_norm",
           "ffn2_w_in", "ffn2_w_out"]
SMALL_PLACE = {"ffn1_norm": (0, 0, 1024), "mix_norm": (1, 0, 1024), "ffn2_norm": (2, 0, 1024), "conv_dw_bias": (3, 0, 1024),
               "conv_ln_g": (4, 0, 1024), "conv_ln_b": (5, 0, 1024), "q_norm": (6, 0, 64), "k_norm": (6, 128, 64),
               "attn_sinks": (6, 256, 16), "rel_bias": (7, 0, 512)}
LOSS_PLACE = (6, 384)


def _pack_small(vals, fill=0.0, loss=None):
    pack = jnp.full((SMALL_ROWS, D_MODEL), fill, F32)
    for name, (row, lane, n) in SMALL_PLACE.items():
        pack = pack.at[row, lane:lane + n].set(vals[name].reshape(n))
    if loss is not None:
        pack = pack.at[LOSS_PLACE[0], LOSS_PLACE[1]].set(loss)
    return pack


def _unpack_small(pack, shapes):
    return {name: pack[row, lane:lane + n].reshape(shapes[name]) for name, (row, lane, n) in SMALL_PLACE.items()}


def _shard_halves(name, a):
    if name == "conv_dw_kernel":
        a = jnp.pad(a, ((0, CONV_PAD - CONV_WIDTH), (0, 0)))
    r, cols = a.shape
    return a.reshape(2, r // 2, cols)


GATHER_GROUPS = {"A": ["ffn1_w_in", "ffn1_w_out"],
                 "B": ["w_in", "conv_dw_kernel", "conv_w_proj", "attn_w_o", "w_out"],
                 "C": ["ffn2_w_in", "ffn2_w_out"]}


class _MeshComm:
    def __init__(self, wts):
        self.c_idx = lax.axis_index("c").astype(jnp.int32).reshape(1)
        self.place_idx = jnp.stack([2 * lax.axis_index("x") + lax.axis_index("y"), lax.axis_index("c")]).astype(jnp.int32)
        self.wts, self.gathers, self.reductions, self.reduced = wts, {}, {}, {}
        self.tokens, self.last_join = (), ()
        self._gather_start("A", ())

    def _gather_start(self, group, after):
        names = GATHER_GROUPS[group]
        shards = [_shard_halves(n, self.wts[n]) if n == "conv_dw_kernel" else _shard_halves(n, self.wts[n]).astype(BF16)
                  for n in names]
        lands = [lax.empty((N_CHIPS,) + s.shape, s.dtype) for s in shards]
        self.gathers[group] = _copy_start("gather_start_" + group, shards, lands, _gather_plan, after=after)
        self.tokens = (self.gathers[group][-1],)

    def weights(self, group, after):
        send_sems, recv_sems, shards, lands, token = self.gathers.pop(group)
        shards, lands = _copy_wait("gather_wait_" + group, send_sems, recv_sems, shards, lands,
                                   token if after is None else after, _gather_plan)
        gathered = _gather_forward("gather_forward_" + group, shards, lands)
        self.tokens = ()
        following = {"A": "B", "B": "C"}.get(group)
        if following:
            self._gather_start(following, (gathered[0],))
        out = {}
        for n, g4 in zip(GATHER_GROUPS[group], gathered):
            r, cols = g4.shape[2] * 2, g4.shape[3]
            if n in COL_SHARDED:
                out[n] = g4.reshape(N_CHIPS, r, cols)
            elif n == "conv_dw_kernel":
                out[n] = g4.reshape(N_CHIPS, r, cols).transpose(1, 0, 2).reshape(r, N_CHIPS * cols)
            else:
                out[n] = g4.reshape(N_CHIPS * r, cols)
        return out

    def reduce_start(self, group, grads):
        names = list(grads)
        g4 = []
        for n in names:
            a = grads[n]
            if n == "conv_dw_kernel":
                a = a.reshape(CONV_PAD, N_CHIPS, -1).transpose(1, 0, 2)
            elif n not in COL_SHARDED:
                a = a.reshape(N_CHIPS, a.shape[0] // N_CHIPS, a.shape[1])
            g4.append(a.reshape(N_CHIPS, 2, a.shape[1] // 2, a.shape[2]))
        got = _exchange_halves("exchange_halves_" + group, g4, after=self.last_join)
        sums = [_add_own_half(self.c_idx, a, b) for a, b in zip(g4, got)]
        lands = [lax.empty((N_CHIPS - 1,) + s.shape[1:], s.dtype) for s in sums]
        started = _copy_start("scatter_start_" + group, sums, lands, _scatter_plan, after=self.last_join)
        self.reductions[group] = (names,) + started
        return (started[-1],)

    def reduce_finish(self, group, after):
        names, send_sems, recv_sems, sums, lands, _ = self.reductions.pop(group)
        sums, lands = _copy_wait("scatter_wait_" + group, send_sems, recv_sems, sums, lands, after, _scatter_plan)
        halves = [_sum_pieces(self.place_idx, s, p) for s, p in zip(sums, lands)]
        joined = _join_halves("join_halves_" + group, halves)
        self.last_join = (joined[0],)
        self.reduced.update(zip(names, joined))


def kernel(x, ffn1_norm, ffn1_w_in, ffn1_w_out, mix_norm, w_in, conv_dw_kernel, conv_dw_bias, conv_ln_g, conv_ln_b, conv_w_proj, q_norm, k_norm, attn_sinks, rel_bias, attn_w_o, w_out, ffn2_norm, ffn2_w_in, ffn2_w_out, loss_target, m_ffn1_norm, m_ffn1_w_in, m_ffn1_w_out, m_mix_norm, m_w_in, m_conv_dw_kernel, m_conv_dw_bias, m_conv_ln_g, m_conv_ln_b, m_conv_w_proj, m_q_norm, m_k_norm, m_attn_sinks, m_rel_bias, m_attn_w_o, m_w_out, m_ffn2_norm, m_ffn2_w_in, m_ffn2_w_out, v_ffn1_norm, v_ffn1_w_in, v_ffn1_w_out, v_mix_norm, v_w_in, v_conv_dw_kernel, v_conv_dw_bias, v_conv_ln_g, v_conv_ln_b, v_conv_w_proj, v_q_norm, v_k_norm, v_attn_sinks, v_rel_bias, v_attn_w_o, v_w_out, v_ffn2_norm, v_ffn2_w_in, v_ffn2_w_out):
    args = dict(locals())
    wts = {n: args[n] for n in WEIGHTS}
    mom = {n: args["m_" + n] for n in WEIGHTS}
    var = {n: args["v_" + n] for n in WEIGHTS}
    comm = _MeshComm(wts)
    small = {n: wts[n] if n in ("attn_sinks", "rel_bias") else wts[n].reshape(1, -1) for n in SMALL}
    loss_part, grad_x, g = _local_step(x[0], loss_target[0], small, comm)

    small_sum = _all_reduce_small(_pack_small(g, loss=loss_part))
    loss = small_sum[LOSS_PLACE[0], LOSS_PLACE[1]]
    small_shapes = {n: wts[n].shape for n in SMALL}
    g_small = _unpack_small(small_sum, small_shapes)

    grads, delta, new_m, new_v = {}, {}, {}, {}
    for n in BIG:
        j = comm.reduced[n]
        gs = j.reshape(j.shape[1] * 2, j.shape[2])
        pad = n == "conv_dw_kernel"
        ws, ms, vs = (_shard_halves(n, a).reshape(gs.shape) for a in (wts[n], mom[n], var[n]))
        d, nm, nv = _adamw("adamw_" + n, ws, gs, ms, vs)
        cut = (lambda a: a[:CONV_WIDTH]) if pad else (lambda a: a)
        grads[n], delta[n], new_m[n], new_v[n] = cut(gs), cut(d), cut(nm), cut(nv)
    d, nm, nv = _adamw("adamw_small", _pack_small(wts), small_sum, _pack_small(mom), _pack_small(var, fill=1.0))
    grads.update(g_small)
    delta.update(_unpack_small(d, small_shapes))
    new_m.update(_unpack_small(nm, small_shapes))
    new_v.update(_unpack_small(nv, small_shapes))

    return (loss, grad_x[None], *[grads[n] for n in WEIGHTS], *[delta[n] for n in WEIGHTS],
            *[new_m[n] for n in WEIGHTS], *[new_v[n] for n in WEIGHTS])
```

```python
import functools
import math

import jax
import jax.numpy as jnp
from jax import lax
from jax.experimental import pallas as pl
from jax.experimental.pallas import tpu as pltpu

F32 = jnp.float32
BF16 = jnp.bfloat16
MESH = pl.DeviceIdType.MESH

EPS = 1e-6
D_MODEL = 1024
D_FF = 2816
N_CHIPS = 4
SHARD_W = 2 * D_FF // N_CHIPS
HEAD_DIM = 64
N_Q_HEADS = 16
N_KV_HEADS = 4
GROUP = N_Q_HEADS // N_KV_HEADS
BLOCK = 128
QROWS = GROUP * BLOCK
N_BUCKETS = 32
MAX_DISTANCE = 128
CONV_WIDTH = 31
CONV_PAD = 32
NEG = float(jnp.finfo(jnp.float32).min)

ADAM_LR = 0.001
ADAM_B1 = 0.9
ADAM_B2 = 0.999
ADAM_EPS = 1e-08
ADAM_WD = 0.01
ADAM_STEP = 10

VMEM_LIMIT_BYTES = 56 * 1024 * 1024
ROW_TILE = 512
TOKEN_TILE = 1024
CONV_TILE = 256
CONV_ROWS = 128
LANES = 128

COL_CONV_A, COL_CONV_G, COL_Q, COL_K, COL_V, COL_GC, COL_GA = 0, 1024, 2048, 3072, 3328, 3584, 4608
IN_W = 5632


def _params(sem, vmem=VMEM_LIMIT_BYTES):
    return pltpu.CompilerParams(dimension_semantics=sem, vmem_limit_bytes=vmem)


def _sigmoid(x):
    return 1.0 / (1.0 + jnp.exp(-x))


def _dot(a, b, trans_a=False, trans_b=False, precision=None):
    dn = (((0,) if trans_a else (1,), (1,) if trans_b else (0,)), ((), ()))
    return lax.dot_general(a, b, dn, preferred_element_type=F32, precision=precision)


def _mm(name, grid, a, a_spec, b, b_spec, acc_shape, *, trans_a=False, trans_b=False, a_pre=None, b_pre=None,
        extras=(), extra_specs=(), tokens=(), out_shape, out_specs, epilogue, sem=("parallel", "parallel", "arbitrary")):
    n_k = grid[2]
    extras = tuple(extras) + tuple(tokens)
    extra_specs = tuple(extra_specs) + (pl.BlockSpec((8, LANES), lambda i, j, kk: (0, 0)),) * len(tokens)
    n_extra = len(extras)
    n_out = len(out_shape)

    def body(a_ref, b_ref, *rest):
        ex = rest[:n_extra]
        outs = rest[n_extra:n_extra + n_out]
        ids = (pl.program_id(0), pl.program_id(1), pl.program_id(2))
        av = a_ref[...]
        bv = b_ref[...]
        if a_pre is not None:
            av = a_pre(av)
        if b_pre is not None:
            bv = b_pre(bv)
        part = _dot(av, bv, trans_a, trans_b)
        if n_k == 1:
            epilogue(part, ex, outs, ids)
        else:
            acc = rest[-1]

            @pl.when(ids[2] == 0)
            def _():
                acc[...] = part

            @pl.when(ids[2] > 0)
            def _():
                acc[...] += part

            @pl.when(ids[2] == n_k - 1)
            def _():
                epilogue(acc[...], ex, outs, ids)

    scratch = [] if n_k == 1 else [pltpu.VMEM(acc_shape, F32)]
    return pl.pallas_call(
        body, name=name, grid=grid,
        in_specs=[a_spec, b_spec, *extra_specs],
        out_specs=list(out_specs), out_shape=list(out_shape),
        scratch_shapes=scratch, compiler_params=_params(sem),
    )(a, b, *extras)


def _half_bf16(v):
    return (0.5 * v).astype(BF16)


def _to_bf16(v):
    return v.astype(BF16)


def _rmsnorm_fwd(name, x, g, tokens=()):
    t, d = x.shape
    tm = min(ROW_TILE, t)

    def body(x_ref, g_ref, *rest):
        o_ref = rest[-1]
        xv = x_ref[...]
        r = lax.rsqrt(jnp.mean(xv * xv, axis=-1, keepdims=True) + EPS)
        o_ref[...] = (xv * r * g_ref[...]).astype(BF16)

    return pl.pallas_call(
        body, name=name, grid=(t // tm,),
        in_specs=[pl.BlockSpec((tm, d), lambda i: (i, 0)), pl.BlockSpec((1, d), lambda i: (0, 0))]
        + [pl.BlockSpec((8, LANES), lambda i: (0, 0))] * len(tokens),
        out_specs=pl.BlockSpec((tm, d), lambda i: (i, 0)),
        out_shape=jax.ShapeDtypeStruct((t, d), BF16),
        compiler_params=_params(("parallel",)),
    )(x, g, *tokens)


def _rms_bwd_epilogue(acc, ex, outs, ids):
    x_ref, g_ref, dres_ref = ex[:3]
    out_ref, dg_ref = outs
    xv = x_ref[...]
    r = lax.rsqrt(jnp.mean(xv * xv, axis=-1, keepdims=True) + EPS)
    w = acc * g_ref[...]
    dx = r * w - xv * (r * r * r) * jnp.mean(xv * w, axis=-1, keepdims=True)
    out_ref[...] = dres_ref[...] + dx
    part = jnp.sum(acc * (xv * r), axis=0, keepdims=True)

    @pl.when(ids[0] == 0)
    def _():
        dg_ref[...] = part

    @pl.when(ids[0] > 0)
    def _():
        dg_ref[...] += part


def _ffn_in(name, n, w_in4):
    t, d = n.shape
    tm = min(ROW_TILE, t)

    def body(n_ref, wa_ref, wb_ref, ab_ref, h_ref):
        nv = n_ref[...]
        a = _dot(nv, wa_ref[...])
        b = _dot(nv, wb_ref[...])
        h_ref[...] = (a * _sigmoid(a) * b).astype(BF16)
        ab_ref[0] = a.astype(BF16)
        ab_ref[1] = b.astype(BF16)

    return pl.pallas_call(
        body, name=name, grid=(2, t // tm),
        in_specs=[pl.BlockSpec((tm, d), lambda j, i: (i, 0)),
                  pl.BlockSpec((None, d, SHARD_W), lambda j, i: (j, 0, 0)),
                  pl.BlockSpec((None, d, SHARD_W), lambda j, i: (j + 2, 0, 0))],
        out_specs=[pl.BlockSpec((2, tm, SHARD_W), lambda j, i: (0, i, j)),
                   pl.BlockSpec((tm, SHARD_W), lambda j, i: (i, j))],
        out_shape=[jax.ShapeDtypeStruct((2, t, D_FF), BF16), jax.ShapeDtypeStruct((t, D_FF), BF16)],
        compiler_params=_params(("parallel", "parallel")),
    )(n, w_in4, w_in4)


def _mm_residual(name, a, w, res, scale):
    t, k = a.shape
    n = w.shape[1]
    tm = min(ROW_TILE, t)

    def epilogue(acc, ex, outs, ids):
        outs[0][...] = ex[0][...] + scale * acc

    return _mm(name, (t // tm, 1, 1), a, pl.BlockSpec((tm, k), lambda i, j, kk: (i, 0)),
               w, pl.BlockSpec((k, n), lambda i, j, kk: (0, 0)), (tm, n),
               extras=(res,), extra_specs=(pl.BlockSpec((tm, n), lambda i, j, kk: (i, 0)),),
               out_shape=(jax.ShapeDtypeStruct((t, n), F32),),
               out_specs=(pl.BlockSpec((tm, n), lambda i, j, kk: (i, 0)),), epilogue=epilogue)[0]


def _ffn_fwd(tag, x, g, w_in4, w_out, tokens=()):
    n = _rmsnorm_fwd(tag + "_norm", x, g, tokens)
    ab, h = _ffn_in(tag + "_in", n, w_in4)
    y = _mm_residual(tag + "_out", h, w_out, x, 0.5)
    return y, (n, ab, h)


def _ffn_bwd(tag, dres, x, g, saved, w_in4, w_out, tokens=(), on_weight_grads=None):
    n, ab, h = saved
    t, d = x.shape
    tm = min(ROW_TILE, t)
    tk = min(TOKEN_TILE, t)
    half_w = SHARD_W

    def dact_epilogue(acc, ex, outs, ids):
        a = ex[0][0].astype(F32)
        b = ex[0][1].astype(F32)
        sig = _sigmoid(a)
        outs[0][0] = (acc * b * (sig * (1.0 + a * (1.0 - sig)))).astype(BF16)
        outs[0][1] = (acc * (a * sig)).astype(BF16)

    du = _mm(tag + "_dact", (2, t // tm, 1),
             dres, pl.BlockSpec((tm, d), lambda j, i, kk: (i, 0)),
             w_out, pl.BlockSpec((half_w, d), lambda j, i, kk: (j, 0)), (tm, half_w),
             trans_b=True, a_pre=_half_bf16,
             extras=(ab,), extra_specs=(pl.BlockSpec((2, tm, half_w), lambda j, i, kk: (0, i, j)),), tokens=tokens,
             out_shape=(jax.ShapeDtypeStruct((2, t, D_FF), BF16),),
             out_specs=(pl.BlockSpec((2, tm, half_w), lambda j, i, kk: (0, i, j)),),
             epilogue=dact_epilogue)[0]

    def store_epilogue(acc, ex, outs, ids):
        outs[0][...] = acc

    dw_out = _mm(tag + "_dwout", (2, 1, t // tk),
                 h, pl.BlockSpec((tk, half_w), lambda i, j, kk: (kk, i)),
                 dres, pl.BlockSpec((tk, d), lambda i, j, kk: (kk, 0)), (half_w, d),
                 trans_a=True, b_pre=_half_bf16,
                 out_shape=(jax.ShapeDtypeStruct((D_FF, d), F32),),
                 out_specs=(pl.BlockSpec((half_w, d), lambda i, j, kk: (i, 0)),),
                 epilogue=store_epilogue)[0]

    dw_in4 = _mm(tag + "_dwin", (1, N_CHIPS, t // tk),
                 n, pl.BlockSpec((tk, d), lambda i, j, kk: (kk, 0)),
                 du, pl.BlockSpec((None, tk, SHARD_W), lambda i, j, kk: (j // 2, kk, j % 2)), (d, SHARD_W),
                 trans_a=True,
                 out_shape=(jax.ShapeDtypeStruct((N_CHIPS, d, SHARD_W), F32),),
                 out_specs=(pl.BlockSpec((None, d, SHARD_W), lambda i, j, kk: (j, 0, 0)),),
                 epilogue=store_epilogue)[0]

    late = () if on_weight_grads is None else on_weight_grads(dw_in4, dw_out)

    dx, dg = _mm(tag + "_dn", (t // tm, 1, N_CHIPS),
                 du, pl.BlockSpec((None, tm, SHARD_W), lambda i, j, kk: (kk // 2, i, kk % 2)),
                 w_in4, pl.BlockSpec((None, d, SHARD_W), lambda i, j, kk: (kk, 0, 0)), (tm, d),
                 trans_b=True,
                 extras=(x, g, dres),
                 extra_specs=(pl.BlockSpec((tm, d), lambda i, j, kk: (i, 0)),
                              pl.BlockSpec((1, d), lambda i, j, kk: (0, 0)),
                              pl.BlockSpec((tm, d), lambda i, j, kk: (i, 0))), tokens=late,
                 out_shape=(jax.ShapeDtypeStruct((t, d), F32), jax.ShapeDtypeStruct((1, d), F32)),
                 out_specs=(pl.BlockSpec((tm, d), lambda i, j, kk: (i, 0)),
                            pl.BlockSpec((1, d), lambda i, j, kk: (0, 0))),
                 epilogue=_rms_bwd_epilogue, sem=("arbitrary", "arbitrary", "arbitrary"))
    return dx, dw_in4, dw_out, dg


def _loss_head(y, target):
    t, d = y.shape
    tm = min(ROW_TILE, t)

    def body(y_ref, t_ref, dy_ref, loss_ref):
        diff = y_ref[...] - t_ref[...]
        dy_ref[...] = diff * (1.0 / d)
        part = jnp.full((8, LANES), 0.5 / d * jnp.sum(diff * diff), F32)
        i = pl.program_id(0)

        @pl.when(i == 0)
        def _():
            loss_ref[...] = part

        @pl.when(i > 0)
        def _():
            loss_ref[...] += part

    return pl.pallas_call(
        body, name="loss_head", grid=(t // tm,),
        in_specs=[pl.BlockSpec((tm, d), lambda i: (i, 0)), pl.BlockSpec((tm, d), lambda i: (i, 0))],
        out_specs=[pl.BlockSpec((tm, d), lambda i: (i, 0)), pl.BlockSpec((8, LANES), lambda i: (0, 0))],
        out_shape=[jax.ShapeDtypeStruct((t, d), F32), jax.ShapeDtypeStruct((8, LANES), F32)],
        compiler_params=_params(("arbitrary",)),
    )(y, target)


def _conv_fill(zp_ref, a_ref, g_ref, ah_ref, gh_ref, i):
    zh = ah_ref[...].astype(F32) * _sigmoid(gh_ref[...].astype(F32))
    zp_ref[pl.ds(0, CONV_PAD), :] = jnp.where(i > 0, zh, 0.0)
    zp_ref[pl.ds(CONV_PAD, a_ref.shape[0]), :] = a_ref[...].astype(F32) * _sigmoid(g_ref[...].astype(F32))


def _shift_groups(shifts):
    groups = {}
    for j, s in shifts:
        groups.setdefault(s % 8, []).append((j, s // 8))
    return groups


def _windows(zp_ref, r0, lanes, groups):
    for q, taps in groups.items():
        deepest = max(p for _, p in taps)
        win = zp_ref[pl.ds(r0 + q, 8 * deepest + CONV_ROWS), lanes]
        for j, p in taps:
            yield j, win[8 * p:8 * p + CONV_ROWS]


def _conv_apply(zp_ref, out_ref, dw_ref, bias_ref, tm, ch, shifts):
    groups = _shift_groups(shifts)
    for cc in range(ch // LANES):
        lanes = pl.ds(cc * LANES, LANES)
        w = [dw_ref[pl.ds(j, 1), lanes] for j in range(CONV_WIDTH)]
        for r0 in range(0, tm, CONV_ROWS):
            if bias_ref is None:
                acc = jnp.zeros((CONV_ROWS, LANES), F32)
            else:
                acc = jnp.broadcast_to(bias_ref[:, lanes], (CONV_ROWS, LANES))
            for j, rows in _windows(zp_ref, r0, lanes, groups):
                acc = acc + w[j] * rows
            out_ref[pl.ds(r0, CONV_ROWS), lanes] = acc


FWD_SHIFTS = [(j, CONV_PAD - (CONV_WIDTH - 1) + j) for j in range(CONV_WIDTH)]
BWD_SHIFTS = [(j, CONV_WIDTH - 1 - j) for j in range(CONV_WIDTH)]


def _conv_taps(zp_ref, z1_ref, dw_ref, bias_ref, tm, ch):
    _conv_apply(zp_ref, z1_ref, dw_ref, bias_ref, tm, ch, FWD_SHIFTS)


def _conv_specs(tm, ch):
    per = tm // CONV_PAD
    cb = COL_CONV_G // ch
    return [pl.BlockSpec((tm, ch), lambda i: (i, 0)),
            pl.BlockSpec((tm, ch), lambda i: (i, cb)),
            pl.BlockSpec((CONV_PAD, ch), lambda i: (jnp.maximum(i * per - 1, 0), 0)),
            pl.BlockSpec((CONV_PAD, ch), lambda i: (jnp.maximum(i * per - 1, 0), cb))]


def _conv_fwd(p, dw, bias, ln_g, ln_b):
    t = p.shape[0]
    ch = D_MODEL
    tm = min(CONV_TILE, t)

    def body(a_ref, g_ref, ah_ref, gh_ref, dw_ref, bias_ref, lg_ref, lb_ref, o_ref, zp_ref, z1_ref):
        i = pl.program_id(0)
        _conv_fill(zp_ref, a_ref, g_ref, ah_ref, gh_ref, i)
        _conv_taps(zp_ref, z1_ref, dw_ref, bias_ref, tm, ch)
        z1 = z1_ref[...]
        mu = jnp.mean(z1, axis=-1, keepdims=True)
        zc = z1 - mu
        rs = lax.rsqrt(jnp.mean(zc * zc, axis=-1, keepdims=True) + EPS)
        z2 = zc * rs * lg_ref[...] + lb_ref[...]
        o_ref[...] = (z2 * _sigmoid(z2)).astype(BF16)

    vec = pl.BlockSpec((1, ch), lambda i: (0, 0))
    return pl.pallas_call(
        body, name="conv_fwd", grid=(t // tm,),
        in_specs=_conv_specs(tm, ch) + [pl.BlockSpec((CONV_PAD, ch), lambda i: (0, 0)), vec, vec, vec],
        out_specs=pl.BlockSpec((tm, ch), lambda i: (i, 0)),
        out_shape=jax.ShapeDtypeStruct((t, ch), BF16),
        scratch_shapes=[pltpu.VMEM((CONV_PAD + tm, ch), F32), pltpu.VMEM((tm, ch), F32)],
        compiler_params=_params(("parallel",)),
    )(p, p, p, p, dw, bias, ln_g, ln_b)


def _conv_bwd_ln(p, dz3, dw, bias, ln_g, ln_b):
    t = p.shape[0]
    ch = D_MODEL
    tm = min(CONV_TILE, t)
    first = CONV_PAD - (CONV_WIDTH - 1)

    def body(a_ref, g_ref, ah_ref, gh_ref, dz3_ref, dw_ref, bias_ref, lg_ref, lb_ref,
             dz1_ref, ddw_ref, dbias_ref, dlg_ref, dlb_ref, zp_ref, z1_ref):
        i = pl.program_id(0)
        _conv_fill(zp_ref, a_ref, g_ref, ah_ref, gh_ref, i)
        _conv_taps(zp_ref, z1_ref, dw_ref, bias_ref, tm, ch)
        z1 = z1_ref[...]
        mu = jnp.mean(z1, axis=-1, keepdims=True)
        zc = z1 - mu
        rs = lax.rsqrt(jnp.mean(zc * zc, axis=-1, keepdims=True) + EPS)
        xh = zc * rs
        z2 = xh * lg_ref[...] + lb_ref[...]
        sig = _sigmoid(z2)
        dz2 = dz3_ref[...].astype(F32) * (sig * (1.0 + z2 * (1.0 - sig)))
        dxh = dz2 * lg_ref[...]
        dz1 = rs * (dxh - jnp.mean(dxh, axis=-1, keepdims=True) - xh * jnp.mean(dxh * xh, axis=-1, keepdims=True))
        dz1_ref[...] = dz1

        @pl.when(i == 0)
        def _():
            ddw_ref[...] = jnp.zeros_like(ddw_ref)
            dbias_ref[...] = jnp.zeros_like(dbias_ref)
            dlg_ref[...] = jnp.zeros_like(dlg_ref)
            dlb_ref[...] = jnp.zeros_like(dlb_ref)

        dlg_ref[...] += jnp.sum(dz2 * xh, axis=0, keepdims=True)
        dlb_ref[...] += jnp.sum(dz2, axis=0, keepdims=True)
        dbias_ref[...] += jnp.sum(dz1, axis=0, keepdims=True)
        groups = _shift_groups(FWD_SHIFTS)
        for cc in range(ch // LANES):
            lanes = pl.ds(cc * LANES, LANES)
            accs = [jnp.zeros((8, LANES), F32) for _ in range(CONV_WIDTH)]
            for r0 in range(0, tm, CONV_ROWS):
                dzc = dz1_ref[pl.ds(r0, CONV_ROWS), lanes]
                for j, rows in _windows(zp_ref, r0, lanes, groups):
                    accs[j] = accs[j] + jnp.sum((dzc * rows).reshape(CONV_ROWS // 8, 8, LANES), axis=0)
            for j in range(CONV_WIDTH):
                ddw_ref[pl.ds(j, 1), lanes] += jnp.sum(accs[j], axis=0, keepdims=True)

    vec = pl.BlockSpec((1, ch), lambda i: (0, 0))
    return pl.pallas_call(
        body, name="conv_bwd_ln", grid=(t // tm,),
        in_specs=_conv_specs(tm, ch) + [pl.BlockSpec((tm, ch), lambda i: (i, 0)),
                                        pl.BlockSpec((CONV_PAD, ch), lambda i: (0, 0)), vec, vec, vec],
        out_specs=[pl.BlockSpec((tm, ch), lambda i: (i, 0)), pl.BlockSpec((CONV_PAD, ch), lambda i: (0, 0)), vec, vec, vec],
        out_shape=[jax.ShapeDtypeStruct((t, ch), F32), jax.ShapeDtypeStruct((CONV_PAD, ch), F32)]
        + [jax.ShapeDtypeStruct((1, ch), F32)] * 3,
        scratch_shapes=[pltpu.VMEM((CONV_PAD + tm, ch), F32), pltpu.VMEM((tm, ch), F32)],
        compiler_params=_params(("arbitrary",)),
    )(p, p, p, p, dz3, dw, bias, ln_g, ln_b)


def _conv_bwd_glu(p, dz1, dw):
    t = p.shape[0]
    ch = D_MODEL
    tm = min(CONV_TILE, t)
    per = tm // CONV_PAD
    n_halo = t // CONV_PAD
    cb = COL_CONV_G // ch

    def body(a_ref, g_ref, dz_ref, dzn_ref, dw_ref, o_ref, zp_ref, z0_ref):
        i = pl.program_id(0)
        zp_ref[pl.ds(0, tm), :] = dz_ref[...]
        zp_ref[pl.ds(tm, CONV_PAD), :] = jnp.where(i < t // tm - 1, dzn_ref[...], 0.0)
        _conv_apply(zp_ref, z0_ref, dw_ref, None, tm, ch, BWD_SHIFTS)
        dz0 = z0_ref[...]
        a = a_ref[...].astype(F32)
        sig = _sigmoid(g_ref[...].astype(F32))
        o_ref[:, pl.ds(0, ch)] = (dz0 * sig).astype(BF16)
        o_ref[:, pl.ds(ch, ch)] = (dz0 * a * sig * (1.0 - sig)).astype(BF16)

    return pl.pallas_call(
        body, name="conv_bwd_glu", grid=(t // tm,),
        in_specs=[pl.BlockSpec((tm, ch), lambda i: (i, 0)), pl.BlockSpec((tm, ch), lambda i: (i, cb)),
                  pl.BlockSpec((tm, ch), lambda i: (i, 0)),
                  pl.BlockSpec((CONV_PAD, ch), lambda i: (jnp.minimum((i + 1) * per, n_halo - 1), 0)),
                  pl.BlockSpec((CONV_PAD, ch), lambda i: (0, 0))],
        out_specs=pl.BlockSpec((tm, 2 * ch), lambda i: (i, 0)),
        out_shape=jax.ShapeDtypeStruct((t, 2 * ch), BF16),
        scratch_shapes=[pltpu.VMEM((tm + CONV_PAD, ch), F32), pltpu.VMEM((tm, ch), F32)],
        compiler_params=_params(("parallel",)),
    )(p, p, dz1, dz1, dw)


def _bucket_onehot():
    qi = jnp.arange(BLOCK, dtype=jnp.int32)[:, None]
    kj = jnp.arange(2 * BLOCK, dtype=jnp.int32)[None, :]
    dist = jnp.maximum(qi + BLOCK - kj, 0)
    max_exact = N_BUCKETS // 2
    dflt = jnp.maximum(dist, 1).astype(F32)
    large = max_exact + (jnp.log(dflt / max_exact) / math.log(MAX_DISTANCE / max_exact)
                         * (N_BUCKETS - max_exact)).astype(jnp.int32)
    large = jnp.minimum(large, N_BUCKETS - 1)
    bucket = jnp.where(dist < max_exact, dist, large)
    onehot = bucket[None] == jnp.arange(N_BUCKETS, dtype=jnp.int32)[:, None, None]
    return onehot.astype(F32).reshape(N_BUCKETS, BLOCK * 2 * BLOCK)


def _bias_table(rel_bias_t, onehot):
    n = onehot.shape[1]
    tn = 4096

    def body(r_ref, oh_ref, o_ref):
        flat = pl.program_id(0) * tn + lax.broadcasted_iota(jnp.int32, (N_Q_HEADS, tn), 1)
        dist = (flat // (2 * BLOCK)) + BLOCK - (flat % (2 * BLOCK))
        bias = _dot(r_ref[...], oh_ref[...], precision=lax.Precision.HIGHEST)
        o_ref[...] = jnp.where((dist >= 0) & (dist < BLOCK), bias, NEG)

    return pl.pallas_call(
        body, name="bias_table", grid=(n // tn,),
        in_specs=[pl.BlockSpec((N_Q_HEADS, N_BUCKETS), lambda i: (0, 0)), pl.BlockSpec((N_BUCKETS, tn), lambda i: (0, i))],
        out_specs=pl.BlockSpec((N_Q_HEADS, tn), lambda i: (0, i)),
        out_shape=jax.ShapeDtypeStruct((N_Q_HEADS, n), F32),
        compiler_params=_params(("parallel",)),
    )(rel_bias_t, onehot)


def _bias_table_bwd(dbias, onehot):
    n = onehot.shape[1]
    tn = 4096

    def body(d_ref, oh_ref, o_ref):
        part = _dot(d_ref[...], oh_ref[...], trans_b=True, precision=lax.Precision.HIGHEST)
        i = pl.program_id(0)

        @pl.when(i == 0)
        def _():
            o_ref[...] = part

        @pl.when(i > 0)
        def _():
            o_ref[...] += part

    return pl.pallas_call(
        body, name="bias_table_bwd", grid=(n // tn,),
        in_specs=[pl.BlockSpec((N_Q_HEADS, tn), lambda i: (0, i)), pl.BlockSpec((N_BUCKETS, tn), lambda i: (0, i))],
        out_specs=pl.BlockSpec((N_Q_HEADS, N_BUCKETS), lambda i: (0, 0)),
        out_shape=jax.ShapeDtypeStruct((N_Q_HEADS, N_BUCKETS), F32),
        compiler_params=_params(("arbitrary",)),
    )(dbias, onehot)


def _attn_probs(q, kp, kc, gq, gk, sink, bias, before_start):
    qf = q.astype(F32)
    rq = lax.rsqrt(jnp.mean(qf * qf, axis=-1, keepdims=True) + EPS)
    qn = qf * rq * gq
    kf = jnp.concatenate([kp, kc], axis=0).astype(F32)
    rk = lax.rsqrt(jnp.mean(kf * kf, axis=-1, keepdims=True) + EPS)
    kn = kf * rk * gk
    s = _dot(qn.astype(BF16), kn.astype(BF16), trans_b=True) * (1.0 / math.sqrt(HEAD_DIM)) + bias
    s = jnp.where(before_start, NEG, s)
    m = jnp.maximum(jnp.max(s, axis=-1, keepdims=True), sink)
    p = jnp.exp(s - m)
    es = jnp.exp(sink - m)
    inv = 1.0 / (jnp.sum(p, axis=-1, keepdims=True) + es)
    return qf, rq, qn, kf, rk, kn, p * inv, es * inv


def _before_start(n):
    col = lax.broadcasted_iota(jnp.int32, (QROWS, 2 * BLOCK), 1)
    return (col < BLOCK) & (n == 0)


def _attn_specs():
    qspec = pl.BlockSpec((N_KV_HEADS, None, QROWS, HEAD_DIM), lambda n: (0, n, 0, 0))
    kprev = pl.BlockSpec((N_KV_HEADS, BLOCK, HEAD_DIM), lambda n: (0, jnp.maximum(n - 1, 0), 0))
    kcur = pl.BlockSpec((N_KV_HEADS, BLOCK, HEAD_DIM), lambda n: (0, n, 0))
    gain = pl.BlockSpec((1, HEAD_DIM), lambda n: (0, 0))
    sink = pl.BlockSpec((N_KV_HEADS, QROWS, 1), lambda n: (0, 0, 0))
    bias = pl.BlockSpec((N_KV_HEADS, QROWS, 2 * BLOCK), lambda n: (0, 0, 0))
    return qspec, kprev, kcur, gain, sink, bias


def _attn_fwd(q4, k3, v3, gq, gk, sink_rows, bias):
    nb = q4.shape[1]
    qspec, kprev, kcur, gain, sink, bspec = _attn_specs()

    def body(q_ref, kp_ref, kc_ref, vp_ref, vc_ref, gq_ref, gk_ref, sink_ref, bias_ref, o_ref):
        before_start = _before_start(pl.program_id(0))
        for h in range(N_KV_HEADS):
            pn = _attn_probs(q_ref[h], kp_ref[h], kc_ref[h], gq_ref[...], gk_ref[...], sink_ref[h], bias_ref[h],
                             before_start)[6]
            v = jnp.concatenate([vp_ref[h], vc_ref[h]], axis=0)
            o_ref[h] = _dot(pn.astype(BF16), v).astype(BF16)

    return pl.pallas_call(
        body, name="attn_fwd", grid=(nb,),
        in_specs=[qspec, kprev, kcur, kprev, kcur, gain, gain, sink, bspec],
        out_specs=qspec, out_shape=jax.ShapeDtypeStruct(q4.shape, BF16),
        compiler_params=_params(("parallel",)),
    )(q4, k3, k3, v3, v3, gq, gk, sink_rows, bias)


def _attn_bwd(q4, k3, v3, do4, gq, gk, sink_rows, bias):
    nb = q4.shape[1]
    qspec, kprev, kcur, gain, sink, bspec = _attn_specs()
    scale = 1.0 / math.sqrt(HEAD_DIM)

    def rms_bwd(dn, xf, r, g):
        w = dn * g
        dx = r * w - xf * (r * r * r) * jnp.mean(xf * w, axis=-1, keepdims=True)
        return dx, jnp.sum(dn * (xf * r), axis=0, keepdims=True)

    def body(q_ref, kp_ref, kc_ref, vp_ref, vc_ref, do_ref, gq_ref, gk_ref, sink_ref, bias_ref,
             dq_ref, dk_ref, dv_ref, dbias_ref, dsink_ref, dgq_ref, dgk_ref):
        n = pl.program_id(0)

        @pl.when(n == 0)
        def _():
            dbias_ref[...] = jnp.zeros_like(dbias_ref)
            dsink_ref[...] = jnp.zeros_like(dsink_ref)
            dgq_ref[...] = jnp.zeros_like(dgq_ref)
            dgk_ref[...] = jnp.zeros_like(dgk_ref)

        before_start = _before_start(n)
        dgq_sum = jnp.zeros((1, HEAD_DIM), F32)
        dgk_sum = jnp.zeros((1, HEAD_DIM), F32)
        for h in range(N_KV_HEADS):
            qf, rq, qn, kf, rk, kn, pn, psink = _attn_probs(
                q_ref[h], kp_ref[h], kc_ref[h], gq_ref[...], gk_ref[...], sink_ref[h], bias_ref[h], before_start)
            do = do_ref[h]
            v = jnp.concatenate([vp_ref[h], vc_ref[h]], axis=0)
            dv_win = _dot(pn.astype(BF16), do, trans_a=True)
            dp = _dot(do, v, trans_b=True)
            delta = jnp.sum(pn * dp, axis=-1, keepdims=True)
            ds = pn * (dp - delta)
            dsc = (ds * scale).astype(BF16)
            dqn = _dot(dsc, kn.astype(BF16))
            dkn = _dot(dsc, qn.astype(BF16), trans_a=True)
            dq, dgq = rms_bwd(dqn, qf, rq, gq_ref[...])
            dk_win, dgk = rms_bwd(dkn, kf, rk, gk_ref[...])
            dq_ref[h] = dq.astype(BF16)
            dk_ref[h, 0] = dk_win[:BLOCK]
            dk_ref[h, 1] = dk_win[BLOCK:]
            dv_ref[h, 0] = dv_win[:BLOCK]
            dv_ref[h, 1] = dv_win[BLOCK:]
            dbias_ref[h] += ds
            dsink_ref[h] += jnp.sum((-psink * delta).reshape(GROUP, BLOCK, 1), axis=1)
            dgq_sum = dgq_sum + dgq
            dgk_sum = dgk_sum + dgk
        dgq_ref[...] += dgq_sum
        dgk_ref[...] += dgk_sum

    kv_out = pl.BlockSpec((N_KV_HEADS, None, 2, BLOCK, HEAD_DIM), lambda n: (0, n, 0, 0, 0))
    kv_shape = jax.ShapeDtypeStruct((N_KV_HEADS, nb, 2, BLOCK, HEAD_DIM), F32)
    return pl.pallas_call(
        body, name="attn_bwd", grid=(nb,),
        in_specs=[qspec, kprev, kcur, kprev, kcur, qspec, gain, gain, sink, bspec],
        out_specs=[qspec, kv_out, kv_out, bspec,
                   pl.BlockSpec((N_KV_HEADS, GROUP, 1), lambda n: (0, 0, 0)), gain, gain],
        out_shape=[jax.ShapeDtypeStruct(q4.shape, BF16), kv_shape, kv_shape,
                   jax.ShapeDtypeStruct((N_KV_HEADS, QROWS, 2 * BLOCK), F32),
                   jax.ShapeDtypeStruct((N_KV_HEADS, GROUP, 1), F32),
                   jax.ShapeDtypeStruct((1, HEAD_DIM), F32),
                   jax.ShapeDtypeStruct((1, HEAD_DIM), F32)],
        compiler_params=_params(("arbitrary",)),
    )(q4, k3, k3, v3, v3, do4, gq, gk, sink_rows, bias)


def _kv_window_sum(parts):
    nb = parts.shape[1]

    def body(cur_ref, nxt_ref, o_ref):
        nxt = jnp.where(pl.program_id(0) < nb - 1, nxt_ref[...], 0.0)
        o_ref[...] = (cur_ref[...] + nxt).astype(BF16)

    return pl.pallas_call(
        body, name="kv_window_sum", grid=(nb,),
        in_specs=[pl.BlockSpec((N_KV_HEADS, None, None, BLOCK, HEAD_DIM), lambda n: (0, n, 1, 0, 0)),
                  pl.BlockSpec((N_KV_HEADS, None, None, BLOCK, HEAD_DIM), lambda n: (0, jnp.minimum(n + 1, nb - 1), 0, 0, 0))],
        out_specs=pl.BlockSpec((N_KV_HEADS, None, BLOCK, HEAD_DIM), lambda n: (0, n, 0, 0)),
        out_shape=jax.ShapeDtypeStruct((N_KV_HEADS, nb, BLOCK, HEAD_DIM), BF16),
        compiler_params=_params(("parallel",)),
    )(parts, parts)


def _q_to_heads(cols, t):
    nb = t // BLOCK
    return cols.reshape(nb, BLOCK, N_KV_HEADS, GROUP, HEAD_DIM).transpose(2, 0, 3, 1, 4).reshape(N_KV_HEADS, nb, QROWS, HEAD_DIM)


def _q_from_heads(q4, t):
    nb = t // BLOCK
    return q4.reshape(N_KV_HEADS, nb, GROUP, BLOCK, HEAD_DIM).transpose(1, 3, 0, 2, 4).reshape(t, N_Q_HEADS * HEAD_DIM)


def _kv_to_heads(cols, t):
    return cols.reshape(t, N_KV_HEADS, HEAD_DIM).transpose(1, 0, 2)


def _kv_from_heads(k3, t):
    return k3.transpose(1, 0, 2).reshape(t, N_KV_HEADS * HEAD_DIM)


GATE_TILE = 512


def _merge_fwd(z3, o, p, w_proj, w_o):
    t, d = z3.shape
    tm = min(ROW_TILE, t)
    tn = GATE_TILE

    def body(z_ref, o_ref, gc_ref, ga_ref, wp_ref, wo_ref, m_ref, a_ref, b_ref):
        a = _dot(z_ref[...], wp_ref[...])
        b = _dot(o_ref[...], wo_ref[...])
        m_ref[...] = (_sigmoid(gc_ref[...].astype(F32)) * a + _sigmoid(ga_ref[...].astype(F32)) * b).astype(BF16)
        a_ref[...] = a.astype(BF16)
        b_ref[...] = b.astype(BF16)

    row = pl.BlockSpec((tm, d), lambda i, j: (i, 0))
    wspec = pl.BlockSpec((d, tn), lambda i, j: (0, j))
    ospec = pl.BlockSpec((tm, tn), lambda i, j: (i, j))
    return pl.pallas_call(
        body, name="merge_fwd", grid=(t // tm, d // tn),
        in_specs=[row, row,
                  pl.BlockSpec((tm, tn), lambda i, j: (i, COL_GC // tn + j)),
                  pl.BlockSpec((tm, tn), lambda i, j: (i, COL_GA // tn + j)), wspec, wspec],
        out_specs=[ospec, ospec, ospec],
        out_shape=[jax.ShapeDtypeStruct((t, d), BF16)] * 3,
        compiler_params=_params(("parallel", "parallel")),
    )(z3, o, p, p, w_proj, w_o)


def _merge_bwd(dres, w_out, a, b, p, tokens=()):
    t, d = dres.shape
    tm = min(ROW_TILE, t)
    tn = GATE_TILE

    def epilogue(acc, ex, outs, ids):
        a_ref, b_ref, gc_ref, ga_ref = ex[:4]
        sc = _sigmoid(gc_ref[...].astype(F32))
        sa = _sigmoid(ga_ref[...].astype(F32))
        outs[0][...] = (acc * sc).astype(BF16)
        outs[1][...] = (acc * sa).astype(BF16)
        outs[2][0] = (acc * a_ref[...].astype(F32) * sc * (1.0 - sc)).astype(BF16)
        outs[2][1] = (acc * b_ref[...].astype(F32) * sa * (1.0 - sa)).astype(BF16)

    ospec = pl.BlockSpec((tm, tn), lambda i, j, kk: (i, j))
    return _mm("merge_bwd", (t // tm, d // tn, 1),
               dres, pl.BlockSpec((tm, d), lambda i, j, kk: (i, 0)),
               w_out, pl.BlockSpec((tn, d), lambda i, j, kk: (j, 0)), (tm, tn),
               trans_b=True, a_pre=_to_bf16,
               extras=(a, b, p, p),
               extra_specs=(ospec, ospec,
                            pl.BlockSpec((tm, tn), lambda i, j, kk: (i, COL_GC // tn + j)),
                            pl.BlockSpec((tm, tn), lambda i, j, kk: (i, COL_GA // tn + j))), tokens=tokens,
               out_shape=(jax.ShapeDtypeStruct((t, d), BF16), jax.ShapeDtypeStruct((t, d), BF16),
                          jax.ShapeDtypeStruct((2, t, d), BF16)),
               out_specs=(ospec, ospec, pl.BlockSpec((2, tm, tn), lambda i, j, kk: (0, i, j))),
               epilogue=epilogue)


def _store_epilogue(acc, ex, outs, ids):
    outs[0][...] = acc


def _store_bf16_epilogue(acc, ex, outs, ids):
    outs[0][...] = acc.astype(BF16)


def _mm_nt(name, a, w, out_dtype=BF16):
    t, n = a.shape
    k = w.shape[0]
    tm = min(ROW_TILE, t)
    return _mm(name, (t // tm, 1, 1), a, pl.BlockSpec((tm, n), lambda i, j, kk: (i, 0)),
               w, pl.BlockSpec((k, n), lambda i, j, kk: (0, 0)), (tm, k), trans_b=True,
               out_shape=(jax.ShapeDtypeStruct((t, k), out_dtype),),
               out_specs=(pl.BlockSpec((tm, k), lambda i, j, kk: (i, 0)),),
               epilogue=_store_bf16_epilogue if out_dtype == BF16 else _store_epilogue)[0]


def _mm_tn(name, a, b, b_pre=None):
    t, m = a.shape
    n = b.shape[1]
    tk = min(TOKEN_TILE, t)
    return _mm(name, (1, 1, t // tk), a, pl.BlockSpec((tk, m), lambda i, j, kk: (kk, 0)),
               b, pl.BlockSpec((tk, n), lambda i, j, kk: (kk, 0)), (m, n), trans_a=True, b_pre=b_pre,
               out_shape=(jax.ShapeDtypeStruct((m, n), F32),),
               out_specs=(pl.BlockSpec((m, n), lambda i, j, kk: (0, 0)),), epilogue=_store_epilogue)[0]


def _local_step(x, target, small, comm):
    t = x.shape[0]
    w = dict(small)

    w.update(comm.weights("A", None))
    x1, ffn1_saved = _ffn_fwd("ffn1", x, w["ffn1_norm"], w["ffn1_w_in"], w["ffn1_w_out"], comm.tokens)
    w.update(comm.weights("B", x1))
    hm = _rmsnorm_fwd("mix_norm", x1, w["mix_norm"], comm.tokens)
    tm = min(ROW_TILE, t)
    p = _mm("mix_in", (N_CHIPS, t // tm, 1),
            hm, pl.BlockSpec((tm, D_MODEL), lambda j, i, kk: (i, 0)),
            w["w_in"], pl.BlockSpec((None, D_MODEL, SHARD_W), lambda j, i, kk: (j, 0, 0)), (tm, SHARD_W),
            out_shape=(jax.ShapeDtypeStruct((t, IN_W), BF16),),
            out_specs=(pl.BlockSpec((tm, SHARD_W), lambda j, i, kk: (i, j)),),
            epilogue=_store_bf16_epilogue)[0]

    z3 = _conv_fwd(p, w["conv_dw_kernel"], w["conv_dw_bias"], w["conv_ln_g"], w["conv_ln_b"])

    onehot = _bucket_onehot()
    bias = _bias_table(w["rel_bias"].T, onehot).reshape(N_KV_HEADS, QROWS, 2 * BLOCK)
    sink_rows = jnp.repeat(w["attn_sinks"].reshape(N_KV_HEADS, GROUP), BLOCK, axis=1)[..., None]
    q4 = _q_to_heads(p[:, COL_Q:COL_K], t)
    k3 = _kv_to_heads(p[:, COL_K:COL_V], t)
    v3 = _kv_to_heads(p[:, COL_V:COL_GC], t)
    o4 = _attn_fwd(q4, k3, v3, w["q_norm"], w["k_norm"], sink_rows, bias)
    o = _q_from_heads(o4, t)

    merged, a, b = _merge_fwd(z3, o, p, w["conv_w_proj"], w["attn_w_o"])
    x2 = _mm_residual("mix_out", merged, w["w_out"], x1, 1.0)
    w.update(comm.weights("C", x2))
    x3, ffn2_saved = _ffn_fwd("ffn2", x2, w["ffn2_norm"], w["ffn2_w_in"], w["ffn2_w_out"])
    dy, loss = _loss_head(x3, target)

    g, big = {}, {}
    dres2, big["ffn2_w_in"], big["ffn2_w_out"], g["ffn2_norm"] = _ffn_bwd(
        "ffn2b", dy, x2, w["ffn2_norm"], ffn2_saved, w["ffn2_w_in"], w["ffn2_w_out"])
    tokens = comm.reduce_start("R1", big)

    da, db, dgates = _merge_bwd(dres2, w["w_out"], a, b, p, tokens)
    big = {}
    big["w_out"] = _mm_tn("d_w_out", merged, dres2, b_pre=_to_bf16)
    big["conv_w_proj"] = _mm_tn("d_w_proj", z3, da)
    big["attn_w_o"] = _mm_tn("d_w_o", o, db)
    dz3 = _mm_nt("d_z3", da, w["conv_w_proj"])
    do = _mm_nt("d_o", db, w["attn_w_o"])

    dq4, dk_parts, dv_parts, dbias, dsink, g["q_norm"], g["k_norm"] = _attn_bwd(
        q4, k3, v3, _q_to_heads(do, t), w["q_norm"], w["k_norm"], sink_rows, bias)
    dk3 = _kv_window_sum(dk_parts).reshape(N_KV_HEADS, t, HEAD_DIM)
    dv3 = _kv_window_sum(dv_parts).reshape(N_KV_HEADS, t, HEAD_DIM)
    g["rel_bias"] = _bias_table_bwd(dbias.reshape(N_Q_HEADS, BLOCK * 2 * BLOCK), onehot).T
    g["attn_sinks"] = dsink.reshape(N_Q_HEADS)

    dz1, big["conv_dw_kernel"], g["conv_dw_bias"], g["conv_ln_g"], g["conv_ln_b"] = _conv_bwd_ln(
        p, dz3, w["conv_dw_kernel"], w["conv_dw_bias"], w["conv_ln_g"], w["conv_ln_b"])
    dconv = _conv_bwd_glu(p, dz1, w["conv_dw_kernel"])

    dp = jnp.concatenate([dconv, _q_from_heads(dq4, t), _kv_from_heads(dk3, t), _kv_from_heads(dv3, t),
                          dgates[0], dgates[1]], axis=1)
    tk = min(TOKEN_TILE, t)
    big["w_in"] = _mm("d_w_in", (1, N_CHIPS, t // tk),
                    hm, pl.BlockSpec((tk, D_MODEL), lambda i, j, kk: (kk, 0)),
                    dp, pl.BlockSpec((tk, SHARD_W), lambda i, j, kk: (kk, j)), (D_MODEL, SHARD_W),
                    trans_a=True,
                    out_shape=(jax.ShapeDtypeStruct((N_CHIPS, D_MODEL, SHARD_W), F32),),
                    out_specs=(pl.BlockSpec((None, D_MODEL, SHARD_W), lambda i, j, kk: (j, 0, 0)),),
                    epilogue=_store_epilogue)[0]
    dres1, g["mix_norm"] = _mm("d_mix", (t // tm, 1, N_CHIPS),
                               dp, pl.BlockSpec((tm, SHARD_W), lambda i, j, kk: (i, kk)),
                               w["w_in"], pl.BlockSpec((None, D_MODEL, SHARD_W), lambda i, j, kk: (kk, 0, 0)),
                               (tm, D_MODEL), trans_b=True,
                               extras=(x1, w["mix_norm"], dres2),
                               extra_specs=(pl.BlockSpec((tm, D_MODEL), lambda i, j, kk: (i, 0)),
                                            pl.BlockSpec((1, D_MODEL), lambda i, j, kk: (0, 0)),
                                            pl.BlockSpec((tm, D_MODEL), lambda i, j, kk: (i, 0))),
                               out_shape=(jax.ShapeDtypeStruct((t, D_MODEL), F32), jax.ShapeDtypeStruct((1, D_MODEL), F32)),
                               out_specs=(pl.BlockSpec((tm, D_MODEL), lambda i, j, kk: (i, 0)),
                                          pl.BlockSpec((1, D_MODEL), lambda i, j, kk: (0, 0))),
                               epilogue=_rms_bwd_epilogue, sem=("arbitrary", "arbitrary", "arbitrary"))

    comm.reduce_finish("R1", dres1)
    tokens = comm.reduce_start("R2", big)

    def ffn1_grads(dw_in4, dw_out):
        comm.reduce_finish("R2", dw_in4)
        return comm.reduce_start("R3", {"ffn1_w_in": dw_in4, "ffn1_w_out": dw_out})

    grad_x, _, _, g["ffn1_norm"] = _ffn_bwd(
        "ffn1b", dres1, x, w["ffn1_norm"], ffn1_saved, w["ffn1_w_in"], w["ffn1_w_out"], tokens, ffn1_grads)
    comm.reduce_finish("R3", grad_x)
    return loss[0, 0], grad_x, g


def _mesh_place():
    x, y, c = lax.axis_index("x"), lax.axis_index("y"), lax.axis_index("c")
    chips = [(1 - x, y), (x, 1 - y), (1 - x, 1 - y)]
    return x, y, c, chips


def _any_specs(n):
    return [pl.BlockSpec(memory_space=pl.ANY)] * n


HBM_SPEC = pl.BlockSpec(memory_space=pltpu.HBM)
SEM_SPEC = pl.BlockSpec(memory_space=pltpu.SEMAPHORE)
EFFECT = pltpu.SideEffectType.DATAFLOW_SIDE_EFFECTING


def _in_hbm(a):
    return pltpu.with_memory_space_constraint(a, pltpu.HBM)


def _copy_start(name, srcs, lands, plan, after=()):
    ns, nb = len(srcs), len(lands)
    n = 3 * ns

    def body(*refs):
        s_refs, l_refs = refs[:ns], refs[ns:ns + nb]
        send_sems, recv_sems = refs[ns + nb + len(after)], refs[ns + nb + len(after) + 1]
        token = refs[-1]
        for k, (src, dst, to, _) in enumerate(plan(s_refs, l_refs)):
            pltpu.make_async_remote_copy(src_ref=src, dst_ref=dst, send_sem=send_sems.at[k], recv_sem=recv_sems.at[k],
                                         device_id=to, device_id_type=MESH).start()
        token[...] = jnp.zeros_like(token)

    bufs = list(srcs) + list(lands)
    outs = pl.pallas_call(
        body, name=name,
        out_shape=(pltpu.SemaphoreType.DMA((n,)), pltpu.SemaphoreType.DMA((n,)),
                   *[pltpu.HBM(a.shape, a.dtype) for a in bufs], jax.ShapeDtypeStruct((8, LANES), F32)),
        in_specs=[HBM_SPEC] * len(bufs) + [pl.BlockSpec(memory_space=pl.ANY)] * len(after),
        out_specs=(SEM_SPEC, SEM_SPEC, *[HBM_SPEC] * len(bufs), pl.BlockSpec(memory_space=pltpu.VMEM)),
        input_output_aliases={i: 2 + i for i in range(len(bufs))},
        compiler_params=pltpu.CompilerParams(has_side_effects=EFFECT),
    )(*[_in_hbm(a) for a in bufs], *after)
    return outs[0], outs[1], list(outs[2:2 + ns]), list(outs[2 + ns:2 + ns + nb]), outs[-1]


def _copy_wait(name, send_sems, recv_sems, srcs, lands, after, plan):
    ns, nb = len(srcs), len(lands)

    def body(*refs):
        s_refs, l_refs = refs[:ns], refs[ns:ns + nb]
        send_sems, recv_sems = refs[ns + nb], refs[ns + nb + 1]
        for k, (src, _, to, mine) in enumerate(plan(s_refs, l_refs)):
            cp = pltpu.make_async_remote_copy(src_ref=src, dst_ref=mine, send_sem=send_sems.at[k], recv_sem=recv_sems.at[k],
                                              device_id=to, device_id_type=MESH)
            cp.wait_send()
            cp.wait_recv()

    bufs = list(srcs) + list(lands)
    outs = pl.pallas_call(
        body, name=name,
        out_shape=tuple(pltpu.HBM(a.shape, a.dtype) for a in bufs),
        in_specs=[HBM_SPEC] * len(bufs) + [SEM_SPEC, SEM_SPEC, pl.BlockSpec(memory_space=pl.ANY)],
        out_specs=tuple([HBM_SPEC] * len(bufs)),
        input_output_aliases={i: i for i in range(len(bufs))},
        compiler_params=pltpu.CompilerParams(has_side_effects=EFFECT),
    )(*bufs, send_sems, recv_sems, after)
    return list(outs[:ns]), list(outs[ns:])


def _gather_plan(s_refs, l_refs):
    x, y, c, chips = _mesh_place()
    jme = 2 * x + y
    return [(s.at[c], land.at[jme, c], (*chip, c), land.at[2 * chip[0] + chip[1], c])
            for s, land in zip(s_refs, l_refs) for chip in chips]


def _scatter_plan(s_refs, l_refs):
    x, y, c, chips = _mesh_place()
    return [(s.at[2 * chip[0] + chip[1]], land.at[k], (*chip, c), land.at[k])
            for s, land in zip(s_refs, l_refs) for k, chip in enumerate(chips)]


def _gather_forward(name, shards, landed):
    nw = len(shards)

    def body(*refs):
        s_refs, o_refs = refs[:nw], refs[2 * nw:3 * nw]
        send_sems, recv_sems = refs[3 * nw:]
        x, y, c, chips = _mesh_place()
        me, sib, jme = (x, y, c), (x, y, 1 - c), 2 * x + y
        sent = []
        for w in range(nw):
            parts = [(o_refs[w].at[2 * chip[0] + chip[1], c], o_refs[w].at[2 * chip[0] + chip[1], c]) for chip in chips]
            parts.append((s_refs[w], o_refs[w].at[jme]))
            for k, (src, dst) in enumerate(parts):
                cp = pltpu.make_async_remote_copy(src_ref=src, dst_ref=dst, send_sem=send_sems.at[4 * w + k],
                                                  recv_sem=recv_sems.at[4 * w + k], device_id=sib, device_id_type=MESH)
                cp.start()
                sent.append(cp)
        for w in range(nw):
            parts = [o_refs[w].at[2 * chip[0] + chip[1], 1 - c] for chip in chips] + [o_refs[w].at[jme]]
            for k, part in enumerate(parts):
                pltpu.make_async_remote_copy(src_ref=part, dst_ref=part, send_sem=send_sems.at[4 * w + k],
                                             recv_sem=recv_sems.at[4 * w + k], device_id=me, device_id_type=MESH).wait_recv()
        for cp in sent:
            cp.wait_send()

    return pl.pallas_call(
        body, name=name,
        in_specs=_any_specs(2 * nw), out_specs=_any_specs(nw),
        out_shape=[jax.ShapeDtypeStruct(a.shape, a.dtype) for a in landed],
        input_output_aliases={nw + i: i for i in range(nw)},
        scratch_shapes=[pltpu.SemaphoreType.DMA((4 * nw,)), pltpu.SemaphoreType.DMA((4 * nw,))],
    )(*shards, *landed)


def _exchange_halves(name, grads, after=()):
    nw = len(grads)

    def body(*refs):
        g_refs, o_refs = refs[:nw], refs[nw + len(after):2 * nw + len(after)]
        send_sems, recv_sems = refs[2 * nw + len(after):]
        x, y, c, _ = _mesh_place()
        copies = []
        for w in range(nw):
            cp = pltpu.make_async_remote_copy(src_ref=g_refs[w].at[:, 1 - c], dst_ref=o_refs[w], send_sem=send_sems.at[w],
                                              recv_sem=recv_sems.at[w], device_id=(x, y, 1 - c), device_id_type=MESH)
            cp.start()
            copies.append(cp)
        for cp in copies:
            cp.wait()

    return pl.pallas_call(
        body, name=name,
        in_specs=_any_specs(nw + len(after)), out_specs=_any_specs(nw),
        out_shape=[jax.ShapeDtypeStruct((N_CHIPS,) + g.shape[2:], F32) for g in grads],
        scratch_shapes=[pltpu.SemaphoreType.DMA((nw,)), pltpu.SemaphoreType.DMA((nw,))],
    )(*grads, *after)


def _row_tile(r):
    for cand in (256, 176, 128, 64, 32, 16, 8):
        if r % cand == 0:
            return cand
    return r


def _add_own_half(c_idx, grad, got):
    _, _, r, cols = grad.shape
    tr = _row_tile(r)

    def body(c_ref, g_ref, o_ref, out_ref):
        out_ref[...] = (g_ref[...] + o_ref[...]).astype(BF16)

    return pl.pallas_call(
        body, name="add_own_half",
        grid_spec=pltpu.PrefetchScalarGridSpec(
            num_scalar_prefetch=1, grid=(N_CHIPS, r // tr),
            in_specs=[pl.BlockSpec((None, None, tr, cols), lambda j, i, c_ref: (j, c_ref[0], i, 0)),
                      pl.BlockSpec((None, tr, cols), lambda j, i, c_ref: (j, i, 0))],
            out_specs=pl.BlockSpec((None, tr, cols), lambda j, i, c_ref: (j, i, 0))),
        out_shape=jax.ShapeDtypeStruct((N_CHIPS, r, cols), BF16),
        compiler_params=_params(("parallel", "parallel")),
    )(c_idx, grad, got)


def _sum_pieces(place_idx, sums, landed):
    _, r, cols = sums.shape
    tr = _row_tile(r)

    def body(j_ref, own_ref, p_ref, o_ref):
        o_ref[...] = ((own_ref[...].astype(F32) + p_ref[0].astype(F32)) + p_ref[1].astype(F32)) + p_ref[2].astype(F32)

    return pl.pallas_call(
        body, name="sum_pieces",
        grid_spec=pltpu.PrefetchScalarGridSpec(
            num_scalar_prefetch=1, grid=(r // tr,),
            in_specs=[pl.BlockSpec((None, tr, cols), lambda i, j_ref: (j_ref[0], i, 0)),
                      pl.BlockSpec((N_CHIPS - 1, tr, cols), lambda i, j_ref: (0, i, 0))],
            out_specs=pl.BlockSpec((None, tr, cols), lambda i, j_ref: (j_ref[1], i, 0))),
        out_shape=jax.ShapeDtypeStruct((2, r, cols), F32),
        compiler_params=_params(("parallel",)),
    )(place_idx, sums, landed)


def _join_halves(name, halves):
    nw = len(halves)

    def body(*refs):
        o_refs = refs[nw:2 * nw]
        send_sems, recv_sems = refs[2 * nw:]
        x, y, c, _ = _mesh_place()
        copies = []
        for w in range(nw):
            cp = pltpu.make_async_remote_copy(src_ref=o_refs[w].at[c], dst_ref=o_refs[w].at[c], send_sem=send_sems.at[w],
                                              recv_sem=recv_sems.at[w], device_id=(x, y, 1 - c), device_id_type=MESH)
            cp.start()
            copies.append(cp)
        for w in range(nw):
            copies[w].wait_send()
            landed = o_refs[w].at[1 - c]
            pltpu.make_async_remote_copy(src_ref=landed, dst_ref=landed, send_sem=send_sems.at[w], recv_sem=recv_sems.at[w],
                                         device_id=(x, y, c), device_id_type=MESH).wait_recv()

    return pl.pallas_call(
        body, name=name,
        in_specs=_any_specs(nw), out_specs=_any_specs(nw),
        out_shape=[jax.ShapeDtypeStruct(h.shape, F32) for h in halves],
        input_output_aliases={i: i for i in range(nw)},
        scratch_shapes=[pltpu.SemaphoreType.DMA((nw,)), pltpu.SemaphoreType.DMA((nw,))],
    )(*halves)


SMALL_ROWS = 8


def _all_reduce_small(pack):
    rows, cols = pack.shape
    n_dev = 8

    def body(p_ref, o_ref, slots, send_sems, recv_sems):
        x, y, c, _ = _mesh_place()
        me = 4 * x + 2 * y + c
        slots[me] = p_ref[...]
        copies = []
        for k in range(1, n_dev):
            peer = (me + k) % n_dev
            cp = pltpu.make_async_remote_copy(src_ref=p_ref, dst_ref=slots.at[me], send_sem=send_sems.at[k],
                                              recv_sem=recv_sems.at[k],
                                              device_id=(peer // 4, (peer // 2) % 2, peer % 2), device_id_type=MESH)
            cp.start()
            copies.append(cp)
        for k in range(1, n_dev):
            src = (me + n_dev - k) % n_dev
            pltpu.make_async_remote_copy(src_ref=p_ref, dst_ref=slots.at[src], send_sem=send_sems.at[k],
                                         recv_sem=recv_sems.at[k], device_id=(x, y, c), device_id_type=MESH).wait_recv()
        for cp in copies:
            cp.wait_send()
        total = slots[0]
        for s in range(1, n_dev):
            total = total + slots[s]
        o_ref[...] = total

    return pl.pallas_call(
        body, name="all_reduce_small",
        in_specs=[pl.BlockSpec(memory_space=pltpu.VMEM)], out_specs=pl.BlockSpec(memory_space=pltpu.VMEM),
        out_shape=jax.ShapeDtypeStruct((rows, cols), F32),
        scratch_shapes=[pltpu.VMEM((n_dev, rows, cols), F32), pltpu.SemaphoreType.DMA((n_dev,)),
                        pltpu.SemaphoreType.DMA((n_dev,))],
    )(pack)


def _adamw(name, w, g, m, v):
    r, cols = w.shape
    tr = _row_tile(r)

    def body(w_ref, g_ref, m_ref, v_ref, d_ref, nm_ref, nv_ref):
        gv = g_ref[...]
        nm = ADAM_B1 * m_ref[...] + (1.0 - ADAM_B1) * gv
        nv = ADAM_B2 * v_ref[...] + (1.0 - ADAM_B2) * (gv * gv)
        m_hat = nm / (1.0 - ADAM_B1 ** ADAM_STEP)
        v_hat = nv / (1.0 - ADAM_B2 ** ADAM_STEP)
        d_ref[...] = -ADAM_LR * (m_hat / (jnp.sqrt(v_hat) + ADAM_EPS) + ADAM_WD * w_ref[...])
        nm_ref[...] = nm
        nv_ref[...] = nv

    spec = pl.BlockSpec((tr, cols), lambda i: (i, 0))
    return pl.pallas_call(
        body, name=name, grid=(r // tr,),
        in_specs=[spec] * 4, out_specs=[spec] * 3,
        out_shape=[jax.ShapeDtypeStruct((r, cols), F32)] * 3,
        compiler_params=_params(("parallel",)),
    )(w, g, m, v)


BIG = ["ffn1_w_in", "ffn1_w_out", "w_in", "conv_w_proj", "attn_w_o", "w_out", "ffn2_w_in", "ffn2_w_out", "conv_dw_kernel"]
COL_SHARDED = ("ffn1_w_in", "w_in", "ffn2_w_in")
SMALL = ["ffn1_norm", "mix_norm", "ffn2_norm", "conv_dw_bias", "conv_ln_g", "conv_ln_b", "q_norm", "k_norm", "attn_sinks", "rel_bias"]
WEIGHTS = ["ffn1_norm", "ffn1_w_in", "ffn1_w_out", "mix_norm", "w_in", "conv_dw_kernel", "conv_dw_bias", "conv_ln_g",
           "conv_ln_b", "conv_w_proj", "q_norm", "k_norm", "attn_sinks", "rel_bias", "attn_w_o", "w_out", "ffn2_norm",
           "ffn2_w_in", "ffn2_w_out"]
SMALL_PLACE = {"ffn1_norm": (0, 0, 1024), "mix_norm": (1, 0, 1024), "ffn2_norm": (2, 0, 1024), "conv_dw_bias": (3, 0, 1024),
               "conv_ln_g": (4, 0, 1024), "conv_ln_b": (5, 0, 1024), "q_norm": (6, 0, 64), "k_norm": (6, 128, 64),
               "attn_sinks": (6, 256, 16), "rel_bias": (7, 0, 512)}
LOSS_PLACE = (6, 384)


def _pack_small(vals, fill=0.0, loss=None):
    pack = jnp.full((SMALL_ROWS, D_MODEL), fill, F32)
    for name, (row, lane, n) in SMALL_PLACE.items():
        pack = pack.at[row, lane:lane + n].set(vals[name].reshape(n))
    if loss is not None:
        pack = pack.at[LOSS_PLACE[0], LOSS_PLACE[1]].set(loss)
    return pack


def _unpack_small(pack, shapes):
    return {name: pack[row, lane:lane + n].reshape(shapes[name]) for name, (row, lane, n) in SMALL_PLACE.items()}


def _shard_halves(name, a):
    if name == "conv_dw_kernel":
        a = jnp.pad(a, ((0, CONV_PAD - CONV_WIDTH), (0, 0)))
    r, cols = a.shape
    return a.reshape(2, r // 2, cols)


GATHER_GROUPS = {"A": ["ffn1_w_in", "ffn1_w_out"],
                 "B": ["w_in", "conv_dw_kernel", "conv_w_proj", "attn_w_o", "w_out"],
                 "C": ["ffn2_w_in", "ffn2_w_out"]}


class _MeshComm:
    def __init__(self, wts):
        self.c_idx = lax.axis_index("c").astype(jnp.int32).reshape(1)
        self.place_idx = jnp.stack([2 * lax.axis_index("x") + lax.axis_index("y"), lax.axis_index("c")]).astype(jnp.int32)
        self.wts, self.gathers, self.reductions, self.reduced = wts, {}, {}, {}
        self.tokens, self.last_join = (), ()
        self._gather_start("A", ())

    def _gather_start(self, group, after):
        names = GATHER_GROUPS[group]
        shards = [_shard_halves(n, self.wts[n]) if n == "conv_dw_kernel" else _shard_halves(n, self.wts[n]).astype(BF16)
                  for n in names]
        lands = [lax.empty((N_CHIPS,) + s.shape, s.dtype) for s in shards]
        self.gathers[group] = _copy_start("gather_start_" + group, shards, lands, _gather_plan, after=after)
        self.tokens = (self.gathers[group][-1],)

    def weights(self, group, after):
        send_sems, recv_sems, shards, lands, token = self.gathers.pop(group)
        shards, lands = _copy_wait("gather_wait_" + group, send_sems, recv_sems, shards, lands,
                                   token if after is None else after, _gather_plan)
        gathered = _gather_forward("gather_forward_" + group, shards, lands)
        self.tokens = ()
        following = {"A": "B", "B": "C"}.get(group)
        if following:
            self._gather_start(following, (gathered[0],))
        out = {}
        for n, g4 in zip(GATHER_GROUPS[group], gathered):
            r, cols = g4.shape[2] * 2, g4.shape[3]
            if n in COL_SHARDED:
                out[n] = g4.reshape(N_CHIPS, r, cols)
            elif n == "conv_dw_kernel":
                out[n] = g4.reshape(N_CHIPS, r, cols).transpose(1, 0, 2).reshape(r, N_CHIPS * cols)
            else:
                out[n] = g4.reshape(N_CHIPS * r, cols)
        return out

    def reduce_start(self, group, grads):
        names = list(grads)
        g4 = []
        for n in names:
            a = grads[n]
            if n == "conv_dw_kernel":
                a = a.reshape(CONV_PAD, N_CHIPS, -1).transpose(1, 0, 2)
            elif n not in COL_SHARDED:
                a = a.reshape(N_CHIPS, a.shape[0] // N_CHIPS, a.shape[1])
            g4.append(a.reshape(N_CHIPS, 2, a.shape[1] // 2, a.shape[2]))
        got = _exchange_halves("exchange_halves_" + group, g4, after=self.last_join)
        sums = [_add_own_half(self.c_idx, a, b) for a, b in zip(g4, got)]
        lands = [lax.empty((N_CHIPS - 1,) + s.shape[1:], s.dtype) for s in sums]
        started = _copy_start("scatter_start_" + group, sums, lands, _scatter_plan, after=self.last_join)
        self.reductions[group] = (names,) + started
        return (started[-1],)

    def reduce_finish(self, group, after):
        names, send_sems, recv_sems, sums, lands, _ = self.reductions.pop(group)
        sums, lands = _copy_wait("scatter_wait_" + group, send_sems, recv_sems, sums, lands, after, _scatter_plan)
        halves = [_sum_pieces(self.place_idx, s, p) for s, p in zip(sums, lands)]
        joined = _join_halves("join_halves_" + group, halves)
        self.last_join = (joined[0],)
        self.reduced.update(zip(names, joined))


def kernel(x, ffn1_norm, ffn1_w_in, ffn1_w_out, mix_norm, w_in, conv_dw_kernel, conv_dw_bias, conv_ln_g, conv_ln_b, conv_w_proj, q_norm, k_norm, attn_sinks, rel_bias, attn_w_o, w_out, ffn2_norm, ffn2_w_in, ffn2_w_out, loss_target, m_ffn1_norm, m_ffn1_w_in, m_ffn1_w_out, m_mix_norm, m_w_in, m_conv_dw_kernel, m_conv_dw_bias, m_conv_ln_g, m_conv_ln_b, m_conv_w_proj, m_q_norm, m_k_norm, m_attn_sinks, m_rel_bias, m_attn_w_o, m_w_out, m_ffn2_norm, m_ffn2_w_in, m_ffn2_w_out, v_ffn1_norm, v_ffn1_w_in, v_ffn1_w_out, v_mix_norm, v_w_in, v_conv_dw_kernel, v_conv_dw_bias, v_conv_ln_g, v_conv_ln_b, v_conv_w_proj, v_q_norm, v_k_norm, v_attn_sinks, v_rel_bias, v_attn_w_o, v_w_out, v_ffn2_norm, v_ffn2_w_in, v_ffn2_w_out):
    args = dict(locals())
    wts = {n: args[n] for n in WEIGHTS}
    mom = {n: args["m_" + n] for n in WEIGHTS}
    var = {n: args["v_" + n] for n in WEIGHTS}
    comm = _MeshComm(wts)
    small = {n: wts[n] if n in ("attn_sinks", "rel_bias") else wts[n].reshape(1, -1) for n in SMALL}
    loss_part, grad_x, g = _local_step(x[0], loss_target[0], small, comm)

    small_sum = _all_reduce_small(_pack_small(g, loss=loss_part))
    loss = small_sum[LOSS_PLACE[0], LOSS_PLACE[1]]
    small_shapes = {n: wts[n].shape for n in SMALL}
    g_small = _unpack_small(small_sum, small_shapes)

    grads, delta, new_m, new_v = {}, {}, {}, {}
    for n in BIG:
        j = comm.reduced[n]
        gs = j.reshape(j.shape[1] * 2, j.shape[2])
        pad = n == "conv_dw_kernel"
        ws, ms, vs = (_shard_halves(n, a).reshape(gs.shape) for a in (wts[n], mom[n], var[n]))
        d, nm, nv = _adamw("adamw_" + n, ws, gs, ms, vs)
        cut = (lambda a: a[:CONV_WIDTH]) if pad else (lambda a: a)
        grads[n], delta[n], new_m[n], new_v[n] = cut(gs), cut(d), cut(nm), cut(nv)
    d, nm, nv = _adamw("adamw_small", _pack_small(wts), small_sum, _pack_small(mom), _pack_small(var, fill=1.0))
    grads.update(g_small)
    delta.update(_unpack_small(d, small_shapes))
    new_m.update(_unpack_small(nm, small_shapes))
    new_v.update(_unpack_small(nv, small_shapes))

    return (loss, grad_x[None], *[grads[n] for n in WEIGHTS], *[delta[n] for n in WEIGHTS],
            *[new_m[n] for n in WEIGHTS], *[new_v[n] for n in WEIGHTS])
```

```python
import functools
import math

import jax
import jax.numpy as jnp
from jax import lax
from jax.experimental import pallas as pl
from jax.experimental.pallas import tpu as pltpu

F32 = jnp.float32
BF16 = jnp.bfloat16
MESH = pl.DeviceIdType.MESH

EPS = 1e-6
D_MODEL = 1024
D_FF = 2816
N_CHIPS = 4
SHARD_W = 2 * D_FF // N_CHIPS
HEAD_DIM = 64
N_Q_HEADS = 16
N_KV_HEADS = 4
GROUP = N_Q_HEADS // N_KV_HEADS
BLOCK = 128
QROWS = GROUP * BLOCK
N_BUCKETS = 32
MAX_DISTANCE = 128
CONV_WIDTH = 31
CONV_PAD = 32
NEG = float(jnp.finfo(jnp.float32).min)

ADAM_LR = 0.001
ADAM_B1 = 0.9
ADAM_B2 = 0.999
ADAM_EPS = 1e-08
ADAM_WD = 0.01
ADAM_STEP = 10

VMEM_LIMIT_BYTES = 56 * 1024 * 1024
ROW_TILE = 512
TOKEN_TILE = 1024
CONV_TILE = 256
CONV_ROWS = 128
LANES = 128

COL_CONV_A, COL_CONV_G, COL_Q, COL_K, COL_V, COL_GC, COL_GA = 0, 1024, 2048, 3072, 3328, 3584, 4608
IN_W = 5632


def _params(sem, vmem=VMEM_LIMIT_BYTES):
    return pltpu.CompilerParams(dimension_semantics=sem, vmem_limit_bytes=vmem)


def _sigmoid(x):
    return 1.0 / (1.0 + jnp.exp(-x))


def _dot(a, b, trans_a=False, trans_b=False, precision=None):
    dn = (((0,) if trans_a else (1,), (1,) if trans_b else (0,)), ((), ()))
    return lax.dot_general(a, b, dn, preferred_element_type=F32, precision=precision)


def _mm(name, grid, a, a_spec, b, b_spec, acc_shape, *, trans_a=False, trans_b=False, a_pre=None, b_pre=None,
        extras=(), extra_specs=(), tokens=(), out_shape, out_specs, epilogue, sem=("parallel", "parallel", "arbitrary")):
    n_k = grid[2]
    extras = tuple(extras) + tuple(tokens)
    extra_specs = tuple(extra_specs) + (pl.BlockSpec((8, LANES), lambda i, j, kk: (0, 0)),) * len(tokens)
    n_extra = len(extras)
    n_out = len(out_shape)

    def body(a_ref, b_ref, *rest):
        ex = rest[:n_extra]
        outs = rest[n_extra:n_extra + n_out]
        ids = (pl.program_id(0), pl.program_id(1), pl.program_id(2))
        av = a_ref[...]
        bv = b_ref[...]
        if a_pre is not None:
            av = a_pre(av)
        if b_pre is not None:
            bv = b_pre(bv)
        part = _dot(av, bv, trans_a, trans_b)
        if n_k == 1:
            epilogue(part, ex, outs, ids)
        else:
            acc = rest[-1]

            @pl.when(ids[2] == 0)
            def _():
                acc[...] = part

            @pl.when(ids[2] > 0)
            def _():
                acc[...] += part

            @pl.when(ids[2] == n_k - 1)
            def _():
                epilogue(acc[...], ex, outs, ids)

    scratch = [] if n_k == 1 else [pltpu.VMEM(acc_shape, F32)]
    return pl.pallas_call(
        body, name=name, grid=grid,
        in_specs=[a_spec, b_spec, *extra_specs],
        out_specs=list(out_specs), out_shape=list(out_shape),
        scratch_shapes=scratch, compiler_params=_params(sem),
    )(a, b, *extras)


def _half_bf16(v):
    return (0.5 * v).astype(BF16)


def _to_bf16(v):
    return v.astype(BF16)


def _rmsnorm_fwd(name, x, g, tokens=()):
    t, d = x.shape
    tm = min(ROW_TILE, t)

    def body(x_ref, g_ref, *rest):
        o_ref = rest[-1]
        xv = x_ref[...]
        r = lax.rsqrt(jnp.mean(xv * xv, axis=-1, keepdims=True) + EPS)
        o_ref[...] = (xv * r * g_ref[...]).astype(BF16)

    return pl.pallas_call(
        body, name=name, grid=(t // tm,),
        in_specs=[pl.BlockSpec((tm, d), lambda i: (i, 0)), pl.BlockSpec((1, d), lambda i: (0, 0))]
        + [pl.BlockSpec((8, LANES), lambda i: (0, 0))] * len(tokens),
        out_specs=pl.BlockSpec((tm, d), lambda i: (i, 0)),
        out_shape=jax.ShapeDtypeStruct((t, d), BF16),
        compiler_params=_params(("parallel",)),
    )(x, g, *tokens)


def _rms_bwd_epilogue(acc, ex, outs, ids):
    x_ref, g_ref, dres_ref = ex[:3]
    out_ref, dg_ref = outs
    xv = x_ref[...]
    r = lax.rsqrt(jnp.mean(xv * xv, axis=-1, keepdims=True) + EPS)
    w = acc * g_ref[...]
    dx = r * w - xv * (r * r * r) * jnp.mean(xv * w, axis=-1, keepdims=True)
    out_ref[...] = dres_ref[...] + dx
    part = jnp.sum(acc * (xv * r), axis=0, keepdims=True)

    @pl.when(ids[0] == 0)
    def _():
        dg_ref[...] = part

    @pl.when(ids[0] > 0)
    def _():
        dg_ref[...] += part


def _ffn_in(name, n, w_in4, tokens=()):
    t, d = n.shape
    tm = min(ROW_TILE, t)

    def body(n_ref, wa_ref, wb_ref, *rest):
        ab_ref, h_ref = rest[-2:]
        nv = n_ref[...]
        a = _dot(nv, wa_ref[...])
        b = _dot(nv, wb_ref[...])
        h_ref[...] = (a * _sigmoid(a) * b).astype(BF16)
        ab_ref[0] = a.astype(BF16)
        ab_ref[1] = b.astype(BF16)

    return pl.pallas_call(
        body, name=name, grid=(2, t // tm),
        in_specs=[pl.BlockSpec((tm, d), lambda j, i: (i, 0)),
                  pl.BlockSpec((None, d, SHARD_W), lambda j, i: (j, 0, 0)),
                  pl.BlockSpec((None, d, SHARD_W), lambda j, i: (j + 2, 0, 0))]
        + [pl.BlockSpec((8, LANES), lambda j, i: (0, 0))] * len(tokens),
        out_specs=[pl.BlockSpec((2, tm, SHARD_W), lambda j, i: (0, i, j)),
                   pl.BlockSpec((tm, SHARD_W), lambda j, i: (i, j))],
        out_shape=[jax.ShapeDtypeStruct((2, t, D_FF), BF16), jax.ShapeDtypeStruct((t, D_FF), BF16)],
        compiler_params=_params(("parallel", "parallel")),
    )(n, w_in4, w_in4, *tokens)


def _mm_residual(name, a, w, res, scale):
    t, k = a.shape
    n = w.shape[1]
    tm = min(ROW_TILE, t)

    def epilogue(acc, ex, outs, ids):
        outs[0][...] = ex[0][...] + scale * acc

    return _mm(name, (t // tm, 1, 1), a, pl.BlockSpec((tm, k), lambda i, j, kk: (i, 0)),
               w, pl.BlockSpec((k, n), lambda i, j, kk: (0, 0)), (tm, n),
               extras=(res,), extra_specs=(pl.BlockSpec((tm, n), lambda i, j, kk: (i, 0)),),
               out_shape=(jax.ShapeDtypeStruct((t, n), F32),),
               out_specs=(pl.BlockSpec((tm, n), lambda i, j, kk: (i, 0)),), epilogue=epilogue)[0]


def _ffn_fwd(tag, x, n, w_in4, w_out, tokens=()):
    ab, h = _ffn_in(tag + "_in", n, w_in4, tokens)
    y = _mm_residual(tag + "_out", h, w_out, x, 0.5)
    return y, (n, ab, h)


def _ffn_bwd(tag, dres, x, g, saved, w_in4, w_out, tokens=(), on_weight_grads=None):
    n, ab, h = saved
    t, d = x.shape
    tm = min(ROW_TILE, t)
    tk = min(TOKEN_TILE, t)
    half_w = SHARD_W

    def dact_epilogue(acc, ex, outs, ids):
        a = ex[0][0].astype(F32)
        b = ex[0][1].astype(F32)
        sig = _sigmoid(a)
        outs[0][0] = (acc * b * (sig * (1.0 + a * (1.0 - sig)))).astype(BF16)
        outs[0][1] = (acc * (a * sig)).astype(BF16)

    du = _mm(tag + "_dact", (2, t // tm, 1),
             dres, pl.BlockSpec((tm, d), lambda j, i, kk: (i, 0)),
             w_out, pl.BlockSpec((half_w, d), lambda j, i, kk: (j, 0)), (tm, half_w),
             trans_b=True, a_pre=_half_bf16,
             extras=(ab,), extra_specs=(pl.BlockSpec((2, tm, half_w), lambda j, i, kk: (0, i, j)),), tokens=tokens,
             out_shape=(jax.ShapeDtypeStruct((2, t, D_FF), BF16),),
             out_specs=(pl.BlockSpec((2, tm, half_w), lambda j, i, kk: (0, i, j)),),
             epilogue=dact_epilogue)[0]

    def store_epilogue(acc, ex, outs, ids):
        outs[0][...] = acc.astype(BF16)

    dw_out = _mm(tag + "_dwout", (2, 1, t // tk),
                 h, pl.BlockSpec((tk, half_w), lambda i, j, kk: (kk, i)),
                 dres, pl.BlockSpec((tk, d), lambda i, j, kk: (kk, 0)), (half_w, d),
                 trans_a=True, b_pre=_half_bf16,
                 out_shape=(jax.ShapeDtypeStruct((D_FF, d), BF16),),
                 out_specs=(pl.BlockSpec((half_w, d), lambda i, j, kk: (i, 0)),),
                 epilogue=store_epilogue)[0]

    dw_in4 = _mm(tag + "_dwin", (1, N_CHIPS, t // tk),
                 n, pl.BlockSpec((tk, d), lambda i, j, kk: (kk, 0)),
                 du, pl.BlockSpec((None, tk, SHARD_W), lambda i, j, kk: (j // 2, kk, j % 2)), (d, SHARD_W),
                 trans_a=True,
                 out_shape=(jax.ShapeDtypeStruct((N_CHIPS, d, SHARD_W), BF16),),
                 out_specs=(pl.BlockSpec((None, d, SHARD_W), lambda i, j, kk: (j, 0, 0)),),
                 epilogue=store_epilogue)[0]

    late = () if on_weight_grads is None else on_weight_grads(dw_in4, dw_out)

    dx, dg = _mm(tag + "_dn", (t // tm, 1, N_CHIPS),
                 du, pl.BlockSpec((None, tm, SHARD_W), lambda i, j, kk: (kk // 2, i, kk % 2)),
                 w_in4, pl.BlockSpec((None, d, SHARD_W), lambda i, j, kk: (kk, 0, 0)), (tm, d),
                 trans_b=True,
                 extras=(x, g, dres),
                 extra_specs=(pl.BlockSpec((tm, d), lambda i, j, kk: (i, 0)),
                              pl.BlockSpec((1, d), lambda i, j, kk: (0, 0)),
                              pl.BlockSpec((tm, d), lambda i, j, kk: (i, 0))), tokens=late,
                 out_shape=(jax.ShapeDtypeStruct((t, d), F32), jax.ShapeDtypeStruct((1, d), F32)),
                 out_specs=(pl.BlockSpec((tm, d), lambda i, j, kk: (i, 0)),
                            pl.BlockSpec((1, d), lambda i, j, kk: (0, 0))),
                 epilogue=_rms_bwd_epilogue, sem=("arbitrary", "arbitrary", "arbitrary"))
    return dx, dw_in4, dw_out, dg


def _loss_head(y, target):
    t, d = y.shape
    tm = min(ROW_TILE, t)

    def body(y_ref, t_ref, dy_ref, loss_ref):
        diff = y_ref[...] - t_ref[...]
        dy_ref[...] = diff * (1.0 / d)
        part = jnp.full((8, LANES), 0.5 / d * jnp.sum(diff * diff), F32)
        i = pl.program_id(0)

        @pl.when(i == 0)
        def _():
            loss_ref[...] = part

        @pl.when(i > 0)
        def _():
            loss_ref[...] += part

    return pl.pallas_call(
        body, name="loss_head", grid=(t // tm,),
        in_specs=[pl.BlockSpec((tm, d), lambda i: (i, 0)), pl.BlockSpec((tm, d), lambda i: (i, 0))],
        out_specs=[pl.BlockSpec((tm, d), lambda i: (i, 0)), pl.BlockSpec((8, LANES), lambda i: (0, 0))],
        out_shape=[jax.ShapeDtypeStruct((t, d), F32), jax.ShapeDtypeStruct((8, LANES), F32)],
        compiler_params=_params(("arbitrary",)),
    )(y, target)


def _conv_fill(zp_ref, a_ref, g_ref, ah_ref, gh_ref, i):
    zh = ah_ref[...].astype(F32) * _sigmoid(gh_ref[...].astype(F32))
    zp_ref[pl.ds(0, CONV_PAD), :] = jnp.where(i > 0, zh, 0.0)
    zp_ref[pl.ds(CONV_PAD, a_ref.shape[0]), :] = a_ref[...].astype(F32) * _sigmoid(g_ref[...].astype(F32))


def _shift_groups(shifts):
    groups = {}
    for j, s in shifts:
        groups.setdefault(s % 8, []).append((j, s // 8))
    return groups


def _windows(zp_ref, r0, lanes, groups):
    for q, taps in groups.items():
        deepest = max(p for _, p in taps)
        win = zp_ref[pl.ds(r0 + q, 8 * deepest + CONV_ROWS), lanes]
        for j, p in taps:
            yield j, win[8 * p:8 * p + CONV_ROWS]


def _conv_apply(zp_ref, out_ref, dw_ref, bias_ref, tm, ch, shifts):
    groups = _shift_groups(shifts)
    for cc in range(ch // LANES):
        lanes = pl.ds(cc * LANES, LANES)
        w = [dw_ref[pl.ds(j, 1), lanes] for j in range(CONV_WIDTH)]
        for r0 in range(0, tm, CONV_ROWS):
            if bias_ref is None:
                acc = jnp.zeros((CONV_ROWS, LANES), F32)
            else:
                acc = jnp.broadcast_to(bias_ref[:, lanes], (CONV_ROWS, LANES))
            for j, rows in _windows(zp_ref, r0, lanes, groups):
                acc = acc + w[j] * rows
            out_ref[pl.ds(r0, CONV_ROWS), lanes] = acc


FWD_SHIFTS = [(j, CONV_PAD - (CONV_WIDTH - 1) + j) for j in range(CONV_WIDTH)]
BWD_SHIFTS = [(j, CONV_WIDTH - 1 - j) for j in range(CONV_WIDTH)]


def _conv_taps(zp_ref, z1_ref, dw_ref, bias_ref, tm, ch):
    _conv_apply(zp_ref, z1_ref, dw_ref, bias_ref, tm, ch, FWD_SHIFTS)


def _conv_specs(tm, ch):
    per = tm // CONV_PAD
    cb = COL_CONV_G // ch
    return [pl.BlockSpec((tm, ch), lambda i: (i, 0)),
            pl.BlockSpec((tm, ch), lambda i: (i, cb)),
            pl.BlockSpec((CONV_PAD, ch), lambda i: (jnp.maximum(i * per - 1, 0), 0)),
            pl.BlockSpec((CONV_PAD, ch), lambda i: (jnp.maximum(i * per - 1, 0), cb))]


def _conv_fwd(p, dw, bias, ln_g, ln_b):
    t = p.shape[0]
    ch = D_MODEL
    tm = min(CONV_TILE, t)

    def body(a_ref, g_ref, ah_ref, gh_ref, dw_ref, bias_ref, lg_ref, lb_ref, o_ref, z1_ref, zp_ref):
        i = pl.program_id(0)
        _conv_fill(zp_ref, a_ref, g_ref, ah_ref, gh_ref, i)
        _conv_taps(zp_ref, z1_ref, dw_ref, bias_ref, tm, ch)
        z1 = z1_ref[...]
        mu = jnp.mean(z1, axis=-1, keepdims=True)
        zc = z1 - mu
        rs = lax.rsqrt(jnp.mean(zc * zc, axis=-1, keepdims=True) + EPS)
        z2 = zc * rs * lg_ref[...] + lb_ref[...]
        o_ref[...] = (z2 * _sigmoid(z2)).astype(BF16)

    vec = pl.BlockSpec((1, ch), lambda i: (0, 0))
    return pl.pallas_call(
        body, name="conv_fwd", grid=(t // tm,),
        in_specs=_conv_specs(tm, ch) + [pl.BlockSpec((CONV_PAD, ch), lambda i: (0, 0)), vec, vec, vec],
        out_specs=[pl.BlockSpec((tm, ch), lambda i: (i, 0)), pl.BlockSpec((tm, ch), lambda i: (i, 0))],
        out_shape=[jax.ShapeDtypeStruct((t, ch), BF16), jax.ShapeDtypeStruct((t, ch), F32)],
        scratch_shapes=[pltpu.VMEM((CONV_PAD + tm, ch), F32)],
        compiler_params=_params(("parallel",)),
    )(p, p, p, p, dw, bias, ln_g, ln_b)


def _conv_bwd_ln(p, z1_saved, dz3, ln_g, ln_b):
    t = p.shape[0]
    ch = D_MODEL
    tm = min(CONV_TILE, t)

    def body(a_ref, g_ref, ah_ref, gh_ref, z1_ref, dz3_ref, lg_ref, lb_ref,
             dz1_ref, ddw_ref, dbias_ref, dlg_ref, dlb_ref, zp_ref):
        i = pl.program_id(0)
        _conv_fill(zp_ref, a_ref, g_ref, ah_ref, gh_ref, i)
        z1 = z1_ref[...]
        mu = jnp.mean(z1, axis=-1, keepdims=True)
        zc = z1 - mu
        rs = lax.rsqrt(jnp.mean(zc * zc, axis=-1, keepdims=True) + EPS)
        xh = zc * rs
        z2 = xh * lg_ref[...] + lb_ref[...]
        sig = _sigmoid(z2)
        dz2 = dz3_ref[...].astype(F32) * (sig * (1.0 + z2 * (1.0 - sig)))
        dxh = dz2 * lg_ref[...]
        dz1 = rs * (dxh - jnp.mean(dxh, axis=-1, keepdims=True) - xh * jnp.mean(dxh * xh, axis=-1, keepdims=True))
        dz1_ref[...] = dz1

        @pl.when(i == 0)
        def _():
            ddw_ref[...] = jnp.zeros_like(ddw_ref)
            dbias_ref[...] = jnp.zeros_like(dbias_ref)
            dlg_ref[...] = jnp.zeros_like(dlg_ref)
            dlb_ref[...] = jnp.zeros_like(dlb_ref)

        dlg_ref[...] += jnp.sum(dz2 * xh, axis=0, keepdims=True)
        dlb_ref[...] += jnp.sum(dz2, axis=0, keepdims=True)
        dbias_ref[...] += jnp.sum(dz1, axis=0, keepdims=True)
        groups = _shift_groups(FWD_SHIFTS)
        for cc in range(ch // LANES):
            lanes = pl.ds(cc * LANES, LANES)
            accs = [jnp.zeros((8, LANES), F32) for _ in range(CONV_WIDTH)]
            for r0 in range(0, tm, CONV_ROWS):
                dzc = dz1_ref[pl.ds(r0, CONV_ROWS), lanes]
                for j, rows in _windows(zp_ref, r0, lanes, groups):
                    accs[j] = accs[j] + jnp.sum((dzc * rows).reshape(CONV_ROWS // 8, 8, LANES), axis=0)
            for j in range(CONV_WIDTH):
                ddw_ref[pl.ds(j, 1), lanes] += jnp.sum(accs[j], axis=0, keepdims=True)

    vec = pl.BlockSpec((1, ch), lambda i: (0, 0))
    return pl.pallas_call(
        body, name="conv_bwd_ln", grid=(t // tm,),
        in_specs=_conv_specs(tm, ch) + [pl.BlockSpec((tm, ch), lambda i: (i, 0)),
                                        pl.BlockSpec((tm, ch), lambda i: (i, 0)), vec, vec],
        out_specs=[pl.BlockSpec((tm, ch), lambda i: (i, 0)), pl.BlockSpec((CONV_PAD, ch), lambda i: (0, 0)), vec, vec, vec],
        out_shape=[jax.ShapeDtypeStruct((t, ch), F32), jax.ShapeDtypeStruct((CONV_PAD, ch), F32)]
        + [jax.ShapeDtypeStruct((1, ch), F32)] * 3,
        scratch_shapes=[pltpu.VMEM((CONV_PAD + tm, ch), F32)],
        compiler_params=_params(("arbitrary",)),
    )(p, p, p, p, z1_saved, dz3, ln_g, ln_b)


def _conv_bwd_glu(p, dz1, dw):
    t = p.shape[0]
    ch = D_MODEL
    tm = min(CONV_TILE, t)
    per = tm // CONV_PAD
    n_halo = t // CONV_PAD
    cb = COL_CONV_G // ch

    def body(a_ref, g_ref, dz_ref, dzn_ref, dw_ref, o_ref, zp_ref, z0_ref):
        i = pl.program_id(0)
        zp_ref[pl.ds(0, tm), :] = dz_ref[...]
        zp_ref[pl.ds(tm, CONV_PAD), :] = jnp.where(i < t // tm - 1, dzn_ref[...], 0.0)
        _conv_apply(zp_ref, z0_ref, dw_ref, None, tm, ch, BWD_SHIFTS)
        dz0 = z0_ref[...]
        a = a_ref[...].astype(F32)
        sig = _sigmoid(g_ref[...].astype(F32))
        o_ref[:, pl.ds(0, ch)] = (dz0 * sig).astype(BF16)
        o_ref[:, pl.ds(ch, ch)] = (dz0 * a * sig * (1.0 - sig)).astype(BF16)

    return pl.pallas_call(
        body, name="conv_bwd_glu", grid=(t // tm,),
        in_specs=[pl.BlockSpec((tm, ch), lambda i: (i, 0)), pl.BlockSpec((tm, ch), lambda i: (i, cb)),
                  pl.BlockSpec((tm, ch), lambda i: (i, 0)),
                  pl.BlockSpec((CONV_PAD, ch), lambda i: (jnp.minimum((i + 1) * per, n_halo - 1), 0)),
                  pl.BlockSpec((CONV_PAD, ch), lambda i: (0, 0))],
        out_specs=pl.BlockSpec((tm, 2 * ch), lambda i: (i, 0)),
        out_shape=jax.ShapeDtypeStruct((t, 2 * ch), BF16),
        scratch_shapes=[pltpu.VMEM((tm + CONV_PAD, ch), F32), pltpu.VMEM((tm, ch), F32)],
        compiler_params=_params(("parallel",)),
    )(p, p, dz1, dz1, dw)


def _bucket_onehot():
    qi = jnp.arange(BLOCK, dtype=jnp.int32)[:, None]
    kj = jnp.arange(2 * BLOCK, dtype=jnp.int32)[None, :]
    dist = jnp.maximum(qi + BLOCK - kj, 0)
    max_exact = N_BUCKETS // 2
    dflt = jnp.maximum(dist, 1).astype(F32)
    large = max_exact + (jnp.log(dflt / max_exact) / math.log(MAX_DISTANCE / max_exact)
                         * (N_BUCKETS - max_exact)).astype(jnp.int32)
    large = jnp.minimum(large, N_BUCKETS - 1)
    bucket = jnp.where(dist < max_exact, dist, large)
    onehot = bucket[None] == jnp.arange(N_BUCKETS, dtype=jnp.int32)[:, None, None]
    return onehot.astype(F32).reshape(N_BUCKETS, BLOCK * 2 * BLOCK)


def _bias_table(rel_bias_t, onehot):
    n = onehot.shape[1]
    tn = 4096

    def body(r_ref, oh_ref, o_ref):
        flat = pl.program_id(0) * tn + lax.broadcasted_iota(jnp.int32, (N_Q_HEADS, tn), 1)
        dist = (flat // (2 * BLOCK)) + BLOCK - (flat % (2 * BLOCK))
        bias = _dot(r_ref[...], oh_ref[...], precision=lax.Precision.HIGHEST)
        o_ref[...] = jnp.where((dist >= 0) & (dist < BLOCK), bias, NEG)

    return pl.pallas_call(
        body, name="bias_table", grid=(n // tn,),
        in_specs=[pl.BlockSpec((N_Q_HEADS, N_BUCKETS), lambda i: (0, 0)), pl.BlockSpec((N_BUCKETS, tn), lambda i: (0, i))],
        out_specs=pl.BlockSpec((N_Q_HEADS, tn), lambda i: (0, i)),
        out_shape=jax.ShapeDtypeStruct((N_Q_HEADS, n), F32),
        compiler_params=_params(("parallel",)),
    )(rel_bias_t, onehot)


def _bias_table_bwd(dbias, onehot):
    n = onehot.shape[1]
    tn = 4096

    def body(d_ref, oh_ref, o_ref):
        part = _dot(d_ref[...], oh_ref[...], trans_b=True, precision=lax.Precision.HIGHEST)
        i = pl.program_id(0)

        @pl.when(i == 0)
        def _():
            o_ref[...] = part

        @pl.when(i > 0)
        def _():
            o_ref[...] += part

    return pl.pallas_call(
        body, name="bias_table_bwd", grid=(n // tn,),
        in_specs=[pl.BlockSpec((N_Q_HEADS, tn), lambda i: (0, i)), pl.BlockSpec((N_BUCKETS, tn), lambda i: (0, i))],
        out_specs=pl.BlockSpec((N_Q_HEADS, N_BUCKETS), lambda i: (0, 0)),
        out_shape=jax.ShapeDtypeStruct((N_Q_HEADS, N_BUCKETS), F32),
        compiler_params=_params(("arbitrary",)),
    )(dbias, onehot)


def _attn_probs(q, kp, kc, gq, gk, sink, bias, before_start):
    qf = q.astype(F32)
    rq = lax.rsqrt(jnp.mean(qf * qf, axis=-1, keepdims=True) + EPS)
    qn = qf * rq * gq
    kf = jnp.concatenate([kp, kc], axis=0).astype(F32)
    rk = lax.rsqrt(jnp.mean(kf * kf, axis=-1, keepdims=True) + EPS)
    kn = kf * rk * gk
    s = _dot(qn.astype(BF16), kn.astype(BF16), trans_b=True) * (1.0 / math.sqrt(HEAD_DIM)) + bias
    s = jnp.where(before_start, NEG, s)
    m = jnp.maximum(jnp.max(s, axis=-1, keepdims=True), sink)
    p = jnp.exp(s - m)
    es = jnp.exp(sink - m)
    inv = 1.0 / (jnp.sum(p, axis=-1, keepdims=True) + es)
    return qf, rq, qn, kf, rk, kn, p * inv, es * inv


def _before_start(n):
    col = lax.broadcasted_iota(jnp.int32, (QROWS, 2 * BLOCK), 1)
    return (col < BLOCK) & (n == 0)


def _attn_specs():
    qspec = pl.BlockSpec((N_KV_HEADS, None, QROWS, HEAD_DIM), lambda n: (0, n, 0, 0))
    kprev = pl.BlockSpec((N_KV_HEADS, BLOCK, HEAD_DIM), lambda n: (0, jnp.maximum(n - 1, 0), 0))
    kcur = pl.BlockSpec((N_KV_HEADS, BLOCK, HEAD_DIM), lambda n: (0, n, 0))
    gain = pl.BlockSpec((1, HEAD_DIM), lambda n: (0, 0))
    sink = pl.BlockSpec((N_KV_HEADS, QROWS, 1), lambda n: (0, 0, 0))
    bias = pl.BlockSpec((N_KV_HEADS, QROWS, 2 * BLOCK), lambda n: (0, 0, 0))
    return qspec, kprev, kcur, gain, sink, bias


def _attn_fwd(q4, k3, v3, gq, gk, sink_rows, bias):
    nb = q4.shape[1]
    qspec, kprev, kcur, gain, sink, bspec = _attn_specs()

    def body(q_ref, kp_ref, kc_ref, vp_ref, vc_ref, gq_ref, gk_ref, sink_ref, bias_ref, o_ref):
        before_start = _before_start(pl.program_id(0))
        for h in range(N_KV_HEADS):
            pn = _attn_probs(q_ref[h], kp_ref[h], kc_ref[h], gq_ref[...], gk_ref[...], sink_ref[h], bias_ref[h],
                             before_start)[6]
            v = jnp.concatenate([vp_ref[h], vc_ref[h]], axis=0)
            o_ref[h] = _dot(pn.astype(BF16), v).astype(BF16)

    return pl.pallas_call(
        body, name="attn_fwd", grid=(nb,),
        in_specs=[qspec, kprev, kcur, kprev, kcur, gain, gain, sink, bspec],
        out_specs=qspec, out_shape=jax.ShapeDtypeStruct(q4.shape, BF16),
        compiler_params=_params(("parallel",)),
    )(q4, k3, k3, v3, v3, gq, gk, sink_rows, bias)


def _attn_bwd(q4, k3, v3, do4, gq, gk, sink_rows, bias):
    nb = q4.shape[1]
    qspec, kprev, kcur, gain, sink, bspec = _attn_specs()
    scale = 1.0 / math.sqrt(HEAD_DIM)

    def rms_bwd(dn, xf, r, g):
        w = dn * g
        dx = r * w - xf * (r * r * r) * jnp.mean(xf * w, axis=-1, keepdims=True)
        return dx, jnp.sum(dn * (xf * r), axis=0, keepdims=True)

    def body(q_ref, kp_ref, kc_ref, vp_ref, vc_ref, do_ref, gq_ref, gk_ref, sink_ref, bias_ref,
             dq_ref, dk_ref, dv_ref, dbias_ref, dsink_ref, dgq_ref, dgk_ref):
        n = pl.program_id(0)

        @pl.when(n == 0)
        def _():
            dbias_ref[...] = jnp.zeros_like(dbias_ref)
            dsink_ref[...] = jnp.zeros_like(dsink_ref)
            dgq_ref[...] = jnp.zeros_like(dgq_ref)
            dgk_ref[...] = jnp.zeros_like(dgk_ref)

        before_start = _before_start(n)
        dgq_sum = jnp.zeros((1, HEAD_DIM), F32)
        dgk_sum = jnp.zeros((1, HEAD_DIM), F32)
        for h in range(N_KV_HEADS):
            qf, rq, qn, kf, rk, kn, pn, psink = _attn_probs(
                q_ref[h], kp_ref[h], kc_ref[h], gq_ref[...], gk_ref[...], sink_ref[h], bias_ref[h], before_start)
            do = do_ref[h]
            v = jnp.concatenate([vp_ref[h], vc_ref[h]], axis=0)
            dv_win = _dot(pn.astype(BF16), do, trans_a=True)
            dp = _dot(do, v, trans_b=True)
            delta = jnp.sum(pn * dp, axis=-1, keepdims=True)
            ds = pn * (dp - delta)
            dsc = (ds * scale).astype(BF16)
            dqn = _dot(dsc, kn.astype(BF16))
            dkn = _dot(dsc, qn.astype(BF16), trans_a=True)
            dq, dgq = rms_bwd(dqn, qf, rq, gq_ref[...])
            dk_win, dgk = rms_bwd(dkn, kf, rk, gk_ref[...])
            dq_ref[h] = dq.astype(BF16)
            dk_ref[h, 0] = dk_win[:BLOCK]
            dk_ref[h, 1] = dk_win[BLOCK:]
            dv_ref[h, 0] = dv_win[:BLOCK]
            dv_ref[h, 1] = dv_win[BLOCK:]
            dbias_ref[h] += ds
            dsink_ref[h] += jnp.sum((-psink * delta).reshape(GROUP, BLOCK, 1), axis=1)
            dgq_sum = dgq_sum + dgq
            dgk_sum = dgk_sum + dgk
        dgq_ref[...] += dgq_sum
        dgk_ref[...] += dgk_sum

    kv_out = pl.BlockSpec((N_KV_HEADS, None, 2, BLOCK, HEAD_DIM), lambda n: (0, n, 0, 0, 0))
    kv_shape = jax.ShapeDtypeStruct((N_KV_HEADS, nb, 2, BLOCK, HEAD_DIM), F32)
    return pl.pallas_call(
        body, name="attn_bwd", grid=(nb,),
        in_specs=[qspec, kprev, kcur, kprev, kcur, qspec, gain, gain, sink, bspec],
        out_specs=[qspec, kv_out, kv_out, bspec,
                   pl.BlockSpec((N_KV_HEADS, GROUP, 1), lambda n: (0, 0, 0)), gain, gain],
        out_shape=[jax.ShapeDtypeStruct(q4.shape, BF16), kv_shape, kv_shape,
                   jax.ShapeDtypeStruct((N_KV_HEADS, QROWS, 2 * BLOCK), F32),
                   jax.ShapeDtypeStruct((N_KV_HEADS, GROUP, 1), F32),
                   jax.ShapeDtypeStruct((1, HEAD_DIM), F32),
                   jax.ShapeDtypeStruct((1, HEAD_DIM), F32)],
        compiler_params=_params(("arbitrary",)),
    )(q4, k3, k3, v3, v3, do4, gq, gk, sink_rows, bias)


def _kv_window_sum(parts):
    nb = parts.shape[1]

    def body(cur_ref, nxt_ref, o_ref):
        nxt = jnp.where(pl.program_id(0) < nb - 1, nxt_ref[...], 0.0)
        o_ref[...] = (cur_ref[...] + nxt).astype(BF16)

    return pl.pallas_call(
        body, name="kv_window_sum", grid=(nb,),
        in_specs=[pl.BlockSpec((N_KV_HEADS, None, None, BLOCK, HEAD_DIM), lambda n: (0, n, 1, 0, 0)),
                  pl.BlockSpec((N_KV_HEADS, None, None, BLOCK, HEAD_DIM), lambda n: (0, jnp.minimum(n + 1, nb - 1), 0, 0, 0))],
        out_specs=pl.BlockSpec((N_KV_HEADS, None, BLOCK, HEAD_DIM), lambda n: (0, n, 0, 0)),
        out_shape=jax.ShapeDtypeStruct((N_KV_HEADS, nb, BLOCK, HEAD_DIM), BF16),
        compiler_params=_params(("parallel",)),
    )(parts, parts)


def _q_to_heads(cols, t):
    nb = t // BLOCK
    return cols.reshape(nb, BLOCK, N_KV_HEADS, GROUP, HEAD_DIM).transpose(2, 0, 3, 1, 4).reshape(N_KV_HEADS, nb, QROWS, HEAD_DIM)


def _q_from_heads(q4, t):
    nb = t // BLOCK
    return q4.reshape(N_KV_HEADS, nb, GROUP, BLOCK, HEAD_DIM).transpose(1, 3, 0, 2, 4).reshape(t, N_Q_HEADS * HEAD_DIM)


def _kv_to_heads(cols, t):
    return cols.reshape(t, N_KV_HEADS, HEAD_DIM).transpose(1, 0, 2)


def _kv_from_heads(k3, t):
    return k3.transpose(1, 0, 2).reshape(t, N_KV_HEADS * HEAD_DIM)


GATE_TILE = 512


def _merge_fwd(z3, o, p, w_proj, w_o):
    t, d = z3.shape
    tm = min(ROW_TILE, t)
    tn = GATE_TILE

    def body(z_ref, o_ref, gc_ref, ga_ref, wp_ref, wo_ref, m_ref, a_ref, b_ref):
        a = _dot(z_ref[...], wp_ref[...])
        b = _dot(o_ref[...], wo_ref[...])
        m_ref[...] = (_sigmoid(gc_ref[...].astype(F32)) * a + _sigmoid(ga_ref[...].astype(F32)) * b).astype(BF16)
        a_ref[...] = a.astype(BF16)
        b_ref[...] = b.astype(BF16)

    row = pl.BlockSpec((tm, d), lambda i, j: (i, 0))
    wspec = pl.BlockSpec((d, tn), lambda i, j: (0, j))
    ospec = pl.BlockSpec((tm, tn), lambda i, j: (i, j))
    return pl.pallas_call(
        body, name="merge_fwd", grid=(t // tm, d // tn),
        in_specs=[row, row,
                  pl.BlockSpec((tm, tn), lambda i, j: (i, COL_GC // tn + j)),
                  pl.BlockSpec((tm, tn), lambda i, j: (i, COL_GA // tn + j)), wspec, wspec],
        out_specs=[ospec, ospec, ospec],
        out_shape=[jax.ShapeDtypeStruct((t, d), BF16)] * 3,
        compiler_params=_params(("parallel", "parallel")),
    )(z3, o, p, p, w_proj, w_o)


def _merge_bwd(dres, w_out, a, b, p, tokens=()):
    t, d = dres.shape
    tm = min(ROW_TILE, t)
    tn = GATE_TILE

    def epilogue(acc, ex, outs, ids):
        a_ref, b_ref, gc_ref, ga_ref = ex[:4]
        sc = _sigmoid(gc_ref[...].astype(F32))
        sa = _sigmoid(ga_ref[...].astype(F32))
        outs[0][...] = (acc * sc).astype(BF16)
        outs[1][...] = (acc * sa).astype(BF16)
        outs[2][0] = (acc * a_ref[...].astype(F32) * sc * (1.0 - sc)).astype(BF16)
        outs[2][1] = (acc * b_ref[...].astype(F32) * sa * (1.0 - sa)).astype(BF16)

    ospec = pl.BlockSpec((tm, tn), lambda i, j, kk: (i, j))
    return _mm("merge_bwd", (t // tm, d // tn, 1),
               dres, pl.BlockSpec((tm, d), lambda i, j, kk: (i, 0)),
               w_out, pl.BlockSpec((tn, d), lambda i, j, kk: (j, 0)), (tm, tn),
               trans_b=True, a_pre=_to_bf16,
               extras=(a, b, p, p),
               extra_specs=(ospec, ospec,
                            pl.BlockSpec((tm, tn), lambda i, j, kk: (i, COL_GC // tn + j)),
                            pl.BlockSpec((tm, tn), lambda i, j, kk: (i, COL_GA // tn + j))), tokens=tokens,
               out_shape=(jax.ShapeDtypeStruct((t, d), BF16), jax.ShapeDtypeStruct((t, d), BF16),
                          jax.ShapeDtypeStruct((2, t, d), BF16)),
               out_specs=(ospec, ospec, pl.BlockSpec((2, tm, tn), lambda i, j, kk: (0, i, j))),
               epilogue=epilogue)


def _store_epilogue(acc, ex, outs, ids):
    outs[0][...] = acc


def _store_bf16_epilogue(acc, ex, outs, ids):
    outs[0][...] = acc.astype(BF16)


def _mm_nt(name, a, w, out_dtype=BF16):
    t, n = a.shape
    k = w.shape[0]
    tm = min(ROW_TILE, t)
    return _mm(name, (t // tm, 1, 1), a, pl.BlockSpec((tm, n), lambda i, j, kk: (i, 0)),
               w, pl.BlockSpec((k, n), lambda i, j, kk: (0, 0)), (tm, k), trans_b=True,
               out_shape=(jax.ShapeDtypeStruct((t, k), out_dtype),),
               out_specs=(pl.BlockSpec((tm, k), lambda i, j, kk: (i, 0)),),
               epilogue=_store_bf16_epilogue if out_dtype == BF16 else _store_epilogue)[0]


def _mm_tn(name, a, b, b_pre=None):
    t, m = a.shape
    n = b.shape[1]
    tk = min(TOKEN_TILE, t)
    return _mm(name, (1, 1, t // tk), a, pl.BlockSpec((tk, m), lambda i, j, kk: (kk, 0)),
               b, pl.BlockSpec((tk, n), lambda i, j, kk: (kk, 0)), (m, n), trans_a=True, b_pre=b_pre,
               out_shape=(jax.ShapeDtypeStruct((m, n), BF16),),
               out_specs=(pl.BlockSpec((m, n), lambda i, j, kk: (0, 0)),), epilogue=_store_bf16_epilogue)[0]


def _local_step(x, target, small, comm):
    t = x.shape[0]
    w = dict(small)

    n1 = _rmsnorm_fwd("ffn1_norm", x, w["ffn1_norm"])
    w.update(comm.weights("A", n1))
    x1, ffn1_saved = _ffn_fwd("ffn1", x, n1, w["ffn1_w_in"], w["ffn1_w_out"], comm.tokens)
    w.update(comm.weights("B", x1))
    hm = _rmsnorm_fwd("mix_norm", x1, w["mix_norm"], comm.tokens)
    tm = min(ROW_TILE, t)
    p = _mm("mix_in", (N_CHIPS, t // tm, 1),
            hm, pl.BlockSpec((tm, D_MODEL), lambda j, i, kk: (i, 0)),
            w["w_in"], pl.BlockSpec((None, D_MODEL, SHARD_W), lambda j, i, kk: (j, 0, 0)), (tm, SHARD_W),
            out_shape=(jax.ShapeDtypeStruct((t, IN_W), BF16),),
            out_specs=(pl.BlockSpec((tm, SHARD_W), lambda j, i, kk: (i, j)),),
            epilogue=_store_bf16_epilogue)[0]

    z3, z1 = _conv_fwd(p, w["conv_dw_kernel"], w["conv_dw_bias"], w["conv_ln_g"], w["conv_ln_b"])

    onehot = _bucket_onehot()
    bias = _bias_table(w["rel_bias"].T, onehot).reshape(N_KV_HEADS, QROWS, 2 * BLOCK)
    sink_rows = jnp.repeat(w["attn_sinks"].reshape(N_KV_HEADS, GROUP), BLOCK, axis=1)[..., None]
    q4 = _q_to_heads(p[:, COL_Q:COL_K], t)
    k3 = _kv_to_heads(p[:, COL_K:COL_V], t)
    v3 = _kv_to_heads(p[:, COL_V:COL_GC], t)
    o4 = _attn_fwd(q4, k3, v3, w["q_norm"], w["k_norm"], sink_rows, bias)
    o = _q_from_heads(o4, t)

    merged, a, b = _merge_fwd(z3, o, p, w["conv_w_proj"], w["attn_w_o"])
    x2 = _mm_residual("mix_out", merged, w["w_out"], x1, 1.0)
    n2 = _rmsnorm_fwd("ffn2_norm", x2, w["ffn2_norm"])
    w.update(comm.weights("C", n2))
    x3, ffn2_saved = _ffn_fwd("ffn2", x2, n2, w["ffn2_w_in"], w["ffn2_w_out"])
    dy, loss = _loss_head(x3, target)

    g, big = {}, {}
    dres2, big["ffn2_w_in"], big["ffn2_w_out"], g["ffn2_norm"] = _ffn_bwd(
        "ffn2b", dy, x2, w["ffn2_norm"], ffn2_saved, w["ffn2_w_in"], w["ffn2_w_out"])
    tokens = comm.reduce_start("R1", big)

    da, db, dgates = _merge_bwd(dres2, w["w_out"], a, b, p, tokens)
    big = {}
    big["w_out"] = _mm_tn("d_w_out", merged, dres2, b_pre=_to_bf16)
    big["conv_w_proj"] = _mm_tn("d_w_proj", z3, da)
    big["attn_w_o"] = _mm_tn("d_w_o", o, db)
    dz3 = _mm_nt("d_z3", da, w["conv_w_proj"])
    do = _mm_nt("d_o", db, w["attn_w_o"])

    dq4, dk_parts, dv_parts, dbias, dsink, g["q_norm"], g["k_norm"] = _attn_bwd(
        q4, k3, v3, _q_to_heads(do, t), w["q_norm"], w["k_norm"], sink_rows, bias)
    dk3 = _kv_window_sum(dk_parts).reshape(N_KV_HEADS, t, HEAD_DIM)
    dv3 = _kv_window_sum(dv_parts).reshape(N_KV_HEADS, t, HEAD_DIM)
    g["rel_bias"] = _bias_table_bwd(dbias.reshape(N_Q_HEADS, BLOCK * 2 * BLOCK), onehot).T
    g["attn_sinks"] = dsink.reshape(N_Q_HEADS)

    dz1, big["conv_dw_kernel"], g["conv_dw_bias"], g["conv_ln_g"], g["conv_ln_b"] = _conv_bwd_ln(
        p, z1, dz3, w["conv_ln_g"], w["conv_ln_b"])
    dconv = _conv_bwd_glu(p, dz1, w["conv_dw_kernel"])

    dp = jnp.concatenate([dconv, _q_from_heads(dq4, t), _kv_from_heads(dk3, t), _kv_from_heads(dv3, t),
                          dgates[0], dgates[1]], axis=1)
    tk = min(TOKEN_TILE, t)
    big["w_in"] = _mm("d_w_in", (1, N_CHIPS, t // tk),
                    hm, pl.BlockSpec((tk, D_MODEL), lambda i, j, kk: (kk, 0)),
                    dp, pl.BlockSpec((tk, SHARD_W), lambda i, j, kk: (kk, j)), (D_MODEL, SHARD_W),
                    trans_a=True,
                    out_shape=(jax.ShapeDtypeStruct((N_CHIPS, D_MODEL, SHARD_W), BF16),),
                    out_specs=(pl.BlockSpec((None, D_MODEL, SHARD_W), lambda i, j, kk: (j, 0, 0)),),
                    epilogue=_store_bf16_epilogue)[0]
    dres1, g["mix_norm"] = _mm("d_mix", (t // tm, 1, N_CHIPS),
                               dp, pl.BlockSpec((tm, SHARD_W), lambda i, j, kk: (i, kk)),
                               w["w_in"], pl.BlockSpec((None, D_MODEL, SHARD_W), lambda i, j, kk: (kk, 0, 0)),
                               (tm, D_MODEL), trans_b=True,
                               extras=(x1, w["mix_norm"], dres2),
                               extra_specs=(pl.BlockSpec((tm, D_MODEL), lambda i, j, kk: (i, 0)),
                                            pl.BlockSpec((1, D_MODEL), lambda i, j, kk: (0, 0)),
                                            pl.BlockSpec((tm, D_MODEL), lambda i, j, kk: (i, 0))),
                               out_shape=(jax.ShapeDtypeStruct((t, D_MODEL), F32), jax.ShapeDtypeStruct((1, D_MODEL), F32)),
                               out_specs=(pl.BlockSpec((tm, D_MODEL), lambda i, j, kk: (i, 0)),
                                          pl.BlockSpec((1, D_MODEL), lambda i, j, kk: (0, 0))),
                               epilogue=_rms_bwd_epilogue, sem=("arbitrary", "arbitrary", "arbitrary"))

    comm.reduce_finish("R1", dres1)
    tokens = comm.reduce_start("R2", big)

    def ffn1_grads(dw_in4, dw_out):
        comm.reduce_finish("R2", dw_in4)
        return comm.reduce_start("R3", {"ffn1_w_in": dw_in4, "ffn1_w_out": dw_out})

    grad_x, _, _, g["ffn1_norm"] = _ffn_bwd(
        "ffn1b", dres1, x, w["ffn1_norm"], ffn1_saved, w["ffn1_w_in"], w["ffn1_w_out"], tokens, ffn1_grads)
    comm.reduce_finish("R3", grad_x)
    return loss[0, 0], grad_x, g


def _mesh_place():
    x, y, c = lax.axis_index("x"), lax.axis_index("y"), lax.axis_index("c")
    chips = [(1 - x, y), (x, 1 - y), (1 - x, 1 - y)]
    return x, y, c, chips


def _any_specs(n):
    return [pl.BlockSpec(memory_space=pl.ANY)] * n


HBM_SPEC = pl.BlockSpec(memory_space=pltpu.HBM)
SEM_SPEC = pl.BlockSpec(memory_space=pltpu.SEMAPHORE)
EFFECT = pltpu.SideEffectType.DATAFLOW_SIDE_EFFECTING


def _in_hbm(a):
    return pltpu.with_memory_space_constraint(a, pltpu.HBM)


def _copy_start(name, srcs, lands, plan, after=()):
    ns, nb = len(srcs), len(lands)
    n = 3 * ns

    def body(*refs):
        s_refs, l_refs = refs[:ns], refs[ns:ns + nb]
        send_sems, recv_sems = refs[ns + nb + len(after)], refs[ns + nb + len(after) + 1]
        token = refs[-1]
        for k, (src, dst, to, _) in enumerate(plan(s_refs, l_refs)):
            pltpu.make_async_remote_copy(src_ref=src, dst_ref=dst, send_sem=send_sems.at[k], recv_sem=recv_sems.at[k],
                                         device_id=to, device_id_type=MESH).start()
        token[...] = jnp.zeros_like(token)

    bufs = list(srcs) + list(lands)
    outs = pl.pallas_call(
        body, name=name,
        out_shape=(pltpu.SemaphoreType.DMA((n,)), pltpu.SemaphoreType.DMA((n,)),
                   *[pltpu.HBM(a.shape, a.dtype) for a in bufs], jax.ShapeDtypeStruct((8, LANES), F32)),
        in_specs=[HBM_SPEC] * len(bufs) + [pl.BlockSpec(memory_space=pl.ANY)] * len(after),
        out_specs=(SEM_SPEC, SEM_SPEC, *[HBM_SPEC] * len(bufs), pl.BlockSpec(memory_space=pltpu.VMEM)),
        input_output_aliases={i: 2 + i for i in range(len(bufs))},
        compiler_params=pltpu.CompilerParams(has_side_effects=EFFECT),
    )(*[_in_hbm(a) for a in bufs], *after)
    return outs[0], outs[1], list(outs[2:2 + ns]), list(outs[2 + ns:2 + ns + nb]), outs[-1]


def _copy_wait(name, send_sems, recv_sems, srcs, lands, after, plan):
    ns, nb = len(srcs), len(lands)

    def body(*refs):
        s_refs, l_refs = refs[:ns], refs[ns:ns + nb]
        send_sems, recv_sems = refs[ns + nb], refs[ns + nb + 1]
        for k, (src, _, to, mine) in enumerate(plan(s_refs, l_refs)):
            cp = pltpu.make_async_remote_copy(src_ref=src, dst_ref=mine, send_sem=send_sems.at[k], recv_sem=recv_sems.at[k],
                                              device_id=to, device_id_type=MESH)
            cp.wait_send()
            cp.wait_recv()

    bufs = list(srcs) + list(lands)
    outs = pl.pallas_call(
        body, name=name,
        out_shape=tuple(pltpu.HBM(a.shape, a.dtype) for a in bufs),
        in_specs=[HBM_SPEC] * len(bufs) + [SEM_SPEC, SEM_SPEC, pl.BlockSpec(memory_space=pl.ANY)],
        out_specs=tuple([HBM_SPEC] * len(bufs)),
        input_output_aliases={i: i for i in range(len(bufs))},
        compiler_params=pltpu.CompilerParams(has_side_effects=EFFECT),
    )(*bufs, send_sems, recv_sems, after)
    return list(outs[:ns]), list(outs[ns:])


def _gather_plan(s_refs, l_refs):
    x, y, c, chips = _mesh_place()
    jme = 2 * x + y
    return [(s.at[c], land.at[jme, c], (*chip, c), land.at[2 * chip[0] + chip[1], c])
            for s, land in zip(s_refs, l_refs) for chip in chips]


def _scatter_plan(s_refs, l_refs):
    x, y, c, chips = _mesh_place()
    return [(s.at[2 * chip[0] + chip[1]], land.at[k], (*chip, c), land.at[k])
            for s, land in zip(s_refs, l_refs) for k, chip in enumerate(chips)]


def _gather_forward(name, shards, landed):
    nw = len(shards)

    def body(*refs):
        s_refs, o_refs = refs[:nw], refs[2 * nw:3 * nw]
        send_sems, recv_sems = refs[3 * nw:]
        x, y, c, chips = _mesh_place()
        me, sib, jme = (x, y, c), (x, y, 1 - c), 2 * x + y
        sent = []
        for w in range(nw):
            parts = [(o_refs[w].at[2 * chip[0] + chip[1], c], o_refs[w].at[2 * chip[0] + chip[1], c]) for chip in chips]
            parts.append((s_refs[w], o_refs[w].at[jme]))
            for k, (src, dst) in enumerate(parts):
                cp = pltpu.make_async_remote_copy(src_ref=src, dst_ref=dst, send_sem=send_sems.at[4 * w + k],
                                                  recv_sem=recv_sems.at[4 * w + k], device_id=sib, device_id_type=MESH)
                cp.start()
                sent.append(cp)
        for w in range(nw):
            parts = [o_refs[w].at[2 * chip[0] + chip[1], 1 - c] for chip in chips] + [o_refs[w].at[jme]]
            for k, part in enumerate(parts):
                pltpu.make_async_remote_copy(src_ref=part, dst_ref=part, send_sem=send_sems.at[4 * w + k],
                                             recv_sem=recv_sems.at[4 * w + k], device_id=me, device_id_type=MESH).wait_recv()
        for cp in sent:
            cp.wait_send()

    return pl.pallas_call(
        body, name=name,
        in_specs=_any_specs(2 * nw), out_specs=_any_specs(nw),
        out_shape=[jax.ShapeDtypeStruct(a.shape, a.dtype) for a in landed],
        input_output_aliases={nw + i: i for i in range(nw)},
        scratch_shapes=[pltpu.SemaphoreType.DMA((4 * nw,)), pltpu.SemaphoreType.DMA((4 * nw,))],
    )(*shards, *landed)


def _exchange_halves(name, grads, after=()):
    nw = len(grads)

    def body(*refs):
        g_refs, o_refs = refs[:nw], refs[nw + len(after):2 * nw + len(after)]
        send_sems, recv_sems = refs[2 * nw + len(after):]
        x, y, c, _ = _mesh_place()
        copies = []
        for w in range(nw):
            cp = pltpu.make_async_remote_copy(src_ref=g_refs[w].at[:, 1 - c], dst_ref=o_refs[w], send_sem=send_sems.at[w],
                                              recv_sem=recv_sems.at[w], device_id=(x, y, 1 - c), device_id_type=MESH)
            cp.start()
            copies.append(cp)
        for cp in copies:
            cp.wait()

    return pl.pallas_call(
        body, name=name,
        in_specs=_any_specs(nw + len(after)), out_specs=_any_specs(nw),
        out_shape=[jax.ShapeDtypeStruct((N_CHIPS,) + g.shape[2:], g.dtype) for g in grads],
        scratch_shapes=[pltpu.SemaphoreType.DMA((nw,)), pltpu.SemaphoreType.DMA((nw,))],
    )(*grads, *after)


def _row_tile(r):
    for cand in (256, 176, 128, 64, 32, 16, 8):
        if r % cand == 0:
            return cand
    return r


def _add_own_half(c_idx, grad, got):
    _, _, r, cols = grad.shape
    tr = _row_tile(r)

    def body(c_ref, g_ref, o_ref, out_ref):
        out_ref[...] = (g_ref[...].astype(F32) + o_ref[...].astype(F32)).astype(BF16)

    return pl.pallas_call(
        body, name="add_own_half",
        grid_spec=pltpu.PrefetchScalarGridSpec(
            num_scalar_prefetch=1, grid=(N_CHIPS, r // tr),
            in_specs=[pl.BlockSpec((None, None, tr, cols), lambda j, i, c_ref: (j, c_ref[0], i, 0)),
                      pl.BlockSpec((None, tr, cols), lambda j, i, c_ref: (j, i, 0))],
            out_specs=pl.BlockSpec((None, tr, cols), lambda j, i, c_ref: (j, i, 0))),
        out_shape=jax.ShapeDtypeStruct((N_CHIPS, r, cols), BF16),
        compiler_params=_params(("parallel", "parallel")),
    )(c_idx, grad, got)


def _sum_pieces(place_idx, sums, landed):
    _, r, cols = sums.shape
    tr = _row_tile(r)

    def body(j_ref, own_ref, p_ref, o_ref):
        o_ref[...] = ((own_ref[...].astype(F32) + p_ref[0].astype(F32)) + p_ref[1].astype(F32)) + p_ref[2].astype(F32)

    return pl.pallas_call(
        body, name="sum_pieces",
        grid_spec=pltpu.PrefetchScalarGridSpec(
            num_scalar_prefetch=1, grid=(r // tr,),
            in_specs=[pl.BlockSpec((None, tr, cols), lambda i, j_ref: (j_ref[0], i, 0)),
                      pl.BlockSpec((N_CHIPS - 1, tr, cols), lambda i, j_ref: (0, i, 0))],
            out_specs=pl.BlockSpec((None, tr, cols), lambda i, j_ref: (j_ref[1], i, 0))),
        out_shape=jax.ShapeDtypeStruct((2, r, cols), F32),
        compiler_params=_params(("parallel",)),
    )(place_idx, sums, landed)


def _join_halves(name, halves):
    nw = len(halves)

    def body(*refs):
        o_refs = refs[nw:2 * nw]
        send_sems, recv_sems = refs[2 * nw:]
        x, y, c, _ = _mesh_place()
        copies = []
        for w in range(nw):
            cp = pltpu.make_async_remote_copy(src_ref=o_refs[w].at[c], dst_ref=o_refs[w].at[c], send_sem=send_sems.at[w],
                                              recv_sem=recv_sems.at[w], device_id=(x, y, 1 - c), device_id_type=MESH)
            cp.start()
            copies.append(cp)
        for w in range(nw):
            copies[w].wait_send()
            landed = o_refs[w].at[1 - c]
            pltpu.make_async_remote_copy(src_ref=landed, dst_ref=landed, send_sem=send_sems.at[w], recv_sem=recv_sems.at[w],
                                         device_id=(x, y, c), device_id_type=MESH).wait_recv()

    return pl.pallas_call(
        body, name=name,
        in_specs=_any_specs(nw), out_specs=_any_specs(nw),
        out_shape=[jax.ShapeDtypeStruct(h.shape, F32) for h in halves],
        input_output_aliases={i: i for i in range(nw)},
        scratch_shapes=[pltpu.SemaphoreType.DMA((nw,)), pltpu.SemaphoreType.DMA((nw,))],
    )(*halves)


SMALL_ROWS = 8


def _all_reduce_small(pack):
    rows, cols = pack.shape
    n_dev = 8

    def body(p_ref, o_ref, slots, send_sems, recv_sems):
        x, y, c, _ = _mesh_place()
        me = 4 * x + 2 * y + c
        slots[me] = p_ref[...]
        copies = []
        for k in range(1, n_dev):
            peer = (me + k) % n_dev
            cp = pltpu.make_async_remote_copy(src_ref=p_ref, dst_ref=slots.at[me], send_sem=send_sems.at[k],
                                              recv_sem=recv_sems.at[k],
                                              device_id=(peer // 4, (peer // 2) % 2, peer % 2), device_id_type=MESH)
            cp.start()
            copies.append(cp)
        for k in range(1, n_dev):
            src = (me + n_dev - k) % n_dev
            pltpu.make_async_remote_copy(src_ref=p_ref, dst_ref=slots.at[src], send_sem=send_sems.at[k],
                                         recv_sem=recv_sems.at[k], device_id=(x, y, c), device_id_type=MESH).wait_recv()
        for cp in copies:
            cp.wait_send()
        total = slots[0]
        for s in range(1, n_dev):
            total = total + slots[s]
        o_ref[...] = total

    return pl.pallas_call(
        body, name="all_reduce_small",
        in_specs=[pl.BlockSpec(memory_space=pltpu.VMEM)], out_specs=pl.BlockSpec(memory_space=pltpu.VMEM),
        out_shape=jax.ShapeDtypeStruct((rows, cols), F32),
        scratch_shapes=[pltpu.VMEM((n_dev, rows, cols), F32), pltpu.SemaphoreType.DMA((n_dev,)),
                        pltpu.SemaphoreType.DMA((n_dev,))],
    )(pack)


def _adamw(name, w, g, m, v):
    r, cols = w.shape
    tr = _row_tile(r)

    def body(w_ref, g_ref, m_ref, v_ref, d_ref, nm_ref, nv_ref):
        gv = g_ref[...]
        nm = ADAM_B1 * m_ref[...] + (1.0 - ADAM_B1) * gv
        nv = ADAM_B2 * v_ref[...] + (1.0 - ADAM_B2) * (gv * gv)
        m_hat = nm / (1.0 - ADAM_B1 ** ADAM_STEP)
        v_hat = nv / (1.0 - ADAM_B2 ** ADAM_STEP)
        d_ref[...] = -ADAM_LR * (m_hat / (jnp.sqrt(v_hat) + ADAM_EPS) + ADAM_WD * w_ref[...])
        nm_ref[...] = nm
        nv_ref[...] = nv

    spec = pl.BlockSpec((tr, cols), lambda i: (i, 0))
    return pl.pallas_call(
        body, name=name, grid=(r // tr,),
        in_specs=[spec] * 4, out_specs=[spec] * 3,
        out_shape=[jax.ShapeDtypeStruct((r, cols), F32)] * 3,
        compiler_params=_params(("parallel",)),
    )(w, g, m, v)


BIG = ["ffn1_w_in", "ffn1_w_out", "w_in", "conv_w_proj", "attn_w_o", "w_out", "ffn2_w_in", "ffn2_w_out", "conv_dw_kernel"]
COL_SHARDED = ("ffn1_w_in", "w_in", "ffn2_w_in")
SMALL = ["ffn1_norm", "mix_norm", "ffn2_norm", "conv_dw_bias", "conv_ln_g", "conv_ln_b", "q_norm", "k_norm", "attn_sinks", "rel_bias"]
WEIGHTS = ["ffn1_norm", "ffn1_w_in", "ffn1_w_out", "mix_norm", "w_in", "conv_dw_kernel", "conv_dw_bias", "conv_ln_g",
           "conv_ln_b", "conv_w_proj", "q_norm", "k_norm", "attn_sinks", "rel_bias", "attn_w_o", "w_out", "ffn2_norm",
           "ffn2_w_in", "ffn2_w_out"]
SMALL_PLACE = {"ffn1_norm": (0, 0, 1024), "mix_norm": (1, 0, 1024), "ffn2_norm": (2, 0, 1024), "conv_dw_bias": (3, 0, 1024),
               "conv_ln_g": (4, 0, 1024), "conv_ln_b": (5, 0, 1024), "q_norm": (6, 0, 64), "k_norm": (6, 128, 64),
               "attn_sinks": (6, 256, 16), "rel_bias": (7, 0, 512)}
LOSS_PLACE = (6, 384)


def _pack_small(vals, fill=0.0, loss=None):
    pack = jnp.full((SMALL_ROWS, D_MODEL), fill, F32)
    for name, (row, lane, n) in SMALL_PLACE.items():
        pack = pack.at[row, lane:lane + n].set(vals[name].reshape(n))
    if loss is not None:
        pack = pack.at[LOSS_PLACE[0], LOSS_PLACE[1]].set(loss)
    return pack


def _unpack_small(pack, shapes):
    return {name: pack[row, lane:lane + n].reshape(shapes[name]) for name, (row, lane, n) in SMALL_PLACE.items()}


def _shard_halves(name, a):
    if name == "conv_dw_kernel":
        a = jnp.pad(a, ((0, CONV_PAD - CONV_WIDTH), (0, 0)))
    r, cols = a.shape
    return a.reshape(2, r // 2, cols)


GATHER_GROUPS = {"A": ["ffn1_w_in", "ffn1_w_out"],
                 "B": ["w_in", "conv_dw_kernel", "conv_w_proj", "attn_w_o", "w_out"],
                 "C": ["ffn2_w_in", "ffn2_w_out"]}


class _MeshComm:
    def __init__(self, wts):
        self.c_idx = lax.axis_index("c").astype(jnp.int32).reshape(1)
        self.place_idx = jnp.stack([2 * lax.axis_index("x") + lax.axis_index("y"), lax.axis_index("c")]).astype(jnp.int32)
        self.wts, self.gathers, self.reductions, self.reduced = wts, {}, {}, {}
        self.tokens, self.last_join = (), ()
        self._gather_start("A", ())

    def _gather_start(self, group, after):
        names = GATHER_GROUPS[group]
        shards = [_shard_halves(n, self.wts[n]) if n == "conv_dw_kernel" else _shard_halves(n, self.wts[n]).astype(BF16)
                  for n in names]
        lands = [lax.empty((N_CHIPS,) + s.shape, s.dtype) for s in shards]
        self.gathers[group] = _copy_start("gather_start_" + group, shards, lands, _gather_plan, after=after)
        self.tokens = (self.gathers[group][-1],)

    def weights(self, group, after):
        send_sems, recv_sems, shards, lands, token = self.gathers.pop(group)
        shards, lands = _copy_wait("gather_wait_" + group, send_sems, recv_sems, shards, lands,
                                   token if after is None else after, _gather_plan)
        gathered = _gather_forward("gather_forward_" + group, shards, lands)
        self.tokens = ()
        following = {"A": "B", "B": "C"}.get(group)
        if following:
            self._gather_start(following, (gathered[0],))
        out = {}
        for n, g4 in zip(GATHER_GROUPS[group], gathered):
            r, cols = g4.shape[2] * 2, g4.shape[3]
            if n in COL_SHARDED:
                out[n] = g4.reshape(N_CHIPS, r, cols)
            elif n == "conv_dw_kernel":
                out[n] = g4.reshape(N_CHIPS, r, cols).transpose(1, 0, 2).reshape(r, N_CHIPS * cols)
            else:
                out[n] = g4.reshape(N_CHIPS * r, cols)
        return out

    def reduce_start(self, group, grads):
        names = list(grads)
        g4 = []
        for n in names:
            a = grads[n]
            if n == "conv_dw_kernel":
                a = a.reshape(CONV_PAD, N_CHIPS, -1).transpose(1, 0, 2)
            elif n not in COL_SHARDED:
                a = a.reshape(N_CHIPS, a.shape[0] // N_CHIPS, a.shape[1])
            g4.append(a.reshape(N_CHIPS, 2, a.shape[1] // 2, a.shape[2]))
        got = _exchange_halves("exchange_halves_" + group, g4, after=self.last_join)
        sums = [_add_own_half(self.c_idx, a, b) for a, b in zip(g4, got)]
        lands = [lax.empty((N_CHIPS - 1,) + s.shape[1:], s.dtype) for s in sums]
        started = _copy_start("scatter_start_" + group, sums, lands, _scatter_plan, after=self.last_join)
        self.reductions[group] = (names,) + started
        return (started[-1],)

    def reduce_finish(self, group, after):
        names, send_sems, recv_sems, sums, lands, _ = self.reductions.pop(group)
        sums, lands = _copy_wait("scatter_wait_" + group, send_sems, recv_sems, sums, lands, after, _scatter_plan)
        halves = [_sum_pieces(self.place_idx, s, p) for s, p in zip(sums, lands)]
        joined = _join_halves("join_halves_" + group, halves)
        self.last_join = (joined[0],)
        self.reduced.update(zip(names, joined))


def kernel(x, ffn1_norm, ffn1_w_in, ffn1_w_out, mix_norm, w_in, conv_dw_kernel, conv_dw_bias, conv_ln_g, conv_ln_b, conv_w_proj, q_norm, k_norm, attn_sinks, rel_bias, attn_w_o, w_out, ffn2_norm, ffn2_w_in, ffn2_w_out, loss_target, m_ffn1_norm, m_ffn1_w_in, m_ffn1_w_out, m_mix_norm, m_w_in, m_conv_dw_kernel, m_conv_dw_bias, m_conv_ln_g, m_conv_ln_b, m_conv_w_proj, m_q_norm, m_k_norm, m_attn_sinks, m_rel_bias, m_attn_w_o, m_w_out, m_ffn2_norm, m_ffn2_w_in, m_ffn2_w_out, v_ffn1_norm, v_ffn1_w_in, v_ffn1_w_out, v_mix_norm, v_w_in, v_conv_dw_kernel, v_conv_dw_bias, v_conv_ln_g, v_conv_ln_b, v_conv_w_proj, v_q_norm, v_k_norm, v_attn_sinks, v_rel_bias, v_attn_w_o, v_w_out, v_ffn2_norm, v_ffn2_w_in, v_ffn2_w_out):
    args = dict(locals())
    wts = {n: args[n] for n in WEIGHTS}
    mom = {n: args["m_" + n] for n in WEIGHTS}
    var = {n: args["v_" + n] for n in WEIGHTS}
    comm = _MeshComm(wts)
    small = {n: wts[n] if n in ("attn_sinks", "rel_bias") else wts[n].reshape(1, -1) for n in SMALL}
    loss_part, grad_x, g = _local_step(x[0], loss_target[0], small, comm)

    small_sum = _all_reduce_small(_pack_small(g, loss=loss_part))
    loss = small_sum[LOSS_PLACE[0], LOSS_PLACE[1]]
    small_shapes = {n: wts[n].shape for n in SMALL}
    g_small = _unpack_small(small_sum, small_shapes)

    grads, delta, new_m, new_v = {}, {}, {}, {}
    for n in BIG:
        j = comm.reduced[n]
        gs = j.reshape(j.shape[1] * 2, j.shape[2])
        pad = n == "conv_dw_kernel"
        ws, ms, vs = (_shard_halves(n, a).reshape(gs.shape) for a in (wts[n], mom[n], var[n]))
        d, nm, nv = _adamw("adamw_" + n, ws, gs, ms, vs)
        cut = (lambda a: a[:CONV_WIDTH]) if pad else (lambda a: a)
        grads[n], delta[n], new_m[n], new_v[n] = cut(gs), cut(d), cut(nm), cut(nv)
    d, nm, nv = _adamw("adamw_small", _pack_small(wts), small_sum, _pack_small(mom), _pack_small(var, fill=1.0))
    grads.update(g_small)
    delta.update(_unpack_small(d, small_shapes))
    new_m.update(_unpack_small(nm, small_shapes))
    new_v.update(_unpack_small(nv, small_shapes))

    return (loss, grad_x[None], *[grads[n] for n in WEIGHTS], *[delta[n] for n in WEIGHTS],
            *[new_m[n] for n in WEIGHTS], *[new_v[n] for n in WEIGHTS])
```

```python
import functools
import math

import jax
import jax.numpy as jnp
from jax import lax
from jax.experimental import pallas as pl
from jax.experimental.pallas import tpu as pltpu

F32 = jnp.float32
BF16 = jnp.bfloat16
MESH = pl.DeviceIdType.MESH

EPS = 1e-6
D_MODEL = 1024
D_FF = 2816
N_CHIPS = 4
SHARD_W = 2 * D_FF // N_CHIPS
HEAD_DIM = 64
N_Q_HEADS = 16
N_KV_HEADS = 4
GROUP = N_Q_HEADS // N_KV_HEADS
BLOCK = 128
QROWS = GROUP * BLOCK
N_BUCKETS = 32
MAX_DISTANCE = 128
CONV_WIDTH = 31
CONV_PAD = 32
NEG = float(jnp.finfo(jnp.float32).min)

ADAM_LR = 0.001
ADAM_B1 = 0.9
ADAM_B2 = 0.999
ADAM_EPS = 1e-08
ADAM_WD = 0.01
ADAM_STEP = 10

VMEM_LIMIT_BYTES = 56 * 1024 * 1024
ROW_TILE = 1024
TOKEN_TILE = 1024
CONV_TILE = 256
CONV_ROWS = 128
LANES = 128

COL_CONV_A, COL_CONV_G, COL_Q, COL_K, COL_V, COL_GC, COL_GA = 0, 1024, 2048, 3072, 3328, 3584, 4608
IN_W = 5632


def _params(sem, vmem=VMEM_LIMIT_BYTES):
    return pltpu.CompilerParams(dimension_semantics=sem, vmem_limit_bytes=vmem)


def _sigmoid(x):
    return 1.0 / (1.0 + jnp.exp(-x))


def _dot(a, b, trans_a=False, trans_b=False, precision=None):
    dn = (((0,) if trans_a else (1,), (1,) if trans_b else (0,)), ((), ()))
    return lax.dot_general(a, b, dn, preferred_element_type=F32, precision=precision)


def _mm(name, grid, a, a_spec, b, b_spec, acc_shape, *, trans_a=False, trans_b=False, a_pre=None, b_pre=None,
        extras=(), extra_specs=(), tokens=(), out_shape, out_specs, epilogue, sem=("parallel", "parallel", "arbitrary")):
    n_k = grid[2]
    extras = tuple(extras) + tuple(tokens)
    extra_specs = tuple(extra_specs) + (pl.BlockSpec((8, LANES), lambda i, j, kk: (0, 0)),) * len(tokens)
    n_extra = len(extras)
    n_out = len(out_shape)

    def body(a_ref, b_ref, *rest):
        ex = rest[:n_extra]
        outs = rest[n_extra:n_extra + n_out]
        ids = (pl.program_id(0), pl.program_id(1), pl.program_id(2))
        av = a_ref[...]
        bv = b_ref[...]
        if a_pre is not None:
            av = a_pre(av)
        if b_pre is not None:
            bv = b_pre(bv)
        part = _dot(av, bv, trans_a, trans_b)
        if n_k == 1:
            epilogue(part, ex, outs, ids)
        else:
            acc = rest[-1]

            @pl.when(ids[2] == 0)
            def _():
                acc[...] = part

            @pl.when(ids[2] > 0)
            def _():
                acc[...] += part

            @pl.when(ids[2] == n_k - 1)
            def _():
                epilogue(acc[...], ex, outs, ids)

    scratch = [] if n_k == 1 else [pltpu.VMEM(acc_shape, F32)]
    return pl.pallas_call(
        body, name=name, grid=grid,
        in_specs=[a_spec, b_spec, *extra_specs],
        out_specs=list(out_specs), out_shape=list(out_shape),
        scratch_shapes=scratch, compiler_params=_params(sem),
    )(a, b, *extras)


def _half_bf16(v):
    return (0.5 * v).astype(BF16)


def _to_bf16(v):
    return v.astype(BF16)


def _rmsnorm_fwd(name, x, g, tokens=()):
    t, d = x.shape
    tm = min(ROW_TILE, t)

    def body(x_ref, g_ref, *rest):
        o_ref = rest[-1]
        xv = x_ref[...]
        r = lax.rsqrt(jnp.mean(xv * xv, axis=-1, keepdims=True) + EPS)
        o_ref[...] = (xv * r * g_ref[...]).astype(BF16)

    return pl.pallas_call(
        body, name=name, grid=(t // tm,),
        in_specs=[pl.BlockSpec((tm, d), lambda i: (i, 0)), pl.BlockSpec((1, d), lambda i: (0, 0))]
        + [pl.BlockSpec((8, LANES), lambda i: (0, 0))] * len(tokens),
        out_specs=pl.BlockSpec((tm, d), lambda i: (i, 0)),
        out_shape=jax.ShapeDtypeStruct((t, d), BF16),
        compiler_params=_params(("parallel",)),
    )(x, g, *tokens)


def _rms_bwd_epilogue(acc, ex, outs, ids):
    x_ref, g_ref, dres_ref = ex[:3]
    out_ref, dg_ref = outs
    xv = x_ref[...]
    r = lax.rsqrt(jnp.mean(xv * xv, axis=-1, keepdims=True) + EPS)
    w = acc * g_ref[...]
    dx = r * w - xv * (r * r * r) * jnp.mean(xv * w, axis=-1, keepdims=True)
    out_ref[...] = dres_ref[...] + dx
    part = jnp.sum(acc * (xv * r), axis=0, keepdims=True)

    @pl.when(ids[0] == 0)
    def _():
        dg_ref[...] = part

    @pl.when(ids[0] > 0)
    def _():
        dg_ref[...] += part


def _ffn_in(name, n, w_in4, tokens=()):
    t, d = n.shape
    tm = min(ROW_TILE, t)

    def body(n_ref, wa_ref, wb_ref, *rest):
        ab_ref, h_ref = rest[-2:]
        nv = n_ref[...]
        a = _dot(nv, wa_ref[...])
        b = _dot(nv, wb_ref[...])
        h_ref[...] = (a * _sigmoid(a) * b).astype(BF16)
        ab_ref[0] = a.astype(BF16)
        ab_ref[1] = b.astype(BF16)

    return pl.pallas_call(
        body, name=name, grid=(2, t // tm),
        in_specs=[pl.BlockSpec((tm, d), lambda j, i: (i, 0)),
                  pl.BlockSpec((None, d, SHARD_W), lambda j, i: (j, 0, 0)),
                  pl.BlockSpec((None, d, SHARD_W), lambda j, i: (j + 2, 0, 0))]
        + [pl.BlockSpec((8, LANES), lambda j, i: (0, 0))] * len(tokens),
        out_specs=[pl.BlockSpec((2, tm, SHARD_W), lambda j, i: (0, i, j)),
                   pl.BlockSpec((tm, SHARD_W), lambda j, i: (i, j))],
        out_shape=[jax.ShapeDtypeStruct((2, t, D_FF), BF16), jax.ShapeDtypeStruct((t, D_FF), BF16)],
        compiler_params=_params(("parallel", "parallel")),
    )(n, w_in4, w_in4, *tokens)


def _mm_residual(name, a, w, res, scale):
    t, k = a.shape
    n = w.shape[1]
    tm = min(ROW_TILE, t)

    def epilogue(acc, ex, outs, ids):
        outs[0][...] = ex[0][...] + scale * acc

    return _mm(name, (t // tm, 1, 1), a, pl.BlockSpec((tm, k), lambda i, j, kk: (i, 0)),
               w, pl.BlockSpec((k, n), lambda i, j, kk: (0, 0)), (tm, n),
               extras=(res,), extra_specs=(pl.BlockSpec((tm, n), lambda i, j, kk: (i, 0)),),
               out_shape=(jax.ShapeDtypeStruct((t, n), F32),),
               out_specs=(pl.BlockSpec((tm, n), lambda i, j, kk: (i, 0)),), epilogue=epilogue)[0]


def _ffn_fwd(tag, x, n, w_in4, w_out, tokens=()):
    ab, h = _ffn_in(tag + "_in", n, w_in4, tokens)
    y = _mm_residual(tag + "_out", h, w_out, x, 0.5)
    return y, (n, ab, h)


def _ffn_bwd(tag, dres, x, g, saved, w_in4, w_out, tokens=(), on_weight_grads=None):
    n, ab, h = saved
    t, d = x.shape
    tm = min(ROW_TILE, t)
    tk = min(TOKEN_TILE, t)
    half_w = SHARD_W

    def dact_epilogue(acc, ex, outs, ids):
        a = ex[0][0].astype(F32)
        b = ex[0][1].astype(F32)
        sig = _sigmoid(a)
        outs[0][0] = (acc * b * (sig * (1.0 + a * (1.0 - sig)))).astype(BF16)
        outs[0][1] = (acc * (a * sig)).astype(BF16)

    du = _mm(tag + "_dact", (2, t // tm, 1),
             dres, pl.BlockSpec((tm, d), lambda j, i, kk: (i, 0)),
             w_out, pl.BlockSpec((half_w, d), lambda j, i, kk: (j, 0)), (tm, half_w),
             trans_b=True, a_pre=_half_bf16,
             extras=(ab,), extra_specs=(pl.BlockSpec((2, tm, half_w), lambda j, i, kk: (0, i, j)),), tokens=tokens,
             out_shape=(jax.ShapeDtypeStruct((2, t, D_FF), BF16),),
             out_specs=(pl.BlockSpec((2, tm, half_w), lambda j, i, kk: (0, i, j)),),
             epilogue=dact_epilogue)[0]

    def store_epilogue(acc, ex, outs, ids):
        outs[0][...] = acc.astype(BF16)

    dw_out = _mm(tag + "_dwout", (2, 1, t // tk),
                 h, pl.BlockSpec((tk, half_w), lambda i, j, kk: (kk, i)),
                 dres, pl.BlockSpec((tk, d), lambda i, j, kk: (kk, 0)), (half_w, d),
                 trans_a=True, b_pre=_half_bf16,
                 out_shape=(jax.ShapeDtypeStruct((D_FF, d), BF16),),
                 out_specs=(pl.BlockSpec((half_w, d), lambda i, j, kk: (i, 0)),),
                 epilogue=store_epilogue)[0]

    dw_in4 = _mm(tag + "_dwin", (1, N_CHIPS, t // tk),
                 n, pl.BlockSpec((tk, d), lambda i, j, kk: (kk, 0)),
                 du, pl.BlockSpec((None, tk, SHARD_W), lambda i, j, kk: (j // 2, kk, j % 2)), (d, SHARD_W),
                 trans_a=True,
                 out_shape=(jax.ShapeDtypeStruct((N_CHIPS, d, SHARD_W), BF16),),
                 out_specs=(pl.BlockSpec((None, d, SHARD_W), lambda i, j, kk: (j, 0, 0)),),
                 epilogue=store_epilogue)[0]

    late = () if on_weight_grads is None else on_weight_grads(dw_in4, dw_out)

    dx, dg = _mm(tag + "_dn", (t // tm, 1, N_CHIPS),
                 du, pl.BlockSpec((None, tm, SHARD_W), lambda i, j, kk: (kk // 2, i, kk % 2)),
                 w_in4, pl.BlockSpec((None, d, SHARD_W), lambda i, j, kk: (kk, 0, 0)), (tm, d),
                 trans_b=True,
                 extras=(x, g, dres),
                 extra_specs=(pl.BlockSpec((tm, d), lambda i, j, kk: (i, 0)),
                              pl.BlockSpec((1, d), lambda i, j, kk: (0, 0)),
                              pl.BlockSpec((tm, d), lambda i, j, kk: (i, 0))), tokens=late,
                 out_shape=(jax.ShapeDtypeStruct((t, d), F32), jax.ShapeDtypeStruct((1, d), F32)),
                 out_specs=(pl.BlockSpec((tm, d), lambda i, j, kk: (i, 0)),
                            pl.BlockSpec((1, d), lambda i, j, kk: (0, 0))),
                 epilogue=_rms_bwd_epilogue, sem=("arbitrary", "arbitrary", "arbitrary"))
    return dx, dw_in4, dw_out, dg


def _loss_head(y, target):
    t, d = y.shape
    tm = min(ROW_TILE, t)

    def body(y_ref, t_ref, dy_ref, loss_ref):
        diff = y_ref[...] - t_ref[...]
        dy_ref[...] = diff * (1.0 / d)
        part = jnp.full((8, LANES), 0.5 / d * jnp.sum(diff * diff), F32)
        i = pl.program_id(0)

        @pl.when(i == 0)
        def _():
            loss_ref[...] = part

        @pl.when(i > 0)
        def _():
            loss_ref[...] += part

    return pl.pallas_call(
        body, name="loss_head", grid=(t // tm,),
        in_specs=[pl.BlockSpec((tm, d), lambda i: (i, 0)), pl.BlockSpec((tm, d), lambda i: (i, 0))],
        out_specs=[pl.BlockSpec((tm, d), lambda i: (i, 0)), pl.BlockSpec((8, LANES), lambda i: (0, 0))],
        out_shape=[jax.ShapeDtypeStruct((t, d), F32), jax.ShapeDtypeStruct((8, LANES), F32)],
        compiler_params=_params(("arbitrary",)),
    )(y, target)


def _conv_fill(zp_ref, a_ref, g_ref, ah_ref, gh_ref, i):
    zh = ah_ref[...].astype(F32) * _sigmoid(gh_ref[...].astype(F32))
    zp_ref[pl.ds(0, CONV_PAD), :] = jnp.where(i > 0, zh, 0.0)
    zp_ref[pl.ds(CONV_PAD, a_ref.shape[0]), :] = a_ref[...].astype(F32) * _sigmoid(g_ref[...].astype(F32))


def _shift_groups(shifts):
    groups = {}
    for j, s in shifts:
        groups.setdefault(s % 8, []).append((j, s // 8))
    return groups


def _windows(zp_ref, r0, lanes, groups):
    for q, taps in groups.items():
        deepest = max(p for _, p in taps)
        win = zp_ref[pl.ds(r0 + q, 8 * deepest + CONV_ROWS), lanes]
        for j, p in taps:
            yield j, win[8 * p:8 * p + CONV_ROWS]


def _conv_apply(zp_ref, out_ref, dw_ref, bias_ref, tm, ch, shifts):
    groups = _shift_groups(shifts)
    for cc in range(ch // LANES):
        lanes = pl.ds(cc * LANES, LANES)
        w = [dw_ref[pl.ds(j, 1), lanes] for j in range(CONV_WIDTH)]
        for r0 in range(0, tm, CONV_ROWS):
            if bias_ref is None:
                acc = jnp.zeros((CONV_ROWS, LANES), F32)
            else:
                acc = jnp.broadcast_to(bias_ref[:, lanes], (CONV_ROWS, LANES))
            for j, rows in _windows(zp_ref, r0, lanes, groups):
                acc = acc + w[j] * rows
            out_ref[pl.ds(r0, CONV_ROWS), lanes] = acc


FWD_SHIFTS = [(j, CONV_PAD - (CONV_WIDTH - 1) + j) for j in range(CONV_WIDTH)]
BWD_SHIFTS = [(j, CONV_WIDTH - 1 - j) for j in range(CONV_WIDTH)]


def _conv_taps(zp_ref, z1_ref, dw_ref, bias_ref, tm, ch):
    _conv_apply(zp_ref, z1_ref, dw_ref, bias_ref, tm, ch, FWD_SHIFTS)


def _conv_specs(tm, ch):
    per = tm // CONV_PAD
    cb = COL_CONV_G // ch
    return [pl.BlockSpec((tm, ch), lambda i: (i, 0)),
            pl.BlockSpec((tm, ch), lambda i: (i, cb)),
            pl.BlockSpec((CONV_PAD, ch), lambda i: (jnp.maximum(i * per - 1, 0), 0)),
            pl.BlockSpec((CONV_PAD, ch), lambda i: (jnp.maximum(i * per - 1, 0), cb))]


def _conv_fwd(p, dw, bias, ln_g, ln_b):
    t = p.shape[0]
    ch = D_MODEL
    tm = min(CONV_TILE, t)

    def body(a_ref, g_ref, ah_ref, gh_ref, dw_ref, bias_ref, lg_ref, lb_ref, o_ref, z1_ref, zp_ref):
        i = pl.program_id(0)
        _conv_fill(zp_ref, a_ref, g_ref, ah_ref, gh_ref, i)
        _conv_taps(zp_ref, z1_ref, dw_ref, bias_ref, tm, ch)
        z1 = z1_ref[...]
        mu = jnp.mean(z1, axis=-1, keepdims=True)
        zc = z1 - mu
        rs = lax.rsqrt(jnp.mean(zc * zc, axis=-1, keepdims=True) + EPS)
        z2 = zc * rs * lg_ref[...] + lb_ref[...]
        o_ref[...] = (z2 * _sigmoid(z2)).astype(BF16)

    vec = pl.BlockSpec((1, ch), lambda i: (0, 0))
    return pl.pallas_call(
        body, name="conv_fwd", grid=(t // tm,),
        in_specs=_conv_specs(tm, ch) + [pl.BlockSpec((CONV_PAD, ch), lambda i: (0, 0)), vec, vec, vec],
        out_specs=[pl.BlockSpec((tm, ch), lambda i: (i, 0)), pl.BlockSpec((tm, ch), lambda i: (i, 0))],
        out_shape=[jax.ShapeDtypeStruct((t, ch), BF16), jax.ShapeDtypeStruct((t, ch), F32)],
        scratch_shapes=[pltpu.VMEM((CONV_PAD + tm, ch), F32)],
        compiler_params=_params(("parallel",)),
    )(p, p, p, p, dw, bias, ln_g, ln_b)


def _conv_bwd_ln(p, z1_saved, dz3, ln_g, ln_b):
    t = p.shape[0]
    ch = D_MODEL
    tm = min(CONV_TILE, t)

    def body(a_ref, g_ref, ah_ref, gh_ref, z1_ref, dz3_ref, lg_ref, lb_ref,
             dz1_ref, ddw_ref, dbias_ref, dlg_ref, dlb_ref, zp_ref):
        i = pl.program_id(0)
        _conv_fill(zp_ref, a_ref, g_ref, ah_ref, gh_ref, i)
        z1 = z1_ref[...]
        mu = jnp.mean(z1, axis=-1, keepdims=True)
        zc = z1 - mu
        rs = lax.rsqrt(jnp.mean(zc * zc, axis=-1, keepdims=True) + EPS)
        xh = zc * rs
        z2 = xh * lg_ref[...] + lb_ref[...]
        sig = _sigmoid(z2)
        dz2 = dz3_ref[...].astype(F32) * (sig * (1.0 + z2 * (1.0 - sig)))
        dxh = dz2 * lg_ref[...]
        dz1 = rs * (dxh - jnp.mean(dxh, axis=-1, keepdims=True) - xh * jnp.mean(dxh * xh, axis=-1, keepdims=True))
        dz1_ref[...] = dz1

        @pl.when(i == 0)
        def _():
            ddw_ref[...] = jnp.zeros_like(ddw_ref)
            dbias_ref[...] = jnp.zeros_like(dbias_ref)
            dlg_ref[...] = jnp.zeros_like(dlg_ref)
            dlb_ref[...] = jnp.zeros_like(dlb_ref)

        dlg_ref[...] += jnp.sum(dz2 * xh, axis=0, keepdims=True)
        dlb_ref[...] += jnp.sum(dz2, axis=0, keepdims=True)
        dbias_ref[...] += jnp.sum(dz1, axis=0, keepdims=True)
        groups = _shift_groups(FWD_SHIFTS)
        for cc in range(ch // LANES):
            lanes = pl.ds(cc * LANES, LANES)
            accs = [jnp.zeros((8, LANES), F32) for _ in range(CONV_WIDTH)]
            for r0 in range(0, tm, CONV_ROWS):
                dzc = dz1_ref[pl.ds(r0, CONV_ROWS), lanes]
                for j, rows in _windows(zp_ref, r0, lanes, groups):
                    accs[j] = accs[j] + jnp.sum((dzc * rows).reshape(CONV_ROWS // 8, 8, LANES), axis=0)
            for j in range(CONV_WIDTH):
                ddw_ref[pl.ds(j, 1), lanes] += jnp.sum(accs[j], axis=0, keepdims=True)

    vec = pl.BlockSpec((1, ch), lambda i: (0, 0))
    return pl.pallas_call(
        body, name="conv_bwd_ln", grid=(t // tm,),
        in_specs=_conv_specs(tm, ch) + [pl.BlockSpec((tm, ch), lambda i: (i, 0)),
                                        pl.BlockSpec((tm, ch), lambda i: (i, 0)), vec, vec],
        out_specs=[pl.BlockSpec((tm, ch), lambda i: (i, 0)), pl.BlockSpec((CONV_PAD, ch), lambda i: (0, 0)), vec, vec, vec],
        out_shape=[jax.ShapeDtypeStruct((t, ch), F32), jax.ShapeDtypeStruct((CONV_PAD, ch), F32)]
        + [jax.ShapeDtypeStruct((1, ch), F32)] * 3,
        scratch_shapes=[pltpu.VMEM((CONV_PAD + tm, ch), F32)],
        compiler_params=_params(("arbitrary",)),
    )(p, p, p, p, z1_saved, dz3, ln_g, ln_b)


def _conv_bwd_glu(p, dz1, dw):
    t = p.shape[0]
    ch = D_MODEL
    tm = min(CONV_TILE, t)
    per = tm // CONV_PAD
    n_halo = t // CONV_PAD
    cb = COL_CONV_G // ch

    def body(a_ref, g_ref, dz_ref, dzn_ref, dw_ref, o_ref, zp_ref, z0_ref):
        i = pl.program_id(0)
        zp_ref[pl.ds(0, tm), :] = dz_ref[...]
        zp_ref[pl.ds(tm, CONV_PAD), :] = jnp.where(i < t // tm - 1, dzn_ref[...], 0.0)
        _conv_apply(zp_ref, z0_ref, dw_ref, None, tm, ch, BWD_SHIFTS)
        dz0 = z0_ref[...]
        a = a_ref[...].astype(F32)
        sig = _sigmoid(g_ref[...].astype(F32))
        o_ref[:, pl.ds(0, ch)] = (dz0 * sig).astype(BF16)
        o_ref[:, pl.ds(ch, ch)] = (dz0 * a * sig * (1.0 - sig)).astype(BF16)

    return pl.pallas_call(
        body, name="conv_bwd_glu", grid=(t // tm,),
        in_specs=[pl.BlockSpec((tm, ch), lambda i: (i, 0)), pl.BlockSpec((tm, ch), lambda i: (i, cb)),
                  pl.BlockSpec((tm, ch), lambda i: (i, 0)),
                  pl.BlockSpec((CONV_PAD, ch), lambda i: (jnp.minimum((i + 1) * per, n_halo - 1), 0)),
                  pl.BlockSpec((CONV_PAD, ch), lambda i: (0, 0))],
        out_specs=pl.BlockSpec((tm, 2 * ch), lambda i: (i, 0)),
        out_shape=jax.ShapeDtypeStruct((t, 2 * ch), BF16),
        scratch_shapes=[pltpu.VMEM((tm + CONV_PAD, ch), F32), pltpu.VMEM((tm, ch), F32)],
        compiler_params=_params(("parallel",)),
    )(p, p, dz1, dz1, dw)


def _bucket_onehot():
    qi = jnp.arange(BLOCK, dtype=jnp.int32)[:, None]
    kj = jnp.arange(2 * BLOCK, dtype=jnp.int32)[None, :]
    dist = jnp.maximum(qi + BLOCK - kj, 0)
    max_exact = N_BUCKETS // 2
    dflt = jnp.maximum(dist, 1).astype(F32)
    large = max_exact + (jnp.log(dflt / max_exact) / math.log(MAX_DISTANCE / max_exact)
                         * (N_BUCKETS - max_exact)).astype(jnp.int32)
    large = jnp.minimum(large, N_BUCKETS - 1)
    bucket = jnp.where(dist < max_exact, dist, large)
    onehot = bucket[None] == jnp.arange(N_BUCKETS, dtype=jnp.int32)[:, None, None]
    return onehot.astype(F32).reshape(N_BUCKETS, BLOCK * 2 * BLOCK)


def _bias_table(rel_bias_t, onehot):
    n = onehot.shape[1]
    tn = 4096

    def body(r_ref, oh_ref, o_ref):
        flat = pl.program_id(0) * tn + lax.broadcasted_iota(jnp.int32, (N_Q_HEADS, tn), 1)
        dist = (flat // (2 * BLOCK)) + BLOCK - (flat % (2 * BLOCK))
        bias = _dot(r_ref[...], oh_ref[...], precision=lax.Precision.HIGHEST)
        o_ref[...] = jnp.where((dist >= 0) & (dist < BLOCK), bias, NEG)

    return pl.pallas_call(
        body, name="bias_table", grid=(n // tn,),
        in_specs=[pl.BlockSpec((N_Q_HEADS, N_BUCKETS), lambda i: (0, 0)), pl.BlockSpec((N_BUCKETS, tn), lambda i: (0, i))],
        out_specs=pl.BlockSpec((N_Q_HEADS, tn), lambda i: (0, i)),
        out_shape=jax.ShapeDtypeStruct((N_Q_HEADS, n), F32),
        compiler_params=_params(("parallel",)),
    )(rel_bias_t, onehot)


def _bias_table_bwd(dbias, onehot):
    n = onehot.shape[1]
    tn = 4096

    def body(d_ref, oh_ref, o_ref):
        part = _dot(d_ref[...], oh_ref[...], trans_b=True, precision=lax.Precision.HIGHEST)
        i = pl.program_id(0)

        @pl.when(i == 0)
        def _():
            o_ref[...] = part

        @pl.when(i > 0)
        def _():
            o_ref[...] += part

    return pl.pallas_call(
        body, name="bias_table_bwd", grid=(n // tn,),
        in_specs=[pl.BlockSpec((N_Q_HEADS, tn), lambda i: (0, i)), pl.BlockSpec((N_BUCKETS, tn), lambda i: (0, i))],
        out_specs=pl.BlockSpec((N_Q_HEADS, N_BUCKETS), lambda i: (0, 0)),
        out_shape=jax.ShapeDtypeStruct((N_Q_HEADS, N_BUCKETS), F32),
        compiler_params=_params(("arbitrary",)),
    )(dbias, onehot)


def _attn_probs(q, kp, kc, gq, gk, sink, bias, before_start):
    qf = q.astype(F32)
    rq = lax.rsqrt(jnp.mean(qf * qf, axis=-1, keepdims=True) + EPS)
    qn = qf * rq * gq
    kf = jnp.concatenate([kp, kc], axis=0).astype(F32)
    rk = lax.rsqrt(jnp.mean(kf * kf, axis=-1, keepdims=True) + EPS)
    kn = kf * rk * gk
    s = _dot(qn.astype(BF16), kn.astype(BF16), trans_b=True) * (1.0 / math.sqrt(HEAD_DIM)) + bias
    s = jnp.where(before_start, NEG, s)
    m = jnp.maximum(jnp.max(s, axis=-1, keepdims=True), sink)
    p = jnp.exp(s - m)
    es = jnp.exp(sink - m)
    inv = 1.0 / (jnp.sum(p, axis=-1, keepdims=True) + es)
    return qf, rq, qn, kf, rk, kn, p * inv, es * inv


def _before_start(n):
    col = lax.broadcasted_iota(jnp.int32, (QROWS, 2 * BLOCK), 1)
    return (col < BLOCK) & (n == 0)


def _attn_specs():
    qspec = pl.BlockSpec((N_KV_HEADS, None, QROWS, HEAD_DIM), lambda n: (0, n, 0, 0))
    kprev = pl.BlockSpec((N_KV_HEADS, BLOCK, HEAD_DIM), lambda n: (0, jnp.maximum(n - 1, 0), 0))
    kcur = pl.BlockSpec((N_KV_HEADS, BLOCK, HEAD_DIM), lambda n: (0, n, 0))
    gain = pl.BlockSpec((1, HEAD_DIM), lambda n: (0, 0))
    sink = pl.BlockSpec((N_KV_HEADS, QROWS, 1), lambda n: (0, 0, 0))
    bias = pl.BlockSpec((N_KV_HEADS, QROWS, 2 * BLOCK), lambda n: (0, 0, 0))
    return qspec, kprev, kcur, gain, sink, bias


def _attn_fwd(q4, k3, v3, gq, gk, sink_rows, bias):
    nb = q4.shape[1]
    qspec, kprev, kcur, gain, sink, bspec = _attn_specs()

    def body(q_ref, kp_ref, kc_ref, vp_ref, vc_ref, gq_ref, gk_ref, sink_ref, bias_ref, o_ref):
        before_start = _before_start(pl.program_id(0))
        for h in range(N_KV_HEADS):
            pn = _attn_probs(q_ref[h], kp_ref[h], kc_ref[h], gq_ref[...], gk_ref[...], sink_ref[h], bias_ref[h],
                             before_start)[6]
            v = jnp.concatenate([vp_ref[h], vc_ref[h]], axis=0)
            o_ref[h] = _dot(pn.astype(BF16), v).astype(BF16)

    return pl.pallas_call(
        body, name="attn_fwd", grid=(nb,),
        in_specs=[qspec, kprev, kcur, kprev, kcur, gain, gain, sink, bspec],
        out_specs=qspec, out_shape=jax.ShapeDtypeStruct(q4.shape, BF16),
        compiler_params=_params(("parallel",)),
    )(q4, k3, k3, v3, v3, gq, gk, sink_rows, bias)


def _attn_bwd(q4, k3, v3, do4, gq, gk, sink_rows, bias):
    nb = q4.shape[1]
    qspec, kprev, kcur, gain, sink, bspec = _attn_specs()
    scale = 1.0 / math.sqrt(HEAD_DIM)

    def rms_bwd(dn, xf, r, g):
        w = dn * g
        dx = r * w - xf * (r * r * r) * jnp.mean(xf * w, axis=-1, keepdims=True)
        return dx, jnp.sum(dn * (xf * r), axis=0, keepdims=True)

    def body(q_ref, kp_ref, kc_ref, vp_ref, vc_ref, do_ref, gq_ref, gk_ref, sink_ref, bias_ref,
             dq_ref, dk_ref, dv_ref, dbias_ref, dsink_ref, dgq_ref, dgk_ref):
        n = pl.program_id(0)

        @pl.when(n == 0)
        def _():
            dbias_ref[...] = jnp.zeros_like(dbias_ref)
            dsink_ref[...] = jnp.zeros_like(dsink_ref)
            dgq_ref[...] = jnp.zeros_like(dgq_ref)
            dgk_ref[...] = jnp.zeros_like(dgk_ref)

        before_start = _before_start(n)
        dgq_sum = jnp.zeros((1, HEAD_DIM), F32)
        dgk_sum = jnp.zeros((1, HEAD_DIM), F32)
        for h in range(N_KV_HEADS):
            qf, rq, qn, kf, rk, kn, pn, psink = _attn_probs(
                q_ref[h], kp_ref[h], kc_ref[h], gq_ref[...], gk_ref[...], sink_ref[h], bias_ref[h], before_start)
            do = do_ref[h]
            v = jnp.concatenate([vp_ref[h], vc_ref[h]], axis=0)
            dv_win = _dot(do, pn.astype(BF16), trans_a=True).T
            dp = _dot(do, v, trans_b=True)
            delta = jnp.sum(pn * dp, axis=-1, keepdims=True)
            ds = pn * (dp - delta)
            dsc = (ds * scale).astype(BF16)
            dqn = _dot(dsc, kn.astype(BF16))
            dkn = _dot(qn.astype(BF16), dsc, trans_a=True).T
            dq, dgq = rms_bwd(dqn, qf, rq, gq_ref[...])
            dk_win, dgk = rms_bwd(dkn, kf, rk, gk_ref[...])
            dq_ref[h] = dq.astype(BF16)
            dk_ref[h, 0] = dk_win[:BLOCK]
            dk_ref[h, 1] = dk_win[BLOCK:]
            dv_ref[h, 0] = dv_win[:BLOCK]
            dv_ref[h, 1] = dv_win[BLOCK:]
            dbias_ref[h] += ds
            dsink_ref[h] += jnp.sum((-psink * delta).reshape(GROUP, BLOCK, 1), axis=1)
            dgq_sum = dgq_sum + dgq
            dgk_sum = dgk_sum + dgk
        dgq_ref[...] += dgq_sum
        dgk_ref[...] += dgk_sum

    kv_out = pl.BlockSpec((N_KV_HEADS, None, 2, BLOCK, HEAD_DIM), lambda n: (0, n, 0, 0, 0))
    kv_shape = jax.ShapeDtypeStruct((N_KV_HEADS, nb, 2, BLOCK, HEAD_DIM), F32)
    return pl.pallas_call(
        body, name="attn_bwd", grid=(nb,),
        in_specs=[qspec, kprev, kcur, kprev, kcur, qspec, gain, gain, sink, bspec],
        out_specs=[qspec, kv_out, kv_out, bspec,
                   pl.BlockSpec((N_KV_HEADS, GROUP, 1), lambda n: (0, 0, 0)), gain, gain],
        out_shape=[jax.ShapeDtypeStruct(q4.shape, BF16), kv_shape, kv_shape,
                   jax.ShapeDtypeStruct((N_KV_HEADS, QROWS, 2 * BLOCK), F32),
                   jax.ShapeDtypeStruct((N_KV_HEADS, GROUP, 1), F32),
                   jax.ShapeDtypeStruct((1, HEAD_DIM), F32),
                   jax.ShapeDtypeStruct((1, HEAD_DIM), F32)],
        compiler_params=_params(("arbitrary",)),
    )(q4, k3, k3, v3, v3, do4, gq, gk, sink_rows, bias)


def _kv_window_sum(parts):
    nb = parts.shape[1]

    def body(cur_ref, nxt_ref, o_ref):
        nxt = jnp.where(pl.program_id(0) < nb - 1, nxt_ref[...], 0.0)
        o_ref[...] = (cur_ref[...] + nxt).astype(BF16)

    return pl.pallas_call(
        body, name="kv_window_sum", grid=(nb,),
        in_specs=[pl.BlockSpec((N_KV_HEADS, None, None, BLOCK, HEAD_DIM), lambda n: (0, n, 1, 0, 0)),
                  pl.BlockSpec((N_KV_HEADS, None, None, BLOCK, HEAD_DIM), lambda n: (0, jnp.minimum(n + 1, nb - 1), 0, 0, 0))],
        out_specs=pl.BlockSpec((N_KV_HEADS, None, BLOCK, HEAD_DIM), lambda n: (0, n, 0, 0)),
        out_shape=jax.ShapeDtypeStruct((N_KV_HEADS, nb, BLOCK, HEAD_DIM), BF16),
        compiler_params=_params(("parallel",)),
    )(parts, parts)


def _q_to_heads(cols, t):
    nb = t // BLOCK
    return cols.reshape(nb, BLOCK, N_KV_HEADS, GROUP, HEAD_DIM).transpose(2, 0, 3, 1, 4).reshape(N_KV_HEADS, nb, QROWS, HEAD_DIM)


def _q_from_heads(q4, t):
    nb = t // BLOCK
    return q4.reshape(N_KV_HEADS, nb, GROUP, BLOCK, HEAD_DIM).transpose(1, 3, 0, 2, 4).reshape(t, N_Q_HEADS * HEAD_DIM)


def _kv_to_heads(cols, t):
    return cols.reshape(t, N_KV_HEADS, HEAD_DIM).transpose(1, 0, 2)


def _kv_from_heads(k3, t):
    return k3.transpose(1, 0, 2).reshape(t, N_KV_HEADS * HEAD_DIM)


GATE_TILE = 512


def _merge_fwd(z3, o, p, w_proj, w_o):
    t, d = z3.shape
    tm = min(ROW_TILE, t)
    tn = GATE_TILE

    def body(z_ref, o_ref, gc_ref, ga_ref, wp_ref, wo_ref, m_ref, a_ref, b_ref):
        a = _dot(z_ref[...], wp_ref[...])
        b = _dot(o_ref[...], wo_ref[...])
        m_ref[...] = (_sigmoid(gc_ref[...].astype(F32)) * a + _sigmoid(ga_ref[...].astype(F32)) * b).astype(BF16)
        a_ref[...] = a.astype(BF16)
        b_ref[...] = b.astype(BF16)

    row = pl.BlockSpec((tm, d), lambda i, j: (i, 0))
    wspec = pl.BlockSpec((d, tn), lambda i, j: (0, j))
    ospec = pl.BlockSpec((tm, tn), lambda i, j: (i, j))
    return pl.pallas_call(
        body, name="merge_fwd", grid=(t // tm, d // tn),
        in_specs=[row, row,
                  pl.BlockSpec((tm, tn), lambda i, j: (i, COL_GC // tn + j)),
                  pl.BlockSpec((tm, tn), lambda i, j: (i, COL_GA // tn + j)), wspec, wspec],
        out_specs=[ospec, ospec, ospec],
        out_shape=[jax.ShapeDtypeStruct((t, d), BF16)] * 3,
        compiler_params=_params(("parallel", "parallel")),
    )(z3, o, p, p, w_proj, w_o)


def _merge_bwd(dres, w_out, a, b, p, tokens=()):
    t, d = dres.shape
    tm = min(ROW_TILE, t)
    tn = GATE_TILE

    def epilogue(acc, ex, outs, ids):
        a_ref, b_ref, gc_ref, ga_ref = ex[:4]
        sc = _sigmoid(gc_ref[...].astype(F32))
        sa = _sigmoid(ga_ref[...].astype(F32))
        outs[0][...] = (acc * sc).astype(BF16)
        outs[1][...] = (acc * sa).astype(BF16)
        outs[2][0] = (acc * a_ref[...].astype(F32) * sc * (1.0 - sc)).astype(BF16)
        outs[2][1] = (acc * b_ref[...].astype(F32) * sa * (1.0 - sa)).astype(BF16)

    ospec = pl.BlockSpec((tm, tn), lambda i, j, kk: (i, j))
    return _mm("merge_bwd", (t // tm, d // tn, 1),
               dres, pl.BlockSpec((tm, d), lambda i, j, kk: (i, 0)),
               w_out, pl.BlockSpec((tn, d), lambda i, j, kk: (j, 0)), (tm, tn),
               trans_b=True, a_pre=_to_bf16,
               extras=(a, b, p, p),
               extra_specs=(ospec, ospec,
                            pl.BlockSpec((tm, tn), lambda i, j, kk: (i, COL_GC // tn + j)),
                            pl.BlockSpec((tm, tn), lambda i, j, kk: (i, COL_GA // tn + j))), tokens=tokens,
               out_shape=(jax.ShapeDtypeStruct((t, d), BF16), jax.ShapeDtypeStruct((t, d), BF16),
                          jax.ShapeDtypeStruct((2, t, d), BF16)),
               out_specs=(ospec, ospec, pl.BlockSpec((2, tm, tn), lambda i, j, kk: (0, i, j))),
               epilogue=epilogue)


def _store_epilogue(acc, ex, outs, ids):
    outs[0][...] = acc


def _store_bf16_epilogue(acc, ex, outs, ids):
    outs[0][...] = acc.astype(BF16)


def _mm_nt(name, a, w, out_dtype=BF16):
    t, n = a.shape
    k = w.shape[0]
    tm = min(ROW_TILE, t)
    return _mm(name, (t // tm, 1, 1), a, pl.BlockSpec((tm, n), lambda i, j, kk: (i, 0)),
               w, pl.BlockSpec((k, n), lambda i, j, kk: (0, 0)), (tm, k), trans_b=True,
               out_shape=(jax.ShapeDtypeStruct((t, k), out_dtype),),
               out_specs=(pl.BlockSpec((tm, k), lambda i, j, kk: (i, 0)),),
               epilogue=_store_bf16_epilogue if out_dtype == BF16 else _store_epilogue)[0]


def _mm_tn(name, a, b, b_pre=None):
    t, m = a.shape
    n = b.shape[1]
    tk = min(TOKEN_TILE, t)
    return _mm(name, (1, 1, t // tk), a, pl.BlockSpec((tk, m), lambda i, j, kk: (kk, 0)),
               b, pl.BlockSpec((tk, n), lambda i, j, kk: (kk, 0)), (m, n), trans_a=True, b_pre=b_pre,
               out_shape=(jax.ShapeDtypeStruct((m, n), BF16),),
               out_specs=(pl.BlockSpec((m, n), lambda i, j, kk: (0, 0)),), epilogue=_store_bf16_epilogue)[0]


def _local_step(x, target, small, comm):
    t = x.shape[0]
    w = dict(small)

    n1 = _rmsnorm_fwd("ffn1_norm", x, w["ffn1_norm"])
    w.update(comm.weights("A", n1))
    x1, ffn1_saved = _ffn_fwd("ffn1", x, n1, w["ffn1_w_in"], w["ffn1_w_out"], comm.tokens)
    w.update(comm.weights("B", x1))
    hm = _rmsnorm_fwd("mix_norm", x1, w["mix_norm"], comm.tokens)
    tm = min(ROW_TILE, t)
    p = _mm("mix_in", (N_CHIPS, t // tm, 1),
            hm, pl.BlockSpec((tm, D_MODEL), lambda j, i, kk: (i, 0)),
            w["w_in"], pl.BlockSpec((None, D_MODEL, SHARD_W), lambda j, i, kk: (j, 0, 0)), (tm, SHARD_W),
            out_shape=(jax.ShapeDtypeStruct((t, IN_W), BF16),),
            out_specs=(pl.BlockSpec((tm, SHARD_W), lambda j, i, kk: (i, j)),),
            epilogue=_store_bf16_epilogue)[0]

    z3, z1 = _conv_fwd(p, w["conv_dw_kernel"], w["conv_dw_bias"], w["conv_ln_g"], w["conv_ln_b"])

    onehot = _bucket_onehot()
    bias = _bias_table(w["rel_bias"].T, onehot).reshape(N_KV_HEADS, QROWS, 2 * BLOCK)
    sink_rows = jnp.repeat(w["attn_sinks"].reshape(N_KV_HEADS, GROUP), BLOCK, axis=1)[..., None]
    q4 = _q_to_heads(p[:, COL_Q:COL_K], t)
    k3 = _kv_to_heads(p[:, COL_K:COL_V], t)
    v3 = _kv_to_heads(p[:, COL_V:COL_GC], t)
    o4 = _attn_fwd(q4, k3, v3, w["q_norm"], w["k_norm"], sink_rows, bias)
    o = _q_from_heads(o4, t)

    merged, a, b = _merge_fwd(z3, o, p, w["conv_w_proj"], w["attn_w_o"])
    x2 = _mm_residual("mix_out", merged, w["w_out"], x1, 1.0)
    n2 = _rmsnorm_fwd("ffn2_norm", x2, w["ffn2_norm"])
    w.update(comm.weights("C", n2))
    x3, ffn2_saved = _ffn_fwd("ffn2", x2, n2, w["ffn2_w_in"], w["ffn2_w_out"])
    dy, loss = _loss_head(x3, target)

    g, big = {}, {}
    dres2, big["ffn2_w_in"], big["ffn2_w_out"], g["ffn2_norm"] = _ffn_bwd(
        "ffn2b", dy, x2, w["ffn2_norm"], ffn2_saved, w["ffn2_w_in"], w["ffn2_w_out"])
    tokens = comm.reduce_start("R1", big)

    da, db, dgates = _merge_bwd(dres2, w["w_out"], a, b, p, tokens)
    big = {}
    big["w_out"] = _mm_tn("d_w_out", merged, dres2, b_pre=_to_bf16)
    big["conv_w_proj"] = _mm_tn("d_w_proj", z3, da)
    big["attn_w_o"] = _mm_tn("d_w_o", o, db)
    dz3 = _mm_nt("d_z3", da, w["conv_w_proj"])
    do = _mm_nt("d_o", db, w["attn_w_o"])

    dq4, dk_parts, dv_parts, dbias, dsink, g["q_norm"], g["k_norm"] = _attn_bwd(
        q4, k3, v3, _q_to_heads(do, t), w["q_norm"], w["k_norm"], sink_rows, bias)
    dk3 = _kv_window_sum(dk_parts).reshape(N_KV_HEADS, t, HEAD_DIM)
    dv3 = _kv_window_sum(dv_parts).reshape(N_KV_HEADS, t, HEAD_DIM)
    g["rel_bias"] = _bias_table_bwd(dbias.reshape(N_Q_HEADS, BLOCK * 2 * BLOCK), onehot).T
    g["attn_sinks"] = dsink.reshape(N_Q_HEADS)

    dz1, big["conv_dw_kernel"], g["conv_dw_bias"], g["conv_ln_g"], g["conv_ln_b"] = _conv_bwd_ln(
        p, z1, dz3, w["conv_ln_g"], w["conv_ln_b"])
    dconv = _conv_bwd_glu(p, dz1, w["conv_dw_kernel"])

    dp = jnp.concatenate([dconv, _q_from_heads(dq4, t), _kv_from_heads(dk3, t), _kv_from_heads(dv3, t),
                          dgates[0], dgates[1]], axis=1)
    tk = min(TOKEN_TILE, t)
    big["w_in"] = _mm("d_w_in", (1, N_CHIPS, t // tk),
                    hm, pl.BlockSpec((tk, D_MODEL), lambda i, j, kk: (kk, 0)),
                    dp, pl.BlockSpec((tk, SHARD_W), lambda i, j, kk: (kk, j)), (D_MODEL, SHARD_W),
                    trans_a=True,
                    out_shape=(jax.ShapeDtypeStruct((N_CHIPS, D_MODEL, SHARD_W), BF16),),
                    out_specs=(pl.BlockSpec((None, D_MODEL, SHARD_W), lambda i, j, kk: (j, 0, 0)),),
                    epilogue=_store_bf16_epilogue)[0]
    dres1, g["mix_norm"] = _mm("d_mix", (t // tm, 1, N_CHIPS),
                               dp, pl.BlockSpec((tm, SHARD_W), lambda i, j, kk: (i, kk)),
                               w["w_in"], pl.BlockSpec((None, D_MODEL, SHARD_W), lambda i, j, kk: (kk, 0, 0)),
                               (tm, D_MODEL), trans_b=True,
                               extras=(x1, w["mix_norm"], dres2),
                               extra_specs=(pl.BlockSpec((tm, D_MODEL), lambda i, j, kk: (i, 0)),
                                            pl.BlockSpec((1, D_MODEL), lambda i, j, kk: (0, 0)),
                                            pl.BlockSpec((tm, D_MODEL), lambda i, j, kk: (i, 0))),
                               out_shape=(jax.ShapeDtypeStruct((t, D_MODEL), F32), jax.ShapeDtypeStruct((1, D_MODEL), F32)),
                               out_specs=(pl.BlockSpec((tm, D_MODEL), lambda i, j, kk: (i, 0)),
                                          pl.BlockSpec((1, D_MODEL), lambda i, j, kk: (0, 0))),
                               epilogue=_rms_bwd_epilogue, sem=("arbitrary", "arbitrary", "arbitrary"))

    comm.reduce_finish("R1", dres1)
    tokens = comm.reduce_start("R2", big)

    def ffn1_grads(dw_in4, dw_out):
        comm.reduce_finish("R2", dw_in4)
        return comm.reduce_start("R3", {"ffn1_w_in": dw_in4, "ffn1_w_out": dw_out})

    grad_x, _, _, g["ffn1_norm"] = _ffn_bwd(
        "ffn1b", dres1, x, w["ffn1_norm"], ffn1_saved, w["ffn1_w_in"], w["ffn1_w_out"], tokens, ffn1_grads)
    comm.reduce_finish("R3", grad_x)
    return loss[0, 0], grad_x, g


def _mesh_place():
    x, y, c = lax.axis_index("x"), lax.axis_index("y"), lax.axis_index("c")
    chips = [(1 - x, y), (x, 1 - y), (1 - x, 1 - y)]
    return x, y, c, chips


def _any_specs(n):
    return [pl.BlockSpec(memory_space=pl.ANY)] * n


HBM_SPEC = pl.BlockSpec(memory_space=pltpu.HBM)
SEM_SPEC = pl.BlockSpec(memory_space=pltpu.SEMAPHORE)
EFFECT = pltpu.SideEffectType.DATAFLOW_SIDE_EFFECTING


def _in_hbm(a):
    return pltpu.with_memory_space_constraint(a, pltpu.HBM)


def _copy_start(name, srcs, lands, plan, after=()):
    ns, nb = len(srcs), len(lands)
    n = 3 * ns

    def body(*refs):
        s_refs, l_refs = refs[:ns], refs[ns:ns + nb]
        send_sems, recv_sems = refs[ns + nb + len(after)], refs[ns + nb + len(after) + 1]
        token = refs[-1]
        for k, (src, dst, to, _) in enumerate(plan(s_refs, l_refs)):
            pltpu.make_async_remote_copy(src_ref=src, dst_ref=dst, send_sem=send_sems.at[k], recv_sem=recv_sems.at[k],
                                         device_id=to, device_id_type=MESH).start()
        token[...] = jnp.zeros_like(token)

    bufs = list(srcs) + list(lands)
    outs = pl.pallas_call(
        body, name=name,
        out_shape=(pltpu.SemaphoreType.DMA((n,)), pltpu.SemaphoreType.DMA((n,)),
                   *[pltpu.HBM(a.shape, a.dtype) for a in bufs], jax.ShapeDtypeStruct((8, LANES), F32)),
        in_specs=[HBM_SPEC] * len(bufs) + [pl.BlockSpec(memory_space=pl.ANY)] * len(after),
        out_specs=(SEM_SPEC, SEM_SPEC, *[HBM_SPEC] * len(bufs), pl.BlockSpec(memory_space=pltpu.VMEM)),
        input_output_aliases={i: 2 + i for i in range(len(bufs))},
        compiler_params=pltpu.CompilerParams(has_side_effects=EFFECT),
    )(*[_in_hbm(a) for a in bufs], *after)
    return outs[0], outs[1], list(outs[2:2 + ns]), list(outs[2 + ns:2 + ns + nb]), outs[-1]


def _copy_wait(name, send_sems, recv_sems, srcs, lands, after, plan):
    ns, nb = len(srcs), len(lands)

    def body(*refs):
        s_refs, l_refs = refs[:ns], refs[ns:ns + nb]
        send_sems, recv_sems = refs[ns + nb], refs[ns + nb + 1]
        for k, (src, _, to, mine) in enumerate(plan(s_refs, l_refs)):
            cp = pltpu.make_async_remote_copy(src_ref=src, dst_ref=mine, send_sem=send_sems.at[k], recv_sem=recv_sems.at[k],
                                              device_id=to, device_id_type=MESH)
            cp.wait_send()
            cp.wait_recv()

    bufs = list(srcs) + list(lands)
    outs = pl.pallas_call(
        body, name=name,
        out_shape=tuple(pltpu.HBM(a.shape, a.dtype) for a in bufs),
        in_specs=[HBM_SPEC] * len(bufs) + [SEM_SPEC, SEM_SPEC, pl.BlockSpec(memory_space=pl.ANY)],
        out_specs=tuple([HBM_SPEC] * len(bufs)),
        input_output_aliases={i: i for i in range(len(bufs))},
        compiler_params=pltpu.CompilerParams(has_side_effects=EFFECT),
    )(*bufs, send_sems, recv_sems, after)
    return list(outs[:ns]), list(outs[ns:])


def _gather_plan(s_refs, l_refs):
    x, y, c, chips = _mesh_place()
    jme = 2 * x + y
    return [(s.at[c], land.at[jme, c], (*chip, c), land.at[2 * chip[0] + chip[1], c])
            for s, land in zip(s_refs, l_refs) for chip in chips]


def _scatter_plan(s_refs, l_refs):
    x, y, c, chips = _mesh_place()
    return [(s.at[2 * chip[0] + chip[1]], land.at[k], (*chip, c), land.at[k])
            for s, land in zip(s_refs, l_refs) for k, chip in enumerate(chips)]


def _gather_forward(name, shards, landed):
    nw = len(shards)

    def body(*refs):
        s_refs, o_refs = refs[:nw], refs[2 * nw:3 * nw]
        send_sems, recv_sems = refs[3 * nw:]
        x, y, c, chips = _mesh_place()
        me, sib, jme = (x, y, c), (x, y, 1 - c), 2 * x + y
        sent = []
        for w in range(nw):
            parts = [(o_refs[w].at[2 * chip[0] + chip[1], c], o_refs[w].at[2 * chip[0] + chip[1], c]) for chip in chips]
            parts.append((s_refs[w], o_refs[w].at[jme]))
            for k, (src, dst) in enumerate(parts):
                cp = pltpu.make_async_remote_copy(src_ref=src, dst_ref=dst, send_sem=send_sems.at[4 * w + k],
                                                  recv_sem=recv_sems.at[4 * w + k], device_id=sib, device_id_type=MESH)
                cp.start()
                sent.append(cp)
        for w in range(nw):
            parts = [o_refs[w].at[2 * chip[0] + chip[1], 1 - c] for chip in chips] + [o_refs[w].at[jme]]
            for k, part in enumerate(parts):
                pltpu.make_async_remote_copy(src_ref=part, dst_ref=part, send_sem=send_sems.at[4 * w + k],
                                             recv_sem=recv_sems.at[4 * w + k], device_id=me, device_id_type=MESH).wait_recv()
        for cp in sent:
            cp.wait_send()

    return pl.pallas_call(
        body, name=name,
        in_specs=_any_specs(2 * nw), out_specs=_any_specs(nw),
        out_shape=[jax.ShapeDtypeStruct(a.shape, a.dtype) for a in landed],
        input_output_aliases={nw + i: i for i in range(nw)},
        scratch_shapes=[pltpu.SemaphoreType.DMA((4 * nw,)), pltpu.SemaphoreType.DMA((4 * nw,))],
    )(*shards, *landed)


def _exchange_halves(name, grads, after=()):
    nw = len(grads)

    def body(*refs):
        g_refs, o_refs = refs[:nw], refs[nw + len(after):2 * nw + len(after)]
        send_sems, recv_sems = refs[2 * nw + len(after):]
        x, y, c, _ = _mesh_place()
        copies = []
        for w in range(nw):
            cp = pltpu.make_async_remote_copy(src_ref=g_refs[w].at[:, 1 - c], dst_ref=o_refs[w], send_sem=send_sems.at[w],
                                              recv_sem=recv_sems.at[w], device_id=(x, y, 1 - c), device_id_type=MESH)
            cp.start()
            copies.append(cp)
        for cp in copies:
            cp.wait()

    return pl.pallas_call(
        body, name=name,
        in_specs=_any_specs(nw + len(after)), out_specs=_any_specs(nw),
        out_shape=[jax.ShapeDtypeStruct((N_CHIPS,) + g.shape[2:], g.dtype) for g in grads],
        scratch_shapes=[pltpu.SemaphoreType.DMA((nw,)), pltpu.SemaphoreType.DMA((nw,))],
    )(*grads, *after)


def _row_tile(r):
    for cand in (256, 176, 128, 64, 32, 16, 8):
        if r % cand == 0:
            return cand
    return r


def _add_own_half(c_idx, grad, got):
    _, _, r, cols = grad.shape
    tr = _row_tile(r)

    def body(c_ref, g_ref, o_ref, out_ref):
        out_ref[...] = (g_ref[...].astype(F32) + o_ref[...].astype(F32)).astype(BF16)

    return pl.pallas_call(
        body, name="add_own_half",
        grid_spec=pltpu.PrefetchScalarGridSpec(
            num_scalar_prefetch=1, grid=(N_CHIPS, r // tr),
            in_specs=[pl.BlockSpec((None, None, tr, cols), lambda j, i, c_ref: (j, c_ref[0], i, 0)),
                      pl.BlockSpec((None, tr, cols), lambda j, i, c_ref: (j, i, 0))],
            out_specs=pl.BlockSpec((None, tr, cols), lambda j, i, c_ref: (j, i, 0))),
        out_shape=jax.ShapeDtypeStruct((N_CHIPS, r, cols), BF16),
        compiler_params=_params(("parallel", "parallel")),
    )(c_idx, grad, got)


def _sum_pieces(place_idx, sums, landed):
    _, r, cols = sums.shape
    tr = _row_tile(r)

    def body(j_ref, own_ref, p_ref, o_ref):
        o_ref[...] = ((own_ref[...].astype(F32) + p_ref[0].astype(F32)) + p_ref[1].astype(F32)) + p_ref[2].astype(F32)

    return pl.pallas_call(
        body, name="sum_pieces",
        grid_spec=pltpu.PrefetchScalarGridSpec(
            num_scalar_prefetch=1, grid=(r // tr,),
            in_specs=[pl.BlockSpec((None, tr, cols), lambda i, j_ref: (j_ref[0], i, 0)),
                      pl.BlockSpec((N_CHIPS - 1, tr, cols), lambda i, j_ref: (0, i, 0))],
            out_specs=pl.BlockSpec((None, tr, cols), lambda i, j_ref: (j_ref[1], i, 0))),
        out_shape=jax.ShapeDtypeStruct((2, r, cols), F32),
        compiler_params=_params(("parallel",)),
    )(place_idx, sums, landed)


def _join_halves(name, halves):
    nw = len(halves)

    def body(*refs):
        o_refs = refs[nw:2 * nw]
        send_sems, recv_sems = refs[2 * nw:]
        x, y, c, _ = _mesh_place()
        copies = []
        for w in range(nw):
            cp = pltpu.make_async_remote_copy(src_ref=o_refs[w].at[c], dst_ref=o_refs[w].at[c], send_sem=send_sems.at[w],
                                              recv_sem=recv_sems.at[w], device_id=(x, y, 1 - c), device_id_type=MESH)
            cp.start()
            copies.append(cp)
        for w in range(nw):
            copies[w].wait_send()
            landed = o_refs[w].at[1 - c]
            pltpu.make_async_remote_copy(src_ref=landed, dst_ref=landed, send_sem=send_sems.at[w], recv_sem=recv_sems.at[w],
                                         device_id=(x, y, c), device_id_type=MESH).wait_recv()

    return pl.pallas_call(
        body, name=name,
        in_specs=_any_specs(nw), out_specs=_any_specs(nw),
        out_shape=[jax.ShapeDtypeStruct(h.shape, F32) for h in halves],
        input_output_aliases={i: i for i in range(nw)},
        scratch_shapes=[pltpu.SemaphoreType.DMA((nw,)), pltpu.SemaphoreType.DMA((nw,))],
    )(*halves)


SMALL_ROWS = 8


def _all_reduce_small(pack):
    rows, cols = pack.shape
    n_dev = 8

    def body(p_ref, o_ref, slots, send_sems, recv_sems):
        x, y, c, _ = _mesh_place()
        me = 4 * x + 2 * y + c
        slots[me] = p_ref[...]
        copies = []
        for k in range(1, n_dev):
            peer = (me + k) % n_dev
            cp = pltpu.make_async_remote_copy(src_ref=p_ref, dst_ref=slots.at[me], send_sem=send_sems.at[k],
                                              recv_sem=recv_sems.at[k],
                                              device_id=(peer // 4, (peer // 2) % 2, peer % 2), device_id_type=MESH)
            cp.start()
            copies.append(cp)
        for k in range(1, n_dev):
            src = (me + n_dev - k) % n_dev
            pltpu.make_async_remote_copy(src_ref=p_ref, dst_ref=slots.at[src], send_sem=send_sems.at[k],
                                         recv_sem=recv_sems.at[k], device_id=(x, y, c), device_id_type=MESH).wait_recv()
        for cp in copies:
            cp.wait_send()
        total = slots[0]
        for s in range(1, n_dev):
            total = total + slots[s]
        o_ref[...] = total

    return pl.pallas_call(
        body, name="all_reduce_small",
        in_specs=[pl.BlockSpec(memory_space=pltpu.VMEM)], out_specs=pl.BlockSpec(memory_space=pltpu.VMEM),
        out_shape=jax.ShapeDtypeStruct((rows, cols), F32),
        scratch_shapes=[pltpu.VMEM((n_dev, rows, cols), F32), pltpu.SemaphoreType.DMA((n_dev,)),
                        pltpu.SemaphoreType.DMA((n_dev,))],
    )(pack)


def _adamw(name, w, g, m, v):
    r, cols = w.shape
    tr = _row_tile(r)

    def body(w_ref, g_ref, m_ref, v_ref, d_ref, nm_ref, nv_ref):
        gv = g_ref[...]
        nm = ADAM_B1 * m_ref[...] + (1.0 - ADAM_B1) * gv
        nv = ADAM_B2 * v_ref[...] + (1.0 - ADAM_B2) * (gv * gv)
        m_hat = nm / (1.0 - ADAM_B1 ** ADAM_STEP)
        v_hat = nv / (1.0 - ADAM_B2 ** ADAM_STEP)
        d_ref[...] = -ADAM_LR * (m_hat / (jnp.sqrt(v_hat) + ADAM_EPS) + ADAM_WD * w_ref[...])
        nm_ref[...] = nm
        nv_ref[...] = nv

    spec = pl.BlockSpec((tr, cols), lambda i: (i, 0))
    return pl.pallas_call(
        body, name=name, grid=(r // tr,),
        in_specs=[spec] * 4, out_specs=[spec] * 3,
        out_shape=[jax.ShapeDtypeStruct((r, cols), F32)] * 3,
        compiler_params=_params(("parallel",)),
    )(w, g, m, v)


BIG = ["ffn1_w_in", "ffn1_w_out", "w_in", "conv_w_proj", "attn_w_o", "w_out", "ffn2_w_in", "ffn2_w_out", "conv_dw_kernel"]
COL_SHARDED = ("ffn1_w_in", "w_in", "ffn2_w_in")
SMALL = ["ffn1_norm", "mix_norm", "ffn2_norm", "conv_dw_bias", "conv_ln_g", "conv_ln_b", "q_norm", "k_norm", "attn_sinks", "rel_bias"]
WEIGHTS = ["ffn1_norm", "ffn1_w_in", "ffn1_w_out", "mix_norm", "w_in", "conv_dw_kernel", "conv_dw_bias", "conv_ln_g",
           "conv_ln_b", "conv_w_proj", "q_norm", "k_norm", "attn_sinks", "rel_bias", "attn_w_o", "w_out", "ffn2_norm",
           "ffn2_w_in", "ffn2_w_out"]
SMALL_PLACE = {"ffn1_norm": (0, 0, 1024), "mix_norm": (1, 0, 1024), "ffn2_norm": (2, 0, 1024), "conv_dw_bias": (3, 0, 1024),
               "conv_ln_g": (4, 0, 1024), "conv_ln_b": (5, 0, 1024), "q_norm": (6, 0, 64), "k_norm": (6, 128, 64),
               "attn_sinks": (6, 256, 16), "rel_bias": (7, 0, 512)}
LOSS_PLACE = (6, 384)


def _pack_small(vals, fill=0.0, loss=None):
    pack = jnp.full((SMALL_ROWS, D_MODEL), fill, F32)
    for name, (row, lane, n) in SMALL_PLACE.items():
        pack = pack.at[row, lane:lane + n].set(vals[name].reshape(n))
    if loss is not None:
        pack = pack.at[LOSS_PLACE[0], LOSS_PLACE[1]].set(loss)
    return pack


def _unpack_small(pack, shapes):
    return {name: pack[row, lane:lane + n].reshape(shapes[name]) for name, (row, lane, n) in SMALL_PLACE.items()}


def _shard_halves(name, a):
    if name == "conv_dw_kernel":
        a = jnp.pad(a, ((0, CONV_PAD - CONV_WIDTH), (0, 0)))
    r, cols = a.shape
    return a.reshape(2, r // 2, cols)


GATHER_GROUPS = {"A": ["ffn1_w_in", "ffn1_w_out"],
                 "B": ["w_in", "conv_dw_kernel", "conv_w_proj", "attn_w_o", "w_out"],
                 "C": ["ffn2_w_in", "ffn2_w_out"]}


class _MeshComm:
    def __init__(self, wts):
        self.c_idx = lax.axis_index("c").astype(jnp.int32).reshape(1)
        self.place_idx = jnp.stack([2 * lax.axis_index("x") + lax.axis_index("y"), lax.axis_index("c")]).astype(jnp.int32)
        self.wts, self.gathers, self.reductions, self.reduced = wts, {}, {}, {}
        self.tokens, self.last_join = (), ()
        self._gather_start("A", ())

    def _gather_start(self, group, after):
        names = GATHER_GROUPS[group]
        shards = [_shard_halves(n, self.wts[n]) if n == "conv_dw_kernel" else _shard_halves(n, self.wts[n]).astype(BF16)
                  for n in names]
        lands = [lax.empty((N_CHIPS,) + s.shape, s.dtype) for s in shards]
        self.gathers[group] = _copy_start("gather_start_" + group, shards, lands, _gather_plan, after=after)
        self.tokens = (self.gathers[group][-1],)

    def weights(self, group, after):
        send_sems, recv_sems, shards, lands, token = self.gathers.pop(group)
        shards, lands = _copy_wait("gather_wait_" + group, send_sems, recv_sems, shards, lands,
                                   token if after is None else after, _gather_plan)
        gathered = _gather_forward("gather_forward_" + group, shards, lands)
        self.tokens = ()
        following = {"A": "B", "B": "C"}.get(group)
        if following:
            self._gather_start(following, (gathered[0],))
        out = {}
        for n, g4 in zip(GATHER_GROUPS[group], gathered):
            r, cols = g4.shape[2] * 2, g4.shape[3]
            if n in COL_SHARDED:
                out[n] = g4.reshape(N_CHIPS, r, cols)
            elif n == "conv_dw_kernel":
                out[n] = g4.reshape(N_CHIPS, r, cols).transpose(1, 0, 2).reshape(r, N_CHIPS * cols)
            else:
                out[n] = g4.reshape(N_CHIPS * r, cols)
        return out

    def reduce_start(self, group, grads):
        names = list(grads)
        g4 = []
        for n in names:
            a = grads[n]
            if n == "conv_dw_kernel":
                a = a.reshape(CONV_PAD, N_CHIPS, -1).transpose(1, 0, 2)
            elif n not in COL_SHARDED:
                a = a.reshape(N_CHIPS, a.shape[0] // N_CHIPS, a.shape[1])
            g4.append(a.reshape(N_CHIPS, 2, a.shape[1] // 2, a.shape[2]))
        got = _exchange_halves("exchange_halves_" + group, g4, after=self.last_join)
        sums = [_add_own_half(self.c_idx, a, b) for a, b in zip(g4, got)]
        lands = [lax.empty((N_CHIPS - 1,) + s.shape[1:], s.dtype) for s in sums]
        started = _copy_start("scatter_start_" + group, sums, lands, _scatter_plan, after=self.last_join)
        self.reductions[group] = (names,) + started
        return (started[-1],)

    def reduce_finish(self, group, after):
        names, send_sems, recv_sems, sums, lands, _ = self.reductions.pop(group)
        sums, lands = _copy_wait("scatter_wait_" + group, send_sems, recv_sems, sums, lands, after, _scatter_plan)
        halves = [_sum_pieces(self.place_idx, s, p) for s, p in zip(sums, lands)]
        joined = _join_halves("join_halves_" + group, halves)
        self.last_join = (joined[0],)
        self.reduced.update(zip(names, joined))


def kernel(x, ffn1_norm, ffn1_w_in, ffn1_w_out, mix_norm, w_in, conv_dw_kernel, conv_dw_bias, conv_ln_g, conv_ln_b, conv_w_proj, q_norm, k_norm, attn_sinks, rel_bias, attn_w_o, w_out, ffn2_norm, ffn2_w_in, ffn2_w_out, loss_target, m_ffn1_norm, m_ffn1_w_in, m_ffn1_w_out, m_mix_norm, m_w_in, m_conv_dw_kernel, m_conv_dw_bias, m_conv_ln_g, m_conv_ln_b, m_conv_w_proj, m_q_norm, m_k_norm, m_attn_sinks, m_rel_bias, m_attn_w_o, m_w_out, m_ffn2_norm, m_ffn2_w_in, m_ffn2_w_out, v_ffn1_norm, v_ffn1_w_in, v_ffn1_w_out, v_mix_norm, v_w_in, v_conv_dw_kernel, v_conv_dw_bias, v_conv_ln_g, v_conv_ln_b, v_conv_w_proj, v_q_norm, v_k_norm, v_attn_sinks, v_rel_bias, v_attn_w_o, v_w_out, v_ffn2_norm, v_ffn2_w_in, v_ffn2_w_out):
    args = dict(locals())
    wts = {n: args[n] for n in WEIGHTS}
    mom = {n: args["m_" + n] for n in WEIGHTS}
    var = {n: args["v_" + n] for n in WEIGHTS}
    comm = _MeshComm(wts)
    small = {n: wts[n] if n in ("attn_sinks", "rel_bias") else wts[n].reshape(1, -1) for n in SMALL}
    loss_part, grad_x, g = _local_step(x[0], loss_target[0], small, comm)

    small_sum = _all_reduce_small(_pack_small(g, loss=loss_part))
    loss = small_sum[LOSS_PLACE[0], LOSS_PLACE[1]]
    small_shapes = {n: wts[n].shape for n in SMALL}
    g_small = _unpack_small(small_sum, small_shapes)

    grads, delta, new_m, new_v = {}, {}, {}, {}
    for n in BIG:
        j = comm.reduced[n]
        gs = j.reshape(j.shape[1] * 2, j.shape[2])
        pad = n == "conv_dw_kernel"
        ws, ms, vs = (_shard_halves(n, a).reshape(gs.shape) for a in (wts[n], mom[n], var[n]))
        d, nm, nv = _adamw("adamw_" + n, ws, gs, ms, vs)
        cut = (lambda a: a[:CONV_WIDTH]) if pad else (lambda a: a)
        grads[n], delta[n], new_m[n], new_v[n] = cut(gs), cut(d), cut(nm), cut(nv)
    d, nm, nv = _adamw("adamw_small", _pack_small(wts), small_sum, _pack_small(mom), _pack_small(var, fill=1.0))
    grads.update(g_small)
    delta.update(_unpack_small(d, small_shapes))
    new_m.update(_unpack_small(nm, small_shapes))
    new_v.update(_unpack_small(nv, small_shapes))

    return (loss, grad_x[None], *[grads[n] for n in WEIGHTS], *[delta[n] for n in WEIGHTS],
            *[new_m[n] for n in WEIGHTS], *[new_v[n] for n in WEIGHTS])
```

```python
import functools
import math

import jax
import jax.numpy as jnp
from jax import lax
from jax.experimental import pallas as pl
from jax.experimental.pallas import tpu as pltpu

F32 = jnp.float32
BF16 = jnp.bfloat16
MESH = pl.DeviceIdType.MESH

EPS = 1e-6
D_MODEL = 1024
D_FF = 2816
N_CHIPS = 4
SHARD_W = 2 * D_FF // N_CHIPS
HEAD_DIM = 64
N_Q_HEADS = 16
N_KV_HEADS = 4
GROUP = N_Q_HEADS // N_KV_HEADS
BLOCK = 128
QROWS = GROUP * BLOCK
N_BUCKETS = 32
MAX_DISTANCE = 128
CONV_WIDTH = 31
CONV_PAD = 32
NEG = float(jnp.finfo(jnp.float32).min)

ADAM_LR = 0.001
ADAM_B1 = 0.9
ADAM_B2 = 0.999
ADAM_EPS = 1e-08
ADAM_WD = 0.01
ADAM_STEP = 10

VMEM_LIMIT_BYTES = 56 * 1024 * 1024
ROW_TILE = 1024
TOKEN_TILE = 1024
CONV_TILE = 256
CONV_ROWS = 128
LANES = 128

COL_CONV_A, COL_CONV_G, COL_Q, COL_K, COL_V, COL_GC, COL_GA = 0, 1024, 2048, 3072, 3328, 3584, 4608
IN_W = 5632


def _params(sem, vmem=VMEM_LIMIT_BYTES):
    return pltpu.CompilerParams(dimension_semantics=sem, vmem_limit_bytes=vmem)


def _sigmoid(x):
    return 1.0 / (1.0 + jnp.exp(-x))


def _dot(a, b, trans_a=False, trans_b=False, precision=None):
    dn = (((0,) if trans_a else (1,), (1,) if trans_b else (0,)), ((), ()))
    return lax.dot_general(a, b, dn, preferred_element_type=F32, precision=precision)


def _mm(name, grid, a, a_spec, b, b_spec, acc_shape, *, trans_a=False, trans_b=False, a_pre=None, b_pre=None,
        extras=(), extra_specs=(), tokens=(), out_shape, out_specs, epilogue, sem=("parallel", "parallel", "arbitrary")):
    n_k = grid[2]
    extras = tuple(extras) + tuple(tokens)
    extra_specs = tuple(extra_specs) + (pl.BlockSpec((8, LANES), lambda i, j, kk: (0, 0)),) * len(tokens)
    n_extra = len(extras)
    n_out = len(out_shape)

    def body(a_ref, b_ref, *rest):
        ex = rest[:n_extra]
        outs = rest[n_extra:n_extra + n_out]
        ids = (pl.program_id(0), pl.program_id(1), pl.program_id(2))
        av = a_ref[...]
        bv = b_ref[...]
        if a_pre is not None:
            av = a_pre(av)
        if b_pre is not None:
            bv = b_pre(bv)
        part = _dot(av, bv, trans_a, trans_b)
        if n_k == 1:
            epilogue(part, ex, outs, ids)
        else:
            acc = rest[-1]

            @pl.when(ids[2] == 0)
            def _():
                acc[...] = part

            @pl.when(ids[2] > 0)
            def _():
                acc[...] += part

            @pl.when(ids[2] == n_k - 1)
            def _():
                epilogue(acc[...], ex, outs, ids)

    scratch = [] if n_k == 1 else [pltpu.VMEM(acc_shape, F32)]
    return pl.pallas_call(
        body, name=name, grid=grid,
        in_specs=[a_spec, b_spec, *extra_specs],
        out_specs=list(out_specs), out_shape=list(out_shape),
        scratch_shapes=scratch, compiler_params=_params(sem),
    )(a, b, *extras)


def _half_bf16(v):
    return (0.5 * v).astype(BF16)


def _to_bf16(v):
    return v.astype(BF16)


def _rmsnorm_fwd(name, x, g, tokens=()):
    t, d = x.shape
    tm = min(ROW_TILE, t)

    def body(x_ref, g_ref, *rest):
        o_ref = rest[-1]
        xv = x_ref[...]
        r = lax.rsqrt(jnp.mean(xv * xv, axis=-1, keepdims=True) + EPS)
        o_ref[...] = (xv * r * g_ref[...]).astype(BF16)

    return pl.pallas_call(
        body, name=name, grid=(t // tm,),
        in_specs=[pl.BlockSpec((tm, d), lambda i: (i, 0)), pl.BlockSpec((1, d), lambda i: (0, 0))]
        + [pl.BlockSpec((8, LANES), lambda i: (0, 0))] * len(tokens),
        out_specs=pl.BlockSpec((tm, d), lambda i: (i, 0)),
        out_shape=jax.ShapeDtypeStruct((t, d), BF16),
        compiler_params=_params(("parallel",)),
    )(x, g, *tokens)


def _rms_bwd_epilogue(acc, ex, outs, ids):
    x_ref, g_ref, dres_ref = ex[:3]
    out_ref, dg_ref = outs
    xv = x_ref[...]
    r = lax.rsqrt(jnp.mean(xv * xv, axis=-1, keepdims=True) + EPS)
    w = acc * g_ref[...]
    dx = r * w - xv * (r * r * r) * jnp.mean(xv * w, axis=-1, keepdims=True)
    out_ref[...] = dres_ref[...] + dx
    part = jnp.sum(acc * (xv * r), axis=0, keepdims=True)

    @pl.when(ids[0] == 0)
    def _():
        dg_ref[...] = part

    @pl.when(ids[0] > 0)
    def _():
        dg_ref[...] += part


def _ffn_in(name, n, w_in4, tokens=()):
    t, d = n.shape
    tm = min(ROW_TILE, t)

    def body(n_ref, wa_ref, wb_ref, *rest):
        ab_ref, h_ref = rest[-2:]
        nv = n_ref[...]
        a = _dot(nv, wa_ref[...])
        b = _dot(nv, wb_ref[...])
        h_ref[...] = (a * _sigmoid(a) * b).astype(BF16)
        ab_ref[0] = a.astype(BF16)
        ab_ref[1] = b.astype(BF16)

    return pl.pallas_call(
        body, name=name, grid=(2, t // tm),
        in_specs=[pl.BlockSpec((tm, d), lambda j, i: (i, 0)),
                  pl.BlockSpec((None, d, SHARD_W), lambda j, i: (j, 0, 0)),
                  pl.BlockSpec((None, d, SHARD_W), lambda j, i: (j + 2, 0, 0))]
        + [pl.BlockSpec((8, LANES), lambda j, i: (0, 0))] * len(tokens),
        out_specs=[pl.BlockSpec((2, tm, SHARD_W), lambda j, i: (0, i, j)),
                   pl.BlockSpec((tm, SHARD_W), lambda j, i: (i, j))],
        out_shape=[jax.ShapeDtypeStruct((2, t, D_FF), BF16), jax.ShapeDtypeStruct((t, D_FF), BF16)],
        compiler_params=_params(("parallel", "parallel")),
    )(n, w_in4, w_in4, *tokens)


def _mm_residual(name, a, w, res, scale):
    t, k = a.shape
    n = w.shape[1]
    tm = min(ROW_TILE, t)

    def epilogue(acc, ex, outs, ids):
        outs[0][...] = ex[0][...] + scale * acc

    return _mm(name, (t // tm, 1, 1), a, pl.BlockSpec((tm, k), lambda i, j, kk: (i, 0)),
               w, pl.BlockSpec((k, n), lambda i, j, kk: (0, 0)), (tm, n),
               extras=(res,), extra_specs=(pl.BlockSpec((tm, n), lambda i, j, kk: (i, 0)),),
               out_shape=(jax.ShapeDtypeStruct((t, n), F32),),
               out_specs=(pl.BlockSpec((tm, n), lambda i, j, kk: (i, 0)),), epilogue=epilogue)[0]


def _ffn_fwd(tag, x, n, w_in4, w_out, tokens=()):
    ab, h = _ffn_in(tag + "_in", n, w_in4, tokens)
    y = _mm_residual(tag + "_out", h, w_out, x, 0.5)
    return y, (n, ab, h)


def _ffn_bwd(tag, dres, x, g, saved, w_in4, w_out, tokens=(), on_weight_grads=None):
    n, ab, h = saved
    t, d = x.shape
    tm = min(ROW_TILE, t)
    tk = min(TOKEN_TILE, t)
    half_w = SHARD_W

    def dact_epilogue(acc, ex, outs, ids):
        a = ex[0][0].astype(F32)
        b = ex[0][1].astype(F32)
        sig = _sigmoid(a)
        outs[0][0] = (acc * b * (sig * (1.0 + a * (1.0 - sig)))).astype(BF16)
        outs[0][1] = (acc * (a * sig)).astype(BF16)

    du = _mm(tag + "_dact", (2, t // tm, 1),
             dres, pl.BlockSpec((tm, d), lambda j, i, kk: (i, 0)),
             w_out, pl.BlockSpec((half_w, d), lambda j, i, kk: (j, 0)), (tm, half_w),
             trans_b=True, a_pre=_half_bf16,
             extras=(ab,), extra_specs=(pl.BlockSpec((2, tm, half_w), lambda j, i, kk: (0, i, j)),), tokens=tokens,
             out_shape=(jax.ShapeDtypeStruct((2, t, D_FF), BF16),),
             out_specs=(pl.BlockSpec((2, tm, half_w), lambda j, i, kk: (0, i, j)),),
             epilogue=dact_epilogue)[0]

    def store_epilogue(acc, ex, outs, ids):
        outs[0][...] = acc.astype(BF16)

    dw_out = _mm(tag + "_dwout", (2, 1, t // tk),
                 h, pl.BlockSpec((tk, half_w), lambda i, j, kk: (kk, i)),
                 dres, pl.BlockSpec((tk, d), lambda i, j, kk: (kk, 0)), (half_w, d),
                 trans_a=True, b_pre=_half_bf16,
                 out_shape=(jax.ShapeDtypeStruct((D_FF, d), BF16),),
                 out_specs=(pl.BlockSpec((half_w, d), lambda i, j, kk: (i, 0)),),
                 epilogue=store_epilogue)[0]

    dw_in4 = _mm(tag + "_dwin", (1, N_CHIPS, t // tk),
                 n, pl.BlockSpec((tk, d), lambda i, j, kk: (kk, 0)),
                 du, pl.BlockSpec((None, tk, SHARD_W), lambda i, j, kk: (j // 2, kk, j % 2)), (d, SHARD_W),
                 trans_a=True,
                 out_shape=(jax.ShapeDtypeStruct((N_CHIPS, d, SHARD_W), BF16),),
                 out_specs=(pl.BlockSpec((None, d, SHARD_W), lambda i, j, kk: (j, 0, 0)),),
                 epilogue=store_epilogue)[0]

    late = () if on_weight_grads is None else on_weight_grads(dw_in4, dw_out)

    dx, dg = _mm(tag + "_dn", (t // tm, 1, N_CHIPS),
                 du, pl.BlockSpec((None, tm, SHARD_W), lambda i, j, kk: (kk // 2, i, kk % 2)),
                 w_in4, pl.BlockSpec((None, d, SHARD_W), lambda i, j, kk: (kk, 0, 0)), (tm, d),
                 trans_b=True,
                 extras=(x, g, dres),
                 extra_specs=(pl.BlockSpec((tm, d), lambda i, j, kk: (i, 0)),
                              pl.BlockSpec((1, d), lambda i, j, kk: (0, 0)),
                              pl.BlockSpec((tm, d), lambda i, j, kk: (i, 0))), tokens=late,
                 out_shape=(jax.ShapeDtypeStruct((t, d), F32), jax.ShapeDtypeStruct((1, d), F32)),
                 out_specs=(pl.BlockSpec((tm, d), lambda i, j, kk: (i, 0)),
                            pl.BlockSpec((1, d), lambda i, j, kk: (0, 0))),
                 epilogue=_rms_bwd_epilogue, sem=("arbitrary", "arbitrary", "arbitrary"))
    return dx, dw_in4, dw_out, dg


def _loss_head(y, target):
    t, d = y.shape
    tm = min(ROW_TILE, t)

    def body(y_ref, t_ref, dy_ref, loss_ref):
        diff = y_ref[...] - t_ref[...]
        dy_ref[...] = diff * (1.0 / d)
        part = jnp.full((8, LANES), 0.5 / d * jnp.sum(diff * diff), F32)
        i = pl.program_id(0)

        @pl.when(i == 0)
        def _():
            loss_ref[...] = part

        @pl.when(i > 0)
        def _():
            loss_ref[...] += part

    return pl.pallas_call(
        body, name="loss_head", grid=(t // tm,),
        in_specs=[pl.BlockSpec((tm, d), lambda i: (i, 0)), pl.BlockSpec((tm, d), lambda i: (i, 0))],
        out_specs=[pl.BlockSpec((tm, d), lambda i: (i, 0)), pl.BlockSpec((8, LANES), lambda i: (0, 0))],
        out_shape=[jax.ShapeDtypeStruct((t, d), F32), jax.ShapeDtypeStruct((8, LANES), F32)],
        compiler_params=_params(("arbitrary",)),
    )(y, target)


def _conv_fill(zp_ref, a_ref, g_ref, ah_ref, gh_ref, i):
    zh = ah_ref[...].astype(F32) * _sigmoid(gh_ref[...].astype(F32))
    zp_ref[pl.ds(0, CONV_PAD), :] = jnp.where(i > 0, zh, 0.0)
    zp_ref[pl.ds(CONV_PAD, a_ref.shape[0]), :] = a_ref[...].astype(F32) * _sigmoid(g_ref[...].astype(F32))


def _shift_groups(shifts):
    groups = {}
    for j, s in shifts:
        groups.setdefault(s % 8, []).append((j, s // 8))
    return groups


def _windows(zp_ref, r0, lanes, groups):
    for q, taps in groups.items():
        deepest = max(p for _, p in taps)
        win = zp_ref[pl.ds(r0 + q, 8 * deepest + CONV_ROWS), lanes]
        for j, p in taps:
            yield j, win[8 * p:8 * p + CONV_ROWS]


def _conv_apply(zp_ref, out_ref, dw_ref, bias_ref, tm, ch, shifts):
    groups = _shift_groups(shifts)
    for cc in range(ch // LANES):
        lanes = pl.ds(cc * LANES, LANES)
        w = [dw_ref[pl.ds(j, 1), lanes] for j in range(CONV_WIDTH)]
        for r0 in range(0, tm, CONV_ROWS):
            if bias_ref is None:
                acc = jnp.zeros((CONV_ROWS, LANES), F32)
            else:
                acc = jnp.broadcast_to(bias_ref[:, lanes], (CONV_ROWS, LANES))
            for j, rows in _windows(zp_ref, r0, lanes, groups):
                acc = acc + w[j] * rows
            out_ref[pl.ds(r0, CONV_ROWS), lanes] = acc


FWD_SHIFTS = [(j, CONV_PAD - (CONV_WIDTH - 1) + j) for j in range(CONV_WIDTH)]
BWD_SHIFTS = [(j, CONV_WIDTH - 1 - j) for j in range(CONV_WIDTH)]


def _conv_taps(zp_ref, z1_ref, dw_ref, bias_ref, tm, ch):
    _conv_apply(zp_ref, z1_ref, dw_ref, bias_ref, tm, ch, FWD_SHIFTS)


def _conv_specs(tm, ch):
    per = tm // CONV_PAD
    cb = COL_CONV_G // ch
    return [pl.BlockSpec((tm, ch), lambda i: (i, 0)),
            pl.BlockSpec((tm, ch), lambda i: (i, cb)),
            pl.BlockSpec((CONV_PAD, ch), lambda i: (jnp.maximum(i * per - 1, 0), 0)),
            pl.BlockSpec((CONV_PAD, ch), lambda i: (jnp.maximum(i * per - 1, 0), cb))]


def _conv_fwd(p, dw, bias, ln_g, ln_b):
    t = p.shape[0]
    ch = D_MODEL
    tm = min(CONV_TILE, t)

    def body(a_ref, g_ref, ah_ref, gh_ref, dw_ref, bias_ref, lg_ref, lb_ref, o_ref, z1_ref, zp_ref):
        i = pl.program_id(0)
        _conv_fill(zp_ref, a_ref, g_ref, ah_ref, gh_ref, i)
        _conv_taps(zp_ref, z1_ref, dw_ref, bias_ref, tm, ch)
        z1 = z1_ref[...]
        mu = jnp.mean(z1, axis=-1, keepdims=True)
        zc = z1 - mu
        rs = lax.rsqrt(jnp.mean(zc * zc, axis=-1, keepdims=True) + EPS)
        z2 = zc * rs * lg_ref[...] + lb_ref[...]
        o_ref[...] = (z2 * _sigmoid(z2)).astype(BF16)

    vec = pl.BlockSpec((1, ch), lambda i: (0, 0))
    return pl.pallas_call(
        body, name="conv_fwd", grid=(t // tm,),
        in_specs=_conv_specs(tm, ch) + [pl.BlockSpec((CONV_PAD, ch), lambda i: (0, 0)), vec, vec, vec],
        out_specs=[pl.BlockSpec((tm, ch), lambda i: (i, 0)), pl.BlockSpec((tm, ch), lambda i: (i, 0))],
        out_shape=[jax.ShapeDtypeStruct((t, ch), BF16), jax.ShapeDtypeStruct((t, ch), F32)],
        scratch_shapes=[pltpu.VMEM((CONV_PAD + tm, ch), F32)],
        compiler_params=_params(("parallel",)),
    )(p, p, p, p, dw, bias, ln_g, ln_b)


def _conv_bwd_ln(p, z1_saved, dz3, ln_g, ln_b):
    t = p.shape[0]
    ch = D_MODEL
    tm = min(CONV_TILE, t)

    def body(a_ref, g_ref, ah_ref, gh_ref, z1_ref, dz3_ref, lg_ref, lb_ref,
             dz1_ref, ddw_ref, dbias_ref, dlg_ref, dlb_ref, zp_ref):
        i = pl.program_id(0)
        _conv_fill(zp_ref, a_ref, g_ref, ah_ref, gh_ref, i)
        z1 = z1_ref[...]
        mu = jnp.mean(z1, axis=-1, keepdims=True)
        zc = z1 - mu
        rs = lax.rsqrt(jnp.mean(zc * zc, axis=-1, keepdims=True) + EPS)
        xh = zc * rs
        z2 = xh * lg_ref[...] + lb_ref[...]
        sig = _sigmoid(z2)
        dz2 = dz3_ref[...].astype(F32) * (sig * (1.0 + z2 * (1.0 - sig)))
        dxh = dz2 * lg_ref[...]
        dz1 = rs * (dxh - jnp.mean(dxh, axis=-1, keepdims=True) - xh * jnp.mean(dxh * xh, axis=-1, keepdims=True))
        dz1_ref[...] = dz1

        @pl.when(i == 0)
        def _():
            ddw_ref[...] = jnp.zeros_like(ddw_ref)
            dbias_ref[...] = jnp.zeros_like(dbias_ref)
            dlg_ref[...] = jnp.zeros_like(dlg_ref)
            dlb_ref[...] = jnp.zeros_like(dlb_ref)

        dlg_ref[...] += jnp.sum(dz2 * xh, axis=0, keepdims=True)
        dlb_ref[...] += jnp.sum(dz2, axis=0, keepdims=True)
        dbias_ref[...] += jnp.sum(dz1, axis=0, keepdims=True)
        groups = _shift_groups(FWD_SHIFTS)
        for cc in range(ch // LANES):
            lanes = pl.ds(cc * LANES, LANES)
            accs = [jnp.zeros((8, LANES), F32) for _ in range(CONV_WIDTH)]
            for r0 in range(0, tm, CONV_ROWS):
                dzc = dz1_ref[pl.ds(r0, CONV_ROWS), lanes]
                for j, rows in _windows(zp_ref, r0, lanes, groups):
                    accs[j] = accs[j] + jnp.sum((dzc * rows).reshape(CONV_ROWS // 8, 8, LANES), axis=0)
            for j in range(CONV_WIDTH):
                ddw_ref[pl.ds(j, 1), lanes] += jnp.sum(accs[j], axis=0, keepdims=True)

    vec = pl.BlockSpec((1, ch), lambda i: (0, 0))
    return pl.pallas_call(
        body, name="conv_bwd_ln", grid=(t // tm,),
        in_specs=_conv_specs(tm, ch) + [pl.BlockSpec((tm, ch), lambda i: (i, 0)),
                                        pl.BlockSpec((tm, ch), lambda i: (i, 0)), vec, vec],
        out_specs=[pl.BlockSpec((tm, ch), lambda i: (i, 0)), pl.BlockSpec((CONV_PAD, ch), lambda i: (0, 0)), vec, vec, vec],
        out_shape=[jax.ShapeDtypeStruct((t, ch), F32), jax.ShapeDtypeStruct((CONV_PAD, ch), F32)]
        + [jax.ShapeDtypeStruct((1, ch), F32)] * 3,
        scratch_shapes=[pltpu.VMEM((CONV_PAD + tm, ch), F32)],
        compiler_params=_params(("arbitrary",)),
    )(p, p, p, p, z1_saved, dz3, ln_g, ln_b)


def _conv_bwd_glu(p, dz1, dw, dq, dkv, dgates):
    t = p.shape[0]
    ch = D_MODEL
    tm = min(CONV_TILE, t)
    per = tm // CONV_PAD
    n_halo = t // CONV_PAD
    cb = COL_CONV_G // ch

    def body(a_ref, g_ref, dz_ref, dzn_ref, dw_ref, dq_ref, dkv_ref, dgates_ref, o_ref, zp_ref, z0_ref):
        i = pl.program_id(0)
        o_ref[:, pl.ds(COL_Q, Q_W)] = dq_ref[...]
        o_ref[:, pl.ds(COL_K, 2 * KV_W)] = dkv_ref[...]
        o_ref[:, pl.ds(COL_GC, ch)] = dgates_ref[0]
        o_ref[:, pl.ds(COL_GA, ch)] = dgates_ref[1]
        zp_ref[pl.ds(0, tm), :] = dz_ref[...]
        zp_ref[pl.ds(tm, CONV_PAD), :] = jnp.where(i < t // tm - 1, dzn_ref[...], 0.0)
        _conv_apply(zp_ref, z0_ref, dw_ref, None, tm, ch, BWD_SHIFTS)
        dz0 = z0_ref[...]
        a = a_ref[...].astype(F32)
        sig = _sigmoid(g_ref[...].astype(F32))
        o_ref[:, pl.ds(0, ch)] = (dz0 * sig).astype(BF16)
        o_ref[:, pl.ds(ch, ch)] = (dz0 * a * sig * (1.0 - sig)).astype(BF16)

    return pl.pallas_call(
        body, name="conv_bwd_glu", grid=(t // tm,),
        in_specs=[pl.BlockSpec((tm, ch), lambda i: (i, 0)), pl.BlockSpec((tm, ch), lambda i: (i, cb)),
                  pl.BlockSpec((tm, ch), lambda i: (i, 0)),
                  pl.BlockSpec((CONV_PAD, ch), lambda i: (jnp.minimum((i + 1) * per, n_halo - 1), 0)),
                  pl.BlockSpec((CONV_PAD, ch), lambda i: (0, 0)),
                  pl.BlockSpec((tm, Q_W), lambda i: (i, 0)), pl.BlockSpec((tm, 2 * KV_W), lambda i: (i, 0)),
                  pl.BlockSpec((2, tm, ch), lambda i: (0, i, 0))],
        out_specs=pl.BlockSpec((tm, IN_W), lambda i: (i, 0)),
        out_shape=jax.ShapeDtypeStruct((t, IN_W), BF16),
        scratch_shapes=[pltpu.VMEM((tm + CONV_PAD, ch), F32), pltpu.VMEM((tm, ch), F32)],
        compiler_params=_params(("parallel",)),
    )(p, p, dz1, dz1, dw, dq, dkv, dgates)


def _bucket_onehot():
    qi = jnp.arange(BLOCK, dtype=jnp.int32)[:, None]
    kj = jnp.arange(2 * BLOCK, dtype=jnp.int32)[None, :]
    dist = jnp.maximum(qi + BLOCK - kj, 0)
    max_exact = N_BUCKETS // 2
    dflt = jnp.maximum(dist, 1).astype(F32)
    large = max_exact + (jnp.log(dflt / max_exact) / math.log(MAX_DISTANCE / max_exact)
                         * (N_BUCKETS - max_exact)).astype(jnp.int32)
    large = jnp.minimum(large, N_BUCKETS - 1)
    bucket = jnp.where(dist < max_exact, dist, large)
    onehot = bucket[None] == jnp.arange(N_BUCKETS, dtype=jnp.int32)[:, None, None]
    return onehot.astype(F32).reshape(N_BUCKETS, BLOCK * 2 * BLOCK)


def _bias_table(rel_bias_t, onehot):
    n = onehot.shape[1]
    tn = 4096

    def body(r_ref, oh_ref, o_ref):
        flat = pl.program_id(0) * tn + lax.broadcasted_iota(jnp.int32, (N_Q_HEADS, tn), 1)
        dist = (flat // (2 * BLOCK)) + BLOCK - (flat % (2 * BLOCK))
        bias = _dot(r_ref[...], oh_ref[...], precision=lax.Precision.HIGHEST)
        o_ref[...] = jnp.where((dist >= 0) & (dist < BLOCK), bias, NEG)

    return pl.pallas_call(
        body, name="bias_table", grid=(n // tn,),
        in_specs=[pl.BlockSpec((N_Q_HEADS, N_BUCKETS), lambda i: (0, 0)), pl.BlockSpec((N_BUCKETS, tn), lambda i: (0, i))],
        out_specs=pl.BlockSpec((N_Q_HEADS, tn), lambda i: (0, i)),
        out_shape=jax.ShapeDtypeStruct((N_Q_HEADS, n), F32),
        compiler_params=_params(("parallel",)),
    )(rel_bias_t, onehot)


def _bias_table_bwd(dbias, onehot):
    n = onehot.shape[1]
    tn = 4096

    def body(d_ref, oh_ref, o_ref):
        part = _dot(d_ref[...], oh_ref[...], trans_b=True, precision=lax.Precision.HIGHEST)
        i = pl.program_id(0)

        @pl.when(i == 0)
        def _():
            o_ref[...] = part

        @pl.when(i > 0)
        def _():
            o_ref[...] += part

    return pl.pallas_call(
        body, name="bias_table_bwd", grid=(n // tn,),
        in_specs=[pl.BlockSpec((N_Q_HEADS, tn), lambda i: (0, i)), pl.BlockSpec((N_BUCKETS, tn), lambda i: (0, i))],
        out_specs=pl.BlockSpec((N_Q_HEADS, N_BUCKETS), lambda i: (0, 0)),
        out_shape=jax.ShapeDtypeStruct((N_Q_HEADS, N_BUCKETS), F32),
        compiler_params=_params(("arbitrary",)),
    )(dbias, onehot)


def _attn_probs(q, kp, kc, gq, gk, sink, bias, before_start):
    qf = q.astype(F32)
    rq = lax.rsqrt(jnp.mean(qf * qf, axis=-1, keepdims=True) + EPS)
    qn = qf * rq * gq
    kf = jnp.concatenate([kp, kc], axis=0).astype(F32)
    rk = lax.rsqrt(jnp.mean(kf * kf, axis=-1, keepdims=True) + EPS)
    kn = kf * rk * gk
    s = _dot(qn.astype(BF16), kn.astype(BF16), trans_b=True) * (1.0 / math.sqrt(HEAD_DIM)) + bias
    s = jnp.where(before_start, NEG, s)
    m = jnp.maximum(jnp.max(s, axis=-1, keepdims=True), sink)
    p = jnp.exp(s - m)
    es = jnp.exp(sink - m)
    inv = 1.0 / (jnp.sum(p, axis=-1, keepdims=True) + es)
    return qf, rq, qn, kf, rk, kn, p * inv, es * inv


def _before_start(n):
    col = lax.broadcasted_iota(jnp.int32, (QROWS, 2 * BLOCK), 1)
    return (col < BLOCK) & (n == 0)


KV_W = N_KV_HEADS * HEAD_DIM
Q_W = N_Q_HEADS * HEAD_DIM


def _attn_specs():
    qspec = pl.BlockSpec((BLOCK, Q_W), lambda n: (n, COL_Q // Q_W))
    kprev = pl.BlockSpec((BLOCK, KV_W), lambda n: (jnp.maximum(n - 1, 0), COL_K // KV_W))
    kcur = pl.BlockSpec((BLOCK, KV_W), lambda n: (n, COL_K // KV_W))
    vprev = pl.BlockSpec((BLOCK, KV_W), lambda n: (jnp.maximum(n - 1, 0), COL_V // KV_W))
    vcur = pl.BlockSpec((BLOCK, KV_W), lambda n: (n, COL_V // KV_W))
    gain = pl.BlockSpec((1, HEAD_DIM), lambda n: (0, 0))
    sink = pl.BlockSpec((N_KV_HEADS, QROWS, 1), lambda n: (0, 0, 0))
    bias = pl.BlockSpec((N_KV_HEADS, QROWS, 2 * BLOCK), lambda n: (0, 0, 0))
    return [qspec, kprev, kcur, vprev, vcur], gain, sink, bias


def _head_selectors():
    row = lax.broadcasted_iota(jnp.int32, (KV_W, HEAD_DIM), 0)
    col = lax.broadcasted_iota(jnp.int32, (KV_W, HEAD_DIM), 1)
    return [(row == col + i * HEAD_DIM).astype(BF16) for i in range(N_KV_HEADS)]


def _take_heads(group, sel):
    return jnp.concatenate([_dot(group, s) for s in sel], axis=0)


def _put_heads(x, sel):
    rows = x.shape[0] // len(sel)
    out = _dot(x[:rows].astype(BF16), sel[0], trans_b=True)
    for i in range(1, len(sel)):
        out = out + _dot(x[i * rows:(i + 1) * rows].astype(BF16), sel[i], trans_b=True)
    return out


def _attn_fwd(p, gq, gk, sink_rows, bias):
    t = p.shape[0]
    nb = t // BLOCK
    qkv, gain, sink, bspec = _attn_specs()

    def body(q_ref, kp_ref, kc_ref, vp_ref, vc_ref, gq_ref, gk_ref, sink_ref, bias_ref, o_ref):
        before_start = _before_start(pl.program_id(0))
        sel = _head_selectors()
        for h in range(N_KV_HEADS):
            q = _take_heads(q_ref[:, pl.ds(h * KV_W, KV_W)], sel)
            pn = _attn_probs(q, _dot(kp_ref[...], sel[h]), _dot(kc_ref[...], sel[h]), gq_ref[...], gk_ref[...],
                             sink_ref[h], bias_ref[h], before_start)[6]
            v = jnp.concatenate([_dot(vp_ref[...], sel[h]), _dot(vc_ref[...], sel[h])], axis=0).astype(BF16)
            o_ref[:, pl.ds(h * KV_W, KV_W)] = _put_heads(_dot(pn.astype(BF16), v), sel).astype(BF16)

    return pl.pallas_call(
        body, name="attn_fwd", grid=(nb,),
        in_specs=qkv + [gain, gain, sink, bspec],
        out_specs=pl.BlockSpec((BLOCK, Q_W), lambda n: (n, 0)), out_shape=jax.ShapeDtypeStruct((t, Q_W), BF16),
        compiler_params=_params(("parallel",)),
    )(p, p, p, p, p, gq, gk, sink_rows, bias)


def _attn_bwd(p, do, gq, gk, sink_rows, bias):
    t = p.shape[0]
    nb = t // BLOCK
    qkv, gain, sink, bspec = _attn_specs()
    scale = 1.0 / math.sqrt(HEAD_DIM)

    def rms_bwd(dn, xf, r, g):
        w = dn * g
        dx = r * w - xf * (r * r * r) * jnp.mean(xf * w, axis=-1, keepdims=True)
        return dx, jnp.sum(dn * (xf * r), axis=0, keepdims=True)

    def body(q_ref, kp_ref, kc_ref, vp_ref, vc_ref, do_ref, gq_ref, gk_ref, sink_ref, bias_ref,
             dq_ref, dkv_ref, dbias_ref, dsink_ref, dgq_ref, dgk_ref):
        n = pl.program_id(0)
        sel = _head_selectors()

        @pl.when(n == 0)
        def _():
            dbias_ref[...] = jnp.zeros_like(dbias_ref)
            dsink_ref[...] = jnp.zeros_like(dsink_ref)
            dgq_ref[...] = jnp.zeros_like(dgq_ref)
            dgk_ref[...] = jnp.zeros_like(dgk_ref)

        before_start = _before_start(n)
        dgq_sum = jnp.zeros((1, HEAD_DIM), F32)
        dgk_sum = jnp.zeros((1, HEAD_DIM), F32)
        dk_rows, dv_rows = [], []
        for h in range(N_KV_HEADS):
            qf, rq, qn, kf, rk, kn, pn, psink = _attn_probs(
                _take_heads(q_ref[:, pl.ds(h * KV_W, KV_W)], sel), _dot(kp_ref[...], sel[h]), _dot(kc_ref[...], sel[h]),
                gq_ref[...], gk_ref[...], sink_ref[h], bias_ref[h], before_start)
            do = _take_heads(do_ref[:, pl.ds(h * KV_W, KV_W)], sel).astype(BF16)
            v = jnp.concatenate([_dot(vp_ref[...], sel[h]), _dot(vc_ref[...], sel[h])], axis=0).astype(BF16)
            dv_win = _dot(do, pn.astype(BF16), trans_a=True).T
            dp = _dot(do, v, trans_b=True)
            delta = jnp.sum(pn * dp, axis=-1, keepdims=True)
            ds = pn * (dp - delta)
            dsc = (ds * scale).astype(BF16)
            dqn = _dot(dsc, kn.astype(BF16))
            dkn = _dot(qn.astype(BF16), dsc, trans_a=True).T
            dq, dgq = rms_bwd(dqn, qf, rq, gq_ref[...])
            dk_win, dgk = rms_bwd(dkn, kf, rk, gk_ref[...])
            dq_ref[:, pl.ds(h * KV_W, KV_W)] = _put_heads(dq, sel).astype(BF16)
            dk_rows += [dk_win[:BLOCK], dk_win[BLOCK:]]
            dv_rows += [dv_win[:BLOCK], dv_win[BLOCK:]]
            dbias_ref[h] += ds
            dsink_ref[h] += jnp.sum((-psink * delta).reshape(GROUP, BLOCK, 1), axis=1)
            dgq_sum = dgq_sum + dgq
            dgk_sum = dgk_sum + dgk
        for part in range(2):
            dkv_ref[part, :, pl.ds(0, KV_W)] = _put_heads(jnp.concatenate(dk_rows[part::2], axis=0), sel).astype(BF16)
            dkv_ref[part, :, pl.ds(KV_W, KV_W)] = _put_heads(jnp.concatenate(dv_rows[part::2], axis=0), sel).astype(BF16)
        dgq_ref[...] += dgq_sum
        dgk_ref[...] += dgk_sum

    row = pl.BlockSpec((BLOCK, Q_W), lambda n: (n, 0))
    return pl.pallas_call(
        body, name="attn_bwd", grid=(nb,),
        in_specs=qkv + [row, gain, gain, sink, bspec],
        out_specs=[row, pl.BlockSpec((None, 2, BLOCK, 2 * KV_W), lambda n: (n, 0, 0, 0)), bspec,
                   pl.BlockSpec((N_KV_HEADS, GROUP, 1), lambda n: (0, 0, 0)), gain, gain],
        out_shape=[jax.ShapeDtypeStruct((t, Q_W), BF16),
                   jax.ShapeDtypeStruct((nb, 2, BLOCK, 2 * KV_W), BF16),
                   jax.ShapeDtypeStruct((N_KV_HEADS, QROWS, 2 * BLOCK), F32),
                   jax.ShapeDtypeStruct((N_KV_HEADS, GROUP, 1), F32),
                   jax.ShapeDtypeStruct((1, HEAD_DIM), F32),
                   jax.ShapeDtypeStruct((1, HEAD_DIM), F32)],
        compiler_params=_params(("arbitrary",)),
    )(p, p, p, p, p, do, gq, gk, sink_rows, bias)


def _kv_window_sum(parts):
    nb = parts.shape[0]

    def body(cur_ref, nxt_ref, o_ref):
        nxt = jnp.where(pl.program_id(0) < nb - 1, nxt_ref[...].astype(F32), 0.0)
        o_ref[...] = (cur_ref[...].astype(F32) + nxt).astype(BF16)

    blk = (None, None, BLOCK, 2 * KV_W)
    return pl.pallas_call(
        body, name="kv_window_sum", grid=(nb,),
        in_specs=[pl.BlockSpec(blk, lambda n: (n, 1, 0, 0)),
                  pl.BlockSpec(blk, lambda n: (jnp.minimum(n + 1, nb - 1), 0, 0, 0))],
        out_specs=pl.BlockSpec((BLOCK, 2 * KV_W), lambda n: (n, 0)),
        out_shape=jax.ShapeDtypeStruct((nb * BLOCK, 2 * KV_W), BF16),
        compiler_params=_params(("parallel",)),
    )(parts, parts)


GATE_TILE = 512


def _merge_fwd(z3, o, p, w_proj, w_o):
    t, d = z3.shape
    tm = min(ROW_TILE, t)
    tn = GATE_TILE

    def body(z_ref, o_ref, gc_ref, ga_ref, wp_ref, wo_ref, m_ref, a_ref, b_ref):
        a = _dot(z_ref[...], wp_ref[...])
        b = _dot(o_ref[...], wo_ref[...])
        m_ref[...] = (_sigmoid(gc_ref[...].astype(F32)) * a + _sigmoid(ga_ref[...].astype(F32)) * b).astype(BF16)
        a_ref[...] = a.astype(BF16)
        b_ref[...] = b.astype(BF16)

    row = pl.BlockSpec((tm, d), lambda i, j: (i, 0))
    wspec = pl.BlockSpec((d, tn), lambda i, j: (0, j))
    ospec = pl.BlockSpec((tm, tn), lambda i, j: (i, j))
    return pl.pallas_call(
        body, name="merge_fwd", grid=(t // tm, d // tn),
        in_specs=[row, row,
                  pl.BlockSpec((tm, tn), lambda i, j: (i, COL_GC // tn + j)),
                  pl.BlockSpec((tm, tn), lambda i, j: (i, COL_GA // tn + j)), wspec, wspec],
        out_specs=[ospec, ospec, ospec],
        out_shape=[jax.ShapeDtypeStruct((t, d), BF16)] * 3,
        compiler_params=_params(("parallel", "parallel")),
    )(z3, o, p, p, w_proj, w_o)


def _merge_bwd(dres, w_out, a, b, p, tokens=()):
    t, d = dres.shape
    tm = min(ROW_TILE, t)
    tn = GATE_TILE

    def epilogue(acc, ex, outs, ids):
        a_ref, b_ref, gc_ref, ga_ref = ex[:4]
        sc = _sigmoid(gc_ref[...].astype(F32))
        sa = _sigmoid(ga_ref[...].astype(F32))
        outs[0][...] = (acc * sc).astype(BF16)
        outs[1][...] = (acc * sa).astype(BF16)
        outs[2][0] = (acc * a_ref[...].astype(F32) * sc * (1.0 - sc)).astype(BF16)
        outs[2][1] = (acc * b_ref[...].astype(F32) * sa * (1.0 - sa)).astype(BF16)

    ospec = pl.BlockSpec((tm, tn), lambda i, j, kk: (i, j))
    return _mm("merge_bwd", (t // tm, d // tn, 1),
               dres, pl.BlockSpec((tm, d), lambda i, j, kk: (i, 0)),
               w_out, pl.BlockSpec((tn, d), lambda i, j, kk: (j, 0)), (tm, tn),
               trans_b=True, a_pre=_to_bf16,
               extras=(a, b, p, p),
               extra_specs=(ospec, ospec,
                            pl.BlockSpec((tm, tn), lambda i, j, kk: (i, COL_GC // tn + j)),
                            pl.BlockSpec((tm, tn), lambda i, j, kk: (i, COL_GA // tn + j))), tokens=tokens,
               out_shape=(jax.ShapeDtypeStruct((t, d), BF16), jax.ShapeDtypeStruct((t, d), BF16),
                          jax.ShapeDtypeStruct((2, t, d), BF16)),
               out_specs=(ospec, ospec, pl.BlockSpec((2, tm, tn), lambda i, j, kk: (0, i, j))),
               epilogue=epilogue)


def _store_epilogue(acc, ex, outs, ids):
    outs[0][...] = acc


def _store_bf16_epilogue(acc, ex, outs, ids):
    outs[0][...] = acc.astype(BF16)


def _mm_nt(name, a, w, out_dtype=BF16):
    t, n = a.shape
    k = w.shape[0]
    tm = min(ROW_TILE, t)
    return _mm(name, (t // tm, 1, 1), a, pl.BlockSpec((tm, n), lambda i, j, kk: (i, 0)),
               w, pl.BlockSpec((k, n), lambda i, j, kk: (0, 0)), (tm, k), trans_b=True,
               out_shape=(jax.ShapeDtypeStruct((t, k), out_dtype),),
               out_specs=(pl.BlockSpec((tm, k), lambda i, j, kk: (i, 0)),),
               epilogue=_store_bf16_epilogue if out_dtype == BF16 else _store_epilogue)[0]


def _mm_tn(name, a, b, b_pre=None):
    t, m = a.shape
    n = b.shape[1]
    tk = min(TOKEN_TILE, t)
    return _mm(name, (1, 1, t // tk), a, pl.BlockSpec((tk, m), lambda i, j, kk: (kk, 0)),
               b, pl.BlockSpec((tk, n), lambda i, j, kk: (kk, 0)), (m, n), trans_a=True, b_pre=b_pre,
               out_shape=(jax.ShapeDtypeStruct((m, n), BF16),),
               out_specs=(pl.BlockSpec((m, n), lambda i, j, kk: (0, 0)),), epilogue=_store_bf16_epilogue)[0]


def _local_step(x, target, small, comm):
    t = x.shape[0]
    w = dict(small)

    n1 = _rmsnorm_fwd("ffn1_norm", x, w["ffn1_norm"])
    w.update(comm.weights("A", n1))
    x1, ffn1_saved = _ffn_fwd("ffn1", x, n1, w["ffn1_w_in"], w["ffn1_w_out"], comm.tokens)
    w.update(comm.weights("B", x1))
    hm = _rmsnorm_fwd("mix_norm", x1, w["mix_norm"], comm.tokens)
    tm = min(ROW_TILE, t)
    p = _mm("mix_in", (N_CHIPS, t // tm, 1),
            hm, pl.BlockSpec((tm, D_MODEL), lambda j, i, kk: (i, 0)),
            w["w_in"], pl.BlockSpec((None, D_MODEL, SHARD_W), lambda j, i, kk: (j, 0, 0)), (tm, SHARD_W),
            out_shape=(jax.ShapeDtypeStruct((t, IN_W), BF16),),
            out_specs=(pl.BlockSpec((tm, SHARD_W), lambda j, i, kk: (i, j)),),
            epilogue=_store_bf16_epilogue)[0]

    z3, z1 = _conv_fwd(p, w["conv_dw_kernel"], w["conv_dw_bias"], w["conv_ln_g"], w["conv_ln_b"])

    onehot = _bucket_onehot()
    bias = _bias_table(w["rel_bias"].T, onehot).reshape(N_KV_HEADS, QROWS, 2 * BLOCK)
    sink_rows = jnp.repeat(w["attn_sinks"].reshape(N_KV_HEADS, GROUP), BLOCK, axis=1)[..., None]
    o = _attn_fwd(p, w["q_norm"], w["k_norm"], sink_rows, bias)

    merged, a, b = _merge_fwd(z3, o, p, w["conv_w_proj"], w["attn_w_o"])
    x2 = _mm_residual("mix_out", merged, w["w_out"], x1, 1.0)
    n2 = _rmsnorm_fwd("ffn2_norm", x2, w["ffn2_norm"])
    w.update(comm.weights("C", n2))
    x3, ffn2_saved = _ffn_fwd("ffn2", x2, n2, w["ffn2_w_in"], w["ffn2_w_out"])
    dy, loss = _loss_head(x3, target)

    g, big = {}, {}
    dres2, big["ffn2_w_in"], big["ffn2_w_out"], g["ffn2_norm"] = _ffn_bwd(
        "ffn2b", dy, x2, w["ffn2_norm"], ffn2_saved, w["ffn2_w_in"], w["ffn2_w_out"])
    tokens = comm.reduce_start("R1", big)

    da, db, dgates = _merge_bwd(dres2, w["w_out"], a, b, p, tokens)
    big = {}
    big["w_out"] = _mm_tn("d_w_out", merged, dres2, b_pre=_to_bf16)
    big["conv_w_proj"] = _mm_tn("d_w_proj", z3, da)
    big["attn_w_o"] = _mm_tn("d_w_o", o, db)
    dz3 = _mm_nt("d_z3", da, w["conv_w_proj"])
    do = _mm_nt("d_o", db, w["attn_w_o"])

    dq, dkv_parts, dbias, dsink, g["q_norm"], g["k_norm"] = _attn_bwd(
        p, do, w["q_norm"], w["k_norm"], sink_rows, bias)
    dkv = _kv_window_sum(dkv_parts)
    g["rel_bias"] = _bias_table_bwd(dbias.reshape(N_Q_HEADS, BLOCK * 2 * BLOCK), onehot).T
    g["attn_sinks"] = dsink.reshape(N_Q_HEADS)

    dz1, big["conv_dw_kernel"], g["conv_dw_bias"], g["conv_ln_g"], g["conv_ln_b"] = _conv_bwd_ln(
        p, z1, dz3, w["conv_ln_g"], w["conv_ln_b"])
    dp = _conv_bwd_glu(p, dz1, w["conv_dw_kernel"], dq, dkv, dgates)
    tk = min(TOKEN_TILE, t)
    big["w_in"] = _mm("d_w_in", (1, N_CHIPS, t // tk),
                    hm, pl.BlockSpec((tk, D_MODEL), lambda i, j, kk: (kk, 0)),
                    dp, pl.BlockSpec((tk, SHARD_W), lambda i, j, kk: (kk, j)), (D_MODEL, SHARD_W),
                    trans_a=True,
                    out_shape=(jax.ShapeDtypeStruct((N_CHIPS, D_MODEL, SHARD_W), BF16),),
                    out_specs=(pl.BlockSpec((None, D_MODEL, SHARD_W), lambda i, j, kk: (j, 0, 0)),),
                    epilogue=_store_bf16_epilogue)[0]
    dres1, g["mix_norm"] = _mm("d_mix", (t // tm, 1, N_CHIPS),
                               dp, pl.BlockSpec((tm, SHARD_W), lambda i, j, kk: (i, kk)),
                               w["w_in"], pl.BlockSpec((None, D_MODEL, SHARD_W), lambda i, j, kk: (kk, 0, 0)),
                               (tm, D_MODEL), trans_b=True,
                               extras=(x1, w["mix_norm"], dres2),
                               extra_specs=(pl.BlockSpec((tm, D_MODEL), lambda i, j, kk: (i, 0)),
                                            pl.BlockSpec((1, D_MODEL), lambda i, j, kk: (0, 0)),
                                            pl.BlockSpec((tm, D_MODEL), lambda i, j, kk: (i, 0))),
                               out_shape=(jax.ShapeDtypeStruct((t, D_MODEL), F32), jax.ShapeDtypeStruct((1, D_MODEL), F32)),
                               out_specs=(pl.BlockSpec((tm, D_MODEL), lambda i, j, kk: (i, 0)),
                                          pl.BlockSpec((1, D_MODEL), lambda i, j, kk: (0, 0))),
                               epilogue=_rms_bwd_epilogue, sem=("arbitrary", "arbitrary", "arbitrary"))

    comm.reduce_finish("R1", dres1)
    tokens = comm.reduce_start("R2", big)

    def ffn1_grads(dw_in4, dw_out):
        comm.reduce_finish("R2", dw_in4)
        return comm.reduce_start("R3", {"ffn1_w_in": dw_in4, "ffn1_w_out": dw_out})

    grad_x, _, _, g["ffn1_norm"] = _ffn_bwd(
        "ffn1b", dres1, x, w["ffn1_norm"], ffn1_saved, w["ffn1_w_in"], w["ffn1_w_out"], tokens, ffn1_grads)
    comm.reduce_finish("R3", grad_x)
    return loss[0, 0], grad_x, g


def _mesh_place():
    x, y, c = lax.axis_index("x"), lax.axis_index("y"), lax.axis_index("c")
    chips = [(1 - x, y), (x, 1 - y), (1 - x, 1 - y)]
    return x, y, c, chips


def _any_specs(n):
    return [pl.BlockSpec(memory_space=pl.ANY)] * n


HBM_SPEC = pl.BlockSpec(memory_space=pltpu.HBM)
SEM_SPEC = pl.BlockSpec(memory_space=pltpu.SEMAPHORE)
EFFECT = pltpu.SideEffectType.DATAFLOW_SIDE_EFFECTING


def _in_hbm(a):
    return pltpu.with_memory_space_constraint(a, pltpu.HBM)


def _copy_start(name, srcs, lands, plan, after=()):
    ns, nb = len(srcs), len(lands)
    n = 3 * ns

    def body(*refs):
        s_refs, l_refs = refs[:ns], refs[ns:ns + nb]
        send_sems, recv_sems = refs[ns + nb + len(after)], refs[ns + nb + len(after) + 1]
        token = refs[-1]
        for k, (src, dst, to, _) in enumerate(plan(s_refs, l_refs)):
            pltpu.make_async_remote_copy(src_ref=src, dst_ref=dst, send_sem=send_sems.at[k], recv_sem=recv_sems.at[k],
                                         device_id=to, device_id_type=MESH).start()
        token[...] = jnp.zeros_like(token)

    bufs = list(srcs) + list(lands)
    outs = pl.pallas_call(
        body, name=name,
        out_shape=(pltpu.SemaphoreType.DMA((n,)), pltpu.SemaphoreType.DMA((n,)),
                   *[pltpu.HBM(a.shape, a.dtype) for a in bufs], jax.ShapeDtypeStruct((8, LANES), F32)),
        in_specs=[HBM_SPEC] * len(bufs) + [pl.BlockSpec(memory_space=pl.ANY)] * len(after),
        out_specs=(SEM_SPEC, SEM_SPEC, *[HBM_SPEC] * len(bufs), pl.BlockSpec(memory_space=pltpu.VMEM)),
        input_output_aliases={i: 2 + i for i in range(len(bufs))},
        compiler_params=pltpu.CompilerParams(has_side_effects=EFFECT),
    )(*[_in_hbm(a) for a in bufs], *after)
    return outs[0], outs[1], list(outs[2:2 + ns]), list(outs[2 + ns:2 + ns + nb]), outs[-1]


def _copy_wait(name, send_sems, recv_sems, srcs, lands, after, plan):
    ns, nb = len(srcs), len(lands)

    def body(*refs):
        s_refs, l_refs = refs[:ns], refs[ns:ns + nb]
        send_sems, recv_sems = refs[ns + nb], refs[ns + nb + 1]
        for k, (src, _, to, mine) in enumerate(plan(s_refs, l_refs)):
            cp = pltpu.make_async_remote_copy(src_ref=src, dst_ref=mine, send_sem=send_sems.at[k], recv_sem=recv_sems.at[k],
                                              device_id=to, device_id_type=MESH)
            cp.wait_send()
            cp.wait_recv()

    bufs = list(srcs) + list(lands)
    outs = pl.pallas_call(
        body, name=name,
        out_shape=tuple(pltpu.HBM(a.shape, a.dtype) for a in bufs),
        in_specs=[HBM_SPEC] * len(bufs) + [SEM_SPEC, SEM_SPEC, pl.BlockSpec(memory_space=pl.ANY)],
        out_specs=tuple([HBM_SPEC] * len(bufs)),
        input_output_aliases={i: i for i in range(len(bufs))},
        compiler_params=pltpu.CompilerParams(has_side_effects=EFFECT),
    )(*bufs, send_sems, recv_sems, after)
    return list(outs[:ns]), list(outs[ns:])


def _gather_plan(s_refs, l_refs):
    x, y, c, chips = _mesh_place()
    jme = 2 * x + y
    return [(s.at[c], land.at[jme, c], (*chip, c), land.at[2 * chip[0] + chip[1], c])
            for s, land in zip(s_refs, l_refs) for chip in chips]


def _scatter_plan(s_refs, l_refs):
    x, y, c, chips = _mesh_place()
    return [(s.at[2 * chip[0] + chip[1]], land.at[k], (*chip, c), land.at[k])
            for s, land in zip(s_refs, l_refs) for k, chip in enumerate(chips)]


def _gather_forward(name, shards, landed):
    nw = len(shards)

    def body(*refs):
        s_refs, o_refs = refs[:nw], refs[2 * nw:3 * nw]
        send_sems, recv_sems = refs[3 * nw:]
        x, y, c, chips = _mesh_place()
        me, sib, jme = (x, y, c), (x, y, 1 - c), 2 * x + y
        sent = []
        for w in range(nw):
            parts = [(o_refs[w].at[2 * chip[0] + chip[1], c], o_refs[w].at[2 * chip[0] + chip[1], c]) for chip in chips]
            parts.append((s_refs[w], o_refs[w].at[jme]))
            for k, (src, dst) in enumerate(parts):
                cp = pltpu.make_async_remote_copy(src_ref=src, dst_ref=dst, send_sem=send_sems.at[4 * w + k],
                                                  recv_sem=recv_sems.at[4 * w + k], device_id=sib, device_id_type=MESH)
                cp.start()
                sent.append(cp)
        for w in range(nw):
            parts = [o_refs[w].at[2 * chip[0] + chip[1], 1 - c] for chip in chips] + [o_refs[w].at[jme]]
            for k, part in enumerate(parts):
                pltpu.make_async_remote_copy(src_ref=part, dst_ref=part, send_sem=send_sems.at[4 * w + k],
                                             recv_sem=recv_sems.at[4 * w + k], device_id=me, device_id_type=MESH).wait_recv()
        for cp in sent:
            cp.wait_send()

    return pl.pallas_call(
        body, name=name,
        in_specs=_any_specs(2 * nw), out_specs=_any_specs(nw),
        out_shape=[jax.ShapeDtypeStruct(a.shape, a.dtype) for a in landed],
        input_output_aliases={nw + i: i for i in range(nw)},
        scratch_shapes=[pltpu.SemaphoreType.DMA((4 * nw,)), pltpu.SemaphoreType.DMA((4 * nw,))],
    )(*shards, *landed)


def _exchange_halves(name, grads, after=()):
    nw = len(grads)

    def body(*refs):
        g_refs, o_refs = refs[:nw], refs[nw + len(after):2 * nw + len(after)]
        send_sems, recv_sems = refs[2 * nw + len(after):]
        x, y, c, _ = _mesh_place()
        copies = []
        for w in range(nw):
            cp = pltpu.make_async_remote_copy(src_ref=g_refs[w].at[:, 1 - c], dst_ref=o_refs[w], send_sem=send_sems.at[w],
                                              recv_sem=recv_sems.at[w], device_id=(x, y, 1 - c), device_id_type=MESH)
            cp.start()
            copies.append(cp)
        for cp in copies:
            cp.wait()

    return pl.pallas_call(
        body, name=name,
        in_specs=_any_specs(nw + len(after)), out_specs=_any_specs(nw),
        out_shape=[jax.ShapeDtypeStruct((N_CHIPS,) + g.shape[2:], g.dtype) for g in grads],
        scratch_shapes=[pltpu.SemaphoreType.DMA((nw,)), pltpu.SemaphoreType.DMA((nw,))],
    )(*grads, *after)


def _row_tile(r):
    for cand in (256, 176, 128, 64, 32, 16, 8):
        if r % cand == 0:
            return cand
    return r


def _add_own_half(c_idx, grad, got):
    _, _, r, cols = grad.shape
    tr = _row_tile(r)

    def body(c_ref, g_ref, o_ref, out_ref):
        out_ref[...] = (g_ref[...].astype(F32) + o_ref[...].astype(F32)).astype(BF16)

    return pl.pallas_call(
        body, name="add_own_half",
        grid_spec=pltpu.PrefetchScalarGridSpec(
            num_scalar_prefetch=1, grid=(N_CHIPS, r // tr),
            in_specs=[pl.BlockSpec((None, None, tr, cols), lambda j, i, c_ref: (j, c_ref[0], i, 0)),
                      pl.BlockSpec((None, tr, cols), lambda j, i, c_ref: (j, i, 0))],
            out_specs=pl.BlockSpec((None, tr, cols), lambda j, i, c_ref: (j, i, 0))),
        out_shape=jax.ShapeDtypeStruct((N_CHIPS, r, cols), BF16),
        compiler_params=_params(("parallel", "parallel")),
    )(c_idx, grad, got)


def _sum_pieces(place_idx, sums, landed):
    _, r, cols = sums.shape
    tr = _row_tile(r)

    def body(j_ref, own_ref, p_ref, o_ref):
        o_ref[...] = ((own_ref[...].astype(F32) + p_ref[0].astype(F32)) + p_ref[1].astype(F32)) + p_ref[2].astype(F32)

    return pl.pallas_call(
        body, name="sum_pieces",
        grid_spec=pltpu.PrefetchScalarGridSpec(
            num_scalar_prefetch=1, grid=(r // tr,),
            in_specs=[pl.BlockSpec((None, tr, cols), lambda i, j_ref: (j_ref[0], i, 0)),
                      pl.BlockSpec((N_CHIPS - 1, tr, cols), lambda i, j_ref: (0, i, 0))],
            out_specs=pl.BlockSpec((None, tr, cols), lambda i, j_ref: (j_ref[1], i, 0))),
        out_shape=jax.ShapeDtypeStruct((2, r, cols), F32),
        compiler_params=_params(("parallel",)),
    )(place_idx, sums, landed)


def _join_halves(name, halves):
    nw = len(halves)

    def body(*refs):
        o_refs = refs[nw:2 * nw]
        send_sems, recv_sems = refs[2 * nw:]
        x, y, c, _ = _mesh_place()
        copies = []
        for w in range(nw):
            cp = pltpu.make_async_remote_copy(src_ref=o_refs[w].at[c], dst_ref=o_refs[w].at[c], send_sem=send_sems.at[w],
                                              recv_sem=recv_sems.at[w], device_id=(x, y, 1 - c), device_id_type=MESH)
            cp.start()
            copies.append(cp)
        for w in range(nw):
            copies[w].wait_send()
            landed = o_refs[w].at[1 - c]
            pltpu.make_async_remote_copy(src_ref=landed, dst_ref=landed, send_sem=send_sems.at[w], recv_sem=recv_sems.at[w],
                                         device_id=(x, y, c), device_id_type=MESH).wait_recv()

    return pl.pallas_call(
        body, name=name,
        in_specs=_any_specs(nw), out_specs=_any_specs(nw),
        out_shape=[jax.ShapeDtypeStruct(h.shape, F32) for h in halves],
        input_output_aliases={i: i for i in range(nw)},
        scratch_shapes=[pltpu.SemaphoreType.DMA((nw,)), pltpu.SemaphoreType.DMA((nw,))],
    )(*halves)


SMALL_ROWS = 8


def _all_reduce_small(pack):
    rows, cols = pack.shape
    n_dev = 8

    def body(p_ref, o_ref, slots, send_sems, recv_sems):
        x, y, c, _ = _mesh_place()
        me = 4 * x + 2 * y + c
        slots[me] = p_ref[...]
        copies = []
        for k in range(1, n_dev):
            peer = (me + k) % n_dev
            cp = pltpu.make_async_remote_copy(src_ref=p_ref, dst_ref=slots.at[me], send_sem=send_sems.at[k],
                                              recv_sem=recv_sems.at[k],
                                              device_id=(peer // 4, (peer // 2) % 2, peer % 2), device_id_type=MESH)
            cp.start()
            copies.append(cp)
        for k in range(1, n_dev):
            src = (me + n_dev - k) % n_dev
            pltpu.make_async_remote_copy(src_ref=p_ref, dst_ref=slots.at[src], send_sem=send_sems.at[k],
                                         recv_sem=recv_sems.at[k], device_id=(x, y, c), device_id_type=MESH).wait_recv()
        for cp in copies:
            cp.wait_send()
        total = slots[0]
        for s in range(1, n_dev):
            total = total + slots[s]
        o_ref[...] = total

    return pl.pallas_call(
        body, name="all_reduce_small",
        in_specs=[pl.BlockSpec(memory_space=pltpu.VMEM)], out_specs=pl.BlockSpec(memory_space=pltpu.VMEM),
        out_shape=jax.ShapeDtypeStruct((rows, cols), F32),
        scratch_shapes=[pltpu.VMEM((n_dev, rows, cols), F32), pltpu.SemaphoreType.DMA((n_dev,)),
                        pltpu.SemaphoreType.DMA((n_dev,))],
    )(pack)


def _adamw(name, w, g, m, v):
    r, cols = w.shape
    tr = _row_tile(r)

    def body(w_ref, g_ref, m_ref, v_ref, d_ref, nm_ref, nv_ref):
        gv = g_ref[...]
        nm = ADAM_B1 * m_ref[...] + (1.0 - ADAM_B1) * gv
        nv = ADAM_B2 * v_ref[...] + (1.0 - ADAM_B2) * (gv * gv)
        m_hat = nm / (1.0 - ADAM_B1 ** ADAM_STEP)
        v_hat = nv / (1.0 - ADAM_B2 ** ADAM_STEP)
        d_ref[...] = -ADAM_LR * (m_hat / (jnp.sqrt(v_hat) + ADAM_EPS) + ADAM_WD * w_ref[...])
        nm_ref[...] = nm
        nv_ref[...] = nv

    spec = pl.BlockSpec((tr, cols), lambda i: (i, 0))
    return pl.pallas_call(
        body, name=name, grid=(r // tr,),
        in_specs=[spec] * 4, out_specs=[spec] * 3,
        out_shape=[jax.ShapeDtypeStruct((r, cols), F32)] * 3,
        compiler_params=_params(("parallel",)),
    )(w, g, m, v)


BIG = ["ffn1_w_in", "ffn1_w_out", "w_in", "conv_w_proj", "attn_w_o", "w_out", "ffn2_w_in", "ffn2_w_out", "conv_dw_kernel"]
COL_SHARDED = ("ffn1_w_in", "w_in", "ffn2_w_in")
SMALL = ["ffn1_norm", "mix_norm", "ffn2_norm", "conv_dw_bias", "conv_ln_g", "conv_ln_b", "q_norm", "k_norm", "attn_sinks", "rel_bias"]
WEIGHTS = ["ffn1_norm", "ffn1_w_in", "ffn1_w_out", "mix_norm", "w_in", "conv_dw_kernel", "conv_dw_bias", "conv_ln_g",
           "conv_ln_b", "conv_w_proj", "q_norm", "k_norm", "attn_sinks", "rel_bias", "attn_w_o", "w_out", "ffn2_norm",
           "ffn2_w_in", "ffn2_w_out"]
SMALL_PLACE = {"ffn1_norm": (0, 0, 1024), "mix_norm": (1, 0, 1024), "ffn2_norm": (2, 0, 1024), "conv_dw_bias": (3, 0, 1024),
               "conv_ln_g": (4, 0, 1024), "conv_ln_b": (5, 0, 1024), "q_norm": (6, 0, 64), "k_norm": (6, 128, 64),
               "attn_sinks": (6, 256, 16), "rel_bias": (7, 0, 512)}
LOSS_PLACE = (6, 384)


def _pack_small(vals, fill=0.0, loss=None):
    pack = jnp.full((SMALL_ROWS, D_MODEL), fill, F32)
    for name, (row, lane, n) in SMALL_PLACE.items():
        pack = pack.at[row, lane:lane + n].set(vals[name].reshape(n))
    if loss is not None:
        pack = pack.at[LOSS_PLACE[0], LOSS_PLACE[1]].set(loss)
    return pack


def _unpack_small(pack, shapes):
    return {name: pack[row, lane:lane + n].reshape(shapes[name]) for name, (row, lane, n) in SMALL_PLACE.items()}


def _shard_halves(name, a):
    if name == "conv_dw_kernel":
        a = jnp.pad(a, ((0, CONV_PAD - CONV_WIDTH), (0, 0)))
    r, cols = a.shape
    return a.reshape(2, r // 2, cols)


GATHER_GROUPS = {"A": ["ffn1_w_in", "ffn1_w_out"],
                 "B": ["w_in", "conv_dw_kernel", "conv_w_proj", "attn_w_o", "w_out"],
                 "C": ["ffn2_w_in", "ffn2_w_out"]}


class _MeshComm:
    def __init__(self, wts):
        self.c_idx = lax.axis_index("c").astype(jnp.int32).reshape(1)
        self.place_idx = jnp.stack([2 * lax.axis_index("x") + lax.axis_index("y"), lax.axis_index("c")]).astype(jnp.int32)
        self.wts, self.gathers, self.reductions, self.reduced = wts, {}, {}, {}
        self.tokens, self.last_join = (), ()
        self._gather_start("A", ())

    def _gather_start(self, group, after):
        names = GATHER_GROUPS[group]
        shards = [_shard_halves(n, self.wts[n]) if n == "conv_dw_kernel" else _shard_halves(n, self.wts[n]).astype(BF16)
                  for n in names]
        lands = [lax.empty((N_CHIPS,) + s.shape, s.dtype) for s in shards]
        self.gathers[group] = _copy_start("gather_start_" + group, shards, lands, _gather_plan, after=after)
        self.tokens = (self.gathers[group][-1],)

    def weights(self, group, after):
        send_sems, recv_sems, shards, lands, token = self.gathers.pop(group)
        shards, lands = _copy_wait("gather_wait_" + group, send_sems, recv_sems, shards, lands,
                                   token if after is None else after, _gather_plan)
        gathered = _gather_forward("gather_forward_" + group, shards, lands)
        self.tokens = ()
        following = {"A": "B", "B": "C"}.get(group)
        if following:
            self._gather_start(following, (gathered[0],))
        out = {}
        for n, g4 in zip(GATHER_GROUPS[group], gathered):
            r, cols = g4.shape[2] * 2, g4.shape[3]
            if n in COL_SHARDED:
                out[n] = g4.reshape(N_CHIPS, r, cols)
            elif n == "conv_dw_kernel":
                out[n] = g4.reshape(N_CHIPS, r, cols).transpose(1, 0, 2).reshape(r, N_CHIPS * cols)
            else:
                out[n] = g4.reshape(N_CHIPS * r, cols)
        return out

    def reduce_start(self, group, grads):
        names = list(grads)
        g4 = []
        for n in names:
            a = grads[n]
            if n == "conv_dw_kernel":
                a = a.reshape(CONV_PAD, N_CHIPS, -1).transpose(1, 0, 2)
            elif n not in COL_SHARDED:
                a = a.reshape(N_CHIPS, a.shape[0] // N_CHIPS, a.shape[1])
            g4.append(a.reshape(N_CHIPS, 2, a.shape[1] // 2, a.shape[2]))
        got = _exchange_halves("exchange_halves_" + group, g4, after=self.last_join)
        sums = [_add_own_half(self.c_idx, a, b) for a, b in zip(g4, got)]
        lands = [lax.empty((N_CHIPS - 1,) + s.shape[1:], s.dtype) for s in sums]
        started = _copy_start("scatter_start_" + group, sums, lands, _scatter_plan, after=self.last_join)
        self.reductions[group] = (names,) + started
        return (started[-1],)

    def reduce_finish(self, group, after):
        names, send_sems, recv_sems, sums, lands, _ = self.reductions.pop(group)
        sums, lands = _copy_wait("scatter_wait_" + group, send_sems, recv_sems, sums, lands, after, _scatter_plan)
        halves = [_sum_pieces(self.place_idx, s, p) for s, p in zip(sums, lands)]
        joined = _join_halves("join_halves_" + group, halves)
        self.last_join = (joined[0],)
        self.reduced.update(zip(names, joined))


def kernel(x, ffn1_norm, ffn1_w_in, ffn1_w_out, mix_norm, w_in, conv_dw_kernel, conv_dw_bias, conv_ln_g, conv_ln_b, conv_w_proj, q_norm, k_norm, attn_sinks, rel_bias, attn_w_o, w_out, ffn2_norm, ffn2_w_in, ffn2_w_out, loss_target, m_ffn1_norm, m_ffn1_w_in, m_ffn1_w_out, m_mix_norm, m_w_in, m_conv_dw_kernel, m_conv_dw_bias, m_conv_ln_g, m_conv_ln_b, m_conv_w_proj, m_q_norm, m_k_norm, m_attn_sinks, m_rel_bias, m_attn_w_o, m_w_out, m_ffn2_norm, m_ffn2_w_in, m_ffn2_w_out, v_ffn1_norm, v_ffn1_w_in, v_ffn1_w_out, v_mix_norm, v_w_in, v_conv_dw_kernel, v_conv_dw_bias, v_conv_ln_g, v_conv_ln_b, v_conv_w_proj, v_q_norm, v_k_norm, v_attn_sinks, v_rel_bias, v_attn_w_o, v_w_out, v_ffn2_norm, v_ffn2_w_in, v_ffn2_w_out):
    args = dict(locals())
    wts = {n: args[n] for n in WEIGHTS}
    mom = {n: args["m_" + n] for n in WEIGHTS}
    var = {n: args["v_" + n] for n in WEIGHTS}
    comm = _MeshComm(wts)
    small = {n: wts[n] if n in ("attn_sinks", "rel_bias") else wts[n].reshape(1, -1) for n in SMALL}
    loss_part, grad_x, g = _local_step(x[0], loss_target[0], small, comm)

    small_sum = _all_reduce_small(_pack_small(g, loss=loss_part))
    loss = small_sum[LOSS_PLACE[0], LOSS_PLACE[1]]
    small_shapes = {n: wts[n].shape for n in SMALL}
    g_small = _unpack_small(small_sum, small_shapes)

    grads, delta, new_m, new_v = {}, {}, {}, {}
    for n in BIG:
        j = comm.reduced[n]
        gs = j.reshape(j.shape[1] * 2, j.shape[2])
        pad = n == "conv_dw_kernel"
        ws, ms, vs = (_shard_halves(n, a).reshape(gs.shape) for a in (wts[n], mom[n], var[n]))
        d, nm, nv = _adamw("adamw_" + n, ws, gs, ms, vs)
        cut = (lambda a: a[:CONV_WIDTH]) if pad else (lambda a: a)
        grads[n], delta[n], new_m[n], new_v[n] = cut(gs), cut(d), cut(nm), cut(nv)
    d, nm, nv = _adamw("adamw_small", _pack_small(wts), small_sum, _pack_small(mom), _pack_small(var, fill=1.0))
    grads.update(g_small)
    delta.update(_unpack_small(d, small_shapes))
    new_m.update(_unpack_small(nm, small_shapes))
    new_v.update(_unpack_small(nv, small_shapes))

    return (loss, grad_x[None], *[grads[n] for n in WEIGHTS], *[delta[n] for n in WEIGHTS],
            *[new_m[n] for n in WEIGHTS], *[new_v[n] for n in WEIGHTS])
```

```python
import functools
import math

import jax
import jax.numpy as jnp
from jax import lax
from jax.experimental import pallas as pl
from jax.experimental.pallas import tpu as pltpu

F32 = jnp.float32
BF16 = jnp.bfloat16
MESH = pl.DeviceIdType.MESH

EPS = 1e-6
D_MODEL = 1024
D_FF = 2816
N_CHIPS = 4
SHARD_W = 2 * D_FF // N_CHIPS
HEAD_DIM = 64
N_Q_HEADS = 16
N_KV_HEADS = 4
GROUP = N_Q_HEADS // N_KV_HEADS
BLOCK = 128
QROWS = GROUP * BLOCK
N_BUCKETS = 32
MAX_DISTANCE = 128
CONV_WIDTH = 31
CONV_PAD = 32
NEG = float(jnp.finfo(jnp.float32).min)

ADAM_LR = 0.001
ADAM_B1 = 0.9
ADAM_B2 = 0.999
ADAM_EPS = 1e-08
ADAM_WD = 0.01
ADAM_STEP = 10

VMEM_LIMIT_BYTES = 56 * 1024 * 1024
ROW_TILE = 1024
TOKEN_TILE = 1024
CONV_TILE = 256
CONV_ROWS = 128
LANES = 128

COL_CONV_A, COL_CONV_G, COL_Q, COL_K, COL_V, COL_GC, COL_GA = 0, 1024, 2048, 3072, 3328, 3584, 4608
IN_W = 5632


def _params(sem, vmem=VMEM_LIMIT_BYTES):
    return pltpu.CompilerParams(dimension_semantics=sem, vmem_limit_bytes=vmem)


def _sigmoid(x):
    return 1.0 / (1.0 + jnp.exp(-x))


def _dot(a, b, trans_a=False, trans_b=False, precision=None):
    dn = (((0,) if trans_a else (1,), (1,) if trans_b else (0,)), ((), ()))
    return lax.dot_general(a, b, dn, preferred_element_type=F32, precision=precision)


def _mm(name, grid, a, a_spec, b, b_spec, acc_shape, *, trans_a=False, trans_b=False, a_pre=None, b_pre=None,
        extras=(), extra_specs=(), tokens=(), out_shape, out_specs, epilogue, chunked=False,
        sem=("parallel", "parallel", "arbitrary")):
    n_k = grid[2]
    assert not chunked or (n_k == 1 and b_pre is None)
    extras = tuple(extras) + tuple(tokens)
    extra_specs = tuple(extra_specs) + (pl.BlockSpec((8, LANES), lambda i, j, kk: (0, 0)),) * len(tokens)
    n_extra = len(extras)
    n_out = len(out_shape)

    def body(a_ref, b_ref, *rest):
        ex = rest[:n_extra]
        outs = rest[n_extra:n_extra + n_out]
        ids = (pl.program_id(0), pl.program_id(1), pl.program_id(2))
        av = a_ref[...]
        if a_pre is not None:
            av = a_pre(av)
        if chunked:
            for c0, cw in _col_chunks(acc_shape[1]):
                cols = pl.ds(c0, cw)
                epilogue(_dot(av, b_ref[cols, :] if trans_b else b_ref[:, cols], trans_a, trans_b), ex, outs, ids, cols)
            return
        bv = b_ref[...]
        if b_pre is not None:
            bv = b_pre(bv)
        part = _dot(av, bv, trans_a, trans_b)
        if n_k == 1:
            epilogue(part, ex, outs, ids)
        else:
            acc = rest[-1]

            @pl.when(ids[2] == 0)
            def _():
                acc[...] = part

            @pl.when(ids[2] > 0)
            def _():
                acc[...] += part

            @pl.when(ids[2] == n_k - 1)
            def _():
                epilogue(acc[...], ex, outs, ids)

    scratch = [] if n_k == 1 else [pltpu.VMEM(acc_shape, F32)]
    return pl.pallas_call(
        body, name=name, grid=grid,
        in_specs=[a_spec, b_spec, *extra_specs],
        out_specs=list(out_specs), out_shape=list(out_shape),
        scratch_shapes=scratch, compiler_params=_params(sem),
    )(a, b, *extras)


MXU_WIDTH = 256


def _col_chunks(n, width=2 * MXU_WIDTH):
    return [(c0, min(width, n - c0)) for c0 in range(0, n, width)]


def _half_bf16(v):
    return (0.5 * v).astype(BF16)


def _to_bf16(v):
    return v.astype(BF16)


def _rmsnorm_fwd(name, x, g, tokens=()):
    t, d = x.shape
    tm = min(ROW_TILE, t)

    def body(x_ref, g_ref, *rest):
        o_ref = rest[-1]
        xv = x_ref[...]
        r = lax.rsqrt(jnp.mean(xv * xv, axis=-1, keepdims=True) + EPS)
        o_ref[...] = (xv * r * g_ref[...]).astype(BF16)

    return pl.pallas_call(
        body, name=name, grid=(t // tm,),
        in_specs=[pl.BlockSpec((tm, d), lambda i: (i, 0)), pl.BlockSpec((1, d), lambda i: (0, 0))]
        + [pl.BlockSpec((8, LANES), lambda i: (0, 0))] * len(tokens),
        out_specs=pl.BlockSpec((tm, d), lambda i: (i, 0)),
        out_shape=jax.ShapeDtypeStruct((t, d), BF16),
        compiler_params=_params(("parallel",)),
    )(x, g, *tokens)


def _rms_bwd_epilogue(acc, ex, outs, ids):
    x_ref, g_ref, dres_ref = ex[:3]
    out_ref, dg_ref = outs
    xv = x_ref[...]
    r = lax.rsqrt(jnp.mean(xv * xv, axis=-1, keepdims=True) + EPS)
    w = acc * g_ref[...]
    dx = r * w - xv * (r * r * r) * jnp.mean(xv * w, axis=-1, keepdims=True)
    out_ref[...] = dres_ref[...] + dx
    part = jnp.sum(acc * (xv * r), axis=0, keepdims=True)

    @pl.when(ids[0] == 0)
    def _():
        dg_ref[...] = part

    @pl.when(ids[0] > 0)
    def _():
        dg_ref[...] += part


def _ffn_in(name, n, w_in4, tokens=()):
    t, d = n.shape
    tm = min(ROW_TILE, t)

    def body(n_ref, wa_ref, wb_ref, *rest):
        ab_ref, h_ref = rest[-2:]
        nv = n_ref[...]
        for c0, cw in _col_chunks(SHARD_W):
            cols = pl.ds(c0, cw)
            a = _dot(nv, wa_ref[:, cols])
            b = _dot(nv, wb_ref[:, cols])
            h_ref[:, cols] = (a * _sigmoid(a) * b).astype(BF16)
            ab_ref[0, :, cols] = a.astype(BF16)
            ab_ref[1, :, cols] = b.astype(BF16)

    return pl.pallas_call(
        body, name=name, grid=(2, t // tm),
        in_specs=[pl.BlockSpec((tm, d), lambda j, i: (i, 0)),
                  pl.BlockSpec((None, d, SHARD_W), lambda j, i: (j, 0, 0)),
                  pl.BlockSpec((None, d, SHARD_W), lambda j, i: (j + 2, 0, 0))]
        + [pl.BlockSpec((8, LANES), lambda j, i: (0, 0))] * len(tokens),
        out_specs=[pl.BlockSpec((2, tm, SHARD_W), lambda j, i: (0, i, j)),
                   pl.BlockSpec((tm, SHARD_W), lambda j, i: (i, j))],
        out_shape=[jax.ShapeDtypeStruct((2, t, D_FF), BF16), jax.ShapeDtypeStruct((t, D_FF), BF16)],
        compiler_params=_params(("parallel", "parallel")),
    )(n, w_in4, w_in4, *tokens)


def _mm_residual(name, a, w, res, scale):
    t, k = a.shape
    n = w.shape[1]
    tm = min(ROW_TILE, t)

    def epilogue(acc, ex, outs, ids):
        outs[0][...] = ex[0][...] + scale * acc

    return _mm(name, (t // tm, 1, 1), a, pl.BlockSpec((tm, k), lambda i, j, kk: (i, 0)),
               w, pl.BlockSpec((k, n), lambda i, j, kk: (0, 0)), (tm, n),
               extras=(res,), extra_specs=(pl.BlockSpec((tm, n), lambda i, j, kk: (i, 0)),),
               out_shape=(jax.ShapeDtypeStruct((t, n), F32),),
               out_specs=(pl.BlockSpec((tm, n), lambda i, j, kk: (i, 0)),), epilogue=epilogue)[0]


def _ffn_fwd(tag, x, n, w_in4, w_out, tokens=()):
    ab, h = _ffn_in(tag + "_in", n, w_in4, tokens)
    y = _mm_residual(tag + "_out", h, w_out, x, 0.5)
    return y, (n, ab, h)


def _ffn_bwd(tag, dres, x, g, saved, w_in4, w_out, tokens=(), on_weight_grads=None):
    n, ab, h = saved
    t, d = x.shape
    tm = min(ROW_TILE, t)
    tk = min(TOKEN_TILE, t)
    half_w = SHARD_W

    def dact_epilogue(acc, ex, outs, ids, cols):
        a = ex[0][0, :, cols].astype(F32)
        b = ex[0][1, :, cols].astype(F32)
        sig = _sigmoid(a)
        outs[0][0, :, cols] = (acc * b * (sig * (1.0 + a * (1.0 - sig)))).astype(BF16)
        outs[0][1, :, cols] = (acc * (a * sig)).astype(BF16)

    du = _mm(tag + "_dact", (2, t // tm, 1),
             dres, pl.BlockSpec((tm, d), lambda j, i, kk: (i, 0)),
             w_out, pl.BlockSpec((half_w, d), lambda j, i, kk: (j, 0)), (tm, half_w),
             trans_b=True, a_pre=_half_bf16,
             extras=(ab,), extra_specs=(pl.BlockSpec((2, tm, half_w), lambda j, i, kk: (0, i, j)),), tokens=tokens,
             out_shape=(jax.ShapeDtypeStruct((2, t, D_FF), BF16),),
             out_specs=(pl.BlockSpec((2, tm, half_w), lambda j, i, kk: (0, i, j)),),
             epilogue=dact_epilogue, chunked=True)[0]

    def store_epilogue(acc, ex, outs, ids):
        outs[0][...] = acc.astype(BF16)

    dw_out = _mm(tag + "_dwout", (2, 1, t // tk),
                 h, pl.BlockSpec((tk, half_w), lambda i, j, kk: (kk, i)),
                 dres, pl.BlockSpec((tk, d), lambda i, j, kk: (kk, 0)), (half_w, d),
                 trans_a=True, b_pre=_half_bf16,
                 out_shape=(jax.ShapeDtypeStruct((D_FF, d), BF16),),
                 out_specs=(pl.BlockSpec((half_w, d), lambda i, j, kk: (i, 0)),),
                 epilogue=store_epilogue)[0]

    dw_in4 = _mm(tag + "_dwin", (1, N_CHIPS, t // tk),
                 n, pl.BlockSpec((tk, d), lambda i, j, kk: (kk, 0)),
                 du, pl.BlockSpec((None, tk, SHARD_W), lambda i, j, kk: (j // 2, kk, j % 2)), (d, SHARD_W),
                 trans_a=True,
                 out_shape=(jax.ShapeDtypeStruct((N_CHIPS, d, SHARD_W), BF16),),
                 out_specs=(pl.BlockSpec((None, d, SHARD_W), lambda i, j, kk: (j, 0, 0)),),
                 epilogue=store_epilogue)[0]

    late = () if on_weight_grads is None else on_weight_grads(dw_in4, dw_out)

    dx, dg = _mm(tag + "_dn", (t // tm, 1, N_CHIPS),
                 du, pl.BlockSpec((None, tm, SHARD_W), lambda i, j, kk: (kk // 2, i, kk % 2)),
                 w_in4, pl.BlockSpec((None, d, SHARD_W), lambda i, j, kk: (kk, 0, 0)), (tm, d),
                 trans_b=True,
                 extras=(x, g, dres),
                 extra_specs=(pl.BlockSpec((tm, d), lambda i, j, kk: (i, 0)),
                              pl.BlockSpec((1, d), lambda i, j, kk: (0, 0)),
                              pl.BlockSpec((tm, d), lambda i, j, kk: (i, 0))), tokens=late,
                 out_shape=(jax.ShapeDtypeStruct((t, d), F32), jax.ShapeDtypeStruct((1, d), F32)),
                 out_specs=(pl.BlockSpec((tm, d), lambda i, j, kk: (i, 0)),
                            pl.BlockSpec((1, d), lambda i, j, kk: (0, 0))),
                 epilogue=_rms_bwd_epilogue, sem=("arbitrary", "arbitrary", "arbitrary"))
    return dx, dw_in4, dw_out, dg


def _loss_head(y, target):
    t, d = y.shape
    tm = min(ROW_TILE, t)

    def body(y_ref, t_ref, dy_ref, loss_ref):
        diff = y_ref[...] - t_ref[...]
        dy_ref[...] = diff * (1.0 / d)
        part = jnp.full((8, LANES), 0.5 / d * jnp.sum(diff * diff), F32)
        i = pl.program_id(0)

        @pl.when(i == 0)
        def _():
            loss_ref[...] = part

        @pl.when(i > 0)
        def _():
            loss_ref[...] += part

    return pl.pallas_call(
        body, name="loss_head", grid=(t // tm,),
        in_specs=[pl.BlockSpec((tm, d), lambda i: (i, 0)), pl.BlockSpec((tm, d), lambda i: (i, 0))],
        out_specs=[pl.BlockSpec((tm, d), lambda i: (i, 0)), pl.BlockSpec((8, LANES), lambda i: (0, 0))],
        out_shape=[jax.ShapeDtypeStruct((t, d), F32), jax.ShapeDtypeStruct((8, LANES), F32)],
        compiler_params=_params(("arbitrary",)),
    )(y, target)


def _conv_fill(zp_ref, a_ref, g_ref, ah_ref, gh_ref, i):
    zh = ah_ref[...].astype(F32) * _sigmoid(gh_ref[...].astype(F32))
    zp_ref[pl.ds(0, CONV_PAD), :] = jnp.where(i > 0, zh, 0.0)
    zp_ref[pl.ds(CONV_PAD, a_ref.shape[0]), :] = a_ref[...].astype(F32) * _sigmoid(g_ref[...].astype(F32))


def _shift_groups(shifts):
    groups = {}
    for j, s in shifts:
        groups.setdefault(s % 8, []).append((j, s // 8))
    return groups


def _windows(zp_ref, r0, lanes, groups):
    for q, taps in groups.items():
        deepest = max(p for _, p in taps)
        win = zp_ref[pl.ds(r0 + q, 8 * deepest + CONV_ROWS), lanes]
        for j, p in taps:
            yield j, win[8 * p:8 * p + CONV_ROWS]


def _conv_apply(zp_ref, out_ref, dw_ref, bias_ref, tm, ch, shifts):
    groups = _shift_groups(shifts)
    for cc in range(ch // LANES):
        lanes = pl.ds(cc * LANES, LANES)
        w = [dw_ref[pl.ds(j, 1), lanes] for j in range(CONV_WIDTH)]
        for r0 in range(0, tm, CONV_ROWS):
            if bias_ref is None:
                acc = jnp.zeros((CONV_ROWS, LANES), F32)
            else:
                acc = jnp.broadcast_to(bias_ref[:, lanes], (CONV_ROWS, LANES))
            for j, rows in _windows(zp_ref, r0, lanes, groups):
                acc = acc + w[j] * rows
            out_ref[pl.ds(r0, CONV_ROWS), lanes] = acc


FWD_SHIFTS = [(j, CONV_PAD - (CONV_WIDTH - 1) + j) for j in range(CONV_WIDTH)]
BWD_SHIFTS = [(j, CONV_WIDTH - 1 - j) for j in range(CONV_WIDTH)]


def _conv_taps(zp_ref, z1_ref, dw_ref, bias_ref, tm, ch):
    _conv_apply(zp_ref, z1_ref, dw_ref, bias_ref, tm, ch, FWD_SHIFTS)


def _conv_specs(tm, ch):
    per = tm // CONV_PAD
    cb = COL_CONV_G // ch
    return [pl.BlockSpec((tm, ch), lambda i: (i, 0)),
            pl.BlockSpec((tm, ch), lambda i: (i, cb)),
            pl.BlockSpec((CONV_PAD, ch), lambda i: (jnp.maximum(i * per - 1, 0), 0)),
            pl.BlockSpec((CONV_PAD, ch), lambda i: (jnp.maximum(i * per - 1, 0), cb))]


def _conv_fwd(p, dw, bias, ln_g, ln_b):
    t = p.shape[0]
    ch = D_MODEL
    tm = min(CONV_TILE, t)

    def body(a_ref, g_ref, ah_ref, gh_ref, dw_ref, bias_ref, lg_ref, lb_ref, o_ref, z1_ref, zp_ref):
        i = pl.program_id(0)
        _conv_fill(zp_ref, a_ref, g_ref, ah_ref, gh_ref, i)
        _conv_taps(zp_ref, z1_ref, dw_ref, bias_ref, tm, ch)
        z1 = z1_ref[...]
        mu = jnp.mean(z1, axis=-1, keepdims=True)
        zc = z1 - mu
        rs = lax.rsqrt(jnp.mean(zc * zc, axis=-1, keepdims=True) + EPS)
        z2 = zc * rs * lg_ref[...] + lb_ref[...]
        o_ref[...] = (z2 * _sigmoid(z2)).astype(BF16)

    vec = pl.BlockSpec((1, ch), lambda i: (0, 0))
    return pl.pallas_call(
        body, name="conv_fwd", grid=(t // tm,),
        in_specs=_conv_specs(tm, ch) + [pl.BlockSpec((CONV_PAD, ch), lambda i: (0, 0)), vec, vec, vec],
        out_specs=[pl.BlockSpec((tm, ch), lambda i: (i, 0)), pl.BlockSpec((tm, ch), lambda i: (i, 0))],
        out_shape=[jax.ShapeDtypeStruct((t, ch), BF16), jax.ShapeDtypeStruct((t, ch), F32)],
        scratch_shapes=[pltpu.VMEM((CONV_PAD + tm, ch), F32)],
        compiler_params=_params(("parallel",)),
    )(p, p, p, p, dw, bias, ln_g, ln_b)


def _conv_bwd_ln(p, z1_saved, dz3, ln_g, ln_b):
    t = p.shape[0]
    ch = D_MODEL
    tm = min(CONV_TILE, t)

    def body(a_ref, g_ref, ah_ref, gh_ref, z1_ref, dz3_ref, lg_ref, lb_ref,
             dz1_ref, ddw_ref, dbias_ref, dlg_ref, dlb_ref, zp_ref):
        i = pl.program_id(0)
        _conv_fill(zp_ref, a_ref, g_ref, ah_ref, gh_ref, i)
        z1 = z1_ref[...]
        mu = jnp.mean(z1, axis=-1, keepdims=True)
        zc = z1 - mu
        rs = lax.rsqrt(jnp.mean(zc * zc, axis=-1, keepdims=True) + EPS)
        xh = zc * rs
        z2 = xh * lg_ref[...] + lb_ref[...]
        sig = _sigmoid(z2)
        dz2 = dz3_ref[...].astype(F32) * (sig * (1.0 + z2 * (1.0 - sig)))
        dxh = dz2 * lg_ref[...]
        dz1 = rs * (dxh - jnp.mean(dxh, axis=-1, keepdims=True) - xh * jnp.mean(dxh * xh, axis=-1, keepdims=True))
        dz1_ref[...] = dz1

        @pl.when(i == 0)
        def _():
            ddw_ref[...] = jnp.zeros_like(ddw_ref)
            dbias_ref[...] = jnp.zeros_like(dbias_ref)
            dlg_ref[...] = jnp.zeros_like(dlg_ref)
            dlb_ref[...] = jnp.zeros_like(dlb_ref)

        dlg_ref[...] += jnp.sum(dz2 * xh, axis=0, keepdims=True)
        dlb_ref[...] += jnp.sum(dz2, axis=0, keepdims=True)
        dbias_ref[...] += jnp.sum(dz1, axis=0, keepdims=True)
        groups = _shift_groups(FWD_SHIFTS)
        for cc in range(ch // LANES):
            lanes = pl.ds(cc * LANES, LANES)
            accs = [jnp.zeros((8, LANES), F32) for _ in range(CONV_WIDTH)]
            for r0 in range(0, tm, CONV_ROWS):
                dzc = dz1_ref[pl.ds(r0, CONV_ROWS), lanes]
                for j, rows in _windows(zp_ref, r0, lanes, groups):
                    accs[j] = accs[j] + jnp.sum((dzc * rows).reshape(CONV_ROWS // 8, 8, LANES), axis=0)
            for j in range(CONV_WIDTH):
                ddw_ref[pl.ds(j, 1), lanes] += jnp.sum(accs[j], axis=0, keepdims=True)

    vec = pl.BlockSpec((1, ch), lambda i: (0, 0))
    return pl.pallas_call(
        body, name="conv_bwd_ln", grid=(t // tm,),
        in_specs=_conv_specs(tm, ch) + [pl.BlockSpec((tm, ch), lambda i: (i, 0)),
                                        pl.BlockSpec((tm, ch), lambda i: (i, 0)), vec, vec],
        out_specs=[pl.BlockSpec((tm, ch), lambda i: (i, 0)), pl.BlockSpec((CONV_PAD, ch), lambda i: (0, 0)), vec, vec, vec],
        out_shape=[jax.ShapeDtypeStruct((t, ch), F32), jax.ShapeDtypeStruct((CONV_PAD, ch), F32)]
        + [jax.ShapeDtypeStruct((1, ch), F32)] * 3,
        scratch_shapes=[pltpu.VMEM((CONV_PAD + tm, ch), F32)],
        compiler_params=_params(("arbitrary",)),
    )(p, p, p, p, z1_saved, dz3, ln_g, ln_b)


def _conv_bwd_glu(p, dz1, dw, dq, dkv, dgates):
    t = p.shape[0]
    ch = D_MODEL
    tm = min(CONV_TILE, t)
    per = tm // CONV_PAD
    n_halo = t // CONV_PAD
    cb = COL_CONV_G // ch

    def body(a_ref, g_ref, dz_ref, dzn_ref, dw_ref, dq_ref, dkv_ref, dgates_ref, o_ref, zp_ref, z0_ref):
        i = pl.program_id(0)
        o_ref[:, pl.ds(COL_Q, Q_W)] = dq_ref[...]
        o_ref[:, pl.ds(COL_K, 2 * KV_W)] = dkv_ref[...]
        o_ref[:, pl.ds(COL_GC, ch)] = dgates_ref[0]
        o_ref[:, pl.ds(COL_GA, ch)] = dgates_ref[1]
        zp_ref[pl.ds(0, tm), :] = dz_ref[...]
        zp_ref[pl.ds(tm, CONV_PAD), :] = jnp.where(i < t // tm - 1, dzn_ref[...], 0.0)
        _conv_apply(zp_ref, z0_ref, dw_ref, None, tm, ch, BWD_SHIFTS)
        dz0 = z0_ref[...]
        a = a_ref[...].astype(F32)
        sig = _sigmoid(g_ref[...].astype(F32))
        o_ref[:, pl.ds(0, ch)] = (dz0 * sig).astype(BF16)
        o_ref[:, pl.ds(ch, ch)] = (dz0 * a * sig * (1.0 - sig)).astype(BF16)

    return pl.pallas_call(
        body, name="conv_bwd_glu", grid=(t // tm,),
        in_specs=[pl.BlockSpec((tm, ch), lambda i: (i, 0)), pl.BlockSpec((tm, ch), lambda i: (i, cb)),
                  pl.BlockSpec((tm, ch), lambda i: (i, 0)),
                  pl.BlockSpec((CONV_PAD, ch), lambda i: (jnp.minimum((i + 1) * per, n_halo - 1), 0)),
                  pl.BlockSpec((CONV_PAD, ch), lambda i: (0, 0)),
                  pl.BlockSpec((tm, Q_W), lambda i: (i, 0)), pl.BlockSpec((tm, 2 * KV_W), lambda i: (i, 0)),
                  pl.BlockSpec((2, tm, ch), lambda i: (0, i, 0))],
        out_specs=pl.BlockSpec((tm, IN_W), lambda i: (i, 0)),
        out_shape=jax.ShapeDtypeStruct((t, IN_W), BF16),
        scratch_shapes=[pltpu.VMEM((tm + CONV_PAD, ch), F32), pltpu.VMEM((tm, ch), F32)],
        compiler_params=_params(("parallel",)),
    )(p, p, dz1, dz1, dw, dq, dkv, dgates)


def _bucket_onehot():
    qi = jnp.arange(BLOCK, dtype=jnp.int32)[:, None]
    kj = jnp.arange(2 * BLOCK, dtype=jnp.int32)[None, :]
    dist = jnp.maximum(qi + BLOCK - kj, 0)
    max_exact = N_BUCKETS // 2
    dflt = jnp.maximum(dist, 1).astype(F32)
    large = max_exact + (jnp.log(dflt / max_exact) / math.log(MAX_DISTANCE / max_exact)
                         * (N_BUCKETS - max_exact)).astype(jnp.int32)
    large = jnp.minimum(large, N_BUCKETS - 1)
    bucket = jnp.where(dist < max_exact, dist, large)
    onehot = bucket[None] == jnp.arange(N_BUCKETS, dtype=jnp.int32)[:, None, None]
    return onehot.astype(F32).reshape(N_BUCKETS, BLOCK * 2 * BLOCK)


def _bias_table(rel_bias_t, onehot):
    n = onehot.shape[1]
    tn = 4096

    def body(r_ref, oh_ref, o_ref):
        flat = pl.program_id(0) * tn + lax.broadcasted_iota(jnp.int32, (N_Q_HEADS, tn), 1)
        dist = (flat // (2 * BLOCK)) + BLOCK - (flat % (2 * BLOCK))
        bias = _dot(r_ref[...], oh_ref[...], precision=lax.Precision.HIGHEST)
        o_ref[...] = jnp.where((dist >= 0) & (dist < BLOCK), bias, NEG)

    return pl.pallas_call(
        body, name="bias_table", grid=(n // tn,),
        in_specs=[pl.BlockSpec((N_Q_HEADS, N_BUCKETS), lambda i: (0, 0)), pl.BlockSpec((N_BUCKETS, tn), lambda i: (0, i))],
        out_specs=pl.BlockSpec((N_Q_HEADS, tn), lambda i: (0, i)),
        out_shape=jax.ShapeDtypeStruct((N_Q_HEADS, n), F32),
        compiler_params=_params(("parallel",)),
    )(rel_bias_t, onehot)


def _bias_table_bwd(dbias, onehot):
    n = onehot.shape[1]
    tn = 4096

    def body(d_ref, oh_ref, o_ref):
        part = _dot(d_ref[...], oh_ref[...], trans_b=True, precision=lax.Precision.HIGHEST)
        i = pl.program_id(0)

        @pl.when(i == 0)
        def _():
            o_ref[...] = part

        @pl.when(i > 0)
        def _():
            o_ref[...] += part

    return pl.pallas_call(
        body, name="bias_table_bwd", grid=(n // tn,),
        in_specs=[pl.BlockSpec((N_Q_HEADS, tn), lambda i: (0, i)), pl.BlockSpec((N_BUCKETS, tn), lambda i: (0, i))],
        out_specs=pl.BlockSpec((N_Q_HEADS, N_BUCKETS), lambda i: (0, 0)),
        out_shape=jax.ShapeDtypeStruct((N_Q_HEADS, N_BUCKETS), F32),
        compiler_params=_params(("arbitrary",)),
    )(dbias, onehot)


def _lane_head(rows):
    return lax.broadcasted_iota(jnp.int32, (rows, KV_W), 1) // HEAD_DIM


def _group_rms(x, gain_wide):
    head = _lane_head(x.shape[0])
    sq = x * x
    r = jnp.zeros_like(x)
    for i in range(N_KV_HEADS):
        ms = jnp.sum(jnp.where(head == i, sq, 0.0), axis=-1, keepdims=True) * (1.0 / HEAD_DIM)
        r = jnp.where(head == i, lax.rsqrt(ms + EPS), r)
    return r, x * r * gain_wide


def _stack_heads(group):
    head = _lane_head(group.shape[0])
    return jnp.concatenate([jnp.where(head == i, group, jnp.zeros_like(group)) for i in range(N_KV_HEADS)], axis=0)


def _unstack_heads(stacked):
    head = _lane_head(BLOCK)
    out = jnp.where(head == 0, stacked[:BLOCK], 0.0)
    for i in range(1, N_KV_HEADS):
        out = out + jnp.where(head == i, stacked[i * BLOCK:(i + 1) * BLOCK], 0.0)
    return out


def _repeaters():
    row = lax.broadcasted_iota(jnp.int32, (KV_W, KV_W), 0)
    col = lax.broadcasted_iota(jnp.int32, (KV_W, KV_W), 1)
    return [(row == h * HEAD_DIM + col % HEAD_DIM).astype(BF16) for h in range(N_KV_HEADS)]


def _attn_probs(q_stack, k_rep, sink, bias, before_start):
    s = _dot(q_stack, k_rep, trans_b=True) * (1.0 / math.sqrt(HEAD_DIM)) + bias
    s = jnp.where(before_start, NEG, s)
    m = jnp.maximum(jnp.max(s, axis=-1, keepdims=True), sink)
    p = jnp.exp(s - m)
    es = jnp.exp(sink - m)
    inv = 1.0 / (jnp.sum(p, axis=-1, keepdims=True) + es)
    return p * inv, es * inv


def _before_start(n):
    col = lax.broadcasted_iota(jnp.int32, (QROWS, 2 * BLOCK), 1)
    return (col < BLOCK) & (n == 0)


KV_W = N_KV_HEADS * HEAD_DIM
Q_W = N_Q_HEADS * HEAD_DIM


def _attn_specs():
    qspec = pl.BlockSpec((BLOCK, Q_W), lambda n: (n, COL_Q // Q_W))
    kprev = pl.BlockSpec((BLOCK, KV_W), lambda n: (jnp.maximum(n - 1, 0), COL_K // KV_W))
    kcur = pl.BlockSpec((BLOCK, KV_W), lambda n: (n, COL_K // KV_W))
    vprev = pl.BlockSpec((BLOCK, KV_W), lambda n: (jnp.maximum(n - 1, 0), COL_V // KV_W))
    vcur = pl.BlockSpec((BLOCK, KV_W), lambda n: (n, COL_V // KV_W))
    gain = pl.BlockSpec((1, KV_W), lambda n: (0, 0))
    sink = pl.BlockSpec((N_KV_HEADS, QROWS, 1), lambda n: (0, 0, 0))
    bias = pl.BlockSpec((N_KV_HEADS, QROWS, 2 * BLOCK), lambda n: (0, 0, 0))
    return [qspec, kprev, kcur, vprev, vcur], gain, sink, bias


def _attn_fwd(p, gq, gk, sink_rows, bias):
    t = p.shape[0]
    nb = t // BLOCK
    qkv, gain, sink, bspec = _attn_specs()

    def body(q_ref, kp_ref, kc_ref, vp_ref, vc_ref, gq_ref, gk_ref, sink_ref, bias_ref, o_ref):
        before_start = _before_start(pl.program_id(0))
        rep = _repeaters()
        kf = jnp.concatenate([kp_ref[...], kc_ref[...]], axis=0).astype(F32)
        kn = _group_rms(kf, gk_ref[...])[1].astype(BF16)
        v = jnp.concatenate([vp_ref[...], vc_ref[...]], axis=0)
        for h in range(N_KV_HEADS):
            qn = _group_rms(q_ref[:, pl.ds(h * KV_W, KV_W)].astype(F32), gq_ref[...])[1]
            k_rep = _dot(kn, rep[h]).astype(BF16)
            pn = _attn_probs(_stack_heads(qn).astype(BF16), k_rep, sink_ref[h], bias_ref[h], before_start)[0]
            v_rep = _dot(v, rep[h]).astype(BF16)
            o_ref[:, pl.ds(h * KV_W, KV_W)] = _unstack_heads(_dot(pn.astype(BF16), v_rep)).astype(BF16)

    return pl.pallas_call(
        body, name="attn_fwd", grid=(nb,),
        in_specs=qkv + [gain, gain, sink, bspec],
        out_specs=pl.BlockSpec((BLOCK, Q_W), lambda n: (n, 0)), out_shape=jax.ShapeDtypeStruct((t, Q_W), BF16),
        compiler_params=_params(("parallel",)),
    )(p, p, p, p, p, gq, gk, sink_rows, bias)


def _attn_bwd(p, do, gq, gk, sink_rows, bias):
    t = p.shape[0]
    nb = t // BLOCK
    qkv, _, sink, bspec = _attn_specs()
    gain = pl.BlockSpec((1, HEAD_DIM), lambda n: (0, 0))
    scale = 1.0 / math.sqrt(HEAD_DIM)

    def head_selectors():
        row = lax.broadcasted_iota(jnp.int32, (KV_W, HEAD_DIM), 0)
        col = lax.broadcasted_iota(jnp.int32, (KV_W, HEAD_DIM), 1)
        return [(row == col + i * HEAD_DIM).astype(BF16) for i in range(N_KV_HEADS)]

    def take_heads(group, sel):
        return jnp.concatenate([_dot(group, s) for s in sel], axis=0)

    def put_heads(x, sel):
        rows = x.shape[0] // len(sel)
        out = _dot(x[:rows].astype(BF16), sel[0], trans_b=True)
        for i in range(1, len(sel)):
            out = out + _dot(x[i * rows:(i + 1) * rows].astype(BF16), sel[i], trans_b=True)
        return out

    def rms(x, g):
        r = lax.rsqrt(jnp.mean(x * x, axis=-1, keepdims=True) + EPS)
        return r, x * r * g

    def rms_bwd(dn, xf, r, g):
        w = dn * g
        dx = r * w - xf * (r * r * r) * jnp.mean(xf * w, axis=-1, keepdims=True)
        return dx, jnp.sum(dn * (xf * r), axis=0, keepdims=True)

    def body(q_ref, kp_ref, kc_ref, vp_ref, vc_ref, do_ref, gq_ref, gk_ref, sink_ref, bias_ref,
             dq_ref, dkv_ref, dbias_ref, dsink_ref, dgq_ref, dgk_ref):
        n = pl.program_id(0)
        sel = head_selectors()

        @pl.when(n == 0)
        def _():
            dbias_ref[...] = jnp.zeros_like(dbias_ref)
            dsink_ref[...] = jnp.zeros_like(dsink_ref)
            dgq_ref[...] = jnp.zeros_like(dgq_ref)
            dgk_ref[...] = jnp.zeros_like(dgk_ref)

        before_start = _before_start(n)
        dgq_sum = jnp.zeros((1, HEAD_DIM), F32)
        dgk_sum = jnp.zeros((1, HEAD_DIM), F32)
        dk_rows, dv_rows = [], []
        for h in range(N_KV_HEADS):
            qf = take_heads(q_ref[:, pl.ds(h * KV_W, KV_W)], sel)
            rq, qn = rms(qf, gq_ref[...])
            kf = jnp.concatenate([_dot(kp_ref[...], sel[h]), _dot(kc_ref[...], sel[h])], axis=0)
            rk, kn = rms(kf, gk_ref[...])
            pn, psink = _attn_probs(qn.astype(BF16), kn.astype(BF16), sink_ref[h], bias_ref[h], before_start)
            do = take_heads(do_ref[:, pl.ds(h * KV_W, KV_W)], sel).astype(BF16)
            v = jnp.concatenate([_dot(vp_ref[...], sel[h]), _dot(vc_ref[...], sel[h])], axis=0).astype(BF16)
            dv_win = _dot(do, pn.astype(BF16), trans_a=True).T
            dp = _dot(do, v, trans_b=True)
            delta = jnp.sum(pn * dp, axis=-1, keepdims=True)
            ds = pn * (dp - delta)
            dsc = (ds * scale).astype(BF16)
            dqn = _dot(dsc, kn.astype(BF16))
            dkn = _dot(qn.astype(BF16), dsc, trans_a=True).T
            dq, dgq = rms_bwd(dqn, qf, rq, gq_ref[...])
            dk_win, dgk = rms_bwd(dkn, kf, rk, gk_ref[...])
            dq_ref[:, pl.ds(h * KV_W, KV_W)] = put_heads(dq, sel).astype(BF16)
            dk_rows += [dk_win[:BLOCK], dk_win[BLOCK:]]
            dv_rows += [dv_win[:BLOCK], dv_win[BLOCK:]]
            dbias_ref[h] += ds
            dsink_ref[h] += jnp.sum((-psink * delta).reshape(GROUP, BLOCK, 1), axis=1)
            dgq_sum = dgq_sum + dgq
            dgk_sum = dgk_sum + dgk
        for part in range(2):
            dkv_ref[part, :, pl.ds(0, KV_W)] = put_heads(jnp.concatenate(dk_rows[part::2], axis=0), sel).astype(BF16)
            dkv_ref[part, :, pl.ds(KV_W, KV_W)] = put_heads(jnp.concatenate(dv_rows[part::2], axis=0), sel).astype(BF16)
        dgq_ref[...] += dgq_sum
        dgk_ref[...] += dgk_sum

    row = pl.BlockSpec((BLOCK, Q_W), lambda n: (n, 0))
    return pl.pallas_call(
        body, name="attn_bwd", grid=(nb,),
        in_specs=qkv + [row, gain, gain, sink, bspec],
        out_specs=[row, pl.BlockSpec((None, 2, BLOCK, 2 * KV_W), lambda n: (n, 0, 0, 0)), bspec,
                   pl.BlockSpec((N_KV_HEADS, GROUP, 1), lambda n: (0, 0, 0)), gain, gain],
        out_shape=[jax.ShapeDtypeStruct((t, Q_W), BF16),
                   jax.ShapeDtypeStruct((nb, 2, BLOCK, 2 * KV_W), BF16),
                   jax.ShapeDtypeStruct((N_KV_HEADS, QROWS, 2 * BLOCK), F32),
                   jax.ShapeDtypeStruct((N_KV_HEADS, GROUP, 1), F32),
                   jax.ShapeDtypeStruct((1, HEAD_DIM), F32),
                   jax.ShapeDtypeStruct((1, HEAD_DIM), F32)],
        compiler_params=_params(("arbitrary",)),
    )(p, p, p, p, p, do, gq, gk, sink_rows, bias)


def _kv_window_sum(parts):
    nb = parts.shape[0]

    def body(cur_ref, nxt_ref, o_ref):
        nxt = jnp.where(pl.program_id(0) < nb - 1, nxt_ref[...].astype(F32), 0.0)
        o_ref[...] = (cur_ref[...].astype(F32) + nxt).astype(BF16)

    blk = (None, None, BLOCK, 2 * KV_W)
    return pl.pallas_call(
        body, name="kv_window_sum", grid=(nb,),
        in_specs=[pl.BlockSpec(blk, lambda n: (n, 1, 0, 0)),
                  pl.BlockSpec(blk, lambda n: (jnp.minimum(n + 1, nb - 1), 0, 0, 0))],
        out_specs=pl.BlockSpec((BLOCK, 2 * KV_W), lambda n: (n, 0)),
        out_shape=jax.ShapeDtypeStruct((nb * BLOCK, 2 * KV_W), BF16),
        compiler_params=_params(("parallel",)),
    )(parts, parts)


GATE_TILE = 512


def _merge_fwd(z3, o, p, w_proj, w_o):
    t, d = z3.shape
    tm = min(ROW_TILE, t)
    tn = GATE_TILE

    def body(z_ref, o_ref, gc_ref, ga_ref, wp_ref, wo_ref, m_ref, a_ref, b_ref):
        a = _dot(z_ref[...], wp_ref[...])
        b = _dot(o_ref[...], wo_ref[...])
        m_ref[...] = (_sigmoid(gc_ref[...].astype(F32)) * a + _sigmoid(ga_ref[...].astype(F32)) * b).astype(BF16)
        a_ref[...] = a.astype(BF16)
        b_ref[...] = b.astype(BF16)

    row = pl.BlockSpec((tm, d), lambda i, j: (i, 0))
    wspec = pl.BlockSpec((d, tn), lambda i, j: (0, j))
    ospec = pl.BlockSpec((tm, tn), lambda i, j: (i, j))
    return pl.pallas_call(
        body, name="merge_fwd", grid=(t // tm, d // tn),
        in_specs=[row, row,
                  pl.BlockSpec((tm, tn), lambda i, j: (i, COL_GC // tn + j)),
                  pl.BlockSpec((tm, tn), lambda i, j: (i, COL_GA // tn + j)), wspec, wspec],
        out_specs=[ospec, ospec, ospec],
        out_shape=[jax.ShapeDtypeStruct((t, d), BF16)] * 3,
        compiler_params=_params(("parallel", "parallel")),
    )(z3, o, p, p, w_proj, w_o)


def _merge_bwd(dres, w_out, a, b, p, tokens=()):
    t, d = dres.shape
    tm = min(ROW_TILE, t)
    tn = GATE_TILE

    def epilogue(acc, ex, outs, ids):
        a_ref, b_ref, gc_ref, ga_ref = ex[:4]
        sc = _sigmoid(gc_ref[...].astype(F32))
        sa = _sigmoid(ga_ref[...].astype(F32))
        outs[0][...] = (acc * sc).astype(BF16)
        outs[1][...] = (acc * sa).astype(BF16)
        outs[2][0] = (acc * a_ref[...].astype(F32) * sc * (1.0 - sc)).astype(BF16)
        outs[2][1] = (acc * b_ref[...].astype(F32) * sa * (1.0 - sa)).astype(BF16)

    ospec = pl.BlockSpec((tm, tn), lambda i, j, kk: (i, j))
    return _mm("merge_bwd", (t // tm, d // tn, 1),
               dres, pl.BlockSpec((tm, d), lambda i, j, kk: (i, 0)),
               w_out, pl.BlockSpec((tn, d), lambda i, j, kk: (j, 0)), (tm, tn),
               trans_b=True, a_pre=_to_bf16,
               extras=(a, b, p, p),
               extra_specs=(ospec, ospec,
                            pl.BlockSpec((tm, tn), lambda i, j, kk: (i, COL_GC // tn + j)),
                            pl.BlockSpec((tm, tn), lambda i, j, kk: (i, COL_GA // tn + j))), tokens=tokens,
               out_shape=(jax.ShapeDtypeStruct((t, d), BF16), jax.ShapeDtypeStruct((t, d), BF16),
                          jax.ShapeDtypeStruct((2, t, d), BF16)),
               out_specs=(ospec, ospec, pl.BlockSpec((2, tm, tn), lambda i, j, kk: (0, i, j))),
               epilogue=epilogue)


def _store_epilogue(acc, ex, outs, ids):
    outs[0][...] = acc


def _store_bf16_epilogue(acc, ex, outs, ids):
    outs[0][...] = acc.astype(BF16)


def _mm_nt(name, a, w, out_dtype=BF16):
    t, n = a.shape
    k = w.shape[0]
    tm = min(ROW_TILE, t)
    return _mm(name, (t // tm, 1, 1), a, pl.BlockSpec((tm, n), lambda i, j, kk: (i, 0)),
               w, pl.BlockSpec((k, n), lambda i, j, kk: (0, 0)), (tm, k), trans_b=True,
               out_shape=(jax.ShapeDtypeStruct((t, k), out_dtype),),
               out_specs=(pl.BlockSpec((tm, k), lambda i, j, kk: (i, 0)),),
               epilogue=_store_bf16_epilogue if out_dtype == BF16 else _store_epilogue)[0]


def _mm_tn(name, a, b, b_pre=None):
    t, m = a.shape
    n = b.shape[1]
    tk = min(TOKEN_TILE, t)
    return _mm(name, (1, 1, t // tk), a, pl.BlockSpec((tk, m), lambda i, j, kk: (kk, 0)),
               b, pl.BlockSpec((tk, n), lambda i, j, kk: (kk, 0)), (m, n), trans_a=True, b_pre=b_pre,
               out_shape=(jax.ShapeDtypeStruct((m, n), BF16),),
               out_specs=(pl.BlockSpec((m, n), lambda i, j, kk: (0, 0)),), epilogue=_store_bf16_epilogue)[0]


def _local_step(x, target, small, comm):
    t = x.shape[0]
    w = dict(small)

    n1 = _rmsnorm_fwd("ffn1_norm", x, w["ffn1_norm"])
    w.update(comm.weights("A", n1))
    x1, ffn1_saved = _ffn_fwd("ffn1", x, n1, w["ffn1_w_in"], w["ffn1_w_out"], comm.tokens)
    w.update(comm.weights("B", x1))
    hm = _rmsnorm_fwd("mix_norm", x1, w["mix_norm"], comm.tokens)
    tm = min(ROW_TILE, t)
    p = _mm("mix_in", (N_CHIPS, t // tm, 1),
            hm, pl.BlockSpec((tm, D_MODEL), lambda j, i, kk: (i, 0)),
            w["w_in"], pl.BlockSpec((None, D_MODEL, SHARD_W), lambda j, i, kk: (j, 0, 0)), (tm, SHARD_W),
            out_shape=(jax.ShapeDtypeStruct((t, IN_W), BF16),),
            out_specs=(pl.BlockSpec((tm, SHARD_W), lambda j, i, kk: (i, j)),),
            epilogue=_store_bf16_epilogue)[0]

    z3, z1 = _conv_fwd(p, w["conv_dw_kernel"], w["conv_dw_bias"], w["conv_ln_g"], w["conv_ln_b"])

    onehot = _bucket_onehot()
    bias = _bias_table(w["rel_bias"].T, onehot).reshape(N_KV_HEADS, QROWS, 2 * BLOCK)
    sink_rows = jnp.repeat(w["attn_sinks"].reshape(N_KV_HEADS, GROUP), BLOCK, axis=1)[..., None]
    gq_wide = jnp.tile(w["q_norm"], (1, N_KV_HEADS))
    gk_wide = jnp.tile(w["k_norm"], (1, N_KV_HEADS))
    o = _attn_fwd(p, gq_wide, gk_wide, sink_rows, bias)

    merged, a, b = _merge_fwd(z3, o, p, w["conv_w_proj"], w["attn_w_o"])
    x2 = _mm_residual("mix_out", merged, w["w_out"], x1, 1.0)
    n2 = _rmsnorm_fwd("ffn2_norm", x2, w["ffn2_norm"])
    w.update(comm.weights("C", n2))
    x3, ffn2_saved = _ffn_fwd("ffn2", x2, n2, w["ffn2_w_in"], w["ffn2_w_out"])
    dy, loss = _loss_head(x3, target)

    g, big = {}, {}
    dres2, big["ffn2_w_in"], big["ffn2_w_out"], g["ffn2_norm"] = _ffn_bwd(
        "ffn2b", dy, x2, w["ffn2_norm"], ffn2_saved, w["ffn2_w_in"], w["ffn2_w_out"])
    tokens = comm.reduce_start("R1", big)

    da, db, dgates = _merge_bwd(dres2, w["w_out"], a, b, p, tokens)
    big = {}
    big["w_out"] = _mm_tn("d_w_out", merged, dres2, b_pre=_to_bf16)
    big["conv_w_proj"] = _mm_tn("d_w_proj", z3, da)
    big["attn_w_o"] = _mm_tn("d_w_o", o, db)
    dz3 = _mm_nt("d_z3", da, w["conv_w_proj"])
    do = _mm_nt("d_o", db, w["attn_w_o"])

    dq, dkv_parts, dbias, dsink, g["q_norm"], g["k_norm"] = _attn_bwd(
        p, do, w["q_norm"], w["k_norm"], sink_rows, bias)
    dkv = _kv_window_sum(dkv_parts)
    g["rel_bias"] = _bias_table_bwd(dbias.reshape(N_Q_HEADS, BLOCK * 2 * BLOCK), onehot).T
    g["attn_sinks"] = dsink.reshape(N_Q_HEADS)

    dz1, big["conv_dw_kernel"], g["conv_dw_bias"], g["conv_ln_g"], g["conv_ln_b"] = _conv_bwd_ln(
        p, z1, dz3, w["conv_ln_g"], w["conv_ln_b"])
    dp = _conv_bwd_glu(p, dz1, w["conv_dw_kernel"], dq, dkv, dgates)
    tk = min(TOKEN_TILE, t)
    big["w_in"] = _mm("d_w_in", (1, N_CHIPS, t // tk),
                    hm, pl.BlockSpec((tk, D_MODEL), lambda i, j, kk: (kk, 0)),
                    dp, pl.BlockSpec((tk, SHARD_W), lambda i, j, kk: (kk, j)), (D_MODEL, SHARD_W),
                    trans_a=True,
                    out_shape=(jax.ShapeDtypeStruct((N_CHIPS, D_MODEL, SHARD_W), BF16),),
                    out_specs=(pl.BlockSpec((None, D_MODEL, SHARD_W), lambda i, j, kk: (j, 0, 0)),),
                    epilogue=_store_bf16_epilogue)[0]
    dres1, g["mix_norm"] = _mm("d_mix", (t // tm, 1, N_CHIPS),
                               dp, pl.BlockSpec((tm, SHARD_W), lambda i, j, kk: (i, kk)),
                               w["w_in"], pl.BlockSpec((None, D_MODEL, SHARD_W), lambda i, j, kk: (kk, 0, 0)),
                               (tm, D_MODEL), trans_b=True,
                               extras=(x1, w["mix_norm"], dres2),
                               extra_specs=(pl.BlockSpec((tm, D_MODEL), lambda i, j, kk: (i, 0)),
                                            pl.BlockSpec((1, D_MODEL), lambda i, j, kk: (0, 0)),
                                            pl.BlockSpec((tm, D_MODEL), lambda i, j, kk: (i, 0))),
                               out_shape=(jax.ShapeDtypeStruct((t, D_MODEL), F32), jax.ShapeDtypeStruct((1, D_MODEL), F32)),
                               out_specs=(pl.BlockSpec((tm, D_MODEL), lambda i, j, kk: (i, 0)),
                                          pl.BlockSpec((1, D_MODEL), lambda i, j, kk: (0, 0))),
                               epilogue=_rms_bwd_epilogue, sem=("arbitrary", "arbitrary", "arbitrary"))

    comm.reduce_finish("R1", dres1)
    tokens = comm.reduce_start("R2", big)

    def ffn1_grads(dw_in4, dw_out):
        comm.reduce_finish("R2", dw_in4)
        return comm.reduce_start("R3", {"ffn1_w_in": dw_in4, "ffn1_w_out": dw_out})

    grad_x, _, _, g["ffn1_norm"] = _ffn_bwd(
        "ffn1b", dres1, x, w["ffn1_norm"], ffn1_saved, w["ffn1_w_in"], w["ffn1_w_out"], tokens, ffn1_grads)
    comm.reduce_finish("R3", grad_x)
    return loss[0, 0], grad_x, g


def _mesh_place():
    x, y, c = lax.axis_index("x"), lax.axis_index("y"), lax.axis_index("c")
    chips = [(1 - x, y), (x, 1 - y), (1 - x, 1 - y)]
    return x, y, c, chips


def _any_specs(n):
    return [pl.BlockSpec(memory_space=pl.ANY)] * n


HBM_SPEC = pl.BlockSpec(memory_space=pltpu.HBM)
SEM_SPEC = pl.BlockSpec(memory_space=pltpu.SEMAPHORE)
EFFECT = pltpu.SideEffectType.DATAFLOW_SIDE_EFFECTING


def _in_hbm(a):
    return pltpu.with_memory_space_constraint(a, pltpu.HBM)


def _copy_start(name, srcs, lands, plan, after=()):
    ns, nb = len(srcs), len(lands)
    n = 3 * ns

    def body(*refs):
        s_refs, l_refs = refs[:ns], refs[ns:ns + nb]
        send_sems, recv_sems = refs[ns + nb + len(after)], refs[ns + nb + len(after) + 1]
        token = refs[-1]
        for k, (src, dst, to, _) in enumerate(plan(s_refs, l_refs)):
            pltpu.make_async_remote_copy(src_ref=src, dst_ref=dst, send_sem=send_sems.at[k], recv_sem=recv_sems.at[k],
                                         device_id=to, device_id_type=MESH).start()
        token[...] = jnp.zeros_like(token)

    bufs = list(srcs) + list(lands)
    outs = pl.pallas_call(
        body, name=name,
        out_shape=(pltpu.SemaphoreType.DMA((n,)), pltpu.SemaphoreType.DMA((n,)),
                   *[pltpu.HBM(a.shape, a.dtype) for a in bufs], jax.ShapeDtypeStruct((8, LANES), F32)),
        in_specs=[HBM_SPEC] * len(bufs) + [pl.BlockSpec(memory_space=pl.ANY)] * len(after),
        out_specs=(SEM_SPEC, SEM_SPEC, *[HBM_SPEC] * len(bufs), pl.BlockSpec(memory_space=pltpu.VMEM)),
        input_output_aliases={i: 2 + i for i in range(len(bufs))},
        compiler_params=pltpu.CompilerParams(has_side_effects=EFFECT),
    )(*[_in_hbm(a) for a in bufs], *after)
    return outs[0], outs[1], list(outs[2:2 + ns]), list(outs[2 + ns:2 + ns + nb]), outs[-1]


def _copy_wait(name, send_sems, recv_sems, srcs, lands, after, plan):
    ns, nb = len(srcs), len(lands)

    def body(*refs):
        s_refs, l_refs = refs[:ns], refs[ns:ns + nb]
        send_sems, recv_sems = refs[ns + nb], refs[ns + nb + 1]
        for k, (src, _, to, mine) in enumerate(plan(s_refs, l_refs)):
            cp = pltpu.make_async_remote_copy(src_ref=src, dst_ref=mine, send_sem=send_sems.at[k], recv_sem=recv_sems.at[k],
                                              device_id=to, device_id_type=MESH)
            cp.wait_send()
            cp.wait_recv()

    bufs = list(srcs) + list(lands)
    outs = pl.pallas_call(
        body, name=name,
        out_shape=tuple(pltpu.HBM(a.shape, a.dtype) for a in bufs),
        in_specs=[HBM_SPEC] * len(bufs) + [SEM_SPEC, SEM_SPEC, pl.BlockSpec(memory_space=pl.ANY)],
        out_specs=tuple([HBM_SPEC] * len(bufs)),
        input_output_aliases={i: i for i in range(len(bufs))},
        compiler_params=pltpu.CompilerParams(has_side_effects=EFFECT),
    )(*bufs, send_sems, recv_sems, after)
    return list(outs[:ns]), list(outs[ns:])


def _gather_plan(s_refs, l_refs):
    x, y, c, chips = _mesh_place()
    jme = 2 * x + y
    return [(s.at[c], land.at[jme, c], (*chip, c), land.at[2 * chip[0] + chip[1], c])
            for s, land in zip(s_refs, l_refs) for chip in chips]


def _scatter_plan(s_refs, l_refs):
    x, y, c, chips = _mesh_place()
    return [(s.at[2 * chip[0] + chip[1]], land.at[k], (*chip, c), land.at[k])
            for s, land in zip(s_refs, l_refs) for k, chip in enumerate(chips)]


def _gather_forward(name, shards, landed):
    nw = len(shards)

    def body(*refs):
        s_refs, o_refs = refs[:nw], refs[2 * nw:3 * nw]
        send_sems, recv_sems = refs[3 * nw:]
        x, y, c, chips = _mesh_place()
        me, sib, jme = (x, y, c), (x, y, 1 - c), 2 * x + y
        sent = []
        for w in range(nw):
            parts = [(o_refs[w].at[2 * chip[0] + chip[1], c], o_refs[w].at[2 * chip[0] + chip[1], c]) for chip in chips]
            parts.append((s_refs[w], o_refs[w].at[jme]))
            for k, (src, dst) in enumerate(parts):
                cp = pltpu.make_async_remote_copy(src_ref=src, dst_ref=dst, send_sem=send_sems.at[4 * w + k],
                                                  recv_sem=recv_sems.at[4 * w + k], device_id=sib, device_id_type=MESH)
                cp.start()
                sent.append(cp)
        for w in range(nw):
            parts = [o_refs[w].at[2 * chip[0] + chip[1], 1 - c] for chip in chips] + [o_refs[w].at[jme]]
            for k, part in enumerate(parts):
                pltpu.make_async_remote_copy(src_ref=part, dst_ref=part, send_sem=send_sems.at[4 * w + k],
                                             recv_sem=recv_sems.at[4 * w + k], device_id=me, device_id_type=MESH).wait_recv()
        for cp in sent:
            cp.wait_send()

    return pl.pallas_call(
        body, name=name,
        in_specs=_any_specs(2 * nw), out_specs=_any_specs(nw),
        out_shape=[jax.ShapeDtypeStruct(a.shape, a.dtype) for a in landed],
        input_output_aliases={nw + i: i for i in range(nw)},
        scratch_shapes=[pltpu.SemaphoreType.DMA((4 * nw,)), pltpu.SemaphoreType.DMA((4 * nw,))],
    )(*shards, *landed)


def _exchange_halves(name, grads, after=()):
    nw = len(grads)

    def body(*refs):
        g_refs, o_refs = refs[:nw], refs[nw + len(after):2 * nw + len(after)]
        send_sems, recv_sems = refs[2 * nw + len(after):]
        x, y, c, _ = _mesh_place()
        copies = []
        for w in range(nw):
            cp = pltpu.make_async_remote_copy(src_ref=g_refs[w].at[:, 1 - c], dst_ref=o_refs[w], send_sem=send_sems.at[w],
                                              recv_sem=recv_sems.at[w], device_id=(x, y, 1 - c), device_id_type=MESH)
            cp.start()
            copies.append(cp)
        for cp in copies:
            cp.wait()

    return pl.pallas_call(
        body, name=name,
        in_specs=_any_specs(nw + len(after)), out_specs=_any_specs(nw),
        out_shape=[jax.ShapeDtypeStruct((N_CHIPS,) + g.shape[2:], g.dtype) for g in grads],
        scratch_shapes=[pltpu.SemaphoreType.DMA((nw,)), pltpu.SemaphoreType.DMA((nw,))],
    )(*grads, *after)


def _row_tile(r):
    for cand in (256, 176, 128, 64, 32, 16, 8):
        if r % cand == 0:
            return cand
    return r


def _add_own_half(c_idx, grad, got):
    _, _, r, cols = grad.shape
    tr = _row_tile(r)

    def body(c_ref, g_ref, o_ref, out_ref):
        out_ref[...] = (g_ref[...].astype(F32) + o_ref[...].astype(F32)).astype(BF16)

    return pl.pallas_call(
        body, name="add_own_half",
        grid_spec=pltpu.PrefetchScalarGridSpec(
            num_scalar_prefetch=1, grid=(N_CHIPS, r // tr),
            in_specs=[pl.BlockSpec((None, None, tr, cols), lambda j, i, c_ref: (j, c_ref[0], i, 0)),
                      pl.BlockSpec((None, tr, cols), lambda j, i, c_ref: (j, i, 0))],
            out_specs=pl.BlockSpec((None, tr, cols), lambda j, i, c_ref: (j, i, 0))),
        out_shape=jax.ShapeDtypeStruct((N_CHIPS, r, cols), BF16),
        compiler_params=_params(("parallel", "parallel")),
    )(c_idx, grad, got)


def _sum_pieces(place_idx, sums, landed):
    _, r, cols = sums.shape
    tr = _row_tile(r)

    def body(j_ref, own_ref, p_ref, o_ref):
        o_ref[...] = ((own_ref[...].astype(F32) + p_ref[0].astype(F32)) + p_ref[1].astype(F32)) + p_ref[2].astype(F32)

    return pl.pallas_call(
        body, name="sum_pieces",
        grid_spec=pltpu.PrefetchScalarGridSpec(
            num_scalar_prefetch=1, grid=(r // tr,),
            in_specs=[pl.BlockSpec((None, tr, cols), lambda i, j_ref: (j_ref[0], i, 0)),
                      pl.BlockSpec((N_CHIPS - 1, tr, cols), lambda i, j_ref: (0, i, 0))],
            out_specs=pl.BlockSpec((None, tr, cols), lambda i, j_ref: (j_ref[1], i, 0))),
        out_shape=jax.ShapeDtypeStruct((2, r, cols), F32),
        compiler_params=_params(("parallel",)),
    )(place_idx, sums, landed)


def _join_halves(name, halves):
    nw = len(halves)

    def body(*refs):
        o_refs = refs[nw:2 * nw]
        send_sems, recv_sems = refs[2 * nw:]
        x, y, c, _ = _mesh_place()
        copies = []
        for w in range(nw):
            cp = pltpu.make_async_remote_copy(src_ref=o_refs[w].at[c], dst_ref=o_refs[w].at[c], send_sem=send_sems.at[w],
                                              recv_sem=recv_sems.at[w], device_id=(x, y, 1 - c), device_id_type=MESH)
            cp.start()
            copies.append(cp)
        for w in range(nw):
            copies[w].wait_send()
            landed = o_refs[w].at[1 - c]
            pltpu.make_async_remote_copy(src_ref=landed, dst_ref=landed, send_sem=send_sems.at[w], recv_sem=recv_sems.at[w],
                                         device_id=(x, y, c), device_id_type=MESH).wait_recv()

    return pl.pallas_call(
        body, name=name,
        in_specs=_any_specs(nw), out_specs=_any_specs(nw),
        out_shape=[jax.ShapeDtypeStruct(h.shape, F32) for h in halves],
        input_output_aliases={i: i for i in range(nw)},
        scratch_shapes=[pltpu.SemaphoreType.DMA((nw,)), pltpu.SemaphoreType.DMA((nw,))],
    )(*halves)


SMALL_ROWS = 8


def _all_reduce_small(pack):
    rows, cols = pack.shape
    n_dev = 8

    def body(p_ref, o_ref, slots, send_sems, recv_sems):
        x, y, c, _ = _mesh_place()
        me = 4 * x + 2 * y + c
        slots[me] = p_ref[...]
        copies = []
        for k in range(1, n_dev):
            peer = (me + k) % n_dev
            cp = pltpu.make_async_remote_copy(src_ref=p_ref, dst_ref=slots.at[me], send_sem=send_sems.at[k],
                                              recv_sem=recv_sems.at[k],
                                              device_id=(peer // 4, (peer // 2) % 2, peer % 2), device_id_type=MESH)
            cp.start()
            copies.append(cp)
        for k in range(1, n_dev):
            src = (me + n_dev - k) % n_dev
            pltpu.make_async_remote_copy(src_ref=p_ref, dst_ref=slots.at[src], send_sem=send_sems.at[k],
                                         recv_sem=recv_sems.at[k], device_id=(x, y, c), device_id_type=MESH).wait_recv()
        for cp in copies:
            cp.wait_send()
        total = slots[0]
        for s in range(1, n_dev):
            total = total + slots[s]
        o_ref[...] = total

    return pl.pallas_call(
        body, name="all_reduce_small",
        in_specs=[pl.BlockSpec(memory_space=pltpu.VMEM)], out_specs=pl.BlockSpec(memory_space=pltpu.VMEM),
        out_shape=jax.ShapeDtypeStruct((rows, cols), F32),
        scratch_shapes=[pltpu.VMEM((n_dev, rows, cols), F32), pltpu.SemaphoreType.DMA((n_dev,)),
                        pltpu.SemaphoreType.DMA((n_dev,))],
    )(pack)


def _adamw(name, w, g, m, v):
    r, cols = w.shape
    tr = _row_tile(r)

    def body(w_ref, g_ref, m_ref, v_ref, d_ref, nm_ref, nv_ref):
        gv = g_ref[...]
        nm = ADAM_B1 * m_ref[...] + (1.0 - ADAM_B1) * gv
        nv = ADAM_B2 * v_ref[...] + (1.0 - ADAM_B2) * (gv * gv)
        m_hat = nm / (1.0 - ADAM_B1 ** ADAM_STEP)
        v_hat = nv / (1.0 - ADAM_B2 ** ADAM_STEP)
        d_ref[...] = -ADAM_LR * (m_hat / (jnp.sqrt(v_hat) + ADAM_EPS) + ADAM_WD * w_ref[...])
        nm_ref[...] = nm
        nv_ref[...] = nv

    spec = pl.BlockSpec((tr, cols), lambda i: (i, 0))
    return pl.pallas_call(
        body, name=name, grid=(r // tr,),
        in_specs=[spec] * 4, out_specs=[spec] * 3,
        out_shape=[jax.ShapeDtypeStruct((r, cols), F32)] * 3,
        compiler_params=_params(("parallel",)),
    )(w, g, m, v)


BIG = ["ffn1_w_in", "ffn1_w_out", "w_in", "conv_w_proj", "attn_w_o", "w_out", "ffn2_w_in", "ffn2_w_out", "conv_dw_kernel"]
COL_SHARDED = ("ffn1_w_in", "w_in", "ffn2_w_in")
SMALL = ["ffn1_norm", "mix_norm", "ffn2_norm", "conv_dw_bias", "conv_ln_g", "conv_ln_b", "q_norm", "k_norm", "attn_sinks", "rel_bias"]
WEIGHTS = ["ffn1_norm", "ffn1_w_in", "ffn1_w_out", "mix_norm", "w_in", "conv_dw_kernel", "conv_dw_bias", "conv_ln_g",
           "conv_ln_b", "conv_w_proj", "q_norm", "k_norm", "attn_sinks", "rel_bias", "attn_w_o", "w_out", "ffn2_norm",
           "ffn2_w_in", "ffn2_w_out"]
SMALL_PLACE = {"ffn1_norm": (0, 0, 1024), "mix_norm": (1, 0, 1024), "ffn2_norm": (2, 0, 1024), "conv_dw_bias": (3, 0, 1024),
               "conv_ln_g": (4, 0, 1024), "conv_ln_b": (5, 0, 1024), "q_norm": (6, 0, 64), "k_norm": (6, 128, 64),
               "attn_sinks": (6, 256, 16), "rel_bias": (7, 0, 512)}
LOSS_PLACE = (6, 384)


def _pack_small(vals, fill=0.0, loss=None):
    pack = jnp.full((SMALL_ROWS, D_MODEL), fill, F32)
    for name, (row, lane, n) in SMALL_PLACE.items():
        pack = pack.at[row, lane:lane + n].set(vals[name].reshape(n))
    if loss is not None:
        pack = pack.at[LOSS_PLACE[0], LOSS_PLACE[1]].set(loss)
    return pack


def _unpack_small(pack, shapes):
    return {name: pack[row, lane:lane + n].reshape(shapes[name]) for name, (row, lane, n) in SMALL_PLACE.items()}


def _shard_halves(name, a):
    if name == "conv_dw_kernel":
        a = jnp.pad(a, ((0, CONV_PAD - CONV_WIDTH), (0, 0)))
    r, cols = a.shape
    return a.reshape(2, r // 2, cols)


GATHER_GROUPS = {"A": ["ffn1_w_in", "ffn1_w_out"],
                 "B": ["w_in", "conv_dw_kernel", "conv_w_proj", "attn_w_o", "w_out"],
                 "C": ["ffn2_w_in", "ffn2_w_out"]}


class _MeshComm:
    def __init__(self, wts):
        self.c_idx = lax.axis_index("c").astype(jnp.int32).reshape(1)
        self.place_idx = jnp.stack([2 * lax.axis_index("x") + lax.axis_index("y"), lax.axis_index("c")]).astype(jnp.int32)
        self.wts, self.gathers, self.reductions, self.reduced = wts, {}, {}, {}
        self.tokens, self.last_join = (), ()
        self._gather_start("A", ())

    def _gather_start(self, group, after):
        names = GATHER_GROUPS[group]
        shards = [_shard_halves(n, self.wts[n]) if n == "conv_dw_kernel" else _shard_halves(n, self.wts[n]).astype(BF16)
                  for n in names]
        lands = [lax.empty((N_CHIPS,) + s.shape, s.dtype) for s in shards]
        self.gathers[group] = _copy_start("gather_start_" + group, shards, lands, _gather_plan, after=after)
        self.tokens = (self.gathers[group][-1],)

    def weights(self, group, after):
        send_sems, recv_sems, shards, lands, token = self.gathers.pop(group)
        shards, lands = _copy_wait("gather_wait_" + group, send_sems, recv_sems, shards, lands,
                                   token if after is None else after, _gather_plan)
        gathered = _gather_forward("gather_forward_" + group, shards, lands)
        self.tokens = ()
        following = {"A": "B", "B": "C"}.get(group)
        if following:
            self._gather_start(following, (gathered[0],))
        out = {}
        for n, g4 in zip(GATHER_GROUPS[group], gathered):
            r, cols = g4.shape[2] * 2, g4.shape[3]
            if n in COL_SHARDED:
                out[n] = g4.reshape(N_CHIPS, r, cols)
            elif n == "conv_dw_kernel":
                out[n] = g4.reshape(N_CHIPS, r, cols).transpose(1, 0, 2).reshape(r, N_CHIPS * cols)
            else:
                out[n] = g4.reshape(N_CHIPS * r, cols)
        return out

    def reduce_start(self, group, grads):
        names = list(grads)
        g4 = []
        for n in names:
            a = grads[n]
            if n == "conv_dw_kernel":
                a = a.reshape(CONV_PAD, N_CHIPS, -1).transpose(1, 0, 2)
            elif n not in COL_SHARDED:
                a = a.reshape(N_CHIPS, a.shape[0] // N_CHIPS, a.shape[1])
            g4.append(a.reshape(N_CHIPS, 2, a.shape[1] // 2, a.shape[2]))
        got = _exchange_halves("exchange_halves_" + group, g4, after=self.last_join)
        sums = [_add_own_half(self.c_idx, a, b) for a, b in zip(g4, got)]
        lands = [lax.empty((N_CHIPS - 1,) + s.shape[1:], s.dtype) for s in sums]
        started = _copy_start("scatter_start_" + group, sums, lands, _scatter_plan, after=self.last_join)
        self.reductions[group] = (names,) + started
        return (started[-1],)

    def reduce_finish(self, group, after):
        names, send_sems, recv_sems, sums, lands, _ = self.reductions.pop(group)
        sums, lands = _copy_wait("scatter_wait_" + group, send_sems, recv_sems, sums, lands, after, _scatter_plan)
        halves = [_sum_pieces(self.place_idx, s, p) for s, p in zip(sums, lands)]
        joined = _join_halves("join_halves_" + group, halves)
        self.last_join = (joined[0],)
        self.reduced.update(zip(names, joined))


def kernel(x, ffn1_norm, ffn1_w_in, ffn1_w_out, mix_norm, w_in, conv_dw_kernel, conv_dw_bias, conv_ln_g, conv_ln_b, conv_w_proj, q_norm, k_norm, attn_sinks, rel_bias, attn_w_o, w_out, ffn2_norm, ffn2_w_in, ffn2_w_out, loss_target, m_ffn1_norm, m_ffn1_w_in, m_ffn1_w_out, m_mix_norm, m_w_in, m_conv_dw_kernel, m_conv_dw_bias, m_conv_ln_g, m_conv_ln_b, m_conv_w_proj, m_q_norm, m_k_norm, m_attn_sinks, m_rel_bias, m_attn_w_o, m_w_out, m_ffn2_norm, m_ffn2_w_in, m_ffn2_w_out, v_ffn1_norm, v_ffn1_w_in, v_ffn1_w_out, v_mix_norm, v_w_in, v_conv_dw_kernel, v_conv_dw_bias, v_conv_ln_g, v_conv_ln_b, v_conv_w_proj, v_q_norm, v_k_norm, v_attn_sinks, v_rel_bias, v_attn_w_o, v_w_out, v_ffn2_norm, v_ffn2_w_in, v_ffn2_w_out):
    args = dict(locals())
    wts = {n: args[n] for n in WEIGHTS}
    mom = {n: args["m_" + n] for n in WEIGHTS}
    var = {n: args["v_" + n] for n in WEIGHTS}
    comm = _MeshComm(wts)
    small = {n: wts[n] if n in ("attn_sinks", "rel_bias") else wts[n].reshape(1, -1) for n in SMALL}
    loss_part, grad_x, g = _local_step(x[0], loss_target[0], small, comm)

    small_sum = _all_reduce_small(_pack_small(g, loss=loss_part))
    loss = small_sum[LOSS_PLACE[0], LOSS_PLACE[1]]
    small_shapes = {n: wts[n].shape for n in SMALL}
    g_small = _unpack_small(small_sum, small_shapes)

    grads, delta, new_m, new_v = {}, {}, {}, {}
    for n in BIG:
        j = comm.reduced[n]
        gs = j.reshape(j.shape[1] * 2, j.shape[2])
        pad = n == "conv_dw_kernel"
        ws, ms, vs = (_shard_halves(n, a).reshape(gs.shape) for a in (wts[n], mom[n], var[n]))
        d, nm, nv = _adamw("adamw_" + n, ws, gs, ms, vs)
        cut = (lambda a: a[:CONV_WIDTH]) if pad else (lambda a: a)
        grads[n], delta[n], new_m[n], new_v[n] = cut(gs), cut(d), cut(nm), cut(nv)
    d, nm, nv = _adamw("adamw_small", _pack_small(wts), small_sum, _pack_small(mom), _pack_small(var, fill=1.0))
    grads.update(g_small)
    delta.update(_unpack_small(d, small_shapes))
    new_m.update(_unpack_small(nm, small_shapes))
    new_v.update(_unpack_small(nv, small_shapes))

    return (loss, grad_x[None], *[grads[n] for n in WEIGHTS], *[delta[n] for n in WEIGHTS],
            *[new_m[n] for n in WEIGHTS], *[new_v[n] for n in WEIGHTS])
```

```python
import functools
import math

import jax
import jax.numpy as jnp
from jax import lax
from jax.experimental import pallas as pl
from jax.experimental.pallas import tpu as pltpu

F32 = jnp.float32
BF16 = jnp.bfloat16
MESH = pl.DeviceIdType.MESH

EPS = 1e-6
D_MODEL = 1024
D_FF = 2816
N_CHIPS = 4
SHARD_W = 2 * D_FF // N_CHIPS
HEAD_DIM = 64
N_Q_HEADS = 16
N_KV_HEADS = 4
GROUP = N_Q_HEADS // N_KV_HEADS
BLOCK = 128
QROWS = GROUP * BLOCK
N_BUCKETS = 32
MAX_DISTANCE = 128
CONV_WIDTH = 31
CONV_PAD = 32
NEG = float(jnp.finfo(jnp.float32).min)

ADAM_LR = 0.001
ADAM_B1 = 0.9
ADAM_B2 = 0.999
ADAM_EPS = 1e-08
ADAM_WD = 0.01
ADAM_STEP = 10

VMEM_LIMIT_BYTES = 56 * 1024 * 1024
ROW_TILE = 1024
TOKEN_TILE = 1024
CONV_TILE = 256
CONV_ROWS = 128
LANES = 128

COL_CONV_A, COL_CONV_G, COL_Q, COL_K, COL_V, COL_GC, COL_GA = 0, 1024, 2048, 3072, 3328, 3584, 4608
IN_W = 5632


def _params(sem, vmem=VMEM_LIMIT_BYTES):
    return pltpu.CompilerParams(dimension_semantics=sem, vmem_limit_bytes=vmem)


def _sigmoid(x):
    return 1.0 / (1.0 + jnp.exp(-x))


def _dot(a, b, trans_a=False, trans_b=False, precision=None):
    dn = (((0,) if trans_a else (1,), (1,) if trans_b else (0,)), ((), ()))
    return lax.dot_general(a, b, dn, preferred_element_type=F32, precision=precision)


def _mm(name, grid, a, a_spec, b, b_spec, acc_shape, *, trans_a=False, trans_b=False, a_pre=None, b_pre=None,
        extras=(), extra_specs=(), tokens=(), out_shape, out_specs, epilogue, chunked=False,
        sem=("parallel", "parallel", "arbitrary")):
    n_k = grid[2]
    assert not chunked or (n_k == 1 and b_pre is None)
    extras = tuple(extras) + tuple(tokens)
    extra_specs = tuple(extra_specs) + (pl.BlockSpec((8, LANES), lambda i, j, kk: (0, 0)),) * len(tokens)
    n_extra = len(extras)
    n_out = len(out_shape)

    def body(a_ref, b_ref, *rest):
        ex = rest[:n_extra]
        outs = rest[n_extra:n_extra + n_out]
        ids = (pl.program_id(0), pl.program_id(1), pl.program_id(2))
        av = a_ref[...]
        if a_pre is not None:
            av = a_pre(av)
        if chunked:
            for c0, cw in _col_chunks(acc_shape[1]):
                cols = pl.ds(c0, cw)
                epilogue(_dot(av, b_ref[cols, :] if trans_b else b_ref[:, cols], trans_a, trans_b), ex, outs, ids, cols)
            return
        bv = b_ref[...]
        if b_pre is not None:
            bv = b_pre(bv)
        if n_k == 1:
            epilogue(_dot(av, bv, trans_a, trans_b), ex, outs, ids)
        else:
            acc = rest[-1]

            @pl.when(ids[2] == 0)
            def _():
                acc[...] = jnp.zeros_like(acc)

            acc[...] += _dot(av, bv, trans_a, trans_b)

            @pl.when(ids[2] == n_k - 1)
            def _():
                epilogue(acc[...], ex, outs, ids)

    scratch = [] if n_k == 1 else [pltpu.VMEM(acc_shape, F32)]
    return pl.pallas_call(
        body, name=name, grid=grid,
        in_specs=[a_spec, b_spec, *extra_specs],
        out_specs=list(out_specs), out_shape=list(out_shape),
        scratch_shapes=scratch, compiler_params=_params(sem),
    )(a, b, *extras)


MXU_WIDTH = 256


def _col_chunks(n, width=2 * MXU_WIDTH):
    return [(c0, min(width, n - c0)) for c0 in range(0, n, width)]


def _half_bf16(v):
    return (0.5 * v).astype(BF16)


def _to_bf16(v):
    return v.astype(BF16)


def _rmsnorm_fwd(name, x, g, tokens=()):
    t, d = x.shape
    tm = min(ROW_TILE, t)

    def body(x_ref, g_ref, *rest):
        o_ref = rest[-1]
        xv = x_ref[...]
        r = lax.rsqrt(jnp.mean(xv * xv, axis=-1, keepdims=True) + EPS)
        o_ref[...] = (xv * r * g_ref[...]).astype(BF16)

    return pl.pallas_call(
        body, name=name, grid=(t // tm,),
        in_specs=[pl.BlockSpec((tm, d), lambda i: (i, 0)), pl.BlockSpec((1, d), lambda i: (0, 0))]
        + [pl.BlockSpec((8, LANES), lambda i: (0, 0))] * len(tokens),
        out_specs=pl.BlockSpec((tm, d), lambda i: (i, 0)),
        out_shape=jax.ShapeDtypeStruct((t, d), BF16),
        compiler_params=_params(("parallel",)),
    )(x, g, *tokens)


def _rms_bwd_epilogue(acc, ex, outs, ids):
    x_ref, g_ref, dres_ref = ex[:3]
    out_ref, dg_ref = outs
    xv = x_ref[...]
    r = lax.rsqrt(jnp.mean(xv * xv, axis=-1, keepdims=True) + EPS)
    w = acc * g_ref[...]
    dx = r * w - xv * (r * r * r) * jnp.mean(xv * w, axis=-1, keepdims=True)
    out_ref[...] = dres_ref[...] + dx
    part = jnp.sum(acc * (xv * r), axis=0, keepdims=True)

    @pl.when(ids[0] == 0)
    def _():
        dg_ref[...] = part

    @pl.when(ids[0] > 0)
    def _():
        dg_ref[...] += part


def _ffn_in(name, n, w_in4, tokens=()):
    t, d = n.shape
    tm = min(ROW_TILE, t)

    def body(n_ref, wa_ref, wb_ref, *rest):
        ab_ref, h_ref = rest[-2:]
        nv = n_ref[...]
        for c0, cw in _col_chunks(SHARD_W):
            cols = pl.ds(c0, cw)
            a = _dot(nv, wa_ref[:, cols])
            b = _dot(nv, wb_ref[:, cols])
            h_ref[:, cols] = (a * _sigmoid(a) * b).astype(BF16)
            ab_ref[0, :, cols] = a.astype(BF16)
            ab_ref[1, :, cols] = b.astype(BF16)

    return pl.pallas_call(
        body, name=name, grid=(2, t // tm),
        in_specs=[pl.BlockSpec((tm, d), lambda j, i: (i, 0)),
                  pl.BlockSpec((None, d, SHARD_W), lambda j, i: (j, 0, 0)),
                  pl.BlockSpec((None, d, SHARD_W), lambda j, i: (j + 2, 0, 0))]
        + [pl.BlockSpec((8, LANES), lambda j, i: (0, 0))] * len(tokens),
        out_specs=[pl.BlockSpec((2, tm, SHARD_W), lambda j, i: (0, i, j)),
                   pl.BlockSpec((tm, SHARD_W), lambda j, i: (i, j))],
        out_shape=[jax.ShapeDtypeStruct((2, t, D_FF), BF16), jax.ShapeDtypeStruct((t, D_FF), BF16)],
        compiler_params=_params(("parallel", "parallel")),
    )(n, w_in4, w_in4, *tokens)


def _mm_residual(name, a, w, res, scale):
    t, k = a.shape
    n = w.shape[1]
    tm = min(ROW_TILE, t)

    def epilogue(acc, ex, outs, ids):
        outs[0][...] = ex[0][...] + scale * acc

    return _mm(name, (t // tm, 1, 1), a, pl.BlockSpec((tm, k), lambda i, j, kk: (i, 0)),
               w, pl.BlockSpec((k, n), lambda i, j, kk: (0, 0)), (tm, n),
               extras=(res,), extra_specs=(pl.BlockSpec((tm, n), lambda i, j, kk: (i, 0)),),
               out_shape=(jax.ShapeDtypeStruct((t, n), F32),),
               out_specs=(pl.BlockSpec((tm, n), lambda i, j, kk: (i, 0)),), epilogue=epilogue)[0]


def _ffn_fwd(tag, x, n, w_in4, w_out, tokens=()):
    ab, h = _ffn_in(tag + "_in", n, w_in4, tokens)
    y = _mm_residual(tag + "_out", h, w_out, x, 0.5)
    return y, (n, ab, h)


def _ffn_bwd(tag, dres, x, g, saved, w_in4, w_out, tokens=(), on_weight_grads=None):
    n, ab, h = saved
    t, d = x.shape
    tm = min(ROW_TILE, t)
    tk = min(TOKEN_TILE, t)
    half_w = SHARD_W

    def dact_epilogue(acc, ex, outs, ids, cols):
        a = ex[0][0, :, cols].astype(F32)
        b = ex[0][1, :, cols].astype(F32)
        sig = _sigmoid(a)
        outs[0][0, :, cols] = (acc * b * (sig * (1.0 + a * (1.0 - sig)))).astype(BF16)
        outs[0][1, :, cols] = (acc * (a * sig)).astype(BF16)

    du = _mm(tag + "_dact", (2, t // tm, 1),
             dres, pl.BlockSpec((tm, d), lambda j, i, kk: (i, 0)),
             w_out, pl.BlockSpec((half_w, d), lambda j, i, kk: (j, 0)), (tm, half_w),
             trans_b=True, a_pre=_half_bf16,
             extras=(ab,), extra_specs=(pl.BlockSpec((2, tm, half_w), lambda j, i, kk: (0, i, j)),), tokens=tokens,
             out_shape=(jax.ShapeDtypeStruct((2, t, D_FF), BF16),),
             out_specs=(pl.BlockSpec((2, tm, half_w), lambda j, i, kk: (0, i, j)),),
             epilogue=dact_epilogue, chunked=True)[0]

    def store_epilogue(acc, ex, outs, ids):
        outs[0][...] = acc.astype(BF16)

    dw_out = _mm(tag + "_dwout", (2, 1, t // tk),
                 h, pl.BlockSpec((tk, half_w), lambda i, j, kk: (kk, i)),
                 dres, pl.BlockSpec((tk, d), lambda i, j, kk: (kk, 0)), (half_w, d),
                 trans_a=True, b_pre=_half_bf16,
                 out_shape=(jax.ShapeDtypeStruct((D_FF, d), BF16),),
                 out_specs=(pl.BlockSpec((half_w, d), lambda i, j, kk: (i, 0)),),
                 epilogue=store_epilogue)[0]

    dw_in4 = _mm(tag + "_dwin", (1, N_CHIPS, t // tk),
                 n, pl.BlockSpec((tk, d), lambda i, j, kk: (kk, 0)),
                 du, pl.BlockSpec((None, tk, SHARD_W), lambda i, j, kk: (j // 2, kk, j % 2)), (d, SHARD_W),
                 trans_a=True,
                 out_shape=(jax.ShapeDtypeStruct((N_CHIPS, d, SHARD_W), BF16),),
                 out_specs=(pl.BlockSpec((None, d, SHARD_W), lambda i, j, kk: (j, 0, 0)),),
                 epilogue=store_epilogue)[0]

    late = () if on_weight_grads is None else on_weight_grads(dw_in4, dw_out)

    dx, dg = _mm(tag + "_dn", (t // tm, 1, N_CHIPS),
                 du, pl.BlockSpec((None, tm, SHARD_W), lambda i, j, kk: (kk // 2, i, kk % 2)),
                 w_in4, pl.BlockSpec((None, d, SHARD_W), lambda i, j, kk: (kk, 0, 0)), (tm, d),
                 trans_b=True,
                 extras=(x, g, dres),
                 extra_specs=(pl.BlockSpec((tm, d), lambda i, j, kk: (i, 0)),
                              pl.BlockSpec((1, d), lambda i, j, kk: (0, 0)),
                              pl.BlockSpec((tm, d), lambda i, j, kk: (i, 0))), tokens=late,
                 out_shape=(jax.ShapeDtypeStruct((t, d), F32), jax.ShapeDtypeStruct((1, d), F32)),
                 out_specs=(pl.BlockSpec((tm, d), lambda i, j, kk: (i, 0)),
                            pl.BlockSpec((1, d), lambda i, j, kk: (0, 0))),
                 epilogue=_rms_bwd_epilogue, sem=("arbitrary", "arbitrary", "arbitrary"))
    return dx, dw_in4, dw_out, dg


def _loss_head(y, target):
    t, d = y.shape
    tm = min(ROW_TILE, t)

    def body(y_ref, t_ref, dy_ref, loss_ref):
        diff = y_ref[...] - t_ref[...]
        dy_ref[...] = diff * (1.0 / d)
        part = jnp.full((8, LANES), 0.5 / d * jnp.sum(diff * diff), F32)
        i = pl.program_id(0)

        @pl.when(i == 0)
        def _():
            loss_ref[...] = part

        @pl.when(i > 0)
        def _():
            loss_ref[...] += part

    return pl.pallas_call(
        body, name="loss_head", grid=(t // tm,),
        in_specs=[pl.BlockSpec((tm, d), lambda i: (i, 0)), pl.BlockSpec((tm, d), lambda i: (i, 0))],
        out_specs=[pl.BlockSpec((tm, d), lambda i: (i, 0)), pl.BlockSpec((8, LANES), lambda i: (0, 0))],
        out_shape=[jax.ShapeDtypeStruct((t, d), F32), jax.ShapeDtypeStruct((8, LANES), F32)],
        compiler_params=_params(("arbitrary",)),
    )(y, target)


def _conv_fill(zp_ref, a_ref, g_ref, ah_ref, gh_ref, i):
    zh = ah_ref[...].astype(F32) * _sigmoid(gh_ref[...].astype(F32))
    zp_ref[pl.ds(0, CONV_PAD), :] = jnp.where(i > 0, zh, 0.0)
    zp_ref[pl.ds(CONV_PAD, a_ref.shape[0]), :] = a_ref[...].astype(F32) * _sigmoid(g_ref[...].astype(F32))


def _shift_groups(shifts):
    groups = {}
    for j, s in shifts:
        groups.setdefault(s % 8, []).append((j, s // 8))
    return groups


def _windows(zp_ref, r0, lanes, groups):
    for q, taps in groups.items():
        deepest = max(p for _, p in taps)
        win = zp_ref[pl.ds(r0 + q, 8 * deepest + CONV_ROWS), lanes]
        for j, p in taps:
            yield j, win[8 * p:8 * p + CONV_ROWS]


def _conv_apply(zp_ref, out_ref, dw_ref, bias_ref, tm, ch, shifts):
    groups = _shift_groups(shifts)
    for cc in range(ch // LANES):
        lanes = pl.ds(cc * LANES, LANES)
        w = [dw_ref[pl.ds(j, 1), lanes] for j in range(CONV_WIDTH)]
        for r0 in range(0, tm, CONV_ROWS):
            if bias_ref is None:
                acc = jnp.zeros((CONV_ROWS, LANES), F32)
            else:
                acc = jnp.broadcast_to(bias_ref[:, lanes], (CONV_ROWS, LANES))
            for j, rows in _windows(zp_ref, r0, lanes, groups):
                acc = acc + w[j] * rows
            out_ref[pl.ds(r0, CONV_ROWS), lanes] = acc


FWD_SHIFTS = [(j, CONV_PAD - (CONV_WIDTH - 1) + j) for j in range(CONV_WIDTH)]
BWD_SHIFTS = [(j, CONV_WIDTH - 1 - j) for j in range(CONV_WIDTH)]


def _conv_taps(zp_ref, z1_ref, dw_ref, bias_ref, tm, ch):
    _conv_apply(zp_ref, z1_ref, dw_ref, bias_ref, tm, ch, FWD_SHIFTS)


def _conv_specs(tm, ch):
    per = tm // CONV_PAD
    cb = COL_CONV_G // ch
    return [pl.BlockSpec((tm, ch), lambda i: (i, 0)),
            pl.BlockSpec((tm, ch), lambda i: (i, cb)),
            pl.BlockSpec((CONV_PAD, ch), lambda i: (jnp.maximum(i * per - 1, 0), 0)),
            pl.BlockSpec((CONV_PAD, ch), lambda i: (jnp.maximum(i * per - 1, 0), cb))]


def _conv_fwd(p, dw, bias, ln_g, ln_b):
    t = p.shape[0]
    ch = D_MODEL
    tm = min(CONV_TILE, t)

    def body(a_ref, g_ref, ah_ref, gh_ref, dw_ref, bias_ref, lg_ref, lb_ref, o_ref, z1_ref, zp_ref):
        i = pl.program_id(0)
        _conv_fill(zp_ref, a_ref, g_ref, ah_ref, gh_ref, i)
        _conv_taps(zp_ref, z1_ref, dw_ref, bias_ref, tm, ch)
        z1 = z1_ref[...]
        mu = jnp.mean(z1, axis=-1, keepdims=True)
        zc = z1 - mu
        rs = lax.rsqrt(jnp.mean(zc * zc, axis=-1, keepdims=True) + EPS)
        z2 = zc * rs * lg_ref[...] + lb_ref[...]
        o_ref[...] = (z2 * _sigmoid(z2)).astype(BF16)

    vec = pl.BlockSpec((1, ch), lambda i: (0, 0))
    return pl.pallas_call(
        body, name="conv_fwd", grid=(t // tm,),
        in_specs=_conv_specs(tm, ch) + [pl.BlockSpec((CONV_PAD, ch), lambda i: (0, 0)), vec, vec, vec],
        out_specs=[pl.BlockSpec((tm, ch), lambda i: (i, 0)), pl.BlockSpec((tm, ch), lambda i: (i, 0))],
        out_shape=[jax.ShapeDtypeStruct((t, ch), BF16), jax.ShapeDtypeStruct((t, ch), F32)],
        scratch_shapes=[pltpu.VMEM((CONV_PAD + tm, ch), F32)],
        compiler_params=_params(("parallel",)),
    )(p, p, p, p, dw, bias, ln_g, ln_b)


def _conv_bwd_ln(p, z1_saved, dz3, ln_g, ln_b):
    t = p.shape[0]
    ch = D_MODEL
    tm = min(CONV_TILE, t)

    def body(a_ref, g_ref, ah_ref, gh_ref, z1_ref, dz3_ref, lg_ref, lb_ref,
             dz1_ref, ddw_ref, dbias_ref, dlg_ref, dlb_ref, zp_ref):
        i = pl.program_id(0)
        _conv_fill(zp_ref, a_ref, g_ref, ah_ref, gh_ref, i)
        z1 = z1_ref[...]
        mu = jnp.mean(z1, axis=-1, keepdims=True)
        zc = z1 - mu
        rs = lax.rsqrt(jnp.mean(zc * zc, axis=-1, keepdims=True) + EPS)
        xh = zc * rs
        z2 = xh * lg_ref[...] + lb_ref[...]
        sig = _sigmoid(z2)
        dz2 = dz3_ref[...].astype(F32) * (sig * (1.0 + z2 * (1.0 - sig)))
        dxh = dz2 * lg_ref[...]
        dz1 = rs * (dxh - jnp.mean(dxh, axis=-1, keepdims=True) - xh * jnp.mean(dxh * xh, axis=-1, keepdims=True))
        dz1_ref[...] = dz1

        @pl.when(i == 0)
        def _():
            ddw_ref[...] = jnp.zeros_like(ddw_ref)
            dbias_ref[...] = jnp.zeros_like(dbias_ref)
            dlg_ref[...] = jnp.zeros_like(dlg_ref)
            dlb_ref[...] = jnp.zeros_like(dlb_ref)

        dlg_ref[...] += jnp.sum(dz2 * xh, axis=0, keepdims=True)
        dlb_ref[...] += jnp.sum(dz2, axis=0, keepdims=True)
        dbias_ref[...] += jnp.sum(dz1, axis=0, keepdims=True)
        groups = _shift_groups(FWD_SHIFTS)
        for cc in range(ch // LANES):
            lanes = pl.ds(cc * LANES, LANES)
            accs = [jnp.zeros((8, LANES), F32) for _ in range(CONV_WIDTH)]
            for r0 in range(0, tm, CONV_ROWS):
                dzc = dz1_ref[pl.ds(r0, CONV_ROWS), lanes]
                for j, rows in _windows(zp_ref, r0, lanes, groups):
                    accs[j] = accs[j] + jnp.sum((dzc * rows).reshape(CONV_ROWS // 8, 8, LANES), axis=0)
            for j in range(CONV_WIDTH):
                ddw_ref[pl.ds(j, 1), lanes] += jnp.sum(accs[j], axis=0, keepdims=True)

    vec = pl.BlockSpec((1, ch), lambda i: (0, 0))
    return pl.pallas_call(
        body, name="conv_bwd_ln", grid=(t // tm,),
        in_specs=_conv_specs(tm, ch) + [pl.BlockSpec((tm, ch), lambda i: (i, 0)),
                                        pl.BlockSpec((tm, ch), lambda i: (i, 0)), vec, vec],
        out_specs=[pl.BlockSpec((tm, ch), lambda i: (i, 0)), pl.BlockSpec((CONV_PAD, ch), lambda i: (0, 0)), vec, vec, vec],
        out_shape=[jax.ShapeDtypeStruct((t, ch), F32), jax.ShapeDtypeStruct((CONV_PAD, ch), F32)]
        + [jax.ShapeDtypeStruct((1, ch), F32)] * 3,
        scratch_shapes=[pltpu.VMEM((CONV_PAD + tm, ch), F32)],
        compiler_params=_params(("arbitrary",)),
    )(p, p, p, p, z1_saved, dz3, ln_g, ln_b)


def _conv_bwd_glu(p, dz1, dw, dq, dkv, dgates):
    t = p.shape[0]
    ch = D_MODEL
    tm = min(CONV_TILE, t)
    per = tm // CONV_PAD
    n_halo = t // CONV_PAD
    cb = COL_CONV_G // ch

    def body(a_ref, g_ref, dz_ref, dzn_ref, dw_ref, dq_ref, dkv_ref, dgates_ref, o_ref, zp_ref, z0_ref):
        i = pl.program_id(0)
        o_ref[:, pl.ds(COL_Q, Q_W)] = dq_ref[...]
        o_ref[:, pl.ds(COL_K, 2 * KV_W)] = dkv_ref[...]
        o_ref[:, pl.ds(COL_GC, ch)] = dgates_ref[0]
        o_ref[:, pl.ds(COL_GA, ch)] = dgates_ref[1]
        zp_ref[pl.ds(0, tm), :] = dz_ref[...]
        zp_ref[pl.ds(tm, CONV_PAD), :] = jnp.where(i < t // tm - 1, dzn_ref[...], 0.0)
        _conv_apply(zp_ref, z0_ref, dw_ref, None, tm, ch, BWD_SHIFTS)
        dz0 = z0_ref[...]
        a = a_ref[...].astype(F32)
        sig = _sigmoid(g_ref[...].astype(F32))
        o_ref[:, pl.ds(0, ch)] = (dz0 * sig).astype(BF16)
        o_ref[:, pl.ds(ch, ch)] = (dz0 * a * sig * (1.0 - sig)).astype(BF16)

    return pl.pallas_call(
        body, name="conv_bwd_glu", grid=(t // tm,),
        in_specs=[pl.BlockSpec((tm, ch), lambda i: (i, 0)), pl.BlockSpec((tm, ch), lambda i: (i, cb)),
                  pl.BlockSpec((tm, ch), lambda i: (i, 0)),
                  pl.BlockSpec((CONV_PAD, ch), lambda i: (jnp.minimum((i + 1) * per, n_halo - 1), 0)),
                  pl.BlockSpec((CONV_PAD, ch), lambda i: (0, 0)),
                  pl.BlockSpec((tm, Q_W), lambda i: (i, 0)), pl.BlockSpec((tm, 2 * KV_W), lambda i: (i, 0)),
                  pl.BlockSpec((2, tm, ch), lambda i: (0, i, 0))],
        out_specs=pl.BlockSpec((tm, IN_W), lambda i: (i, 0)),
        out_shape=jax.ShapeDtypeStruct((t, IN_W), BF16),
        scratch_shapes=[pltpu.VMEM((tm + CONV_PAD, ch), F32), pltpu.VMEM((tm, ch), F32)],
        compiler_params=_params(("parallel",)),
    )(p, p, dz1, dz1, dw, dq, dkv, dgates)


def _bucket_onehot():
    qi = jnp.arange(BLOCK, dtype=jnp.int32)[:, None]
    kj = jnp.arange(2 * BLOCK, dtype=jnp.int32)[None, :]
    dist = jnp.maximum(qi + BLOCK - kj, 0)
    max_exact = N_BUCKETS // 2
    dflt = jnp.maximum(dist, 1).astype(F32)
    large = max_exact + (jnp.log(dflt / max_exact) / math.log(MAX_DISTANCE / max_exact)
                         * (N_BUCKETS - max_exact)).astype(jnp.int32)
    large = jnp.minimum(large, N_BUCKETS - 1)
    bucket = jnp.where(dist < max_exact, dist, large)
    onehot = bucket[None] == jnp.arange(N_BUCKETS, dtype=jnp.int32)[:, None, None]
    return onehot.astype(F32).reshape(N_BUCKETS, BLOCK * 2 * BLOCK)


def _bias_table(rel_bias_t, onehot):
    n = onehot.shape[1]
    tn = 4096

    def body(r_ref, oh_ref, o_ref):
        flat = pl.program_id(0) * tn + lax.broadcasted_iota(jnp.int32, (N_Q_HEADS, tn), 1)
        dist = (flat // (2 * BLOCK)) + BLOCK - (flat % (2 * BLOCK))
        bias = _dot(r_ref[...], oh_ref[...], precision=lax.Precision.HIGHEST)
        o_ref[...] = jnp.where((dist >= 0) & (dist < BLOCK), bias, NEG)

    return pl.pallas_call(
        body, name="bias_table", grid=(n // tn,),
        in_specs=[pl.BlockSpec((N_Q_HEADS, N_BUCKETS), lambda i: (0, 0)), pl.BlockSpec((N_BUCKETS, tn), lambda i: (0, i))],
        out_specs=pl.BlockSpec((N_Q_HEADS, tn), lambda i: (0, i)),
        out_shape=jax.ShapeDtypeStruct((N_Q_HEADS, n), F32),
        compiler_params=_params(("parallel",)),
    )(rel_bias_t, onehot)


def _bias_table_bwd(dbias, onehot):
    n = onehot.shape[1]
    tn = 4096

    def body(d_ref, oh_ref, o_ref):
        part = _dot(d_ref[...], oh_ref[...], trans_b=True, precision=lax.Precision.HIGHEST)
        i = pl.program_id(0)

        @pl.when(i == 0)
        def _():
            o_ref[...] = part

        @pl.when(i > 0)
        def _():
            o_ref[...] += part

    return pl.pallas_call(
        body, name="bias_table_bwd", grid=(n // tn,),
        in_specs=[pl.BlockSpec((N_Q_HEADS, tn), lambda i: (0, i)), pl.BlockSpec((N_BUCKETS, tn), lambda i: (0, i))],
        out_specs=pl.BlockSpec((N_Q_HEADS, N_BUCKETS), lambda i: (0, 0)),
        out_shape=jax.ShapeDtypeStruct((N_Q_HEADS, N_BUCKETS), F32),
        compiler_params=_params(("arbitrary",)),
    )(dbias, onehot)


def _lane_head(rows):
    return lax.broadcasted_iota(jnp.int32, (rows, KV_W), 1) // HEAD_DIM


def _group_rms(x, gain_wide):
    head = _lane_head(x.shape[0])
    sq = x * x
    r = jnp.zeros_like(x)
    for i in range(N_KV_HEADS):
        ms = jnp.sum(jnp.where(head == i, sq, 0.0), axis=-1, keepdims=True) * (1.0 / HEAD_DIM)
        r = jnp.where(head == i, lax.rsqrt(ms + EPS), r)
    return r, x * r * gain_wide


def _stack_heads(group):
    head = _lane_head(group.shape[0])
    return jnp.concatenate([jnp.where(head == i, group, jnp.zeros_like(group)) for i in range(N_KV_HEADS)], axis=0)


def _unstack_heads(stacked):
    head = _lane_head(BLOCK)
    out = jnp.where(head == 0, stacked[:BLOCK], 0.0)
    for i in range(1, N_KV_HEADS):
        out = out + jnp.where(head == i, stacked[i * BLOCK:(i + 1) * BLOCK], 0.0)
    return out


def _repeaters():
    row = lax.broadcasted_iota(jnp.int32, (KV_W, KV_W), 0)
    col = lax.broadcasted_iota(jnp.int32, (KV_W, KV_W), 1)
    return [(row == h * HEAD_DIM + col % HEAD_DIM).astype(BF16) for h in range(N_KV_HEADS)]


def _attn_probs(q_stack, k_rep, sink, bias, before_start):
    s = _dot(q_stack, k_rep, trans_b=True) * (1.0 / math.sqrt(HEAD_DIM)) + bias
    s = jnp.where(before_start, NEG, s)
    m = jnp.maximum(jnp.max(s, axis=-1, keepdims=True), sink)
    p = jnp.exp(s - m)
    es = jnp.exp(sink - m)
    inv = 1.0 / (jnp.sum(p, axis=-1, keepdims=True) + es)
    return p * inv, es * inv


def _before_start(n):
    col = lax.broadcasted_iota(jnp.int32, (QROWS, 2 * BLOCK), 1)
    return (col < BLOCK) & (n == 0)


KV_W = N_KV_HEADS * HEAD_DIM
Q_W = N_Q_HEADS * HEAD_DIM


def _attn_specs():
    qspec = pl.BlockSpec((BLOCK, Q_W), lambda n: (n, COL_Q // Q_W))
    kprev = pl.BlockSpec((BLOCK, KV_W), lambda n: (jnp.maximum(n - 1, 0), COL_K // KV_W))
    kcur = pl.BlockSpec((BLOCK, KV_W), lambda n: (n, COL_K // KV_W))
    vprev = pl.BlockSpec((BLOCK, KV_W), lambda n: (jnp.maximum(n - 1, 0), COL_V // KV_W))
    vcur = pl.BlockSpec((BLOCK, KV_W), lambda n: (n, COL_V // KV_W))
    gain = pl.BlockSpec((1, KV_W), lambda n: (0, 0))
    sink = pl.BlockSpec((N_KV_HEADS, QROWS, 1), lambda n: (0, 0, 0))
    bias = pl.BlockSpec((N_KV_HEADS, QROWS, 2 * BLOCK), lambda n: (0, 0, 0))
    return [qspec, kprev, kcur, vprev, vcur], gain, sink, bias


def _attn_fwd(p, gq, gk, sink_rows, bias):
    t = p.shape[0]
    nb = t // BLOCK
    qkv, gain, sink, bspec = _attn_specs()

    def body(q_ref, kp_ref, kc_ref, vp_ref, vc_ref, gq_ref, gk_ref, sink_ref, bias_ref, o_ref):
        before_start = _before_start(pl.program_id(0))
        rep = _repeaters()
        kf = jnp.concatenate([kp_ref[...], kc_ref[...]], axis=0).astype(F32)
        kn = _group_rms(kf, gk_ref[...])[1].astype(BF16)
        v = jnp.concatenate([vp_ref[...], vc_ref[...]], axis=0)
        for h in range(N_KV_HEADS):
            qn = _group_rms(q_ref[:, pl.ds(h * KV_W, KV_W)].astype(F32), gq_ref[...])[1]
            k_rep = _dot(kn, rep[h]).astype(BF16)
            pn = _attn_probs(_stack_heads(qn).astype(BF16), k_rep, sink_ref[h], bias_ref[h], before_start)[0]
            v_rep = _dot(v, rep[h]).astype(BF16)
            o_ref[:, pl.ds(h * KV_W, KV_W)] = _unstack_heads(_dot(pn.astype(BF16), v_rep)).astype(BF16)

    return pl.pallas_call(
        body, name="attn_fwd", grid=(nb,),
        in_specs=qkv + [gain, gain, sink, bspec],
        out_specs=pl.BlockSpec((BLOCK, Q_W), lambda n: (n, 0)), out_shape=jax.ShapeDtypeStruct((t, Q_W), BF16),
        compiler_params=_params(("parallel",)),
    )(p, p, p, p, p, gq, gk, sink_rows, bias)


def _attn_bwd(p, do, gq, gk, sink_rows, bias):
    t = p.shape[0]
    nb = t // BLOCK
    qkv, _, sink, bspec = _attn_specs()
    gain = pl.BlockSpec((1, HEAD_DIM), lambda n: (0, 0))
    scale = 1.0 / math.sqrt(HEAD_DIM)

    def head_selectors():
        row = lax.broadcasted_iota(jnp.int32, (KV_W, HEAD_DIM), 0)
        col = lax.broadcasted_iota(jnp.int32, (KV_W, HEAD_DIM), 1)
        return [(row == col + i * HEAD_DIM).astype(BF16) for i in range(N_KV_HEADS)]

    def take_heads(group, sel):
        return jnp.concatenate([_dot(group, s) for s in sel], axis=0)

    def put_heads(x, sel):
        rows = x.shape[0] // len(sel)
        out = _dot(x[:rows].astype(BF16), sel[0], trans_b=True)
        for i in range(1, len(sel)):
            out = out + _dot(x[i * rows:(i + 1) * rows].astype(BF16), sel[i], trans_b=True)
        return out

    def rms(x, g):
        r = lax.rsqrt(jnp.mean(x * x, axis=-1, keepdims=True) + EPS)
        return r, x * r * g

    def rms_bwd(dn, xf, r, g):
        w = dn * g
        dx = r * w - xf * (r * r * r) * jnp.mean(xf * w, axis=-1, keepdims=True)
        return dx, jnp.sum(dn * (xf * r), axis=0, keepdims=True)

    def body(q_ref, kp_ref, kc_ref, vp_ref, vc_ref, do_ref, gq_ref, gk_ref, sink_ref, bias_ref,
             dq_ref, dkv_ref, dbias_ref, dsink_ref, dgq_ref, dgk_ref):
        n = pl.program_id(0)
        sel = head_selectors()

        @pl.when(n == 0)
        def _():
            dbias_ref[...] = jnp.zeros_like(dbias_ref)
            dsink_ref[...] = jnp.zeros_like(dsink_ref)
            dgq_ref[...] = jnp.zeros_like(dgq_ref)
            dgk_ref[...] = jnp.zeros_like(dgk_ref)

        before_start = _before_start(n)
        dgq_sum = jnp.zeros((1, HEAD_DIM), F32)
        dgk_sum = jnp.zeros((1, HEAD_DIM), F32)
        dk_rows, dv_rows = [], []
        for h in range(N_KV_HEADS):
            qf = take_heads(q_ref[:, pl.ds(h * KV_W, KV_W)], sel)
            rq, qn = rms(qf, gq_ref[...])
            kf = jnp.concatenate([_dot(kp_ref[...], sel[h]), _dot(kc_ref[...], sel[h])], axis=0)
            rk, kn = rms(kf, gk_ref[...])
            pn, psink = _attn_probs(qn.astype(BF16), kn.astype(BF16), sink_ref[h], bias_ref[h], before_start)
            do = take_heads(do_ref[:, pl.ds(h * KV_W, KV_W)], sel).astype(BF16)
            v = jnp.concatenate([_dot(vp_ref[...], sel[h]), _dot(vc_ref[...], sel[h])], axis=0).astype(BF16)
            dv_win = _dot(do, pn.astype(BF16), trans_a=True).T
            dp = _dot(do, v, trans_b=True)
            delta = jnp.sum(pn * dp, axis=-1, keepdims=True)
            ds = pn * (dp - delta)
            dsc = (ds * scale).astype(BF16)
            dqn = _dot(dsc, kn.astype(BF16))
            dkn = _dot(qn.astype(BF16), dsc, trans_a=True).T
            dq, dgq = rms_bwd(dqn, qf, rq, gq_ref[...])
            dk_win, dgk = rms_bwd(dkn, kf, rk, gk_ref[...])
            dq_ref[:, pl.ds(h * KV_W, KV_W)] = put_heads(dq, sel).astype(BF16)
            dk_rows += [dk_win[:BLOCK], dk_win[BLOCK:]]
            dv_rows += [dv_win[:BLOCK], dv_win[BLOCK:]]
            dbias_ref[h] += ds
            dsink_ref[h] += jnp.sum((-psink * delta).reshape(GROUP, BLOCK, 1), axis=1)
            dgq_sum = dgq_sum + dgq
            dgk_sum = dgk_sum + dgk
        for part in range(2):
            dkv_ref[part, :, pl.ds(0, KV_W)] = put_heads(jnp.concatenate(dk_rows[part::2], axis=0), sel).astype(BF16)
            dkv_ref[part, :, pl.ds(KV_W, KV_W)] = put_heads(jnp.concatenate(dv_rows[part::2], axis=0), sel).astype(BF16)
        dgq_ref[...] += dgq_sum
        dgk_ref[...] += dgk_sum

    row = pl.BlockSpec((BLOCK, Q_W), lambda n: (n, 0))
    return pl.pallas_call(
        body, name="attn_bwd", grid=(nb,),
        in_specs=qkv + [row, gain, gain, sink, bspec],
        out_specs=[row, pl.BlockSpec((None, 2, BLOCK, 2 * KV_W), lambda n: (n, 0, 0, 0)), bspec,
                   pl.BlockSpec((N_KV_HEADS, GROUP, 1), lambda n: (0, 0, 0)), gain, gain],
        out_shape=[jax.ShapeDtypeStruct((t, Q_W), BF16),
                   jax.ShapeDtypeStruct((nb, 2, BLOCK, 2 * KV_W), BF16),
                   jax.ShapeDtypeStruct((N_KV_HEADS, QROWS, 2 * BLOCK), F32),
                   jax.ShapeDtypeStruct((N_KV_HEADS, GROUP, 1), F32),
                   jax.ShapeDtypeStruct((1, HEAD_DIM), F32),
                   jax.ShapeDtypeStruct((1, HEAD_DIM), F32)],
        compiler_params=_params(("arbitrary",)),
    )(p, p, p, p, p, do, gq, gk, sink_rows, bias)


def _kv_window_sum(parts):
    nb = parts.shape[0]

    def body(cur_ref, nxt_ref, o_ref):
        nxt = jnp.where(pl.program_id(0) < nb - 1, nxt_ref[...].astype(F32), 0.0)
        o_ref[...] = (cur_ref[...].astype(F32) + nxt).astype(BF16)

    blk = (None, None, BLOCK, 2 * KV_W)
    return pl.pallas_call(
        body, name="kv_window_sum", grid=(nb,),
        in_specs=[pl.BlockSpec(blk, lambda n: (n, 1, 0, 0)),
                  pl.BlockSpec(blk, lambda n: (jnp.minimum(n + 1, nb - 1), 0, 0, 0))],
        out_specs=pl.BlockSpec((BLOCK, 2 * KV_W), lambda n: (n, 0)),
        out_shape=jax.ShapeDtypeStruct((nb * BLOCK, 2 * KV_W), BF16),
        compiler_params=_params(("parallel",)),
    )(parts, parts)


GATE_TILE = 512


def _merge_fwd(z3, o, p, w_proj, w_o):
    t, d = z3.shape
    tm = min(ROW_TILE, t)
    tn = GATE_TILE

    def body(z_ref, o_ref, gc_ref, ga_ref, wp_ref, wo_ref, m_ref, a_ref, b_ref):
        a = _dot(z_ref[...], wp_ref[...])
        b = _dot(o_ref[...], wo_ref[...])
        m_ref[...] = (_sigmoid(gc_ref[...].astype(F32)) * a + _sigmoid(ga_ref[...].astype(F32)) * b).astype(BF16)
        a_ref[...] = a.astype(BF16)
        b_ref[...] = b.astype(BF16)

    row = pl.BlockSpec((tm, d), lambda i, j: (i, 0))
    wspec = pl.BlockSpec((d, tn), lambda i, j: (0, j))
    ospec = pl.BlockSpec((tm, tn), lambda i, j: (i, j))
    return pl.pallas_call(
        body, name="merge_fwd", grid=(t // tm, d // tn),
        in_specs=[row, row,
                  pl.BlockSpec((tm, tn), lambda i, j: (i, COL_GC // tn + j)),
                  pl.BlockSpec((tm, tn), lambda i, j: (i, COL_GA // tn + j)), wspec, wspec],
        out_specs=[ospec, ospec, ospec],
        out_shape=[jax.ShapeDtypeStruct((t, d), BF16)] * 3,
        compiler_params=_params(("parallel", "parallel")),
    )(z3, o, p, p, w_proj, w_o)


def _merge_bwd(dres, w_out, a, b, p, tokens=()):
    t, d = dres.shape
    tm = min(ROW_TILE, t)
    tn = GATE_TILE

    def epilogue(acc, ex, outs, ids):
        a_ref, b_ref, gc_ref, ga_ref = ex[:4]
        sc = _sigmoid(gc_ref[...].astype(F32))
        sa = _sigmoid(ga_ref[...].astype(F32))
        outs[0][...] = (acc * sc).astype(BF16)
        outs[1][...] = (acc * sa).astype(BF16)
        outs[2][0] = (acc * a_ref[...].astype(F32) * sc * (1.0 - sc)).astype(BF16)
        outs[2][1] = (acc * b_ref[...].astype(F32) * sa * (1.0 - sa)).astype(BF16)

    ospec = pl.BlockSpec((tm, tn), lambda i, j, kk: (i, j))
    return _mm("merge_bwd", (t // tm, d // tn, 1),
               dres, pl.BlockSpec((tm, d), lambda i, j, kk: (i, 0)),
               w_out, pl.BlockSpec((tn, d), lambda i, j, kk: (j, 0)), (tm, tn),
               trans_b=True, a_pre=_to_bf16,
               extras=(a, b, p, p),
               extra_specs=(ospec, ospec,
                            pl.BlockSpec((tm, tn), lambda i, j, kk: (i, COL_GC // tn + j)),
                            pl.BlockSpec((tm, tn), lambda i, j, kk: (i, COL_GA // tn + j))), tokens=tokens,
               out_shape=(jax.ShapeDtypeStruct((t, d), BF16), jax.ShapeDtypeStruct((t, d), BF16),
                          jax.ShapeDtypeStruct((2, t, d), BF16)),
               out_specs=(ospec, ospec, pl.BlockSpec((2, tm, tn), lambda i, j, kk: (0, i, j))),
               epilogue=epilogue)


def _store_epilogue(acc, ex, outs, ids):
    outs[0][...] = acc


def _store_bf16_epilogue(acc, ex, outs, ids):
    outs[0][...] = acc.astype(BF16)


def _mm_nt(name, a, w, out_dtype=BF16):
    t, n = a.shape
    k = w.shape[0]
    tm = min(ROW_TILE, t)
    return _mm(name, (t // tm, 1, 1), a, pl.BlockSpec((tm, n), lambda i, j, kk: (i, 0)),
               w, pl.BlockSpec((k, n), lambda i, j, kk: (0, 0)), (tm, k), trans_b=True,
               out_shape=(jax.ShapeDtypeStruct((t, k), out_dtype),),
               out_specs=(pl.BlockSpec((tm, k), lambda i, j, kk: (i, 0)),),
               epilogue=_store_bf16_epilogue if out_dtype == BF16 else _store_epilogue)[0]


def _mm_tn(name, a, b, b_pre=None):
    t, m = a.shape
    n = b.shape[1]
    tk = min(TOKEN_TILE, t)
    return _mm(name, (1, 1, t // tk), a, pl.BlockSpec((tk, m), lambda i, j, kk: (kk, 0)),
               b, pl.BlockSpec((tk, n), lambda i, j, kk: (kk, 0)), (m, n), trans_a=True, b_pre=b_pre,
               out_shape=(jax.ShapeDtypeStruct((m, n), BF16),),
               out_specs=(pl.BlockSpec((m, n), lambda i, j, kk: (0, 0)),), epilogue=_store_bf16_epilogue)[0]


def _local_step(x, target, small, comm):
    t = x.shape[0]
    w = dict(small)

    n1 = _rmsnorm_fwd("ffn1_norm", x, w["ffn1_norm"])
    w.update(comm.weights("A", n1))
    x1, ffn1_saved = _ffn_fwd("ffn1", x, n1, w["ffn1_w_in"], w["ffn1_w_out"], comm.tokens)
    w.update(comm.weights("B", x1))
    hm = _rmsnorm_fwd("mix_norm", x1, w["mix_norm"], comm.tokens)
    tm = min(ROW_TILE, t)
    p = _mm("mix_in", (N_CHIPS, t // tm, 1),
            hm, pl.BlockSpec((tm, D_MODEL), lambda j, i, kk: (i, 0)),
            w["w_in"], pl.BlockSpec((None, D_MODEL, SHARD_W), lambda j, i, kk: (j, 0, 0)), (tm, SHARD_W),
            out_shape=(jax.ShapeDtypeStruct((t, IN_W), BF16),),
            out_specs=(pl.BlockSpec((tm, SHARD_W), lambda j, i, kk: (i, j)),),
            epilogue=_store_bf16_epilogue)[0]

    z3, z1 = _conv_fwd(p, w["conv_dw_kernel"], w["conv_dw_bias"], w["conv_ln_g"], w["conv_ln_b"])

    onehot = _bucket_onehot()
    bias = _bias_table(w["rel_bias"].T, onehot).reshape(N_KV_HEADS, QROWS, 2 * BLOCK)
    sink_rows = jnp.repeat(w["attn_sinks"].reshape(N_KV_HEADS, GROUP), BLOCK, axis=1)[..., None]
    gq_wide = jnp.tile(w["q_norm"], (1, N_KV_HEADS))
    gk_wide = jnp.tile(w["k_norm"], (1, N_KV_HEADS))
    o = _attn_fwd(p, gq_wide, gk_wide, sink_rows, bias)

    merged, a, b = _merge_fwd(z3, o, p, w["conv_w_proj"], w["attn_w_o"])
    x2 = _mm_residual("mix_out", merged, w["w_out"], x1, 1.0)
    n2 = _rmsnorm_fwd("ffn2_norm", x2, w["ffn2_norm"])
    w.update(comm.weights("C", n2))
    x3, ffn2_saved = _ffn_fwd("ffn2", x2, n2, w["ffn2_w_in"], w["ffn2_w_out"])
    dy, loss = _loss_head(x3, target)

    g, big = {}, {}
    dres2, big["ffn2_w_in"], big["ffn2_w_out"], g["ffn2_norm"] = _ffn_bwd(
        "ffn2b", dy, x2, w["ffn2_norm"], ffn2_saved, w["ffn2_w_in"], w["ffn2_w_out"])
    tokens = comm.reduce_start("R1", big)

    da, db, dgates = _merge_bwd(dres2, w["w_out"], a, b, p, tokens)
    big = {}
    big["w_out"] = _mm_tn("d_w_out", merged, dres2, b_pre=_to_bf16)
    big["conv_w_proj"] = _mm_tn("d_w_proj", z3, da)
    big["attn_w_o"] = _mm_tn("d_w_o", o, db)
    dz3 = _mm_nt("d_z3", da, w["conv_w_proj"])
    do = _mm_nt("d_o", db, w["attn_w_o"])

    dq, dkv_parts, dbias, dsink, g["q_norm"], g["k_norm"] = _attn_bwd(
        p, do, w["q_norm"], w["k_norm"], sink_rows, bias)
    dkv = _kv_window_sum(dkv_parts)
    g["rel_bias"] = _bias_table_bwd(dbias.reshape(N_Q_HEADS, BLOCK * 2 * BLOCK), onehot).T
    g["attn_sinks"] = dsink.reshape(N_Q_HEADS)

    dz1, big["conv_dw_kernel"], g["conv_dw_bias"], g["conv_ln_g"], g["conv_ln_b"] = _conv_bwd_ln(
        p, z1, dz3, w["conv_ln_g"], w["conv_ln_b"])
    dp = _conv_bwd_glu(p, dz1, w["conv_dw_kernel"], dq, dkv, dgates)
    tk = min(TOKEN_TILE, t)
    big["w_in"] = _mm("d_w_in", (1, N_CHIPS, t // tk),
                    hm, pl.BlockSpec((tk, D_MODEL), lambda i, j, kk: (kk, 0)),
                    dp, pl.BlockSpec((tk, SHARD_W), lambda i, j, kk: (kk, j)), (D_MODEL, SHARD_W),
                    trans_a=True,
                    out_shape=(jax.ShapeDtypeStruct((N_CHIPS, D_MODEL, SHARD_W), BF16),),
                    out_specs=(pl.BlockSpec((None, D_MODEL, SHARD_W), lambda i, j, kk: (j, 0, 0)),),
                    epilogue=_store_bf16_epilogue)[0]
    dres1, g["mix_norm"] = _mm("d_mix", (t // tm, 1, N_CHIPS),
                               dp, pl.BlockSpec((tm, SHARD_W), lambda i, j, kk: (i, kk)),
                               w["w_in"], pl.BlockSpec((None, D_MODEL, SHARD_W), lambda i, j, kk: (kk, 0, 0)),
                               (tm, D_MODEL), trans_b=True,
                               extras=(x1, w["mix_norm"], dres2),
                               extra_specs=(pl.BlockSpec((tm, D_MODEL), lambda i, j, kk: (i, 0)),
                                            pl.BlockSpec((1, D_MODEL), lambda i, j, kk: (0, 0)),
                                            pl.BlockSpec((tm, D_MODEL), lambda i, j, kk: (i, 0))),
                               out_shape=(jax.ShapeDtypeStruct((t, D_MODEL), F32), jax.ShapeDtypeStruct((1, D_MODEL), F32)),
                               out_specs=(pl.BlockSpec((tm, D_MODEL), lambda i, j, kk: (i, 0)),
                                          pl.BlockSpec((1, D_MODEL), lambda i, j, kk: (0, 0))),
                               epilogue=_rms_bwd_epilogue, sem=("arbitrary", "arbitrary", "arbitrary"))

    comm.reduce_finish("R1", dres1)
    tokens = comm.reduce_start("R2", big)

    def ffn1_grads(dw_in4, dw_out):
        comm.reduce_finish("R2", dw_in4)
        return comm.reduce_start("R3", {"ffn1_w_in": dw_in4, "ffn1_w_out": dw_out})

    grad_x, _, _, g["ffn1_norm"] = _ffn_bwd(
        "ffn1b", dres1, x, w["ffn1_norm"], ffn1_saved, w["ffn1_w_in"], w["ffn1_w_out"], tokens, ffn1_grads)
    comm.reduce_finish("R3", grad_x)
    return loss[0, 0], grad_x, g


def _mesh_place():
    x, y, c = lax.axis_index("x"), lax.axis_index("y"), lax.axis_index("c")
    chips = [(1 - x, y), (x, 1 - y), (1 - x, 1 - y)]
    return x, y, c, chips


def _any_specs(n):
    return [pl.BlockSpec(memory_space=pl.ANY)] * n


HBM_SPEC = pl.BlockSpec(memory_space=pltpu.HBM)
SEM_SPEC = pl.BlockSpec(memory_space=pltpu.SEMAPHORE)
EFFECT = pltpu.SideEffectType.DATAFLOW_SIDE_EFFECTING


def _in_hbm(a):
    return pltpu.with_memory_space_constraint(a, pltpu.HBM)


def _copy_start(name, srcs, lands, plan, after=()):
    ns, nb = len(srcs), len(lands)
    n = plan.copies_per_source * ns

    def body(*refs):
        s_refs, l_refs = refs[:ns], refs[ns:ns + nb]
        send_sems, recv_sems = refs[ns + nb + len(after)], refs[ns + nb + len(after) + 1]
        token = refs[-1]
        for k, (src, dst, to, _) in enumerate(plan(s_refs, l_refs)):
            pltpu.make_async_remote_copy(src_ref=src, dst_ref=dst, send_sem=send_sems.at[k], recv_sem=recv_sems.at[k],
                                         device_id=to, device_id_type=MESH).start()
        token[...] = jnp.zeros_like(token)

    bufs = list(srcs) + list(lands)
    outs = pl.pallas_call(
        body, name=name,
        out_shape=(pltpu.SemaphoreType.DMA((n,)), pltpu.SemaphoreType.DMA((n,)),
                   *[pltpu.HBM(a.shape, a.dtype) for a in bufs], jax.ShapeDtypeStruct((8, LANES), F32)),
        in_specs=[HBM_SPEC] * len(bufs) + [pl.BlockSpec(memory_space=pl.ANY)] * len(after),
        out_specs=(SEM_SPEC, SEM_SPEC, *[HBM_SPEC] * len(bufs), pl.BlockSpec(memory_space=pltpu.VMEM)),
        input_output_aliases={i: 2 + i for i in range(len(bufs))},
        compiler_params=pltpu.CompilerParams(has_side_effects=EFFECT),
    )(*[_in_hbm(a) for a in bufs], *after)
    return outs[0], outs[1], list(outs[2:2 + ns]), list(outs[2 + ns:2 + ns + nb]), outs[-1]


def _copy_wait(name, send_sems, recv_sems, srcs, lands, after, plan):
    ns, nb = len(srcs), len(lands)

    def body(*refs):
        s_refs, l_refs = refs[:ns], refs[ns:ns + nb]
        send_sems, recv_sems = refs[ns + nb], refs[ns + nb + 1]
        for k, (src, _, to, mine) in enumerate(plan(s_refs, l_refs)):
            cp = pltpu.make_async_remote_copy(src_ref=src, dst_ref=mine, send_sem=send_sems.at[k], recv_sem=recv_sems.at[k],
                                              device_id=to, device_id_type=MESH)
            cp.wait_send()
            cp.wait_recv()

    bufs = list(srcs) + list(lands)
    outs = pl.pallas_call(
        body, name=name,
        out_shape=tuple(pltpu.HBM(a.shape, a.dtype) for a in bufs),
        in_specs=[HBM_SPEC] * len(bufs) + [SEM_SPEC, SEM_SPEC, pl.BlockSpec(memory_space=pl.ANY)],
        out_specs=tuple([HBM_SPEC] * len(bufs)),
        input_output_aliases={i: i for i in range(len(bufs))},
        compiler_params=pltpu.CompilerParams(has_side_effects=EFFECT),
    )(*bufs, send_sems, recv_sems, after)
    return list(outs[:ns]), list(outs[ns:])


def _gather_plan(s_refs, l_refs):
    x, y, c, chips = _mesh_place()
    jme = 2 * x + y
    return [(s.at[c], land.at[jme, c], (*chip, c), land.at[2 * chip[0] + chip[1], c])
            for s, land in zip(s_refs, l_refs) for chip in chips]


_gather_plan.copies_per_source = 3


def _gather_both_cores_plan(s_refs, l_refs):
    x, y, c, chips = _mesh_place()
    jme = 2 * x + y
    plan = []
    for s, land in zip(s_refs, l_refs):
        for chip in chips:
            for peer_core in (c, 1 - c):
                plan.append((s.at[c], land.at[jme, c], (*chip, peer_core), land.at[2 * chip[0] + chip[1], peer_core]))
        plan.append((s, land.at[jme], (x, y, 1 - c), land.at[jme]))
    return plan


_gather_both_cores_plan.copies_per_source = 7


def _scatter_plan(s_refs, l_refs):
    x, y, c, chips = _mesh_place()
    return [(s.at[2 * chip[0] + chip[1]], land.at[k], (*chip, c), land.at[k])
            for s, land in zip(s_refs, l_refs) for k, chip in enumerate(chips)]


_scatter_plan.copies_per_source = 3


def _gather_forward(name, shards, landed):
    nw = len(shards)

    def body(*refs):
        s_refs, o_refs = refs[:nw], refs[2 * nw:3 * nw]
        send_sems, recv_sems = refs[3 * nw:]
        x, y, c, chips = _mesh_place()
        me, sib, jme = (x, y, c), (x, y, 1 - c), 2 * x + y
        sent = []
        for w in range(nw):
            parts = [(o_refs[w].at[2 * chip[0] + chip[1], c], o_refs[w].at[2 * chip[0] + chip[1], c]) for chip in chips]
            parts.append((s_refs[w], o_refs[w].at[jme]))
            for k, (src, dst) in enumerate(parts):
                cp = pltpu.make_async_remote_copy(src_ref=src, dst_ref=dst, send_sem=send_sems.at[4 * w + k],
                                                  recv_sem=recv_sems.at[4 * w + k], device_id=sib, device_id_type=MESH)
                cp.start()
                sent.append(cp)
        for w in range(nw):
            parts = [o_refs[w].at[2 * chip[0] + chip[1], 1 - c] for chip in chips] + [o_refs[w].at[jme]]
            for k, part in enumerate(parts):
                pltpu.make_async_remote_copy(src_ref=part, dst_ref=part, send_sem=send_sems.at[4 * w + k],
                                             recv_sem=recv_sems.at[4 * w + k], device_id=me, device_id_type=MESH).wait_recv()
        for cp in sent:
            cp.wait_send()

    return pl.pallas_call(
        body, name=name,
        in_specs=_any_specs(2 * nw), out_specs=_any_specs(nw),
        out_shape=[jax.ShapeDtypeStruct(a.shape, a.dtype) for a in landed],
        input_output_aliases={nw + i: i for i in range(nw)},
        scratch_shapes=[pltpu.SemaphoreType.DMA((4 * nw,)), pltpu.SemaphoreType.DMA((4 * nw,))],
    )(*shards, *landed)


def _exchange_halves(name, grads, after=()):
    nw = len(grads)

    def body(*refs):
        g_refs, o_refs = refs[:nw], refs[nw + len(after):2 * nw + len(after)]
        send_sems, recv_sems = refs[2 * nw + len(after):]
        x, y, c, _ = _mesh_place()
        copies = []
        for w in range(nw):
            cp = pltpu.make_async_remote_copy(src_ref=g_refs[w].at[:, 1 - c], dst_ref=o_refs[w], send_sem=send_sems.at[w],
                                              recv_sem=recv_sems.at[w], device_id=(x, y, 1 - c), device_id_type=MESH)
            cp.start()
            copies.append(cp)
        for cp in copies:
            cp.wait()

    return pl.pallas_call(
        body, name=name,
        in_specs=_any_specs(nw + len(after)), out_specs=_any_specs(nw),
        out_shape=[jax.ShapeDtypeStruct((N_CHIPS,) + g.shape[2:], g.dtype) for g in grads],
        scratch_shapes=[pltpu.SemaphoreType.DMA((nw,)), pltpu.SemaphoreType.DMA((nw,))],
    )(*grads, *after)


def _row_tile(r):
    for cand in (256, 176, 128, 64, 32, 16, 8):
        if r % cand == 0:
            return cand
    return r


def _add_own_half(c_idx, grad, got):
    _, _, r, cols = grad.shape
    tr = _row_tile(r)

    def body(c_ref, g_ref, o_ref, out_ref):
        out_ref[...] = (g_ref[...].astype(F32) + o_ref[...].astype(F32)).astype(BF16)

    return pl.pallas_call(
        body, name="add_own_half",
        grid_spec=pltpu.PrefetchScalarGridSpec(
            num_scalar_prefetch=1, grid=(N_CHIPS, r // tr),
            in_specs=[pl.BlockSpec((None, None, tr, cols), lambda j, i, c_ref: (j, c_ref[0], i, 0)),
                      pl.BlockSpec((None, tr, cols), lambda j, i, c_ref: (j, i, 0))],
            out_specs=pl.BlockSpec((None, tr, cols), lambda j, i, c_ref: (j, i, 0))),
        out_shape=jax.ShapeDtypeStruct((N_CHIPS, r, cols), BF16),
        compiler_params=_params(("parallel", "parallel")),
    )(c_idx, grad, got)


def _sum_pieces(place_idx, sums, landed):
    _, r, cols = sums.shape
    tr = _row_tile(r)

    def body(j_ref, own_ref, p_ref, o_ref):
        o_ref[...] = ((own_ref[...].astype(F32) + p_ref[0].astype(F32)) + p_ref[1].astype(F32)) + p_ref[2].astype(F32)

    return pl.pallas_call(
        body, name="sum_pieces",
        grid_spec=pltpu.PrefetchScalarGridSpec(
            num_scalar_prefetch=1, grid=(r // tr,),
            in_specs=[pl.BlockSpec((None, tr, cols), lambda i, j_ref: (j_ref[0], i, 0)),
                      pl.BlockSpec((N_CHIPS - 1, tr, cols), lambda i, j_ref: (0, i, 0))],
            out_specs=pl.BlockSpec((None, tr, cols), lambda i, j_ref: (j_ref[1], i, 0))),
        out_shape=jax.ShapeDtypeStruct((2, r, cols), F32),
        compiler_params=_params(("parallel",)),
    )(place_idx, sums, landed)


def _join_halves(name, halves):
    nw = len(halves)

    def body(*refs):
        o_refs = refs[nw:2 * nw]
        send_sems, recv_sems = refs[2 * nw:]
        x, y, c, _ = _mesh_place()
        copies = []
        for w in range(nw):
            cp = pltpu.make_async_remote_copy(src_ref=o_refs[w].at[c], dst_ref=o_refs[w].at[c], send_sem=send_sems.at[w],
                                              recv_sem=recv_sems.at[w], device_id=(x, y, 1 - c), device_id_type=MESH)
            cp.start()
            copies.append(cp)
        for w in range(nw):
            copies[w].wait_send()
            landed = o_refs[w].at[1 - c]
            pltpu.make_async_remote_copy(src_ref=landed, dst_ref=landed, send_sem=send_sems.at[w], recv_sem=recv_sems.at[w],
                                         device_id=(x, y, c), device_id_type=MESH).wait_recv()

    return pl.pallas_call(
        body, name=name,
        in_specs=_any_specs(nw), out_specs=_any_specs(nw),
        out_shape=[jax.ShapeDtypeStruct(h.shape, F32) for h in halves],
        input_output_aliases={i: i for i in range(nw)},
        scratch_shapes=[pltpu.SemaphoreType.DMA((nw,)), pltpu.SemaphoreType.DMA((nw,))],
    )(*halves)


SMALL_ROWS = 8


def _all_reduce_small(pack):
    rows, cols = pack.shape
    n_dev = 8

    def body(p_ref, o_ref, slots, send_sems, recv_sems):
        x, y, c, _ = _mesh_place()
        me = 4 * x + 2 * y + c
        slots[me] = p_ref[...]
        copies = []
        for k in range(1, n_dev):
            peer = (me + k) % n_dev
            cp = pltpu.make_async_remote_copy(src_ref=p_ref, dst_ref=slots.at[me], send_sem=send_sems.at[k],
                                              recv_sem=recv_sems.at[k],
                                              device_id=(peer // 4, (peer // 2) % 2, peer % 2), device_id_type=MESH)
            cp.start()
            copies.append(cp)
        for k in range(1, n_dev):
            src = (me + n_dev - k) % n_dev
            pltpu.make_async_remote_copy(src_ref=p_ref, dst_ref=slots.at[src], send_sem=send_sems.at[k],
                                         recv_sem=recv_sems.at[k], device_id=(x, y, c), device_id_type=MESH).wait_recv()
        for cp in copies:
            cp.wait_send()
        total = slots[0]
        for s in range(1, n_dev):
            total = total + slots[s]
        o_ref[...] = total

    return pl.pallas_call(
        body, name="all_reduce_small",
        in_specs=[pl.BlockSpec(memory_space=pltpu.VMEM)], out_specs=pl.BlockSpec(memory_space=pltpu.VMEM),
        out_shape=jax.ShapeDtypeStruct((rows, cols), F32),
        scratch_shapes=[pltpu.VMEM((n_dev, rows, cols), F32), pltpu.SemaphoreType.DMA((n_dev,)),
                        pltpu.SemaphoreType.DMA((n_dev,))],
    )(pack)


def _adamw(name, w, g, m, v):
    r, cols = w.shape
    tr = _row_tile(r)

    def body(w_ref, g_ref, m_ref, v_ref, d_ref, nm_ref, nv_ref):
        gv = g_ref[...]
        nm = ADAM_B1 * m_ref[...] + (1.0 - ADAM_B1) * gv
        nv = ADAM_B2 * v_ref[...] + (1.0 - ADAM_B2) * (gv * gv)
        m_hat = nm / (1.0 - ADAM_B1 ** ADAM_STEP)
        v_hat = nv / (1.0 - ADAM_B2 ** ADAM_STEP)
        d_ref[...] = -ADAM_LR * (m_hat / (jnp.sqrt(v_hat) + ADAM_EPS) + ADAM_WD * w_ref[...])
        nm_ref[...] = nm
        nv_ref[...] = nv

    spec = pl.BlockSpec((tr, cols), lambda i: (i, 0))
    return pl.pallas_call(
        body, name=name, grid=(r // tr,),
        in_specs=[spec] * 4, out_specs=[spec] * 3,
        out_shape=[jax.ShapeDtypeStruct((r, cols), F32)] * 3,
        compiler_params=_params(("parallel",)),
    )(w, g, m, v)


BIG = ["ffn1_w_in", "ffn1_w_out", "w_in", "conv_w_proj", "attn_w_o", "w_out", "ffn2_w_in", "ffn2_w_out", "conv_dw_kernel"]
COL_SHARDED = ("ffn1_w_in", "w_in", "ffn2_w_in")
SMALL = ["ffn1_norm", "mix_norm", "ffn2_norm", "conv_dw_bias", "conv_ln_g", "conv_ln_b", "q_norm", "k_norm", "attn_sinks", "rel_bias"]
WEIGHTS = ["ffn1_norm", "ffn1_w_in", "ffn1_w_out", "mix_norm", "w_in", "conv_dw_kernel", "conv_dw_bias", "conv_ln_g",
           "conv_ln_b", "conv_w_proj", "q_norm", "k_norm", "attn_sinks", "rel_bias", "attn_w_o", "w_out", "ffn2_norm",
           "ffn2_w_in", "ffn2_w_out"]
SMALL_PLACE = {"ffn1_norm": (0, 0, 1024), "mix_norm": (1, 0, 1024), "ffn2_norm": (2, 0, 1024), "conv_dw_bias": (3, 0, 1024),
               "conv_ln_g": (4, 0, 1024), "conv_ln_b": (5, 0, 1024), "q_norm": (6, 0, 64), "k_norm": (6, 128, 64),
               "attn_sinks": (6, 256, 16), "rel_bias": (7, 0, 512)}
LOSS_PLACE = (6, 384)


def _pack_small(vals, fill=0.0, loss=None):
    pack = jnp.full((SMALL_ROWS, D_MODEL), fill, F32)
    for name, (row, lane, n) in SMALL_PLACE.items():
        pack = pack.at[row, lane:lane + n].set(vals[name].reshape(n))
    if loss is not None:
        pack = pack.at[LOSS_PLACE[0], LOSS_PLACE[1]].set(loss)
    return pack


def _unpack_small(pack, shapes):
    return {name: pack[row, lane:lane + n].reshape(shapes[name]) for name, (row, lane, n) in SMALL_PLACE.items()}


def _shard_halves(name, a):
    if name == "conv_dw_kernel":
        a = jnp.pad(a, ((0, CONV_PAD - CONV_WIDTH), (0, 0)))
    r, cols = a.shape
    return a.reshape(2, r // 2, cols)


GATHER_GROUPS = {"A": ["ffn1_w_in", "ffn1_w_out"],
                 "B": ["w_in", "conv_dw_kernel", "conv_w_proj", "attn_w_o", "w_out"],
                 "C": ["ffn2_w_in", "ffn2_w_out"]}


class _MeshComm:
    def __init__(self, wts):
        self.c_idx = lax.axis_index("c").astype(jnp.int32).reshape(1)
        self.place_idx = jnp.stack([2 * lax.axis_index("x") + lax.axis_index("y"), lax.axis_index("c")]).astype(jnp.int32)
        self.wts, self.gathers, self.reductions, self.reduced = wts, {}, {}, {}
        self.tokens, self.last_join = (), ()
        self._gather_start("A", ())

    def _gather_start(self, group, after):
        names = GATHER_GROUPS[group]
        shards = [_shard_halves(n, self.wts[n]) if n == "conv_dw_kernel" else _shard_halves(n, self.wts[n]).astype(BF16)
                  for n in names]
        lands = [lax.empty((N_CHIPS,) + s.shape, s.dtype) for s in shards]
        self.gathers[group] = _copy_start("gather_start_" + group, shards, lands, self._gather_plan(group), after=after)
        self.tokens = (self.gathers[group][-1],)

    @staticmethod
    def _gather_plan(group):
        return _gather_plan if group == "A" else _gather_both_cores_plan

    def weights(self, group, after):
        send_sems, recv_sems, shards, lands, token = self.gathers.pop(group)
        shards, lands = _copy_wait("gather_wait_" + group, send_sems, recv_sems, shards, lands,
                                   token if after is None else after, self._gather_plan(group))
        gathered = _gather_forward("gather_forward_" + group, shards, lands) if group == "A" else lands
        self.tokens = ()
        following = {"A": "B", "B": "C"}.get(group)
        if following:
            self._gather_start(following, (gathered[0],))
        out = {}
        for n, g4 in zip(GATHER_GROUPS[group], gathered):
            r, cols = g4.shape[2] * 2, g4.shape[3]
            if n in COL_SHARDED:
                out[n] = g4.reshape(N_CHIPS, r, cols)
            elif n == "conv_dw_kernel":
                out[n] = g4.reshape(N_CHIPS, r, cols).transpose(1, 0, 2).reshape(r, N_CHIPS * cols)
            else:
                out[n] = g4.reshape(N_CHIPS * r, cols)
        return out

    def reduce_start(self, group, grads):
        names = list(grads)
        g4 = []
        for n in names:
            a = grads[n]
            if n == "conv_dw_kernel":
                a = a.reshape(CONV_PAD, N_CHIPS, -1).transpose(1, 0, 2)
            elif n not in COL_SHARDED:
                a = a.reshape(N_CHIPS, a.shape[0] // N_CHIPS, a.shape[1])
            g4.append(a.reshape(N_CHIPS, 2, a.shape[1] // 2, a.shape[2]))
        got = _exchange_halves("exchange_halves_" + group, g4, after=self.last_join)
        sums = [_add_own_half(self.c_idx, a, b) for a, b in zip(g4, got)]
        lands = [lax.empty((N_CHIPS - 1,) + s.shape[1:], s.dtype) for s in sums]
        started = _copy_start("scatter_start_" + group, sums, lands, _scatter_plan, after=self.last_join)
        self.reductions[group] = (names,) + started
        return (started[-1],)

    def reduce_finish(self, group, after):
        names, send_sems, recv_sems, sums, lands, _ = self.reductions.pop(group)
        sums, lands = _copy_wait("scatter_wait_" + group, send_sems, recv_sems, sums, lands, after, _scatter_plan)
        halves = [_sum_pieces(self.place_idx, s, p) for s, p in zip(sums, lands)]
        joined = _join_halves("join_halves_" + group, halves)
        self.last_join = (joined[0],)
        self.reduced.update(zip(names, joined))


def kernel(x, ffn1_norm, ffn1_w_in, ffn1_w_out, mix_norm, w_in, conv_dw_kernel, conv_dw_bias, conv_ln_g, conv_ln_b, conv_w_proj, q_norm, k_norm, attn_sinks, rel_bias, attn_w_o, w_out, ffn2_norm, ffn2_w_in, ffn2_w_out, loss_target, m_ffn1_norm, m_ffn1_w_in, m_ffn1_w_out, m_mix_norm, m_w_in, m_conv_dw_kernel, m_conv_dw_bias, m_conv_ln_g, m_conv_ln_b, m_conv_w_proj, m_q_norm, m_k_norm, m_attn_sinks, m_rel_bias, m_attn_w_o, m_w_out, m_ffn2_norm, m_ffn2_w_in, m_ffn2_w_out, v_ffn1_norm, v_ffn1_w_in, v_ffn1_w_out, v_mix_norm, v_w_in, v_conv_dw_kernel, v_conv_dw_bias, v_conv_ln_g, v_conv_ln_b, v_conv_w_proj, v_q_norm, v_k_norm, v_attn_sinks, v_rel_bias, v_attn_w_o, v_w_out, v_ffn2_norm, v_ffn2_w_in, v_ffn2_w_out):
    args = dict(locals())
    wts = {n: args[n] for n in WEIGHTS}
    mom = {n: args["m_" + n] for n in WEIGHTS}
    var = {n: args["v_" + n] for n in WEIGHTS}
    comm = _MeshComm(wts)
    small = {n: wts[n] if n in ("attn_sinks", "rel_bias") else wts[n].reshape(1, -1) for n in SMALL}
    loss_part, grad_x, g = _local_step(x[0], loss_target[0], small, comm)

    small_sum = _all_reduce_small(_pack_small(g, loss=loss_part))
    loss = small_sum[LOSS_PLACE[0], LOSS_PLACE[1]]
    small_shapes = {n: wts[n].shape for n in SMALL}
    g_small = _unpack_small(small_sum, small_shapes)

    grads, delta, new_m, new_v = {}, {}, {}, {}
    for n in BIG:
        j = comm.reduced[n]
        gs = j.reshape(j.shape[1] * 2, j.shape[2])
        pad = n == "conv_dw_kernel"
        ws, ms, vs = (_shard_halves(n, a).reshape(gs.shape) for a in (wts[n], mom[n], var[n]))
        d, nm, nv = _adamw("adamw_" + n, ws, gs, ms, vs)
        cut = (lambda a: a[:CONV_WIDTH]) if pad else (lambda a: a)
        grads[n], delta[n], new_m[n], new_v[n] = cut(gs), cut(d), cut(nm), cut(nv)
    d, nm, nv = _adamw("adamw_small", _pack_small(wts), small_sum, _pack_small(mom), _pack_small(var, fill=1.0))
    grads.update(g_small)
    delta.update(_unpack_small(d, small_shapes))
    new_m.update(_unpack_small(nm, small_shapes))
    new_v.update(_unpack_small(nv, small_shapes))

    return (loss, grad_x[None], *[grads[n] for n in WEIGHTS], *[delta[n] for n in WEIGHTS],
            *[new_m[n] for n in WEIGHTS], *[new_v[n] for n in WEIGHTS])
```

```python
import functools
import math

import jax
import jax.numpy as jnp
from jax import lax
from jax.experimental import pallas as pl
from jax.experimental.pallas import tpu as pltpu

F32 = jnp.float32
BF16 = jnp.bfloat16
MESH = pl.DeviceIdType.MESH

EPS = 1e-6
D_MODEL = 1024
D_FF = 2816
N_CHIPS = 4
SHARD_W = 2 * D_FF // N_CHIPS
HEAD_DIM = 64
N_Q_HEADS = 16
N_KV_HEADS = 4
GROUP = N_Q_HEADS // N_KV_HEADS
BLOCK = 128
QROWS = GROUP * BLOCK
N_BUCKETS = 32
MAX_DISTANCE = 128
CONV_WIDTH = 31
CONV_PAD = 32
NEG = float(jnp.finfo(jnp.float32).min)

ADAM_LR = 0.001
ADAM_B1 = 0.9
ADAM_B2 = 0.999
ADAM_EPS = 1e-08
ADAM_WD = 0.01
ADAM_STEP = 10

VMEM_LIMIT_BYTES = 56 * 1024 * 1024
ROW_TILE = 1024
TOKEN_TILE = 1024
CONV_TILE = 256
CONV_ROWS = 128
LANES = 128

COL_CONV_A, COL_CONV_G, COL_Q, COL_K, COL_V, COL_GC, COL_GA = 0, 1024, 2048, 3072, 3328, 3584, 4608
IN_W = 5632


def _params(sem, vmem=VMEM_LIMIT_BYTES):
    return pltpu.CompilerParams(dimension_semantics=sem, vmem_limit_bytes=vmem)


def _sigmoid(x):
    return 1.0 / (1.0 + jnp.exp(-x))


def _dot(a, b, trans_a=False, trans_b=False, precision=None):
    dn = (((0,) if trans_a else (1,), (1,) if trans_b else (0,)), ((), ()))
    return lax.dot_general(a, b, dn, preferred_element_type=F32, precision=precision)


def _mm(name, grid, a, a_spec, b, b_spec, acc_shape, *, trans_a=False, trans_b=False, a_pre=None, b_pre=None,
        extras=(), extra_specs=(), tokens=(), out_shape, out_specs, epilogue, chunked=False,
        sem=("parallel", "parallel", "arbitrary")):
    n_k = grid[2]
    assert not chunked or (n_k == 1 and b_pre is None)
    extras = tuple(extras) + tuple(tokens)
    extra_specs = tuple(extra_specs) + (pl.BlockSpec((8, LANES), lambda i, j, kk: (0, 0)),) * len(tokens)
    n_extra = len(extras)
    n_out = len(out_shape)

    def body(a_ref, b_ref, *rest):
        ex = rest[:n_extra]
        outs = rest[n_extra:n_extra + n_out]
        ids = (pl.program_id(0), pl.program_id(1), pl.program_id(2))
        av = a_ref[...]
        if a_pre is not None:
            av = a_pre(av)
        if chunked:
            for c0, cw in _col_chunks(acc_shape[1]):
                cols = pl.ds(c0, cw)
                epilogue(_dot(av, b_ref[cols, :] if trans_b else b_ref[:, cols], trans_a, trans_b), ex, outs, ids, cols)
            return
        bv = b_ref[...]
        if b_pre is not None:
            bv = b_pre(bv)
        if n_k == 1:
            epilogue(_dot(av, bv, trans_a, trans_b), ex, outs, ids)
        else:
            acc = rest[-1]

            @pl.when(ids[2] == 0)
            def _():
                acc[...] = jnp.zeros_like(acc)

            acc[...] += _dot(av, bv, trans_a, trans_b)

            @pl.when(ids[2] == n_k - 1)
            def _():
                epilogue(acc[...], ex, outs, ids)

    scratch = [] if n_k == 1 else [pltpu.VMEM(acc_shape, F32)]
    return pl.pallas_call(
        body, name=name, grid=grid,
        in_specs=[a_spec, b_spec, *extra_specs],
        out_specs=list(out_specs), out_shape=list(out_shape),
        scratch_shapes=scratch, compiler_params=_params(sem),
    )(a, b, *extras)


MXU_WIDTH = 256


def _col_chunks(n, width=2 * MXU_WIDTH):
    return [(c0, min(width, n - c0)) for c0 in range(0, n, width)]


def _half_bf16(v):
    return (0.5 * v).astype(BF16)


def _to_bf16(v):
    return v.astype(BF16)


def _rmsnorm_fwd(name, x, g, tokens=()):
    t, d = x.shape
    tm = min(ROW_TILE, t)

    def body(x_ref, g_ref, *rest):
        o_ref = rest[-1]
        xv = x_ref[...]
        r = lax.rsqrt(jnp.mean(xv * xv, axis=-1, keepdims=True) + EPS)
        o_ref[...] = (xv * r * g_ref[...]).astype(BF16)

    return pl.pallas_call(
        body, name=name, grid=(t // tm,),
        in_specs=[pl.BlockSpec((tm, d), lambda i: (i, 0)), pl.BlockSpec((1, d), lambda i: (0, 0))]
        + [pl.BlockSpec((8, LANES), lambda i: (0, 0))] * len(tokens),
        out_specs=pl.BlockSpec((tm, d), lambda i: (i, 0)),
        out_shape=jax.ShapeDtypeStruct((t, d), BF16),
        compiler_params=_params(("parallel",)),
    )(x, g, *tokens)


def _rms_bwd_epilogue(acc, ex, outs, ids):
    x_ref, g_ref, dres_ref = ex[:3]
    out_ref, dg_ref = outs
    xv = x_ref[...]
    r = lax.rsqrt(jnp.mean(xv * xv, axis=-1, keepdims=True) + EPS)
    w = acc * g_ref[...]
    dx = r * w - xv * (r * r * r) * jnp.mean(xv * w, axis=-1, keepdims=True)
    out_ref[...] = dres_ref[...] + dx
    part = jnp.sum(acc * (xv * r), axis=0, keepdims=True)

    @pl.when(ids[0] == 0)
    def _():
        dg_ref[...] = part

    @pl.when(ids[0] > 0)
    def _():
        dg_ref[...] += part


def _ffn_in(name, n, w_in4, tokens=()):
    t, d = n.shape
    tm = min(ROW_TILE, t)

    def body(n_ref, wa_ref, wb_ref, *rest):
        ab_ref, h_ref = rest[-2:]
        nv = n_ref[...]
        for c0, cw in _col_chunks(SHARD_W):
            cols = pl.ds(c0, cw)
            a = _dot(nv, wa_ref[:, cols])
            b = _dot(nv, wb_ref[:, cols])
            h_ref[:, cols] = (a * _sigmoid(a) * b).astype(BF16)
            ab_ref[0, :, cols] = a.astype(BF16)
            ab_ref[1, :, cols] = b.astype(BF16)

    return pl.pallas_call(
        body, name=name, grid=(2, t // tm),
        in_specs=[pl.BlockSpec((tm, d), lambda j, i: (i, 0)),
                  pl.BlockSpec((None, d, SHARD_W), lambda j, i: (j, 0, 0)),
                  pl.BlockSpec((None, d, SHARD_W), lambda j, i: (j + 2, 0, 0))]
        + [pl.BlockSpec((8, LANES), lambda j, i: (0, 0))] * len(tokens),
        out_specs=[pl.BlockSpec((2, tm, SHARD_W), lambda j, i: (0, i, j)),
                   pl.BlockSpec((tm, SHARD_W), lambda j, i: (i, j))],
        out_shape=[jax.ShapeDtypeStruct((2, t, D_FF), BF16), jax.ShapeDtypeStruct((t, D_FF), BF16)],
        compiler_params=_params(("parallel", "parallel")),
    )(n, w_in4, w_in4, *tokens)


def _mm_residual(name, a, w, res, scale, next_gain=None, loss_target=None):
    t, k = a.shape
    n = w.shape[1]
    tm = min(ROW_TILE, t)
    row = pl.BlockSpec((tm, n), lambda i, j, kk: (i, 0))
    extras, specs = [res], [row]
    shapes, out_specs = [jax.ShapeDtypeStruct((t, n), F32)], [row]
    if next_gain is not None:
        extras.append(next_gain)
        specs.append(pl.BlockSpec((1, n), lambda i, j, kk: (0, 0)))
        shapes.append(jax.ShapeDtypeStruct((t, n), BF16))
        out_specs.append(row)
    if loss_target is not None:
        extras.append(loss_target)
        specs.append(row)
        shapes.append(jax.ShapeDtypeStruct((8, LANES), F32))
        out_specs.append(pl.BlockSpec((8, LANES), lambda i, j, kk: (0, 0)))

    def epilogue(acc, ex, outs, ids):
        y = ex[0][...] + scale * acc
        if loss_target is None:
            outs[0][...] = y
        if next_gain is not None:
            r = lax.rsqrt(jnp.mean(y * y, axis=-1, keepdims=True) + EPS)
            outs[1][...] = (y * r * ex[1][...]).astype(BF16)
        if loss_target is not None:
            diff = y - ex[1][...]
            outs[0][...] = diff * (1.0 / n)
            part = jnp.full((8, LANES), 0.5 / n * jnp.sum(diff * diff), F32)

            @pl.when(ids[0] == 0)
            def _():
                outs[1][...] = part

            @pl.when(ids[0] > 0)
            def _():
                outs[1][...] += part

    sem = ("parallel" if loss_target is None else "arbitrary", "parallel", "arbitrary")
    out = _mm(name, (t // tm, 1, 1), a, pl.BlockSpec((tm, k), lambda i, j, kk: (i, 0)),
              w, pl.BlockSpec((k, n), lambda i, j, kk: (0, 0)), (tm, n),
              extras=extras, extra_specs=specs, out_shape=shapes, out_specs=out_specs, epilogue=epilogue, sem=sem)
    return out[0] if len(out) == 1 else tuple(out)


def _ffn_fwd(tag, x, n, w_in4, w_out, tokens=(), **tail):
    ab, h = _ffn_in(tag + "_in", n, w_in4, tokens)
    y = _mm_residual(tag + "_out", h, w_out, x, 0.5, **tail)
    return y, (n, ab, h)


def _ffn_bwd(tag, dres, x, g, saved, w_in4, w_out, tokens=(), on_weight_grads=None):
    n, ab, h = saved
    t, d = x.shape
    tm = min(ROW_TILE, t)
    tk = min(TOKEN_TILE, t)
    half_w = SHARD_W

    def dact_epilogue(acc, ex, outs, ids, cols):
        a = ex[0][0, :, cols].astype(F32)
        b = ex[0][1, :, cols].astype(F32)
        sig = _sigmoid(a)
        outs[0][0, :, cols] = (acc * b * (sig * (1.0 + a * (1.0 - sig)))).astype(BF16)
        outs[0][1, :, cols] = (acc * (a * sig)).astype(BF16)

    du = _mm(tag + "_dact", (2, t // tm, 1),
             dres, pl.BlockSpec((tm, d), lambda j, i, kk: (i, 0)),
             w_out, pl.BlockSpec((half_w, d), lambda j, i, kk: (j, 0)), (tm, half_w),
             trans_b=True, a_pre=_half_bf16,
             extras=(ab,), extra_specs=(pl.BlockSpec((2, tm, half_w), lambda j, i, kk: (0, i, j)),), tokens=tokens,
             out_shape=(jax.ShapeDtypeStruct((2, t, D_FF), BF16),),
             out_specs=(pl.BlockSpec((2, tm, half_w), lambda j, i, kk: (0, i, j)),),
             epilogue=dact_epilogue, chunked=True)[0]

    def store_epilogue(acc, ex, outs, ids):
        outs[0][...] = acc.astype(BF16)

    dw_out = _mm(tag + "_dwout", (2, 1, t // tk),
                 h, pl.BlockSpec((tk, half_w), lambda i, j, kk: (kk, i)),
                 dres, pl.BlockSpec((tk, d), lambda i, j, kk: (kk, 0)), (half_w, d),
                 trans_a=True, b_pre=_half_bf16,
                 out_shape=(jax.ShapeDtypeStruct((D_FF, d), BF16),),
                 out_specs=(pl.BlockSpec((half_w, d), lambda i, j, kk: (i, 0)),),
                 epilogue=store_epilogue)[0]

    dw_in4 = _mm(tag + "_dwin", (1, N_CHIPS, t // tk),
                 n, pl.BlockSpec((tk, d), lambda i, j, kk: (kk, 0)),
                 du, pl.BlockSpec((None, tk, SHARD_W), lambda i, j, kk: (j // 2, kk, j % 2)), (d, SHARD_W),
                 trans_a=True,
                 out_shape=(jax.ShapeDtypeStruct((N_CHIPS, d, SHARD_W), BF16),),
                 out_specs=(pl.BlockSpec((None, d, SHARD_W), lambda i, j, kk: (j, 0, 0)),),
                 epilogue=store_epilogue)[0]

    late = () if on_weight_grads is None else on_weight_grads(dw_in4, dw_out)

    dx, dg = _mm(tag + "_dn", (t // tm, 1, N_CHIPS),
                 du, pl.BlockSpec((None, tm, SHARD_W), lambda i, j, kk: (kk // 2, i, kk % 2)),
                 w_in4, pl.BlockSpec((None, d, SHARD_W), lambda i, j, kk: (kk, 0, 0)), (tm, d),
                 trans_b=True,
                 extras=(x, g, dres),
                 extra_specs=(pl.BlockSpec((tm, d), lambda i, j, kk: (i, 0)),
                              pl.BlockSpec((1, d), lambda i, j, kk: (0, 0)),
                              pl.BlockSpec((tm, d), lambda i, j, kk: (i, 0))), tokens=late,
                 out_shape=(jax.ShapeDtypeStruct((t, d), F32), jax.ShapeDtypeStruct((1, d), F32)),
                 out_specs=(pl.BlockSpec((tm, d), lambda i, j, kk: (i, 0)),
                            pl.BlockSpec((1, d), lambda i, j, kk: (0, 0))),
                 epilogue=_rms_bwd_epilogue, sem=("arbitrary", "arbitrary", "arbitrary"))
    return dx, dw_in4, dw_out, dg


def _conv_fill(zp_ref, a_ref, g_ref, ah_ref, gh_ref, i):
    zh = ah_ref[...].astype(F32) * _sigmoid(gh_ref[...].astype(F32))
    zp_ref[pl.ds(0, CONV_PAD), :] = jnp.where(i > 0, zh, 0.0)
    zp_ref[pl.ds(CONV_PAD, a_ref.shape[0]), :] = a_ref[...].astype(F32) * _sigmoid(g_ref[...].astype(F32))


def _shift_groups(shifts):
    groups = {}
    for j, s in shifts:
        groups.setdefault(s % 8, []).append((j, s // 8))
    return groups


def _windows(zp_ref, r0, lanes, groups):
    for q, taps in groups.items():
        deepest = max(p for _, p in taps)
        win = zp_ref[pl.ds(r0 + q, 8 * deepest + CONV_ROWS), lanes]
        for j, p in taps:
            yield j, win[8 * p:8 * p + CONV_ROWS]


def _conv_apply(zp_ref, out_ref, dw_ref, bias_ref, tm, ch, shifts):
    groups = _shift_groups(shifts)
    for cc in range(ch // LANES):
        lanes = pl.ds(cc * LANES, LANES)
        w = [dw_ref[pl.ds(j, 1), lanes] for j in range(CONV_WIDTH)]
        for r0 in range(0, tm, CONV_ROWS):
            if bias_ref is None:
                acc = jnp.zeros((CONV_ROWS, LANES), F32)
            else:
                acc = jnp.broadcast_to(bias_ref[:, lanes], (CONV_ROWS, LANES))
            for j, rows in _windows(zp_ref, r0, lanes, groups):
                acc = acc + w[j] * rows
            out_ref[pl.ds(r0, CONV_ROWS), lanes] = acc


FWD_SHIFTS = [(j, CONV_PAD - (CONV_WIDTH - 1) + j) for j in range(CONV_WIDTH)]
BWD_SHIFTS = [(j, CONV_WIDTH - 1 - j) for j in range(CONV_WIDTH)]


def _conv_taps(zp_ref, z1_ref, dw_ref, bias_ref, tm, ch):
    _conv_apply(zp_ref, z1_ref, dw_ref, bias_ref, tm, ch, FWD_SHIFTS)


def _conv_specs(tm, ch):
    per = tm // CONV_PAD
    cb = COL_CONV_G // ch
    return [pl.BlockSpec((tm, ch), lambda i: (i, 0)),
            pl.BlockSpec((tm, ch), lambda i: (i, cb)),
            pl.BlockSpec((CONV_PAD, ch), lambda i: (jnp.maximum(i * per - 1, 0), 0)),
            pl.BlockSpec((CONV_PAD, ch), lambda i: (jnp.maximum(i * per - 1, 0), cb))]


def _conv_fwd(p, dw, bias, ln_g, ln_b):
    t = p.shape[0]
    ch = D_MODEL
    tm = min(CONV_TILE, t)

    def body(a_ref, g_ref, ah_ref, gh_ref, dw_ref, bias_ref, lg_ref, lb_ref, o_ref, z1_ref, zp_ref):
        i = pl.program_id(0)
        _conv_fill(zp_ref, a_ref, g_ref, ah_ref, gh_ref, i)
        _conv_taps(zp_ref, z1_ref, dw_ref, bias_ref, tm, ch)
        z1 = z1_ref[...]
        mu = jnp.mean(z1, axis=-1, keepdims=True)
        zc = z1 - mu
        rs = lax.rsqrt(jnp.mean(zc * zc, axis=-1, keepdims=True) + EPS)
        z2 = zc * rs * lg_ref[...] + lb_ref[...]
        o_ref[...] = (z2 * _sigmoid(z2)).astype(BF16)

    vec = pl.BlockSpec((1, ch), lambda i: (0, 0))
    return pl.pallas_call(
        body, name="conv_fwd", grid=(t // tm,),
        in_specs=_conv_specs(tm, ch) + [pl.BlockSpec((CONV_PAD, ch), lambda i: (0, 0)), vec, vec, vec],
        out_specs=[pl.BlockSpec((tm, ch), lambda i: (i, 0)), pl.BlockSpec((tm, ch), lambda i: (i, 0))],
        out_shape=[jax.ShapeDtypeStruct((t, ch), BF16), jax.ShapeDtypeStruct((t, ch), F32)],
        scratch_shapes=[pltpu.VMEM((CONV_PAD + tm, ch), F32)],
        compiler_params=_params(("parallel",)),
    )(p, p, p, p, dw, bias, ln_g, ln_b)


def _conv_bwd_ln(p, z1_saved, dz3, ln_g, ln_b):
    t = p.shape[0]
    ch = D_MODEL
    tm = min(CONV_TILE, t)

    def body(a_ref, g_ref, ah_ref, gh_ref, z1_ref, dz3_ref, lg_ref, lb_ref,
             dz1_ref, ddw_ref, dbias_ref, dlg_ref, dlb_ref, zp_ref):
        i = pl.program_id(0)
        _conv_fill(zp_ref, a_ref, g_ref, ah_ref, gh_ref, i)
        z1 = z1_ref[...]
        mu = jnp.mean(z1, axis=-1, keepdims=True)
        zc = z1 - mu
        rs = lax.rsqrt(jnp.mean(zc * zc, axis=-1, keepdims=True) + EPS)
        xh = zc * rs
        z2 = xh * lg_ref[...] + lb_ref[...]
        sig = _sigmoid(z2)
        dz2 = dz3_ref[...].astype(F32) * (sig * (1.0 + z2 * (1.0 - sig)))
        dxh = dz2 * lg_ref[...]
        dz1 = rs * (dxh - jnp.mean(dxh, axis=-1, keepdims=True) - xh * jnp.mean(dxh * xh, axis=-1, keepdims=True))
        dz1_ref[...] = dz1

        @pl.when(i == 0)
        def _():
            ddw_ref[...] = jnp.zeros_like(ddw_ref)
            dbias_ref[...] = jnp.zeros_like(dbias_ref)
            dlg_ref[...] = jnp.zeros_like(dlg_ref)
            dlb_ref[...] = jnp.zeros_like(dlb_ref)

        dlg_ref[...] += jnp.sum(dz2 * xh, axis=0, keepdims=True)
        dlb_ref[...] += jnp.sum(dz2, axis=0, keepdims=True)
        dbias_ref[...] += jnp.sum(dz1, axis=0, keepdims=True)
        groups = _shift_groups(FWD_SHIFTS)
        for cc in range(ch // LANES):
            lanes = pl.ds(cc * LANES, LANES)
            accs = [jnp.zeros((8, LANES), F32) for _ in range(CONV_WIDTH)]
            for r0 in range(0, tm, CONV_ROWS):
                dzc = dz1_ref[pl.ds(r0, CONV_ROWS), lanes]
                for j, rows in _windows(zp_ref, r0, lanes, groups):
                    accs[j] = accs[j] + jnp.sum((dzc * rows).reshape(CONV_ROWS // 8, 8, LANES), axis=0)
            for j in range(CONV_WIDTH):
                ddw_ref[pl.ds(j, 1), lanes] += jnp.sum(accs[j], axis=0, keepdims=True)

    vec = pl.BlockSpec((1, ch), lambda i: (0, 0))
    return pl.pallas_call(
        body, name="conv_bwd_ln", grid=(t // tm,),
        in_specs=_conv_specs(tm, ch) + [pl.BlockSpec((tm, ch), lambda i: (i, 0)),
                                        pl.BlockSpec((tm, ch), lambda i: (i, 0)), vec, vec],
        out_specs=[pl.BlockSpec((tm, ch), lambda i: (i, 0)), pl.BlockSpec((CONV_PAD, ch), lambda i: (0, 0)), vec, vec, vec],
        out_shape=[jax.ShapeDtypeStruct((t, ch), F32), jax.ShapeDtypeStruct((CONV_PAD, ch), F32)]
        + [jax.ShapeDtypeStruct((1, ch), F32)] * 3,
        scratch_shapes=[pltpu.VMEM((CONV_PAD + tm, ch), F32)],
        compiler_params=_params(("arbitrary",)),
    )(p, p, p, p, z1_saved, dz3, ln_g, ln_b)


def _conv_bwd_glu(p, dz1, dw, dq, dkv, dgates):
    t = p.shape[0]
    ch = D_MODEL
    tm = min(CONV_TILE, t)
    per = tm // CONV_PAD
    n_halo = t // CONV_PAD
    cb = COL_CONV_G // ch

    def body(a_ref, g_ref, dz_ref, dzn_ref, dw_ref, dq_ref, dkv_ref, dgates_ref, o_ref, zp_ref, z0_ref):
        i = pl.program_id(0)
        o_ref[:, pl.ds(COL_Q, Q_W)] = dq_ref[...]
        o_ref[:, pl.ds(COL_K, 2 * KV_W)] = dkv_ref[...]
        o_ref[:, pl.ds(COL_GC, ch)] = dgates_ref[0]
        o_ref[:, pl.ds(COL_GA, ch)] = dgates_ref[1]
        zp_ref[pl.ds(0, tm), :] = dz_ref[...]
        zp_ref[pl.ds(tm, CONV_PAD), :] = jnp.where(i < t // tm - 1, dzn_ref[...], 0.0)
        _conv_apply(zp_ref, z0_ref, dw_ref, None, tm, ch, BWD_SHIFTS)
        dz0 = z0_ref[...]
        a = a_ref[...].astype(F32)
        sig = _sigmoid(g_ref[...].astype(F32))
        o_ref[:, pl.ds(0, ch)] = (dz0 * sig).astype(BF16)
        o_ref[:, pl.ds(ch, ch)] = (dz0 * a * sig * (1.0 - sig)).astype(BF16)

    return pl.pallas_call(
        body, name="conv_bwd_glu", grid=(t // tm,),
        in_specs=[pl.BlockSpec((tm, ch), lambda i: (i, 0)), pl.BlockSpec((tm, ch), lambda i: (i, cb)),
                  pl.BlockSpec((tm, ch), lambda i: (i, 0)),
                  pl.BlockSpec((CONV_PAD, ch), lambda i: (jnp.minimum((i + 1) * per, n_halo - 1), 0)),
                  pl.BlockSpec((CONV_PAD, ch), lambda i: (0, 0)),
                  pl.BlockSpec((tm, Q_W), lambda i: (i, 0)), pl.BlockSpec((tm, 2 * KV_W), lambda i: (i, 0)),
                  pl.BlockSpec((2, tm, ch), lambda i: (0, i, 0))],
        out_specs=pl.BlockSpec((tm, IN_W), lambda i: (i, 0)),
        out_shape=jax.ShapeDtypeStruct((t, IN_W), BF16),
        scratch_shapes=[pltpu.VMEM((tm + CONV_PAD, ch), F32), pltpu.VMEM((tm, ch), F32)],
        compiler_params=_params(("parallel",)),
    )(p, p, dz1, dz1, dw, dq, dkv, dgates)


def _bucket_onehot():
    qi = jnp.arange(BLOCK, dtype=jnp.int32)[:, None]
    kj = jnp.arange(2 * BLOCK, dtype=jnp.int32)[None, :]
    dist = jnp.maximum(qi + BLOCK - kj, 0)
    max_exact = N_BUCKETS // 2
    dflt = jnp.maximum(dist, 1).astype(F32)
    large = max_exact + (jnp.log(dflt / max_exact) / math.log(MAX_DISTANCE / max_exact)
                         * (N_BUCKETS - max_exact)).astype(jnp.int32)
    large = jnp.minimum(large, N_BUCKETS - 1)
    bucket = jnp.where(dist < max_exact, dist, large)
    onehot = bucket[None] == jnp.arange(N_BUCKETS, dtype=jnp.int32)[:, None, None]
    return onehot.astype(F32).reshape(N_BUCKETS, BLOCK * 2 * BLOCK)


def _bias_table(rel_bias_t, onehot):
    n = onehot.shape[1]
    tn = 4096

    def body(r_ref, oh_ref, o_ref):
        flat = pl.program_id(0) * tn + lax.broadcasted_iota(jnp.int32, (N_Q_HEADS, tn), 1)
        dist = (flat // (2 * BLOCK)) + BLOCK - (flat % (2 * BLOCK))
        bias = _dot(r_ref[...], oh_ref[...], precision=lax.Precision.HIGHEST)
        o_ref[...] = jnp.where((dist >= 0) & (dist < BLOCK), bias, NEG)

    return pl.pallas_call(
        body, name="bias_table", grid=(n // tn,),
        in_specs=[pl.BlockSpec((N_Q_HEADS, N_BUCKETS), lambda i: (0, 0)), pl.BlockSpec((N_BUCKETS, tn), lambda i: (0, i))],
        out_specs=pl.BlockSpec((N_Q_HEADS, tn), lambda i: (0, i)),
        out_shape=jax.ShapeDtypeStruct((N_Q_HEADS, n), F32),
        compiler_params=_params(("parallel",)),
    )(rel_bias_t, onehot)


def _bias_table_bwd(dbias, onehot):
    n = onehot.shape[1]
    tn = 4096

    def body(d_ref, oh_ref, o_ref):
        part = _dot(d_ref[...], oh_ref[...], trans_b=True, precision=lax.Precision.HIGHEST)
        i = pl.program_id(0)

        @pl.when(i == 0)
        def _():
            o_ref[...] = part

        @pl.when(i > 0)
        def _():
            o_ref[...] += part

    return pl.pallas_call(
        body, name="bias_table_bwd", grid=(n // tn,),
        in_specs=[pl.BlockSpec((N_Q_HEADS, tn), lambda i: (0, i)), pl.BlockSpec((N_BUCKETS, tn), lambda i: (0, i))],
        out_specs=pl.BlockSpec((N_Q_HEADS, N_BUCKETS), lambda i: (0, 0)),
        out_shape=jax.ShapeDtypeStruct((N_Q_HEADS, N_BUCKETS), F32),
        compiler_params=_params(("arbitrary",)),
    )(dbias, onehot)


def _lane_head(rows):
    return lax.broadcasted_iota(jnp.int32, (rows, KV_W), 1) // HEAD_DIM


def _group_rms(x, gain_wide):
    head = _lane_head(x.shape[0])
    sq = x * x
    r = jnp.zeros_like(x)
    for i in range(N_KV_HEADS):
        ms = jnp.sum(jnp.where(head == i, sq, 0.0), axis=-1, keepdims=True) * (1.0 / HEAD_DIM)
        r = jnp.where(head == i, lax.rsqrt(ms + EPS), r)
    return r, x * r * gain_wide


def _stack_heads(group):
    head = _lane_head(group.shape[0])
    return jnp.concatenate([jnp.where(head == i, group, jnp.zeros_like(group)) for i in range(N_KV_HEADS)], axis=0)


def _unstack_heads(stacked):
    head = _lane_head(BLOCK)
    out = jnp.where(head == 0, stacked[:BLOCK], 0.0)
    for i in range(1, N_KV_HEADS):
        out = out + jnp.where(head == i, stacked[i * BLOCK:(i + 1) * BLOCK], 0.0)
    return out


def _repeaters():
    row = lax.broadcasted_iota(jnp.int32, (KV_W, KV_W), 0)
    col = lax.broadcasted_iota(jnp.int32, (KV_W, KV_W), 1)
    return [(row == h * HEAD_DIM + col % HEAD_DIM).astype(BF16) for h in range(N_KV_HEADS)]


def _attn_probs(q_stack, k_rep, sink, bias, before_start):
    s = _dot(q_stack, k_rep, trans_b=True) * (1.0 / math.sqrt(HEAD_DIM)) + bias
    s = jnp.where(before_start, NEG, s)
    m = jnp.maximum(jnp.max(s, axis=-1, keepdims=True), sink)
    p = jnp.exp(s - m)
    es = jnp.exp(sink - m)
    inv = 1.0 / (jnp.sum(p, axis=-1, keepdims=True) + es)
    return p * inv, es * inv


def _before_start(n):
    col = lax.broadcasted_iota(jnp.int32, (QROWS, 2 * BLOCK), 1)
    return (col < BLOCK) & (n == 0)


KV_W = N_KV_HEADS * HEAD_DIM
Q_W = N_Q_HEADS * HEAD_DIM


def _attn_specs():
    qspec = pl.BlockSpec((BLOCK, Q_W), lambda n: (n, COL_Q // Q_W))
    kprev = pl.BlockSpec((BLOCK, KV_W), lambda n: (jnp.maximum(n - 1, 0), COL_K // KV_W))
    kcur = pl.BlockSpec((BLOCK, KV_W), lambda n: (n, COL_K // KV_W))
    vprev = pl.BlockSpec((BLOCK, KV_W), lambda n: (jnp.maximum(n - 1, 0), COL_V // KV_W))
    vcur = pl.BlockSpec((BLOCK, KV_W), lambda n: (n, COL_V // KV_W))
    gain = pl.BlockSpec((1, KV_W), lambda n: (0, 0))
    sink = pl.BlockSpec((N_KV_HEADS, QROWS, 1), lambda n: (0, 0, 0))
    bias = pl.BlockSpec((N_KV_HEADS, QROWS, 2 * BLOCK), lambda n: (0, 0, 0))
    return [qspec, kprev, kcur, vprev, vcur], gain, sink, bias


def _attn_fwd(p, gq, gk, sink_rows, bias):
    t = p.shape[0]
    nb = t // BLOCK
    qkv, gain, sink, bspec = _attn_specs()

    def body(q_ref, kp_ref, kc_ref, vp_ref, vc_ref, gq_ref, gk_ref, sink_ref, bias_ref, o_ref):
        before_start = _before_start(pl.program_id(0))
        rep = _repeaters()
        kf = jnp.concatenate([kp_ref[...], kc_ref[...]], axis=0).astype(F32)
        kn = _group_rms(kf, gk_ref[...])[1].astype(BF16)
        v = jnp.concatenate([vp_ref[...], vc_ref[...]], axis=0)
        for h in range(N_KV_HEADS):
            qn = _group_rms(q_ref[:, pl.ds(h * KV_W, KV_W)].astype(F32), gq_ref[...])[1]
            k_rep = _dot(kn, rep[h]).astype(BF16)
            pn = _attn_probs(_stack_heads(qn).astype(BF16), k_rep, sink_ref[h], bias_ref[h], before_start)[0]
            v_rep = _dot(v, rep[h]).astype(BF16)
            o_ref[:, pl.ds(h * KV_W, KV_W)] = _unstack_heads(_dot(pn.astype(BF16), v_rep)).astype(BF16)

    return pl.pallas_call(
        body, name="attn_fwd", grid=(nb,),
        in_specs=qkv + [gain, gain, sink, bspec],
        out_specs=pl.BlockSpec((BLOCK, Q_W), lambda n: (n, 0)), out_shape=jax.ShapeDtypeStruct((t, Q_W), BF16),
        compiler_params=_params(("parallel",)),
    )(p, p, p, p, p, gq, gk, sink_rows, bias)


def _attn_bwd(p, do, gq, gk, sink_rows, bias):
    t = p.shape[0]
    nb = t // BLOCK
    qkv, _, sink, bspec = _attn_specs()
    gain = pl.BlockSpec((1, HEAD_DIM), lambda n: (0, 0))
    scale = 1.0 / math.sqrt(HEAD_DIM)

    def head_selectors():
        row = lax.broadcasted_iota(jnp.int32, (KV_W, HEAD_DIM), 0)
        col = lax.broadcasted_iota(jnp.int32, (KV_W, HEAD_DIM), 1)
        return [(row == col + i * HEAD_DIM).astype(BF16) for i in range(N_KV_HEADS)]

    def take_heads(group, sel):
        return jnp.concatenate([_dot(group, s) for s in sel], axis=0)

    def put_heads(x, sel):
        rows = x.shape[0] // len(sel)
        out = _dot(x[:rows].astype(BF16), sel[0], trans_b=True)
        for i in range(1, len(sel)):
            out = out + _dot(x[i * rows:(i + 1) * rows].astype(BF16), sel[i], trans_b=True)
        return out

    def rms(x, g):
        r = lax.rsqrt(jnp.mean(x * x, axis=-1, keepdims=True) + EPS)
        return r, x * r * g

    def rms_bwd(dn, xf, r, g):
        w = dn * g
        dx = r * w - xf * (r * r * r) * jnp.mean(xf * w, axis=-1, keepdims=True)
        return dx, jnp.sum(dn * (xf * r), axis=0, keepdims=True)

    def body(q_ref, kp_ref, kc_ref, vp_ref, vc_ref, do_ref, gq_ref, gk_ref, sink_ref, bias_ref,
             dq_ref, dkv_ref, dbias_ref, dsink_ref, dgq_ref, dgk_ref):
        n = pl.program_id(0)
        sel = head_selectors()

        @pl.when(n == 0)
        def _():
            dbias_ref[...] = jnp.zeros_like(dbias_ref)
            dsink_ref[...] = jnp.zeros_like(dsink_ref)
            dgq_ref[...] = jnp.zeros_like(dgq_ref)
            dgk_ref[...] = jnp.zeros_like(dgk_ref)

        before_start = _before_start(n)
        dgq_sum = jnp.zeros((1, HEAD_DIM), F32)
        dgk_sum = jnp.zeros((1, HEAD_DIM), F32)
        dk_rows, dv_rows = [], []
        for h in range(N_KV_HEADS):
            qf = take_heads(q_ref[:, pl.ds(h * KV_W, KV_W)], sel)
            rq, qn = rms(qf, gq_ref[...])
            kf = jnp.concatenate([_dot(kp_ref[...], sel[h]), _dot(kc_ref[...], sel[h])], axis=0)
            rk, kn = rms(kf, gk_ref[...])
            pn, psink = _attn_probs(qn.astype(BF16), kn.astype(BF16), sink_ref[h], bias_ref[h], before_start)
            do = take_heads(do_ref[:, pl.ds(h * KV_W, KV_W)], sel).astype(BF16)
            v = jnp.concatenate([_dot(vp_ref[...], sel[h]), _dot(vc_ref[...], sel[h])], axis=0).astype(BF16)
            dv_win = _dot(do, pn.astype(BF16), trans_a=True).T
            dp = _dot(do, v, trans_b=True)
            delta = jnp.sum(pn * dp, axis=-1, keepdims=True)
            ds = pn * (dp - delta)
            dsc = (ds * scale).astype(BF16)
            dqn = _dot(dsc, kn.astype(BF16))
            dkn = _dot(qn.astype(BF16), dsc, trans_a=True).T
            dq, dgq = rms_bwd(dqn, qf, rq, gq_ref[...])
            dk_win, dgk = rms_bwd(dkn, kf, rk, gk_ref[...])
            dq_ref[:, pl.ds(h * KV_W, KV_W)] = put_heads(dq, sel).astype(BF16)
            dk_rows += [dk_win[:BLOCK], dk_win[BLOCK:]]
            dv_rows += [dv_win[:BLOCK], dv_win[BLOCK:]]
            dbias_ref[h] += ds
            dsink_ref[h] += jnp.sum((-psink * delta).reshape(GROUP, BLOCK, 1), axis=1)
            dgq_sum = dgq_sum + dgq
            dgk_sum = dgk_sum + dgk
        for part in range(2):
            dkv_ref[part, :, pl.ds(0, KV_W)] = put_heads(jnp.concatenate(dk_rows[part::2], axis=0), sel).astype(BF16)
            dkv_ref[part, :, pl.ds(KV_W, KV_W)] = put_heads(jnp.concatenate(dv_rows[part::2], axis=0), sel).astype(BF16)
        dgq_ref[...] += dgq_sum
        dgk_ref[...] += dgk_sum

    row = pl.BlockSpec((BLOCK, Q_W), lambda n: (n, 0))
    return pl.pallas_call(
        body, name="attn_bwd", grid=(nb,),
        in_specs=qkv + [row, gain, gain, sink, bspec],
        out_specs=[row, pl.BlockSpec((None, 2, BLOCK, 2 * KV_W), lambda n: (n, 0, 0, 0)), bspec,
                   pl.BlockSpec((N_KV_HEADS, GROUP, 1), lambda n: (0, 0, 0)), gain, gain],
        out_shape=[jax.ShapeDtypeStruct((t, Q_W), BF16),
                   jax.ShapeDtypeStruct((nb, 2, BLOCK, 2 * KV_W), BF16),
                   jax.ShapeDtypeStruct((N_KV_HEADS, QROWS, 2 * BLOCK), F32),
                   jax.ShapeDtypeStruct((N_KV_HEADS, GROUP, 1), F32),
                   jax.ShapeDtypeStruct((1, HEAD_DIM), F32),
                   jax.ShapeDtypeStruct((1, HEAD_DIM), F32)],
        compiler_params=_params(("arbitrary",)),
    )(p, p, p, p, p, do, gq, gk, sink_rows, bias)


def _kv_window_sum(parts):
    nb = parts.shape[0]

    def body(cur_ref, nxt_ref, o_ref):
        nxt = jnp.where(pl.program_id(0) < nb - 1, nxt_ref[...].astype(F32), 0.0)
        o_ref[...] = (cur_ref[...].astype(F32) + nxt).astype(BF16)

    blk = (None, None, BLOCK, 2 * KV_W)
    return pl.pallas_call(
        body, name="kv_window_sum", grid=(nb,),
        in_specs=[pl.BlockSpec(blk, lambda n: (n, 1, 0, 0)),
                  pl.BlockSpec(blk, lambda n: (jnp.minimum(n + 1, nb - 1), 0, 0, 0))],
        out_specs=pl.BlockSpec((BLOCK, 2 * KV_W), lambda n: (n, 0)),
        out_shape=jax.ShapeDtypeStruct((nb * BLOCK, 2 * KV_W), BF16),
        compiler_params=_params(("parallel",)),
    )(parts, parts)


GATE_TILE = 512


def _merge_fwd(z3, o, p, w_proj, w_o):
    t, d = z3.shape
    tm = min(ROW_TILE, t)
    tn = GATE_TILE

    def body(z_ref, o_ref, gc_ref, ga_ref, wp_ref, wo_ref, m_ref, a_ref, b_ref):
        a = _dot(z_ref[...], wp_ref[...])
        b = _dot(o_ref[...], wo_ref[...])
        m_ref[...] = (_sigmoid(gc_ref[...].astype(F32)) * a + _sigmoid(ga_ref[...].astype(F32)) * b).astype(BF16)
        a_ref[...] = a.astype(BF16)
        b_ref[...] = b.astype(BF16)

    row = pl.BlockSpec((tm, d), lambda i, j: (i, 0))
    wspec = pl.BlockSpec((d, tn), lambda i, j: (0, j))
    ospec = pl.BlockSpec((tm, tn), lambda i, j: (i, j))
    return pl.pallas_call(
        body, name="merge_fwd", grid=(t // tm, d // tn),
        in_specs=[row, row,
                  pl.BlockSpec((tm, tn), lambda i, j: (i, COL_GC // tn + j)),
                  pl.BlockSpec((tm, tn), lambda i, j: (i, COL_GA // tn + j)), wspec, wspec],
        out_specs=[ospec, ospec, ospec],
        out_shape=[jax.ShapeDtypeStruct((t, d), BF16)] * 3,
        compiler_params=_params(("parallel", "parallel")),
    )(z3, o, p, p, w_proj, w_o)


def _merge_bwd(dres, w_out, a, b, p, tokens=()):
    t, d = dres.shape
    tm = min(ROW_TILE, t)
    tn = GATE_TILE

    def epilogue(acc, ex, outs, ids):
        a_ref, b_ref, gc_ref, ga_ref = ex[:4]
        sc = _sigmoid(gc_ref[...].astype(F32))
        sa = _sigmoid(ga_ref[...].astype(F32))
        outs[0][...] = (acc * sc).astype(BF16)
        outs[1][...] = (acc * sa).astype(BF16)
        outs[2][0] = (acc * a_ref[...].astype(F32) * sc * (1.0 - sc)).astype(BF16)
        outs[2][1] = (acc * b_ref[...].astype(F32) * sa * (1.0 - sa)).astype(BF16)

    ospec = pl.BlockSpec((tm, tn), lambda i, j, kk: (i, j))
    return _mm("merge_bwd", (t // tm, d // tn, 1),
               dres, pl.BlockSpec((tm, d), lambda i, j, kk: (i, 0)),
               w_out, pl.BlockSpec((tn, d), lambda i, j, kk: (j, 0)), (tm, tn),
               trans_b=True, a_pre=_to_bf16,
               extras=(a, b, p, p),
               extra_specs=(ospec, ospec,
                            pl.BlockSpec((tm, tn), lambda i, j, kk: (i, COL_GC // tn + j)),
                            pl.BlockSpec((tm, tn), lambda i, j, kk: (i, COL_GA // tn + j))), tokens=tokens,
               out_shape=(jax.ShapeDtypeStruct((t, d), BF16), jax.ShapeDtypeStruct((t, d), BF16),
                          jax.ShapeDtypeStruct((2, t, d), BF16)),
               out_specs=(ospec, ospec, pl.BlockSpec((2, tm, tn), lambda i, j, kk: (0, i, j))),
               epilogue=epilogue)


def _store_epilogue(acc, ex, outs, ids):
    outs[0][...] = acc


def _store_bf16_epilogue(acc, ex, outs, ids):
    outs[0][...] = acc.astype(BF16)


def _mm_nt(name, a, w, out_dtype=BF16):
    t, n = a.shape
    k = w.shape[0]
    tm = min(ROW_TILE, t)
    return _mm(name, (t // tm, 1, 1), a, pl.BlockSpec((tm, n), lambda i, j, kk: (i, 0)),
               w, pl.BlockSpec((k, n), lambda i, j, kk: (0, 0)), (tm, k), trans_b=True,
               out_shape=(jax.ShapeDtypeStruct((t, k), out_dtype),),
               out_specs=(pl.BlockSpec((tm, k), lambda i, j, kk: (i, 0)),),
               epilogue=_store_bf16_epilogue if out_dtype == BF16 else _store_epilogue)[0]


def _mm_tn(name, a, b, b_pre=None):
    t, m = a.shape
    n = b.shape[1]
    tk = min(TOKEN_TILE, t)
    return _mm(name, (1, 1, t // tk), a, pl.BlockSpec((tk, m), lambda i, j, kk: (kk, 0)),
               b, pl.BlockSpec((tk, n), lambda i, j, kk: (kk, 0)), (m, n), trans_a=True, b_pre=b_pre,
               out_shape=(jax.ShapeDtypeStruct((m, n), BF16),),
               out_specs=(pl.BlockSpec((m, n), lambda i, j, kk: (0, 0)),), epilogue=_store_bf16_epilogue)[0]


def _local_step(x, target, small, comm):
    t = x.shape[0]
    w = dict(small)

    n1 = _rmsnorm_fwd("ffn1_norm", x, w["ffn1_norm"])
    w.update(comm.weights("A", n1))
    (x1, hm), ffn1_saved = _ffn_fwd("ffn1", x, n1, w["ffn1_w_in"], w["ffn1_w_out"], comm.tokens,
                                    next_gain=w["mix_norm"])
    w.update(comm.weights("B", x1))
    tm = min(ROW_TILE, t)
    p = _mm("mix_in", (N_CHIPS, t // tm, 1),
            hm, pl.BlockSpec((tm, D_MODEL), lambda j, i, kk: (i, 0)),
            w["w_in"], pl.BlockSpec((None, D_MODEL, SHARD_W), lambda j, i, kk: (j, 0, 0)), (tm, SHARD_W),
            tokens=comm.tokens,
            out_shape=(jax.ShapeDtypeStruct((t, IN_W), BF16),),
            out_specs=(pl.BlockSpec((tm, SHARD_W), lambda j, i, kk: (i, j)),),
            epilogue=_store_bf16_epilogue)[0]

    z3, z1 = _conv_fwd(p, w["conv_dw_kernel"], w["conv_dw_bias"], w["conv_ln_g"], w["conv_ln_b"])

    onehot = _bucket_onehot()
    bias = _bias_table(w["rel_bias"].T, onehot).reshape(N_KV_HEADS, QROWS, 2 * BLOCK)
    sink_rows = jnp.repeat(w["attn_sinks"].reshape(N_KV_HEADS, GROUP), BLOCK, axis=1)[..., None]
    gq_wide = jnp.tile(w["q_norm"], (1, N_KV_HEADS))
    gk_wide = jnp.tile(w["k_norm"], (1, N_KV_HEADS))
    o = _attn_fwd(p, gq_wide, gk_wide, sink_rows, bias)

    merged, a, b = _merge_fwd(z3, o, p, w["conv_w_proj"], w["attn_w_o"])
    x2, n2 = _mm_residual("mix_out", merged, w["w_out"], x1, 1.0, next_gain=w["ffn2_norm"])
    w.update(comm.weights("C", n2))
    (dy, loss), ffn2_saved = _ffn_fwd("ffn2", x2, n2, w["ffn2_w_in"], w["ffn2_w_out"], loss_target=target)

    g, big = {}, {}
    dres2, big["ffn2_w_in"], big["ffn2_w_out"], g["ffn2_norm"] = _ffn_bwd(
        "ffn2b", dy, x2, w["ffn2_norm"], ffn2_saved, w["ffn2_w_in"], w["ffn2_w_out"])
    tokens = comm.reduce_start("R1", big)

    da, db, dgates = _merge_bwd(dres2, w["w_out"], a, b, p, tokens)
    big = {}
    big["w_out"] = _mm_tn("d_w_out", merged, dres2, b_pre=_to_bf16)
    big["conv_w_proj"] = _mm_tn("d_w_proj", z3, da)
    big["attn_w_o"] = _mm_tn("d_w_o", o, db)
    dz3 = _mm_nt("d_z3", da, w["conv_w_proj"])
    do = _mm_nt("d_o", db, w["attn_w_o"])

    dq, dkv_parts, dbias, dsink, g["q_norm"], g["k_norm"] = _attn_bwd(
        p, do, w["q_norm"], w["k_norm"], sink_rows, bias)
    dkv = _kv_window_sum(dkv_parts)
    g["rel_bias"] = _bias_table_bwd(dbias.reshape(N_Q_HEADS, BLOCK * 2 * BLOCK), onehot).T
    g["attn_sinks"] = dsink.reshape(N_Q_HEADS)

    dz1, big["conv_dw_kernel"], g["conv_dw_bias"], g["conv_ln_g"], g["conv_ln_b"] = _conv_bwd_ln(
        p, z1, dz3, w["conv_ln_g"], w["conv_ln_b"])
    dp = _conv_bwd_glu(p, dz1, w["conv_dw_kernel"], dq, dkv, dgates)
    tk = min(TOKEN_TILE, t)
    big["w_in"] = _mm("d_w_in", (1, N_CHIPS, t // tk),
                    hm, pl.BlockSpec((tk, D_MODEL), lambda i, j, kk: (kk, 0)),
                    dp, pl.BlockSpec((tk, SHARD_W), lambda i, j, kk: (kk, j)), (D_MODEL, SHARD_W),
                    trans_a=True,
                    out_shape=(jax.ShapeDtypeStruct((N_CHIPS, D_MODEL, SHARD_W), BF16),),
                    out_specs=(pl.BlockSpec((None, D_MODEL, SHARD_W), lambda i, j, kk: (j, 0, 0)),),
                    epilogue=_store_bf16_epilogue)[0]
    dres1, g["mix_norm"] = _mm("d_mix", (t // tm, 1, N_CHIPS),
                               dp, pl.BlockSpec((tm, SHARD_W), lambda i, j, kk: (i, kk)),
                               w["w_in"], pl.BlockSpec((None, D_MODEL, SHARD_W), lambda i, j, kk: (kk, 0, 0)),
                               (tm, D_MODEL), trans_b=True,
                               extras=(x1, w["mix_norm"], dres2),
                               extra_specs=(pl.BlockSpec((tm, D_MODEL), lambda i, j, kk: (i, 0)),
                                            pl.BlockSpec((1, D_MODEL), lambda i, j, kk: (0, 0)),
                                            pl.BlockSpec((tm, D_MODEL), lambda i, j, kk: (i, 0))),
                               out_shape=(jax.ShapeDtypeStruct((t, D_MODEL), F32), jax.ShapeDtypeStruct((1, D_MODEL), F32)),
                               out_specs=(pl.BlockSpec((tm, D_MODEL), lambda i, j, kk: (i, 0)),
                                          pl.BlockSpec((1, D_MODEL), lambda i, j, kk: (0, 0))),
                               epilogue=_rms_bwd_epilogue, sem=("arbitrary", "arbitrary", "arbitrary"))

    comm.reduce_finish("R1", dres1)
    tokens = comm.reduce_start("R2", big)

    def ffn1_grads(dw_in4, dw_out):
        comm.reduce_finish("R2", dw_in4)
        return comm.reduce_start("R3", {"ffn1_w_in": dw_in4, "ffn1_w_out": dw_out})

    grad_x, _, _, g["ffn1_norm"] = _ffn_bwd(
        "ffn1b", dres1, x, w["ffn1_norm"], ffn1_saved, w["ffn1_w_in"], w["ffn1_w_out"], tokens, ffn1_grads)
    comm.reduce_finish("R3", grad_x)
    return loss[0, 0], grad_x, g


def _mesh_place():
    x, y, c = lax.axis_index("x"), lax.axis_index("y"), lax.axis_index("c")
    chips = [(1 - x, y), (x, 1 - y), (1 - x, 1 - y)]
    return x, y, c, chips


def _any_specs(n):
    return [pl.BlockSpec(memory_space=pl.ANY)] * n


HBM_SPEC = pl.BlockSpec(memory_space=pltpu.HBM)
SEM_SPEC = pl.BlockSpec(memory_space=pltpu.SEMAPHORE)
EFFECT = pltpu.SideEffectType.DATAFLOW_SIDE_EFFECTING


def _in_hbm(a):
    return pltpu.with_memory_space_constraint(a, pltpu.HBM)


def _copy_start(name, srcs, lands, plan, after=()):
    ns, nb = len(srcs), len(lands)
    n = plan.copies_per_source * ns

    def body(*refs):
        s_refs, l_refs = refs[:ns], refs[ns:ns + nb]
        send_sems, recv_sems = refs[ns + nb + len(after)], refs[ns + nb + len(after) + 1]
        token = refs[-1]
        for k, (src, dst, to, _) in enumerate(plan(s_refs, l_refs)):
            pltpu.make_async_remote_copy(src_ref=src, dst_ref=dst, send_sem=send_sems.at[k], recv_sem=recv_sems.at[k],
                                         device_id=to, device_id_type=MESH).start()
        token[...] = jnp.zeros_like(token)

    bufs = list(srcs) + list(lands)
    outs = pl.pallas_call(
        body, name=name,
        out_shape=(pltpu.SemaphoreType.DMA((n,)), pltpu.SemaphoreType.DMA((n,)),
                   *[pltpu.HBM(a.shape, a.dtype) for a in bufs], jax.ShapeDtypeStruct((8, LANES), F32)),
        in_specs=[HBM_SPEC] * len(bufs) + [pl.BlockSpec(memory_space=pl.ANY)] * len(after),
        out_specs=(SEM_SPEC, SEM_SPEC, *[HBM_SPEC] * len(bufs), pl.BlockSpec(memory_space=pltpu.VMEM)),
        input_output_aliases={i: 2 + i for i in range(len(bufs))},
        compiler_params=pltpu.CompilerParams(has_side_effects=EFFECT),
    )(*[_in_hbm(a) for a in bufs], *after)
    return outs[0], outs[1], list(outs[2:2 + ns]), list(outs[2 + ns:2 + ns + nb]), outs[-1]


def _copy_wait(name, send_sems, recv_sems, srcs, lands, after, plan):
    ns, nb = len(srcs), len(lands)

    def body(*refs):
        s_refs, l_refs = refs[:ns], refs[ns:ns + nb]
        send_sems, recv_sems = refs[ns + nb], refs[ns + nb + 1]
        for k, (src, _, to, mine) in enumerate(plan(s_refs, l_refs)):
            cp = pltpu.make_async_remote_copy(src_ref=src, dst_ref=mine, send_sem=send_sems.at[k], recv_sem=recv_sems.at[k],
                                              device_id=to, device_id_type=MESH)
            cp.wait_send()
            cp.wait_recv()

    bufs = list(srcs) + list(lands)
    outs = pl.pallas_call(
        body, name=name,
        out_shape=tuple(pltpu.HBM(a.shape, a.dtype) for a in bufs),
        in_specs=[HBM_SPEC] * len(bufs) + [SEM_SPEC, SEM_SPEC, pl.BlockSpec(memory_space=pl.ANY)],
        out_specs=tuple([HBM_SPEC] * len(bufs)),
        input_output_aliases={i: i for i in range(len(bufs))},
        compiler_params=pltpu.CompilerParams(has_side_effects=EFFECT),
    )(*bufs, send_sems, recv_sems, after)
    return list(outs[:ns]), list(outs[ns:])


def _gather_plan(s_refs, l_refs):
    x, y, c, chips = _mesh_place()
    jme = 2 * x + y
    return [(s.at[c], land.at[jme, c], (*chip, c), land.at[2 * chip[0] + chip[1], c])
            for s, land in zip(s_refs, l_refs) for chip in chips]


_gather_plan.copies_per_source = 3


def _gather_both_cores_plan(s_refs, l_refs):
    x, y, c, chips = _mesh_place()
    jme = 2 * x + y
    plan = []
    for s, land in zip(s_refs, l_refs):
        for chip in chips:
            for peer_core in (c, 1 - c):
                plan.append((s.at[c], land.at[jme, c], (*chip, peer_core), land.at[2 * chip[0] + chip[1], peer_core]))
        plan.append((s, land.at[jme], (x, y, 1 - c), land.at[jme]))
    return plan


_gather_both_cores_plan.copies_per_source = 7


def _scatter_plan(s_refs, l_refs):
    x, y, c, chips = _mesh_place()
    return [(s.at[2 * chip[0] + chip[1]], land.at[k], (*chip, c), land.at[k])
            for s, land in zip(s_refs, l_refs) for k, chip in enumerate(chips)]


_scatter_plan.copies_per_source = 3


def _gather_forward(name, shards, landed):
    nw = len(shards)

    def body(*refs):
        s_refs, o_refs = refs[:nw], refs[2 * nw:3 * nw]
        send_sems, recv_sems = refs[3 * nw:]
        x, y, c, chips = _mesh_place()
        me, sib, jme = (x, y, c), (x, y, 1 - c), 2 * x + y
        sent = []
        for w in range(nw):
            parts = [(o_refs[w].at[2 * chip[0] + chip[1], c], o_refs[w].at[2 * chip[0] + chip[1], c]) for chip in chips]
            parts.append((s_refs[w], o_refs[w].at[jme]))
            for k, (src, dst) in enumerate(parts):
                cp = pltpu.make_async_remote_copy(src_ref=src, dst_ref=dst, send_sem=send_sems.at[4 * w + k],
                                                  recv_sem=recv_sems.at[4 * w + k], device_id=sib, device_id_type=MESH)
                cp.start()
                sent.append(cp)
        for w in range(nw):
            parts = [o_refs[w].at[2 * chip[0] + chip[1], 1 - c] for chip in chips] + [o_refs[w].at[jme]]
            for k, part in enumerate(parts):
                pltpu.make_async_remote_copy(src_ref=part, dst_ref=part, send_sem=send_sems.at[4 * w + k],
                                             recv_sem=recv_sems.at[4 * w + k], device_id=me, device_id_type=MESH).wait_recv()
        for cp in sent:
            cp.wait_send()

    return pl.pallas_call(
        body, name=name,
        in_specs=_any_specs(2 * nw), out_specs=_any_specs(nw),
        out_shape=[jax.ShapeDtypeStruct(a.shape, a.dtype) for a in landed],
        input_output_aliases={nw + i: i for i in range(nw)},
        scratch_shapes=[pltpu.SemaphoreType.DMA((4 * nw,)), pltpu.SemaphoreType.DMA((4 * nw,))],
    )(*shards, *landed)


def _exchange_halves(name, grads, after=()):
    nw = len(grads)

    def body(*refs):
        g_refs, o_refs = refs[:nw], refs[nw + len(after):2 * nw + len(after)]
        send_sems, recv_sems = refs[2 * nw + len(after):]
        x, y, c, _ = _mesh_place()
        copies = []
        for w in range(nw):
            cp = pltpu.make_async_remote_copy(src_ref=g_refs[w].at[:, 1 - c], dst_ref=o_refs[w], send_sem=send_sems.at[w],
                                              recv_sem=recv_sems.at[w], device_id=(x, y, 1 - c), device_id_type=MESH)
            cp.start()
            copies.append(cp)
        for cp in copies:
            cp.wait()

    return pl.pallas_call(
        body, name=name,
        in_specs=_any_specs(nw + len(after)), out_specs=_any_specs(nw),
        out_shape=[jax.ShapeDtypeStruct((N_CHIPS,) + g.shape[2:], g.dtype) for g in grads],
        scratch_shapes=[pltpu.SemaphoreType.DMA((nw,)), pltpu.SemaphoreType.DMA((nw,))],
    )(*grads, *after)


def _row_tile(r):
    for cand in (256, 176, 128, 64, 32, 16, 8):
        if r % cand == 0:
            return cand
    return r


def _add_own_half(c_idx, grad, got):
    _, _, r, cols = grad.shape
    tr = _row_tile(r)

    def body(c_ref, g_ref, o_ref, out_ref):
        out_ref[...] = (g_ref[...].astype(F32) + o_ref[...].astype(F32)).astype(BF16)

    return pl.pallas_call(
        body, name="add_own_half",
        grid_spec=pltpu.PrefetchScalarGridSpec(
            num_scalar_prefetch=1, grid=(N_CHIPS, r // tr),
            in_specs=[pl.BlockSpec((None, None, tr, cols), lambda j, i, c_ref: (j, c_ref[0], i, 0)),
                      pl.BlockSpec((None, tr, cols), lambda j, i, c_ref: (j, i, 0))],
            out_specs=pl.BlockSpec((None, tr, cols), lambda j, i, c_ref: (j, i, 0))),
        out_shape=jax.ShapeDtypeStruct((N_CHIPS, r, cols), BF16),
        compiler_params=_params(("parallel", "parallel")),
    )(c_idx, grad, got)


def _sum_pieces(place_idx, sums, landed):
    _, r, cols = sums.shape
    tr = _row_tile(r)

    def body(j_ref, own_ref, p_ref, o_ref):
        o_ref[...] = ((own_ref[...].astype(F32) + p_ref[0].astype(F32)) + p_ref[1].astype(F32)) + p_ref[2].astype(F32)

    return pl.pallas_call(
        body, name="sum_pieces",
        grid_spec=pltpu.PrefetchScalarGridSpec(
            num_scalar_prefetch=1, grid=(r // tr,),
            in_specs=[pl.BlockSpec((None, tr, cols), lambda i, j_ref: (j_ref[0], i, 0)),
                      pl.BlockSpec((N_CHIPS - 1, tr, cols), lambda i, j_ref: (0, i, 0))],
            out_specs=pl.BlockSpec((None, tr, cols), lambda i, j_ref: (j_ref[1], i, 0))),
        out_shape=jax.ShapeDtypeStruct((2, r, cols), F32),
        compiler_params=_params(("parallel",)),
    )(place_idx, sums, landed)


def _join_halves(name, halves):
    nw = len(halves)

    def body(*refs):
        o_refs = refs[nw:2 * nw]
        send_sems, recv_sems = refs[2 * nw:]
        x, y, c, _ = _mesh_place()
        copies = []
        for w in range(nw):
            cp = pltpu.make_async_remote_copy(src_ref=o_refs[w].at[c], dst_ref=o_refs[w].at[c], send_sem=send_sems.at[w],
                                              recv_sem=recv_sems.at[w], device_id=(x, y, 1 - c), device_id_type=MESH)
            cp.start()
            copies.append(cp)
        for w in range(nw):
            copies[w].wait_send()
            landed = o_refs[w].at[1 - c]
            pltpu.make_async_remote_copy(src_ref=landed, dst_ref=landed, send_sem=send_sems.at[w], recv_sem=recv_sems.at[w],
                                         device_id=(x, y, c), device_id_type=MESH).wait_recv()

    return pl.pallas_call(
        body, name=name,
        in_specs=_any_specs(nw), out_specs=_any_specs(nw),
        out_shape=[jax.ShapeDtypeStruct(h.shape, F32) for h in halves],
        input_output_aliases={i: i for i in range(nw)},
        scratch_shapes=[pltpu.SemaphoreType.DMA((nw,)), pltpu.SemaphoreType.DMA((nw,))],
    )(*halves)


SMALL_ROWS = 8


def _all_reduce_small(pack):
    rows, cols = pack.shape
    n_dev = 8

    def body(p_ref, o_ref, slots, send_sems, recv_sems):
        x, y, c, _ = _mesh_place()
        me = 4 * x + 2 * y + c
        slots[me] = p_ref[...]
        copies = []
        for k in range(1, n_dev):
            peer = (me + k) % n_dev
            cp = pltpu.make_async_remote_copy(src_ref=p_ref, dst_ref=slots.at[me], send_sem=send_sems.at[k],
                                              recv_sem=recv_sems.at[k],
                                              device_id=(peer // 4, (peer // 2) % 2, peer % 2), device_id_type=MESH)
            cp.start()
            copies.append(cp)
        for k in range(1, n_dev):
            src = (me + n_dev - k) % n_dev
            pltpu.make_async_remote_copy(src_ref=p_ref, dst_ref=slots.at[src], send_sem=send_sems.at[k],
                                         recv_sem=recv_sems.at[k], device_id=(x, y, c), device_id_type=MESH).wait_recv()
        for cp in copies:
            cp.wait_send()
        total = slots[0]
        for s in range(1, n_dev):
            total = total + slots[s]
        o_ref[...] = total

    return pl.pallas_call(
        body, name="all_reduce_small",
        in_specs=[pl.BlockSpec(memory_space=pltpu.VMEM)], out_specs=pl.BlockSpec(memory_space=pltpu.VMEM),
        out_shape=jax.ShapeDtypeStruct((rows, cols), F32),
        scratch_shapes=[pltpu.VMEM((n_dev, rows, cols), F32), pltpu.SemaphoreType.DMA((n_dev,)),
                        pltpu.SemaphoreType.DMA((n_dev,))],
    )(pack)


def _adamw(name, w, g, m, v):
    r, cols = w.shape
    tr = _row_tile(r)

    def body(w_ref, g_ref, m_ref, v_ref, d_ref, nm_ref, nv_ref):
        gv = g_ref[...]
        nm = ADAM_B1 * m_ref[...] + (1.0 - ADAM_B1) * gv
        nv = ADAM_B2 * v_ref[...] + (1.0 - ADAM_B2) * (gv * gv)
        m_hat = nm / (1.0 - ADAM_B1 ** ADAM_STEP)
        v_hat = nv / (1.0 - ADAM_B2 ** ADAM_STEP)
        d_ref[...] = -ADAM_LR * (m_hat / (jnp.sqrt(v_hat) + ADAM_EPS) + ADAM_WD * w_ref[...])
        nm_ref[...] = nm
        nv_ref[...] = nv

    spec = pl.BlockSpec((tr, cols), lambda i: (i, 0))
    return pl.pallas_call(
        body, name=name, grid=(r // tr,),
        in_specs=[spec] * 4, out_specs=[spec] * 3,
        out_shape=[jax.ShapeDtypeStruct((r, cols), F32)] * 3,
        compiler_params=_params(("parallel",)),
    )(w, g, m, v)


BIG = ["ffn1_w_in", "ffn1_w_out", "w_in", "conv_w_proj", "attn_w_o", "w_out", "ffn2_w_in", "ffn2_w_out", "conv_dw_kernel"]
COL_SHARDED = ("ffn1_w_in", "w_in", "ffn2_w_in")
SMALL = ["ffn1_norm", "mix_norm", "ffn2_norm", "conv_dw_bias", "conv_ln_g", "conv_ln_b", "q_norm", "k_norm", "attn_sinks", "rel_bias"]
WEIGHTS = ["ffn1_norm", "ffn1_w_in", "ffn1_w_out", "mix_norm", "w_in", "conv_dw_kernel", "conv_dw_bias", "conv_ln_g",
           "conv_ln_b", "conv_w_proj", "q_norm", "k_norm", "attn_sinks", "rel_bias", "attn_w_o", "w_out", "ffn2_norm",
           "ffn2_w_in", "ffn2_w_out"]
SMALL_PLACE = {"ffn1_norm": (0, 0, 1024), "mix_norm": (1, 0, 1024), "ffn2_norm": (2, 0, 1024), "conv_dw_bias": (3, 0, 1024),
               "conv_ln_g": (4, 0, 1024), "conv_ln_b": (5, 0, 1024), "q_norm": (6, 0, 64), "k_norm": (6, 128, 64),
               "attn_sinks": (6, 256, 16), "rel_bias": (7, 0, 512)}
LOSS_PLACE = (6, 384)


def _pack_small(vals, fill=0.0, loss=None):
    pack = jnp.full((SMALL_ROWS, D_MODEL), fill, F32)
    for name, (row, lane, n) in SMALL_PLACE.items():
        pack = pack.at[row, lane:lane + n].set(vals[name].reshape(n))
    if loss is not None:
        pack = pack.at[LOSS_PLACE[0], LOSS_PLACE[1]].set(loss)
    return pack


def _unpack_small(pack, shapes):
    return {name: pack[row, lane:lane + n].reshape(shapes[name]) for name, (row, lane, n) in SMALL_PLACE.items()}


def _shard_halves(name, a):
    if name == "conv_dw_kernel":
        a = jnp.pad(a, ((0, CONV_PAD - CONV_WIDTH), (0, 0)))
    r, cols = a.shape
    return a.reshape(2, r // 2, cols)


GATHER_GROUPS = {"A": ["ffn1_w_in", "ffn1_w_out"],
                 "B": ["w_in", "conv_dw_kernel", "conv_w_proj", "attn_w_o", "w_out"],
                 "C": ["ffn2_w_in", "ffn2_w_out"]}


class _MeshComm:
    def __init__(self, wts):
        self.c_idx = lax.axis_index("c").astype(jnp.int32).reshape(1)
        self.place_idx = jnp.stack([2 * lax.axis_index("x") + lax.axis_index("y"), lax.axis_index("c")]).astype(jnp.int32)
        self.wts, self.gathers, self.reductions, self.reduced = wts, {}, {}, {}
        self.tokens, self.last_join = (), ()
        self._gather_start("A", ())

    def _gather_start(self, group, after):
        names = GATHER_GROUPS[group]
        shards = [_shard_halves(n, self.wts[n]) if n == "conv_dw_kernel" else _shard_halves(n, self.wts[n]).astype(BF16)
                  for n in names]
        lands = [lax.empty((N_CHIPS,) + s.shape, s.dtype) for s in shards]
        self.gathers[group] = _copy_start("gather_start_" + group, shards, lands, self._gather_plan(group), after=after)
        self.tokens = (self.gathers[group][-1],)

    @staticmethod
    def _gather_plan(group):
        return _gather_both_cores_plan if group == "C" else _gather_plan

    def weights(self, group, after):
        send_sems, recv_sems, shards, lands, token = self.gathers.pop(group)
        shards, lands = _copy_wait("gather_wait_" + group, send_sems, recv_sems, shards, lands,
                                   token if after is None else after, self._gather_plan(group))
        gathered = lands if group == "C" else _gather_forward("gather_forward_" + group, shards, lands)
        self.tokens = ()
        following = {"A": "B", "B": "C"}.get(group)
        if following:
            self._gather_start(following, (gathered[0],))
        out = {}
        for n, g4 in zip(GATHER_GROUPS[group], gathered):
            r, cols = g4.shape[2] * 2, g4.shape[3]
            if n in COL_SHARDED:
                out[n] = g4.reshape(N_CHIPS, r, cols)
            elif n == "conv_dw_kernel":
                out[n] = g4.reshape(N_CHIPS, r, cols).transpose(1, 0, 2).reshape(r, N_CHIPS * cols)
            else:
                out[n] = g4.reshape(N_CHIPS * r, cols)
        return out

    def reduce_start(self, group, grads):
        names = list(grads)
        g4 = []
        for n in names:
            a = grads[n]
            if n == "conv_dw_kernel":
                a = a.reshape(CONV_PAD, N_CHIPS, -1).transpose(1, 0, 2)
            elif n not in COL_SHARDED:
                a = a.reshape(N_CHIPS, a.shape[0] // N_CHIPS, a.shape[1])
            g4.append(a.reshape(N_CHIPS, 2, a.shape[1] // 2, a.shape[2]))
        got = _exchange_halves("exchange_halves_" + group, g4, after=self.last_join)
        sums = [_add_own_half(self.c_idx, a, b) for a, b in zip(g4, got)]
        lands = [lax.empty((N_CHIPS - 1,) + s.shape[1:], s.dtype) for s in sums]
        started = _copy_start("scatter_start_" + group, sums, lands, _scatter_plan, after=self.last_join)
        self.reductions[group] = (names,) + started
        return (started[-1],)

    def reduce_finish(self, group, after):
        names, send_sems, recv_sems, sums, lands, _ = self.reductions.pop(group)
        sums, lands = _copy_wait("scatter_wait_" + group, send_sems, recv_sems, sums, lands, after, _scatter_plan)
        halves = [_sum_pieces(self.place_idx, s, p) for s, p in zip(sums, lands)]
        joined = _join_halves("join_halves_" + group, halves)
        self.last_join = (joined[0],)
        self.reduced.update(zip(names, joined))


def kernel(x, ffn1_norm, ffn1_w_in, ffn1_w_out, mix_norm, w_in, conv_dw_kernel, conv_dw_bias, conv_ln_g, conv_ln_b, conv_w_proj, q_norm, k_norm, attn_sinks, rel_bias, attn_w_o, w_out, ffn2_norm, ffn2_w_in, ffn2_w_out, loss_target, m_ffn1_norm, m_ffn1_w_in, m_ffn1_w_out, m_mix_norm, m_w_in, m_conv_dw_kernel, m_conv_dw_bias, m_conv_ln_g, m_conv_ln_b, m_conv_w_proj, m_q_norm, m_k_norm, m_attn_sinks, m_rel_bias, m_attn_w_o, m_w_out, m_ffn2_norm, m_ffn2_w_in, m_ffn2_w_out, v_ffn1_norm, v_ffn1_w_in, v_ffn1_w_out, v_mix_norm, v_w_in, v_conv_dw_kernel, v_conv_dw_bias, v_conv_ln_g, v_conv_ln_b, v_conv_w_proj, v_q_norm, v_k_norm, v_attn_sinks, v_rel_bias, v_attn_w_o, v_w_out, v_ffn2_norm, v_ffn2_w_in, v_ffn2_w_out):
    args = dict(locals())
    wts = {n: args[n] for n in WEIGHTS}
    mom = {n: args["m_" + n] for n in WEIGHTS}
    var = {n: args["v_" + n] for n in WEIGHTS}
    comm = _MeshComm(wts)
    small = {n: wts[n] if n in ("attn_sinks", "rel_bias") else wts[n].reshape(1, -1) for n in SMALL}
    loss_part, grad_x, g = _local_step(x[0], loss_target[0], small, comm)

    small_sum = _all_reduce_small(_pack_small(g, loss=loss_part))
    loss = small_sum[LOSS_PLACE[0], LOSS_PLACE[1]]
    small_shapes = {n: wts[n].shape for n in SMALL}
    g_small = _unpack_small(small_sum, small_shapes)

    grads, delta, new_m, new_v = {}, {}, {}, {}
    for n in BIG:
        j = comm.reduced[n]
        gs = j.reshape(j.shape[1] * 2, j.shape[2])
        pad = n == "conv_dw_kernel"
        ws, ms, vs = (_shard_halves(n, a).reshape(gs.shape) for a in (wts[n], mom[n], var[n]))
        d, nm, nv = _adamw("adamw_" + n, ws, gs, ms, vs)
        cut = (lambda a: a[:CONV_WIDTH]) if pad else (lambda a: a)
        grads[n], delta[n], new_m[n], new_v[n] = cut(gs), cut(d), cut(nm), cut(nv)
    d, nm, nv = _adamw("adamw_small", _pack_small(wts), small_sum, _pack_small(mom), _pack_small(var, fill=1.0))
    grads.update(g_small)
    delta.update(_unpack_small(d, small_shapes))
    new_m.update(_unpack_small(nm, small_shapes))
    new_v.update(_unpack_small(nv, small_shapes))

    return (loss, grad_x[None], *[grads[n] for n in WEIGHTS], *[delta[n] for n in WEIGHTS],
            *[new_m[n] for n in WEIGHTS], *[new_v[n] for n in WEIGHTS])
```

```python
import functools
import math

import jax
import jax.numpy as jnp
from jax import lax
from jax.experimental import pallas as pl
from jax.experimental.pallas import tpu as pltpu

F32 = jnp.float32
BF16 = jnp.bfloat16
MESH = pl.DeviceIdType.MESH

EPS = 1e-6
D_MODEL = 1024
D_FF = 2816
N_CHIPS = 4
SHARD_W = 2 * D_FF // N_CHIPS
HEAD_DIM = 64
N_Q_HEADS = 16
N_KV_HEADS = 4
GROUP = N_Q_HEADS // N_KV_HEADS
BLOCK = 128
QROWS = GROUP * BLOCK
N_BUCKETS = 32
MAX_DISTANCE = 128
CONV_WIDTH = 31
CONV_PAD = 32
NEG = float(jnp.finfo(jnp.float32).min)

ADAM_LR = 0.001
ADAM_B1 = 0.9
ADAM_B2 = 0.999
ADAM_EPS = 1e-08
ADAM_WD = 0.01
ADAM_STEP = 10

VMEM_LIMIT_BYTES = 56 * 1024 * 1024
ROW_TILE = 1024
TOKEN_TILE = 1024
CONV_TILE = 256
CONV_ROWS = 128
LANES = 128

COL_CONV_A, COL_CONV_G, COL_Q, COL_K, COL_V, COL_GC, COL_GA = 0, 1024, 2048, 3072, 3328, 3584, 4608
IN_W = 5632


def _params(sem, vmem=VMEM_LIMIT_BYTES):
    return pltpu.CompilerParams(dimension_semantics=sem, vmem_limit_bytes=vmem)


def _sigmoid(x):
    return 1.0 / (1.0 + jnp.exp(-x))


def _dot(a, b, trans_a=False, trans_b=False, precision=None):
    dn = (((0,) if trans_a else (1,), (1,) if trans_b else (0,)), ((), ()))
    return lax.dot_general(a, b, dn, preferred_element_type=F32, precision=precision)


def _mm(name, grid, a, a_spec, b, b_spec, acc_shape, *, trans_a=False, trans_b=False, a_pre=None, b_pre=None,
        extras=(), extra_specs=(), tokens=(), out_shape, out_specs, epilogue, chunked=False,
        sem=("parallel", "parallel", "arbitrary")):
    n_k = grid[2]
    assert not chunked or (n_k == 1 and b_pre is None)
    extras = tuple(extras) + tuple(tokens)
    extra_specs = tuple(extra_specs) + (pl.BlockSpec((8, LANES), lambda i, j, kk: (0, 0)),) * len(tokens)
    n_extra = len(extras)
    n_out = len(out_shape)

    def body(a_ref, b_ref, *rest):
        ex = rest[:n_extra]
        outs = rest[n_extra:n_extra + n_out]
        ids = (pl.program_id(0), pl.program_id(1), pl.program_id(2))
        av = a_ref[...]
        if a_pre is not None:
            av = a_pre(av)
        if chunked:
            for c0, cw in _col_chunks(acc_shape[1]):
                cols = pl.ds(c0, cw)
                epilogue(_dot(av, b_ref[cols, :] if trans_b else b_ref[:, cols], trans_a, trans_b), ex, outs, ids, cols)
            return
        bv = b_ref[...]
        if b_pre is not None:
            bv = b_pre(bv)
        if n_k == 1:
            epilogue(_dot(av, bv, trans_a, trans_b), ex, outs, ids)
        else:
            acc = rest[-1]

            @pl.when(ids[2] == 0)
            def _():
                acc[...] = jnp.zeros_like(acc)

            acc[...] += _dot(av, bv, trans_a, trans_b)

            @pl.when(ids[2] == n_k - 1)
            def _():
                epilogue(acc[...], ex, outs, ids)

    scratch = [] if n_k == 1 else [pltpu.VMEM(acc_shape, F32)]
    return pl.pallas_call(
        body, name=name, grid=grid,
        in_specs=[a_spec, b_spec, *extra_specs],
        out_specs=list(out_specs), out_shape=list(out_shape),
        scratch_shapes=scratch, compiler_params=_params(sem),
    )(a, b, *extras)


MXU_WIDTH = 256


def _col_chunks(n, width=2 * MXU_WIDTH):
    return [(c0, min(width, n - c0)) for c0 in range(0, n, width)]


def _half_bf16(v):
    return (0.5 * v).astype(BF16)


def _to_bf16(v):
    return v.astype(BF16)


def _rmsnorm_fwd(name, x, g, tokens=()):
    t, d = x.shape
    tm = min(ROW_TILE, t)

    def body(x_ref, g_ref, *rest):
        o_ref = rest[-1]
        xv = x_ref[...]
        r = lax.rsqrt(jnp.mean(xv * xv, axis=-1, keepdims=True) + EPS)
        o_ref[...] = (xv * r * g_ref[...]).astype(BF16)

    return pl.pallas_call(
        body, name=name, grid=(t // tm,),
        in_specs=[pl.BlockSpec((tm, d), lambda i: (i, 0)), pl.BlockSpec((1, d), lambda i: (0, 0))]
        + [pl.BlockSpec((8, LANES), lambda i: (0, 0))] * len(tokens),
        out_specs=pl.BlockSpec((tm, d), lambda i: (i, 0)),
        out_shape=jax.ShapeDtypeStruct((t, d), BF16),
        compiler_params=_params(("parallel",)),
    )(x, g, *tokens)


def _rms_bwd_epilogue(acc, ex, outs, ids):
    x_ref, g_ref, dres_ref = ex[:3]
    out_ref, dg_ref = outs
    xv = x_ref[...]
    r = lax.rsqrt(jnp.mean(xv * xv, axis=-1, keepdims=True) + EPS)
    w = acc * g_ref[...]
    dx = r * w - xv * (r * r * r) * jnp.mean(xv * w, axis=-1, keepdims=True)
    out_ref[...] = dres_ref[...] + dx
    part = jnp.sum(acc * (xv * r), axis=0, keepdims=True)

    @pl.when(ids[0] == 0)
    def _():
        dg_ref[...] = part

    @pl.when(ids[0] > 0)
    def _():
        dg_ref[...] += part


def _ffn_in(name, n, w_in4, tokens=()):
    t, d = n.shape
    tm = min(ROW_TILE, t)

    def body(n_ref, wa_ref, wb_ref, *rest):
        ab_ref, h_ref = rest[-2:]
        nv = n_ref[...]
        for c0, cw in _col_chunks(SHARD_W):
            cols = pl.ds(c0, cw)
            a = _dot(nv, wa_ref[:, cols])
            b = _dot(nv, wb_ref[:, cols])
            h_ref[:, cols] = (a * _sigmoid(a) * b).astype(BF16)
            ab_ref[0, :, cols] = a.astype(BF16)
            ab_ref[1, :, cols] = b.astype(BF16)

    return pl.pallas_call(
        body, name=name, grid=(2, t // tm),
        in_specs=[pl.BlockSpec((tm, d), lambda j, i: (i, 0)),
                  pl.BlockSpec((None, d, SHARD_W), lambda j, i: (j, 0, 0)),
                  pl.BlockSpec((None, d, SHARD_W), lambda j, i: (j + 2, 0, 0))]
        + [pl.BlockSpec((8, LANES), lambda j, i: (0, 0))] * len(tokens),
        out_specs=[pl.BlockSpec((2, tm, SHARD_W), lambda j, i: (0, i, j)),
                   pl.BlockSpec((tm, SHARD_W), lambda j, i: (i, j))],
        out_shape=[jax.ShapeDtypeStruct((2, t, D_FF), BF16), jax.ShapeDtypeStruct((t, D_FF), BF16)],
        compiler_params=_params(("parallel", "parallel")),
    )(n, w_in4, w_in4, *tokens)


def _mm_residual(name, a, w, res, scale, next_gain=None, loss_target=None):
    t, k = a.shape
    n = w.shape[1]
    tm = min(ROW_TILE, t)
    row = pl.BlockSpec((tm, n), lambda i, j, kk: (i, 0))
    extras, specs = [res], [row]
    shapes, out_specs = [jax.ShapeDtypeStruct((t, n), F32)], [row]
    if next_gain is not None:
        extras.append(next_gain)
        specs.append(pl.BlockSpec((1, n), lambda i, j, kk: (0, 0)))
        shapes.append(jax.ShapeDtypeStruct((t, n), BF16))
        out_specs.append(row)
    if loss_target is not None:
        extras.append(loss_target)
        specs.append(row)
        shapes.append(jax.ShapeDtypeStruct((8, LANES), F32))
        out_specs.append(pl.BlockSpec((8, LANES), lambda i, j, kk: (0, 0)))

    def epilogue(acc, ex, outs, ids):
        y = ex[0][...] + scale * acc
        if loss_target is None:
            outs[0][...] = y
        if next_gain is not None:
            r = lax.rsqrt(jnp.mean(y * y, axis=-1, keepdims=True) + EPS)
            outs[1][...] = (y * r * ex[1][...]).astype(BF16)
        if loss_target is not None:
            diff = y - ex[1][...]
            outs[0][...] = diff * (1.0 / n)
            part = jnp.full((8, LANES), 0.5 / n * jnp.sum(diff * diff), F32)

            @pl.when(ids[0] == 0)
            def _():
                outs[1][...] = part

            @pl.when(ids[0] > 0)
            def _():
                outs[1][...] += part

    sem = ("parallel" if loss_target is None else "arbitrary", "parallel", "arbitrary")
    out = _mm(name, (t // tm, 1, 1), a, pl.BlockSpec((tm, k), lambda i, j, kk: (i, 0)),
              w, pl.BlockSpec((k, n), lambda i, j, kk: (0, 0)), (tm, n),
              extras=extras, extra_specs=specs, out_shape=shapes, out_specs=out_specs, epilogue=epilogue, sem=sem)
    return out[0] if len(out) == 1 else tuple(out)


def _ffn_fwd(tag, x, n, w_in4, w_out, tokens=(), **tail):
    ab, h = _ffn_in(tag + "_in", n, w_in4, tokens)
    y = _mm_residual(tag + "_out", h, w_out, x, 0.5, **tail)
    return y, (n, ab, h)


def _ffn_bwd(tag, dres, x, g, saved, w_in4, w_out, tokens=(), on_weight_grads=None):
    n, ab, h = saved
    t, d = x.shape
    tm = min(ROW_TILE, t)
    tk = min(TOKEN_TILE, t)
    half_w = SHARD_W

    def dact_epilogue(acc, ex, outs, ids, cols):
        a = ex[0][0, :, cols].astype(F32)
        b = ex[0][1, :, cols].astype(F32)
        sig = _sigmoid(a)
        outs[0][0, :, cols] = (acc * b * (sig * (1.0 + a * (1.0 - sig)))).astype(BF16)
        outs[0][1, :, cols] = (acc * (a * sig)).astype(BF16)

    du = _mm(tag + "_dact", (2, t // tm, 1),
             dres, pl.BlockSpec((tm, d), lambda j, i, kk: (i, 0)),
             w_out, pl.BlockSpec((half_w, d), lambda j, i, kk: (j, 0)), (tm, half_w),
             trans_b=True, a_pre=_half_bf16,
             extras=(ab,), extra_specs=(pl.BlockSpec((2, tm, half_w), lambda j, i, kk: (0, i, j)),), tokens=tokens,
             out_shape=(jax.ShapeDtypeStruct((2, t, D_FF), BF16),),
             out_specs=(pl.BlockSpec((2, tm, half_w), lambda j, i, kk: (0, i, j)),),
             epilogue=dact_epilogue, chunked=True)[0]

    def store_epilogue(acc, ex, outs, ids):
        outs[0][...] = acc.astype(BF16)

    dw_out = _mm(tag + "_dwout", (2, 1, t // tk),
                 h, pl.BlockSpec((tk, half_w), lambda i, j, kk: (kk, i)),
                 dres, pl.BlockSpec((tk, d), lambda i, j, kk: (kk, 0)), (half_w, d),
                 trans_a=True, b_pre=_half_bf16,
                 out_shape=(jax.ShapeDtypeStruct((D_FF, d), BF16),),
                 out_specs=(pl.BlockSpec((half_w, d), lambda i, j, kk: (i, 0)),),
                 epilogue=store_epilogue)[0]

    dw_in4 = _mm(tag + "_dwin", (1, N_CHIPS, t // tk),
                 n, pl.BlockSpec((tk, d), lambda i, j, kk: (kk, 0)),
                 du, pl.BlockSpec((None, tk, SHARD_W), lambda i, j, kk: (j // 2, kk, j % 2)), (d, SHARD_W),
                 trans_a=True,
                 out_shape=(jax.ShapeDtypeStruct((N_CHIPS, d, SHARD_W), BF16),),
                 out_specs=(pl.BlockSpec((None, d, SHARD_W), lambda i, j, kk: (j, 0, 0)),),
                 epilogue=store_epilogue)[0]

    late = () if on_weight_grads is None else on_weight_grads(dw_in4, dw_out)

    dx, dg = _mm(tag + "_dn", (t // tm, 1, N_CHIPS),
                 du, pl.BlockSpec((None, tm, SHARD_W), lambda i, j, kk: (kk // 2, i, kk % 2)),
                 w_in4, pl.BlockSpec((None, d, SHARD_W), lambda i, j, kk: (kk, 0, 0)), (tm, d),
                 trans_b=True,
                 extras=(x, g, dres),
                 extra_specs=(pl.BlockSpec((tm, d), lambda i, j, kk: (i, 0)),
                              pl.BlockSpec((1, d), lambda i, j, kk: (0, 0)),
                              pl.BlockSpec((tm, d), lambda i, j, kk: (i, 0))), tokens=late,
                 out_shape=(jax.ShapeDtypeStruct((t, d), F32), jax.ShapeDtypeStruct((1, d), F32)),
                 out_specs=(pl.BlockSpec((tm, d), lambda i, j, kk: (i, 0)),
                            pl.BlockSpec((1, d), lambda i, j, kk: (0, 0))),
                 epilogue=_rms_bwd_epilogue, sem=("arbitrary", "arbitrary", "arbitrary"))
    return dx, dw_in4, dw_out, dg


def _conv_fill(zp_ref, a_ref, g_ref, ah_ref, gh_ref, i):
    zh = ah_ref[...].astype(F32) * _sigmoid(gh_ref[...].astype(F32))
    zp_ref[pl.ds(0, CONV_PAD), :] = jnp.where(i > 0, zh, 0.0)
    zp_ref[pl.ds(CONV_PAD, a_ref.shape[0]), :] = a_ref[...].astype(F32) * _sigmoid(g_ref[...].astype(F32))


def _shift_groups(shifts):
    groups = {}
    for j, s in shifts:
        groups.setdefault(s % 8, []).append((j, s // 8))
    return groups


def _windows(zp_ref, r0, lanes, groups):
    for q, taps in groups.items():
        deepest = max(p for _, p in taps)
        win = zp_ref[pl.ds(r0 + q, 8 * deepest + CONV_ROWS), lanes]
        for j, p in taps:
            yield j, win[8 * p:8 * p + CONV_ROWS]


def _conv_apply(zp_ref, out_ref, dw_ref, bias_ref, tm, ch, shifts):
    groups = _shift_groups(shifts)
    for cc in range(ch // LANES):
        lanes = pl.ds(cc * LANES, LANES)
        w = [dw_ref[pl.ds(j, 1), lanes] for j in range(CONV_WIDTH)]
        for r0 in range(0, tm, CONV_ROWS):
            if bias_ref is None:
                acc = jnp.zeros((CONV_ROWS, LANES), F32)
            else:
                acc = jnp.broadcast_to(bias_ref[:, lanes], (CONV_ROWS, LANES))
            for j, rows in _windows(zp_ref, r0, lanes, groups):
                acc = acc + w[j] * rows
            out_ref[pl.ds(r0, CONV_ROWS), lanes] = acc


FWD_SHIFTS = [(j, CONV_PAD - (CONV_WIDTH - 1) + j) for j in range(CONV_WIDTH)]
BWD_SHIFTS = [(j, CONV_WIDTH - 1 - j) for j in range(CONV_WIDTH)]


def _conv_taps(zp_ref, z1_ref, dw_ref, bias_ref, tm, ch):
    _conv_apply(zp_ref, z1_ref, dw_ref, bias_ref, tm, ch, FWD_SHIFTS)


def _conv_specs(tm, ch):
    per = tm // CONV_PAD
    cb = COL_CONV_G // ch
    return [pl.BlockSpec((tm, ch), lambda i: (i, 0)),
            pl.BlockSpec((tm, ch), lambda i: (i, cb)),
            pl.BlockSpec((CONV_PAD, ch), lambda i: (jnp.maximum(i * per - 1, 0), 0)),
            pl.BlockSpec((CONV_PAD, ch), lambda i: (jnp.maximum(i * per - 1, 0), cb))]


def _conv_fwd(p, dw, bias, ln_g, ln_b):
    t = p.shape[0]
    ch = D_MODEL
    tm = min(CONV_TILE, t)

    def body(a_ref, g_ref, ah_ref, gh_ref, dw_ref, bias_ref, lg_ref, lb_ref, o_ref, z1_ref, zp_ref):
        i = pl.program_id(0)
        _conv_fill(zp_ref, a_ref, g_ref, ah_ref, gh_ref, i)
        _conv_taps(zp_ref, z1_ref, dw_ref, bias_ref, tm, ch)
        z1 = z1_ref[...]
        mu = jnp.mean(z1, axis=-1, keepdims=True)
        zc = z1 - mu
        rs = lax.rsqrt(jnp.mean(zc * zc, axis=-1, keepdims=True) + EPS)
        z2 = zc * rs * lg_ref[...] + lb_ref[...]
        o_ref[...] = (z2 * _sigmoid(z2)).astype(BF16)

    vec = pl.BlockSpec((1, ch), lambda i: (0, 0))
    return pl.pallas_call(
        body, name="conv_fwd", grid=(t // tm,),
        in_specs=_conv_specs(tm, ch) + [pl.BlockSpec((CONV_PAD, ch), lambda i: (0, 0)), vec, vec, vec],
        out_specs=[pl.BlockSpec((tm, ch), lambda i: (i, 0)), pl.BlockSpec((tm, ch), lambda i: (i, 0))],
        out_shape=[jax.ShapeDtypeStruct((t, ch), BF16), jax.ShapeDtypeStruct((t, ch), F32)],
        scratch_shapes=[pltpu.VMEM((CONV_PAD + tm, ch), F32)],
        compiler_params=_params(("parallel",)),
    )(p, p, p, p, dw, bias, ln_g, ln_b)


def _conv_bwd_ln(p, z1_saved, dz3, ln_g, ln_b):
    t = p.shape[0]
    ch = D_MODEL
    tm = min(CONV_TILE, t)

    def body(a_ref, g_ref, ah_ref, gh_ref, z1_ref, dz3_ref, lg_ref, lb_ref,
             dz1_ref, ddw_ref, dbias_ref, dlg_ref, dlb_ref, zp_ref):
        i = pl.program_id(0)
        _conv_fill(zp_ref, a_ref, g_ref, ah_ref, gh_ref, i)
        z1 = z1_ref[...]
        mu = jnp.mean(z1, axis=-1, keepdims=True)
        zc = z1 - mu
        rs = lax.rsqrt(jnp.mean(zc * zc, axis=-1, keepdims=True) + EPS)
        xh = zc * rs
        z2 = xh * lg_ref[...] + lb_ref[...]
        sig = _sigmoid(z2)
        dz2 = dz3_ref[...].astype(F32) * (sig * (1.0 + z2 * (1.0 - sig)))
        dxh = dz2 * lg_ref[...]
        dz1 = rs * (dxh - jnp.mean(dxh, axis=-1, keepdims=True) - xh * jnp.mean(dxh * xh, axis=-1, keepdims=True))
        dz1_ref[...] = dz1

        @pl.when(i == 0)
        def _():
            ddw_ref[...] = jnp.zeros_like(ddw_ref)
            dbias_ref[...] = jnp.zeros_like(dbias_ref)
            dlg_ref[...] = jnp.zeros_like(dlg_ref)
            dlb_ref[...] = jnp.zeros_like(dlb_ref)

        dlg_ref[...] += jnp.sum(dz2 * xh, axis=0, keepdims=True)
        dlb_ref[...] += jnp.sum(dz2, axis=0, keepdims=True)
        dbias_ref[...] += jnp.sum(dz1, axis=0, keepdims=True)
        groups = _shift_groups(FWD_SHIFTS)
        for cc in range(ch // LANES):
            lanes = pl.ds(cc * LANES, LANES)
            accs = [jnp.zeros((8, LANES), F32) for _ in range(CONV_WIDTH)]
            for r0 in range(0, tm, CONV_ROWS):
                dzc = dz1_ref[pl.ds(r0, CONV_ROWS), lanes]
                for j, rows in _windows(zp_ref, r0, lanes, groups):
                    accs[j] = accs[j] + jnp.sum((dzc * rows).reshape(CONV_ROWS // 8, 8, LANES), axis=0)
            for j in range(CONV_WIDTH):
                ddw_ref[pl.ds(j, 1), lanes] += jnp.sum(accs[j], axis=0, keepdims=True)

    vec = pl.BlockSpec((1, ch), lambda i: (0, 0))
    return pl.pallas_call(
        body, name="conv_bwd_ln", grid=(t // tm,),
        in_specs=_conv_specs(tm, ch) + [pl.BlockSpec((tm, ch), lambda i: (i, 0)),
                                        pl.BlockSpec((tm, ch), lambda i: (i, 0)), vec, vec],
        out_specs=[pl.BlockSpec((tm, ch), lambda i: (i, 0)), pl.BlockSpec((CONV_PAD, ch), lambda i: (0, 0)), vec, vec, vec],
        out_shape=[jax.ShapeDtypeStruct((t, ch), F32), jax.ShapeDtypeStruct((CONV_PAD, ch), F32)]
        + [jax.ShapeDtypeStruct((1, ch), F32)] * 3,
        scratch_shapes=[pltpu.VMEM((CONV_PAD + tm, ch), F32)],
        compiler_params=_params(("arbitrary",)),
    )(p, p, p, p, z1_saved, dz3, ln_g, ln_b)


def _conv_bwd_glu(p, dz1, dw, dq, dkv, dgates):
    t = p.shape[0]
    ch = D_MODEL
    tm = min(CONV_TILE, t)
    per = tm // CONV_PAD
    n_halo = t // CONV_PAD
    cb = COL_CONV_G // ch

    def body(a_ref, g_ref, dz_ref, dzn_ref, dw_ref, dq_ref, dkv_ref, dgates_ref, o_ref, zp_ref, z0_ref):
        i = pl.program_id(0)
        o_ref[:, pl.ds(COL_Q, Q_W)] = dq_ref[...]
        o_ref[:, pl.ds(COL_K, 2 * KV_W)] = dkv_ref[...]
        o_ref[:, pl.ds(COL_GC, ch)] = dgates_ref[0]
        o_ref[:, pl.ds(COL_GA, ch)] = dgates_ref[1]
        zp_ref[pl.ds(0, tm), :] = dz_ref[...]
        zp_ref[pl.ds(tm, CONV_PAD), :] = jnp.where(i < t // tm - 1, dzn_ref[...], 0.0)
        _conv_apply(zp_ref, z0_ref, dw_ref, None, tm, ch, BWD_SHIFTS)
        dz0 = z0_ref[...]
        a = a_ref[...].astype(F32)
        sig = _sigmoid(g_ref[...].astype(F32))
        o_ref[:, pl.ds(0, ch)] = (dz0 * sig).astype(BF16)
        o_ref[:, pl.ds(ch, ch)] = (dz0 * a * sig * (1.0 - sig)).astype(BF16)

    return pl.pallas_call(
        body, name="conv_bwd_glu", grid=(t // tm,),
        in_specs=[pl.BlockSpec((tm, ch), lambda i: (i, 0)), pl.BlockSpec((tm, ch), lambda i: (i, cb)),
                  pl.BlockSpec((tm, ch), lambda i: (i, 0)),
                  pl.BlockSpec((CONV_PAD, ch), lambda i: (jnp.minimum((i + 1) * per, n_halo - 1), 0)),
                  pl.BlockSpec((CONV_PAD, ch), lambda i: (0, 0)),
                  pl.BlockSpec((tm, Q_W), lambda i: (i, 0)), pl.BlockSpec((tm, 2 * KV_W), lambda i: (i, 0)),
                  pl.BlockSpec((2, tm, ch), lambda i: (0, i, 0))],
        out_specs=pl.BlockSpec((tm, IN_W), lambda i: (i, 0)),
        out_shape=jax.ShapeDtypeStruct((t, IN_W), BF16),
        scratch_shapes=[pltpu.VMEM((tm + CONV_PAD, ch), F32), pltpu.VMEM((tm, ch), F32)],
        compiler_params=_params(("parallel",)),
    )(p, p, dz1, dz1, dw, dq, dkv, dgates)


def _bucket_onehot():
    qi = jnp.arange(BLOCK, dtype=jnp.int32)[:, None]
    kj = jnp.arange(2 * BLOCK, dtype=jnp.int32)[None, :]
    dist = jnp.maximum(qi + BLOCK - kj, 0)
    max_exact = N_BUCKETS // 2
    dflt = jnp.maximum(dist, 1).astype(F32)
    large = max_exact + (jnp.log(dflt / max_exact) / math.log(MAX_DISTANCE / max_exact)
                         * (N_BUCKETS - max_exact)).astype(jnp.int32)
    large = jnp.minimum(large, N_BUCKETS - 1)
    bucket = jnp.where(dist < max_exact, dist, large)
    onehot = bucket[None] == jnp.arange(N_BUCKETS, dtype=jnp.int32)[:, None, None]
    return onehot.astype(F32).reshape(N_BUCKETS, BLOCK * 2 * BLOCK)


def _bias_table(rel_bias_t, onehot):
    n = onehot.shape[1]
    tn = 4096

    def body(r_ref, oh_ref, o_ref):
        flat = pl.program_id(0) * tn + lax.broadcasted_iota(jnp.int32, (N_Q_HEADS, tn), 1)
        dist = (flat // (2 * BLOCK)) + BLOCK - (flat % (2 * BLOCK))
        bias = _dot(r_ref[...], oh_ref[...], precision=lax.Precision.HIGHEST)
        o_ref[...] = jnp.where((dist >= 0) & (dist < BLOCK), bias, NEG)

    return pl.pallas_call(
        body, name="bias_table", grid=(n // tn,),
        in_specs=[pl.BlockSpec((N_Q_HEADS, N_BUCKETS), lambda i: (0, 0)), pl.BlockSpec((N_BUCKETS, tn), lambda i: (0, i))],
        out_specs=pl.BlockSpec((N_Q_HEADS, tn), lambda i: (0, i)),
        out_shape=jax.ShapeDtypeStruct((N_Q_HEADS, n), F32),
        compiler_params=_params(("parallel",)),
    )(rel_bias_t, onehot)


def _bias_table_bwd(dbias, onehot):
    n = onehot.shape[1]
    tn = 4096

    def body(d_ref, oh_ref, o_ref):
        part = _dot(d_ref[...], oh_ref[...], trans_b=True, precision=lax.Precision.HIGHEST)
        i = pl.program_id(0)

        @pl.when(i == 0)
        def _():
            o_ref[...] = part

        @pl.when(i > 0)
        def _():
            o_ref[...] += part

    return pl.pallas_call(
        body, name="bias_table_bwd", grid=(n // tn,),
        in_specs=[pl.BlockSpec((N_Q_HEADS, tn), lambda i: (0, i)), pl.BlockSpec((N_BUCKETS, tn), lambda i: (0, i))],
        out_specs=pl.BlockSpec((N_Q_HEADS, N_BUCKETS), lambda i: (0, 0)),
        out_shape=jax.ShapeDtypeStruct((N_Q_HEADS, N_BUCKETS), F32),
        compiler_params=_params(("arbitrary",)),
    )(dbias, onehot)


def _lane_head(rows):
    return lax.broadcasted_iota(jnp.int32, (rows, KV_W), 1) // HEAD_DIM


def _group_rms(x, gain_wide):
    head = _lane_head(x.shape[0])
    sq = x * x
    r = jnp.zeros_like(x)
    for i in range(N_KV_HEADS):
        ms = jnp.sum(jnp.where(head == i, sq, 0.0), axis=-1, keepdims=True) * (1.0 / HEAD_DIM)
        r = jnp.where(head == i, lax.rsqrt(ms + EPS), r)
    return r, x * r * gain_wide


def _stack_heads(group):
    head = _lane_head(group.shape[0])
    return jnp.concatenate([jnp.where(head == i, group, jnp.zeros_like(group)) for i in range(N_KV_HEADS)], axis=0)


def _unstack_heads(stacked):
    head = _lane_head(BLOCK)
    out = jnp.where(head == 0, stacked[:BLOCK], 0.0)
    for i in range(1, N_KV_HEADS):
        out = out + jnp.where(head == i, stacked[i * BLOCK:(i + 1) * BLOCK], 0.0)
    return out


def _repeaters():
    row = lax.broadcasted_iota(jnp.int32, (KV_W, KV_W), 0)
    col = lax.broadcasted_iota(jnp.int32, (KV_W, KV_W), 1)
    return [(row == h * HEAD_DIM + col % HEAD_DIM).astype(BF16) for h in range(N_KV_HEADS)]


def _attn_probs(q_stack, k_rep, sink, bias, before_start):
    s = _dot(q_stack, k_rep, trans_b=True) * (1.0 / math.sqrt(HEAD_DIM)) + bias
    s = jnp.where(before_start, NEG, s)
    m = jnp.maximum(jnp.max(s, axis=-1, keepdims=True), sink)
    p = jnp.exp(s - m)
    es = jnp.exp(sink - m)
    inv = 1.0 / (jnp.sum(p, axis=-1, keepdims=True) + es)
    return p * inv, es * inv


def _before_start(n):
    col = lax.broadcasted_iota(jnp.int32, (QROWS, 2 * BLOCK), 1)
    return (col < BLOCK) & (n == 0)


KV_W = N_KV_HEADS * HEAD_DIM
Q_W = N_Q_HEADS * HEAD_DIM


def _attn_specs():
    qspec = pl.BlockSpec((BLOCK, Q_W), lambda n: (n, COL_Q // Q_W))
    kprev = pl.BlockSpec((BLOCK, KV_W), lambda n: (jnp.maximum(n - 1, 0), COL_K // KV_W))
    kcur = pl.BlockSpec((BLOCK, KV_W), lambda n: (n, COL_K // KV_W))
    vprev = pl.BlockSpec((BLOCK, KV_W), lambda n: (jnp.maximum(n - 1, 0), COL_V // KV_W))
    vcur = pl.BlockSpec((BLOCK, KV_W), lambda n: (n, COL_V // KV_W))
    gain = pl.BlockSpec((1, KV_W), lambda n: (0, 0))
    sink = pl.BlockSpec((N_KV_HEADS, QROWS, 1), lambda n: (0, 0, 0))
    bias = pl.BlockSpec((N_KV_HEADS, QROWS, 2 * BLOCK), lambda n: (0, 0, 0))
    return [qspec, kprev, kcur, vprev, vcur], gain, sink, bias


def _attn_fwd(p, gq, gk, sink_rows, bias):
    t = p.shape[0]
    nb = t // BLOCK
    qkv, gain, sink, bspec = _attn_specs()

    def body(q_ref, kp_ref, kc_ref, vp_ref, vc_ref, gq_ref, gk_ref, sink_ref, bias_ref, o_ref, p_ref, ps_ref):
        before_start = _before_start(pl.program_id(0))
        rep = _repeaters()
        kf = jnp.concatenate([kp_ref[...], kc_ref[...]], axis=0).astype(F32)
        kn = _group_rms(kf, gk_ref[...])[1].astype(BF16)
        v = jnp.concatenate([vp_ref[...], vc_ref[...]], axis=0)
        for h in range(N_KV_HEADS):
            qn = _group_rms(q_ref[:, pl.ds(h * KV_W, KV_W)].astype(F32), gq_ref[...])[1]
            k_rep = _dot(kn, rep[h]).astype(BF16)
            pn, ps_ref[h] = _attn_probs(_stack_heads(qn).astype(BF16), k_rep, sink_ref[h], bias_ref[h], before_start)
            pn = pn.astype(BF16)
            p_ref[h] = pn
            v_rep = _dot(v, rep[h]).astype(BF16)
            o_ref[:, pl.ds(h * KV_W, KV_W)] = _unstack_heads(_dot(pn, v_rep)).astype(BF16)

    return pl.pallas_call(
        body, name="attn_fwd", grid=(nb,),
        in_specs=qkv + [gain, gain, sink, bspec],
        out_specs=[pl.BlockSpec((BLOCK, Q_W), lambda n: (n, 0)),
                   pl.BlockSpec((None, N_KV_HEADS, QROWS, 2 * BLOCK), lambda n: (n, 0, 0, 0)),
                   pl.BlockSpec((None, N_KV_HEADS, QROWS, 1), lambda n: (n, 0, 0, 0))],
        out_shape=[jax.ShapeDtypeStruct((t, Q_W), BF16),
                   jax.ShapeDtypeStruct((nb, N_KV_HEADS, QROWS, 2 * BLOCK), BF16),
                   jax.ShapeDtypeStruct((nb, N_KV_HEADS, QROWS, 1), F32)],
        compiler_params=_params(("parallel",)),
    )(p, p, p, p, p, gq, gk, sink_rows, bias)


def _attn_bwd(p, do, gq, gk, probs, sink_probs):
    t = p.shape[0]
    nb = t // BLOCK
    qkv, _, sink, bspec = _attn_specs()
    gain = pl.BlockSpec((1, HEAD_DIM), lambda n: (0, 0))
    scale = 1.0 / math.sqrt(HEAD_DIM)

    def head_selectors():
        row = lax.broadcasted_iota(jnp.int32, (KV_W, HEAD_DIM), 0)
        col = lax.broadcasted_iota(jnp.int32, (KV_W, HEAD_DIM), 1)
        return [(row == col + i * HEAD_DIM).astype(BF16) for i in range(N_KV_HEADS)]

    def take_heads(group, sel):
        return jnp.concatenate([_dot(group, s) for s in sel], axis=0)

    def put_heads(x, sel):
        rows = x.shape[0] // len(sel)
        out = _dot(x[:rows].astype(BF16), sel[0], trans_b=True)
        for i in range(1, len(sel)):
            out = out + _dot(x[i * rows:(i + 1) * rows].astype(BF16), sel[i], trans_b=True)
        return out

    def rms(x, g):
        r = lax.rsqrt(jnp.mean(x * x, axis=-1, keepdims=True) + EPS)
        return r, x * r * g

    def rms_bwd(dn, xf, r, g):
        w = dn * g
        dx = r * w - xf * (r * r * r) * jnp.mean(xf * w, axis=-1, keepdims=True)
        return dx, jnp.sum(dn * (xf * r), axis=0, keepdims=True)

    def body(q_ref, kp_ref, kc_ref, vp_ref, vc_ref, do_ref, gq_ref, gk_ref, p_ref, ps_ref,
             dq_ref, dkv_ref, dbias_ref, dsink_ref, dgq_ref, dgk_ref):
        n = pl.program_id(0)
        sel = head_selectors()

        @pl.when(n == 0)
        def _():
            dbias_ref[...] = jnp.zeros_like(dbias_ref)
            dsink_ref[...] = jnp.zeros_like(dsink_ref)
            dgq_ref[...] = jnp.zeros_like(dgq_ref)
            dgk_ref[...] = jnp.zeros_like(dgk_ref)

        dgq_sum = jnp.zeros((1, HEAD_DIM), F32)
        dgk_sum = jnp.zeros((1, HEAD_DIM), F32)
        dk_rows, dv_rows = [], []
        for h in range(N_KV_HEADS):
            qf = take_heads(q_ref[:, pl.ds(h * KV_W, KV_W)], sel)
            rq, qn = rms(qf, gq_ref[...])
            kf = jnp.concatenate([_dot(kp_ref[...], sel[h]), _dot(kc_ref[...], sel[h])], axis=0)
            rk, kn = rms(kf, gk_ref[...])
            pn_bf16, psink = p_ref[h], ps_ref[h]
            pn = pn_bf16.astype(F32)
            do = take_heads(do_ref[:, pl.ds(h * KV_W, KV_W)], sel).astype(BF16)
            v = jnp.concatenate([_dot(vp_ref[...], sel[h]), _dot(vc_ref[...], sel[h])], axis=0).astype(BF16)
            dv_win = _dot(do, pn_bf16, trans_a=True).T
            dp = _dot(do, v, trans_b=True)
            delta = jnp.sum(pn * dp, axis=-1, keepdims=True)
            ds = pn * (dp - delta)
            dsc = (ds * scale).astype(BF16)
            dqn = _dot(dsc, kn.astype(BF16))
            dkn = _dot(qn.astype(BF16), dsc, trans_a=True).T
            dq, dgq = rms_bwd(dqn, qf, rq, gq_ref[...])
            dk_win, dgk = rms_bwd(dkn, kf, rk, gk_ref[...])
            dq_ref[:, pl.ds(h * KV_W, KV_W)] = put_heads(dq, sel).astype(BF16)
            dk_rows += [dk_win[:BLOCK], dk_win[BLOCK:]]
            dv_rows += [dv_win[:BLOCK], dv_win[BLOCK:]]
            dbias_ref[h] += ds
            dsink_ref[h] += jnp.sum((-psink * delta).reshape(GROUP, BLOCK, 1), axis=1)
            dgq_sum = dgq_sum + dgq
            dgk_sum = dgk_sum + dgk
        for part in range(2):
            dkv_ref[part, :, pl.ds(0, KV_W)] = put_heads(jnp.concatenate(dk_rows[part::2], axis=0), sel).astype(BF16)
            dkv_ref[part, :, pl.ds(KV_W, KV_W)] = put_heads(jnp.concatenate(dv_rows[part::2], axis=0), sel).astype(BF16)
        dgq_ref[...] += dgq_sum
        dgk_ref[...] += dgk_sum

    row = pl.BlockSpec((BLOCK, Q_W), lambda n: (n, 0))
    return pl.pallas_call(
        body, name="attn_bwd", grid=(nb,),
        in_specs=qkv + [row, gain, gain,
                        pl.BlockSpec((None, N_KV_HEADS, QROWS, 2 * BLOCK), lambda n: (n, 0, 0, 0)),
                        pl.BlockSpec((None, N_KV_HEADS, QROWS, 1), lambda n: (n, 0, 0, 0))],
        out_specs=[row, pl.BlockSpec((None, 2, BLOCK, 2 * KV_W), lambda n: (n, 0, 0, 0)), bspec,
                   pl.BlockSpec((N_KV_HEADS, GROUP, 1), lambda n: (0, 0, 0)), gain, gain],
        out_shape=[jax.ShapeDtypeStruct((t, Q_W), BF16),
                   jax.ShapeDtypeStruct((nb, 2, BLOCK, 2 * KV_W), BF16),
                   jax.ShapeDtypeStruct((N_KV_HEADS, QROWS, 2 * BLOCK), F32),
                   jax.ShapeDtypeStruct((N_KV_HEADS, GROUP, 1), F32),
                   jax.ShapeDtypeStruct((1, HEAD_DIM), F32),
                   jax.ShapeDtypeStruct((1, HEAD_DIM), F32)],
        compiler_params=_params(("arbitrary",)),
    )(p, p, p, p, p, do, gq, gk, probs, sink_probs)


def _kv_window_sum(parts):
    nb = parts.shape[0]

    def body(cur_ref, nxt_ref, o_ref):
        nxt = jnp.where(pl.program_id(0) < nb - 1, nxt_ref[...].astype(F32), 0.0)
        o_ref[...] = (cur_ref[...].astype(F32) + nxt).astype(BF16)

    blk = (None, None, BLOCK, 2 * KV_W)
    return pl.pallas_call(
        body, name="kv_window_sum", grid=(nb,),
        in_specs=[pl.BlockSpec(blk, lambda n: (n, 1, 0, 0)),
                  pl.BlockSpec(blk, lambda n: (jnp.minimum(n + 1, nb - 1), 0, 0, 0))],
        out_specs=pl.BlockSpec((BLOCK, 2 * KV_W), lambda n: (n, 0)),
        out_shape=jax.ShapeDtypeStruct((nb * BLOCK, 2 * KV_W), BF16),
        compiler_params=_params(("parallel",)),
    )(parts, parts)


GATE_TILE = 512


def _merge_fwd(z3, o, p, w_proj, w_o):
    t, d = z3.shape
    tm = min(ROW_TILE, t)
    tn = GATE_TILE

    def body(z_ref, o_ref, gc_ref, ga_ref, wp_ref, wo_ref, m_ref, a_ref, b_ref):
        a = _dot(z_ref[...], wp_ref[...])
        b = _dot(o_ref[...], wo_ref[...])
        m_ref[...] = (_sigmoid(gc_ref[...].astype(F32)) * a + _sigmoid(ga_ref[...].astype(F32)) * b).astype(BF16)
        a_ref[...] = a.astype(BF16)
        b_ref[...] = b.astype(BF16)

    row = pl.BlockSpec((tm, d), lambda i, j: (i, 0))
    wspec = pl.BlockSpec((d, tn), lambda i, j: (0, j))
    ospec = pl.BlockSpec((tm, tn), lambda i, j: (i, j))
    return pl.pallas_call(
        body, name="merge_fwd", grid=(t // tm, d // tn),
        in_specs=[row, row,
                  pl.BlockSpec((tm, tn), lambda i, j: (i, COL_GC // tn + j)),
                  pl.BlockSpec((tm, tn), lambda i, j: (i, COL_GA // tn + j)), wspec, wspec],
        out_specs=[ospec, ospec, ospec],
        out_shape=[jax.ShapeDtypeStruct((t, d), BF16)] * 3,
        compiler_params=_params(("parallel", "parallel")),
    )(z3, o, p, p, w_proj, w_o)


def _merge_bwd(dres, w_out, a, b, p, tokens=()):
    t, d = dres.shape
    tm = min(ROW_TILE, t)
    tn = GATE_TILE

    def epilogue(acc, ex, outs, ids):
        a_ref, b_ref, gc_ref, ga_ref = ex[:4]
        sc = _sigmoid(gc_ref[...].astype(F32))
        sa = _sigmoid(ga_ref[...].astype(F32))
        outs[0][...] = (acc * sc).astype(BF16)
        outs[1][...] = (acc * sa).astype(BF16)
        outs[2][0] = (acc * a_ref[...].astype(F32) * sc * (1.0 - sc)).astype(BF16)
        outs[2][1] = (acc * b_ref[...].astype(F32) * sa * (1.0 - sa)).astype(BF16)

    ospec = pl.BlockSpec((tm, tn), lambda i, j, kk: (i, j))
    return _mm("merge_bwd", (t // tm, d // tn, 1),
               dres, pl.BlockSpec((tm, d), lambda i, j, kk: (i, 0)),
               w_out, pl.BlockSpec((tn, d), lambda i, j, kk: (j, 0)), (tm, tn),
               trans_b=True, a_pre=_to_bf16,
               extras=(a, b, p, p),
               extra_specs=(ospec, ospec,
                            pl.BlockSpec((tm, tn), lambda i, j, kk: (i, COL_GC // tn + j)),
                            pl.BlockSpec((tm, tn), lambda i, j, kk: (i, COL_GA // tn + j))), tokens=tokens,
               out_shape=(jax.ShapeDtypeStruct((t, d), BF16), jax.ShapeDtypeStruct((t, d), BF16),
                          jax.ShapeDtypeStruct((2, t, d), BF16)),
               out_specs=(ospec, ospec, pl.BlockSpec((2, tm, tn), lambda i, j, kk: (0, i, j))),
               epilogue=epilogue)


def _store_epilogue(acc, ex, outs, ids):
    outs[0][...] = acc


def _store_bf16_epilogue(acc, ex, outs, ids):
    outs[0][...] = acc.astype(BF16)


def _mm_nt(name, a, w, out_dtype=BF16):
    t, n = a.shape
    k = w.shape[0]
    tm = min(ROW_TILE, t)
    return _mm(name, (t // tm, 1, 1), a, pl.BlockSpec((tm, n), lambda i, j, kk: (i, 0)),
               w, pl.BlockSpec((k, n), lambda i, j, kk: (0, 0)), (tm, k), trans_b=True,
               out_shape=(jax.ShapeDtypeStruct((t, k), out_dtype),),
               out_specs=(pl.BlockSpec((tm, k), lambda i, j, kk: (i, 0)),),
               epilogue=_store_bf16_epilogue if out_dtype == BF16 else _store_epilogue)[0]


def _mm_tn(name, a, b, b_pre=None):
    t, m = a.shape
    n = b.shape[1]
    tk = min(TOKEN_TILE, t)
    return _mm(name, (1, 1, t // tk), a, pl.BlockSpec((tk, m), lambda i, j, kk: (kk, 0)),
               b, pl.BlockSpec((tk, n), lambda i, j, kk: (kk, 0)), (m, n), trans_a=True, b_pre=b_pre,
               out_shape=(jax.ShapeDtypeStruct((m, n), BF16),),
               out_specs=(pl.BlockSpec((m, n), lambda i, j, kk: (0, 0)),), epilogue=_store_bf16_epilogue)[0]


def _local_step(x, target, small, comm):
    t = x.shape[0]
    w = dict(small)

    n1 = _rmsnorm_fwd("ffn1_norm", x, w["ffn1_norm"])
    w.update(comm.weights("A", n1))
    (x1, hm), ffn1_saved = _ffn_fwd("ffn1", x, n1, w["ffn1_w_in"], w["ffn1_w_out"], comm.tokens,
                                    next_gain=w["mix_norm"])
    w.update(comm.weights("B", x1))
    tm = min(ROW_TILE, t)
    p = _mm("mix_in", (N_CHIPS, t // tm, 1),
            hm, pl.BlockSpec((tm, D_MODEL), lambda j, i, kk: (i, 0)),
            w["w_in"], pl.BlockSpec((None, D_MODEL, SHARD_W), lambda j, i, kk: (j, 0, 0)), (tm, SHARD_W),
            tokens=comm.tokens,
            out_shape=(jax.ShapeDtypeStruct((t, IN_W), BF16),),
            out_specs=(pl.BlockSpec((tm, SHARD_W), lambda j, i, kk: (i, j)),),
            epilogue=_store_bf16_epilogue)[0]

    z3, z1 = _conv_fwd(p, w["conv_dw_kernel"], w["conv_dw_bias"], w["conv_ln_g"], w["conv_ln_b"])

    onehot = _bucket_onehot()
    bias = _bias_table(w["rel_bias"].T, onehot).reshape(N_KV_HEADS, QROWS, 2 * BLOCK)
    sink_rows = jnp.repeat(w["attn_sinks"].reshape(N_KV_HEADS, GROUP), BLOCK, axis=1)[..., None]
    gq_wide = jnp.tile(w["q_norm"], (1, N_KV_HEADS))
    gk_wide = jnp.tile(w["k_norm"], (1, N_KV_HEADS))
    o, probs, sink_probs = _attn_fwd(p, gq_wide, gk_wide, sink_rows, bias)

    merged, a, b = _merge_fwd(z3, o, p, w["conv_w_proj"], w["attn_w_o"])
    x2, n2 = _mm_residual("mix_out", merged, w["w_out"], x1, 1.0, next_gain=w["ffn2_norm"])
    w.update(comm.weights("C", n2))
    (dy, loss), ffn2_saved = _ffn_fwd("ffn2", x2, n2, w["ffn2_w_in"], w["ffn2_w_out"], loss_target=target)

    g, big = {}, {}
    dres2, big["ffn2_w_in"], big["ffn2_w_out"], g["ffn2_norm"] = _ffn_bwd(
        "ffn2b", dy, x2, w["ffn2_norm"], ffn2_saved, w["ffn2_w_in"], w["ffn2_w_out"])
    tokens = comm.reduce_start("R1", big)

    da, db, dgates = _merge_bwd(dres2, w["w_out"], a, b, p, tokens)
    big = {}
    big["w_out"] = _mm_tn("d_w_out", merged, dres2, b_pre=_to_bf16)
    big["conv_w_proj"] = _mm_tn("d_w_proj", z3, da)
    big["attn_w_o"] = _mm_tn("d_w_o", o, db)
    dz3 = _mm_nt("d_z3", da, w["conv_w_proj"])
    do = _mm_nt("d_o", db, w["attn_w_o"])

    dq, dkv_parts, dbias, dsink, g["q_norm"], g["k_norm"] = _attn_bwd(
        p, do, w["q_norm"], w["k_norm"], probs, sink_probs)
    dkv = _kv_window_sum(dkv_parts)
    g["rel_bias"] = _bias_table_bwd(dbias.reshape(N_Q_HEADS, BLOCK * 2 * BLOCK), onehot).T
    g["attn_sinks"] = dsink.reshape(N_Q_HEADS)

    dz1, big["conv_dw_kernel"], g["conv_dw_bias"], g["conv_ln_g"], g["conv_ln_b"] = _conv_bwd_ln(
        p, z1, dz3, w["conv_ln_g"], w["conv_ln_b"])
    dp = _conv_bwd_glu(p, dz1, w["conv_dw_kernel"], dq, dkv, dgates)
    tk = min(TOKEN_TILE, t)
    big["w_in"] = _mm("d_w_in", (1, N_CHIPS, t // tk),
                    hm, pl.BlockSpec((tk, D_MODEL), lambda i, j, kk: (kk, 0)),
                    dp, pl.BlockSpec((tk, SHARD_W), lambda i, j, kk: (kk, j)), (D_MODEL, SHARD_W),
                    trans_a=True,
                    out_shape=(jax.ShapeDtypeStruct((N_CHIPS, D_MODEL, SHARD_W), BF16),),
                    out_specs=(pl.BlockSpec((None, D_MODEL, SHARD_W), lambda i, j, kk: (j, 0, 0)),),
                    epilogue=_store_bf16_epilogue)[0]
    dres1, g["mix_norm"] = _mm("d_mix", (t // tm, 1, N_CHIPS),
                               dp, pl.BlockSpec((tm, SHARD_W), lambda i, j, kk: (i, kk)),
                               w["w_in"], pl.BlockSpec((None, D_MODEL, SHARD_W), lambda i, j, kk: (kk, 0, 0)),
                               (tm, D_MODEL), trans_b=True,
                               extras=(x1, w["mix_norm"], dres2),
                               extra_specs=(pl.BlockSpec((tm, D_MODEL), lambda i, j, kk: (i, 0)),
                                            pl.BlockSpec((1, D_MODEL), lambda i, j, kk: (0, 0)),
                                            pl.BlockSpec((tm, D_MODEL), lambda i, j, kk: (i, 0))),
                               out_shape=(jax.ShapeDtypeStruct((t, D_MODEL), F32), jax.ShapeDtypeStruct((1, D_MODEL), F32)),
                               out_specs=(pl.BlockSpec((tm, D_MODEL), lambda i, j, kk: (i, 0)),
                                          pl.BlockSpec((1, D_MODEL), lambda i, j, kk: (0, 0))),
                               epilogue=_rms_bwd_epilogue, sem=("arbitrary", "arbitrary", "arbitrary"))

    comm.reduce_finish("R1", dres1)
    tokens = comm.reduce_start("R2", big)

    def ffn1_grads(dw_in4, dw_out):
        comm.reduce_finish("R2", dw_in4)
        return comm.reduce_start("R3", {"ffn1_w_in": dw_in4, "ffn1_w_out": dw_out})

    grad_x, _, _, g["ffn1_norm"] = _ffn_bwd(
        "ffn1b", dres1, x, w["ffn1_norm"], ffn1_saved, w["ffn1_w_in"], w["ffn1_w_out"], tokens, ffn1_grads)
    comm.reduce_finish("R3", grad_x)
    return loss[0, 0], grad_x, g


def _mesh_place():
    x, y, c = lax.axis_index("x"), lax.axis_index("y"), lax.axis_index("c")
    chips = [(1 - x, y), (x, 1 - y), (1 - x, 1 - y)]
    return x, y, c, chips


def _any_specs(n):
    return [pl.BlockSpec(memory_space=pl.ANY)] * n


HBM_SPEC = pl.BlockSpec(memory_space=pltpu.HBM)
SEM_SPEC = pl.BlockSpec(memory_space=pltpu.SEMAPHORE)
EFFECT = pltpu.SideEffectType.DATAFLOW_SIDE_EFFECTING


def _in_hbm(a):
    return pltpu.with_memory_space_constraint(a, pltpu.HBM)


def _copy_start(name, srcs, lands, plan, after=()):
    ns, nb = len(srcs), len(lands)
    n = plan.copies_per_source * ns

    def body(*refs):
        s_refs, l_refs = refs[:ns], refs[ns:ns + nb]
        send_sems, recv_sems = refs[ns + nb + len(after)], refs[ns + nb + len(after) + 1]
        token = refs[-1]
        for k, (src, dst, to, _) in enumerate(plan(s_refs, l_refs)):
            pltpu.make_async_remote_copy(src_ref=src, dst_ref=dst, send_sem=send_sems.at[k], recv_sem=recv_sems.at[k],
                                         device_id=to, device_id_type=MESH).start()
        token[...] = jnp.zeros_like(token)

    bufs = list(srcs) + list(lands)
    outs = pl.pallas_call(
        body, name=name,
        out_shape=(pltpu.SemaphoreType.DMA((n,)), pltpu.SemaphoreType.DMA((n,)),
                   *[pltpu.HBM(a.shape, a.dtype) for a in bufs], jax.ShapeDtypeStruct((8, LANES), F32)),
        in_specs=[HBM_SPEC] * len(bufs) + [pl.BlockSpec(memory_space=pl.ANY)] * len(after),
        out_specs=(SEM_SPEC, SEM_SPEC, *[HBM_SPEC] * len(bufs), pl.BlockSpec(memory_space=pltpu.VMEM)),
        input_output_aliases={i: 2 + i for i in range(len(bufs))},
        compiler_params=pltpu.CompilerParams(has_side_effects=EFFECT),
    )(*[_in_hbm(a) for a in bufs], *after)
    return outs[0], outs[1], list(outs[2:2 + ns]), list(outs[2 + ns:2 + ns + nb]), outs[-1]


def _copy_wait(name, send_sems, recv_sems, srcs, lands, after, plan):
    ns, nb = len(srcs), len(lands)

    def body(*refs):
        s_refs, l_refs = refs[:ns], refs[ns:ns + nb]
        send_sems, recv_sems = refs[ns + nb], refs[ns + nb + 1]
        for k, (src, _, to, mine) in enumerate(plan(s_refs, l_refs)):
            cp = pltpu.make_async_remote_copy(src_ref=src, dst_ref=mine, send_sem=send_sems.at[k], recv_sem=recv_sems.at[k],
                                              device_id=to, device_id_type=MESH)
            cp.wait_send()
            cp.wait_recv()

    bufs = list(srcs) + list(lands)
    outs = pl.pallas_call(
        body, name=name,
        out_shape=tuple(pltpu.HBM(a.shape, a.dtype) for a in bufs),
        in_specs=[HBM_SPEC] * len(bufs) + [SEM_SPEC, SEM_SPEC, pl.BlockSpec(memory_space=pl.ANY)],
        out_specs=tuple([HBM_SPEC] * len(bufs)),
        input_output_aliases={i: i for i in range(len(bufs))},
        compiler_params=pltpu.CompilerParams(has_side_effects=EFFECT),
    )(*bufs, send_sems, recv_sems, after)
    return list(outs[:ns]), list(outs[ns:])


def _gather_plan(s_refs, l_refs):
    x, y, c, chips = _mesh_place()
    jme = 2 * x + y
    return [(s.at[c], land.at[jme, c], (*chip, c), land.at[2 * chip[0] + chip[1], c])
            for s, land in zip(s_refs, l_refs) for chip in chips]


_gather_plan.copies_per_source = 3


def _gather_both_cores_plan(s_refs, l_refs):
    x, y, c, chips = _mesh_place()
    jme = 2 * x + y
    plan = []
    for s, land in zip(s_refs, l_refs):
        for chip in chips:
            for peer_core in (c, 1 - c):
                plan.append((s.at[c], land.at[jme, c], (*chip, peer_core), land.at[2 * chip[0] + chip[1], peer_core]))
        plan.append((s, land.at[jme], (x, y, 1 - c), land.at[jme]))
    return plan


_gather_both_cores_plan.copies_per_source = 7


def _scatter_plan(s_refs, l_refs):
    x, y, c, chips = _mesh_place()
    return [(s.at[2 * chip[0] + chip[1]], land.at[k], (*chip, c), land.at[k])
            for s, land in zip(s_refs, l_refs) for k, chip in enumerate(chips)]


_scatter_plan.copies_per_source = 3


def _gather_forward(name, shards, landed):
    nw = len(shards)

    def body(*refs):
        s_refs, o_refs = refs[:nw], refs[2 * nw:3 * nw]
        send_sems, recv_sems = refs[3 * nw:]
        x, y, c, chips = _mesh_place()
        me, sib, jme = (x, y, c), (x, y, 1 - c), 2 * x + y
        sent = []
        for w in range(nw):
            parts = [(o_refs[w].at[2 * chip[0] + chip[1], c], o_refs[w].at[2 * chip[0] + chip[1], c]) for chip in chips]
            parts.append((s_refs[w], o_refs[w].at[jme]))
            for k, (src, dst) in enumerate(parts):
                cp = pltpu.make_async_remote_copy(src_ref=src, dst_ref=dst, send_sem=send_sems.at[4 * w + k],
                                                  recv_sem=recv_sems.at[4 * w + k], device_id=sib, device_id_type=MESH)
                cp.start()
                sent.append(cp)
        for w in range(nw):
            parts = [o_refs[w].at[2 * chip[0] + chip[1], 1 - c] for chip in chips] + [o_refs[w].at[jme]]
            for k, part in enumerate(parts):
                pltpu.make_async_remote_copy(src_ref=part, dst_ref=part, send_sem=send_sems.at[4 * w + k],
                                             recv_sem=recv_sems.at[4 * w + k], device_id=me, device_id_type=MESH).wait_recv()
        for cp in sent:
            cp.wait_send()

    return pl.pallas_call(
        body, name=name,
        in_specs=_any_specs(2 * nw), out_specs=_any_specs(nw),
        out_shape=[jax.ShapeDtypeStruct(a.shape, a.dtype) for a in landed],
        input_output_aliases={nw + i: i for i in range(nw)},
        scratch_shapes=[pltpu.SemaphoreType.DMA((4 * nw,)), pltpu.SemaphoreType.DMA((4 * nw,))],
    )(*shards, *landed)


def _exchange_halves(name, grads, after=()):
    nw = len(grads)

    def body(*refs):
        g_refs, o_refs = refs[:nw], refs[nw + len(after):2 * nw + len(after)]
        send_sems, recv_sems = refs[2 * nw + len(after):]
        x, y, c, _ = _mesh_place()
        copies = []
        for w in range(nw):
            cp = pltpu.make_async_remote_copy(src_ref=g_refs[w].at[:, 1 - c], dst_ref=o_refs[w], send_sem=send_sems.at[w],
                                              recv_sem=recv_sems.at[w], device_id=(x, y, 1 - c), device_id_type=MESH)
            cp.start()
            copies.append(cp)
        for cp in copies:
            cp.wait()

    return pl.pallas_call(
        body, name=name,
        in_specs=_any_specs(nw + len(after)), out_specs=_any_specs(nw),
        out_shape=[jax.ShapeDtypeStruct((N_CHIPS,) + g.shape[2:], g.dtype) for g in grads],
        scratch_shapes=[pltpu.SemaphoreType.DMA((nw,)), pltpu.SemaphoreType.DMA((nw,))],
    )(*grads, *after)


def _row_tile(r):
    for cand in (256, 176, 128, 64, 32, 16, 8):
        if r % cand == 0:
            return cand
    return r


def _add_own_half(c_idx, grad, got):
    _, _, r, cols = grad.shape
    tr = _row_tile(r)

    def body(c_ref, g_ref, o_ref, out_ref):
        out_ref[...] = (g_ref[...].astype(F32) + o_ref[...].astype(F32)).astype(BF16)

    return pl.pallas_call(
        body, name="add_own_half",
        grid_spec=pltpu.PrefetchScalarGridSpec(
            num_scalar_prefetch=1, grid=(N_CHIPS, r // tr),
            in_specs=[pl.BlockSpec((None, None, tr, cols), lambda j, i, c_ref: (j, c_ref[0], i, 0)),
                      pl.BlockSpec((None, tr, cols), lambda j, i, c_ref: (j, i, 0))],
            out_specs=pl.BlockSpec((None, tr, cols), lambda j, i, c_ref: (j, i, 0))),
        out_shape=jax.ShapeDtypeStruct((N_CHIPS, r, cols), BF16),
        compiler_params=_params(("parallel", "parallel")),
    )(c_idx, grad, got)


def _sum_pieces(place_idx, sums, landed):
    _, r, cols = sums.shape
    tr = _row_tile(r)

    def body(j_ref, own_ref, p_ref, o_ref):
        o_ref[...] = ((own_ref[...].astype(F32) + p_ref[0].astype(F32)) + p_ref[1].astype(F32)) + p_ref[2].astype(F32)

    return pl.pallas_call(
        body, name="sum_pieces",
        grid_spec=pltpu.PrefetchScalarGridSpec(
            num_scalar_prefetch=1, grid=(r // tr,),
            in_specs=[pl.BlockSpec((None, tr, cols), lambda i, j_ref: (j_ref[0], i, 0)),
                      pl.BlockSpec((N_CHIPS - 1, tr, cols), lambda i, j_ref: (0, i, 0))],
            out_specs=pl.BlockSpec((None, tr, cols), lambda i, j_ref: (j_ref[1], i, 0))),
        out_shape=jax.ShapeDtypeStruct((2, r, cols), F32),
        compiler_params=_params(("parallel",)),
    )(place_idx, sums, landed)


def _join_halves(name, halves):
    nw = len(halves)

    def body(*refs):
        o_refs = refs[nw:2 * nw]
        send_sems, recv_sems = refs[2 * nw:]
        x, y, c, _ = _mesh_place()
        copies = []
        for w in range(nw):
            cp = pltpu.make_async_remote_copy(src_ref=o_refs[w].at[c], dst_ref=o_refs[w].at[c], send_sem=send_sems.at[w],
                                              recv_sem=recv_sems.at[w], device_id=(x, y, 1 - c), device_id_type=MESH)
            cp.start()
            copies.append(cp)
        for w in range(nw):
            copies[w].wait_send()
            landed = o_refs[w].at[1 - c]
            pltpu.make_async_remote_copy(src_ref=landed, dst_ref=landed, send_sem=send_sems.at[w], recv_sem=recv_sems.at[w],
                                         device_id=(x, y, c), device_id_type=MESH).wait_recv()

    return pl.pallas_call(
        body, name=name,
        in_specs=_any_specs(nw), out_specs=_any_specs(nw),
        out_shape=[jax.ShapeDtypeStruct(h.shape, F32) for h in halves],
        input_output_aliases={i: i for i in range(nw)},
        scratch_shapes=[pltpu.SemaphoreType.DMA((nw,)), pltpu.SemaphoreType.DMA((nw,))],
    )(*halves)


SMALL_ROWS = 8


def _all_reduce_small(pack):
    rows, cols = pack.shape
    n_dev = 8

    def body(p_ref, o_ref, slots, send_sems, recv_sems):
        x, y, c, _ = _mesh_place()
        me = 4 * x + 2 * y + c
        slots[me] = p_ref[...]
        copies = []
        for k in range(1, n_dev):
            peer = (me + k) % n_dev
            cp = pltpu.make_async_remote_copy(src_ref=p_ref, dst_ref=slots.at[me], send_sem=send_sems.at[k],
                                              recv_sem=recv_sems.at[k],
                                              device_id=(peer // 4, (peer // 2) % 2, peer % 2), device_id_type=MESH)
            cp.start()
            copies.append(cp)
        for k in range(1, n_dev):
            src = (me + n_dev - k) % n_dev
            pltpu.make_async_remote_copy(src_ref=p_ref, dst_ref=slots.at[src], send_sem=send_sems.at[k],
                                         recv_sem=recv_sems.at[k], device_id=(x, y, c), device_id_type=MESH).wait_recv()
        for cp in copies:
            cp.wait_send()
        total = slots[0]
        for s in range(1, n_dev):
            total = total + slots[s]
        o_ref[...] = total

    return pl.pallas_call(
        body, name="all_reduce_small",
        in_specs=[pl.BlockSpec(memory_space=pltpu.VMEM)], out_specs=pl.BlockSpec(memory_space=pltpu.VMEM),
        out_shape=jax.ShapeDtypeStruct((rows, cols), F32),
        scratch_shapes=[pltpu.VMEM((n_dev, rows, cols), F32), pltpu.SemaphoreType.DMA((n_dev,)),
                        pltpu.SemaphoreType.DMA((n_dev,))],
    )(pack)


def _adamw(name, w, g, m, v):
    r, cols = w.shape
    tr = _row_tile(r)

    def body(w_ref, g_ref, m_ref, v_ref, d_ref, nm_ref, nv_ref):
        gv = g_ref[...]
        nm = ADAM_B1 * m_ref[...] + (1.0 - ADAM_B1) * gv
        nv = ADAM_B2 * v_ref[...] + (1.0 - ADAM_B2) * (gv * gv)
        m_hat = nm / (1.0 - ADAM_B1 ** ADAM_STEP)
        v_hat = nv / (1.0 - ADAM_B2 ** ADAM_STEP)
        d_ref[...] = -ADAM_LR * (m_hat / (jnp.sqrt(v_hat) + ADAM_EPS) + ADAM_WD * w_ref[...])
        nm_ref[...] = nm
        nv_ref[...] = nv

    spec = pl.BlockSpec((tr, cols), lambda i: (i, 0))
    return pl.pallas_call(
        body, name=name, grid=(r // tr,),
        in_specs=[spec] * 4, out_specs=[spec] * 3,
        out_shape=[jax.ShapeDtypeStruct((r, cols), F32)] * 3,
        compiler_params=_params(("parallel",)),
    )(w, g, m, v)


BIG = ["ffn1_w_in", "ffn1_w_out", "w_in", "conv_w_proj", "attn_w_o", "w_out", "ffn2_w_in", "ffn2_w_out", "conv_dw_kernel"]
COL_SHARDED = ("ffn1_w_in", "w_in", "ffn2_w_in")
SMALL = ["ffn1_norm", "mix_norm", "ffn2_norm", "conv_dw_bias", "conv_ln_g", "conv_ln_b", "q_norm", "k_norm", "attn_sinks", "rel_bias"]
WEIGHTS = ["ffn1_norm", "ffn1_w_in", "ffn1_w_out", "mix_norm", "w_in", "conv_dw_kernel", "conv_dw_bias", "conv_ln_g",
           "conv_ln_b", "conv_w_proj", "q_norm", "k_norm", "attn_sinks", "rel_bias", "attn_w_o", "w_out", "ffn2_norm",
           "ffn2_w_in", "ffn2_w_out"]
SMALL_PLACE = {"ffn1_norm": (0, 0, 1024), "mix_norm": (1, 0, 1024), "ffn2_norm": (2, 0, 1024), "conv_dw_bias": (3, 0, 1024),
               "conv_ln_g": (4, 0, 1024), "conv_ln_b": (5, 0, 1024), "q_norm": (6, 0, 64), "k_norm": (6, 128, 64),
               "attn_sinks": (6, 256, 16), "rel_bias": (7, 0, 512)}
LOSS_PLACE = (6, 384)


def _pack_small(vals, fill=0.0, loss=None):
    pack = jnp.full((SMALL_ROWS, D_MODEL), fill, F32)
    for name, (row, lane, n) in SMALL_PLACE.items():
        pack = pack.at[row, lane:lane + n].set(vals[name].reshape(n))
    if loss is not None:
        pack = pack.at[LOSS_PLACE[0], LOSS_PLACE[1]].set(loss)
    return pack


def _unpack_small(pack, shapes):
    return {name: pack[row, lane:lane + n].reshape(shapes[name]) for name, (row, lane, n) in SMALL_PLACE.items()}


def _shard_halves(name, a):
    if name == "conv_dw_kernel":
        a = jnp.pad(a, ((0, CONV_PAD - CONV_WIDTH), (0, 0)))
    r, cols = a.shape
    return a.reshape(2, r // 2, cols)


GATHER_GROUPS = {"A": ["ffn1_w_in", "ffn1_w_out"],
                 "B": ["w_in", "conv_dw_kernel", "conv_w_proj", "attn_w_o", "w_out"],
                 "C": ["ffn2_w_in", "ffn2_w_out"]}


class _MeshComm:
    def __init__(self, wts):
        self.c_idx = lax.axis_index("c").astype(jnp.int32).reshape(1)
        self.place_idx = jnp.stack([2 * lax.axis_index("x") + lax.axis_index("y"), lax.axis_index("c")]).astype(jnp.int32)
        self.wts, self.gathers, self.reductions, self.reduced = wts, {}, {}, {}
        self.tokens, self.last_join = (), ()
        self._gather_start("A", ())

    def _gather_start(self, group, after):
        names = GATHER_GROUPS[group]
        shards = [_shard_halves(n, self.wts[n]) if n == "conv_dw_kernel" else _shard_halves(n, self.wts[n]).astype(BF16)
                  for n in names]
        lands = [lax.empty((N_CHIPS,) + s.shape, s.dtype) for s in shards]
        self.gathers[group] = _copy_start("gather_start_" + group, shards, lands, self._gather_plan(group), after=after)
        self.tokens = (self.gathers[group][-1],)

    @staticmethod
    def _gather_plan(group):
        return _gather_both_cores_plan if group == "C" else _gather_plan

    def weights(self, group, after):
        send_sems, recv_sems, shards, lands, token = self.gathers.pop(group)
        shards, lands = _copy_wait("gather_wait_" + group, send_sems, recv_sems, shards, lands,
                                   token if after is None else after, self._gather_plan(group))
        gathered = lands if group == "C" else _gather_forward("gather_forward_" + group, shards, lands)
        self.tokens = ()
        following = {"A": "B", "B": "C"}.get(group)
        if following:
            self._gather_start(following, (gathered[0],))
        out = {}
        for n, g4 in zip(GATHER_GROUPS[group], gathered):
            r, cols = g4.shape[2] * 2, g4.shape[3]
            if n in COL_SHARDED:
                out[n] = g4.reshape(N_CHIPS, r, cols)
            elif n == "conv_dw_kernel":
                out[n] = g4.reshape(N_CHIPS, r, cols).transpose(1, 0, 2).reshape(r, N_CHIPS * cols)
            else:
                out[n] = g4.reshape(N_CHIPS * r, cols)
        return out

    def reduce_start(self, group, grads):
        names = list(grads)
        g4 = []
        for n in names:
            a = grads[n]
            if n == "conv_dw_kernel":
                a = a.reshape(CONV_PAD, N_CHIPS, -1).transpose(1, 0, 2)
            elif n not in COL_SHARDED:
                a = a.reshape(N_CHIPS, a.shape[0] // N_CHIPS, a.shape[1])
            g4.append(a.reshape(N_CHIPS, 2, a.shape[1] // 2, a.shape[2]))
        got = _exchange_halves("exchange_halves_" + group, g4, after=self.last_join)
        sums = [_add_own_half(self.c_idx, a, b) for a, b in zip(g4, got)]
        lands = [lax.empty((N_CHIPS - 1,) + s.shape[1:], s.dtype) for s in sums]
        started = _copy_start("scatter_start_" + group, sums, lands, _scatter_plan, after=self.last_join)
        self.reductions[group] = (names,) + started
        return (started[-1],)

    def reduce_finish(self, group, after):
        names, send_sems, recv_sems, sums, lands, _ = self.reductions.pop(group)
        sums, lands = _copy_wait("scatter_wait_" + group, send_sems, recv_sems, sums, lands, after, _scatter_plan)
        halves = [_sum_pieces(self.place_idx, s, p) for s, p in zip(sums, lands)]
        joined = _join_halves("join_halves_" + group, halves)
        self.last_join = (joined[0],)
        self.reduced.update(zip(names, joined))


def kernel(x, ffn1_norm, ffn1_w_in, ffn1_w_out, mix_norm, w_in, conv_dw_kernel, conv_dw_bias, conv_ln_g, conv_ln_b, conv_w_proj, q_norm, k_norm, attn_sinks, rel_bias, attn_w_o, w_out, ffn2_norm, ffn2_w_in, ffn2_w_out, loss_target, m_ffn1_norm, m_ffn1_w_in, m_ffn1_w_out, m_mix_norm, m_w_in, m_conv_dw_kernel, m_conv_dw_bias, m_conv_ln_g, m_conv_ln_b, m_conv_w_proj, m_q_norm, m_k_norm, m_attn_sinks, m_rel_bias, m_attn_w_o, m_w_out, m_ffn2_norm, m_ffn2_w_in, m_ffn2_w_out, v_ffn1_norm, v_ffn1_w_in, v_ffn1_w_out, v_mix_norm, v_w_in, v_conv_dw_kernel, v_conv_dw_bias, v_conv_ln_g, v_conv_ln_b, v_conv_w_proj, v_q_norm, v_k_norm, v_attn_sinks, v_rel_bias, v_attn_w_o, v_w_out, v_ffn2_norm, v_ffn2_w_in, v_ffn2_w_out):
    args = dict(locals())
    wts = {n: args[n] for n in WEIGHTS}
    mom = {n: args["m_" + n] for n in WEIGHTS}
    var = {n: args["v_" + n] for n in WEIGHTS}
    comm = _MeshComm(wts)
    small = {n: wts[n] if n in ("attn_sinks", "rel_bias") else wts[n].reshape(1, -1) for n in SMALL}
    loss_part, grad_x, g = _local_step(x[0], loss_target[0], small, comm)

    small_sum = _all_reduce_small(_pack_small(g, loss=loss_part))
    loss = small_sum[LOSS_PLACE[0], LOSS_PLACE[1]]
    small_shapes = {n: wts[n].shape for n in SMALL}
    g_small = _unpack_small(small_sum, small_shapes)

    grads, delta, new_m, new_v = {}, {}, {}, {}
    for n in BIG:
        j = comm.reduced[n]
        gs = j.reshape(j.shape[1] * 2, j.shape[2])
        pad = n == "conv_dw_kernel"
        ws, ms, vs = (_shard_halves(n, a).reshape(gs.shape) for a in (wts[n], mom[n], var[n]))
        d, nm, nv = _adamw("adamw_" + n, ws, gs, ms, vs)
        cut = (lambda a: a[:CONV_WIDTH]) if pad else (lambda a: a)
        grads[n], delta[n], new_m[n], new_v[n] = cut(gs), cut(d), cut(nm), cut(nv)
    d, nm, nv = _adamw("adamw_small", _pack_small(wts), small_sum, _pack_small(mom), _pack_small(var, fill=1.0))
    grads.update(g_small)
    delta.update(_unpack_small(d, small_shapes))
    new_m.update(_unpack_small(nm, small_shapes))
    new_v.update(_unpack_small(nv, small_shapes))

    return (loss, grad_x[None], *[grads[n] for n in WEIGHTS], *[delta[n] for n in WEIGHTS],
            *[new_m[n] for n in WEIGHTS], *[new_v[n] for n in WEIGHTS])
```

```python
import functools
import math

import jax
import jax.numpy as jnp
from jax import lax
from jax.experimental import pallas as pl
from jax.experimental.pallas import tpu as pltpu

F32 = jnp.float32
BF16 = jnp.bfloat16
MESH = pl.DeviceIdType.MESH

EPS = 1e-6
D_MODEL = 1024
D_FF = 2816
N_CHIPS = 4
SHARD_W = 2 * D_FF // N_CHIPS
HEAD_DIM = 64
N_Q_HEADS = 16
N_KV_HEADS = 4
GROUP = N_Q_HEADS // N_KV_HEADS
BLOCK = 128
QROWS = GROUP * BLOCK
N_BUCKETS = 32
MAX_DISTANCE = 128
CONV_WIDTH = 31
CONV_PAD = 32
NEG = float(jnp.finfo(jnp.float32).min)

ADAM_LR = 0.001
ADAM_B1 = 0.9
ADAM_B2 = 0.999
ADAM_EPS = 1e-08
ADAM_WD = 0.01
ADAM_STEP = 10

VMEM_LIMIT_BYTES = 56 * 1024 * 1024
ROW_TILE = 1024
TOKEN_TILE = 1024
CONV_TILE = 256
CONV_ROWS = 128
LANES = 128

COL_CONV_A, COL_CONV_G, COL_Q, COL_K, COL_V, COL_GC, COL_GA = 0, 1024, 2048, 3072, 3328, 3584, 4608
IN_W = 5632


def _params(sem, vmem=VMEM_LIMIT_BYTES):
    return pltpu.CompilerParams(dimension_semantics=sem, vmem_limit_bytes=vmem)


def _sigmoid(x):
    return 1.0 / (1.0 + jnp.exp(-x))


def _dot(a, b, trans_a=False, trans_b=False, precision=None):
    dn = (((0,) if trans_a else (1,), (1,) if trans_b else (0,)), ((), ()))
    return lax.dot_general(a, b, dn, preferred_element_type=F32, precision=precision)


def _mm(name, grid, a, a_spec, b, b_spec, acc_shape, *, trans_a=False, trans_b=False, a_pre=None, b_pre=None,
        extras=(), extra_specs=(), tokens=(), out_shape, out_specs, epilogue, chunked=False,
        sem=("parallel", "parallel", "arbitrary")):
    n_k = grid[2]
    assert not chunked or (n_k == 1 and b_pre is None)
    extras = tuple(extras) + tuple(tokens)
    extra_specs = tuple(extra_specs) + (pl.BlockSpec((8, LANES), lambda i, j, kk: (0, 0)),) * len(tokens)
    n_extra = len(extras)
    n_out = len(out_shape)

    def body(a_ref, b_ref, *rest):
        ex = rest[:n_extra]
        outs = rest[n_extra:n_extra + n_out]
        ids = (pl.program_id(0), pl.program_id(1), pl.program_id(2))
        av = a_ref[...]
        if a_pre is not None:
            av = a_pre(av)
        if chunked:
            for c0, cw in _col_chunks(acc_shape[1]):
                cols = pl.ds(c0, cw)
                epilogue(_dot(av, b_ref[cols, :] if trans_b else b_ref[:, cols], trans_a, trans_b), ex, outs, ids, cols)
            return
        bv = b_ref[...]
        if b_pre is not None:
            bv = b_pre(bv)
        if n_k == 1:
            epilogue(_dot(av, bv, trans_a, trans_b), ex, outs, ids)
        else:
            acc = rest[-1]

            @pl.when(ids[2] == 0)
            def _():
                acc[...] = jnp.zeros_like(acc)

            acc[...] += _dot(av, bv, trans_a, trans_b)

            @pl.when(ids[2] == n_k - 1)
            def _():
                epilogue(acc[...], ex, outs, ids)

    scratch = [] if n_k == 1 else [pltpu.VMEM(acc_shape, F32)]
    return pl.pallas_call(
        body, name=name, grid=grid,
        in_specs=[a_spec, b_spec, *extra_specs],
        out_specs=list(out_specs), out_shape=list(out_shape),
        scratch_shapes=scratch, compiler_params=_params(sem),
    )(a, b, *extras)


MXU_WIDTH = 256


def _col_chunks(n, width=2 * MXU_WIDTH):
    return [(c0, min(width, n - c0)) for c0 in range(0, n, width)]


def _half_bf16(v):
    return (0.5 * v).astype(BF16)


def _to_bf16(v):
    return v.astype(BF16)


def _rmsnorm_fwd(name, x, g, tokens=()):
    t, d = x.shape
    tm = min(ROW_TILE, t)

    def body(x_ref, g_ref, *rest):
        o_ref = rest[-1]
        xv = x_ref[...]
        r = lax.rsqrt(jnp.mean(xv * xv, axis=-1, keepdims=True) + EPS)
        o_ref[...] = (xv * r * g_ref[...]).astype(BF16)

    return pl.pallas_call(
        body, name=name, grid=(t // tm,),
        in_specs=[pl.BlockSpec((tm, d), lambda i: (i, 0)), pl.BlockSpec((1, d), lambda i: (0, 0))]
        + [pl.BlockSpec((8, LANES), lambda i: (0, 0))] * len(tokens),
        out_specs=pl.BlockSpec((tm, d), lambda i: (i, 0)),
        out_shape=jax.ShapeDtypeStruct((t, d), BF16),
        compiler_params=_params(("parallel",)),
    )(x, g, *tokens)


def _rms_bwd_epilogue(acc, ex, outs, ids):
    x_ref, g_ref, dres_ref = ex[:3]
    out_ref, dg_ref = outs
    xv = x_ref[...]
    r = lax.rsqrt(jnp.mean(xv * xv, axis=-1, keepdims=True) + EPS)
    w = acc * g_ref[...]
    dx = r * w - xv * (r * r * r) * jnp.mean(xv * w, axis=-1, keepdims=True)
    out_ref[...] = dres_ref[...] + dx
    part = jnp.sum(acc * (xv * r), axis=0, keepdims=True)

    @pl.when(ids[0] == 0)
    def _():
        dg_ref[...] = part

    @pl.when(ids[0] > 0)
    def _():
        dg_ref[...] += part


def _ffn_in(name, n, w_in4, tokens=()):
    t, d = n.shape
    tm = min(ROW_TILE, t)

    def body(n_ref, wa_ref, wb_ref, *rest):
        ab_ref, h_ref = rest[-2:]
        nv = n_ref[...]
        for c0, cw in _col_chunks(SHARD_W):
            cols = pl.ds(c0, cw)
            a = _dot(nv, wa_ref[:, cols])
            b = _dot(nv, wb_ref[:, cols])
            h_ref[:, cols] = (a * _sigmoid(a) * b).astype(BF16)
            ab_ref[0, :, cols] = a.astype(BF16)
            ab_ref[1, :, cols] = b.astype(BF16)

    return pl.pallas_call(
        body, name=name, grid=(2, t // tm),
        in_specs=[pl.BlockSpec((tm, d), lambda j, i: (i, 0)),
                  pl.BlockSpec((None, d, SHARD_W), lambda j, i: (j, 0, 0)),
                  pl.BlockSpec((None, d, SHARD_W), lambda j, i: (j + 2, 0, 0))]
        + [pl.BlockSpec((8, LANES), lambda j, i: (0, 0))] * len(tokens),
        out_specs=[pl.BlockSpec((2, tm, SHARD_W), lambda j, i: (0, i, j)),
                   pl.BlockSpec((tm, SHARD_W), lambda j, i: (i, j))],
        out_shape=[jax.ShapeDtypeStruct((2, t, D_FF), BF16), jax.ShapeDtypeStruct((t, D_FF), BF16)],
        compiler_params=_params(("parallel", "parallel")),
    )(n, w_in4, w_in4, *tokens)


def _mm_residual(name, a, w, res, scale, next_gain=None, loss_target=None):
    t, k = a.shape
    n = w.shape[1]
    tm = min(ROW_TILE, t)
    row = pl.BlockSpec((tm, n), lambda i, j, kk: (i, 0))
    extras, specs = [res], [row]
    shapes, out_specs = [jax.ShapeDtypeStruct((t, n), F32)], [row]
    if next_gain is not None:
        extras.append(next_gain)
        specs.append(pl.BlockSpec((1, n), lambda i, j, kk: (0, 0)))
        shapes.append(jax.ShapeDtypeStruct((t, n), BF16))
        out_specs.append(row)
    if loss_target is not None:
        extras.append(loss_target)
        specs.append(row)
        shapes.append(jax.ShapeDtypeStruct((8, LANES), F32))
        out_specs.append(pl.BlockSpec((8, LANES), lambda i, j, kk: (0, 0)))

    def epilogue(acc, ex, outs, ids):
        y = ex[0][...] + scale * acc
        if loss_target is None:
            outs[0][...] = y
        if next_gain is not None:
            r = lax.rsqrt(jnp.mean(y * y, axis=-1, keepdims=True) + EPS)
            outs[1][...] = (y * r * ex[1][...]).astype(BF16)
        if loss_target is not None:
            diff = y - ex[1][...]
            outs[0][...] = diff * (1.0 / n)
            part = jnp.full((8, LANES), 0.5 / n * jnp.sum(diff * diff), F32)

            @pl.when(ids[0] == 0)
            def _():
                outs[1][...] = part

            @pl.when(ids[0] > 0)
            def _():
                outs[1][...] += part

    sem = ("parallel" if loss_target is None else "arbitrary", "parallel", "arbitrary")
    out = _mm(name, (t // tm, 1, 1), a, pl.BlockSpec((tm, k), lambda i, j, kk: (i, 0)),
              w, pl.BlockSpec((k, n), lambda i, j, kk: (0, 0)), (tm, n),
              extras=extras, extra_specs=specs, out_shape=shapes, out_specs=out_specs, epilogue=epilogue, sem=sem)
    return out[0] if len(out) == 1 else tuple(out)


def _ffn_fwd(tag, x, n, w_in4, w_out, tokens=(), **tail):
    ab, h = _ffn_in(tag + "_in", n, w_in4, tokens)
    y = _mm_residual(tag + "_out", h, w_out, x, 0.5, **tail)
    return y, (n, ab, h)


def _ffn_bwd(tag, dres, x, g, saved, w_in4, w_out, tokens=(), on_first=None, on_weight_grads=None):
    n, ab, h = saved
    t, d = x.shape
    tm = min(ROW_TILE, t)
    tk = min(TOKEN_TILE, t)
    half_w = SHARD_W

    def dact_epilogue(acc, ex, outs, ids, cols):
        a = ex[0][0, :, cols].astype(F32)
        b = ex[0][1, :, cols].astype(F32)
        sig = _sigmoid(a)
        outs[0][0, :, cols] = (acc * b * (sig * (1.0 + a * (1.0 - sig)))).astype(BF16)
        outs[0][1, :, cols] = (acc * (a * sig)).astype(BF16)

    du = _mm(tag + "_dact", (2, t // tm, 1),
             dres, pl.BlockSpec((tm, d), lambda j, i, kk: (i, 0)),
             w_out, pl.BlockSpec((half_w, d), lambda j, i, kk: (j, 0)), (tm, half_w),
             trans_b=True, a_pre=_half_bf16,
             extras=(ab,), extra_specs=(pl.BlockSpec((2, tm, half_w), lambda j, i, kk: (0, i, j)),), tokens=tokens,
             out_shape=(jax.ShapeDtypeStruct((2, t, D_FF), BF16),),
             out_specs=(pl.BlockSpec((2, tm, half_w), lambda j, i, kk: (0, i, j)),),
             epilogue=dact_epilogue, chunked=True)[0]

    def store_epilogue(acc, ex, outs, ids):
        outs[0][...] = acc.astype(BF16)

    early = () if on_first is None else on_first(du)

    dw_out = _mm(tag + "_dwout", (2, 1, t // tk),
                 h, pl.BlockSpec((tk, half_w), lambda i, j, kk: (kk, i)),
                 dres, pl.BlockSpec((tk, d), lambda i, j, kk: (kk, 0)), (half_w, d),
                 trans_a=True, b_pre=_half_bf16, tokens=early,
                 out_shape=(jax.ShapeDtypeStruct((D_FF, d), BF16),),
                 out_specs=(pl.BlockSpec((half_w, d), lambda i, j, kk: (i, 0)),),
                 epilogue=store_epilogue)[0]

    dw_in4 = _mm(tag + "_dwin", (1, N_CHIPS, t // tk),
                 n, pl.BlockSpec((tk, d), lambda i, j, kk: (kk, 0)),
                 du, pl.BlockSpec((None, tk, SHARD_W), lambda i, j, kk: (j // 2, kk, j % 2)), (d, SHARD_W),
                 trans_a=True,
                 out_shape=(jax.ShapeDtypeStruct((N_CHIPS, d, SHARD_W), BF16),),
                 out_specs=(pl.BlockSpec((None, d, SHARD_W), lambda i, j, kk: (j, 0, 0)),),
                 epilogue=store_epilogue)[0]

    late = () if on_weight_grads is None else on_weight_grads(dw_in4, dw_out)

    dx, dg = _mm(tag + "_dn", (t // tm, 1, N_CHIPS),
                 du, pl.BlockSpec((None, tm, SHARD_W), lambda i, j, kk: (kk // 2, i, kk % 2)),
                 w_in4, pl.BlockSpec((None, d, SHARD_W), lambda i, j, kk: (kk, 0, 0)), (tm, d),
                 trans_b=True,
                 extras=(x, g, dres),
                 extra_specs=(pl.BlockSpec((tm, d), lambda i, j, kk: (i, 0)),
                              pl.BlockSpec((1, d), lambda i, j, kk: (0, 0)),
                              pl.BlockSpec((tm, d), lambda i, j, kk: (i, 0))), tokens=late,
                 out_shape=(jax.ShapeDtypeStruct((t, d), F32), jax.ShapeDtypeStruct((1, d), F32)),
                 out_specs=(pl.BlockSpec((tm, d), lambda i, j, kk: (i, 0)),
                            pl.BlockSpec((1, d), lambda i, j, kk: (0, 0))),
                 epilogue=_rms_bwd_epilogue, sem=("arbitrary", "arbitrary", "arbitrary"))
    return dx, dw_in4, dw_out, dg


def _conv_fill(zp_ref, a_ref, g_ref, ah_ref, gh_ref, i):
    zh = ah_ref[...].astype(F32) * _sigmoid(gh_ref[...].astype(F32))
    zp_ref[pl.ds(0, CONV_PAD), :] = jnp.where(i > 0, zh, 0.0)
    zp_ref[pl.ds(CONV_PAD, a_ref.shape[0]), :] = a_ref[...].astype(F32) * _sigmoid(g_ref[...].astype(F32))


def _shift_groups(shifts):
    groups = {}
    for j, s in shifts:
        groups.setdefault(s % 8, []).append((j, s // 8))
    return groups


def _windows(zp_ref, r0, lanes, groups):
    for q, taps in groups.items():
        deepest = max(p for _, p in taps)
        win = zp_ref[pl.ds(r0 + q, 8 * deepest + CONV_ROWS), lanes]
        for j, p in taps:
            yield j, win[8 * p:8 * p + CONV_ROWS]


def _conv_apply(zp_ref, out_ref, dw_ref, bias_ref, tm, ch, shifts):
    groups = _shift_groups(shifts)
    for cc in range(ch // LANES):
        lanes = pl.ds(cc * LANES, LANES)
        w = [dw_ref[pl.ds(j, 1), lanes] for j in range(CONV_WIDTH)]
        for r0 in range(0, tm, CONV_ROWS):
            if bias_ref is None:
                acc = jnp.zeros((CONV_ROWS, LANES), F32)
            else:
                acc = jnp.broadcast_to(bias_ref[:, lanes], (CONV_ROWS, LANES))
            for j, rows in _windows(zp_ref, r0, lanes, groups):
                acc = acc + w[j] * rows
            out_ref[pl.ds(r0, CONV_ROWS), lanes] = acc


FWD_SHIFTS = [(j, CONV_PAD - (CONV_WIDTH - 1) + j) for j in range(CONV_WIDTH)]
BWD_SHIFTS = [(j, CONV_WIDTH - 1 - j) for j in range(CONV_WIDTH)]


def _conv_taps(zp_ref, z1_ref, dw_ref, bias_ref, tm, ch):
    _conv_apply(zp_ref, z1_ref, dw_ref, bias_ref, tm, ch, FWD_SHIFTS)


def _conv_specs(tm, ch):
    per = tm // CONV_PAD
    cb = COL_CONV_G // ch
    return [pl.BlockSpec((tm, ch), lambda i: (i, 0)),
            pl.BlockSpec((tm, ch), lambda i: (i, cb)),
            pl.BlockSpec((CONV_PAD, ch), lambda i: (jnp.maximum(i * per - 1, 0), 0)),
            pl.BlockSpec((CONV_PAD, ch), lambda i: (jnp.maximum(i * per - 1, 0), cb))]


def _conv_fwd(p, dw, bias, ln_g, ln_b):
    t = p.shape[0]
    ch = D_MODEL
    tm = min(CONV_TILE, t)

    def body(a_ref, g_ref, ah_ref, gh_ref, dw_ref, bias_ref, lg_ref, lb_ref, o_ref, z1_ref, zp_ref):
        i = pl.program_id(0)
        _conv_fill(zp_ref, a_ref, g_ref, ah_ref, gh_ref, i)
        _conv_taps(zp_ref, z1_ref, dw_ref, bias_ref, tm, ch)
        z1 = z1_ref[...]
        mu = jnp.mean(z1, axis=-1, keepdims=True)
        zc = z1 - mu
        rs = lax.rsqrt(jnp.mean(zc * zc, axis=-1, keepdims=True) + EPS)
        z2 = zc * rs * lg_ref[...] + lb_ref[...]
        o_ref[...] = (z2 * _sigmoid(z2)).astype(BF16)

    vec = pl.BlockSpec((1, ch), lambda i: (0, 0))
    return pl.pallas_call(
        body, name="conv_fwd", grid=(t // tm,),
        in_specs=_conv_specs(tm, ch) + [pl.BlockSpec((CONV_PAD, ch), lambda i: (0, 0)), vec, vec, vec],
        out_specs=[pl.BlockSpec((tm, ch), lambda i: (i, 0)), pl.BlockSpec((tm, ch), lambda i: (i, 0))],
        out_shape=[jax.ShapeDtypeStruct((t, ch), BF16), jax.ShapeDtypeStruct((t, ch), F32)],
        scratch_shapes=[pltpu.VMEM((CONV_PAD + tm, ch), F32)],
        compiler_params=_params(("parallel",)),
    )(p, p, p, p, dw, bias, ln_g, ln_b)


def _conv_bwd_ln(p, z1_saved, dz3, ln_g, ln_b):
    t = p.shape[0]
    ch = D_MODEL
    tm = min(CONV_TILE, t)

    def body(a_ref, g_ref, ah_ref, gh_ref, z1_ref, dz3_ref, lg_ref, lb_ref,
             dz1_ref, ddw_ref, dbias_ref, dlg_ref, dlb_ref, zp_ref):
        i = pl.program_id(0)
        _conv_fill(zp_ref, a_ref, g_ref, ah_ref, gh_ref, i)
        z1 = z1_ref[...]
        mu = jnp.mean(z1, axis=-1, keepdims=True)
        zc = z1 - mu
        rs = lax.rsqrt(jnp.mean(zc * zc, axis=-1, keepdims=True) + EPS)
        xh = zc * rs
        z2 = xh * lg_ref[...] + lb_ref[...]
        sig = _sigmoid(z2)
        dz2 = dz3_ref[...].astype(F32) * (sig * (1.0 + z2 * (1.0 - sig)))
        dxh = dz2 * lg_ref[...]
        dz1 = rs * (dxh - jnp.mean(dxh, axis=-1, keepdims=True) - xh * jnp.mean(dxh * xh, axis=-1, keepdims=True))
        dz1_ref[...] = dz1

        @pl.when(i == 0)
        def _():
            ddw_ref[...] = jnp.zeros_like(ddw_ref)
            dbias_ref[...] = jnp.zeros_like(dbias_ref)
            dlg_ref[...] = jnp.zeros_like(dlg_ref)
            dlb_ref[...] = jnp.zeros_like(dlb_ref)

        dlg_ref[...] += jnp.sum(dz2 * xh, axis=0, keepdims=True)
        dlb_ref[...] += jnp.sum(dz2, axis=0, keepdims=True)
        dbias_ref[...] += jnp.sum(dz1, axis=0, keepdims=True)
        groups = _shift_groups(FWD_SHIFTS)
        for cc in range(ch // LANES):
            lanes = pl.ds(cc * LANES, LANES)
            accs = [jnp.zeros((8, LANES), F32) for _ in range(CONV_WIDTH)]
            for r0 in range(0, tm, CONV_ROWS):
                dzc = dz1_ref[pl.ds(r0, CONV_ROWS), lanes]
                for j, rows in _windows(zp_ref, r0, lanes, groups):
                    accs[j] = accs[j] + jnp.sum((dzc * rows).reshape(CONV_ROWS // 8, 8, LANES), axis=0)
            for j in range(CONV_WIDTH):
                ddw_ref[pl.ds(j, 1), lanes] += jnp.sum(accs[j], axis=0, keepdims=True)

    vec = pl.BlockSpec((1, ch), lambda i: (0, 0))
    return pl.pallas_call(
        body, name="conv_bwd_ln", grid=(t // tm,),
        in_specs=_conv_specs(tm, ch) + [pl.BlockSpec((tm, ch), lambda i: (i, 0)),
                                        pl.BlockSpec((tm, ch), lambda i: (i, 0)), vec, vec],
        out_specs=[pl.BlockSpec((tm, ch), lambda i: (i, 0)), pl.BlockSpec((CONV_PAD, ch), lambda i: (0, 0)), vec, vec, vec],
        out_shape=[jax.ShapeDtypeStruct((t, ch), F32), jax.ShapeDtypeStruct((CONV_PAD, ch), F32)]
        + [jax.ShapeDtypeStruct((1, ch), F32)] * 3,
        scratch_shapes=[pltpu.VMEM((CONV_PAD + tm, ch), F32)],
        compiler_params=_params(("arbitrary",)),
    )(p, p, p, p, z1_saved, dz3, ln_g, ln_b)


def _conv_bwd_glu(p, dz1, dw, dq, dkv, dgates):
    t = p.shape[0]
    ch = D_MODEL
    tm = min(CONV_TILE, t)
    per = tm // CONV_PAD
    n_halo = t // CONV_PAD
    cb = COL_CONV_G // ch

    def body(a_ref, g_ref, dz_ref, dzn_ref, dw_ref, dq_ref, dkv_ref, dgates_ref, o_ref, zp_ref, z0_ref):
        i = pl.program_id(0)
        o_ref[:, pl.ds(COL_Q, Q_W)] = dq_ref[...]
        o_ref[:, pl.ds(COL_K, 2 * KV_W)] = dkv_ref[...]
        o_ref[:, pl.ds(COL_GC, ch)] = dgates_ref[0]
        o_ref[:, pl.ds(COL_GA, ch)] = dgates_ref[1]
        zp_ref[pl.ds(0, tm), :] = dz_ref[...]
        zp_ref[pl.ds(tm, CONV_PAD), :] = jnp.where(i < t // tm - 1, dzn_ref[...], 0.0)
        _conv_apply(zp_ref, z0_ref, dw_ref, None, tm, ch, BWD_SHIFTS)
        dz0 = z0_ref[...]
        a = a_ref[...].astype(F32)
        sig = _sigmoid(g_ref[...].astype(F32))
        o_ref[:, pl.ds(0, ch)] = (dz0 * sig).astype(BF16)
        o_ref[:, pl.ds(ch, ch)] = (dz0 * a * sig * (1.0 - sig)).astype(BF16)

    return pl.pallas_call(
        body, name="conv_bwd_glu", grid=(t // tm,),
        in_specs=[pl.BlockSpec((tm, ch), lambda i: (i, 0)), pl.BlockSpec((tm, ch), lambda i: (i, cb)),
                  pl.BlockSpec((tm, ch), lambda i: (i, 0)),
                  pl.BlockSpec((CONV_PAD, ch), lambda i: (jnp.minimum((i + 1) * per, n_halo - 1), 0)),
                  pl.BlockSpec((CONV_PAD, ch), lambda i: (0, 0)),
                  pl.BlockSpec((tm, Q_W), lambda i: (i, 0)), pl.BlockSpec((tm, 2 * KV_W), lambda i: (i, 0)),
                  pl.BlockSpec((2, tm, ch), lambda i: (0, i, 0))],
        out_specs=pl.BlockSpec((tm, IN_W), lambda i: (i, 0)),
        out_shape=jax.ShapeDtypeStruct((t, IN_W), BF16),
        scratch_shapes=[pltpu.VMEM((tm + CONV_PAD, ch), F32), pltpu.VMEM((tm, ch), F32)],
        compiler_params=_params(("parallel",)),
    )(p, p, dz1, dz1, dw, dq, dkv, dgates)


def _bucket_onehot():
    qi = jnp.arange(BLOCK, dtype=jnp.int32)[:, None]
    kj = jnp.arange(2 * BLOCK, dtype=jnp.int32)[None, :]
    dist = jnp.maximum(qi + BLOCK - kj, 0)
    max_exact = N_BUCKETS // 2
    dflt = jnp.maximum(dist, 1).astype(F32)
    large = max_exact + (jnp.log(dflt / max_exact) / math.log(MAX_DISTANCE / max_exact)
                         * (N_BUCKETS - max_exact)).astype(jnp.int32)
    large = jnp.minimum(large, N_BUCKETS - 1)
    bucket = jnp.where(dist < max_exact, dist, large)
    onehot = bucket[None] == jnp.arange(N_BUCKETS, dtype=jnp.int32)[:, None, None]
    return onehot.astype(F32).reshape(N_BUCKETS, BLOCK * 2 * BLOCK)


def _bias_table(rel_bias_t, onehot):
    n = onehot.shape[1]
    tn = 4096

    def body(r_ref, oh_ref, o_ref):
        flat = pl.program_id(0) * tn + lax.broadcasted_iota(jnp.int32, (N_Q_HEADS, tn), 1)
        dist = (flat // (2 * BLOCK)) + BLOCK - (flat % (2 * BLOCK))
        bias = _dot(r_ref[...], oh_ref[...], precision=lax.Precision.HIGHEST)
        o_ref[...] = jnp.where((dist >= 0) & (dist < BLOCK), bias, NEG)

    return pl.pallas_call(
        body, name="bias_table", grid=(n // tn,),
        in_specs=[pl.BlockSpec((N_Q_HEADS, N_BUCKETS), lambda i: (0, 0)), pl.BlockSpec((N_BUCKETS, tn), lambda i: (0, i))],
        out_specs=pl.BlockSpec((N_Q_HEADS, tn), lambda i: (0, i)),
        out_shape=jax.ShapeDtypeStruct((N_Q_HEADS, n), F32),
        compiler_params=_params(("parallel",)),
    )(rel_bias_t, onehot)


def _bias_table_bwd(dbias, onehot):
    n = onehot.shape[1]
    tn = 4096

    def body(d_ref, oh_ref, o_ref):
        part = _dot(d_ref[...], oh_ref[...], trans_b=True, precision=lax.Precision.HIGHEST)
        i = pl.program_id(0)

        @pl.when(i == 0)
        def _():
            o_ref[...] = part

        @pl.when(i > 0)
        def _():
            o_ref[...] += part

    return pl.pallas_call(
        body, name="bias_table_bwd", grid=(n // tn,),
        in_specs=[pl.BlockSpec((N_Q_HEADS, tn), lambda i: (0, i)), pl.BlockSpec((N_BUCKETS, tn), lambda i: (0, i))],
        out_specs=pl.BlockSpec((N_Q_HEADS, N_BUCKETS), lambda i: (0, 0)),
        out_shape=jax.ShapeDtypeStruct((N_Q_HEADS, N_BUCKETS), F32),
        compiler_params=_params(("arbitrary",)),
    )(dbias, onehot)


def _lane_head(rows):
    return lax.broadcasted_iota(jnp.int32, (rows, KV_W), 1) // HEAD_DIM


def _group_rms(x, gain_wide):
    head = _lane_head(x.shape[0])
    sq = x * x
    r = jnp.zeros_like(x)
    for i in range(N_KV_HEADS):
        ms = jnp.sum(jnp.where(head == i, sq, 0.0), axis=-1, keepdims=True) * (1.0 / HEAD_DIM)
        r = jnp.where(head == i, lax.rsqrt(ms + EPS), r)
    return r, x * r * gain_wide


def _stack_heads(group):
    head = _lane_head(group.shape[0])
    return jnp.concatenate([jnp.where(head == i, group, jnp.zeros_like(group)) for i in range(N_KV_HEADS)], axis=0)


def _unstack_heads(stacked):
    head = _lane_head(BLOCK)
    out = jnp.where(head == 0, stacked[:BLOCK], 0.0)
    for i in range(1, N_KV_HEADS):
        out = out + jnp.where(head == i, stacked[i * BLOCK:(i + 1) * BLOCK], 0.0)
    return out


def _repeaters():
    row = lax.broadcasted_iota(jnp.int32, (KV_W, KV_W), 0)
    col = lax.broadcasted_iota(jnp.int32, (KV_W, KV_W), 1)
    return [(row == h * HEAD_DIM + col % HEAD_DIM).astype(BF16) for h in range(N_KV_HEADS)]


def _attn_probs(q_stack, k_rep, sink, bias, before_start):
    s = _dot(q_stack, k_rep, trans_b=True) * (1.0 / math.sqrt(HEAD_DIM)) + bias
    s = jnp.where(before_start, NEG, s)
    m = jnp.maximum(jnp.max(s, axis=-1, keepdims=True), sink)
    p = jnp.exp(s - m)
    es = jnp.exp(sink - m)
    inv = 1.0 / (jnp.sum(p, axis=-1, keepdims=True) + es)
    return p * inv, es * inv


def _before_start(n):
    col = lax.broadcasted_iota(jnp.int32, (QROWS, 2 * BLOCK), 1)
    return (col < BLOCK) & (n == 0)


KV_W = N_KV_HEADS * HEAD_DIM
Q_W = N_Q_HEADS * HEAD_DIM


def _attn_specs():
    qspec = pl.BlockSpec((BLOCK, Q_W), lambda n: (n, COL_Q // Q_W))
    kprev = pl.BlockSpec((BLOCK, KV_W), lambda n: (jnp.maximum(n - 1, 0), COL_K // KV_W))
    kcur = pl.BlockSpec((BLOCK, KV_W), lambda n: (n, COL_K // KV_W))
    vprev = pl.BlockSpec((BLOCK, KV_W), lambda n: (jnp.maximum(n - 1, 0), COL_V // KV_W))
    vcur = pl.BlockSpec((BLOCK, KV_W), lambda n: (n, COL_V // KV_W))
    gain = pl.BlockSpec((1, KV_W), lambda n: (0, 0))
    sink = pl.BlockSpec((N_KV_HEADS, QROWS, 1), lambda n: (0, 0, 0))
    bias = pl.BlockSpec((N_KV_HEADS, QROWS, 2 * BLOCK), lambda n: (0, 0, 0))
    return [qspec, kprev, kcur, vprev, vcur], gain, sink, bias


def _attn_fwd(p, gq, gk, sink_rows, bias):
    t = p.shape[0]
    nb = t // BLOCK
    qkv, gain, sink, bspec = _attn_specs()

    def body(q_ref, kp_ref, kc_ref, vp_ref, vc_ref, gq_ref, gk_ref, sink_ref, bias_ref, o_ref, p_ref, ps_ref):
        before_start = _before_start(pl.program_id(0))
        rep = _repeaters()
        kf = jnp.concatenate([kp_ref[...], kc_ref[...]], axis=0).astype(F32)
        kn = _group_rms(kf, gk_ref[...])[1].astype(BF16)
        v = jnp.concatenate([vp_ref[...], vc_ref[...]], axis=0)
        for h in range(N_KV_HEADS):
            qn = _group_rms(q_ref[:, pl.ds(h * KV_W, KV_W)].astype(F32), gq_ref[...])[1]
            k_rep = _dot(kn, rep[h]).astype(BF16)
            pn, ps_ref[h] = _attn_probs(_stack_heads(qn).astype(BF16), k_rep, sink_ref[h], bias_ref[h], before_start)
            pn = pn.astype(BF16)
            p_ref[h] = pn
            v_rep = _dot(v, rep[h]).astype(BF16)
            o_ref[:, pl.ds(h * KV_W, KV_W)] = _unstack_heads(_dot(pn, v_rep)).astype(BF16)

    return pl.pallas_call(
        body, name="attn_fwd", grid=(nb,),
        in_specs=qkv + [gain, gain, sink, bspec],
        out_specs=[pl.BlockSpec((BLOCK, Q_W), lambda n: (n, 0)),
                   pl.BlockSpec((None, N_KV_HEADS, QROWS, 2 * BLOCK), lambda n: (n, 0, 0, 0)),
                   pl.BlockSpec((None, N_KV_HEADS, QROWS, 1), lambda n: (n, 0, 0, 0))],
        out_shape=[jax.ShapeDtypeStruct((t, Q_W), BF16),
                   jax.ShapeDtypeStruct((nb, N_KV_HEADS, QROWS, 2 * BLOCK), BF16),
                   jax.ShapeDtypeStruct((nb, N_KV_HEADS, QROWS, 1), F32)],
        compiler_params=_params(("parallel",)),
    )(p, p, p, p, p, gq, gk, sink_rows, bias)


def _attn_bwd(p, do, gq, gk, probs, sink_probs):
    t = p.shape[0]
    nb = t // BLOCK
    qkv, _, sink, bspec = _attn_specs()
    gain = pl.BlockSpec((1, HEAD_DIM), lambda n: (0, 0))
    scale = 1.0 / math.sqrt(HEAD_DIM)

    def head_selectors():
        row = lax.broadcasted_iota(jnp.int32, (KV_W, HEAD_DIM), 0)
        col = lax.broadcasted_iota(jnp.int32, (KV_W, HEAD_DIM), 1)
        return [(row == col + i * HEAD_DIM).astype(BF16) for i in range(N_KV_HEADS)]

    def take_heads(group, sel):
        return jnp.concatenate([_dot(group, s) for s in sel], axis=0)

    def put_heads(x, sel):
        rows = x.shape[0] // len(sel)
        out = _dot(x[:rows].astype(BF16), sel[0], trans_b=True)
        for i in range(1, len(sel)):
            out = out + _dot(x[i * rows:(i + 1) * rows].astype(BF16), sel[i], trans_b=True)
        return out

    def rms(x, g):
        r = lax.rsqrt(jnp.mean(x * x, axis=-1, keepdims=True) + EPS)
        return r, x * r * g

    def rms_bwd(dn, xf, r, g):
        w = dn * g
        dx = r * w - xf * (r * r * r) * jnp.mean(xf * w, axis=-1, keepdims=True)
        return dx, jnp.sum(dn * (xf * r), axis=0, keepdims=True)

    def body(q_ref, kp_ref, kc_ref, vp_ref, vc_ref, do_ref, gq_ref, gk_ref, p_ref, ps_ref,
             dq_ref, dkv_ref, dbias_ref, dsink_ref, dgq_ref, dgk_ref):
        n = pl.program_id(0)
        sel = head_selectors()

        @pl.when(n == 0)
        def _():
            dbias_ref[...] = jnp.zeros_like(dbias_ref)
            dsink_ref[...] = jnp.zeros_like(dsink_ref)
            dgq_ref[...] = jnp.zeros_like(dgq_ref)
            dgk_ref[...] = jnp.zeros_like(dgk_ref)

        dgq_sum = jnp.zeros((1, HEAD_DIM), F32)
        dgk_sum = jnp.zeros((1, HEAD_DIM), F32)
        dk_rows, dv_rows = [], []
        for h in range(N_KV_HEADS):
            qf = take_heads(q_ref[:, pl.ds(h * KV_W, KV_W)], sel)
            rq, qn = rms(qf, gq_ref[...])
            kf = jnp.concatenate([_dot(kp_ref[...], sel[h]), _dot(kc_ref[...], sel[h])], axis=0)
            rk, kn = rms(kf, gk_ref[...])
            pn_bf16, psink = p_ref[h], ps_ref[h]
            pn = pn_bf16.astype(F32)
            do = take_heads(do_ref[:, pl.ds(h * KV_W, KV_W)], sel).astype(BF16)
            v = jnp.concatenate([_dot(vp_ref[...], sel[h]), _dot(vc_ref[...], sel[h])], axis=0).astype(BF16)
            dv_win = _dot(do, pn_bf16, trans_a=True).T
            dp = _dot(do, v, trans_b=True)
            delta = jnp.sum(pn * dp, axis=-1, keepdims=True)
            ds = pn * (dp - delta)
            dsc = (ds * scale).astype(BF16)
            dqn = _dot(dsc, kn.astype(BF16))
            dkn = _dot(qn.astype(BF16), dsc, trans_a=True).T
            dq, dgq = rms_bwd(dqn, qf, rq, gq_ref[...])
            dk_win, dgk = rms_bwd(dkn, kf, rk, gk_ref[...])
            dq_ref[:, pl.ds(h * KV_W, KV_W)] = put_heads(dq, sel).astype(BF16)
            dk_rows += [dk_win[:BLOCK], dk_win[BLOCK:]]
            dv_rows += [dv_win[:BLOCK], dv_win[BLOCK:]]
            dbias_ref[h] += ds
            dsink_ref[h] += jnp.sum((-psink * delta).reshape(GROUP, BLOCK, 1), axis=1)
            dgq_sum = dgq_sum + dgq
            dgk_sum = dgk_sum + dgk
        for part in range(2):
            dkv_ref[part, :, pl.ds(0, KV_W)] = put_heads(jnp.concatenate(dk_rows[part::2], axis=0), sel).astype(BF16)
            dkv_ref[part, :, pl.ds(KV_W, KV_W)] = put_heads(jnp.concatenate(dv_rows[part::2], axis=0), sel).astype(BF16)
        dgq_ref[...] += dgq_sum
        dgk_ref[...] += dgk_sum

    row = pl.BlockSpec((BLOCK, Q_W), lambda n: (n, 0))
    return pl.pallas_call(
        body, name="attn_bwd", grid=(nb,),
        in_specs=qkv + [row, gain, gain,
                        pl.BlockSpec((None, N_KV_HEADS, QROWS, 2 * BLOCK), lambda n: (n, 0, 0, 0)),
                        pl.BlockSpec((None, N_KV_HEADS, QROWS, 1), lambda n: (n, 0, 0, 0))],
        out_specs=[row, pl.BlockSpec((None, 2, BLOCK, 2 * KV_W), lambda n: (n, 0, 0, 0)), bspec,
                   pl.BlockSpec((N_KV_HEADS, GROUP, 1), lambda n: (0, 0, 0)), gain, gain],
        out_shape=[jax.ShapeDtypeStruct((t, Q_W), BF16),
                   jax.ShapeDtypeStruct((nb, 2, BLOCK, 2 * KV_W), BF16),
                   jax.ShapeDtypeStruct((N_KV_HEADS, QROWS, 2 * BLOCK), F32),
                   jax.ShapeDtypeStruct((N_KV_HEADS, GROUP, 1), F32),
                   jax.ShapeDtypeStruct((1, HEAD_DIM), F32),
                   jax.ShapeDtypeStruct((1, HEAD_DIM), F32)],
        compiler_params=_params(("arbitrary",)),
    )(p, p, p, p, p, do, gq, gk, probs, sink_probs)


def _kv_window_sum(parts):
    nb = parts.shape[0]

    def body(cur_ref, nxt_ref, o_ref):
        nxt = jnp.where(pl.program_id(0) < nb - 1, nxt_ref[...].astype(F32), 0.0)
        o_ref[...] = (cur_ref[...].astype(F32) + nxt).astype(BF16)

    blk = (None, None, BLOCK, 2 * KV_W)
    return pl.pallas_call(
        body, name="kv_window_sum", grid=(nb,),
        in_specs=[pl.BlockSpec(blk, lambda n: (n, 1, 0, 0)),
                  pl.BlockSpec(blk, lambda n: (jnp.minimum(n + 1, nb - 1), 0, 0, 0))],
        out_specs=pl.BlockSpec((BLOCK, 2 * KV_W), lambda n: (n, 0)),
        out_shape=jax.ShapeDtypeStruct((nb * BLOCK, 2 * KV_W), BF16),
        compiler_params=_params(("parallel",)),
    )(parts, parts)


GATE_TILE = 512


def _merge_fwd(z3, o, p, w_proj, w_o):
    t, d = z3.shape
    tm = min(ROW_TILE, t)
    tn = GATE_TILE

    def body(z_ref, o_ref, gc_ref, ga_ref, wp_ref, wo_ref, m_ref, a_ref, b_ref):
        a = _dot(z_ref[...], wp_ref[...])
        b = _dot(o_ref[...], wo_ref[...])
        m_ref[...] = (_sigmoid(gc_ref[...].astype(F32)) * a + _sigmoid(ga_ref[...].astype(F32)) * b).astype(BF16)
        a_ref[...] = a.astype(BF16)
        b_ref[...] = b.astype(BF16)

    row = pl.BlockSpec((tm, d), lambda i, j: (i, 0))
    wspec = pl.BlockSpec((d, tn), lambda i, j: (0, j))
    ospec = pl.BlockSpec((tm, tn), lambda i, j: (i, j))
    return pl.pallas_call(
        body, name="merge_fwd", grid=(t // tm, d // tn),
        in_specs=[row, row,
                  pl.BlockSpec((tm, tn), lambda i, j: (i, COL_GC // tn + j)),
                  pl.BlockSpec((tm, tn), lambda i, j: (i, COL_GA // tn + j)), wspec, wspec],
        out_specs=[ospec, ospec, ospec],
        out_shape=[jax.ShapeDtypeStruct((t, d), BF16)] * 3,
        compiler_params=_params(("parallel", "parallel")),
    )(z3, o, p, p, w_proj, w_o)


def _merge_bwd(dres, w_out, a, b, p, tokens=()):
    t, d = dres.shape
    tm = min(ROW_TILE, t)
    tn = GATE_TILE

    def epilogue(acc, ex, outs, ids):
        a_ref, b_ref, gc_ref, ga_ref = ex[:4]
        sc = _sigmoid(gc_ref[...].astype(F32))
        sa = _sigmoid(ga_ref[...].astype(F32))
        outs[0][...] = (acc * sc).astype(BF16)
        outs[1][...] = (acc * sa).astype(BF16)
        outs[2][0] = (acc * a_ref[...].astype(F32) * sc * (1.0 - sc)).astype(BF16)
        outs[2][1] = (acc * b_ref[...].astype(F32) * sa * (1.0 - sa)).astype(BF16)

    ospec = pl.BlockSpec((tm, tn), lambda i, j, kk: (i, j))
    return _mm("merge_bwd", (t // tm, d // tn, 1),
               dres, pl.BlockSpec((tm, d), lambda i, j, kk: (i, 0)),
               w_out, pl.BlockSpec((tn, d), lambda i, j, kk: (j, 0)), (tm, tn),
               trans_b=True, a_pre=_to_bf16,
               extras=(a, b, p, p),
               extra_specs=(ospec, ospec,
                            pl.BlockSpec((tm, tn), lambda i, j, kk: (i, COL_GC // tn + j)),
                            pl.BlockSpec((tm, tn), lambda i, j, kk: (i, COL_GA // tn + j))), tokens=tokens,
               out_shape=(jax.ShapeDtypeStruct((t, d), BF16), jax.ShapeDtypeStruct((t, d), BF16),
                          jax.ShapeDtypeStruct((2, t, d), BF16)),
               out_specs=(ospec, ospec, pl.BlockSpec((2, tm, tn), lambda i, j, kk: (0, i, j))),
               epilogue=epilogue)


def _store_epilogue(acc, ex, outs, ids):
    outs[0][...] = acc


def _store_bf16_epilogue(acc, ex, outs, ids):
    outs[0][...] = acc.astype(BF16)


def _mm_nt(name, a, w, out_dtype=BF16):
    t, n = a.shape
    k = w.shape[0]
    tm = min(ROW_TILE, t)
    return _mm(name, (t // tm, 1, 1), a, pl.BlockSpec((tm, n), lambda i, j, kk: (i, 0)),
               w, pl.BlockSpec((k, n), lambda i, j, kk: (0, 0)), (tm, k), trans_b=True,
               out_shape=(jax.ShapeDtypeStruct((t, k), out_dtype),),
               out_specs=(pl.BlockSpec((tm, k), lambda i, j, kk: (i, 0)),),
               epilogue=_store_bf16_epilogue if out_dtype == BF16 else _store_epilogue)[0]


def _mm_tn(name, a, b, b_pre=None, tokens=()):
    t, m = a.shape
    n = b.shape[1]
    tk = min(TOKEN_TILE, t)
    return _mm(name, (1, 1, t // tk), a, pl.BlockSpec((tk, m), lambda i, j, kk: (kk, 0)),
               b, pl.BlockSpec((tk, n), lambda i, j, kk: (kk, 0)), (m, n), trans_a=True, b_pre=b_pre, tokens=tokens,
               out_shape=(jax.ShapeDtypeStruct((m, n), BF16),),
               out_specs=(pl.BlockSpec((m, n), lambda i, j, kk: (0, 0)),), epilogue=_store_bf16_epilogue)[0]


def _local_step(x, target, small, comm):
    t = x.shape[0]
    w = dict(small)

    n1 = _rmsnorm_fwd("ffn1_norm", x, w["ffn1_norm"])
    w.update(comm.weights("A", n1))
    (x1, hm), ffn1_saved = _ffn_fwd("ffn1", x, n1, w["ffn1_w_in"], w["ffn1_w_out"], comm.tokens,
                                    next_gain=w["mix_norm"])
    w.update(comm.weights("B", x1))
    tm = min(ROW_TILE, t)
    p = _mm("mix_in", (N_CHIPS, t // tm, 1),
            hm, pl.BlockSpec((tm, D_MODEL), lambda j, i, kk: (i, 0)),
            w["w_in"], pl.BlockSpec((None, D_MODEL, SHARD_W), lambda j, i, kk: (j, 0, 0)), (tm, SHARD_W),
            tokens=comm.tokens,
            out_shape=(jax.ShapeDtypeStruct((t, IN_W), BF16),),
            out_specs=(pl.BlockSpec((tm, SHARD_W), lambda j, i, kk: (i, j)),),
            epilogue=_store_bf16_epilogue)[0]

    z3, z1 = _conv_fwd(p, w["conv_dw_kernel"], w["conv_dw_bias"], w["conv_ln_g"], w["conv_ln_b"])

    onehot = _bucket_onehot()
    bias = _bias_table(w["rel_bias"].T, onehot).reshape(N_KV_HEADS, QROWS, 2 * BLOCK)
    sink_rows = jnp.repeat(w["attn_sinks"].reshape(N_KV_HEADS, GROUP), BLOCK, axis=1)[..., None]
    gq_wide = jnp.tile(w["q_norm"], (1, N_KV_HEADS))
    gk_wide = jnp.tile(w["k_norm"], (1, N_KV_HEADS))
    o, probs, sink_probs = _attn_fwd(p, gq_wide, gk_wide, sink_rows, bias)

    merged, a, b = _merge_fwd(z3, o, p, w["conv_w_proj"], w["attn_w_o"])
    x2, n2 = _mm_residual("mix_out", merged, w["w_out"], x1, 1.0, next_gain=w["ffn2_norm"])
    w.update(comm.weights("C", n2))
    (dy, loss), ffn2_saved = _ffn_fwd("ffn2", x2, n2, w["ffn2_w_in"], w["ffn2_w_out"], loss_target=target)

    g, big = {}, {}
    dres2, big["ffn2_w_in"], big["ffn2_w_out"], g["ffn2_norm"] = _ffn_bwd(
        "ffn2b", dy, x2, w["ffn2_norm"], ffn2_saved, w["ffn2_w_in"], w["ffn2_w_out"])
    tokens = comm.reduce_start("R1", big, behind=True)

    da, db, dgates = _merge_bwd(dres2, w["w_out"], a, b, p, tokens)
    tokens = comm.exchange_finish("R1", da)
    big = {}
    big["w_out"] = _mm_tn("d_w_out", merged, dres2, b_pre=_to_bf16, tokens=tokens)
    big["conv_w_proj"] = _mm_tn("d_w_proj", z3, da)
    big["attn_w_o"] = _mm_tn("d_w_o", o, db)
    dz3 = _mm_nt("d_z3", da, w["conv_w_proj"])
    do = _mm_nt("d_o", db, w["attn_w_o"])

    dq, dkv_parts, dbias, dsink, g["q_norm"], g["k_norm"] = _attn_bwd(
        p, do, w["q_norm"], w["k_norm"], probs, sink_probs)
    dkv = _kv_window_sum(dkv_parts)
    g["rel_bias"] = _bias_table_bwd(dbias.reshape(N_Q_HEADS, BLOCK * 2 * BLOCK), onehot).T
    g["attn_sinks"] = dsink.reshape(N_Q_HEADS)

    dz1, big["conv_dw_kernel"], g["conv_dw_bias"], g["conv_ln_g"], g["conv_ln_b"] = _conv_bwd_ln(
        p, z1, dz3, w["conv_ln_g"], w["conv_ln_b"])
    dp = _conv_bwd_glu(p, dz1, w["conv_dw_kernel"], dq, dkv, dgates)
    tk = min(TOKEN_TILE, t)
    big["w_in"] = _mm("d_w_in", (1, N_CHIPS, t // tk),
                    hm, pl.BlockSpec((tk, D_MODEL), lambda i, j, kk: (kk, 0)),
                    dp, pl.BlockSpec((tk, SHARD_W), lambda i, j, kk: (kk, j)), (D_MODEL, SHARD_W),
                    trans_a=True,
                    out_shape=(jax.ShapeDtypeStruct((N_CHIPS, D_MODEL, SHARD_W), BF16),),
                    out_specs=(pl.BlockSpec((None, D_MODEL, SHARD_W), lambda i, j, kk: (j, 0, 0)),),
                    epilogue=_store_bf16_epilogue)[0]
    dres1, g["mix_norm"] = _mm("d_mix", (t // tm, 1, N_CHIPS),
                               dp, pl.BlockSpec((tm, SHARD_W), lambda i, j, kk: (i, kk)),
                               w["w_in"], pl.BlockSpec((None, D_MODEL, SHARD_W), lambda i, j, kk: (kk, 0, 0)),
                               (tm, D_MODEL), trans_b=True,
                               extras=(x1, w["mix_norm"], dres2),
                               extra_specs=(pl.BlockSpec((tm, D_MODEL), lambda i, j, kk: (i, 0)),
                                            pl.BlockSpec((1, D_MODEL), lambda i, j, kk: (0, 0)),
                                            pl.BlockSpec((tm, D_MODEL), lambda i, j, kk: (i, 0))),
                               out_shape=(jax.ShapeDtypeStruct((t, D_MODEL), F32), jax.ShapeDtypeStruct((1, D_MODEL), F32)),
                               out_specs=(pl.BlockSpec((tm, D_MODEL), lambda i, j, kk: (i, 0)),
                                          pl.BlockSpec((1, D_MODEL), lambda i, j, kk: (0, 0))),
                               epilogue=_rms_bwd_epilogue, sem=("arbitrary", "arbitrary", "arbitrary"))

    tokens = comm.reduce_finish("R1", dres1, behind=True) + comm.reduce_start("R2", big, behind=True)

    def ffn1_first(du):
        comm.join_finish("R1", du)
        return comm.exchange_finish("R2", du)

    def ffn1_grads(dw_in4, dw_out):
        late = comm.reduce_finish("R2", dw_in4, behind=True)
        return late + comm.reduce_start("R3", {"ffn1_w_in": dw_in4, "ffn1_w_out": dw_out})

    grad_x, _, _, g["ffn1_norm"] = _ffn_bwd(
        "ffn1b", dres1, x, w["ffn1_norm"], ffn1_saved, w["ffn1_w_in"], w["ffn1_w_out"], tokens, ffn1_first, ffn1_grads)
    comm.join_finish("R2", grad_x)
    comm.reduce_finish("R3", grad_x)
    return loss[0, 0], grad_x, g


def _mesh_place():
    x, y, c = lax.axis_index("x"), lax.axis_index("y"), lax.axis_index("c")
    chips = [(1 - x, y), (x, 1 - y), (1 - x, 1 - y)]
    return x, y, c, chips


def _any_specs(n):
    return [pl.BlockSpec(memory_space=pl.ANY)] * n


HBM_SPEC = pl.BlockSpec(memory_space=pltpu.HBM)
SEM_SPEC = pl.BlockSpec(memory_space=pltpu.SEMAPHORE)
EFFECT = pltpu.SideEffectType.DATAFLOW_SIDE_EFFECTING


def _in_hbm(a):
    return pltpu.with_memory_space_constraint(a, pltpu.HBM)


def _copy_start(name, srcs, lands, plan, after=()):
    ns, nb = len(srcs), len(lands)
    n = plan.copies_per_source * ns

    def body(*refs):
        s_refs, l_refs = refs[:ns], refs[ns:ns + nb]
        send_sems, recv_sems = refs[ns + nb + len(after)], refs[ns + nb + len(after) + 1]
        token = refs[-1]
        for k, (src, dst, to, _) in enumerate(plan(s_refs, l_refs)):
            pltpu.make_async_remote_copy(src_ref=src, dst_ref=dst, send_sem=send_sems.at[k], recv_sem=recv_sems.at[k],
                                         device_id=to, device_id_type=MESH).start()
        token[...] = jnp.zeros_like(token)

    bufs = list(srcs) + list(lands)
    outs = pl.pallas_call(
        body, name=name,
        out_shape=(pltpu.SemaphoreType.DMA((n,)), pltpu.SemaphoreType.DMA((n,)),
                   *[pltpu.HBM(a.shape, a.dtype) for a in bufs], jax.ShapeDtypeStruct((8, LANES), F32)),
        in_specs=[HBM_SPEC] * len(bufs) + [pl.BlockSpec(memory_space=pl.ANY)] * len(after),
        out_specs=(SEM_SPEC, SEM_SPEC, *[HBM_SPEC] * len(bufs), pl.BlockSpec(memory_space=pltpu.VMEM)),
        input_output_aliases={i: 2 + i for i in range(len(bufs))},
        compiler_params=pltpu.CompilerParams(has_side_effects=EFFECT),
    )(*[_in_hbm(a) for a in bufs], *after)
    return outs[0], outs[1], list(outs[2:2 + ns]), list(outs[2 + ns:2 + ns + nb]), outs[-1]


def _copy_wait(name, send_sems, recv_sems, srcs, lands, after, plan):
    ns, nb = len(srcs), len(lands)
    after = tuple(after) if isinstance(after, (tuple, list)) else (after,)

    def body(*refs):
        s_refs, l_refs = refs[:ns], refs[ns:ns + nb]
        send_sems, recv_sems = refs[ns + nb], refs[ns + nb + 1]
        for k, (src, _, to, mine) in enumerate(plan(s_refs, l_refs)):
            cp = pltpu.make_async_remote_copy(src_ref=src, dst_ref=mine, send_sem=send_sems.at[k], recv_sem=recv_sems.at[k],
                                              device_id=to, device_id_type=MESH)
            cp.wait_send()
            cp.wait_recv()

    bufs = list(srcs) + list(lands)
    outs = pl.pallas_call(
        body, name=name,
        out_shape=tuple(pltpu.HBM(a.shape, a.dtype) for a in bufs),
        in_specs=[HBM_SPEC] * len(bufs) + [SEM_SPEC, SEM_SPEC] + [pl.BlockSpec(memory_space=pl.ANY)] * len(after),
        out_specs=tuple([HBM_SPEC] * len(bufs)),
        input_output_aliases={i: i for i in range(len(bufs))},
        compiler_params=pltpu.CompilerParams(has_side_effects=EFFECT),
    )(*bufs, send_sems, recv_sems, *after)
    return list(outs[:ns]), list(outs[ns:])


def _gather_plan(s_refs, l_refs):
    x, y, c, chips = _mesh_place()
    jme = 2 * x + y
    return [(s.at[c], land.at[jme, c], (*chip, c), land.at[2 * chip[0] + chip[1], c])
            for s, land in zip(s_refs, l_refs) for chip in chips]


_gather_plan.copies_per_source = 3


def _gather_both_cores_plan(s_refs, l_refs):
    x, y, c, chips = _mesh_place()
    jme = 2 * x + y
    plan = []
    for s, land in zip(s_refs, l_refs):
        for chip in chips:
            for peer_core in (c, 1 - c):
                plan.append((s.at[c], land.at[jme, c], (*chip, peer_core), land.at[2 * chip[0] + chip[1], peer_core]))
        plan.append((s, land.at[jme], (x, y, 1 - c), land.at[jme]))
    return plan


_gather_both_cores_plan.copies_per_source = 7


def _scatter_plan(s_refs, l_refs):
    x, y, c, chips = _mesh_place()
    return [(s.at[2 * chip[0] + chip[1]], land.at[k], (*chip, c), land.at[k])
            for s, land in zip(s_refs, l_refs) for k, chip in enumerate(chips)]


_scatter_plan.copies_per_source = 3


def _exchange_plan(g_refs, l_refs):
    x, y, c, _ = _mesh_place()
    return [(g.at[:, 1 - c], land, (x, y, 1 - c), land) for g, land in zip(g_refs, l_refs)]


_exchange_plan.copies_per_source = 1


def _join_plan(h_refs, l_refs):
    x, y, c, _ = _mesh_place()
    return [(h.at[c], h.at[c], (x, y, 1 - c), h.at[1 - c]) for h in h_refs]


_join_plan.copies_per_source = 1


def _gather_forward(name, shards, landed):
    nw = len(shards)

    def body(*refs):
        s_refs, o_refs = refs[:nw], refs[2 * nw:3 * nw]
        send_sems, recv_sems = refs[3 * nw:]
        x, y, c, chips = _mesh_place()
        me, sib, jme = (x, y, c), (x, y, 1 - c), 2 * x + y
        sent = []
        for w in range(nw):
            parts = [(o_refs[w].at[2 * chip[0] + chip[1], c], o_refs[w].at[2 * chip[0] + chip[1], c]) for chip in chips]
            parts.append((s_refs[w], o_refs[w].at[jme]))
            for k, (src, dst) in enumerate(parts):
                cp = pltpu.make_async_remote_copy(src_ref=src, dst_ref=dst, send_sem=send_sems.at[4 * w + k],
                                                  recv_sem=recv_sems.at[4 * w + k], device_id=sib, device_id_type=MESH)
                cp.start()
                sent.append(cp)
        for w in range(nw):
            parts = [o_refs[w].at[2 * chip[0] + chip[1], 1 - c] for chip in chips] + [o_refs[w].at[jme]]
            for k, part in enumerate(parts):
                pltpu.make_async_remote_copy(src_ref=part, dst_ref=part, send_sem=send_sems.at[4 * w + k],
                                             recv_sem=recv_sems.at[4 * w + k], device_id=me, device_id_type=MESH).wait_recv()
        for cp in sent:
            cp.wait_send()

    return pl.pallas_call(
        body, name=name,
        in_specs=_any_specs(2 * nw), out_specs=_any_specs(nw),
        out_shape=[jax.ShapeDtypeStruct(a.shape, a.dtype) for a in landed],
        input_output_aliases={nw + i: i for i in range(nw)},
        scratch_shapes=[pltpu.SemaphoreType.DMA((4 * nw,)), pltpu.SemaphoreType.DMA((4 * nw,))],
    )(*shards, *landed)


def _exchange_halves(name, grads, after=()):
    nw = len(grads)

    def body(*refs):
        g_refs, o_refs = refs[:nw], refs[nw + len(after):2 * nw + len(after)]
        send_sems, recv_sems = refs[2 * nw + len(after):]
        x, y, c, _ = _mesh_place()
        copies = []
        for w in range(nw):
            cp = pltpu.make_async_remote_copy(src_ref=g_refs[w].at[:, 1 - c], dst_ref=o_refs[w], send_sem=send_sems.at[w],
                                              recv_sem=recv_sems.at[w], device_id=(x, y, 1 - c), device_id_type=MESH)
            cp.start()
            copies.append(cp)
        for cp in copies:
            cp.wait()

    return pl.pallas_call(
        body, name=name,
        in_specs=_any_specs(nw + len(after)), out_specs=_any_specs(nw),
        out_shape=[jax.ShapeDtypeStruct((N_CHIPS,) + g.shape[2:], g.dtype) for g in grads],
        scratch_shapes=[pltpu.SemaphoreType.DMA((nw,)), pltpu.SemaphoreType.DMA((nw,))],
    )(*grads, *after)


def _row_tile(r):
    for cand in (256, 176, 128, 64, 32, 16, 8):
        if r % cand == 0:
            return cand
    return r


def _add_own_half(c_idx, grad, got):
    _, _, r, cols = grad.shape
    tr = _row_tile(r)

    def body(c_ref, g_ref, o_ref, out_ref):
        out_ref[...] = (g_ref[...].astype(F32) + o_ref[...].astype(F32)).astype(BF16)

    return pl.pallas_call(
        body, name="add_own_half",
        grid_spec=pltpu.PrefetchScalarGridSpec(
            num_scalar_prefetch=1, grid=(N_CHIPS, r // tr),
            in_specs=[pl.BlockSpec((None, None, tr, cols), lambda j, i, c_ref: (j, c_ref[0], i, 0)),
                      pl.BlockSpec((None, tr, cols), lambda j, i, c_ref: (j, i, 0))],
            out_specs=pl.BlockSpec((None, tr, cols), lambda j, i, c_ref: (j, i, 0))),
        out_shape=jax.ShapeDtypeStruct((N_CHIPS, r, cols), BF16),
        compiler_params=_params(("parallel", "parallel")),
    )(c_idx, grad, got)


def _sum_pieces(place_idx, sums, landed):
    _, r, cols = sums.shape
    tr = _row_tile(r)

    def body(j_ref, own_ref, p_ref, o_ref):
        o_ref[...] = ((own_ref[...].astype(F32) + p_ref[0].astype(F32)) + p_ref[1].astype(F32)) + p_ref[2].astype(F32)

    return pl.pallas_call(
        body, name="sum_pieces",
        grid_spec=pltpu.PrefetchScalarGridSpec(
            num_scalar_prefetch=1, grid=(r // tr,),
            in_specs=[pl.BlockSpec((None, tr, cols), lambda i, j_ref: (j_ref[0], i, 0)),
                      pl.BlockSpec((N_CHIPS - 1, tr, cols), lambda i, j_ref: (0, i, 0))],
            out_specs=pl.BlockSpec((None, tr, cols), lambda i, j_ref: (j_ref[1], i, 0))),
        out_shape=jax.ShapeDtypeStruct((2, r, cols), F32),
        compiler_params=_params(("parallel",)),
    )(place_idx, sums, landed)


def _join_halves(name, halves):
    nw = len(halves)

    def body(*refs):
        o_refs = refs[nw:2 * nw]
        send_sems, recv_sems = refs[2 * nw:]
        x, y, c, _ = _mesh_place()
        copies = []
        for w in range(nw):
            cp = pltpu.make_async_remote_copy(src_ref=o_refs[w].at[c], dst_ref=o_refs[w].at[c], send_sem=send_sems.at[w],
                                              recv_sem=recv_sems.at[w], device_id=(x, y, 1 - c), device_id_type=MESH)
            cp.start()
            copies.append(cp)
        for w in range(nw):
            copies[w].wait_send()
            landed = o_refs[w].at[1 - c]
            pltpu.make_async_remote_copy(src_ref=landed, dst_ref=landed, send_sem=send_sems.at[w], recv_sem=recv_sems.at[w],
                                         device_id=(x, y, c), device_id_type=MESH).wait_recv()

    return pl.pallas_call(
        body, name=name,
        in_specs=_any_specs(nw), out_specs=_any_specs(nw),
        out_shape=[jax.ShapeDtypeStruct(h.shape, F32) for h in halves],
        input_output_aliases={i: i for i in range(nw)},
        scratch_shapes=[pltpu.SemaphoreType.DMA((nw,)), pltpu.SemaphoreType.DMA((nw,))],
    )(*halves)


SMALL_ROWS = 8


def _all_reduce_small(pack):
    rows, cols = pack.shape
    n_dev = 8

    def body(p_ref, o_ref, slots, send_sems, recv_sems):
        x, y, c, _ = _mesh_place()
        me = 4 * x + 2 * y + c
        slots[me] = p_ref[...]
        copies = []
        for k in range(1, n_dev):
            peer = (me + k) % n_dev
            cp = pltpu.make_async_remote_copy(src_ref=p_ref, dst_ref=slots.at[me], send_sem=send_sems.at[k],
                                              recv_sem=recv_sems.at[k],
                                              device_id=(peer // 4, (peer // 2) % 2, peer % 2), device_id_type=MESH)
            cp.start()
            copies.append(cp)
        for k in range(1, n_dev):
            src = (me + n_dev - k) % n_dev
            pltpu.make_async_remote_copy(src_ref=p_ref, dst_ref=slots.at[src], send_sem=send_sems.at[k],
                                         recv_sem=recv_sems.at[k], device_id=(x, y, c), device_id_type=MESH).wait_recv()
        for cp in copies:
            cp.wait_send()
        total = slots[0]
        for s in range(1, n_dev):
            total = total + slots[s]
        o_ref[...] = total

    return pl.pallas_call(
        body, name="all_reduce_small",
        in_specs=[pl.BlockSpec(memory_space=pltpu.VMEM)], out_specs=pl.BlockSpec(memory_space=pltpu.VMEM),
        out_shape=jax.ShapeDtypeStruct((rows, cols), F32),
        scratch_shapes=[pltpu.VMEM((n_dev, rows, cols), F32), pltpu.SemaphoreType.DMA((n_dev,)),
                        pltpu.SemaphoreType.DMA((n_dev,))],
    )(pack)


def _adamw(name, w, g, m, v):
    r, cols = w.shape
    tr = _row_tile(r)

    def body(w_ref, g_ref, m_ref, v_ref, d_ref, nm_ref, nv_ref):
        gv = g_ref[...]
        nm = ADAM_B1 * m_ref[...] + (1.0 - ADAM_B1) * gv
        nv = ADAM_B2 * v_ref[...] + (1.0 - ADAM_B2) * (gv * gv)
        m_hat = nm / (1.0 - ADAM_B1 ** ADAM_STEP)
        v_hat = nv / (1.0 - ADAM_B2 ** ADAM_STEP)
        d_ref[...] = -ADAM_LR * (m_hat / (jnp.sqrt(v_hat) + ADAM_EPS) + ADAM_WD * w_ref[...])
        nm_ref[...] = nm
        nv_ref[...] = nv

    spec = pl.BlockSpec((tr, cols), lambda i: (i, 0))
    return pl.pallas_call(
        body, name=name, grid=(r // tr,),
        in_specs=[spec] * 4, out_specs=[spec] * 3,
        out_shape=[jax.ShapeDtypeStruct((r, cols), F32)] * 3,
        compiler_params=_params(("parallel",)),
    )(w, g, m, v)


BIG = ["ffn1_w_in", "ffn1_w_out", "w_in", "conv_w_proj", "attn_w_o", "w_out", "ffn2_w_in", "ffn2_w_out", "conv_dw_kernel"]
COL_SHARDED = ("ffn1_w_in", "w_in", "ffn2_w_in")
SMALL = ["ffn1_norm", "mix_norm", "ffn2_norm", "conv_dw_bias", "conv_ln_g", "conv_ln_b", "q_norm", "k_norm", "attn_sinks", "rel_bias"]
WEIGHTS = ["ffn1_norm", "ffn1_w_in", "ffn1_w_out", "mix_norm", "w_in", "conv_dw_kernel", "conv_dw_bias", "conv_ln_g",
           "conv_ln_b", "conv_w_proj", "q_norm", "k_norm", "attn_sinks", "rel_bias", "attn_w_o", "w_out", "ffn2_norm",
           "ffn2_w_in", "ffn2_w_out"]
SMALL_PLACE = {"ffn1_norm": (0, 0, 1024), "mix_norm": (1, 0, 1024), "ffn2_norm": (2, 0, 1024), "conv_dw_bias": (3, 0, 1024),
               "conv_ln_g": (4, 0, 1024), "conv_ln_b": (5, 0, 1024), "q_norm": (6, 0, 64), "k_norm": (6, 128, 64),
               "attn_sinks": (6, 256, 16), "rel_bias": (7, 0, 512)}
LOSS_PLACE = (6, 384)


def _pack_small(vals, fill=0.0, loss=None):
    pack = jnp.full((SMALL_ROWS, D_MODEL), fill, F32)
    for name, (row, lane, n) in SMALL_PLACE.items():
        pack = pack.at[row, lane:lane + n].set(vals[name].reshape(n))
    if loss is not None:
        pack = pack.at[LOSS_PLACE[0], LOSS_PLACE[1]].set(loss)
    return pack


def _unpack_small(pack, shapes):
    return {name: pack[row, lane:lane + n].reshape(shapes[name]) for name, (row, lane, n) in SMALL_PLACE.items()}


def _shard_halves(name, a):
    if name == "conv_dw_kernel":
        a = jnp.pad(a, ((0, CONV_PAD - CONV_WIDTH), (0, 0)))
    r, cols = a.shape
    return a.reshape(2, r // 2, cols)


GATHER_GROUPS = {"A": ["ffn1_w_in", "ffn1_w_out"],
                 "B": ["w_in", "conv_dw_kernel", "conv_w_proj", "attn_w_o", "w_out"],
                 "C": ["ffn2_w_in", "ffn2_w_out"]}


class _MeshComm:
    def __init__(self, wts):
        self.c_idx = lax.axis_index("c").astype(jnp.int32).reshape(1)
        self.place_idx = jnp.stack([2 * lax.axis_index("x") + lax.axis_index("y"), lax.axis_index("c")]).astype(jnp.int32)
        self.gathers, self.exchanges, self.reductions, self.joins, self.reduced = {}, {}, {}, {}, {}
        self.tokens = ()
        self.shards = {n: _shard_halves(n, wts[n]) if n == "conv_dw_kernel" else _shard_halves(n, wts[n]).astype(BF16)
                       for n in BIG}
        self._gather_start("A", ())

    def _gather_start(self, group, after):
        shards = [self.shards[n] for n in GATHER_GROUPS[group]]
        lands = [lax.empty((N_CHIPS,) + s.shape, s.dtype) for s in shards]
        self.gathers[group] = _copy_start("gather_start_" + group, shards, lands, self._gather_plan(group), after=after)
        self.tokens = (self.gathers[group][-1],)

    @staticmethod
    def _gather_plan(group):
        return _gather_both_cores_plan if group == "C" else _gather_plan

    def weights(self, group, after):
        send_sems, recv_sems, shards, lands, token = self.gathers.pop(group)
        after = [token if after is None else after]
        if group == "A":
            after += [self.shards[n] for g in ("B", "C") for n in GATHER_GROUPS[g]]
        shards, lands = _copy_wait("gather_wait_" + group, send_sems, recv_sems, shards, lands, after,
                                   self._gather_plan(group))
        gathered = lands if group == "C" else _gather_forward("gather_forward_" + group, shards, lands)
        self.tokens = ()
        following = {"A": "B", "B": "C"}.get(group)
        if following:
            self._gather_start(following, (gathered[0],))
        out = {}
        for n, g4 in zip(GATHER_GROUPS[group], gathered):
            r, cols = g4.shape[2] * 2, g4.shape[3]
            if n in COL_SHARDED:
                out[n] = g4.reshape(N_CHIPS, r, cols)
            elif n == "conv_dw_kernel":
                out[n] = g4.reshape(N_CHIPS, r, cols).transpose(1, 0, 2).reshape(r, N_CHIPS * cols)
            else:
                out[n] = g4.reshape(N_CHIPS * r, cols)
        return out

    def reduce_start(self, group, grads, behind=False):
        names = list(grads)
        g4 = []
        for n in names:
            a = grads[n]
            if n == "conv_dw_kernel":
                a = a.reshape(CONV_PAD, N_CHIPS, -1).transpose(1, 0, 2)
            elif n not in COL_SHARDED:
                a = a.reshape(N_CHIPS, a.shape[0] // N_CHIPS, a.shape[1])
            g4.append(a.reshape(N_CHIPS, 2, a.shape[1] // 2, a.shape[2]))
        if behind:
            lands = [lax.empty((N_CHIPS,) + g.shape[2:], g.dtype) for g in g4]
            started = _copy_start("exchange_start_" + group, g4, lands, _exchange_plan)
            self.exchanges[group] = (names,) + started
            return (started[-1],)
        return self._scatter_start(group, names, g4, _exchange_halves("exchange_halves_" + group, g4))

    def exchange_finish(self, group, after):
        names, send_sems, recv_sems, g4, lands, _ = self.exchanges.pop(group)
        g4, got = _copy_wait("exchange_wait_" + group, send_sems, recv_sems, g4, lands, after, _exchange_plan)
        return self._scatter_start(group, names, g4, got)

    def _scatter_start(self, group, names, g4, got):
        sums = [_add_own_half(self.c_idx, a, b) for a, b in zip(g4, got)]
        lands = [lax.empty((N_CHIPS - 1,) + s.shape[1:], s.dtype) for s in sums]
        started = _copy_start("scatter_start_" + group, sums, lands, _scatter_plan)
        self.reductions[group] = (names,) + started
        return (started[-1],)

    def reduce_finish(self, group, after, behind=False):
        names, send_sems, recv_sems, sums, lands, _ = self.reductions.pop(group)
        sums, lands = _copy_wait("scatter_wait_" + group, send_sems, recv_sems, sums, lands, after, _scatter_plan)
        halves = [_sum_pieces(self.place_idx, s, p) for s, p in zip(sums, lands)]
        if behind:
            started = _copy_start("join_start_" + group, halves, [], _join_plan)
            self.joins[group] = (names,) + started
            return (started[-1],)
        self.reduced.update(zip(names, _join_halves("join_halves_" + group, halves)))
        return ()

    def join_finish(self, group, after):
        names, send_sems, recv_sems, halves, _, _ = self.joins.pop(group)
        self.reduced.update(zip(names, _copy_wait("join_wait_" + group, send_sems, recv_sems, halves, [], after, _join_plan)[0]))


def kernel(x, ffn1_norm, ffn1_w_in, ffn1_w_out, mix_norm, w_in, conv_dw_kernel, conv_dw_bias, conv_ln_g, conv_ln_b, conv_w_proj, q_norm, k_norm, attn_sinks, rel_bias, attn_w_o, w_out, ffn2_norm, ffn2_w_in, ffn2_w_out, loss_target, m_ffn1_norm, m_ffn1_w_in, m_ffn1_w_out, m_mix_norm, m_w_in, m_conv_dw_kernel, m_conv_dw_bias, m_conv_ln_g, m_conv_ln_b, m_conv_w_proj, m_q_norm, m_k_norm, m_attn_sinks, m_rel_bias, m_attn_w_o, m_w_out, m_ffn2_norm, m_ffn2_w_in, m_ffn2_w_out, v_ffn1_norm, v_ffn1_w_in, v_ffn1_w_out, v_mix_norm, v_w_in, v_conv_dw_kernel, v_conv_dw_bias, v_conv_ln_g, v_conv_ln_b, v_conv_w_proj, v_q_norm, v_k_norm, v_attn_sinks, v_rel_bias, v_attn_w_o, v_w_out, v_ffn2_norm, v_ffn2_w_in, v_ffn2_w_out):
    args = dict(locals())
    wts = {n: args[n] for n in WEIGHTS}
    mom = {n: args["m_" + n] for n in WEIGHTS}
    var = {n: args["v_" + n] for n in WEIGHTS}
    comm = _MeshComm(wts)
    small = {n: wts[n] if n in ("attn_sinks", "rel_bias") else wts[n].reshape(1, -1) for n in SMALL}
    loss_part, grad_x, g = _local_step(x[0], loss_target[0], small, comm)

    small_sum = _all_reduce_small(_pack_small(g, loss=loss_part))
    loss = small_sum[LOSS_PLACE[0], LOSS_PLACE[1]]
    small_shapes = {n: wts[n].shape for n in SMALL}
    g_small = _unpack_small(small_sum, small_shapes)

    grads, delta, new_m, new_v = {}, {}, {}, {}
    for n in BIG:
        j = comm.reduced[n]
        gs = j.reshape(j.shape[1] * 2, j.shape[2])
        pad = n == "conv_dw_kernel"
        ws, ms, vs = (_shard_halves(n, a).reshape(gs.shape) for a in (wts[n], mom[n], var[n]))
        d, nm, nv = _adamw("adamw_" + n, ws, gs, ms, vs)
        cut = (lambda a: a[:CONV_WIDTH]) if pad else (lambda a: a)
        grads[n], delta[n], new_m[n], new_v[n] = cut(gs), cut(d), cut(nm), cut(nv)
    d, nm, nv = _adamw("adamw_small", _pack_small(wts), small_sum, _pack_small(mom), _pack_small(var, fill=1.0))
    grads.update(g_small)
    delta.update(_unpack_small(d, small_shapes))
    new_m.update(_unpack_small(nm, small_shapes))
    new_v.update(_unpack_small(nv, small_shapes))

    return (loss, grad_x[None], *[grads[n] for n in WEIGHTS], *[delta[n] for n in WEIGHTS],
            *[new_m[n] for n in WEIGHTS], *[new_v[n] for n in WEIGHTS])
```

```python
import functools
import math

import jax
import jax.numpy as jnp
from jax import lax
from jax.experimental import pallas as pl
from jax.experimental.pallas import tpu as pltpu

F32 = jnp.float32
BF16 = jnp.bfloat16
MESH = pl.DeviceIdType.MESH

EPS = 1e-6
D_MODEL = 1024
D_FF = 2816
N_CHIPS = 4
SHARD_W = 2 * D_FF // N_CHIPS
HEAD_DIM = 64
N_Q_HEADS = 16
N_KV_HEADS = 4
GROUP = N_Q_HEADS // N_KV_HEADS
BLOCK = 128
QROWS = GROUP * BLOCK
N_BUCKETS = 32
MAX_DISTANCE = 128
CONV_WIDTH = 31
CONV_PAD = 32
NEG = float(jnp.finfo(jnp.float32).min)

ADAM_LR = 0.001
ADAM_B1 = 0.9
ADAM_B2 = 0.999
ADAM_EPS = 1e-08
ADAM_WD = 0.01
ADAM_STEP = 10

VMEM_LIMIT_BYTES = 56 * 1024 * 1024
ROW_TILE = 1024
TOKEN_TILE = 1024
CONV_TILE = 256
CONV_ROWS = 128
LANES = 128

COL_CONV_A, COL_CONV_G, COL_Q, COL_K, COL_V, COL_GC, COL_GA = 0, 1024, 2048, 3072, 3328, 3584, 4608
IN_W = 5632


def _params(sem, vmem=VMEM_LIMIT_BYTES):
    return pltpu.CompilerParams(dimension_semantics=sem, vmem_limit_bytes=vmem)


def _sigmoid(x):
    return 1.0 / (1.0 + jnp.exp(-x))


def _dot(a, b, trans_a=False, trans_b=False, precision=None):
    dn = (((0,) if trans_a else (1,), (1,) if trans_b else (0,)), ((), ()))
    return lax.dot_general(a, b, dn, preferred_element_type=F32, precision=precision)


def _mm(name, grid, a, a_spec, b, b_spec, acc_shape, *, trans_a=False, trans_b=False, a_pre=None, b_pre=None,
        extras=(), extra_specs=(), tokens=(), out_shape, out_specs, epilogue, chunked=False,
        sem=("parallel", "parallel", "arbitrary")):
    n_k = grid[2]
    assert not chunked or (n_k == 1 and b_pre is None)
    extras = tuple(extras) + tuple(tokens)
    extra_specs = tuple(extra_specs) + (pl.BlockSpec((8, LANES), lambda i, j, kk: (0, 0)),) * len(tokens)
    n_extra = len(extras)
    n_out = len(out_shape)

    def body(a_ref, b_ref, *rest):
        ex = rest[:n_extra]
        outs = rest[n_extra:n_extra + n_out]
        ids = (pl.program_id(0), pl.program_id(1), pl.program_id(2))
        av = a_ref[...]
        if a_pre is not None:
            av = a_pre(av)
        if chunked:
            for c0, cw in _col_chunks(acc_shape[1]):
                cols = pl.ds(c0, cw)
                epilogue(_dot(av, b_ref[cols, :] if trans_b else b_ref[:, cols], trans_a, trans_b), ex, outs, ids, cols)
            return
        bv = b_ref[...]
        if b_pre is not None:
            bv = b_pre(bv)
        if n_k == 1:
            epilogue(_dot(av, bv, trans_a, trans_b), ex, outs, ids)
        else:
            acc = rest[-1]

            @pl.when(ids[2] == 0)
            def _():
                acc[...] = jnp.zeros_like(acc)

            acc[...] += _dot(av, bv, trans_a, trans_b)

            @pl.when(ids[2] == n_k - 1)
            def _():
                epilogue(acc[...], ex, outs, ids)

    scratch = [] if n_k == 1 else [pltpu.VMEM(acc_shape, F32)]
    return pl.pallas_call(
        body, name=name, grid=grid,
        in_specs=[a_spec, b_spec, *extra_specs],
        out_specs=list(out_specs), out_shape=list(out_shape),
        scratch_shapes=scratch, compiler_params=_params(sem),
    )(a, b, *extras)


MXU_WIDTH = 256


def _col_chunks(n, width=2 * MXU_WIDTH):
    return [(c0, min(width, n - c0)) for c0 in range(0, n, width)]


def _half_bf16(v):
    return (0.5 * v).astype(BF16)


def _to_bf16(v):
    return v.astype(BF16)


def _rmsnorm_fwd(name, x, g, tokens=()):
    t, d = x.shape
    tm = min(ROW_TILE, t)

    def body(x_ref, g_ref, *rest):
        o_ref = rest[-1]
        xv = x_ref[...]
        r = lax.rsqrt(jnp.mean(xv * xv, axis=-1, keepdims=True) + EPS)
        o_ref[...] = (xv * r * g_ref[...]).astype(BF16)

    return pl.pallas_call(
        body, name=name, grid=(t // tm,),
        in_specs=[pl.BlockSpec((tm, d), lambda i: (i, 0)), pl.BlockSpec((1, d), lambda i: (0, 0))]
        + [pl.BlockSpec((8, LANES), lambda i: (0, 0))] * len(tokens),
        out_specs=pl.BlockSpec((tm, d), lambda i: (i, 0)),
        out_shape=jax.ShapeDtypeStruct((t, d), BF16),
        compiler_params=_params(("parallel",)),
    )(x, g, *tokens)


def _rms_bwd_epilogue(acc, ex, outs, ids):
    x_ref, g_ref, dres_ref = ex[:3]
    out_ref, dg_ref = outs
    xv = x_ref[...]
    r = lax.rsqrt(jnp.mean(xv * xv, axis=-1, keepdims=True) + EPS)
    w = acc * g_ref[...]
    dx = r * w - xv * (r * r * r) * jnp.mean(xv * w, axis=-1, keepdims=True)
    out_ref[...] = dres_ref[...] + dx
    part = jnp.sum(acc * (xv * r), axis=0, keepdims=True)

    @pl.when(ids[0] == 0)
    def _():
        dg_ref[...] = part

    @pl.when(ids[0] > 0)
    def _():
        dg_ref[...] += part


def _ffn_in(name, n, w_in4, tokens=()):
    t, d = n.shape
    tm = min(ROW_TILE, t)

    def body(n_ref, wa_ref, wb_ref, *rest):
        ab_ref, h_ref = rest[-2:]
        nv = n_ref[...]
        for c0, cw in _col_chunks(SHARD_W):
            cols = pl.ds(c0, cw)
            a = _dot(nv, wa_ref[:, cols])
            b = _dot(nv, wb_ref[:, cols])
            h_ref[:, cols] = (a * _sigmoid(a) * b).astype(BF16)
            ab_ref[0, :, cols] = a.astype(BF16)
            ab_ref[1, :, cols] = b.astype(BF16)

    return pl.pallas_call(
        body, name=name, grid=(2, t // tm),
        in_specs=[pl.BlockSpec((tm, d), lambda j, i: (i, 0)),
                  pl.BlockSpec((None, d, SHARD_W), lambda j, i: (j, 0, 0)),
                  pl.BlockSpec((None, d, SHARD_W), lambda j, i: (j + 2, 0, 0))]
        + [pl.BlockSpec((8, LANES), lambda j, i: (0, 0))] * len(tokens),
        out_specs=[pl.BlockSpec((2, tm, SHARD_W), lambda j, i: (0, i, j)),
                   pl.BlockSpec((tm, SHARD_W), lambda j, i: (i, j))],
        out_shape=[jax.ShapeDtypeStruct((2, t, D_FF), BF16), jax.ShapeDtypeStruct((t, D_FF), BF16)],
        compiler_params=_params(("parallel", "parallel")),
    )(n, w_in4, w_in4, *tokens)


def _mm_residual(name, a, w, res, scale, next_gain=None, loss_target=None):
    t, k = a.shape
    n = w.shape[1]
    tm = min(ROW_TILE, t)
    row = pl.BlockSpec((tm, n), lambda i, j, kk: (i, 0))
    extras, specs = [res], [row]
    shapes, out_specs = [jax.ShapeDtypeStruct((t, n), F32)], [row]
    if next_gain is not None:
        extras.append(next_gain)
        specs.append(pl.BlockSpec((1, n), lambda i, j, kk: (0, 0)))
        shapes.append(jax.ShapeDtypeStruct((t, n), BF16))
        out_specs.append(row)
    if loss_target is not None:
        extras.append(loss_target)
        specs.append(row)
        shapes.append(jax.ShapeDtypeStruct((8, LANES), F32))
        out_specs.append(pl.BlockSpec((8, LANES), lambda i, j, kk: (0, 0)))

    def epilogue(acc, ex, outs, ids):
        y = ex[0][...] + scale * acc
        if loss_target is None:
            outs[0][...] = y
        if next_gain is not None:
            r = lax.rsqrt(jnp.mean(y * y, axis=-1, keepdims=True) + EPS)
            outs[1][...] = (y * r * ex[1][...]).astype(BF16)
        if loss_target is not None:
            diff = y - ex[1][...]
            outs[0][...] = diff * (1.0 / n)
            part = jnp.full((8, LANES), 0.5 / n * jnp.sum(diff * diff), F32)

            @pl.when(ids[0] == 0)
            def _():
                outs[1][...] = part

            @pl.when(ids[0] > 0)
            def _():
                outs[1][...] += part

    sem = ("parallel" if loss_target is None else "arbitrary", "parallel", "arbitrary")
    out = _mm(name, (t // tm, 1, 1), a, pl.BlockSpec((tm, k), lambda i, j, kk: (i, 0)),
              w, pl.BlockSpec((k, n), lambda i, j, kk: (0, 0)), (tm, n),
              extras=extras, extra_specs=specs, out_shape=shapes, out_specs=out_specs, epilogue=epilogue, sem=sem)
    return out[0] if len(out) == 1 else tuple(out)


def _ffn_fwd(tag, x, n, w_in4, w_out, tokens=(), **tail):
    ab, h = _ffn_in(tag + "_in", n, w_in4, tokens)
    y = _mm_residual(tag + "_out", h, w_out, x, 0.5, **tail)
    return y, (n, ab, h)


def _ffn_bwd(tag, dres, x, g, saved, w_in4, w_out, tokens=(), on_first=None, on_weight_grads=None):
    n, ab, h = saved
    t, d = x.shape
    tm = min(ROW_TILE, t)
    tk = min(TOKEN_TILE, t)
    half_w = SHARD_W

    def dact_epilogue(acc, ex, outs, ids, cols):
        a = ex[0][0, :, cols].astype(F32)
        b = ex[0][1, :, cols].astype(F32)
        sig = _sigmoid(a)
        outs[0][0, :, cols] = (acc * b * (sig * (1.0 + a * (1.0 - sig)))).astype(BF16)
        outs[0][1, :, cols] = (acc * (a * sig)).astype(BF16)

    du = _mm(tag + "_dact", (2, t // tm, 1),
             dres, pl.BlockSpec((tm, d), lambda j, i, kk: (i, 0)),
             w_out, pl.BlockSpec((half_w, d), lambda j, i, kk: (j, 0)), (tm, half_w),
             trans_b=True, a_pre=_half_bf16,
             extras=(ab,), extra_specs=(pl.BlockSpec((2, tm, half_w), lambda j, i, kk: (0, i, j)),), tokens=tokens,
             out_shape=(jax.ShapeDtypeStruct((2, t, D_FF), BF16),),
             out_specs=(pl.BlockSpec((2, tm, half_w), lambda j, i, kk: (0, i, j)),),
             epilogue=dact_epilogue, chunked=True)[0]

    def store_epilogue(acc, ex, outs, ids):
        outs[0][...] = acc.astype(BF16)

    early = () if on_first is None else on_first(du)

    dw_out = _mm(tag + "_dwout", (2, 1, t // tk),
                 h, pl.BlockSpec((tk, half_w), lambda i, j, kk: (kk, i)),
                 dres, pl.BlockSpec((tk, d), lambda i, j, kk: (kk, 0)), (half_w, d),
                 trans_a=True, b_pre=_half_bf16, tokens=early,
                 out_shape=(jax.ShapeDtypeStruct((D_FF, d), BF16),),
                 out_specs=(pl.BlockSpec((half_w, d), lambda i, j, kk: (i, 0)),),
                 epilogue=store_epilogue)[0]

    dw_in4 = _mm(tag + "_dwin", (1, N_CHIPS, t // tk),
                 n, pl.BlockSpec((tk, d), lambda i, j, kk: (kk, 0)),
                 du, pl.BlockSpec((None, tk, SHARD_W), lambda i, j, kk: (j // 2, kk, j % 2)), (d, SHARD_W),
                 trans_a=True,
                 out_shape=(jax.ShapeDtypeStruct((N_CHIPS, d, SHARD_W), BF16),),
                 out_specs=(pl.BlockSpec((None, d, SHARD_W), lambda i, j, kk: (j, 0, 0)),),
                 epilogue=store_epilogue)[0]

    late = () if on_weight_grads is None else on_weight_grads(dw_in4, dw_out)

    dx, dg = _mm(tag + "_dn", (t // tm, 1, N_CHIPS),
                 du, pl.BlockSpec((None, tm, SHARD_W), lambda i, j, kk: (kk // 2, i, kk % 2)),
                 w_in4, pl.BlockSpec((None, d, SHARD_W), lambda i, j, kk: (kk, 0, 0)), (tm, d),
                 trans_b=True,
                 extras=(x, g, dres),
                 extra_specs=(pl.BlockSpec((tm, d), lambda i, j, kk: (i, 0)),
                              pl.BlockSpec((1, d), lambda i, j, kk: (0, 0)),
                              pl.BlockSpec((tm, d), lambda i, j, kk: (i, 0))), tokens=late,
                 out_shape=(jax.ShapeDtypeStruct((t, d), F32), jax.ShapeDtypeStruct((1, d), F32)),
                 out_specs=(pl.BlockSpec((tm, d), lambda i, j, kk: (i, 0)),
                            pl.BlockSpec((1, d), lambda i, j, kk: (0, 0))),
                 epilogue=_rms_bwd_epilogue, sem=("arbitrary", "arbitrary", "arbitrary"))
    return dx, dw_in4, dw_out, dg


def _conv_fill(zp_ref, a_ref, g_ref, ah_ref, gh_ref, i):
    zh = ah_ref[...].astype(F32) * _sigmoid(gh_ref[...].astype(F32))
    zp_ref[pl.ds(0, CONV_PAD), :] = jnp.where(i > 0, zh, 0.0)
    zp_ref[pl.ds(CONV_PAD, a_ref.shape[0]), :] = a_ref[...].astype(F32) * _sigmoid(g_ref[...].astype(F32))


def _shift_groups(shifts):
    groups = {}
    for j, s in shifts:
        groups.setdefault(s % 8, []).append((j, s // 8))
    return groups


def _windows(zp_ref, r0, lanes, groups):
    for q, taps in groups.items():
        deepest = max(p for _, p in taps)
        win = zp_ref[pl.ds(r0 + q, 8 * deepest + CONV_ROWS), lanes]
        for j, p in taps:
            yield j, win[8 * p:8 * p + CONV_ROWS]


def _conv_apply(zp_ref, out_ref, dw_ref, bias_ref, tm, ch, shifts):
    groups = _shift_groups(shifts)
    for cc in range(ch // LANES):
        lanes = pl.ds(cc * LANES, LANES)
        w = [dw_ref[pl.ds(j, 1), lanes] for j in range(CONV_WIDTH)]
        for r0 in range(0, tm, CONV_ROWS):
            if bias_ref is None:
                acc = jnp.zeros((CONV_ROWS, LANES), F32)
            else:
                acc = jnp.broadcast_to(bias_ref[:, lanes], (CONV_ROWS, LANES))
            for j, rows in _windows(zp_ref, r0, lanes, groups):
                acc = acc + w[j] * rows
            out_ref[pl.ds(r0, CONV_ROWS), lanes] = acc


FWD_SHIFTS = [(j, CONV_PAD - (CONV_WIDTH - 1) + j) for j in range(CONV_WIDTH)]
BWD_SHIFTS = [(j, CONV_WIDTH - 1 - j) for j in range(CONV_WIDTH)]


def _conv_taps(zp_ref, z1_ref, dw_ref, bias_ref, tm, ch):
    _conv_apply(zp_ref, z1_ref, dw_ref, bias_ref, tm, ch, FWD_SHIFTS)


def _conv_specs(tm, ch):
    per = tm // CONV_PAD
    cb = COL_CONV_G // ch
    return [pl.BlockSpec((tm, ch), lambda i: (i, 0)),
            pl.BlockSpec((tm, ch), lambda i: (i, cb)),
            pl.BlockSpec((CONV_PAD, ch), lambda i: (jnp.maximum(i * per - 1, 0), 0)),
            pl.BlockSpec((CONV_PAD, ch), lambda i: (jnp.maximum(i * per - 1, 0), cb))]


def _conv_fwd(p, dw, bias, ln_g, ln_b):
    t = p.shape[0]
    ch = D_MODEL
    tm = min(CONV_TILE, t)

    def body(a_ref, g_ref, ah_ref, gh_ref, dw_ref, bias_ref, lg_ref, lb_ref, o_ref, z1_ref, zp_ref):
        i = pl.program_id(0)
        _conv_fill(zp_ref, a_ref, g_ref, ah_ref, gh_ref, i)
        _conv_taps(zp_ref, z1_ref, dw_ref, bias_ref, tm, ch)
        z1 = z1_ref[...]
        mu = jnp.mean(z1, axis=-1, keepdims=True)
        zc = z1 - mu
        rs = lax.rsqrt(jnp.mean(zc * zc, axis=-1, keepdims=True) + EPS)
        z2 = zc * rs * lg_ref[...] + lb_ref[...]
        o_ref[...] = (z2 * _sigmoid(z2)).astype(BF16)

    vec = pl.BlockSpec((1, ch), lambda i: (0, 0))
    return pl.pallas_call(
        body, name="conv_fwd", grid=(t // tm,),
        in_specs=_conv_specs(tm, ch) + [pl.BlockSpec((CONV_PAD, ch), lambda i: (0, 0)), vec, vec, vec],
        out_specs=[pl.BlockSpec((tm, ch), lambda i: (i, 0)), pl.BlockSpec((tm, ch), lambda i: (i, 0))],
        out_shape=[jax.ShapeDtypeStruct((t, ch), BF16), jax.ShapeDtypeStruct((t, ch), F32)],
        scratch_shapes=[pltpu.VMEM((CONV_PAD + tm, ch), F32)],
        compiler_params=_params(("parallel",)),
    )(p, p, p, p, dw, bias, ln_g, ln_b)


def _conv_bwd_ln(p, z1_saved, dz3, ln_g, ln_b):
    t = p.shape[0]
    ch = D_MODEL
    tm = min(CONV_TILE, t)

    def body(a_ref, g_ref, ah_ref, gh_ref, z1_ref, dz3_ref, lg_ref, lb_ref,
             dz1_ref, ddw_ref, dbias_ref, dlg_ref, dlb_ref, zp_ref):
        i = pl.program_id(0)
        _conv_fill(zp_ref, a_ref, g_ref, ah_ref, gh_ref, i)
        z1 = z1_ref[...]
        mu = jnp.mean(z1, axis=-1, keepdims=True)
        zc = z1 - mu
        rs = lax.rsqrt(jnp.mean(zc * zc, axis=-1, keepdims=True) + EPS)
        xh = zc * rs
        z2 = xh * lg_ref[...] + lb_ref[...]
        sig = _sigmoid(z2)
        dz2 = dz3_ref[...].astype(F32) * (sig * (1.0 + z2 * (1.0 - sig)))
        dxh = dz2 * lg_ref[...]
        dz1 = rs * (dxh - jnp.mean(dxh, axis=-1, keepdims=True) - xh * jnp.mean(dxh * xh, axis=-1, keepdims=True))
        dz1_ref[...] = dz1

        @pl.when(i == 0)
        def _():
            ddw_ref[...] = jnp.zeros_like(ddw_ref)
            dbias_ref[...] = jnp.zeros_like(dbias_ref)
            dlg_ref[...] = jnp.zeros_like(dlg_ref)
            dlb_ref[...] = jnp.zeros_like(dlb_ref)

        dlg_ref[...] += jnp.sum(dz2 * xh, axis=0, keepdims=True)
        dlb_ref[...] += jnp.sum(dz2, axis=0, keepdims=True)
        dbias_ref[...] += jnp.sum(dz1, axis=0, keepdims=True)
        groups = _shift_groups(FWD_SHIFTS)
        for cc in range(ch // LANES):
            lanes = pl.ds(cc * LANES, LANES)
            accs = [jnp.zeros((8, LANES), F32) for _ in range(CONV_WIDTH)]
            for r0 in range(0, tm, CONV_ROWS):
                dzc = dz1_ref[pl.ds(r0, CONV_ROWS), lanes]
                for j, rows in _windows(zp_ref, r0, lanes, groups):
                    accs[j] = accs[j] + jnp.sum((dzc * rows).reshape(CONV_ROWS // 8, 8, LANES), axis=0)
            for j in range(CONV_WIDTH):
                ddw_ref[pl.ds(j, 1), lanes] += jnp.sum(accs[j], axis=0, keepdims=True)

    vec = pl.BlockSpec((1, ch), lambda i: (0, 0))
    return pl.pallas_call(
        body, name="conv_bwd_ln", grid=(t // tm,),
        in_specs=_conv_specs(tm, ch) + [pl.BlockSpec((tm, ch), lambda i: (i, 0)),
                                        pl.BlockSpec((tm, ch), lambda i: (i, 0)), vec, vec],
        out_specs=[pl.BlockSpec((tm, ch), lambda i: (i, 0)), pl.BlockSpec((CONV_PAD, ch), lambda i: (0, 0)), vec, vec, vec],
        out_shape=[jax.ShapeDtypeStruct((t, ch), F32), jax.ShapeDtypeStruct((CONV_PAD, ch), F32)]
        + [jax.ShapeDtypeStruct((1, ch), F32)] * 3,
        scratch_shapes=[pltpu.VMEM((CONV_PAD + tm, ch), F32)],
        compiler_params=_params(("arbitrary",)),
    )(p, p, p, p, z1_saved, dz3, ln_g, ln_b)


def _conv_bwd_glu(p, dz1, dw, dq, dkv, dgates):
    t = p.shape[0]
    ch = D_MODEL
    tm = min(CONV_TILE, t)
    per = tm // CONV_PAD
    n_halo = t // CONV_PAD
    cb = COL_CONV_G // ch

    def body(a_ref, g_ref, dz_ref, dzn_ref, dw_ref, dq_ref, dkv_ref, dgates_ref, o_ref, zp_ref, z0_ref):
        i = pl.program_id(0)
        o_ref[:, pl.ds(COL_Q, Q_W)] = dq_ref[...]
        o_ref[:, pl.ds(COL_K, 2 * KV_W)] = dkv_ref[...]
        o_ref[:, pl.ds(COL_GC, ch)] = dgates_ref[0]
        o_ref[:, pl.ds(COL_GA, ch)] = dgates_ref[1]
        zp_ref[pl.ds(0, tm), :] = dz_ref[...]
        zp_ref[pl.ds(tm, CONV_PAD), :] = jnp.where(i < t // tm - 1, dzn_ref[...], 0.0)
        _conv_apply(zp_ref, z0_ref, dw_ref, None, tm, ch, BWD_SHIFTS)
        dz0 = z0_ref[...]
        a = a_ref[...].astype(F32)
        sig = _sigmoid(g_ref[...].astype(F32))
        o_ref[:, pl.ds(0, ch)] = (dz0 * sig).astype(BF16)
        o_ref[:, pl.ds(ch, ch)] = (dz0 * a * sig * (1.0 - sig)).astype(BF16)

    return pl.pallas_call(
        body, name="conv_bwd_glu", grid=(t // tm,),
        in_specs=[pl.BlockSpec((tm, ch), lambda i: (i, 0)), pl.BlockSpec((tm, ch), lambda i: (i, cb)),
                  pl.BlockSpec((tm, ch), lambda i: (i, 0)),
                  pl.BlockSpec((CONV_PAD, ch), lambda i: (jnp.minimum((i + 1) * per, n_halo - 1), 0)),
                  pl.BlockSpec((CONV_PAD, ch), lambda i: (0, 0)),
                  pl.BlockSpec((tm, Q_W), lambda i: (i, 0)), pl.BlockSpec((tm, 2 * KV_W), lambda i: (i, 0)),
                  pl.BlockSpec((2, tm, ch), lambda i: (0, i, 0))],
        out_specs=pl.BlockSpec((tm, IN_W), lambda i: (i, 0)),
        out_shape=jax.ShapeDtypeStruct((t, IN_W), BF16),
        scratch_shapes=[pltpu.VMEM((tm + CONV_PAD, ch), F32), pltpu.VMEM((tm, ch), F32)],
        compiler_params=_params(("parallel",)),
    )(p, p, dz1, dz1, dw, dq, dkv, dgates)


def _bucket_onehot():
    qi = jnp.arange(BLOCK, dtype=jnp.int32)[:, None]
    kj = jnp.arange(2 * BLOCK, dtype=jnp.int32)[None, :]
    dist = jnp.maximum(qi + BLOCK - kj, 0)
    max_exact = N_BUCKETS // 2
    dflt = jnp.maximum(dist, 1).astype(F32)
    large = max_exact + (jnp.log(dflt / max_exact) / math.log(MAX_DISTANCE / max_exact)
                         * (N_BUCKETS - max_exact)).astype(jnp.int32)
    large = jnp.minimum(large, N_BUCKETS - 1)
    bucket = jnp.where(dist < max_exact, dist, large)
    onehot = bucket[None] == jnp.arange(N_BUCKETS, dtype=jnp.int32)[:, None, None]
    return onehot.astype(F32).reshape(N_BUCKETS, BLOCK * 2 * BLOCK)


def _bias_table(rel_bias_t, onehot):
    n = onehot.shape[1]
    tn = 4096

    def body(r_ref, oh_ref, o_ref):
        flat = pl.program_id(0) * tn + lax.broadcasted_iota(jnp.int32, (N_Q_HEADS, tn), 1)
        dist = (flat // (2 * BLOCK)) + BLOCK - (flat % (2 * BLOCK))
        bias = _dot(r_ref[...], oh_ref[...], precision=lax.Precision.HIGHEST)
        o_ref[...] = jnp.where((dist >= 0) & (dist < BLOCK), bias, NEG)

    return pl.pallas_call(
        body, name="bias_table", grid=(n // tn,),
        in_specs=[pl.BlockSpec((N_Q_HEADS, N_BUCKETS), lambda i: (0, 0)), pl.BlockSpec((N_BUCKETS, tn), lambda i: (0, i))],
        out_specs=pl.BlockSpec((N_Q_HEADS, tn), lambda i: (0, i)),
        out_shape=jax.ShapeDtypeStruct((N_Q_HEADS, n), F32),
        compiler_params=_params(("parallel",)),
    )(rel_bias_t, onehot)


def _bias_table_bwd(dbias, onehot):
    n = onehot.shape[1]
    tn = 4096

    def body(d_ref, oh_ref, o_ref):
        part = _dot(d_ref[...], oh_ref[...], trans_b=True, precision=lax.Precision.HIGHEST)
        i = pl.program_id(0)

        @pl.when(i == 0)
        def _():
            o_ref[...] = part

        @pl.when(i > 0)
        def _():
            o_ref[...] += part

    return pl.pallas_call(
        body, name="bias_table_bwd", grid=(n // tn,),
        in_specs=[pl.BlockSpec((N_Q_HEADS, tn), lambda i: (0, i)), pl.BlockSpec((N_BUCKETS, tn), lambda i: (0, i))],
        out_specs=pl.BlockSpec((N_Q_HEADS, N_BUCKETS), lambda i: (0, 0)),
        out_shape=jax.ShapeDtypeStruct((N_Q_HEADS, N_BUCKETS), F32),
        compiler_params=_params(("arbitrary",)),
    )(dbias, onehot)


def _lane_head(rows):
    return lax.broadcasted_iota(jnp.int32, (rows, KV_W), 1) // HEAD_DIM


def _group_rms(x, gain_wide):
    head = _lane_head(x.shape[0])
    sq = x * x
    r = jnp.zeros_like(x)
    for i in range(N_KV_HEADS):
        ms = jnp.sum(jnp.where(head == i, sq, 0.0), axis=-1, keepdims=True) * (1.0 / HEAD_DIM)
        r = jnp.where(head == i, lax.rsqrt(ms + EPS), r)
    return r, x * r * gain_wide


def _stack_heads(group):
    head = _lane_head(group.shape[0])
    return jnp.concatenate([jnp.where(head == i, group, jnp.zeros_like(group)) for i in range(N_KV_HEADS)], axis=0)


def _unstack_heads(stacked):
    head = _lane_head(BLOCK)
    out = jnp.where(head == 0, stacked[:BLOCK], 0.0)
    for i in range(1, N_KV_HEADS):
        out = out + jnp.where(head == i, stacked[i * BLOCK:(i + 1) * BLOCK], 0.0)
    return out


def _repeaters():
    row = lax.broadcasted_iota(jnp.int32, (KV_W, KV_W), 0)
    col = lax.broadcasted_iota(jnp.int32, (KV_W, KV_W), 1)
    return [(row == h * HEAD_DIM + col % HEAD_DIM).astype(BF16) for h in range(N_KV_HEADS)]


def _attn_probs(q_stack, k_rep, sink, bias, before_start):
    s = _dot(q_stack, k_rep, trans_b=True) * (1.0 / math.sqrt(HEAD_DIM)) + bias
    s = jnp.where(before_start, NEG, s)
    m = jnp.maximum(jnp.max(s, axis=-1, keepdims=True), sink)
    p = jnp.exp(s - m)
    es = jnp.exp(sink - m)
    inv = 1.0 / (jnp.sum(p, axis=-1, keepdims=True) + es)
    return p * inv, es * inv


def _before_start(n):
    col = lax.broadcasted_iota(jnp.int32, (QROWS, 2 * BLOCK), 1)
    return (col < BLOCK) & (n == 0)


FWD_BLOCKS = 2
KV_W = N_KV_HEADS * HEAD_DIM
Q_W = N_Q_HEADS * HEAD_DIM


def _attn_specs():
    qspec = pl.BlockSpec((BLOCK, Q_W), lambda n: (n, COL_Q // Q_W))
    kprev = pl.BlockSpec((BLOCK, KV_W), lambda n: (jnp.maximum(n - 1, 0), COL_K // KV_W))
    kcur = pl.BlockSpec((BLOCK, KV_W), lambda n: (n, COL_K // KV_W))
    vprev = pl.BlockSpec((BLOCK, KV_W), lambda n: (jnp.maximum(n - 1, 0), COL_V // KV_W))
    vcur = pl.BlockSpec((BLOCK, KV_W), lambda n: (n, COL_V // KV_W))
    gain = pl.BlockSpec((1, KV_W), lambda n: (0, 0))
    sink = pl.BlockSpec((N_KV_HEADS, QROWS, 1), lambda n: (0, 0, 0))
    bias = pl.BlockSpec((N_KV_HEADS, QROWS, 2 * BLOCK), lambda n: (0, 0, 0))
    return [qspec, kprev, kcur, vprev, vcur], gain, sink, bias


def _attn_fwd(p, gq, gk, sink_rows, bias):
    t = p.shape[0]
    nb = t // BLOCK
    per = FWD_BLOCKS if nb % FWD_BLOCKS == 0 else 1
    _, gain, sink, bspec = _attn_specs()

    def body(q_ref, kp_ref, kc_ref, vp_ref, vc_ref, gq_ref, gk_ref, sink_ref, bias_ref, o_ref, p_ref, ps_ref):
        first = pl.program_id(0) * per
        rep = _repeaters()
        kf = jnp.concatenate([kp_ref[...], kc_ref[...]], axis=0).astype(F32)
        kn = _group_rms(kf, gk_ref[...])[1].astype(BF16)
        v = jnp.concatenate([vp_ref[...], vc_ref[...]], axis=0)
        for h in range(N_KV_HEADS):
            k_rep = _dot(kn, rep[h]).astype(BF16)
            v_rep = _dot(v, rep[h]).astype(BF16)
            for sub in range(per):
                rows = pl.ds(sub * BLOCK, BLOCK)
                window = slice(sub * BLOCK, (sub + 2) * BLOCK)
                qn = _group_rms(q_ref[rows, pl.ds(h * KV_W, KV_W)].astype(F32), gq_ref[...])[1]
                pn, ps_ref[sub, h] = _attn_probs(_stack_heads(qn).astype(BF16), k_rep[window], sink_ref[h], bias_ref[h],
                                                 _before_start(first + sub))
                pn = pn.astype(BF16)
                p_ref[sub, h] = pn
                o_ref[rows, pl.ds(h * KV_W, KV_W)] = _unstack_heads(_dot(pn, v_rep[window])).astype(BF16)

    def kv_specs(col):
        return [pl.BlockSpec((BLOCK, KV_W), lambda n: (jnp.maximum(n * per - 1, 0), col // KV_W)),
                pl.BlockSpec((per * BLOCK, KV_W), lambda n: (n, col // KV_W))]

    return pl.pallas_call(
        body, name="attn_fwd", grid=(nb // per,),
        in_specs=[pl.BlockSpec((per * BLOCK, Q_W), lambda n: (n, COL_Q // Q_W))] + kv_specs(COL_K) + kv_specs(COL_V)
        + [gain, gain, sink, bspec],
        out_specs=[pl.BlockSpec((per * BLOCK, Q_W), lambda n: (n, 0)),
                   pl.BlockSpec((per, N_KV_HEADS, QROWS, 2 * BLOCK), lambda n: (n, 0, 0, 0)),
                   pl.BlockSpec((per, N_KV_HEADS, QROWS, 1), lambda n: (n, 0, 0, 0))],
        out_shape=[jax.ShapeDtypeStruct((t, Q_W), BF16),
                   jax.ShapeDtypeStruct((nb, N_KV_HEADS, QROWS, 2 * BLOCK), BF16),
                   jax.ShapeDtypeStruct((nb, N_KV_HEADS, QROWS, 1), F32)],
        compiler_params=_params(("parallel",)),
    )(p, p, p, p, p, gq, gk, sink_rows, bias)


def _attn_bwd(p, do, gq, gk, probs, sink_probs):
    t = p.shape[0]
    nb = t // BLOCK
    qkv, _, sink, bspec = _attn_specs()
    gain = pl.BlockSpec((1, HEAD_DIM), lambda n: (0, 0))
    scale = 1.0 / math.sqrt(HEAD_DIM)

    def head_selectors():
        row = lax.broadcasted_iota(jnp.int32, (KV_W, HEAD_DIM), 0)
        col = lax.broadcasted_iota(jnp.int32, (KV_W, HEAD_DIM), 1)
        return [(row == col + i * HEAD_DIM).astype(BF16) for i in range(N_KV_HEADS)]

    def take_heads(group, sel):
        return jnp.concatenate([_dot(group, s) for s in sel], axis=0)

    def put_heads(x, sel):
        rows = x.shape[0] // len(sel)
        out = _dot(x[:rows].astype(BF16), sel[0], trans_b=True)
        for i in range(1, len(sel)):
            out = out + _dot(x[i * rows:(i + 1) * rows].astype(BF16), sel[i], trans_b=True)
        return out

    def rms(x, g):
        r = lax.rsqrt(jnp.mean(x * x, axis=-1, keepdims=True) + EPS)
        return r, x * r * g

    def rms_bwd(dn, xf, r, g):
        w = dn * g
        dx = r * w - xf * (r * r * r) * jnp.mean(xf * w, axis=-1, keepdims=True)
        return dx, jnp.sum(dn * (xf * r), axis=0, keepdims=True)

    def body(q_ref, kp_ref, kc_ref, vp_ref, vc_ref, do_ref, gq_ref, gk_ref, p_ref, ps_ref,
             dq_ref, dkv_ref, dbias_ref, dsink_ref, dgq_ref, dgk_ref):
        n = pl.program_id(0)
        sel = head_selectors()

        @pl.when(n == 0)
        def _():
            dbias_ref[...] = jnp.zeros_like(dbias_ref)
            dsink_ref[...] = jnp.zeros_like(dsink_ref)
            dgq_ref[...] = jnp.zeros_like(dgq_ref)
            dgk_ref[...] = jnp.zeros_like(dgk_ref)

        dgq_sum = jnp.zeros((1, HEAD_DIM), F32)
        dgk_sum = jnp.zeros((1, HEAD_DIM), F32)
        dk_rows, dv_rows = [], []
        for h in range(N_KV_HEADS):
            qf = take_heads(q_ref[:, pl.ds(h * KV_W, KV_W)], sel)
            rq, qn = rms(qf, gq_ref[...])
            kf = jnp.concatenate([_dot(kp_ref[...], sel[h]), _dot(kc_ref[...], sel[h])], axis=0)
            rk, kn = rms(kf, gk_ref[...])
            pn_bf16, psink = p_ref[h], ps_ref[h]
            pn = pn_bf16.astype(F32)
            do = take_heads(do_ref[:, pl.ds(h * KV_W, KV_W)], sel).astype(BF16)
            v = jnp.concatenate([_dot(vp_ref[...], sel[h]), _dot(vc_ref[...], sel[h])], axis=0).astype(BF16)
            dv_win = _dot(do, pn_bf16, trans_a=True).T
            dp = _dot(do, v, trans_b=True)
            delta = jnp.sum(pn * dp, axis=-1, keepdims=True)
            ds = pn * (dp - delta)
            dsc = (ds * scale).astype(BF16)
            dqn = _dot(dsc, kn.astype(BF16))
            dkn = _dot(qn.astype(BF16), dsc, trans_a=True).T
            dq, dgq = rms_bwd(dqn, qf, rq, gq_ref[...])
            dk_win, dgk = rms_bwd(dkn, kf, rk, gk_ref[...])
            dq_ref[:, pl.ds(h * KV_W, KV_W)] = put_heads(dq, sel).astype(BF16)
            dk_rows += [dk_win[:BLOCK], dk_win[BLOCK:]]
            dv_rows += [dv_win[:BLOCK], dv_win[BLOCK:]]
            dbias_ref[h] += ds
            dsink_ref[h] += jnp.sum((-psink * delta).reshape(GROUP, BLOCK, 1), axis=1)
            dgq_sum = dgq_sum + dgq
            dgk_sum = dgk_sum + dgk
        for part in range(2):
            dkv_ref[part, :, pl.ds(0, KV_W)] = put_heads(jnp.concatenate(dk_rows[part::2], axis=0), sel).astype(BF16)
            dkv_ref[part, :, pl.ds(KV_W, KV_W)] = put_heads(jnp.concatenate(dv_rows[part::2], axis=0), sel).astype(BF16)
        dgq_ref[...] += dgq_sum
        dgk_ref[...] += dgk_sum

    row = pl.BlockSpec((BLOCK, Q_W), lambda n: (n, 0))
    return pl.pallas_call(
        body, name="attn_bwd", grid=(nb,),
        in_specs=qkv + [row, gain, gain,
                        pl.BlockSpec((None, N_KV_HEADS, QROWS, 2 * BLOCK), lambda n: (n, 0, 0, 0)),
                        pl.BlockSpec((None, N_KV_HEADS, QROWS, 1), lambda n: (n, 0, 0, 0))],
        out_specs=[row, pl.BlockSpec((None, 2, BLOCK, 2 * KV_W), lambda n: (n, 0, 0, 0)), bspec,
                   pl.BlockSpec((N_KV_HEADS, GROUP, 1), lambda n: (0, 0, 0)), gain, gain],
        out_shape=[jax.ShapeDtypeStruct((t, Q_W), BF16),
                   jax.ShapeDtypeStruct((nb, 2, BLOCK, 2 * KV_W), BF16),
                   jax.ShapeDtypeStruct((N_KV_HEADS, QROWS, 2 * BLOCK), F32),
                   jax.ShapeDtypeStruct((N_KV_HEADS, GROUP, 1), F32),
                   jax.ShapeDtypeStruct((1, HEAD_DIM), F32),
                   jax.ShapeDtypeStruct((1, HEAD_DIM), F32)],
        compiler_params=_params(("arbitrary",)),
    )(p, p, p, p, p, do, gq, gk, probs, sink_probs)


def _kv_window_sum(parts):
    nb = parts.shape[0]

    def body(cur_ref, nxt_ref, o_ref):
        nxt = jnp.where(pl.program_id(0) < nb - 1, nxt_ref[...].astype(F32), 0.0)
        o_ref[...] = (cur_ref[...].astype(F32) + nxt).astype(BF16)

    blk = (None, None, BLOCK, 2 * KV_W)
    return pl.pallas_call(
        body, name="kv_window_sum", grid=(nb,),
        in_specs=[pl.BlockSpec(blk, lambda n: (n, 1, 0, 0)),
                  pl.BlockSpec(blk, lambda n: (jnp.minimum(n + 1, nb - 1), 0, 0, 0))],
        out_specs=pl.BlockSpec((BLOCK, 2 * KV_W), lambda n: (n, 0)),
        out_shape=jax.ShapeDtypeStruct((nb * BLOCK, 2 * KV_W), BF16),
        compiler_params=_params(("parallel",)),
    )(parts, parts)


GATE_TILE = 512


def _merge_fwd(z3, o, p, w_proj, w_o):
    t, d = z3.shape
    tm = min(ROW_TILE, t)
    tn = GATE_TILE

    def body(z_ref, o_ref, gc_ref, ga_ref, wp_ref, wo_ref, m_ref, a_ref, b_ref):
        a = _dot(z_ref[...], wp_ref[...])
        b = _dot(o_ref[...], wo_ref[...])
        m_ref[...] = (_sigmoid(gc_ref[...].astype(F32)) * a + _sigmoid(ga_ref[...].astype(F32)) * b).astype(BF16)
        a_ref[...] = a.astype(BF16)
        b_ref[...] = b.astype(BF16)

    row = pl.BlockSpec((tm, d), lambda i, j: (i, 0))
    wspec = pl.BlockSpec((d, tn), lambda i, j: (0, j))
    ospec = pl.BlockSpec((tm, tn), lambda i, j: (i, j))
    return pl.pallas_call(
        body, name="merge_fwd", grid=(t // tm, d // tn),
        in_specs=[row, row,
                  pl.BlockSpec((tm, tn), lambda i, j: (i, COL_GC // tn + j)),
                  pl.BlockSpec((tm, tn), lambda i, j: (i, COL_GA // tn + j)), wspec, wspec],
        out_specs=[ospec, ospec, ospec],
        out_shape=[jax.ShapeDtypeStruct((t, d), BF16)] * 3,
        compiler_params=_params(("parallel", "parallel")),
    )(z3, o, p, p, w_proj, w_o)


def _merge_bwd(dres, w_out, a, b, p, tokens=()):
    t, d = dres.shape
    tm = min(ROW_TILE, t)
    tn = GATE_TILE

    def epilogue(acc, ex, outs, ids):
        a_ref, b_ref, gc_ref, ga_ref = ex[:4]
        sc = _sigmoid(gc_ref[...].astype(F32))
        sa = _sigmoid(ga_ref[...].astype(F32))
        outs[0][...] = (acc * sc).astype(BF16)
        outs[1][...] = (acc * sa).astype(BF16)
        outs[2][0] = (acc * a_ref[...].astype(F32) * sc * (1.0 - sc)).astype(BF16)
        outs[2][1] = (acc * b_ref[...].astype(F32) * sa * (1.0 - sa)).astype(BF16)

    ospec = pl.BlockSpec((tm, tn), lambda i, j, kk: (i, j))
    return _mm("merge_bwd", (t // tm, d // tn, 1),
               dres, pl.BlockSpec((tm, d), lambda i, j, kk: (i, 0)),
               w_out, pl.BlockSpec((tn, d), lambda i, j, kk: (j, 0)), (tm, tn),
               trans_b=True, a_pre=_to_bf16,
               extras=(a, b, p, p),
               extra_specs=(ospec, ospec,
                            pl.BlockSpec((tm, tn), lambda i, j, kk: (i, COL_GC // tn + j)),
                            pl.BlockSpec((tm, tn), lambda i, j, kk: (i, COL_GA // tn + j))), tokens=tokens,
               out_shape=(jax.ShapeDtypeStruct((t, d), BF16), jax.ShapeDtypeStruct((t, d), BF16),
                          jax.ShapeDtypeStruct((2, t, d), BF16)),
               out_specs=(ospec, ospec, pl.BlockSpec((2, tm, tn), lambda i, j, kk: (0, i, j))),
               epilogue=epilogue)


def _store_epilogue(acc, ex, outs, ids):
    outs[0][...] = acc


def _store_bf16_epilogue(acc, ex, outs, ids):
    outs[0][...] = acc.astype(BF16)


def _mm_nt(name, a, w, out_dtype=BF16):
    t, n = a.shape
    k = w.shape[0]
    tm = min(ROW_TILE, t)
    return _mm(name, (t // tm, 1, 1), a, pl.BlockSpec((tm, n), lambda i, j, kk: (i, 0)),
               w, pl.BlockSpec((k, n), lambda i, j, kk: (0, 0)), (tm, k), trans_b=True,
               out_shape=(jax.ShapeDtypeStruct((t, k), out_dtype),),
               out_specs=(pl.BlockSpec((tm, k), lambda i, j, kk: (i, 0)),),
               epilogue=_store_bf16_epilogue if out_dtype == BF16 else _store_epilogue)[0]


def _mm_tn(name, a, b, b_pre=None, tokens=()):
    t, m = a.shape
    n = b.shape[1]
    tk = min(TOKEN_TILE, t)
    return _mm(name, (1, 1, t // tk), a, pl.BlockSpec((tk, m), lambda i, j, kk: (kk, 0)),
               b, pl.BlockSpec((tk, n), lambda i, j, kk: (kk, 0)), (m, n), trans_a=True, b_pre=b_pre, tokens=tokens,
               out_shape=(jax.ShapeDtypeStruct((m, n), BF16),),
               out_specs=(pl.BlockSpec((m, n), lambda i, j, kk: (0, 0)),), epilogue=_store_bf16_epilogue)[0]


def _local_step(x, target, small, comm):
    t = x.shape[0]
    w = dict(small)

    n1 = _rmsnorm_fwd("ffn1_norm", x, w["ffn1_norm"])
    w.update(comm.weights("A", n1))
    (x1, hm), ffn1_saved = _ffn_fwd("ffn1", x, n1, w["ffn1_w_in"], w["ffn1_w_out"], comm.tokens,
                                    next_gain=w["mix_norm"])
    w.update(comm.weights("B", x1))
    tm = min(ROW_TILE, t)
    p = _mm("mix_in", (N_CHIPS, t // tm, 1),
            hm, pl.BlockSpec((tm, D_MODEL), lambda j, i, kk: (i, 0)),
            w["w_in"], pl.BlockSpec((None, D_MODEL, SHARD_W), lambda j, i, kk: (j, 0, 0)), (tm, SHARD_W),
            tokens=comm.tokens,
            out_shape=(jax.ShapeDtypeStruct((t, IN_W), BF16),),
            out_specs=(pl.BlockSpec((tm, SHARD_W), lambda j, i, kk: (i, j)),),
            epilogue=_store_bf16_epilogue)[0]

    z3, z1 = _conv_fwd(p, w["conv_dw_kernel"], w["conv_dw_bias"], w["conv_ln_g"], w["conv_ln_b"])

    onehot = _bucket_onehot()
    bias = _bias_table(w["rel_bias"].T, onehot).reshape(N_KV_HEADS, QROWS, 2 * BLOCK)
    sink_rows = jnp.repeat(w["attn_sinks"].reshape(N_KV_HEADS, GROUP), BLOCK, axis=1)[..., None]
    gq_wide = jnp.tile(w["q_norm"], (1, N_KV_HEADS))
    gk_wide = jnp.tile(w["k_norm"], (1, N_KV_HEADS))
    o, probs, sink_probs = _attn_fwd(p, gq_wide, gk_wide, sink_rows, bias)

    merged, a, b = _merge_fwd(z3, o, p, w["conv_w_proj"], w["attn_w_o"])
    x2, n2 = _mm_residual("mix_out", merged, w["w_out"], x1, 1.0, next_gain=w["ffn2_norm"])
    w.update(comm.weights("C", n2))
    (dy, loss), ffn2_saved = _ffn_fwd("ffn2", x2, n2, w["ffn2_w_in"], w["ffn2_w_out"], loss_target=target)

    g, big = {}, {}
    dres2, big["ffn2_w_in"], big["ffn2_w_out"], g["ffn2_norm"] = _ffn_bwd(
        "ffn2b", dy, x2, w["ffn2_norm"], ffn2_saved, w["ffn2_w_in"], w["ffn2_w_out"])
    tokens = comm.reduce_start("R1", big, behind=True)

    da, db, dgates = _merge_bwd(dres2, w["w_out"], a, b, p, tokens)
    tokens = comm.exchange_finish("R1", da)
    big = {}
    big["w_out"] = _mm_tn("d_w_out", merged, dres2, b_pre=_to_bf16, tokens=tokens)
    big["conv_w_proj"] = _mm_tn("d_w_proj", z3, da)
    big["attn_w_o"] = _mm_tn("d_w_o", o, db)
    dz3 = _mm_nt("d_z3", da, w["conv_w_proj"])
    do = _mm_nt("d_o", db, w["attn_w_o"])

    dq, dkv_parts, dbias, dsink, g["q_norm"], g["k_norm"] = _attn_bwd(
        p, do, w["q_norm"], w["k_norm"], probs, sink_probs)
    dkv = _kv_window_sum(dkv_parts)
    g["rel_bias"] = _bias_table_bwd(dbias.reshape(N_Q_HEADS, BLOCK * 2 * BLOCK), onehot).T
    g["attn_sinks"] = dsink.reshape(N_Q_HEADS)

    dz1, big["conv_dw_kernel"], g["conv_dw_bias"], g["conv_ln_g"], g["conv_ln_b"] = _conv_bwd_ln(
        p, z1, dz3, w["conv_ln_g"], w["conv_ln_b"])
    dp = _conv_bwd_glu(p, dz1, w["conv_dw_kernel"], dq, dkv, dgates)
    tk = min(TOKEN_TILE, t)
    big["w_in"] = _mm("d_w_in", (1, N_CHIPS, t // tk),
                    hm, pl.BlockSpec((tk, D_MODEL), lambda i, j, kk: (kk, 0)),
                    dp, pl.BlockSpec((tk, SHARD_W), lambda i, j, kk: (kk, j)), (D_MODEL, SHARD_W),
                    trans_a=True,
                    out_shape=(jax.ShapeDtypeStruct((N_CHIPS, D_MODEL, SHARD_W), BF16),),
                    out_specs=(pl.BlockSpec((None, D_MODEL, SHARD_W), lambda i, j, kk: (j, 0, 0)),),
                    epilogue=_store_bf16_epilogue)[0]
    dres1, g["mix_norm"] = _mm("d_mix", (t // tm, 1, N_CHIPS),
                               dp, pl.BlockSpec((tm, SHARD_W), lambda i, j, kk: (i, kk)),
                               w["w_in"], pl.BlockSpec((None, D_MODEL, SHARD_W), lambda i, j, kk: (kk, 0, 0)),
                               (tm, D_MODEL), trans_b=True,
                               extras=(x1, w["mix_norm"], dres2),
                               extra_specs=(pl.BlockSpec((tm, D_MODEL), lambda i, j, kk: (i, 0)),
                                            pl.BlockSpec((1, D_MODEL), lambda i, j, kk: (0, 0)),
                                            pl.BlockSpec((tm, D_MODEL), lambda i, j, kk: (i, 0))),
                               out_shape=(jax.ShapeDtypeStruct((t, D_MODEL), F32), jax.ShapeDtypeStruct((1, D_MODEL), F32)),
                               out_specs=(pl.BlockSpec((tm, D_MODEL), lambda i, j, kk: (i, 0)),
                                          pl.BlockSpec((1, D_MODEL), lambda i, j, kk: (0, 0))),
                               epilogue=_rms_bwd_epilogue, sem=("arbitrary", "arbitrary", "arbitrary"))

    tokens = comm.reduce_finish("R1", dres1, behind=True) + comm.reduce_start("R2", big, behind=True)

    def ffn1_first(du):
        comm.join_finish("R1", du)
        return comm.exchange_finish("R2", du)

    def ffn1_grads(dw_in4, dw_out):
        late = comm.reduce_finish("R2", dw_in4, behind=True)
        return late + comm.reduce_start("R3", {"ffn1_w_in": dw_in4, "ffn1_w_out": dw_out})

    grad_x, _, _, g["ffn1_norm"] = _ffn_bwd(
        "ffn1b", dres1, x, w["ffn1_norm"], ffn1_saved, w["ffn1_w_in"], w["ffn1_w_out"], tokens, ffn1_first, ffn1_grads)
    comm.join_finish("R2", grad_x)
    comm.reduce_finish("R3", grad_x)
    return loss[0, 0], grad_x, g


def _mesh_place():
    x, y, c = lax.axis_index("x"), lax.axis_index("y"), lax.axis_index("c")
    chips = [(1 - x, y), (x, 1 - y), (1 - x, 1 - y)]
    return x, y, c, chips


def _any_specs(n):
    return [pl.BlockSpec(memory_space=pl.ANY)] * n


HBM_SPEC = pl.BlockSpec(memory_space=pltpu.HBM)
SEM_SPEC = pl.BlockSpec(memory_space=pltpu.SEMAPHORE)
EFFECT = pltpu.SideEffectType.DATAFLOW_SIDE_EFFECTING


def _in_hbm(a):
    return pltpu.with_memory_space_constraint(a, pltpu.HBM)


def _copy_start(name, srcs, lands, plan, after=()):
    ns, nb = len(srcs), len(lands)
    n = plan.copies_per_source * ns

    def body(*refs):
        s_refs, l_refs = refs[:ns], refs[ns:ns + nb]
        send_sems, recv_sems = refs[ns + nb + len(after)], refs[ns + nb + len(after) + 1]
        token = refs[-1]
        for k, (src, dst, to, _) in enumerate(plan(s_refs, l_refs)):
            pltpu.make_async_remote_copy(src_ref=src, dst_ref=dst, send_sem=send_sems.at[k], recv_sem=recv_sems.at[k],
                                         device_id=to, device_id_type=MESH).start()
        token[...] = jnp.zeros_like(token)

    bufs = list(srcs) + list(lands)
    outs = pl.pallas_call(
        body, name=name,
        out_shape=(pltpu.SemaphoreType.DMA((n,)), pltpu.SemaphoreType.DMA((n,)),
                   *[pltpu.HBM(a.shape, a.dtype) for a in bufs], jax.ShapeDtypeStruct((8, LANES), F32)),
        in_specs=[HBM_SPEC] * len(bufs) + [pl.BlockSpec(memory_space=pl.ANY)] * len(after),
        out_specs=(SEM_SPEC, SEM_SPEC, *[HBM_SPEC] * len(bufs), pl.BlockSpec(memory_space=pltpu.VMEM)),
        input_output_aliases={i: 2 + i for i in range(len(bufs))},
        compiler_params=pltpu.CompilerParams(has_side_effects=EFFECT),
    )(*[_in_hbm(a) for a in bufs], *after)
    return outs[0], outs[1], list(outs[2:2 + ns]), list(outs[2 + ns:2 + ns + nb]), outs[-1]


def _copy_wait(name, send_sems, recv_sems, srcs, lands, after, plan):
    ns, nb = len(srcs), len(lands)
    after = tuple(after) if isinstance(after, (tuple, list)) else (after,)

    def body(*refs):
        s_refs, l_refs = refs[:ns], refs[ns:ns + nb]
        send_sems, recv_sems = refs[ns + nb], refs[ns + nb + 1]
        for k, (src, _, to, mine) in enumerate(plan(s_refs, l_refs)):
            cp = pltpu.make_async_remote_copy(src_ref=src, dst_ref=mine, send_sem=send_sems.at[k], recv_sem=recv_sems.at[k],
                                              device_id=to, device_id_type=MESH)
            cp.wait_send()
            cp.wait_recv()

    bufs = list(srcs) + list(lands)
    outs = pl.pallas_call(
        body, name=name,
        out_shape=tuple(pltpu.HBM(a.shape, a.dtype) for a in bufs),
        in_specs=[HBM_SPEC] * len(bufs) + [SEM_SPEC, SEM_SPEC] + [pl.BlockSpec(memory_space=pl.ANY)] * len(after),
        out_specs=tuple([HBM_SPEC] * len(bufs)),
        input_output_aliases={i: i for i in range(len(bufs))},
        compiler_params=pltpu.CompilerParams(has_side_effects=EFFECT),
    )(*bufs, send_sems, recv_sems, *after)
    return list(outs[:ns]), list(outs[ns:])


def _gather_plan(s_refs, l_refs):
    x, y, c, chips = _mesh_place()
    jme = 2 * x + y
    return [(s.at[c], land.at[jme, c], (*chip, c), land.at[2 * chip[0] + chip[1], c])
            for s, land in zip(s_refs, l_refs) for chip in chips]


_gather_plan.copies_per_source = 3


def _gather_both_cores_plan(s_refs, l_refs):
    x, y, c, chips = _mesh_place()
    jme = 2 * x + y
    plan = []
    for s, land in zip(s_refs, l_refs):
        for chip in chips:
            for peer_core in (c, 1 - c):
                plan.append((s.at[c], land.at[jme, c], (*chip, peer_core), land.at[2 * chip[0] + chip[1], peer_core]))
        plan.append((s, land.at[jme], (x, y, 1 - c), land.at[jme]))
    return plan


_gather_both_cores_plan.copies_per_source = 7


def _scatter_plan(s_refs, l_refs):
    x, y, c, chips = _mesh_place()
    return [(s.at[2 * chip[0] + chip[1]], land.at[k], (*chip, c), land.at[k])
            for s, land in zip(s_refs, l_refs) for k, chip in enumerate(chips)]


_scatter_plan.copies_per_source = 3


def _exchange_plan(g_refs, l_refs):
    x, y, c, _ = _mesh_place()
    return [(g.at[:, 1 - c], land, (x, y, 1 - c), land) for g, land in zip(g_refs, l_refs)]


_exchange_plan.copies_per_source = 1


def _join_plan(h_refs, l_refs):
    x, y, c, _ = _mesh_place()
    return [(h.at[c], h.at[c], (x, y, 1 - c), h.at[1 - c]) for h in h_refs]


_join_plan.copies_per_source = 1


def _gather_forward(name, shards, landed):
    nw = len(shards)

    def body(*refs):
        s_refs, o_refs = refs[:nw], refs[2 * nw:3 * nw]
        send_sems, recv_sems = refs[3 * nw:]
        x, y, c, chips = _mesh_place()
        me, sib, jme = (x, y, c), (x, y, 1 - c), 2 * x + y
        sent = []
        for w in range(nw):
            parts = [(o_refs[w].at[2 * chip[0] + chip[1], c], o_refs[w].at[2 * chip[0] + chip[1], c]) for chip in chips]
            parts.append((s_refs[w], o_refs[w].at[jme]))
            for k, (src, dst) in enumerate(parts):
                cp = pltpu.make_async_remote_copy(src_ref=src, dst_ref=dst, send_sem=send_sems.at[4 * w + k],
                                                  recv_sem=recv_sems.at[4 * w + k], device_id=sib, device_id_type=MESH)
                cp.start()
                sent.append(cp)
        for w in range(nw):
            parts = [o_refs[w].at[2 * chip[0] + chip[1], 1 - c] for chip in chips] + [o_refs[w].at[jme]]
            for k, part in enumerate(parts):
                pltpu.make_async_remote_copy(src_ref=part, dst_ref=part, send_sem=send_sems.at[4 * w + k],
                                             recv_sem=recv_sems.at[4 * w + k], device_id=me, device_id_type=MESH).wait_recv()
        for cp in sent:
            cp.wait_send()

    return pl.pallas_call(
        body, name=name,
        in_specs=_any_specs(2 * nw), out_specs=_any_specs(nw),
        out_shape=[jax.ShapeDtypeStruct(a.shape, a.dtype) for a in landed],
        input_output_aliases={nw + i: i for i in range(nw)},
        scratch_shapes=[pltpu.SemaphoreType.DMA((4 * nw,)), pltpu.SemaphoreType.DMA((4 * nw,))],
    )(*shards, *landed)


def _exchange_halves(name, grads, after=()):
    nw = len(grads)

    def body(*refs):
        g_refs, o_refs = refs[:nw], refs[nw + len(after):2 * nw + len(after)]
        send_sems, recv_sems = refs[2 * nw + len(after):]
        x, y, c, _ = _mesh_place()
        copies = []
        for w in range(nw):
            cp = pltpu.make_async_remote_copy(src_ref=g_refs[w].at[:, 1 - c], dst_ref=o_refs[w], send_sem=send_sems.at[w],
                                              recv_sem=recv_sems.at[w], device_id=(x, y, 1 - c), device_id_type=MESH)
            cp.start()
            copies.append(cp)
        for cp in copies:
            cp.wait()

    return pl.pallas_call(
        body, name=name,
        in_specs=_any_specs(nw + len(after)), out_specs=_any_specs(nw),
        out_shape=[jax.ShapeDtypeStruct((N_CHIPS,) + g.shape[2:], g.dtype) for g in grads],
        scratch_shapes=[pltpu.SemaphoreType.DMA((nw,)), pltpu.SemaphoreType.DMA((nw,))],
    )(*grads, *after)


ELEMENTWISE_ROWS = 512


def _row_tile(r):
    for cand in range(min(r, ELEMENTWISE_ROWS) // 16 * 16, 0, -16):
        if r % cand == 0:
            return cand
    return r


def _add_own_half(c_idx, grad, got):
    _, _, r, cols = grad.shape
    tr = _row_tile(r)

    def body(c_ref, g_ref, o_ref, out_ref):
        out_ref[...] = (g_ref[...].astype(F32) + o_ref[...].astype(F32)).astype(BF16)

    return pl.pallas_call(
        body, name="add_own_half",
        grid_spec=pltpu.PrefetchScalarGridSpec(
            num_scalar_prefetch=1, grid=(N_CHIPS, r // tr),
            in_specs=[pl.BlockSpec((None, None, tr, cols), lambda j, i, c_ref: (j, c_ref[0], i, 0)),
                      pl.BlockSpec((None, tr, cols), lambda j, i, c_ref: (j, i, 0))],
            out_specs=pl.BlockSpec((None, tr, cols), lambda j, i, c_ref: (j, i, 0))),
        out_shape=jax.ShapeDtypeStruct((N_CHIPS, r, cols), BF16),
        compiler_params=_params(("parallel", "parallel")),
    )(c_idx, grad, got)


def _sum_pieces(place_idx, sums, landed):
    _, r, cols = sums.shape
    tr = _row_tile(r)

    def body(j_ref, own_ref, p_ref, o_ref):
        o_ref[...] = ((own_ref[...].astype(F32) + p_ref[0].astype(F32)) + p_ref[1].astype(F32)) + p_ref[2].astype(F32)

    return pl.pallas_call(
        body, name="sum_pieces",
        grid_spec=pltpu.PrefetchScalarGridSpec(
            num_scalar_prefetch=1, grid=(r // tr,),
            in_specs=[pl.BlockSpec((None, tr, cols), lambda i, j_ref: (j_ref[0], i, 0)),
                      pl.BlockSpec((N_CHIPS - 1, tr, cols), lambda i, j_ref: (0, i, 0))],
            out_specs=pl.BlockSpec((None, tr, cols), lambda i, j_ref: (j_ref[1], i, 0))),
        out_shape=jax.ShapeDtypeStruct((2, r, cols), F32),
        compiler_params=_params(("parallel",)),
    )(place_idx, sums, landed)


def _join_halves(name, halves):
    nw = len(halves)

    def body(*refs):
        o_refs = refs[nw:2 * nw]
        send_sems, recv_sems = refs[2 * nw:]
        x, y, c, _ = _mesh_place()
        copies = []
        for w in range(nw):
            cp = pltpu.make_async_remote_copy(src_ref=o_refs[w].at[c], dst_ref=o_refs[w].at[c], send_sem=send_sems.at[w],
                                              recv_sem=recv_sems.at[w], device_id=(x, y, 1 - c), device_id_type=MESH)
            cp.start()
            copies.append(cp)
        for w in range(nw):
            copies[w].wait_send()
            landed = o_refs[w].at[1 - c]
            pltpu.make_async_remote_copy(src_ref=landed, dst_ref=landed, send_sem=send_sems.at[w], recv_sem=recv_sems.at[w],
                                         device_id=(x, y, c), device_id_type=MESH).wait_recv()

    return pl.pallas_call(
        body, name=name,
        in_specs=_any_specs(nw), out_specs=_any_specs(nw),
        out_shape=[jax.ShapeDtypeStruct(h.shape, F32) for h in halves],
        input_output_aliases={i: i for i in range(nw)},
        scratch_shapes=[pltpu.SemaphoreType.DMA((nw,)), pltpu.SemaphoreType.DMA((nw,))],
    )(*halves)


SMALL_ROWS = 8


def _all_reduce_small(pack):
    rows, cols = pack.shape
    n_dev = 8

    def body(p_ref, o_ref, slots, send_sems, recv_sems):
        x, y, c, _ = _mesh_place()
        me = 4 * x + 2 * y + c
        slots[me] = p_ref[...]
        copies = []
        for k in range(1, n_dev):
            peer = (me + k) % n_dev
            cp = pltpu.make_async_remote_copy(src_ref=p_ref, dst_ref=slots.at[me], send_sem=send_sems.at[k],
                                              recv_sem=recv_sems.at[k],
                                              device_id=(peer // 4, (peer // 2) % 2, peer % 2), device_id_type=MESH)
            cp.start()
            copies.append(cp)
        for k in range(1, n_dev):
            src = (me + n_dev - k) % n_dev
            pltpu.make_async_remote_copy(src_ref=p_ref, dst_ref=slots.at[src], send_sem=send_sems.at[k],
                                         recv_sem=recv_sems.at[k], device_id=(x, y, c), device_id_type=MESH).wait_recv()
        for cp in copies:
            cp.wait_send()
        total = slots[0]
        for s in range(1, n_dev):
            total = total + slots[s]
        o_ref[...] = total

    return pl.pallas_call(
        body, name="all_reduce_small",
        in_specs=[pl.BlockSpec(memory_space=pltpu.VMEM)], out_specs=pl.BlockSpec(memory_space=pltpu.VMEM),
        out_shape=jax.ShapeDtypeStruct((rows, cols), F32),
        scratch_shapes=[pltpu.VMEM((n_dev, rows, cols), F32), pltpu.SemaphoreType.DMA((n_dev,)),
                        pltpu.SemaphoreType.DMA((n_dev,))],
    )(pack)


def _adamw(name, w, g, m, v):
    r, cols = w.shape
    tr = _row_tile(r)

    def body(w_ref, g_ref, m_ref, v_ref, d_ref, nm_ref, nv_ref):
        gv = g_ref[...]
        nm = ADAM_B1 * m_ref[...] + (1.0 - ADAM_B1) * gv
        nv = ADAM_B2 * v_ref[...] + (1.0 - ADAM_B2) * (gv * gv)
        m_hat = nm / (1.0 - ADAM_B1 ** ADAM_STEP)
        v_hat = nv / (1.0 - ADAM_B2 ** ADAM_STEP)
        d_ref[...] = -ADAM_LR * (m_hat / (jnp.sqrt(v_hat) + ADAM_EPS) + ADAM_WD * w_ref[...])
        nm_ref[...] = nm
        nv_ref[...] = nv

    spec = pl.BlockSpec((tr, cols), lambda i: (i, 0))
    return pl.pallas_call(
        body, name=name, grid=(r // tr,),
        in_specs=[spec] * 4, out_specs=[spec] * 3,
        out_shape=[jax.ShapeDtypeStruct((r, cols), F32)] * 3,
        compiler_params=_params(("parallel",)),
    )(w, g, m, v)


BIG = ["ffn1_w_in", "ffn1_w_out", "w_in", "conv_w_proj", "attn_w_o", "w_out", "ffn2_w_in", "ffn2_w_out", "conv_dw_kernel"]
COL_SHARDED = ("ffn1_w_in", "w_in", "ffn2_w_in")
SMALL = ["ffn1_norm", "mix_norm", "ffn2_norm", "conv_dw_bias", "conv_ln_g", "conv_ln_b", "q_norm", "k_norm", "attn_sinks", "rel_bias"]
WEIGHTS = ["ffn1_norm", "ffn1_w_in", "ffn1_w_out", "mix_norm", "w_in", "conv_dw_kernel", "conv_dw_bias", "conv_ln_g",
           "conv_ln_b", "conv_w_proj", "q_norm", "k_norm", "attn_sinks", "rel_bias", "attn_w_o", "w_out", "ffn2_norm",
           "ffn2_w_in", "ffn2_w_out"]
SMALL_PLACE = {"ffn1_norm": (0, 0, 1024), "mix_norm": (1, 0, 1024), "ffn2_norm": (2, 0, 1024), "conv_dw_bias": (3, 0, 1024),
               "conv_ln_g": (4, 0, 1024), "conv_ln_b": (5, 0, 1024), "q_norm": (6, 0, 64), "k_norm": (6, 128, 64),
               "attn_sinks": (6, 256, 16), "rel_bias": (7, 0, 512)}
LOSS_PLACE = (6, 384)


def _pack_small(vals, fill=0.0, loss=None):
    pack = jnp.full((SMALL_ROWS, D_MODEL), fill, F32)
    for name, (row, lane, n) in SMALL_PLACE.items():
        pack = pack.at[row, lane:lane + n].set(vals[name].reshape(n))
    if loss is not None:
        pack = pack.at[LOSS_PLACE[0], LOSS_PLACE[1]].set(loss)
    return pack


def _unpack_small(pack, shapes):
    return {name: pack[row, lane:lane + n].reshape(shapes[name]) for name, (row, lane, n) in SMALL_PLACE.items()}


def _shard_halves(name, a):
    if name == "conv_dw_kernel":
        a = jnp.pad(a, ((0, CONV_PAD - CONV_WIDTH), (0, 0)))
    r, cols = a.shape
    return a.reshape(2, r // 2, cols)


GATHER_GROUPS = {"A": ["ffn1_w_in", "ffn1_w_out"],
                 "B": ["w_in", "conv_dw_kernel", "conv_w_proj", "attn_w_o", "w_out"],
                 "C": ["ffn2_w_in", "ffn2_w_out"]}


class _MeshComm:
    def __init__(self, wts):
        self.c_idx = lax.axis_index("c").astype(jnp.int32).reshape(1)
        self.place_idx = jnp.stack([2 * lax.axis_index("x") + lax.axis_index("y"), lax.axis_index("c")]).astype(jnp.int32)
        self.gathers, self.exchanges, self.reductions, self.joins, self.reduced = {}, {}, {}, {}, {}
        self.tokens = ()
        self.shards = {n: _shard_halves(n, wts[n]) if n == "conv_dw_kernel" else _shard_halves(n, wts[n]).astype(BF16)
                       for n in BIG}
        self._gather_start("A", ())

    def _gather_start(self, group, after):
        shards = [self.shards[n] for n in GATHER_GROUPS[group]]
        lands = [lax.empty((N_CHIPS,) + s.shape, s.dtype) for s in shards]
        self.gathers[group] = _copy_start("gather_start_" + group, shards, lands, self._gather_plan(group), after=after)
        self.tokens = (self.gathers[group][-1],)

    @staticmethod
    def _gather_plan(group):
        return _gather_both_cores_plan if group == "C" else _gather_plan

    def weights(self, group, after):
        send_sems, recv_sems, shards, lands, token = self.gathers.pop(group)
        after = [token if after is None else after]
        if group == "A":
            after += [self.shards[n] for g in ("B", "C") for n in GATHER_GROUPS[g]]
        shards, lands = _copy_wait("gather_wait_" + group, send_sems, recv_sems, shards, lands, after,
                                   self._gather_plan(group))
        gathered = lands if group == "C" else _gather_forward("gather_forward_" + group, shards, lands)
        self.tokens = ()
        following = {"A": "B", "B": "C"}.get(group)
        if following:
            self._gather_start(following, (gathered[0],))
        out = {}
        for n, g4 in zip(GATHER_GROUPS[group], gathered):
            r, cols = g4.shape[2] * 2, g4.shape[3]
            if n in COL_SHARDED:
                out[n] = g4.reshape(N_CHIPS, r, cols)
            elif n == "conv_dw_kernel":
                out[n] = g4.reshape(N_CHIPS, r, cols).transpose(1, 0, 2).reshape(r, N_CHIPS * cols)
            else:
                out[n] = g4.reshape(N_CHIPS * r, cols)
        return out

    def reduce_start(self, group, grads, behind=False):
        names = list(grads)
        g4 = []
        for n in names:
            a = grads[n]
            if n == "conv_dw_kernel":
                a = a.reshape(CONV_PAD, N_CHIPS, -1).transpose(1, 0, 2)
            elif n not in COL_SHARDED:
                a = a.reshape(N_CHIPS, a.shape[0] // N_CHIPS, a.shape[1])
            g4.append(a.reshape(N_CHIPS, 2, a.shape[1] // 2, a.shape[2]))
        if behind:
            lands = [lax.empty((N_CHIPS,) + g.shape[2:], g.dtype) for g in g4]
            started = _copy_start("exchange_start_" + group, g4, lands, _exchange_plan)
            self.exchanges[group] = (names,) + started
            return (started[-1],)
        return self._scatter_start(group, names, g4, _exchange_halves("exchange_halves_" + group, g4))

    def exchange_finish(self, group, after):
        names, send_sems, recv_sems, g4, lands, _ = self.exchanges.pop(group)
        g4, got = _copy_wait("exchange_wait_" + group, send_sems, recv_sems, g4, lands, after, _exchange_plan)
        return self._scatter_start(group, names, g4, got)

    def _scatter_start(self, group, names, g4, got):
        sums = [_add_own_half(self.c_idx, a, b) for a, b in zip(g4, got)]
        lands = [lax.empty((N_CHIPS - 1,) + s.shape[1:], s.dtype) for s in sums]
        started = _copy_start("scatter_start_" + group, sums, lands, _scatter_plan)
        self.reductions[group] = (names,) + started
        return (started[-1],)

    def reduce_finish(self, group, after, behind=False):
        names, send_sems, recv_sems, sums, lands, _ = self.reductions.pop(group)
        sums, lands = _copy_wait("scatter_wait_" + group, send_sems, recv_sems, sums, lands, after, _scatter_plan)
        halves = [_sum_pieces(self.place_idx, s, p) for s, p in zip(sums, lands)]
        if behind:
            started = _copy_start("join_start_" + group, halves, [], _join_plan)
            self.joins[group] = (names,) + started
            return (started[-1],)
        self.reduced.update(zip(names, _join_halves("join_halves_" + group, halves)))
        return ()

    def join_finish(self, group, after):
        names, send_sems, recv_sems, halves, _, _ = self.joins.pop(group)
        self.reduced.update(zip(names, _copy_wait("join_wait_" + group, send_sems, recv_sems, halves, [], after, _join_plan)[0]))


def kernel(x, ffn1_norm, ffn1_w_in, ffn1_w_out, mix_norm, w_in, conv_dw_kernel, conv_dw_bias, conv_ln_g, conv_ln_b, conv_w_proj, q_norm, k_norm, attn_sinks, rel_bias, attn_w_o, w_out, ffn2_norm, ffn2_w_in, ffn2_w_out, loss_target, m_ffn1_norm, m_ffn1_w_in, m_ffn1_w_out, m_mix_norm, m_w_in, m_conv_dw_kernel, m_conv_dw_bias, m_conv_ln_g, m_conv_ln_b, m_conv_w_proj, m_q_norm, m_k_norm, m_attn_sinks, m_rel_bias, m_attn_w_o, m_w_out, m_ffn2_norm, m_ffn2_w_in, m_ffn2_w_out, v_ffn1_norm, v_ffn1_w_in, v_ffn1_w_out, v_mix_norm, v_w_in, v_conv_dw_kernel, v_conv_dw_bias, v_conv_ln_g, v_conv_ln_b, v_conv_w_proj, v_q_norm, v_k_norm, v_attn_sinks, v_rel_bias, v_attn_w_o, v_w_out, v_ffn2_norm, v_ffn2_w_in, v_ffn2_w_out):
    args = dict(locals())
    wts = {n: args[n] for n in WEIGHTS}
    mom = {n: args["m_" + n] for n in WEIGHTS}
    var = {n: args["v_" + n] for n in WEIGHTS}
    comm = _MeshComm(wts)
    small = {n: wts[n] if n in ("attn_sinks", "rel_bias") else wts[n].reshape(1, -1) for n in SMALL}
    loss_part, grad_x, g = _local_step(x[0], loss_target[0], small, comm)

    small_sum = _all_reduce_small(_pack_small(g, loss=loss_part))
    loss = small_sum[LOSS_PLACE[0], LOSS_PLACE[1]]
    small_shapes = {n: wts[n].shape for n in SMALL}
    g_small = _unpack_small(small_sum, small_shapes)

    grads, delta, new_m, new_v = {}, {}, {}, {}
    for n in BIG:
        j = comm.reduced[n]
        gs = j.reshape(j.shape[1] * 2, j.shape[2])
        pad = n == "conv_dw_kernel"
        ws, ms, vs = (_shard_halves(n, a).reshape(gs.shape) for a in (wts[n], mom[n], var[n]))
        d, nm, nv = _adamw("adamw_" + n, ws, gs, ms, vs)
        cut = (lambda a: a[:CONV_WIDTH]) if pad else (lambda a: a)
        grads[n], delta[n], new_m[n], new_v[n] = cut(gs), cut(d), cut(nm), cut(nv)
    d, nm, nv = _adamw("adamw_small", _pack_small(wts), small_sum, _pack_small(mom), _pack_small(var, fill=1.0))
    grads.update(g_small)
    delta.update(_unpack_small(d, small_shapes))
    new_m.update(_unpack_small(nm, small_shapes))
    new_v.update(_unpack_small(nv, small_shapes))

    return (loss, grad_x[None], *[grads[n] for n in WEIGHTS], *[delta[n] for n in WEIGHTS],
            *[new_m[n] for n in WEIGHTS], *[new_v[n] for n in WEIGHTS])
```

```python
import functools
import math

import jax
import jax.numpy as jnp
from jax import lax
from jax.experimental import pallas as pl
from jax.experimental.pallas import tpu as pltpu

F32 = jnp.float32
BF16 = jnp.bfloat16
MESH = pl.DeviceIdType.MESH

EPS = 1e-6
D_MODEL = 1024
D_FF = 2816
N_CHIPS = 4
SHARD_W = 2 * D_FF // N_CHIPS
HEAD_DIM = 64
N_Q_HEADS = 16
N_KV_HEADS = 4
GROUP = N_Q_HEADS // N_KV_HEADS
BLOCK = 128
QROWS = GROUP * BLOCK
N_BUCKETS = 32
MAX_DISTANCE = 128
CONV_WIDTH = 31
CONV_PAD = 32
NEG = float(jnp.finfo(jnp.float32).min)

ADAM_LR = 0.001
ADAM_B1 = 0.9
ADAM_B2 = 0.999
ADAM_EPS = 1e-08
ADAM_WD = 0.01
ADAM_STEP = 10

VMEM_LIMIT_BYTES = 56 * 1024 * 1024
ROW_TILE = 1024
TOKEN_TILE = 1024
CONV_TILE = 256
CONV_ROWS = 128
LANES = 128

COL_CONV_A, COL_CONV_G, COL_Q, COL_K, COL_V, COL_GC, COL_GA = 0, 1024, 2048, 3072, 3328, 3584, 4608
IN_W = 5632


def _params(sem, vmem=VMEM_LIMIT_BYTES):
    return pltpu.CompilerParams(dimension_semantics=sem, vmem_limit_bytes=vmem)


def _sigmoid(x):
    return 1.0 / (1.0 + jnp.exp(-x))


def _dot(a, b, trans_a=False, trans_b=False, precision=None):
    dn = (((0,) if trans_a else (1,), (1,) if trans_b else (0,)), ((), ()))
    return lax.dot_general(a, b, dn, preferred_element_type=F32, precision=precision)


def _mm(name, grid, a, a_spec, b, b_spec, acc_shape, *, trans_a=False, trans_b=False, a_pre=None, b_pre=None,
        extras=(), extra_specs=(), tokens=(), out_shape, out_specs, epilogue, chunked=False,
        sem=("parallel", "parallel", "arbitrary")):
    n_k = grid[2]
    assert not chunked or (n_k == 1 and b_pre is None)
    extras = tuple(extras) + tuple(tokens)
    extra_specs = tuple(extra_specs) + (pl.BlockSpec((8, LANES), lambda i, j, kk: (0, 0)),) * len(tokens)
    n_extra = len(extras)
    n_out = len(out_shape)

    def body(a_ref, b_ref, *rest):
        ex = rest[:n_extra]
        outs = rest[n_extra:n_extra + n_out]
        ids = (pl.program_id(0), pl.program_id(1), pl.program_id(2))
        av = a_ref[...]
        if a_pre is not None:
            av = a_pre(av)
        if chunked:
            for c0, cw in _col_chunks(acc_shape[1]):
                cols = pl.ds(c0, cw)
                epilogue(_dot(av, b_ref[cols, :] if trans_b else b_ref[:, cols], trans_a, trans_b), ex, outs, ids, cols)
            return
        bv = b_ref[...]
        if b_pre is not None:
            bv = b_pre(bv)
        if n_k == 1:
            epilogue(_dot(av, bv, trans_a, trans_b), ex, outs, ids)
        else:
            acc = rest[-1]

            @pl.when(ids[2] == 0)
            def _():
                acc[...] = jnp.zeros_like(acc)

            acc[...] += _dot(av, bv, trans_a, trans_b)

            @pl.when(ids[2] == n_k - 1)
            def _():
                epilogue(acc[...], ex, outs, ids)

    scratch = [] if n_k == 1 else [pltpu.VMEM(acc_shape, F32)]
    return pl.pallas_call(
        body, name=name, grid=grid,
        in_specs=[a_spec, b_spec, *extra_specs],
        out_specs=list(out_specs), out_shape=list(out_shape),
        scratch_shapes=scratch, compiler_params=_params(sem),
    )(a, b, *extras)


MXU_WIDTH = 256


def _col_chunks(n, width=2 * MXU_WIDTH):
    return [(c0, min(width, n - c0)) for c0 in range(0, n, width)]


def _half_bf16(v):
    return (0.5 * v).astype(BF16)


def _to_bf16(v):
    return v.astype(BF16)


def _rmsnorm_fwd(name, x, g, tokens=()):
    t, d = x.shape
    tm = min(ROW_TILE, t)

    def body(x_ref, g_ref, *rest):
        o_ref = rest[-1]
        xv = x_ref[...]
        r = lax.rsqrt(jnp.mean(xv * xv, axis=-1, keepdims=True) + EPS)
        o_ref[...] = (xv * r * g_ref[...]).astype(BF16)

    return pl.pallas_call(
        body, name=name, grid=(t // tm,),
        in_specs=[pl.BlockSpec((tm, d), lambda i: (i, 0)), pl.BlockSpec((1, d), lambda i: (0, 0))]
        + [pl.BlockSpec((8, LANES), lambda i: (0, 0))] * len(tokens),
        out_specs=pl.BlockSpec((tm, d), lambda i: (i, 0)),
        out_shape=jax.ShapeDtypeStruct((t, d), BF16),
        compiler_params=_params(("parallel",)),
    )(x, g, *tokens)


def _rms_bwd_epilogue(acc, ex, outs, ids):
    x_ref, g_ref, dres_ref = ex[:3]
    out_ref, dg_ref = outs
    xv = x_ref[...]
    r = lax.rsqrt(jnp.mean(xv * xv, axis=-1, keepdims=True) + EPS)
    w = acc * g_ref[...]
    dx = r * w - xv * (r * r * r) * jnp.mean(xv * w, axis=-1, keepdims=True)
    out_ref[...] = dres_ref[...] + dx
    part = jnp.sum(acc * (xv * r), axis=0, keepdims=True)

    @pl.when(ids[0] == 0)
    def _():
        dg_ref[...] = part

    @pl.when(ids[0] > 0)
    def _():
        dg_ref[...] += part


def _ffn_in(name, n, w_in4, tokens=()):
    t, d = n.shape
    tm = min(ROW_TILE, t)

    def body(n_ref, wa_ref, wb_ref, *rest):
        ab_ref, h_ref = rest[-2:]
        nv = n_ref[...]
        for c0, cw in _col_chunks(SHARD_W):
            cols = pl.ds(c0, cw)
            a = _dot(nv, wa_ref[:, cols])
            b = _dot(nv, wb_ref[:, cols])
            h_ref[:, cols] = (a * _sigmoid(a) * b).astype(BF16)
            ab_ref[0, :, cols] = a.astype(BF16)
            ab_ref[1, :, cols] = b.astype(BF16)

    return pl.pallas_call(
        body, name=name, grid=(2, t // tm),
        in_specs=[pl.BlockSpec((tm, d), lambda j, i: (i, 0)),
                  pl.BlockSpec((None, d, SHARD_W), lambda j, i: (j, 0, 0)),
                  pl.BlockSpec((None, d, SHARD_W), lambda j, i: (j + 2, 0, 0))]
        + [pl.BlockSpec((8, LANES), lambda j, i: (0, 0))] * len(tokens),
        out_specs=[pl.BlockSpec((2, tm, SHARD_W), lambda j, i: (0, i, j)),
                   pl.BlockSpec((tm, SHARD_W), lambda j, i: (i, j))],
        out_shape=[jax.ShapeDtypeStruct((2, t, D_FF), BF16), jax.ShapeDtypeStruct((t, D_FF), BF16)],
        compiler_params=_params(("parallel", "parallel")),
    )(n, w_in4, w_in4, *tokens)


def _mm_residual(name, a, w, res, scale, next_gain=None, loss_target=None):
    t, k = a.shape
    n = w.shape[1]
    tm = min(ROW_TILE, t)
    row = pl.BlockSpec((tm, n), lambda i, j, kk: (i, 0))
    extras, specs = [res], [row]
    shapes, out_specs = [jax.ShapeDtypeStruct((t, n), F32)], [row]
    if next_gain is not None:
        extras.append(next_gain)
        specs.append(pl.BlockSpec((1, n), lambda i, j, kk: (0, 0)))
        shapes.append(jax.ShapeDtypeStruct((t, n), BF16))
        out_specs.append(row)
    if loss_target is not None:
        extras.append(loss_target)
        specs.append(row)
        shapes.append(jax.ShapeDtypeStruct((8, LANES), F32))
        out_specs.append(pl.BlockSpec((8, LANES), lambda i, j, kk: (0, 0)))

    def epilogue(acc, ex, outs, ids):
        y = ex[0][...] + scale * acc
        if loss_target is None:
            outs[0][...] = y
        if next_gain is not None:
            r = lax.rsqrt(jnp.mean(y * y, axis=-1, keepdims=True) + EPS)
            outs[1][...] = (y * r * ex[1][...]).astype(BF16)
        if loss_target is not None:
            diff = y - ex[1][...]
            outs[0][...] = diff * (1.0 / n)
            part = jnp.full((8, LANES), 0.5 / n * jnp.sum(diff * diff), F32)

            @pl.when(ids[0] == 0)
            def _():
                outs[1][...] = part

            @pl.when(ids[0] > 0)
            def _():
                outs[1][...] += part

    sem = ("parallel" if loss_target is None else "arbitrary", "parallel", "arbitrary")
    out = _mm(name, (t // tm, 1, 1), a, pl.BlockSpec((tm, k), lambda i, j, kk: (i, 0)),
              w, pl.BlockSpec((k, n), lambda i, j, kk: (0, 0)), (tm, n),
              extras=extras, extra_specs=specs, out_shape=shapes, out_specs=out_specs, epilogue=epilogue, sem=sem)
    return out[0] if len(out) == 1 else tuple(out)


def _ffn_fwd(tag, x, n, w_in4, w_out, tokens=(), **tail):
    ab, h = _ffn_in(tag + "_in", n, w_in4, tokens)
    y = _mm_residual(tag + "_out", h, w_out, x, 0.5, **tail)
    return y, (n, ab, h)


def _ffn_bwd(tag, dres, x, g, saved, w_in4, w_out, tokens=(), on_first=None, on_weight_grads=None):
    n, ab, h = saved
    t, d = x.shape
    tm = min(ROW_TILE, t)
    tk = min(TOKEN_TILE, t)
    half_w = SHARD_W

    def dact_epilogue(acc, ex, outs, ids, cols):
        a = ex[0][0, :, cols].astype(F32)
        b = ex[0][1, :, cols].astype(F32)
        sig = _sigmoid(a)
        outs[0][0, :, cols] = (acc * b * (sig * (1.0 + a * (1.0 - sig)))).astype(BF16)
        outs[0][1, :, cols] = (acc * (a * sig)).astype(BF16)

    du = _mm(tag + "_dact", (2, t // tm, 1),
             dres, pl.BlockSpec((tm, d), lambda j, i, kk: (i, 0)),
             w_out, pl.BlockSpec((half_w, d), lambda j, i, kk: (j, 0)), (tm, half_w),
             trans_b=True, a_pre=_half_bf16,
             extras=(ab,), extra_specs=(pl.BlockSpec((2, tm, half_w), lambda j, i, kk: (0, i, j)),), tokens=tokens,
             out_shape=(jax.ShapeDtypeStruct((2, t, D_FF), BF16),),
             out_specs=(pl.BlockSpec((2, tm, half_w), lambda j, i, kk: (0, i, j)),),
             epilogue=dact_epilogue, chunked=True)[0]

    def store_epilogue(acc, ex, outs, ids):
        outs[0][...] = acc.astype(BF16)

    early = () if on_first is None else on_first(du)

    dw_out = _mm(tag + "_dwout", (2, 1, t // tk),
                 h, pl.BlockSpec((tk, half_w), lambda i, j, kk: (kk, i)),
                 dres, pl.BlockSpec((tk, d), lambda i, j, kk: (kk, 0)), (half_w, d),
                 trans_a=True, b_pre=_half_bf16, tokens=early,
                 out_shape=(jax.ShapeDtypeStruct((D_FF, d), BF16),),
                 out_specs=(pl.BlockSpec((half_w, d), lambda i, j, kk: (i, 0)),),
                 epilogue=store_epilogue)[0]

    dw_in4 = _mm(tag + "_dwin", (1, N_CHIPS, t // tk),
                 n, pl.BlockSpec((tk, d), lambda i, j, kk: (kk, 0)),
                 du, pl.BlockSpec((None, tk, SHARD_W), lambda i, j, kk: (j // 2, kk, j % 2)), (d, SHARD_W),
                 trans_a=True,
                 out_shape=(jax.ShapeDtypeStruct((N_CHIPS, d, SHARD_W), BF16),),
                 out_specs=(pl.BlockSpec((None, d, SHARD_W), lambda i, j, kk: (j, 0, 0)),),
                 epilogue=store_epilogue)[0]

    late = () if on_weight_grads is None else on_weight_grads(dw_in4, dw_out)

    dx, dg = _mm(tag + "_dn", (t // tm, 1, N_CHIPS),
                 du, pl.BlockSpec((None, tm, SHARD_W), lambda i, j, kk: (kk // 2, i, kk % 2)),
                 w_in4, pl.BlockSpec((None, d, SHARD_W), lambda i, j, kk: (kk, 0, 0)), (tm, d),
                 trans_b=True,
                 extras=(x, g, dres),
                 extra_specs=(pl.BlockSpec((tm, d), lambda i, j, kk: (i, 0)),
                              pl.BlockSpec((1, d), lambda i, j, kk: (0, 0)),
                              pl.BlockSpec((tm, d), lambda i, j, kk: (i, 0))), tokens=late,
                 out_shape=(jax.ShapeDtypeStruct((t, d), F32), jax.ShapeDtypeStruct((1, d), F32)),
                 out_specs=(pl.BlockSpec((tm, d), lambda i, j, kk: (i, 0)),
                            pl.BlockSpec((1, d), lambda i, j, kk: (0, 0))),
                 epilogue=_rms_bwd_epilogue, sem=("arbitrary", "arbitrary", "arbitrary"))
    return dx, dw_in4, dw_out, dg


def _conv_fill(zp_ref, a_ref, g_ref, ah_ref, gh_ref, i):
    zh = ah_ref[...].astype(F32) * _sigmoid(gh_ref[...].astype(F32))
    zp_ref[pl.ds(0, CONV_PAD), :] = jnp.where(i > 0, zh, 0.0)
    zp_ref[pl.ds(CONV_PAD, a_ref.shape[0]), :] = a_ref[...].astype(F32) * _sigmoid(g_ref[...].astype(F32))


def _shift_groups(shifts):
    groups = {}
    for j, s in shifts:
        groups.setdefault(s % 8, []).append((j, s // 8))
    return groups


def _windows(zp_ref, r0, lanes, groups):
    for q, taps in groups.items():
        deepest = max(p for _, p in taps)
        win = zp_ref[pl.ds(r0 + q, 8 * deepest + CONV_ROWS), lanes]
        for j, p in taps:
            yield j, win[8 * p:8 * p + CONV_ROWS]


def _conv_apply(zp_ref, out_ref, dw_ref, bias_ref, tm, ch, shifts):
    groups = _shift_groups(shifts)
    for cc in range(ch // LANES):
        lanes = pl.ds(cc * LANES, LANES)
        w = [dw_ref[pl.ds(j, 1), lanes] for j in range(CONV_WIDTH)]
        for r0 in range(0, tm, CONV_ROWS):
            if bias_ref is None:
                acc = jnp.zeros((CONV_ROWS, LANES), F32)
            else:
                acc = jnp.broadcast_to(bias_ref[:, lanes], (CONV_ROWS, LANES))
            for j, rows in _windows(zp_ref, r0, lanes, groups):
                acc = acc + w[j] * rows
            out_ref[pl.ds(r0, CONV_ROWS), lanes] = acc


FWD_SHIFTS = [(j, CONV_PAD - (CONV_WIDTH - 1) + j) for j in range(CONV_WIDTH)]
BWD_SHIFTS = [(j, CONV_WIDTH - 1 - j) for j in range(CONV_WIDTH)]


def _conv_taps(zp_ref, z1_ref, dw_ref, bias_ref, tm, ch):
    _conv_apply(zp_ref, z1_ref, dw_ref, bias_ref, tm, ch, FWD_SHIFTS)


def _conv_specs(tm, ch):
    per = tm // CONV_PAD
    cb = COL_CONV_G // ch
    return [pl.BlockSpec((tm, ch), lambda i: (i, 0)),
            pl.BlockSpec((tm, ch), lambda i: (i, cb)),
            pl.BlockSpec((CONV_PAD, ch), lambda i: (jnp.maximum(i * per - 1, 0), 0)),
            pl.BlockSpec((CONV_PAD, ch), lambda i: (jnp.maximum(i * per - 1, 0), cb))]


def _conv_fwd(p, dw, bias, ln_g, ln_b):
    t = p.shape[0]
    ch = D_MODEL
    tm = min(CONV_TILE, t)

    def body(a_ref, g_ref, ah_ref, gh_ref, dw_ref, bias_ref, lg_ref, lb_ref, o_ref, z1_ref, zp_ref):
        i = pl.program_id(0)
        _conv_fill(zp_ref, a_ref, g_ref, ah_ref, gh_ref, i)
        _conv_taps(zp_ref, z1_ref, dw_ref, bias_ref, tm, ch)
        z1 = z1_ref[...]
        mu = jnp.mean(z1, axis=-1, keepdims=True)
        zc = z1 - mu
        rs = lax.rsqrt(jnp.mean(zc * zc, axis=-1, keepdims=True) + EPS)
        z2 = zc * rs * lg_ref[...] + lb_ref[...]
        o_ref[...] = (z2 * _sigmoid(z2)).astype(BF16)

    vec = pl.BlockSpec((1, ch), lambda i: (0, 0))
    return pl.pallas_call(
        body, name="conv_fwd", grid=(t // tm,),
        in_specs=_conv_specs(tm, ch) + [pl.BlockSpec((CONV_PAD, ch), lambda i: (0, 0)), vec, vec, vec],
        out_specs=[pl.BlockSpec((tm, ch), lambda i: (i, 0)), pl.BlockSpec((tm, ch), lambda i: (i, 0))],
        out_shape=[jax.ShapeDtypeStruct((t, ch), BF16), jax.ShapeDtypeStruct((t, ch), F32)],
        scratch_shapes=[pltpu.VMEM((CONV_PAD + tm, ch), F32)],
        compiler_params=_params(("parallel",)),
    )(p, p, p, p, dw, bias, ln_g, ln_b)


def _conv_bwd_ln(p, z1_saved, dz3, ln_g, ln_b):
    t = p.shape[0]
    ch = D_MODEL
    tm = min(CONV_TILE, t)

    def body(a_ref, g_ref, ah_ref, gh_ref, z1_ref, dz3_ref, lg_ref, lb_ref,
             dz1_ref, ddw_ref, dbias_ref, dlg_ref, dlb_ref, zp_ref):
        i = pl.program_id(0)
        _conv_fill(zp_ref, a_ref, g_ref, ah_ref, gh_ref, i)
        z1 = z1_ref[...]
        mu = jnp.mean(z1, axis=-1, keepdims=True)
        zc = z1 - mu
        rs = lax.rsqrt(jnp.mean(zc * zc, axis=-1, keepdims=True) + EPS)
        xh = zc * rs
        z2 = xh * lg_ref[...] + lb_ref[...]
        sig = _sigmoid(z2)
        dz2 = dz3_ref[...].astype(F32) * (sig * (1.0 + z2 * (1.0 - sig)))
        dxh = dz2 * lg_ref[...]
        dz1 = rs * (dxh - jnp.mean(dxh, axis=-1, keepdims=True) - xh * jnp.mean(dxh * xh, axis=-1, keepdims=True))
        dz1_ref[...] = dz1

        @pl.when(i == 0)
        def _():
            ddw_ref[...] = jnp.zeros_like(ddw_ref)
            dbias_ref[...] = jnp.zeros_like(dbias_ref)
            dlg_ref[...] = jnp.zeros_like(dlg_ref)
            dlb_ref[...] = jnp.zeros_like(dlb_ref)

        dlg_ref[...] += jnp.sum(dz2 * xh, axis=0, keepdims=True)
        dlb_ref[...] += jnp.sum(dz2, axis=0, keepdims=True)
        dbias_ref[...] += jnp.sum(dz1, axis=0, keepdims=True)
        groups = _shift_groups(FWD_SHIFTS)
        for cc in range(ch // LANES):
            lanes = pl.ds(cc * LANES, LANES)
            accs = [jnp.zeros((8, LANES), F32) for _ in range(CONV_WIDTH)]
            for r0 in range(0, tm, CONV_ROWS):
                dzc = dz1_ref[pl.ds(r0, CONV_ROWS), lanes]
                for j, rows in _windows(zp_ref, r0, lanes, groups):
                    accs[j] = accs[j] + jnp.sum((dzc * rows).reshape(CONV_ROWS // 8, 8, LANES), axis=0)
            for j in range(CONV_WIDTH):
                ddw_ref[pl.ds(j, 1), lanes] += jnp.sum(accs[j], axis=0, keepdims=True)

    vec = pl.BlockSpec((1, ch), lambda i: (0, 0))
    return pl.pallas_call(
        body, name="conv_bwd_ln", grid=(t // tm,),
        in_specs=_conv_specs(tm, ch) + [pl.BlockSpec((tm, ch), lambda i: (i, 0)),
                                        pl.BlockSpec((tm, ch), lambda i: (i, 0)), vec, vec],
        out_specs=[pl.BlockSpec((tm, ch), lambda i: (i, 0)), pl.BlockSpec((CONV_PAD, ch), lambda i: (0, 0)), vec, vec, vec],
        out_shape=[jax.ShapeDtypeStruct((t, ch), F32), jax.ShapeDtypeStruct((CONV_PAD, ch), F32)]
        + [jax.ShapeDtypeStruct((1, ch), F32)] * 3,
        scratch_shapes=[pltpu.VMEM((CONV_PAD + tm, ch), F32)],
        compiler_params=_params(("arbitrary",)),
    )(p, p, p, p, z1_saved, dz3, ln_g, ln_b)


def _conv_bwd_glu(p, dz1, dw, dq, dkv, dgates):
    t = p.shape[0]
    ch = D_MODEL
    tm = min(CONV_TILE, t)
    per = tm // CONV_PAD
    n_halo = t // CONV_PAD
    cb = COL_CONV_G // ch

    def body(a_ref, g_ref, dz_ref, dzn_ref, dw_ref, dq_ref, dkv_ref, dgates_ref, o_ref, zp_ref, z0_ref):
        i = pl.program_id(0)
        o_ref[:, pl.ds(COL_Q, Q_W)] = dq_ref[...]
        o_ref[:, pl.ds(COL_K, 2 * KV_W)] = dkv_ref[...]
        o_ref[:, pl.ds(COL_GC, ch)] = dgates_ref[0]
        o_ref[:, pl.ds(COL_GA, ch)] = dgates_ref[1]
        zp_ref[pl.ds(0, tm), :] = dz_ref[...]
        zp_ref[pl.ds(tm, CONV_PAD), :] = jnp.where(i < t // tm - 1, dzn_ref[...], 0.0)
        _conv_apply(zp_ref, z0_ref, dw_ref, None, tm, ch, BWD_SHIFTS)
        dz0 = z0_ref[...]
        a = a_ref[...].astype(F32)
        sig = _sigmoid(g_ref[...].astype(F32))
        o_ref[:, pl.ds(0, ch)] = (dz0 * sig).astype(BF16)
        o_ref[:, pl.ds(ch, ch)] = (dz0 * a * sig * (1.0 - sig)).astype(BF16)

    return pl.pallas_call(
        body, name="conv_bwd_glu", grid=(t // tm,),
        in_specs=[pl.BlockSpec((tm, ch), lambda i: (i, 0)), pl.BlockSpec((tm, ch), lambda i: (i, cb)),
                  pl.BlockSpec((tm, ch), lambda i: (i, 0)),
                  pl.BlockSpec((CONV_PAD, ch), lambda i: (jnp.minimum((i + 1) * per, n_halo - 1), 0)),
                  pl.BlockSpec((CONV_PAD, ch), lambda i: (0, 0)),
                  pl.BlockSpec((tm, Q_W), lambda i: (i, 0)), pl.BlockSpec((tm, 2 * KV_W), lambda i: (i, 0)),
                  pl.BlockSpec((2, tm, ch), lambda i: (0, i, 0))],
        out_specs=pl.BlockSpec((tm, IN_W), lambda i: (i, 0)),
        out_shape=jax.ShapeDtypeStruct((t, IN_W), BF16),
        scratch_shapes=[pltpu.VMEM((tm + CONV_PAD, ch), F32), pltpu.VMEM((tm, ch), F32)],
        compiler_params=_params(("parallel",)),
    )(p, p, dz1, dz1, dw, dq, dkv, dgates)


def _bucket_onehot():
    qi = jnp.arange(BLOCK, dtype=jnp.int32)[:, None]
    kj = jnp.arange(2 * BLOCK, dtype=jnp.int32)[None, :]
    dist = jnp.maximum(qi + BLOCK - kj, 0)
    max_exact = N_BUCKETS // 2
    dflt = jnp.maximum(dist, 1).astype(F32)
    large = max_exact + (jnp.log(dflt / max_exact) / math.log(MAX_DISTANCE / max_exact)
                         * (N_BUCKETS - max_exact)).astype(jnp.int32)
    large = jnp.minimum(large, N_BUCKETS - 1)
    bucket = jnp.where(dist < max_exact, dist, large)
    onehot = bucket[None] == jnp.arange(N_BUCKETS, dtype=jnp.int32)[:, None, None]
    return onehot.astype(F32).reshape(N_BUCKETS, BLOCK * 2 * BLOCK)


def _bias_table(rel_bias_t, onehot):
    n = onehot.shape[1]
    tn = 4096

    def body(r_ref, oh_ref, o_ref):
        flat = pl.program_id(0) * tn + lax.broadcasted_iota(jnp.int32, (N_Q_HEADS, tn), 1)
        dist = (flat // (2 * BLOCK)) + BLOCK - (flat % (2 * BLOCK))
        bias = _dot(r_ref[...], oh_ref[...], precision=lax.Precision.HIGHEST)
        o_ref[...] = jnp.where((dist >= 0) & (dist < BLOCK), bias, NEG)

    return pl.pallas_call(
        body, name="bias_table", grid=(n // tn,),
        in_specs=[pl.BlockSpec((N_Q_HEADS, N_BUCKETS), lambda i: (0, 0)), pl.BlockSpec((N_BUCKETS, tn), lambda i: (0, i))],
        out_specs=pl.BlockSpec((N_Q_HEADS, tn), lambda i: (0, i)),
        out_shape=jax.ShapeDtypeStruct((N_Q_HEADS, n), F32),
        compiler_params=_params(("parallel",)),
    )(rel_bias_t, onehot)


def _bias_table_bwd(dbias, onehot):
    n = onehot.shape[1]
    tn = 4096

    def body(d_ref, oh_ref, o_ref):
        part = _dot(d_ref[...], oh_ref[...], trans_b=True, precision=lax.Precision.HIGHEST)
        i = pl.program_id(0)

        @pl.when(i == 0)
        def _():
            o_ref[...] = part

        @pl.when(i > 0)
        def _():
            o_ref[...] += part

    return pl.pallas_call(
        body, name="bias_table_bwd", grid=(n // tn,),
        in_specs=[pl.BlockSpec((N_Q_HEADS, tn), lambda i: (0, i)), pl.BlockSpec((N_BUCKETS, tn), lambda i: (0, i))],
        out_specs=pl.BlockSpec((N_Q_HEADS, N_BUCKETS), lambda i: (0, 0)),
        out_shape=jax.ShapeDtypeStruct((N_Q_HEADS, N_BUCKETS), F32),
        compiler_params=_params(("arbitrary",)),
    )(dbias, onehot)


def _lane_head(rows):
    return lax.broadcasted_iota(jnp.int32, (rows, KV_W), 1) // HEAD_DIM


def _group_rms(x, gain_wide):
    head = _lane_head(x.shape[0])
    sq = x * x
    r = jnp.zeros_like(x)
    for i in range(N_KV_HEADS):
        ms = jnp.sum(jnp.where(head == i, sq, 0.0), axis=-1, keepdims=True) * (1.0 / HEAD_DIM)
        r = jnp.where(head == i, lax.rsqrt(ms + EPS), r)
    return r, x * r * gain_wide


def _stack_heads(group):
    head = _lane_head(group.shape[0])
    return jnp.concatenate([jnp.where(head == i, group, jnp.zeros_like(group)) for i in range(N_KV_HEADS)], axis=0)


def _unstack_heads(stacked):
    head = _lane_head(BLOCK)
    out = jnp.where(head == 0, stacked[:BLOCK], 0.0)
    for i in range(1, N_KV_HEADS):
        out = out + jnp.where(head == i, stacked[i * BLOCK:(i + 1) * BLOCK], 0.0)
    return out


def _repeaters():
    row = lax.broadcasted_iota(jnp.int32, (KV_W, KV_W), 0)
    col = lax.broadcasted_iota(jnp.int32, (KV_W, KV_W), 1)
    return [(row == h * HEAD_DIM + col % HEAD_DIM).astype(BF16) for h in range(N_KV_HEADS)]


def _attn_probs(q_stack, k_rep, sink, bias, before_start):
    s = _dot(q_stack, k_rep, trans_b=True) * (1.0 / math.sqrt(HEAD_DIM)) + bias
    s = jnp.where(before_start, NEG, s)
    m = jnp.maximum(jnp.max(s, axis=-1, keepdims=True), sink)
    p = jnp.exp(s - m)
    es = jnp.exp(sink - m)
    inv = 1.0 / (jnp.sum(p, axis=-1, keepdims=True) + es)
    return p * inv, es * inv


def _before_start(n):
    col = lax.broadcasted_iota(jnp.int32, (QROWS, 2 * BLOCK), 1)
    return (col < BLOCK) & (n == 0)


STEP_BLOCKS = 4
KV_W = N_KV_HEADS * HEAD_DIM
Q_W = N_Q_HEADS * HEAD_DIM


def _attn_specs():
    qspec = pl.BlockSpec((BLOCK, Q_W), lambda n: (n, COL_Q // Q_W))
    kprev = pl.BlockSpec((BLOCK, KV_W), lambda n: (jnp.maximum(n - 1, 0), COL_K // KV_W))
    kcur = pl.BlockSpec((BLOCK, KV_W), lambda n: (n, COL_K // KV_W))
    vprev = pl.BlockSpec((BLOCK, KV_W), lambda n: (jnp.maximum(n - 1, 0), COL_V // KV_W))
    vcur = pl.BlockSpec((BLOCK, KV_W), lambda n: (n, COL_V // KV_W))
    gain = pl.BlockSpec((1, KV_W), lambda n: (0, 0))
    sink = pl.BlockSpec((N_KV_HEADS, QROWS, 1), lambda n: (0, 0, 0))
    bias = pl.BlockSpec((N_KV_HEADS, QROWS, 2 * BLOCK), lambda n: (0, 0, 0))
    return [qspec, kprev, kcur, vprev, vcur], gain, sink, bias


def _attn_fwd(p, gq, gk, sink_rows, bias):
    t = p.shape[0]
    nb = t // BLOCK
    per = STEP_BLOCKS if nb % STEP_BLOCKS == 0 else 1
    _, gain, sink, bspec = _attn_specs()

    def body(q_ref, kp_ref, kc_ref, vp_ref, vc_ref, gq_ref, gk_ref, sink_ref, bias_ref, o_ref, p_ref, ps_ref):
        first = pl.program_id(0) * per
        rep = _repeaters()
        kf = jnp.concatenate([kp_ref[...], kc_ref[...]], axis=0).astype(F32)
        kn = _group_rms(kf, gk_ref[...])[1].astype(BF16)
        v = jnp.concatenate([vp_ref[...], vc_ref[...]], axis=0)
        for h in range(N_KV_HEADS):
            k_rep = _dot(kn, rep[h]).astype(BF16)
            v_rep = _dot(v, rep[h]).astype(BF16)
            for sub in range(per):
                rows = pl.ds(sub * BLOCK, BLOCK)
                window = slice(sub * BLOCK, (sub + 2) * BLOCK)
                qn = _group_rms(q_ref[rows, pl.ds(h * KV_W, KV_W)].astype(F32), gq_ref[...])[1]
                pn, ps_ref[sub, h] = _attn_probs(_stack_heads(qn).astype(BF16), k_rep[window], sink_ref[h], bias_ref[h],
                                                 _before_start(first + sub))
                pn = pn.astype(BF16)
                p_ref[sub, h] = pn
                o_ref[rows, pl.ds(h * KV_W, KV_W)] = _unstack_heads(_dot(pn, v_rep[window])).astype(BF16)

    def kv_specs(col):
        return [pl.BlockSpec((BLOCK, KV_W), lambda n: (jnp.maximum(n * per - 1, 0), col // KV_W)),
                pl.BlockSpec((per * BLOCK, KV_W), lambda n: (n, col // KV_W))]

    return pl.pallas_call(
        body, name="attn_fwd", grid=(nb // per,),
        in_specs=[pl.BlockSpec((per * BLOCK, Q_W), lambda n: (n, COL_Q // Q_W))] + kv_specs(COL_K) + kv_specs(COL_V)
        + [gain, gain, sink, bspec],
        out_specs=[pl.BlockSpec((per * BLOCK, Q_W), lambda n: (n, 0)),
                   pl.BlockSpec((per, N_KV_HEADS, QROWS, 2 * BLOCK), lambda n: (n, 0, 0, 0)),
                   pl.BlockSpec((per, N_KV_HEADS, QROWS, 1), lambda n: (n, 0, 0, 0))],
        out_shape=[jax.ShapeDtypeStruct((t, Q_W), BF16),
                   jax.ShapeDtypeStruct((nb, N_KV_HEADS, QROWS, 2 * BLOCK), BF16),
                   jax.ShapeDtypeStruct((nb, N_KV_HEADS, QROWS, 1), F32)],
        compiler_params=_params(("parallel",)),
    )(p, p, p, p, p, gq, gk, sink_rows, bias)


def _attn_bwd(p, do, gq, gk, probs, sink_probs):
    t = p.shape[0]
    nb = t // BLOCK
    per = STEP_BLOCKS if nb % STEP_BLOCKS == 0 else 1
    bspec = _attn_specs()[3]
    gain = pl.BlockSpec((1, HEAD_DIM), lambda n: (0, 0))
    scale = 1.0 / math.sqrt(HEAD_DIM)

    def head_selectors():
        row = lax.broadcasted_iota(jnp.int32, (KV_W, HEAD_DIM), 0)
        col = lax.broadcasted_iota(jnp.int32, (KV_W, HEAD_DIM), 1)
        return [(row == col + i * HEAD_DIM).astype(BF16) for i in range(N_KV_HEADS)]

    def take_heads(group, sel):
        return jnp.concatenate([_dot(group, s) for s in sel], axis=0)

    def put_heads(x, sel):
        rows = x.shape[0] // len(sel)
        out = _dot(x[:rows].astype(BF16), sel[0], trans_b=True)
        for i in range(1, len(sel)):
            out = out + _dot(x[i * rows:(i + 1) * rows].astype(BF16), sel[i], trans_b=True)
        return out

    def rms(x, g):
        r = lax.rsqrt(jnp.mean(x * x, axis=-1, keepdims=True) + EPS)
        return r, x * r * g

    def rms_bwd(dn, xf, r, g):
        w = dn * g
        dx = r * w - xf * (r * r * r) * jnp.mean(xf * w, axis=-1, keepdims=True)
        return dx, jnp.sum(dn * (xf * r), axis=0, keepdims=True)

    def body(q_ref, kp_ref, kc_ref, vp_ref, vc_ref, do_ref, gq_ref, gk_ref, p_ref, ps_ref,
             dq_ref, dkv_ref, dbias_ref, dsink_ref, dgq_ref, dgk_ref):
        n = pl.program_id(0)
        sel = head_selectors()

        @pl.when(n == 0)
        def _():
            dbias_ref[...] = jnp.zeros_like(dbias_ref)
            dsink_ref[...] = jnp.zeros_like(dsink_ref)
            dgq_ref[...] = jnp.zeros_like(dgq_ref)
            dgk_ref[...] = jnp.zeros_like(dgk_ref)

        dgq_sum = jnp.zeros((1, HEAD_DIM), F32)
        dgk_sum = jnp.zeros((1, HEAD_DIM), F32)
        dk_rows, dv_rows = [[] for _ in range(per)], [[] for _ in range(per)]
        for h in range(N_KV_HEADS):
            kf_all = jnp.concatenate([_dot(kp_ref[...], sel[h]), _dot(kc_ref[...], sel[h])], axis=0)
            rk_all, kn_all = rms(kf_all, gk_ref[...])
            kn_all = kn_all.astype(BF16)
            v_all = jnp.concatenate([_dot(vp_ref[...], sel[h]), _dot(vc_ref[...], sel[h])], axis=0).astype(BF16)
            for sub in range(per):
                rows = pl.ds(sub * BLOCK, BLOCK)
                window = slice(sub * BLOCK, (sub + 2) * BLOCK)
                qf = take_heads(q_ref[rows, pl.ds(h * KV_W, KV_W)], sel)
                rq, qn = rms(qf, gq_ref[...])
                pn_bf16, psink = p_ref[sub, h], ps_ref[sub, h]
                pn = pn_bf16.astype(F32)
                do = take_heads(do_ref[rows, pl.ds(h * KV_W, KV_W)], sel).astype(BF16)
                dv_win = _dot(do, pn_bf16, trans_a=True).T
                dp = _dot(do, v_all[window], trans_b=True)
                delta = jnp.sum(pn * dp, axis=-1, keepdims=True)
                ds = pn * (dp - delta)
                dsc = (ds * scale).astype(BF16)
                dqn = _dot(dsc, kn_all[window])
                dkn = _dot(qn.astype(BF16), dsc, trans_a=True).T
                dq, dgq = rms_bwd(dqn, qf, rq, gq_ref[...])
                dk_win, dgk = rms_bwd(dkn, kf_all[window], rk_all[window], gk_ref[...])
                dq_ref[rows, pl.ds(h * KV_W, KV_W)] = put_heads(dq, sel).astype(BF16)
                dk_rows[sub] += [dk_win[:BLOCK], dk_win[BLOCK:]]
                dv_rows[sub] += [dv_win[:BLOCK], dv_win[BLOCK:]]
                dbias_ref[h] += ds
                dsink_ref[h] += jnp.sum((-psink * delta).reshape(GROUP, BLOCK, 1), axis=1)
                dgq_sum = dgq_sum + dgq
                dgk_sum = dgk_sum + dgk
        for sub in range(per):
            for part in range(2):
                dkv_ref[sub, part, :, pl.ds(0, KV_W)] = put_heads(
                    jnp.concatenate(dk_rows[sub][part::2], axis=0), sel).astype(BF16)
                dkv_ref[sub, part, :, pl.ds(KV_W, KV_W)] = put_heads(
                    jnp.concatenate(dv_rows[sub][part::2], axis=0), sel).astype(BF16)
        dgq_ref[...] += dgq_sum
        dgk_ref[...] += dgk_sum

    def kv_specs(col):
        return [pl.BlockSpec((BLOCK, KV_W), lambda n: (jnp.maximum(n * per - 1, 0), col // KV_W)),
                pl.BlockSpec((per * BLOCK, KV_W), lambda n: (n, col // KV_W))]

    row = pl.BlockSpec((per * BLOCK, Q_W), lambda n: (n, 0))
    return pl.pallas_call(
        body, name="attn_bwd", grid=(nb // per,),
        in_specs=[pl.BlockSpec((per * BLOCK, Q_W), lambda n: (n, COL_Q // Q_W))] + kv_specs(COL_K) + kv_specs(COL_V)
        + [row, gain, gain,
           pl.BlockSpec((per, N_KV_HEADS, QROWS, 2 * BLOCK), lambda n: (n, 0, 0, 0)),
           pl.BlockSpec((per, N_KV_HEADS, QROWS, 1), lambda n: (n, 0, 0, 0))],
        out_specs=[row, pl.BlockSpec((per, 2, BLOCK, 2 * KV_W), lambda n: (n, 0, 0, 0)), bspec,
                   pl.BlockSpec((N_KV_HEADS, GROUP, 1), lambda n: (0, 0, 0)), gain, gain],
        out_shape=[jax.ShapeDtypeStruct((t, Q_W), BF16),
                   jax.ShapeDtypeStruct((nb, 2, BLOCK, 2 * KV_W), BF16),
                   jax.ShapeDtypeStruct((N_KV_HEADS, QROWS, 2 * BLOCK), F32),
                   jax.ShapeDtypeStruct((N_KV_HEADS, GROUP, 1), F32),
                   jax.ShapeDtypeStruct((1, HEAD_DIM), F32),
                   jax.ShapeDtypeStruct((1, HEAD_DIM), F32)],
        compiler_params=_params(("arbitrary",)),
    )(p, p, p, p, p, do, gq, gk, probs, sink_probs)


def _kv_window_sum(parts):
    nb = parts.shape[0]

    def body(cur_ref, nxt_ref, o_ref):
        nxt = jnp.where(pl.program_id(0) < nb - 1, nxt_ref[...].astype(F32), 0.0)
        o_ref[...] = (cur_ref[...].astype(F32) + nxt).astype(BF16)

    blk = (None, None, BLOCK, 2 * KV_W)
    return pl.pallas_call(
        body, name="kv_window_sum", grid=(nb,),
        in_specs=[pl.BlockSpec(blk, lambda n: (n, 1, 0, 0)),
                  pl.BlockSpec(blk, lambda n: (jnp.minimum(n + 1, nb - 1), 0, 0, 0))],
        out_specs=pl.BlockSpec((BLOCK, 2 * KV_W), lambda n: (n, 0)),
        out_shape=jax.ShapeDtypeStruct((nb * BLOCK, 2 * KV_W), BF16),
        compiler_params=_params(("parallel",)),
    )(parts, parts)


GATE_TILE = 512


def _merge_fwd(z3, o, p, w_proj, w_o):
    t, d = z3.shape
    tm = min(ROW_TILE, t)
    tn = GATE_TILE

    def body(z_ref, o_ref, gc_ref, ga_ref, wp_ref, wo_ref, m_ref, a_ref, b_ref):
        a = _dot(z_ref[...], wp_ref[...])
        b = _dot(o_ref[...], wo_ref[...])
        m_ref[...] = (_sigmoid(gc_ref[...].astype(F32)) * a + _sigmoid(ga_ref[...].astype(F32)) * b).astype(BF16)
        a_ref[...] = a.astype(BF16)
        b_ref[...] = b.astype(BF16)

    row = pl.BlockSpec((tm, d), lambda i, j: (i, 0))
    wspec = pl.BlockSpec((d, tn), lambda i, j: (0, j))
    ospec = pl.BlockSpec((tm, tn), lambda i, j: (i, j))
    return pl.pallas_call(
        body, name="merge_fwd", grid=(t // tm, d // tn),
        in_specs=[row, row,
                  pl.BlockSpec((tm, tn), lambda i, j: (i, COL_GC // tn + j)),
                  pl.BlockSpec((tm, tn), lambda i, j: (i, COL_GA // tn + j)), wspec, wspec],
        out_specs=[ospec, ospec, ospec],
        out_shape=[jax.ShapeDtypeStruct((t, d), BF16)] * 3,
        compiler_params=_params(("parallel", "parallel")),
    )(z3, o, p, p, w_proj, w_o)


def _merge_bwd(dres, w_out, a, b, p, tokens=()):
    t, d = dres.shape
    tm = min(ROW_TILE, t)
    tn = GATE_TILE

    def epilogue(acc, ex, outs, ids):
        a_ref, b_ref, gc_ref, ga_ref = ex[:4]
        sc = _sigmoid(gc_ref[...].astype(F32))
        sa = _sigmoid(ga_ref[...].astype(F32))
        outs[0][...] = (acc * sc).astype(BF16)
        outs[1][...] = (acc * sa).astype(BF16)
        outs[2][0] = (acc * a_ref[...].astype(F32) * sc * (1.0 - sc)).astype(BF16)
        outs[2][1] = (acc * b_ref[...].astype(F32) * sa * (1.0 - sa)).astype(BF16)

    ospec = pl.BlockSpec((tm, tn), lambda i, j, kk: (i, j))
    return _mm("merge_bwd", (t // tm, d // tn, 1),
               dres, pl.BlockSpec((tm, d), lambda i, j, kk: (i, 0)),
               w_out, pl.BlockSpec((tn, d), lambda i, j, kk: (j, 0)), (tm, tn),
               trans_b=True, a_pre=_to_bf16,
               extras=(a, b, p, p),
               extra_specs=(ospec, ospec,
                            pl.BlockSpec((tm, tn), lambda i, j, kk: (i, COL_GC // tn + j)),
                            pl.BlockSpec((tm, tn), lambda i, j, kk: (i, COL_GA // tn + j))), tokens=tokens,
               out_shape=(jax.ShapeDtypeStruct((t, d), BF16), jax.ShapeDtypeStruct((t, d), BF16),
                          jax.ShapeDtypeStruct((2, t, d), BF16)),
               out_specs=(ospec, ospec, pl.BlockSpec((2, tm, tn), lambda i, j, kk: (0, i, j))),
               epilogue=epilogue)


def _store_epilogue(acc, ex, outs, ids):
    outs[0][...] = acc


def _store_bf16_epilogue(acc, ex, outs, ids):
    outs[0][...] = acc.astype(BF16)


def _mm_nt(name, a, w, out_dtype=BF16):
    t, n = a.shape
    k = w.shape[0]
    tm = min(ROW_TILE, t)
    return _mm(name, (t // tm, 1, 1), a, pl.BlockSpec((tm, n), lambda i, j, kk: (i, 0)),
               w, pl.BlockSpec((k, n), lambda i, j, kk: (0, 0)), (tm, k), trans_b=True,
               out_shape=(jax.ShapeDtypeStruct((t, k), out_dtype),),
               out_specs=(pl.BlockSpec((tm, k), lambda i, j, kk: (i, 0)),),
               epilogue=_store_bf16_epilogue if out_dtype == BF16 else _store_epilogue)[0]


def _mm_tn(name, a, b, b_pre=None, tokens=()):
    t, m = a.shape
    n = b.shape[1]
    tk = min(TOKEN_TILE, t)
    return _mm(name, (1, 1, t // tk), a, pl.BlockSpec((tk, m), lambda i, j, kk: (kk, 0)),
               b, pl.BlockSpec((tk, n), lambda i, j, kk: (kk, 0)), (m, n), trans_a=True, b_pre=b_pre, tokens=tokens,
               out_shape=(jax.ShapeDtypeStruct((m, n), BF16),),
               out_specs=(pl.BlockSpec((m, n), lambda i, j, kk: (0, 0)),), epilogue=_store_bf16_epilogue)[0]


def _local_step(x, target, small, comm):
    t = x.shape[0]
    w = dict(small)

    n1 = _rmsnorm_fwd("ffn1_norm", x, w["ffn1_norm"])
    w.update(comm.weights("A", n1))
    (x1, hm), ffn1_saved = _ffn_fwd("ffn1", x, n1, w["ffn1_w_in"], w["ffn1_w_out"], comm.tokens,
                                    next_gain=w["mix_norm"])
    w.update(comm.weights("B", x1))
    tm = min(ROW_TILE, t)
    p = _mm("mix_in", (N_CHIPS, t // tm, 1),
            hm, pl.BlockSpec((tm, D_MODEL), lambda j, i, kk: (i, 0)),
            w["w_in"], pl.BlockSpec((None, D_MODEL, SHARD_W), lambda j, i, kk: (j, 0, 0)), (tm, SHARD_W),
            tokens=comm.tokens,
            out_shape=(jax.ShapeDtypeStruct((t, IN_W), BF16),),
            out_specs=(pl.BlockSpec((tm, SHARD_W), lambda j, i, kk: (i, j)),),
            epilogue=_store_bf16_epilogue)[0]

    z3, z1 = _conv_fwd(p, w["conv_dw_kernel"], w["conv_dw_bias"], w["conv_ln_g"], w["conv_ln_b"])

    onehot = _bucket_onehot()
    bias = _bias_table(w["rel_bias"].T, onehot).reshape(N_KV_HEADS, QROWS, 2 * BLOCK)
    sink_rows = jnp.repeat(w["attn_sinks"].reshape(N_KV_HEADS, GROUP), BLOCK, axis=1)[..., None]
    gq_wide = jnp.tile(w["q_norm"], (1, N_KV_HEADS))
    gk_wide = jnp.tile(w["k_norm"], (1, N_KV_HEADS))
    o, probs, sink_probs = _attn_fwd(p, gq_wide, gk_wide, sink_rows, bias)

    merged, a, b = _merge_fwd(z3, o, p, w["conv_w_proj"], w["attn_w_o"])
    x2, n2 = _mm_residual("mix_out", merged, w["w_out"], x1, 1.0, next_gain=w["ffn2_norm"])
    w.update(comm.weights("C", n2))
    (dy, loss), ffn2_saved = _ffn_fwd("ffn2", x2, n2, w["ffn2_w_in"], w["ffn2_w_out"], loss_target=target)

    g, big = {}, {}
    dres2, big["ffn2_w_in"], big["ffn2_w_out"], g["ffn2_norm"] = _ffn_bwd(
        "ffn2b", dy, x2, w["ffn2_norm"], ffn2_saved, w["ffn2_w_in"], w["ffn2_w_out"])
    tokens = comm.reduce_start("R1", big, behind=True)

    da, db, dgates = _merge_bwd(dres2, w["w_out"], a, b, p, tokens)
    tokens = comm.exchange_finish("R1", da)
    big = {}
    big["w_out"] = _mm_tn("d_w_out", merged, dres2, b_pre=_to_bf16, tokens=tokens)
    big["conv_w_proj"] = _mm_tn("d_w_proj", z3, da)
    big["attn_w_o"] = _mm_tn("d_w_o", o, db)
    dz3 = _mm_nt("d_z3", da, w["conv_w_proj"])
    do = _mm_nt("d_o", db, w["attn_w_o"])

    dq, dkv_parts, dbias, dsink, g["q_norm"], g["k_norm"] = _attn_bwd(
        p, do, w["q_norm"], w["k_norm"], probs, sink_probs)
    dkv = _kv_window_sum(dkv_parts)
    g["rel_bias"] = _bias_table_bwd(dbias.reshape(N_Q_HEADS, BLOCK * 2 * BLOCK), onehot).T
    g["attn_sinks"] = dsink.reshape(N_Q_HEADS)

    dz1, big["conv_dw_kernel"], g["conv_dw_bias"], g["conv_ln_g"], g["conv_ln_b"] = _conv_bwd_ln(
        p, z1, dz3, w["conv_ln_g"], w["conv_ln_b"])
    dp = _conv_bwd_glu(p, dz1, w["conv_dw_kernel"], dq, dkv, dgates)
    tk = min(TOKEN_TILE, t)
    big["w_in"] = _mm("d_w_in", (1, N_CHIPS, t // tk),
                    hm, pl.BlockSpec((tk, D_MODEL), lambda i, j, kk: (kk, 0)),
                    dp, pl.BlockSpec((tk, SHARD_W), lambda i, j, kk: (kk, j)), (D_MODEL, SHARD_W),
                    trans_a=True,
                    out_shape=(jax.ShapeDtypeStruct((N_CHIPS, D_MODEL, SHARD_W), BF16),),
                    out_specs=(pl.BlockSpec((None, D_MODEL, SHARD_W), lambda i, j, kk: (j, 0, 0)),),
                    epilogue=_store_bf16_epilogue)[0]
    dres1, g["mix_norm"] = _mm("d_mix", (t // tm, 1, N_CHIPS),
                               dp, pl.BlockSpec((tm, SHARD_W), lambda i, j, kk: (i, kk)),
                               w["w_in"], pl.BlockSpec((None, D_MODEL, SHARD_W), lambda i, j, kk: (kk, 0, 0)),
                               (tm, D_MODEL), trans_b=True,
                               extras=(x1, w["mix_norm"], dres2),
                               extra_specs=(pl.BlockSpec((tm, D_MODEL), lambda i, j, kk: (i, 0)),
                                            pl.BlockSpec((1, D_MODEL), lambda i, j, kk: (0, 0)),
                                            pl.BlockSpec((tm, D_MODEL), lambda i, j, kk: (i, 0))),
                               out_shape=(jax.ShapeDtypeStruct((t, D_MODEL), F32), jax.ShapeDtypeStruct((1, D_MODEL), F32)),
                               out_specs=(pl.BlockSpec((tm, D_MODEL), lambda i, j, kk: (i, 0)),
                                          pl.BlockSpec((1, D_MODEL), lambda i, j, kk: (0, 0))),
                               epilogue=_rms_bwd_epilogue, sem=("arbitrary", "arbitrary", "arbitrary"))

    tokens = comm.reduce_finish("R1", dres1, behind=True) + comm.reduce_start("R2", big, behind=True)

    def ffn1_first(du):
        comm.join_finish("R1", du)
        return comm.exchange_finish("R2", du)

    def ffn1_grads(dw_in4, dw_out):
        late = comm.reduce_finish("R2", dw_in4, behind=True)
        return late + comm.reduce_start("R3", {"ffn1_w_in": dw_in4, "ffn1_w_out": dw_out})

    grad_x, _, _, g["ffn1_norm"] = _ffn_bwd(
        "ffn1b", dres1, x, w["ffn1_norm"], ffn1_saved, w["ffn1_w_in"], w["ffn1_w_out"], tokens, ffn1_first, ffn1_grads)
    comm.join_finish("R2", grad_x)
    comm.reduce_finish("R3", grad_x)
    return loss[0, 0], grad_x, g


def _mesh_place():
    x, y, c = lax.axis_index("x"), lax.axis_index("y"), lax.axis_index("c")
    chips = [(1 - x, y), (x, 1 - y), (1 - x, 1 - y)]
    return x, y, c, chips


def _any_specs(n):
    return [pl.BlockSpec(memory_space=pl.ANY)] * n


HBM_SPEC = pl.BlockSpec(memory_space=pltpu.HBM)
SEM_SPEC = pl.BlockSpec(memory_space=pltpu.SEMAPHORE)
EFFECT = pltpu.SideEffectType.DATAFLOW_SIDE_EFFECTING


def _in_hbm(a):
    return pltpu.with_memory_space_constraint(a, pltpu.HBM)


def _copy_start(name, srcs, lands, plan, after=()):
    ns, nb = len(srcs), len(lands)
    n = plan.copies_per_source * ns

    def body(*refs):
        s_refs, l_refs = refs[:ns], refs[ns:ns + nb]
        send_sems, recv_sems = refs[ns + nb + len(after)], refs[ns + nb + len(after) + 1]
        token = refs[-1]
        for k, (src, dst, to, _) in enumerate(plan(s_refs, l_refs)):
            pltpu.make_async_remote_copy(src_ref=src, dst_ref=dst, send_sem=send_sems.at[k], recv_sem=recv_sems.at[k],
                                         device_id=to, device_id_type=MESH).start()
        token[...] = jnp.zeros_like(token)

    bufs = list(srcs) + list(lands)
    outs = pl.pallas_call(
        body, name=name,
        out_shape=(pltpu.SemaphoreType.DMA((n,)), pltpu.SemaphoreType.DMA((n,)),
                   *[pltpu.HBM(a.shape, a.dtype) for a in bufs], jax.ShapeDtypeStruct((8, LANES), F32)),
        in_specs=[HBM_SPEC] * len(bufs) + [pl.BlockSpec(memory_space=pl.ANY)] * len(after),
        out_specs=(SEM_SPEC, SEM_SPEC, *[HBM_SPEC] * len(bufs), pl.BlockSpec(memory_space=pltpu.VMEM)),
        input_output_aliases={i: 2 + i for i in range(len(bufs))},
        compiler_params=pltpu.CompilerParams(has_side_effects=EFFECT),
    )(*[_in_hbm(a) for a in bufs], *after)
    return outs[0], outs[1], list(outs[2:2 + ns]), list(outs[2 + ns:2 + ns + nb]), outs[-1]


def _copy_wait(name, send_sems, recv_sems, srcs, lands, after, plan):
    ns, nb = len(srcs), len(lands)
    after = tuple(after) if isinstance(after, (tuple, list)) else (after,)

    def body(*refs):
        s_refs, l_refs = refs[:ns], refs[ns:ns + nb]
        send_sems, recv_sems = refs[ns + nb], refs[ns + nb + 1]
        for k, (src, _, to, mine) in enumerate(plan(s_refs, l_refs)):
            cp = pltpu.make_async_remote_copy(src_ref=src, dst_ref=mine, send_sem=send_sems.at[k], recv_sem=recv_sems.at[k],
                                              device_id=to, device_id_type=MESH)
            cp.wait_send()
            cp.wait_recv()

    bufs = list(srcs) + list(lands)
    outs = pl.pallas_call(
        body, name=name,
        out_shape=tuple(pltpu.HBM(a.shape, a.dtype) for a in bufs),
        in_specs=[HBM_SPEC] * len(bufs) + [SEM_SPEC, SEM_SPEC] + [pl.BlockSpec(memory_space=pl.ANY)] * len(after),
        out_specs=tuple([HBM_SPEC] * len(bufs)),
        input_output_aliases={i: i for i in range(len(bufs))},
        compiler_params=pltpu.CompilerParams(has_side_effects=EFFECT),
    )(*bufs, send_sems, recv_sems, *after)
    return list(outs[:ns]), list(outs[ns:])


def _gather_plan(s_refs, l_refs):
    x, y, c, chips = _mesh_place()
    jme = 2 * x + y
    return [(s.at[c], land.at[jme, c], (*chip, c), land.at[2 * chip[0] + chip[1], c])
            for s, land in zip(s_refs, l_refs) for chip in chips]


_gather_plan.copies_per_source = 3


def _gather_both_cores_plan(s_refs, l_refs):
    x, y, c, chips = _mesh_place()
    jme = 2 * x + y
    plan = []
    for s, land in zip(s_refs, l_refs):
        for chip in chips:
            for peer_core in (c, 1 - c):
                plan.append((s.at[c], land.at[jme, c], (*chip, peer_core), land.at[2 * chip[0] + chip[1], peer_core]))
        plan.append((s, land.at[jme], (x, y, 1 - c), land.at[jme]))
    return plan


_gather_both_cores_plan.copies_per_source = 7


def _scatter_plan(s_refs, l_refs):
    x, y, c, chips = _mesh_place()
    return [(s.at[2 * chip[0] + chip[1]], land.at[k], (*chip, c), land.at[k])
            for s, land in zip(s_refs, l_refs) for k, chip in enumerate(chips)]


_scatter_plan.copies_per_source = 3


def _exchange_plan(g_refs, l_refs):
    x, y, c, _ = _mesh_place()
    return [(g.at[:, 1 - c], land, (x, y, 1 - c), land) for g, land in zip(g_refs, l_refs)]


_exchange_plan.copies_per_source = 1


def _join_plan(h_refs, l_refs):
    x, y, c, _ = _mesh_place()
    return [(h.at[c], h.at[c], (x, y, 1 - c), h.at[1 - c]) for h in h_refs]


_join_plan.copies_per_source = 1


def _gather_forward(name, shards, landed):
    nw = len(shards)

    def body(*refs):
        s_refs, o_refs = refs[:nw], refs[2 * nw:3 * nw]
        send_sems, recv_sems = refs[3 * nw:]
        x, y, c, chips = _mesh_place()
        me, sib, jme = (x, y, c), (x, y, 1 - c), 2 * x + y
        sent = []
        for w in range(nw):
            parts = [(o_refs[w].at[2 * chip[0] + chip[1], c], o_refs[w].at[2 * chip[0] + chip[1], c]) for chip in chips]
            parts.append((s_refs[w], o_refs[w].at[jme]))
            for k, (src, dst) in enumerate(parts):
                cp = pltpu.make_async_remote_copy(src_ref=src, dst_ref=dst, send_sem=send_sems.at[4 * w + k],
                                                  recv_sem=recv_sems.at[4 * w + k], device_id=sib, device_id_type=MESH)
                cp.start()
                sent.append(cp)
        for w in range(nw):
            parts = [o_refs[w].at[2 * chip[0] + chip[1], 1 - c] for chip in chips] + [o_refs[w].at[jme]]
            for k, part in enumerate(parts):
                pltpu.make_async_remote_copy(src_ref=part, dst_ref=part, send_sem=send_sems.at[4 * w + k],
                                             recv_sem=recv_sems.at[4 * w + k], device_id=me, device_id_type=MESH).wait_recv()
        for cp in sent:
            cp.wait_send()

    return pl.pallas_call(
        body, name=name,
        in_specs=_any_specs(2 * nw), out_specs=_any_specs(nw),
        out_shape=[jax.ShapeDtypeStruct(a.shape, a.dtype) for a in landed],
        input_output_aliases={nw + i: i for i in range(nw)},
        scratch_shapes=[pltpu.SemaphoreType.DMA((4 * nw,)), pltpu.SemaphoreType.DMA((4 * nw,))],
    )(*shards, *landed)


def _exchange_halves(name, grads, after=()):
    nw = len(grads)

    def body(*refs):
        g_refs, o_refs = refs[:nw], refs[nw + len(after):2 * nw + len(after)]
        send_sems, recv_sems = refs[2 * nw + len(after):]
        x, y, c, _ = _mesh_place()
        copies = []
        for w in range(nw):
            cp = pltpu.make_async_remote_copy(src_ref=g_refs[w].at[:, 1 - c], dst_ref=o_refs[w], send_sem=send_sems.at[w],
                                              recv_sem=recv_sems.at[w], device_id=(x, y, 1 - c), device_id_type=MESH)
            cp.start()
            copies.append(cp)
        for cp in copies:
            cp.wait()

    return pl.pallas_call(
        body, name=name,
        in_specs=_any_specs(nw + len(after)), out_specs=_any_specs(nw),
        out_shape=[jax.ShapeDtypeStruct((N_CHIPS,) + g.shape[2:], g.dtype) for g in grads],
        scratch_shapes=[pltpu.SemaphoreType.DMA((nw,)), pltpu.SemaphoreType.DMA((nw,))],
    )(*grads, *after)


ELEMENTWISE_ROWS = 512


def _row_tile(r):
    for cand in range(min(r, ELEMENTWISE_ROWS) // 16 * 16, 0, -16):
        if r % cand == 0:
            return cand
    return r


def _add_own_half(c_idx, grad, got):
    _, _, r, cols = grad.shape
    tr = _row_tile(r)

    def body(c_ref, g_ref, o_ref, out_ref):
        out_ref[...] = (g_ref[...].astype(F32) + o_ref[...].astype(F32)).astype(BF16)

    return pl.pallas_call(
        body, name="add_own_half",
        grid_spec=pltpu.PrefetchScalarGridSpec(
            num_scalar_prefetch=1, grid=(N_CHIPS, r // tr),
            in_specs=[pl.BlockSpec((None, None, tr, cols), lambda j, i, c_ref: (j, c_ref[0], i, 0)),
                      pl.BlockSpec((None, tr, cols), lambda j, i, c_ref: (j, i, 0))],
            out_specs=pl.BlockSpec((None, tr, cols), lambda j, i, c_ref: (j, i, 0))),
        out_shape=jax.ShapeDtypeStruct((N_CHIPS, r, cols), BF16),
        compiler_params=_params(("parallel", "parallel")),
    )(c_idx, grad, got)


def _sum_pieces(place_idx, sums, landed):
    _, r, cols = sums.shape
    tr = _row_tile(r)

    def body(j_ref, own_ref, p_ref, o_ref):
        o_ref[...] = ((own_ref[...].astype(F32) + p_ref[0].astype(F32)) + p_ref[1].astype(F32)) + p_ref[2].astype(F32)

    return pl.pallas_call(
        body, name="sum_pieces",
        grid_spec=pltpu.PrefetchScalarGridSpec(
            num_scalar_prefetch=1, grid=(r // tr,),
            in_specs=[pl.BlockSpec((None, tr, cols), lambda i, j_ref: (j_ref[0], i, 0)),
                      pl.BlockSpec((N_CHIPS - 1, tr, cols), lambda i, j_ref: (0, i, 0))],
            out_specs=pl.BlockSpec((None, tr, cols), lambda i, j_ref: (j_ref[1], i, 0))),
        out_shape=jax.ShapeDtypeStruct((2, r, cols), F32),
        compiler_params=_params(("parallel",)),
    )(place_idx, sums, landed)


def _join_halves(name, halves):
    nw = len(halves)

    def body(*refs):
        o_refs = refs[nw:2 * nw]
        send_sems, recv_sems = refs[2 * nw:]
        x, y, c, _ = _mesh_place()
        copies = []
        for w in range(nw):
            cp = pltpu.make_async_remote_copy(src_ref=o_refs[w].at[c], dst_ref=o_refs[w].at[c], send_sem=send_sems.at[w],
                                              recv_sem=recv_sems.at[w], device_id=(x, y, 1 - c), device_id_type=MESH)
            cp.start()
            copies.append(cp)
        for w in range(nw):
            copies[w].wait_send()
            landed = o_refs[w].at[1 - c]
            pltpu.make_async_remote_copy(src_ref=landed, dst_ref=landed, send_sem=send_sems.at[w], recv_sem=recv_sems.at[w],
                                         device_id=(x, y, c), device_id_type=MESH).wait_recv()

    return pl.pallas_call(
        body, name=name,
        in_specs=_any_specs(nw), out_specs=_any_specs(nw),
        out_shape=[jax.ShapeDtypeStruct(h.shape, F32) for h in halves],
        input_output_aliases={i: i for i in range(nw)},
        scratch_shapes=[pltpu.SemaphoreType.DMA((nw,)), pltpu.SemaphoreType.DMA((nw,))],
    )(*halves)


SMALL_ROWS = 8


def _all_reduce_small(pack):
    rows, cols = pack.shape
    n_dev = 8

    def body(p_ref, o_ref, slots, send_sems, recv_sems):
        x, y, c, _ = _mesh_place()
        me = 4 * x + 2 * y + c
        slots[me] = p_ref[...]
        copies = []
        for k in range(1, n_dev):
            peer = (me + k) % n_dev
            cp = pltpu.make_async_remote_copy(src_ref=p_ref, dst_ref=slots.at[me], send_sem=send_sems.at[k],
                                              recv_sem=recv_sems.at[k],
                                              device_id=(peer // 4, (peer // 2) % 2, peer % 2), device_id_type=MESH)
            cp.start()
            copies.append(cp)
        for k in range(1, n_dev):
            src = (me + n_dev - k) % n_dev
            pltpu.make_async_remote_copy(src_ref=p_ref, dst_ref=slots.at[src], send_sem=send_sems.at[k],
                                         recv_sem=recv_sems.at[k], device_id=(x, y, c), device_id_type=MESH).wait_recv()
        for cp in copies:
            cp.wait_send()
        total = slots[0]
        for s in range(1, n_dev):
            total = total + slots[s]
        o_ref[...] = total

    return pl.pallas_call(
        body, name="all_reduce_small",
        in_specs=[pl.BlockSpec(memory_space=pltpu.VMEM)], out_specs=pl.BlockSpec(memory_space=pltpu.VMEM),
        out_shape=jax.ShapeDtypeStruct((rows, cols), F32),
        scratch_shapes=[pltpu.VMEM((n_dev, rows, cols), F32), pltpu.SemaphoreType.DMA((n_dev,)),
                        pltpu.SemaphoreType.DMA((n_dev,))],
    )(pack)


def _adamw(name, w, g, m, v):
    r, cols = w.shape
    tr = _row_tile(r)

    def body(w_ref, g_ref, m_ref, v_ref, d_ref, nm_ref, nv_ref):
        gv = g_ref[...]
        nm = ADAM_B1 * m_ref[...] + (1.0 - ADAM_B1) * gv
        nv = ADAM_B2 * v_ref[...] + (1.0 - ADAM_B2) * (gv * gv)
        m_hat = nm / (1.0 - ADAM_B1 ** ADAM_STEP)
        v_hat = nv / (1.0 - ADAM_B2 ** ADAM_STEP)
        d_ref[...] = -ADAM_LR * (m_hat / (jnp.sqrt(v_hat) + ADAM_EPS) + ADAM_WD * w_ref[...])
        nm_ref[...] = nm
        nv_ref[...] = nv

    spec = pl.BlockSpec((tr, cols), lambda i: (i, 0))
    return pl.pallas_call(
        body, name=name, grid=(r // tr,),
        in_specs=[spec] * 4, out_specs=[spec] * 3,
        out_shape=[jax.ShapeDtypeStruct((r, cols), F32)] * 3,
        compiler_params=_params(("parallel",)),
    )(w, g, m, v)


BIG = ["ffn1_w_in", "ffn1_w_out", "w_in", "conv_w_proj", "attn_w_o", "w_out", "ffn2_w_in", "ffn2_w_out", "conv_dw_kernel"]
COL_SHARDED = ("ffn1_w_in", "w_in", "ffn2_w_in")
SMALL = ["ffn1_norm", "mix_norm", "ffn2_norm", "conv_dw_bias", "conv_ln_g", "conv_ln_b", "q_norm", "k_norm", "attn_sinks", "rel_bias"]
WEIGHTS = ["ffn1_norm", "ffn1_w_in", "ffn1_w_out", "mix_norm", "w_in", "conv_dw_kernel", "conv_dw_bias", "conv_ln_g",
           "conv_ln_b", "conv_w_proj", "q_norm", "k_norm", "attn_sinks", "rel_bias", "attn_w_o", "w_out", "ffn2_norm",
           "ffn2_w_in", "ffn2_w_out"]
SMALL_PLACE = {"ffn1_norm": (0, 0, 1024), "mix_norm": (1, 0, 1024), "ffn2_norm": (2, 0, 1024), "conv_dw_bias": (3, 0, 1024),
               "conv_ln_g": (4, 0, 1024), "conv_ln_b": (5, 0, 1024), "q_norm": (6, 0, 64), "k_norm": (6, 128, 64),
               "attn_sinks": (6, 256, 16), "rel_bias": (7, 0, 512)}
LOSS_PLACE = (6, 384)


def _pack_small(vals, fill=0.0, loss=None):
    pack = jnp.full((SMALL_ROWS, D_MODEL), fill, F32)
    for name, (row, lane, n) in SMALL_PLACE.items():
        pack = pack.at[row, lane:lane + n].set(vals[name].reshape(n))
    if loss is not None:
        pack = pack.at[LOSS_PLACE[0], LOSS_PLACE[1]].set(loss)
    return pack


def _unpack_small(pack, shapes):
    return {name: pack[row, lane:lane + n].reshape(shapes[name]) for name, (row, lane, n) in SMALL_PLACE.items()}


def _shard_halves(name, a):
    if name == "conv_dw_kernel":
        a = jnp.pad(a, ((0, CONV_PAD - CONV_WIDTH), (0, 0)))
    r, cols = a.shape
    return a.reshape(2, r // 2, cols)


GATHER_GROUPS = {"A": ["ffn1_w_in", "ffn1_w_out"],
                 "B": ["w_in", "conv_dw_kernel", "conv_w_proj", "attn_w_o", "w_out"],
                 "C": ["ffn2_w_in", "ffn2_w_out"]}


class _MeshComm:
    def __init__(self, wts):
        self.c_idx = lax.axis_index("c").astype(jnp.int32).reshape(1)
        self.place_idx = jnp.stack([2 * lax.axis_index("x") + lax.axis_index("y"), lax.axis_index("c")]).astype(jnp.int32)
        self.gathers, self.exchanges, self.reductions, self.joins, self.reduced = {}, {}, {}, {}, {}
        self.tokens = ()
        self.shards = {n: _shard_halves(n, wts[n]) if n == "conv_dw_kernel" else _shard_halves(n, wts[n]).astype(BF16)
                       for n in BIG}
        self._gather_start("A", ())

    def _gather_start(self, group, after):
        shards = [self.shards[n] for n in GATHER_GROUPS[group]]
        lands = [lax.empty((N_CHIPS,) + s.shape, s.dtype) for s in shards]
        self.gathers[group] = _copy_start("gather_start_" + group, shards, lands, self._gather_plan(group), after=after)
        self.tokens = (self.gathers[group][-1],)

    @staticmethod
    def _gather_plan(group):
        return _gather_both_cores_plan if group == "C" else _gather_plan

    def weights(self, group, after):
        send_sems, recv_sems, shards, lands, token = self.gathers.pop(group)
        after = [token if after is None else after]
        if group == "A":
            after += [self.shards[n] for g in ("B", "C") for n in GATHER_GROUPS[g]]
        shards, lands = _copy_wait("gather_wait_" + group, send_sems, recv_sems, shards, lands, after,
                                   self._gather_plan(group))
        gathered = lands if group == "C" else _gather_forward("gather_forward_" + group, shards, lands)
        self.tokens = ()
        following = {"A": "B", "B": "C"}.get(group)
        if following:
            self._gather_start(following, (gathered[0],))
        out = {}
        for n, g4 in zip(GATHER_GROUPS[group], gathered):
            r, cols = g4.shape[2] * 2, g4.shape[3]
            if n in COL_SHARDED:
                out[n] = g4.reshape(N_CHIPS, r, cols)
            elif n == "conv_dw_kernel":
                out[n] = g4.reshape(N_CHIPS, r, cols).transpose(1, 0, 2).reshape(r, N_CHIPS * cols)
            else:
                out[n] = g4.reshape(N_CHIPS * r, cols)
        return out

    def reduce_start(self, group, grads, behind=False):
        names = list(grads)
        g4 = []
        for n in names:
            a = grads[n]
            if n == "conv_dw_kernel":
                a = a.reshape(CONV_PAD, N_CHIPS, -1).transpose(1, 0, 2)
            elif n not in COL_SHARDED:
                a = a.reshape(N_CHIPS, a.shape[0] // N_CHIPS, a.shape[1])
            g4.append(a.reshape(N_CHIPS, 2, a.shape[1] // 2, a.shape[2]))
        if behind:
            lands = [lax.empty((N_CHIPS,) + g.shape[2:], g.dtype) for g in g4]
            started = _copy_start("exchange_start_" + group, g4, lands, _exchange_plan)
            self.exchanges[group] = (names,) + started
            return (started[-1],)
        return self._scatter_start(group, names, g4, _exchange_halves("exchange_halves_" + group, g4))

    def exchange_finish(self, group, after):
        names, send_sems, recv_sems, g4, lands, _ = self.exchanges.pop(group)
        g4, got = _copy_wait("exchange_wait_" + group, send_sems, recv_sems, g4, lands, after, _exchange_plan)
        return self._scatter_start(group, names, g4, got)

    def _scatter_start(self, group, names, g4, got):
        sums = [_add_own_half(self.c_idx, a, b) for a, b in zip(g4, got)]
        lands = [lax.empty((N_CHIPS - 1,) + s.shape[1:], s.dtype) for s in sums]
        started = _copy_start("scatter_start_" + group, sums, lands, _scatter_plan)
        self.reductions[group] = (names,) + started
        return (started[-1],)

    def reduce_finish(self, group, after, behind=False):
        names, send_sems, recv_sems, sums, lands, _ = self.reductions.pop(group)
        sums, lands = _copy_wait("scatter_wait_" + group, send_sems, recv_sems, sums, lands, after, _scatter_plan)
        halves = [_sum_pieces(self.place_idx, s, p) for s, p in zip(sums, lands)]
        if behind:
            started = _copy_start("join_start_" + group, halves, [], _join_plan)
            self.joins[group] = (names,) + started
            return (started[-1],)
        self.reduced.update(zip(names, _join_halves("join_halves_" + group, halves)))
        return ()

    def join_finish(self, group, after):
        names, send_sems, recv_sems, halves, _, _ = self.joins.pop(group)
        self.reduced.update(zip(names, _copy_wait("join_wait_" + group, send_sems, recv_sems, halves, [], after, _join_plan)[0]))


def kernel(x, ffn1_norm, ffn1_w_in, ffn1_w_out, mix_norm, w_in, conv_dw_kernel, conv_dw_bias, conv_ln_g, conv_ln_b, conv_w_proj, q_norm, k_norm, attn_sinks, rel_bias, attn_w_o, w_out, ffn2_norm, ffn2_w_in, ffn2_w_out, loss_target, m_ffn1_norm, m_ffn1_w_in, m_ffn1_w_out, m_mix_norm, m_w_in, m_conv_dw_kernel, m_conv_dw_bias, m_conv_ln_g, m_conv_ln_b, m_conv_w_proj, m_q_norm, m_k_norm, m_attn_sinks, m_rel_bias, m_attn_w_o, m_w_out, m_ffn2_norm, m_ffn2_w_in, m_ffn2_w_out, v_ffn1_norm, v_ffn1_w_in, v_ffn1_w_out, v_mix_norm, v_w_in, v_conv_dw_kernel, v_conv_dw_bias, v_conv_ln_g, v_conv_ln_b, v_conv_w_proj, v_q_norm, v_k_norm, v_attn_sinks, v_rel_bias, v_attn_w_o, v_w_out, v_ffn2_norm, v_ffn2_w_in, v_ffn2_w_out):
    args = dict(locals())
    wts = {n: args[n] for n in WEIGHTS}
    mom = {n: args["m_" + n] for n in WEIGHTS}
    var = {n: args["v_" + n] for n in WEIGHTS}
    comm = _MeshComm(wts)
    small = {n: wts[n] if n in ("attn_sinks", "rel_bias") else wts[n].reshape(1, -1) for n in SMALL}
    loss_part, grad_x, g = _local_step(x[0], loss_target[0], small, comm)

    small_sum = _all_reduce_small(_pack_small(g, loss=loss_part))
    loss = small_sum[LOSS_PLACE[0], LOSS_PLACE[1]]
    small_shapes = {n: wts[n].shape for n in SMALL}
    g_small = _unpack_small(small_sum, small_shapes)

    grads, delta, new_m, new_v = {}, {}, {}, {}
    for n in BIG:
        j = comm.reduced[n]
        gs = j.reshape(j.shape[1] * 2, j.shape[2])
        pad = n == "conv_dw_kernel"
        ws, ms, vs = (_shard_halves(n, a).reshape(gs.shape) for a in (wts[n], mom[n], var[n]))
        d, nm, nv = _adamw("adamw_" + n, ws, gs, ms, vs)
        cut = (lambda a: a[:CONV_WIDTH]) if pad else (lambda a: a)
        grads[n], delta[n], new_m[n], new_v[n] = cut(gs), cut(d), cut(nm), cut(nv)
    d, nm, nv = _adamw("adamw_small", _pack_small(wts), small_sum, _pack_small(mom), _pack_small(var, fill=1.0))
    grads.update(g_small)
    delta.update(_unpack_small(d, small_shapes))
    new_m.update(_unpack_small(nm, small_shapes))
    new_v.update(_unpack_small(nv, small_shapes))

    return (loss, grad_x[None], *[grads[n] for n in WEIGHTS], *[delta[n] for n in WEIGHTS],
            *[new_m[n] for n in WEIGHTS], *[new_v[n] for n in WEIGHTS])
```

```python
import functools
import math

import jax
import jax.numpy as jnp
from jax import lax
from jax.experimental import pallas as pl
from jax.experimental.pallas import tpu as pltpu

F32 = jnp.float32
BF16 = jnp.bfloat16
MESH = pl.DeviceIdType.MESH

EPS = 1e-6
D_MODEL = 1024
D_FF = 2816
N_CHIPS = 4
SHARD_W = 2 * D_FF // N_CHIPS
HEAD_DIM = 64
N_Q_HEADS = 16
N_KV_HEADS = 4
GROUP = N_Q_HEADS // N_KV_HEADS
BLOCK = 128
QROWS = GROUP * BLOCK
N_BUCKETS = 32
MAX_DISTANCE = 128
CONV_WIDTH = 31
CONV_PAD = 32
NEG = float(jnp.finfo(jnp.float32).min)

ADAM_LR = 0.001
ADAM_B1 = 0.9
ADAM_B2 = 0.999
ADAM_EPS = 1e-08
ADAM_WD = 0.01
ADAM_STEP = 10

VMEM_LIMIT_BYTES = 56 * 1024 * 1024
ROW_TILE = 1024
TOKEN_TILE = 1024
CONV_TILE = 256
CONV_ROWS = 128
LANES = 128

COL_CONV_A, COL_CONV_G, COL_Q, COL_K, COL_V, COL_GC, COL_GA = 0, 1024, 2048, 3072, 3328, 3584, 4608
IN_W = 5632


def _params(sem, vmem=VMEM_LIMIT_BYTES):
    return pltpu.CompilerParams(dimension_semantics=sem, vmem_limit_bytes=vmem)


def _sigmoid(x):
    return 1.0 / (1.0 + jnp.exp(-x))


def _dot(a, b, trans_a=False, trans_b=False, precision=None):
    dn = (((0,) if trans_a else (1,), (1,) if trans_b else (0,)), ((), ()))
    return lax.dot_general(a, b, dn, preferred_element_type=F32, precision=precision)


def _mm(name, grid, a, a_spec, b, b_spec, acc_shape, *, trans_a=False, trans_b=False, a_pre=None, b_pre=None,
        extras=(), extra_specs=(), tokens=(), out_shape, out_specs, epilogue, chunked=False,
        sem=("parallel", "parallel", "arbitrary")):
    n_k = grid[2]
    assert not chunked or (n_k == 1 and b_pre is None)
    extras = tuple(extras) + tuple(tokens)
    extra_specs = tuple(extra_specs) + (pl.BlockSpec((8, LANES), lambda i, j, kk: (0, 0)),) * len(tokens)
    n_extra = len(extras)
    n_out = len(out_shape)

    def body(a_ref, b_ref, *rest):
        ex = rest[:n_extra]
        outs = rest[n_extra:n_extra + n_out]
        ids = (pl.program_id(0), pl.program_id(1), pl.program_id(2))
        av = a_ref[...]
        if a_pre is not None:
            av = a_pre(av)
        if chunked:
            for c0, cw in _col_chunks(acc_shape[1]):
                cols = pl.ds(c0, cw)
                epilogue(_dot(av, b_ref[cols, :] if trans_b else b_ref[:, cols], trans_a, trans_b), ex, outs, ids, cols)
            return
        bv = b_ref[...]
        if b_pre is not None:
            bv = b_pre(bv)
        if n_k == 1:
            epilogue(_dot(av, bv, trans_a, trans_b), ex, outs, ids)
        else:
            acc = rest[-1]

            @pl.when(ids[2] == 0)
            def _():
                acc[...] = jnp.zeros_like(acc)

            acc[...] += _dot(av, bv, trans_a, trans_b)

            @pl.when(ids[2] == n_k - 1)
            def _():
                epilogue(acc[...], ex, outs, ids)

    scratch = [] if n_k == 1 else [pltpu.VMEM(acc_shape, F32)]
    return pl.pallas_call(
        body, name=name, grid=grid,
        in_specs=[a_spec, b_spec, *extra_specs],
        out_specs=list(out_specs), out_shape=list(out_shape),
        scratch_shapes=scratch, compiler_params=_params(sem),
    )(a, b, *extras)


MXU_WIDTH = 256


def _col_chunks(n, width=2 * MXU_WIDTH):
    return [(c0, min(width, n - c0)) for c0 in range(0, n, width)]


def _half_bf16(v):
    return (0.5 * v).astype(BF16)


def _to_bf16(v):
    return v.astype(BF16)


def _rmsnorm_fwd(name, x, g, tokens=()):
    t, d = x.shape
    tm = min(ROW_TILE, t)

    def body(x_ref, g_ref, *rest):
        o_ref = rest[-1]
        xv = x_ref[...]
        r = lax.rsqrt(jnp.mean(xv * xv, axis=-1, keepdims=True) + EPS)
        o_ref[...] = (xv * r * g_ref[...]).astype(BF16)

    return pl.pallas_call(
        body, name=name, grid=(t // tm,),
        in_specs=[pl.BlockSpec((tm, d), lambda i: (i, 0)), pl.BlockSpec((1, d), lambda i: (0, 0))]
        + [pl.BlockSpec((8, LANES), lambda i: (0, 0))] * len(tokens),
        out_specs=pl.BlockSpec((tm, d), lambda i: (i, 0)),
        out_shape=jax.ShapeDtypeStruct((t, d), BF16),
        compiler_params=_params(("parallel",)),
    )(x, g, *tokens)


def _ffn_in(name, n, w_in4, tokens=()):
    t, d = n.shape
    tm = min(ROW_TILE, t)

    def body(n_ref, wa_ref, wb_ref, *rest):
        ab_ref, h_ref = rest[-2:]
        nv = n_ref[...]
        for c0, cw in _col_chunks(SHARD_W):
            cols = pl.ds(c0, cw)
            a = _dot(nv, wa_ref[:, cols])
            b = _dot(nv, wb_ref[:, cols])
            h_ref[:, cols] = (a * _sigmoid(a) * b).astype(BF16)
            ab_ref[0, :, cols] = a.astype(BF16)
            ab_ref[1, :, cols] = b.astype(BF16)

    return pl.pallas_call(
        body, name=name, grid=(2, t // tm),
        in_specs=[pl.BlockSpec((tm, d), lambda j, i: (i, 0)),
                  pl.BlockSpec((None, d, SHARD_W), lambda j, i: (j, 0, 0)),
                  pl.BlockSpec((None, d, SHARD_W), lambda j, i: (j + 2, 0, 0))]
        + [pl.BlockSpec((8, LANES), lambda j, i: (0, 0))] * len(tokens),
        out_specs=[pl.BlockSpec((2, tm, SHARD_W), lambda j, i: (0, i, j)),
                   pl.BlockSpec((tm, SHARD_W), lambda j, i: (i, j))],
        out_shape=[jax.ShapeDtypeStruct((2, t, D_FF), BF16), jax.ShapeDtypeStruct((t, D_FF), BF16)],
        compiler_params=_params(("parallel", "parallel")),
    )(n, w_in4, w_in4, *tokens)


def _mm_residual(name, a, w, res, scale, next_gain=None, loss_target=None):
    t, k = a.shape
    n = w.shape[1]
    tm = min(ROW_TILE, t)
    row = pl.BlockSpec((tm, n), lambda i, j, kk: (i, 0))
    extras, specs = [res], [row]
    shapes, out_specs = [jax.ShapeDtypeStruct((t, n), F32)], [row]
    if next_gain is not None:
        extras.append(next_gain)
        specs.append(pl.BlockSpec((1, n), lambda i, j, kk: (0, 0)))
        shapes.append(jax.ShapeDtypeStruct((t, n), BF16))
        out_specs.append(row)
    if loss_target is not None:
        extras.append(loss_target)
        specs.append(row)
        shapes.append(jax.ShapeDtypeStruct((8, LANES), F32))
        out_specs.append(pl.BlockSpec((8, LANES), lambda i, j, kk: (0, 0)))

    def epilogue(acc, ex, outs, ids):
        y = ex[0][...] + scale * acc
        if loss_target is None:
            outs[0][...] = y
        if next_gain is not None:
            r = lax.rsqrt(jnp.mean(y * y, axis=-1, keepdims=True) + EPS)
            outs[1][...] = (y * r * ex[1][...]).astype(BF16)
        if loss_target is not None:
            diff = y - ex[1][...]
            outs[0][...] = diff * (1.0 / n)
            part = jnp.full((8, LANES), 0.5 / n * jnp.sum(diff * diff), F32)

            @pl.when(ids[0] == 0)
            def _():
                outs[1][...] = part

            @pl.when(ids[0] > 0)
            def _():
                outs[1][...] += part

    sem = ("parallel" if loss_target is None else "arbitrary", "parallel", "arbitrary")
    out = _mm(name, (t // tm, 1, 1), a, pl.BlockSpec((tm, k), lambda i, j, kk: (i, 0)),
              w, pl.BlockSpec((k, n), lambda i, j, kk: (0, 0)), (tm, n),
              extras=extras, extra_specs=specs, out_shape=shapes, out_specs=out_specs, epilogue=epilogue, sem=sem)
    return out[0] if len(out) == 1 else tuple(out)


def _ffn_fwd(tag, x, n, w_in4, w_out, tokens=(), **tail):
    ab, h = _ffn_in(tag + "_in", n, w_in4, tokens)
    y = _mm_residual(tag + "_out", h, w_out, x, 0.5, **tail)
    return y, (n, ab, h)


def _ffn_bwd(tag, dres, x, g, saved, w_in4_t, w_out, tokens=(), on_first=None, on_weight_grads=None):
    n, ab, h = saved
    t, d = x.shape
    tm = min(ROW_TILE, t)
    tk = min(TOKEN_TILE, t)
    half_w = SHARD_W

    def dact_epilogue(acc, ex, outs, ids, cols):
        a = ex[0][0, :, cols].astype(F32)
        b = ex[0][1, :, cols].astype(F32)
        sig = _sigmoid(a)
        outs[0][0, :, cols] = (acc * b * (sig * (1.0 + a * (1.0 - sig)))).astype(BF16)
        outs[0][1, :, cols] = (acc * (a * sig)).astype(BF16)

    du = _mm(tag + "_dact", (2, t // tm, 1),
             dres, pl.BlockSpec((tm, d), lambda j, i, kk: (i, 0)),
             w_out, pl.BlockSpec((half_w, d), lambda j, i, kk: (j, 0)), (tm, half_w),
             trans_b=True, a_pre=_half_bf16,
             extras=(ab,), extra_specs=(pl.BlockSpec((2, tm, half_w), lambda j, i, kk: (0, i, j)),), tokens=tokens,
             out_shape=(jax.ShapeDtypeStruct((2, t, D_FF), BF16),),
             out_specs=(pl.BlockSpec((2, tm, half_w), lambda j, i, kk: (0, i, j)),),
             epilogue=dact_epilogue, chunked=True)[0]

    def store_epilogue(acc, ex, outs, ids):
        outs[0][...] = acc.astype(BF16)

    early = () if on_first is None else on_first(du)

    dw_out = _mm(tag + "_dwout", (2, 1, t // tk),
                 h, pl.BlockSpec((tk, half_w), lambda i, j, kk: (kk, i)),
                 dres, pl.BlockSpec((tk, d), lambda i, j, kk: (kk, 0)), (half_w, d),
                 trans_a=True, b_pre=_half_bf16, tokens=early,
                 out_shape=(jax.ShapeDtypeStruct((D_FF, d), BF16),),
                 out_specs=(pl.BlockSpec((half_w, d), lambda i, j, kk: (i, 0)),),
                 epilogue=store_epilogue)[0]

    dw_in4 = _mm(tag + "_dwin", (1, N_CHIPS, t // tk),
                 n, pl.BlockSpec((tk, d), lambda i, j, kk: (kk, 0)),
                 du, pl.BlockSpec((None, tk, SHARD_W), lambda i, j, kk: (j // 2, kk, j % 2)), (d, SHARD_W),
                 trans_a=True,
                 out_shape=(jax.ShapeDtypeStruct((N_CHIPS, d, SHARD_W), BF16),),
                 out_specs=(pl.BlockSpec((None, d, SHARD_W), lambda i, j, kk: (j, 0, 0)),),
                 epilogue=store_epilogue)[0]

    late = () if on_weight_grads is None else on_weight_grads(dw_in4, dw_out)

    tn = min(NORM_GRAD_TILE, t)
    shard_specs = [pl.BlockSpec((None, tn, SHARD_W), functools.partial(lambda i, s: (s // 2, i, s % 2), s=s))
                   for s in range(N_CHIPS)]
    dx, dg = _norm_input_grad(tag + "_dn", du, shard_specs, w_in4_t, x, g, dres, late)
    return dx, dw_in4, dw_out, dg


NORM_GRAD_TILE = 512
NORM_GRAD_CHUNK = 256


def _transpose_shards(name, w4):
    n, r, c = w4.shape

    def body(w_ref, o_ref):
        o_ref[...] = w_ref[...].T

    return pl.pallas_call(
        body, name=name, grid=(n,),
        in_specs=[pl.BlockSpec((None, r, c), lambda s: (s, 0, 0))],
        out_specs=pl.BlockSpec((None, c, r), lambda s: (s, 0, 0)),
        out_shape=jax.ShapeDtypeStruct((n, c, r), w4.dtype),
        compiler_params=_params(("parallel",)),
    )(w4)


def _norm_input_grad(name, a, shard_specs, w4_t, x, g, dres, tokens=()):
    t, d = x.shape
    tn = min(NORM_GRAD_TILE, t)
    chunk = min(NORM_GRAD_CHUNK, tn)
    ns = len(shard_specs)

    def body(*refs):
        a_refs, (wt_ref, x_ref, g_ref, dres_ref) = refs[:ns], refs[ns:ns + 4]
        out_ref, dg_ref = refs[-2:]
        part = jnp.zeros((1, d), F32)
        for r0 in range(0, tn, chunk):
            rows = pl.ds(r0, chunk)
            acc = _dot(a_refs[0][rows, :], wt_ref[0])
            for s in range(1, ns):
                acc = acc + _dot(a_refs[s][rows, :], wt_ref[s])
            xv = x_ref[rows, :]
            r = lax.rsqrt(jnp.mean(xv * xv, axis=-1, keepdims=True) + EPS)
            w = acc * g_ref[...]
            out_ref[rows, :] = dres_ref[rows, :] + (r * w - xv * (r * r * r) * jnp.mean(xv * w, axis=-1, keepdims=True))
            part = part + jnp.sum(acc * (xv * r), axis=0, keepdims=True)
        i = pl.program_id(0)

        @pl.when(i == 0)
        def _():
            dg_ref[...] = part

        @pl.when(i > 0)
        def _():
            dg_ref[...] += part

    row = pl.BlockSpec((tn, d), lambda i: (i, 0))
    vec = pl.BlockSpec((1, d), lambda i: (0, 0))
    return pl.pallas_call(
        body, name=name, grid=(t // tn,),
        in_specs=list(shard_specs) + [pl.BlockSpec(w4_t.shape, lambda i: (0, 0, 0)), row, vec, row]
        + [pl.BlockSpec((8, LANES), lambda i: (0, 0))] * len(tokens),
        out_specs=[row, vec],
        out_shape=[jax.ShapeDtypeStruct((t, d), F32), jax.ShapeDtypeStruct((1, d), F32)],
        compiler_params=_params(("arbitrary",)),
    )(*[a] * ns, w4_t, x, g, dres, *tokens)


def _conv_fill(zp_ref, a_ref, g_ref, ah_ref, gh_ref, i):
    zh = ah_ref[...].astype(F32) * _sigmoid(gh_ref[...].astype(F32))
    zp_ref[pl.ds(0, CONV_PAD), :] = jnp.where(i > 0, zh, 0.0)
    zp_ref[pl.ds(CONV_PAD, a_ref.shape[0]), :] = a_ref[...].astype(F32) * _sigmoid(g_ref[...].astype(F32))


def _shift_groups(shifts):
    groups = {}
    for j, s in shifts:
        groups.setdefault(s % 8, []).append((j, s // 8))
    return groups


def _windows(zp_ref, r0, lanes, groups):
    for q, taps in groups.items():
        deepest = max(p for _, p in taps)
        win = zp_ref[pl.ds(r0 + q, 8 * deepest + CONV_ROWS), lanes]
        for j, p in taps:
            yield j, win[8 * p:8 * p + CONV_ROWS]


def _conv_apply(zp_ref, out_ref, dw_ref, bias_ref, tm, ch, shifts):
    groups = _shift_groups(shifts)
    for cc in range(ch // LANES):
        lanes = pl.ds(cc * LANES, LANES)
        w = [dw_ref[pl.ds(j, 1), lanes] for j in range(CONV_WIDTH)]
        for r0 in range(0, tm, CONV_ROWS):
            if bias_ref is None:
                acc = jnp.zeros((CONV_ROWS, LANES), F32)
            else:
                acc = jnp.broadcast_to(bias_ref[:, lanes], (CONV_ROWS, LANES))
            for j, rows in _windows(zp_ref, r0, lanes, groups):
                acc = acc + w[j] * rows
            out_ref[pl.ds(r0, CONV_ROWS), lanes] = acc


FWD_SHIFTS = [(j, CONV_PAD - (CONV_WIDTH - 1) + j) for j in range(CONV_WIDTH)]
BWD_SHIFTS = [(j, CONV_WIDTH - 1 - j) for j in range(CONV_WIDTH)]


def _conv_taps(zp_ref, z1_ref, dw_ref, bias_ref, tm, ch):
    _conv_apply(zp_ref, z1_ref, dw_ref, bias_ref, tm, ch, FWD_SHIFTS)


def _conv_specs(tm, ch):
    per = tm // CONV_PAD
    cb = COL_CONV_G // ch
    return [pl.BlockSpec((tm, ch), lambda i: (i, 0)),
            pl.BlockSpec((tm, ch), lambda i: (i, cb)),
            pl.BlockSpec((CONV_PAD, ch), lambda i: (jnp.maximum(i * per - 1, 0), 0)),
            pl.BlockSpec((CONV_PAD, ch), lambda i: (jnp.maximum(i * per - 1, 0), cb))]


def _conv_fwd(p, dw, bias, ln_g, ln_b):
    t = p.shape[0]
    ch = D_MODEL
    tm = min(CONV_TILE, t)

    def body(a_ref, g_ref, ah_ref, gh_ref, dw_ref, bias_ref, lg_ref, lb_ref, o_ref, z1_ref, zp_ref):
        i = pl.program_id(0)
        _conv_fill(zp_ref, a_ref, g_ref, ah_ref, gh_ref, i)
        _conv_taps(zp_ref, z1_ref, dw_ref, bias_ref, tm, ch)
        z1 = z1_ref[...]
        mu = jnp.mean(z1, axis=-1, keepdims=True)
        zc = z1 - mu
        rs = lax.rsqrt(jnp.mean(zc * zc, axis=-1, keepdims=True) + EPS)
        z2 = zc * rs * lg_ref[...] + lb_ref[...]
        o_ref[...] = (z2 * _sigmoid(z2)).astype(BF16)

    vec = pl.BlockSpec((1, ch), lambda i: (0, 0))
    return pl.pallas_call(
        body, name="conv_fwd", grid=(t // tm,),
        in_specs=_conv_specs(tm, ch) + [pl.BlockSpec((CONV_PAD, ch), lambda i: (0, 0)), vec, vec, vec],
        out_specs=[pl.BlockSpec((tm, ch), lambda i: (i, 0)), pl.BlockSpec((tm, ch), lambda i: (i, 0))],
        out_shape=[jax.ShapeDtypeStruct((t, ch), BF16), jax.ShapeDtypeStruct((t, ch), F32)],
        scratch_shapes=[pltpu.VMEM((CONV_PAD + tm, ch), F32)],
        compiler_params=_params(("parallel",)),
    )(p, p, p, p, dw, bias, ln_g, ln_b)


def _conv_bwd_ln(p, z1_saved, dz3, ln_g, ln_b):
    t = p.shape[0]
    ch = D_MODEL
    tm = min(CONV_TILE, t)

    def body(a_ref, g_ref, ah_ref, gh_ref, z1_ref, dz3_ref, lg_ref, lb_ref,
             dz1_ref, ddw_ref, dbias_ref, dlg_ref, dlb_ref, zp_ref):
        i = pl.program_id(0)
        _conv_fill(zp_ref, a_ref, g_ref, ah_ref, gh_ref, i)
        z1 = z1_ref[...]
        mu = jnp.mean(z1, axis=-1, keepdims=True)
        zc = z1 - mu
        rs = lax.rsqrt(jnp.mean(zc * zc, axis=-1, keepdims=True) + EPS)
        xh = zc * rs
        z2 = xh * lg_ref[...] + lb_ref[...]
        sig = _sigmoid(z2)
        dz2 = dz3_ref[...].astype(F32) * (sig * (1.0 + z2 * (1.0 - sig)))
        dxh = dz2 * lg_ref[...]
        dz1 = rs * (dxh - jnp.mean(dxh, axis=-1, keepdims=True) - xh * jnp.mean(dxh * xh, axis=-1, keepdims=True))
        dz1_ref[...] = dz1

        @pl.when(i == 0)
        def _():
            ddw_ref[...] = jnp.zeros_like(ddw_ref)
            dbias_ref[...] = jnp.zeros_like(dbias_ref)
            dlg_ref[...] = jnp.zeros_like(dlg_ref)
            dlb_ref[...] = jnp.zeros_like(dlb_ref)

        dlg_ref[...] += jnp.sum(dz2 * xh, axis=0, keepdims=True)
        dlb_ref[...] += jnp.sum(dz2, axis=0, keepdims=True)
        dbias_ref[...] += jnp.sum(dz1, axis=0, keepdims=True)
        groups = _shift_groups(FWD_SHIFTS)
        for cc in range(ch // LANES):
            lanes = pl.ds(cc * LANES, LANES)
            accs = [jnp.zeros((8, LANES), F32) for _ in range(CONV_WIDTH)]
            for r0 in range(0, tm, CONV_ROWS):
                dzc = dz1_ref[pl.ds(r0, CONV_ROWS), lanes]
                for j, rows in _windows(zp_ref, r0, lanes, groups):
                    accs[j] = accs[j] + jnp.sum((dzc * rows).reshape(CONV_ROWS // 8, 8, LANES), axis=0)
            for j in range(CONV_WIDTH):
                ddw_ref[pl.ds(j, 1), lanes] += jnp.sum(accs[j], axis=0, keepdims=True)

    vec = pl.BlockSpec((1, ch), lambda i: (0, 0))
    return pl.pallas_call(
        body, name="conv_bwd_ln", grid=(t // tm,),
        in_specs=_conv_specs(tm, ch) + [pl.BlockSpec((tm, ch), lambda i: (i, 0)),
                                        pl.BlockSpec((tm, ch), lambda i: (i, 0)), vec, vec],
        out_specs=[pl.BlockSpec((tm, ch), lambda i: (i, 0)), pl.BlockSpec((CONV_PAD, ch), lambda i: (0, 0)), vec, vec, vec],
        out_shape=[jax.ShapeDtypeStruct((t, ch), F32), jax.ShapeDtypeStruct((CONV_PAD, ch), F32)]
        + [jax.ShapeDtypeStruct((1, ch), F32)] * 3,
        scratch_shapes=[pltpu.VMEM((CONV_PAD + tm, ch), F32)],
        compiler_params=_params(("arbitrary",)),
    )(p, p, p, p, z1_saved, dz3, ln_g, ln_b)


def _conv_bwd_glu(p, dz1, dw, dq, dkv, dgates):
    t = p.shape[0]
    ch = D_MODEL
    tm = min(CONV_TILE, t)
    per = tm // CONV_PAD
    n_halo = t // CONV_PAD
    cb = COL_CONV_G // ch

    def body(a_ref, g_ref, dz_ref, dzn_ref, dw_ref, dq_ref, dkv_ref, dgates_ref, o_ref, zp_ref, z0_ref):
        i = pl.program_id(0)
        o_ref[:, pl.ds(COL_Q, Q_W)] = dq_ref[...]
        o_ref[:, pl.ds(COL_K, 2 * KV_W)] = dkv_ref[...]
        o_ref[:, pl.ds(COL_GC, ch)] = dgates_ref[0]
        o_ref[:, pl.ds(COL_GA, ch)] = dgates_ref[1]
        zp_ref[pl.ds(0, tm), :] = dz_ref[...]
        zp_ref[pl.ds(tm, CONV_PAD), :] = jnp.where(i < t // tm - 1, dzn_ref[...], 0.0)
        _conv_apply(zp_ref, z0_ref, dw_ref, None, tm, ch, BWD_SHIFTS)
        dz0 = z0_ref[...]
        a = a_ref[...].astype(F32)
        sig = _sigmoid(g_ref[...].astype(F32))
        o_ref[:, pl.ds(0, ch)] = (dz0 * sig).astype(BF16)
        o_ref[:, pl.ds(ch, ch)] = (dz0 * a * sig * (1.0 - sig)).astype(BF16)

    return pl.pallas_call(
        body, name="conv_bwd_glu", grid=(t // tm,),
        in_specs=[pl.BlockSpec((tm, ch), lambda i: (i, 0)), pl.BlockSpec((tm, ch), lambda i: (i, cb)),
                  pl.BlockSpec((tm, ch), lambda i: (i, 0)),
                  pl.BlockSpec((CONV_PAD, ch), lambda i: (jnp.minimum((i + 1) * per, n_halo - 1), 0)),
                  pl.BlockSpec((CONV_PAD, ch), lambda i: (0, 0)),
                  pl.BlockSpec((tm, Q_W), lambda i: (i, 0)), pl.BlockSpec((tm, 2 * KV_W), lambda i: (i, 0)),
                  pl.BlockSpec((2, tm, ch), lambda i: (0, i, 0))],
        out_specs=pl.BlockSpec((tm, IN_W), lambda i: (i, 0)),
        out_shape=jax.ShapeDtypeStruct((t, IN_W), BF16),
        scratch_shapes=[pltpu.VMEM((tm + CONV_PAD, ch), F32), pltpu.VMEM((tm, ch), F32)],
        compiler_params=_params(("parallel",)),
    )(p, p, dz1, dz1, dw, dq, dkv, dgates)


def _bucket_onehot():
    qi = jnp.arange(BLOCK, dtype=jnp.int32)[:, None]
    kj = jnp.arange(2 * BLOCK, dtype=jnp.int32)[None, :]
    dist = jnp.maximum(qi + BLOCK - kj, 0)
    max_exact = N_BUCKETS // 2
    dflt = jnp.maximum(dist, 1).astype(F32)
    large = max_exact + (jnp.log(dflt / max_exact) / math.log(MAX_DISTANCE / max_exact)
                         * (N_BUCKETS - max_exact)).astype(jnp.int32)
    large = jnp.minimum(large, N_BUCKETS - 1)
    bucket = jnp.where(dist < max_exact, dist, large)
    onehot = bucket[None] == jnp.arange(N_BUCKETS, dtype=jnp.int32)[:, None, None]
    return onehot.astype(F32).reshape(N_BUCKETS, BLOCK * 2 * BLOCK)


def _bias_table(rel_bias_t, onehot):
    n = onehot.shape[1]
    tn = 4096

    def body(r_ref, oh_ref, o_ref):
        flat = pl.program_id(0) * tn + lax.broadcasted_iota(jnp.int32, (N_Q_HEADS, tn), 1)
        dist = (flat // (2 * BLOCK)) + BLOCK - (flat % (2 * BLOCK))
        bias = _dot(r_ref[...], oh_ref[...], precision=lax.Precision.HIGHEST)
        o_ref[...] = jnp.where((dist >= 0) & (dist < BLOCK), bias, NEG)

    return pl.pallas_call(
        body, name="bias_table", grid=(n // tn,),
        in_specs=[pl.BlockSpec((N_Q_HEADS, N_BUCKETS), lambda i: (0, 0)), pl.BlockSpec((N_BUCKETS, tn), lambda i: (0, i))],
        out_specs=pl.BlockSpec((N_Q_HEADS, tn), lambda i: (0, i)),
        out_shape=jax.ShapeDtypeStruct((N_Q_HEADS, n), F32),
        compiler_params=_params(("parallel",)),
    )(rel_bias_t, onehot)


def _bias_table_bwd(dbias, onehot):
    n = onehot.shape[1]
    tn = 4096

    def body(d_ref, oh_ref, o_ref):
        part = _dot(d_ref[...], oh_ref[...], trans_b=True, precision=lax.Precision.HIGHEST)
        i = pl.program_id(0)

        @pl.when(i == 0)
        def _():
            o_ref[...] = part

        @pl.when(i > 0)
        def _():
            o_ref[...] += part

    return pl.pallas_call(
        body, name="bias_table_bwd", grid=(n // tn,),
        in_specs=[pl.BlockSpec((N_Q_HEADS, tn), lambda i: (0, i)), pl.BlockSpec((N_BUCKETS, tn), lambda i: (0, i))],
        out_specs=pl.BlockSpec((N_Q_HEADS, N_BUCKETS), lambda i: (0, 0)),
        out_shape=jax.ShapeDtypeStruct((N_Q_HEADS, N_BUCKETS), F32),
        compiler_params=_params(("arbitrary",)),
    )(dbias, onehot)


def _lane_head(rows):
    return lax.broadcasted_iota(jnp.int32, (rows, KV_W), 1) // HEAD_DIM


def _group_rms(x, gain_wide):
    head = _lane_head(x.shape[0])
    sq = x * x
    r = jnp.zeros_like(x)
    for i in range(N_KV_HEADS):
        ms = jnp.sum(jnp.where(head == i, sq, 0.0), axis=-1, keepdims=True) * (1.0 / HEAD_DIM)
        r = jnp.where(head == i, lax.rsqrt(ms + EPS), r)
    return r, x * r * gain_wide


def _stack_heads(group):
    head = _lane_head(group.shape[0])
    return jnp.concatenate([jnp.where(head == i, group, jnp.zeros_like(group)) for i in range(N_KV_HEADS)], axis=0)


def _unstack_heads(stacked):
    head = _lane_head(BLOCK)
    out = jnp.where(head == 0, stacked[:BLOCK], 0.0)
    for i in range(1, N_KV_HEADS):
        out = out + jnp.where(head == i, stacked[i * BLOCK:(i + 1) * BLOCK], 0.0)
    return out


def _repeaters():
    row = lax.broadcasted_iota(jnp.int32, (KV_W, KV_W), 0)
    col = lax.broadcasted_iota(jnp.int32, (KV_W, KV_W), 1)
    return [(row == h * HEAD_DIM + col % HEAD_DIM).astype(BF16) for h in range(N_KV_HEADS)]


def _attn_probs(q_stack, k_rep, sink, bias, before_start):
    s = _dot(q_stack, k_rep, trans_b=True) * (1.0 / math.sqrt(HEAD_DIM)) + bias
    s = jnp.where(before_start, NEG, s)
    m = jnp.maximum(jnp.max(s, axis=-1, keepdims=True), sink)
    p = jnp.exp(s - m)
    es = jnp.exp(sink - m)
    inv = 1.0 / (jnp.sum(p, axis=-1, keepdims=True) + es)
    return p * inv, es * inv


def _before_start(n):
    col = lax.broadcasted_iota(jnp.int32, (QROWS, 2 * BLOCK), 1)
    return (col < BLOCK) & (n == 0)


STEP_BLOCKS = 4
KV_W = N_KV_HEADS * HEAD_DIM
Q_W = N_Q_HEADS * HEAD_DIM


def _attn_specs():
    qspec = pl.BlockSpec((BLOCK, Q_W), lambda n: (n, COL_Q // Q_W))
    kprev = pl.BlockSpec((BLOCK, KV_W), lambda n: (jnp.maximum(n - 1, 0), COL_K // KV_W))
    kcur = pl.BlockSpec((BLOCK, KV_W), lambda n: (n, COL_K // KV_W))
    vprev = pl.BlockSpec((BLOCK, KV_W), lambda n: (jnp.maximum(n - 1, 0), COL_V // KV_W))
    vcur = pl.BlockSpec((BLOCK, KV_W), lambda n: (n, COL_V // KV_W))
    gain = pl.BlockSpec((1, KV_W), lambda n: (0, 0))
    sink = pl.BlockSpec((N_KV_HEADS, QROWS, 1), lambda n: (0, 0, 0))
    bias = pl.BlockSpec((N_KV_HEADS, QROWS, 2 * BLOCK), lambda n: (0, 0, 0))
    return [qspec, kprev, kcur, vprev, vcur], gain, sink, bias


def _attn_fwd(p, gq, gk, sink_rows, bias):
    t = p.shape[0]
    nb = t // BLOCK
    per = STEP_BLOCKS if nb % STEP_BLOCKS == 0 else 1
    _, gain, sink, bspec = _attn_specs()

    def body(q_ref, kp_ref, kc_ref, vp_ref, vc_ref, gq_ref, gk_ref, sink_ref, bias_ref, o_ref, p_ref, ps_ref):
        first = pl.program_id(0) * per
        rep = _repeaters()
        kf = jnp.concatenate([kp_ref[...], kc_ref[...]], axis=0).astype(F32)
        kn = _group_rms(kf, gk_ref[...])[1].astype(BF16)
        v = jnp.concatenate([vp_ref[...], vc_ref[...]], axis=0)
        for h in range(N_KV_HEADS):
            k_rep = _dot(kn, rep[h]).astype(BF16)
            v_rep = _dot(v, rep[h]).astype(BF16)
            for sub in range(per):
                rows = pl.ds(sub * BLOCK, BLOCK)
                window = slice(sub * BLOCK, (sub + 2) * BLOCK)
                qn = _group_rms(q_ref[rows, pl.ds(h * KV_W, KV_W)].astype(F32), gq_ref[...])[1]
                pn, ps_ref[sub, h] = _attn_probs(_stack_heads(qn).astype(BF16), k_rep[window], sink_ref[h], bias_ref[h],
                                                 _before_start(first + sub))
                pn = pn.astype(BF16)
                p_ref[sub, h] = pn
                o_ref[rows, pl.ds(h * KV_W, KV_W)] = _unstack_heads(_dot(pn, v_rep[window])).astype(BF16)

    def kv_specs(col):
        return [pl.BlockSpec((BLOCK, KV_W), lambda n: (jnp.maximum(n * per - 1, 0), col // KV_W)),
                pl.BlockSpec((per * BLOCK, KV_W), lambda n: (n, col // KV_W))]

    return pl.pallas_call(
        body, name="attn_fwd", grid=(nb // per,),
        in_specs=[pl.BlockSpec((per * BLOCK, Q_W), lambda n: (n, COL_Q // Q_W))] + kv_specs(COL_K) + kv_specs(COL_V)
        + [gain, gain, sink, bspec],
        out_specs=[pl.BlockSpec((per * BLOCK, Q_W), lambda n: (n, 0)),
                   pl.BlockSpec((per, N_KV_HEADS, QROWS, 2 * BLOCK), lambda n: (n, 0, 0, 0)),
                   pl.BlockSpec((per, N_KV_HEADS, QROWS, 1), lambda n: (n, 0, 0, 0))],
        out_shape=[jax.ShapeDtypeStruct((t, Q_W), BF16),
                   jax.ShapeDtypeStruct((nb, N_KV_HEADS, QROWS, 2 * BLOCK), BF16),
                   jax.ShapeDtypeStruct((nb, N_KV_HEADS, QROWS, 1), F32)],
        compiler_params=_params(("parallel",)),
    )(p, p, p, p, p, gq, gk, sink_rows, bias)


def _attn_bwd(p, do, gq, gk, probs, sink_probs):
    t = p.shape[0]
    nb = t // BLOCK
    per = STEP_BLOCKS if nb % STEP_BLOCKS == 0 else 1
    bspec = _attn_specs()[3]
    gain = pl.BlockSpec((1, HEAD_DIM), lambda n: (0, 0))
    scale = 1.0 / math.sqrt(HEAD_DIM)

    def head_selectors():
        row = lax.broadcasted_iota(jnp.int32, (KV_W, HEAD_DIM), 0)
        col = lax.broadcasted_iota(jnp.int32, (KV_W, HEAD_DIM), 1)
        return [(row == col + i * HEAD_DIM).astype(BF16) for i in range(N_KV_HEADS)]

    def take_heads(group, sel):
        return jnp.concatenate([_dot(group, s) for s in sel], axis=0)

    def put_heads(x, sel):
        rows = x.shape[0] // len(sel)
        out = _dot(x[:rows].astype(BF16), sel[0], trans_b=True)
        for i in range(1, len(sel)):
            out = out + _dot(x[i * rows:(i + 1) * rows].astype(BF16), sel[i], trans_b=True)
        return out

    def rms(x, g):
        r = lax.rsqrt(jnp.mean(x * x, axis=-1, keepdims=True) + EPS)
        return r, x * r * g

    def rms_bwd(dn, xf, r, g):
        w = dn * g
        dx = r * w - xf * (r * r * r) * jnp.mean(xf * w, axis=-1, keepdims=True)
        return dx, jnp.sum(dn * (xf * r), axis=0, keepdims=True)

    def body(q_ref, kp_ref, kc_ref, vp_ref, vc_ref, do_ref, gq_ref, gk_ref, p_ref, ps_ref,
             dq_ref, dkv_ref, dbias_ref, dsink_ref, dgq_ref, dgk_ref):
        n = pl.program_id(0)
        sel = head_selectors()

        @pl.when(n == 0)
        def _():
            dbias_ref[...] = jnp.zeros_like(dbias_ref)
            dsink_ref[...] = jnp.zeros_like(dsink_ref)
            dgq_ref[...] = jnp.zeros_like(dgq_ref)
            dgk_ref[...] = jnp.zeros_like(dgk_ref)

        dgq_sum = jnp.zeros((1, HEAD_DIM), F32)
        dgk_sum = jnp.zeros((1, HEAD_DIM), F32)
        dk_rows, dv_rows = [[] for _ in range(per)], [[] for _ in range(per)]
        for h in range(N_KV_HEADS):
            kf_all = jnp.concatenate([_dot(kp_ref[...], sel[h]), _dot(kc_ref[...], sel[h])], axis=0)
            rk_all, kn_all = rms(kf_all, gk_ref[...])
            kn_all = kn_all.astype(BF16)
            v_all = jnp.concatenate([_dot(vp_ref[...], sel[h]), _dot(vc_ref[...], sel[h])], axis=0).astype(BF16)
            for sub in range(per):
                rows = pl.ds(sub * BLOCK, BLOCK)
                window = slice(sub * BLOCK, (sub + 2) * BLOCK)
                qf = take_heads(q_ref[rows, pl.ds(h * KV_W, KV_W)], sel)
                rq, qn = rms(qf, gq_ref[...])
                pn_bf16, psink = p_ref[sub, h], ps_ref[sub, h]
                pn = pn_bf16.astype(F32)
                do = take_heads(do_ref[rows, pl.ds(h * KV_W, KV_W)], sel).astype(BF16)
                dv_win = _dot(do, pn_bf16, trans_a=True).T
                dp = _dot(do, v_all[window], trans_b=True)
                delta = jnp.sum(pn * dp, axis=-1, keepdims=True)
                ds = pn * (dp - delta)
                dsc = (ds * scale).astype(BF16)
                dqn = _dot(dsc, kn_all[window])
                dkn = _dot(qn.astype(BF16), dsc, trans_a=True).T
                dq, dgq = rms_bwd(dqn, qf, rq, gq_ref[...])
                dk_win, dgk = rms_bwd(dkn, kf_all[window], rk_all[window], gk_ref[...])
                dq_ref[rows, pl.ds(h * KV_W, KV_W)] = put_heads(dq, sel).astype(BF16)
                dk_rows[sub] += [dk_win[:BLOCK], dk_win[BLOCK:]]
                dv_rows[sub] += [dv_win[:BLOCK], dv_win[BLOCK:]]
                dbias_ref[h] += ds
                dsink_ref[h] += jnp.sum((-psink * delta).reshape(GROUP, BLOCK, 1), axis=1)
                dgq_sum = dgq_sum + dgq
                dgk_sum = dgk_sum + dgk
        for sub in range(per):
            for part in range(2):
                dkv_ref[sub, part, :, pl.ds(0, KV_W)] = put_heads(
                    jnp.concatenate(dk_rows[sub][part::2], axis=0), sel).astype(BF16)
                dkv_ref[sub, part, :, pl.ds(KV_W, KV_W)] = put_heads(
                    jnp.concatenate(dv_rows[sub][part::2], axis=0), sel).astype(BF16)
        dgq_ref[...] += dgq_sum
        dgk_ref[...] += dgk_sum

    def kv_specs(col):
        return [pl.BlockSpec((BLOCK, KV_W), lambda n: (jnp.maximum(n * per - 1, 0), col // KV_W)),
                pl.BlockSpec((per * BLOCK, KV_W), lambda n: (n, col // KV_W))]

    row = pl.BlockSpec((per * BLOCK, Q_W), lambda n: (n, 0))
    return pl.pallas_call(
        body, name="attn_bwd", grid=(nb // per,),
        in_specs=[pl.BlockSpec((per * BLOCK, Q_W), lambda n: (n, COL_Q // Q_W))] + kv_specs(COL_K) + kv_specs(COL_V)
        + [row, gain, gain,
           pl.BlockSpec((per, N_KV_HEADS, QROWS, 2 * BLOCK), lambda n: (n, 0, 0, 0)),
           pl.BlockSpec((per, N_KV_HEADS, QROWS, 1), lambda n: (n, 0, 0, 0))],
        out_specs=[row, pl.BlockSpec((per, 2, BLOCK, 2 * KV_W), lambda n: (n, 0, 0, 0)), bspec,
                   pl.BlockSpec((N_KV_HEADS, GROUP, 1), lambda n: (0, 0, 0)), gain, gain],
        out_shape=[jax.ShapeDtypeStruct((t, Q_W), BF16),
                   jax.ShapeDtypeStruct((nb, 2, BLOCK, 2 * KV_W), BF16),
                   jax.ShapeDtypeStruct((N_KV_HEADS, QROWS, 2 * BLOCK), F32),
                   jax.ShapeDtypeStruct((N_KV_HEADS, GROUP, 1), F32),
                   jax.ShapeDtypeStruct((1, HEAD_DIM), F32),
                   jax.ShapeDtypeStruct((1, HEAD_DIM), F32)],
        compiler_params=_params(("arbitrary",)),
    )(p, p, p, p, p, do, gq, gk, probs, sink_probs)


def _kv_window_sum(parts):
    nb = parts.shape[0]

    def body(cur_ref, nxt_ref, o_ref):
        nxt = jnp.where(pl.program_id(0) < nb - 1, nxt_ref[...].astype(F32), 0.0)
        o_ref[...] = (cur_ref[...].astype(F32) + nxt).astype(BF16)

    blk = (None, None, BLOCK, 2 * KV_W)
    return pl.pallas_call(
        body, name="kv_window_sum", grid=(nb,),
        in_specs=[pl.BlockSpec(blk, lambda n: (n, 1, 0, 0)),
                  pl.BlockSpec(blk, lambda n: (jnp.minimum(n + 1, nb - 1), 0, 0, 0))],
        out_specs=pl.BlockSpec((BLOCK, 2 * KV_W), lambda n: (n, 0)),
        out_shape=jax.ShapeDtypeStruct((nb * BLOCK, 2 * KV_W), BF16),
        compiler_params=_params(("parallel",)),
    )(parts, parts)


GATE_TILE = 512


def _merge_fwd(z3, o, p, w_proj, w_o):
    t, d = z3.shape
    tm = min(ROW_TILE, t)
    tn = GATE_TILE

    def body(z_ref, o_ref, gc_ref, ga_ref, wp_ref, wo_ref, m_ref, a_ref, b_ref):
        a = _dot(z_ref[...], wp_ref[...])
        b = _dot(o_ref[...], wo_ref[...])
        m_ref[...] = (_sigmoid(gc_ref[...].astype(F32)) * a + _sigmoid(ga_ref[...].astype(F32)) * b).astype(BF16)
        a_ref[...] = a.astype(BF16)
        b_ref[...] = b.astype(BF16)

    row = pl.BlockSpec((tm, d), lambda i, j: (i, 0))
    wspec = pl.BlockSpec((d, tn), lambda i, j: (0, j))
    ospec = pl.BlockSpec((tm, tn), lambda i, j: (i, j))
    return pl.pallas_call(
        body, name="merge_fwd", grid=(t // tm, d // tn),
        in_specs=[row, row,
                  pl.BlockSpec((tm, tn), lambda i, j: (i, COL_GC // tn + j)),
                  pl.BlockSpec((tm, tn), lambda i, j: (i, COL_GA // tn + j)), wspec, wspec],
        out_specs=[ospec, ospec, ospec],
        out_shape=[jax.ShapeDtypeStruct((t, d), BF16)] * 3,
        compiler_params=_params(("parallel", "parallel")),
    )(z3, o, p, p, w_proj, w_o)


def _merge_bwd(dres, w_out, a, b, p, tokens=()):
    t, d = dres.shape
    tm = min(ROW_TILE, t)
    tn = GATE_TILE

    def epilogue(acc, ex, outs, ids):
        a_ref, b_ref, gc_ref, ga_ref = ex[:4]
        sc = _sigmoid(gc_ref[...].astype(F32))
        sa = _sigmoid(ga_ref[...].astype(F32))
        outs[0][...] = (acc * sc).astype(BF16)
        outs[1][...] = (acc * sa).astype(BF16)
        outs[2][0] = (acc * a_ref[...].astype(F32) * sc * (1.0 - sc)).astype(BF16)
        outs[2][1] = (acc * b_ref[...].astype(F32) * sa * (1.0 - sa)).astype(BF16)

    ospec = pl.BlockSpec((tm, tn), lambda i, j, kk: (i, j))
    return _mm("merge_bwd", (t // tm, d // tn, 1),
               dres, pl.BlockSpec((tm, d), lambda i, j, kk: (i, 0)),
               w_out, pl.BlockSpec((tn, d), lambda i, j, kk: (j, 0)), (tm, tn),
               trans_b=True, a_pre=_to_bf16,
               extras=(a, b, p, p),
               extra_specs=(ospec, ospec,
                            pl.BlockSpec((tm, tn), lambda i, j, kk: (i, COL_GC // tn + j)),
                            pl.BlockSpec((tm, tn), lambda i, j, kk: (i, COL_GA // tn + j))), tokens=tokens,
               out_shape=(jax.ShapeDtypeStruct((t, d), BF16), jax.ShapeDtypeStruct((t, d), BF16),
                          jax.ShapeDtypeStruct((2, t, d), BF16)),
               out_specs=(ospec, ospec, pl.BlockSpec((2, tm, tn), lambda i, j, kk: (0, i, j))),
               epilogue=epilogue)


def _store_epilogue(acc, ex, outs, ids):
    outs[0][...] = acc


def _store_bf16_epilogue(acc, ex, outs, ids):
    outs[0][...] = acc.astype(BF16)


def _mm_nt(name, a, w, out_dtype=BF16):
    t, n = a.shape
    k = w.shape[0]
    tm = min(ROW_TILE, t)
    return _mm(name, (t // tm, 1, 1), a, pl.BlockSpec((tm, n), lambda i, j, kk: (i, 0)),
               w, pl.BlockSpec((k, n), lambda i, j, kk: (0, 0)), (tm, k), trans_b=True,
               out_shape=(jax.ShapeDtypeStruct((t, k), out_dtype),),
               out_specs=(pl.BlockSpec((tm, k), lambda i, j, kk: (i, 0)),),
               epilogue=_store_bf16_epilogue if out_dtype == BF16 else _store_epilogue)[0]


def _mm_tn(name, a, b, b_pre=None, tokens=()):
    t, m = a.shape
    n = b.shape[1]
    tk = min(TOKEN_TILE, t)
    return _mm(name, (1, 1, t // tk), a, pl.BlockSpec((tk, m), lambda i, j, kk: (kk, 0)),
               b, pl.BlockSpec((tk, n), lambda i, j, kk: (kk, 0)), (m, n), trans_a=True, b_pre=b_pre, tokens=tokens,
               out_shape=(jax.ShapeDtypeStruct((m, n), BF16),),
               out_specs=(pl.BlockSpec((m, n), lambda i, j, kk: (0, 0)),), epilogue=_store_bf16_epilogue)[0]


def _local_step(x, target, small, comm):
    t = x.shape[0]
    w = dict(small)

    n1 = _rmsnorm_fwd("ffn1_norm", x, w["ffn1_norm"])
    w.update(comm.weights("A", n1))
    w["ffn1_w_in_t"] = _transpose_shards("ffn1_w_in_t", w["ffn1_w_in"])
    (x1, hm), ffn1_saved = _ffn_fwd("ffn1", x, n1, w["ffn1_w_in"], w["ffn1_w_out"], comm.tokens,
                                    next_gain=w["mix_norm"])
    w.update(comm.weights("B", x1))
    w["w_in_t"] = _transpose_shards("w_in_t", w["w_in"])
    tm = min(ROW_TILE, t)
    p = _mm("mix_in", (N_CHIPS, t // tm, 1),
            hm, pl.BlockSpec((tm, D_MODEL), lambda j, i, kk: (i, 0)),
            w["w_in"], pl.BlockSpec((None, D_MODEL, SHARD_W), lambda j, i, kk: (j, 0, 0)), (tm, SHARD_W),
            tokens=comm.tokens,
            out_shape=(jax.ShapeDtypeStruct((t, IN_W), BF16),),
            out_specs=(pl.BlockSpec((tm, SHARD_W), lambda j, i, kk: (i, j)),),
            epilogue=_store_bf16_epilogue)[0]

    z3, z1 = _conv_fwd(p, w["conv_dw_kernel"], w["conv_dw_bias"], w["conv_ln_g"], w["conv_ln_b"])

    onehot = _bucket_onehot()
    bias = _bias_table(w["rel_bias"].T, onehot).reshape(N_KV_HEADS, QROWS, 2 * BLOCK)
    sink_rows = jnp.repeat(w["attn_sinks"].reshape(N_KV_HEADS, GROUP), BLOCK, axis=1)[..., None]
    gq_wide = jnp.tile(w["q_norm"], (1, N_KV_HEADS))
    gk_wide = jnp.tile(w["k_norm"], (1, N_KV_HEADS))
    o, probs, sink_probs = _attn_fwd(p, gq_wide, gk_wide, sink_rows, bias)

    merged, a, b = _merge_fwd(z3, o, p, w["conv_w_proj"], w["attn_w_o"])
    x2, n2 = _mm_residual("mix_out", merged, w["w_out"], x1, 1.0, next_gain=w["ffn2_norm"])
    w.update(comm.weights("C", n2))
    w["ffn2_w_in_t"] = _transpose_shards("ffn2_w_in_t", w["ffn2_w_in"])
    (dy, loss), ffn2_saved = _ffn_fwd("ffn2", x2, n2, w["ffn2_w_in"], w["ffn2_w_out"], loss_target=target)

    g, big = {}, {}
    dres2, big["ffn2_w_in"], big["ffn2_w_out"], g["ffn2_norm"] = _ffn_bwd(
        "ffn2b", dy, x2, w["ffn2_norm"], ffn2_saved, w["ffn2_w_in_t"], w["ffn2_w_out"])
    tokens = comm.reduce_start("R1", big, behind=True)

    da, db, dgates = _merge_bwd(dres2, w["w_out"], a, b, p, tokens)
    tokens = comm.exchange_finish("R1", da)
    big = {}
    big["w_out"] = _mm_tn("d_w_out", merged, dres2, b_pre=_to_bf16, tokens=tokens)
    big["conv_w_proj"] = _mm_tn("d_w_proj", z3, da)
    big["attn_w_o"] = _mm_tn("d_w_o", o, db)
    dz3 = _mm_nt("d_z3", da, w["conv_w_proj"])
    do = _mm_nt("d_o", db, w["attn_w_o"])

    dq, dkv_parts, dbias, dsink, g["q_norm"], g["k_norm"] = _attn_bwd(
        p, do, w["q_norm"], w["k_norm"], probs, sink_probs)
    dkv = _kv_window_sum(dkv_parts)
    g["rel_bias"] = _bias_table_bwd(dbias.reshape(N_Q_HEADS, BLOCK * 2 * BLOCK), onehot).T
    g["attn_sinks"] = dsink.reshape(N_Q_HEADS)

    dz1, big["conv_dw_kernel"], g["conv_dw_bias"], g["conv_ln_g"], g["conv_ln_b"] = _conv_bwd_ln(
        p, z1, dz3, w["conv_ln_g"], w["conv_ln_b"])
    dp = _conv_bwd_glu(p, dz1, w["conv_dw_kernel"], dq, dkv, dgates)
    tk = min(TOKEN_TILE, t)
    big["w_in"] = _mm("d_w_in", (1, N_CHIPS, t // tk),
                    hm, pl.BlockSpec((tk, D_MODEL), lambda i, j, kk: (kk, 0)),
                    dp, pl.BlockSpec((tk, SHARD_W), lambda i, j, kk: (kk, j)), (D_MODEL, SHARD_W),
                    trans_a=True,
                    out_shape=(jax.ShapeDtypeStruct((N_CHIPS, D_MODEL, SHARD_W), BF16),),
                    out_specs=(pl.BlockSpec((None, D_MODEL, SHARD_W), lambda i, j, kk: (j, 0, 0)),),
                    epilogue=_store_bf16_epilogue)[0]
    tn = min(NORM_GRAD_TILE, t)
    dres1, g["mix_norm"] = _norm_input_grad(
        "d_mix", dp, [pl.BlockSpec((tn, SHARD_W), functools.partial(lambda i, s: (i, s), s=s)) for s in range(N_CHIPS)],
        w["w_in_t"], x1, w["mix_norm"], dres2)

    tokens = comm.reduce_finish("R1", dres1, behind=True) + comm.reduce_start("R2", big, behind=True)

    def ffn1_first(du):
        comm.join_finish("R1", du)
        return comm.exchange_finish("R2", du)

    def ffn1_grads(dw_in4, dw_out):
        late = comm.reduce_finish("R2", dw_in4, behind=True)
        return late + comm.reduce_start("R3", {"ffn1_w_in": dw_in4, "ffn1_w_out": dw_out})

    grad_x, _, _, g["ffn1_norm"] = _ffn_bwd(
        "ffn1b", dres1, x, w["ffn1_norm"], ffn1_saved, w["ffn1_w_in_t"], w["ffn1_w_out"], tokens, ffn1_first, ffn1_grads)
    comm.join_finish("R2", grad_x)
    comm.reduce_finish("R3", grad_x)
    return loss[0, 0], grad_x, g


def _mesh_place():
    x, y, c = lax.axis_index("x"), lax.axis_index("y"), lax.axis_index("c")
    chips = [(1 - x, y), (x, 1 - y), (1 - x, 1 - y)]
    return x, y, c, chips


def _any_specs(n):
    return [pl.BlockSpec(memory_space=pl.ANY)] * n


HBM_SPEC = pl.BlockSpec(memory_space=pltpu.HBM)
SEM_SPEC = pl.BlockSpec(memory_space=pltpu.SEMAPHORE)
EFFECT = pltpu.SideEffectType.DATAFLOW_SIDE_EFFECTING


def _in_hbm(a):
    return pltpu.with_memory_space_constraint(a, pltpu.HBM)


def _copy_start(name, srcs, lands, plan, after=()):
    ns, nb = len(srcs), len(lands)
    n = plan.copies_per_source * ns

    def body(*refs):
        s_refs, l_refs = refs[:ns], refs[ns:ns + nb]
        send_sems, recv_sems = refs[ns + nb + len(after)], refs[ns + nb + len(after) + 1]
        token = refs[-1]
        for k, (src, dst, to, _) in enumerate(plan(s_refs, l_refs)):
            pltpu.make_async_remote_copy(src_ref=src, dst_ref=dst, send_sem=send_sems.at[k], recv_sem=recv_sems.at[k],
                                         device_id=to, device_id_type=MESH).start()
        token[...] = jnp.zeros_like(token)

    bufs = list(srcs) + list(lands)
    outs = pl.pallas_call(
        body, name=name,
        out_shape=(pltpu.SemaphoreType.DMA((n,)), pltpu.SemaphoreType.DMA((n,)),
                   *[pltpu.HBM(a.shape, a.dtype) for a in bufs], jax.ShapeDtypeStruct((8, LANES), F32)),
        in_specs=[HBM_SPEC] * len(bufs) + [pl.BlockSpec(memory_space=pl.ANY)] * len(after),
        out_specs=(SEM_SPEC, SEM_SPEC, *[HBM_SPEC] * len(bufs), pl.BlockSpec(memory_space=pltpu.VMEM)),
        input_output_aliases={i: 2 + i for i in range(len(bufs))},
        compiler_params=pltpu.CompilerParams(has_side_effects=EFFECT),
    )(*[_in_hbm(a) for a in bufs], *after)
    return outs[0], outs[1], list(outs[2:2 + ns]), list(outs[2 + ns:2 + ns + nb]), outs[-1]


def _copy_wait(name, send_sems, recv_sems, srcs, lands, after, plan):
    ns, nb = len(srcs), len(lands)
    after = tuple(after) if isinstance(after, (tuple, list)) else (after,)

    def body(*refs):
        s_refs, l_refs = refs[:ns], refs[ns:ns + nb]
        send_sems, recv_sems = refs[ns + nb], refs[ns + nb + 1]
        for k, (src, _, to, mine) in enumerate(plan(s_refs, l_refs)):
            cp = pltpu.make_async_remote_copy(src_ref=src, dst_ref=mine, send_sem=send_sems.at[k], recv_sem=recv_sems.at[k],
                                              device_id=to, device_id_type=MESH)
            cp.wait_send()
            cp.wait_recv()

    bufs = list(srcs) + list(lands)
    outs = pl.pallas_call(
        body, name=name,
        out_shape=tuple(pltpu.HBM(a.shape, a.dtype) for a in bufs),
        in_specs=[HBM_SPEC] * len(bufs) + [SEM_SPEC, SEM_SPEC] + [pl.BlockSpec(memory_space=pl.ANY)] * len(after),
        out_specs=tuple([HBM_SPEC] * len(bufs)),
        input_output_aliases={i: i for i in range(len(bufs))},
        compiler_params=pltpu.CompilerParams(has_side_effects=EFFECT),
    )(*bufs, send_sems, recv_sems, *after)
    return list(outs[:ns]), list(outs[ns:])


def _gather_plan(s_refs, l_refs):
    x, y, c, chips = _mesh_place()
    jme = 2 * x + y
    return [(s.at[c], land.at[jme, c], (*chip, c), land.at[2 * chip[0] + chip[1], c])
            for s, land in zip(s_refs, l_refs) for chip in chips]


_gather_plan.copies_per_source = 3


def _gather_both_cores_plan(s_refs, l_refs):
    x, y, c, chips = _mesh_place()
    jme = 2 * x + y
    plan = []
    for s, land in zip(s_refs, l_refs):
        for chip in chips:
            for peer_core in (c, 1 - c):
                plan.append((s.at[c], land.at[jme, c], (*chip, peer_core), land.at[2 * chip[0] + chip[1], peer_core]))
        plan.append((s, land.at[jme], (x, y, 1 - c), land.at[jme]))
    return plan


_gather_both_cores_plan.copies_per_source = 7


def _scatter_plan(s_refs, l_refs):
    x, y, c, chips = _mesh_place()
    return [(s.at[2 * chip[0] + chip[1]], land.at[k], (*chip, c), land.at[k])
            for s, land in zip(s_refs, l_refs) for k, chip in enumerate(chips)]


_scatter_plan.copies_per_source = 3


def _exchange_plan(g_refs, l_refs):
    x, y, c, _ = _mesh_place()
    return [(g.at[:, 1 - c], land, (x, y, 1 - c), land) for g, land in zip(g_refs, l_refs)]


_exchange_plan.copies_per_source = 1


def _join_plan(h_refs, l_refs):
    x, y, c, _ = _mesh_place()
    return [(h.at[c], h.at[c], (x, y, 1 - c), h.at[1 - c]) for h in h_refs]


_join_plan.copies_per_source = 1


def _gather_forward(name, shards, landed):
    nw = len(shards)

    def body(*refs):
        s_refs, o_refs = refs[:nw], refs[2 * nw:3 * nw]
        send_sems, recv_sems = refs[3 * nw:]
        x, y, c, chips = _mesh_place()
        me, sib, jme = (x, y, c), (x, y, 1 - c), 2 * x + y
        sent = []
        for w in range(nw):
            parts = [(o_refs[w].at[2 * chip[0] + chip[1], c], o_refs[w].at[2 * chip[0] + chip[1], c]) for chip in chips]
            parts.append((s_refs[w], o_refs[w].at[jme]))
            for k, (src, dst) in enumerate(parts):
                cp = pltpu.make_async_remote_copy(src_ref=src, dst_ref=dst, send_sem=send_sems.at[4 * w + k],
                                                  recv_sem=recv_sems.at[4 * w + k], device_id=sib, device_id_type=MESH)
                cp.start()
                sent.append(cp)
        for w in range(nw):
            parts = [o_refs[w].at[2 * chip[0] + chip[1], 1 - c] for chip in chips] + [o_refs[w].at[jme]]
            for k, part in enumerate(parts):
                pltpu.make_async_remote_copy(src_ref=part, dst_ref=part, send_sem=send_sems.at[4 * w + k],
                                             recv_sem=recv_sems.at[4 * w + k], device_id=me, device_id_type=MESH).wait_recv()
        for cp in sent:
            cp.wait_send()

    return pl.pallas_call(
        body, name=name,
        in_specs=_any_specs(2 * nw), out_specs=_any_specs(nw),
        out_shape=[jax.ShapeDtypeStruct(a.shape, a.dtype) for a in landed],
        input_output_aliases={nw + i: i for i in range(nw)},
        scratch_shapes=[pltpu.SemaphoreType.DMA((4 * nw,)), pltpu.SemaphoreType.DMA((4 * nw,))],
    )(*shards, *landed)


def _exchange_halves(name, grads, after=()):
    nw = len(grads)

    def body(*refs):
        g_refs, o_refs = refs[:nw], refs[nw + len(after):2 * nw + len(after)]
        send_sems, recv_sems = refs[2 * nw + len(after):]
        x, y, c, _ = _mesh_place()
        copies = []
        for w in range(nw):
            cp = pltpu.make_async_remote_copy(src_ref=g_refs[w].at[:, 1 - c], dst_ref=o_refs[w], send_sem=send_sems.at[w],
                                              recv_sem=recv_sems.at[w], device_id=(x, y, 1 - c), device_id_type=MESH)
            cp.start()
            copies.append(cp)
        for cp in copies:
            cp.wait()

    return pl.pallas_call(
        body, name=name,
        in_specs=_any_specs(nw + len(after)), out_specs=_any_specs(nw),
        out_shape=[jax.ShapeDtypeStruct((N_CHIPS,) + g.shape[2:], g.dtype) for g in grads],
        scratch_shapes=[pltpu.SemaphoreType.DMA((nw,)), pltpu.SemaphoreType.DMA((nw,))],
    )(*grads, *after)


ELEMENTWISE_ROWS = 512


def _row_tile(r):
    for cand in range(min(r, ELEMENTWISE_ROWS) // 16 * 16, 0, -16):
        if r % cand == 0:
            return cand
    return r


def _add_own_half(c_idx, grad, got):
    _, _, r, cols = grad.shape
    tr = _row_tile(r)

    def body(c_ref, g_ref, o_ref, out_ref):
        out_ref[...] = (g_ref[...].astype(F32) + o_ref[...].astype(F32)).astype(BF16)

    return pl.pallas_call(
        body, name="add_own_half",
        grid_spec=pltpu.PrefetchScalarGridSpec(
            num_scalar_prefetch=1, grid=(N_CHIPS, r // tr),
            in_specs=[pl.BlockSpec((None, None, tr, cols), lambda j, i, c_ref: (j, c_ref[0], i, 0)),
                      pl.BlockSpec((None, tr, cols), lambda j, i, c_ref: (j, i, 0))],
            out_specs=pl.BlockSpec((None, tr, cols), lambda j, i, c_ref: (j, i, 0))),
        out_shape=jax.ShapeDtypeStruct((N_CHIPS, r, cols), BF16),
        compiler_params=_params(("parallel", "parallel")),
    )(c_idx, grad, got)


def _sum_pieces(place_idx, sums, landed):
    _, r, cols = sums.shape
    tr = _row_tile(r)

    def body(j_ref, own_ref, p_ref, o_ref):
        o_ref[...] = ((own_ref[...].astype(F32) + p_ref[0].astype(F32)) + p_ref[1].astype(F32)) + p_ref[2].astype(F32)

    return pl.pallas_call(
        body, name="sum_pieces",
        grid_spec=pltpu.PrefetchScalarGridSpec(
            num_scalar_prefetch=1, grid=(r // tr,),
            in_specs=[pl.BlockSpec((None, tr, cols), lambda i, j_ref: (j_ref[0], i, 0)),
                      pl.BlockSpec((N_CHIPS - 1, tr, cols), lambda i, j_ref: (0, i, 0))],
            out_specs=pl.BlockSpec((None, tr, cols), lambda i, j_ref: (j_ref[1], i, 0))),
        out_shape=jax.ShapeDtypeStruct((2, r, cols), F32),
        compiler_params=_params(("parallel",)),
    )(place_idx, sums, landed)


def _join_halves(name, halves):
    nw = len(halves)

    def body(*refs):
        o_refs = refs[nw:2 * nw]
        send_sems, recv_sems = refs[2 * nw:]
        x, y, c, _ = _mesh_place()
        copies = []
        for w in range(nw):
            cp = pltpu.make_async_remote_copy(src_ref=o_refs[w].at[c], dst_ref=o_refs[w].at[c], send_sem=send_sems.at[w],
                                              recv_sem=recv_sems.at[w], device_id=(x, y, 1 - c), device_id_type=MESH)
            cp.start()
            copies.append(cp)
        for w in range(nw):
            copies[w].wait_send()
            landed = o_refs[w].at[1 - c]
            pltpu.make_async_remote_copy(src_ref=landed, dst_ref=landed, send_sem=send_sems.at[w], recv_sem=recv_sems.at[w],
                                         device_id=(x, y, c), device_id_type=MESH).wait_recv()

    return pl.pallas_call(
        body, name=name,
        in_specs=_any_specs(nw), out_specs=_any_specs(nw),
        out_shape=[jax.ShapeDtypeStruct(h.shape, F32) for h in halves],
        input_output_aliases={i: i for i in range(nw)},
        scratch_shapes=[pltpu.SemaphoreType.DMA((nw,)), pltpu.SemaphoreType.DMA((nw,))],
    )(*halves)


SMALL_ROWS = 8


def _all_reduce_small(pack):
    rows, cols = pack.shape
    n_dev = 8

    def body(p_ref, o_ref, slots, send_sems, recv_sems):
        x, y, c, _ = _mesh_place()
        me = 4 * x + 2 * y + c
        slots[me] = p_ref[...]
        copies = []
        for k in range(1, n_dev):
            peer = (me + k) % n_dev
            cp = pltpu.make_async_remote_copy(src_ref=p_ref, dst_ref=slots.at[me], send_sem=send_sems.at[k],
                                              recv_sem=recv_sems.at[k],
                                              device_id=(peer // 4, (peer // 2) % 2, peer % 2), device_id_type=MESH)
            cp.start()
            copies.append(cp)
        for k in range(1, n_dev):
            src = (me + n_dev - k) % n_dev
            pltpu.make_async_remote_copy(src_ref=p_ref, dst_ref=slots.at[src], send_sem=send_sems.at[k],
                                         recv_sem=recv_sems.at[k], device_id=(x, y, c), device_id_type=MESH).wait_recv()
        for cp in copies:
            cp.wait_send()
        total = slots[0]
        for s in range(1, n_dev):
            total = total + slots[s]
        o_ref[...] = total

    return pl.pallas_call(
        body, name="all_reduce_small",
        in_specs=[pl.BlockSpec(memory_space=pltpu.VMEM)], out_specs=pl.BlockSpec(memory_space=pltpu.VMEM),
        out_shape=jax.ShapeDtypeStruct((rows, cols), F32),
        scratch_shapes=[pltpu.VMEM((n_dev, rows, cols), F32), pltpu.SemaphoreType.DMA((n_dev,)),
                        pltpu.SemaphoreType.DMA((n_dev,))],
    )(pack)


def _adamw(name, w, g, m, v):
    r, cols = w.shape
    tr = _row_tile(r)

    def body(w_ref, g_ref, m_ref, v_ref, d_ref, nm_ref, nv_ref):
        gv = g_ref[...]
        nm = ADAM_B1 * m_ref[...] + (1.0 - ADAM_B1) * gv
        nv = ADAM_B2 * v_ref[...] + (1.0 - ADAM_B2) * (gv * gv)
        m_hat = nm / (1.0 - ADAM_B1 ** ADAM_STEP)
        v_hat = nv / (1.0 - ADAM_B2 ** ADAM_STEP)
        d_ref[...] = -ADAM_LR * (m_hat / (jnp.sqrt(v_hat) + ADAM_EPS) + ADAM_WD * w_ref[...])
        nm_ref[...] = nm
        nv_ref[...] = nv

    spec = pl.BlockSpec((tr, cols), lambda i: (i, 0))
    return pl.pallas_call(
        body, name=name, grid=(r // tr,),
        in_specs=[spec] * 4, out_specs=[spec] * 3,
        out_shape=[jax.ShapeDtypeStruct((r, cols), F32)] * 3,
        compiler_params=_params(("parallel",)),
    )(w, g, m, v)


BIG = ["ffn1_w_in", "ffn1_w_out", "w_in", "conv_w_proj", "attn_w_o", "w_out", "ffn2_w_in", "ffn2_w_out", "conv_dw_kernel"]
COL_SHARDED = ("ffn1_w_in", "w_in", "ffn2_w_in")
SMALL = ["ffn1_norm", "mix_norm", "ffn2_norm", "conv_dw_bias", "conv_ln_g", "conv_ln_b", "q_norm", "k_norm", "attn_sinks", "rel_bias"]
WEIGHTS = ["ffn1_norm", "ffn1_w_in", "ffn1_w_out", "mix_norm", "w_in", "conv_dw_kernel", "conv_dw_bias", "conv_ln_g",
           "conv_ln_b", "conv_w_proj", "q_norm", "k_norm", "attn_sinks", "rel_bias", "attn_w_o", "w_out", "ffn2_norm",
           "ffn2_w_in", "ffn2_w_out"]
SMALL_PLACE = {"ffn1_norm": (0, 0, 1024), "mix_norm": (1, 0, 1024), "ffn2_norm": (2, 0, 1024), "conv_dw_bias": (3, 0, 1024),
               "conv_ln_g": (4, 0, 1024), "conv_ln_b": (5, 0, 1024), "q_norm": (6, 0, 64), "k_norm": (6, 128, 64),
               "attn_sinks": (6, 256, 16), "rel_bias": (7, 0, 512)}
LOSS_PLACE = (6, 384)


def _pack_small(vals, fill=0.0, loss=None):
    pack = jnp.full((SMALL_ROWS, D_MODEL), fill, F32)
    for name, (row, lane, n) in SMALL_PLACE.items():
        pack = pack.at[row, lane:lane + n].set(vals[name].reshape(n))
    if loss is not None:
        pack = pack.at[LOSS_PLACE[0], LOSS_PLACE[1]].set(loss)
    return pack


def _unpack_small(pack, shapes):
    return {name: pack[row, lane:lane + n].reshape(shapes[name]) for name, (row, lane, n) in SMALL_PLACE.items()}


def _shard_halves(name, a):
    if name == "conv_dw_kernel":
        a = jnp.pad(a, ((0, CONV_PAD - CONV_WIDTH), (0, 0)))
    r, cols = a.shape
    return a.reshape(2, r // 2, cols)


GATHER_GROUPS = {"A": ["ffn1_w_in", "ffn1_w_out"],
                 "B": ["w_in", "conv_dw_kernel", "conv_w_proj", "attn_w_o", "w_out"],
                 "C": ["ffn2_w_in", "ffn2_w_out"]}


class _MeshComm:
    def __init__(self, wts):
        self.c_idx = lax.axis_index("c").astype(jnp.int32).reshape(1)
        self.place_idx = jnp.stack([2 * lax.axis_index("x") + lax.axis_index("y"), lax.axis_index("c")]).astype(jnp.int32)
        self.gathers, self.exchanges, self.reductions, self.joins, self.reduced = {}, {}, {}, {}, {}
        self.tokens = ()
        self.shards = {n: _shard_halves(n, wts[n]) if n == "conv_dw_kernel" else _shard_halves(n, wts[n]).astype(BF16)
                       for n in BIG}
        self._gather_start("A", ())

    def _gather_start(self, group, after):
        shards = [self.shards[n] for n in GATHER_GROUPS[group]]
        lands = [lax.empty((N_CHIPS,) + s.shape, s.dtype) for s in shards]
        self.gathers[group] = _copy_start("gather_start_" + group, shards, lands, self._gather_plan(group), after=after)
        self.tokens = (self.gathers[group][-1],)

    @staticmethod
    def _gather_plan(group):
        return _gather_both_cores_plan if group == "C" else _gather_plan

    def weights(self, group, after):
        send_sems, recv_sems, shards, lands, token = self.gathers.pop(group)
        after = [token if after is None else after]
        if group == "A":
            after += [self.shards[n] for g in ("B", "C") for n in GATHER_GROUPS[g]]
        shards, lands = _copy_wait("gather_wait_" + group, send_sems, recv_sems, shards, lands, after,
                                   self._gather_plan(group))
        gathered = lands if group == "C" else _gather_forward("gather_forward_" + group, shards, lands)
        self.tokens = ()
        following = {"A": "B", "B": "C"}.get(group)
        if following:
            self._gather_start(following, (gathered[0],))
        out = {}
        for n, g4 in zip(GATHER_GROUPS[group], gathered):
            r, cols = g4.shape[2] * 2, g4.shape[3]
            if n in COL_SHARDED:
                out[n] = g4.reshape(N_CHIPS, r, cols)
            elif n == "conv_dw_kernel":
                out[n] = g4.reshape(N_CHIPS, r, cols).transpose(1, 0, 2).reshape(r, N_CHIPS * cols)
            else:
                out[n] = g4.reshape(N_CHIPS * r, cols)
        return out

    def reduce_start(self, group, grads, behind=False):
        names = list(grads)
        g4 = []
        for n in names:
            a = grads[n]
            if n == "conv_dw_kernel":
                a = a.reshape(CONV_PAD, N_CHIPS, -1).transpose(1, 0, 2)
            elif n not in COL_SHARDED:
                a = a.reshape(N_CHIPS, a.shape[0] // N_CHIPS, a.shape[1])
            g4.append(a.reshape(N_CHIPS, 2, a.shape[1] // 2, a.shape[2]))
        if behind:
            lands = [lax.empty((N_CHIPS,) + g.shape[2:], g.dtype) for g in g4]
            started = _copy_start("exchange_start_" + group, g4, lands, _exchange_plan)
            self.exchanges[group] = (names,) + started
            return (started[-1],)
        return self._scatter_start(group, names, g4, _exchange_halves("exchange_halves_" + group, g4))

    def exchange_finish(self, group, after):
        names, send_sems, recv_sems, g4, lands, _ = self.exchanges.pop(group)
        g4, got = _copy_wait("exchange_wait_" + group, send_sems, recv_sems, g4, lands, after, _exchange_plan)
        return self._scatter_start(group, names, g4, got)

    def _scatter_start(self, group, names, g4, got):
        sums = [_add_own_half(self.c_idx, a, b) for a, b in zip(g4, got)]
        lands = [lax.empty((N_CHIPS - 1,) + s.shape[1:], s.dtype) for s in sums]
        started = _copy_start("scatter_start_" + group, sums, lands, _scatter_plan)
        self.reductions[group] = (names,) + started
        return (started[-1],)

    def reduce_finish(self, group, after, behind=False):
        names, send_sems, recv_sems, sums, lands, _ = self.reductions.pop(group)
        sums, lands = _copy_wait("scatter_wait_" + group, send_sems, recv_sems, sums, lands, after, _scatter_plan)
        halves = [_sum_pieces(self.place_idx, s, p) for s, p in zip(sums, lands)]
        if behind:
            started = _copy_start("join_start_" + group, halves, [], _join_plan)
            self.joins[group] = (names,) + started
            return (started[-1],)
        self.reduced.update(zip(names, _join_halves("join_halves_" + group, halves)))
        return ()

    def join_finish(self, group, after):
        names, send_sems, recv_sems, halves, _, _ = self.joins.pop(group)
        self.reduced.update(zip(names, _copy_wait("join_wait_" + group, send_sems, recv_sems, halves, [], after, _join_plan)[0]))


def kernel(x, ffn1_norm, ffn1_w_in, ffn1_w_out, mix_norm, w_in, conv_dw_kernel, conv_dw_bias, conv_ln_g, conv_ln_b, conv_w_proj, q_norm, k_norm, attn_sinks, rel_bias, attn_w_o, w_out, ffn2_norm, ffn2_w_in, ffn2_w_out, loss_target, m_ffn1_norm, m_ffn1_w_in, m_ffn1_w_out, m_mix_norm, m_w_in, m_conv_dw_kernel, m_conv_dw_bias, m_conv_ln_g, m_conv_ln_b, m_conv_w_proj, m_q_norm, m_k_norm, m_attn_sinks, m_rel_bias, m_attn_w_o, m_w_out, m_ffn2_norm, m_ffn2_w_in, m_ffn2_w_out, v_ffn1_norm, v_ffn1_w_in, v_ffn1_w_out, v_mix_norm, v_w_in, v_conv_dw_kernel, v_conv_dw_bias, v_conv_ln_g, v_conv_ln_b, v_conv_w_proj, v_q_norm, v_k_norm, v_attn_sinks, v_rel_bias, v_attn_w_o, v_w_out, v_ffn2_norm, v_ffn2_w_in, v_ffn2_w_out):
    args = dict(locals())
    wts = {n: args[n] for n in WEIGHTS}
    mom = {n: args["m_" + n] for n in WEIGHTS}
    var = {n: args["v_" + n] for n in WEIGHTS}
    comm = _MeshComm(wts)
    small = {n: wts[n] if n in ("attn_sinks", "rel_bias") else wts[n].reshape(1, -1) for n in SMALL}
    loss_part, grad_x, g = _local_step(x[0], loss_target[0], small, comm)

    small_sum = _all_reduce_small(_pack_small(g, loss=loss_part))
    loss = small_sum[LOSS_PLACE[0], LOSS_PLACE[1]]
    small_shapes = {n: wts[n].shape for n in SMALL}
    g_small = _unpack_small(small_sum, small_shapes)

    grads, delta, new_m, new_v = {}, {}, {}, {}
    for n in BIG:
        j = comm.reduced[n]
        gs = j.reshape(j.shape[1] * 2, j.shape[2])
        pad = n == "conv_dw_kernel"
        ws, ms, vs = (_shard_halves(n, a).reshape(gs.shape) for a in (wts[n], mom[n], var[n]))
        d, nm, nv = _adamw("adamw_" + n, ws, gs, ms, vs)
        cut = (lambda a: a[:CONV_WIDTH]) if pad else (lambda a: a)
        grads[n], delta[n], new_m[n], new_v[n] = cut(gs), cut(d), cut(nm), cut(nv)
    d, nm, nv = _adamw("adamw_small", _pack_small(wts), small_sum, _pack_small(mom), _pack_small(var, fill=1.0))
    grads.update(g_small)
    delta.update(_unpack_small(d, small_shapes))
    new_m.update(_unpack_small(nm, small_shapes))
    new_v.update(_unpack_small(nv, small_shapes))

    return (loss, grad_x[None], *[grads[n] for n in WEIGHTS], *[delta[n] for n in WEIGHTS],
            *[new_m[n] for n in WEIGHTS], *[new_v[n] for n in WEIGHTS])
```

```python
import functools
import math

import jax
import jax.numpy as jnp
from jax import lax
from jax.experimental import pallas as pl
from jax.experimental.pallas import tpu as pltpu

F32 = jnp.float32
BF16 = jnp.bfloat16
MESH = pl.DeviceIdType.MESH

EPS = 1e-6
D_MODEL = 1024
D_FF = 2816
N_CHIPS = 4
SHARD_W = 2 * D_FF // N_CHIPS
HEAD_DIM = 64
N_Q_HEADS = 16
N_KV_HEADS = 4
GROUP = N_Q_HEADS // N_KV_HEADS
BLOCK = 128
QROWS = GROUP * BLOCK
N_BUCKETS = 32
MAX_DISTANCE = 128
CONV_WIDTH = 31
CONV_PAD = 32
NEG = float(jnp.finfo(jnp.float32).min)

ADAM_LR = 0.001
ADAM_B1 = 0.9
ADAM_B2 = 0.999
ADAM_EPS = 1e-08
ADAM_WD = 0.01
ADAM_STEP = 10

VMEM_LIMIT_BYTES = 56 * 1024 * 1024
ROW_TILE = 1024
TOKEN_TILE = 1024
CONV_TILE = 256
CONV_ROWS = 128
LANES = 128

COL_CONV_A, COL_CONV_G, COL_Q, COL_K, COL_V, COL_GC, COL_GA = 0, 1024, 2048, 3072, 3328, 3584, 4608
IN_W = 5632


def _params(sem, vmem=VMEM_LIMIT_BYTES):
    return pltpu.CompilerParams(dimension_semantics=sem, vmem_limit_bytes=vmem)


def _sigmoid(x):
    return 1.0 / (1.0 + jnp.exp(-x))


def _dot(a, b, trans_a=False, trans_b=False, precision=None):
    dn = (((0,) if trans_a else (1,), (1,) if trans_b else (0,)), ((), ()))
    return lax.dot_general(a, b, dn, preferred_element_type=F32, precision=precision)


def _mm(name, grid, a, a_spec, b, b_spec, acc_shape, *, trans_a=False, trans_b=False, a_pre=None, b_pre=None,
        extras=(), extra_specs=(), tokens=(), out_shape, out_specs, epilogue,
        sem=("parallel", "parallel", "arbitrary")):
    n_k = grid[2]
    extras = tuple(extras) + tuple(tokens)
    extra_specs = tuple(extra_specs) + (pl.BlockSpec((8, LANES), lambda i, j, kk: (0, 0)),) * len(tokens)
    n_extra = len(extras)
    n_out = len(out_shape)

    def body(a_ref, b_ref, *rest):
        ex = rest[:n_extra]
        outs = rest[n_extra:n_extra + n_out]
        ids = (pl.program_id(0), pl.program_id(1), pl.program_id(2))
        av = a_ref[...]
        if a_pre is not None:
            av = a_pre(av)
        bv = b_ref[...]
        if b_pre is not None:
            bv = b_pre(bv)
        if n_k == 1:
            epilogue(_dot(av, bv, trans_a, trans_b), ex, outs, ids)
        else:
            acc = rest[-1]

            @pl.when(ids[2] == 0)
            def _():
                acc[...] = jnp.zeros_like(acc)

            acc[...] += _dot(av, bv, trans_a, trans_b)

            @pl.when(ids[2] == n_k - 1)
            def _():
                epilogue(acc[...], ex, outs, ids)

    scratch = [] if n_k == 1 else [pltpu.VMEM(acc_shape, F32)]
    return pl.pallas_call(
        body, name=name, grid=grid,
        in_specs=[a_spec, b_spec, *extra_specs],
        out_specs=list(out_specs), out_shape=list(out_shape),
        scratch_shapes=scratch, compiler_params=_params(sem),
    )(a, b, *extras)


def _half_bf16(v):
    return (0.5 * v).astype(BF16)


def _to_bf16(v):
    return v.astype(BF16)


def _rmsnorm_fwd(name, x, g, tokens=()):
    t, d = x.shape
    tm = min(ROW_TILE, t)

    def body(x_ref, g_ref, *rest):
        o_ref = rest[-1]
        xv = x_ref[...]
        r = lax.rsqrt(jnp.mean(xv * xv, axis=-1, keepdims=True) + EPS)
        o_ref[...] = (xv * r * g_ref[...]).astype(BF16)

    return pl.pallas_call(
        body, name=name, grid=(t // tm,),
        in_specs=[pl.BlockSpec((tm, d), lambda i: (i, 0)), pl.BlockSpec((1, d), lambda i: (0, 0))]
        + [pl.BlockSpec((8, LANES), lambda i: (0, 0))] * len(tokens),
        out_specs=pl.BlockSpec((tm, d), lambda i: (i, 0)),
        out_shape=jax.ShapeDtypeStruct((t, d), BF16),
        compiler_params=_params(("parallel",)),
    )(x, g, *tokens)


def _ffn_in(name, n, w_in4, tokens=()):
    t, d = n.shape
    tm = min(ROW_TILE, t)

    def body(n_ref, wa_ref, wb_ref, *rest):
        ab_ref, h_ref = rest[-2:]
        nv = n_ref[...]
        a = _dot(nv, wa_ref[...])
        b = _dot(nv, wb_ref[...])
        h_ref[...] = (a * _sigmoid(a) * b).astype(BF16)
        ab_ref[0] = a.astype(BF16)
        ab_ref[1] = b.astype(BF16)

    return pl.pallas_call(
        body, name=name, grid=(2, t // tm),
        in_specs=[pl.BlockSpec((tm, d), lambda j, i: (i, 0)),
                  pl.BlockSpec((None, d, SHARD_W), lambda j, i: (j, 0, 0)),
                  pl.BlockSpec((None, d, SHARD_W), lambda j, i: (j + 2, 0, 0))]
        + [pl.BlockSpec((8, LANES), lambda j, i: (0, 0))] * len(tokens),
        out_specs=[pl.BlockSpec((2, tm, SHARD_W), lambda j, i: (0, i, j)),
                   pl.BlockSpec((tm, SHARD_W), lambda j, i: (i, j))],
        out_shape=[jax.ShapeDtypeStruct((2, t, D_FF), BF16), jax.ShapeDtypeStruct((t, D_FF), BF16)],
        compiler_params=_params(("parallel", "parallel")),
    )(n, w_in4, w_in4, *tokens)


def _mm_residual(name, a, w, res, scale, next_gain=None, loss_target=None):
    t, k = a.shape
    n = w.shape[1]
    tm = min(ROW_TILE, t)
    row = pl.BlockSpec((tm, n), lambda i, j, kk: (i, 0))
    extras, specs = [res], [row]
    shapes, out_specs = [jax.ShapeDtypeStruct((t, n), F32)], [row]
    if next_gain is not None:
        extras.append(next_gain)
        specs.append(pl.BlockSpec((1, n), lambda i, j, kk: (0, 0)))
        shapes.append(jax.ShapeDtypeStruct((t, n), BF16))
        out_specs.append(row)
    if loss_target is not None:
        extras.append(loss_target)
        specs.append(row)
        shapes.append(jax.ShapeDtypeStruct((8, LANES), F32))
        out_specs.append(pl.BlockSpec((8, LANES), lambda i, j, kk: (0, 0)))

    def epilogue(acc, ex, outs, ids):
        y = ex[0][...] + scale * acc
        if loss_target is None:
            outs[0][...] = y
        if next_gain is not None:
            r = lax.rsqrt(jnp.mean(y * y, axis=-1, keepdims=True) + EPS)
            outs[1][...] = (y * r * ex[1][...]).astype(BF16)
        if loss_target is not None:
            diff = y - ex[1][...]
            outs[0][...] = diff * (1.0 / n)
            part = jnp.full((8, LANES), 0.5 / n * jnp.sum(diff * diff), F32)

            @pl.when(ids[0] == 0)
            def _():
                outs[1][...] = part

            @pl.when(ids[0] > 0)
            def _():
                outs[1][...] += part

    sem = ("parallel" if loss_target is None else "arbitrary", "parallel", "arbitrary")
    out = _mm(name, (t // tm, 1, 1), a, pl.BlockSpec((tm, k), lambda i, j, kk: (i, 0)),
              w, pl.BlockSpec((k, n), lambda i, j, kk: (0, 0)), (tm, n),
              extras=extras, extra_specs=specs, out_shape=shapes, out_specs=out_specs, epilogue=epilogue, sem=sem)
    return out[0] if len(out) == 1 else tuple(out)


def _ffn_fwd(tag, x, n, w_in4, w_out, tokens=(), **tail):
    ab, h = _ffn_in(tag + "_in", n, w_in4, tokens)
    y = _mm_residual(tag + "_out", h, w_out, x, 0.5, **tail)
    return y, (n, ab, h)


def _ffn_bwd(tag, dres, x, g, saved, w_in4_t, w_out, tokens=(), on_first=None, on_weight_grads=None):
    n, ab, h = saved
    t, d = x.shape
    tm = min(ROW_TILE, t)
    tk = min(TOKEN_TILE, t)
    half_w = SHARD_W

    def dact_epilogue(acc, ex, outs, ids):
        a = ex[0][0].astype(F32)
        b = ex[0][1].astype(F32)
        sig = _sigmoid(a)
        outs[0][0] = (acc * b * (sig * (1.0 + a * (1.0 - sig)))).astype(BF16)
        outs[0][1] = (acc * (a * sig)).astype(BF16)

    du = _mm(tag + "_dact", (2, t // tm, 1),
             dres, pl.BlockSpec((tm, d), lambda j, i, kk: (i, 0)),
             w_out, pl.BlockSpec((half_w, d), lambda j, i, kk: (j, 0)), (tm, half_w),
             trans_b=True, a_pre=_half_bf16,
             extras=(ab,), extra_specs=(pl.BlockSpec((2, tm, half_w), lambda j, i, kk: (0, i, j)),), tokens=tokens,
             out_shape=(jax.ShapeDtypeStruct((2, t, D_FF), BF16),),
             out_specs=(pl.BlockSpec((2, tm, half_w), lambda j, i, kk: (0, i, j)),),
             epilogue=dact_epilogue)[0]

    def store_epilogue(acc, ex, outs, ids):
        outs[0][...] = acc.astype(BF16)

    early = () if on_first is None else on_first(du)

    dw_out = _mm(tag + "_dwout", (2, 1, t // tk),
                 h, pl.BlockSpec((tk, half_w), lambda i, j, kk: (kk, i)),
                 dres, pl.BlockSpec((tk, d), lambda i, j, kk: (kk, 0)), (half_w, d),
                 trans_a=True, b_pre=_half_bf16, tokens=early,
                 out_shape=(jax.ShapeDtypeStruct((D_FF, d), BF16),),
                 out_specs=(pl.BlockSpec((half_w, d), lambda i, j, kk: (i, 0)),),
                 epilogue=store_epilogue)[0]

    dw_in4 = _mm(tag + "_dwin", (1, N_CHIPS, t // tk),
                 n, pl.BlockSpec((tk, d), lambda i, j, kk: (kk, 0)),
                 du, pl.BlockSpec((None, tk, SHARD_W), lambda i, j, kk: (j // 2, kk, j % 2)), (d, SHARD_W),
                 trans_a=True,
                 out_shape=(jax.ShapeDtypeStruct((N_CHIPS, d, SHARD_W), BF16),),
                 out_specs=(pl.BlockSpec((None, d, SHARD_W), lambda i, j, kk: (j, 0, 0)),),
                 epilogue=store_epilogue)[0]

    late = () if on_weight_grads is None else on_weight_grads(dw_in4, dw_out)

    tn = min(NORM_GRAD_TILE, t)
    shard_specs = [pl.BlockSpec((None, tn, SHARD_W), functools.partial(lambda i, s: (s // 2, i, s % 2), s=s))
                   for s in range(N_CHIPS)]
    dx, dg = _norm_input_grad(tag + "_dn", du, shard_specs, w_in4_t, x, g, dres, late)
    return dx, dw_in4, dw_out, dg


NORM_GRAD_TILE = 512
NORM_GRAD_CHUNK = 256


def _transpose_shards(name, w4):
    n, r, c = w4.shape

    def body(w_ref, o_ref):
        o_ref[...] = w_ref[...].T

    return pl.pallas_call(
        body, name=name, grid=(n,),
        in_specs=[pl.BlockSpec((None, r, c), lambda s: (s, 0, 0))],
        out_specs=pl.BlockSpec((None, c, r), lambda s: (s, 0, 0)),
        out_shape=jax.ShapeDtypeStruct((n, c, r), w4.dtype),
        compiler_params=_params(("parallel",)),
    )(w4)


def _norm_input_grad(name, a, shard_specs, w4_t, x, g, dres, tokens=()):
    t, d = x.shape
    tn = min(NORM_GRAD_TILE, t)
    chunk = min(NORM_GRAD_CHUNK, tn)
    ns = len(shard_specs)

    def body(*refs):
        a_refs, (wt_ref, x_ref, g_ref, dres_ref) = refs[:ns], refs[ns:ns + 4]
        out_ref, dg_ref = refs[-2:]
        part = jnp.zeros((1, d), F32)
        for r0 in range(0, tn, chunk):
            rows = pl.ds(r0, chunk)
            acc = _dot(a_refs[0][rows, :], wt_ref[0])
            for s in range(1, ns):
                acc = acc + _dot(a_refs[s][rows, :], wt_ref[s])
            xv = x_ref[rows, :]
            r = lax.rsqrt(jnp.mean(xv * xv, axis=-1, keepdims=True) + EPS)
            w = acc * g_ref[...]
            out_ref[rows, :] = dres_ref[rows, :] + (r * w - xv * (r * r * r) * jnp.mean(xv * w, axis=-1, keepdims=True))
            part = part + jnp.sum(acc * (xv * r), axis=0, keepdims=True)
        i = pl.program_id(0)

        @pl.when(i == 0)
        def _():
            dg_ref[...] = part

        @pl.when(i > 0)
        def _():
            dg_ref[...] += part

    row = pl.BlockSpec((tn, d), lambda i: (i, 0))
    vec = pl.BlockSpec((1, d), lambda i: (0, 0))
    return pl.pallas_call(
        body, name=name, grid=(t // tn,),
        in_specs=list(shard_specs) + [pl.BlockSpec(w4_t.shape, lambda i: (0, 0, 0)), row, vec, row]
        + [pl.BlockSpec((8, LANES), lambda i: (0, 0))] * len(tokens),
        out_specs=[row, vec],
        out_shape=[jax.ShapeDtypeStruct((t, d), F32), jax.ShapeDtypeStruct((1, d), F32)],
        compiler_params=_params(("arbitrary",)),
    )(*[a] * ns, w4_t, x, g, dres, *tokens)


def _conv_fill(zp_ref, a_ref, g_ref, ah_ref, gh_ref, i):
    zh = ah_ref[...].astype(F32) * _sigmoid(gh_ref[...].astype(F32))
    zp_ref[pl.ds(0, CONV_PAD), :] = jnp.where(i > 0, zh, 0.0)
    zp_ref[pl.ds(CONV_PAD, a_ref.shape[0]), :] = a_ref[...].astype(F32) * _sigmoid(g_ref[...].astype(F32))


def _shift_groups(shifts):
    groups = {}
    for j, s in shifts:
        groups.setdefault(s % 8, []).append((j, s // 8))
    return groups


def _windows(zp_ref, r0, lanes, groups):
    for q, taps in groups.items():
        deepest = max(p for _, p in taps)
        win = zp_ref[pl.ds(r0 + q, 8 * deepest + CONV_ROWS), lanes]
        for j, p in taps:
            yield j, win[8 * p:8 * p + CONV_ROWS]


def _conv_apply(zp_ref, out_ref, dw_ref, bias_ref, tm, ch, shifts):
    groups = _shift_groups(shifts)
    for cc in range(ch // LANES):
        lanes = pl.ds(cc * LANES, LANES)
        w = [dw_ref[pl.ds(j, 1), lanes] for j in range(CONV_WIDTH)]
        for r0 in range(0, tm, CONV_ROWS):
            if bias_ref is None:
                acc = jnp.zeros((CONV_ROWS, LANES), F32)
            else:
                acc = jnp.broadcast_to(bias_ref[:, lanes], (CONV_ROWS, LANES))
            for j, rows in _windows(zp_ref, r0, lanes, groups):
                acc = acc + w[j] * rows
            out_ref[pl.ds(r0, CONV_ROWS), lanes] = acc


FWD_SHIFTS = [(j, CONV_PAD - (CONV_WIDTH - 1) + j) for j in range(CONV_WIDTH)]
BWD_SHIFTS = [(j, CONV_WIDTH - 1 - j) for j in range(CONV_WIDTH)]


def _conv_taps(zp_ref, z1_ref, dw_ref, bias_ref, tm, ch):
    _conv_apply(zp_ref, z1_ref, dw_ref, bias_ref, tm, ch, FWD_SHIFTS)


def _conv_specs(tm, ch):
    per = tm // CONV_PAD
    cb = COL_CONV_G // ch
    return [pl.BlockSpec((tm, ch), lambda i: (i, 0)),
            pl.BlockSpec((tm, ch), lambda i: (i, cb)),
            pl.BlockSpec((CONV_PAD, ch), lambda i: (jnp.maximum(i * per - 1, 0), 0)),
            pl.BlockSpec((CONV_PAD, ch), lambda i: (jnp.maximum(i * per - 1, 0), cb))]


def _conv_fwd(p, dw, bias, ln_g, ln_b):
    t = p.shape[0]
    ch = D_MODEL
    tm = min(CONV_TILE, t)

    def body(a_ref, g_ref, ah_ref, gh_ref, dw_ref, bias_ref, lg_ref, lb_ref, o_ref, z1_ref, zp_ref):
        i = pl.program_id(0)
        _conv_fill(zp_ref, a_ref, g_ref, ah_ref, gh_ref, i)
        _conv_taps(zp_ref, z1_ref, dw_ref, bias_ref, tm, ch)
        z1 = z1_ref[...]
        mu = jnp.mean(z1, axis=-1, keepdims=True)
        zc = z1 - mu
        rs = lax.rsqrt(jnp.mean(zc * zc, axis=-1, keepdims=True) + EPS)
        z2 = zc * rs * lg_ref[...] + lb_ref[...]
        o_ref[...] = (z2 * _sigmoid(z2)).astype(BF16)

    vec = pl.BlockSpec((1, ch), lambda i: (0, 0))
    return pl.pallas_call(
        body, name="conv_fwd", grid=(t // tm,),
        in_specs=_conv_specs(tm, ch) + [pl.BlockSpec((CONV_PAD, ch), lambda i: (0, 0)), vec, vec, vec],
        out_specs=[pl.BlockSpec((tm, ch), lambda i: (i, 0)), pl.BlockSpec((tm, ch), lambda i: (i, 0))],
        out_shape=[jax.ShapeDtypeStruct((t, ch), BF16), jax.ShapeDtypeStruct((t, ch), F32)],
        scratch_shapes=[pltpu.VMEM((CONV_PAD + tm, ch), F32)],
        compiler_params=_params(("parallel",)),
    )(p, p, p, p, dw, bias, ln_g, ln_b)


def _conv_bwd_ln(p, z1_saved, dz3, ln_g, ln_b):
    t = p.shape[0]
    ch = D_MODEL
    tm = min(CONV_TILE, t)

    def body(a_ref, g_ref, ah_ref, gh_ref, z1_ref, dz3_ref, lg_ref, lb_ref,
             dz1_ref, ddw_ref, dbias_ref, dlg_ref, dlb_ref, zp_ref):
        i = pl.program_id(0)
        _conv_fill(zp_ref, a_ref, g_ref, ah_ref, gh_ref, i)
        z1 = z1_ref[...]
        mu = jnp.mean(z1, axis=-1, keepdims=True)
        zc = z1 - mu
        rs = lax.rsqrt(jnp.mean(zc * zc, axis=-1, keepdims=True) + EPS)
        xh = zc * rs
        z2 = xh * lg_ref[...] + lb_ref[...]
        sig = _sigmoid(z2)
        dz2 = dz3_ref[...].astype(F32) * (sig * (1.0 + z2 * (1.0 - sig)))
        dxh = dz2 * lg_ref[...]
        dz1 = rs * (dxh - jnp.mean(dxh, axis=-1, keepdims=True) - xh * jnp.mean(dxh * xh, axis=-1, keepdims=True))
        dz1_ref[...] = dz1

        @pl.when(i == 0)
        def _():
            ddw_ref[...] = jnp.zeros_like(ddw_ref)
            dbias_ref[...] = jnp.zeros_like(dbias_ref)
            dlg_ref[...] = jnp.zeros_like(dlg_ref)
            dlb_ref[...] = jnp.zeros_like(dlb_ref)

        dlg_ref[...] += jnp.sum(dz2 * xh, axis=0, keepdims=True)
        dlb_ref[...] += jnp.sum(dz2, axis=0, keepdims=True)
        dbias_ref[...] += jnp.sum(dz1, axis=0, keepdims=True)
        groups = _shift_groups(FWD_SHIFTS)
        for cc in range(ch // LANES):
            lanes = pl.ds(cc * LANES, LANES)
            accs = [jnp.zeros((8, LANES), F32) for _ in range(CONV_WIDTH)]
            for r0 in range(0, tm, CONV_ROWS):
                dzc = dz1_ref[pl.ds(r0, CONV_ROWS), lanes]
                for j, rows in _windows(zp_ref, r0, lanes, groups):
                    accs[j] = accs[j] + jnp.sum((dzc * rows).reshape(CONV_ROWS // 8, 8, LANES), axis=0)
            for j in range(CONV_WIDTH):
                ddw_ref[pl.ds(j, 1), lanes] += jnp.sum(accs[j], axis=0, keepdims=True)

    vec = pl.BlockSpec((1, ch), lambda i: (0, 0))
    return pl.pallas_call(
        body, name="conv_bwd_ln", grid=(t // tm,),
        in_specs=_conv_specs(tm, ch) + [pl.BlockSpec((tm, ch), lambda i: (i, 0)),
                                        pl.BlockSpec((tm, ch), lambda i: (i, 0)), vec, vec],
        out_specs=[pl.BlockSpec((tm, ch), lambda i: (i, 0)), pl.BlockSpec((CONV_PAD, ch), lambda i: (0, 0)), vec, vec, vec],
        out_shape=[jax.ShapeDtypeStruct((t, ch), F32), jax.ShapeDtypeStruct((CONV_PAD, ch), F32)]
        + [jax.ShapeDtypeStruct((1, ch), F32)] * 3,
        scratch_shapes=[pltpu.VMEM((CONV_PAD + tm, ch), F32)],
        compiler_params=_params(("arbitrary",)),
    )(p, p, p, p, z1_saved, dz3, ln_g, ln_b)


def _conv_bwd_glu(p, dz1, dw, dq, dkv, dgates):
    t = p.shape[0]
    ch = D_MODEL
    tm = min(CONV_TILE, t)
    per = tm // CONV_PAD
    n_halo = t // CONV_PAD
    cb = COL_CONV_G // ch

    def body(a_ref, g_ref, dz_ref, dzn_ref, dw_ref, dq_ref, dkv_ref, dgates_ref, o_ref, zp_ref, z0_ref):
        i = pl.program_id(0)
        o_ref[:, pl.ds(COL_Q, Q_W)] = dq_ref[...]
        o_ref[:, pl.ds(COL_K, 2 * KV_W)] = dkv_ref[...]
        o_ref[:, pl.ds(COL_GC, ch)] = dgates_ref[0]
        o_ref[:, pl.ds(COL_GA, ch)] = dgates_ref[1]
        zp_ref[pl.ds(0, tm), :] = dz_ref[...]
        zp_ref[pl.ds(tm, CONV_PAD), :] = jnp.where(i < t // tm - 1, dzn_ref[...], 0.0)
        _conv_apply(zp_ref, z0_ref, dw_ref, None, tm, ch, BWD_SHIFTS)
        dz0 = z0_ref[...]
        a = a_ref[...].astype(F32)
        sig = _sigmoid(g_ref[...].astype(F32))
        o_ref[:, pl.ds(0, ch)] = (dz0 * sig).astype(BF16)
        o_ref[:, pl.ds(ch, ch)] = (dz0 * a * sig * (1.0 - sig)).astype(BF16)

    return pl.pallas_call(
        body, name="conv_bwd_glu", grid=(t // tm,),
        in_specs=[pl.BlockSpec((tm, ch), lambda i: (i, 0)), pl.BlockSpec((tm, ch), lambda i: (i, cb)),
                  pl.BlockSpec((tm, ch), lambda i: (i, 0)),
                  pl.BlockSpec((CONV_PAD, ch), lambda i: (jnp.minimum((i + 1) * per, n_halo - 1), 0)),
                  pl.BlockSpec((CONV_PAD, ch), lambda i: (0, 0)),
                  pl.BlockSpec((tm, Q_W), lambda i: (i, 0)), pl.BlockSpec((tm, 2 * KV_W), lambda i: (i, 0)),
                  pl.BlockSpec((2, tm, ch), lambda i: (0, i, 0))],
        out_specs=pl.BlockSpec((tm, IN_W), lambda i: (i, 0)),
        out_shape=jax.ShapeDtypeStruct((t, IN_W), BF16),
        scratch_shapes=[pltpu.VMEM((tm + CONV_PAD, ch), F32), pltpu.VMEM((tm, ch), F32)],
        compiler_params=_params(("parallel",)),
    )(p, p, dz1, dz1, dw, dq, dkv, dgates)


def _bucket_onehot():
    qi = jnp.arange(BLOCK, dtype=jnp.int32)[:, None]
    kj = jnp.arange(2 * BLOCK, dtype=jnp.int32)[None, :]
    dist = jnp.maximum(qi + BLOCK - kj, 0)
    max_exact = N_BUCKETS // 2
    dflt = jnp.maximum(dist, 1).astype(F32)
    large = max_exact + (jnp.log(dflt / max_exact) / math.log(MAX_DISTANCE / max_exact)
                         * (N_BUCKETS - max_exact)).astype(jnp.int32)
    large = jnp.minimum(large, N_BUCKETS - 1)
    bucket = jnp.where(dist < max_exact, dist, large)
    onehot = bucket[None] == jnp.arange(N_BUCKETS, dtype=jnp.int32)[:, None, None]
    return onehot.astype(F32).reshape(N_BUCKETS, BLOCK * 2 * BLOCK)


def _bias_table(rel_bias_t, onehot):
    n = onehot.shape[1]
    tn = 4096

    def body(r_ref, oh_ref, o_ref):
        flat = pl.program_id(0) * tn + lax.broadcasted_iota(jnp.int32, (N_Q_HEADS, tn), 1)
        dist = (flat // (2 * BLOCK)) + BLOCK - (flat % (2 * BLOCK))
        bias = _dot(r_ref[...], oh_ref[...], precision=lax.Precision.HIGHEST)
        o_ref[...] = jnp.where((dist >= 0) & (dist < BLOCK), bias, NEG)

    return pl.pallas_call(
        body, name="bias_table", grid=(n // tn,),
        in_specs=[pl.BlockSpec((N_Q_HEADS, N_BUCKETS), lambda i: (0, 0)), pl.BlockSpec((N_BUCKETS, tn), lambda i: (0, i))],
        out_specs=pl.BlockSpec((N_Q_HEADS, tn), lambda i: (0, i)),
        out_shape=jax.ShapeDtypeStruct((N_Q_HEADS, n), F32),
        compiler_params=_params(("parallel",)),
    )(rel_bias_t, onehot)


def _bias_table_bwd(dbias, onehot):
    n = onehot.shape[1]
    tn = 4096

    def body(d_ref, oh_ref, o_ref):
        part = _dot(d_ref[...], oh_ref[...], trans_b=True, precision=lax.Precision.HIGHEST)
        i = pl.program_id(0)

        @pl.when(i == 0)
        def _():
            o_ref[...] = part

        @pl.when(i > 0)
        def _():
            o_ref[...] += part

    return pl.pallas_call(
        body, name="bias_table_bwd", grid=(n // tn,),
        in_specs=[pl.BlockSpec((N_Q_HEADS, tn), lambda i: (0, i)), pl.BlockSpec((N_BUCKETS, tn), lambda i: (0, i))],
        out_specs=pl.BlockSpec((N_Q_HEADS, N_BUCKETS), lambda i: (0, 0)),
        out_shape=jax.ShapeDtypeStruct((N_Q_HEADS, N_BUCKETS), F32),
        compiler_params=_params(("arbitrary",)),
    )(dbias, onehot)


def _lane_head(rows):
    return lax.broadcasted_iota(jnp.int32, (rows, KV_W), 1) // HEAD_DIM


def _group_rms(x, gain_wide):
    head = _lane_head(x.shape[0])
    sq = x * x
    r = jnp.zeros_like(x)
    for i in range(N_KV_HEADS):
        ms = jnp.sum(jnp.where(head == i, sq, 0.0), axis=-1, keepdims=True) * (1.0 / HEAD_DIM)
        r = jnp.where(head == i, lax.rsqrt(ms + EPS), r)
    return r, x * r * gain_wide


def _stack_heads(group):
    head = _lane_head(group.shape[0])
    return jnp.concatenate([jnp.where(head == i, group, jnp.zeros_like(group)) for i in range(N_KV_HEADS)], axis=0)


def _unstack_heads(stacked):
    head = _lane_head(BLOCK)
    out = jnp.where(head == 0, stacked[:BLOCK], 0.0)
    for i in range(1, N_KV_HEADS):
        out = out + jnp.where(head == i, stacked[i * BLOCK:(i + 1) * BLOCK], 0.0)
    return out


def _repeaters():
    row = lax.broadcasted_iota(jnp.int32, (KV_W, KV_W), 0)
    col = lax.broadcasted_iota(jnp.int32, (KV_W, KV_W), 1)
    return [(row == h * HEAD_DIM + col % HEAD_DIM).astype(BF16) for h in range(N_KV_HEADS)]


def _attn_probs(q_stack, k_rep, sink, bias, before_start):
    s = _dot(q_stack, k_rep, trans_b=True) * (1.0 / math.sqrt(HEAD_DIM)) + bias
    s = jnp.where(before_start, NEG, s)
    m = jnp.maximum(jnp.max(s, axis=-1, keepdims=True), sink)
    p = jnp.exp(s - m)
    es = jnp.exp(sink - m)
    inv = 1.0 / (jnp.sum(p, axis=-1, keepdims=True) + es)
    return p * inv, es * inv


def _before_start(n):
    col = lax.broadcasted_iota(jnp.int32, (QROWS, 2 * BLOCK), 1)
    return (col < BLOCK) & (n == 0)


STEP_BLOCKS = 4
KV_W = N_KV_HEADS * HEAD_DIM
Q_W = N_Q_HEADS * HEAD_DIM


def _attn_specs():
    gain = pl.BlockSpec((1, KV_W), lambda n: (0, 0))
    sink = pl.BlockSpec((N_KV_HEADS, QROWS, 1), lambda n: (0, 0, 0))
    bias = pl.BlockSpec((N_KV_HEADS, QROWS, 2 * BLOCK), lambda n: (0, 0, 0))
    return gain, sink, bias


def _attn_fwd(p, gq, gk, sink_rows, bias):
    t = p.shape[0]
    nb = t // BLOCK
    per = STEP_BLOCKS if nb % STEP_BLOCKS == 0 else 1
    gain, sink, bspec = _attn_specs()

    def body(q_ref, kp_ref, kc_ref, vp_ref, vc_ref, gq_ref, gk_ref, sink_ref, bias_ref, o_ref, p_ref, ps_ref):
        first = pl.program_id(0) * per
        rep = _repeaters()
        kf = jnp.concatenate([kp_ref[...], kc_ref[...]], axis=0).astype(F32)
        kn = _group_rms(kf, gk_ref[...])[1].astype(BF16)
        v = jnp.concatenate([vp_ref[...], vc_ref[...]], axis=0)
        for h in range(N_KV_HEADS):
            k_rep = _dot(kn, rep[h]).astype(BF16)
            v_rep = _dot(v, rep[h]).astype(BF16)
            for sub in range(per):
                rows = pl.ds(sub * BLOCK, BLOCK)
                window = slice(sub * BLOCK, (sub + 2) * BLOCK)
                qn = _group_rms(q_ref[rows, pl.ds(h * KV_W, KV_W)].astype(F32), gq_ref[...])[1]
                pn, ps_ref[sub, h] = _attn_probs(_stack_heads(qn).astype(BF16), k_rep[window], sink_ref[h], bias_ref[h],
                                                 _before_start(first + sub))
                pn = pn.astype(BF16)
                p_ref[sub, h] = pn
                o_ref[rows, pl.ds(h * KV_W, KV_W)] = _unstack_heads(_dot(pn, v_rep[window])).astype(BF16)

    def kv_specs(col):
        return [pl.BlockSpec((BLOCK, KV_W), lambda n: (jnp.maximum(n * per - 1, 0), col // KV_W)),
                pl.BlockSpec((per * BLOCK, KV_W), lambda n: (n, col // KV_W))]

    return pl.pallas_call(
        body, name="attn_fwd", grid=(nb // per,),
        in_specs=[pl.BlockSpec((per * BLOCK, Q_W), lambda n: (n, COL_Q // Q_W))] + kv_specs(COL_K) + kv_specs(COL_V)
        + [gain, gain, sink, bspec],
        out_specs=[pl.BlockSpec((per * BLOCK, Q_W), lambda n: (n, 0)),
                   pl.BlockSpec((per, N_KV_HEADS, QROWS, 2 * BLOCK), lambda n: (n, 0, 0, 0)),
                   pl.BlockSpec((per, N_KV_HEADS, QROWS, 1), lambda n: (n, 0, 0, 0))],
        out_shape=[jax.ShapeDtypeStruct((t, Q_W), BF16),
                   jax.ShapeDtypeStruct((nb, N_KV_HEADS, QROWS, 2 * BLOCK), BF16),
                   jax.ShapeDtypeStruct((nb, N_KV_HEADS, QROWS, 1), F32)],
        compiler_params=_params(("parallel",)),
    )(p, p, p, p, p, gq, gk, sink_rows, bias)


def _attn_bwd(p, do, gq, gk, probs, sink_probs):
    t = p.shape[0]
    nb = t // BLOCK
    per = STEP_BLOCKS if nb % STEP_BLOCKS == 0 else 1
    bspec = _attn_specs()[2]
    gain = pl.BlockSpec((1, HEAD_DIM), lambda n: (0, 0))
    scale = 1.0 / math.sqrt(HEAD_DIM)

    def head_selectors():
        row = lax.broadcasted_iota(jnp.int32, (KV_W, HEAD_DIM), 0)
        col = lax.broadcasted_iota(jnp.int32, (KV_W, HEAD_DIM), 1)
        return [(row == col + i * HEAD_DIM).astype(BF16) for i in range(N_KV_HEADS)]

    def take_heads(group, sel):
        return jnp.concatenate([_dot(group, s) for s in sel], axis=0)

    def put_heads(x, sel):
        rows = x.shape[0] // len(sel)
        out = _dot(x[:rows].astype(BF16), sel[0], trans_b=True)
        for i in range(1, len(sel)):
            out = out + _dot(x[i * rows:(i + 1) * rows].astype(BF16), sel[i], trans_b=True)
        return out

    def rms(x, g):
        r = lax.rsqrt(jnp.mean(x * x, axis=-1, keepdims=True) + EPS)
        return r, x * r * g

    def rms_bwd(dn, xf, r, g):
        w = dn * g
        dx = r * w - xf * (r * r * r) * jnp.mean(xf * w, axis=-1, keepdims=True)
        return dx, jnp.sum(dn * (xf * r), axis=0, keepdims=True)

    def body(q_ref, kp_ref, kc_ref, vp_ref, vc_ref, do_ref, gq_ref, gk_ref, p_ref, ps_ref,
             dq_ref, dkv_ref, dbias_ref, dsink_ref, dgq_ref, dgk_ref):
        n = pl.program_id(0)
        sel = head_selectors()

        @pl.when(n == 0)
        def _():
            dbias_ref[...] = jnp.zeros_like(dbias_ref)
            dsink_ref[...] = jnp.zeros_like(dsink_ref)
            dgq_ref[...] = jnp.zeros_like(dgq_ref)
            dgk_ref[...] = jnp.zeros_like(dgk_ref)

        dgq_sum = jnp.zeros((1, HEAD_DIM), F32)
        dgk_sum = jnp.zeros((1, HEAD_DIM), F32)
        dk_rows, dv_rows = [[] for _ in range(per)], [[] for _ in range(per)]
        for h in range(N_KV_HEADS):
            kf_all = jnp.concatenate([_dot(kp_ref[...], sel[h]), _dot(kc_ref[...], sel[h])], axis=0)
            rk_all, kn_all = rms(kf_all, gk_ref[...])
            kn_all = kn_all.astype(BF16)
            v_all = jnp.concatenate([_dot(vp_ref[...], sel[h]), _dot(vc_ref[...], sel[h])], axis=0).astype(BF16)
            for sub in range(per):
                rows = pl.ds(sub * BLOCK, BLOCK)
                window = slice(sub * BLOCK, (sub + 2) * BLOCK)
                qf = take_heads(q_ref[rows, pl.ds(h * KV_W, KV_W)], sel)
                rq, qn = rms(qf, gq_ref[...])
                pn_bf16, psink = p_ref[sub, h], ps_ref[sub, h]
                pn = pn_bf16.astype(F32)
                do = take_heads(do_ref[rows, pl.ds(h * KV_W, KV_W)], sel).astype(BF16)
                dv_win = _dot(do, pn_bf16, trans_a=True).T
                dp = _dot(do, v_all[window], trans_b=True)
                delta = jnp.sum(pn * dp, axis=-1, keepdims=True)
                ds = pn * (dp - delta)
                dsc = (ds * scale).astype(BF16)
                dqn = _dot(dsc, kn_all[window])
                dkn = _dot(qn.astype(BF16), dsc, trans_a=True).T
                dq, dgq = rms_bwd(dqn, qf, rq, gq_ref[...])
                dk_win, dgk = rms_bwd(dkn, kf_all[window], rk_all[window], gk_ref[...])
                dq_ref[rows, pl.ds(h * KV_W, KV_W)] = put_heads(dq, sel).astype(BF16)
                dk_rows[sub] += [dk_win[:BLOCK], dk_win[BLOCK:]]
                dv_rows[sub] += [dv_win[:BLOCK], dv_win[BLOCK:]]
                dbias_ref[h] += ds
                dsink_ref[h] += jnp.sum((-psink * delta).reshape(GROUP, BLOCK, 1), axis=1)
                dgq_sum = dgq_sum + dgq
                dgk_sum = dgk_sum + dgk
        for sub in range(per):
            for part in range(2):
                dkv_ref[sub, part, :, pl.ds(0, KV_W)] = put_heads(
                    jnp.concatenate(dk_rows[sub][part::2], axis=0), sel).astype(BF16)
                dkv_ref[sub, part, :, pl.ds(KV_W, KV_W)] = put_heads(
                    jnp.concatenate(dv_rows[sub][part::2], axis=0), sel).astype(BF16)
        dgq_ref[...] += dgq_sum
        dgk_ref[...] += dgk_sum

    def kv_specs(col):
        return [pl.BlockSpec((BLOCK, KV_W), lambda n: (jnp.maximum(n * per - 1, 0), col // KV_W)),
                pl.BlockSpec((per * BLOCK, KV_W), lambda n: (n, col // KV_W))]

    row = pl.BlockSpec((per * BLOCK, Q_W), lambda n: (n, 0))
    return pl.pallas_call(
        body, name="attn_bwd", grid=(nb // per,),
        in_specs=[pl.BlockSpec((per * BLOCK, Q_W), lambda n: (n, COL_Q // Q_W))] + kv_specs(COL_K) + kv_specs(COL_V)
        + [row, gain, gain,
           pl.BlockSpec((per, N_KV_HEADS, QROWS, 2 * BLOCK), lambda n: (n, 0, 0, 0)),
           pl.BlockSpec((per, N_KV_HEADS, QROWS, 1), lambda n: (n, 0, 0, 0))],
        out_specs=[row, pl.BlockSpec((per, 2, BLOCK, 2 * KV_W), lambda n: (n, 0, 0, 0)), bspec,
                   pl.BlockSpec((N_KV_HEADS, GROUP, 1), lambda n: (0, 0, 0)), gain, gain],
        out_shape=[jax.ShapeDtypeStruct((t, Q_W), BF16),
                   jax.ShapeDtypeStruct((nb, 2, BLOCK, 2 * KV_W), BF16),
                   jax.ShapeDtypeStruct((N_KV_HEADS, QROWS, 2 * BLOCK), F32),
                   jax.ShapeDtypeStruct((N_KV_HEADS, GROUP, 1), F32),
                   jax.ShapeDtypeStruct((1, HEAD_DIM), F32),
                   jax.ShapeDtypeStruct((1, HEAD_DIM), F32)],
        compiler_params=_params(("arbitrary",)),
    )(p, p, p, p, p, do, gq, gk, probs, sink_probs)


def _kv_window_sum(parts):
    nb = parts.shape[0]

    def body(cur_ref, nxt_ref, o_ref):
        nxt = jnp.where(pl.program_id(0) < nb - 1, nxt_ref[...].astype(F32), 0.0)
        o_ref[...] = (cur_ref[...].astype(F32) + nxt).astype(BF16)

    blk = (None, None, BLOCK, 2 * KV_W)
    return pl.pallas_call(
        body, name="kv_window_sum", grid=(nb,),
        in_specs=[pl.BlockSpec(blk, lambda n: (n, 1, 0, 0)),
                  pl.BlockSpec(blk, lambda n: (jnp.minimum(n + 1, nb - 1), 0, 0, 0))],
        out_specs=pl.BlockSpec((BLOCK, 2 * KV_W), lambda n: (n, 0)),
        out_shape=jax.ShapeDtypeStruct((nb * BLOCK, 2 * KV_W), BF16),
        compiler_params=_params(("parallel",)),
    )(parts, parts)


GATE_TILE = 512


def _merge_fwd(z3, o, p, w_proj, w_o):
    t, d = z3.shape
    tm = min(ROW_TILE, t)
    tn = GATE_TILE

    def body(z_ref, o_ref, gc_ref, ga_ref, wp_ref, wo_ref, m_ref, a_ref, b_ref):
        a = _dot(z_ref[...], wp_ref[...])
        b = _dot(o_ref[...], wo_ref[...])
        m_ref[...] = (_sigmoid(gc_ref[...].astype(F32)) * a + _sigmoid(ga_ref[...].astype(F32)) * b).astype(BF16)
        a_ref[...] = a.astype(BF16)
        b_ref[...] = b.astype(BF16)

    row = pl.BlockSpec((tm, d), lambda i, j: (i, 0))
    wspec = pl.BlockSpec((d, tn), lambda i, j: (0, j))
    ospec = pl.BlockSpec((tm, tn), lambda i, j: (i, j))
    return pl.pallas_call(
        body, name="merge_fwd", grid=(t // tm, d // tn),
        in_specs=[row, row,
                  pl.BlockSpec((tm, tn), lambda i, j: (i, COL_GC // tn + j)),
                  pl.BlockSpec((tm, tn), lambda i, j: (i, COL_GA // tn + j)), wspec, wspec],
        out_specs=[ospec, ospec, ospec],
        out_shape=[jax.ShapeDtypeStruct((t, d), BF16)] * 3,
        compiler_params=_params(("parallel", "parallel")),
    )(z3, o, p, p, w_proj, w_o)


def _merge_bwd(dres, w_out, a, b, p, tokens=()):
    t, d = dres.shape
    tm = min(ROW_TILE, t)
    tn = GATE_TILE

    def epilogue(acc, ex, outs, ids):
        a_ref, b_ref, gc_ref, ga_ref = ex[:4]
        sc = _sigmoid(gc_ref[...].astype(F32))
        sa = _sigmoid(ga_ref[...].astype(F32))
        outs[0][...] = (acc * sc).astype(BF16)
        outs[1][...] = (acc * sa).astype(BF16)
        outs[2][0] = (acc * a_ref[...].astype(F32) * sc * (1.0 - sc)).astype(BF16)
        outs[2][1] = (acc * b_ref[...].astype(F32) * sa * (1.0 - sa)).astype(BF16)

    ospec = pl.BlockSpec((tm, tn), lambda i, j, kk: (i, j))
    return _mm("merge_bwd", (t // tm, d // tn, 1),
               dres, pl.BlockSpec((tm, d), lambda i, j, kk: (i, 0)),
               w_out, pl.BlockSpec((tn, d), lambda i, j, kk: (j, 0)), (tm, tn),
               trans_b=True, a_pre=_to_bf16,
               extras=(a, b, p, p),
               extra_specs=(ospec, ospec,
                            pl.BlockSpec((tm, tn), lambda i, j, kk: (i, COL_GC // tn + j)),
                            pl.BlockSpec((tm, tn), lambda i, j, kk: (i, COL_GA // tn + j))), tokens=tokens,
               out_shape=(jax.ShapeDtypeStruct((t, d), BF16), jax.ShapeDtypeStruct((t, d), BF16),
                          jax.ShapeDtypeStruct((2, t, d), BF16)),
               out_specs=(ospec, ospec, pl.BlockSpec((2, tm, tn), lambda i, j, kk: (0, i, j))),
               epilogue=epilogue)


def _store_epilogue(acc, ex, outs, ids):
    outs[0][...] = acc


def _store_bf16_epilogue(acc, ex, outs, ids):
    outs[0][...] = acc.astype(BF16)


def _mm_nt(name, a, w, out_dtype=BF16):
    t, n = a.shape
    k = w.shape[0]
    tm = min(ROW_TILE, t)
    return _mm(name, (t // tm, 1, 1), a, pl.BlockSpec((tm, n), lambda i, j, kk: (i, 0)),
               w, pl.BlockSpec((k, n), lambda i, j, kk: (0, 0)), (tm, k), trans_b=True,
               out_shape=(jax.ShapeDtypeStruct((t, k), out_dtype),),
               out_specs=(pl.BlockSpec((tm, k), lambda i, j, kk: (i, 0)),),
               epilogue=_store_bf16_epilogue if out_dtype == BF16 else _store_epilogue)[0]


def _mm_tn(name, a, b, b_pre=None, tokens=()):
    t, m = a.shape
    n = b.shape[1]
    tk = min(TOKEN_TILE, t)
    return _mm(name, (1, 1, t // tk), a, pl.BlockSpec((tk, m), lambda i, j, kk: (kk, 0)),
               b, pl.BlockSpec((tk, n), lambda i, j, kk: (kk, 0)), (m, n), trans_a=True, b_pre=b_pre, tokens=tokens,
               out_shape=(jax.ShapeDtypeStruct((m, n), BF16),),
               out_specs=(pl.BlockSpec((m, n), lambda i, j, kk: (0, 0)),), epilogue=_store_bf16_epilogue)[0]


def _local_step(x, target, small, comm):
    t = x.shape[0]
    w = dict(small)

    n1 = _rmsnorm_fwd("ffn1_norm", x, w["ffn1_norm"])
    onehot = _bucket_onehot()
    bias = _bias_table(w["rel_bias"].T, onehot).reshape(N_KV_HEADS, QROWS, 2 * BLOCK)
    sink_rows = jnp.repeat(w["attn_sinks"].reshape(N_KV_HEADS, GROUP), BLOCK, axis=1)[..., None]
    gq_wide = jnp.tile(w["q_norm"], (1, N_KV_HEADS))
    gk_wide = jnp.tile(w["k_norm"], (1, N_KV_HEADS))
    w.update(comm.weights("A", [n1, onehot, bias, sink_rows, gq_wide, gk_wide]))
    w["ffn1_w_in_t"] = _transpose_shards("ffn1_w_in_t", w["ffn1_w_in"])
    (x1, hm), ffn1_saved = _ffn_fwd("ffn1", x, n1, w["ffn1_w_in"], w["ffn1_w_out"], comm.tokens,
                                    next_gain=w["mix_norm"])
    w.update(comm.weights("B", x1))
    w["w_in_t"] = _transpose_shards("w_in_t", w["w_in"])
    tm = min(ROW_TILE, t)
    p = _mm("mix_in", (N_CHIPS, t // tm, 1),
            hm, pl.BlockSpec((tm, D_MODEL), lambda j, i, kk: (i, 0)),
            w["w_in"], pl.BlockSpec((None, D_MODEL, SHARD_W), lambda j, i, kk: (j, 0, 0)), (tm, SHARD_W),
            tokens=comm.tokens,
            out_shape=(jax.ShapeDtypeStruct((t, IN_W), BF16),),
            out_specs=(pl.BlockSpec((tm, SHARD_W), lambda j, i, kk: (i, j)),),
            epilogue=_store_bf16_epilogue)[0]

    z3, z1 = _conv_fwd(p, w["conv_dw_kernel"], w["conv_dw_bias"], w["conv_ln_g"], w["conv_ln_b"])

    o, probs, sink_probs = _attn_fwd(p, gq_wide, gk_wide, sink_rows, bias)

    merged, a, b = _merge_fwd(z3, o, p, w["conv_w_proj"], w["attn_w_o"])
    x2, n2 = _mm_residual("mix_out", merged, w["w_out"], x1, 1.0, next_gain=w["ffn2_norm"])
    w.update(comm.weights("C", n2))
    w["ffn2_w_in_t"] = _transpose_shards("ffn2_w_in_t", w["ffn2_w_in"])
    (dy, loss), ffn2_saved = _ffn_fwd("ffn2", x2, n2, w["ffn2_w_in"], w["ffn2_w_out"], loss_target=target)

    g, big = {}, {}
    dres2, big["ffn2_w_in"], big["ffn2_w_out"], g["ffn2_norm"] = _ffn_bwd(
        "ffn2b", dy, x2, w["ffn2_norm"], ffn2_saved, w["ffn2_w_in_t"], w["ffn2_w_out"])
    tokens = comm.reduce_start("R1", big, behind=True)

    da, db, dgates = _merge_bwd(dres2, w["w_out"], a, b, p, tokens)
    tokens = comm.exchange_finish("R1", da)
    big = {}
    big["w_out"] = _mm_tn("d_w_out", merged, dres2, b_pre=_to_bf16, tokens=tokens)
    big["conv_w_proj"] = _mm_tn("d_w_proj", z3, da)
    big["attn_w_o"] = _mm_tn("d_w_o", o, db)
    dz3 = _mm_nt("d_z3", da, w["conv_w_proj"])
    do = _mm_nt("d_o", db, w["attn_w_o"])

    dq, dkv_parts, dbias, dsink, g["q_norm"], g["k_norm"] = _attn_bwd(
        p, do, w["q_norm"], w["k_norm"], probs, sink_probs)
    dkv = _kv_window_sum(dkv_parts)
    g["rel_bias"] = _bias_table_bwd(dbias.reshape(N_Q_HEADS, BLOCK * 2 * BLOCK), onehot).T
    g["attn_sinks"] = dsink.reshape(N_Q_HEADS)

    dz1, big["conv_dw_kernel"], g["conv_dw_bias"], g["conv_ln_g"], g["conv_ln_b"] = _conv_bwd_ln(
        p, z1, dz3, w["conv_ln_g"], w["conv_ln_b"])
    dp = _conv_bwd_glu(p, dz1, w["conv_dw_kernel"], dq, dkv, dgates)
    tk = min(TOKEN_TILE, t)
    big["w_in"] = _mm("d_w_in", (1, N_CHIPS, t // tk),
                    hm, pl.BlockSpec((tk, D_MODEL), lambda i, j, kk: (kk, 0)),
                    dp, pl.BlockSpec((tk, SHARD_W), lambda i, j, kk: (kk, j)), (D_MODEL, SHARD_W),
                    trans_a=True,
                    out_shape=(jax.ShapeDtypeStruct((N_CHIPS, D_MODEL, SHARD_W), BF16),),
                    out_specs=(pl.BlockSpec((None, D_MODEL, SHARD_W), lambda i, j, kk: (j, 0, 0)),),
                    epilogue=_store_bf16_epilogue)[0]
    tn = min(NORM_GRAD_TILE, t)
    dres1, g["mix_norm"] = _norm_input_grad(
        "d_mix", dp, [pl.BlockSpec((tn, SHARD_W), functools.partial(lambda i, s: (i, s), s=s)) for s in range(N_CHIPS)],
        w["w_in_t"], x1, w["mix_norm"], dres2)

    tokens = comm.reduce_finish("R1", dres1, behind=True) + comm.reduce_start("R2", big, behind=True)

    def ffn1_first(du):
        comm.join_finish("R1", du)
        return comm.exchange_finish("R2", du)

    def ffn1_grads(dw_in4, dw_out):
        late = comm.reduce_finish("R2", dw_in4, behind=True)
        return late + comm.reduce_start("R3", {"ffn1_w_in": dw_in4, "ffn1_w_out": dw_out})

    grad_x, _, _, g["ffn1_norm"] = _ffn_bwd(
        "ffn1b", dres1, x, w["ffn1_norm"], ffn1_saved, w["ffn1_w_in_t"], w["ffn1_w_out"], tokens, ffn1_first, ffn1_grads)
    comm.join_finish("R2", grad_x)
    comm.reduce_finish("R3", grad_x)
    return loss[0, 0], grad_x, g


def _mesh_place():
    x, y, c = lax.axis_index("x"), lax.axis_index("y"), lax.axis_index("c")
    chips = [(1 - x, y), (x, 1 - y), (1 - x, 1 - y)]
    return x, y, c, chips


def _any_specs(n):
    return [pl.BlockSpec(memory_space=pl.ANY)] * n


HBM_SPEC = pl.BlockSpec(memory_space=pltpu.HBM)
SEM_SPEC = pl.BlockSpec(memory_space=pltpu.SEMAPHORE)
EFFECT = pltpu.SideEffectType.DATAFLOW_SIDE_EFFECTING


def _in_hbm(a):
    return pltpu.with_memory_space_constraint(a, pltpu.HBM)


def _copy_start(name, srcs, lands, plan, after=()):
    ns, nb = len(srcs), len(lands)
    n = plan.copies_per_source * ns

    def body(*refs):
        s_refs, l_refs = refs[:ns], refs[ns:ns + nb]
        send_sems, recv_sems = refs[ns + nb + len(after)], refs[ns + nb + len(after) + 1]
        token = refs[-1]
        for k, (src, dst, to, _) in enumerate(plan(s_refs, l_refs)):
            pltpu.make_async_remote_copy(src_ref=src, dst_ref=dst, send_sem=send_sems.at[k], recv_sem=recv_sems.at[k],
                                         device_id=to, device_id_type=MESH).start()
        token[...] = jnp.zeros_like(token)

    bufs = list(srcs) + list(lands)
    outs = pl.pallas_call(
        body, name=name,
        out_shape=(pltpu.SemaphoreType.DMA((n,)), pltpu.SemaphoreType.DMA((n,)),
                   *[pltpu.HBM(a.shape, a.dtype) for a in bufs], jax.ShapeDtypeStruct((8, LANES), F32)),
        in_specs=[HBM_SPEC] * len(bufs) + [pl.BlockSpec(memory_space=pl.ANY)] * len(after),
        out_specs=(SEM_SPEC, SEM_SPEC, *[HBM_SPEC] * len(bufs), pl.BlockSpec(memory_space=pltpu.VMEM)),
        input_output_aliases={i: 2 + i for i in range(len(bufs))},
        compiler_params=pltpu.CompilerParams(has_side_effects=EFFECT),
    )(*[_in_hbm(a) for a in bufs], *after)
    return outs[0], outs[1], list(outs[2:2 + ns]), list(outs[2 + ns:2 + ns + nb]), outs[-1]


def _copy_wait(name, send_sems, recv_sems, srcs, lands, after, plan):
    ns, nb = len(srcs), len(lands)
    after = tuple(after) if isinstance(after, (tuple, list)) else (after,)

    def body(*refs):
        s_refs, l_refs = refs[:ns], refs[ns:ns + nb]
        send_sems, recv_sems = refs[ns + nb], refs[ns + nb + 1]
        for k, (src, _, to, mine) in enumerate(plan(s_refs, l_refs)):
            cp = pltpu.make_async_remote_copy(src_ref=src, dst_ref=mine, send_sem=send_sems.at[k], recv_sem=recv_sems.at[k],
                                              device_id=to, device_id_type=MESH)
            cp.wait_send()
            cp.wait_recv()

    bufs = list(srcs) + list(lands)
    outs = pl.pallas_call(
        body, name=name,
        out_shape=tuple(pltpu.HBM(a.shape, a.dtype) for a in bufs),
        in_specs=[HBM_SPEC] * len(bufs) + [SEM_SPEC, SEM_SPEC] + [pl.BlockSpec(memory_space=pl.ANY)] * len(after),
        out_specs=tuple([HBM_SPEC] * len(bufs)),
        input_output_aliases={i: i for i in range(len(bufs))},
        compiler_params=pltpu.CompilerParams(has_side_effects=EFFECT),
    )(*bufs, send_sems, recv_sems, *after)
    return list(outs[:ns]), list(outs[ns:])


def _gather_plan(s_refs, l_refs):
    x, y, c, chips = _mesh_place()
    jme = 2 * x + y
    return [(s.at[c], land.at[jme, c], (*chip, c), land.at[2 * chip[0] + chip[1], c])
            for s, land in zip(s_refs, l_refs) for chip in chips]


_gather_plan.copies_per_source = 3


def _gather_both_cores_plan(s_refs, l_refs):
    x, y, c, chips = _mesh_place()
    jme = 2 * x + y
    plan = []
    for s, land in zip(s_refs, l_refs):
        for chip in chips:
            for peer_core in (c, 1 - c):
                plan.append((s.at[c], land.at[jme, c], (*chip, peer_core), land.at[2 * chip[0] + chip[1], peer_core]))
        plan.append((s, land.at[jme], (x, y, 1 - c), land.at[jme]))
    return plan


_gather_both_cores_plan.copies_per_source = 7


def _scatter_plan(s_refs, l_refs):
    x, y, c, chips = _mesh_place()
    return [(s.at[2 * chip[0] + chip[1]], land.at[k], (*chip, c), land.at[k])
            for s, land in zip(s_refs, l_refs) for k, chip in enumerate(chips)]


_scatter_plan.copies_per_source = 3


def _exchange_plan(g_refs, l_refs):
    x, y, c, _ = _mesh_place()
    return [(g.at[:, 1 - c], land, (x, y, 1 - c), land) for g, land in zip(g_refs, l_refs)]


_exchange_plan.copies_per_source = 1


def _join_plan(h_refs, l_refs):
    x, y, c, _ = _mesh_place()
    return [(h.at[c], h.at[c], (x, y, 1 - c), h.at[1 - c]) for h in h_refs]


_join_plan.copies_per_source = 1


def _gather_forward(name, shards, landed):
    nw = len(shards)

    def body(*refs):
        s_refs, o_refs = refs[:nw], refs[2 * nw:3 * nw]
        send_sems, recv_sems = refs[3 * nw:]
        x, y, c, chips = _mesh_place()
        me, sib, jme = (x, y, c), (x, y, 1 - c), 2 * x + y
        sent = []
        for w in range(nw):
            parts = [(o_refs[w].at[2 * chip[0] + chip[1], c], o_refs[w].at[2 * chip[0] + chip[1], c]) for chip in chips]
            parts.append((s_refs[w], o_refs[w].at[jme]))
            for k, (src, dst) in enumerate(parts):
                cp = pltpu.make_async_remote_copy(src_ref=src, dst_ref=dst, send_sem=send_sems.at[4 * w + k],
                                                  recv_sem=recv_sems.at[4 * w + k], device_id=sib, device_id_type=MESH)
                cp.start()
                sent.append(cp)
        for w in range(nw):
            parts = [o_refs[w].at[2 * chip[0] + chip[1], 1 - c] for chip in chips] + [o_refs[w].at[jme]]
            for k, part in enumerate(parts):
                pltpu.make_async_remote_copy(src_ref=part, dst_ref=part, send_sem=send_sems.at[4 * w + k],
                                             recv_sem=recv_sems.at[4 * w + k], device_id=me, device_id_type=MESH).wait_recv()
        for cp in sent:
            cp.wait_send()

    return pl.pallas_call(
        body, name=name,
        in_specs=_any_specs(2 * nw), out_specs=_any_specs(nw),
        out_shape=[jax.ShapeDtypeStruct(a.shape, a.dtype) for a in landed],
        input_output_aliases={nw + i: i for i in range(nw)},
        scratch_shapes=[pltpu.SemaphoreType.DMA((4 * nw,)), pltpu.SemaphoreType.DMA((4 * nw,))],
    )(*shards, *landed)


def _exchange_halves(name, grads, after=()):
    nw = len(grads)

    def body(*refs):
        g_refs, o_refs = refs[:nw], refs[nw + len(after):2 * nw + len(after)]
        send_sems, recv_sems = refs[2 * nw + len(after):]
        x, y, c, _ = _mesh_place()
        copies = []
        for w in range(nw):
            cp = pltpu.make_async_remote_copy(src_ref=g_refs[w].at[:, 1 - c], dst_ref=o_refs[w], send_sem=send_sems.at[w],
                                              recv_sem=recv_sems.at[w], device_id=(x, y, 1 - c), device_id_type=MESH)
            cp.start()
            copies.append(cp)
        for cp in copies:
            cp.wait()

    return pl.pallas_call(
        body, name=name,
        in_specs=_any_specs(nw + len(after)), out_specs=_any_specs(nw),
        out_shape=[jax.ShapeDtypeStruct((N_CHIPS,) + g.shape[2:], g.dtype) for g in grads],
        scratch_shapes=[pltpu.SemaphoreType.DMA((nw,)), pltpu.SemaphoreType.DMA((nw,))],
    )(*grads, *after)


ELEMENTWISE_ROWS = 512


def _row_tile(r):
    for cand in range(min(r, ELEMENTWISE_ROWS) // 16 * 16, 0, -16):
        if r % cand == 0:
            return cand
    return r


def _add_own_half(c_idx, grad, got):
    _, _, r, cols = grad.shape
    tr = _row_tile(r)

    def body(c_ref, g_ref, o_ref, out_ref):
        out_ref[...] = (g_ref[...].astype(F32) + o_ref[...].astype(F32)).astype(BF16)

    return pl.pallas_call(
        body, name="add_own_half",
        grid_spec=pltpu.PrefetchScalarGridSpec(
            num_scalar_prefetch=1, grid=(N_CHIPS, r // tr),
            in_specs=[pl.BlockSpec((None, None, tr, cols), lambda j, i, c_ref: (j, c_ref[0], i, 0)),
                      pl.BlockSpec((None, tr, cols), lambda j, i, c_ref: (j, i, 0))],
            out_specs=pl.BlockSpec((None, tr, cols), lambda j, i, c_ref: (j, i, 0))),
        out_shape=jax.ShapeDtypeStruct((N_CHIPS, r, cols), BF16),
        compiler_params=_params(("parallel", "parallel")),
    )(c_idx, grad, got)


def _sum_pieces(place_idx, sums, landed):
    _, r, cols = sums.shape
    tr = _row_tile(r)

    def body(j_ref, own_ref, p_ref, o_ref):
        o_ref[...] = ((own_ref[...].astype(F32) + p_ref[0].astype(F32)) + p_ref[1].astype(F32)) + p_ref[2].astype(F32)

    return pl.pallas_call(
        body, name="sum_pieces",
        grid_spec=pltpu.PrefetchScalarGridSpec(
            num_scalar_prefetch=1, grid=(r // tr,),
            in_specs=[pl.BlockSpec((None, tr, cols), lambda i, j_ref: (j_ref[0], i, 0)),
                      pl.BlockSpec((N_CHIPS - 1, tr, cols), lambda i, j_ref: (0, i, 0))],
            out_specs=pl.BlockSpec((None, tr, cols), lambda i, j_ref: (j_ref[1], i, 0))),
        out_shape=jax.ShapeDtypeStruct((2, r, cols), F32),
        compiler_params=_params(("parallel",)),
    )(place_idx, sums, landed)


def _join_halves(name, halves):
    nw = len(halves)

    def body(*refs):
        o_refs = refs[nw:2 * nw]
        send_sems, recv_sems = refs[2 * nw:]
        x, y, c, _ = _mesh_place()
        copies = []
        for w in range(nw):
            cp = pltpu.make_async_remote_copy(src_ref=o_refs[w].at[c], dst_ref=o_refs[w].at[c], send_sem=send_sems.at[w],
                                              recv_sem=recv_sems.at[w], device_id=(x, y, 1 - c), device_id_type=MESH)
            cp.start()
            copies.append(cp)
        for w in range(nw):
            copies[w].wait_send()
            landed = o_refs[w].at[1 - c]
            pltpu.make_async_remote_copy(src_ref=landed, dst_ref=landed, send_sem=send_sems.at[w], recv_sem=recv_sems.at[w],
                                         device_id=(x, y, c), device_id_type=MESH).wait_recv()

    return pl.pallas_call(
        body, name=name,
        in_specs=_any_specs(nw), out_specs=_any_specs(nw),
        out_shape=[jax.ShapeDtypeStruct(h.shape, F32) for h in halves],
        input_output_aliases={i: i for i in range(nw)},
        scratch_shapes=[pltpu.SemaphoreType.DMA((nw,)), pltpu.SemaphoreType.DMA((nw,))],
    )(*halves)


SMALL_ROWS = 8


def _all_reduce_small(pack):
    rows, cols = pack.shape
    n_dev = 8

    def body(p_ref, o_ref, slots, send_sems, recv_sems):
        x, y, c, _ = _mesh_place()
        me = 4 * x + 2 * y + c
        slots[me] = p_ref[...]
        copies = []
        for k in range(1, n_dev):
            peer = (me + k) % n_dev
            cp = pltpu.make_async_remote_copy(src_ref=p_ref, dst_ref=slots.at[me], send_sem=send_sems.at[k],
                                              recv_sem=recv_sems.at[k],
                                              device_id=(peer // 4, (peer // 2) % 2, peer % 2), device_id_type=MESH)
            cp.start()
            copies.append(cp)
        for k in range(1, n_dev):
            src = (me + n_dev - k) % n_dev
            pltpu.make_async_remote_copy(src_ref=p_ref, dst_ref=slots.at[src], send_sem=send_sems.at[k],
                                         recv_sem=recv_sems.at[k], device_id=(x, y, c), device_id_type=MESH).wait_recv()
        for cp in copies:
            cp.wait_send()
        total = slots[0]
        for s in range(1, n_dev):
            total = total + slots[s]
        o_ref[...] = total

    return pl.pallas_call(
        body, name="all_reduce_small",
        in_specs=[pl.BlockSpec(memory_space=pltpu.VMEM)], out_specs=pl.BlockSpec(memory_space=pltpu.VMEM),
        out_shape=jax.ShapeDtypeStruct((rows, cols), F32),
        scratch_shapes=[pltpu.VMEM((n_dev, rows, cols), F32), pltpu.SemaphoreType.DMA((n_dev,)),
                        pltpu.SemaphoreType.DMA((n_dev,))],
    )(pack)


def _adamw(name, w, g, m, v):
    r, cols = w.shape
    tr = _row_tile(r)

    def body(w_ref, g_ref, m_ref, v_ref, d_ref, nm_ref, nv_ref):
        gv = g_ref[...]
        nm = ADAM_B1 * m_ref[...] + (1.0 - ADAM_B1) * gv
        nv = ADAM_B2 * v_ref[...] + (1.0 - ADAM_B2) * (gv * gv)
        m_hat = nm / (1.0 - ADAM_B1 ** ADAM_STEP)
        v_hat = nv / (1.0 - ADAM_B2 ** ADAM_STEP)
        d_ref[...] = -ADAM_LR * (m_hat / (jnp.sqrt(v_hat) + ADAM_EPS) + ADAM_WD * w_ref[...])
        nm_ref[...] = nm
        nv_ref[...] = nv

    spec = pl.BlockSpec((tr, cols), lambda i: (i, 0))
    return pl.pallas_call(
        body, name=name, grid=(r // tr,),
        in_specs=[spec] * 4, out_specs=[spec] * 3,
        out_shape=[jax.ShapeDtypeStruct((r, cols), F32)] * 3,
        compiler_params=_params(("parallel",)),
    )(w, g, m, v)


BIG = ["ffn1_w_in", "ffn1_w_out", "w_in", "conv_w_proj", "attn_w_o", "w_out", "ffn2_w_in", "ffn2_w_out", "conv_dw_kernel"]
COL_SHARDED = ("ffn1_w_in", "w_in", "ffn2_w_in")
SMALL = ["ffn1_norm", "mix_norm", "ffn2_norm", "conv_dw_bias", "conv_ln_g", "conv_ln_b", "q_norm", "k_norm", "attn_sinks", "rel_bias"]
WEIGHTS = ["ffn1_norm", "ffn1_w_in", "ffn1_w_out", "mix_norm", "w_in", "conv_dw_kernel", "conv_dw_bias", "conv_ln_g",
           "conv_ln_b", "conv_w_proj", "q_norm", "k_norm", "attn_sinks", "rel_bias", "attn_w_o", "w_out", "ffn2_norm",
           "ffn2_w_in", "ffn2_w_out"]
SMALL_PLACE = {"ffn1_norm": (0, 0, 1024), "mix_norm": (1, 0, 1024), "ffn2_norm": (2, 0, 1024), "conv_dw_bias": (3, 0, 1024),
               "conv_ln_g": (4, 0, 1024), "conv_ln_b": (5, 0, 1024), "q_norm": (6, 0, 64), "k_norm": (6, 128, 64),
               "attn_sinks": (6, 256, 16), "rel_bias": (7, 0, 512)}
LOSS_PLACE = (6, 384)


def _pack_small(vals, fill=0.0, loss=None):
    pack = jnp.full((SMALL_ROWS, D_MODEL), fill, F32)
    for name, (row, lane, n) in SMALL_PLACE.items():
        pack = pack.at[row, lane:lane + n].set(vals[name].reshape(n))
    if loss is not None:
        pack = pack.at[LOSS_PLACE[0], LOSS_PLACE[1]].set(loss)
    return pack


def _unpack_small(pack, shapes):
    return {name: pack[row, lane:lane + n].reshape(shapes[name]) for name, (row, lane, n) in SMALL_PLACE.items()}


def _shard_halves(name, a):
    if name == "conv_dw_kernel":
        a = jnp.pad(a, ((0, CONV_PAD - CONV_WIDTH), (0, 0)))
    r, cols = a.shape
    return a.reshape(2, r // 2, cols)


GATHER_GROUPS = {"A": ["ffn1_w_in", "ffn1_w_out"],
                 "B": ["w_in", "conv_dw_kernel", "conv_w_proj", "attn_w_o", "w_out"],
                 "C": ["ffn2_w_in", "ffn2_w_out"]}


class _MeshComm:
    def __init__(self, wts, idle_work=()):
        self.idle_work = idle_work
        self.c_idx = lax.axis_index("c").astype(jnp.int32).reshape(1)
        self.place_idx = jnp.stack([2 * lax.axis_index("x") + lax.axis_index("y"), lax.axis_index("c")]).astype(jnp.int32)
        self.gathers, self.exchanges, self.reductions, self.joins, self.reduced = {}, {}, {}, {}, {}
        self.tokens = ()
        self.shards = {n: _shard_halves(n, wts[n]) if n == "conv_dw_kernel" else _shard_halves(n, wts[n]).astype(BF16)
                       for n in BIG}
        self._gather_start("A", ())

    def _gather_start(self, group, after):
        shards = [self.shards[n] for n in GATHER_GROUPS[group]]
        lands = [lax.empty((N_CHIPS,) + s.shape, s.dtype) for s in shards]
        self.gathers[group] = _copy_start("gather_start_" + group, shards, lands, self._gather_plan(group), after=after)
        self.tokens = (self.gathers[group][-1],)

    @staticmethod
    def _gather_plan(group):
        return _gather_both_cores_plan if group == "C" else _gather_plan

    def weights(self, group, after):
        send_sems, recv_sems, shards, lands, token = self.gathers.pop(group)
        after = [token] if after is None else list(after) if isinstance(after, (list, tuple)) else [after]
        if group == "A":
            after += [self.shards[n] for g in ("B", "C") for n in GATHER_GROUPS[g]] + list(self.idle_work)
        shards, lands = _copy_wait("gather_wait_" + group, send_sems, recv_sems, shards, lands, after,
                                   self._gather_plan(group))
        gathered = lands if group == "C" else _gather_forward("gather_forward_" + group, shards, lands)
        self.tokens = ()
        following = {"A": "B", "B": "C"}.get(group)
        if following:
            self._gather_start(following, (gathered[0],))
        out = {}
        for n, g4 in zip(GATHER_GROUPS[group], gathered):
            r, cols = g4.shape[2] * 2, g4.shape[3]
            if n in COL_SHARDED:
                out[n] = g4.reshape(N_CHIPS, r, cols)
            elif n == "conv_dw_kernel":
                out[n] = g4.reshape(N_CHIPS, r, cols).transpose(1, 0, 2).reshape(r, N_CHIPS * cols)
            else:
                out[n] = g4.reshape(N_CHIPS * r, cols)
        return out

    def reduce_start(self, group, grads, behind=False):
        names = list(grads)
        g4 = []
        for n in names:
            a = grads[n]
            if n == "conv_dw_kernel":
                a = a.reshape(CONV_PAD, N_CHIPS, -1).transpose(1, 0, 2)
            elif n not in COL_SHARDED:
                a = a.reshape(N_CHIPS, a.shape[0] // N_CHIPS, a.shape[1])
            g4.append(a.reshape(N_CHIPS, 2, a.shape[1] // 2, a.shape[2]))
        if behind:
            lands = [lax.empty((N_CHIPS,) + g.shape[2:], g.dtype) for g in g4]
            started = _copy_start("exchange_start_" + group, g4, lands, _exchange_plan)
            self.exchanges[group] = (names,) + started
            return (started[-1],)
        return self._scatter_start(group, names, g4, _exchange_halves("exchange_halves_" + group, g4))

    def exchange_finish(self, group, after):
        names, send_sems, recv_sems, g4, lands, _ = self.exchanges.pop(group)
        g4, got = _copy_wait("exchange_wait_" + group, send_sems, recv_sems, g4, lands, after, _exchange_plan)
        return self._scatter_start(group, names, g4, got)

    def _scatter_start(self, group, names, g4, got):
        sums = [_add_own_half(self.c_idx, a, b) for a, b in zip(g4, got)]
        lands = [lax.empty((N_CHIPS - 1,) + s.shape[1:], s.dtype) for s in sums]
        started = _copy_start("scatter_start_" + group, sums, lands, _scatter_plan)
        self.reductions[group] = (names,) + started
        return (started[-1],)

    def reduce_finish(self, group, after, behind=False):
        names, send_sems, recv_sems, sums, lands, _ = self.reductions.pop(group)
        sums, lands = _copy_wait("scatter_wait_" + group, send_sems, recv_sems, sums, lands, after, _scatter_plan)
        halves = [_sum_pieces(self.place_idx, s, p) for s, p in zip(sums, lands)]
        if behind:
            started = _copy_start("join_start_" + group, halves, [], _join_plan)
            self.joins[group] = (names,) + started
            return (started[-1],)
        self.reduced.update(zip(names, _join_halves("join_halves_" + group, halves)))
        return ()

    def join_finish(self, group, after):
        names, send_sems, recv_sems, halves, _, _ = self.joins.pop(group)
        self.reduced.update(zip(names, _copy_wait("join_wait_" + group, send_sems, recv_sems, halves, [], after, _join_plan)[0]))


def kernel(x, ffn1_norm, ffn1_w_in, ffn1_w_out, mix_norm, w_in, conv_dw_kernel, conv_dw_bias, conv_ln_g, conv_ln_b, conv_w_proj, q_norm, k_norm, attn_sinks, rel_bias, attn_w_o, w_out, ffn2_norm, ffn2_w_in, ffn2_w_out, loss_target, m_ffn1_norm, m_ffn1_w_in, m_ffn1_w_out, m_mix_norm, m_w_in, m_conv_dw_kernel, m_conv_dw_bias, m_conv_ln_g, m_conv_ln_b, m_conv_w_proj, m_q_norm, m_k_norm, m_attn_sinks, m_rel_bias, m_attn_w_o, m_w_out, m_ffn2_norm, m_ffn2_w_in, m_ffn2_w_out, v_ffn1_norm, v_ffn1_w_in, v_ffn1_w_out, v_mix_norm, v_w_in, v_conv_dw_kernel, v_conv_dw_bias, v_conv_ln_g, v_conv_ln_b, v_conv_w_proj, v_q_norm, v_k_norm, v_attn_sinks, v_rel_bias, v_attn_w_o, v_w_out, v_ffn2_norm, v_ffn2_w_in, v_ffn2_w_out):
    args = dict(locals())
    wts = {n: args[n] for n in WEIGHTS}
    mom = {n: args["m_" + n] for n in WEIGHTS}
    var = {n: args["v_" + n] for n in WEIGHTS}
    small_packs = [_pack_small(wts), _pack_small(mom), _pack_small(var, fill=1.0)]
    dw_moments = [_shard_halves("conv_dw_kernel", a) for a in (wts["conv_dw_kernel"], mom["conv_dw_kernel"], var["conv_dw_kernel"])]
    comm = _MeshComm(wts, idle_work=small_packs + dw_moments)
    small = {n: wts[n] if n in ("attn_sinks", "rel_bias") else wts[n].reshape(1, -1) for n in SMALL}
    loss_part, grad_x, g = _local_step(x[0], loss_target[0], small, comm)

    small_sum = _all_reduce_small(_pack_small(g, loss=loss_part))
    loss = small_sum[LOSS_PLACE[0], LOSS_PLACE[1]]
    small_shapes = {n: wts[n].shape for n in SMALL}
    g_small = _unpack_small(small_sum, small_shapes)

    grads, delta, new_m, new_v = {}, {}, {}, {}
    for n in BIG:
        j = comm.reduced[n]
        gs = j.reshape(j.shape[1] * 2, j.shape[2])
        pad = n == "conv_dw_kernel"
        ws, ms, vs = (a.reshape(gs.shape) for a in (dw_moments if pad else (wts[n], mom[n], var[n])))
        d, nm, nv = _adamw("adamw_" + n, ws, gs, ms, vs)
        cut = (lambda a: a[:CONV_WIDTH]) if pad else (lambda a: a)
        grads[n], delta[n], new_m[n], new_v[n] = cut(gs), cut(d), cut(nm), cut(nv)
    d, nm, nv = _adamw("adamw_small", small_packs[0], small_sum, small_packs[1], small_packs[2])
    grads.update(g_small)
    delta.update(_unpack_small(d, small_shapes))
    new_m.update(_unpack_small(nm, small_shapes))
    new_v.update(_unpack_small(nv, small_shapes))

    return (loss, grad_x[None], *[grads[n] for n in WEIGHTS], *[delta[n] for n in WEIGHTS],
            *[new_m[n] for n in WEIGHTS], *[new_v[n] for n in WEIGHTS])
```

```python
import functools
import math

import jax
import jax.numpy as jnp
from jax import lax
from jax.experimental import pallas as pl
from jax.experimental.pallas import tpu as pltpu

F32 = jnp.float32
BF16 = jnp.bfloat16
MESH = pl.DeviceIdType.MESH

EPS = 1e-6
D_MODEL = 1024
D_FF = 2816
N_CHIPS = 4
SHARD_W = 2 * D_FF // N_CHIPS
HEAD_DIM = 64
N_Q_HEADS = 16
N_KV_HEADS = 4
GROUP = N_Q_HEADS // N_KV_HEADS
BLOCK = 128
QROWS = GROUP * BLOCK
N_BUCKETS = 32
MAX_DISTANCE = 128
CONV_WIDTH = 31
CONV_PAD = 32
NEG = float(jnp.finfo(jnp.float32).min)

ADAM_LR = 0.001
ADAM_B1 = 0.9
ADAM_B2 = 0.999
ADAM_EPS = 1e-08
ADAM_WD = 0.01
ADAM_STEP = 10

VMEM_LIMIT_BYTES = 56 * 1024 * 1024
ROW_TILE = 1024
TOKEN_TILE = 4096
CONV_TILE = 256
CONV_ROWS = 128
LANES = 128

COL_CONV_A, COL_CONV_G, COL_Q, COL_K, COL_V, COL_GC, COL_GA = 0, 1024, 2048, 3072, 3328, 3584, 4608
IN_W = 5632


def _params(sem, vmem=VMEM_LIMIT_BYTES):
    return pltpu.CompilerParams(dimension_semantics=sem, vmem_limit_bytes=vmem)


def _sigmoid(x):
    return 1.0 / (1.0 + jnp.exp(-x))


def _dot(a, b, trans_a=False, trans_b=False, precision=None):
    dn = (((0,) if trans_a else (1,), (1,) if trans_b else (0,)), ((), ()))
    return lax.dot_general(a, b, dn, preferred_element_type=F32, precision=precision)


def _mm(name, grid, a, a_spec, b, b_spec, acc_shape, *, trans_a=False, trans_b=False, a_pre=None, b_pre=None,
        extras=(), extra_specs=(), tokens=(), out_shape, out_specs, epilogue,
        sem=("parallel", "parallel", "arbitrary")):
    n_k = grid[2]
    extras = tuple(extras) + tuple(tokens)
    extra_specs = tuple(extra_specs) + (pl.BlockSpec((8, LANES), lambda i, j, kk: (0, 0)),) * len(tokens)
    n_extra = len(extras)
    n_out = len(out_shape)

    def body(a_ref, b_ref, *rest):
        ex = rest[:n_extra]
        outs = rest[n_extra:n_extra + n_out]
        ids = (pl.program_id(0), pl.program_id(1), pl.program_id(2))
        av = a_ref[...]
        if a_pre is not None:
            av = a_pre(av)
        bv = b_ref[...]
        if b_pre is not None:
            bv = b_pre(bv)
        if n_k == 1:
            epilogue(_dot(av, bv, trans_a, trans_b), ex, outs, ids)
        else:
            acc = rest[-1]

            @pl.when(ids[2] == 0)
            def _():
                acc[...] = jnp.zeros_like(acc)

            acc[...] += _dot(av, bv, trans_a, trans_b)

            @pl.when(ids[2] == n_k - 1)
            def _():
                epilogue(acc[...], ex, outs, ids)

    scratch = [] if n_k == 1 else [pltpu.VMEM(acc_shape, F32)]
    return pl.pallas_call(
        body, name=name, grid=grid,
        in_specs=[a_spec, b_spec, *extra_specs],
        out_specs=list(out_specs), out_shape=list(out_shape),
        scratch_shapes=scratch, compiler_params=_params(sem),
    )(a, b, *extras)


def _half_bf16(v):
    return (0.5 * v).astype(BF16)


def _to_bf16(v):
    return v.astype(BF16)


def _rmsnorm_fwd(name, x, g, tokens=()):
    t, d = x.shape
    tm = min(ROW_TILE, t)

    def body(x_ref, g_ref, *rest):
        o_ref = rest[-1]
        xv = x_ref[...]
        r = lax.rsqrt(jnp.mean(xv * xv, axis=-1, keepdims=True) + EPS)
        o_ref[...] = (xv * r * g_ref[...]).astype(BF16)

    return pl.pallas_call(
        body, name=name, grid=(t // tm,),
        in_specs=[pl.BlockSpec((tm, d), lambda i: (i, 0)), pl.BlockSpec((1, d), lambda i: (0, 0))]
        + [pl.BlockSpec((8, LANES), lambda i: (0, 0))] * len(tokens),
        out_specs=pl.BlockSpec((tm, d), lambda i: (i, 0)),
        out_shape=jax.ShapeDtypeStruct((t, d), BF16),
        compiler_params=_params(("parallel",)),
    )(x, g, *tokens)


def _ffn_in(name, n, w_in4, tokens=()):
    t, d = n.shape
    tm = min(ROW_TILE, t)

    def body(n_ref, wa_ref, wb_ref, *rest):
        ab_ref, h_ref = rest[-2:]
        nv = n_ref[...]
        a = _dot(nv, wa_ref[...])
        b = _dot(nv, wb_ref[...])
        h_ref[...] = (a * _sigmoid(a) * b).astype(BF16)
        ab_ref[0] = a.astype(BF16)
        ab_ref[1] = b.astype(BF16)

    return pl.pallas_call(
        body, name=name, grid=(2, t // tm),
        in_specs=[pl.BlockSpec((tm, d), lambda j, i: (i, 0)),
                  pl.BlockSpec((None, d, SHARD_W), lambda j, i: (j, 0, 0)),
                  pl.BlockSpec((None, d, SHARD_W), lambda j, i: (j + 2, 0, 0))]
        + [pl.BlockSpec((8, LANES), lambda j, i: (0, 0))] * len(tokens),
        out_specs=[pl.BlockSpec((2, tm, SHARD_W), lambda j, i: (0, i, j)),
                   pl.BlockSpec((tm, SHARD_W), lambda j, i: (i, j))],
        out_shape=[jax.ShapeDtypeStruct((2, t, D_FF), BF16), jax.ShapeDtypeStruct((t, D_FF), BF16)],
        compiler_params=_params(("parallel", "parallel")),
    )(n, w_in4, w_in4, *tokens)


def _mm_residual(name, a, w, res, scale, next_gain=None, loss_target=None):
    t, k = a.shape
    n = w.shape[1]
    tm = min(ROW_TILE, t)
    row = pl.BlockSpec((tm, n), lambda i, j, kk: (i, 0))
    extras, specs = [res], [row]
    shapes, out_specs = [jax.ShapeDtypeStruct((t, n), F32)], [row]
    if next_gain is not None:
        extras.append(next_gain)
        specs.append(pl.BlockSpec((1, n), lambda i, j, kk: (0, 0)))
        shapes.append(jax.ShapeDtypeStruct((t, n), BF16))
        out_specs.append(row)
    if loss_target is not None:
        extras.append(loss_target)
        specs.append(row)
        shapes.append(jax.ShapeDtypeStruct((8, LANES), F32))
        out_specs.append(pl.BlockSpec((8, LANES), lambda i, j, kk: (0, 0)))

    def epilogue(acc, ex, outs, ids):
        y = ex[0][...] + scale * acc
        if loss_target is None:
            outs[0][...] = y
        if next_gain is not None:
            r = lax.rsqrt(jnp.mean(y * y, axis=-1, keepdims=True) + EPS)
            outs[1][...] = (y * r * ex[1][...]).astype(BF16)
        if loss_target is not None:
            diff = y - ex[1][...]
            outs[0][...] = diff * (1.0 / n)
            part = jnp.full((8, LANES), 0.5 / n * jnp.sum(diff * diff), F32)

            @pl.when(ids[0] == 0)
            def _():
                outs[1][...] = part

            @pl.when(ids[0] > 0)
            def _():
                outs[1][...] += part

    sem = ("parallel" if loss_target is None else "arbitrary", "parallel", "arbitrary")
    out = _mm(name, (t // tm, 1, 1), a, pl.BlockSpec((tm, k), lambda i, j, kk: (i, 0)),
              w, pl.BlockSpec((k, n), lambda i, j, kk: (0, 0)), (tm, n),
              extras=extras, extra_specs=specs, out_shape=shapes, out_specs=out_specs, epilogue=epilogue, sem=sem)
    return out[0] if len(out) == 1 else tuple(out)


def _ffn_fwd(tag, x, n, w_in4, w_out, tokens=(), **tail):
    ab, h = _ffn_in(tag + "_in", n, w_in4, tokens)
    y = _mm_residual(tag + "_out", h, w_out, x, 0.5, **tail)
    return y, (n, ab, h)


def _ffn_bwd(tag, dres, x, g, saved, w_in4, w_out, tokens=(), on_first=None, on_weight_grads=None):
    n, ab, h = saved
    t, d = x.shape
    tm = min(ROW_TILE, t)
    tk = min(TOKEN_TILE, t)
    half_w = SHARD_W

    def dact_epilogue(acc, ex, outs, ids):
        a = ex[0][0].astype(F32)
        b = ex[0][1].astype(F32)
        sig = _sigmoid(a)
        outs[0][0] = (acc * b * (sig * (1.0 + a * (1.0 - sig)))).astype(BF16)
        outs[0][1] = (acc * (a * sig)).astype(BF16)

    du = _mm(tag + "_dact", (2, t // tm, 1),
             dres, pl.BlockSpec((tm, d), lambda j, i, kk: (i, 0)),
             w_out, pl.BlockSpec((half_w, d), lambda j, i, kk: (j, 0)), (tm, half_w),
             trans_b=True, a_pre=_half_bf16,
             extras=(ab,), extra_specs=(pl.BlockSpec((2, tm, half_w), lambda j, i, kk: (0, i, j)),), tokens=tokens,
             out_shape=(jax.ShapeDtypeStruct((2, t, D_FF), BF16),),
             out_specs=(pl.BlockSpec((2, tm, half_w), lambda j, i, kk: (0, i, j)),),
             epilogue=dact_epilogue)[0]

    def store_epilogue(acc, ex, outs, ids):
        outs[0][...] = acc.astype(BF16)

    early = () if on_first is None else on_first(du)

    tk_out = min(TOKEN_TILE // 2, t)
    dw_out = _mm(tag + "_dwout", (2, 1, t // tk_out),
                 h, pl.BlockSpec((tk_out, half_w), lambda i, j, kk: (kk, i)),
                 dres, pl.BlockSpec((tk_out, d), lambda i, j, kk: (kk, 0)), (half_w, d),
                 trans_a=True, b_pre=_half_bf16, tokens=early,
                 out_shape=(jax.ShapeDtypeStruct((D_FF, d), BF16),),
                 out_specs=(pl.BlockSpec((half_w, d), lambda i, j, kk: (i, 0)),),
                 epilogue=store_epilogue)[0]

    dw_in4 = _mm(tag + "_dwin", (1, N_CHIPS, t // tk),
                 n, pl.BlockSpec((tk, d), lambda i, j, kk: (kk, 0)),
                 du, pl.BlockSpec((None, tk, SHARD_W), lambda i, j, kk: (j // 2, kk, j % 2)), (d, SHARD_W),
                 trans_a=True,
                 out_shape=(jax.ShapeDtypeStruct((N_CHIPS, d, SHARD_W), BF16),),
                 out_specs=(pl.BlockSpec((None, d, SHARD_W), lambda i, j, kk: (j, 0, 0)),),
                 epilogue=store_epilogue)[0]

    late = () if on_weight_grads is None else on_weight_grads(dw_in4, dw_out)

    tn = min(NORM_GRAD_TILE, t)
    shard_specs = [pl.BlockSpec((None, tn, SHARD_W), functools.partial(lambda i, s: (s // 2, i, s % 2), s=s))
                   for s in range(N_CHIPS)]
    dx, dg = _norm_input_grad(tag + "_dn", du, shard_specs, w_in4, x, g, dres, late)
    return dx, dw_in4, dw_out, dg


NORM_GRAD_TILE = 512


def _norm_input_grad(name, a, shard_specs, w4, x, g, dres, tokens=()):
    t, d = x.shape
    tn = min(NORM_GRAD_TILE, t)
    ns = len(shard_specs)

    def body(*refs):
        a_refs, (w_ref, x_ref, g_ref, dres_ref) = refs[:ns], refs[ns:ns + 4]
        out_ref, dg_ref = refs[-2:]
        acc = _dot(a_refs[0][...], w_ref[0], trans_b=True)
        for s in range(1, ns):
            acc = acc + _dot(a_refs[s][...], w_ref[s], trans_b=True)
        xv = x_ref[...]
        r = lax.rsqrt(jnp.mean(xv * xv, axis=-1, keepdims=True) + EPS)
        w = acc * g_ref[...]
        out_ref[...] = dres_ref[...] + (r * w - xv * (r * r * r) * jnp.mean(xv * w, axis=-1, keepdims=True))
        part = jnp.sum(acc * (xv * r), axis=0, keepdims=True)
        i = pl.program_id(0)

        @pl.when(i == 0)
        def _():
            dg_ref[...] = part

        @pl.when(i > 0)
        def _():
            dg_ref[...] += part

    row = pl.BlockSpec((tn, d), lambda i: (i, 0))
    vec = pl.BlockSpec((1, d), lambda i: (0, 0))
    return pl.pallas_call(
        body, name=name, grid=(t // tn,),
        in_specs=list(shard_specs) + [pl.BlockSpec(w4.shape, lambda i: (0, 0, 0)), row, vec, row]
        + [pl.BlockSpec((8, LANES), lambda i: (0, 0))] * len(tokens),
        out_specs=[row, vec],
        out_shape=[jax.ShapeDtypeStruct((t, d), F32), jax.ShapeDtypeStruct((1, d), F32)],
        compiler_params=_params(("arbitrary",)),
    )(*[a] * ns, w4, x, g, dres, *tokens)


def _conv_fill(zp_ref, a_ref, g_ref, ah_ref, gh_ref, i):
    zh = ah_ref[...].astype(F32) * _sigmoid(gh_ref[...].astype(F32))
    zp_ref[pl.ds(0, CONV_PAD), :] = jnp.where(i > 0, zh, 0.0)
    zp_ref[pl.ds(CONV_PAD, a_ref.shape[0]), :] = a_ref[...].astype(F32) * _sigmoid(g_ref[...].astype(F32))


def _shift_groups(shifts):
    groups = {}
    for j, s in shifts:
        groups.setdefault(s % 8, []).append((j, s // 8))
    return groups


def _windows(zp_ref, r0, lanes, groups):
    for q, taps in groups.items():
        deepest = max(p for _, p in taps)
        win = zp_ref[pl.ds(r0 + q, 8 * deepest + CONV_ROWS), lanes]
        for j, p in taps:
            yield j, win[8 * p:8 * p + CONV_ROWS]


def _conv_apply(zp_ref, out_ref, dw_ref, bias_ref, tm, ch, shifts):
    groups = _shift_groups(shifts)
    for cc in range(ch // LANES):
        lanes = pl.ds(cc * LANES, LANES)
        w = [dw_ref[pl.ds(j, 1), lanes] for j in range(CONV_WIDTH)]
        for r0 in range(0, tm, CONV_ROWS):
            if bias_ref is None:
                acc = jnp.zeros((CONV_ROWS, LANES), F32)
            else:
                acc = jnp.broadcast_to(bias_ref[:, lanes], (CONV_ROWS, LANES))
            for j, rows in _windows(zp_ref, r0, lanes, groups):
                acc = acc + w[j] * rows
            out_ref[pl.ds(r0, CONV_ROWS), lanes] = acc


FWD_SHIFTS = [(j, CONV_PAD - (CONV_WIDTH - 1) + j) for j in range(CONV_WIDTH)]
BWD_SHIFTS = [(j, CONV_WIDTH - 1 - j) for j in range(CONV_WIDTH)]


def _conv_taps(zp_ref, z1_ref, dw_ref, bias_ref, tm, ch):
    _conv_apply(zp_ref, z1_ref, dw_ref, bias_ref, tm, ch, FWD_SHIFTS)


def _conv_specs(tm, ch):
    per = tm // CONV_PAD
    cb = COL_CONV_G // ch
    return [pl.BlockSpec((tm, ch), lambda i: (i, 0)),
            pl.BlockSpec((tm, ch), lambda i: (i, cb)),
            pl.BlockSpec((CONV_PAD, ch), lambda i: (jnp.maximum(i * per - 1, 0), 0)),
            pl.BlockSpec((CONV_PAD, ch), lambda i: (jnp.maximum(i * per - 1, 0), cb))]


def _conv_fwd(p, dw, bias, ln_g, ln_b):
    t = p.shape[0]
    ch = D_MODEL
    tm = min(CONV_TILE, t)

    def body(a_ref, g_ref, ah_ref, gh_ref, dw_ref, bias_ref, lg_ref, lb_ref, o_ref, z1_ref, zp_ref):
        i = pl.program_id(0)
        _conv_fill(zp_ref, a_ref, g_ref, ah_ref, gh_ref, i)
        _conv_taps(zp_ref, z1_ref, dw_ref, bias_ref, tm, ch)
        z1 = z1_ref[...]
        mu = jnp.mean(z1, axis=-1, keepdims=True)
        zc = z1 - mu
        rs = lax.rsqrt(jnp.mean(zc * zc, axis=-1, keepdims=True) + EPS)
        z2 = zc * rs * lg_ref[...] + lb_ref[...]
        o_ref[...] = (z2 * _sigmoid(z2)).astype(BF16)

    vec = pl.BlockSpec((1, ch), lambda i: (0, 0))
    return pl.pallas_call(
        body, name="conv_fwd", grid=(t // tm,),
        in_specs=_conv_specs(tm, ch) + [pl.BlockSpec((CONV_PAD, ch), lambda i: (0, 0)), vec, vec, vec],
        out_specs=[pl.BlockSpec((tm, ch), lambda i: (i, 0)), pl.BlockSpec((tm, ch), lambda i: (i, 0))],
        out_shape=[jax.ShapeDtypeStruct((t, ch), BF16), jax.ShapeDtypeStruct((t, ch), F32)],
        scratch_shapes=[pltpu.VMEM((CONV_PAD + tm, ch), F32)],
        compiler_params=_params(("parallel",)),
    )(p, p, p, p, dw, bias, ln_g, ln_b)


def _conv_bwd_ln(p, z1_saved, dz3, ln_g, ln_b):
    t = p.shape[0]
    ch = D_MODEL
    tm = min(CONV_TILE, t)

    def body(a_ref, g_ref, ah_ref, gh_ref, z1_ref, dz3_ref, lg_ref, lb_ref,
             dz1_ref, ddw_ref, dbias_ref, dlg_ref, dlb_ref, zp_ref):
        i = pl.program_id(0)
        _conv_fill(zp_ref, a_ref, g_ref, ah_ref, gh_ref, i)
        z1 = z1_ref[...]
        mu = jnp.mean(z1, axis=-1, keepdims=True)
        zc = z1 - mu
        rs = lax.rsqrt(jnp.mean(zc * zc, axis=-1, keepdims=True) + EPS)
        xh = zc * rs
        z2 = xh * lg_ref[...] + lb_ref[...]
        sig = _sigmoid(z2)
        dz2 = dz3_ref[...].astype(F32) * (sig * (1.0 + z2 * (1.0 - sig)))
        dxh = dz2 * lg_ref[...]
        dz1 = rs * (dxh - jnp.mean(dxh, axis=-1, keepdims=True) - xh * jnp.mean(dxh * xh, axis=-1, keepdims=True))
        dz1_ref[...] = dz1

        @pl.when(i == 0)
        def _():
            ddw_ref[...] = jnp.zeros_like(ddw_ref)
            dbias_ref[...] = jnp.zeros_like(dbias_ref)
            dlg_ref[...] = jnp.zeros_like(dlg_ref)
            dlb_ref[...] = jnp.zeros_like(dlb_ref)

        dlg_ref[...] += jnp.sum(dz2 * xh, axis=0, keepdims=True)
        dlb_ref[...] += jnp.sum(dz2, axis=0, keepdims=True)
        dbias_ref[...] += jnp.sum(dz1, axis=0, keepdims=True)
        groups = _shift_groups(FWD_SHIFTS)
        for cc in range(ch // LANES):
            lanes = pl.ds(cc * LANES, LANES)
            accs = [jnp.zeros((8, LANES), F32) for _ in range(CONV_WIDTH)]
            for r0 in range(0, tm, CONV_ROWS):
                dzc = dz1_ref[pl.ds(r0, CONV_ROWS), lanes]
                for j, rows in _windows(zp_ref, r0, lanes, groups):
                    accs[j] = accs[j] + jnp.sum((dzc * rows).reshape(CONV_ROWS // 8, 8, LANES), axis=0)
            for j in range(CONV_WIDTH):
                ddw_ref[pl.ds(j, 1), lanes] += jnp.sum(accs[j], axis=0, keepdims=True)

    vec = pl.BlockSpec((1, ch), lambda i: (0, 0))
    return pl.pallas_call(
        body, name="conv_bwd_ln", grid=(t // tm,),
        in_specs=_conv_specs(tm, ch) + [pl.BlockSpec((tm, ch), lambda i: (i, 0)),
                                        pl.BlockSpec((tm, ch), lambda i: (i, 0)), vec, vec],
        out_specs=[pl.BlockSpec((tm, ch), lambda i: (i, 0)), pl.BlockSpec((CONV_PAD, ch), lambda i: (0, 0)), vec, vec, vec],
        out_shape=[jax.ShapeDtypeStruct((t, ch), F32), jax.ShapeDtypeStruct((CONV_PAD, ch), F32)]
        + [jax.ShapeDtypeStruct((1, ch), F32)] * 3,
        scratch_shapes=[pltpu.VMEM((CONV_PAD + tm, ch), F32)],
        compiler_params=_params(("arbitrary",)),
    )(p, p, p, p, z1_saved, dz3, ln_g, ln_b)


def _conv_bwd_glu(p, dz1, dw, dq, dkv, dgates):
    t = p.shape[0]
    ch = D_MODEL
    tm = min(CONV_TILE, t)
    per = tm // CONV_PAD
    n_halo = t // CONV_PAD
    cb = COL_CONV_G // ch

    def body(a_ref, g_ref, dz_ref, dzn_ref, dw_ref, dq_ref, dkv_ref, dgates_ref, o_ref, zp_ref, z0_ref):
        i = pl.program_id(0)
        o_ref[:, pl.ds(COL_Q, Q_W)] = dq_ref[...]
        o_ref[:, pl.ds(COL_K, 2 * KV_W)] = dkv_ref[...]
        o_ref[:, pl.ds(COL_GC, ch)] = dgates_ref[0]
        o_ref[:, pl.ds(COL_GA, ch)] = dgates_ref[1]
        zp_ref[pl.ds(0, tm), :] = dz_ref[...]
        zp_ref[pl.ds(tm, CONV_PAD), :] = jnp.where(i < t // tm - 1, dzn_ref[...], 0.0)
        _conv_apply(zp_ref, z0_ref, dw_ref, None, tm, ch, BWD_SHIFTS)
        dz0 = z0_ref[...]
        a = a_ref[...].astype(F32)
        sig = _sigmoid(g_ref[...].astype(F32))
        o_ref[:, pl.ds(0, ch)] = (dz0 * sig).astype(BF16)
        o_ref[:, pl.ds(ch, ch)] = (dz0 * a * sig * (1.0 - sig)).astype(BF16)

    return pl.pallas_call(
        body, name="conv_bwd_glu", grid=(t // tm,),
        in_specs=[pl.BlockSpec((tm, ch), lambda i: (i, 0)), pl.BlockSpec((tm, ch), lambda i: (i, cb)),
                  pl.BlockSpec((tm, ch), lambda i: (i, 0)),
                  pl.BlockSpec((CONV_PAD, ch), lambda i: (jnp.minimum((i + 1) * per, n_halo - 1), 0)),
                  pl.BlockSpec((CONV_PAD, ch), lambda i: (0, 0)),
                  pl.BlockSpec((tm, Q_W), lambda i: (i, 0)), pl.BlockSpec((tm, 2 * KV_W), lambda i: (i, 0)),
                  pl.BlockSpec((2, tm, ch), lambda i: (0, i, 0))],
        out_specs=pl.BlockSpec((tm, IN_W), lambda i: (i, 0)),
        out_shape=jax.ShapeDtypeStruct((t, IN_W), BF16),
        scratch_shapes=[pltpu.VMEM((tm + CONV_PAD, ch), F32), pltpu.VMEM((tm, ch), F32)],
        compiler_params=_params(("parallel",)),
    )(p, p, dz1, dz1, dw, dq, dkv, dgates)


def _bucket_onehot():
    qi = jnp.arange(BLOCK, dtype=jnp.int32)[:, None]
    kj = jnp.arange(2 * BLOCK, dtype=jnp.int32)[None, :]
    dist = jnp.maximum(qi + BLOCK - kj, 0)
    max_exact = N_BUCKETS // 2
    dflt = jnp.maximum(dist, 1).astype(F32)
    large = max_exact + (jnp.log(dflt / max_exact) / math.log(MAX_DISTANCE / max_exact)
                         * (N_BUCKETS - max_exact)).astype(jnp.int32)
    large = jnp.minimum(large, N_BUCKETS - 1)
    bucket = jnp.where(dist < max_exact, dist, large)
    onehot = bucket[None] == jnp.arange(N_BUCKETS, dtype=jnp.int32)[:, None, None]
    return onehot.astype(F32).reshape(N_BUCKETS, BLOCK * 2 * BLOCK)


def _bias_table(rel_bias_t, onehot):
    n = onehot.shape[1]
    tn = 4096

    def body(r_ref, oh_ref, o_ref):
        flat = pl.program_id(0) * tn + lax.broadcasted_iota(jnp.int32, (N_Q_HEADS, tn), 1)
        dist = (flat // (2 * BLOCK)) + BLOCK - (flat % (2 * BLOCK))
        bias = _dot(r_ref[...], oh_ref[...], precision=lax.Precision.HIGHEST)
        o_ref[...] = jnp.where((dist >= 0) & (dist < BLOCK), bias, NEG)

    return pl.pallas_call(
        body, name="bias_table", grid=(n // tn,),
        in_specs=[pl.BlockSpec((N_Q_HEADS, N_BUCKETS), lambda i: (0, 0)), pl.BlockSpec((N_BUCKETS, tn), lambda i: (0, i))],
        out_specs=pl.BlockSpec((N_Q_HEADS, tn), lambda i: (0, i)),
        out_shape=jax.ShapeDtypeStruct((N_Q_HEADS, n), F32),
        compiler_params=_params(("parallel",)),
    )(rel_bias_t, onehot)


def _bias_table_bwd(dbias, onehot):
    n = onehot.shape[1]
    tn = 4096

    def body(d_ref, oh_ref, o_ref):
        part = _dot(d_ref[...], oh_ref[...], trans_b=True, precision=lax.Precision.HIGHEST)
        i = pl.program_id(0)

        @pl.when(i == 0)
        def _():
            o_ref[...] = part

        @pl.when(i > 0)
        def _():
            o_ref[...] += part

    return pl.pallas_call(
        body, name="bias_table_bwd", grid=(n // tn,),
        in_specs=[pl.BlockSpec((N_Q_HEADS, tn), lambda i: (0, i)), pl.BlockSpec((N_BUCKETS, tn), lambda i: (0, i))],
        out_specs=pl.BlockSpec((N_Q_HEADS, N_BUCKETS), lambda i: (0, 0)),
        out_shape=jax.ShapeDtypeStruct((N_Q_HEADS, N_BUCKETS), F32),
        compiler_params=_params(("arbitrary",)),
    )(dbias, onehot)


def _lane_head(rows):
    return lax.broadcasted_iota(jnp.int32, (rows, KV_W), 1) // HEAD_DIM


def _group_rms(x, gain_wide):
    head = _lane_head(x.shape[0])
    sq = x * x
    r = jnp.zeros_like(x)
    for i in range(N_KV_HEADS):
        ms = jnp.sum(jnp.where(head == i, sq, 0.0), axis=-1, keepdims=True) * (1.0 / HEAD_DIM)
        r = jnp.where(head == i, lax.rsqrt(ms + EPS), r)
    return r, x * r * gain_wide


def _stack_heads(group):
    head = _lane_head(group.shape[0])
    return jnp.concatenate([jnp.where(head == i, group, jnp.zeros_like(group)) for i in range(N_KV_HEADS)], axis=0)


def _unstack_heads(stacked):
    head = _lane_head(BLOCK)
    out = jnp.where(head == 0, stacked[:BLOCK], 0.0)
    for i in range(1, N_KV_HEADS):
        out = out + jnp.where(head == i, stacked[i * BLOCK:(i + 1) * BLOCK], 0.0)
    return out


def _repeaters():
    row = lax.broadcasted_iota(jnp.int32, (KV_W, KV_W), 0)
    col = lax.broadcasted_iota(jnp.int32, (KV_W, KV_W), 1)
    return [(row == h * HEAD_DIM + col % HEAD_DIM).astype(BF16) for h in range(N_KV_HEADS)]


def _attn_probs(q_stack, k_rep, sink, bias, before_start):
    s = _dot(q_stack, k_rep, trans_b=True) * (1.0 / math.sqrt(HEAD_DIM)) + bias
    s = jnp.where(before_start, NEG, s)
    m = jnp.maximum(jnp.max(s, axis=-1, keepdims=True), sink)
    p = jnp.exp(s - m)
    es = jnp.exp(sink - m)
    inv = 1.0 / (jnp.sum(p, axis=-1, keepdims=True) + es)
    return p * inv, es * inv


def _before_start(n):
    col = lax.broadcasted_iota(jnp.int32, (QROWS, 2 * BLOCK), 1)
    return (col < BLOCK) & (n == 0)


STEP_BLOCKS = 4
KV_W = N_KV_HEADS * HEAD_DIM
Q_W = N_Q_HEADS * HEAD_DIM


def _attn_specs():
    gain = pl.BlockSpec((1, KV_W), lambda n: (0, 0))
    sink = pl.BlockSpec((N_KV_HEADS, QROWS, 1), lambda n: (0, 0, 0))
    bias = pl.BlockSpec((N_KV_HEADS, QROWS, 2 * BLOCK), lambda n: (0, 0, 0))
    return gain, sink, bias


def _attn_fwd(p, gq, gk, sink_rows, bias):
    t = p.shape[0]
    nb = t // BLOCK
    per = STEP_BLOCKS if nb % STEP_BLOCKS == 0 else 1
    gain, sink, bspec = _attn_specs()

    def body(q_ref, kp_ref, kc_ref, vp_ref, vc_ref, gq_ref, gk_ref, sink_ref, bias_ref, o_ref, p_ref, ps_ref):
        first = pl.program_id(0) * per
        rep = _repeaters()
        kf = jnp.concatenate([kp_ref[...], kc_ref[...]], axis=0).astype(F32)
        kn = _group_rms(kf, gk_ref[...])[1].astype(BF16)
        v = jnp.concatenate([vp_ref[...], vc_ref[...]], axis=0)
        for h in range(N_KV_HEADS):
            k_rep = _dot(kn, rep[h]).astype(BF16)
            v_rep = _dot(v, rep[h]).astype(BF16)
            for sub in range(per):
                rows = pl.ds(sub * BLOCK, BLOCK)
                window = slice(sub * BLOCK, (sub + 2) * BLOCK)
                qn = _group_rms(q_ref[rows, pl.ds(h * KV_W, KV_W)].astype(F32), gq_ref[...])[1]
                pn, ps_ref[sub, h] = _attn_probs(_stack_heads(qn).astype(BF16), k_rep[window], sink_ref[h], bias_ref[h],
                                                 _before_start(first + sub))
                pn = pn.astype(BF16)
                p_ref[sub, h] = pn
                o_ref[rows, pl.ds(h * KV_W, KV_W)] = _unstack_heads(_dot(pn, v_rep[window])).astype(BF16)

    def kv_specs(col):
        return [pl.BlockSpec((BLOCK, KV_W), lambda n: (jnp.maximum(n * per - 1, 0), col // KV_W)),
                pl.BlockSpec((per * BLOCK, KV_W), lambda n: (n, col // KV_W))]

    return pl.pallas_call(
        body, name="attn_fwd", grid=(nb // per,),
        in_specs=[pl.BlockSpec((per * BLOCK, Q_W), lambda n: (n, COL_Q // Q_W))] + kv_specs(COL_K) + kv_specs(COL_V)
        + [gain, gain, sink, bspec],
        out_specs=[pl.BlockSpec((per * BLOCK, Q_W), lambda n: (n, 0)),
                   pl.BlockSpec((per, N_KV_HEADS, QROWS, 2 * BLOCK), lambda n: (n, 0, 0, 0)),
                   pl.BlockSpec((per, N_KV_HEADS, QROWS, 1), lambda n: (n, 0, 0, 0))],
        out_shape=[jax.ShapeDtypeStruct((t, Q_W), BF16),
                   jax.ShapeDtypeStruct((nb, N_KV_HEADS, QROWS, 2 * BLOCK), BF16),
                   jax.ShapeDtypeStruct((nb, N_KV_HEADS, QROWS, 1), F32)],
        compiler_params=_params(("parallel",)),
    )(p, p, p, p, p, gq, gk, sink_rows, bias)


def _attn_bwd(p, do, gq, gk, probs, sink_probs):
    t = p.shape[0]
    nb = t // BLOCK
    per = STEP_BLOCKS if nb % STEP_BLOCKS == 0 else 1
    bspec = _attn_specs()[2]
    gain = pl.BlockSpec((1, HEAD_DIM), lambda n: (0, 0))
    scale = 1.0 / math.sqrt(HEAD_DIM)

    def head_selectors():
        row = lax.broadcasted_iota(jnp.int32, (KV_W, HEAD_DIM), 0)
        col = lax.broadcasted_iota(jnp.int32, (KV_W, HEAD_DIM), 1)
        return [(row == col + i * HEAD_DIM).astype(BF16) for i in range(N_KV_HEADS)]

    def take_heads(group, sel):
        return jnp.concatenate([_dot(group, s) for s in sel], axis=0)

    def put_heads(x, sel):
        rows = x.shape[0] // len(sel)
        out = _dot(x[:rows].astype(BF16), sel[0], trans_b=True)
        for i in range(1, len(sel)):
            out = out + _dot(x[i * rows:(i + 1) * rows].astype(BF16), sel[i], trans_b=True)
        return out

    def rms(x, g):
        r = lax.rsqrt(jnp.mean(x * x, axis=-1, keepdims=True) + EPS)
        return r, x * r * g

    def rms_bwd(dn, xf, r, g):
        w = dn * g
        dx = r * w - xf * (r * r * r) * jnp.mean(xf * w, axis=-1, keepdims=True)
        return dx, jnp.sum(dn * (xf * r), axis=0, keepdims=True)

    def body(q_ref, kp_ref, kc_ref, vp_ref, vc_ref, do_ref, gq_ref, gk_ref, p_ref, ps_ref,
             dq_ref, dkv_ref, dbias_ref, dsink_ref, dgq_ref, dgk_ref):
        n = pl.program_id(0)
        sel = head_selectors()

        @pl.when(n == 0)
        def _():
            dbias_ref[...] = jnp.zeros_like(dbias_ref)
            dsink_ref[...] = jnp.zeros_like(dsink_ref)
            dgq_ref[...] = jnp.zeros_like(dgq_ref)
            dgk_ref[...] = jnp.zeros_like(dgk_ref)

        dgq_sum = jnp.zeros((1, HEAD_DIM), F32)
        dgk_sum = jnp.zeros((1, HEAD_DIM), F32)
        dk_rows, dv_rows = [[] for _ in range(per)], [[] for _ in range(per)]
        for h in range(N_KV_HEADS):
            kf_all = jnp.concatenate([_dot(kp_ref[...], sel[h]), _dot(kc_ref[...], sel[h])], axis=0)
            rk_all, kn_all = rms(kf_all, gk_ref[...])
            kn_all = kn_all.astype(BF16)
            v_all = jnp.concatenate([_dot(vp_ref[...], sel[h]), _dot(vc_ref[...], sel[h])], axis=0).astype(BF16)
            for sub in range(per):
                rows = pl.ds(sub * BLOCK, BLOCK)
                window = slice(sub * BLOCK, (sub + 2) * BLOCK)
                qf = take_heads(q_ref[rows, pl.ds(h * KV_W, KV_W)], sel)
                rq, qn = rms(qf, gq_ref[...])
                pn_bf16, psink = p_ref[sub, h], ps_ref[sub, h]
                pn = pn_bf16.astype(F32)
                do = take_heads(do_ref[rows, pl.ds(h * KV_W, KV_W)], sel).astype(BF16)
                dv_win = _dot(do, pn_bf16, trans_a=True).T
                dp = _dot(do, v_all[window], trans_b=True)
                delta = jnp.sum(pn * dp, axis=-1, keepdims=True)
                ds = pn * (dp - delta)
                dsc = (ds * scale).astype(BF16)
                dqn = _dot(dsc, kn_all[window])
                dkn = _dot(qn.astype(BF16), dsc, trans_a=True).T
                dq, dgq = rms_bwd(dqn, qf, rq, gq_ref[...])
                dk_win, dgk = rms_bwd(dkn, kf_all[window], rk_all[window], gk_ref[...])
                dq_ref[rows, pl.ds(h * KV_W, KV_W)] = put_heads(dq, sel).astype(BF16)
                dk_rows[sub] += [dk_win[:BLOCK], dk_win[BLOCK:]]
                dv_rows[sub] += [dv_win[:BLOCK], dv_win[BLOCK:]]
                dbias_ref[h] += ds
                dsink_ref[h] += jnp.sum((-psink * delta).reshape(GROUP, BLOCK, 1), axis=1)
                dgq_sum = dgq_sum + dgq
                dgk_sum = dgk_sum + dgk
        for sub in range(per):
            for part in range(2):
                dkv_ref[sub, part, :, pl.ds(0, KV_W)] = put_heads(
                    jnp.concatenate(dk_rows[sub][part::2], axis=0), sel).astype(BF16)
                dkv_ref[sub, part, :, pl.ds(KV_W, KV_W)] = put_heads(
                    jnp.concatenate(dv_rows[sub][part::2], axis=0), sel).astype(BF16)
        dgq_ref[...] += dgq_sum
        dgk_ref[...] += dgk_sum

    def kv_specs(col):
        return [pl.BlockSpec((BLOCK, KV_W), lambda n: (jnp.maximum(n * per - 1, 0), col // KV_W)),
                pl.BlockSpec((per * BLOCK, KV_W), lambda n: (n, col // KV_W))]

    row = pl.BlockSpec((per * BLOCK, Q_W), lambda n: (n, 0))
    return pl.pallas_call(
        body, name="attn_bwd", grid=(nb // per,),
        in_specs=[pl.BlockSpec((per * BLOCK, Q_W), lambda n: (n, COL_Q // Q_W))] + kv_specs(COL_K) + kv_specs(COL_V)
        + [row, gain, gain,
           pl.BlockSpec((per, N_KV_HEADS, QROWS, 2 * BLOCK), lambda n: (n, 0, 0, 0)),
           pl.BlockSpec((per, N_KV_HEADS, QROWS, 1), lambda n: (n, 0, 0, 0))],
        out_specs=[row, pl.BlockSpec((per, 2, BLOCK, 2 * KV_W), lambda n: (n, 0, 0, 0)), bspec,
                   pl.BlockSpec((N_KV_HEADS, GROUP, 1), lambda n: (0, 0, 0)), gain, gain],
        out_shape=[jax.ShapeDtypeStruct((t, Q_W), BF16),
                   jax.ShapeDtypeStruct((nb, 2, BLOCK, 2 * KV_W), BF16),
                   jax.ShapeDtypeStruct((N_KV_HEADS, QROWS, 2 * BLOCK), F32),
                   jax.ShapeDtypeStruct((N_KV_HEADS, GROUP, 1), F32),
                   jax.ShapeDtypeStruct((1, HEAD_DIM), F32),
                   jax.ShapeDtypeStruct((1, HEAD_DIM), F32)],
        compiler_params=_params(("arbitrary",)),
    )(p, p, p, p, p, do, gq, gk, probs, sink_probs)


def _kv_window_sum(parts):
    nb = parts.shape[0]

    def body(cur_ref, nxt_ref, o_ref):
        nxt = jnp.where(pl.program_id(0) < nb - 1, nxt_ref[...].astype(F32), 0.0)
        o_ref[...] = (cur_ref[...].astype(F32) + nxt).astype(BF16)

    blk = (None, None, BLOCK, 2 * KV_W)
    return pl.pallas_call(
        body, name="kv_window_sum", grid=(nb,),
        in_specs=[pl.BlockSpec(blk, lambda n: (n, 1, 0, 0)),
                  pl.BlockSpec(blk, lambda n: (jnp.minimum(n + 1, nb - 1), 0, 0, 0))],
        out_specs=pl.BlockSpec((BLOCK, 2 * KV_W), lambda n: (n, 0)),
        out_shape=jax.ShapeDtypeStruct((nb * BLOCK, 2 * KV_W), BF16),
        compiler_params=_params(("parallel",)),
    )(parts, parts)


GATE_TILE = 512


def _merge_fwd(z3, o, p, w_proj, w_o):
    t, d = z3.shape
    tm = min(ROW_TILE, t)
    tn = GATE_TILE

    def body(z_ref, o_ref, gc_ref, ga_ref, wp_ref, wo_ref, m_ref, a_ref, b_ref):
        a = _dot(z_ref[...], wp_ref[...])
        b = _dot(o_ref[...], wo_ref[...])
        m_ref[...] = (_sigmoid(gc_ref[...].astype(F32)) * a + _sigmoid(ga_ref[...].astype(F32)) * b).astype(BF16)
        a_ref[...] = a.astype(BF16)
        b_ref[...] = b.astype(BF16)

    row = pl.BlockSpec((tm, d), lambda i, j: (i, 0))
    wspec = pl.BlockSpec((d, tn), lambda i, j: (0, j))
    ospec = pl.BlockSpec((tm, tn), lambda i, j: (i, j))
    return pl.pallas_call(
        body, name="merge_fwd", grid=(t // tm, d // tn),
        in_specs=[row, row,
                  pl.BlockSpec((tm, tn), lambda i, j: (i, COL_GC // tn + j)),
                  pl.BlockSpec((tm, tn), lambda i, j: (i, COL_GA // tn + j)), wspec, wspec],
        out_specs=[ospec, ospec, ospec],
        out_shape=[jax.ShapeDtypeStruct((t, d), BF16)] * 3,
        compiler_params=_params(("parallel", "parallel")),
    )(z3, o, p, p, w_proj, w_o)


def _merge_bwd(dres, w_out, a, b, p, tokens=()):
    t, d = dres.shape
    tm = min(ROW_TILE, t)
    tn = GATE_TILE

    def epilogue(acc, ex, outs, ids):
        a_ref, b_ref, gc_ref, ga_ref = ex[:4]
        sc = _sigmoid(gc_ref[...].astype(F32))
        sa = _sigmoid(ga_ref[...].astype(F32))
        outs[0][...] = (acc * sc).astype(BF16)
        outs[1][...] = (acc * sa).astype(BF16)
        outs[2][0] = (acc * a_ref[...].astype(F32) * sc * (1.0 - sc)).astype(BF16)
        outs[2][1] = (acc * b_ref[...].astype(F32) * sa * (1.0 - sa)).astype(BF16)

    ospec = pl.BlockSpec((tm, tn), lambda i, j, kk: (i, j))
    return _mm("merge_bwd", (t // tm, d // tn, 1),
               dres, pl.BlockSpec((tm, d), lambda i, j, kk: (i, 0)),
               w_out, pl.BlockSpec((tn, d), lambda i, j, kk: (j, 0)), (tm, tn),
               trans_b=True, a_pre=_to_bf16,
               extras=(a, b, p, p),
               extra_specs=(ospec, ospec,
                            pl.BlockSpec((tm, tn), lambda i, j, kk: (i, COL_GC // tn + j)),
                            pl.BlockSpec((tm, tn), lambda i, j, kk: (i, COL_GA // tn + j))), tokens=tokens,
               out_shape=(jax.ShapeDtypeStruct((t, d), BF16), jax.ShapeDtypeStruct((t, d), BF16),
                          jax.ShapeDtypeStruct((2, t, d), BF16)),
               out_specs=(ospec, ospec, pl.BlockSpec((2, tm, tn), lambda i, j, kk: (0, i, j))),
               epilogue=epilogue)


def _store_epilogue(acc, ex, outs, ids):
    outs[0][...] = acc


def _store_bf16_epilogue(acc, ex, outs, ids):
    outs[0][...] = acc.astype(BF16)


def _mm_nt(name, a, w, out_dtype=BF16):
    t, n = a.shape
    k = w.shape[0]
    tm = min(ROW_TILE, t)
    return _mm(name, (t // tm, 1, 1), a, pl.BlockSpec((tm, n), lambda i, j, kk: (i, 0)),
               w, pl.BlockSpec((k, n), lambda i, j, kk: (0, 0)), (tm, k), trans_b=True,
               out_shape=(jax.ShapeDtypeStruct((t, k), out_dtype),),
               out_specs=(pl.BlockSpec((tm, k), lambda i, j, kk: (i, 0)),),
               epilogue=_store_bf16_epilogue if out_dtype == BF16 else _store_epilogue)[0]


def _mm_tn(name, a, b, b_pre=None, tokens=()):
    t, m = a.shape
    n = b.shape[1]
    tk = min(TOKEN_TILE // (2 if b.dtype == F32 else 1), t)
    return _mm(name, (1, 1, t // tk), a, pl.BlockSpec((tk, m), lambda i, j, kk: (kk, 0)),
               b, pl.BlockSpec((tk, n), lambda i, j, kk: (kk, 0)), (m, n), trans_a=True, b_pre=b_pre, tokens=tokens,
               out_shape=(jax.ShapeDtypeStruct((m, n), BF16),),
               out_specs=(pl.BlockSpec((m, n), lambda i, j, kk: (0, 0)),), epilogue=_store_bf16_epilogue)[0]


def _local_step(x, target, small, comm):
    t = x.shape[0]
    w = dict(small)

    n1 = _rmsnorm_fwd("ffn1_norm", x, w["ffn1_norm"])
    onehot = _bucket_onehot()
    bias = _bias_table(w["rel_bias"].T, onehot).reshape(N_KV_HEADS, QROWS, 2 * BLOCK)
    sink_rows = jnp.repeat(w["attn_sinks"].reshape(N_KV_HEADS, GROUP), BLOCK, axis=1)[..., None]
    gq_wide = jnp.tile(w["q_norm"], (1, N_KV_HEADS))
    gk_wide = jnp.tile(w["k_norm"], (1, N_KV_HEADS))
    w.update(comm.weights("A", [n1, onehot, bias, sink_rows, gq_wide, gk_wide]))
    (x1, hm), ffn1_saved = _ffn_fwd("ffn1", x, n1, w["ffn1_w_in"], w["ffn1_w_out"], comm.tokens,
                                    next_gain=w["mix_norm"])
    w.update(comm.weights("B", x1))
    tm = min(ROW_TILE, t)
    p = _mm("mix_in", (N_CHIPS, t // tm, 1),
            hm, pl.BlockSpec((tm, D_MODEL), lambda j, i, kk: (i, 0)),
            w["w_in"], pl.BlockSpec((None, D_MODEL, SHARD_W), lambda j, i, kk: (j, 0, 0)), (tm, SHARD_W),
            tokens=comm.tokens,
            out_shape=(jax.ShapeDtypeStruct((t, IN_W), BF16),),
            out_specs=(pl.BlockSpec((tm, SHARD_W), lambda j, i, kk: (i, j)),),
            epilogue=_store_bf16_epilogue)[0]

    z3, z1 = _conv_fwd(p, w["conv_dw_kernel"], w["conv_dw_bias"], w["conv_ln_g"], w["conv_ln_b"])

    o, probs, sink_probs = _attn_fwd(p, gq_wide, gk_wide, sink_rows, bias)

    merged, a, b = _merge_fwd(z3, o, p, w["conv_w_proj"], w["attn_w_o"])
    x2, n2 = _mm_residual("mix_out", merged, w["w_out"], x1, 1.0, next_gain=w["ffn2_norm"])
    w.update(comm.weights("C", n2))
    (dy, loss), ffn2_saved = _ffn_fwd("ffn2", x2, n2, w["ffn2_w_in"], w["ffn2_w_out"], loss_target=target)

    g, big = {}, {}
    dres2, big["ffn2_w_in"], big["ffn2_w_out"], g["ffn2_norm"] = _ffn_bwd(
        "ffn2b", dy, x2, w["ffn2_norm"], ffn2_saved, w["ffn2_w_in"], w["ffn2_w_out"])
    tokens = comm.reduce_start("R1", big, behind=True)

    da, db, dgates = _merge_bwd(dres2, w["w_out"], a, b, p, tokens)
    tokens = comm.exchange_finish("R1", da)
    big = {}
    big["w_out"] = _mm_tn("d_w_out", merged, dres2, b_pre=_to_bf16, tokens=tokens)
    big["conv_w_proj"] = _mm_tn("d_w_proj", z3, da)
    big["attn_w_o"] = _mm_tn("d_w_o", o, db)
    dz3 = _mm_nt("d_z3", da, w["conv_w_proj"])
    do = _mm_nt("d_o", db, w["attn_w_o"])

    dq, dkv_parts, dbias, dsink, g["q_norm"], g["k_norm"] = _attn_bwd(
        p, do, w["q_norm"], w["k_norm"], probs, sink_probs)
    dkv = _kv_window_sum(dkv_parts)
    g["rel_bias"] = _bias_table_bwd(dbias.reshape(N_Q_HEADS, BLOCK * 2 * BLOCK), onehot).T
    g["attn_sinks"] = dsink.reshape(N_Q_HEADS)

    dz1, big["conv_dw_kernel"], g["conv_dw_bias"], g["conv_ln_g"], g["conv_ln_b"] = _conv_bwd_ln(
        p, z1, dz3, w["conv_ln_g"], w["conv_ln_b"])
    dp = _conv_bwd_glu(p, dz1, w["conv_dw_kernel"], dq, dkv, dgates)
    tk = min(TOKEN_TILE, t)
    big["w_in"] = _mm("d_w_in", (1, N_CHIPS, t // tk),
                    hm, pl.BlockSpec((tk, D_MODEL), lambda i, j, kk: (kk, 0)),
                    dp, pl.BlockSpec((tk, SHARD_W), lambda i, j, kk: (kk, j)), (D_MODEL, SHARD_W),
                    trans_a=True,
                    out_shape=(jax.ShapeDtypeStruct((N_CHIPS, D_MODEL, SHARD_W), BF16),),
                    out_specs=(pl.BlockSpec((None, D_MODEL, SHARD_W), lambda i, j, kk: (j, 0, 0)),),
                    epilogue=_store_bf16_epilogue)[0]
    tn = min(NORM_GRAD_TILE, t)
    dres1, g["mix_norm"] = _norm_input_grad(
        "d_mix", dp, [pl.BlockSpec((tn, SHARD_W), functools.partial(lambda i, s: (i, s), s=s)) for s in range(N_CHIPS)],
        w["w_in"], x1, w["mix_norm"], dres2)

    tokens = comm.reduce_finish("R1", dres1, behind=True) + comm.reduce_start("R2", big, behind=True)

    def ffn1_first(du):
        comm.join_finish("R1", du)
        return comm.exchange_finish("R2", du)

    def ffn1_grads(dw_in4, dw_out):
        late = comm.reduce_finish("R2", dw_in4, behind=True)
        return late + comm.reduce_start("R3", {"ffn1_w_in": dw_in4, "ffn1_w_out": dw_out})

    grad_x, _, _, g["ffn1_norm"] = _ffn_bwd(
        "ffn1b", dres1, x, w["ffn1_norm"], ffn1_saved, w["ffn1_w_in"], w["ffn1_w_out"], tokens, ffn1_first, ffn1_grads)
    comm.join_finish("R2", grad_x)
    comm.reduce_finish("R3", grad_x)
    return loss[0, 0], grad_x, g


def _mesh_place():
    x, y, c = lax.axis_index("x"), lax.axis_index("y"), lax.axis_index("c")
    chips = [(1 - x, y), (x, 1 - y), (1 - x, 1 - y)]
    return x, y, c, chips


def _any_specs(n):
    return [pl.BlockSpec(memory_space=pl.ANY)] * n


HBM_SPEC = pl.BlockSpec(memory_space=pltpu.HBM)
SEM_SPEC = pl.BlockSpec(memory_space=pltpu.SEMAPHORE)
EFFECT = pltpu.SideEffectType.DATAFLOW_SIDE_EFFECTING


def _in_hbm(a):
    return pltpu.with_memory_space_constraint(a, pltpu.HBM)


def _copy_start(name, srcs, lands, plan, after=()):
    ns, nb = len(srcs), len(lands)
    n = plan.copies_per_source * ns

    def body(*refs):
        s_refs, l_refs = refs[:ns], refs[ns:ns + nb]
        send_sems, recv_sems = refs[ns + nb + len(after)], refs[ns + nb + len(after) + 1]
        token = refs[-1]
        for k, (src, dst, to, _) in enumerate(plan(s_refs, l_refs)):
            pltpu.make_async_remote_copy(src_ref=src, dst_ref=dst, send_sem=send_sems.at[k], recv_sem=recv_sems.at[k],
                                         device_id=to, device_id_type=MESH).start()
        token[...] = jnp.zeros_like(token)

    bufs = list(srcs) + list(lands)
    outs = pl.pallas_call(
        body, name=name,
        out_shape=(pltpu.SemaphoreType.DMA((n,)), pltpu.SemaphoreType.DMA((n,)),
                   *[pltpu.HBM(a.shape, a.dtype) for a in bufs], jax.ShapeDtypeStruct((8, LANES), F32)),
        in_specs=[HBM_SPEC] * len(bufs) + [pl.BlockSpec(memory_space=pl.ANY)] * len(after),
        out_specs=(SEM_SPEC, SEM_SPEC, *[HBM_SPEC] * len(bufs), pl.BlockSpec(memory_space=pltpu.VMEM)),
        input_output_aliases={i: 2 + i for i in range(len(bufs))},
        compiler_params=pltpu.CompilerParams(has_side_effects=EFFECT),
    )(*[_in_hbm(a) for a in bufs], *after)
    return outs[0], outs[1], list(outs[2:2 + ns]), list(outs[2 + ns:2 + ns + nb]), outs[-1]


def _copy_wait(name, send_sems, recv_sems, srcs, lands, after, plan):
    ns, nb = len(srcs), len(lands)
    after = tuple(after) if isinstance(after, (tuple, list)) else (after,)

    def body(*refs):
        s_refs, l_refs = refs[:ns], refs[ns:ns + nb]
        send_sems, recv_sems = refs[ns + nb], refs[ns + nb + 1]
        for k, (src, _, to, mine) in enumerate(plan(s_refs, l_refs)):
            cp = pltpu.make_async_remote_copy(src_ref=src, dst_ref=mine, send_sem=send_sems.at[k], recv_sem=recv_sems.at[k],
                                              device_id=to, device_id_type=MESH)
            cp.wait_send()
            cp.wait_recv()

    bufs = list(srcs) + list(lands)
    outs = pl.pallas_call(
        body, name=name,
        out_shape=tuple(pltpu.HBM(a.shape, a.dtype) for a in bufs),
        in_specs=[HBM_SPEC] * len(bufs) + [SEM_SPEC, SEM_SPEC] + [pl.BlockSpec(memory_space=pl.ANY)] * len(after),
        out_specs=tuple([HBM_SPEC] * len(bufs)),
        input_output_aliases={i: i for i in range(len(bufs))},
        compiler_params=pltpu.CompilerParams(has_side_effects=EFFECT),
    )(*bufs, send_sems, recv_sems, *after)
    return list(outs[:ns]), list(outs[ns:])


def _gather_plan(s_refs, l_refs):
    x, y, c, chips = _mesh_place()
    jme = 2 * x + y
    return [(s.at[c], land.at[jme, c], (*chip, c), land.at[2 * chip[0] + chip[1], c])
            for s, land in zip(s_refs, l_refs) for chip in chips]


_gather_plan.copies_per_source = 3


def _gather_both_cores_plan(s_refs, l_refs):
    x, y, c, chips = _mesh_place()
    jme = 2 * x + y
    plan = []
    for s, land in zip(s_refs, l_refs):
        for chip in chips:
            for peer_core in (c, 1 - c):
                plan.append((s.at[c], land.at[jme, c], (*chip, peer_core), land.at[2 * chip[0] + chip[1], peer_core]))
        plan.append((s, land.at[jme], (x, y, 1 - c), land.at[jme]))
    return plan


_gather_both_cores_plan.copies_per_source = 7


def _scatter_plan(s_refs, l_refs):
    x, y, c, chips = _mesh_place()
    return [(s.at[2 * chip[0] + chip[1]], land.at[k], (*chip, c), land.at[k])
            for s, land in zip(s_refs, l_refs) for k, chip in enumerate(chips)]


_scatter_plan.copies_per_source = 3


def _exchange_plan(g_refs, l_refs):
    x, y, c, _ = _mesh_place()
    return [(g.at[:, 1 - c], land, (x, y, 1 - c), land) for g, land in zip(g_refs, l_refs)]


_exchange_plan.copies_per_source = 1


def _join_plan(h_refs, l_refs):
    x, y, c, _ = _mesh_place()
    return [(h.at[c], h.at[c], (x, y, 1 - c), h.at[1 - c]) for h in h_refs]


_join_plan.copies_per_source = 1


def _gather_forward(name, shards, landed):
    nw = len(shards)

    def body(*refs):
        s_refs, o_refs = refs[:nw], refs[2 * nw:3 * nw]
        send_sems, recv_sems = refs[3 * nw:]
        x, y, c, chips = _mesh_place()
        me, sib, jme = (x, y, c), (x, y, 1 - c), 2 * x + y
        sent = []
        for w in range(nw):
            parts = [(o_refs[w].at[2 * chip[0] + chip[1], c], o_refs[w].at[2 * chip[0] + chip[1], c]) for chip in chips]
            parts.append((s_refs[w], o_refs[w].at[jme]))
            for k, (src, dst) in enumerate(parts):
                cp = pltpu.make_async_remote_copy(src_ref=src, dst_ref=dst, send_sem=send_sems.at[4 * w + k],
                                                  recv_sem=recv_sems.at[4 * w + k], device_id=sib, device_id_type=MESH)
                cp.start()
                sent.append(cp)
        for w in range(nw):
            parts = [o_refs[w].at[2 * chip[0] + chip[1], 1 - c] for chip in chips] + [o_refs[w].at[jme]]
            for k, part in enumerate(parts):
                pltpu.make_async_remote_copy(src_ref=part, dst_ref=part, send_sem=send_sems.at[4 * w + k],
                                             recv_sem=recv_sems.at[4 * w + k], device_id=me, device_id_type=MESH).wait_recv()
        for cp in sent:
            cp.wait_send()

    return pl.pallas_call(
        body, name=name,
        in_specs=_any_specs(2 * nw), out_specs=_any_specs(nw),
        out_shape=[jax.ShapeDtypeStruct(a.shape, a.dtype) for a in landed],
        input_output_aliases={nw + i: i for i in range(nw)},
        scratch_shapes=[pltpu.SemaphoreType.DMA((4 * nw,)), pltpu.SemaphoreType.DMA((4 * nw,))],
    )(*shards, *landed)


def _exchange_halves(name, grads, after=()):
    nw = len(grads)

    def body(*refs):
        g_refs, o_refs = refs[:nw], refs[nw + len(after):2 * nw + len(after)]
        send_sems, recv_sems = refs[2 * nw + len(after):]
        x, y, c, _ = _mesh_place()
        copies = []
        for w in range(nw):
            cp = pltpu.make_async_remote_copy(src_ref=g_refs[w].at[:, 1 - c], dst_ref=o_refs[w], send_sem=send_sems.at[w],
                                              recv_sem=recv_sems.at[w], device_id=(x, y, 1 - c), device_id_type=MESH)
            cp.start()
            copies.append(cp)
        for cp in copies:
            cp.wait()

    return pl.pallas_call(
        body, name=name,
        in_specs=_any_specs(nw + len(after)), out_specs=_any_specs(nw),
        out_shape=[jax.ShapeDtypeStruct((N_CHIPS,) + g.shape[2:], g.dtype) for g in grads],
        scratch_shapes=[pltpu.SemaphoreType.DMA((nw,)), pltpu.SemaphoreType.DMA((nw,))],
    )(*grads, *after)


ELEMENTWISE_ROWS = 512


def _row_tile(r):
    for cand in range(min(r, ELEMENTWISE_ROWS) // 16 * 16, 0, -16):
        if r % cand == 0:
            return cand
    return r


def _add_own_half(c_idx, grad, got):
    _, _, r, cols = grad.shape
    tr = _row_tile(r)

    def body(c_ref, g_ref, o_ref, out_ref):
        out_ref[...] = (g_ref[...].astype(F32) + o_ref[...].astype(F32)).astype(BF16)

    return pl.pallas_call(
        body, name="add_own_half",
        grid_spec=pltpu.PrefetchScalarGridSpec(
            num_scalar_prefetch=1, grid=(N_CHIPS, r // tr),
            in_specs=[pl.BlockSpec((None, None, tr, cols), lambda j, i, c_ref: (j, c_ref[0], i, 0)),
                      pl.BlockSpec((None, tr, cols), lambda j, i, c_ref: (j, i, 0))],
            out_specs=pl.BlockSpec((None, tr, cols), lambda j, i, c_ref: (j, i, 0))),
        out_shape=jax.ShapeDtypeStruct((N_CHIPS, r, cols), BF16),
        compiler_params=_params(("parallel", "parallel")),
    )(c_idx, grad, got)


def _sum_pieces(place_idx, sums, landed):
    _, r, cols = sums.shape
    tr = _row_tile(r)

    def body(j_ref, own_ref, p_ref, o_ref):
        o_ref[...] = ((own_ref[...].astype(F32) + p_ref[0].astype(F32)) + p_ref[1].astype(F32)) + p_ref[2].astype(F32)

    return pl.pallas_call(
        body, name="sum_pieces",
        grid_spec=pltpu.PrefetchScalarGridSpec(
            num_scalar_prefetch=1, grid=(r // tr,),
            in_specs=[pl.BlockSpec((None, tr, cols), lambda i, j_ref: (j_ref[0], i, 0)),
                      pl.BlockSpec((N_CHIPS - 1, tr, cols), lambda i, j_ref: (0, i, 0))],
            out_specs=pl.BlockSpec((None, tr, cols), lambda i, j_ref: (j_ref[1], i, 0))),
        out_shape=jax.ShapeDtypeStruct((2, r, cols), F32),
        compiler_params=_params(("parallel",)),
    )(place_idx, sums, landed)


def _join_halves(name, halves):
    nw = len(halves)

    def body(*refs):
        o_refs = refs[nw:2 * nw]
        send_sems, recv_sems = refs[2 * nw:]
        x, y, c, _ = _mesh_place()
        copies = []
        for w in range(nw):
            cp = pltpu.make_async_remote_copy(src_ref=o_refs[w].at[c], dst_ref=o_refs[w].at[c], send_sem=send_sems.at[w],
                                              recv_sem=recv_sems.at[w], device_id=(x, y, 1 - c), device_id_type=MESH)
            cp.start()
            copies.append(cp)
        for w in range(nw):
            copies[w].wait_send()
            landed = o_refs[w].at[1 - c]
            pltpu.make_async_remote_copy(src_ref=landed, dst_ref=landed, send_sem=send_sems.at[w], recv_sem=recv_sems.at[w],
                                         device_id=(x, y, c), device_id_type=MESH).wait_recv()

    return pl.pallas_call(
        body, name=name,
        in_specs=_any_specs(nw), out_specs=_any_specs(nw),
        out_shape=[jax.ShapeDtypeStruct(h.shape, F32) for h in halves],
        input_output_aliases={i: i for i in range(nw)},
        scratch_shapes=[pltpu.SemaphoreType.DMA((nw,)), pltpu.SemaphoreType.DMA((nw,))],
    )(*halves)


SMALL_ROWS = 8


def _all_reduce_small(pack):
    rows, cols = pack.shape
    n_dev = 8

    def body(p_ref, o_ref, slots, send_sems, recv_sems):
        x, y, c, _ = _mesh_place()
        me = 4 * x + 2 * y + c
        slots[me] = p_ref[...]
        copies = []
        for k in range(1, n_dev):
            peer = (me + k) % n_dev
            cp = pltpu.make_async_remote_copy(src_ref=p_ref, dst_ref=slots.at[me], send_sem=send_sems.at[k],
                                              recv_sem=recv_sems.at[k],
                                              device_id=(peer // 4, (peer // 2) % 2, peer % 2), device_id_type=MESH)
            cp.start()
            copies.append(cp)
        for k in range(1, n_dev):
            src = (me + n_dev - k) % n_dev
            pltpu.make_async_remote_copy(src_ref=p_ref, dst_ref=slots.at[src], send_sem=send_sems.at[k],
                                         recv_sem=recv_sems.at[k], device_id=(x, y, c), device_id_type=MESH).wait_recv()
        for cp in copies:
            cp.wait_send()
        total = slots[0]
        for s in range(1, n_dev):
            total = total + slots[s]
        o_ref[...] = total

    return pl.pallas_call(
        body, name="all_reduce_small",
        in_specs=[pl.BlockSpec(memory_space=pltpu.VMEM)], out_specs=pl.BlockSpec(memory_space=pltpu.VMEM),
        out_shape=jax.ShapeDtypeStruct((rows, cols), F32),
        scratch_shapes=[pltpu.VMEM((n_dev, rows, cols), F32), pltpu.SemaphoreType.DMA((n_dev,)),
                        pltpu.SemaphoreType.DMA((n_dev,))],
    )(pack)


def _adamw(name, w, g, m, v):
    r, cols = w.shape
    tr = _row_tile(r)

    def body(w_ref, g_ref, m_ref, v_ref, d_ref, nm_ref, nv_ref):
        gv = g_ref[...]
        nm = ADAM_B1 * m_ref[...] + (1.0 - ADAM_B1) * gv
        nv = ADAM_B2 * v_ref[...] + (1.0 - ADAM_B2) * (gv * gv)
        m_hat = nm / (1.0 - ADAM_B1 ** ADAM_STEP)
        v_hat = nv / (1.0 - ADAM_B2 ** ADAM_STEP)
        d_ref[...] = -ADAM_LR * (m_hat / (jnp.sqrt(v_hat) + ADAM_EPS) + ADAM_WD * w_ref[...])
        nm_ref[...] = nm
        nv_ref[...] = nv

    spec = pl.BlockSpec((tr, cols), lambda i: (i, 0))
    return pl.pallas_call(
        body, name=name, grid=(r // tr,),
        in_specs=[spec] * 4, out_specs=[spec] * 3,
        out_shape=[jax.ShapeDtypeStruct((r, cols), F32)] * 3,
        compiler_params=_params(("parallel",)),
    )(w, g, m, v)


BIG = ["ffn1_w_in", "ffn1_w_out", "w_in", "conv_w_proj", "attn_w_o", "w_out", "ffn2_w_in", "ffn2_w_out", "conv_dw_kernel"]
COL_SHARDED = ("ffn1_w_in", "w_in", "ffn2_w_in")
SMALL = ["ffn1_norm", "mix_norm", "ffn2_norm", "conv_dw_bias", "conv_ln_g", "conv_ln_b", "q_norm", "k_norm", "attn_sinks", "rel_bias"]
WEIGHTS = ["ffn1_norm", "ffn1_w_in", "ffn1_w_out", "mix_norm", "w_in", "conv_dw_kernel", "conv_dw_bias", "conv_ln_g",
           "conv_ln_b", "conv_w_proj", "q_norm", "k_norm", "attn_sinks", "rel_bias", "attn_w_o", "w_out", "ffn2_norm",
           "ffn2_w_in", "ffn2_w_out"]
SMALL_PLACE = {"ffn1_norm": (0, 0, 1024), "mix_norm": (1, 0, 1024), "ffn2_norm": (2, 0, 1024), "conv_dw_bias": (3, 0, 1024),
               "conv_ln_g": (4, 0, 1024), "conv_ln_b": (5, 0, 1024), "q_norm": (6, 0, 64), "k_norm": (6, 128, 64),
               "attn_sinks": (6, 256, 16), "rel_bias": (7, 0, 512)}
LOSS_PLACE = (6, 384)


def _pack_small(vals, fill=0.0, loss=None):
    pack = jnp.full((SMALL_ROWS, D_MODEL), fill, F32)
    for name, (row, lane, n) in SMALL_PLACE.items():
        pack = pack.at[row, lane:lane + n].set(vals[name].reshape(n))
    if loss is not None:
        pack = pack.at[LOSS_PLACE[0], LOSS_PLACE[1]].set(loss)
    return pack


def _unpack_small(pack, shapes):
    return {name: pack[row, lane:lane + n].reshape(shapes[name]) for name, (row, lane, n) in SMALL_PLACE.items()}


def _shard_halves(name, a):
    if name == "conv_dw_kernel":
        a = jnp.pad(a, ((0, CONV_PAD - CONV_WIDTH), (0, 0)))
    r, cols = a.shape
    return a.reshape(2, r // 2, cols)


GATHER_GROUPS = {"A": ["ffn1_w_in", "ffn1_w_out"],
                 "B": ["w_in", "conv_dw_kernel", "conv_w_proj", "attn_w_o", "w_out"],
                 "C": ["ffn2_w_in", "ffn2_w_out"]}


class _MeshComm:
    def __init__(self, wts, idle_work=()):
        self.idle_work = idle_work
        self.c_idx = lax.axis_index("c").astype(jnp.int32).reshape(1)
        self.place_idx = jnp.stack([2 * lax.axis_index("x") + lax.axis_index("y"), lax.axis_index("c")]).astype(jnp.int32)
        self.gathers, self.exchanges, self.reductions, self.joins, self.reduced = {}, {}, {}, {}, {}
        self.tokens = ()
        self.shards = {n: _shard_halves(n, wts[n]) if n == "conv_dw_kernel" else _shard_halves(n, wts[n]).astype(BF16)
                       for n in BIG}
        self._gather_start("A", ())

    def _gather_start(self, group, after):
        shards = [self.shards[n] for n in GATHER_GROUPS[group]]
        lands = [lax.empty((N_CHIPS,) + s.shape, s.dtype) for s in shards]
        self.gathers[group] = _copy_start("gather_start_" + group, shards, lands, self._gather_plan(group), after=after)
        self.tokens = (self.gathers[group][-1],)

    @staticmethod
    def _gather_plan(group):
        return _gather_both_cores_plan if group == "C" else _gather_plan

    def weights(self, group, after):
        send_sems, recv_sems, shards, lands, token = self.gathers.pop(group)
        after = [token] if after is None else list(after) if isinstance(after, (list, tuple)) else [after]
        if group == "A":
            after += [self.shards[n] for g in ("B", "C") for n in GATHER_GROUPS[g]] + list(self.idle_work)
        shards, lands = _copy_wait("gather_wait_" + group, send_sems, recv_sems, shards, lands, after,
                                   self._gather_plan(group))
        gathered = lands if group == "C" else _gather_forward("gather_forward_" + group, shards, lands)
        self.tokens = ()
        following = {"A": "B", "B": "C"}.get(group)
        if following:
            self._gather_start(following, (gathered[0],))
        out = {}
        for n, g4 in zip(GATHER_GROUPS[group], gathered):
            r, cols = g4.shape[2] * 2, g4.shape[3]
            if n in COL_SHARDED:
                out[n] = g4.reshape(N_CHIPS, r, cols)
            elif n == "conv_dw_kernel":
                out[n] = g4.reshape(N_CHIPS, r, cols).transpose(1, 0, 2).reshape(r, N_CHIPS * cols)
            else:
                out[n] = g4.reshape(N_CHIPS * r, cols)
        return out

    def reduce_start(self, group, grads, behind=False):
        names = list(grads)
        g4 = []
        for n in names:
            a = grads[n]
            if n == "conv_dw_kernel":
                a = a.reshape(CONV_PAD, N_CHIPS, -1).transpose(1, 0, 2)
            elif n not in COL_SHARDED:
                a = a.reshape(N_CHIPS, a.shape[0] // N_CHIPS, a.shape[1])
            g4.append(a.reshape(N_CHIPS, 2, a.shape[1] // 2, a.shape[2]))
        if behind:
            lands = [lax.empty((N_CHIPS,) + g.shape[2:], g.dtype) for g in g4]
            started = _copy_start("exchange_start_" + group, g4, lands, _exchange_plan)
            self.exchanges[group] = (names,) + started
            return (started[-1],)
        return self._scatter_start(group, names, g4, _exchange_halves("exchange_halves_" + group, g4))

    def exchange_finish(self, group, after):
        names, send_sems, recv_sems, g4, lands, _ = self.exchanges.pop(group)
        g4, got = _copy_wait("exchange_wait_" + group, send_sems, recv_sems, g4, lands, after, _exchange_plan)
        return self._scatter_start(group, names, g4, got)

    def _scatter_start(self, group, names, g4, got):
        sums = [_add_own_half(self.c_idx, a, b) for a, b in zip(g4, got)]
        lands = [lax.empty((N_CHIPS - 1,) + s.shape[1:], s.dtype) for s in sums]
        started = _copy_start("scatter_start_" + group, sums, lands, _scatter_plan)
        self.reductions[group] = (names,) + started
        return (started[-1],)

    def reduce_finish(self, group, after, behind=False):
        names, send_sems, recv_sems, sums, lands, _ = self.reductions.pop(group)
        sums, lands = _copy_wait("scatter_wait_" + group, send_sems, recv_sems, sums, lands, after, _scatter_plan)
        halves = [_sum_pieces(self.place_idx, s, p) for s, p in zip(sums, lands)]
        if behind:
            started = _copy_start("join_start_" + group, halves, [], _join_plan)
            self.joins[group] = (names,) + started
            return (started[-1],)
        self.reduced.update(zip(names, _join_halves("join_halves_" + group, halves)))
        return ()

    def join_finish(self, group, after):
        names, send_sems, recv_sems, halves, _, _ = self.joins.pop(group)
        self.reduced.update(zip(names, _copy_wait("join_wait_" + group, send_sems, recv_sems, halves, [], after, _join_plan)[0]))


def kernel(x, ffn1_norm, ffn1_w_in, ffn1_w_out, mix_norm, w_in, conv_dw_kernel, conv_dw_bias, conv_ln_g, conv_ln_b, conv_w_proj, q_norm, k_norm, attn_sinks, rel_bias, attn_w_o, w_out, ffn2_norm, ffn2_w_in, ffn2_w_out, loss_target, m_ffn1_norm, m_ffn1_w_in, m_ffn1_w_out, m_mix_norm, m_w_in, m_conv_dw_kernel, m_conv_dw_bias, m_conv_ln_g, m_conv_ln_b, m_conv_w_proj, m_q_norm, m_k_norm, m_attn_sinks, m_rel_bias, m_attn_w_o, m_w_out, m_ffn2_norm, m_ffn2_w_in, m_ffn2_w_out, v_ffn1_norm, v_ffn1_w_in, v_ffn1_w_out, v_mix_norm, v_w_in, v_conv_dw_kernel, v_conv_dw_bias, v_conv_ln_g, v_conv_ln_b, v_conv_w_proj, v_q_norm, v_k_norm, v_attn_sinks, v_rel_bias, v_attn_w_o, v_w_out, v_ffn2_norm, v_ffn2_w_in, v_ffn2_w_out):
    args = dict(locals())
    wts = {n: args[n] for n in WEIGHTS}
    mom = {n: args["m_" + n] for n in WEIGHTS}
    var = {n: args["v_" + n] for n in WEIGHTS}
    small_packs = [_pack_small(wts), _pack_small(mom), _pack_small(var, fill=1.0)]
    dw_moments = [_shard_halves("conv_dw_kernel", a) for a in (wts["conv_dw_kernel"], mom["conv_dw_kernel"], var["conv_dw_kernel"])]
    comm = _MeshComm(wts, idle_work=small_packs + dw_moments)
    small = {n: wts[n] if n in ("attn_sinks", "rel_bias") else wts[n].reshape(1, -1) for n in SMALL}
    loss_part, grad_x, g = _local_step(x[0], loss_target[0], small, comm)

    small_sum = _all_reduce_small(_pack_small(g, loss=loss_part))
    loss = small_sum[LOSS_PLACE[0], LOSS_PLACE[1]]
    small_shapes = {n: wts[n].shape for n in SMALL}
    g_small = _unpack_small(small_sum, small_shapes)

    grads, delta, new_m, new_v = {}, {}, {}, {}
    for n in BIG:
        j = comm.reduced[n]
        gs = j.reshape(j.shape[1] * 2, j.shape[2])
        pad = n == "conv_dw_kernel"
        ws, ms, vs = (a.reshape(gs.shape) for a in (dw_moments if pad else (wts[n], mom[n], var[n])))
        d, nm, nv = _adamw("adamw_" + n, ws, gs, ms, vs)
        cut = (lambda a: a[:CONV_WIDTH]) if pad else (lambda a: a)
        grads[n], delta[n], new_m[n], new_v[n] = cut(gs), cut(d), cut(nm), cut(nv)
    d, nm, nv = _adamw("adamw_small", small_packs[0], small_sum, small_packs[1], small_packs[2])
    grads.update(g_small)
    delta.update(_unpack_small(d, small_shapes))
    new_m.update(_unpack_small(nm, small_shapes))
    new_v.update(_unpack_small(nv, small_shapes))

    return (loss, grad_x[None], *[grads[n] for n in WEIGHTS], *[delta[n] for n in WEIGHTS],
            *[new_m[n] for n in WEIGHTS], *[new_v[n] for n in WEIGHTS])
```

```python
import functools
import math

import jax
import jax.numpy as jnp
from jax import lax
from jax.experimental import pallas as pl
from jax.experimental.pallas import tpu as pltpu

F32 = jnp.float32
BF16 = jnp.bfloat16
MESH = pl.DeviceIdType.MESH

EPS = 1e-6
D_MODEL = 1024
D_FF = 2816
N_CHIPS = 4
SHARD_W = 2 * D_FF // N_CHIPS
HEAD_DIM = 64
N_Q_HEADS = 16
N_KV_HEADS = 4
GROUP = N_Q_HEADS // N_KV_HEADS
BLOCK = 128
QROWS = GROUP * BLOCK
N_BUCKETS = 32
MAX_DISTANCE = 128
CONV_WIDTH = 31
CONV_PAD = 32
NEG = float(jnp.finfo(jnp.float32).min)

ADAM_LR = 0.001
ADAM_B1 = 0.9
ADAM_B2 = 0.999
ADAM_EPS = 1e-08
ADAM_WD = 0.01
ADAM_STEP = 10

VMEM_LIMIT_BYTES = 56 * 1024 * 1024
ROW_TILE = 1024
TOKEN_TILE = 4096
CONV_TILE = 256
CONV_ROWS = 128
LANES = 128

COL_CONV_A, COL_CONV_G, COL_Q, COL_K, COL_V, COL_GC, COL_GA = 0, 1024, 2048, 3072, 3328, 3584, 4608
IN_W = 5632


def _params(sem, vmem=VMEM_LIMIT_BYTES):
    return pltpu.CompilerParams(dimension_semantics=sem, vmem_limit_bytes=vmem)


def _sigmoid(x):
    return 1.0 / (1.0 + jnp.exp(-x))


def _dot(a, b, trans_a=False, trans_b=False, precision=None):
    dn = (((0,) if trans_a else (1,), (1,) if trans_b else (0,)), ((), ()))
    return lax.dot_general(a, b, dn, preferred_element_type=F32, precision=precision)


def _mm(name, grid, a, a_spec, b, b_spec, acc_shape, *, trans_a=False, trans_b=False, a_pre=None, b_pre=None,
        extras=(), extra_specs=(), tokens=(), out_shape, out_specs, epilogue,
        sem=("parallel", "parallel", "arbitrary")):
    n_k = grid[2]
    extras = tuple(extras) + tuple(tokens)
    extra_specs = tuple(extra_specs) + (pl.BlockSpec((8, LANES), lambda i, j, kk: (0, 0)),) * len(tokens)
    n_extra = len(extras)
    n_out = len(out_shape)

    def body(a_ref, b_ref, *rest):
        ex = rest[:n_extra]
        outs = rest[n_extra:n_extra + n_out]
        ids = (pl.program_id(0), pl.program_id(1), pl.program_id(2))
        av = a_ref[...]
        if a_pre is not None:
            av = a_pre(av)
        bv = b_ref[...]
        if b_pre is not None:
            bv = b_pre(bv)
        if n_k == 1:
            epilogue(_dot(av, bv, trans_a, trans_b), ex, outs, ids)
        else:
            acc = rest[-1]

            @pl.when(ids[2] == 0)
            def _():
                acc[...] = jnp.zeros_like(acc)

            acc[...] += _dot(av, bv, trans_a, trans_b)

            @pl.when(ids[2] == n_k - 1)
            def _():
                epilogue(acc[...], ex, outs, ids)

    scratch = [] if n_k == 1 else [pltpu.VMEM(acc_shape, F32)]
    return pl.pallas_call(
        body, name=name, grid=grid,
        in_specs=[a_spec, b_spec, *extra_specs],
        out_specs=list(out_specs), out_shape=list(out_shape),
        scratch_shapes=scratch, compiler_params=_params(sem),
    )(a, b, *extras)


def _half_bf16(v):
    return (0.5 * v).astype(BF16)


def _to_bf16(v):
    return v.astype(BF16)


def _rmsnorm_fwd(name, x, g, tokens=()):
    t, d = x.shape
    tm = min(ROW_TILE, t)

    def body(x_ref, g_ref, *rest):
        o_ref = rest[-1]
        xv = x_ref[...]
        r = lax.rsqrt(jnp.mean(xv * xv, axis=-1, keepdims=True) + EPS)
        o_ref[...] = (xv * r * g_ref[...]).astype(BF16)

    return pl.pallas_call(
        body, name=name, grid=(t // tm,),
        in_specs=[pl.BlockSpec((tm, d), lambda i: (i, 0)), pl.BlockSpec((1, d), lambda i: (0, 0))]
        + [pl.BlockSpec((8, LANES), lambda i: (0, 0))] * len(tokens),
        out_specs=pl.BlockSpec((tm, d), lambda i: (i, 0)),
        out_shape=jax.ShapeDtypeStruct((t, d), BF16),
        compiler_params=_params(("parallel",)),
    )(x, g, *tokens)


def _ffn_in(name, n, w_in4, tokens=()):
    t, d = n.shape
    tm = min(ROW_TILE, t)

    def body(n_ref, wa_ref, wb_ref, *rest):
        ab_ref, h_ref = rest[-2:]
        nv = n_ref[...]
        a = _dot(nv, wa_ref[...])
        b = _dot(nv, wb_ref[...])
        h_ref[...] = (a * _sigmoid(a) * b).astype(BF16)
        ab_ref[0] = a.astype(BF16)
        ab_ref[1] = b.astype(BF16)

    return pl.pallas_call(
        body, name=name, grid=(2, t // tm),
        in_specs=[pl.BlockSpec((tm, d), lambda j, i: (i, 0)),
                  pl.BlockSpec((None, d, SHARD_W), lambda j, i: (j, 0, 0)),
                  pl.BlockSpec((None, d, SHARD_W), lambda j, i: (j + 2, 0, 0))]
        + [pl.BlockSpec((8, LANES), lambda j, i: (0, 0))] * len(tokens),
        out_specs=[pl.BlockSpec((2, tm, SHARD_W), lambda j, i: (0, i, j)),
                   pl.BlockSpec((tm, SHARD_W), lambda j, i: (i, j))],
        out_shape=[jax.ShapeDtypeStruct((2, t, D_FF), BF16), jax.ShapeDtypeStruct((t, D_FF), BF16)],
        compiler_params=_params(("parallel", "parallel")),
    )(n, w_in4, w_in4, *tokens)


def _mm_residual(name, a, w, res, scale, next_gain=None, loss_target=None):
    t, k = a.shape
    n = w.shape[1]
    tm = min(ROW_TILE, t)
    row = pl.BlockSpec((tm, n), lambda i, j, kk: (i, 0))
    extras, specs = [res], [row]
    shapes, out_specs = [jax.ShapeDtypeStruct((t, n), F32)], [row]
    if next_gain is not None:
        extras.append(next_gain)
        specs.append(pl.BlockSpec((1, n), lambda i, j, kk: (0, 0)))
        shapes.append(jax.ShapeDtypeStruct((t, n), BF16))
        out_specs.append(row)
    if loss_target is not None:
        extras.append(loss_target)
        specs.append(row)
        shapes.append(jax.ShapeDtypeStruct((8, LANES), F32))
        out_specs.append(pl.BlockSpec((8, LANES), lambda i, j, kk: (0, 0)))

    def epilogue(acc, ex, outs, ids):
        y = ex[0][...] + scale * acc
        if loss_target is None:
            outs[0][...] = y
        if next_gain is not None:
            r = lax.rsqrt(jnp.mean(y * y, axis=-1, keepdims=True) + EPS)
            outs[1][...] = (y * r * ex[1][...]).astype(BF16)
        if loss_target is not None:
            diff = y - ex[1][...]
            outs[0][...] = diff * (1.0 / n)
            part = jnp.full((8, LANES), 0.5 / n * jnp.sum(diff * diff), F32)

            @pl.when(ids[0] == 0)
            def _():
                outs[1][...] = part

            @pl.when(ids[0] > 0)
            def _():
                outs[1][...] += part

    sem = ("parallel" if loss_target is None else "arbitrary", "parallel", "arbitrary")
    out = _mm(name, (t // tm, 1, 1), a, pl.BlockSpec((tm, k), lambda i, j, kk: (i, 0)),
              w, pl.BlockSpec((k, n), lambda i, j, kk: (0, 0)), (tm, n),
              extras=extras, extra_specs=specs, out_shape=shapes, out_specs=out_specs, epilogue=epilogue, sem=sem)
    return out[0] if len(out) == 1 else tuple(out)


def _ffn_fwd(tag, x, n, w_in4, w_out, tokens=(), **tail):
    ab, h = _ffn_in(tag + "_in", n, w_in4, tokens)
    y = _mm_residual(tag + "_out", h, w_out, x, 0.5, **tail)
    return y, (n, ab, h)


def _ffn_bwd(tag, dres, x, g, saved, w_in4, w_out, tokens=(), on_first=None, on_weight_grads=None):
    n, ab, h = saved
    t, d = x.shape
    tm = min(ROW_TILE, t)
    tk = min(TOKEN_TILE, t)
    half_w = SHARD_W

    def dact_epilogue(acc, ex, outs, ids):
        a = ex[0][0].astype(F32)
        b = ex[0][1].astype(F32)
        sig = _sigmoid(a)
        outs[0][0] = (acc * b * (sig * (1.0 + a * (1.0 - sig)))).astype(BF16)
        outs[0][1] = (acc * (a * sig)).astype(BF16)

    du = _mm(tag + "_dact", (2, t // tm, 1),
             dres, pl.BlockSpec((tm, d), lambda j, i, kk: (i, 0)),
             w_out, pl.BlockSpec((half_w, d), lambda j, i, kk: (j, 0)), (tm, half_w),
             trans_b=True, a_pre=_half_bf16,
             extras=(ab,), extra_specs=(pl.BlockSpec((2, tm, half_w), lambda j, i, kk: (0, i, j)),), tokens=tokens,
             out_shape=(jax.ShapeDtypeStruct((2, t, D_FF), BF16),),
             out_specs=(pl.BlockSpec((2, tm, half_w), lambda j, i, kk: (0, i, j)),),
             epilogue=dact_epilogue)[0]

    def store_epilogue(acc, ex, outs, ids):
        outs[0][...] = acc.astype(BF16)

    early = () if on_first is None else on_first(du)

    tk_out = min(TOKEN_TILE // 2, t)
    dw_out = _mm(tag + "_dwout", (2, 1, t // tk_out),
                 h, pl.BlockSpec((tk_out, half_w), lambda i, j, kk: (kk, i)),
                 dres, pl.BlockSpec((tk_out, d), lambda i, j, kk: (kk, 0)), (half_w, d),
                 trans_a=True, b_pre=_half_bf16, tokens=early,
                 out_shape=(jax.ShapeDtypeStruct((D_FF, d), BF16),),
                 out_specs=(pl.BlockSpec((half_w, d), lambda i, j, kk: (i, 0)),),
                 epilogue=store_epilogue)[0]

    dw_in4 = _mm(tag + "_dwin", (1, N_CHIPS, t // tk),
                 n, pl.BlockSpec((tk, d), lambda i, j, kk: (kk, 0)),
                 du, pl.BlockSpec((None, tk, SHARD_W), lambda i, j, kk: (j // 2, kk, j % 2)), (d, SHARD_W),
                 trans_a=True,
                 out_shape=(jax.ShapeDtypeStruct((N_CHIPS, d, SHARD_W), BF16),),
                 out_specs=(pl.BlockSpec((None, d, SHARD_W), lambda i, j, kk: (j, 0, 0)),),
                 epilogue=store_epilogue)[0]

    late = () if on_weight_grads is None else on_weight_grads(dw_in4, dw_out)

    tn = min(NORM_GRAD_TILE, t)
    shard_specs = [pl.BlockSpec((None, tn, SHARD_W), functools.partial(lambda i, s: (s // 2, i, s % 2), s=s))
                   for s in range(N_CHIPS)]
    dx, dg = _norm_input_grad(tag + "_dn", du, shard_specs, w_in4, x, g, dres, late)
    return dx, dw_in4, dw_out, dg


NORM_GRAD_TILE = 512


def _norm_input_grad(name, a, shard_specs, w4, x, g, dres, tokens=()):
    t, d = x.shape
    tn = min(NORM_GRAD_TILE, t)
    ns = len(shard_specs)

    def body(*refs):
        a_refs, (w_ref, x_ref, g_ref, dres_ref) = refs[:ns], refs[ns:ns + 4]
        out_ref, dg_ref = refs[-2:]
        acc = _dot(a_refs[0][...], w_ref[0], trans_b=True)
        for s in range(1, ns):
            acc = acc + _dot(a_refs[s][...], w_ref[s], trans_b=True)
        xv = x_ref[...]
        r = lax.rsqrt(jnp.mean(xv * xv, axis=-1, keepdims=True) + EPS)
        w = acc * g_ref[...]
        out_ref[...] = dres_ref[...] + (r * w - xv * (r * r * r) * jnp.mean(xv * w, axis=-1, keepdims=True))
        part = jnp.sum(acc * (xv * r), axis=0, keepdims=True)
        i = pl.program_id(0)

        @pl.when(i == 0)
        def _():
            dg_ref[...] = part

        @pl.when(i > 0)
        def _():
            dg_ref[...] += part

    row = pl.BlockSpec((tn, d), lambda i: (i, 0))
    vec = pl.BlockSpec((1, d), lambda i: (0, 0))
    return pl.pallas_call(
        body, name=name, grid=(t // tn,),
        in_specs=list(shard_specs) + [pl.BlockSpec(w4.shape, lambda i: (0, 0, 0)), row, vec, row]
        + [pl.BlockSpec((8, LANES), lambda i: (0, 0))] * len(tokens),
        out_specs=[row, vec],
        out_shape=[jax.ShapeDtypeStruct((t, d), F32), jax.ShapeDtypeStruct((1, d), F32)],
        compiler_params=_params(("arbitrary",)),
    )(*[a] * ns, w4, x, g, dres, *tokens)


def _conv_fill(zp_ref, a_ref, g_ref, ah_ref, gh_ref, i):
    zh = ah_ref[...].astype(F32) * _sigmoid(gh_ref[...].astype(F32))
    zp_ref[pl.ds(0, CONV_PAD), :] = jnp.where(i > 0, zh, 0.0)
    zp_ref[pl.ds(CONV_PAD, a_ref.shape[0]), :] = a_ref[...].astype(F32) * _sigmoid(g_ref[...].astype(F32))


def _shift_groups(shifts):
    groups = {}
    for j, s in shifts:
        groups.setdefault(s % 8, []).append((j, s // 8))
    return groups


def _windows(zp_ref, r0, lanes, groups):
    for q, taps in groups.items():
        deepest = max(p for _, p in taps)
        win = zp_ref[pl.ds(r0 + q, 8 * deepest + CONV_ROWS), lanes]
        for j, p in taps:
            yield j, win[8 * p:8 * p + CONV_ROWS]


def _conv_apply(zp_ref, out_ref, dw_ref, bias_ref, tm, ch, shifts):
    groups = _shift_groups(shifts)
    for cc in range(ch // LANES):
        lanes = pl.ds(cc * LANES, LANES)
        w = [dw_ref[pl.ds(j, 1), lanes] for j in range(CONV_WIDTH)]
        for r0 in range(0, tm, CONV_ROWS):
            if bias_ref is None:
                acc = jnp.zeros((CONV_ROWS, LANES), F32)
            else:
                acc = jnp.broadcast_to(bias_ref[:, lanes], (CONV_ROWS, LANES))
            for j, rows in _windows(zp_ref, r0, lanes, groups):
                acc = acc + w[j] * rows
            out_ref[pl.ds(r0, CONV_ROWS), lanes] = acc


FWD_SHIFTS = [(j, CONV_PAD - (CONV_WIDTH - 1) + j) for j in range(CONV_WIDTH)]
BWD_SHIFTS = [(j, CONV_WIDTH - 1 - j) for j in range(CONV_WIDTH)]


def _conv_taps(zp_ref, z1_ref, dw_ref, bias_ref, tm, ch):
    _conv_apply(zp_ref, z1_ref, dw_ref, bias_ref, tm, ch, FWD_SHIFTS)


def _conv_specs(tm, ch):
    per = tm // CONV_PAD
    cb = COL_CONV_G // ch
    return [pl.BlockSpec((tm, ch), lambda i: (i, 0)),
            pl.BlockSpec((tm, ch), lambda i: (i, cb)),
            pl.BlockSpec((CONV_PAD, ch), lambda i: (jnp.maximum(i * per - 1, 0), 0)),
            pl.BlockSpec((CONV_PAD, ch), lambda i: (jnp.maximum(i * per - 1, 0), cb))]


def _conv_fwd(p, dw, bias, ln_g, ln_b):
    t = p.shape[0]
    ch = D_MODEL
    tm = min(CONV_TILE, t)

    def body(a_ref, g_ref, ah_ref, gh_ref, dw_ref, bias_ref, lg_ref, lb_ref, o_ref, z1_ref, zp_ref):
        i = pl.program_id(0)
        _conv_fill(zp_ref, a_ref, g_ref, ah_ref, gh_ref, i)
        _conv_taps(zp_ref, z1_ref, dw_ref, bias_ref, tm, ch)
        z1 = z1_ref[...]
        mu = jnp.mean(z1, axis=-1, keepdims=True)
        zc = z1 - mu
        rs = lax.rsqrt(jnp.mean(zc * zc, axis=-1, keepdims=True) + EPS)
        z2 = zc * rs * lg_ref[...] + lb_ref[...]
        o_ref[...] = (z2 * _sigmoid(z2)).astype(BF16)

    vec = pl.BlockSpec((1, ch), lambda i: (0, 0))
    return pl.pallas_call(
        body, name="conv_fwd", grid=(t // tm,),
        in_specs=_conv_specs(tm, ch) + [pl.BlockSpec((CONV_PAD, ch), lambda i: (0, 0)), vec, vec, vec],
        out_specs=[pl.BlockSpec((tm, ch), lambda i: (i, 0)), pl.BlockSpec((tm, ch), lambda i: (i, 0))],
        out_shape=[jax.ShapeDtypeStruct((t, ch), BF16), jax.ShapeDtypeStruct((t, ch), F32)],
        scratch_shapes=[pltpu.VMEM((CONV_PAD + tm, ch), F32)],
        compiler_params=_params(("parallel",)),
    )(p, p, p, p, dw, bias, ln_g, ln_b)


def _conv_bwd_ln(p, z1_saved, dz3, ln_g, ln_b):
    t = p.shape[0]
    ch = D_MODEL
    tm = min(CONV_TILE, t)

    def body(a_ref, g_ref, ah_ref, gh_ref, z1_ref, dz3_ref, lg_ref, lb_ref,
             dz1_ref, ddw_ref, dbias_ref, dlg_ref, dlb_ref, zp_ref):
        i = pl.program_id(0)
        _conv_fill(zp_ref, a_ref, g_ref, ah_ref, gh_ref, i)
        z1 = z1_ref[...]
        mu = jnp.mean(z1, axis=-1, keepdims=True)
        zc = z1 - mu
        rs = lax.rsqrt(jnp.mean(zc * zc, axis=-1, keepdims=True) + EPS)
        xh = zc * rs
        z2 = xh * lg_ref[...] + lb_ref[...]
        sig = _sigmoid(z2)
        dz2 = dz3_ref[...].astype(F32) * (sig * (1.0 + z2 * (1.0 - sig)))
        dxh = dz2 * lg_ref[...]
        dz1 = rs * (dxh - jnp.mean(dxh, axis=-1, keepdims=True) - xh * jnp.mean(dxh * xh, axis=-1, keepdims=True))
        dz1_ref[...] = dz1

        @pl.when(i == 0)
        def _():
            ddw_ref[...] = jnp.zeros_like(ddw_ref)
            dbias_ref[...] = jnp.zeros_like(dbias_ref)
            dlg_ref[...] = jnp.zeros_like(dlg_ref)
            dlb_ref[...] = jnp.zeros_like(dlb_ref)

        dlg_ref[...] += jnp.sum(dz2 * xh, axis=0, keepdims=True)
        dlb_ref[...] += jnp.sum(dz2, axis=0, keepdims=True)
        dbias_ref[...] += jnp.sum(dz1, axis=0, keepdims=True)
        groups = _shift_groups(FWD_SHIFTS)
        for cc in range(ch // LANES):
            lanes = pl.ds(cc * LANES, LANES)
            accs = [jnp.zeros((8, LANES), F32) for _ in range(CONV_WIDTH)]
            for r0 in range(0, tm, CONV_ROWS):
                dzc = dz1_ref[pl.ds(r0, CONV_ROWS), lanes]
                for j, rows in _windows(zp_ref, r0, lanes, groups):
                    accs[j] = accs[j] + jnp.sum((dzc * rows).reshape(CONV_ROWS // 8, 8, LANES), axis=0)
            for j in range(CONV_WIDTH):
                ddw_ref[pl.ds(j, 1), lanes] += jnp.sum(accs[j], axis=0, keepdims=True)

    vec = pl.BlockSpec((1, ch), lambda i: (0, 0))
    return pl.pallas_call(
        body, name="conv_bwd_ln", grid=(t // tm,),
        in_specs=_conv_specs(tm, ch) + [pl.BlockSpec((tm, ch), lambda i: (i, 0)),
                                        pl.BlockSpec((tm, ch), lambda i: (i, 0)), vec, vec],
        out_specs=[pl.BlockSpec((tm, ch), lambda i: (i, 0)), pl.BlockSpec((CONV_PAD, ch), lambda i: (0, 0)), vec, vec, vec],
        out_shape=[jax.ShapeDtypeStruct((t, ch), F32), jax.ShapeDtypeStruct((CONV_PAD, ch), F32)]
        + [jax.ShapeDtypeStruct((1, ch), F32)] * 3,
        scratch_shapes=[pltpu.VMEM((CONV_PAD + tm, ch), F32)],
        compiler_params=_params(("arbitrary",)),
    )(p, p, p, p, z1_saved, dz3, ln_g, ln_b)


def _conv_bwd_glu(p, dz1, dw, dq, dkv, dgates):
    t = p.shape[0]
    ch = D_MODEL
    tm = min(CONV_TILE, t)
    per = tm // CONV_PAD
    n_halo = t // CONV_PAD
    cb = COL_CONV_G // ch

    def body(a_ref, g_ref, dz_ref, dzn_ref, dw_ref, dq_ref, dkv_ref, dgates_ref, o_ref, zp_ref, z0_ref):
        i = pl.program_id(0)
        o_ref[:, pl.ds(COL_Q, Q_W)] = dq_ref[...]
        o_ref[:, pl.ds(COL_K, 2 * KV_W)] = dkv_ref[...]
        o_ref[:, pl.ds(COL_GC, ch)] = dgates_ref[0]
        o_ref[:, pl.ds(COL_GA, ch)] = dgates_ref[1]
        zp_ref[pl.ds(0, tm), :] = dz_ref[...]
        zp_ref[pl.ds(tm, CONV_PAD), :] = jnp.where(i < t // tm - 1, dzn_ref[...], 0.0)
        _conv_apply(zp_ref, z0_ref, dw_ref, None, tm, ch, BWD_SHIFTS)
        dz0 = z0_ref[...]
        a = a_ref[...].astype(F32)
        sig = _sigmoid(g_ref[...].astype(F32))
        o_ref[:, pl.ds(0, ch)] = (dz0 * sig).astype(BF16)
        o_ref[:, pl.ds(ch, ch)] = (dz0 * a * sig * (1.0 - sig)).astype(BF16)

    return pl.pallas_call(
        body, name="conv_bwd_glu", grid=(t // tm,),
        in_specs=[pl.BlockSpec((tm, ch), lambda i: (i, 0)), pl.BlockSpec((tm, ch), lambda i: (i, cb)),
                  pl.BlockSpec((tm, ch), lambda i: (i, 0)),
                  pl.BlockSpec((CONV_PAD, ch), lambda i: (jnp.minimum((i + 1) * per, n_halo - 1), 0)),
                  pl.BlockSpec((CONV_PAD, ch), lambda i: (0, 0)),
                  pl.BlockSpec((tm, Q_W), lambda i: (i, 0)), pl.BlockSpec((tm, 2 * KV_W), lambda i: (i, 0)),
                  pl.BlockSpec((2, tm, ch), lambda i: (0, i, 0))],
        out_specs=pl.BlockSpec((tm, IN_W), lambda i: (i, 0)),
        out_shape=jax.ShapeDtypeStruct((t, IN_W), BF16),
        scratch_shapes=[pltpu.VMEM((tm + CONV_PAD, ch), F32), pltpu.VMEM((tm, ch), F32)],
        compiler_params=_params(("parallel",)),
    )(p, p, dz1, dz1, dw, dq, dkv, dgates)


def _bucket_onehot():
    qi = jnp.arange(BLOCK, dtype=jnp.int32)[:, None]
    kj = jnp.arange(2 * BLOCK, dtype=jnp.int32)[None, :]
    dist = jnp.maximum(qi + BLOCK - kj, 0)
    max_exact = N_BUCKETS // 2
    dflt = jnp.maximum(dist, 1).astype(F32)
    large = max_exact + (jnp.log(dflt / max_exact) / math.log(MAX_DISTANCE / max_exact)
                         * (N_BUCKETS - max_exact)).astype(jnp.int32)
    large = jnp.minimum(large, N_BUCKETS - 1)
    bucket = jnp.where(dist < max_exact, dist, large)
    onehot = bucket[None] == jnp.arange(N_BUCKETS, dtype=jnp.int32)[:, None, None]
    return onehot.astype(F32).reshape(N_BUCKETS, BLOCK * 2 * BLOCK)


def _bias_table(rel_bias_t, onehot):
    n = onehot.shape[1]
    tn = 4096

    def body(r_ref, oh_ref, o_ref):
        flat = pl.program_id(0) * tn + lax.broadcasted_iota(jnp.int32, (N_Q_HEADS, tn), 1)
        dist = (flat // (2 * BLOCK)) + BLOCK - (flat % (2 * BLOCK))
        bias = _dot(r_ref[...], oh_ref[...], precision=lax.Precision.HIGHEST)
        o_ref[...] = jnp.where((dist >= 0) & (dist < BLOCK), bias, NEG)

    return pl.pallas_call(
        body, name="bias_table", grid=(n // tn,),
        in_specs=[pl.BlockSpec((N_Q_HEADS, N_BUCKETS), lambda i: (0, 0)), pl.BlockSpec((N_BUCKETS, tn), lambda i: (0, i))],
        out_specs=pl.BlockSpec((N_Q_HEADS, tn), lambda i: (0, i)),
        out_shape=jax.ShapeDtypeStruct((N_Q_HEADS, n), F32),
        compiler_params=_params(("parallel",)),
    )(rel_bias_t, onehot)


def _bias_table_bwd(dbias, onehot):
    n = onehot.shape[1]
    tn = 4096

    def body(d_ref, oh_ref, o_ref):
        part = _dot(d_ref[...], oh_ref[...], trans_b=True, precision=lax.Precision.HIGHEST)
        i = pl.program_id(0)

        @pl.when(i == 0)
        def _():
            o_ref[...] = part

        @pl.when(i > 0)
        def _():
            o_ref[...] += part

    return pl.pallas_call(
        body, name="bias_table_bwd", grid=(n // tn,),
        in_specs=[pl.BlockSpec((N_Q_HEADS, tn), lambda i: (0, i)), pl.BlockSpec((N_BUCKETS, tn), lambda i: (0, i))],
        out_specs=pl.BlockSpec((N_Q_HEADS, N_BUCKETS), lambda i: (0, 0)),
        out_shape=jax.ShapeDtypeStruct((N_Q_HEADS, N_BUCKETS), F32),
        compiler_params=_params(("arbitrary",)),
    )(dbias, onehot)


def _lane_head(rows):
    return lax.broadcasted_iota(jnp.int32, (rows, KV_W), 1) // HEAD_DIM


def _group_rms(x, gain_wide):
    head = _lane_head(x.shape[0])
    sq = x * x
    r = jnp.zeros_like(x)
    for i in range(N_KV_HEADS):
        ms = jnp.sum(jnp.where(head == i, sq, 0.0), axis=-1, keepdims=True) * (1.0 / HEAD_DIM)
        r = jnp.where(head == i, lax.rsqrt(ms + EPS), r)
    return r, x * r * gain_wide


def _stack_heads(group):
    head = _lane_head(group.shape[0])
    return jnp.concatenate([jnp.where(head == i, group, jnp.zeros_like(group)) for i in range(N_KV_HEADS)], axis=0)


def _unstack_heads(stacked):
    head = _lane_head(BLOCK)
    out = jnp.where(head == 0, stacked[:BLOCK], 0.0)
    for i in range(1, N_KV_HEADS):
        out = out + jnp.where(head == i, stacked[i * BLOCK:(i + 1) * BLOCK], 0.0)
    return out


def _repeaters():
    row = lax.broadcasted_iota(jnp.int32, (KV_W, KV_W), 0)
    col = lax.broadcasted_iota(jnp.int32, (KV_W, KV_W), 1)
    return [(row == h * HEAD_DIM + col % HEAD_DIM).astype(BF16) for h in range(N_KV_HEADS)]


def _attn_probs(q_stack, k_rep, sink, bias, before_start):
    s = _dot(q_stack, k_rep, trans_b=True) * (1.0 / math.sqrt(HEAD_DIM)) + bias
    s = jnp.where(before_start, NEG, s)
    m = jnp.maximum(jnp.max(s, axis=-1, keepdims=True), sink)
    p = jnp.exp(s - m)
    es = jnp.exp(sink - m)
    inv = 1.0 / (jnp.sum(p, axis=-1, keepdims=True) + es)
    return p * inv, es * inv


def _before_start(n):
    col = lax.broadcasted_iota(jnp.int32, (QROWS, 2 * BLOCK), 1)
    return (col < BLOCK) & (n == 0)


STEP_BLOCKS = 4
KV_W = N_KV_HEADS * HEAD_DIM
Q_W = N_Q_HEADS * HEAD_DIM


def _attn_specs():
    gain = pl.BlockSpec((1, KV_W), lambda n: (0, 0))
    sink = pl.BlockSpec((N_KV_HEADS, QROWS, 1), lambda n: (0, 0, 0))
    bias = pl.BlockSpec((N_KV_HEADS, QROWS, 2 * BLOCK), lambda n: (0, 0, 0))
    return gain, sink, bias


def _attn_fwd(p, gq, gk, sink_rows, bias):
    t = p.shape[0]
    nb = t // BLOCK
    per = STEP_BLOCKS if nb % STEP_BLOCKS == 0 else 1
    gain, sink, bspec = _attn_specs()

    def body(q_ref, kp_ref, kc_ref, vp_ref, vc_ref, gq_ref, gk_ref, sink_ref, bias_ref, o_ref, p_ref, ps_ref):
        first = pl.program_id(0) * per
        rep = _repeaters()
        kf = jnp.concatenate([kp_ref[...], kc_ref[...]], axis=0).astype(F32)
        kn = _group_rms(kf, gk_ref[...])[1].astype(BF16)
        v = jnp.concatenate([vp_ref[...], vc_ref[...]], axis=0)
        for h in range(N_KV_HEADS):
            k_rep = _dot(kn, rep[h]).astype(BF16)
            v_rep = _dot(v, rep[h]).astype(BF16)
            for sub in range(per):
                rows = pl.ds(sub * BLOCK, BLOCK)
                window = slice(sub * BLOCK, (sub + 2) * BLOCK)
                qn = _group_rms(q_ref[rows, pl.ds(h * KV_W, KV_W)].astype(F32), gq_ref[...])[1]
                pn, ps_ref[sub, h] = _attn_probs(_stack_heads(qn).astype(BF16), k_rep[window], sink_ref[h], bias_ref[h],
                                                 _before_start(first + sub))
                pn = pn.astype(BF16)
                p_ref[sub, h] = pn
                o_ref[rows, pl.ds(h * KV_W, KV_W)] = _unstack_heads(_dot(pn, v_rep[window])).astype(BF16)

    def kv_specs(col):
        return [pl.BlockSpec((BLOCK, KV_W), lambda n: (jnp.maximum(n * per - 1, 0), col // KV_W)),
                pl.BlockSpec((per * BLOCK, KV_W), lambda n: (n, col // KV_W))]

    return pl.pallas_call(
        body, name="attn_fwd", grid=(nb // per,),
        in_specs=[pl.BlockSpec((per * BLOCK, Q_W), lambda n: (n, COL_Q // Q_W))] + kv_specs(COL_K) + kv_specs(COL_V)
        + [gain, gain, sink, bspec],
        out_specs=[pl.BlockSpec((per * BLOCK, Q_W), lambda n: (n, 0)),
                   pl.BlockSpec((per, N_KV_HEADS, QROWS, 2 * BLOCK), lambda n: (n, 0, 0, 0)),
                   pl.BlockSpec((per, N_KV_HEADS, QROWS, 1), lambda n: (n, 0, 0, 0))],
        out_shape=[jax.ShapeDtypeStruct((t, Q_W), BF16),
                   jax.ShapeDtypeStruct((nb, N_KV_HEADS, QROWS, 2 * BLOCK), BF16),
                   jax.ShapeDtypeStruct((nb, N_KV_HEADS, QROWS, 1), F32)],
        compiler_params=_params(("parallel",)),
    )(p, p, p, p, p, gq, gk, sink_rows, bias)


def _attn_bwd(p, do, gq, gk, probs, sink_probs):
    t = p.shape[0]
    nb = t // BLOCK
    per = STEP_BLOCKS if nb % STEP_BLOCKS == 0 else 1
    bspec = _attn_specs()[2]
    gain = pl.BlockSpec((1, HEAD_DIM), lambda n: (0, 0))
    scale = 1.0 / math.sqrt(HEAD_DIM)

    def head_selectors():
        row = lax.broadcasted_iota(jnp.int32, (KV_W, HEAD_DIM), 0)
        col = lax.broadcasted_iota(jnp.int32, (KV_W, HEAD_DIM), 1)
        return [(row == col + i * HEAD_DIM).astype(BF16) for i in range(N_KV_HEADS)]

    def take_heads(group, sel):
        return jnp.concatenate([_dot(group, s) for s in sel], axis=0)

    def put_heads(x, sel):
        rows = x.shape[0] // len(sel)
        out = _dot(x[:rows].astype(BF16), sel[0], trans_b=True)
        for i in range(1, len(sel)):
            out = out + _dot(x[i * rows:(i + 1) * rows].astype(BF16), sel[i], trans_b=True)
        return out

    def rms(x, g):
        r = lax.rsqrt(jnp.mean(x * x, axis=-1, keepdims=True) + EPS)
        return r, x * r * g

    def rms_bwd(dn, xf, r, g):
        w = dn * g
        dx = r * w - xf * (r * r * r) * jnp.mean(xf * w, axis=-1, keepdims=True)
        return dx, jnp.sum(dn * (xf * r), axis=0, keepdims=True)

    def body(q_ref, kp_ref, kc_ref, vp_ref, vc_ref, do_ref, gq_ref, gk_ref, p_ref, ps_ref,
             dq_ref, dkv_ref, dbias_ref, dsink_ref, dgq_ref, dgk_ref):
        n = pl.program_id(0)
        sel = head_selectors()

        @pl.when(n == 0)
        def _():
            dbias_ref[...] = jnp.zeros_like(dbias_ref)
            dsink_ref[...] = jnp.zeros_like(dsink_ref)
            dgq_ref[...] = jnp.zeros_like(dgq_ref)
            dgk_ref[...] = jnp.zeros_like(dgk_ref)

        dgq_sum = jnp.zeros((1, HEAD_DIM), F32)
        dgk_sum = jnp.zeros((1, HEAD_DIM), F32)
        dk_rows, dv_rows = [[] for _ in range(per)], [[] for _ in range(per)]
        for h in range(N_KV_HEADS):
            kf_all = jnp.concatenate([_dot(kp_ref[...], sel[h]), _dot(kc_ref[...], sel[h])], axis=0)
            rk_all, kn_all = rms(kf_all, gk_ref[...])
            kn_all = kn_all.astype(BF16)
            v_all = jnp.concatenate([_dot(vp_ref[...], sel[h]), _dot(vc_ref[...], sel[h])], axis=0).astype(BF16)
            for sub in range(per):
                rows = pl.ds(sub * BLOCK, BLOCK)
                window = slice(sub * BLOCK, (sub + 2) * BLOCK)
                qf = take_heads(q_ref[rows, pl.ds(h * KV_W, KV_W)], sel)
                rq, qn = rms(qf, gq_ref[...])
                pn_bf16, psink = p_ref[sub, h], ps_ref[sub, h]
                pn = pn_bf16.astype(F32)
                do = take_heads(do_ref[rows, pl.ds(h * KV_W, KV_W)], sel).astype(BF16)
                dv_win = _dot(do, pn_bf16, trans_a=True).T
                dp = _dot(do, v_all[window], trans_b=True)
                delta = jnp.sum(pn * dp, axis=-1, keepdims=True)
                ds = pn * (dp - delta)
                dsc = (ds * scale).astype(BF16)
                dqn = _dot(dsc, kn_all[window])
                dkn = _dot(qn.astype(BF16), dsc, trans_a=True).T
                dq, dgq = rms_bwd(dqn, qf, rq, gq_ref[...])
                dk_win, dgk = rms_bwd(dkn, kf_all[window], rk_all[window], gk_ref[...])
                dq_ref[rows, pl.ds(h * KV_W, KV_W)] = put_heads(dq, sel).astype(BF16)
                dk_rows[sub] += [dk_win[:BLOCK], dk_win[BLOCK:]]
                dv_rows[sub] += [dv_win[:BLOCK], dv_win[BLOCK:]]
                dbias_ref[h] += ds
                dsink_ref[h] += jnp.sum((-psink * delta).reshape(GROUP, BLOCK, 1), axis=1)
                dgq_sum = dgq_sum + dgq
                dgk_sum = dgk_sum + dgk
        for sub in range(per):
            for part in range(2):
                dkv_ref[sub, part, :, pl.ds(0, KV_W)] = put_heads(
                    jnp.concatenate(dk_rows[sub][part::2], axis=0), sel).astype(BF16)
                dkv_ref[sub, part, :, pl.ds(KV_W, KV_W)] = put_heads(
                    jnp.concatenate(dv_rows[sub][part::2], axis=0), sel).astype(BF16)
        dgq_ref[...] += dgq_sum
        dgk_ref[...] += dgk_sum

    def kv_specs(col):
        return [pl.BlockSpec((BLOCK, KV_W), lambda n: (jnp.maximum(n * per - 1, 0), col // KV_W)),
                pl.BlockSpec((per * BLOCK, KV_W), lambda n: (n, col // KV_W))]

    row = pl.BlockSpec((per * BLOCK, Q_W), lambda n: (n, 0))
    return pl.pallas_call(
        body, name="attn_bwd", grid=(nb // per,),
        in_specs=[pl.BlockSpec((per * BLOCK, Q_W), lambda n: (n, COL_Q // Q_W))] + kv_specs(COL_K) + kv_specs(COL_V)
        + [row, gain, gain,
           pl.BlockSpec((per, N_KV_HEADS, QROWS, 2 * BLOCK), lambda n: (n, 0, 0, 0)),
           pl.BlockSpec((per, N_KV_HEADS, QROWS, 1), lambda n: (n, 0, 0, 0))],
        out_specs=[row, pl.BlockSpec((per, 2, BLOCK, 2 * KV_W), lambda n: (n, 0, 0, 0)), bspec,
                   pl.BlockSpec((N_KV_HEADS, GROUP, 1), lambda n: (0, 0, 0)), gain, gain],
        out_shape=[jax.ShapeDtypeStruct((t, Q_W), BF16),
                   jax.ShapeDtypeStruct((nb, 2, BLOCK, 2 * KV_W), BF16),
                   jax.ShapeDtypeStruct((N_KV_HEADS, QROWS, 2 * BLOCK), F32),
                   jax.ShapeDtypeStruct((N_KV_HEADS, GROUP, 1), F32),
                   jax.ShapeDtypeStruct((1, HEAD_DIM), F32),
                   jax.ShapeDtypeStruct((1, HEAD_DIM), F32)],
        compiler_params=_params(("arbitrary",)),
    )(p, p, p, p, p, do, gq, gk, probs, sink_probs)


def _kv_window_sum(parts):
    nb = parts.shape[0]

    def body(cur_ref, nxt_ref, o_ref):
        nxt = jnp.where(pl.program_id(0) < nb - 1, nxt_ref[...].astype(F32), 0.0)
        o_ref[...] = (cur_ref[...].astype(F32) + nxt).astype(BF16)

    blk = (None, None, BLOCK, 2 * KV_W)
    return pl.pallas_call(
        body, name="kv_window_sum", grid=(nb,),
        in_specs=[pl.BlockSpec(blk, lambda n: (n, 1, 0, 0)),
                  pl.BlockSpec(blk, lambda n: (jnp.minimum(n + 1, nb - 1), 0, 0, 0))],
        out_specs=pl.BlockSpec((BLOCK, 2 * KV_W), lambda n: (n, 0)),
        out_shape=jax.ShapeDtypeStruct((nb * BLOCK, 2 * KV_W), BF16),
        compiler_params=_params(("parallel",)),
    )(parts, parts)


GATE_TILE = 512


def _merge_fwd(z3, o, p, w_proj, w_o):
    t, d = z3.shape
    tm = min(ROW_TILE, t)
    tn = GATE_TILE

    def body(z_ref, o_ref, gc_ref, ga_ref, wp_ref, wo_ref, m_ref, a_ref, b_ref):
        a = _dot(z_ref[...], wp_ref[...])
        b = _dot(o_ref[...], wo_ref[...])
        m_ref[...] = (_sigmoid(gc_ref[...].astype(F32)) * a + _sigmoid(ga_ref[...].astype(F32)) * b).astype(BF16)
        a_ref[...] = a.astype(BF16)
        b_ref[...] = b.astype(BF16)

    row = pl.BlockSpec((tm, d), lambda i, j: (i, 0))
    wspec = pl.BlockSpec((d, tn), lambda i, j: (0, j))
    ospec = pl.BlockSpec((tm, tn), lambda i, j: (i, j))
    return pl.pallas_call(
        body, name="merge_fwd", grid=(t // tm, d // tn),
        in_specs=[row, row,
                  pl.BlockSpec((tm, tn), lambda i, j: (i, COL_GC // tn + j)),
                  pl.BlockSpec((tm, tn), lambda i, j: (i, COL_GA // tn + j)), wspec, wspec],
        out_specs=[ospec, ospec, ospec],
        out_shape=[jax.ShapeDtypeStruct((t, d), BF16)] * 3,
        compiler_params=_params(("parallel", "parallel")),
    )(z3, o, p, p, w_proj, w_o)


def _merge_bwd(dres, w_out, a, b, p, tokens=()):
    t, d = dres.shape
    tm = min(ROW_TILE, t)
    tn = GATE_TILE

    def epilogue(acc, ex, outs, ids):
        a_ref, b_ref, gc_ref, ga_ref = ex[:4]
        sc = _sigmoid(gc_ref[...].astype(F32))
        sa = _sigmoid(ga_ref[...].astype(F32))
        outs[0][...] = (acc * sc).astype(BF16)
        outs[1][...] = (acc * sa).astype(BF16)
        outs[2][0] = (acc * a_ref[...].astype(F32) * sc * (1.0 - sc)).astype(BF16)
        outs[2][1] = (acc * b_ref[...].astype(F32) * sa * (1.0 - sa)).astype(BF16)

    ospec = pl.BlockSpec((tm, tn), lambda i, j, kk: (i, j))
    return _mm("merge_bwd", (t // tm, d // tn, 1),
               dres, pl.BlockSpec((tm, d), lambda i, j, kk: (i, 0)),
               w_out, pl.BlockSpec((tn, d), lambda i, j, kk: (j, 0)), (tm, tn),
               trans_b=True, a_pre=_to_bf16,
               extras=(a, b, p, p),
               extra_specs=(ospec, ospec,
                            pl.BlockSpec((tm, tn), lambda i, j, kk: (i, COL_GC // tn + j)),
                            pl.BlockSpec((tm, tn), lambda i, j, kk: (i, COL_GA // tn + j))), tokens=tokens,
               out_shape=(jax.ShapeDtypeStruct((t, d), BF16), jax.ShapeDtypeStruct((t, d), BF16),
                          jax.ShapeDtypeStruct((2, t, d), BF16)),
               out_specs=(ospec, ospec, pl.BlockSpec((2, tm, tn), lambda i, j, kk: (0, i, j))),
               epilogue=epilogue)


def _store_epilogue(acc, ex, outs, ids):
    outs[0][...] = acc


def _store_bf16_epilogue(acc, ex, outs, ids):
    outs[0][...] = acc.astype(BF16)


def _mm_nt(name, a, w, out_dtype=BF16):
    t, n = a.shape
    k = w.shape[0]
    tm = min(ROW_TILE, t)
    return _mm(name, (t // tm, 1, 1), a, pl.BlockSpec((tm, n), lambda i, j, kk: (i, 0)),
               w, pl.BlockSpec((k, n), lambda i, j, kk: (0, 0)), (tm, k), trans_b=True,
               out_shape=(jax.ShapeDtypeStruct((t, k), out_dtype),),
               out_specs=(pl.BlockSpec((tm, k), lambda i, j, kk: (i, 0)),),
               epilogue=_store_bf16_epilogue if out_dtype == BF16 else _store_epilogue)[0]


def _mm_tn(name, a, b, b_pre=None, tokens=()):
    t, m = a.shape
    n = b.shape[1]
    tk = min(TOKEN_TILE // (2 if b.dtype == F32 else 1), t)
    return _mm(name, (1, 1, t // tk), a, pl.BlockSpec((tk, m), lambda i, j, kk: (kk, 0)),
               b, pl.BlockSpec((tk, n), lambda i, j, kk: (kk, 0)), (m, n), trans_a=True, b_pre=b_pre, tokens=tokens,
               out_shape=(jax.ShapeDtypeStruct((m, n), BF16),),
               out_specs=(pl.BlockSpec((m, n), lambda i, j, kk: (0, 0)),), epilogue=_store_bf16_epilogue)[0]


def _local_step(x, target, small, comm):
    t = x.shape[0]
    w = dict(small)

    n1 = _rmsnorm_fwd("ffn1_norm", x, w["ffn1_norm"])
    onehot = _bucket_onehot()
    bias = _bias_table(w["rel_bias"].T, onehot).reshape(N_KV_HEADS, QROWS, 2 * BLOCK)
    sink_rows = jnp.repeat(w["attn_sinks"].reshape(N_KV_HEADS, GROUP), BLOCK, axis=1)[..., None]
    gq_wide = jnp.tile(w["q_norm"], (1, N_KV_HEADS))
    gk_wide = jnp.tile(w["k_norm"], (1, N_KV_HEADS))
    w.update(comm.weights("A", [n1, onehot, bias, sink_rows, gq_wide, gk_wide]))
    (x1, hm), ffn1_saved = _ffn_fwd("ffn1", x, n1, w["ffn1_w_in"], w["ffn1_w_out"], comm.tokens,
                                    next_gain=w["mix_norm"])
    w.update(comm.weights("B", x1))
    tm = min(ROW_TILE, t)
    p = _mm("mix_in", (N_CHIPS, t // tm, 1),
            hm, pl.BlockSpec((tm, D_MODEL), lambda j, i, kk: (i, 0)),
            w["w_in"], pl.BlockSpec((None, D_MODEL, SHARD_W), lambda j, i, kk: (j, 0, 0)), (tm, SHARD_W),
            tokens=comm.tokens,
            out_shape=(jax.ShapeDtypeStruct((t, IN_W), BF16),),
            out_specs=(pl.BlockSpec((tm, SHARD_W), lambda j, i, kk: (i, j)),),
            epilogue=_store_bf16_epilogue)[0]

    z3, z1 = _conv_fwd(p, w["conv_dw_kernel"], w["conv_dw_bias"], w["conv_ln_g"], w["conv_ln_b"])

    o, probs, sink_probs = _attn_fwd(p, gq_wide, gk_wide, sink_rows, bias)

    merged, a, b = _merge_fwd(z3, o, p, w["conv_w_proj"], w["attn_w_o"])
    x2, n2 = _mm_residual("mix_out", merged, w["w_out"], x1, 1.0, next_gain=w["ffn2_norm"])
    w.update(comm.weights("C", n2))
    (dy, loss), ffn2_saved = _ffn_fwd("ffn2", x2, n2, w["ffn2_w_in"], w["ffn2_w_out"], loss_target=target)

    g, big = {}, {}
    dres2, big["ffn2_w_in"], big["ffn2_w_out"], g["ffn2_norm"] = _ffn_bwd(
        "ffn2b", dy, x2, w["ffn2_norm"], ffn2_saved, w["ffn2_w_in"], w["ffn2_w_out"])
    tokens = comm.reduce_start("R1", big, behind=True)

    da, db, dgates = _merge_bwd(dres2, w["w_out"], a, b, p, tokens)
    tokens = comm.exchange_finish("R1", da)
    big = {}
    big["w_out"] = _mm_tn("d_w_out", merged, dres2, b_pre=_to_bf16, tokens=tokens)
    big["conv_w_proj"] = _mm_tn("d_w_proj", z3, da)
    big["attn_w_o"] = _mm_tn("d_w_o", o, db)
    dz3 = _mm_nt("d_z3", da, w["conv_w_proj"])
    do = _mm_nt("d_o", db, w["attn_w_o"])

    dq, dkv_parts, dbias, dsink, g["q_norm"], g["k_norm"] = _attn_bwd(
        p, do, w["q_norm"], w["k_norm"], probs, sink_probs)
    dkv = _kv_window_sum(dkv_parts)
    g["rel_bias"] = _bias_table_bwd(dbias.reshape(N_Q_HEADS, BLOCK * 2 * BLOCK), onehot).T
    g["attn_sinks"] = dsink.reshape(N_Q_HEADS)

    dz1, big["conv_dw_kernel"], g["conv_dw_bias"], g["conv_ln_g"], g["conv_ln_b"] = _conv_bwd_ln(
        p, z1, dz3, w["conv_ln_g"], w["conv_ln_b"])
    dp = _conv_bwd_glu(p, dz1, w["conv_dw_kernel"], dq, dkv, dgates)
    tk = min(TOKEN_TILE, t)
    big["w_in"] = _mm("d_w_in", (1, N_CHIPS, t // tk),
                    hm, pl.BlockSpec((tk, D_MODEL), lambda i, j, kk: (kk, 0)),
                    dp, pl.BlockSpec((tk, SHARD_W), lambda i, j, kk: (kk, j)), (D_MODEL, SHARD_W),
                    trans_a=True,
                    out_shape=(jax.ShapeDtypeStruct((N_CHIPS, D_MODEL, SHARD_W), BF16),),
                    out_specs=(pl.BlockSpec((None, D_MODEL, SHARD_W), lambda i, j, kk: (j, 0, 0)),),
                    epilogue=_store_bf16_epilogue)[0]
    tn = min(NORM_GRAD_TILE, t)
    dres1, g["mix_norm"] = _norm_input_grad(
        "d_mix", dp, [pl.BlockSpec((tn, SHARD_W), functools.partial(lambda i, s: (i, s), s=s)) for s in range(N_CHIPS)],
        w["w_in"], x1, w["mix_norm"], dres2)

    tokens = comm.reduce_finish("R1", dres1, behind=True) + comm.reduce_start("R2", big, behind=True)

    def ffn1_first(du):
        comm.join_finish("R1", du)
        return comm.exchange_finish("R2", du)

    def ffn1_grads(dw_in4, dw_out):
        late = comm.reduce_finish("R2", dw_in4, behind=True)
        return late + comm.reduce_start("R3", {"ffn1_w_in": dw_in4, "ffn1_w_out": dw_out})

    grad_x, _, _, g["ffn1_norm"] = _ffn_bwd(
        "ffn1b", dres1, x, w["ffn1_norm"], ffn1_saved, w["ffn1_w_in"], w["ffn1_w_out"], tokens, ffn1_first, ffn1_grads)
    comm.join_finish("R2", grad_x)
    comm.reduce_finish("R3", grad_x)
    return loss[0, 0], grad_x, g


def _mesh_place():
    x, y, c = lax.axis_index("x"), lax.axis_index("y"), lax.axis_index("c")
    chips = [(1 - x, y), (x, 1 - y), (1 - x, 1 - y)]
    return x, y, c, chips


def _any_specs(n):
    return [pl.BlockSpec(memory_space=pl.ANY)] * n


HBM_SPEC = pl.BlockSpec(memory_space=pltpu.HBM)
SEM_SPEC = pl.BlockSpec(memory_space=pltpu.SEMAPHORE)
EFFECT = pltpu.SideEffectType.DATAFLOW_SIDE_EFFECTING


def _in_hbm(a):
    return pltpu.with_memory_space_constraint(a, pltpu.HBM)


def _copy_start(name, srcs, lands, plan, after=()):
    ns, nb = len(srcs), len(lands)
    n = plan.copies_per_source * ns

    def body(*refs):
        s_refs, l_refs = refs[:ns], refs[ns:ns + nb]
        send_sems, recv_sems = refs[ns + nb + len(after)], refs[ns + nb + len(after) + 1]
        token = refs[-1]
        for k, (src, dst, to, _) in enumerate(plan(s_refs, l_refs)):
            pltpu.make_async_remote_copy(src_ref=src, dst_ref=dst, send_sem=send_sems.at[k], recv_sem=recv_sems.at[k],
                                         device_id=to, device_id_type=MESH).start()
        token[...] = jnp.zeros_like(token)

    bufs = list(srcs) + list(lands)
    outs = pl.pallas_call(
        body, name=name,
        out_shape=(pltpu.SemaphoreType.DMA((n,)), pltpu.SemaphoreType.DMA((n,)),
                   *[pltpu.HBM(a.shape, a.dtype) for a in bufs], jax.ShapeDtypeStruct((8, LANES), F32)),
        in_specs=[HBM_SPEC] * len(bufs) + [pl.BlockSpec(memory_space=pl.ANY)] * len(after),
        out_specs=(SEM_SPEC, SEM_SPEC, *[HBM_SPEC] * len(bufs), pl.BlockSpec(memory_space=pltpu.VMEM)),
        input_output_aliases={i: 2 + i for i in range(len(bufs))},
        compiler_params=pltpu.CompilerParams(has_side_effects=EFFECT),
    )(*[_in_hbm(a) for a in bufs], *after)
    return outs[0], outs[1], list(outs[2:2 + ns]), list(outs[2 + ns:2 + ns + nb]), outs[-1]


def _copy_wait(name, send_sems, recv_sems, srcs, lands, after, plan):
    ns, nb = len(srcs), len(lands)
    after = tuple(after) if isinstance(after, (tuple, list)) else (after,)

    def body(*refs):
        s_refs, l_refs = refs[:ns], refs[ns:ns + nb]
        send_sems, recv_sems = refs[ns + nb], refs[ns + nb + 1]
        for k, (src, _, to, mine) in enumerate(plan(s_refs, l_refs)):
            cp = pltpu.make_async_remote_copy(src_ref=src, dst_ref=mine, send_sem=send_sems.at[k], recv_sem=recv_sems.at[k],
                                              device_id=to, device_id_type=MESH)
            cp.wait_send()
            cp.wait_recv()

    bufs = list(srcs) + list(lands)
    outs = pl.pallas_call(
        body, name=name,
        out_shape=tuple(pltpu.HBM(a.shape, a.dtype) for a in bufs),
        in_specs=[HBM_SPEC] * len(bufs) + [SEM_SPEC, SEM_SPEC] + [pl.BlockSpec(memory_space=pl.ANY)] * len(after),
        out_specs=tuple([HBM_SPEC] * len(bufs)),
        input_output_aliases={i: i for i in range(len(bufs))},
        compiler_params=pltpu.CompilerParams(has_side_effects=EFFECT),
    )(*bufs, send_sems, recv_sems, *after)
    return list(outs[:ns]), list(outs[ns:])


def _gather_plan(s_refs, l_refs):
    x, y, c, chips = _mesh_place()
    jme = 2 * x + y
    return [(s.at[c], land.at[jme, c], (*chip, c), land.at[2 * chip[0] + chip[1], c])
            for s, land in zip(s_refs, l_refs) for chip in chips]


_gather_plan.copies_per_source = 3


def _gather_both_cores_plan(s_refs, l_refs):
    x, y, c, chips = _mesh_place()
    jme = 2 * x + y
    plan = []
    for s, land in zip(s_refs, l_refs):
        for chip in chips:
            for peer_core in (c, 1 - c):
                plan.append((s.at[c], land.at[jme, c], (*chip, peer_core), land.at[2 * chip[0] + chip[1], peer_core]))
        plan.append((s, land.at[jme], (x, y, 1 - c), land.at[jme]))
    return plan


_gather_both_cores_plan.copies_per_source = 7


def _scatter_plan(s_refs, l_refs):
    x, y, c, chips = _mesh_place()
    return [(s.at[2 * chip[0] + chip[1]], land.at[k], (*chip, c), land.at[k])
            for s, land in zip(s_refs, l_refs) for k, chip in enumerate(chips)]


_scatter_plan.copies_per_source = 3


def _exchange_plan(g_refs, l_refs):
    x, y, c, _ = _mesh_place()
    return [(g.at[:, 1 - c], land, (x, y, 1 - c), land) for g, land in zip(g_refs, l_refs)]


_exchange_plan.copies_per_source = 1


def _join_plan(h_refs, l_refs):
    x, y, c, _ = _mesh_place()
    return [(h.at[c], h.at[c], (x, y, 1 - c), h.at[1 - c]) for h in h_refs]


_join_plan.copies_per_source = 1


def _gather_forward(name, shards, landed):
    nw = len(shards)

    def body(*refs):
        s_refs, o_refs = refs[:nw], refs[2 * nw:3 * nw]
        send_sems, recv_sems = refs[3 * nw:]
        x, y, c, chips = _mesh_place()
        me, sib, jme = (x, y, c), (x, y, 1 - c), 2 * x + y
        sent = []
        for w in range(nw):
            parts = [(o_refs[w].at[2 * chip[0] + chip[1], c], o_refs[w].at[2 * chip[0] + chip[1], c]) for chip in chips]
            parts.append((s_refs[w], o_refs[w].at[jme]))
            for k, (src, dst) in enumerate(parts):
                cp = pltpu.make_async_remote_copy(src_ref=src, dst_ref=dst, send_sem=send_sems.at[4 * w + k],
                                                  recv_sem=recv_sems.at[4 * w + k], device_id=sib, device_id_type=MESH)
                cp.start()
                sent.append(cp)
        for w in range(nw):
            parts = [o_refs[w].at[2 * chip[0] + chip[1], 1 - c] for chip in chips] + [o_refs[w].at[jme]]
            for k, part in enumerate(parts):
                pltpu.make_async_remote_copy(src_ref=part, dst_ref=part, send_sem=send_sems.at[4 * w + k],
                                             recv_sem=recv_sems.at[4 * w + k], device_id=me, device_id_type=MESH).wait_recv()
        for cp in sent:
            cp.wait_send()

    return pl.pallas_call(
        body, name=name,
        in_specs=_any_specs(2 * nw), out_specs=_any_specs(nw),
        out_shape=[jax.ShapeDtypeStruct(a.shape, a.dtype) for a in landed],
        input_output_aliases={nw + i: i for i in range(nw)},
        scratch_shapes=[pltpu.SemaphoreType.DMA((4 * nw,)), pltpu.SemaphoreType.DMA((4 * nw,))],
    )(*shards, *landed)


def _exchange_halves(name, grads, after=()):
    nw = len(grads)

    def body(*refs):
        g_refs, o_refs = refs[:nw], refs[nw + len(after):2 * nw + len(after)]
        send_sems, recv_sems = refs[2 * nw + len(after):]
        x, y, c, _ = _mesh_place()
        copies = []
        for w in range(nw):
            cp = pltpu.make_async_remote_copy(src_ref=g_refs[w].at[:, 1 - c], dst_ref=o_refs[w], send_sem=send_sems.at[w],
                                              recv_sem=recv_sems.at[w], device_id=(x, y, 1 - c), device_id_type=MESH)
            cp.start()
            copies.append(cp)
        for cp in copies:
            cp.wait()

    return pl.pallas_call(
        body, name=name,
        in_specs=_any_specs(nw + len(after)), out_specs=_any_specs(nw),
        out_shape=[jax.ShapeDtypeStruct((N_CHIPS,) + g.shape[2:], g.dtype) for g in grads],
        scratch_shapes=[pltpu.SemaphoreType.DMA((nw,)), pltpu.SemaphoreType.DMA((nw,))],
    )(*grads, *after)


ELEMENTWISE_ROWS = 512


def _row_tile(r):
    for cand in range(min(r, ELEMENTWISE_ROWS) // 16 * 16, 0, -16):
        if r % cand == 0:
            return cand
    return r


def _add_own_halves(name, c_idx, grads, gots):
    n = len(grads)

    def body(c_ref, *refs):
        for g_ref, o_ref, out_ref in zip(refs[:n], refs[n:2 * n], refs[2 * n:]):
            out_ref[...] = (g_ref[...].astype(F32) + o_ref[...].astype(F32)).astype(BF16)

    def whole(a):
        return pl.BlockSpec((None,) + a.shape[1:], lambda j, c_ref: (j, 0, 0))

    return pl.pallas_call(
        body, name=name,
        grid_spec=pltpu.PrefetchScalarGridSpec(
            num_scalar_prefetch=1, grid=(N_CHIPS,),
            in_specs=[pl.BlockSpec((None, None) + g.shape[2:], lambda j, c_ref: (j, c_ref[0], 0, 0)) for g in grads]
            + [whole(o) for o in gots],
            out_specs=[whole(o) for o in gots]),
        out_shape=[jax.ShapeDtypeStruct(o.shape, BF16) for o in gots],
        compiler_params=_params(("parallel",)),
    )(c_idx, *grads, *gots)


def _sum_pieces(place_idx, sums, landed):
    _, r, cols = sums.shape
    tr = _row_tile(r)

    def body(j_ref, own_ref, p_ref, o_ref):
        o_ref[...] = ((own_ref[...].astype(F32) + p_ref[0].astype(F32)) + p_ref[1].astype(F32)) + p_ref[2].astype(F32)

    return pl.pallas_call(
        body, name="sum_pieces",
        grid_spec=pltpu.PrefetchScalarGridSpec(
            num_scalar_prefetch=1, grid=(r // tr,),
            in_specs=[pl.BlockSpec((None, tr, cols), lambda i, j_ref: (j_ref[0], i, 0)),
                      pl.BlockSpec((N_CHIPS - 1, tr, cols), lambda i, j_ref: (0, i, 0))],
            out_specs=pl.BlockSpec((None, tr, cols), lambda i, j_ref: (j_ref[1], i, 0))),
        out_shape=jax.ShapeDtypeStruct((2, r, cols), F32),
        compiler_params=_params(("parallel",)),
    )(place_idx, sums, landed)


def _join_halves(name, halves):
    nw = len(halves)

    def body(*refs):
        o_refs = refs[nw:2 * nw]
        send_sems, recv_sems = refs[2 * nw:]
        x, y, c, _ = _mesh_place()
        copies = []
        for w in range(nw):
            cp = pltpu.make_async_remote_copy(src_ref=o_refs[w].at[c], dst_ref=o_refs[w].at[c], send_sem=send_sems.at[w],
                                              recv_sem=recv_sems.at[w], device_id=(x, y, 1 - c), device_id_type=MESH)
            cp.start()
            copies.append(cp)
        for w in range(nw):
            copies[w].wait_send()
            landed = o_refs[w].at[1 - c]
            pltpu.make_async_remote_copy(src_ref=landed, dst_ref=landed, send_sem=send_sems.at[w], recv_sem=recv_sems.at[w],
                                         device_id=(x, y, c), device_id_type=MESH).wait_recv()

    return pl.pallas_call(
        body, name=name,
        in_specs=_any_specs(nw), out_specs=_any_specs(nw),
        out_shape=[jax.ShapeDtypeStruct(h.shape, F32) for h in halves],
        input_output_aliases={i: i for i in range(nw)},
        scratch_shapes=[pltpu.SemaphoreType.DMA((nw,)), pltpu.SemaphoreType.DMA((nw,))],
    )(*halves)


SMALL_ROWS = 8


def _all_reduce_small(pack):
    rows, cols = pack.shape
    n_dev = 8

    def body(p_ref, o_ref, slots, send_sems, recv_sems):
        x, y, c, _ = _mesh_place()
        me = 4 * x + 2 * y + c
        slots[me] = p_ref[...]
        copies = []
        for k in range(1, n_dev):
            peer = (me + k) % n_dev
            cp = pltpu.make_async_remote_copy(src_ref=p_ref, dst_ref=slots.at[me], send_sem=send_sems.at[k],
                                              recv_sem=recv_sems.at[k],
                                              device_id=(peer // 4, (peer // 2) % 2, peer % 2), device_id_type=MESH)
            cp.start()
            copies.append(cp)
        for k in range(1, n_dev):
            src = (me + n_dev - k) % n_dev
            pltpu.make_async_remote_copy(src_ref=p_ref, dst_ref=slots.at[src], send_sem=send_sems.at[k],
                                         recv_sem=recv_sems.at[k], device_id=(x, y, c), device_id_type=MESH).wait_recv()
        for cp in copies:
            cp.wait_send()
        total = slots[0]
        for s in range(1, n_dev):
            total = total + slots[s]
        o_ref[...] = total

    return pl.pallas_call(
        body, name="all_reduce_small",
        in_specs=[pl.BlockSpec(memory_space=pltpu.VMEM)], out_specs=pl.BlockSpec(memory_space=pltpu.VMEM),
        out_shape=jax.ShapeDtypeStruct((rows, cols), F32),
        scratch_shapes=[pltpu.VMEM((n_dev, rows, cols), F32), pltpu.SemaphoreType.DMA((n_dev,)),
                        pltpu.SemaphoreType.DMA((n_dev,))],
    )(pack)


def _adamw(name, w, g, m, v):
    r, cols = w.shape
    tr = _row_tile(r)

    def body(w_ref, g_ref, m_ref, v_ref, d_ref, nm_ref, nv_ref):
        gv = g_ref[...]
        nm = ADAM_B1 * m_ref[...] + (1.0 - ADAM_B1) * gv
        nv = ADAM_B2 * v_ref[...] + (1.0 - ADAM_B2) * (gv * gv)
        m_hat = nm / (1.0 - ADAM_B1 ** ADAM_STEP)
        v_hat = nv / (1.0 - ADAM_B2 ** ADAM_STEP)
        d_ref[...] = -ADAM_LR * (m_hat / (jnp.sqrt(v_hat) + ADAM_EPS) + ADAM_WD * w_ref[...])
        nm_ref[...] = nm
        nv_ref[...] = nv

    spec = pl.BlockSpec((tr, cols), lambda i: (i, 0))
    return pl.pallas_call(
        body, name=name, grid=(r // tr,),
        in_specs=[spec] * 4, out_specs=[spec] * 3,
        out_shape=[jax.ShapeDtypeStruct((r, cols), F32)] * 3,
        compiler_params=_params(("parallel",)),
    )(w, g, m, v)


BIG = ["ffn1_w_in", "ffn1_w_out", "w_in", "conv_w_proj", "attn_w_o", "w_out", "ffn2_w_in", "ffn2_w_out", "conv_dw_kernel"]
COL_SHARDED = ("ffn1_w_in", "w_in", "ffn2_w_in")
SMALL = ["ffn1_norm", "mix_norm", "ffn2_norm", "conv_dw_bias", "conv_ln_g", "conv_ln_b", "q_norm", "k_norm", "attn_sinks", "rel_bias"]
WEIGHTS = ["ffn1_norm", "ffn1_w_in", "ffn1_w_out", "mix_norm", "w_in", "conv_dw_kernel", "conv_dw_bias", "conv_ln_g",
           "conv_ln_b", "conv_w_proj", "q_norm", "k_norm", "attn_sinks", "rel_bias", "attn_w_o", "w_out", "ffn2_norm",
           "ffn2_w_in", "ffn2_w_out"]
SMALL_PLACE = {"ffn1_norm": (0, 0, 1024), "mix_norm": (1, 0, 1024), "ffn2_norm": (2, 0, 1024), "conv_dw_bias": (3, 0, 1024),
               "conv_ln_g": (4, 0, 1024), "conv_ln_b": (5, 0, 1024), "q_norm": (6, 0, 64), "k_norm": (6, 128, 64),
               "attn_sinks": (6, 256, 16), "rel_bias": (7, 0, 512)}
LOSS_PLACE = (6, 384)


def _pack_small(vals, fill=0.0, loss=None):
    pack = jnp.full((SMALL_ROWS, D_MODEL), fill, F32)
    for name, (row, lane, n) in SMALL_PLACE.items():
        pack = pack.at[row, lane:lane + n].set(vals[name].reshape(n))
    if loss is not None:
        pack = pack.at[LOSS_PLACE[0], LOSS_PLACE[1]].set(loss)
    return pack


def _unpack_small(pack, shapes):
    return {name: pack[row, lane:lane + n].reshape(shapes[name]) for name, (row, lane, n) in SMALL_PLACE.items()}


def _shard_halves(name, a):
    if name == "conv_dw_kernel":
        a = jnp.pad(a, ((0, CONV_PAD - CONV_WIDTH), (0, 0)))
    r, cols = a.shape
    return a.reshape(2, r // 2, cols)


GATHER_GROUPS = {"A": ["ffn1_w_in", "ffn1_w_out"],
                 "B": ["w_in", "conv_dw_kernel", "conv_w_proj", "attn_w_o", "w_out"],
                 "C": ["ffn2_w_in", "ffn2_w_out"]}


class _MeshComm:
    def __init__(self, wts, idle_work=()):
        self.idle_work = idle_work
        self.c_idx = lax.axis_index("c").astype(jnp.int32).reshape(1)
        self.place_idx = jnp.stack([2 * lax.axis_index("x") + lax.axis_index("y"), lax.axis_index("c")]).astype(jnp.int32)
        self.gathers, self.exchanges, self.reductions, self.joins, self.reduced = {}, {}, {}, {}, {}
        self.tokens = ()
        self.shards = {n: _shard_halves(n, wts[n]) if n == "conv_dw_kernel" else _shard_halves(n, wts[n]).astype(BF16)
                       for n in BIG}
        self._gather_start("A", ())

    def _gather_start(self, group, after):
        shards = [self.shards[n] for n in GATHER_GROUPS[group]]
        lands = [lax.empty((N_CHIPS,) + s.shape, s.dtype) for s in shards]
        self.gathers[group] = _copy_start("gather_start_" + group, shards, lands, self._gather_plan(group), after=after)
        self.tokens = (self.gathers[group][-1],)

    @staticmethod
    def _gather_plan(group):
        return _gather_both_cores_plan if group == "C" else _gather_plan

    def weights(self, group, after):
        send_sems, recv_sems, shards, lands, token = self.gathers.pop(group)
        after = [token] if after is None else list(after) if isinstance(after, (list, tuple)) else [after]
        if group == "A":
            after += [self.shards[n] for g in ("B", "C") for n in GATHER_GROUPS[g]] + list(self.idle_work)
        shards, lands = _copy_wait("gather_wait_" + group, send_sems, recv_sems, shards, lands, after,
                                   self._gather_plan(group))
        gathered = lands if group == "C" else _gather_forward("gather_forward_" + group, shards, lands)
        self.tokens = ()
        following = {"A": "B", "B": "C"}.get(group)
        if following:
            self._gather_start(following, (gathered[0],))
        out = {}
        for n, g4 in zip(GATHER_GROUPS[group], gathered):
            r, cols = g4.shape[2] * 2, g4.shape[3]
            if n in COL_SHARDED:
                out[n] = g4.reshape(N_CHIPS, r, cols)
            elif n == "conv_dw_kernel":
                out[n] = g4.reshape(N_CHIPS, r, cols).transpose(1, 0, 2).reshape(r, N_CHIPS * cols)
            else:
                out[n] = g4.reshape(N_CHIPS * r, cols)
        return out

    def reduce_start(self, group, grads, behind=False):
        names = list(grads)
        g4 = []
        for n in names:
            a = grads[n]
            if n == "conv_dw_kernel":
                a = a.reshape(CONV_PAD, N_CHIPS, -1).transpose(1, 0, 2)
            elif n not in COL_SHARDED:
                a = a.reshape(N_CHIPS, a.shape[0] // N_CHIPS, a.shape[1])
            g4.append(a.reshape(N_CHIPS, 2, a.shape[1] // 2, a.shape[2]))
        if behind:
            lands = [lax.empty((N_CHIPS,) + g.shape[2:], g.dtype) for g in g4]
            started = _copy_start("exchange_start_" + group, g4, lands, _exchange_plan)
            self.exchanges[group] = (names,) + started
            return (started[-1],)
        return self._scatter_start(group, names, g4, _exchange_halves("exchange_halves_" + group, g4))

    def exchange_finish(self, group, after):
        names, send_sems, recv_sems, g4, lands, _ = self.exchanges.pop(group)
        g4, got = _copy_wait("exchange_wait_" + group, send_sems, recv_sems, g4, lands, after, _exchange_plan)
        return self._scatter_start(group, names, g4, got)

    def _scatter_start(self, group, names, g4, got):
        sums = _add_own_halves("add_own_halves_" + group, self.c_idx, g4, got)
        lands = [lax.empty((N_CHIPS - 1,) + s.shape[1:], s.dtype) for s in sums]
        started = _copy_start("scatter_start_" + group, sums, lands, _scatter_plan)
        self.reductions[group] = (names,) + started
        return (started[-1],)

    def reduce_finish(self, group, after, behind=False):
        names, send_sems, recv_sems, sums, lands, _ = self.reductions.pop(group)
        sums, lands = _copy_wait("scatter_wait_" + group, send_sems, recv_sems, sums, lands, after, _scatter_plan)
        halves = [_sum_pieces(self.place_idx, s, p) for s, p in zip(sums, lands)]
        if behind:
            started = _copy_start("join_start_" + group, halves, [], _join_plan)
            self.joins[group] = (names,) + started
            return (started[-1],)
        self.reduced.update(zip(names, _join_halves("join_halves_" + group, halves)))
        return ()

    def join_finish(self, group, after):
        names, send_sems, recv_sems, halves, _, _ = self.joins.pop(group)
        self.reduced.update(zip(names, _copy_wait("join_wait_" + group, send_sems, recv_sems, halves, [], after, _join_plan)[0]))


def kernel(x, ffn1_norm, ffn1_w_in, ffn1_w_out, mix_norm, w_in, conv_dw_kernel, conv_dw_bias, conv_ln_g, conv_ln_b, conv_w_proj, q_norm, k_norm, attn_sinks, rel_bias, attn_w_o, w_out, ffn2_norm, ffn2_w_in, ffn2_w_out, loss_target, m_ffn1_norm, m_ffn1_w_in, m_ffn1_w_out, m_mix_norm, m_w_in, m_conv_dw_kernel, m_conv_dw_bias, m_conv_ln_g, m_conv_ln_b, m_conv_w_proj, m_q_norm, m_k_norm, m_attn_sinks, m_rel_bias, m_attn_w_o, m_w_out, m_ffn2_norm, m_ffn2_w_in, m_ffn2_w_out, v_ffn1_norm, v_ffn1_w_in, v_ffn1_w_out, v_mix_norm, v_w_in, v_conv_dw_kernel, v_conv_dw_bias, v_conv_ln_g, v_conv_ln_b, v_conv_w_proj, v_q_norm, v_k_norm, v_attn_sinks, v_rel_bias, v_attn_w_o, v_w_out, v_ffn2_norm, v_ffn2_w_in, v_ffn2_w_out):
    args = dict(locals())
    wts = {n: args[n] for n in WEIGHTS}
    mom = {n: args["m_" + n] for n in WEIGHTS}
    var = {n: args["v_" + n] for n in WEIGHTS}
    small_packs = [_pack_small(wts), _pack_small(mom), _pack_small(var, fill=1.0)]
    dw_moments = [_shard_halves("conv_dw_kernel", a) for a in (wts["conv_dw_kernel"], mom["conv_dw_kernel"], var["conv_dw_kernel"])]
    comm = _MeshComm(wts, idle_work=small_packs + dw_moments)
    small = {n: wts[n] if n in ("attn_sinks", "rel_bias") else wts[n].reshape(1, -1) for n in SMALL}
    loss_part, grad_x, g = _local_step(x[0], loss_target[0], small, comm)

    small_sum = _all_reduce_small(_pack_small(g, loss=loss_part))
    loss = small_sum[LOSS_PLACE[0], LOSS_PLACE[1]]
    small_shapes = {n: wts[n].shape for n in SMALL}
    g_small = _unpack_small(small_sum, small_shapes)

    grads, delta, new_m, new_v = {}, {}, {}, {}
    for n in BIG:
        j = comm.reduced[n]
        gs = j.reshape(j.shape[1] * 2, j.shape[2])
        pad = n == "conv_dw_kernel"
        ws, ms, vs = (a.reshape(gs.shape) for a in (dw_moments if pad else (wts[n], mom[n], var[n])))
        d, nm, nv = _adamw("adamw_" + n, ws, gs, ms, vs)
        cut = (lambda a: a[:CONV_WIDTH]) if pad else (lambda a: a)
        grads[n], delta[n], new_m[n], new_v[n] = cut(gs), cut(d), cut(nm), cut(nv)
    d, nm, nv = _adamw("adamw_small", small_packs[0], small_sum, small_packs[1], small_packs[2])
    grads.update(g_small)
    delta.update(_unpack_small(d, small_shapes))
    new_m.update(_unpack_small(nm, small_shapes))
    new_v.update(_unpack_small(nv, small_shapes))

    return (loss, grad_x[None], *[grads[n] for n in WEIGHTS], *[delta[n] for n in WEIGHTS],
            *[new_m[n] for n in WEIGHTS], *[new_v[n] for n in WEIGHTS])
```

```python
import functools
import math

import jax
import jax.numpy as jnp
from jax import lax
from jax.experimental import pallas as pl
from jax.experimental.pallas import tpu as pltpu

F32 = jnp.float32
BF16 = jnp.bfloat16
MESH = pl.DeviceIdType.MESH

EPS = 1e-6
D_MODEL = 1024
D_FF = 2816
N_CHIPS = 4
SHARD_W = 2 * D_FF // N_CHIPS
HEAD_DIM = 64
N_Q_HEADS = 16
N_KV_HEADS = 4
GROUP = N_Q_HEADS // N_KV_HEADS
BLOCK = 128
QROWS = GROUP * BLOCK
N_BUCKETS = 32
MAX_DISTANCE = 128
CONV_WIDTH = 31
CONV_PAD = 32
NEG = float(jnp.finfo(jnp.float32).min)

ADAM_LR = 0.001
ADAM_B1 = 0.9
ADAM_B2 = 0.999
ADAM_EPS = 1e-08
ADAM_WD = 0.01
ADAM_STEP = 10

VMEM_LIMIT_BYTES = 56 * 1024 * 1024
ROW_TILE = 1024
TOKEN_TILE = 4096
CONV_TILE = 256
CONV_ROWS = 128
LANES = 128

COL_CONV_A, COL_CONV_G, COL_Q, COL_K, COL_V, COL_GC, COL_GA = 0, 1024, 2048, 3072, 3328, 3584, 4608
IN_W = 5632


def _params(sem, vmem=VMEM_LIMIT_BYTES):
    return pltpu.CompilerParams(dimension_semantics=sem, vmem_limit_bytes=vmem)


def _sigmoid(x):
    return 1.0 / (1.0 + jnp.exp(-x))


def _dot(a, b, trans_a=False, trans_b=False, precision=None):
    dn = (((0,) if trans_a else (1,), (1,) if trans_b else (0,)), ((), ()))
    return lax.dot_general(a, b, dn, preferred_element_type=F32, precision=precision)


def _mm(name, grid, a, a_spec, b, b_spec, acc_shape, *, trans_a=False, trans_b=False, a_pre=None, b_pre=None,
        extras=(), extra_specs=(), tokens=(), out_shape, out_specs, epilogue,
        sem=("parallel", "parallel", "arbitrary")):
    n_k = grid[2]
    extras = tuple(extras) + tuple(tokens)
    extra_specs = tuple(extra_specs) + (pl.BlockSpec((8, LANES), lambda i, j, kk: (0, 0)),) * len(tokens)
    n_extra = len(extras)
    n_out = len(out_shape)

    def body(a_ref, b_ref, *rest):
        ex = rest[:n_extra]
        outs = rest[n_extra:n_extra + n_out]
        ids = (pl.program_id(0), pl.program_id(1), pl.program_id(2))
        av = a_ref[...]
        if a_pre is not None:
            av = a_pre(av)
        bv = b_ref[...]
        if b_pre is not None:
            bv = b_pre(bv)
        if n_k == 1:
            epilogue(_dot(av, bv, trans_a, trans_b), ex, outs, ids)
        else:
            acc = rest[-1]

            @pl.when(ids[2] == 0)
            def _():
                acc[...] = jnp.zeros_like(acc)

            acc[...] += _dot(av, bv, trans_a, trans_b)

            @pl.when(ids[2] == n_k - 1)
            def _():
                epilogue(acc[...], ex, outs, ids)

    scratch = [] if n_k == 1 else [pltpu.VMEM(acc_shape, F32)]
    return pl.pallas_call(
        body, name=name, grid=grid,
        in_specs=[a_spec, b_spec, *extra_specs],
        out_specs=list(out_specs), out_shape=list(out_shape),
        scratch_shapes=scratch, compiler_params=_params(sem),
    )(a, b, *extras)


def _half_bf16(v):
    return (0.5 * v).astype(BF16)


def _to_bf16(v):
    return v.astype(BF16)


def _rmsnorm_fwd(name, x, g, tokens=()):
    t, d = x.shape
    tm = min(ROW_TILE, t)

    def body(x_ref, g_ref, *rest):
        o_ref = rest[-1]
        xv = x_ref[...]
        r = lax.rsqrt(jnp.mean(xv * xv, axis=-1, keepdims=True) + EPS)
        o_ref[...] = (xv * r * g_ref[...]).astype(BF16)

    return pl.pallas_call(
        body, name=name, grid=(t // tm,),
        in_specs=[pl.BlockSpec((tm, d), lambda i: (i, 0)), pl.BlockSpec((1, d), lambda i: (0, 0))]
        + [pl.BlockSpec((8, LANES), lambda i: (0, 0))] * len(tokens),
        out_specs=pl.BlockSpec((tm, d), lambda i: (i, 0)),
        out_shape=jax.ShapeDtypeStruct((t, d), BF16),
        compiler_params=_params(("parallel",)),
    )(x, g, *tokens)


def _ffn_in(name, n, w_in4, tokens=()):
    t, d = n.shape
    tm = min(ROW_TILE, t)

    def body(n_ref, wa_ref, wb_ref, *rest):
        ab_ref, h_ref = rest[-2:]
        nv = n_ref[...]
        a = _dot(nv, wa_ref[...])
        b = _dot(nv, wb_ref[...])
        h_ref[...] = (a * _sigmoid(a) * b).astype(BF16)
        ab_ref[0] = a.astype(BF16)
        ab_ref[1] = b.astype(BF16)

    return pl.pallas_call(
        body, name=name, grid=(2, t // tm),
        in_specs=[pl.BlockSpec((tm, d), lambda j, i: (i, 0)),
                  pl.BlockSpec((None, d, SHARD_W), lambda j, i: (j, 0, 0)),
                  pl.BlockSpec((None, d, SHARD_W), lambda j, i: (j + 2, 0, 0))]
        + [pl.BlockSpec((8, LANES), lambda j, i: (0, 0))] * len(tokens),
        out_specs=[pl.BlockSpec((2, tm, SHARD_W), lambda j, i: (0, i, j)),
                   pl.BlockSpec((tm, SHARD_W), lambda j, i: (i, j))],
        out_shape=[jax.ShapeDtypeStruct((2, t, D_FF), BF16), jax.ShapeDtypeStruct((t, D_FF), BF16)],
        compiler_params=_params(("parallel", "parallel")),
    )(n, w_in4, w_in4, *tokens)


def _mm_residual(name, a, w, res, scale, next_gain=None, loss_target=None):
    t, k = a.shape
    n = w.shape[1]
    tm = min(ROW_TILE, t)
    row = pl.BlockSpec((tm, n), lambda i, j, kk: (i, 0))
    extras, specs = [res], [row]
    shapes, out_specs = [jax.ShapeDtypeStruct((t, n), F32)], [row]
    if next_gain is not None:
        extras.append(next_gain)
        specs.append(pl.BlockSpec((1, n), lambda i, j, kk: (0, 0)))
        shapes.append(jax.ShapeDtypeStruct((t, n), BF16))
        out_specs.append(row)
    if loss_target is not None:
        extras.append(loss_target)
        specs.append(row)
        shapes.append(jax.ShapeDtypeStruct((8, LANES), F32))
        out_specs.append(pl.BlockSpec((8, LANES), lambda i, j, kk: (0, 0)))

    def epilogue(acc, ex, outs, ids):
        y = ex[0][...] + scale * acc
        if loss_target is None:
            outs[0][...] = y
        if next_gain is not None:
            r = lax.rsqrt(jnp.mean(y * y, axis=-1, keepdims=True) + EPS)
            outs[1][...] = (y * r * ex[1][...]).astype(BF16)
        if loss_target is not None:
            diff = y - ex[1][...]
            outs[0][...] = diff * (1.0 / n)
            part = jnp.full((8, LANES), 0.5 / n * jnp.sum(diff * diff), F32)

            @pl.when(ids[0] == 0)
            def _():
                outs[1][...] = part

            @pl.when(ids[0] > 0)
            def _():
                outs[1][...] += part

    sem = ("parallel" if loss_target is None else "arbitrary", "parallel", "arbitrary")
    out = _mm(name, (t // tm, 1, 1), a, pl.BlockSpec((tm, k), lambda i, j, kk: (i, 0)),
              w, pl.BlockSpec((k, n), lambda i, j, kk: (0, 0)), (tm, n),
              extras=extras, extra_specs=specs, out_shape=shapes, out_specs=out_specs, epilogue=epilogue, sem=sem)
    return out[0] if len(out) == 1 else tuple(out)


def _ffn_fwd(tag, x, n, w_in4, w_out, tokens=(), **tail):
    ab, h = _ffn_in(tag + "_in", n, w_in4, tokens)
    y = _mm_residual(tag + "_out", h, w_out, x, 0.5, **tail)
    return y, (n, ab, h)


def _ffn_bwd(tag, dres, x, g, saved, w_in4, w_out, tokens=(), on_first=None, on_weight_grads=None):
    n, ab, h = saved
    t, d = x.shape
    tm = min(ROW_TILE, t)
    tk = min(TOKEN_TILE, t)
    half_w = SHARD_W

    def dact_epilogue(acc, ex, outs, ids):
        a = ex[0][0].astype(F32)
        b = ex[0][1].astype(F32)
        sig = _sigmoid(a)
        outs[0][0] = (acc * b * (sig * (1.0 + a * (1.0 - sig)))).astype(BF16)
        outs[0][1] = (acc * (a * sig)).astype(BF16)

    du = _mm(tag + "_dact", (2, t // tm, 1),
             dres, pl.BlockSpec((tm, d), lambda j, i, kk: (i, 0)),
             w_out, pl.BlockSpec((half_w, d), lambda j, i, kk: (j, 0)), (tm, half_w),
             trans_b=True, a_pre=_half_bf16,
             extras=(ab,), extra_specs=(pl.BlockSpec((2, tm, half_w), lambda j, i, kk: (0, i, j)),), tokens=tokens,
             out_shape=(jax.ShapeDtypeStruct((2, t, D_FF), BF16),),
             out_specs=(pl.BlockSpec((2, tm, half_w), lambda j, i, kk: (0, i, j)),),
             epilogue=dact_epilogue)[0]

    def store_epilogue(acc, ex, outs, ids):
        outs[0][...] = acc.astype(BF16)

    early = () if on_first is None else on_first(du)

    tk_out = min(TOKEN_TILE // 2, t)
    dw_out = _mm(tag + "_dwout", (2, 1, t // tk_out),
                 h, pl.BlockSpec((tk_out, half_w), lambda i, j, kk: (kk, i)),
                 dres, pl.BlockSpec((tk_out, d), lambda i, j, kk: (kk, 0)), (half_w, d),
                 trans_a=True, b_pre=_half_bf16, tokens=early,
                 out_shape=(jax.ShapeDtypeStruct((D_FF, d), BF16),),
                 out_specs=(pl.BlockSpec((half_w, d), lambda i, j, kk: (i, 0)),),
                 epilogue=store_epilogue)[0]

    dw_in4 = _mm(tag + "_dwin", (1, N_CHIPS, t // tk),
                 n, pl.BlockSpec((tk, d), lambda i, j, kk: (kk, 0)),
                 du, pl.BlockSpec((None, tk, SHARD_W), lambda i, j, kk: (j // 2, kk, j % 2)), (d, SHARD_W),
                 trans_a=True,
                 out_shape=(jax.ShapeDtypeStruct((N_CHIPS, d, SHARD_W), BF16),),
                 out_specs=(pl.BlockSpec((None, d, SHARD_W), lambda i, j, kk: (j, 0, 0)),),
                 epilogue=store_epilogue)[0]

    late = () if on_weight_grads is None else on_weight_grads(dw_in4, dw_out)

    tn = min(NORM_GRAD_TILE, t)
    shard_specs = [pl.BlockSpec((None, tn, SHARD_W), functools.partial(lambda i, s: (s // 2, i, s % 2), s=s))
                   for s in range(N_CHIPS)]
    dx, dg = _norm_input_grad(tag + "_dn", du, shard_specs, w_in4, x, g, dres, late)
    return dx, dw_in4, dw_out, dg


NORM_GRAD_TILE = 512


def _norm_input_grad(name, a, shard_specs, w4, x, g, dres, tokens=()):
    t, d = x.shape
    tn = min(NORM_GRAD_TILE, t)
    ns = len(shard_specs)

    def body(*refs):
        a_refs, (w_ref, x_ref, g_ref, dres_ref) = refs[:ns], refs[ns:ns + 4]
        out_ref, dg_ref = refs[-2:]
        acc = _dot(a_refs[0][...], w_ref[0], trans_b=True)
        for s in range(1, ns):
            acc = acc + _dot(a_refs[s][...], w_ref[s], trans_b=True)
        xv = x_ref[...]
        r = lax.rsqrt(jnp.mean(xv * xv, axis=-1, keepdims=True) + EPS)
        w = acc * g_ref[...]
        out_ref[...] = dres_ref[...] + (r * w - xv * (r * r * r) * jnp.mean(xv * w, axis=-1, keepdims=True))
        part = jnp.sum(acc * (xv * r), axis=0, keepdims=True)
        i = pl.program_id(0)

        @pl.when(i == 0)
        def _():
            dg_ref[...] = part

        @pl.when(i > 0)
        def _():
            dg_ref[...] += part

    row = pl.BlockSpec((tn, d), lambda i: (i, 0))
    vec = pl.BlockSpec((1, d), lambda i: (0, 0))
    return pl.pallas_call(
        body, name=name, grid=(t // tn,),
        in_specs=list(shard_specs) + [pl.BlockSpec(w4.shape, lambda i: (0, 0, 0)), row, vec, row]
        + [pl.BlockSpec((8, LANES), lambda i: (0, 0))] * len(tokens),
        out_specs=[row, vec],
        out_shape=[jax.ShapeDtypeStruct((t, d), F32), jax.ShapeDtypeStruct((1, d), F32)],
        compiler_params=_params(("arbitrary",)),
    )(*[a] * ns, w4, x, g, dres, *tokens)


def _conv_fill(zp_ref, a_ref, g_ref, ah_ref, gh_ref, i):
    zh = ah_ref[...].astype(F32) * _sigmoid(gh_ref[...].astype(F32))
    zp_ref[pl.ds(0, CONV_PAD), :] = jnp.where(i > 0, zh, 0.0)
    zp_ref[pl.ds(CONV_PAD, a_ref.shape[0]), :] = a_ref[...].astype(F32) * _sigmoid(g_ref[...].astype(F32))


def _shift_groups(shifts):
    groups = {}
    for j, s in shifts:
        groups.setdefault(s % 8, []).append((j, s // 8))
    return groups


def _windows(zp_ref, r0, lanes, groups):
    for q, taps in groups.items():
        deepest = max(p for _, p in taps)
        win = zp_ref[pl.ds(r0 + q, 8 * deepest + CONV_ROWS), lanes]
        for j, p in taps:
            yield j, win[8 * p:8 * p + CONV_ROWS]


def _conv_apply(zp_ref, out_ref, dw_ref, bias_ref, tm, ch, shifts):
    groups = _shift_groups(shifts)
    for cc in range(ch // LANES):
        lanes = pl.ds(cc * LANES, LANES)
        w = [dw_ref[pl.ds(j, 1), lanes] for j in range(CONV_WIDTH)]
        for r0 in range(0, tm, CONV_ROWS):
            if bias_ref is None:
                acc = jnp.zeros((CONV_ROWS, LANES), F32)
            else:
                acc = jnp.broadcast_to(bias_ref[:, lanes], (CONV_ROWS, LANES))
            for j, rows in _windows(zp_ref, r0, lanes, groups):
                acc = acc + w[j] * rows
            out_ref[pl.ds(r0, CONV_ROWS), lanes] = acc


FWD_SHIFTS = [(j, CONV_PAD - (CONV_WIDTH - 1) + j) for j in range(CONV_WIDTH)]
BWD_SHIFTS = [(j, CONV_WIDTH - 1 - j) for j in range(CONV_WIDTH)]


def _conv_taps(zp_ref, z1_ref, dw_ref, bias_ref, tm, ch):
    _conv_apply(zp_ref, z1_ref, dw_ref, bias_ref, tm, ch, FWD_SHIFTS)


def _conv_specs(tm, ch):
    per = tm // CONV_PAD
    cb = COL_CONV_G // ch
    return [pl.BlockSpec((tm, ch), lambda i: (i, 0)),
            pl.BlockSpec((tm, ch), lambda i: (i, cb)),
            pl.BlockSpec((CONV_PAD, ch), lambda i: (jnp.maximum(i * per - 1, 0), 0)),
            pl.BlockSpec((CONV_PAD, ch), lambda i: (jnp.maximum(i * per - 1, 0), cb))]


def _conv_fwd(p, dw, bias, ln_g, ln_b):
    t = p.shape[0]
    ch = D_MODEL
    tm = min(CONV_TILE, t)

    def body(a_ref, g_ref, ah_ref, gh_ref, dw_ref, bias_ref, lg_ref, lb_ref, o_ref, z1_ref, zp_ref):
        i = pl.program_id(0)
        _conv_fill(zp_ref, a_ref, g_ref, ah_ref, gh_ref, i)
        _conv_taps(zp_ref, z1_ref, dw_ref, bias_ref, tm, ch)
        z1 = z1_ref[...]
        mu = jnp.mean(z1, axis=-1, keepdims=True)
        zc = z1 - mu
        rs = lax.rsqrt(jnp.mean(zc * zc, axis=-1, keepdims=True) + EPS)
        z2 = zc * rs * lg_ref[...] + lb_ref[...]
        o_ref[...] = (z2 * _sigmoid(z2)).astype(BF16)

    vec = pl.BlockSpec((1, ch), lambda i: (0, 0))
    return pl.pallas_call(
        body, name="conv_fwd", grid=(t // tm,),
        in_specs=_conv_specs(tm, ch) + [pl.BlockSpec((CONV_PAD, ch), lambda i: (0, 0)), vec, vec, vec],
        out_specs=[pl.BlockSpec((tm, ch), lambda i: (i, 0)), pl.BlockSpec((tm, ch), lambda i: (i, 0))],
        out_shape=[jax.ShapeDtypeStruct((t, ch), BF16), jax.ShapeDtypeStruct((t, ch), F32)],
        scratch_shapes=[pltpu.VMEM((CONV_PAD + tm, ch), F32)],
        compiler_params=_params(("parallel",)),
    )(p, p, p, p, dw, bias, ln_g, ln_b)


def _conv_bwd_ln(p, z1_saved, dz3, ln_g, ln_b):
    t = p.shape[0]
    ch = D_MODEL
    tm = min(CONV_TILE, t)

    def body(a_ref, g_ref, ah_ref, gh_ref, z1_ref, dz3_ref, lg_ref, lb_ref,
             dz1_ref, ddw_ref, dbias_ref, dlg_ref, dlb_ref, zp_ref):
        i = pl.program_id(0)
        _conv_fill(zp_ref, a_ref, g_ref, ah_ref, gh_ref, i)
        z1 = z1_ref[...]
        mu = jnp.mean(z1, axis=-1, keepdims=True)
        zc = z1 - mu
        rs = lax.rsqrt(jnp.mean(zc * zc, axis=-1, keepdims=True) + EPS)
        xh = zc * rs
        z2 = xh * lg_ref[...] + lb_ref[...]
        sig = _sigmoid(z2)
        dz2 = dz3_ref[...].astype(F32) * (sig * (1.0 + z2 * (1.0 - sig)))
        dxh = dz2 * lg_ref[...]
        dz1 = rs * (dxh - jnp.mean(dxh, axis=-1, keepdims=True) - xh * jnp.mean(dxh * xh, axis=-1, keepdims=True))
        dz1_ref[...] = dz1

        @pl.when(i == 0)
        def _():
            ddw_ref[...] = jnp.zeros_like(ddw_ref)
            dbias_ref[...] = jnp.zeros_like(dbias_ref)
            dlg_ref[...] = jnp.zeros_like(dlg_ref)
            dlb_ref[...] = jnp.zeros_like(dlb_ref)

        dlg_ref[...] += jnp.sum(dz2 * xh, axis=0, keepdims=True)
        dlb_ref[...] += jnp.sum(dz2, axis=0, keepdims=True)
        dbias_ref[...] += jnp.sum(dz1, axis=0, keepdims=True)
        groups = _shift_groups(FWD_SHIFTS)
        for cc in range(ch // LANES):
            lanes = pl.ds(cc * LANES, LANES)
            accs = [jnp.zeros((8, LANES), F32) for _ in range(CONV_WIDTH)]
            for r0 in range(0, tm, CONV_ROWS):
                dzc = dz1_ref[pl.ds(r0, CONV_ROWS), lanes]
                for j, rows in _windows(zp_ref, r0, lanes, groups):
                    accs[j] = accs[j] + jnp.sum((dzc * rows).reshape(CONV_ROWS // 8, 8, LANES), axis=0)
            for j in range(CONV_WIDTH):
                ddw_ref[pl.ds(j, 1), lanes] += jnp.sum(accs[j], axis=0, keepdims=True)

    vec = pl.BlockSpec((1, ch), lambda i: (0, 0))
    return pl.pallas_call(
        body, name="conv_bwd_ln", grid=(t // tm,),
        in_specs=_conv_specs(tm, ch) + [pl.BlockSpec((tm, ch), lambda i: (i, 0)),
                                        pl.BlockSpec((tm, ch), lambda i: (i, 0)), vec, vec],
        out_specs=[pl.BlockSpec((tm, ch), lambda i: (i, 0)), pl.BlockSpec((CONV_PAD, ch), lambda i: (0, 0)), vec, vec, vec],
        out_shape=[jax.ShapeDtypeStruct((t, ch), F32), jax.ShapeDtypeStruct((CONV_PAD, ch), F32)]
        + [jax.ShapeDtypeStruct((1, ch), F32)] * 3,
        scratch_shapes=[pltpu.VMEM((CONV_PAD + tm, ch), F32)],
        compiler_params=_params(("arbitrary",)),
    )(p, p, p, p, z1_saved, dz3, ln_g, ln_b)


def _conv_bwd_glu(p, dz1, dw, dq, dkv, dgates):
    t = p.shape[0]
    ch = D_MODEL
    tm = min(CONV_TILE, t)
    per = tm // CONV_PAD
    n_halo = t // CONV_PAD
    cb = COL_CONV_G // ch

    def body(a_ref, g_ref, dz_ref, dzn_ref, dw_ref, dq_ref, dkv_ref, dgates_ref, o_ref, zp_ref, z0_ref):
        i = pl.program_id(0)
        o_ref[:, pl.ds(COL_Q, Q_W)] = dq_ref[...]
        o_ref[:, pl.ds(COL_K, 2 * KV_W)] = dkv_ref[...]
        o_ref[:, pl.ds(COL_GC, ch)] = dgates_ref[0]
        o_ref[:, pl.ds(COL_GA, ch)] = dgates_ref[1]
        zp_ref[pl.ds(0, tm), :] = dz_ref[...]
        zp_ref[pl.ds(tm, CONV_PAD), :] = jnp.where(i < t // tm - 1, dzn_ref[...], 0.0)
        _conv_apply(zp_ref, z0_ref, dw_ref, None, tm, ch, BWD_SHIFTS)
        dz0 = z0_ref[...]
        a = a_ref[...].astype(F32)
        sig = _sigmoid(g_ref[...].astype(F32))
        o_ref[:, pl.ds(0, ch)] = (dz0 * sig).astype(BF16)
        o_ref[:, pl.ds(ch, ch)] = (dz0 * a * sig * (1.0 - sig)).astype(BF16)

    return pl.pallas_call(
        body, name="conv_bwd_glu", grid=(t // tm,),
        in_specs=[pl.BlockSpec((tm, ch), lambda i: (i, 0)), pl.BlockSpec((tm, ch), lambda i: (i, cb)),
                  pl.BlockSpec((tm, ch), lambda i: (i, 0)),
                  pl.BlockSpec((CONV_PAD, ch), lambda i: (jnp.minimum((i + 1) * per, n_halo - 1), 0)),
                  pl.BlockSpec((CONV_PAD, ch), lambda i: (0, 0)),
                  pl.BlockSpec((tm, Q_W), lambda i: (i, 0)), pl.BlockSpec((tm, 2 * KV_W), lambda i: (i, 0)),
                  pl.BlockSpec((2, tm, ch), lambda i: (0, i, 0))],
        out_specs=pl.BlockSpec((tm, IN_W), lambda i: (i, 0)),
        out_shape=jax.ShapeDtypeStruct((t, IN_W), BF16),
        scratch_shapes=[pltpu.VMEM((tm + CONV_PAD, ch), F32), pltpu.VMEM((tm, ch), F32)],
        compiler_params=_params(("parallel",)),
    )(p, p, dz1, dz1, dw, dq, dkv, dgates)


def _bucket_onehot():
    qi = jnp.arange(BLOCK, dtype=jnp.int32)[:, None]
    kj = jnp.arange(2 * BLOCK, dtype=jnp.int32)[None, :]
    dist = jnp.maximum(qi + BLOCK - kj, 0)
    max_exact = N_BUCKETS // 2
    dflt = jnp.maximum(dist, 1).astype(F32)
    large = max_exact + (jnp.log(dflt / max_exact) / math.log(MAX_DISTANCE / max_exact)
                         * (N_BUCKETS - max_exact)).astype(jnp.int32)
    large = jnp.minimum(large, N_BUCKETS - 1)
    bucket = jnp.where(dist < max_exact, dist, large)
    onehot = bucket[None] == jnp.arange(N_BUCKETS, dtype=jnp.int32)[:, None, None]
    return onehot.astype(F32).reshape(N_BUCKETS, BLOCK * 2 * BLOCK)


def _bias_table(rel_bias_t, onehot):
    n = onehot.shape[1]
    tn = 4096

    def body(r_ref, oh_ref, o_ref):
        flat = pl.program_id(0) * tn + lax.broadcasted_iota(jnp.int32, (N_Q_HEADS, tn), 1)
        dist = (flat // (2 * BLOCK)) + BLOCK - (flat % (2 * BLOCK))
        bias = _dot(r_ref[...], oh_ref[...], precision=lax.Precision.HIGHEST)
        o_ref[...] = jnp.where((dist >= 0) & (dist < BLOCK), bias, NEG)

    return pl.pallas_call(
        body, name="bias_table", grid=(n // tn,),
        in_specs=[pl.BlockSpec((N_Q_HEADS, N_BUCKETS), lambda i: (0, 0)), pl.BlockSpec((N_BUCKETS, tn), lambda i: (0, i))],
        out_specs=pl.BlockSpec((N_Q_HEADS, tn), lambda i: (0, i)),
        out_shape=jax.ShapeDtypeStruct((N_Q_HEADS, n), F32),
        compiler_params=_params(("parallel",)),
    )(rel_bias_t, onehot)


def _bias_table_bwd(dbias, onehot):
    n = onehot.shape[1]
    tn = 4096

    def body(d_ref, oh_ref, o_ref):
        part = _dot(d_ref[...], oh_ref[...], trans_b=True, precision=lax.Precision.HIGHEST)
        i = pl.program_id(0)

        @pl.when(i == 0)
        def _():
            o_ref[...] = part

        @pl.when(i > 0)
        def _():
            o_ref[...] += part

    return pl.pallas_call(
        body, name="bias_table_bwd", grid=(n // tn,),
        in_specs=[pl.BlockSpec((N_Q_HEADS, tn), lambda i: (0, i)), pl.BlockSpec((N_BUCKETS, tn), lambda i: (0, i))],
        out_specs=pl.BlockSpec((N_Q_HEADS, N_BUCKETS), lambda i: (0, 0)),
        out_shape=jax.ShapeDtypeStruct((N_Q_HEADS, N_BUCKETS), F32),
        compiler_params=_params(("arbitrary",)),
    )(dbias, onehot)


def _lane_head(rows):
    return lax.broadcasted_iota(jnp.int32, (rows, KV_W), 1) // HEAD_DIM


def _group_rms(x, gain_wide):
    head = _lane_head(x.shape[0])
    sq = x * x
    r = jnp.zeros_like(x)
    for i in range(N_KV_HEADS):
        ms = jnp.sum(jnp.where(head == i, sq, 0.0), axis=-1, keepdims=True) * (1.0 / HEAD_DIM)
        r = jnp.where(head == i, lax.rsqrt(ms + EPS), r)
    return r, x * r * gain_wide


def _stack_heads(group):
    head = _lane_head(group.shape[0])
    return jnp.concatenate([jnp.where(head == i, group, jnp.zeros_like(group)) for i in range(N_KV_HEADS)], axis=0)


def _unstack_heads(stacked):
    head = _lane_head(BLOCK)
    out = jnp.where(head == 0, stacked[:BLOCK], 0.0)
    for i in range(1, N_KV_HEADS):
        out = out + jnp.where(head == i, stacked[i * BLOCK:(i + 1) * BLOCK], 0.0)
    return out


def _repeaters():
    row = lax.broadcasted_iota(jnp.int32, (KV_W, KV_W), 0)
    col = lax.broadcasted_iota(jnp.int32, (KV_W, KV_W), 1)
    return [(row == h * HEAD_DIM + col % HEAD_DIM).astype(BF16) for h in range(N_KV_HEADS)]


def _attn_probs(q_stack, k_rep, sink, bias, before_start):
    s = _dot(q_stack, k_rep, trans_b=True) * (1.0 / math.sqrt(HEAD_DIM)) + bias
    s = jnp.where(before_start, NEG, s)
    m = jnp.maximum(jnp.max(s, axis=-1, keepdims=True), sink)
    p = jnp.exp(s - m)
    es = jnp.exp(sink - m)
    inv = 1.0 / (jnp.sum(p, axis=-1, keepdims=True) + es)
    return p * inv, es * inv


def _before_start(n):
    col = lax.broadcasted_iota(jnp.int32, (QROWS, 2 * BLOCK), 1)
    return (col < BLOCK) & (n == 0)


STEP_BLOCKS = 4
KV_W = N_KV_HEADS * HEAD_DIM
Q_W = N_Q_HEADS * HEAD_DIM


def _attn_specs():
    gain = pl.BlockSpec((1, KV_W), lambda n: (0, 0))
    sink = pl.BlockSpec((N_KV_HEADS, QROWS, 1), lambda n: (0, 0, 0))
    bias = pl.BlockSpec((N_KV_HEADS, QROWS, 2 * BLOCK), lambda n: (0, 0, 0))
    return gain, sink, bias


def _attn_fwd(p, gq, gk, sink_rows, bias):
    t = p.shape[0]
    nb = t // BLOCK
    per = STEP_BLOCKS if nb % STEP_BLOCKS == 0 else 1
    gain, sink, bspec = _attn_specs()

    def body(q_ref, kp_ref, kc_ref, vp_ref, vc_ref, gq_ref, gk_ref, sink_ref, bias_ref, o_ref, p_ref, ps_ref):
        first = pl.program_id(0) * per
        rep = _repeaters()
        kf = jnp.concatenate([kp_ref[...], kc_ref[...]], axis=0).astype(F32)
        kn = _group_rms(kf, gk_ref[...])[1].astype(BF16)
        v = jnp.concatenate([vp_ref[...], vc_ref[...]], axis=0)
        for h in range(N_KV_HEADS):
            k_rep = _dot(kn, rep[h]).astype(BF16)
            v_rep = _dot(v, rep[h]).astype(BF16)
            for sub in range(per):
                rows = pl.ds(sub * BLOCK, BLOCK)
                window = slice(sub * BLOCK, (sub + 2) * BLOCK)
                qn = _group_rms(q_ref[rows, pl.ds(h * KV_W, KV_W)].astype(F32), gq_ref[...])[1]
                pn, ps_ref[sub, h] = _attn_probs(_stack_heads(qn).astype(BF16), k_rep[window], sink_ref[h], bias_ref[h],
                                                 _before_start(first + sub))
                pn = pn.astype(BF16)
                p_ref[sub, h] = pn
                o_ref[rows, pl.ds(h * KV_W, KV_W)] = _unstack_heads(_dot(pn, v_rep[window])).astype(BF16)

    def kv_specs(col):
        return [pl.BlockSpec((BLOCK, KV_W), lambda n: (jnp.maximum(n * per - 1, 0), col // KV_W)),
                pl.BlockSpec((per * BLOCK, KV_W), lambda n: (n, col // KV_W))]

    return pl.pallas_call(
        body, name="attn_fwd", grid=(nb // per,),
        in_specs=[pl.BlockSpec((per * BLOCK, Q_W), lambda n: (n, COL_Q // Q_W))] + kv_specs(COL_K) + kv_specs(COL_V)
        + [gain, gain, sink, bspec],
        out_specs=[pl.BlockSpec((per * BLOCK, Q_W), lambda n: (n, 0)),
                   pl.BlockSpec((per, N_KV_HEADS, QROWS, 2 * BLOCK), lambda n: (n, 0, 0, 0)),
                   pl.BlockSpec((per, N_KV_HEADS, QROWS, 1), lambda n: (n, 0, 0, 0))],
        out_shape=[jax.ShapeDtypeStruct((t, Q_W), BF16),
                   jax.ShapeDtypeStruct((nb, N_KV_HEADS, QROWS, 2 * BLOCK), BF16),
                   jax.ShapeDtypeStruct((nb, N_KV_HEADS, QROWS, 1), F32)],
        compiler_params=_params(("parallel",)),
    )(p, p, p, p, p, gq, gk, sink_rows, bias)


def _attn_bwd(p, do, gq, gk, probs, sink_probs):
    t = p.shape[0]
    nb = t // BLOCK
    per = STEP_BLOCKS if nb % STEP_BLOCKS == 0 else 1
    bspec = _attn_specs()[2]
    gain = pl.BlockSpec((1, HEAD_DIM), lambda n: (0, 0))
    scale = 1.0 / math.sqrt(HEAD_DIM)

    def head_selectors():
        row = lax.broadcasted_iota(jnp.int32, (KV_W, HEAD_DIM), 0)
        col = lax.broadcasted_iota(jnp.int32, (KV_W, HEAD_DIM), 1)
        return [(row == col + i * HEAD_DIM).astype(BF16) for i in range(N_KV_HEADS)]

    def take_heads(group, sel):
        return jnp.concatenate([_dot(group, s) for s in sel], axis=0)

    def put_heads(x, sel):
        rows = x.shape[0] // len(sel)
        out = _dot(x[:rows].astype(BF16), sel[0], trans_b=True)
        for i in range(1, len(sel)):
            out = out + _dot(x[i * rows:(i + 1) * rows].astype(BF16), sel[i], trans_b=True)
        return out

    def rms(x, g):
        r = lax.rsqrt(jnp.mean(x * x, axis=-1, keepdims=True) + EPS)
        return r, x * r * g

    def rms_bwd(dn, xf, r, g):
        w = dn * g
        dx = r * w - xf * (r * r * r) * jnp.mean(xf * w, axis=-1, keepdims=True)
        return dx, jnp.sum(dn * (xf * r), axis=0, keepdims=True)

    def body(q_ref, kp_ref, kc_ref, vp_ref, vc_ref, do_ref, gq_ref, gk_ref, p_ref, ps_ref,
             dq_ref, dkv_ref, dbias_ref, dsink_ref, dgq_ref, dgk_ref):
        n = pl.program_id(0)
        sel = head_selectors()

        @pl.when(n == 0)
        def _():
            dbias_ref[...] = jnp.zeros_like(dbias_ref)
            dsink_ref[...] = jnp.zeros_like(dsink_ref)
            dgq_ref[...] = jnp.zeros_like(dgq_ref)
            dgk_ref[...] = jnp.zeros_like(dgk_ref)

        dgq_sum = jnp.zeros((1, HEAD_DIM), F32)
        dgk_sum = jnp.zeros((1, HEAD_DIM), F32)
        dk_rows, dv_rows = [[] for _ in range(per)], [[] for _ in range(per)]
        for h in range(N_KV_HEADS):
            kf_all = jnp.concatenate([_dot(kp_ref[...], sel[h]), _dot(kc_ref[...], sel[h])], axis=0)
            rk_all, kn_all = rms(kf_all, gk_ref[...])
            kn_all = kn_all.astype(BF16)
            v_all = jnp.concatenate([_dot(vp_ref[...], sel[h]), _dot(vc_ref[...], sel[h])], axis=0).astype(BF16)
            for sub in range(per):
                rows = pl.ds(sub * BLOCK, BLOCK)
                window = slice(sub * BLOCK, (sub + 2) * BLOCK)
                qf = take_heads(q_ref[rows, pl.ds(h * KV_W, KV_W)], sel)
                rq, qn = rms(qf, gq_ref[...])
                pn_bf16, psink = p_ref[sub, h], ps_ref[sub, h]
                pn = pn_bf16.astype(F32)
                do = take_heads(do_ref[rows, pl.ds(h * KV_W, KV_W)], sel).astype(BF16)
                dv_win = _dot(do, pn_bf16, trans_a=True).T
                dp = _dot(do, v_all[window], trans_b=True)
                delta = jnp.sum(pn * dp, axis=-1, keepdims=True)
                ds = pn * (dp - delta)
                dsc = (ds * scale).astype(BF16)
                dqn = _dot(dsc, kn_all[window])
                dkn = _dot(qn.astype(BF16), dsc, trans_a=True).T
                dq, dgq = rms_bwd(dqn, qf, rq, gq_ref[...])
                dk_win, dgk = rms_bwd(dkn, kf_all[window], rk_all[window], gk_ref[...])
                dq_ref[rows, pl.ds(h * KV_W, KV_W)] = put_heads(dq, sel).astype(BF16)
                dk_rows[sub] += [dk_win[:BLOCK], dk_win[BLOCK:]]
                dv_rows[sub] += [dv_win[:BLOCK], dv_win[BLOCK:]]
                dbias_ref[h] += ds
                dsink_ref[h] += jnp.sum((-psink * delta).reshape(GROUP, BLOCK, 1), axis=1)
                dgq_sum = dgq_sum + dgq
                dgk_sum = dgk_sum + dgk
        for sub in range(per):
            for part in range(2):
                dkv_ref[sub, part, :, pl.ds(0, KV_W)] = put_heads(
                    jnp.concatenate(dk_rows[sub][part::2], axis=0), sel).astype(BF16)
                dkv_ref[sub, part, :, pl.ds(KV_W, KV_W)] = put_heads(
                    jnp.concatenate(dv_rows[sub][part::2], axis=0), sel).astype(BF16)
        dgq_ref[...] += dgq_sum
        dgk_ref[...] += dgk_sum

    def kv_specs(col):
        return [pl.BlockSpec((BLOCK, KV_W), lambda n: (jnp.maximum(n * per - 1, 0), col // KV_W)),
                pl.BlockSpec((per * BLOCK, KV_W), lambda n: (n, col // KV_W))]

    row = pl.BlockSpec((per * BLOCK, Q_W), lambda n: (n, 0))
    return pl.pallas_call(
        body, name="attn_bwd", grid=(nb // per,),
        in_specs=[pl.BlockSpec((per * BLOCK, Q_W), lambda n: (n, COL_Q // Q_W))] + kv_specs(COL_K) + kv_specs(COL_V)
        + [row, gain, gain,
           pl.BlockSpec((per, N_KV_HEADS, QROWS, 2 * BLOCK), lambda n: (n, 0, 0, 0)),
           pl.BlockSpec((per, N_KV_HEADS, QROWS, 1), lambda n: (n, 0, 0, 0))],
        out_specs=[row, pl.BlockSpec((per, 2, BLOCK, 2 * KV_W), lambda n: (n, 0, 0, 0)), bspec,
                   pl.BlockSpec((N_KV_HEADS, GROUP, 1), lambda n: (0, 0, 0)), gain, gain],
        out_shape=[jax.ShapeDtypeStruct((t, Q_W), BF16),
                   jax.ShapeDtypeStruct((nb, 2, BLOCK, 2 * KV_W), BF16),
                   jax.ShapeDtypeStruct((N_KV_HEADS, QROWS, 2 * BLOCK), F32),
                   jax.ShapeDtypeStruct((N_KV_HEADS, GROUP, 1), F32),
                   jax.ShapeDtypeStruct((1, HEAD_DIM), F32),
                   jax.ShapeDtypeStruct((1, HEAD_DIM), F32)],
        compiler_params=_params(("arbitrary",)),
    )(p, p, p, p, p, do, gq, gk, probs, sink_probs)


def _kv_window_sum(parts):
    nb = parts.shape[0]

    def body(cur_ref, nxt_ref, o_ref):
        nxt = jnp.where(pl.program_id(0) < nb - 1, nxt_ref[...].astype(F32), 0.0)
        o_ref[...] = (cur_ref[...].astype(F32) + nxt).astype(BF16)

    blk = (None, None, BLOCK, 2 * KV_W)
    return pl.pallas_call(
        body, name="kv_window_sum", grid=(nb,),
        in_specs=[pl.BlockSpec(blk, lambda n: (n, 1, 0, 0)),
                  pl.BlockSpec(blk, lambda n: (jnp.minimum(n + 1, nb - 1), 0, 0, 0))],
        out_specs=pl.BlockSpec((BLOCK, 2 * KV_W), lambda n: (n, 0)),
        out_shape=jax.ShapeDtypeStruct((nb * BLOCK, 2 * KV_W), BF16),
        compiler_params=_params(("parallel",)),
    )(parts, parts)


GATE_TILE = 512


def _merge_fwd(z3, o, p, w_proj, w_o):
    t, d = z3.shape
    tm = min(ROW_TILE, t)
    tn = GATE_TILE

    def body(z_ref, o_ref, gc_ref, ga_ref, wp_ref, wo_ref, m_ref, a_ref, b_ref):
        a = _dot(z_ref[...], wp_ref[...])
        b = _dot(o_ref[...], wo_ref[...])
        m_ref[...] = (_sigmoid(gc_ref[...].astype(F32)) * a + _sigmoid(ga_ref[...].astype(F32)) * b).astype(BF16)
        a_ref[...] = a.astype(BF16)
        b_ref[...] = b.astype(BF16)

    row = pl.BlockSpec((tm, d), lambda i, j: (i, 0))
    wspec = pl.BlockSpec((d, tn), lambda i, j: (0, j))
    ospec = pl.BlockSpec((tm, tn), lambda i, j: (i, j))
    return pl.pallas_call(
        body, name="merge_fwd", grid=(t // tm, d // tn),
        in_specs=[row, row,
                  pl.BlockSpec((tm, tn), lambda i, j: (i, COL_GC // tn + j)),
                  pl.BlockSpec((tm, tn), lambda i, j: (i, COL_GA // tn + j)), wspec, wspec],
        out_specs=[ospec, ospec, ospec],
        out_shape=[jax.ShapeDtypeStruct((t, d), BF16)] * 3,
        compiler_params=_params(("parallel", "parallel")),
    )(z3, o, p, p, w_proj, w_o)


def _merge_bwd(dres, w_out, a, b, p, tokens=()):
    t, d = dres.shape
    tm = min(ROW_TILE, t)
    tn = GATE_TILE

    def epilogue(acc, ex, outs, ids):
        a_ref, b_ref, gc_ref, ga_ref = ex[:4]
        sc = _sigmoid(gc_ref[...].astype(F32))
        sa = _sigmoid(ga_ref[...].astype(F32))
        outs[0][...] = (acc * sc).astype(BF16)
        outs[1][...] = (acc * sa).astype(BF16)
        outs[2][0] = (acc * a_ref[...].astype(F32) * sc * (1.0 - sc)).astype(BF16)
        outs[2][1] = (acc * b_ref[...].astype(F32) * sa * (1.0 - sa)).astype(BF16)

    ospec = pl.BlockSpec((tm, tn), lambda i, j, kk: (i, j))
    return _mm("merge_bwd", (t // tm, d // tn, 1),
               dres, pl.BlockSpec((tm, d), lambda i, j, kk: (i, 0)),
               w_out, pl.BlockSpec((tn, d), lambda i, j, kk: (j, 0)), (tm, tn),
               trans_b=True, a_pre=_to_bf16,
               extras=(a, b, p, p),
               extra_specs=(ospec, ospec,
                            pl.BlockSpec((tm, tn), lambda i, j, kk: (i, COL_GC // tn + j)),
                            pl.BlockSpec((tm, tn), lambda i, j, kk: (i, COL_GA // tn + j))), tokens=tokens,
               out_shape=(jax.ShapeDtypeStruct((t, d), BF16), jax.ShapeDtypeStruct((t, d), BF16),
                          jax.ShapeDtypeStruct((2, t, d), BF16)),
               out_specs=(ospec, ospec, pl.BlockSpec((2, tm, tn), lambda i, j, kk: (0, i, j))),
               epilogue=epilogue)


def _store_epilogue(acc, ex, outs, ids):
    outs[0][...] = acc


def _store_bf16_epilogue(acc, ex, outs, ids):
    outs[0][...] = acc.astype(BF16)


def _mm_nt(name, a, w, out_dtype=BF16):
    t, n = a.shape
    k = w.shape[0]
    tm = min(ROW_TILE, t)
    return _mm(name, (t // tm, 1, 1), a, pl.BlockSpec((tm, n), lambda i, j, kk: (i, 0)),
               w, pl.BlockSpec((k, n), lambda i, j, kk: (0, 0)), (tm, k), trans_b=True,
               out_shape=(jax.ShapeDtypeStruct((t, k), out_dtype),),
               out_specs=(pl.BlockSpec((tm, k), lambda i, j, kk: (i, 0)),),
               epilogue=_store_bf16_epilogue if out_dtype == BF16 else _store_epilogue)[0]


def _mm_tn(name, a, b, b_pre=None, tokens=()):
    t, m = a.shape
    n = b.shape[1]
    tk = min(TOKEN_TILE // (2 if b.dtype == F32 else 1), t)
    return _mm(name, (1, 1, t // tk), a, pl.BlockSpec((tk, m), lambda i, j, kk: (kk, 0)),
               b, pl.BlockSpec((tk, n), lambda i, j, kk: (kk, 0)), (m, n), trans_a=True, b_pre=b_pre, tokens=tokens,
               out_shape=(jax.ShapeDtypeStruct((m, n), BF16),),
               out_specs=(pl.BlockSpec((m, n), lambda i, j, kk: (0, 0)),), epilogue=_store_bf16_epilogue)[0]


def _local_step(x, target, small, comm):
    t = x.shape[0]
    w = dict(small)

    n1 = _rmsnorm_fwd("ffn1_norm", x, w["ffn1_norm"])
    onehot = _bucket_onehot()
    bias = _bias_table(w["rel_bias"].T, onehot).reshape(N_KV_HEADS, QROWS, 2 * BLOCK)
    sink_rows = jnp.repeat(w["attn_sinks"].reshape(N_KV_HEADS, GROUP), BLOCK, axis=1)[..., None]
    gq_wide = jnp.tile(w["q_norm"], (1, N_KV_HEADS))
    gk_wide = jnp.tile(w["k_norm"], (1, N_KV_HEADS))
    w.update(comm.weights("A", [n1, onehot, bias, sink_rows, gq_wide, gk_wide]))
    (x1, hm), ffn1_saved = _ffn_fwd("ffn1", x, n1, w["ffn1_w_in"], w["ffn1_w_out"], comm.tokens,
                                    next_gain=w["mix_norm"])
    w.update(comm.weights("B", x1))
    tm = min(ROW_TILE, t)
    p = _mm("mix_in", (N_CHIPS, t // tm, 1),
            hm, pl.BlockSpec((tm, D_MODEL), lambda j, i, kk: (i, 0)),
            w["w_in"], pl.BlockSpec((None, D_MODEL, SHARD_W), lambda j, i, kk: (j, 0, 0)), (tm, SHARD_W),
            tokens=comm.tokens,
            out_shape=(jax.ShapeDtypeStruct((t, IN_W), BF16),),
            out_specs=(pl.BlockSpec((tm, SHARD_W), lambda j, i, kk: (i, j)),),
            epilogue=_store_bf16_epilogue)[0]

    z3, z1 = _conv_fwd(p, w["conv_dw_kernel"], w["conv_dw_bias"], w["conv_ln_g"], w["conv_ln_b"])

    o, probs, sink_probs = _attn_fwd(p, gq_wide, gk_wide, sink_rows, bias)

    merged, a, b = _merge_fwd(z3, o, p, w["conv_w_proj"], w["attn_w_o"])
    x2, n2 = _mm_residual("mix_out", merged, w["w_out"], x1, 1.0, next_gain=w["ffn2_norm"])
    w.update(comm.weights("C", n2))
    (dy, loss), ffn2_saved = _ffn_fwd("ffn2", x2, n2, w["ffn2_w_in"], w["ffn2_w_out"], loss_target=target)

    g, big = {}, {}
    dres2, big["ffn2_w_in"], big["ffn2_w_out"], g["ffn2_norm"] = _ffn_bwd(
        "ffn2b", dy, x2, w["ffn2_norm"], ffn2_saved, w["ffn2_w_in"], w["ffn2_w_out"])
    tokens = comm.reduce_start("R1", big, behind=True)

    da, db, dgates = _merge_bwd(dres2, w["w_out"], a, b, p, tokens)
    tokens = comm.exchange_finish("R1", da)
    big = {}
    big["w_out"] = _mm_tn("d_w_out", merged, dres2, b_pre=_to_bf16, tokens=tokens)
    big["conv_w_proj"] = _mm_tn("d_w_proj", z3, da)
    big["attn_w_o"] = _mm_tn("d_w_o", o, db)
    dz3 = _mm_nt("d_z3", da, w["conv_w_proj"])
    do = _mm_nt("d_o", db, w["attn_w_o"])

    dq, dkv_parts, dbias, dsink, g["q_norm"], g["k_norm"] = _attn_bwd(
        p, do, w["q_norm"], w["k_norm"], probs, sink_probs)
    dkv = _kv_window_sum(dkv_parts)
    g["rel_bias"] = _bias_table_bwd(dbias.reshape(N_Q_HEADS, BLOCK * 2 * BLOCK), onehot).T
    g["attn_sinks"] = dsink.reshape(N_Q_HEADS)

    dz1, big["conv_dw_kernel"], g["conv_dw_bias"], g["conv_ln_g"], g["conv_ln_b"] = _conv_bwd_ln(
        p, z1, dz3, w["conv_ln_g"], w["conv_ln_b"])
    dp = _conv_bwd_glu(p, dz1, w["conv_dw_kernel"], dq, dkv, dgates)
    tk = min(TOKEN_TILE, t)
    big["w_in"] = _mm("d_w_in", (1, N_CHIPS, t // tk),
                    hm, pl.BlockSpec((tk, D_MODEL), lambda i, j, kk: (kk, 0)),
                    dp, pl.BlockSpec((tk, SHARD_W), lambda i, j, kk: (kk, j)), (D_MODEL, SHARD_W),
                    trans_a=True,
                    out_shape=(jax.ShapeDtypeStruct((N_CHIPS, D_MODEL, SHARD_W), BF16),),
                    out_specs=(pl.BlockSpec((None, D_MODEL, SHARD_W), lambda i, j, kk: (j, 0, 0)),),
                    epilogue=_store_bf16_epilogue)[0]
    tn = min(NORM_GRAD_TILE, t)
    dres1, g["mix_norm"] = _norm_input_grad(
        "d_mix", dp, [pl.BlockSpec((tn, SHARD_W), functools.partial(lambda i, s: (i, s), s=s)) for s in range(N_CHIPS)],
        w["w_in"], x1, w["mix_norm"], dres2)

    tokens = comm.reduce_finish("R1", dres1, behind=True) + comm.reduce_start("R2", big, behind=True)

    def ffn1_first(du):
        comm.join_finish("R1", du)
        return comm.exchange_finish("R2", du)

    def ffn1_grads(dw_in4, dw_out):
        late = comm.reduce_finish("R2", dw_in4, behind=True)
        return late + comm.reduce_start("R3", {"ffn1_w_in": dw_in4, "ffn1_w_out": dw_out})

    grad_x, _, _, g["ffn1_norm"] = _ffn_bwd(
        "ffn1b", dres1, x, w["ffn1_norm"], ffn1_saved, w["ffn1_w_in"], w["ffn1_w_out"], tokens, ffn1_first, ffn1_grads)
    comm.join_finish("R2", grad_x)
    comm.reduce_finish("R3", grad_x)
    return loss[0, 0], grad_x, g


def _mesh_place():
    x, y, c = lax.axis_index("x"), lax.axis_index("y"), lax.axis_index("c")
    chips = [(1 - x, y), (x, 1 - y), (1 - x, 1 - y)]
    return x, y, c, chips


def _any_specs(n):
    return [pl.BlockSpec(memory_space=pl.ANY)] * n


HBM_SPEC = pl.BlockSpec(memory_space=pltpu.HBM)
SEM_SPEC = pl.BlockSpec(memory_space=pltpu.SEMAPHORE)
EFFECT = pltpu.SideEffectType.DATAFLOW_SIDE_EFFECTING


def _in_hbm(a):
    return pltpu.with_memory_space_constraint(a, pltpu.HBM)


def _copy_start(name, srcs, lands, plan, after=()):
    ns, nb = len(srcs), len(lands)
    n = plan.copies_per_source * ns

    def body(*refs):
        s_refs, l_refs = refs[:ns], refs[ns:ns + nb]
        send_sems, recv_sems = refs[ns + nb + len(after)], refs[ns + nb + len(after) + 1]
        token = refs[-1]
        for k, (src, dst, to, _) in enumerate(plan(s_refs, l_refs)):
            pltpu.make_async_remote_copy(src_ref=src, dst_ref=dst, send_sem=send_sems.at[k], recv_sem=recv_sems.at[k],
                                         device_id=to, device_id_type=MESH).start()
        token[...] = jnp.zeros_like(token)

    bufs = list(srcs) + list(lands)
    outs = pl.pallas_call(
        body, name=name,
        out_shape=(pltpu.SemaphoreType.DMA((n,)), pltpu.SemaphoreType.DMA((n,)),
                   *[pltpu.HBM(a.shape, a.dtype) for a in bufs], jax.ShapeDtypeStruct((8, LANES), F32)),
        in_specs=[HBM_SPEC] * len(bufs) + [pl.BlockSpec(memory_space=pl.ANY)] * len(after),
        out_specs=(SEM_SPEC, SEM_SPEC, *[HBM_SPEC] * len(bufs), pl.BlockSpec(memory_space=pltpu.VMEM)),
        input_output_aliases={i: 2 + i for i in range(len(bufs))},
        compiler_params=pltpu.CompilerParams(has_side_effects=EFFECT),
    )(*[_in_hbm(a) for a in bufs], *after)
    return outs[0], outs[1], list(outs[2:2 + ns]), list(outs[2 + ns:2 + ns + nb]), outs[-1]


def _copy_wait(name, send_sems, recv_sems, srcs, lands, after, plan):
    ns, nb = len(srcs), len(lands)
    after = tuple(after) if isinstance(after, (tuple, list)) else (after,)

    def body(*refs):
        s_refs, l_refs = refs[:ns], refs[ns:ns + nb]
        send_sems, recv_sems = refs[ns + nb], refs[ns + nb + 1]
        for k, (src, _, to, mine) in enumerate(plan(s_refs, l_refs)):
            cp = pltpu.make_async_remote_copy(src_ref=src, dst_ref=mine, send_sem=send_sems.at[k], recv_sem=recv_sems.at[k],
                                              device_id=to, device_id_type=MESH)
            cp.wait_send()
            cp.wait_recv()

    bufs = list(srcs) + list(lands)
    outs = pl.pallas_call(
        body, name=name,
        out_shape=tuple(pltpu.HBM(a.shape, a.dtype) for a in bufs),
        in_specs=[HBM_SPEC] * len(bufs) + [SEM_SPEC, SEM_SPEC] + [pl.BlockSpec(memory_space=pl.ANY)] * len(after),
        out_specs=tuple([HBM_SPEC] * len(bufs)),
        input_output_aliases={i: i for i in range(len(bufs))},
        compiler_params=pltpu.CompilerParams(has_side_effects=EFFECT),
    )(*bufs, send_sems, recv_sems, *after)
    return list(outs[:ns]), list(outs[ns:])


def _gather_plan(s_refs, l_refs):
    x, y, c, chips = _mesh_place()
    jme = 2 * x + y
    return [(s.at[c], land.at[jme, c], (*chip, c), land.at[2 * chip[0] + chip[1], c])
            for s, land in zip(s_refs, l_refs) for chip in chips]


_gather_plan.copies_per_source = 3


def _gather_both_cores_plan(s_refs, l_refs):
    x, y, c, chips = _mesh_place()
    jme = 2 * x + y
    plan = []
    for s, land in zip(s_refs, l_refs):
        for chip in chips:
            for peer_core in (c, 1 - c):
                plan.append((s.at[c], land.at[jme, c], (*chip, peer_core), land.at[2 * chip[0] + chip[1], peer_core]))
        plan.append((s, land.at[jme], (x, y, 1 - c), land.at[jme]))
    return plan


_gather_both_cores_plan.copies_per_source = 7


def _scatter_plan(s_refs, l_refs):
    x, y, c, chips = _mesh_place()
    return [(s.at[2 * chip[0] + chip[1]], land.at[k], (*chip, c), land.at[k])
            for s, land in zip(s_refs, l_refs) for k, chip in enumerate(chips)]


_scatter_plan.copies_per_source = 3


def _exchange_plan(g_refs, l_refs):
    x, y, c, _ = _mesh_place()
    return [(g.at[:, 1 - c], land, (x, y, 1 - c), land) for g, land in zip(g_refs, l_refs)]


_exchange_plan.copies_per_source = 1


def _join_plan(h_refs, l_refs):
    x, y, c, _ = _mesh_place()
    return [(h.at[c], h.at[c], (x, y, 1 - c), h.at[1 - c]) for h in h_refs]


_join_plan.copies_per_source = 1


def _gather_forward(name, shards, landed):
    nw = len(shards)

    def body(*refs):
        s_refs, o_refs = refs[:nw], refs[2 * nw:3 * nw]
        send_sems, recv_sems = refs[3 * nw:]
        x, y, c, chips = _mesh_place()
        me, sib, jme = (x, y, c), (x, y, 1 - c), 2 * x + y
        sent = []
        for w in range(nw):
            parts = [(o_refs[w].at[2 * chip[0] + chip[1], c], o_refs[w].at[2 * chip[0] + chip[1], c]) for chip in chips]
            parts.append((s_refs[w], o_refs[w].at[jme]))
            for k, (src, dst) in enumerate(parts):
                cp = pltpu.make_async_remote_copy(src_ref=src, dst_ref=dst, send_sem=send_sems.at[4 * w + k],
                                                  recv_sem=recv_sems.at[4 * w + k], device_id=sib, device_id_type=MESH)
                cp.start()
                sent.append(cp)
        for w in range(nw):
            parts = [o_refs[w].at[2 * chip[0] + chip[1], 1 - c] for chip in chips] + [o_refs[w].at[jme]]
            for k, part in enumerate(parts):
                pltpu.make_async_remote_copy(src_ref=part, dst_ref=part, send_sem=send_sems.at[4 * w + k],
                                             recv_sem=recv_sems.at[4 * w + k], device_id=me, device_id_type=MESH).wait_recv()
        for cp in sent:
            cp.wait_send()

    return pl.pallas_call(
        body, name=name,
        in_specs=_any_specs(2 * nw), out_specs=_any_specs(nw),
        out_shape=[jax.ShapeDtypeStruct(a.shape, a.dtype) for a in landed],
        input_output_aliases={nw + i: i for i in range(nw)},
        scratch_shapes=[pltpu.SemaphoreType.DMA((4 * nw,)), pltpu.SemaphoreType.DMA((4 * nw,))],
    )(*shards, *landed)


def _exchange_halves(name, grads, after=()):
    nw = len(grads)

    def body(*refs):
        g_refs, o_refs = refs[:nw], refs[nw + len(after):2 * nw + len(after)]
        send_sems, recv_sems = refs[2 * nw + len(after):]
        x, y, c, _ = _mesh_place()
        copies = []
        for w in range(nw):
            cp = pltpu.make_async_remote_copy(src_ref=g_refs[w].at[:, 1 - c], dst_ref=o_refs[w], send_sem=send_sems.at[w],
                                              recv_sem=recv_sems.at[w], device_id=(x, y, 1 - c), device_id_type=MESH)
            cp.start()
            copies.append(cp)
        for cp in copies:
            cp.wait()

    return pl.pallas_call(
        body, name=name,
        in_specs=_any_specs(nw + len(after)), out_specs=_any_specs(nw),
        out_shape=[jax.ShapeDtypeStruct((N_CHIPS,) + g.shape[2:], g.dtype) for g in grads],
        scratch_shapes=[pltpu.SemaphoreType.DMA((nw,)), pltpu.SemaphoreType.DMA((nw,))],
    )(*grads, *after)


ELEMENTWISE_ROWS = 512


def _row_tile(r):
    for cand in range(min(r, ELEMENTWISE_ROWS) // 16 * 16, 0, -16):
        if r % cand == 0:
            return cand
    return r


def _add_own_halves(name, c_idx, grads, gots):
    n = len(grads)

    def body(c_ref, *refs):
        for g_ref, o_ref, out_ref in zip(refs[:n], refs[n:2 * n], refs[2 * n:]):
            out_ref[...] = (g_ref[...].astype(F32) + o_ref[...].astype(F32)).astype(BF16)

    def whole(a):
        return pl.BlockSpec((None,) + a.shape[1:], lambda j, c_ref: (j, 0, 0))

    return pl.pallas_call(
        body, name=name,
        grid_spec=pltpu.PrefetchScalarGridSpec(
            num_scalar_prefetch=1, grid=(N_CHIPS,),
            in_specs=[pl.BlockSpec((None, None) + g.shape[2:], lambda j, c_ref: (j, c_ref[0], 0, 0)) for g in grads]
            + [whole(o) for o in gots],
            out_specs=[whole(o) for o in gots]),
        out_shape=[jax.ShapeDtypeStruct(o.shape, BF16) for o in gots],
        compiler_params=_params(("parallel",)),
    )(c_idx, *grads, *gots)


def _sum_pieces(name, place_idx, sums, landed):
    n = len(sums)

    def body(j_ref, *refs):
        for own_ref, p_ref, o_ref in zip(refs[:n], refs[n:2 * n], refs[2 * n:]):
            o_ref[...] = ((own_ref[...].astype(F32) + p_ref[0].astype(F32)) + p_ref[1].astype(F32)) + p_ref[2].astype(F32)

    return pl.pallas_call(
        body, name=name,
        grid_spec=pltpu.PrefetchScalarGridSpec(
            num_scalar_prefetch=1, grid=(1,),
            in_specs=[pl.BlockSpec((None,) + s.shape[1:], lambda i, j_ref: (j_ref[0], 0, 0)) for s in sums]
            + [pl.BlockSpec(p.shape, lambda i, j_ref: (0, 0, 0)) for p in landed],
            out_specs=[pl.BlockSpec((None,) + s.shape[1:], lambda i, j_ref: (j_ref[1], 0, 0)) for s in sums]),
        out_shape=[jax.ShapeDtypeStruct((2,) + s.shape[1:], F32) for s in sums],
        compiler_params=_params(("arbitrary",)),
    )(place_idx, *sums, *landed)


def _join_halves(name, halves):
    nw = len(halves)

    def body(*refs):
        o_refs = refs[nw:2 * nw]
        send_sems, recv_sems = refs[2 * nw:]
        x, y, c, _ = _mesh_place()
        copies = []
        for w in range(nw):
            cp = pltpu.make_async_remote_copy(src_ref=o_refs[w].at[c], dst_ref=o_refs[w].at[c], send_sem=send_sems.at[w],
                                              recv_sem=recv_sems.at[w], device_id=(x, y, 1 - c), device_id_type=MESH)
            cp.start()
            copies.append(cp)
        for w in range(nw):
            copies[w].wait_send()
            landed = o_refs[w].at[1 - c]
            pltpu.make_async_remote_copy(src_ref=landed, dst_ref=landed, send_sem=send_sems.at[w], recv_sem=recv_sems.at[w],
                                         device_id=(x, y, c), device_id_type=MESH).wait_recv()

    return pl.pallas_call(
        body, name=name,
        in_specs=_any_specs(nw), out_specs=_any_specs(nw),
        out_shape=[jax.ShapeDtypeStruct(h.shape, F32) for h in halves],
        input_output_aliases={i: i for i in range(nw)},
        scratch_shapes=[pltpu.SemaphoreType.DMA((nw,)), pltpu.SemaphoreType.DMA((nw,))],
    )(*halves)


SMALL_ROWS = 8


def _all_reduce_small(pack):
    rows, cols = pack.shape
    n_dev = 8

    def body(p_ref, o_ref, slots, send_sems, recv_sems):
        x, y, c, _ = _mesh_place()
        me = 4 * x + 2 * y + c
        slots[me] = p_ref[...]
        copies = []
        for k in range(1, n_dev):
            peer = (me + k) % n_dev
            cp = pltpu.make_async_remote_copy(src_ref=p_ref, dst_ref=slots.at[me], send_sem=send_sems.at[k],
                                              recv_sem=recv_sems.at[k],
                                              device_id=(peer // 4, (peer // 2) % 2, peer % 2), device_id_type=MESH)
            cp.start()
            copies.append(cp)
        for k in range(1, n_dev):
            src = (me + n_dev - k) % n_dev
            pltpu.make_async_remote_copy(src_ref=p_ref, dst_ref=slots.at[src], send_sem=send_sems.at[k],
                                         recv_sem=recv_sems.at[k], device_id=(x, y, c), device_id_type=MESH).wait_recv()
        for cp in copies:
            cp.wait_send()
        total = slots[0]
        for s in range(1, n_dev):
            total = total + slots[s]
        o_ref[...] = total

    return pl.pallas_call(
        body, name="all_reduce_small",
        in_specs=[pl.BlockSpec(memory_space=pltpu.VMEM)], out_specs=pl.BlockSpec(memory_space=pltpu.VMEM),
        out_shape=jax.ShapeDtypeStruct((rows, cols), F32),
        scratch_shapes=[pltpu.VMEM((n_dev, rows, cols), F32), pltpu.SemaphoreType.DMA((n_dev,)),
                        pltpu.SemaphoreType.DMA((n_dev,))],
    )(pack)


def _adamw(name, w, g, m, v):
    r, cols = w.shape
    tr = _row_tile(r)

    def body(w_ref, g_ref, m_ref, v_ref, d_ref, nm_ref, nv_ref):
        gv = g_ref[...]
        nm = ADAM_B1 * m_ref[...] + (1.0 - ADAM_B1) * gv
        nv = ADAM_B2 * v_ref[...] + (1.0 - ADAM_B2) * (gv * gv)
        m_hat = nm / (1.0 - ADAM_B1 ** ADAM_STEP)
        v_hat = nv / (1.0 - ADAM_B2 ** ADAM_STEP)
        d_ref[...] = -ADAM_LR * (m_hat / (jnp.sqrt(v_hat) + ADAM_EPS) + ADAM_WD * w_ref[...])
        nm_ref[...] = nm
        nv_ref[...] = nv

    spec = pl.BlockSpec((tr, cols), lambda i: (i, 0))
    return pl.pallas_call(
        body, name=name, grid=(r // tr,),
        in_specs=[spec] * 4, out_specs=[spec] * 3,
        out_shape=[jax.ShapeDtypeStruct((r, cols), F32)] * 3,
        compiler_params=_params(("parallel",)),
    )(w, g, m, v)


BIG = ["ffn1_w_in", "ffn1_w_out", "w_in", "conv_w_proj", "attn_w_o", "w_out", "ffn2_w_in", "ffn2_w_out", "conv_dw_kernel"]
COL_SHARDED = ("ffn1_w_in", "w_in", "ffn2_w_in")
SMALL = ["ffn1_norm", "mix_norm", "ffn2_norm", "conv_dw_bias", "conv_ln_g", "conv_ln_b", "q_norm", "k_norm", "attn_sinks", "rel_bias"]
WEIGHTS = ["ffn1_norm", "ffn1_w_in", "ffn1_w_out", "mix_norm", "w_in", "conv_dw_kernel", "conv_dw_bias", "conv_ln_g",
           "conv_ln_b", "conv_w_proj", "q_norm", "k_norm", "attn_sinks", "rel_bias", "attn_w_o", "w_out", "ffn2_norm",
           "ffn2_w_in", "ffn2_w_out"]
SMALL_PLACE = {"ffn1_norm": (0, 0, 1024), "mix_norm": (1, 0, 1024), "ffn2_norm": (2, 0, 1024), "conv_dw_bias": (3, 0, 1024),
               "conv_ln_g": (4, 0, 1024), "conv_ln_b": (5, 0, 1024), "q_norm": (6, 0, 64), "k_norm": (6, 128, 64),
               "attn_sinks": (6, 256, 16), "rel_bias": (7, 0, 512)}
LOSS_PLACE = (6, 384)


def _pack_small(vals, fill=0.0, loss=None):
    pack = jnp.full((SMALL_ROWS, D_MODEL), fill, F32)
    for name, (row, lane, n) in SMALL_PLACE.items():
        pack = pack.at[row, lane:lane + n].set(vals[name].reshape(n))
    if loss is not None:
        pack = pack.at[LOSS_PLACE[0], LOSS_PLACE[1]].set(loss)
    return pack


def _unpack_small(pack, shapes):
    return {name: pack[row, lane:lane + n].reshape(shapes[name]) for name, (row, lane, n) in SMALL_PLACE.items()}


def _shard_halves(name, a):
    if name == "conv_dw_kernel":
        a = jnp.pad(a, ((0, CONV_PAD - CONV_WIDTH), (0, 0)))
    r, cols = a.shape
    return a.reshape(2, r // 2, cols)


GATHER_GROUPS = {"A": ["ffn1_w_in", "ffn1_w_out"],
                 "B": ["w_in", "conv_dw_kernel", "conv_w_proj", "attn_w_o", "w_out"],
                 "C": ["ffn2_w_in", "ffn2_w_out"]}


class _MeshComm:
    def __init__(self, wts, idle_work=()):
        self.idle_work = idle_work
        self.c_idx = lax.axis_index("c").astype(jnp.int32).reshape(1)
        self.place_idx = jnp.stack([2 * lax.axis_index("x") + lax.axis_index("y"), lax.axis_index("c")]).astype(jnp.int32)
        self.gathers, self.exchanges, self.reductions, self.joins, self.reduced = {}, {}, {}, {}, {}
        self.tokens = ()
        self.shards = {n: _shard_halves(n, wts[n]) if n == "conv_dw_kernel" else _shard_halves(n, wts[n]).astype(BF16)
                       for n in BIG}
        self._gather_start("A", ())

    def _gather_start(self, group, after):
        shards = [self.shards[n] for n in GATHER_GROUPS[group]]
        lands = [lax.empty((N_CHIPS,) + s.shape, s.dtype) for s in shards]
        self.gathers[group] = _copy_start("gather_start_" + group, shards, lands, self._gather_plan(group), after=after)
        self.tokens = (self.gathers[group][-1],)

    @staticmethod
    def _gather_plan(group):
        return _gather_both_cores_plan if group == "C" else _gather_plan

    def weights(self, group, after):
        send_sems, recv_sems, shards, lands, token = self.gathers.pop(group)
        after = [token] if after is None else list(after) if isinstance(after, (list, tuple)) else [after]
        if group == "A":
            after += [self.shards[n] for g in ("B", "C") for n in GATHER_GROUPS[g]] + list(self.idle_work)
        shards, lands = _copy_wait("gather_wait_" + group, send_sems, recv_sems, shards, lands, after,
                                   self._gather_plan(group))
        gathered = lands if group == "C" else _gather_forward("gather_forward_" + group, shards, lands)
        self.tokens = ()
        following = {"A": "B", "B": "C"}.get(group)
        if following:
            self._gather_start(following, (gathered[0],))
        out = {}
        for n, g4 in zip(GATHER_GROUPS[group], gathered):
            r, cols = g4.shape[2] * 2, g4.shape[3]
            if n in COL_SHARDED:
                out[n] = g4.reshape(N_CHIPS, r, cols)
            elif n == "conv_dw_kernel":
                out[n] = g4.reshape(N_CHIPS, r, cols).transpose(1, 0, 2).reshape(r, N_CHIPS * cols)
            else:
                out[n] = g4.reshape(N_CHIPS * r, cols)
        return out

    def reduce_start(self, group, grads, behind=False):
        names = list(grads)
        g4 = []
        for n in names:
            a = grads[n]
            if n == "conv_dw_kernel":
                a = a.reshape(CONV_PAD, N_CHIPS, -1).transpose(1, 0, 2)
            elif n not in COL_SHARDED:
                a = a.reshape(N_CHIPS, a.shape[0] // N_CHIPS, a.shape[1])
            g4.append(a.reshape(N_CHIPS, 2, a.shape[1] // 2, a.shape[2]))
        if behind:
            lands = [lax.empty((N_CHIPS,) + g.shape[2:], g.dtype) for g in g4]
            started = _copy_start("exchange_start_" + group, g4, lands, _exchange_plan)
            self.exchanges[group] = (names,) + started
            return (started[-1],)
        return self._scatter_start(group, names, g4, _exchange_halves("exchange_halves_" + group, g4))

    def exchange_finish(self, group, after):
        names, send_sems, recv_sems, g4, lands, _ = self.exchanges.pop(group)
        g4, got = _copy_wait("exchange_wait_" + group, send_sems, recv_sems, g4, lands, after, _exchange_plan)
        return self._scatter_start(group, names, g4, got)

    def _scatter_start(self, group, names, g4, got):
        sums = _add_own_halves("add_own_halves_" + group, self.c_idx, g4, got)
        lands = [lax.empty((N_CHIPS - 1,) + s.shape[1:], s.dtype) for s in sums]
        started = _copy_start("scatter_start_" + group, sums, lands, _scatter_plan)
        self.reductions[group] = (names,) + started
        return (started[-1],)

    def reduce_finish(self, group, after, behind=False):
        names, send_sems, recv_sems, sums, lands, _ = self.reductions.pop(group)
        sums, lands = _copy_wait("scatter_wait_" + group, send_sems, recv_sems, sums, lands, after, _scatter_plan)
        halves = _sum_pieces("sum_pieces_" + group, self.place_idx, sums, lands)
        if behind:
            started = _copy_start("join_start_" + group, halves, [], _join_plan)
            self.joins[group] = (names,) + started
            return (started[-1],)
        self.reduced.update(zip(names, _join_halves("join_halves_" + group, halves)))
        return ()

    def join_finish(self, group, after):
        names, send_sems, recv_sems, halves, _, _ = self.joins.pop(group)
        self.reduced.update(zip(names, _copy_wait("join_wait_" + group, send_sems, recv_sems, halves, [], after, _join_plan)[0]))


def kernel(x, ffn1_norm, ffn1_w_in, ffn1_w_out, mix_norm, w_in, conv_dw_kernel, conv_dw_bias, conv_ln_g, conv_ln_b, conv_w_proj, q_norm, k_norm, attn_sinks, rel_bias, attn_w_o, w_out, ffn2_norm, ffn2_w_in, ffn2_w_out, loss_target, m_ffn1_norm, m_ffn1_w_in, m_ffn1_w_out, m_mix_norm, m_w_in, m_conv_dw_kernel, m_conv_dw_bias, m_conv_ln_g, m_conv_ln_b, m_conv_w_proj, m_q_norm, m_k_norm, m_attn_sinks, m_rel_bias, m_attn_w_o, m_w_out, m_ffn2_norm, m_ffn2_w_in, m_ffn2_w_out, v_ffn1_norm, v_ffn1_w_in, v_ffn1_w_out, v_mix_norm, v_w_in, v_conv_dw_kernel, v_conv_dw_bias, v_conv_ln_g, v_conv_ln_b, v_conv_w_proj, v_q_norm, v_k_norm, v_attn_sinks, v_rel_bias, v_attn_w_o, v_w_out, v_ffn2_norm, v_ffn2_w_in, v_ffn2_w_out):
    args = dict(locals())
    wts = {n: args[n] for n in WEIGHTS}
    mom = {n: args["m_" + n] for n in WEIGHTS}
    var = {n: args["v_" + n] for n in WEIGHTS}
    small_packs = [_pack_small(wts), _pack_small(mom), _pack_small(var, fill=1.0)]
    dw_moments = [_shard_halves("conv_dw_kernel", a) for a in (wts["conv_dw_kernel"], mom["conv_dw_kernel"], var["conv_dw_kernel"])]
    comm = _MeshComm(wts, idle_work=small_packs + dw_moments)
    small = {n: wts[n] if n in ("attn_sinks", "rel_bias") else wts[n].reshape(1, -1) for n in SMALL}
    loss_part, grad_x, g = _local_step(x[0], loss_target[0], small, comm)

    small_sum = _all_reduce_small(_pack_small(g, loss=loss_part))
    loss = small_sum[LOSS_PLACE[0], LOSS_PLACE[1]]
    small_shapes = {n: wts[n].shape for n in SMALL}
    g_small = _unpack_small(small_sum, small_shapes)

    grads, delta, new_m, new_v = {}, {}, {}, {}
    for n in BIG:
        j = comm.reduced[n]
        gs = j.reshape(j.shape[1] * 2, j.shape[2])
        pad = n == "conv_dw_kernel"
        ws, ms, vs = (a.reshape(gs.shape) for a in (dw_moments if pad else (wts[n], mom[n], var[n])))
        d, nm, nv = _adamw("adamw_" + n, ws, gs, ms, vs)
        cut = (lambda a: a[:CONV_WIDTH]) if pad else (lambda a: a)
        grads[n], delta[n], new_m[n], new_v[n] = cut(gs), cut(d), cut(nm), cut(nv)
    d, nm, nv = _adamw("adamw_small", small_packs[0], small_sum, small_packs[1], small_packs[2])
    grads.update(g_small)
    delta.update(_unpack_small(d, small_shapes))
    new_m.update(_unpack_small(nm, small_shapes))
    new_v.update(_unpack_small(nv, small_shapes))

    return (loss, grad_x[None], *[grads[n] for n in WEIGHTS], *[delta[n] for n in WEIGHTS],
            *[new_m[n] for n in WEIGHTS], *[new_v[n] for n in WEIGHTS])
```

```python
import functools
import math

import jax
import jax.numpy as jnp
from jax import lax
from jax.experimental import pallas as pl
from jax.experimental.pallas import tpu as pltpu

F32 = jnp.float32
BF16 = jnp.bfloat16
MESH = pl.DeviceIdType.MESH

EPS = 1e-6
D_MODEL = 1024
D_FF = 2816
N_CHIPS = 4
SHARD_W = 2 * D_FF // N_CHIPS
HEAD_DIM = 64
N_Q_HEADS = 16
N_KV_HEADS = 4
GROUP = N_Q_HEADS // N_KV_HEADS
BLOCK = 128
QROWS = GROUP * BLOCK
N_BUCKETS = 32
MAX_DISTANCE = 128
CONV_WIDTH = 31
CONV_PAD = 32
NEG = float(jnp.finfo(jnp.float32).min)

ADAM_LR = 0.001
ADAM_B1 = 0.9
ADAM_B2 = 0.999
ADAM_EPS = 1e-08
ADAM_WD = 0.01
ADAM_STEP = 10

VMEM_LIMIT_BYTES = 56 * 1024 * 1024
ROW_TILE = 1024
TOKEN_TILE = 4096
CONV_TILE = 256
CONV_ROWS = 128
LANES = 128

COL_CONV_A, COL_CONV_G, COL_Q, COL_K, COL_V, COL_GC, COL_GA = 0, 1024, 2048, 3072, 3328, 3584, 4608
IN_W = 5632


def _params(sem, vmem=VMEM_LIMIT_BYTES):
    return pltpu.CompilerParams(dimension_semantics=sem, vmem_limit_bytes=vmem)


def _sigmoid(x):
    return 1.0 / (1.0 + jnp.exp(-x))


def _dot(a, b, trans_a=False, trans_b=False, precision=None):
    dn = (((0,) if trans_a else (1,), (1,) if trans_b else (0,)), ((), ()))
    return lax.dot_general(a, b, dn, preferred_element_type=F32, precision=precision)


def _mm(name, grid, a, a_spec, b, b_spec, acc_shape, *, trans_a=False, trans_b=False, a_pre=None, b_pre=None,
        extras=(), extra_specs=(), tokens=(), out_shape, out_specs, epilogue,
        sem=("parallel", "parallel", "arbitrary")):
    n_k = grid[2]
    extras = tuple(extras) + tuple(tokens)
    extra_specs = tuple(extra_specs) + (pl.BlockSpec((8, LANES), lambda i, j, kk: (0, 0)),) * len(tokens)
    n_extra = len(extras)
    n_out = len(out_shape)

    def body(a_ref, b_ref, *rest):
        ex = rest[:n_extra]
        outs = rest[n_extra:n_extra + n_out]
        ids = (pl.program_id(0), pl.program_id(1), pl.program_id(2))
        av = a_ref[...]
        if a_pre is not None:
            av = a_pre(av)
        bv = b_ref[...]
        if b_pre is not None:
            bv = b_pre(bv)
        if n_k == 1:
            epilogue(_dot(av, bv, trans_a, trans_b), ex, outs, ids)
        else:
            acc = rest[-1]

            @pl.when(ids[2] == 0)
            def _():
                acc[...] = jnp.zeros_like(acc)

            acc[...] += _dot(av, bv, trans_a, trans_b)

            @pl.when(ids[2] == n_k - 1)
            def _():
                epilogue(acc[...], ex, outs, ids)

    scratch = [] if n_k == 1 else [pltpu.VMEM(acc_shape, F32)]
    return pl.pallas_call(
        body, name=name, grid=grid,
        in_specs=[a_spec, b_spec, *extra_specs],
        out_specs=list(out_specs), out_shape=list(out_shape),
        scratch_shapes=scratch, compiler_params=_params(sem),
    )(a, b, *extras)


def _half_bf16(v):
    return (0.5 * v).astype(BF16)


def _to_bf16(v):
    return v.astype(BF16)


def _rmsnorm_fwd(name, x, g, tokens=()):
    t, d = x.shape
    tm = min(ROW_TILE, t)

    def body(x_ref, g_ref, *rest):
        o_ref = rest[-1]
        xv = x_ref[...]
        r = lax.rsqrt(jnp.mean(xv * xv, axis=-1, keepdims=True) + EPS)
        o_ref[...] = (xv * r * g_ref[...]).astype(BF16)

    return pl.pallas_call(
        body, name=name, grid=(t // tm,),
        in_specs=[pl.BlockSpec((tm, d), lambda i: (i, 0)), pl.BlockSpec((1, d), lambda i: (0, 0))]
        + [pl.BlockSpec((8, LANES), lambda i: (0, 0))] * len(tokens),
        out_specs=pl.BlockSpec((tm, d), lambda i: (i, 0)),
        out_shape=jax.ShapeDtypeStruct((t, d), BF16),
        compiler_params=_params(("parallel",)),
    )(x, g, *tokens)


def _ffn_in(name, n, w_in4, tokens=()):
    t, d = n.shape
    tm = min(ROW_TILE, t)

    def body(n_ref, wa_ref, wb_ref, *rest):
        ab_ref, h_ref = rest[-2:]
        nv = n_ref[...]
        a = _dot(nv, wa_ref[...])
        b = _dot(nv, wb_ref[...])
        h_ref[...] = (a * _sigmoid(a) * b).astype(BF16)
        ab_ref[0] = a.astype(BF16)
        ab_ref[1] = b.astype(BF16)

    return pl.pallas_call(
        body, name=name, grid=(2, t // tm),
        in_specs=[pl.BlockSpec((tm, d), lambda j, i: (i, 0)),
                  pl.BlockSpec((None, d, SHARD_W), lambda j, i: (j, 0, 0)),
                  pl.BlockSpec((None, d, SHARD_W), lambda j, i: (j + 2, 0, 0))]
        + [pl.BlockSpec((8, LANES), lambda j, i: (0, 0))] * len(tokens),
        out_specs=[pl.BlockSpec((2, tm, SHARD_W), lambda j, i: (0, i, j)),
                   pl.BlockSpec((tm, SHARD_W), lambda j, i: (i, j))],
        out_shape=[jax.ShapeDtypeStruct((2, t, D_FF), BF16), jax.ShapeDtypeStruct((t, D_FF), BF16)],
        compiler_params=_params(("parallel", "parallel")),
    )(n, w_in4, w_in4, *tokens)


def _mm_residual(name, a, w, res, scale, next_gain=None, loss_target=None):
    t, k = a.shape
    n = w.shape[1]
    tm = min(ROW_TILE, t)
    row = pl.BlockSpec((tm, n), lambda i, j, kk: (i, 0))
    extras, specs = [res], [row]
    shapes, out_specs = [jax.ShapeDtypeStruct((t, n), F32)], [row]
    if next_gain is not None:
        extras.append(next_gain)
        specs.append(pl.BlockSpec((1, n), lambda i, j, kk: (0, 0)))
        shapes.append(jax.ShapeDtypeStruct((t, n), BF16))
        out_specs.append(row)
    if loss_target is not None:
        extras.append(loss_target)
        specs.append(row)
        shapes.append(jax.ShapeDtypeStruct((8, LANES), F32))
        out_specs.append(pl.BlockSpec((8, LANES), lambda i, j, kk: (0, 0)))

    def epilogue(acc, ex, outs, ids):
        y = ex[0][...] + scale * acc
        if loss_target is None:
            outs[0][...] = y
        if next_gain is not None:
            r = lax.rsqrt(jnp.mean(y * y, axis=-1, keepdims=True) + EPS)
            outs[1][...] = (y * r * ex[1][...]).astype(BF16)
        if loss_target is not None:
            diff = y - ex[1][...]
            outs[0][...] = diff * (1.0 / n)
            part = jnp.full((8, LANES), 0.5 / n * jnp.sum(diff * diff), F32)

            @pl.when(ids[0] == 0)
            def _():
                outs[1][...] = part

            @pl.when(ids[0] > 0)
            def _():
                outs[1][...] += part

    sem = ("parallel" if loss_target is None else "arbitrary", "parallel", "arbitrary")
    out = _mm(name, (t // tm, 1, 1), a, pl.BlockSpec((tm, k), lambda i, j, kk: (i, 0)),
              w, pl.BlockSpec((k, n), lambda i, j, kk: (0, 0)), (tm, n),
              extras=extras, extra_specs=specs, out_shape=shapes, out_specs=out_specs, epilogue=epilogue, sem=sem)
    return out[0] if len(out) == 1 else tuple(out)


def _ffn_fwd(tag, x, n, w_in4, w_out, tokens=(), **tail):
    ab, h = _ffn_in(tag + "_in", n, w_in4, tokens)
    y = _mm_residual(tag + "_out", h, w_out, x, 0.5, **tail)
    return y, (n, ab, h)


def _ffn_bwd(tag, dres, x, g, saved, w_in4, w_out, tokens=(), on_first=None, on_weight_grads=None):
    n, ab, h = saved
    t, d = x.shape
    tm = min(ROW_TILE, t)
    tk = min(TOKEN_TILE, t)
    half_w = SHARD_W

    def dact_epilogue(acc, ex, outs, ids):
        a = ex[0][0].astype(F32)
        b = ex[0][1].astype(F32)
        sig = _sigmoid(a)
        outs[0][0] = (acc * b * (sig * (1.0 + a * (1.0 - sig)))).astype(BF16)
        outs[0][1] = (acc * (a * sig)).astype(BF16)

    du = _mm(tag + "_dact", (2, t // tm, 1),
             dres, pl.BlockSpec((tm, d), lambda j, i, kk: (i, 0)),
             w_out, pl.BlockSpec((half_w, d), lambda j, i, kk: (j, 0)), (tm, half_w),
             trans_b=True, a_pre=_half_bf16,
             extras=(ab,), extra_specs=(pl.BlockSpec((2, tm, half_w), lambda j, i, kk: (0, i, j)),), tokens=tokens,
             out_shape=(jax.ShapeDtypeStruct((2, t, D_FF), BF16),),
             out_specs=(pl.BlockSpec((2, tm, half_w), lambda j, i, kk: (0, i, j)),),
             epilogue=dact_epilogue)[0]

    def store_epilogue(acc, ex, outs, ids):
        outs[0][...] = acc.astype(BF16)

    early = () if on_first is None else on_first(du)

    tk_out = min(TOKEN_TILE // 2, t)
    dw_out = _mm(tag + "_dwout", (2, 1, t // tk_out),
                 h, pl.BlockSpec((tk_out, half_w), lambda i, j, kk: (kk, i)),
                 dres, pl.BlockSpec((tk_out, d), lambda i, j, kk: (kk, 0)), (half_w, d),
                 trans_a=True, b_pre=_half_bf16, tokens=early,
                 out_shape=(jax.ShapeDtypeStruct((D_FF, d), BF16),),
                 out_specs=(pl.BlockSpec((half_w, d), lambda i, j, kk: (i, 0)),),
                 epilogue=store_epilogue)[0]

    dw_in4 = _mm(tag + "_dwin", (1, N_CHIPS, t // tk),
                 n, pl.BlockSpec((tk, d), lambda i, j, kk: (kk, 0)),
                 du, pl.BlockSpec((None, tk, SHARD_W), lambda i, j, kk: (j // 2, kk, j % 2)), (d, SHARD_W),
                 trans_a=True,
                 out_shape=(jax.ShapeDtypeStruct((N_CHIPS, d, SHARD_W), BF16),),
                 out_specs=(pl.BlockSpec((None, d, SHARD_W), lambda i, j, kk: (j, 0, 0)),),
                 epilogue=store_epilogue)[0]

    late = () if on_weight_grads is None else on_weight_grads(dw_in4, dw_out)

    tn = min(NORM_GRAD_TILE, t)
    shard_specs = [pl.BlockSpec((None, tn, SHARD_W), functools.partial(lambda i, s: (s // 2, i, s % 2), s=s))
                   for s in range(N_CHIPS)]
    dx, dg = _norm_input_grad(tag + "_dn", du, shard_specs, w_in4, x, g, dres, late)
    return dx, dw_in4, dw_out, dg


NORM_GRAD_TILE = 512


def _norm_input_grad(name, a, shard_specs, w4, x, g, dres, tokens=()):
    t, d = x.shape
    tn = min(NORM_GRAD_TILE, t)
    ns = len(shard_specs)

    def body(*refs):
        a_refs, (w_ref, x_ref, g_ref, dres_ref) = refs[:ns], refs[ns:ns + 4]
        out_ref, dg_ref = refs[-2:]
        acc = _dot(a_refs[0][...], w_ref[0], trans_b=True)
        for s in range(1, ns):
            acc = acc + _dot(a_refs[s][...], w_ref[s], trans_b=True)
        xv = x_ref[...]
        r = lax.rsqrt(jnp.mean(xv * xv, axis=-1, keepdims=True) + EPS)
        w = acc * g_ref[...]
        out_ref[...] = dres_ref[...] + (r * w - xv * (r * r * r) * jnp.mean(xv * w, axis=-1, keepdims=True))
        part = jnp.sum(acc * (xv * r), axis=0, keepdims=True)
        i = pl.program_id(0)

        @pl.when(i == 0)
        def _():
            dg_ref[...] = part

        @pl.when(i > 0)
        def _():
            dg_ref[...] += part

    row = pl.BlockSpec((tn, d), lambda i: (i, 0))
    vec = pl.BlockSpec((1, d), lambda i: (0, 0))
    return pl.pallas_call(
        body, name=name, grid=(t // tn,),
        in_specs=list(shard_specs) + [pl.BlockSpec(w4.shape, lambda i: (0, 0, 0)), row, vec, row]
        + [pl.BlockSpec((8, LANES), lambda i: (0, 0))] * len(tokens),
        out_specs=[row, vec],
        out_shape=[jax.ShapeDtypeStruct((t, d), F32), jax.ShapeDtypeStruct((1, d), F32)],
        compiler_params=_params(("arbitrary",)),
    )(*[a] * ns, w4, x, g, dres, *tokens)


def _conv_fill(zp_ref, a_ref, g_ref, ah_ref, gh_ref, i):
    zh = ah_ref[...].astype(F32) * _sigmoid(gh_ref[...].astype(F32))
    zp_ref[pl.ds(0, CONV_PAD), :] = jnp.where(i > 0, zh, 0.0)
    zp_ref[pl.ds(CONV_PAD, a_ref.shape[0]), :] = a_ref[...].astype(F32) * _sigmoid(g_ref[...].astype(F32))


def _shift_groups(shifts):
    groups = {}
    for j, s in shifts:
        groups.setdefault(s % 8, []).append((j, s // 8))
    return groups


def _windows(zp_ref, r0, lanes, groups):
    for q, taps in groups.items():
        deepest = max(p for _, p in taps)
        win = zp_ref[pl.ds(r0 + q, 8 * deepest + CONV_ROWS), lanes]
        for j, p in taps:
            yield j, win[8 * p:8 * p + CONV_ROWS]


def _conv_apply(zp_ref, out_ref, dw_ref, bias_ref, tm, ch, shifts):
    groups = _shift_groups(shifts)
    for cc in range(ch // LANES):
        lanes = pl.ds(cc * LANES, LANES)
        w = [dw_ref[pl.ds(j, 1), lanes] for j in range(CONV_WIDTH)]
        for r0 in range(0, tm, CONV_ROWS):
            if bias_ref is None:
                acc = jnp.zeros((CONV_ROWS, LANES), F32)
            else:
                acc = jnp.broadcast_to(bias_ref[:, lanes], (CONV_ROWS, LANES))
            for j, rows in _windows(zp_ref, r0, lanes, groups):
                acc = acc + w[j] * rows
            out_ref[pl.ds(r0, CONV_ROWS), lanes] = acc


FWD_SHIFTS = [(j, CONV_PAD - (CONV_WIDTH - 1) + j) for j in range(CONV_WIDTH)]
BWD_SHIFTS = [(j, CONV_WIDTH - 1 - j) for j in range(CONV_WIDTH)]


def _conv_taps(zp_ref, z1_ref, dw_ref, bias_ref, tm, ch):
    _conv_apply(zp_ref, z1_ref, dw_ref, bias_ref, tm, ch, FWD_SHIFTS)


def _conv_specs(tm, ch):
    per = tm // CONV_PAD
    cb = COL_CONV_G // ch
    return [pl.BlockSpec((tm, ch), lambda i: (i, 0)),
            pl.BlockSpec((tm, ch), lambda i: (i, cb)),
            pl.BlockSpec((CONV_PAD, ch), lambda i: (jnp.maximum(i * per - 1, 0), 0)),
            pl.BlockSpec((CONV_PAD, ch), lambda i: (jnp.maximum(i * per - 1, 0), cb))]


def _conv_fwd(p, dw, bias, ln_g, ln_b):
    t = p.shape[0]
    ch = D_MODEL
    tm = min(CONV_TILE, t)

    def body(a_ref, g_ref, ah_ref, gh_ref, dw_ref, bias_ref, lg_ref, lb_ref, o_ref, z1_ref, zp_ref):
        i = pl.program_id(0)
        _conv_fill(zp_ref, a_ref, g_ref, ah_ref, gh_ref, i)
        _conv_taps(zp_ref, z1_ref, dw_ref, bias_ref, tm, ch)
        z1 = z1_ref[...]
        mu = jnp.mean(z1, axis=-1, keepdims=True)
        zc = z1 - mu
        rs = lax.rsqrt(jnp.mean(zc * zc, axis=-1, keepdims=True) + EPS)
        z2 = zc * rs * lg_ref[...] + lb_ref[...]
        o_ref[...] = (z2 * _sigmoid(z2)).astype(BF16)

    vec = pl.BlockSpec((1, ch), lambda i: (0, 0))
    return pl.pallas_call(
        body, name="conv_fwd", grid=(t // tm,),
        in_specs=_conv_specs(tm, ch) + [pl.BlockSpec((CONV_PAD, ch), lambda i: (0, 0)), vec, vec, vec],
        out_specs=[pl.BlockSpec((tm, ch), lambda i: (i, 0)), pl.BlockSpec((tm, ch), lambda i: (i, 0))],
        out_shape=[jax.ShapeDtypeStruct((t, ch), BF16), jax.ShapeDtypeStruct((t, ch), F32)],
        scratch_shapes=[pltpu.VMEM((CONV_PAD + tm, ch), F32)],
        compiler_params=_params(("parallel",)),
    )(p, p, p, p, dw, bias, ln_g, ln_b)


def _conv_bwd_ln(p, z1_saved, dz3, ln_g, ln_b):
    t = p.shape[0]
    ch = D_MODEL
    tm = min(CONV_TILE, t)

    def body(a_ref, g_ref, ah_ref, gh_ref, z1_ref, dz3_ref, lg_ref, lb_ref,
             dz1_ref, ddw_ref, dbias_ref, dlg_ref, dlb_ref, zp_ref):
        i = pl.program_id(0)
        _conv_fill(zp_ref, a_ref, g_ref, ah_ref, gh_ref, i)
        z1 = z1_ref[...]
        mu = jnp.mean(z1, axis=-1, keepdims=True)
        zc = z1 - mu
        rs = lax.rsqrt(jnp.mean(zc * zc, axis=-1, keepdims=True) + EPS)
        xh = zc * rs
        z2 = xh * lg_ref[...] + lb_ref[...]
        sig = _sigmoid(z2)
        dz2 = dz3_ref[...].astype(F32) * (sig * (1.0 + z2 * (1.0 - sig)))
        dxh = dz2 * lg_ref[...]
        dz1 = rs * (dxh - jnp.mean(dxh, axis=-1, keepdims=True) - xh * jnp.mean(dxh * xh, axis=-1, keepdims=True))
        dz1_ref[...] = dz1

        @pl.when(i == 0)
        def _():
            ddw_ref[...] = jnp.zeros_like(ddw_ref)
            dbias_ref[...] = jnp.zeros_like(dbias_ref)
            dlg_ref[...] = jnp.zeros_like(dlg_ref)
            dlb_ref[...] = jnp.zeros_like(dlb_ref)

        dlg_ref[...] += jnp.sum(dz2 * xh, axis=0, keepdims=True)
        dlb_ref[...] += jnp.sum(dz2, axis=0, keepdims=True)
        dbias_ref[...] += jnp.sum(dz1, axis=0, keepdims=True)
        groups = _shift_groups(FWD_SHIFTS)
        for cc in range(ch // LANES):
            lanes = pl.ds(cc * LANES, LANES)
            accs = [jnp.zeros((8, LANES), F32) for _ in range(CONV_WIDTH)]
            for r0 in range(0, tm, CONV_ROWS):
                dzc = dz1_ref[pl.ds(r0, CONV_ROWS), lanes]
                for j, rows in _windows(zp_ref, r0, lanes, groups):
                    accs[j] = accs[j] + jnp.sum((dzc * rows).reshape(CONV_ROWS // 8, 8, LANES), axis=0)
            for j in range(CONV_WIDTH):
                ddw_ref[pl.ds(j, 1), lanes] += jnp.sum(accs[j], axis=0, keepdims=True)

    vec = pl.BlockSpec((1, ch), lambda i: (0, 0))
    return pl.pallas_call(
        body, name="conv_bwd_ln", grid=(t // tm,),
        in_specs=_conv_specs(tm, ch) + [pl.BlockSpec((tm, ch), lambda i: (i, 0)),
                                        pl.BlockSpec((tm, ch), lambda i: (i, 0)), vec, vec],
        out_specs=[pl.BlockSpec((tm, ch), lambda i: (i, 0)), pl.BlockSpec((CONV_PAD, ch), lambda i: (0, 0)), vec, vec, vec],
        out_shape=[jax.ShapeDtypeStruct((t, ch), F32), jax.ShapeDtypeStruct((CONV_PAD, ch), F32)]
        + [jax.ShapeDtypeStruct((1, ch), F32)] * 3,
        scratch_shapes=[pltpu.VMEM((CONV_PAD + tm, ch), F32)],
        compiler_params=_params(("arbitrary",)),
    )(p, p, p, p, z1_saved, dz3, ln_g, ln_b)


def _conv_bwd_glu(p, dz1, dw, dq, dkv, dgates):
    t = p.shape[0]
    ch = D_MODEL
    tm = min(CONV_TILE, t)
    per = tm // CONV_PAD
    n_halo = t // CONV_PAD
    cb = COL_CONV_G // ch

    def body(a_ref, g_ref, dz_ref, dzn_ref, dw_ref, dq_ref, dkv_ref, dgates_ref, o_ref, zp_ref, z0_ref):
        i = pl.program_id(0)
        o_ref[:, pl.ds(COL_Q, Q_W)] = dq_ref[...]
        o_ref[:, pl.ds(COL_K, 2 * KV_W)] = dkv_ref[...]
        o_ref[:, pl.ds(COL_GC, ch)] = dgates_ref[0]
        o_ref[:, pl.ds(COL_GA, ch)] = dgates_ref[1]
        zp_ref[pl.ds(0, tm), :] = dz_ref[...]
        zp_ref[pl.ds(tm, CONV_PAD), :] = jnp.where(i < t // tm - 1, dzn_ref[...], 0.0)
        _conv_apply(zp_ref, z0_ref, dw_ref, None, tm, ch, BWD_SHIFTS)
        dz0 = z0_ref[...]
        a = a_ref[...].astype(F32)
        sig = _sigmoid(g_ref[...].astype(F32))
        o_ref[:, pl.ds(0, ch)] = (dz0 * sig).astype(BF16)
        o_ref[:, pl.ds(ch, ch)] = (dz0 * a * sig * (1.0 - sig)).astype(BF16)

    return pl.pallas_call(
        body, name="conv_bwd_glu", grid=(t // tm,),
        in_specs=[pl.BlockSpec((tm, ch), lambda i: (i, 0)), pl.BlockSpec((tm, ch), lambda i: (i, cb)),
                  pl.BlockSpec((tm, ch), lambda i: (i, 0)),
                  pl.BlockSpec((CONV_PAD, ch), lambda i: (jnp.minimum((i + 1) * per, n_halo - 1), 0)),
                  pl.BlockSpec((CONV_PAD, ch), lambda i: (0, 0)),
                  pl.BlockSpec((tm, Q_W), lambda i: (i, 0)), pl.BlockSpec((tm, 2 * KV_W), lambda i: (i, 0)),
                  pl.BlockSpec((2, tm, ch), lambda i: (0, i, 0))],
        out_specs=pl.BlockSpec((tm, IN_W), lambda i: (i, 0)),
        out_shape=jax.ShapeDtypeStruct((t, IN_W), BF16),
        scratch_shapes=[pltpu.VMEM((tm + CONV_PAD, ch), F32), pltpu.VMEM((tm, ch), F32)],
        compiler_params=_params(("parallel",)),
    )(p, p, dz1, dz1, dw, dq, dkv, dgates)


def _bucket_onehot():
    qi = jnp.arange(BLOCK, dtype=jnp.int32)[:, None]
    kj = jnp.arange(2 * BLOCK, dtype=jnp.int32)[None, :]
    dist = jnp.maximum(qi + BLOCK - kj, 0)
    max_exact = N_BUCKETS // 2
    dflt = jnp.maximum(dist, 1).astype(F32)
    large = max_exact + (jnp.log(dflt / max_exact) / math.log(MAX_DISTANCE / max_exact)
                         * (N_BUCKETS - max_exact)).astype(jnp.int32)
    large = jnp.minimum(large, N_BUCKETS - 1)
    bucket = jnp.where(dist < max_exact, dist, large)
    onehot = bucket[None] == jnp.arange(N_BUCKETS, dtype=jnp.int32)[:, None, None]
    return onehot.astype(F32).reshape(N_BUCKETS, BLOCK * 2 * BLOCK)


def _bias_table(rel_bias_t, onehot):
    n = onehot.shape[1]
    tn = 4096

    def body(r_ref, oh_ref, o_ref):
        flat = pl.program_id(0) * tn + lax.broadcasted_iota(jnp.int32, (N_Q_HEADS, tn), 1)
        dist = (flat // (2 * BLOCK)) + BLOCK - (flat % (2 * BLOCK))
        bias = _dot(r_ref[...], oh_ref[...], precision=lax.Precision.HIGHEST)
        o_ref[...] = jnp.where((dist >= 0) & (dist < BLOCK), bias, NEG)

    return pl.pallas_call(
        body, name="bias_table", grid=(n // tn,),
        in_specs=[pl.BlockSpec((N_Q_HEADS, N_BUCKETS), lambda i: (0, 0)), pl.BlockSpec((N_BUCKETS, tn), lambda i: (0, i))],
        out_specs=pl.BlockSpec((N_Q_HEADS, tn), lambda i: (0, i)),
        out_shape=jax.ShapeDtypeStruct((N_Q_HEADS, n), F32),
        compiler_params=_params(("parallel",)),
    )(rel_bias_t, onehot)


def _bias_table_bwd(dbias, onehot):
    n = onehot.shape[1]
    tn = 4096

    def body(d_ref, oh_ref, o_ref):
        part = _dot(d_ref[...], oh_ref[...], trans_b=True, precision=lax.Precision.HIGHEST)
        i = pl.program_id(0)

        @pl.when(i == 0)
        def _():
            o_ref[...] = part

        @pl.when(i > 0)
        def _():
            o_ref[...] += part

    return pl.pallas_call(
        body, name="bias_table_bwd", grid=(n // tn,),
        in_specs=[pl.BlockSpec((N_Q_HEADS, tn), lambda i: (0, i)), pl.BlockSpec((N_BUCKETS, tn), lambda i: (0, i))],
        out_specs=pl.BlockSpec((N_Q_HEADS, N_BUCKETS), lambda i: (0, 0)),
        out_shape=jax.ShapeDtypeStruct((N_Q_HEADS, N_BUCKETS), F32),
        compiler_params=_params(("arbitrary",)),
    )(dbias, onehot)


def _lane_head(rows):
    return lax.broadcasted_iota(jnp.int32, (rows, KV_W), 1) // HEAD_DIM


def _group_rms(x, gain_wide):
    head = _lane_head(x.shape[0])
    sq = x * x
    r = jnp.zeros_like(x)
    for i in range(N_KV_HEADS):
        ms = jnp.sum(jnp.where(head == i, sq, 0.0), axis=-1, keepdims=True) * (1.0 / HEAD_DIM)
        r = jnp.where(head == i, lax.rsqrt(ms + EPS), r)
    return r, x * r * gain_wide


def _stack_heads(group):
    head = _lane_head(group.shape[0])
    return jnp.concatenate([jnp.where(head == i, group, jnp.zeros_like(group)) for i in range(N_KV_HEADS)], axis=0)


def _unstack_heads(stacked):
    head = _lane_head(BLOCK)
    out = jnp.where(head == 0, stacked[:BLOCK], 0.0)
    for i in range(1, N_KV_HEADS):
        out = out + jnp.where(head == i, stacked[i * BLOCK:(i + 1) * BLOCK], 0.0)
    return out


def _repeaters():
    row = lax.broadcasted_iota(jnp.int32, (KV_W, KV_W), 0)
    col = lax.broadcasted_iota(jnp.int32, (KV_W, KV_W), 1)
    return [(row == h * HEAD_DIM + col % HEAD_DIM).astype(BF16) for h in range(N_KV_HEADS)]


def _attn_probs(q_stack, k_rep, sink, bias, before_start):
    s = _dot(q_stack, k_rep, trans_b=True) * (1.0 / math.sqrt(HEAD_DIM)) + bias
    s = jnp.where(before_start, NEG, s)
    m = jnp.maximum(jnp.max(s, axis=-1, keepdims=True), sink)
    p = jnp.exp(s - m)
    es = jnp.exp(sink - m)
    inv = 1.0 / (jnp.sum(p, axis=-1, keepdims=True) + es)
    return p * inv, es * inv


def _before_start(n):
    col = lax.broadcasted_iota(jnp.int32, (QROWS, 2 * BLOCK), 1)
    return (col < BLOCK) & (n == 0)


STEP_BLOCKS = 4
KV_W = N_KV_HEADS * HEAD_DIM
Q_W = N_Q_HEADS * HEAD_DIM


def _attn_specs():
    gain = pl.BlockSpec((1, KV_W), lambda n: (0, 0))
    sink = pl.BlockSpec((N_KV_HEADS, QROWS, 1), lambda n: (0, 0, 0))
    bias = pl.BlockSpec((N_KV_HEADS, QROWS, 2 * BLOCK), lambda n: (0, 0, 0))
    return gain, sink, bias


def _attn_fwd(p, gq, gk, sink_rows, bias):
    t = p.shape[0]
    nb = t // BLOCK
    per = STEP_BLOCKS if nb % STEP_BLOCKS == 0 else 1
    gain, sink, bspec = _attn_specs()

    def body(q_ref, kp_ref, kc_ref, vp_ref, vc_ref, gq_ref, gk_ref, sink_ref, bias_ref, o_ref, p_ref, ps_ref):
        first = pl.program_id(0) * per
        rep = _repeaters()
        kf = jnp.concatenate([kp_ref[...], kc_ref[...]], axis=0).astype(F32)
        kn = _group_rms(kf, gk_ref[...])[1].astype(BF16)
        v = jnp.concatenate([vp_ref[...], vc_ref[...]], axis=0)
        for h in range(N_KV_HEADS):
            k_rep = _dot(kn, rep[h]).astype(BF16)
            v_rep = _dot(v, rep[h]).astype(BF16)
            for sub in range(per):
                rows = pl.ds(sub * BLOCK, BLOCK)
                window = slice(sub * BLOCK, (sub + 2) * BLOCK)
                qn = _group_rms(q_ref[rows, pl.ds(h * KV_W, KV_W)].astype(F32), gq_ref[...])[1]
                pn, ps_ref[sub, h] = _attn_probs(_stack_heads(qn).astype(BF16), k_rep[window], sink_ref[h], bias_ref[h],
                                                 _before_start(first + sub))
                pn = pn.astype(BF16)
                p_ref[sub, h] = pn
                o_ref[rows, pl.ds(h * KV_W, KV_W)] = _unstack_heads(_dot(pn, v_rep[window])).astype(BF16)

    def kv_specs(col):
        return [pl.BlockSpec((BLOCK, KV_W), lambda n: (jnp.maximum(n * per - 1, 0), col // KV_W)),
                pl.BlockSpec((per * BLOCK, KV_W), lambda n: (n, col // KV_W))]

    return pl.pallas_call(
        body, name="attn_fwd", grid=(nb // per,),
        in_specs=[pl.BlockSpec((per * BLOCK, Q_W), lambda n: (n, COL_Q // Q_W))] + kv_specs(COL_K) + kv_specs(COL_V)
        + [gain, gain, sink, bspec],
        out_specs=[pl.BlockSpec((per * BLOCK, Q_W), lambda n: (n, 0)),
                   pl.BlockSpec((per, N_KV_HEADS, QROWS, 2 * BLOCK), lambda n: (n, 0, 0, 0)),
                   pl.BlockSpec((per, N_KV_HEADS, QROWS, 1), lambda n: (n, 0, 0, 0))],
        out_shape=[jax.ShapeDtypeStruct((t, Q_W), BF16),
                   jax.ShapeDtypeStruct((nb, N_KV_HEADS, QROWS, 2 * BLOCK), BF16),
                   jax.ShapeDtypeStruct((nb, N_KV_HEADS, QROWS, 1), F32)],
        compiler_params=_params(("parallel",)),
    )(p, p, p, p, p, gq, gk, sink_rows, bias)


def _attn_bwd(p, do, gq, gk, probs, sink_probs):
    t = p.shape[0]
    nb = t // BLOCK
    per = STEP_BLOCKS if nb % STEP_BLOCKS == 0 else 1
    bspec = _attn_specs()[2]
    gain = pl.BlockSpec((1, HEAD_DIM), lambda n: (0, 0))
    scale = 1.0 / math.sqrt(HEAD_DIM)

    def head_selectors():
        row = lax.broadcasted_iota(jnp.int32, (KV_W, HEAD_DIM), 0)
        col = lax.broadcasted_iota(jnp.int32, (KV_W, HEAD_DIM), 1)
        return [(row == col + i * HEAD_DIM).astype(BF16) for i in range(N_KV_HEADS)]

    def take_heads(group, sel):
        return jnp.concatenate([_dot(group, s) for s in sel], axis=0)

    def put_heads(x, sel):
        rows = x.shape[0] // len(sel)
        out = _dot(x[:rows].astype(BF16), sel[0], trans_b=True)
        for i in range(1, len(sel)):
            out = out + _dot(x[i * rows:(i + 1) * rows].astype(BF16), sel[i], trans_b=True)
        return out

    def rms(x, g):
        r = lax.rsqrt(jnp.mean(x * x, axis=-1, keepdims=True) + EPS)
        return r, x * r * g

    def rms_bwd(dn, xf, r, g):
        w = dn * g
        dx = r * w - xf * (r * r * r) * jnp.mean(xf * w, axis=-1, keepdims=True)
        return dx, jnp.sum(dn * (xf * r), axis=0, keepdims=True)

    def body(q_ref, kp_ref, kc_ref, vp_ref, vc_ref, do_ref, gq_ref, gk_ref, p_ref, ps_ref,
             dq_ref, dkv_ref, dbias_ref, dsink_ref, dgq_ref, dgk_ref):
        n = pl.program_id(0)
        sel = head_selectors()

        @pl.when(n == 0)
        def _():
            dbias_ref[...] = jnp.zeros_like(dbias_ref)
            dsink_ref[...] = jnp.zeros_like(dsink_ref)
            dgq_ref[...] = jnp.zeros_like(dgq_ref)
            dgk_ref[...] = jnp.zeros_like(dgk_ref)

        dgq_sum = jnp.zeros((1, HEAD_DIM), F32)
        dgk_sum = jnp.zeros((1, HEAD_DIM), F32)
        dk_rows, dv_rows = [[] for _ in range(per)], [[] for _ in range(per)]
        for h in range(N_KV_HEADS):
            kf_all = jnp.concatenate([_dot(kp_ref[...], sel[h]), _dot(kc_ref[...], sel[h])], axis=0)
            rk_all, kn_all = rms(kf_all, gk_ref[...])
            kn_all = kn_all.astype(BF16)
            v_all = jnp.concatenate([_dot(vp_ref[...], sel[h]), _dot(vc_ref[...], sel[h])], axis=0).astype(BF16)
            for sub in range(per):
                rows = pl.ds(sub * BLOCK, BLOCK)
                window = slice(sub * BLOCK, (sub + 2) * BLOCK)
                qf = take_heads(q_ref[rows, pl.ds(h * KV_W, KV_W)], sel)
                rq, qn = rms(qf, gq_ref[...])
                pn_bf16, psink = p_ref[sub, h], ps_ref[sub, h]
                pn = pn_bf16.astype(F32)
                do = take_heads(do_ref[rows, pl.ds(h * KV_W, KV_W)], sel).astype(BF16)
                dv_win = _dot(do, pn_bf16, trans_a=True).T
                dp = _dot(do, v_all[window], trans_b=True)
                delta = jnp.sum(pn * dp, axis=-1, keepdims=True)
                ds = pn * (dp - delta)
                dsc = (ds * scale).astype(BF16)
                dqn = _dot(dsc, kn_all[window])
                dkn = _dot(qn.astype(BF16), dsc, trans_a=True).T
                dq, dgq = rms_bwd(dqn, qf, rq, gq_ref[...])
                dk_win, dgk = rms_bwd(dkn, kf_all[window], rk_all[window], gk_ref[...])
                dq_ref[rows, pl.ds(h * KV_W, KV_W)] = put_heads(dq, sel).astype(BF16)
                dk_rows[sub] += [dk_win[:BLOCK], dk_win[BLOCK:]]
                dv_rows[sub] += [dv_win[:BLOCK], dv_win[BLOCK:]]
                dbias_ref[h] += ds
                dsink_ref[h] += jnp.sum((-psink * delta).reshape(GROUP, BLOCK, 1), axis=1)
                dgq_sum = dgq_sum + dgq
                dgk_sum = dgk_sum + dgk
        for sub in range(per):
            for part in range(2):
                dkv_ref[sub, part, :, pl.ds(0, KV_W)] = put_heads(
                    jnp.concatenate(dk_rows[sub][part::2], axis=0), sel).astype(BF16)
                dkv_ref[sub, part, :, pl.ds(KV_W, KV_W)] = put_heads(
                    jnp.concatenate(dv_rows[sub][part::2], axis=0), sel).astype(BF16)
        dgq_ref[...] += dgq_sum
        dgk_ref[...] += dgk_sum

    def kv_specs(col):
        return [pl.BlockSpec((BLOCK, KV_W), lambda n: (jnp.maximum(n * per - 1, 0), col // KV_W)),
                pl.BlockSpec((per * BLOCK, KV_W), lambda n: (n, col // KV_W))]

    row = pl.BlockSpec((per * BLOCK, Q_W), lambda n: (n, 0))
    return pl.pallas_call(
        body, name="attn_bwd", grid=(nb // per,),
        in_specs=[pl.BlockSpec((per * BLOCK, Q_W), lambda n: (n, COL_Q // Q_W))] + kv_specs(COL_K) + kv_specs(COL_V)
        + [row, gain, gain,
           pl.BlockSpec((per, N_KV_HEADS, QROWS, 2 * BLOCK), lambda n: (n, 0, 0, 0)),
           pl.BlockSpec((per, N_KV_HEADS, QROWS, 1), lambda n: (n, 0, 0, 0))],
        out_specs=[row, pl.BlockSpec((per, 2, BLOCK, 2 * KV_W), lambda n: (n, 0, 0, 0)), bspec,
                   pl.BlockSpec((N_KV_HEADS, GROUP, 1), lambda n: (0, 0, 0)), gain, gain],
        out_shape=[jax.ShapeDtypeStruct((t, Q_W), BF16),
                   jax.ShapeDtypeStruct((nb, 2, BLOCK, 2 * KV_W), BF16),
                   jax.ShapeDtypeStruct((N_KV_HEADS, QROWS, 2 * BLOCK), F32),
                   jax.ShapeDtypeStruct((N_KV_HEADS, GROUP, 1), F32),
                   jax.ShapeDtypeStruct((1, HEAD_DIM), F32),
                   jax.ShapeDtypeStruct((1, HEAD_DIM), F32)],
        compiler_params=_params(("arbitrary",)),
    )(p, p, p, p, p, do, gq, gk, probs, sink_probs)


def _kv_window_sum(parts):
    nb = parts.shape[0]

    def body(cur_ref, nxt_ref, o_ref):
        nxt = jnp.where(pl.program_id(0) < nb - 1, nxt_ref[...].astype(F32), 0.0)
        o_ref[...] = (cur_ref[...].astype(F32) + nxt).astype(BF16)

    blk = (None, None, BLOCK, 2 * KV_W)
    return pl.pallas_call(
        body, name="kv_window_sum", grid=(nb,),
        in_specs=[pl.BlockSpec(blk, lambda n: (n, 1, 0, 0)),
                  pl.BlockSpec(blk, lambda n: (jnp.minimum(n + 1, nb - 1), 0, 0, 0))],
        out_specs=pl.BlockSpec((BLOCK, 2 * KV_W), lambda n: (n, 0)),
        out_shape=jax.ShapeDtypeStruct((nb * BLOCK, 2 * KV_W), BF16),
        compiler_params=_params(("parallel",)),
    )(parts, parts)


GATE_TILE = 512


def _merge_fwd(z3, o, p, w_proj, w_o):
    t, d = z3.shape
    tm = min(ROW_TILE, t)
    tn = GATE_TILE

    def body(z_ref, o_ref, gc_ref, ga_ref, wp_ref, wo_ref, m_ref, a_ref, b_ref):
        a = _dot(z_ref[...], wp_ref[...])
        b = _dot(o_ref[...], wo_ref[...])
        m_ref[...] = (_sigmoid(gc_ref[...].astype(F32)) * a + _sigmoid(ga_ref[...].astype(F32)) * b).astype(BF16)
        a_ref[...] = a.astype(BF16)
        b_ref[...] = b.astype(BF16)

    row = pl.BlockSpec((tm, d), lambda i, j: (i, 0))
    wspec = pl.BlockSpec((d, tn), lambda i, j: (0, j))
    ospec = pl.BlockSpec((tm, tn), lambda i, j: (i, j))
    return pl.pallas_call(
        body, name="merge_fwd", grid=(t // tm, d // tn),
        in_specs=[row, row,
                  pl.BlockSpec((tm, tn), lambda i, j: (i, COL_GC // tn + j)),
                  pl.BlockSpec((tm, tn), lambda i, j: (i, COL_GA // tn + j)), wspec, wspec],
        out_specs=[ospec, ospec, ospec],
        out_shape=[jax.ShapeDtypeStruct((t, d), BF16)] * 3,
        compiler_params=_params(("parallel", "parallel")),
    )(z3, o, p, p, w_proj, w_o)


def _merge_bwd(dres, w_out, a, b, p, tokens=()):
    t, d = dres.shape
    tm = min(ROW_TILE, t)
    tn = GATE_TILE

    def epilogue(acc, ex, outs, ids):
        a_ref, b_ref, gc_ref, ga_ref = ex[:4]
        sc = _sigmoid(gc_ref[...].astype(F32))
        sa = _sigmoid(ga_ref[...].astype(F32))
        outs[0][...] = (acc * sc).astype(BF16)
        outs[1][...] = (acc * sa).astype(BF16)
        outs[2][0] = (acc * a_ref[...].astype(F32) * sc * (1.0 - sc)).astype(BF16)
        outs[2][1] = (acc * b_ref[...].astype(F32) * sa * (1.0 - sa)).astype(BF16)

    ospec = pl.BlockSpec((tm, tn), lambda i, j, kk: (i, j))
    return _mm("merge_bwd", (t // tm, d // tn, 1),
               dres, pl.BlockSpec((tm, d), lambda i, j, kk: (i, 0)),
               w_out, pl.BlockSpec((tn, d), lambda i, j, kk: (j, 0)), (tm, tn),
               trans_b=True, a_pre=_to_bf16,
               extras=(a, b, p, p),
               extra_specs=(ospec, ospec,
                            pl.BlockSpec((tm, tn), lambda i, j, kk: (i, COL_GC // tn + j)),
                            pl.BlockSpec((tm, tn), lambda i, j, kk: (i, COL_GA // tn + j))), tokens=tokens,
               out_shape=(jax.ShapeDtypeStruct((t, d), BF16), jax.ShapeDtypeStruct((t, d), BF16),
                          jax.ShapeDtypeStruct((2, t, d), BF16)),
               out_specs=(ospec, ospec, pl.BlockSpec((2, tm, tn), lambda i, j, kk: (0, i, j))),
               epilogue=epilogue)


def _store_epilogue(acc, ex, outs, ids):
    outs[0][...] = acc


def _store_bf16_epilogue(acc, ex, outs, ids):
    outs[0][...] = acc.astype(BF16)


def _mm_nt(name, a, w, out_dtype=BF16):
    t, n = a.shape
    k = w.shape[0]
    tm = min(ROW_TILE, t)
    return _mm(name, (t // tm, 1, 1), a, pl.BlockSpec((tm, n), lambda i, j, kk: (i, 0)),
               w, pl.BlockSpec((k, n), lambda i, j, kk: (0, 0)), (tm, k), trans_b=True,
               out_shape=(jax.ShapeDtypeStruct((t, k), out_dtype),),
               out_specs=(pl.BlockSpec((tm, k), lambda i, j, kk: (i, 0)),),
               epilogue=_store_bf16_epilogue if out_dtype == BF16 else _store_epilogue)[0]


def _mm_tn(name, a, b, b_pre=None, tokens=()):
    t, m = a.shape
    n = b.shape[1]
    tk = min(TOKEN_TILE // (2 if b.dtype == F32 else 1), t)
    return _mm(name, (1, 1, t // tk), a, pl.BlockSpec((tk, m), lambda i, j, kk: (kk, 0)),
               b, pl.BlockSpec((tk, n), lambda i, j, kk: (kk, 0)), (m, n), trans_a=True, b_pre=b_pre, tokens=tokens,
               out_shape=(jax.ShapeDtypeStruct((m, n), BF16),),
               out_specs=(pl.BlockSpec((m, n), lambda i, j, kk: (0, 0)),), epilogue=_store_bf16_epilogue)[0]


def _local_step(x, target, small, comm):
    t = x.shape[0]
    w = dict(small)

    n1 = _rmsnorm_fwd("ffn1_norm", x, w["ffn1_norm"])
    onehot = _bucket_onehot()
    bias = _bias_table(w["rel_bias"].T, onehot).reshape(N_KV_HEADS, QROWS, 2 * BLOCK)
    sink_rows = jnp.repeat(w["attn_sinks"].reshape(N_KV_HEADS, GROUP), BLOCK, axis=1)[..., None]
    gq_wide = jnp.tile(w["q_norm"], (1, N_KV_HEADS))
    gk_wide = jnp.tile(w["k_norm"], (1, N_KV_HEADS))
    w.update(comm.weights("A", [n1, onehot, bias, sink_rows, gq_wide, gk_wide]))
    (x1, hm), ffn1_saved = _ffn_fwd("ffn1", x, n1, w["ffn1_w_in"], w["ffn1_w_out"], comm.tokens,
                                    next_gain=w["mix_norm"])
    w.update(comm.weights("B", x1))
    tm = min(ROW_TILE, t)
    p = _mm("mix_in", (N_CHIPS, t // tm, 1),
            hm, pl.BlockSpec((tm, D_MODEL), lambda j, i, kk: (i, 0)),
            w["w_in"], pl.BlockSpec((None, D_MODEL, SHARD_W), lambda j, i, kk: (j, 0, 0)), (tm, SHARD_W),
            tokens=comm.tokens,
            out_shape=(jax.ShapeDtypeStruct((t, IN_W), BF16),),
            out_specs=(pl.BlockSpec((tm, SHARD_W), lambda j, i, kk: (i, j)),),
            epilogue=_store_bf16_epilogue)[0]

    z3, z1 = _conv_fwd(p, w["conv_dw_kernel"], w["conv_dw_bias"], w["conv_ln_g"], w["conv_ln_b"])

    o, probs, sink_probs = _attn_fwd(p, gq_wide, gk_wide, sink_rows, bias)

    merged, a, b = _merge_fwd(z3, o, p, w["conv_w_proj"], w["attn_w_o"])
    x2, n2 = _mm_residual("mix_out", merged, w["w_out"], x1, 1.0, next_gain=w["ffn2_norm"])
    w.update(comm.weights("C", n2))
    (dy, loss), ffn2_saved = _ffn_fwd("ffn2", x2, n2, w["ffn2_w_in"], w["ffn2_w_out"], loss_target=target)

    g, big = {}, {}
    dres2, big["ffn2_w_in"], big["ffn2_w_out"], g["ffn2_norm"] = _ffn_bwd(
        "ffn2b", dy, x2, w["ffn2_norm"], ffn2_saved, w["ffn2_w_in"], w["ffn2_w_out"])
    tokens = comm.reduce_start("R1", big, behind=True)

    da, db, dgates = _merge_bwd(dres2, w["w_out"], a, b, p, tokens)
    tokens = comm.exchange_finish("R1", da)
    big = {}
    big["w_out"] = _mm_tn("d_w_out", merged, dres2, b_pre=_to_bf16, tokens=tokens)
    big["conv_w_proj"] = _mm_tn("d_w_proj", z3, da)
    big["attn_w_o"] = _mm_tn("d_w_o", o, db)
    dz3 = _mm_nt("d_z3", da, w["conv_w_proj"])
    do = _mm_nt("d_o", db, w["attn_w_o"])

    dq, dkv_parts, dbias, dsink, g["q_norm"], g["k_norm"] = _attn_bwd(
        p, do, w["q_norm"], w["k_norm"], probs, sink_probs)
    dkv = _kv_window_sum(dkv_parts)
    g["rel_bias"] = _bias_table_bwd(dbias.reshape(N_Q_HEADS, BLOCK * 2 * BLOCK), onehot).T
    g["attn_sinks"] = dsink.reshape(N_Q_HEADS)

    dz1, big["conv_dw_kernel"], g["conv_dw_bias"], g["conv_ln_g"], g["conv_ln_b"] = _conv_bwd_ln(
        p, z1, dz3, w["conv_ln_g"], w["conv_ln_b"])
    dp = _conv_bwd_glu(p, dz1, w["conv_dw_kernel"], dq, dkv, dgates)
    tk = min(TOKEN_TILE, t)
    big["w_in"] = _mm("d_w_in", (1, N_CHIPS, t // tk),
                    hm, pl.BlockSpec((tk, D_MODEL), lambda i, j, kk: (kk, 0)),
                    dp, pl.BlockSpec((tk, SHARD_W), lambda i, j, kk: (kk, j)), (D_MODEL, SHARD_W),
                    trans_a=True,
                    out_shape=(jax.ShapeDtypeStruct((N_CHIPS, D_MODEL, SHARD_W), BF16),),
                    out_specs=(pl.BlockSpec((None, D_MODEL, SHARD_W), lambda i, j, kk: (j, 0, 0)),),
                    epilogue=_store_bf16_epilogue)[0]
    tn = min(NORM_GRAD_TILE, t)
    dres1, g["mix_norm"] = _norm_input_grad(
        "d_mix", dp, [pl.BlockSpec((tn, SHARD_W), functools.partial(lambda i, s: (i, s), s=s)) for s in range(N_CHIPS)],
        w["w_in"], x1, w["mix_norm"], dres2)

    tokens = comm.reduce_finish("R1", dres1, behind=True) + comm.reduce_start("R2", big, behind=True)

    def ffn1_first(du):
        comm.join_finish("R1", du)
        return comm.exchange_finish("R2", du)

    def ffn1_grads(dw_in4, dw_out):
        late = comm.reduce_finish("R2", dw_in4, behind=True)
        return late + comm.reduce_start("R3", {"ffn1_w_in": dw_in4, "ffn1_w_out": dw_out})

    grad_x, _, _, g["ffn1_norm"] = _ffn_bwd(
        "ffn1b", dres1, x, w["ffn1_norm"], ffn1_saved, w["ffn1_w_in"], w["ffn1_w_out"], tokens, ffn1_first, ffn1_grads)
    comm.join_finish("R2", grad_x)
    comm.reduce_finish("R3", grad_x)
    return loss[0, 0], grad_x, g


def _mesh_place():
    x, y, c = lax.axis_index("x"), lax.axis_index("y"), lax.axis_index("c")
    chips = [(1 - x, y), (x, 1 - y), (1 - x, 1 - y)]
    return x, y, c, chips


def _any_specs(n):
    return [pl.BlockSpec(memory_space=pl.ANY)] * n


HBM_SPEC = pl.BlockSpec(memory_space=pltpu.HBM)
SEM_SPEC = pl.BlockSpec(memory_space=pltpu.SEMAPHORE)
EFFECT = pltpu.SideEffectType.DATAFLOW_SIDE_EFFECTING


def _in_hbm(a):
    return pltpu.with_memory_space_constraint(a, pltpu.HBM)


def _copy_start(name, srcs, lands, plan, after=()):
    ns, nb = len(srcs), len(lands)
    n = plan.copies_per_source * ns

    def body(*refs):
        s_refs, l_refs = refs[:ns], refs[ns:ns + nb]
        send_sems, recv_sems = refs[ns + nb + len(after)], refs[ns + nb + len(after) + 1]
        token = refs[-1]
        for k, (src, dst, to, _) in enumerate(plan(s_refs, l_refs)):
            pltpu.make_async_remote_copy(src_ref=src, dst_ref=dst, send_sem=send_sems.at[k], recv_sem=recv_sems.at[k],
                                         device_id=to, device_id_type=MESH).start()
        token[...] = jnp.zeros_like(token)

    bufs = list(srcs) + list(lands)
    outs = pl.pallas_call(
        body, name=name,
        out_shape=(pltpu.SemaphoreType.DMA((n,)), pltpu.SemaphoreType.DMA((n,)),
                   *[pltpu.HBM(a.shape, a.dtype) for a in bufs], jax.ShapeDtypeStruct((8, LANES), F32)),
        in_specs=[HBM_SPEC] * len(bufs) + [pl.BlockSpec(memory_space=pl.ANY)] * len(after),
        out_specs=(SEM_SPEC, SEM_SPEC, *[HBM_SPEC] * len(bufs), pl.BlockSpec(memory_space=pltpu.VMEM)),
        input_output_aliases={i: 2 + i for i in range(len(bufs))},
        compiler_params=pltpu.CompilerParams(has_side_effects=EFFECT),
    )(*[_in_hbm(a) for a in bufs], *after)
    return outs[0], outs[1], list(outs[2:2 + ns]), list(outs[2 + ns:2 + ns + nb]), outs[-1]


def _copy_wait(name, send_sems, recv_sems, srcs, lands, after, plan):
    ns, nb = len(srcs), len(lands)
    after = tuple(after) if isinstance(after, (tuple, list)) else (after,)

    def body(*refs):
        s_refs, l_refs = refs[:ns], refs[ns:ns + nb]
        send_sems, recv_sems = refs[ns + nb], refs[ns + nb + 1]
        for k, (src, _, to, mine) in enumerate(plan(s_refs, l_refs)):
            cp = pltpu.make_async_remote_copy(src_ref=src, dst_ref=mine, send_sem=send_sems.at[k], recv_sem=recv_sems.at[k],
                                              device_id=to, device_id_type=MESH)
            cp.wait_send()
            cp.wait_recv()

    bufs = list(srcs) + list(lands)
    outs = pl.pallas_call(
        body, name=name,
        out_shape=tuple(pltpu.HBM(a.shape, a.dtype) for a in bufs),
        in_specs=[HBM_SPEC] * len(bufs) + [SEM_SPEC, SEM_SPEC] + [pl.BlockSpec(memory_space=pl.ANY)] * len(after),
        out_specs=tuple([HBM_SPEC] * len(bufs)),
        input_output_aliases={i: i for i in range(len(bufs))},
        compiler_params=pltpu.CompilerParams(has_side_effects=EFFECT),
    )(*bufs, send_sems, recv_sems, *after)
    return list(outs[:ns]), list(outs[ns:])


def _gather_plan(s_refs, l_refs):
    x, y, c, chips = _mesh_place()
    jme = 2 * x + y
    return [(s.at[c], land.at[jme, c], (*chip, c), land.at[2 * chip[0] + chip[1], c])
            for s, land in zip(s_refs, l_refs) for chip in chips]


_gather_plan.copies_per_source = 3


def _gather_both_cores_plan(s_refs, l_refs):
    x, y, c, chips = _mesh_place()
    jme = 2 * x + y
    plan = []
    for s, land in zip(s_refs, l_refs):
        for chip in chips:
            for peer_core in (c, 1 - c):
                plan.append((s.at[c], land.at[jme, c], (*chip, peer_core), land.at[2 * chip[0] + chip[1], peer_core]))
        plan.append((s, land.at[jme], (x, y, 1 - c), land.at[jme]))
    return plan


_gather_both_cores_plan.copies_per_source = 7


def _scatter_plan(s_refs, l_refs):
    x, y, c, chips = _mesh_place()
    return [(s.at[2 * chip[0] + chip[1]], land.at[k], (*chip, c), land.at[k])
            for s, land in zip(s_refs, l_refs) for k, chip in enumerate(chips)]


_scatter_plan.copies_per_source = 3


def _exchange_plan(g_refs, l_refs):
    x, y, c, _ = _mesh_place()
    return [(g.at[:, 1 - c], land, (x, y, 1 - c), land) for g, land in zip(g_refs, l_refs)]


_exchange_plan.copies_per_source = 1


def _join_plan(h_refs, l_refs):
    x, y, c, _ = _mesh_place()
    return [(h.at[c], h.at[c], (x, y, 1 - c), h.at[1 - c]) for h in h_refs]


_join_plan.copies_per_source = 1


def _gather_forward(name, shards, landed):
    nw = len(shards)

    def body(*refs):
        s_refs, o_refs = refs[:nw], refs[2 * nw:3 * nw]
        send_sems, recv_sems = refs[3 * nw:]
        x, y, c, chips = _mesh_place()
        me, sib, jme = (x, y, c), (x, y, 1 - c), 2 * x + y
        sent = []
        for w in range(nw):
            parts = [(o_refs[w].at[2 * chip[0] + chip[1], c], o_refs[w].at[2 * chip[0] + chip[1], c]) for chip in chips]
            parts.append((s_refs[w], o_refs[w].at[jme]))
            for k, (src, dst) in enumerate(parts):
                cp = pltpu.make_async_remote_copy(src_ref=src, dst_ref=dst, send_sem=send_sems.at[4 * w + k],
                                                  recv_sem=recv_sems.at[4 * w + k], device_id=sib, device_id_type=MESH)
                cp.start()
                sent.append(cp)
        for w in range(nw):
            parts = [o_refs[w].at[2 * chip[0] + chip[1], 1 - c] for chip in chips] + [o_refs[w].at[jme]]
            for k, part in enumerate(parts):
                pltpu.make_async_remote_copy(src_ref=part, dst_ref=part, send_sem=send_sems.at[4 * w + k],
                                             recv_sem=recv_sems.at[4 * w + k], device_id=me, device_id_type=MESH).wait_recv()
        for cp in sent:
            cp.wait_send()

    return pl.pallas_call(
        body, name=name,
        in_specs=_any_specs(2 * nw), out_specs=_any_specs(nw),
        out_shape=[jax.ShapeDtypeStruct(a.shape, a.dtype) for a in landed],
        input_output_aliases={nw + i: i for i in range(nw)},
        scratch_shapes=[pltpu.SemaphoreType.DMA((4 * nw,)), pltpu.SemaphoreType.DMA((4 * nw,))],
    )(*shards, *landed)


def _exchange_halves(name, grads, after=()):
    nw = len(grads)

    def body(*refs):
        g_refs, o_refs = refs[:nw], refs[nw + len(after):2 * nw + len(after)]
        send_sems, recv_sems = refs[2 * nw + len(after):]
        x, y, c, _ = _mesh_place()
        copies = []
        for w in range(nw):
            cp = pltpu.make_async_remote_copy(src_ref=g_refs[w].at[:, 1 - c], dst_ref=o_refs[w], send_sem=send_sems.at[w],
                                              recv_sem=recv_sems.at[w], device_id=(x, y, 1 - c), device_id_type=MESH)
            cp.start()
            copies.append(cp)
        for cp in copies:
            cp.wait()

    return pl.pallas_call(
        body, name=name,
        in_specs=_any_specs(nw + len(after)), out_specs=_any_specs(nw),
        out_shape=[jax.ShapeDtypeStruct((N_CHIPS,) + g.shape[2:], g.dtype) for g in grads],
        scratch_shapes=[pltpu.SemaphoreType.DMA((nw,)), pltpu.SemaphoreType.DMA((nw,))],
    )(*grads, *after)


ELEMENTWISE_ROWS = 512


def _row_tile(r):
    for cand in range(min(r, ELEMENTWISE_ROWS) // 16 * 16, 0, -16):
        if r % cand == 0:
            return cand
    return r


def _add_own_halves(name, c_idx, grads, gots):
    n = len(grads)

    def body(c_ref, *refs):
        for g_ref, o_ref, out_ref in zip(refs[:n], refs[n:2 * n], refs[2 * n:]):
            out_ref[...] = (g_ref[...].astype(F32) + o_ref[...].astype(F32)).astype(BF16)

    def whole(a):
        return pl.BlockSpec((None,) + a.shape[1:], lambda j, c_ref: (j, 0, 0))

    return pl.pallas_call(
        body, name=name,
        grid_spec=pltpu.PrefetchScalarGridSpec(
            num_scalar_prefetch=1, grid=(N_CHIPS,),
            in_specs=[pl.BlockSpec((None, None) + g.shape[2:], lambda j, c_ref: (j, c_ref[0], 0, 0)) for g in grads]
            + [whole(o) for o in gots],
            out_specs=[whole(o) for o in gots]),
        out_shape=[jax.ShapeDtypeStruct(o.shape, BF16) for o in gots],
        compiler_params=_params(("parallel",)),
    )(c_idx, *grads, *gots)


def _sum_pieces(name, place_idx, sums, landed):
    n = len(sums)

    def body(j_ref, *refs):
        for own_ref, p_ref, o_ref in zip(refs[:n], refs[n:2 * n], refs[2 * n:]):
            o_ref[...] = ((own_ref[...].astype(F32) + p_ref[0].astype(F32)) + p_ref[1].astype(F32)) + p_ref[2].astype(F32)

    return pl.pallas_call(
        body, name=name,
        grid_spec=pltpu.PrefetchScalarGridSpec(
            num_scalar_prefetch=1, grid=(1,),
            in_specs=[pl.BlockSpec((None,) + s.shape[1:], lambda i, j_ref: (j_ref[0], 0, 0)) for s in sums]
            + [pl.BlockSpec(p.shape, lambda i, j_ref: (0, 0, 0)) for p in landed],
            out_specs=[pl.BlockSpec((None,) + s.shape[1:], lambda i, j_ref: (j_ref[1], 0, 0)) for s in sums]),
        out_shape=[jax.ShapeDtypeStruct((2,) + s.shape[1:], F32) for s in sums],
        compiler_params=_params(("arbitrary",)),
    )(place_idx, *sums, *landed)


def _join_halves(name, halves):
    nw = len(halves)

    def body(*refs):
        o_refs = refs[nw:2 * nw]
        send_sems, recv_sems = refs[2 * nw:]
        x, y, c, _ = _mesh_place()
        copies = []
        for w in range(nw):
            cp = pltpu.make_async_remote_copy(src_ref=o_refs[w].at[c], dst_ref=o_refs[w].at[c], send_sem=send_sems.at[w],
                                              recv_sem=recv_sems.at[w], device_id=(x, y, 1 - c), device_id_type=MESH)
            cp.start()
            copies.append(cp)
        for w in range(nw):
            copies[w].wait_send()
            landed = o_refs[w].at[1 - c]
            pltpu.make_async_remote_copy(src_ref=landed, dst_ref=landed, send_sem=send_sems.at[w], recv_sem=recv_sems.at[w],
                                         device_id=(x, y, c), device_id_type=MESH).wait_recv()

    return pl.pallas_call(
        body, name=name,
        in_specs=_any_specs(nw), out_specs=_any_specs(nw),
        out_shape=[jax.ShapeDtypeStruct(h.shape, F32) for h in halves],
        input_output_aliases={i: i for i in range(nw)},
        scratch_shapes=[pltpu.SemaphoreType.DMA((nw,)), pltpu.SemaphoreType.DMA((nw,))],
    )(*halves)


SMALL_ROWS = 8


def _all_reduce_small(pack):
    rows, cols = pack.shape
    n_dev = 8

    def body(p_ref, o_ref, slots, send_sems, recv_sems):
        x, y, c, _ = _mesh_place()
        me = 4 * x + 2 * y + c
        slots[me] = p_ref[...]
        copies = []
        for k in range(1, n_dev):
            peer = (me + k) % n_dev
            cp = pltpu.make_async_remote_copy(src_ref=p_ref, dst_ref=slots.at[me], send_sem=send_sems.at[k],
                                              recv_sem=recv_sems.at[k],
                                              device_id=(peer // 4, (peer // 2) % 2, peer % 2), device_id_type=MESH)
            cp.start()
            copies.append(cp)
        for k in range(1, n_dev):
            src = (me + n_dev - k) % n_dev
            pltpu.make_async_remote_copy(src_ref=p_ref, dst_ref=slots.at[src], send_sem=send_sems.at[k],
                                         recv_sem=recv_sems.at[k], device_id=(x, y, c), device_id_type=MESH).wait_recv()
        for cp in copies:
            cp.wait_send()
        total = slots[0]
        for s in range(1, n_dev):
            total = total + slots[s]
        o_ref[...] = total

    return pl.pallas_call(
        body, name="all_reduce_small",
        in_specs=[pl.BlockSpec(memory_space=pltpu.VMEM)], out_specs=pl.BlockSpec(memory_space=pltpu.VMEM),
        out_shape=jax.ShapeDtypeStruct((rows, cols), F32),
        scratch_shapes=[pltpu.VMEM((n_dev, rows, cols), F32), pltpu.SemaphoreType.DMA((n_dev,)),
                        pltpu.SemaphoreType.DMA((n_dev,))],
    )(pack)


ADAMW_STEPS = 4


def _adamw(name, ws, gs, ms, vs):
    n = len(ws)
    steps = ADAMW_STEPS if all(w.shape[0] % (8 * ADAMW_STEPS) == 0 for w in ws) else 1

    def body(*refs):
        w_refs, g_refs, m_refs, v_refs = (refs[k * n:(k + 1) * n] for k in range(4))
        d_refs, nm_refs, nv_refs = (refs[(4 + k) * n:(5 + k) * n] for k in range(3))
        for w_ref, g_ref, m_ref, v_ref, d_ref, nm_ref, nv_ref in zip(w_refs, g_refs, m_refs, v_refs, d_refs, nm_refs, nv_refs):
            gv = g_ref[...]
            nm = ADAM_B1 * m_ref[...] + (1.0 - ADAM_B1) * gv
            nv = ADAM_B2 * v_ref[...] + (1.0 - ADAM_B2) * (gv * gv)
            m_hat = nm / (1.0 - ADAM_B1 ** ADAM_STEP)
            v_hat = nv / (1.0 - ADAM_B2 ** ADAM_STEP)
            d_ref[...] = -ADAM_LR * (m_hat / (jnp.sqrt(v_hat) + ADAM_EPS) + ADAM_WD * w_ref[...])
            nm_ref[...] = nm
            nv_ref[...] = nv

    specs = [pl.BlockSpec((w.shape[0] // steps, w.shape[1]), lambda i: (i, 0)) for w in ws]
    shapes = [jax.ShapeDtypeStruct(w.shape, F32) for w in ws]
    out = pl.pallas_call(
        body, name=name, grid=(steps,),
        in_specs=specs * 4, out_specs=specs * 3, out_shape=shapes * 3,
        compiler_params=_params(("parallel",)),
    )(*ws, *gs, *ms, *vs)
    return out[:n], out[n:2 * n], out[2 * n:]


BIG = ["ffn1_w_in", "ffn1_w_out", "w_in", "conv_w_proj", "attn_w_o", "w_out", "ffn2_w_in", "ffn2_w_out", "conv_dw_kernel"]
COL_SHARDED = ("ffn1_w_in", "w_in", "ffn2_w_in")
SMALL = ["ffn1_norm", "mix_norm", "ffn2_norm", "conv_dw_bias", "conv_ln_g", "conv_ln_b", "q_norm", "k_norm", "attn_sinks", "rel_bias"]
WEIGHTS = ["ffn1_norm", "ffn1_w_in", "ffn1_w_out", "mix_norm", "w_in", "conv_dw_kernel", "conv_dw_bias", "conv_ln_g",
           "conv_ln_b", "conv_w_proj", "q_norm", "k_norm", "attn_sinks", "rel_bias", "attn_w_o", "w_out", "ffn2_norm",
           "ffn2_w_in", "ffn2_w_out"]
SMALL_PLACE = {"ffn1_norm": (0, 0, 1024), "mix_norm": (1, 0, 1024), "ffn2_norm": (2, 0, 1024), "conv_dw_bias": (3, 0, 1024),
               "conv_ln_g": (4, 0, 1024), "conv_ln_b": (5, 0, 1024), "q_norm": (6, 0, 64), "k_norm": (6, 128, 64),
               "attn_sinks": (6, 256, 16), "rel_bias": (7, 0, 512)}
LOSS_PLACE = (6, 384)


def _pack_small(vals, fill=0.0, loss=None):
    pack = jnp.full((SMALL_ROWS, D_MODEL), fill, F32)
    for name, (row, lane, n) in SMALL_PLACE.items():
        pack = pack.at[row, lane:lane + n].set(vals[name].reshape(n))
    if loss is not None:
        pack = pack.at[LOSS_PLACE[0], LOSS_PLACE[1]].set(loss)
    return pack


def _unpack_small(pack, shapes):
    return {name: pack[row, lane:lane + n].reshape(shapes[name]) for name, (row, lane, n) in SMALL_PLACE.items()}


def _shard_halves(name, a):
    if name == "conv_dw_kernel":
        a = jnp.pad(a, ((0, CONV_PAD - CONV_WIDTH), (0, 0)))
    r, cols = a.shape
    return a.reshape(2, r // 2, cols)


GATHER_GROUPS = {"A": ["ffn1_w_in", "ffn1_w_out"],
                 "B": ["w_in", "conv_dw_kernel", "conv_w_proj", "attn_w_o", "w_out"],
                 "C": ["ffn2_w_in", "ffn2_w_out"]}


REDUCE_GROUPS = {"R1": ["ffn2_w_in", "ffn2_w_out"],
                 "R2": ["w_out", "conv_w_proj", "attn_w_o", "conv_dw_kernel", "w_in"],
                 "R3": ["ffn1_w_in", "ffn1_w_out"]}


class _MeshComm:
    def __init__(self, wts, idle_work=()):
        self.idle_work = idle_work
        self.c_idx = lax.axis_index("c").astype(jnp.int32).reshape(1)
        self.place_idx = jnp.stack([2 * lax.axis_index("x") + lax.axis_index("y"), lax.axis_index("c")]).astype(jnp.int32)
        self.gathers, self.exchanges, self.reductions, self.joins, self.reduced = {}, {}, {}, {}, {}
        self.tokens = ()
        self.shards = {n: _shard_halves(n, wts[n]) if n == "conv_dw_kernel" else _shard_halves(n, wts[n]).astype(BF16)
                       for n in BIG}
        self._gather_start("A", ())

    def _gather_start(self, group, after):
        shards = [self.shards[n] for n in GATHER_GROUPS[group]]
        lands = [lax.empty((N_CHIPS,) + s.shape, s.dtype) for s in shards]
        self.gathers[group] = _copy_start("gather_start_" + group, shards, lands, self._gather_plan(group), after=after)
        self.tokens = (self.gathers[group][-1],)

    @staticmethod
    def _gather_plan(group):
        return _gather_both_cores_plan if group == "C" else _gather_plan

    def weights(self, group, after):
        send_sems, recv_sems, shards, lands, token = self.gathers.pop(group)
        after = [token] if after is None else list(after) if isinstance(after, (list, tuple)) else [after]
        if group == "A":
            after += [self.shards[n] for g in ("B", "C") for n in GATHER_GROUPS[g]] + list(self.idle_work)
        shards, lands = _copy_wait("gather_wait_" + group, send_sems, recv_sems, shards, lands, after,
                                   self._gather_plan(group))
        gathered = lands if group == "C" else _gather_forward("gather_forward_" + group, shards, lands)
        self.tokens = ()
        following = {"A": "B", "B": "C"}.get(group)
        if following:
            self._gather_start(following, (gathered[0],))
        out = {}
        for n, g4 in zip(GATHER_GROUPS[group], gathered):
            r, cols = g4.shape[2] * 2, g4.shape[3]
            if n in COL_SHARDED:
                out[n] = g4.reshape(N_CHIPS, r, cols)
            elif n == "conv_dw_kernel":
                out[n] = g4.reshape(N_CHIPS, r, cols).transpose(1, 0, 2).reshape(r, N_CHIPS * cols)
            else:
                out[n] = g4.reshape(N_CHIPS * r, cols)
        return out

    def reduce_start(self, group, grads, behind=False):
        names = list(grads)
        g4 = []
        for n in names:
            a = grads[n]
            if n == "conv_dw_kernel":
                a = a.reshape(CONV_PAD, N_CHIPS, -1).transpose(1, 0, 2)
            elif n not in COL_SHARDED:
                a = a.reshape(N_CHIPS, a.shape[0] // N_CHIPS, a.shape[1])
            g4.append(a.reshape(N_CHIPS, 2, a.shape[1] // 2, a.shape[2]))
        if behind:
            lands = [lax.empty((N_CHIPS,) + g.shape[2:], g.dtype) for g in g4]
            started = _copy_start("exchange_start_" + group, g4, lands, _exchange_plan)
            self.exchanges[group] = (names,) + started
            return (started[-1],)
        return self._scatter_start(group, names, g4, _exchange_halves("exchange_halves_" + group, g4))

    def exchange_finish(self, group, after):
        names, send_sems, recv_sems, g4, lands, _ = self.exchanges.pop(group)
        g4, got = _copy_wait("exchange_wait_" + group, send_sems, recv_sems, g4, lands, after, _exchange_plan)
        return self._scatter_start(group, names, g4, got)

    def _scatter_start(self, group, names, g4, got):
        sums = _add_own_halves("add_own_halves_" + group, self.c_idx, g4, got)
        lands = [lax.empty((N_CHIPS - 1,) + s.shape[1:], s.dtype) for s in sums]
        started = _copy_start("scatter_start_" + group, sums, lands, _scatter_plan)
        self.reductions[group] = (names,) + started
        return (started[-1],)

    def reduce_finish(self, group, after, behind=False):
        names, send_sems, recv_sems, sums, lands, _ = self.reductions.pop(group)
        sums, lands = _copy_wait("scatter_wait_" + group, send_sems, recv_sems, sums, lands, after, _scatter_plan)
        halves = _sum_pieces("sum_pieces_" + group, self.place_idx, sums, lands)
        if behind:
            started = _copy_start("join_start_" + group, halves, [], _join_plan)
            self.joins[group] = (names,) + started
            return (started[-1],)
        self.reduced.update(zip(names, _join_halves("join_halves_" + group, halves)))
        return ()

    def join_finish(self, group, after):
        names, send_sems, recv_sems, halves, _, _ = self.joins.pop(group)
        self.reduced.update(zip(names, _copy_wait("join_wait_" + group, send_sems, recv_sems, halves, [], after, _join_plan)[0]))


def kernel(x, ffn1_norm, ffn1_w_in, ffn1_w_out, mix_norm, w_in, conv_dw_kernel, conv_dw_bias, conv_ln_g, conv_ln_b, conv_w_proj, q_norm, k_norm, attn_sinks, rel_bias, attn_w_o, w_out, ffn2_norm, ffn2_w_in, ffn2_w_out, loss_target, m_ffn1_norm, m_ffn1_w_in, m_ffn1_w_out, m_mix_norm, m_w_in, m_conv_dw_kernel, m_conv_dw_bias, m_conv_ln_g, m_conv_ln_b, m_conv_w_proj, m_q_norm, m_k_norm, m_attn_sinks, m_rel_bias, m_attn_w_o, m_w_out, m_ffn2_norm, m_ffn2_w_in, m_ffn2_w_out, v_ffn1_norm, v_ffn1_w_in, v_ffn1_w_out, v_mix_norm, v_w_in, v_conv_dw_kernel, v_conv_dw_bias, v_conv_ln_g, v_conv_ln_b, v_conv_w_proj, v_q_norm, v_k_norm, v_attn_sinks, v_rel_bias, v_attn_w_o, v_w_out, v_ffn2_norm, v_ffn2_w_in, v_ffn2_w_out):
    args = dict(locals())
    wts = {n: args[n] for n in WEIGHTS}
    mom = {n: args["m_" + n] for n in WEIGHTS}
    var = {n: args["v_" + n] for n in WEIGHTS}
    small_packs = [_pack_small(wts), _pack_small(mom), _pack_small(var, fill=1.0)]
    dw_moments = [_shard_halves("conv_dw_kernel", a) for a in (wts["conv_dw_kernel"], mom["conv_dw_kernel"], var["conv_dw_kernel"])]
    comm = _MeshComm(wts, idle_work=small_packs + dw_moments)
    small = {n: wts[n] if n in ("attn_sinks", "rel_bias") else wts[n].reshape(1, -1) for n in SMALL}
    loss_part, grad_x, g = _local_step(x[0], loss_target[0], small, comm)

    small_sum = _all_reduce_small(_pack_small(g, loss=loss_part))
    loss = small_sum[LOSS_PLACE[0], LOSS_PLACE[1]]
    small_shapes = {n: wts[n].shape for n in SMALL}
    g_small = _unpack_small(small_sum, small_shapes)

    grads, delta, new_m, new_v = {}, {}, {}, {}
    for group, names in REDUCE_GROUPS.items():
        gs = [comm.reduced[n].reshape(-1, comm.reduced[n].shape[2]) for n in names]
        state = [[a.reshape(g.shape) for a in (dw_moments if n == "conv_dw_kernel" else (wts[n], mom[n], var[n]))]
                 for n, g in zip(names, gs)]
        ds, nms, nvs = _adamw("adamw_" + group, [s[0] for s in state], gs, [s[1] for s in state], [s[2] for s in state])
        for n, g, d, nm, nv in zip(names, gs, ds, nms, nvs):
            cut = (lambda a: a[:CONV_WIDTH]) if n == "conv_dw_kernel" else (lambda a: a)
            grads[n], delta[n], new_m[n], new_v[n] = cut(g), cut(d), cut(nm), cut(nv)
    (d,), (nm,), (nv,) = _adamw("adamw_small", [small_packs[0]], [small_sum], [small_packs[1]], [small_packs[2]])
    grads.update(g_small)
    delta.update(_unpack_small(d, small_shapes))
    new_m.update(_unpack_small(nm, small_shapes))
    new_v.update(_unpack_small(nv, small_shapes))

    return (loss, grad_x[None], *[grads[n] for n in WEIGHTS], *[delta[n] for n in WEIGHTS],
            *[new_m[n] for n in WEIGHTS], *[new_v[n] for n in WEIGHTS])
```

```python
import functools
import math

import jax
import jax.numpy as jnp
from jax import lax
from jax.experimental import pallas as pl
from jax.experimental.pallas import tpu as pltpu

F32 = jnp.float32
BF16 = jnp.bfloat16
MESH = pl.DeviceIdType.MESH

EPS = 1e-6
D_MODEL = 1024
D_FF = 2816
N_CHIPS = 4
SHARD_W = 2 * D_FF // N_CHIPS
HEAD_DIM = 64
N_Q_HEADS = 16
N_KV_HEADS = 4
GROUP = N_Q_HEADS // N_KV_HEADS
BLOCK = 128
QROWS = GROUP * BLOCK
N_BUCKETS = 32
MAX_DISTANCE = 128
CONV_WIDTH = 31
CONV_PAD = 32
NEG = float(jnp.finfo(jnp.float32).min)

ADAM_LR = 0.001
ADAM_B1 = 0.9
ADAM_B2 = 0.999
ADAM_EPS = 1e-08
ADAM_WD = 0.01
ADAM_STEP = 10

VMEM_LIMIT_BYTES = 56 * 1024 * 1024
ROW_TILE = 1024
TOKEN_TILE = 4096
CONV_TILE = 256
CONV_ROWS = 128
LANES = 128

COL_CONV_A, COL_CONV_G, COL_Q, COL_K, COL_V, COL_GC, COL_GA = 0, 1024, 2048, 3072, 3328, 3584, 4608
IN_W = 5632


def _params(sem, vmem=VMEM_LIMIT_BYTES):
    return pltpu.CompilerParams(dimension_semantics=sem, vmem_limit_bytes=vmem)


def _sigmoid(x):
    return 1.0 / (1.0 + jnp.exp(-x))


def _dot(a, b, trans_a=False, trans_b=False, precision=None):
    dn = (((0,) if trans_a else (1,), (1,) if trans_b else (0,)), ((), ()))
    return lax.dot_general(a, b, dn, preferred_element_type=F32, precision=precision)


def _mm(name, grid, a, a_spec, b, b_spec, acc_shape, *, trans_a=False, trans_b=False, a_pre=None, b_pre=None,
        extras=(), extra_specs=(), tokens=(), out_shape, out_specs, epilogue,
        sem=("parallel", "parallel", "arbitrary")):
    n_k = grid[2]
    extras = tuple(extras) + tuple(tokens)
    extra_specs = tuple(extra_specs) + (pl.BlockSpec((8, LANES), lambda i, j, kk: (0, 0)),) * len(tokens)
    n_extra = len(extras)
    n_out = len(out_shape)

    def body(a_ref, b_ref, *rest):
        ex = rest[:n_extra]
        outs = rest[n_extra:n_extra + n_out]
        ids = (pl.program_id(0), pl.program_id(1), pl.program_id(2))
        av = a_ref[...]
        if a_pre is not None:
            av = a_pre(av)
        bv = b_ref[...]
        if b_pre is not None:
            bv = b_pre(bv)
        if n_k == 1:
            epilogue(_dot(av, bv, trans_a, trans_b), ex, outs, ids)
        else:
            acc = rest[-1]

            @pl.when(ids[2] == 0)
            def _():
                acc[...] = jnp.zeros_like(acc)

            acc[...] += _dot(av, bv, trans_a, trans_b)

            @pl.when(ids[2] == n_k - 1)
            def _():
                epilogue(acc[...], ex, outs, ids)

    scratch = [] if n_k == 1 else [pltpu.VMEM(acc_shape, F32)]
    return pl.pallas_call(
        body, name=name, grid=grid,
        in_specs=[a_spec, b_spec, *extra_specs],
        out_specs=list(out_specs), out_shape=list(out_shape),
        scratch_shapes=scratch, compiler_params=_params(sem),
    )(a, b, *extras)


def _half_bf16(v):
    return (0.5 * v).astype(BF16)


def _to_bf16(v):
    return v.astype(BF16)


def _rmsnorm_fwd(name, x, g, tokens=()):
    t, d = x.shape
    tm = min(ROW_TILE, t)

    def body(x_ref, g_ref, *rest):
        o_ref = rest[-1]
        xv = x_ref[...]
        r = lax.rsqrt(jnp.mean(xv * xv, axis=-1, keepdims=True) + EPS)
        o_ref[...] = (xv * r * g_ref[...]).astype(BF16)

    return pl.pallas_call(
        body, name=name, grid=(t // tm,),
        in_specs=[pl.BlockSpec((tm, d), lambda i: (i, 0)), pl.BlockSpec((1, d), lambda i: (0, 0))]
        + [pl.BlockSpec((8, LANES), lambda i: (0, 0))] * len(tokens),
        out_specs=pl.BlockSpec((tm, d), lambda i: (i, 0)),
        out_shape=jax.ShapeDtypeStruct((t, d), BF16),
        compiler_params=_params(("parallel",)),
    )(x, g, *tokens)


def _ffn_in(name, n, w_in4, tokens=()):
    t, d = n.shape
    tm = min(ROW_TILE, t)

    def body(n_ref, wa_ref, wb_ref, *rest):
        ab_ref, h_ref = rest[-2:]
        nv = n_ref[...]
        a = _dot(nv, wa_ref[...])
        b = _dot(nv, wb_ref[...])
        h_ref[...] = (a * _sigmoid(a) * b).astype(BF16)
        ab_ref[0] = a.astype(BF16)
        ab_ref[1] = b.astype(BF16)

    return pl.pallas_call(
        body, name=name, grid=(2, t // tm),
        in_specs=[pl.BlockSpec((tm, d), lambda j, i: (i, 0)),
                  pl.BlockSpec((None, d, SHARD_W), lambda j, i: (j, 0, 0)),
                  pl.BlockSpec((None, d, SHARD_W), lambda j, i: (j + 2, 0, 0))]
        + [pl.BlockSpec((8, LANES), lambda j, i: (0, 0))] * len(tokens),
        out_specs=[pl.BlockSpec((2, tm, SHARD_W), lambda j, i: (0, i, j)),
                   pl.BlockSpec((tm, SHARD_W), lambda j, i: (i, j))],
        out_shape=[jax.ShapeDtypeStruct((2, t, D_FF), BF16), jax.ShapeDtypeStruct((t, D_FF), BF16)],
        compiler_params=_params(("parallel", "parallel")),
    )(n, w_in4, w_in4, *tokens)


def _mm_residual(name, a, w, res, scale, next_gain=None, loss_target=None):
    t, k = a.shape
    n = w.shape[1]
    tm = min(ROW_TILE, t)
    row = pl.BlockSpec((tm, n), lambda i, j, kk: (i, 0))
    extras, specs = [res], [row]
    shapes, out_specs = [jax.ShapeDtypeStruct((t, n), F32)], [row]
    if next_gain is not None:
        extras.append(next_gain)
        specs.append(pl.BlockSpec((1, n), lambda i, j, kk: (0, 0)))
        shapes.append(jax.ShapeDtypeStruct((t, n), BF16))
        out_specs.append(row)
    if loss_target is not None:
        extras.append(loss_target)
        specs.append(row)
        shapes.append(jax.ShapeDtypeStruct((8, LANES), F32))
        out_specs.append(pl.BlockSpec((8, LANES), lambda i, j, kk: (0, 0)))

    def epilogue(acc, ex, outs, ids):
        y = ex[0][...] + scale * acc
        if loss_target is None:
            outs[0][...] = y
        if next_gain is not None:
            r = lax.rsqrt(jnp.mean(y * y, axis=-1, keepdims=True) + EPS)
            outs[1][...] = (y * r * ex[1][...]).astype(BF16)
        if loss_target is not None:
            diff = y - ex[1][...]
            outs[0][...] = diff * (1.0 / n)
            part = jnp.full((8, LANES), 0.5 / n * jnp.sum(diff * diff), F32)

            @pl.when(ids[0] == 0)
            def _():
                outs[1][...] = part

            @pl.when(ids[0] > 0)
            def _():
                outs[1][...] += part

    sem = ("parallel" if loss_target is None else "arbitrary", "parallel", "arbitrary")
    out = _mm(name, (t // tm, 1, 1), a, pl.BlockSpec((tm, k), lambda i, j, kk: (i, 0)),
              w, pl.BlockSpec((k, n), lambda i, j, kk: (0, 0)), (tm, n),
              extras=extras, extra_specs=specs, out_shape=shapes, out_specs=out_specs, epilogue=epilogue, sem=sem)
    return out[0] if len(out) == 1 else tuple(out)


def _ffn_fwd(tag, x, n, w_in4, w_out, tokens=(), **tail):
    ab, h = _ffn_in(tag + "_in", n, w_in4, tokens)
    y = _mm_residual(tag + "_out", h, w_out, x, 0.5, **tail)
    return y, (n, ab, h)


def _ffn_bwd(tag, dres, x, g, saved, w_in4, w_out, tokens=(), on_first=None, on_weight_grads=None):
    n, ab, h = saved
    t, d = x.shape
    tm = min(ROW_TILE, t)
    tk = min(TOKEN_TILE, t)
    half_w = SHARD_W

    def dact_epilogue(acc, ex, outs, ids):
        a = ex[0][0].astype(F32)
        b = ex[0][1].astype(F32)
        sig = _sigmoid(a)
        outs[0][0] = (acc * b * (sig * (1.0 + a * (1.0 - sig)))).astype(BF16)
        outs[0][1] = (acc * (a * sig)).astype(BF16)

    du = _mm(tag + "_dact", (2, t // tm, 1),
             dres, pl.BlockSpec((tm, d), lambda j, i, kk: (i, 0)),
             w_out, pl.BlockSpec((half_w, d), lambda j, i, kk: (j, 0)), (tm, half_w),
             trans_b=True, a_pre=_half_bf16,
             extras=(ab,), extra_specs=(pl.BlockSpec((2, tm, half_w), lambda j, i, kk: (0, i, j)),), tokens=tokens,
             out_shape=(jax.ShapeDtypeStruct((2, t, D_FF), BF16),),
             out_specs=(pl.BlockSpec((2, tm, half_w), lambda j, i, kk: (0, i, j)),),
             epilogue=dact_epilogue)[0]

    def store_epilogue(acc, ex, outs, ids):
        outs[0][...] = acc.astype(BF16)

    early = () if on_first is None else on_first(du)

    tk_out = min(TOKEN_TILE // 2, t)
    dw_out = _mm(tag + "_dwout", (2, 1, t // tk_out),
                 h, pl.BlockSpec((tk_out, half_w), lambda i, j, kk: (kk, i)),
                 dres, pl.BlockSpec((tk_out, d), lambda i, j, kk: (kk, 0)), (half_w, d),
                 trans_a=True, b_pre=_half_bf16, tokens=early,
                 out_shape=(jax.ShapeDtypeStruct((D_FF, d), BF16),),
                 out_specs=(pl.BlockSpec((half_w, d), lambda i, j, kk: (i, 0)),),
                 epilogue=store_epilogue)[0]

    dw_in4 = _mm(tag + "_dwin", (1, N_CHIPS, t // tk),
                 n, pl.BlockSpec((tk, d), lambda i, j, kk: (kk, 0)),
                 du, pl.BlockSpec((None, tk, SHARD_W), lambda i, j, kk: (j // 2, kk, j % 2)), (d, SHARD_W),
                 trans_a=True,
                 out_shape=(jax.ShapeDtypeStruct((N_CHIPS, d, SHARD_W), BF16),),
                 out_specs=(pl.BlockSpec((None, d, SHARD_W), lambda i, j, kk: (j, 0, 0)),),
                 epilogue=store_epilogue)[0]

    late = () if on_weight_grads is None else on_weight_grads(dw_in4, dw_out)

    tn = min(NORM_GRAD_TILE, t)
    shard_specs = [pl.BlockSpec((None, tn, SHARD_W), functools.partial(lambda i, s: (s // 2, i, s % 2), s=s))
                   for s in range(N_CHIPS)]
    dx, dg = _norm_input_grad(tag + "_dn", du, shard_specs, w_in4, x, g, dres, late)
    return dx, dw_in4, dw_out, dg


NORM_GRAD_TILE = 512


def _norm_input_grad(name, a, shard_specs, w4, x, g, dres, tokens=()):
    t, d = x.shape
    tn = min(NORM_GRAD_TILE, t)
    ns = len(shard_specs)

    def body(*refs):
        a_refs, (w_ref, x_ref, g_ref, dres_ref) = refs[:ns], refs[ns:ns + 4]
        out_ref, dg_ref = refs[-2:]
        acc = _dot(a_refs[0][...], w_ref[0], trans_b=True)
        for s in range(1, ns):
            acc = acc + _dot(a_refs[s][...], w_ref[s], trans_b=True)
        xv = x_ref[...]
        r = lax.rsqrt(jnp.mean(xv * xv, axis=-1, keepdims=True) + EPS)
        w = acc * g_ref[...]
        out_ref[...] = dres_ref[...] + (r * w - xv * (r * r * r) * jnp.mean(xv * w, axis=-1, keepdims=True))
        part = jnp.sum(acc * (xv * r), axis=0, keepdims=True)
        i = pl.program_id(0)

        @pl.when(i == 0)
        def _():
            dg_ref[...] = part

        @pl.when(i > 0)
        def _():
            dg_ref[...] += part

    row = pl.BlockSpec((tn, d), lambda i: (i, 0))
    vec = pl.BlockSpec((1, d), lambda i: (0, 0))
    return pl.pallas_call(
        body, name=name, grid=(t // tn,),
        in_specs=list(shard_specs) + [pl.BlockSpec(w4.shape, lambda i: (0, 0, 0)), row, vec, row]
        + [pl.BlockSpec((8, LANES), lambda i: (0, 0))] * len(tokens),
        out_specs=[row, vec],
        out_shape=[jax.ShapeDtypeStruct((t, d), F32), jax.ShapeDtypeStruct((1, d), F32)],
        compiler_params=_params(("arbitrary",)),
    )(*[a] * ns, w4, x, g, dres, *tokens)


def _conv_fill(zp_ref, a_ref, g_ref, ah_ref, gh_ref, i):
    zh = ah_ref[...].astype(F32) * _sigmoid(gh_ref[...].astype(F32))
    zp_ref[pl.ds(0, CONV_PAD), :] = jnp.where(i > 0, zh, 0.0)
    zp_ref[pl.ds(CONV_PAD, a_ref.shape[0]), :] = a_ref[...].astype(F32) * _sigmoid(g_ref[...].astype(F32))


def _shift_groups(shifts):
    groups = {}
    for j, s in shifts:
        groups.setdefault(s % 8, []).append((j, s // 8))
    return groups


def _windows(zp_ref, r0, lanes, groups):
    for q, taps in groups.items():
        deepest = max(p for _, p in taps)
        win = zp_ref[pl.ds(r0 + q, 8 * deepest + CONV_ROWS), lanes]
        for j, p in taps:
            yield j, win[8 * p:8 * p + CONV_ROWS]


def _conv_apply(zp_ref, out_ref, dw_ref, bias_ref, tm, ch, shifts):
    groups = _shift_groups(shifts)
    for cc in range(ch // LANES):
        lanes = pl.ds(cc * LANES, LANES)
        w = [dw_ref[pl.ds(j, 1), lanes] for j in range(CONV_WIDTH)]
        for r0 in range(0, tm, CONV_ROWS):
            if bias_ref is None:
                acc = jnp.zeros((CONV_ROWS, LANES), F32)
            else:
                acc = jnp.broadcast_to(bias_ref[:, lanes], (CONV_ROWS, LANES))
            for j, rows in _windows(zp_ref, r0, lanes, groups):
                acc = acc + w[j] * rows
            out_ref[pl.ds(r0, CONV_ROWS), lanes] = acc


FWD_SHIFTS = [(j, CONV_PAD - (CONV_WIDTH - 1) + j) for j in range(CONV_WIDTH)]
BWD_SHIFTS = [(j, CONV_WIDTH - 1 - j) for j in range(CONV_WIDTH)]


def _conv_taps(zp_ref, z1_ref, dw_ref, bias_ref, tm, ch):
    _conv_apply(zp_ref, z1_ref, dw_ref, bias_ref, tm, ch, FWD_SHIFTS)


def _conv_specs(tm, ch):
    per = tm // CONV_PAD
    cb = COL_CONV_G // ch
    return [pl.BlockSpec((tm, ch), lambda i: (i, 0)),
            pl.BlockSpec((tm, ch), lambda i: (i, cb)),
            pl.BlockSpec((CONV_PAD, ch), lambda i: (jnp.maximum(i * per - 1, 0), 0)),
            pl.BlockSpec((CONV_PAD, ch), lambda i: (jnp.maximum(i * per - 1, 0), cb))]


def _conv_fwd(p, dw, bias, ln_g, ln_b):
    t = p.shape[0]
    ch = D_MODEL
    tm = min(CONV_TILE, t)

    def body(a_ref, g_ref, ah_ref, gh_ref, dw_ref, bias_ref, lg_ref, lb_ref, o_ref, z1_ref, zp_ref):
        i = pl.program_id(0)
        _conv_fill(zp_ref, a_ref, g_ref, ah_ref, gh_ref, i)
        _conv_taps(zp_ref, z1_ref, dw_ref, bias_ref, tm, ch)
        z1 = z1_ref[...]
        mu = jnp.mean(z1, axis=-1, keepdims=True)
        zc = z1 - mu
        rs = lax.rsqrt(jnp.mean(zc * zc, axis=-1, keepdims=True) + EPS)
        z2 = zc * rs * lg_ref[...] + lb_ref[...]
        o_ref[...] = (z2 * _sigmoid(z2)).astype(BF16)

    vec = pl.BlockSpec((1, ch), lambda i: (0, 0))
    return pl.pallas_call(
        body, name="conv_fwd", grid=(t // tm,),
        in_specs=_conv_specs(tm, ch) + [pl.BlockSpec((CONV_PAD, ch), lambda i: (0, 0)), vec, vec, vec],
        out_specs=[pl.BlockSpec((tm, ch), lambda i: (i, 0)), pl.BlockSpec((tm, ch), lambda i: (i, 0))],
        out_shape=[jax.ShapeDtypeStruct((t, ch), BF16), jax.ShapeDtypeStruct((t, ch), F32)],
        scratch_shapes=[pltpu.VMEM((CONV_PAD + tm, ch), F32)],
        compiler_params=_params(("parallel",)),
    )(p, p, p, p, dw, bias, ln_g, ln_b)


def _conv_bwd_ln(p, z1_saved, dz3, ln_g, ln_b):
    t = p.shape[0]
    ch = D_MODEL
    tm = min(CONV_TILE, t)

    def body(a_ref, g_ref, ah_ref, gh_ref, z1_ref, dz3_ref, lg_ref, lb_ref,
             dz1_ref, ddw_ref, dbias_ref, dlg_ref, dlb_ref, zp_ref):
        i = pl.program_id(0)
        _conv_fill(zp_ref, a_ref, g_ref, ah_ref, gh_ref, i)
        z1 = z1_ref[...]
        mu = jnp.mean(z1, axis=-1, keepdims=True)
        zc = z1 - mu
        rs = lax.rsqrt(jnp.mean(zc * zc, axis=-1, keepdims=True) + EPS)
        xh = zc * rs
        z2 = xh * lg_ref[...] + lb_ref[...]
        sig = _sigmoid(z2)
        dz2 = dz3_ref[...].astype(F32) * (sig * (1.0 + z2 * (1.0 - sig)))
        dxh = dz2 * lg_ref[...]
        dz1 = rs * (dxh - jnp.mean(dxh, axis=-1, keepdims=True) - xh * jnp.mean(dxh * xh, axis=-1, keepdims=True))
        dz1_ref[...] = dz1

        @pl.when(i == 0)
        def _():
            ddw_ref[...] = jnp.zeros_like(ddw_ref)
            dbias_ref[...] = jnp.zeros_like(dbias_ref)
            dlg_ref[...] = jnp.zeros_like(dlg_ref)
            dlb_ref[...] = jnp.zeros_like(dlb_ref)

        dlg_ref[...] += jnp.sum(dz2 * xh, axis=0, keepdims=True)
        dlb_ref[...] += jnp.sum(dz2, axis=0, keepdims=True)
        dbias_ref[...] += jnp.sum(dz1, axis=0, keepdims=True)
        groups = _shift_groups(FWD_SHIFTS)
        for cc in range(ch // LANES):
            lanes = pl.ds(cc * LANES, LANES)
            accs = [jnp.zeros((8, LANES), F32) for _ in range(CONV_WIDTH)]
            for r0 in range(0, tm, CONV_ROWS):
                dzc = dz1_ref[pl.ds(r0, CONV_ROWS), lanes]
                for j, rows in _windows(zp_ref, r0, lanes, groups):
                    accs[j] = accs[j] + jnp.sum((dzc * rows).reshape(CONV_ROWS // 8, 8, LANES), axis=0)
            for j in range(CONV_WIDTH):
                ddw_ref[pl.ds(j, 1), lanes] += jnp.sum(accs[j], axis=0, keepdims=True)

    vec = pl.BlockSpec((1, ch), lambda i: (0, 0))
    return pl.pallas_call(
        body, name="conv_bwd_ln", grid=(t // tm,),
        in_specs=_conv_specs(tm, ch) + [pl.BlockSpec((tm, ch), lambda i: (i, 0)),
                                        pl.BlockSpec((tm, ch), lambda i: (i, 0)), vec, vec],
        out_specs=[pl.BlockSpec((tm, ch), lambda i: (i, 0)), pl.BlockSpec((CONV_PAD, ch), lambda i: (0, 0)), vec, vec, vec],
        out_shape=[jax.ShapeDtypeStruct((t, ch), F32), jax.ShapeDtypeStruct((CONV_PAD, ch), F32)]
        + [jax.ShapeDtypeStruct((1, ch), F32)] * 3,
        scratch_shapes=[pltpu.VMEM((CONV_PAD + tm, ch), F32)],
        compiler_params=_params(("arbitrary",)),
    )(p, p, p, p, z1_saved, dz3, ln_g, ln_b)


def _conv_bwd_glu(p, dz1, dw, dq, dkv_parts, dgates):
    t = p.shape[0]
    ch = D_MODEL
    tm = min(CONV_TILE, t)
    per = tm // CONV_PAD
    n_halo = t // CONV_PAD
    cb = COL_CONV_G // ch
    blocks = tm // BLOCK
    nb = t // BLOCK

    def body(a_ref, g_ref, dz_ref, dzn_ref, dw_ref, dq_ref, dkv_ref, dkv_next_ref, dgates_ref, o_ref, zp_ref, z0_ref):
        i = pl.program_id(0)
        o_ref[:, pl.ds(COL_Q, Q_W)] = dq_ref[...]
        for b in range(blocks):
            if b + 1 < blocks:
                following = dkv_ref[b + 1, 0].astype(F32)
            else:
                following = jnp.where(i < t // tm - 1, dkv_next_ref[...].astype(F32), 0.0)
            o_ref[pl.ds(b * BLOCK, BLOCK), pl.ds(COL_K, 2 * KV_W)] = (dkv_ref[b, 1].astype(F32) + following).astype(BF16)
        o_ref[:, pl.ds(COL_GC, ch)] = dgates_ref[0]
        o_ref[:, pl.ds(COL_GA, ch)] = dgates_ref[1]
        zp_ref[pl.ds(0, tm), :] = dz_ref[...]
        zp_ref[pl.ds(tm, CONV_PAD), :] = jnp.where(i < t // tm - 1, dzn_ref[...], 0.0)
        _conv_apply(zp_ref, z0_ref, dw_ref, None, tm, ch, BWD_SHIFTS)
        dz0 = z0_ref[...]
        a = a_ref[...].astype(F32)
        sig = _sigmoid(g_ref[...].astype(F32))
        o_ref[:, pl.ds(0, ch)] = (dz0 * sig).astype(BF16)
        o_ref[:, pl.ds(ch, ch)] = (dz0 * a * sig * (1.0 - sig)).astype(BF16)

    return pl.pallas_call(
        body, name="conv_bwd_glu", grid=(t // tm,),
        in_specs=[pl.BlockSpec((tm, ch), lambda i: (i, 0)), pl.BlockSpec((tm, ch), lambda i: (i, cb)),
                  pl.BlockSpec((tm, ch), lambda i: (i, 0)),
                  pl.BlockSpec((CONV_PAD, ch), lambda i: (jnp.minimum((i + 1) * per, n_halo - 1), 0)),
                  pl.BlockSpec((CONV_PAD, ch), lambda i: (0, 0)),
                  pl.BlockSpec((tm, Q_W), lambda i: (i, 0)),
                  pl.BlockSpec((blocks, 2, BLOCK, 2 * KV_W), lambda i: (i, 0, 0, 0)),
                  pl.BlockSpec((None, None, BLOCK, 2 * KV_W), lambda i: (jnp.minimum((i + 1) * blocks, nb - 1), 0, 0, 0)),
                  pl.BlockSpec((2, tm, ch), lambda i: (0, i, 0))],
        out_specs=pl.BlockSpec((tm, IN_W), lambda i: (i, 0)),
        out_shape=jax.ShapeDtypeStruct((t, IN_W), BF16),
        scratch_shapes=[pltpu.VMEM((tm + CONV_PAD, ch), F32), pltpu.VMEM((tm, ch), F32)],
        compiler_params=_params(("parallel",)),
    )(p, p, dz1, dz1, dw, dq, dkv_parts, dkv_parts, dgates)


def _bucket_onehot():
    qi = jnp.arange(BLOCK, dtype=jnp.int32)[:, None]
    kj = jnp.arange(2 * BLOCK, dtype=jnp.int32)[None, :]
    dist = jnp.maximum(qi + BLOCK - kj, 0)
    max_exact = N_BUCKETS // 2
    dflt = jnp.maximum(dist, 1).astype(F32)
    large = max_exact + (jnp.log(dflt / max_exact) / math.log(MAX_DISTANCE / max_exact)
                         * (N_BUCKETS - max_exact)).astype(jnp.int32)
    large = jnp.minimum(large, N_BUCKETS - 1)
    bucket = jnp.where(dist < max_exact, dist, large)
    onehot = bucket[None] == jnp.arange(N_BUCKETS, dtype=jnp.int32)[:, None, None]
    return onehot.astype(F32).reshape(N_BUCKETS, BLOCK * 2 * BLOCK)


def _bias_table(rel_bias_t, onehot):
    n = onehot.shape[1]
    tn = 4096

    def body(r_ref, oh_ref, o_ref):
        flat = pl.program_id(0) * tn + lax.broadcasted_iota(jnp.int32, (N_Q_HEADS, tn), 1)
        dist = (flat // (2 * BLOCK)) + BLOCK - (flat % (2 * BLOCK))
        bias = _dot(r_ref[...], oh_ref[...], precision=lax.Precision.HIGHEST)
        o_ref[...] = jnp.where((dist >= 0) & (dist < BLOCK), bias, NEG)

    return pl.pallas_call(
        body, name="bias_table", grid=(n // tn,),
        in_specs=[pl.BlockSpec((N_Q_HEADS, N_BUCKETS), lambda i: (0, 0)), pl.BlockSpec((N_BUCKETS, tn), lambda i: (0, i))],
        out_specs=pl.BlockSpec((N_Q_HEADS, tn), lambda i: (0, i)),
        out_shape=jax.ShapeDtypeStruct((N_Q_HEADS, n), F32),
        compiler_params=_params(("parallel",)),
    )(rel_bias_t, onehot)


def _bias_table_bwd(dbias, onehot):
    n = onehot.shape[1]
    tn = 4096

    def body(d_ref, oh_ref, o_ref):
        part = _dot(d_ref[...], oh_ref[...], trans_b=True, precision=lax.Precision.HIGHEST)
        i = pl.program_id(0)

        @pl.when(i == 0)
        def _():
            o_ref[...] = part

        @pl.when(i > 0)
        def _():
            o_ref[...] += part

    return pl.pallas_call(
        body, name="bias_table_bwd", grid=(n // tn,),
        in_specs=[pl.BlockSpec((N_Q_HEADS, tn), lambda i: (0, i)), pl.BlockSpec((N_BUCKETS, tn), lambda i: (0, i))],
        out_specs=pl.BlockSpec((N_Q_HEADS, N_BUCKETS), lambda i: (0, 0)),
        out_shape=jax.ShapeDtypeStruct((N_Q_HEADS, N_BUCKETS), F32),
        compiler_params=_params(("arbitrary",)),
    )(dbias, onehot)


def _lane_head(rows):
    return lax.broadcasted_iota(jnp.int32, (rows, KV_W), 1) // HEAD_DIM


def _group_rms(x, gain_wide):
    head = _lane_head(x.shape[0])
    sq = x * x
    r = jnp.zeros_like(x)
    for i in range(N_KV_HEADS):
        ms = jnp.sum(jnp.where(head == i, sq, 0.0), axis=-1, keepdims=True) * (1.0 / HEAD_DIM)
        r = jnp.where(head == i, lax.rsqrt(ms + EPS), r)
    return r, x * r * gain_wide


def _stack_heads(group):
    head = _lane_head(group.shape[0])
    return jnp.concatenate([jnp.where(head == i, group, jnp.zeros_like(group)) for i in range(N_KV_HEADS)], axis=0)


def _unstack_heads(stacked):
    head = _lane_head(BLOCK)
    out = jnp.where(head == 0, stacked[:BLOCK], 0.0)
    for i in range(1, N_KV_HEADS):
        out = out + jnp.where(head == i, stacked[i * BLOCK:(i + 1) * BLOCK], 0.0)
    return out


def _repeaters():
    row = lax.broadcasted_iota(jnp.int32, (KV_W, KV_W), 0)
    col = lax.broadcasted_iota(jnp.int32, (KV_W, KV_W), 1)
    return [(row == h * HEAD_DIM + col % HEAD_DIM).astype(BF16) for h in range(N_KV_HEADS)]


def _attn_probs(q_stack, k_rep, sink, bias, before_start):
    s = _dot(q_stack, k_rep, trans_b=True) * (1.0 / math.sqrt(HEAD_DIM)) + bias
    s = jnp.where(before_start, NEG, s)
    m = jnp.maximum(jnp.max(s, axis=-1, keepdims=True), sink)
    p = jnp.exp(s - m)
    es = jnp.exp(sink - m)
    inv = 1.0 / (jnp.sum(p, axis=-1, keepdims=True) + es)
    return p * inv, es * inv


def _before_start(n):
    col = lax.broadcasted_iota(jnp.int32, (QROWS, 2 * BLOCK), 1)
    return (col < BLOCK) & (n == 0)


STEP_BLOCKS = 4
KV_W = N_KV_HEADS * HEAD_DIM
Q_W = N_Q_HEADS * HEAD_DIM


def _attn_specs():
    gain = pl.BlockSpec((1, KV_W), lambda n: (0, 0))
    sink = pl.BlockSpec((N_KV_HEADS, QROWS, 1), lambda n: (0, 0, 0))
    bias = pl.BlockSpec((N_KV_HEADS, QROWS, 2 * BLOCK), lambda n: (0, 0, 0))
    return gain, sink, bias


def _attn_fwd(p, gq, gk, sink_rows, bias):
    t = p.shape[0]
    nb = t // BLOCK
    per = STEP_BLOCKS if nb % STEP_BLOCKS == 0 else 1
    gain, sink, bspec = _attn_specs()

    def body(q_ref, kp_ref, kc_ref, vp_ref, vc_ref, gq_ref, gk_ref, sink_ref, bias_ref, o_ref, p_ref, ps_ref):
        first = pl.program_id(0) * per
        rep = _repeaters()
        kf = jnp.concatenate([kp_ref[...], kc_ref[...]], axis=0).astype(F32)
        kn = _group_rms(kf, gk_ref[...])[1].astype(BF16)
        v = jnp.concatenate([vp_ref[...], vc_ref[...]], axis=0)
        for h in range(N_KV_HEADS):
            k_rep = _dot(kn, rep[h]).astype(BF16)
            v_rep = _dot(v, rep[h]).astype(BF16)
            for sub in range(per):
                rows = pl.ds(sub * BLOCK, BLOCK)
                window = slice(sub * BLOCK, (sub + 2) * BLOCK)
                qn = _group_rms(q_ref[rows, pl.ds(h * KV_W, KV_W)].astype(F32), gq_ref[...])[1]
                pn, ps_ref[sub, h] = _attn_probs(_stack_heads(qn).astype(BF16), k_rep[window], sink_ref[h], bias_ref[h],
                                                 _before_start(first + sub))
                pn = pn.astype(BF16)
                p_ref[sub, h] = pn
                o_ref[rows, pl.ds(h * KV_W, KV_W)] = _unstack_heads(_dot(pn, v_rep[window])).astype(BF16)

    def kv_specs(col):
        return [pl.BlockSpec((BLOCK, KV_W), lambda n: (jnp.maximum(n * per - 1, 0), col // KV_W)),
                pl.BlockSpec((per * BLOCK, KV_W), lambda n: (n, col // KV_W))]

    return pl.pallas_call(
        body, name="attn_fwd", grid=(nb // per,),
        in_specs=[pl.BlockSpec((per * BLOCK, Q_W), lambda n: (n, COL_Q // Q_W))] + kv_specs(COL_K) + kv_specs(COL_V)
        + [gain, gain, sink, bspec],
        out_specs=[pl.BlockSpec((per * BLOCK, Q_W), lambda n: (n, 0)),
                   pl.BlockSpec((per, N_KV_HEADS, QROWS, 2 * BLOCK), lambda n: (n, 0, 0, 0)),
                   pl.BlockSpec((per, N_KV_HEADS, QROWS, 1), lambda n: (n, 0, 0, 0))],
        out_shape=[jax.ShapeDtypeStruct((t, Q_W), BF16),
                   jax.ShapeDtypeStruct((nb, N_KV_HEADS, QROWS, 2 * BLOCK), BF16),
                   jax.ShapeDtypeStruct((nb, N_KV_HEADS, QROWS, 1), F32)],
        compiler_params=_params(("parallel",)),
    )(p, p, p, p, p, gq, gk, sink_rows, bias)


def _attn_bwd(p, do, gq, gk, probs, sink_probs):
    t = p.shape[0]
    nb = t // BLOCK
    per = STEP_BLOCKS if nb % STEP_BLOCKS == 0 else 1
    bspec = _attn_specs()[2]
    gain = pl.BlockSpec((1, HEAD_DIM), lambda n: (0, 0))
    scale = 1.0 / math.sqrt(HEAD_DIM)

    def head_selectors():
        row = lax.broadcasted_iota(jnp.int32, (KV_W, HEAD_DIM), 0)
        col = lax.broadcasted_iota(jnp.int32, (KV_W, HEAD_DIM), 1)
        return [(row == col + i * HEAD_DIM).astype(BF16) for i in range(N_KV_HEADS)]

    def take_heads(group, sel):
        return jnp.concatenate([_dot(group, s) for s in sel], axis=0)

    def put_heads(x, sel):
        rows = x.shape[0] // len(sel)
        out = _dot(x[:rows].astype(BF16), sel[0], trans_b=True)
        for i in range(1, len(sel)):
            out = out + _dot(x[i * rows:(i + 1) * rows].astype(BF16), sel[i], trans_b=True)
        return out

    def rms(x, g):
        r = lax.rsqrt(jnp.mean(x * x, axis=-1, keepdims=True) + EPS)
        return r, x * r * g

    def rms_bwd(dn, xf, r, g):
        w = dn * g
        dx = r * w - xf * (r * r * r) * jnp.mean(xf * w, axis=-1, keepdims=True)
        return dx, jnp.sum(dn * (xf * r), axis=0, keepdims=True)

    def body(q_ref, kp_ref, kc_ref, vp_ref, vc_ref, do_ref, gq_ref, gk_ref, p_ref, ps_ref,
             dq_ref, dkv_ref, dbias_ref, dsink_ref, dgq_ref, dgk_ref):
        n = pl.program_id(0)
        sel = head_selectors()

        @pl.when(n == 0)
        def _():
            dbias_ref[...] = jnp.zeros_like(dbias_ref)
            dsink_ref[...] = jnp.zeros_like(dsink_ref)
            dgq_ref[...] = jnp.zeros_like(dgq_ref)
            dgk_ref[...] = jnp.zeros_like(dgk_ref)

        dgq_sum = jnp.zeros((1, HEAD_DIM), F32)
        dgk_sum = jnp.zeros((1, HEAD_DIM), F32)
        dk_rows, dv_rows = [[] for _ in range(per)], [[] for _ in range(per)]
        for h in range(N_KV_HEADS):
            kf_all = jnp.concatenate([_dot(kp_ref[...], sel[h]), _dot(kc_ref[...], sel[h])], axis=0)
            rk_all, kn_all = rms(kf_all, gk_ref[...])
            kn_all = kn_all.astype(BF16)
            v_all = jnp.concatenate([_dot(vp_ref[...], sel[h]), _dot(vc_ref[...], sel[h])], axis=0).astype(BF16)
            for sub in range(per):
                rows = pl.ds(sub * BLOCK, BLOCK)
                window = slice(sub * BLOCK, (sub + 2) * BLOCK)
                qf = take_heads(q_ref[rows, pl.ds(h * KV_W, KV_W)], sel)
                rq, qn = rms(qf, gq_ref[...])
                pn_bf16, psink = p_ref[sub, h], ps_ref[sub, h]
                pn = pn_bf16.astype(F32)
                do = take_heads(do_ref[rows, pl.ds(h * KV_W, KV_W)], sel).astype(BF16)
                dv_win = _dot(do, pn_bf16, trans_a=True).T
                dp = _dot(do, v_all[window], trans_b=True)
                delta = jnp.sum(pn * dp, axis=-1, keepdims=True)
                ds = pn * (dp - delta)
                dsc = (ds * scale).astype(BF16)
                dqn = _dot(dsc, kn_all[window])
                dkn = _dot(qn.astype(BF16), dsc, trans_a=True).T
                dq, dgq = rms_bwd(dqn, qf, rq, gq_ref[...])
                dk_win, dgk = rms_bwd(dkn, kf_all[window], rk_all[window], gk_ref[...])
                dq_ref[rows, pl.ds(h * KV_W, KV_W)] = put_heads(dq, sel).astype(BF16)
                dk_rows[sub] += [dk_win[:BLOCK], dk_win[BLOCK:]]
                dv_rows[sub] += [dv_win[:BLOCK], dv_win[BLOCK:]]
                dbias_ref[h] += ds
                dsink_ref[h] += jnp.sum((-psink * delta).reshape(GROUP, BLOCK, 1), axis=1)
                dgq_sum = dgq_sum + dgq
                dgk_sum = dgk_sum + dgk
        for sub in range(per):
            for part in range(2):
                dkv_ref[sub, part, :, pl.ds(0, KV_W)] = put_heads(
                    jnp.concatenate(dk_rows[sub][part::2], axis=0), sel).astype(BF16)
                dkv_ref[sub, part, :, pl.ds(KV_W, KV_W)] = put_heads(
                    jnp.concatenate(dv_rows[sub][part::2], axis=0), sel).astype(BF16)
        dgq_ref[...] += dgq_sum
        dgk_ref[...] += dgk_sum

    def kv_specs(col):
        return [pl.BlockSpec((BLOCK, KV_W), lambda n: (jnp.maximum(n * per - 1, 0), col // KV_W)),
                pl.BlockSpec((per * BLOCK, KV_W), lambda n: (n, col // KV_W))]

    row = pl.BlockSpec((per * BLOCK, Q_W), lambda n: (n, 0))
    return pl.pallas_call(
        body, name="attn_bwd", grid=(nb // per,),
        in_specs=[pl.BlockSpec((per * BLOCK, Q_W), lambda n: (n, COL_Q // Q_W))] + kv_specs(COL_K) + kv_specs(COL_V)
        + [row, gain, gain,
           pl.BlockSpec((per, N_KV_HEADS, QROWS, 2 * BLOCK), lambda n: (n, 0, 0, 0)),
           pl.BlockSpec((per, N_KV_HEADS, QROWS, 1), lambda n: (n, 0, 0, 0))],
        out_specs=[row, pl.BlockSpec((per, 2, BLOCK, 2 * KV_W), lambda n: (n, 0, 0, 0)), bspec,
                   pl.BlockSpec((N_KV_HEADS, GROUP, 1), lambda n: (0, 0, 0)), gain, gain],
        out_shape=[jax.ShapeDtypeStruct((t, Q_W), BF16),
                   jax.ShapeDtypeStruct((nb, 2, BLOCK, 2 * KV_W), BF16),
                   jax.ShapeDtypeStruct((N_KV_HEADS, QROWS, 2 * BLOCK), F32),
                   jax.ShapeDtypeStruct((N_KV_HEADS, GROUP, 1), F32),
                   jax.ShapeDtypeStruct((1, HEAD_DIM), F32),
                   jax.ShapeDtypeStruct((1, HEAD_DIM), F32)],
        compiler_params=_params(("arbitrary",)),
    )(p, p, p, p, p, do, gq, gk, probs, sink_probs)


GATE_TILE = 512


def _merge_fwd(z3, o, p, w_proj, w_o):
    t, d = z3.shape
    tm = min(ROW_TILE, t)
    tn = GATE_TILE

    def body(z_ref, o_ref, gc_ref, ga_ref, wp_ref, wo_ref, m_ref, a_ref, b_ref):
        a = _dot(z_ref[...], wp_ref[...])
        b = _dot(o_ref[...], wo_ref[...])
        m_ref[...] = (_sigmoid(gc_ref[...].astype(F32)) * a + _sigmoid(ga_ref[...].astype(F32)) * b).astype(BF16)
        a_ref[...] = a.astype(BF16)
        b_ref[...] = b.astype(BF16)

    row = pl.BlockSpec((tm, d), lambda i, j: (i, 0))
    wspec = pl.BlockSpec((d, tn), lambda i, j: (0, j))
    ospec = pl.BlockSpec((tm, tn), lambda i, j: (i, j))
    return pl.pallas_call(
        body, name="merge_fwd", grid=(t // tm, d // tn),
        in_specs=[row, row,
                  pl.BlockSpec((tm, tn), lambda i, j: (i, COL_GC // tn + j)),
                  pl.BlockSpec((tm, tn), lambda i, j: (i, COL_GA // tn + j)), wspec, wspec],
        out_specs=[ospec, ospec, ospec],
        out_shape=[jax.ShapeDtypeStruct((t, d), BF16)] * 3,
        compiler_params=_params(("parallel", "parallel")),
    )(z3, o, p, p, w_proj, w_o)


def _merge_bwd(dres, w_out, a, b, p, tokens=()):
    t, d = dres.shape
    tm = min(ROW_TILE, t)
    tn = GATE_TILE

    def epilogue(acc, ex, outs, ids):
        a_ref, b_ref, gc_ref, ga_ref = ex[:4]
        sc = _sigmoid(gc_ref[...].astype(F32))
        sa = _sigmoid(ga_ref[...].astype(F32))
        outs[0][...] = (acc * sc).astype(BF16)
        outs[1][...] = (acc * sa).astype(BF16)
        outs[2][0] = (acc * a_ref[...].astype(F32) * sc * (1.0 - sc)).astype(BF16)
        outs[2][1] = (acc * b_ref[...].astype(F32) * sa * (1.0 - sa)).astype(BF16)

    ospec = pl.BlockSpec((tm, tn), lambda i, j, kk: (i, j))
    return _mm("merge_bwd", (t // tm, d // tn, 1),
               dres, pl.BlockSpec((tm, d), lambda i, j, kk: (i, 0)),
               w_out, pl.BlockSpec((tn, d), lambda i, j, kk: (j, 0)), (tm, tn),
               trans_b=True, a_pre=_to_bf16,
               extras=(a, b, p, p),
               extra_specs=(ospec, ospec,
                            pl.BlockSpec((tm, tn), lambda i, j, kk: (i, COL_GC // tn + j)),
                            pl.BlockSpec((tm, tn), lambda i, j, kk: (i, COL_GA // tn + j))), tokens=tokens,
               out_shape=(jax.ShapeDtypeStruct((t, d), BF16), jax.ShapeDtypeStruct((t, d), BF16),
                          jax.ShapeDtypeStruct((2, t, d), BF16)),
               out_specs=(ospec, ospec, pl.BlockSpec((2, tm, tn), lambda i, j, kk: (0, i, j))),
               epilogue=epilogue)


def _store_epilogue(acc, ex, outs, ids):
    outs[0][...] = acc


def _store_bf16_epilogue(acc, ex, outs, ids):
    outs[0][...] = acc.astype(BF16)


def _mm_nt(name, a, w, out_dtype=BF16):
    t, n = a.shape
    k = w.shape[0]
    tm = min(ROW_TILE, t)
    return _mm(name, (t // tm, 1, 1), a, pl.BlockSpec((tm, n), lambda i, j, kk: (i, 0)),
               w, pl.BlockSpec((k, n), lambda i, j, kk: (0, 0)), (tm, k), trans_b=True,
               out_shape=(jax.ShapeDtypeStruct((t, k), out_dtype),),
               out_specs=(pl.BlockSpec((tm, k), lambda i, j, kk: (i, 0)),),
               epilogue=_store_bf16_epilogue if out_dtype == BF16 else _store_epilogue)[0]


def _mm_tn(name, a, b, b_pre=None, tokens=()):
    t, m = a.shape
    n = b.shape[1]
    tk = min(TOKEN_TILE // (2 if b.dtype == F32 else 1), t)
    return _mm(name, (1, 1, t // tk), a, pl.BlockSpec((tk, m), lambda i, j, kk: (kk, 0)),
               b, pl.BlockSpec((tk, n), lambda i, j, kk: (kk, 0)), (m, n), trans_a=True, b_pre=b_pre, tokens=tokens,
               out_shape=(jax.ShapeDtypeStruct((m, n), BF16),),
               out_specs=(pl.BlockSpec((m, n), lambda i, j, kk: (0, 0)),), epilogue=_store_bf16_epilogue)[0]


def _local_step(x, target, small, comm):
    t = x.shape[0]
    w = dict(small)

    n1 = _rmsnorm_fwd("ffn1_norm", x, w["ffn1_norm"])
    onehot = _bucket_onehot()
    bias = _bias_table(w["rel_bias"].T, onehot).reshape(N_KV_HEADS, QROWS, 2 * BLOCK)
    sink_rows = jnp.repeat(w["attn_sinks"].reshape(N_KV_HEADS, GROUP), BLOCK, axis=1)[..., None]
    gq_wide = jnp.tile(w["q_norm"], (1, N_KV_HEADS))
    gk_wide = jnp.tile(w["k_norm"], (1, N_KV_HEADS))
    w.update(comm.weights("A", [n1, onehot, bias, sink_rows, gq_wide, gk_wide]))
    (x1, hm), ffn1_saved = _ffn_fwd("ffn1", x, n1, w["ffn1_w_in"], w["ffn1_w_out"], comm.tokens,
                                    next_gain=w["mix_norm"])
    w.update(comm.weights("B", x1))
    tm = min(ROW_TILE, t)
    p = _mm("mix_in", (N_CHIPS, t // tm, 1),
            hm, pl.BlockSpec((tm, D_MODEL), lambda j, i, kk: (i, 0)),
            w["w_in"], pl.BlockSpec((None, D_MODEL, SHARD_W), lambda j, i, kk: (j, 0, 0)), (tm, SHARD_W),
            tokens=comm.tokens,
            out_shape=(jax.ShapeDtypeStruct((t, IN_W), BF16),),
            out_specs=(pl.BlockSpec((tm, SHARD_W), lambda j, i, kk: (i, j)),),
            epilogue=_store_bf16_epilogue)[0]

    z3, z1 = _conv_fwd(p, w["conv_dw_kernel"], w["conv_dw_bias"], w["conv_ln_g"], w["conv_ln_b"])

    o, probs, sink_probs = _attn_fwd(p, gq_wide, gk_wide, sink_rows, bias)

    merged, a, b = _merge_fwd(z3, o, p, w["conv_w_proj"], w["attn_w_o"])
    x2, n2 = _mm_residual("mix_out", merged, w["w_out"], x1, 1.0, next_gain=w["ffn2_norm"])
    w.update(comm.weights("C", n2))
    (dy, loss), ffn2_saved = _ffn_fwd("ffn2", x2, n2, w["ffn2_w_in"], w["ffn2_w_out"], loss_target=target)

    g, big = {}, {}
    dres2, big["ffn2_w_in"], big["ffn2_w_out"], g["ffn2_norm"] = _ffn_bwd(
        "ffn2b", dy, x2, w["ffn2_norm"], ffn2_saved, w["ffn2_w_in"], w["ffn2_w_out"])
    tokens = comm.reduce_start("R1", big, behind=True)

    da, db, dgates = _merge_bwd(dres2, w["w_out"], a, b, p, tokens)
    tokens = comm.exchange_finish("R1", da)
    big = {}
    big["w_out"] = _mm_tn("d_w_out", merged, dres2, b_pre=_to_bf16, tokens=tokens)
    big["conv_w_proj"] = _mm_tn("d_w_proj", z3, da)
    big["attn_w_o"] = _mm_tn("d_w_o", o, db)
    dz3 = _mm_nt("d_z3", da, w["conv_w_proj"])
    do = _mm_nt("d_o", db, w["attn_w_o"])

    dq, dkv_parts, dbias, dsink, g["q_norm"], g["k_norm"] = _attn_bwd(
        p, do, w["q_norm"], w["k_norm"], probs, sink_probs)
    g["rel_bias"] = _bias_table_bwd(dbias.reshape(N_Q_HEADS, BLOCK * 2 * BLOCK), onehot).T
    g["attn_sinks"] = dsink.reshape(N_Q_HEADS)

    dz1, big["conv_dw_kernel"], g["conv_dw_bias"], g["conv_ln_g"], g["conv_ln_b"] = _conv_bwd_ln(
        p, z1, dz3, w["conv_ln_g"], w["conv_ln_b"])
    dp = _conv_bwd_glu(p, dz1, w["conv_dw_kernel"], dq, dkv_parts, dgates)
    tk = min(TOKEN_TILE, t)
    big["w_in"] = _mm("d_w_in", (1, N_CHIPS, t // tk),
                    hm, pl.BlockSpec((tk, D_MODEL), lambda i, j, kk: (kk, 0)),
                    dp, pl.BlockSpec((tk, SHARD_W), lambda i, j, kk: (kk, j)), (D_MODEL, SHARD_W),
                    trans_a=True,
                    out_shape=(jax.ShapeDtypeStruct((N_CHIPS, D_MODEL, SHARD_W), BF16),),
                    out_specs=(pl.BlockSpec((None, D_MODEL, SHARD_W), lambda i, j, kk: (j, 0, 0)),),
                    epilogue=_store_bf16_epilogue)[0]
    tn = min(NORM_GRAD_TILE, t)
    dres1, g["mix_norm"] = _norm_input_grad(
        "d_mix", dp, [pl.BlockSpec((tn, SHARD_W), functools.partial(lambda i, s: (i, s), s=s)) for s in range(N_CHIPS)],
        w["w_in"], x1, w["mix_norm"], dres2)

    tokens = comm.reduce_finish("R1", dres1, behind=True) + comm.reduce_start("R2", big, behind=True)

    def ffn1_first(du):
        comm.join_finish("R1", du)
        return comm.exchange_finish("R2", du)

    def ffn1_grads(dw_in4, dw_out):
        late = comm.reduce_finish("R2", dw_in4, behind=True)
        return late + comm.reduce_start("R3", {"ffn1_w_in": dw_in4, "ffn1_w_out": dw_out})

    grad_x, _, _, g["ffn1_norm"] = _ffn_bwd(
        "ffn1b", dres1, x, w["ffn1_norm"], ffn1_saved, w["ffn1_w_in"], w["ffn1_w_out"], tokens, ffn1_first, ffn1_grads)
    comm.join_finish("R2", grad_x)
    comm.reduce_finish("R3", grad_x)
    return loss[0, 0], grad_x, g


def _mesh_place():
    x, y, c = lax.axis_index("x"), lax.axis_index("y"), lax.axis_index("c")
    chips = [(1 - x, y), (x, 1 - y), (1 - x, 1 - y)]
    return x, y, c, chips


def _any_specs(n):
    return [pl.BlockSpec(memory_space=pl.ANY)] * n


HBM_SPEC = pl.BlockSpec(memory_space=pltpu.HBM)
SEM_SPEC = pl.BlockSpec(memory_space=pltpu.SEMAPHORE)
EFFECT = pltpu.SideEffectType.DATAFLOW_SIDE_EFFECTING


def _in_hbm(a):
    return pltpu.with_memory_space_constraint(a, pltpu.HBM)


def _copy_start(name, srcs, lands, plan, after=()):
    ns, nb = len(srcs), len(lands)
    n = plan.copies_per_source * ns

    def body(*refs):
        s_refs, l_refs = refs[:ns], refs[ns:ns + nb]
        send_sems, recv_sems = refs[ns + nb + len(after)], refs[ns + nb + len(after) + 1]
        token = refs[-1]
        for k, (src, dst, to, _) in enumerate(plan(s_refs, l_refs)):
            pltpu.make_async_remote_copy(src_ref=src, dst_ref=dst, send_sem=send_sems.at[k], recv_sem=recv_sems.at[k],
                                         device_id=to, device_id_type=MESH).start()
        token[...] = jnp.zeros_like(token)

    bufs = list(srcs) + list(lands)
    outs = pl.pallas_call(
        body, name=name,
        out_shape=(pltpu.SemaphoreType.DMA((n,)), pltpu.SemaphoreType.DMA((n,)),
                   *[pltpu.HBM(a.shape, a.dtype) for a in bufs], jax.ShapeDtypeStruct((8, LANES), F32)),
        in_specs=[HBM_SPEC] * len(bufs) + [pl.BlockSpec(memory_space=pl.ANY)] * len(after),
        out_specs=(SEM_SPEC, SEM_SPEC, *[HBM_SPEC] * len(bufs), pl.BlockSpec(memory_space=pltpu.VMEM)),
        input_output_aliases={i: 2 + i for i in range(len(bufs))},
        compiler_params=pltpu.CompilerParams(has_side_effects=EFFECT),
    )(*[_in_hbm(a) for a in bufs], *after)
    return outs[0], outs[1], list(outs[2:2 + ns]), list(outs[2 + ns:2 + ns + nb]), outs[-1]


def _copy_wait(name, send_sems, recv_sems, srcs, lands, after, plan):
    ns, nb = len(srcs), len(lands)
    after = tuple(after) if isinstance(after, (tuple, list)) else (after,)

    def body(*refs):
        s_refs, l_refs = refs[:ns], refs[ns:ns + nb]
        send_sems, recv_sems = refs[ns + nb], refs[ns + nb + 1]
        for k, (src, _, to, mine) in enumerate(plan(s_refs, l_refs)):
            cp = pltpu.make_async_remote_copy(src_ref=src, dst_ref=mine, send_sem=send_sems.at[k], recv_sem=recv_sems.at[k],
                                              device_id=to, device_id_type=MESH)
            cp.wait_send()
            cp.wait_recv()

    bufs = list(srcs) + list(lands)
    outs = pl.pallas_call(
        body, name=name,
        out_shape=tuple(pltpu.HBM(a.shape, a.dtype) for a in bufs),
        in_specs=[HBM_SPEC] * len(bufs) + [SEM_SPEC, SEM_SPEC] + [pl.BlockSpec(memory_space=pl.ANY)] * len(after),
        out_specs=tuple([HBM_SPEC] * len(bufs)),
        input_output_aliases={i: i for i in range(len(bufs))},
        compiler_params=pltpu.CompilerParams(has_side_effects=EFFECT),
    )(*bufs, send_sems, recv_sems, *after)
    return list(outs[:ns]), list(outs[ns:])


def _gather_plan(s_refs, l_refs):
    x, y, c, chips = _mesh_place()
    jme = 2 * x + y
    return [(s.at[c], land.at[jme, c], (*chip, c), land.at[2 * chip[0] + chip[1], c])
            for s, land in zip(s_refs, l_refs) for chip in chips]


_gather_plan.copies_per_source = 3


def _gather_both_cores_plan(s_refs, l_refs):
    x, y, c, chips = _mesh_place()
    jme = 2 * x + y
    plan = []
    for s, land in zip(s_refs, l_refs):
        for chip in chips:
            for peer_core in (c, 1 - c):
                plan.append((s.at[c], land.at[jme, c], (*chip, peer_core), land.at[2 * chip[0] + chip[1], peer_core]))
        plan.append((s, land.at[jme], (x, y, 1 - c), land.at[jme]))
    return plan


_gather_both_cores_plan.copies_per_source = 7


def _scatter_plan(s_refs, l_refs):
    x, y, c, chips = _mesh_place()
    return [(s.at[2 * chip[0] + chip[1]], land.at[k], (*chip, c), land.at[k])
            for s, land in zip(s_refs, l_refs) for k, chip in enumerate(chips)]


_scatter_plan.copies_per_source = 3


def _exchange_plan(g_refs, l_refs):
    x, y, c, _ = _mesh_place()
    return [(g.at[:, 1 - c], land, (x, y, 1 - c), land) for g, land in zip(g_refs, l_refs)]


_exchange_plan.copies_per_source = 1


def _join_plan(h_refs, l_refs):
    x, y, c, _ = _mesh_place()
    return [(h.at[c], h.at[c], (x, y, 1 - c), h.at[1 - c]) for h in h_refs]


_join_plan.copies_per_source = 1


def _gather_forward(name, shards, landed):
    nw = len(shards)

    def body(*refs):
        s_refs, o_refs = refs[:nw], refs[2 * nw:3 * nw]
        send_sems, recv_sems = refs[3 * nw:]
        x, y, c, chips = _mesh_place()
        me, sib, jme = (x, y, c), (x, y, 1 - c), 2 * x + y
        sent = []
        for w in range(nw):
            parts = [(o_refs[w].at[2 * chip[0] + chip[1], c], o_refs[w].at[2 * chip[0] + chip[1], c]) for chip in chips]
            parts.append((s_refs[w], o_refs[w].at[jme]))
            for k, (src, dst) in enumerate(parts):
                cp = pltpu.make_async_remote_copy(src_ref=src, dst_ref=dst, send_sem=send_sems.at[4 * w + k],
                                                  recv_sem=recv_sems.at[4 * w + k], device_id=sib, device_id_type=MESH)
                cp.start()
                sent.append(cp)
        for w in range(nw):
            parts = [o_refs[w].at[2 * chip[0] + chip[1], 1 - c] for chip in chips] + [o_refs[w].at[jme]]
            for k, part in enumerate(parts):
                pltpu.make_async_remote_copy(src_ref=part, dst_ref=part, send_sem=send_sems.at[4 * w + k],
                                             recv_sem=recv_sems.at[4 * w + k], device_id=me, device_id_type=MESH).wait_recv()
        for cp in sent:
            cp.wait_send()

    return pl.pallas_call(
        body, name=name,
        in_specs=_any_specs(2 * nw), out_specs=_any_specs(nw),
        out_shape=[jax.ShapeDtypeStruct(a.shape, a.dtype) for a in landed],
        input_output_aliases={nw + i: i for i in range(nw)},
        scratch_shapes=[pltpu.SemaphoreType.DMA((4 * nw,)), pltpu.SemaphoreType.DMA((4 * nw,))],
    )(*shards, *landed)


def _exchange_halves(name, grads, after=()):
    nw = len(grads)

    def body(*refs):
        g_refs, o_refs = refs[:nw], refs[nw + len(after):2 * nw + len(after)]
        send_sems, recv_sems = refs[2 * nw + len(after):]
        x, y, c, _ = _mesh_place()
        copies = []
        for w in range(nw):
            cp = pltpu.make_async_remote_copy(src_ref=g_refs[w].at[:, 1 - c], dst_ref=o_refs[w], send_sem=send_sems.at[w],
                                              recv_sem=recv_sems.at[w], device_id=(x, y, 1 - c), device_id_type=MESH)
            cp.start()
            copies.append(cp)
        for cp in copies:
            cp.wait()

    return pl.pallas_call(
        body, name=name,
        in_specs=_any_specs(nw + len(after)), out_specs=_any_specs(nw),
        out_shape=[jax.ShapeDtypeStruct((N_CHIPS,) + g.shape[2:], g.dtype) for g in grads],
        scratch_shapes=[pltpu.SemaphoreType.DMA((nw,)), pltpu.SemaphoreType.DMA((nw,))],
    )(*grads, *after)


def _add_own_halves(name, c_idx, grads, gots):
    n = len(grads)

    def body(c_ref, *refs):
        for g_ref, o_ref, out_ref in zip(refs[:n], refs[n:2 * n], refs[2 * n:]):
            out_ref[...] = (g_ref[...].astype(F32) + o_ref[...].astype(F32)).astype(BF16)

    def whole(a):
        return pl.BlockSpec((None,) + a.shape[1:], lambda j, c_ref: (j, 0, 0))

    return pl.pallas_call(
        body, name=name,
        grid_spec=pltpu.PrefetchScalarGridSpec(
            num_scalar_prefetch=1, grid=(N_CHIPS,),
            in_specs=[pl.BlockSpec((None, None) + g.shape[2:], lambda j, c_ref: (j, c_ref[0], 0, 0)) for g in grads]
            + [whole(o) for o in gots],
            out_specs=[whole(o) for o in gots]),
        out_shape=[jax.ShapeDtypeStruct(o.shape, BF16) for o in gots],
        compiler_params=_params(("parallel",)),
    )(c_idx, *grads, *gots)


def _sum_pieces(name, place_idx, sums, landed):
    n = len(sums)

    def body(j_ref, *refs):
        for own_ref, p_ref, o_ref in zip(refs[:n], refs[n:2 * n], refs[2 * n:]):
            o_ref[...] = ((own_ref[...].astype(F32) + p_ref[0].astype(F32)) + p_ref[1].astype(F32)) + p_ref[2].astype(F32)

    return pl.pallas_call(
        body, name=name,
        grid_spec=pltpu.PrefetchScalarGridSpec(
            num_scalar_prefetch=1, grid=(1,),
            in_specs=[pl.BlockSpec((None,) + s.shape[1:], lambda i, j_ref: (j_ref[0], 0, 0)) for s in sums]
            + [pl.BlockSpec(p.shape, lambda i, j_ref: (0, 0, 0)) for p in landed],
            out_specs=[pl.BlockSpec((None,) + s.shape[1:], lambda i, j_ref: (j_ref[1], 0, 0)) for s in sums]),
        out_shape=[jax.ShapeDtypeStruct((2,) + s.shape[1:], F32) for s in sums],
        compiler_params=_params(("arbitrary",)),
    )(place_idx, *sums, *landed)


def _join_halves(name, halves):
    nw = len(halves)

    def body(*refs):
        o_refs = refs[nw:2 * nw]
        send_sems, recv_sems = refs[2 * nw:]
        x, y, c, _ = _mesh_place()
        copies = []
        for w in range(nw):
            cp = pltpu.make_async_remote_copy(src_ref=o_refs[w].at[c], dst_ref=o_refs[w].at[c], send_sem=send_sems.at[w],
                                              recv_sem=recv_sems.at[w], device_id=(x, y, 1 - c), device_id_type=MESH)
            cp.start()
            copies.append(cp)
        for w in range(nw):
            copies[w].wait_send()
            landed = o_refs[w].at[1 - c]
            pltpu.make_async_remote_copy(src_ref=landed, dst_ref=landed, send_sem=send_sems.at[w], recv_sem=recv_sems.at[w],
                                         device_id=(x, y, c), device_id_type=MESH).wait_recv()

    return pl.pallas_call(
        body, name=name,
        in_specs=_any_specs(nw), out_specs=_any_specs(nw),
        out_shape=[jax.ShapeDtypeStruct(h.shape, F32) for h in halves],
        input_output_aliases={i: i for i in range(nw)},
        scratch_shapes=[pltpu.SemaphoreType.DMA((nw,)), pltpu.SemaphoreType.DMA((nw,))],
    )(*halves)


SMALL_ROWS = 8


def _all_reduce_small(pack):
    rows, cols = pack.shape
    n_dev = 8

    def body(p_ref, o_ref, slots, send_sems, recv_sems):
        x, y, c, _ = _mesh_place()
        me = 4 * x + 2 * y + c
        slots[me] = p_ref[...]
        copies = []
        for k in range(1, n_dev):
            peer = (me + k) % n_dev
            cp = pltpu.make_async_remote_copy(src_ref=p_ref, dst_ref=slots.at[me], send_sem=send_sems.at[k],
                                              recv_sem=recv_sems.at[k],
                                              device_id=(peer // 4, (peer // 2) % 2, peer % 2), device_id_type=MESH)
            cp.start()
            copies.append(cp)
        for k in range(1, n_dev):
            src = (me + n_dev - k) % n_dev
            pltpu.make_async_remote_copy(src_ref=p_ref, dst_ref=slots.at[src], send_sem=send_sems.at[k],
                                         recv_sem=recv_sems.at[k], device_id=(x, y, c), device_id_type=MESH).wait_recv()
        for cp in copies:
            cp.wait_send()
        total = slots[0]
        for s in range(1, n_dev):
            total = total + slots[s]
        o_ref[...] = total

    return pl.pallas_call(
        body, name="all_reduce_small",
        in_specs=[pl.BlockSpec(memory_space=pltpu.VMEM)], out_specs=pl.BlockSpec(memory_space=pltpu.VMEM),
        out_shape=jax.ShapeDtypeStruct((rows, cols), F32),
        scratch_shapes=[pltpu.VMEM((n_dev, rows, cols), F32), pltpu.SemaphoreType.DMA((n_dev,)),
                        pltpu.SemaphoreType.DMA((n_dev,))],
    )(pack)


ADAMW_STEPS = 4


def _adamw(name, ws, gs, ms, vs):
    n = len(ws)
    steps = ADAMW_STEPS if all(w.shape[0] % (8 * ADAMW_STEPS) == 0 for w in ws) else 1

    def body(*refs):
        w_refs, g_refs, m_refs, v_refs = (refs[k * n:(k + 1) * n] for k in range(4))
        d_refs, nm_refs, nv_refs = (refs[(4 + k) * n:(5 + k) * n] for k in range(3))
        for w_ref, g_ref, m_ref, v_ref, d_ref, nm_ref, nv_ref in zip(w_refs, g_refs, m_refs, v_refs, d_refs, nm_refs, nv_refs):
            gv = g_ref[...]
            nm = ADAM_B1 * m_ref[...] + (1.0 - ADAM_B1) * gv
            nv = ADAM_B2 * v_ref[...] + (1.0 - ADAM_B2) * (gv * gv)
            m_hat = nm / (1.0 - ADAM_B1 ** ADAM_STEP)
            v_hat = nv / (1.0 - ADAM_B2 ** ADAM_STEP)
            d_ref[...] = -ADAM_LR * (m_hat / (jnp.sqrt(v_hat) + ADAM_EPS) + ADAM_WD * w_ref[...])
            nm_ref[...] = nm
            nv_ref[...] = nv

    specs = [pl.BlockSpec((w.shape[0] // steps, w.shape[1]), lambda i: (i, 0)) for w in ws]
    shapes = [jax.ShapeDtypeStruct(w.shape, F32) for w in ws]
    out = pl.pallas_call(
        body, name=name, grid=(steps,),
        in_specs=specs * 4, out_specs=specs * 3, out_shape=shapes * 3,
        compiler_params=_params(("parallel",)),
    )(*ws, *gs, *ms, *vs)
    return out[:n], out[n:2 * n], out[2 * n:]


BIG = ["ffn1_w_in", "ffn1_w_out", "w_in", "conv_w_proj", "attn_w_o", "w_out", "ffn2_w_in", "ffn2_w_out", "conv_dw_kernel"]
COL_SHARDED = ("ffn1_w_in", "w_in", "ffn2_w_in")
SMALL = ["ffn1_norm", "mix_norm", "ffn2_norm", "conv_dw_bias", "conv_ln_g", "conv_ln_b", "q_norm", "k_norm", "attn_sinks", "rel_bias"]
WEIGHTS = ["ffn1_norm", "ffn1_w_in", "ffn1_w_out", "mix_norm", "w_in", "conv_dw_kernel", "conv_dw_bias", "conv_ln_g",
           "conv_ln_b", "conv_w_proj", "q_norm", "k_norm", "attn_sinks", "rel_bias", "attn_w_o", "w_out", "ffn2_norm",
           "ffn2_w_in", "ffn2_w_out"]
SMALL_PLACE = {"ffn1_norm": (0, 0, 1024), "mix_norm": (1, 0, 1024), "ffn2_norm": (2, 0, 1024), "conv_dw_bias": (3, 0, 1024),
               "conv_ln_g": (4, 0, 1024), "conv_ln_b": (5, 0, 1024), "q_norm": (6, 0, 64), "k_norm": (6, 128, 64),
               "attn_sinks": (6, 256, 16), "rel_bias": (7, 0, 512)}
LOSS_PLACE = (6, 384)


def _pack_small(vals, fill=0.0, loss=None):
    pack = jnp.full((SMALL_ROWS, D_MODEL), fill, F32)
    for name, (row, lane, n) in SMALL_PLACE.items():
        pack = pack.at[row, lane:lane + n].set(vals[name].reshape(n))
    if loss is not None:
        pack = pack.at[LOSS_PLACE[0], LOSS_PLACE[1]].set(loss)
    return pack


def _unpack_small(pack, shapes):
    return {name: pack[row, lane:lane + n].reshape(shapes[name]) for name, (row, lane, n) in SMALL_PLACE.items()}


def _shard_halves(name, a):
    if name == "conv_dw_kernel":
        a = jnp.pad(a, ((0, CONV_PAD - CONV_WIDTH), (0, 0)))
    r, cols = a.shape
    return a.reshape(2, r // 2, cols)


GATHER_GROUPS = {"A": ["ffn1_w_in", "ffn1_w_out"],
                 "B": ["w_in", "conv_dw_kernel", "conv_w_proj", "attn_w_o", "w_out"],
                 "C": ["ffn2_w_in", "ffn2_w_out"]}


REDUCE_GROUPS = {"R1": ["ffn2_w_in", "ffn2_w_out"],
                 "R2": ["w_out", "conv_w_proj", "attn_w_o", "conv_dw_kernel", "w_in"],
                 "R3": ["ffn1_w_in", "ffn1_w_out"]}


class _MeshComm:
    def __init__(self, wts, idle_work=()):
        self.idle_work = idle_work
        self.c_idx = lax.axis_index("c").astype(jnp.int32).reshape(1)
        self.place_idx = jnp.stack([2 * lax.axis_index("x") + lax.axis_index("y"), lax.axis_index("c")]).astype(jnp.int32)
        self.gathers, self.exchanges, self.reductions, self.joins, self.reduced = {}, {}, {}, {}, {}
        self.tokens = ()
        self.shards = {n: _shard_halves(n, wts[n]) if n == "conv_dw_kernel" else _shard_halves(n, wts[n]).astype(BF16)
                       for n in BIG}
        self._gather_start("A", ())

    def _gather_start(self, group, after):
        shards = [self.shards[n] for n in GATHER_GROUPS[group]]
        lands = [lax.empty((N_CHIPS,) + s.shape, s.dtype) for s in shards]
        self.gathers[group] = _copy_start("gather_start_" + group, shards, lands, self._gather_plan(group), after=after)
        self.tokens = (self.gathers[group][-1],)

    @staticmethod
    def _gather_plan(group):
        return _gather_both_cores_plan if group == "C" else _gather_plan

    def weights(self, group, after):
        send_sems, recv_sems, shards, lands, token = self.gathers.pop(group)
        after = [token] if after is None else list(after) if isinstance(after, (list, tuple)) else [after]
        if group == "A":
            after += [self.shards[n] for g in ("B", "C") for n in GATHER_GROUPS[g]] + list(self.idle_work)
        shards, lands = _copy_wait("gather_wait_" + group, send_sems, recv_sems, shards, lands, after,
                                   self._gather_plan(group))
        gathered = lands if group == "C" else _gather_forward("gather_forward_" + group, shards, lands)
        self.tokens = ()
        following = {"A": "B", "B": "C"}.get(group)
        if following:
            self._gather_start(following, (gathered[0],))
        out = {}
        for n, g4 in zip(GATHER_GROUPS[group], gathered):
            r, cols = g4.shape[2] * 2, g4.shape[3]
            if n in COL_SHARDED:
                out[n] = g4.reshape(N_CHIPS, r, cols)
            elif n == "conv_dw_kernel":
                out[n] = g4.reshape(N_CHIPS, r, cols).transpose(1, 0, 2).reshape(r, N_CHIPS * cols)
            else:
                out[n] = g4.reshape(N_CHIPS * r, cols)
        return out

    def reduce_start(self, group, grads, behind=False):
        names = list(grads)
        g4 = []
        for n in names:
            a = grads[n]
            if n == "conv_dw_kernel":
                a = a.reshape(CONV_PAD, N_CHIPS, -1).transpose(1, 0, 2)
            elif n not in COL_SHARDED:
                a = a.reshape(N_CHIPS, a.shape[0] // N_CHIPS, a.shape[1])
            g4.append(a.reshape(N_CHIPS, 2, a.shape[1] // 2, a.shape[2]))
        if behind:
            lands = [lax.empty((N_CHIPS,) + g.shape[2:], g.dtype) for g in g4]
            started = _copy_start("exchange_start_" + group, g4, lands, _exchange_plan)
            self.exchanges[group] = (names,) + started
            return (started[-1],)
        return self._scatter_start(group, names, g4, _exchange_halves("exchange_halves_" + group, g4))

    def exchange_finish(self, group, after):
        names, send_sems, recv_sems, g4, lands, _ = self.exchanges.pop(group)
        g4, got = _copy_wait("exchange_wait_" + group, send_sems, recv_sems, g4, lands, after, _exchange_plan)
        return self._scatter_start(group, names, g4, got)

    def _scatter_start(self, group, names, g4, got):
        sums = _add_own_halves("add_own_halves_" + group, self.c_idx, g4, got)
        lands = [lax.empty((N_CHIPS - 1,) + s.shape[1:], s.dtype) for s in sums]
        started = _copy_start("scatter_start_" + group, sums, lands, _scatter_plan)
        self.reductions[group] = (names,) + started
        return (started[-1],)

    def reduce_finish(self, group, after, behind=False):
        names, send_sems, recv_sems, sums, lands, _ = self.reductions.pop(group)
        sums, lands = _copy_wait("scatter_wait_" + group, send_sems, recv_sems, sums, lands, after, _scatter_plan)
        halves = _sum_pieces("sum_pieces_" + group, self.place_idx, sums, lands)
        if behind:
            started = _copy_start("join_start_" + group, halves, [], _join_plan)
            self.joins[group] = (names,) + started
            return (started[-1],)
        self.reduced.update(zip(names, _join_halves("join_halves_" + group, halves)))
        return ()

    def join_finish(self, group, after):
        names, send_sems, recv_sems, halves, _, _ = self.joins.pop(group)
        self.reduced.update(zip(names, _copy_wait("join_wait_" + group, send_sems, recv_sems, halves, [], after, _join_plan)[0]))


def kernel(x, ffn1_norm, ffn1_w_in, ffn1_w_out, mix_norm, w_in, conv_dw_kernel, conv_dw_bias, conv_ln_g, conv_ln_b, conv_w_proj, q_norm, k_norm, attn_sinks, rel_bias, attn_w_o, w_out, ffn2_norm, ffn2_w_in, ffn2_w_out, loss_target, m_ffn1_norm, m_ffn1_w_in, m_ffn1_w_out, m_mix_norm, m_w_in, m_conv_dw_kernel, m_conv_dw_bias, m_conv_ln_g, m_conv_ln_b, m_conv_w_proj, m_q_norm, m_k_norm, m_attn_sinks, m_rel_bias, m_attn_w_o, m_w_out, m_ffn2_norm, m_ffn2_w_in, m_ffn2_w_out, v_ffn1_norm, v_ffn1_w_in, v_ffn1_w_out, v_mix_norm, v_w_in, v_conv_dw_kernel, v_conv_dw_bias, v_conv_ln_g, v_conv_ln_b, v_conv_w_proj, v_q_norm, v_k_norm, v_attn_sinks, v_rel_bias, v_attn_w_o, v_w_out, v_ffn2_norm, v_ffn2_w_in, v_ffn2_w_out):
    args = dict(locals())
    wts = {n: args[n] for n in WEIGHTS}
    mom = {n: args["m_" + n] for n in WEIGHTS}
    var = {n: args["v_" + n] for n in WEIGHTS}
    small_packs = [_pack_small(wts), _pack_small(mom), _pack_small(var, fill=1.0)]
    dw_moments = [_shard_halves("conv_dw_kernel", a) for a in (wts["conv_dw_kernel"], mom["conv_dw_kernel"], var["conv_dw_kernel"])]
    comm = _MeshComm(wts, idle_work=small_packs + dw_moments)
    small = {n: wts[n] if n in ("attn_sinks", "rel_bias") else wts[n].reshape(1, -1) for n in SMALL}
    loss_part, grad_x, g = _local_step(x[0], loss_target[0], small, comm)

    small_sum = _all_reduce_small(_pack_small(g, loss=loss_part))
    loss = small_sum[LOSS_PLACE[0], LOSS_PLACE[1]]
    small_shapes = {n: wts[n].shape for n in SMALL}
    g_small = _unpack_small(small_sum, small_shapes)

    grads, delta, new_m, new_v = {}, {}, {}, {}
    for group, names in REDUCE_GROUPS.items():
        gs = [comm.reduced[n].reshape(-1, comm.reduced[n].shape[2]) for n in names]
        state = [[a.reshape(g.shape) for a in (dw_moments if n == "conv_dw_kernel" else (wts[n], mom[n], var[n]))]
                 for n, g in zip(names, gs)]
        ds, nms, nvs = _adamw("adamw_" + group, [s[0] for s in state], gs, [s[1] for s in state], [s[2] for s in state])
        for n, g, d, nm, nv in zip(names, gs, ds, nms, nvs):
            cut = (lambda a: a[:CONV_WIDTH]) if n == "conv_dw_kernel" else (lambda a: a)
            grads[n], delta[n], new_m[n], new_v[n] = cut(g), cut(d), cut(nm), cut(nv)
    (d,), (nm,), (nv,) = _adamw("adamw_small", [small_packs[0]], [small_sum], [small_packs[1]], [small_packs[2]])
    grads.update(g_small)
    delta.update(_unpack_small(d, small_shapes))
    new_m.update(_unpack_small(nm, small_shapes))
    new_v.update(_unpack_small(nv, small_shapes))

    return (loss, grad_x[None], *[grads[n] for n in WEIGHTS], *[delta[n] for n in WEIGHTS],
            *[new_m[n] for n in WEIGHTS], *[new_v[n] for n in WEIGHTS])
```
